```python
import math
import jax
import jax.numpy as jnp
from jax import lax
import numpy as np

D_MODEL = 1024
BATCH = 8
SEQ = 2048
DEPTH = 1

GDN_HEADS = 8
GDN_HEAD_DIM = 128
GDN_WIDTH = GDN_HEADS * GDN_HEAD_DIM
GDN_CONV = 4
CHUNK = 64
CONV_WIDTH = 1024
SHORT_CONV = 3
MIX_WIDTH = GDN_WIDTH + CONV_WIDTH
EPS = 1e-6

PROJ_SPLITS = (
    3 * GDN_WIDTH,
    GDN_WIDTH,
    GDN_HEADS,
    GDN_HEADS,
    CONV_WIDTH,
    CONV_WIDTH,
    CONV_WIDTH,
    CONV_WIDTH,
)
PROJ_WIDTH = sum(PROJ_SPLITS)

kernel_name = "hybrid_gdn_shortconv_block"


def rmsnorm(x, w):
    xf = x.astype(jnp.float32)
    xf = xf * lax.rsqrt(jnp.mean(xf * xf, axis=-1, keepdims=True) + EPS)
    return (xf * w.astype(jnp.float32)).astype(x.dtype)


def l2norm(x):
    return x * lax.rsqrt(jnp.sum(x * x, axis=-1, keepdims=True) + EPS)


def causal_depthwise_conv(x, w):
    K = w.shape[0]
    L = x.shape[1]
    xp = jnp.pad(x, ((0, 0), (K - 1, 0), (0, 0)))
    return sum(xp[:, j:j + L] * w[j] for j in range(K))


def gated_delta_rule_chunked(q, k, v, g, beta):
    Bsz, L, H, DK = q.shape
    DV = v.shape[-1]
    n = L // CHUNK

    def chunks(t):
        return t.reshape(Bsz, n, CHUNK, H, -1).transpose(0, 3, 1, 2, 4)

    q, k, v = chunks(q), chunks(k), chunks(v)
    g = g.reshape(Bsz, n, CHUNK, H).transpose(0, 3, 1, 2)
    beta = beta.reshape(Bsz, n, CHUNK, H).transpose(0, 3, 1, 2)
    g = jnp.cumsum(g, axis=-1)

    causal = jnp.tril(jnp.ones((CHUNK, CHUNK), dtype=bool))
    strict = jnp.tril(jnp.ones((CHUNK, CHUNK), dtype=bool), k=-1)
    decay = jnp.exp(jnp.where(causal, g[..., :, None] - g[..., None, :], -jnp.inf))

    k_beta = k * beta[..., None]
    v_beta = v * beta[..., None]
    A = jnp.where(strict, jnp.einsum('bhnid,bhnjd->bhnij', k_beta, k) * decay, 0.0)
    eye = jnp.eye(CHUNK, dtype=q.dtype)
    rhs = jnp.concatenate([v_beta, k_beta * jnp.exp(g)[..., None]], axis=-1)
    sol = lax.linalg.triangular_solve(eye + A, rhs, left_side=True, lower=True,
                                      unit_diagonal=True)
    u = sol[..., :DV]
    w = sol[..., DV:]

    attn_intra = jnp.where(causal, jnp.einsum('bhnid,bhnjd->bhnij', q, k) * decay, 0.0)
    g_last = g[..., -1]
    k_state = k * jnp.exp(g_last[..., None] - g)[..., None]
    q_decay = q * jnp.exp(g)[..., None]

    def step(S, inp):
        qd, w_c, u_c, a_c, ks, gl = inp
        v_new = u_c - jnp.einsum('bhck,bhkv->bhcv', w_c, S)
        o = jnp.einsum('bhck,bhkv->bhcv', qd, S) + jnp.einsum('bhij,bhjv->bhiv', a_c, v_new)
        S = S * jnp.exp(gl)[..., None, None] + jnp.einsum('bhck,bhcv->bhkv', ks, v_new)
        return S, o

    xs = tuple(jnp.moveaxis(t, 2, 0) for t in (q_decay, w, u, attn_intra, k_state, g_last))
    S0 = jnp.zeros((Bsz, H, DK, DV), dtype=q.dtype)
    _, o = lax.scan(step, S0, xs)
    return o.transpose(1, 0, 3, 2, 4).reshape(Bsz, L, H, DV)


def hybrid_layer(x, norm_in_w, w_in, conv_qkv_w, A_log, dt_bias, gdn_norm_w,
                 conv_w, conv_b, w_out):
    Bsz, L, _ = x.shape
    h = rmsnorm(x, norm_in_w)
    proj = h @ w_in
    split_at = [int(i) for i in np.cumsum(PROJ_SPLITS)[:-1]]
    qkv, z_g, b_g, a_g, gate_b, gate_c, h_c, z_c = jnp.split(proj, split_at, axis=-1)

    qkv = jax.nn.silu(causal_depthwise_conv(qkv, conv_qkv_w))
    q, k, v = jnp.split(qkv, 3, axis=-1)
    shp = (Bsz, L, GDN_HEADS, GDN_HEAD_DIM)
    q = l2norm(q.reshape(shp).astype(jnp.float32)) * (GDN_HEAD_DIM ** -0.5)
    k = l2norm(k.reshape(shp).astype(jnp.float32))
    v = v.reshape(shp).astype(jnp.float32)
    beta = jax.nn.sigmoid(b_g.astype(jnp.float32))
    g = -jnp.exp(A_log.astype(jnp.float32)) * jax.nn.softplus(
        a_g.astype(jnp.float32) + dt_bias.astype(jnp.float32))
    o = gated_delta_rule_chunked(q, k, v, g, beta).astype(x.dtype)
    o = rmsnorm(o, gdn_norm_w) * jax.nn.silu(z_g.reshape(shp))
    o = o.reshape(Bsz, L, GDN_WIDTH)

    y_c = gate_b * (causal_depthwise_conv(gate_c * h_c, conv_w) + conv_b)
    y_c = y_c * jax.nn.silu(z_c)

    mix = jnp.concatenate([o, y_c], axis=-1)
    return x + mix @ w_out


def _fwd_setup_inputs(seed: int = 0) -> dict:
    key = jax.random.key(seed)
    ks = jax.random.split(key, 12)
    f32 = jnp.float32
    x = jax.random.normal(ks[0], (BATCH, SEQ, D_MODEL), f32)
    norm_in_w = 1.0 + 0.02 * jax.random.normal(ks[1], (DEPTH, D_MODEL), f32)
    w_in = jax.random.normal(ks[2], (DEPTH, D_MODEL, PROJ_WIDTH), f32) * D_MODEL ** -0.5
    conv_qkv_w = jax.random.normal(ks[3], (DEPTH, GDN_CONV, 3 * GDN_WIDTH), f32) * GDN_CONV ** -0.5
    A_log = jnp.log(jax.random.uniform(ks[4], (DEPTH, GDN_HEADS), f32, minval=1.0, maxval=16.0))
    dt = jnp.exp(jax.random.uniform(ks[5], (DEPTH, GDN_HEADS), f32,
                                    minval=math.log(1e-3), maxval=math.log(1e-1)))
    dt_bias = dt + jnp.log(-jnp.expm1(-dt))
    gdn_norm_w = 1.0 + 0.02 * jax.random.normal(ks[6], (DEPTH, GDN_HEAD_DIM), f32)
    conv_w = jax.random.normal(ks[7], (DEPTH, SHORT_CONV, CONV_WIDTH), f32) * SHORT_CONV ** -0.5
    conv_b = 0.01 * jax.random.normal(ks[8], (DEPTH, CONV_WIDTH), f32)
    w_out = jax.random.normal(ks[9], (DEPTH, MIX_WIDTH, D_MODEL), f32) * MIX_WIDTH ** -0.5
    final_norm_w = 1.0 + 0.02 * jax.random.normal(ks[10], (D_MODEL,), f32)
    return {"x": x, "norm_in_w": norm_in_w, "w_in": w_in, "conv_qkv_w": conv_qkv_w,
            "A_log": A_log, "dt_bias": dt_bias, "gdn_norm_w": gdn_norm_w,
            "conv_w": conv_w, "conv_b": conv_b, "w_out": w_out,
            "final_norm_w": final_norm_w}


def _fwd_reference(x, norm_in_w, w_in, conv_qkv_w, A_log, dt_bias, gdn_norm_w,
              conv_w, conv_b, w_out, final_norm_w):
    for layer in range(DEPTH):
        x = hybrid_layer(x, norm_in_w[layer], w_in[layer], conv_qkv_w[layer],
                         A_log[layer], dt_bias[layer], gdn_norm_w[layer],
                         conv_w[layer], conv_b[layer], w_out[layer])
    return rmsnorm(x, final_norm_w)


import jax as _jax
import jax.numpy as _jnp

TWIN_FORMAT = 'train_step'
FWD_PARAMS = ['x', 'norm_in_w', 'w_in', 'conv_qkv_w', 'A_log', 'dt_bias', 'gdn_norm_w', 'conv_w', 'conv_b', 'w_out', 'final_norm_w']
TWIN_WEIGHTS = ['norm_in_w', 'w_in', 'conv_qkv_w', 'A_log', 'dt_bias', 'gdn_norm_w', 'conv_w', 'conv_b', 'w_out', 'final_norm_w']
TWIN_DIFF_INPUT = 'x'
TWIN_INPUTS = ['x', 'norm_in_w', 'w_in', 'conv_qkv_w', 'A_log', 'dt_bias', 'gdn_norm_w', 'conv_w', 'conv_b', 'w_out', 'final_norm_w', 'loss_target', 'm_norm_in_w', 'm_w_in', 'm_conv_qkv_w', 'm_A_log', 'm_dt_bias', 'm_gdn_norm_w', 'm_conv_w', 'm_conv_b', 'm_w_out', 'm_final_norm_w', 'v_norm_in_w', 'v_w_in', 'v_conv_qkv_w', 'v_A_log', 'v_dt_bias', 'v_gdn_norm_w', 'v_conv_w', 'v_conv_b', 'v_w_out', 'v_final_norm_w']
TWIN_OUTPUTS = ['loss', 'grad_x', 'grad_norm_in_w', 'grad_w_in', 'grad_conv_qkv_w', 'grad_A_log', 'grad_dt_bias', 'grad_gdn_norm_w', 'grad_conv_w', 'grad_conv_b', 'grad_w_out', 'grad_final_norm_w', 'delta_norm_in_w', 'delta_w_in', 'delta_conv_qkv_w', 'delta_A_log', 'delta_dt_bias', 'delta_gdn_norm_w', 'delta_conv_w', 'delta_conv_b', 'delta_w_out', 'delta_final_norm_w', 'new_m_norm_in_w', 'new_m_w_in', 'new_m_conv_qkv_w', 'new_m_A_log', 'new_m_dt_bias', 'new_m_gdn_norm_w', 'new_m_conv_w', 'new_m_conv_b', 'new_m_w_out', 'new_m_final_norm_w', 'new_v_norm_in_w', 'new_v_w_in', 'new_v_conv_qkv_w', 'new_v_A_log', 'new_v_dt_bias', 'new_v_gdn_norm_w', 'new_v_conv_w', 'new_v_conv_b', 'new_v_w_out', 'new_v_final_norm_w']
TWIN_LEAF_KINDS = {'loss': 'loss', 'grad_x': 'grad_x', 'grad_norm_in_w': 'grad_w', 'grad_w_in': 'grad_w', 'grad_conv_qkv_w': 'grad_w', 'grad_A_log': 'grad_w', 'grad_dt_bias': 'grad_w', 'grad_gdn_norm_w': 'grad_w', 'grad_conv_w': 'grad_w', 'grad_conv_b': 'grad_w', 'grad_w_out': 'grad_w', 'grad_final_norm_w': 'grad_w', 'delta_norm_in_w': 'delta_w', 'delta_w_in': 'delta_w', 'delta_conv_qkv_w': 'delta_w', 'delta_A_log': 'delta_w', 'delta_dt_bias': 'delta_w', 'delta_gdn_norm_w': 'delta_w', 'delta_conv_w': 'delta_w', 'delta_conv_b': 'delta_w', 'delta_w_out': 'delta_w', 'delta_final_norm_w': 'delta_w', 'new_m_norm_in_w': 'new_m', 'new_m_w_in': 'new_m', 'new_m_conv_qkv_w': 'new_m', 'new_m_A_log': 'new_m', 'new_m_dt_bias': 'new_m', 'new_m_gdn_norm_w': 'new_m', 'new_m_conv_w': 'new_m', 'new_m_conv_b': 'new_m', 'new_m_w_out': 'new_m', 'new_m_final_norm_w': 'new_m', 'new_v_norm_in_w': 'new_v', 'new_v_w_in': 'new_v', 'new_v_conv_qkv_w': 'new_v', 'new_v_A_log': 'new_v', 'new_v_dt_bias': 'new_v', 'new_v_gdn_norm_w': 'new_v', 'new_v_conv_w': 'new_v', 'new_v_conv_b': 'new_v', 'new_v_w_out': 'new_v', 'new_v_final_norm_w': 'new_v'}


def _forward(args):
    return _fwd_reference(*[args[k] for k in FWD_PARAMS])


def _output_shape():
    out = _jax.eval_shape(lambda: _forward(_fwd_setup_inputs(0)))
    return out.shape, out.dtype

N_MICROBATCH = 1
ADAM_LR = 0.001
ADAM_B1 = 0.9
ADAM_B2 = 0.999
ADAM_EPS = 1e-08
ADAM_WD = 0.01
ADAM_STEP = 10
PER_EXAMPLE_BATCH_AXIS = {'x': 0, 'loss_target': 0}
SHARED_INPUTS = []
_WEIGHT_DTYPES = {'norm_in_w': _jnp.float32, 'w_in': _jnp.float32, 'conv_qkv_w': _jnp.float32, 'A_log': _jnp.float32, 'dt_bias': _jnp.float32, 'gdn_norm_w': _jnp.float32, 'conv_w': _jnp.float32, 'conv_b': _jnp.float32, 'w_out': _jnp.float32, 'final_norm_w': _jnp.float32}
MOMENT_SCALE = {'norm_in_w': 1.292583e-01, 'w_in': 4.164209e-02, 'conv_qkv_w': 3.499825e-02, 'A_log': 2.329263e-01, 'dt_bias': 2.255886e-01, 'gdn_norm_w': 1.322453e-01, 'conv_w': 4.723346e-02, 'conv_b': 4.522681e-02, 'w_out': 6.331256e-02, 'final_norm_w': 1.600971e+01}


def _to_microbatches(a, axis):
    t = _jnp.moveaxis(a, axis, 0)
    t = t.reshape((N_MICROBATCH, t.shape[0] // N_MICROBATCH) + t.shape[1:])
    return _jnp.moveaxis(t, 1, axis + 1)


def setup_inputs(seed: int = 0) -> dict:
    inp = _fwd_setup_inputs(seed)
    key = _jax.random.fold_in(_jax.random.key(seed), 7919)
    shape, _ = _output_shape()
    out = dict(inp)
    out["loss_target"] = _jax.random.normal(_jax.random.fold_in(key, 0), shape, _jnp.float32)
    for i, name in enumerate(TWIN_WEIGHTS):
        w = inp[name].astype(_jnp.float32)
        if MOMENT_SCALE is None:
            s = _jnp.sqrt(_jnp.mean(_jnp.square(w)) + 1e-30)
        else:
            s = MOMENT_SCALE[name]
        km, kv = _jax.random.split(_jax.random.fold_in(key, i + 1))
        out[name] = w
        out["m_" + name] = s * _jax.random.normal(km, w.shape, _jnp.float32)
        out["v_" + name] = (s * s) * _jax.random.uniform(kv, w.shape, _jnp.float32, 0.5, 1.5)
    if N_MICROBATCH > 1:
        for name, axis in PER_EXAMPLE_BATCH_AXIS.items():
            out[name] = _to_microbatches(out[name], axis)
    return {'x': out['x'], 'norm_in_w': out['norm_in_w'], 'w_in': out['w_in'], 'conv_qkv_w': out['conv_qkv_w'], 'A_log': out['A_log'], 'dt_bias': out['dt_bias'], 'gdn_norm_w': out['gdn_norm_w'], 'conv_w': out['conv_w'], 'conv_b': out['conv_b'], 'w_out': out['w_out'], 'final_norm_w': out['final_norm_w'], 'loss_target': out['loss_target'], 'm_norm_in_w': out['m_norm_in_w'], 'm_w_in': out['m_w_in'], 'm_conv_qkv_w': out['m_conv_qkv_w'], 'm_A_log': out['m_A_log'], 'm_dt_bias': out['m_dt_bias'], 'm_gdn_norm_w': out['m_gdn_norm_w'], 'm_conv_w': out['m_conv_w'], 'm_conv_b': out['m_conv_b'], 'm_w_out': out['m_w_out'], 'm_final_norm_w': out['m_final_norm_w'], 'v_norm_in_w': out['v_norm_in_w'], 'v_w_in': out['v_w_in'], 'v_conv_qkv_w': out['v_conv_qkv_w'], 'v_A_log': out['v_A_log'], 'v_dt_bias': out['v_dt_bias'], 'v_gdn_norm_w': out['v_gdn_norm_w'], 'v_conv_w': out['v_conv_w'], 'v_conv_b': out['v_conv_b'], 'v_w_out': out['v_w_out'], 'v_final_norm_w': out['v_final_norm_w']}


def _loss(weights, diff, rest, loss_target):
    with _jax.named_scope("forward"):
        args = {**rest, TWIN_DIFF_INPUT: diff, **{k: w.astype(_WEIGHT_DTYPES[k]) for k, w in weights.items()}}
        y = _forward(args)
    with _jax.named_scope("loss_head"):
        err = _jnp.square(y.astype(_jnp.float32) - loss_target)
        return 0.5 * _jnp.sum(_jnp.mean(err, axis=-1)) if err.ndim else 0.5 * err


def _adamw(w, g, m, v):
    m = ADAM_B1 * m + (1.0 - ADAM_B1) * g
    v = ADAM_B2 * v + (1.0 - ADAM_B2) * _jnp.square(g)
    m_hat = m / (1.0 - ADAM_B1 ** ADAM_STEP)
    v_hat = v / (1.0 - ADAM_B2 ** ADAM_STEP)
    delta = -ADAM_LR * (m_hat / (_jnp.sqrt(v_hat) + ADAM_EPS) + ADAM_WD * w)
    return delta, m, v


def reference(x, norm_in_w, w_in, conv_qkv_w, A_log, dt_bias, gdn_norm_w, conv_w, conv_b, w_out, final_norm_w, loss_target, m_norm_in_w, m_w_in, m_conv_qkv_w, m_A_log, m_dt_bias, m_gdn_norm_w, m_conv_w, m_conv_b, m_w_out, m_final_norm_w, v_norm_in_w, v_w_in, v_conv_qkv_w, v_A_log, v_dt_bias, v_gdn_norm_w, v_conv_w, v_conv_b, v_w_out, v_final_norm_w):
    given = dict(x=x, norm_in_w=norm_in_w, w_in=w_in, conv_qkv_w=conv_qkv_w, A_log=A_log, dt_bias=dt_bias, gdn_norm_w=gdn_norm_w, conv_w=conv_w, conv_b=conv_b, w_out=w_out, final_norm_w=final_norm_w, loss_target=loss_target, m_norm_in_w=m_norm_in_w, m_w_in=m_w_in, m_conv_qkv_w=m_conv_qkv_w, m_A_log=m_A_log, m_dt_bias=m_dt_bias, m_gdn_norm_w=m_gdn_norm_w, m_conv_w=m_conv_w, m_conv_b=m_conv_b, m_w_out=m_w_out, m_final_norm_w=m_final_norm_w, v_norm_in_w=v_norm_in_w, v_w_in=v_w_in, v_conv_qkv_w=v_conv_qkv_w, v_A_log=v_A_log, v_dt_bias=v_dt_bias, v_gdn_norm_w=v_gdn_norm_w, v_conv_w=v_conv_w, v_conv_b=v_conv_b, v_w_out=v_w_out, v_final_norm_w=v_final_norm_w)
    weights = {n: given[n] for n in TWIN_WEIGHTS}
    shared = {n: given[n] for n in SHARED_INPUTS}
    per_example = {n: given[n] for n in ['x']}
    grad_fn = _jax.value_and_grad(_loss, argnums=(0, 1))

    def one_microbatch(ex, loss_target):
        ex = dict(ex)
        diff = ex.pop(TWIN_DIFF_INPUT)
        return grad_fn(weights, diff, {**shared, **ex}, loss_target)

    if N_MICROBATCH == 1:
        loss, (grad_w, grad_x) = one_microbatch(per_example, given["loss_target"])
    else:
        def body(carry, xs):
            loss_sum, grad_sum = carry
            l_k, (gw_k, gx_k) = one_microbatch(xs[0], xs[1])
            with _jax.named_scope("update"):
                return (loss_sum + l_k, _jax.tree.map(_jnp.add, grad_sum, gw_k)), gx_k

        init = (_jnp.zeros((), _jnp.float32), _jax.tree.map(_jnp.zeros_like, weights))
        (loss, grad_w), grad_x = _jax.lax.scan(body, init, (per_example, given["loss_target"]))
    with _jax.named_scope("update"):
        delta_w, new_m, new_v = {}, {}, {}
        for n in TWIN_WEIGHTS:
            delta_w[n], new_m[n], new_v[n] = _adamw(weights[n], grad_w[n], given["m_" + n], given["v_" + n])
    return (loss, grad_x, *[grad_w[n] for n in TWIN_WEIGHTS], *[delta_w[n] for n in TWIN_WEIGHTS],
            *[new_m[n] for n in TWIN_WEIGHTS], *[new_v[n] for n in TWIN_WEIGHTS])
```

```python
import functools
import math

import jax
import jax.numpy as jnp
from jax import lax
from jax.experimental import pallas as pl
from jax.experimental.pallas import tpu as pltpu

F32 = jnp.float32
BF16 = jnp.bfloat16
MESH = pl.DeviceIdType.MESH
ANY = pl.BlockSpec(memory_space=pl.ANY)

HEADS = 8
DH = 128
CH = 64
GW = HEADS * DH
EPS = 1e-6
VMEM_V7X = 64 * 1024 * 1024

QB, KB, VB, ZB, BAB, GBB, GCB, HCB, ZCB = 0, 8, 16, 24, 32, 33, 41, 49, 57
NPB = 65
PW = NPB * DH

ADAM_LR, ADAM_B1, ADAM_B2, ADAM_EPS, ADAM_WD, ADAM_STEP = 0.001, 0.9, 0.999, 1e-08, 0.01, 10

R_NIN, R_CB, R_FN, R_AD, R_GN, R_CQ, R_CW, R_LOSS, PACK_ROWS = 0, 1, 2, 3, 4, 5, 17, 20, 24

NN = ((1,), (0,))
NT = ((1,), (1,))
TN = ((0,), (0,))


def _dot(a, b, dims=NN, exact=False):
    prec = lax.Precision.HIGHEST if exact else None
    return lax.dot_general(a, b, (dims, ((), ())), precision=prec, preferred_element_type=F32)


def _params(sem=None, vmem=None):
    kw = {}
    if sem is not None:
        kw["dimension_semantics"] = sem
    if vmem is not None:
        kw["vmem_limit_bytes"] = int(min(max(vmem, 32 * 2**20), VMEM_V7X - 8 * 2**20))
    return pltpu.CompilerParams(**kw)


def _sigmoid(x):
    return 1.0 / (1.0 + jnp.exp(-x))


def _dsilu(x, s):
    return s * (1.0 + x * (1.0 - s))


def _rows(shape):
    return lax.broadcasted_iota(jnp.int32, shape, 0)


def _shift_down(x, s):
    if s == 0:
        return x
    return jnp.where(_rows(x.shape) >= s, pltpu.roll(x, s, 0), 0.0)


def _shift_up(x, s):
    if s == 0:
        return x
    n = x.shape[0]
    return jnp.where(_rows(x.shape) < n - s, pltpu.roll(x, n - s, 0), 0.0)


def _matmul(a, b, dims, out_dtype, tm, tn, tk, name, add=None):
    if dims == NN:
        (m, k), (_, n) = a.shape, b.shape
    elif dims == NT:
        (m, k), (n, _) = a.shape, b.shape
    else:
        (k, m), (_, n) = a.shape, b.shape
    tm, tn, tk = min(tm, m), min(tn, n), min(tk, k)
    assert m % tm == 0 and n % tn == 0 and k % tk == 0, (name, m, n, k, tm, tn, tk)
    nk = k // tk

    def body(*refs):
        if add is None:
            a_ref, b_ref, o_ref = refs[:3]
            add_ref = None
        else:
            a_ref, b_ref, add_ref, o_ref = refs[:4]
        part = _dot(a_ref[...], b_ref[...], dims)
        if nk == 1:
            if add_ref is not None:
                part = part + add_ref[...]
            o_ref[...] = part.astype(out_dtype)
            return
        acc = refs[-1]
        kk = pl.program_id(2)

        @pl.when(kk == 0)
        def _():
            acc[...] = part

        @pl.when(kk > 0)
        def _():
            acc[...] += part

        @pl.when(kk == nk - 1)
        def _():
            r = acc[...]
            if add_ref is not None:
                r = r + add_ref[...]
            o_ref[...] = r.astype(out_dtype)

    if dims == TN:
        a_spec = pl.BlockSpec((tk, tm), lambda i, j, kk: (kk, i))
    else:
        a_spec = pl.BlockSpec((tm, tk), lambda i, j, kk: (i, kk))
    if dims == NT:
        b_spec = pl.BlockSpec((tn, tk), lambda i, j, kk: (j, kk))
    else:
        b_spec = pl.BlockSpec((tk, tn), lambda i, j, kk: (kk, j))
    o_spec = pl.BlockSpec((tm, tn), lambda i, j, kk: (i, j))
    in_specs = [a_spec, b_spec]
    args = [a, b]
    if add is not None:
        in_specs.append(o_spec)
        args.append(add)
    osz = jnp.dtype(out_dtype).itemsize
    est = 2 * (tm * tk * a.dtype.itemsize + tk * tn * b.dtype.itemsize + tm * tn * osz)
    est += 3 * tm * tn * 4 + (2 * tm * tn * 4 if add is not None else 0)
    return pl.pallas_call(
        body, name=name, grid=(m // tm, n // tn, nk),
        in_specs=in_specs, out_specs=o_spec,
        out_shape=jax.ShapeDtypeStruct((m, n), out_dtype),
        scratch_shapes=[pltpu.VMEM((tm, tn), F32)] if nk > 1 else [],
        compiler_params=_params(("parallel", "parallel", "arbitrary"), est + 8 * 2**20),
    )(*args)


def _cast_bf16(a, rows, name):
    r, c = a.shape
    rows = min(rows, r)

    def body(a_ref, o_ref):
        o_ref[...] = a_ref[...].astype(BF16)

    return pl.pallas_call(
        body, name=name, grid=(r // rows,),
        in_specs=[pl.BlockSpec((rows, c), lambda i: (i, 0))],
        out_specs=pl.BlockSpec((rows, c), lambda i: (i, 0)),
        out_shape=jax.ShapeDtypeStruct((r, c), BF16),
        compiler_params=_params(("parallel",)),
    )(a)


def _rms_in(x, w):
    n, d = x.shape
    tr = min(256, n)

    def body(x_ref, w_ref, h_ref):
        xv = x_ref[...]
        r = lax.rsqrt(jnp.mean(xv * xv, axis=-1, keepdims=True) + EPS)
        h_ref[...] = (xv * r * w_ref[...]).astype(BF16)

    return pl.pallas_call(
        body, name="rms_in", grid=(n // tr,),
        in_specs=[pl.BlockSpec((tr, d), lambda i: (i, 0)), pl.BlockSpec((1, d), lambda i: (0, 0))],
        out_specs=pl.BlockSpec((tr, d), lambda i: (i, 0)),
        out_shape=jax.ShapeDtypeStruct((n, d), BF16),
        compiler_params=_params(("parallel",)),
    )(x, w)


def _conv_silu(p, w_ref, taps):
    c = None
    for j in range(taps):
        t = _shift_down(p, taps - 1 - j) * w_ref[j:j + 1, :]
        c = t if c is None else c + t
    return c


def _prep_qkv(proj, cw):
    n = proj.shape[0]

    def body(pq, pk, pv, wq, wk, wv, q_ref, k_ref, v_ref):
        for p_ref, w_ref, o_ref, kind in ((pq, wq, q_ref, 0), (pk, wk, k_ref, 1), (pv, wv, v_ref, 2)):
            c = _conv_silu(p_ref[...], w_ref, 4)
            a = c * _sigmoid(c)
            if kind < 2:
                r = lax.rsqrt(jnp.sum(a * a, axis=-1, keepdims=True) + EPS)
                a = a * (r * (DH ** -0.5 if kind == 0 else 1.0))
            o_ref[...] = a

    col = lambda base: pl.BlockSpec((n, DH), lambda h: (0, base + h))
    wcol = lambda base: pl.BlockSpec((4, DH), lambda h: (0, base + h))
    out = jax.ShapeDtypeStruct((n, GW), F32)
    return pl.pallas_call(
        body, name="prep_qkv", grid=(HEADS,),
        in_specs=[col(QB), col(KB), col(VB), wcol(QB), wcol(KB), wcol(VB)],
        out_specs=[col(0)] * 3, out_shape=[out] * 3,
        compiler_params=_params(("parallel",), 40 * 2**20),
    )(proj, proj, proj, cw, cw, cw)


def _prep_qkv_bwd(proj, cw, dq, dk, dv):
    n = proj.shape[0]

    def body(pq, pk, pv, wq, wk, wv, dq_ref, dk_ref, dv_ref, oq, ok, ov, gq, gk, gv):
        for p_ref, w_ref, d_ref, o_ref, g_ref, kind in (
                (pq, wq, dq_ref, oq, gq, 0), (pk, wk, dk_ref, ok, gk, 1), (pv, wv, dv_ref, ov, gv, 2)):
            p = p_ref[...]
            c = _conv_silu(p, w_ref, 4)
            s = _sigmoid(c)
            a = c * s
            d = d_ref[...]
            if kind < 2:
                r = lax.rsqrt(jnp.sum(a * a, axis=-1, keepdims=True) + EPS)
                sc = DH ** -0.5 if kind == 0 else 1.0
                d = (sc * r) * (d - a * ((r * r) * jnp.sum(d * a, axis=-1, keepdims=True)))
            dc = d * _dsilu(c, s)
            dp = None
            for j in range(4):
                g_ref[j:j + 1, :] = jnp.sum(dc * _shift_down(p, 3 - j), axis=0, keepdims=True)
                t = _shift_up(dc, 3 - j) * w_ref[j:j + 1, :]
                dp = t if dp is None else dp + t
            o_ref[...] = dp.astype(BF16)

    col = lambda base: pl.BlockSpec((n, DH), lambda h: (0, base + h))
    wcol = lambda base: pl.BlockSpec((4, DH), lambda h: (0, base + h))
    return pl.pallas_call(
        body, name="prep_qkv_bwd", grid=(HEADS,),
        in_specs=[col(QB), col(KB), col(VB), wcol(QB), wcol(KB), wcol(VB), col(0), col(0), col(0)],
        out_specs=[col(0)] * 3 + [wcol(0)] * 3,
        out_shape=[jax.ShapeDtypeStruct((n, GW), BF16)] * 3 + [jax.ShapeDtypeStruct((4, GW), F32)] * 3,
        compiler_params=_params(("parallel",), 48 * 2**20),
    )(proj, proj, proj, cw, cw, cw, dq, dk, dv)


def _tri(lower_incl):
    i = lax.broadcasted_iota(jnp.int32, (CH, CH), 0)
    j = lax.broadcasted_iota(jnp.int32, (CH, CH), 1)
    return jnp.where(i >= j, 1.0, 0.0) if lower_incl else jnp.where(j >= i, 1.0, 0.0)


def _lane(shape):
    return lax.broadcasted_iota(jnp.int32, shape, 1)


def _prep_bg(proj, ad):
    n = proj.shape[0]
    nch = n // CH

    def body(p_ref, ad_ref, bg_ref, bgt_ref):
        p = p_ref[...]
        lane = _lane(p.shape)
        beta = _sigmoid(p)
        xa = p + ad_ref[1:2, :]
        sp = jnp.maximum(xa, 0.0) + jnp.log(1.0 + jnp.exp(-jnp.abs(xa)))
        g = -jnp.exp(ad_ref[0:1, :]) * sp
        gc = _dot(_tri(True), g, NN, exact=True)
        bg = jnp.where(lane < HEADS, beta, jnp.where(lane < 2 * HEADS, gc, 0.0))
        bg_ref[...] = bg
        bgt_ref[0] = bg.T

    return pl.pallas_call(
        body, name="prep_bg", grid=(nch,),
        in_specs=[pl.BlockSpec((CH, DH), lambda i: (i, BAB)), pl.BlockSpec((2, DH), lambda i: (0, 0))],
        out_specs=[pl.BlockSpec((CH, DH), lambda i: (i, 0)), pl.BlockSpec((1, DH, CH), lambda i: (i, 0, 0))],
        out_shape=[jax.ShapeDtypeStruct((n, DH), F32), jax.ShapeDtypeStruct((nch, DH, CH), F32)],
        compiler_params=_params(("parallel",)),
    )(proj, ad)


def _prep_bg_bwd(proj, ad, dbg):
    n = proj.shape[0]
    nch = n // CH

    def body(p_ref, ad_ref, d_ref, o_ref, ga_ref, gd_ref):
        p = p_ref[...]
        d = d_ref[...]
        lane = _lane(p.shape)
        beta = _sigmoid(p)
        xa = p + ad_ref[1:2, :]
        sp = jnp.maximum(xa, 0.0) + jnp.log(1.0 + jnp.exp(-jnp.abs(xa)))
        na = -jnp.exp(ad_ref[0:1, :])
        dg = _dot(_tri(False), d, NN, exact=True)
        da = dg * na * _sigmoid(xa)
        is_g = (lane >= HEADS) & (lane < 2 * HEADS)
        o_ref[...] = jnp.where(lane < HEADS, d * beta * (1.0 - beta), jnp.where(is_g, da, 0.0)).astype(BF16)
        ga = jnp.sum(jnp.where(is_g, dg * na * sp, 0.0), axis=0, keepdims=True)
        gd = jnp.sum(jnp.where(is_g, da, 0.0), axis=0, keepdims=True)

        @pl.when(pl.program_id(0) == 0)
        def _():
            ga_ref[...] = jnp.zeros_like(ga_ref)
            gd_ref[...] = jnp.zeros_like(gd_ref)

        ga_ref[...] += ga
        gd_ref[...] += gd

    one = pl.BlockSpec((1, DH), lambda i: (0, 0))
    return pl.pallas_call(
        body, name="prep_bg_bwd", grid=(nch,),
        in_specs=[pl.BlockSpec((CH, DH), lambda i: (i, BAB)), pl.BlockSpec((2, DH), lambda i: (0, 0)),
                  pl.BlockSpec((CH, DH), lambda i: (i, 0))],
        out_specs=[pl.BlockSpec((CH, DH), lambda i: (i, 0)), one, one],
        out_shape=[jax.ShapeDtypeStruct((n, DH), BF16), jax.ShapeDtypeStruct((1, DH), F32),
                   jax.ShapeDtypeStruct((1, DH), F32)],
        compiler_params=_params(("arbitrary",)),
    )(proj, ad, dbg)


def _gdn_out(o, proj, wg):
    n = o.shape[0]

    def body(o_ref, z_ref, w_ref, y_ref):
        ov, z = o_ref[...], z_ref[...]
        r = lax.rsqrt(jnp.mean(ov * ov, axis=-1, keepdims=True) + EPS)
        y_ref[...] = (ov * r * w_ref[...] * (z * _sigmoid(z))).astype(BF16)

    return pl.pallas_call(
        body, name="gdn_out", grid=(HEADS,),
        in_specs=[pl.BlockSpec((n, DH), lambda h: (0, h)), pl.BlockSpec((n, DH), lambda h: (0, ZB + h)),
                  pl.BlockSpec((1, DH), lambda h: (0, 0))],
        out_specs=pl.BlockSpec((n, DH), lambda h: (0, h)),
        out_shape=jax.ShapeDtypeStruct((n, GW), BF16),
        compiler_params=_params(("parallel",)),
    )(o, proj, wg)


def _gdn_out_bwd(o, proj, wg, dmix):
    n = o.shape[0]

    def body(o_ref, z_ref, w_ref, d_ref, do_ref, dz_ref, gw_ref):
        ov, z, d, w = o_ref[...], z_ref[...], d_ref[...], w_ref[...]
        r = lax.rsqrt(jnp.mean(ov * ov, axis=-1, keepdims=True) + EPS)
        nrm = ov * r
        s = _sigmoid(z)
        dz_ref[...] = (d * (nrm * w) * _dsilu(z, s)).astype(BF16)
        dn_w = d * (z * s)
        gw = jnp.sum(dn_w * nrm, axis=0, keepdims=True)
        dn = dn_w * w
        do_ref[...] = r * (dn - nrm * jnp.mean(dn * nrm, axis=-1, keepdims=True))

        @pl.when(pl.program_id(0) == 0)
        def _():
            gw_ref[...] = jnp.zeros_like(gw_ref)

        gw_ref[...] += gw

    return pl.pallas_call(
        body, name="gdn_out_bwd", grid=(HEADS,),
        in_specs=[pl.BlockSpec((n, DH), lambda h: (0, h)), pl.BlockSpec((n, DH), lambda h: (0, ZB + h)),
                  pl.BlockSpec((1, DH), lambda h: (0, 0)), pl.BlockSpec((n, DH), lambda h: (0, h))],
        out_specs=[pl.BlockSpec((n, DH), lambda h: (0, h)), pl.BlockSpec((n, DH), lambda h: (0, h)),
                   pl.BlockSpec((1, DH), lambda h: (0, 0))],
        out_shape=[jax.ShapeDtypeStruct((n, GW), F32), jax.ShapeDtypeStruct((n, GW), BF16),
                   jax.ShapeDtypeStruct((1, DH), F32)],
        compiler_params=_params(("arbitrary",)),
    )(o, proj, wg, dmix)


def _conv_branch(proj, w3, b):
    n = proj.shape[0]

    def body(gb_ref, gc_ref, hc_ref, zc_ref, w_ref, b_ref, y_ref):
        u = gc_ref[...] * hc_ref[...]
        cc = _conv_silu(u, w_ref, 3) + b_ref[...]
        z = zc_ref[...]
        y_ref[...] = (gb_ref[...] * cc * (z * _sigmoid(z))).astype(BF16)

    col = lambda base: pl.BlockSpec((n, DH), lambda h: (0, base + h))
    return pl.pallas_call(
        body, name="conv_branch", grid=(HEADS,),
        in_specs=[col(GBB), col(GCB), col(HCB), col(ZCB), pl.BlockSpec((3, DH), lambda h: (0, h)),
                  pl.BlockSpec((1, DH), lambda h: (0, h))],
        out_specs=col(0), out_shape=jax.ShapeDtypeStruct((n, GW), BF16),
        compiler_params=_params(("parallel",), 40 * 2**20),
    )(proj, proj, proj, proj, w3, b)


def _conv_branch_bwd(proj, w3, b, dmix):
    n = proj.shape[0]

    def body(gb_ref, gc_ref, hc_ref, zc_ref, w_ref, b_ref, d_ref, dgb, dgc, dhc, dzc, gw_ref, gbias_ref):
        gb, gcv, hc, z, d = gb_ref[...], gc_ref[...], hc_ref[...], zc_ref[...], d_ref[...]
        u = gcv * hc
        cc = _conv_silu(u, w_ref, 3) + b_ref[...]
        s = _sigmoid(z)
        dzc[...] = (d * (gb * cc) * _dsilu(z, s)).astype(BF16)
        dp = d * (z * s)
        dgb[...] = (dp * cc).astype(BF16)
        dcc = dp * gb
        gbias_ref[...] = jnp.sum(dcc, axis=0, keepdims=True)
        du = None
        for j in range(3):
            gw_ref[j:j + 1, :] = jnp.sum(dcc * _shift_down(u, 2 - j), axis=0, keepdims=True)
            t = _shift_up(dcc, 2 - j) * w_ref[j:j + 1, :]
            du = t if du is None else du + t
        dgc[...] = (du * hc).astype(BF16)
        dhc[...] = (du * gcv).astype(BF16)

    col = lambda base: pl.BlockSpec((n, DH), lambda h: (0, base + h))
    big = jax.ShapeDtypeStruct((n, GW), BF16)
    return pl.pallas_call(
        body, name="conv_branch_bwd", grid=(HEADS,),
        in_specs=[col(GBB), col(GCB), col(HCB), col(ZCB), pl.BlockSpec((3, DH), lambda h: (0, h)),
                  pl.BlockSpec((1, DH), lambda h: (0, h)), col(GW // DH)],
        out_specs=[col(0)] * 4 + [pl.BlockSpec((3, DH), lambda h: (0, h)), pl.BlockSpec((1, DH), lambda h: (0, h))],
        out_shape=[big] * 4 + [jax.ShapeDtypeStruct((3, GW), F32), jax.ShapeDtypeStruct((1, GW), F32)],
        compiler_params=_params(("parallel",), 48 * 2**20),
    )(proj, proj, proj, proj, w3, b, dmix)


def _final_loss(out, tgt, wf):
    n, d = out.shape
    tr = min(256, n)

    def body(o_ref, t_ref, w_ref, do_ref, dob_ref, gw_ref, loss_ref):
        ov, w = o_ref[...], w_ref[...]
        r = lax.rsqrt(jnp.mean(ov * ov, axis=-1, keepdims=True) + EPS)
        nrm = ov * r
        e = nrm * w - t_ref[...]
        dy = e * (1.0 / d)
        dn = dy * w
        dout = r * (dn - nrm * jnp.mean(dn * nrm, axis=-1, keepdims=True))
        do_ref[...] = dout
        dob_ref[...] = dout.astype(BF16)

        @pl.when(pl.program_id(0) == 0)
        def _():
            gw_ref[...] = jnp.zeros_like(gw_ref)
            loss_ref[...] = jnp.zeros_like(loss_ref)

        gw_ref[...] += jnp.sum(dy * nrm, axis=0, keepdims=True)
        loss_ref[...] += (0.5 / d) * jnp.sum(jnp.sum(e * e, axis=-1, keepdims=True), axis=0, keepdims=True)

    row = pl.BlockSpec((tr, d), lambda i: (i, 0))
    return pl.pallas_call(
        body, name="final_loss", grid=(n // tr,),
        in_specs=[row, row, pl.BlockSpec((1, d), lambda i: (0, 0))],
        out_specs=[row, row, pl.BlockSpec((1, d), lambda i: (0, 0)), pl.BlockSpec((1, 1), lambda i: (0, 0))],
        out_shape=[jax.ShapeDtypeStruct((n, d), F32), jax.ShapeDtypeStruct((n, d), BF16),
                   jax.ShapeDtypeStruct((1, d), F32), jax.ShapeDtypeStruct((1, 1), F32)],
        compiler_params=_params(("arbitrary",)),
    )(out, tgt, wf)


def _rms_in_bwd(x, w, dh, dout):
    n, d = x.shape
    tr = min(256, n)

    def body(x_ref, w_ref, dh_ref, do_ref, dx_ref, gw_ref):
        xv, dhv = x_ref[...], dh_ref[...]
        r = lax.rsqrt(jnp.mean(xv * xv, axis=-1, keepdims=True) + EPS)
        xn = xv * r
        dxn = dhv * w_ref[...]
        dx_ref[...] = r * (dxn - xn * jnp.mean(dxn * xn, axis=-1, keepdims=True)) + do_ref[...]

        @pl.when(pl.program_id(0) == 0)
        def _():
            gw_ref[...] = jnp.zeros_like(gw_ref)

        gw_ref[...] += jnp.sum(dhv * xn, axis=0, keepdims=True)

    row = pl.BlockSpec((tr, d), lambda i: (i, 0))
    one = pl.BlockSpec((1, d), lambda i: (0, 0))
    return pl.pallas_call(
        body, name="rms_in_bwd", grid=(n // tr,),
        in_specs=[row, one, row, row], out_specs=[row, one],
        out_shape=[jax.ShapeDtypeStruct((n, d), F32), jax.ShapeDtypeStruct((1, d), F32)],
        compiler_params=_params(("arbitrary",)),
    )(x, w, dh, dout)


def _ij():
    i = lax.broadcasted_iota(jnp.int32, (CH, CH), 0)
    j = lax.broadcasted_iota(jnp.int32, (CH, CH), 1)
    return i, j


def _unit_lower_inverse(a):
    i, j = _ij()
    eye = jnp.where(i == j, 1.0, 0.0)
    same16 = (i // 16) == (j // 16)
    same32 = (i // 32) == (j // 32)
    n1 = jnp.where(same16, -a, 0.0)
    n2 = _dot(n1, n1, exact=True)
    n4 = _dot(n2, n2, exact=True)
    n8 = _dot(n4, n4, exact=True)
    t = eye + n1 + n2 + _dot(n1, n2, exact=True)
    t = t + _dot(t, n4, exact=True)
    t = t + _dot(t, n8, exact=True)
    a1 = jnp.where(same32 & jnp.logical_not(same16), a, 0.0)
    t = t - _dot(t, _dot(a1, t, exact=True), exact=True)
    a2 = jnp.where(same32, 0.0, a)
    t = t - _dot(t, _dot(a2, t, exact=True), exact=True)
    return t


def _head_vectors(bg, bgt, h):
    bcol = bg[:, h:h + 1]
    gcol = bg[:, HEADS + h:HEADS + h + 1]
    grow = bgt[HEADS + h:HEADS + h + 1, :]
    return bcol, gcol, grow


def _decay(gcol, grow):
    i, j = _ij()
    return jnp.where(i >= j, jnp.exp(jnp.where(i >= j, gcol - grow, 0.0)), 0.0)


def _gdn_intra(q, k, v, bg, bgt):
    n = q.shape[0]
    nch = n // CH

    def body(q_ref, k_ref, v_ref, bg_ref, bgt_ref, u_ref, w_ref, p_ref, t_ref):
        bg, bgt = bg_ref[...], bgt_ref[0]
        i, j = _ij()
        for h in range(HEADS):
            sl = slice(h * DH, (h + 1) * DH)
            qh, kh, vh = q_ref[:, sl], k_ref[:, sl], v_ref[:, sl]
            bcol, gcol, grow = _head_vectors(bg, bgt, h)
            dec = _decay(gcol, grow)
            kk = _dot(kh, kh, NT, exact=True)
            a = jnp.where(i > j, bcol * kk * dec, 0.0)
            t = _unit_lower_inverse(a)
            p_ref[0, h] = _dot(qh, kh, NT, exact=True) * dec
            t_ref[0, h] = t
            u_ref[:, sl] = _dot(t, vh * bcol, exact=True)
            w_ref[:, sl] = _dot(t, kh * (bcol * jnp.exp(gcol)), exact=True)

    row = pl.BlockSpec((CH, GW), lambda c: (c, 0))
    sq = pl.BlockSpec((1, HEADS, CH, CH), lambda c: (c, 0, 0, 0))
    big = jax.ShapeDtypeStruct((n, GW), F32)
    sqs = jax.ShapeDtypeStruct((nch, HEADS, CH, CH), F32)
    return pl.pallas_call(
        body, name="gdn_intra", grid=(nch,),
        in_specs=[row, row, row, pl.BlockSpec((CH, DH), lambda c: (c, 0)),
                  pl.BlockSpec((1, DH, CH), lambda c: (c, 0, 0))],
        out_specs=[row, row, sq, sq], out_shape=[big, big, sqs, sqs],
        compiler_params=_params(("parallel",)),
    )(q, k, v, bg, bgt)


def _gdn_scan(q, k, bg, u, w, p):
    n = q.shape[0]
    nch = n // CH

    def body(q_ref, k_ref, bg_ref, u_ref, w_ref, p_ref, o_ref, vn_ref, s_out, s_scr):
        @pl.when(pl.program_id(0) == 0)
        def _():
            s_scr[...] = jnp.zeros_like(s_scr)

        bg = bg_ref[...]
        for h in range(HEADS):
            sl = slice(h * DH, (h + 1) * DH)
            gcol = bg[:, HEADS + h:HEADS + h + 1]
            glast = gcol[CH - 1:CH, :]
            s = s_scr[h]
            s_out[0, :, sl] = s
            vn = u_ref[:, sl] - _dot(w_ref[:, sl], s, exact=True)
            vn_ref[:, sl] = vn
            o_ref[:, sl] = _dot(q_ref[:, sl] * jnp.exp(gcol), s, exact=True) + _dot(p_ref[0, h], vn, exact=True)
            ks = k_ref[:, sl] * jnp.exp(glast - gcol)
            s_scr[h] = s * jnp.exp(glast) + _dot(ks, vn, TN, exact=True)

    row = pl.BlockSpec((CH, GW), lambda c: (c, 0))
    big = jax.ShapeDtypeStruct((n, GW), F32)
    return pl.pallas_call(
        body, name="gdn_scan", grid=(nch,),
        in_specs=[row, row, pl.BlockSpec((CH, DH), lambda c: (c, 0)), row, row,
                  pl.BlockSpec((1, HEADS, CH, CH), lambda c: (c, 0, 0, 0))],
        out_specs=[row, row, pl.BlockSpec((1, DH, GW), lambda c: (c, 0, 0))],
        out_shape=[big, big, jax.ShapeDtypeStruct((nch, DH, GW), F32)],
        scratch_shapes=[pltpu.VMEM((HEADS, DH, DH), F32)],
        compiler_params=_params(("arbitrary",)),
    )(q, k, bg, u, w, p)


def _gdn_scan_bwd(q, k, bg, w, p, vn, s_in, do):
    n = q.shape[0]
    nch = n // CH
    rev = lambda c: nch - 1 - c

    def body(q_ref, k_ref, bg_ref, w_ref, p_ref, vn_ref, s_ref, do_ref,
             dqg_ref, dp_ref, du_ref, dw_ref, dks_ref, dgam_ref, ds_scr):
        @pl.when(pl.program_id(0) == 0)
        def _():
            ds_scr[...] = jnp.zeros_like(ds_scr)

        bg = bg_ref[...]
        lane = _lane((1, DH))
        dgam = jnp.zeros((1, DH), F32)
        for h in range(HEADS):
            sl = slice(h * DH, (h + 1) * DH)
            gcol = bg[:, HEADS + h:HEADS + h + 1]
            glast = gcol[CH - 1:CH, :]
            s = s_ref[0, :, sl]
            ds = ds_scr[h]
            dov, vnh, wh = do_ref[:, sl], vn_ref[:, sl], w_ref[:, sl]
            qg = q_ref[:, sl] * jnp.exp(gcol)
            ks = k_ref[:, sl] * jnp.exp(glast - gcol)
            dqg_ref[:, sl] = _dot(dov, s, NT, exact=True)
            dp_ref[0, h] = _dot(dov, vnh, NT, exact=True)
            dvn = _dot(p_ref[0, h], dov, TN, exact=True) + _dot(ks, ds, exact=True)
            du_ref[:, sl] = dvn
            dw_ref[:, sl] = -_dot(dvn, s, NT, exact=True)
            dks_ref[:, sl] = _dot(vnh, ds, NT, exact=True)
            tot = jnp.sum(jnp.sum(ds * s, axis=-1, keepdims=True), axis=0, keepdims=True)
            dgam = dgam + jnp.where(lane == h, tot, 0.0)
            ds_scr[h] = ds * jnp.exp(glast) + _dot(qg, dov, TN, exact=True) - _dot(wh, dvn, TN, exact=True)
        dgam_ref[0] = jnp.broadcast_to(dgam, (8, DH))

    row = pl.BlockSpec((CH, GW), lambda c: (rev(c), 0))
    sq = pl.BlockSpec((1, HEADS, CH, CH), lambda c: (rev(c), 0, 0, 0))
    big = jax.ShapeDtypeStruct((n, GW), F32)
    return pl.pallas_call(
        body, name="gdn_scan_bwd", grid=(nch,),
        in_specs=[row, row, pl.BlockSpec((CH, DH), lambda c: (rev(c), 0)), row, sq, row,
                  pl.BlockSpec((1, DH, GW), lambda c: (rev(c), 0, 0)), row],
        out_specs=[row, sq, row, row, row, pl.BlockSpec((1, 8, DH), lambda c: (rev(c), 0, 0))],
        out_shape=[big, jax.ShapeDtypeStruct((nch, HEADS, CH, CH), F32), big, big, big,
                   jax.ShapeDtypeStruct((nch, 8, DH), F32)],
        scratch_shapes=[pltpu.VMEM((HEADS, DH, DH), F32)],
        compiler_params=_params(("arbitrary",)),
    )(q, k, bg, w, p, vn, s_in, do)


def _gdn_intra_bwd(q, k, v, bg, bgt, t, u, w, p, dqg, dp, du, dw, dks, dgam):
    n = q.shape[0]
    nch = n // CH

    def body(q_ref, k_ref, v_ref, bg_ref, bgt_ref, t_ref, u_ref, w_ref, p_ref,
             dqg_ref, dp_ref, du_ref, dw_ref, dks_ref, dgam_ref, dq_ref, dk_ref, dv_ref, dbg_ref):
        bg, bgt = bg_ref[...], bgt_ref[0]
        dgam_all = dgam_ref[0]
        i, j = _ij()
        rows1 = lax.broadcasted_iota(jnp.int32, (CH, 1), 0)
        lane = _lane((CH, DH))
        dbg = jnp.zeros((CH, DH), F32)
        rsum = lambda x: jnp.sum(x, axis=-1, keepdims=True)
        for h in range(HEADS):
            sl = slice(h * DH, (h + 1) * DH)
            qh, kh, vh = q_ref[:, sl], k_ref[:, sl], v_ref[:, sl]
            bcol, gcol, grow = _head_vectors(bg, bgt, h)
            dec = _decay(gcol, grow)
            gam = jnp.exp(gcol)
            glast = gcol[CH - 1:CH, :]
            e = jnp.exp(glast - gcol)
            th = t_ref[0, h]
            kg = kh * gam
            dru = _dot(th, du_ref[:, sl], TN, exact=True)
            drw = _dot(th, dw_ref[:, sl], TN, exact=True)
            da = -(_dot(dru, u_ref[:, sl], NT, exact=True) + _dot(drw, w_ref[:, sl], NT, exact=True))
            da = jnp.where(i > j, da, 0.0)
            dv_ref[:, sl] = bcol * dru
            kk = _dot(kh, kh, NT, exact=True)
            dbeta = rsum(dru * vh) + rsum(drw * kg) + rsum(da * kk * dec)
            dgc = rsum(drw * kg) * bcol
            dkk = da * bcol * dec
            dqk = dp_ref[0, h] * dec
            dq_p = _dot(dqk, kh, exact=True)
            dk_p = _dot(dqk, qh, TN, exact=True)
            dk_a = _dot(dkk, kh, exact=True)
            dk_b = _dot(dkk, kh, TN, exact=True)
            dqg = dqg_ref[:, sl]
            dksh = dks_ref[:, sl]
            dq_ref[:, sl] = gam * dqg + dq_p
            dk_ref[:, sl] = (bcol * gam) * drw + dk_p + dk_a + dk_b + dksh * e
            tk = rsum(dksh * kh) * e
            dgc = dgc + rsum(qh * dq_p) + rsum(kh * dk_a) - rsum(kh * dk_p) - rsum(kh * dk_b)
            dgc = dgc + rsum(dqg * qh) * gam - tk
            dglast = jnp.sum(tk, axis=0, keepdims=True) + dgam_all[0:1, h:h + 1] * jnp.exp(glast)
            dgc = dgc + jnp.where(rows1 == CH - 1, dglast, 0.0)
            dbg = dbg + jnp.where(lane == h, dbeta, 0.0) + jnp.where(lane == HEADS + h, dgc, 0.0)
        dbg_ref[...] = dbg

    row = pl.BlockSpec((CH, GW), lambda c: (c, 0))
    sq = pl.BlockSpec((1, HEADS, CH, CH), lambda c: (c, 0, 0, 0))
    small = pl.BlockSpec((CH, DH), lambda c: (c, 0))
    big = jax.ShapeDtypeStruct((n, GW), F32)
    return pl.pallas_call(
        body, name="gdn_intra_bwd", grid=(nch,),
        in_specs=[row, row, row, small, pl.BlockSpec((1, DH, CH), lambda c: (c, 0, 0)), sq, row, row, sq,
                  row, sq, row, row, row, pl.BlockSpec((1, 8, DH), lambda c: (c, 0, 0))],
        out_specs=[row, row, row, small],
        out_shape=[big, big, big, jax.ShapeDtypeStruct((n, DH), F32)],
        compiler_params=_params(("parallel",)),
    )(q, k, v, bg, bgt, t, u, w, p, dqg, dp, du, dw, dks, dgam)


def _local_step(x, tgt, w_cat, w_out, norm_in_w, cqw, ad, gdn_norm_w, conv_w, conv_b, final_norm_w):
    h = _rms_in(x, norm_in_w)
    proj = _matmul(h, w_cat, NN, F32, 512, 640, 1024, "mm_proj")
    q, k, v = _prep_qkv(proj, cqw)
    bg, bgt = _prep_bg(proj, ad)
    u, w, p, t = _gdn_intra(q, k, v, bg, bgt)
    o, vn, s_in = _gdn_scan(q, k, bg, u, w, p)
    mix = jnp.concatenate([_gdn_out(o, proj, gdn_norm_w), _conv_branch(proj, conv_w, conv_b)], axis=1)
    out = _matmul(mix, w_out, NN, F32, 512, 512, 2048, "mm_out", add=x)
    dout, dout_b, g_fn, loss = _final_loss(out, tgt, final_norm_w)

    dmix = _matmul(dout_b, w_out, NT, F32, 512, 1024, 1024, "mm_dmix")
    g_wout = _matmul(mix, dout_b, TN, BF16, 512, 512, 2048, "mm_gwout")
    do, dz, g_gn = _gdn_out_bwd(o, proj, gdn_norm_w, dmix)
    dgb, dgc, dhc, dzc, g_cw, g_cb = _conv_branch_bwd(proj, conv_w, conv_b, dmix)
    dqg, dp, du, dw, dks, dgam = _gdn_scan_bwd(q, k, bg, w, p, vn, s_in, do)
    dq, dk, dv, dbg = _gdn_intra_bwd(q, k, v, bg, bgt, t, u, w, p, dqg, dp, du, dw, dks, dgam)
    dpq, dpk, dpv, gq, gk, gv = _prep_qkv_bwd(proj, cqw, dq, dk, dv)
    dba, g_al, g_dt = _prep_bg_bwd(proj, ad, dbg)
    dproj = jnp.concatenate([dpq, dpk, dpv, dz, dba, dgb, dgc, dhc, dzc], axis=1)
    g_wcat = _matmul(h, dproj, TN, BF16, 512, 640, 2048, "mm_gwin")
    dh = _matmul(dproj, w_cat, NT, F32, 512, 1024, 640, "mm_dh")
    gx, g_nin = _rms_in_bwd(x, norm_in_w, dh, dout)
    g_cqw = jnp.concatenate([gq, gk, gv], axis=1)
    return gx, g_wcat, g_wout, dict(nin=g_nin, cb=g_cb, fn=g_fn, al=g_al, dt=g_dt, gn=g_gn, cqw=g_cqw,
                                    cw=g_cw, loss=loss)


def _place():
    x, y, c = lax.axis_index("x"), lax.axis_index("y"), lax.axis_index("c")
    chips = [(1 - x, y), (x, 1 - y), (1 - x, 1 - y)]
    return x, y, c, chips


def _all_gather_shards(shards):
    nsh = len(shards)

    def body(*refs):
        ins, outs = refs[:nsh], refs[nsh:2 * nsh]
        send_sems, recv_sems, loc_sems = refs[2 * nsh:]
        x, y, c, chips = _place()
        mine = 2 * x + y
        local = [pltpu.make_async_copy(ins[a], outs[a].at[mine], loc_sems.at[a]) for a in range(nsh)]
        for cp in local:
            cp.start()
        remote = []
        for jj, (px, py) in enumerate(chips):
            for a in range(nsh):
                remote.append(pltpu.make_async_remote_copy(
                    src_ref=ins[a], dst_ref=outs[a].at[mine],
                    send_sem=send_sems.at[jj * nsh + a], recv_sem=recv_sems.at[jj * nsh + a],
                    device_id=(px, py, c), device_id_type=MESH))
        for cp in remote:
            cp.start()
        for cp in remote:
            cp.wait_recv()
        for cp in remote:
            cp.wait_send()
        for cp in local:
            cp.wait()

    return pl.pallas_call(
        body, name="all_gather_shards",
        in_specs=[ANY] * nsh, out_specs=[ANY] * nsh,
        out_shape=[jax.ShapeDtypeStruct((4,) + s.shape, s.dtype) for s in shards],
        scratch_shapes=[pltpu.SemaphoreType.DMA((3 * nsh,)), pltpu.SemaphoreType.DMA((3 * nsh,)),
                        pltpu.SemaphoreType.DMA((nsh,))],
    )(*shards)


def _exchange_grads(blocks, pack):
    nb = len(blocks)

    def body(*refs):
        ins, pack_ref = refs[:nb], refs[nb]
        lands, packs = refs[nb + 1:2 * nb + 1], refs[2 * nb + 1]
        send_sems, recv_sems, psend, precv, loc_sems = refs[2 * nb + 2:]
        x, y, c, chips = _place()
        me = 4 * x + 2 * y + c
        local = [pltpu.make_async_copy(pack_ref, packs.at[me], loc_sems.at[nb])]
        local += [pltpu.make_async_copy(ins[a].at[2 * x + y], lands[a].at[3], loc_sems.at[a]) for a in range(nb)]
        for cp in local:
            cp.start()
        remote = []
        for jj, (px, py) in enumerate(chips):
            for a in range(nb):
                remote.append(pltpu.make_async_remote_copy(
                    src_ref=ins[a].at[2 * px + py], dst_ref=lands[a].at[jj],
                    send_sem=send_sems.at[jj * nb + a], recv_sem=recv_sems.at[jj * nb + a],
                    device_id=(px, py, c), device_id_type=MESH))
        for r in range(1, 8):
            dx, dy, dc = (r >> 2) & 1, (r >> 1) & 1, r & 1
            peer = (x + dx - 2 * x * dx, y + dy - 2 * y * dy, c + dc - 2 * c * dc)
            remote.append(pltpu.make_async_remote_copy(
                src_ref=pack_ref, dst_ref=packs.at[me], send_sem=psend.at[r - 1], recv_sem=precv.at[r - 1],
                device_id=peer, device_id_type=MESH))
        for cp in remote:
            cp.start()
        for cp in remote:
            cp.wait_recv()
        for cp in remote:
            cp.wait_send()
        for cp in local:
            cp.wait()

    out_shape = [jax.ShapeDtypeStruct((4,) + b.shape[1:], b.dtype) for b in blocks]
    out_shape.append(jax.ShapeDtypeStruct((8,) + pack.shape, pack.dtype))
    return pl.pallas_call(
        body, name="exchange_grads",
        in_specs=[ANY] * (nb + 1), out_specs=[ANY] * (nb + 1), out_shape=out_shape,
        scratch_shapes=[pltpu.SemaphoreType.DMA((3 * nb,)), pltpu.SemaphoreType.DMA((3 * nb,)),
                        pltpu.SemaphoreType.DMA((7,)), pltpu.SemaphoreType.DMA((7,)), pltpu.SemaphoreType.DMA((nb + 1,))],
    )(*blocks, pack)


def _swap_with_sibling(parts):
    npart = len(parts)

    def body(*refs):
        ins, outs = refs[:npart], refs[npart:2 * npart]
        send_sems, recv_sems = refs[2 * npart:]
        x, y, c, _ = _place()
        cps = [pltpu.make_async_remote_copy(
            src_ref=ins[a], dst_ref=outs[a], send_sem=send_sems.at[a], recv_sem=recv_sems.at[a],
            device_id=(x, y, 1 - c), device_id_type=MESH) for a in range(npart)]
        for cp in cps:
            cp.start()
        for cp in cps:
            cp.wait_recv()
        for cp in cps:
            cp.wait_send()

    return pl.pallas_call(
        body, name="swap_with_sibling",
        in_specs=[ANY] * npart, out_specs=[ANY] * npart,
        out_shape=[jax.ShapeDtypeStruct(p.shape, p.dtype) for p in parts],
        scratch_shapes=[pltpu.SemaphoreType.DMA((npart,)), pltpu.SemaphoreType.DMA((npart,))],
    )(*parts)


def _sum_blocks(land, rows, name):
    _, r, cdim = land.shape
    rows = min(rows, r)

    def body(land_ref, o_ref):
        acc = land_ref[3].astype(F32)
        for jj in range(3):
            acc = acc + land_ref[jj].astype(F32)
        o_ref[...] = acc

    return pl.pallas_call(
        body, name=name, grid=(r // rows,),
        in_specs=[pl.BlockSpec((4, rows, cdim), lambda i: (0, i, 0))],
        out_specs=pl.BlockSpec((rows, cdim), lambda i: (i, 0)),
        out_shape=jax.ShapeDtypeStruct((r, cdim), F32),
        compiler_params=_params(("parallel",), 40 * 2**20),
    )(land)


def _sum_packs(packs):
    def body(p_ref, o_ref):
        acc = p_ref[0]
        for d in range(1, 8):
            acc = acc + p_ref[d]
        o_ref[...] = acc

    return pl.pallas_call(
        body, name="sum_packs", out_shape=jax.ShapeDtypeStruct(packs.shape[1:], F32),
    )(packs)


def _adamw(w, m, v, g1, g2, rows, name):
    r, cdim = w.shape
    rows = min(rows, r)
    c1 = 1.0 / (1.0 - ADAM_B1 ** ADAM_STEP)
    c2 = 1.0 / (1.0 - ADAM_B2 ** ADAM_STEP)

    def body(*refs):
        if g2 is None:
            w_ref, m_ref, v_ref, g_ref, go, do, mo, vo = refs
            g = g_ref[...]
        else:
            w_ref, m_ref, v_ref, g_ref, g2_ref, go, do, mo, vo = refs
            g = g_ref[...] + g2_ref[...]
        mn = ADAM_B1 * m_ref[...] + (1.0 - ADAM_B1) * g
        vn = ADAM_B2 * v_ref[...] + (1.0 - ADAM_B2) * (g * g)
        go[...] = g
        mo[...] = mn
        vo[...] = vn
        do[...] = -ADAM_LR * ((mn * c1) / (jnp.sqrt(vn * c2) + ADAM_EPS) + ADAM_WD * w_ref[...])

    blk = pl.BlockSpec((rows, cdim), lambda i: (i, 0))
    args = [w, m, v, g1] + ([] if g2 is None else [g2])
    shp = jax.ShapeDtypeStruct((r, cdim), F32)
    return pl.pallas_call(
        body, name=name, grid=(r // rows,),
        in_specs=[blk] * len(args), out_specs=[blk] * 4, out_shape=[shp] * 4,
        compiler_params=_params(("parallel",), 18 * rows * cdim * 4 + 8 * 2**20),
    )(*args)


def _pad_lanes(a, width):
    return jnp.pad(a, ((0, 0), (0, width - a.shape[1])))


def _cat_layout(w_full):
    return jnp.concatenate([w_full[:, :4096], _pad_lanes(w_full[:, 4096:4112], DH), w_full[:, 4112:]], axis=1)


def _uncat_layout(g_cat):
    return jnp.concatenate([g_cat[:, :4096], g_cat[:, 4096:4112], g_cat[:, 4096 + DH:]], axis=1)


def _gathered_to_full(g):
    return jnp.transpose(g, (1, 0, 2)).reshape(g.shape[1], 4 * g.shape[2])


def _full_to_blocks(a):
    r, c4 = a.shape
    return jnp.transpose(a.reshape(r, 4, c4 // 4), (1, 0, 2))


def _row(a):
    return _pad_lanes(a.reshape(1, -1), 1024)


def _small_pack(nin, cb, fn, al, dt, gn, cqw_shard, cw_shard):
    ad = jnp.concatenate([al.reshape(1, -1), dt.reshape(1, -1)], axis=1)
    rows = [_row(nin), _row(cb), _row(fn), _row(ad), _row(gn), cqw_shard.reshape(3, 1024), _row(cw_shard)]
    out = jnp.concatenate(rows, axis=0)
    return jnp.pad(out, ((0, 16 - out.shape[0]), (0, 0)))


def kernel(x, norm_in_w, w_in, conv_qkv_w, A_log, dt_bias, gdn_norm_w, conv_w, conv_b, w_out, final_norm_w, loss_target, m_norm_in_w, m_w_in, m_conv_qkv_w, m_A_log, m_dt_bias, m_gdn_norm_w, m_conv_w, m_conv_b, m_w_out, m_final_norm_w, v_norm_in_w, v_w_in, v_conv_qkv_w, v_A_log, v_dt_bias, v_gdn_norm_w, v_conv_w, v_conv_b, v_w_out, v_final_norm_w):
    chip = 2 * lax.axis_index("x") + lax.axis_index("y")
    wi_b = _cast_bf16(w_in[0], 256, "cast_w_in")
    wo_b = _cast_bf16(w_out[0], 256, "cast_w_out")
    wi_g, wo_g, cq_g, cw_g = _all_gather_shards([wi_b, wo_b, conv_qkv_w[0], conv_w[0]])
    w_cat = _cat_layout(_gathered_to_full(wi_g))
    w_out_full = wo_g.reshape(2 * GW, -1)
    cqw = _gathered_to_full(cq_g)
    cw = _gathered_to_full(cw_g)
    ad = _pad_lanes(jnp.concatenate([jnp.zeros((2, HEADS), F32), jnp.concatenate([A_log, dt_bias], axis=0)], axis=1), DH)

    gx, g_wcat, g_wout, sm = _local_step(x[0], loss_target[0], w_cat, w_out_full, norm_in_w, cqw, ad,
                                         gdn_norm_w, cw, conv_b, final_norm_w.reshape(1, -1))

    ad_g = jnp.concatenate([sm["al"][:, HEADS:2 * HEADS], sm["dt"][:, HEADS:2 * HEADS]], axis=1)
    pack = jnp.concatenate([_row(sm["nin"]), _row(sm["cb"]), _row(sm["fn"]), _row(ad_g), _row(sm["gn"]),
                            sm["cqw"].reshape(12, 1024), sm["cw"], _row(sm["loss"])], axis=0)
    pack = jnp.pad(pack, ((0, PACK_ROWS - pack.shape[0]), (0, 0)))
    blk_in = _full_to_blocks(_uncat_layout(g_wcat))
    blk_out = g_wout.reshape(4, GW // 2, -1)
    land_in, land_out, packs = _exchange_grads([blk_in, blk_out], pack)
    part_in = _sum_blocks(land_in, 128, "sum_w_in")
    part_out = _sum_blocks(land_out, 128, "sum_w_out")
    sib_in, sib_out = _swap_with_sibling([part_in, part_out])
    tot = _sum_packs(packs)

    g_wi, d_wi, m_wi, v_wi = _adamw(w_in[0], m_w_in[0], v_w_in[0], part_in, sib_in, 64, "adamw_w_in")
    g_wo, d_wo, m_wo, v_wo = _adamw(w_out[0], m_w_out[0], v_w_out[0], part_out, sib_out, 128, "adamw_w_out")
    g_cq_sh = lax.dynamic_slice_in_dim(tot[R_CQ:R_CQ + 12].reshape(4, 3 * GW), chip * 768, 768, axis=1)
    g_cw_sh = lax.dynamic_slice_in_dim(tot[R_CW:R_CW + 3], chip * 256, 256, axis=1)
    sp = lambda nin, cb, fn, al, dt, gn, cq, cwv: _small_pack(nin, cb, fn, al, dt, gn, cq[0], cwv[0])
    g_s = _small_pack(tot[R_NIN], tot[R_CB], tot[R_FN], tot[R_AD, :HEADS], tot[R_AD, HEADS:2 * HEADS],
                      tot[R_GN, :DH], g_cq_sh, g_cw_sh)
    w_s = sp(norm_in_w, conv_b, final_norm_w, A_log, dt_bias, gdn_norm_w, conv_qkv_w, conv_w)
    m_s = sp(m_norm_in_w, m_conv_b, m_final_norm_w, m_A_log, m_dt_bias, m_gdn_norm_w, m_conv_qkv_w, m_conv_w)
    v_s = sp(v_norm_in_w, v_conv_b, v_final_norm_w, v_A_log, v_dt_bias, v_gdn_norm_w, v_conv_qkv_w, v_conv_w)
    small = _adamw(w_s, m_s, v_s, g_s, None, 16, "adamw_small")

    def unpack(a, big_in, big_out):
        return (a[0:1], big_in[None], a[5:8].reshape(1, 4, 768), a[3:4, :HEADS], a[3:4, HEADS:2 * HEADS],
                a[4:5, :DH], a[8, :768].reshape(1, 3, 256), a[1:2], big_out[None], a[2])

    loss = tot[R_LOSS, 0]
    return (loss, gx[None], *unpack(small[0], g_wi, g_wo), *unpack(small[1], d_wi, d_wo),
            *unpack(small[2], m_wi, m_wo), *unpack(small[3], v_wi, v_wo))
```

```python
import functools
import math

import jax
import jax.numpy as jnp
from jax import lax
from jax.experimental import pallas as pl
from jax.experimental.pallas import tpu as pltpu

F32 = jnp.float32
BF16 = jnp.bfloat16
MESH = pl.DeviceIdType.MESH
ANY = pl.BlockSpec(memory_space=pl.ANY)

HEADS = 8
DH = 128
CH = 64
GW = HEADS * DH
EPS = 1e-6
VMEM_V7X = 64 * 1024 * 1024

QB, KB, VB, ZB, BAB, GBB, GCB, HCB, ZCB = 0, 8, 16, 24, 32, 33, 41, 49, 57
NPB = 65
PW = NPB * DH

ADAM_LR, ADAM_B1, ADAM_B2, ADAM_EPS, ADAM_WD, ADAM_STEP = 0.001, 0.9, 0.999, 1e-08, 0.01, 10

R_NIN, R_CB, R_FN, R_AD, R_GN, R_CQ, R_CW, R_LOSS, PACK_ROWS = 0, 1, 2, 3, 4, 5, 17, 20, 24

NN = ((1,), (0,))
NT = ((1,), (1,))
TN = ((0,), (0,))


def _dot(a, b, dims=NN, mode="lo"):
    dn = (dims, ((), ()))
    if mode == "hi":
        return lax.dot_general(a, b, dn, precision=lax.Precision.HIGHEST, preferred_element_type=F32)
    ah, bh = a.astype(BF16), b.astype(BF16)
    out = lax.dot_general(ah, bh, dn, preferred_element_type=F32)
    if mode == "x3":
        al = (a - ah.astype(F32)).astype(BF16)
        bl = (b - bh.astype(F32)).astype(BF16)
        out = out + lax.dot_general(ah, bl, dn, preferred_element_type=F32)
        out = out + lax.dot_general(al, bh, dn, preferred_element_type=F32)
    return out


P_GRAM, P_INV, P_SOL, P_SCAN, P_SCANB, P_BWD = "lo", "lo", "lo", "lo", "lo", "lo"
P_CUM = "x3"


def _params(sem=None, vmem=None):
    kw = {}
    if sem is not None:
        kw["dimension_semantics"] = sem
    if vmem is not None:
        kw["vmem_limit_bytes"] = int(min(max(vmem, 32 * 2**20), VMEM_V7X - 8 * 2**20))
    return pltpu.CompilerParams(**kw)


def _sigmoid(x):
    return 1.0 / (1.0 + jnp.exp(-x))


def _dsilu(x, s):
    return s * (1.0 + x * (1.0 - s))


def _rows(shape):
    return lax.broadcasted_iota(jnp.int32, shape, 0)


def _shift_down(x, s):
    if s == 0:
        return x
    return jnp.where(_rows(x.shape) >= s, pltpu.roll(x, s, 0), 0.0)


def _shift_up(x, s):
    if s == 0:
        return x
    n = x.shape[0]
    return jnp.where(_rows(x.shape) < n - s, pltpu.roll(x, n - s, 0), 0.0)


def _matmul(a, b, dims, out_dtype, tm, tn, tk, name, add=None):
    if dims == NN:
        (m, k), (_, n) = a.shape, b.shape
    elif dims == NT:
        (m, k), (n, _) = a.shape, b.shape
    else:
        (k, m), (_, n) = a.shape, b.shape
    tm, tn, tk = min(tm, m), min(tn, n), min(tk, k)
    assert m % tm == 0 and n % tn == 0 and k % tk == 0, (name, m, n, k, tm, tn, tk)
    nk = k // tk

    def body(*refs):
        if add is None:
            a_ref, b_ref, o_ref = refs[:3]
            add_ref = None
        else:
            a_ref, b_ref, add_ref, o_ref = refs[:4]
        part = _dot(a_ref[...], b_ref[...], dims)
        if nk == 1:
            if add_ref is not None:
                part = part + add_ref[...]
            o_ref[...] = part.astype(out_dtype)
            return
        acc = refs[-1]
        kk = pl.program_id(2)

        @pl.when(kk == 0)
        def _():
            acc[...] = part

        @pl.when(kk > 0)
        def _():
            acc[...] += part

        @pl.when(kk == nk - 1)
        def _():
            r = acc[...]
            if add_ref is not None:
                r = r + add_ref[...]
            o_ref[...] = r.astype(out_dtype)

    if dims == TN:
        a_spec = pl.BlockSpec((tk, tm), lambda i, j, kk: (kk, i))
    else:
        a_spec = pl.BlockSpec((tm, tk), lambda i, j, kk: (i, kk))
    if dims == NT:
        b_spec = pl.BlockSpec((tn, tk), lambda i, j, kk: (j, kk))
    else:
        b_spec = pl.BlockSpec((tk, tn), lambda i, j, kk: (kk, j))
    o_spec = pl.BlockSpec((tm, tn), lambda i, j, kk: (i, j))
    in_specs = [a_spec, b_spec]
    args = [a, b]
    if add is not None:
        in_specs.append(o_spec)
        args.append(add)
    osz = jnp.dtype(out_dtype).itemsize
    est = 2 * (tm * tk * a.dtype.itemsize + tk * tn * b.dtype.itemsize + tm * tn * osz)
    est += 3 * tm * tn * 4 + (2 * tm * tn * 4 if add is not None else 0)
    return pl.pallas_call(
        body, name=name, grid=(m // tm, n // tn, nk),
        in_specs=in_specs, out_specs=o_spec,
        out_shape=jax.ShapeDtypeStruct((m, n), out_dtype),
        scratch_shapes=[pltpu.VMEM((tm, tn), F32)] if nk > 1 else [],
        compiler_params=_params(("parallel", "parallel", "arbitrary"), est + 8 * 2**20),
    )(*args)


def _cast_bf16(a, rows, name):
    r, c = a.shape
    rows = min(rows, r)

    def body(a_ref, o_ref):
        o_ref[...] = a_ref[...].astype(BF16)

    return pl.pallas_call(
        body, name=name, grid=(r // rows,),
        in_specs=[pl.BlockSpec((rows, c), lambda i: (i, 0))],
        out_specs=pl.BlockSpec((rows, c), lambda i: (i, 0)),
        out_shape=jax.ShapeDtypeStruct((r, c), BF16),
        compiler_params=_params(("parallel",)),
    )(a)


def _rms_in(x, w):
    n, d = x.shape
    tr = min(256, n)

    def body(x_ref, w_ref, h_ref):
        xv = x_ref[...]
        r = lax.rsqrt(jnp.mean(xv * xv, axis=-1, keepdims=True) + EPS)
        h_ref[...] = (xv * r * w_ref[...]).astype(BF16)

    return pl.pallas_call(
        body, name="rms_in", grid=(n // tr,),
        in_specs=[pl.BlockSpec((tr, d), lambda i: (i, 0)), pl.BlockSpec((1, d), lambda i: (0, 0))],
        out_specs=pl.BlockSpec((tr, d), lambda i: (i, 0)),
        out_shape=jax.ShapeDtypeStruct((n, d), BF16),
        compiler_params=_params(("parallel",)),
    )(x, w)


def _conv_silu(p, w_ref, taps):
    c = None
    for j in range(taps):
        t = _shift_down(p, taps - 1 - j) * w_ref[j:j + 1, :]
        c = t if c is None else c + t
    return c


def _prep_qkv(proj, cw):
    n = proj.shape[0]

    def body(pq, pk, pv, wq, wk, wv, q_ref, k_ref, v_ref):
        for p_ref, w_ref, o_ref, kind in ((pq, wq, q_ref, 0), (pk, wk, k_ref, 1), (pv, wv, v_ref, 2)):
            c = _conv_silu(p_ref[...], w_ref, 4)
            a = c * _sigmoid(c)
            if kind < 2:
                r = lax.rsqrt(jnp.sum(a * a, axis=-1, keepdims=True) + EPS)
                a = a * (r * (DH ** -0.5 if kind == 0 else 1.0))
            o_ref[...] = a

    col = lambda base: pl.BlockSpec((n, DH), lambda h: (0, base + h))
    wcol = lambda base: pl.BlockSpec((4, DH), lambda h: (0, base + h))
    out = jax.ShapeDtypeStruct((n, GW), F32)
    return pl.pallas_call(
        body, name="prep_qkv", grid=(HEADS,),
        in_specs=[col(QB), col(KB), col(VB), wcol(QB), wcol(KB), wcol(VB)],
        out_specs=[col(0)] * 3, out_shape=[out] * 3,
        compiler_params=_params(("parallel",), 40 * 2**20),
    )(proj, proj, proj, cw, cw, cw)


def _prep_qkv_bwd(proj, cw, dq, dk, dv):
    n = proj.shape[0]

    def body(pq, pk, pv, wq, wk, wv, dq_ref, dk_ref, dv_ref, oq, ok, ov, gq, gk, gv):
        for p_ref, w_ref, d_ref, o_ref, g_ref, kind in (
                (pq, wq, dq_ref, oq, gq, 0), (pk, wk, dk_ref, ok, gk, 1), (pv, wv, dv_ref, ov, gv, 2)):
            p = p_ref[...]
            c = _conv_silu(p, w_ref, 4)
            s = _sigmoid(c)
            a = c * s
            d = d_ref[...]
            if kind < 2:
                r = lax.rsqrt(jnp.sum(a * a, axis=-1, keepdims=True) + EPS)
                sc = DH ** -0.5 if kind == 0 else 1.0
                d = (sc * r) * (d - a * ((r * r) * jnp.sum(d * a, axis=-1, keepdims=True)))
            dc = d * _dsilu(c, s)
            dp = None
            for j in range(4):
                g_ref[j:j + 1, :] = jnp.sum(dc * _shift_down(p, 3 - j), axis=0, keepdims=True)
                t = _shift_up(dc, 3 - j) * w_ref[j:j + 1, :]
                dp = t if dp is None else dp + t
            o_ref[...] = dp.astype(BF16)

    col = lambda base: pl.BlockSpec((n, DH), lambda h: (0, base + h))
    wcol = lambda base: pl.BlockSpec((4, DH), lambda h: (0, base + h))
    return pl.pallas_call(
        body, name="prep_qkv_bwd", grid=(HEADS,),
        in_specs=[col(QB), col(KB), col(VB), wcol(QB), wcol(KB), wcol(VB), col(0), col(0), col(0)],
        out_specs=[col(0)] * 3 + [wcol(0)] * 3,
        out_shape=[jax.ShapeDtypeStruct((n, GW), BF16)] * 3 + [jax.ShapeDtypeStruct((4, GW), F32)] * 3,
        compiler_params=_params(("parallel",), 48 * 2**20),
    )(proj, proj, proj, cw, cw, cw, dq, dk, dv)


def _tri(lower_incl):
    i = lax.broadcasted_iota(jnp.int32, (CH, CH), 0)
    j = lax.broadcasted_iota(jnp.int32, (CH, CH), 1)
    return jnp.where(i >= j, 1.0, 0.0) if lower_incl else jnp.where(j >= i, 1.0, 0.0)


def _lane(shape):
    return lax.broadcasted_iota(jnp.int32, shape, 1)


def _prep_bg(proj, ad):
    n = proj.shape[0]
    nch = n // CH

    def body(p_ref, ad_ref, bg_ref, bgt_ref):
        p = p_ref[...]
        lane = _lane(p.shape)
        beta = _sigmoid(p)
        xa = p + ad_ref[1:2, :]
        sp = jnp.maximum(xa, 0.0) + jnp.log(1.0 + jnp.exp(-jnp.abs(xa)))
        g = -jnp.exp(ad_ref[0:1, :]) * sp
        gc = _dot(_tri(True), g, NN, P_CUM)
        bg = jnp.where(lane < HEADS, beta, jnp.where(lane < 2 * HEADS, gc, 0.0))
        bg_ref[...] = bg
        bgt_ref[0] = bg.T

    return pl.pallas_call(
        body, name="prep_bg", grid=(nch,),
        in_specs=[pl.BlockSpec((CH, DH), lambda i: (i, BAB)), pl.BlockSpec((2, DH), lambda i: (0, 0))],
        out_specs=[pl.BlockSpec((CH, DH), lambda i: (i, 0)), pl.BlockSpec((1, DH, CH), lambda i: (i, 0, 0))],
        out_shape=[jax.ShapeDtypeStruct((n, DH), F32), jax.ShapeDtypeStruct((nch, DH, CH), F32)],
        compiler_params=_params(("parallel",)),
    )(proj, ad)


def _prep_bg_bwd(proj, ad, dbg):
    n = proj.shape[0]
    nch = n // CH

    def body(p_ref, ad_ref, d_ref, o_ref, ga_ref, gd_ref):
        p = p_ref[...]
        d = d_ref[...]
        lane = _lane(p.shape)
        beta = _sigmoid(p)
        xa = p + ad_ref[1:2, :]
        sp = jnp.maximum(xa, 0.0) + jnp.log(1.0 + jnp.exp(-jnp.abs(xa)))
        na = -jnp.exp(ad_ref[0:1, :])
        dg = _dot(_tri(False), d, NN, P_CUM)
        da = dg * na * _sigmoid(xa)
        is_g = (lane >= HEADS) & (lane < 2 * HEADS)
        o_ref[...] = jnp.where(lane < HEADS, d * beta * (1.0 - beta), jnp.where(is_g, da, 0.0)).astype(BF16)
        ga = jnp.sum(jnp.where(is_g, dg * na * sp, 0.0), axis=0, keepdims=True)
        gd = jnp.sum(jnp.where(is_g, da, 0.0), axis=0, keepdims=True)

        @pl.when(pl.program_id(0) == 0)
        def _():
            ga_ref[...] = jnp.zeros_like(ga_ref)
            gd_ref[...] = jnp.zeros_like(gd_ref)

        ga_ref[...] += ga
        gd_ref[...] += gd

    one = pl.BlockSpec((1, DH), lambda i: (0, 0))
    return pl.pallas_call(
        body, name="prep_bg_bwd", grid=(nch,),
        in_specs=[pl.BlockSpec((CH, DH), lambda i: (i, BAB)), pl.BlockSpec((2, DH), lambda i: (0, 0)),
                  pl.BlockSpec((CH, DH), lambda i: (i, 0))],
        out_specs=[pl.BlockSpec((CH, DH), lambda i: (i, 0)), one, one],
        out_shape=[jax.ShapeDtypeStruct((n, DH), BF16), jax.ShapeDtypeStruct((1, DH), F32),
                   jax.ShapeDtypeStruct((1, DH), F32)],
        compiler_params=_params(("arbitrary",)),
    )(proj, ad, dbg)


def _gdn_out(o, proj, wg):
    n = o.shape[0]

    def body(o_ref, z_ref, w_ref, y_ref):
        ov, z = o_ref[...], z_ref[...]
        r = lax.rsqrt(jnp.mean(ov * ov, axis=-1, keepdims=True) + EPS)
        y_ref[...] = (ov * r * w_ref[...] * (z * _sigmoid(z))).astype(BF16)

    return pl.pallas_call(
        body, name="gdn_out", grid=(HEADS,),
        in_specs=[pl.BlockSpec((n, DH), lambda h: (0, h)), pl.BlockSpec((n, DH), lambda h: (0, ZB + h)),
                  pl.BlockSpec((1, DH), lambda h: (0, 0))],
        out_specs=pl.BlockSpec((n, DH), lambda h: (0, h)),
        out_shape=jax.ShapeDtypeStruct((n, GW), BF16),
        compiler_params=_params(("parallel",)),
    )(o, proj, wg)


def _gdn_out_bwd(o, proj, wg, dmix):
    n = o.shape[0]

    def body(o_ref, z_ref, w_ref, d_ref, do_ref, dz_ref, gw_ref):
        ov, z, d, w = o_ref[...], z_ref[...], d_ref[...], w_ref[...]
        r = lax.rsqrt(jnp.mean(ov * ov, axis=-1, keepdims=True) + EPS)
        nrm = ov * r
        s = _sigmoid(z)
        dz_ref[...] = (d * (nrm * w) * _dsilu(z, s)).astype(BF16)
        dn_w = d * (z * s)
        gw = jnp.sum(dn_w * nrm, axis=0, keepdims=True)
        dn = dn_w * w
        do_ref[...] = r * (dn - nrm * jnp.mean(dn * nrm, axis=-1, keepdims=True))

        @pl.when(pl.program_id(0) == 0)
        def _():
            gw_ref[...] = jnp.zeros_like(gw_ref)

        gw_ref[...] += gw

    return pl.pallas_call(
        body, name="gdn_out_bwd", grid=(HEADS,),
        in_specs=[pl.BlockSpec((n, DH), lambda h: (0, h)), pl.BlockSpec((n, DH), lambda h: (0, ZB + h)),
                  pl.BlockSpec((1, DH), lambda h: (0, 0)), pl.BlockSpec((n, DH), lambda h: (0, h))],
        out_specs=[pl.BlockSpec((n, DH), lambda h: (0, h)), pl.BlockSpec((n, DH), lambda h: (0, h)),
                   pl.BlockSpec((1, DH), lambda h: (0, 0))],
        out_shape=[jax.ShapeDtypeStruct((n, GW), F32), jax.ShapeDtypeStruct((n, GW), BF16),
                   jax.ShapeDtypeStruct((1, DH), F32)],
        compiler_params=_params(("arbitrary",)),
    )(o, proj, wg, dmix)


def _conv_branch(proj, w3, b):
    n = proj.shape[0]

    def body(gb_ref, gc_ref, hc_ref, zc_ref, w_ref, b_ref, y_ref):
        u = gc_ref[...] * hc_ref[...]
        cc = _conv_silu(u, w_ref, 3) + b_ref[...]
        z = zc_ref[...]
        y_ref[...] = (gb_ref[...] * cc * (z * _sigmoid(z))).astype(BF16)

    col = lambda base: pl.BlockSpec((n, DH), lambda h: (0, base + h))
    return pl.pallas_call(
        body, name="conv_branch", grid=(HEADS,),
        in_specs=[col(GBB), col(GCB), col(HCB), col(ZCB), pl.BlockSpec((3, DH), lambda h: (0, h)),
                  pl.BlockSpec((1, DH), lambda h: (0, h))],
        out_specs=col(0), out_shape=jax.ShapeDtypeStruct((n, GW), BF16),
        compiler_params=_params(("parallel",), 40 * 2**20),
    )(proj, proj, proj, proj, w3, b)


def _conv_branch_bwd(proj, w3, b, dmix):
    n = proj.shape[0]

    def body(gb_ref, gc_ref, hc_ref, zc_ref, w_ref, b_ref, d_ref, dgb, dgc, dhc, dzc, gw_ref, gbias_ref):
        gb, gcv, hc, z, d = gb_ref[...], gc_ref[...], hc_ref[...], zc_ref[...], d_ref[...]
        u = gcv * hc
        cc = _conv_silu(u, w_ref, 3) + b_ref[...]
        s = _sigmoid(z)
        dzc[...] = (d * (gb * cc) * _dsilu(z, s)).astype(BF16)
        dp = d * (z * s)
        dgb[...] = (dp * cc).astype(BF16)
        dcc = dp * gb
        gbias_ref[...] = jnp.sum(dcc, axis=0, keepdims=True)
        du = None
        for j in range(3):
            gw_ref[j:j + 1, :] = jnp.sum(dcc * _shift_down(u, 2 - j), axis=0, keepdims=True)
            t = _shift_up(dcc, 2 - j) * w_ref[j:j + 1, :]
            du = t if du is None else du + t
        dgc[...] = (du * hc).astype(BF16)
        dhc[...] = (du * gcv).astype(BF16)

    col = lambda base: pl.BlockSpec((n, DH), lambda h: (0, base + h))
    big = jax.ShapeDtypeStruct((n, GW), BF16)
    return pl.pallas_call(
        body, name="conv_branch_bwd", grid=(HEADS,),
        in_specs=[col(GBB), col(GCB), col(HCB), col(ZCB), pl.BlockSpec((3, DH), lambda h: (0, h)),
                  pl.BlockSpec((1, DH), lambda h: (0, h)), col(GW // DH)],
        out_specs=[col(0)] * 4 + [pl.BlockSpec((3, DH), lambda h: (0, h)), pl.BlockSpec((1, DH), lambda h: (0, h))],
        out_shape=[big] * 4 + [jax.ShapeDtypeStruct((3, GW), F32), jax.ShapeDtypeStruct((1, GW), F32)],
        compiler_params=_params(("parallel",), 48 * 2**20),
    )(proj, proj, proj, proj, w3, b, dmix)


def _final_loss(out, tgt, wf):
    n, d = out.shape
    tr = min(256, n)

    def body(o_ref, t_ref, w_ref, do_ref, dob_ref, gw_ref, loss_ref):
        ov, w = o_ref[...], w_ref[...]
        r = lax.rsqrt(jnp.mean(ov * ov, axis=-1, keepdims=True) + EPS)
        nrm = ov * r
        e = nrm * w - t_ref[...]
        dy = e * (1.0 / d)
        dn = dy * w
        dout = r * (dn - nrm * jnp.mean(dn * nrm, axis=-1, keepdims=True))
        do_ref[...] = dout
        dob_ref[...] = dout.astype(BF16)

        @pl.when(pl.program_id(0) == 0)
        def _():
            gw_ref[...] = jnp.zeros_like(gw_ref)
            loss_ref[...] = jnp.zeros_like(loss_ref)

        gw_ref[...] += jnp.sum(dy * nrm, axis=0, keepdims=True)
        loss_ref[...] += (0.5 / d) * jnp.sum(jnp.sum(e * e, axis=-1, keepdims=True), axis=0, keepdims=True)

    row = pl.BlockSpec((tr, d), lambda i: (i, 0))
    return pl.pallas_call(
        body, name="final_loss", grid=(n // tr,),
        in_specs=[row, row, pl.BlockSpec((1, d), lambda i: (0, 0))],
        out_specs=[row, row, pl.BlockSpec((1, d), lambda i: (0, 0)), pl.BlockSpec((1, 1), lambda i: (0, 0))],
        out_shape=[jax.ShapeDtypeStruct((n, d), F32), jax.ShapeDtypeStruct((n, d), BF16),
                   jax.ShapeDtypeStruct((1, d), F32), jax.ShapeDtypeStruct((1, 1), F32)],
        compiler_params=_params(("arbitrary",)),
    )(out, tgt, wf)


def _rms_in_bwd(x, w, dh, dout):
    n, d = x.shape
    tr = min(256, n)

    def body(x_ref, w_ref, dh_ref, do_ref, dx_ref, gw_ref):
        xv, dhv = x_ref[...], dh_ref[...]
        r = lax.rsqrt(jnp.mean(xv * xv, axis=-1, keepdims=True) + EPS)
        xn = xv * r
        dxn = dhv * w_ref[...]
        dx_ref[...] = r * (dxn - xn * jnp.mean(dxn * xn, axis=-1, keepdims=True)) + do_ref[...]

        @pl.when(pl.program_id(0) == 0)
        def _():
            gw_ref[...] = jnp.zeros_like(gw_ref)

        gw_ref[...] += jnp.sum(dhv * xn, axis=0, keepdims=True)

    row = pl.BlockSpec((tr, d), lambda i: (i, 0))
    one = pl.BlockSpec((1, d), lambda i: (0, 0))
    return pl.pallas_call(
        body, name="rms_in_bwd", grid=(n // tr,),
        in_specs=[row, one, row, row], out_specs=[row, one],
        out_shape=[jax.ShapeDtypeStruct((n, d), F32), jax.ShapeDtypeStruct((1, d), F32)],
        compiler_params=_params(("arbitrary",)),
    )(x, w, dh, dout)


def _ij():
    i = lax.broadcasted_iota(jnp.int32, (CH, CH), 0)
    j = lax.broadcasted_iota(jnp.int32, (CH, CH), 1)
    return i, j


def _unit_lower_inverse(a):
    i, j = _ij()
    eye = jnp.where(i == j, 1.0, 0.0)
    same16 = (i // 16) == (j // 16)
    same32 = (i // 32) == (j // 32)
    n1 = jnp.where(same16, -a, 0.0)
    n2 = _dot(n1, n1, NN, P_INV)
    n4 = _dot(n2, n2, NN, P_INV)
    n8 = _dot(n4, n4, NN, P_INV)
    t = eye + n1 + n2 + _dot(n1, n2, NN, P_INV)
    t = t + _dot(t, n4, NN, P_INV)
    t = t + _dot(t, n8, NN, P_INV)
    a1 = jnp.where(same32 & jnp.logical_not(same16), a, 0.0)
    t = t - _dot(t, _dot(a1, t, NN, P_INV), NN, P_INV)
    a2 = jnp.where(same32, 0.0, a)
    t = t - _dot(t, _dot(a2, t, NN, P_INV), NN, P_INV)
    return t


def _head_vectors(bg, bgt, h):
    bcol = bg[:, h:h + 1]
    gcol = bg[:, HEADS + h:HEADS + h + 1]
    grow = bgt[HEADS + h:HEADS + h + 1, :]
    return bcol, gcol, grow


def _decay(gcol, grow):
    i, j = _ij()
    return jnp.where(i >= j, jnp.exp(jnp.where(i >= j, gcol - grow, 0.0)), 0.0)


def _gdn_intra(q, k, v, bg, bgt):
    n = q.shape[0]
    nch = n // CH

    def body(q_ref, k_ref, v_ref, bg_ref, bgt_ref, u_ref, w_ref, p_ref, t_ref):
        bg, bgt = bg_ref[...], bgt_ref[0]
        i, j = _ij()
        for h in range(HEADS):
            sl = slice(h * DH, (h + 1) * DH)
            qh, kh, vh = q_ref[:, sl], k_ref[:, sl], v_ref[:, sl]
            bcol, gcol, grow = _head_vectors(bg, bgt, h)
            dec = _decay(gcol, grow)
            kk = _dot(kh, kh, NT, P_GRAM)
            a = jnp.where(i > j, bcol * kk * dec, 0.0)
            t = _unit_lower_inverse(a)
            p_ref[0, h] = _dot(qh, kh, NT, P_GRAM) * dec
            t_ref[0, h] = t
            u_ref[:, sl] = _dot(t, vh * bcol, NN, P_SOL)
            w_ref[:, sl] = _dot(t, kh * (bcol * jnp.exp(gcol)), NN, P_SOL)

    row = pl.BlockSpec((CH, GW), lambda c: (c, 0))
    sq = pl.BlockSpec((1, HEADS, CH, CH), lambda c: (c, 0, 0, 0))
    big = jax.ShapeDtypeStruct((n, GW), F32)
    sqs = jax.ShapeDtypeStruct((nch, HEADS, CH, CH), F32)
    return pl.pallas_call(
        body, name="gdn_intra", grid=(nch,),
        in_specs=[row, row, row, pl.BlockSpec((CH, DH), lambda c: (c, 0)),
                  pl.BlockSpec((1, DH, CH), lambda c: (c, 0, 0))],
        out_specs=[row, row, sq, sq], out_shape=[big, big, sqs, sqs],
        compiler_params=_params(("parallel",)),
    )(q, k, v, bg, bgt)


def _gdn_scan(q, k, bg, u, w, p):
    n = q.shape[0]
    nch = n // CH

    def body(q_ref, k_ref, bg_ref, u_ref, w_ref, p_ref, o_ref, vn_ref, s_out, s_scr):
        @pl.when(pl.program_id(0) == 0)
        def _():
            s_scr[...] = jnp.zeros_like(s_scr)

        bg = bg_ref[...]
        for h in range(HEADS):
            sl = slice(h * DH, (h + 1) * DH)
            gcol = bg[:, HEADS + h:HEADS + h + 1]
            glast = gcol[CH - 1:CH, :]
            s = s_scr[h]
            s_out[0, :, sl] = s
            vn = u_ref[:, sl] - _dot(w_ref[:, sl], s, NN, P_SCAN)
            vn_ref[:, sl] = vn
            o_ref[:, sl] = _dot(q_ref[:, sl] * jnp.exp(gcol), s, NN, P_SCAN) + _dot(p_ref[0, h], vn, NN, P_SCAN)
            ks = k_ref[:, sl] * jnp.exp(glast - gcol)
            s_scr[h] = s * jnp.exp(glast) + _dot(ks, vn, TN, P_SCAN)

    row = pl.BlockSpec((CH, GW), lambda c: (c, 0))
    big = jax.ShapeDtypeStruct((n, GW), F32)
    return pl.pallas_call(
        body, name="gdn_scan", grid=(nch,),
        in_specs=[row, row, pl.BlockSpec((CH, DH), lambda c: (c, 0)), row, row,
                  pl.BlockSpec((1, HEADS, CH, CH), lambda c: (c, 0, 0, 0))],
        out_specs=[row, row, pl.BlockSpec((1, DH, GW), lambda c: (c, 0, 0))],
        out_shape=[big, big, jax.ShapeDtypeStruct((nch, DH, GW), F32)],
        scratch_shapes=[pltpu.VMEM((HEADS, DH, DH), F32)],
        compiler_params=_params(("arbitrary",)),
    )(q, k, bg, u, w, p)


def _gdn_scan_bwd(q, k, bg, w, p, vn, s_in, do):
    n = q.shape[0]
    nch = n // CH
    rev = lambda c: nch - 1 - c

    def body(q_ref, k_ref, bg_ref, w_ref, p_ref, vn_ref, s_ref, do_ref,
             dqg_ref, dp_ref, du_ref, dw_ref, dks_ref, dgam_ref, ds_scr):
        @pl.when(pl.program_id(0) == 0)
        def _():
            ds_scr[...] = jnp.zeros_like(ds_scr)

        bg = bg_ref[...]
        lane = _lane((1, DH))
        dgam = jnp.zeros((1, DH), F32)
        for h in range(HEADS):
            sl = slice(h * DH, (h + 1) * DH)
            gcol = bg[:, HEADS + h:HEADS + h + 1]
            glast = gcol[CH - 1:CH, :]
            s = s_ref[0, :, sl]
            ds = ds_scr[h]
            dov, vnh, wh = do_ref[:, sl], vn_ref[:, sl], w_ref[:, sl]
            qg = q_ref[:, sl] * jnp.exp(gcol)
            ks = k_ref[:, sl] * jnp.exp(glast - gcol)
            dqg_ref[:, sl] = _dot(dov, s, NT, P_SCANB)
            dp_ref[0, h] = _dot(dov, vnh, NT, P_SCANB)
            dvn = _dot(p_ref[0, h], dov, TN, P_SCANB) + _dot(ks, ds, NN, P_SCANB)
            du_ref[:, sl] = dvn
            dw_ref[:, sl] = -_dot(dvn, s, NT, P_SCANB)
            dks_ref[:, sl] = _dot(vnh, ds, NT, P_SCANB)
            tot = jnp.sum(jnp.sum(ds * s, axis=-1, keepdims=True), axis=0, keepdims=True)
            dgam = dgam + jnp.where(lane == h, tot, 0.0)
            ds_scr[h] = ds * jnp.exp(glast) + _dot(qg, dov, TN, P_SCANB) - _dot(wh, dvn, TN, P_SCANB)
        dgam_ref[0] = jnp.broadcast_to(dgam, (8, DH))

    row = pl.BlockSpec((CH, GW), lambda c: (rev(c), 0))
    sq = pl.BlockSpec((1, HEADS, CH, CH), lambda c: (rev(c), 0, 0, 0))
    big = jax.ShapeDtypeStruct((n, GW), F32)
    return pl.pallas_call(
        body, name="gdn_scan_bwd", grid=(nch,),
        in_specs=[row, row, pl.BlockSpec((CH, DH), lambda c: (rev(c), 0)), row, sq, row,
                  pl.BlockSpec((1, DH, GW), lambda c: (rev(c), 0, 0)), row],
        out_specs=[row, sq, row, row, row, pl.BlockSpec((1, 8, DH), lambda c: (rev(c), 0, 0))],
        out_shape=[big, jax.ShapeDtypeStruct((nch, HEADS, CH, CH), F32), big, big, big,
                   jax.ShapeDtypeStruct((nch, 8, DH), F32)],
        scratch_shapes=[pltpu.VMEM((HEADS, DH, DH), F32)],
        compiler_params=_params(("arbitrary",)),
    )(q, k, bg, w, p, vn, s_in, do)


def _gdn_intra_bwd(q, k, v, bg, bgt, t, u, w, p, dqg, dp, du, dw, dks, dgam):
    n = q.shape[0]
    nch = n // CH

    def body(q_ref, k_ref, v_ref, bg_ref, bgt_ref, t_ref, u_ref, w_ref, p_ref,
             dqg_ref, dp_ref, du_ref, dw_ref, dks_ref, dgam_ref, dq_ref, dk_ref, dv_ref, dbg_ref):
        bg, bgt = bg_ref[...], bgt_ref[0]
        dgam_all = dgam_ref[0]
        i, j = _ij()
        rows1 = lax.broadcasted_iota(jnp.int32, (CH, 1), 0)
        lane = _lane((CH, DH))
        dbg = jnp.zeros((CH, DH), F32)
        rsum = lambda x: jnp.sum(x, axis=-1, keepdims=True)
        for h in range(HEADS):
            sl = slice(h * DH, (h + 1) * DH)
            qh, kh, vh = q_ref[:, sl], k_ref[:, sl], v_ref[:, sl]
            bcol, gcol, grow = _head_vectors(bg, bgt, h)
            dec = _decay(gcol, grow)
            gam = jnp.exp(gcol)
            glast = gcol[CH - 1:CH, :]
            e = jnp.exp(glast - gcol)
            th = t_ref[0, h]
            kg = kh * gam
            dru = _dot(th, du_ref[:, sl], TN, P_BWD)
            drw = _dot(th, dw_ref[:, sl], TN, P_BWD)
            da = -(_dot(dru, u_ref[:, sl], NT, P_BWD) + _dot(drw, w_ref[:, sl], NT, P_BWD))
            da = jnp.where(i > j, da, 0.0)
            dv_ref[:, sl] = bcol * dru
            kk = _dot(kh, kh, NT, P_GRAM)
            dbeta =rsum(dru * vh) + rsum(drw * kg) + rsum(da * kk * dec)
            dgc = rsum(drw * kg) * bcol
            dkk = da * bcol * dec
            dqk = dp_ref[0, h] * dec
            dq_p = _dot(dqk, kh, NN, P_BWD)
            dk_p = _dot(dqk, qh, TN, P_BWD)
            dk_a = _dot(dkk, kh, NN, P_BWD)
            dk_b = _dot(dkk, kh, TN, P_BWD)
            dqg = dqg_ref[:, sl]
            dksh = dks_ref[:, sl]
            dq_ref[:, sl] = gam * dqg + dq_p
            dk_ref[:, sl] = (bcol * gam) * drw + dk_p + dk_a + dk_b + dksh * e
            tk = rsum(dksh * kh) * e
            mdec = da * (bcol * kk * dec) + dp_ref[0, h] * p_ref[0, h]
            col = rsum(jnp.where(i == j, jnp.sum(mdec, axis=0, keepdims=True), 0.0))
            dgc = dgc + rsum(mdec) - col
            dgc = dgc + rsum(dqg * qh) * gam - tk
            dglast = jnp.sum(tk, axis=0, keepdims=True) + dgam_all[0:1, h:h + 1] * jnp.exp(glast)
            dgc = dgc + jnp.where(rows1 == CH - 1, dglast, 0.0)
            dbg = dbg + jnp.where(lane == h, dbeta, 0.0) + jnp.where(lane == HEADS + h, dgc, 0.0)
        dbg_ref[...] = dbg

    row = pl.BlockSpec((CH, GW), lambda c: (c, 0))
    sq = pl.BlockSpec((1, HEADS, CH, CH), lambda c: (c, 0, 0, 0))
    small = pl.BlockSpec((CH, DH), lambda c: (c, 0))
    big = jax.ShapeDtypeStruct((n, GW), F32)
    return pl.pallas_call(
        body, name="gdn_intra_bwd", grid=(nch,),
        in_specs=[row, row, row, small, pl.BlockSpec((1, DH, CH), lambda c: (c, 0, 0)), sq, row, row, sq,
                  row, sq, row, row, row, pl.BlockSpec((1, 8, DH), lambda c: (c, 0, 0))],
        out_specs=[row, row, row, small],
        out_shape=[big, big, big, jax.ShapeDtypeStruct((n, DH), F32)],
        compiler_params=_params(("parallel",)),
    )(q, k, v, bg, bgt, t, u, w, p, dqg, dp, du, dw, dks, dgam)


def _local_step(x, tgt, w_cat, w_out, norm_in_w, cqw, ad, gdn_norm_w, conv_w, conv_b, final_norm_w):
    h = _rms_in(x, norm_in_w)
    proj = _matmul(h, w_cat, NN, F32, 512, 640, 1024, "mm_proj")
    q, k, v = _prep_qkv(proj, cqw)
    bg, bgt = _prep_bg(proj, ad)
    u, w, p, t = _gdn_intra(q, k, v, bg, bgt)
    o, vn, s_in = _gdn_scan(q, k, bg, u, w, p)
    mix = jnp.concatenate([_gdn_out(o, proj, gdn_norm_w), _conv_branch(proj, conv_w, conv_b)], axis=1)
    out = _matmul(mix, w_out, NN, F32, 512, 512, 2048, "mm_out", add=x)
    dout, dout_b, g_fn, loss = _final_loss(out, tgt, final_norm_w)

    dmix = _matmul(dout_b, w_out, NT, F32, 512, 1024, 1024, "mm_dmix")
    g_wout = _matmul(mix, dout_b, TN, BF16, 512, 512, 2048, "mm_gwout")
    do, dz, g_gn = _gdn_out_bwd(o, proj, gdn_norm_w, dmix)
    dgb, dgc, dhc, dzc, g_cw, g_cb = _conv_branch_bwd(proj, conv_w, conv_b, dmix)
    dqg, dp, du, dw, dks, dgam = _gdn_scan_bwd(q, k, bg, w, p, vn, s_in, do)
    dq, dk, dv, dbg = _gdn_intra_bwd(q, k, v, bg, bgt, t, u, w, p, dqg, dp, du, dw, dks, dgam)
    dpq, dpk, dpv, gq, gk, gv = _prep_qkv_bwd(proj, cqw, dq, dk, dv)
    dba, g_al, g_dt = _prep_bg_bwd(proj, ad, dbg)
    dproj = jnp.concatenate([dpq, dpk, dpv, dz, dba, dgb, dgc, dhc, dzc], axis=1)
    g_wcat = _matmul(h, dproj, TN, BF16, 512, 640, 2048, "mm_gwin")
    dh = _matmul(dproj, w_cat, NT, F32, 512, 1024, 640, "mm_dh")
    gx, g_nin = _rms_in_bwd(x, norm_in_w, dh, dout)
    g_cqw = jnp.concatenate([gq, gk, gv], axis=1)
    return gx, g_wcat, g_wout, dict(nin=g_nin, cb=g_cb, fn=g_fn, al=g_al, dt=g_dt, gn=g_gn, cqw=g_cqw,
                                    cw=g_cw, loss=loss)


def _place():
    x, y, c = lax.axis_index("x"), lax.axis_index("y"), lax.axis_index("c")
    chips = [(1 - x, y), (x, 1 - y), (1 - x, 1 - y)]
    return x, y, c, chips


def _all_gather_shards(shards):
    nsh = len(shards)

    def body(*refs):
        ins, outs = refs[:nsh], refs[nsh:2 * nsh]
        send_sems, recv_sems, loc_sems = refs[2 * nsh:]
        x, y, c, chips = _place()
        mine = 2 * x + y
        local = [pltpu.make_async_copy(ins[a], outs[a].at[mine], loc_sems.at[a]) for a in range(nsh)]
        for cp in local:
            cp.start()
        remote = []
        for jj, (px, py) in enumerate(chips):
            for a in range(nsh):
                remote.append(pltpu.make_async_remote_copy(
                    src_ref=ins[a], dst_ref=outs[a].at[mine],
                    send_sem=send_sems.at[jj * nsh + a], recv_sem=recv_sems.at[jj * nsh + a],
                    device_id=(px, py, c), device_id_type=MESH))
        for cp in remote:
            cp.start()
        for cp in remote:
            cp.wait_recv()
        for cp in remote:
            cp.wait_send()
        for cp in local:
            cp.wait()

    return pl.pallas_call(
        body, name="all_gather_shards",
        in_specs=[ANY] * nsh, out_specs=[ANY] * nsh,
        out_shape=[jax.ShapeDtypeStruct((4,) + s.shape, s.dtype) for s in shards],
        scratch_shapes=[pltpu.SemaphoreType.DMA((3 * nsh,)), pltpu.SemaphoreType.DMA((3 * nsh,)),
                        pltpu.SemaphoreType.DMA((nsh,))],
    )(*shards)


def _exchange_grads(blocks, pack):
    nb = len(blocks)

    def body(*refs):
        ins, pack_ref = refs[:nb], refs[nb]
        lands, packs = refs[nb + 1:2 * nb + 1], refs[2 * nb + 1]
        send_sems, recv_sems, psend, precv, loc_sems = refs[2 * nb + 2:]
        x, y, c, chips = _place()
        me = 4 * x + 2 * y + c
        local = [pltpu.make_async_copy(pack_ref, packs.at[me], loc_sems.at[nb])]
        local += [pltpu.make_async_copy(ins[a].at[2 * x + y], lands[a].at[3], loc_sems.at[a]) for a in range(nb)]
        for cp in local:
            cp.start()
        remote = []
        for jj, (px, py) in enumerate(chips):
            for a in range(nb):
                remote.append(pltpu.make_async_remote_copy(
                    src_ref=ins[a].at[2 * px + py], dst_ref=lands[a].at[jj],
                    send_sem=send_sems.at[jj * nb + a], recv_sem=recv_sems.at[jj * nb + a],
                    device_id=(px, py, c), device_id_type=MESH))
        for r in range(1, 8):
            dx, dy, dc = (r >> 2) & 1, (r >> 1) & 1, r & 1
            peer = (x + dx - 2 * x * dx, y + dy - 2 * y * dy, c + dc - 2 * c * dc)
            remote.append(pltpu.make_async_remote_copy(
                src_ref=pack_ref, dst_ref=packs.at[me], send_sem=psend.at[r - 1], recv_sem=precv.at[r - 1],
                device_id=peer, device_id_type=MESH))
        for cp in remote:
            cp.start()
        for cp in remote:
            cp.wait_recv()
        for cp in remote:
            cp.wait_send()
        for cp in local:
            cp.wait()

    out_shape = [jax.ShapeDtypeStruct((4,) + b.shape[1:], b.dtype) for b in blocks]
    out_shape.append(jax.ShapeDtypeStruct((8,) + pack.shape, pack.dtype))
    return pl.pallas_call(
        body, name="exchange_grads",
        in_specs=[ANY] * (nb + 1), out_specs=[ANY] * (nb + 1), out_shape=out_shape,
        scratch_shapes=[pltpu.SemaphoreType.DMA((3 * nb,)), pltpu.SemaphoreType.DMA((3 * nb,)),
                        pltpu.SemaphoreType.DMA((7,)), pltpu.SemaphoreType.DMA((7,)), pltpu.SemaphoreType.DMA((nb + 1,))],
    )(*blocks, pack)


def _swap_with_sibling(parts):
    npart = len(parts)

    def body(*refs):
        ins, outs = refs[:npart], refs[npart:2 * npart]
        send_sems, recv_sems = refs[2 * npart:]
        x, y, c, _ = _place()
        cps = [pltpu.make_async_remote_copy(
            src_ref=ins[a], dst_ref=outs[a], send_sem=send_sems.at[a], recv_sem=recv_sems.at[a],
            device_id=(x, y, 1 - c), device_id_type=MESH) for a in range(npart)]
        for cp in cps:
            cp.start()
        for cp in cps:
            cp.wait_recv()
        for cp in cps:
            cp.wait_send()

    return pl.pallas_call(
        body, name="swap_with_sibling",
        in_specs=[ANY] * npart, out_specs=[ANY] * npart,
        out_shape=[jax.ShapeDtypeStruct(p.shape, p.dtype) for p in parts],
        scratch_shapes=[pltpu.SemaphoreType.DMA((npart,)), pltpu.SemaphoreType.DMA((npart,))],
    )(*parts)


def _sum_blocks(land, rows, name):
    _, r, cdim = land.shape
    rows = min(rows, r)

    def body(land_ref, o_ref):
        acc = land_ref[3].astype(F32)
        for jj in range(3):
            acc = acc + land_ref[jj].astype(F32)
        o_ref[...] = acc

    return pl.pallas_call(
        body, name=name, grid=(r // rows,),
        in_specs=[pl.BlockSpec((4, rows, cdim), lambda i: (0, i, 0))],
        out_specs=pl.BlockSpec((rows, cdim), lambda i: (i, 0)),
        out_shape=jax.ShapeDtypeStruct((r, cdim), F32),
        compiler_params=_params(("parallel",), 40 * 2**20),
    )(land)


def _sum_packs(packs):
    def body(p_ref, o_ref):
        acc = p_ref[0]
        for d in range(1, 8):
            acc = acc + p_ref[d]
        o_ref[...] = acc

    return pl.pallas_call(
        body, name="sum_packs", out_shape=jax.ShapeDtypeStruct(packs.shape[1:], F32),
    )(packs)


def _adamw(w, m, v, g1, g2, rows, name):
    r, cdim = w.shape
    rows = min(rows, r)
    c1 = 1.0 / (1.0 - ADAM_B1 ** ADAM_STEP)
    c2 = 1.0 / (1.0 - ADAM_B2 ** ADAM_STEP)

    def body(*refs):
        if g2 is None:
            w_ref, m_ref, v_ref, g_ref, go, do, mo, vo = refs
            g = g_ref[...]
        else:
            w_ref, m_ref, v_ref, g_ref, g2_ref, go, do, mo, vo = refs
            g = g_ref[...] + g2_ref[...]
        mn = ADAM_B1 * m_ref[...] + (1.0 - ADAM_B1) * g
        vn = ADAM_B2 * v_ref[...] + (1.0 - ADAM_B2) * (g * g)
        go[...] = g
        mo[...] = mn
        vo[...] = vn
        do[...] = -ADAM_LR * ((mn * c1) / (jnp.sqrt(vn * c2) + ADAM_EPS) + ADAM_WD * w_ref[...])

    blk = pl.BlockSpec((rows, cdim), lambda i: (i, 0))
    args = [w, m, v, g1] + ([] if g2 is None else [g2])
    shp = jax.ShapeDtypeStruct((r, cdim), F32)
    return pl.pallas_call(
        body, name=name, grid=(r // rows,),
        in_specs=[blk] * len(args), out_specs=[blk] * 4, out_shape=[shp] * 4,
        compiler_params=_params(("parallel",), 18 * rows * cdim * 4 + 8 * 2**20),
    )(*args)


def _pad_lanes(a, width):
    return jnp.pad(a, ((0, 0), (0, width - a.shape[1])))


def _cat_layout(w_full):
    return jnp.concatenate([w_full[:, :4096], _pad_lanes(w_full[:, 4096:4112], DH), w_full[:, 4112:]], axis=1)


def _uncat_layout(g_cat):
    return jnp.concatenate([g_cat[:, :4096], g_cat[:, 4096:4112], g_cat[:, 4096 + DH:]], axis=1)


def _gathered_to_full(g):
    return jnp.transpose(g, (1, 0, 2)).reshape(g.shape[1], 4 * g.shape[2])


def _full_to_blocks(a):
    r, c4 = a.shape
    return jnp.transpose(a.reshape(r, 4, c4 // 4), (1, 0, 2))


def _row(a):
    return _pad_lanes(a.reshape(1, -1), 1024)


def _small_pack(nin, cb, fn, al, dt, gn, cqw_shard, cw_shard):
    ad = jnp.concatenate([al.reshape(1, -1), dt.reshape(1, -1)], axis=1)
    rows = [_row(nin), _row(cb), _row(fn), _row(ad), _row(gn), cqw_shard.reshape(3, 1024), _row(cw_shard)]
    out = jnp.concatenate(rows, axis=0)
    return jnp.pad(out, ((0, 16 - out.shape[0]), (0, 0)))


def kernel(x, norm_in_w, w_in, conv_qkv_w, A_log, dt_bias, gdn_norm_w, conv_w, conv_b, w_out, final_norm_w, loss_target, m_norm_in_w, m_w_in, m_conv_qkv_w, m_A_log, m_dt_bias, m_gdn_norm_w, m_conv_w, m_conv_b, m_w_out, m_final_norm_w, v_norm_in_w, v_w_in, v_conv_qkv_w, v_A_log, v_dt_bias, v_gdn_norm_w, v_conv_w, v_conv_b, v_w_out, v_final_norm_w):
    chip = 2 * lax.axis_index("x") + lax.axis_index("y")
    wi_b = _cast_bf16(w_in[0], 256, "cast_w_in")
    wo_b = _cast_bf16(w_out[0], 256, "cast_w_out")
    wi_g, wo_g, cq_g, cw_g = _all_gather_shards([wi_b, wo_b, conv_qkv_w[0], conv_w[0]])
    w_cat = _cat_layout(_gathered_to_full(wi_g))
    w_out_full = wo_g.reshape(2 * GW, -1)
    cqw = _gathered_to_full(cq_g)
    cw = _gathered_to_full(cw_g)
    ad = _pad_lanes(jnp.concatenate([jnp.zeros((2, HEADS), F32), jnp.concatenate([A_log, dt_bias], axis=0)], axis=1), DH)

    gx, g_wcat, g_wout, sm = _local_step(x[0], loss_target[0], w_cat, w_out_full, norm_in_w, cqw, ad,
                                         gdn_norm_w, cw, conv_b, final_norm_w.reshape(1, -1))

    ad_g = jnp.concatenate([sm["al"][:, HEADS:2 * HEADS], sm["dt"][:, HEADS:2 * HEADS]], axis=1)
    pack = jnp.concatenate([_row(sm["nin"]), _row(sm["cb"]), _row(sm["fn"]), _row(ad_g), _row(sm["gn"]),
                            sm["cqw"].reshape(12, 1024), sm["cw"], _row(sm["loss"])], axis=0)
    pack = jnp.pad(pack, ((0, PACK_ROWS - pack.shape[0]), (0, 0)))
    blk_in = _full_to_blocks(_uncat_layout(g_wcat))
    blk_out = g_wout.reshape(4, GW // 2, -1)
    land_in, land_out, packs = _exchange_grads([blk_in, blk_out], pack)
    part_in = _sum_blocks(land_in, 128, "sum_w_in")
    part_out = _sum_blocks(land_out, 128, "sum_w_out")
    sib_in, sib_out = _swap_with_sibling([part_in, part_out])
    tot = _sum_packs(packs)

    g_wi, d_wi, m_wi, v_wi = _adamw(w_in[0], m_w_in[0], v_w_in[0], part_in, sib_in, 64, "adamw_w_in")
    g_wo, d_wo, m_wo, v_wo = _adamw(w_out[0], m_w_out[0], v_w_out[0], part_out, sib_out, 128, "adamw_w_out")
    g_cq_sh = lax.dynamic_slice_in_dim(tot[R_CQ:R_CQ + 12].reshape(4, 3 * GW), chip * 768, 768, axis=1)
    g_cw_sh = lax.dynamic_slice_in_dim(tot[R_CW:R_CW + 3], chip * 256, 256, axis=1)
    sp = lambda nin, cb, fn, al, dt, gn, cq, cwv: _small_pack(nin, cb, fn, al, dt, gn, cq[0], cwv[0])
    g_s = _small_pack(tot[R_NIN], tot[R_CB], tot[R_FN], tot[R_AD, :HEADS], tot[R_AD, HEADS:2 * HEADS],
                      tot[R_GN, :DH], g_cq_sh, g_cw_sh)
    w_s = sp(norm_in_w, conv_b, final_norm_w, A_log, dt_bias, gdn_norm_w, conv_qkv_w, conv_w)
    m_s = sp(m_norm_in_w, m_conv_b, m_final_norm_w, m_A_log, m_dt_bias, m_gdn_norm_w, m_conv_qkv_w, m_conv_w)
    v_s = sp(v_norm_in_w, v_conv_b, v_final_norm_w, v_A_log, v_dt_bias, v_gdn_norm_w, v_conv_qkv_w, v_conv_w)
    small = _adamw(w_s, m_s, v_s, g_s, None, 16, "adamw_small")

    def unpack(a, big_in, big_out):
        return (a[0:1], big_in[None], a[5:8].reshape(1, 4, 768), a[3:4, :HEADS], a[3:4, HEADS:2 * HEADS],
                a[4:5, :DH], a[8, :768].reshape(1, 3, 256), a[1:2], big_out[None], a[2])

    loss = tot[R_LOSS, 0]
    return (loss, gx[None], *unpack(small[0], g_wi, g_wo), *unpack(small[1], d_wi, d_wo),
            *unpack(small[2], m_wi, m_wo), *unpack(small[3], v_wi, v_wo))
```

```python
import functools
import math

import jax
import jax.numpy as jnp
from jax import lax
from jax.experimental import pallas as pl
from jax.experimental.pallas import tpu as pltpu

F32 = jnp.float32
BF16 = jnp.bfloat16
MESH = pl.DeviceIdType.MESH
ANY = pl.BlockSpec(memory_space=pl.ANY)

HEADS = 8
DH = 128
CH = 64
GW = HEADS * DH
EPS = 1e-6
VMEM_V7X = 64 * 1024 * 1024

QB, KB, VB, ZB, BAB, GBB, GCB, HCB, ZCB = 0, 8, 16, 24, 32, 33, 41, 49, 57
NPB = 65
PW = NPB * DH

ADAM_LR, ADAM_B1, ADAM_B2, ADAM_EPS, ADAM_WD, ADAM_STEP = 0.001, 0.9, 0.999, 1e-08, 0.01, 10

R_NIN, R_CB, R_FN, R_AD, R_GN, R_CQ, R_CW, R_LOSS, PACK_ROWS = 0, 1, 2, 3, 4, 5, 17, 20, 24

NN = ((1,), (0,))
NT = ((1,), (1,))
TN = ((0,), (0,))


def _dot(a, b, dims=NN, mode="lo"):
    dn = (dims, ((), ()))
    if mode == "hi":
        return lax.dot_general(a, b, dn, precision=lax.Precision.HIGHEST, preferred_element_type=F32)
    ah, bh = a.astype(BF16), b.astype(BF16)
    out = lax.dot_general(ah, bh, dn, preferred_element_type=F32)
    if mode == "x3":
        al = (a - ah.astype(F32)).astype(BF16)
        bl = (b - bh.astype(F32)).astype(BF16)
        out = out + lax.dot_general(ah, bl, dn, preferred_element_type=F32)
        out = out + lax.dot_general(al, bh, dn, preferred_element_type=F32)
    return out


P_GRAM, P_INV, P_SOL, P_SCAN, P_SCANB, P_BWD = "lo", "lo", "lo", "lo", "lo", "lo"
P_CUM = "x3"


def _params(sem=None, vmem=None):
    kw = {}
    if sem is not None:
        kw["dimension_semantics"] = sem
    if vmem is not None:
        kw["vmem_limit_bytes"] = int(min(max(vmem, 32 * 2**20), VMEM_V7X - 8 * 2**20))
    return pltpu.CompilerParams(**kw)


def _sigmoid(x):
    return 1.0 / (1.0 + jnp.exp(-x))


def _dsilu(x, s):
    return s * (1.0 + x * (1.0 - s))


def _rows(shape):
    return lax.broadcasted_iota(jnp.int32, shape, 0)


def _shift_down(x, s):
    if s == 0:
        return x
    return jnp.where(_rows(x.shape) >= s, pltpu.roll(x, s, 0), 0.0)


def _shift_up(x, s):
    if s == 0:
        return x
    n = x.shape[0]
    return jnp.where(_rows(x.shape) < n - s, pltpu.roll(x, n - s, 0), 0.0)


def _matmul(a, b, dims, out_dtype, tm, tn, tk, name, add=None):
    if dims == NN:
        (m, k), (_, n) = a.shape, b.shape
    elif dims == NT:
        (m, k), (n, _) = a.shape, b.shape
    else:
        (k, m), (_, n) = a.shape, b.shape
    tm, tn, tk = min(tm, m), min(tn, n), min(tk, k)
    assert m % tm == 0 and n % tn == 0 and k % tk == 0, (name, m, n, k, tm, tn, tk)
    nk = k // tk

    def body(*refs):
        if add is None:
            a_ref, b_ref, o_ref = refs[:3]
            add_ref = None
        else:
            a_ref, b_ref, add_ref, o_ref = refs[:4]
        part = _dot(a_ref[...], b_ref[...], dims)
        if nk == 1:
            if add_ref is not None:
                part = part + add_ref[...]
            o_ref[...] = part.astype(out_dtype)
            return
        acc = refs[-1]
        kk = pl.program_id(2)

        @pl.when(kk == 0)
        def _():
            acc[...] = part

        @pl.when(kk > 0)
        def _():
            acc[...] += part

        @pl.when(kk == nk - 1)
        def _():
            r = acc[...]
            if add_ref is not None:
                r = r + add_ref[...]
            o_ref[...] = r.astype(out_dtype)

    if dims == TN:
        a_spec = pl.BlockSpec((tk, tm), lambda i, j, kk: (kk, i))
    else:
        a_spec = pl.BlockSpec((tm, tk), lambda i, j, kk: (i, kk))
    if dims == NT:
        b_spec = pl.BlockSpec((tn, tk), lambda i, j, kk: (j, kk))
    else:
        b_spec = pl.BlockSpec((tk, tn), lambda i, j, kk: (kk, j))
    o_spec = pl.BlockSpec((tm, tn), lambda i, j, kk: (i, j))
    in_specs = [a_spec, b_spec]
    args = [a, b]
    if add is not None:
        in_specs.append(o_spec)
        args.append(add)
    osz = jnp.dtype(out_dtype).itemsize
    est = 2 * (tm * tk * a.dtype.itemsize + tk * tn * b.dtype.itemsize + tm * tn * osz)
    est += 3 * tm * tn * 4 + (2 * tm * tn * 4 if add is not None else 0)
    return pl.pallas_call(
        body, name=name, grid=(m // tm, n // tn, nk),
        in_specs=in_specs, out_specs=o_spec,
        out_shape=jax.ShapeDtypeStruct((m, n), out_dtype),
        scratch_shapes=[pltpu.VMEM((tm, tn), F32)] if nk > 1 else [],
        compiler_params=_params(("parallel", "parallel", "arbitrary"), est + 8 * 2**20),
    )(*args)


def _cast_bf16(a, rows, name):
    r, c = a.shape
    rows = min(rows, r)

    def body(a_ref, o_ref):
        o_ref[...] = a_ref[...].astype(BF16)

    return pl.pallas_call(
        body, name=name, grid=(r // rows,),
        in_specs=[pl.BlockSpec((rows, c), lambda i: (i, 0))],
        out_specs=pl.BlockSpec((rows, c), lambda i: (i, 0)),
        out_shape=jax.ShapeDtypeStruct((r, c), BF16),
        compiler_params=_params(("parallel",)),
    )(a)


def _rms_in(x, w):
    n, d = x.shape
    tr = min(256, n)

    def body(x_ref, w_ref, h_ref):
        xv = x_ref[...]
        r = lax.rsqrt(jnp.mean(xv * xv, axis=-1, keepdims=True) + EPS)
        h_ref[...] = (xv * r * w_ref[...]).astype(BF16)

    return pl.pallas_call(
        body, name="rms_in", grid=(n // tr,),
        in_specs=[pl.BlockSpec((tr, d), lambda i: (i, 0)), pl.BlockSpec((1, d), lambda i: (0, 0))],
        out_specs=pl.BlockSpec((tr, d), lambda i: (i, 0)),
        out_shape=jax.ShapeDtypeStruct((n, d), BF16),
        compiler_params=_params(("parallel",)),
    )(x, w)


def _conv_silu(p, w_ref, taps):
    c = None
    for j in range(taps):
        t = _shift_down(p, taps - 1 - j) * w_ref[j:j + 1, :]
        c = t if c is None else c + t
    return c


def _prep_qkv(proj, cw):
    n = proj.shape[0]

    def body(pq, pk, pv, wq, wk, wv, q_ref, k_ref, v_ref):
        for p_ref, w_ref, o_ref, kind in ((pq, wq, q_ref, 0), (pk, wk, k_ref, 1), (pv, wv, v_ref, 2)):
            c = _conv_silu(p_ref[...], w_ref, 4)
            a = c * _sigmoid(c)
            if kind < 2:
                r = lax.rsqrt(jnp.sum(a * a, axis=-1, keepdims=True) + EPS)
                a = a * (r * (DH ** -0.5 if kind == 0 else 1.0))
            o_ref[...] = a

    col = lambda base: pl.BlockSpec((n, DH), lambda h: (0, base + h))
    wcol = lambda base: pl.BlockSpec((4, DH), lambda h: (0, base + h))
    out = jax.ShapeDtypeStruct((n, GW), F32)
    return pl.pallas_call(
        body, name="prep_qkv", grid=(HEADS,),
        in_specs=[col(QB), col(KB), col(VB), wcol(QB), wcol(KB), wcol(VB)],
        out_specs=[col(0)] * 3, out_shape=[out] * 3,
        compiler_params=_params(("parallel",), 40 * 2**20),
    )(proj, proj, proj, cw, cw, cw)


def _prep_qkv_bwd(proj, cw, dq, dk, dv):
    n = proj.shape[0]

    def body(pq, pk, pv, wq, wk, wv, dq_ref, dk_ref, dv_ref, oq, ok, ov, gq, gk, gv):
        for p_ref, w_ref, d_ref, o_ref, g_ref, kind in (
                (pq, wq, dq_ref, oq, gq, 0), (pk, wk, dk_ref, ok, gk, 1), (pv, wv, dv_ref, ov, gv, 2)):
            p = p_ref[...]
            c = _conv_silu(p, w_ref, 4)
            s = _sigmoid(c)
            a = c * s
            d = d_ref[...]
            if kind < 2:
                r = lax.rsqrt(jnp.sum(a * a, axis=-1, keepdims=True) + EPS)
                sc = DH ** -0.5 if kind == 0 else 1.0
                d = (sc * r) * (d - a * ((r * r) * jnp.sum(d * a, axis=-1, keepdims=True)))
            dc = d * _dsilu(c, s)
            dp = None
            for j in range(4):
                g_ref[j:j + 1, :] = jnp.sum(dc * _shift_down(p, 3 - j), axis=0, keepdims=True)
                t = _shift_up(dc, 3 - j) * w_ref[j:j + 1, :]
                dp = t if dp is None else dp + t
            o_ref[...] = dp.astype(BF16)

    col = lambda base: pl.BlockSpec((n, DH), lambda h: (0, base + h))
    wcol = lambda base: pl.BlockSpec((4, DH), lambda h: (0, base + h))
    return pl.pallas_call(
        body, name="prep_qkv_bwd", grid=(HEADS,),
        in_specs=[col(QB), col(KB), col(VB), wcol(QB), wcol(KB), wcol(VB), col(0), col(0), col(0)],
        out_specs=[col(0)] * 3 + [wcol(0)] * 3,
        out_shape=[jax.ShapeDtypeStruct((n, GW), BF16)] * 3 + [jax.ShapeDtypeStruct((4, GW), F32)] * 3,
        compiler_params=_params(("parallel",), 48 * 2**20),
    )(proj, proj, proj, cw, cw, cw, dq, dk, dv)


def _tri(lower_incl):
    i = lax.broadcasted_iota(jnp.int32, (CH, CH), 0)
    j = lax.broadcasted_iota(jnp.int32, (CH, CH), 1)
    return jnp.where(i >= j, 1.0, 0.0) if lower_incl else jnp.where(j >= i, 1.0, 0.0)


def _lane(shape):
    return lax.broadcasted_iota(jnp.int32, shape, 1)


def _prep_bg(proj, ad):
    n = proj.shape[0]
    nch = n // CH

    def body(p_ref, ad_ref, bg_ref, bgt_ref):
        p = p_ref[...]
        lane = _lane(p.shape)
        beta = _sigmoid(p)
        xa = p + ad_ref[1:2, :]
        sp = jnp.maximum(xa, 0.0) + jnp.log(1.0 + jnp.exp(-jnp.abs(xa)))
        g = -jnp.exp(ad_ref[0:1, :]) * sp
        gc = _dot(_tri(True), g, NN, P_CUM)
        bg = jnp.where(lane < HEADS, beta, jnp.where(lane < 2 * HEADS, gc, 0.0))
        bg_ref[...] = bg
        bgt_ref[0] = bg.T

    return pl.pallas_call(
        body, name="prep_bg", grid=(nch,),
        in_specs=[pl.BlockSpec((CH, DH), lambda i: (i, BAB)), pl.BlockSpec((2, DH), lambda i: (0, 0))],
        out_specs=[pl.BlockSpec((CH, DH), lambda i: (i, 0)), pl.BlockSpec((1, DH, CH), lambda i: (i, 0, 0))],
        out_shape=[jax.ShapeDtypeStruct((n, DH), F32), jax.ShapeDtypeStruct((nch, DH, CH), F32)],
        compiler_params=_params(("parallel",)),
    )(proj, ad)


def _prep_bg_bwd(proj, ad, dbg):
    n = proj.shape[0]
    nch = n // CH

    def body(p_ref, ad_ref, d_ref, o_ref, ga_ref, gd_ref):
        p = p_ref[...]
        d = d_ref[...]
        lane = _lane(p.shape)
        beta = _sigmoid(p)
        xa = p + ad_ref[1:2, :]
        sp = jnp.maximum(xa, 0.0) + jnp.log(1.0 + jnp.exp(-jnp.abs(xa)))
        na = -jnp.exp(ad_ref[0:1, :])
        dg = _dot(_tri(False), d, NN, P_CUM)
        da = dg * na * _sigmoid(xa)
        is_g = (lane >= HEADS) & (lane < 2 * HEADS)
        o_ref[...] = jnp.where(lane < HEADS, d * beta * (1.0 - beta), jnp.where(is_g, da, 0.0)).astype(BF16)
        ga = jnp.sum(jnp.where(is_g, dg * na * sp, 0.0), axis=0, keepdims=True)
        gd = jnp.sum(jnp.where(is_g, da, 0.0), axis=0, keepdims=True)

        @pl.when(pl.program_id(0) == 0)
        def _():
            ga_ref[...] = jnp.zeros_like(ga_ref)
            gd_ref[...] = jnp.zeros_like(gd_ref)

        ga_ref[...] += ga
        gd_ref[...] += gd

    one = pl.BlockSpec((1, DH), lambda i: (0, 0))
    return pl.pallas_call(
        body, name="prep_bg_bwd", grid=(nch,),
        in_specs=[pl.BlockSpec((CH, DH), lambda i: (i, BAB)), pl.BlockSpec((2, DH), lambda i: (0, 0)),
                  pl.BlockSpec((CH, DH), lambda i: (i, 0))],
        out_specs=[pl.BlockSpec((CH, DH), lambda i: (i, 0)), one, one],
        out_shape=[jax.ShapeDtypeStruct((n, DH), BF16), jax.ShapeDtypeStruct((1, DH), F32),
                   jax.ShapeDtypeStruct((1, DH), F32)],
        compiler_params=_params(("arbitrary",)),
    )(proj, ad, dbg)


def _gdn_out(o, proj, wg):
    n = o.shape[0]

    def body(o_ref, z_ref, w_ref, y_ref):
        ov, z = o_ref[...], z_ref[...]
        r = lax.rsqrt(jnp.mean(ov * ov, axis=-1, keepdims=True) + EPS)
        y_ref[...] = (ov * r * w_ref[...] * (z * _sigmoid(z))).astype(BF16)

    return pl.pallas_call(
        body, name="gdn_out", grid=(HEADS,),
        in_specs=[pl.BlockSpec((n, DH), lambda h: (0, h)), pl.BlockSpec((n, DH), lambda h: (0, ZB + h)),
                  pl.BlockSpec((1, DH), lambda h: (0, 0))],
        out_specs=pl.BlockSpec((n, DH), lambda h: (0, h)),
        out_shape=jax.ShapeDtypeStruct((n, GW), BF16),
        compiler_params=_params(("parallel",)),
    )(o, proj, wg)


def _gdn_out_bwd(o, proj, wg, dmix):
    n = o.shape[0]

    def body(o_ref, z_ref, w_ref, d_ref, do_ref, dz_ref, gw_ref):
        ov, z, d, w = o_ref[...], z_ref[...], d_ref[...], w_ref[...]
        r = lax.rsqrt(jnp.mean(ov * ov, axis=-1, keepdims=True) + EPS)
        nrm = ov * r
        s = _sigmoid(z)
        dz_ref[...] = (d * (nrm * w) * _dsilu(z, s)).astype(BF16)
        dn_w = d * (z * s)
        gw = jnp.sum(dn_w * nrm, axis=0, keepdims=True)
        dn = dn_w * w
        do_ref[...] = r * (dn - nrm * jnp.mean(dn * nrm, axis=-1, keepdims=True))

        @pl.when(pl.program_id(0) == 0)
        def _():
            gw_ref[...] = jnp.zeros_like(gw_ref)

        gw_ref[...] += gw

    return pl.pallas_call(
        body, name="gdn_out_bwd", grid=(HEADS,),
        in_specs=[pl.BlockSpec((n, DH), lambda h: (0, h)), pl.BlockSpec((n, DH), lambda h: (0, ZB + h)),
                  pl.BlockSpec((1, DH), lambda h: (0, 0)), pl.BlockSpec((n, DH), lambda h: (0, h))],
        out_specs=[pl.BlockSpec((n, DH), lambda h: (0, h)), pl.BlockSpec((n, DH), lambda h: (0, h)),
                   pl.BlockSpec((1, DH), lambda h: (0, 0))],
        out_shape=[jax.ShapeDtypeStruct((n, GW), F32), jax.ShapeDtypeStruct((n, GW), BF16),
                   jax.ShapeDtypeStruct((1, DH), F32)],
        compiler_params=_params(("arbitrary",)),
    )(o, proj, wg, dmix)


def _conv_branch(proj, w3, b):
    n = proj.shape[0]

    def body(gb_ref, gc_ref, hc_ref, zc_ref, w_ref, b_ref, y_ref):
        u = gc_ref[...] * hc_ref[...]
        cc = _conv_silu(u, w_ref, 3) + b_ref[...]
        z = zc_ref[...]
        y_ref[...] = (gb_ref[...] * cc * (z * _sigmoid(z))).astype(BF16)

    col = lambda base: pl.BlockSpec((n, DH), lambda h: (0, base + h))
    return pl.pallas_call(
        body, name="conv_branch", grid=(HEADS,),
        in_specs=[col(GBB), col(GCB), col(HCB), col(ZCB), pl.BlockSpec((3, DH), lambda h: (0, h)),
                  pl.BlockSpec((1, DH), lambda h: (0, h))],
        out_specs=col(0), out_shape=jax.ShapeDtypeStruct((n, GW), BF16),
        compiler_params=_params(("parallel",), 40 * 2**20),
    )(proj, proj, proj, proj, w3, b)


def _conv_branch_bwd(proj, w3, b, dmix):
    n = proj.shape[0]

    def body(gb_ref, gc_ref, hc_ref, zc_ref, w_ref, b_ref, d_ref, dgb, dgc, dhc, dzc, gw_ref, gbias_ref):
        gb, gcv, hc, z, d = gb_ref[...], gc_ref[...], hc_ref[...], zc_ref[...], d_ref[...]
        u = gcv * hc
        cc = _conv_silu(u, w_ref, 3) + b_ref[...]
        s = _sigmoid(z)
        dzc[...] = (d * (gb * cc) * _dsilu(z, s)).astype(BF16)
        dp = d * (z * s)
        dgb[...] = (dp * cc).astype(BF16)
        dcc = dp * gb
        gbias_ref[...] = jnp.sum(dcc, axis=0, keepdims=True)
        du = None
        for j in range(3):
            gw_ref[j:j + 1, :] = jnp.sum(dcc * _shift_down(u, 2 - j), axis=0, keepdims=True)
            t = _shift_up(dcc, 2 - j) * w_ref[j:j + 1, :]
            du = t if du is None else du + t
        dgc[...] = (du * hc).astype(BF16)
        dhc[...] = (du * gcv).astype(BF16)

    col = lambda base: pl.BlockSpec((n, DH), lambda h: (0, base + h))
    big = jax.ShapeDtypeStruct((n, GW), BF16)
    return pl.pallas_call(
        body, name="conv_branch_bwd", grid=(HEADS,),
        in_specs=[col(GBB), col(GCB), col(HCB), col(ZCB), pl.BlockSpec((3, DH), lambda h: (0, h)),
                  pl.BlockSpec((1, DH), lambda h: (0, h)), col(GW // DH)],
        out_specs=[col(0)] * 4 + [pl.BlockSpec((3, DH), lambda h: (0, h)), pl.BlockSpec((1, DH), lambda h: (0, h))],
        out_shape=[big] * 4 + [jax.ShapeDtypeStruct((3, GW), F32), jax.ShapeDtypeStruct((1, GW), F32)],
        compiler_params=_params(("parallel",), 48 * 2**20),
    )(proj, proj, proj, proj, w3, b, dmix)


def _final_loss(out, tgt, wf):
    n, d = out.shape
    tr = min(256, n)

    def body(o_ref, t_ref, w_ref, do_ref, dob_ref, gw_ref, loss_ref):
        ov, w = o_ref[...], w_ref[...]
        r = lax.rsqrt(jnp.mean(ov * ov, axis=-1, keepdims=True) + EPS)
        nrm = ov * r
        e = nrm * w - t_ref[...]
        dy = e * (1.0 / d)
        dn = dy * w
        dout = r * (dn - nrm * jnp.mean(dn * nrm, axis=-1, keepdims=True))
        do_ref[...] = dout
        dob_ref[...] = dout.astype(BF16)

        @pl.when(pl.program_id(0) == 0)
        def _():
            gw_ref[...] = jnp.zeros_like(gw_ref)
            loss_ref[...] = jnp.zeros_like(loss_ref)

        gw_ref[...] += jnp.sum(dy * nrm, axis=0, keepdims=True)
        loss_ref[...] += (0.5 / d) * jnp.sum(jnp.sum(e * e, axis=-1, keepdims=True), axis=0, keepdims=True)

    row = pl.BlockSpec((tr, d), lambda i: (i, 0))
    return pl.pallas_call(
        body, name="final_loss", grid=(n // tr,),
        in_specs=[row, row, pl.BlockSpec((1, d), lambda i: (0, 0))],
        out_specs=[row, row, pl.BlockSpec((1, d), lambda i: (0, 0)), pl.BlockSpec((1, 1), lambda i: (0, 0))],
        out_shape=[jax.ShapeDtypeStruct((n, d), F32), jax.ShapeDtypeStruct((n, d), BF16),
                   jax.ShapeDtypeStruct((1, d), F32), jax.ShapeDtypeStruct((1, 1), F32)],
        compiler_params=_params(("arbitrary",)),
    )(out, tgt, wf)


def _rms_in_bwd(x, w, dh, dout):
    n, d = x.shape
    tr = min(256, n)

    def body(x_ref, w_ref, dh_ref, do_ref, dx_ref, gw_ref):
        xv, dhv = x_ref[...], dh_ref[...]
        r = lax.rsqrt(jnp.mean(xv * xv, axis=-1, keepdims=True) + EPS)
        xn = xv * r
        dxn = dhv * w_ref[...]
        dx_ref[...] = r * (dxn - xn * jnp.mean(dxn * xn, axis=-1, keepdims=True)) + do_ref[...]

        @pl.when(pl.program_id(0) == 0)
        def _():
            gw_ref[...] = jnp.zeros_like(gw_ref)

        gw_ref[...] += jnp.sum(dhv * xn, axis=0, keepdims=True)

    row = pl.BlockSpec((tr, d), lambda i: (i, 0))
    one = pl.BlockSpec((1, d), lambda i: (0, 0))
    return pl.pallas_call(
        body, name="rms_in_bwd", grid=(n // tr,),
        in_specs=[row, one, row, row], out_specs=[row, one],
        out_shape=[jax.ShapeDtypeStruct((n, d), F32), jax.ShapeDtypeStruct((1, d), F32)],
        compiler_params=_params(("arbitrary",)),
    )(x, w, dh, dout)


def _ij():
    i = lax.broadcasted_iota(jnp.int32, (CH, CH), 0)
    j = lax.broadcasted_iota(jnp.int32, (CH, CH), 1)
    return i, j


def _unit_lower_inverse(mats):
    i, j = _ij()
    eye = jnp.where(i == j, 1.0, 0.0)
    same16 = (i // 16) == (j // 16)
    same32 = (i // 32) == (j // 32)
    mm = lambda xs, ys: [_dot(x, y, NN, P_INV) for x, y in zip(xs, ys)]
    n1 = [jnp.where(same16, -a, 0.0) for a in mats]
    n2 = mm(n1, n1)
    n4 = mm(n2, n2)
    n8 = mm(n4, n4)
    t = [eye + x1 + x2 + x3 for x1, x2, x3 in zip(n1, n2, mm(n1, n2))]
    t = [x + y for x, y in zip(t, mm(t, n4))]
    t = [x + y for x, y in zip(t, mm(t, n8))]
    a1 = [jnp.where(same32 & jnp.logical_not(same16), a, 0.0) for a in mats]
    t = [x - y for x, y in zip(t, mm(t, mm(a1, t)))]
    a2 = [jnp.where(same32, 0.0, a) for a in mats]
    t = [x - y for x, y in zip(t, mm(t, mm(a2, t)))]
    return t


def _head_vectors(bg, bgt, h):
    bcol = bg[:, h:h + 1]
    gcol = bg[:, HEADS + h:HEADS + h + 1]
    grow = bgt[HEADS + h:HEADS + h + 1, :]
    return bcol, gcol, grow


def _decay(gcol, grow):
    i, j = _ij()
    return jnp.where(i >= j, jnp.exp(jnp.where(i >= j, gcol - grow, 0.0)), 0.0)


def _gdn_intra(q, k, v, bg, bgt):
    n = q.shape[0]
    nch = n // CH

    def body(q_ref, k_ref, v_ref, bg_ref, bgt_ref, u_ref, w_ref, p_ref, t_ref):
        bg, bgt = bg_ref[...], bgt_ref[0]
        i, j = _ij()
        sls = [slice(h * DH, (h + 1) * DH) for h in range(HEADS)]
        ks = [k_ref[:, sl] for sl in sls]
        vecs = [_head_vectors(bg, bgt, h) for h in range(HEADS)]
        decs = [_decay(gcol, grow) for _, gcol, grow in vecs]
        kks = [_dot(kh, kh, NT, P_GRAM) for kh in ks]
        qks = [_dot(q_ref[:, sl], kh, NT, P_GRAM) for sl, kh in zip(sls, ks)]
        ts = _unit_lower_inverse([jnp.where(i > j, bcol * kk * dec, 0.0)
                                  for (bcol, _, _), kk, dec in zip(vecs, kks, decs)])
        us = [_dot(t, v_ref[:, sl] * bcol, NN, P_SOL) for t, sl, (bcol, _, _) in zip(ts, sls, vecs)]
        ws = [_dot(t, kh * (bcol * jnp.exp(gcol)), NN, P_SOL) for t, kh, (bcol, gcol, _) in zip(ts, ks, vecs)]
        for h, sl in enumerate(sls):
            p_ref[0, h] = qks[h] * decs[h]
            t_ref[0, h] = ts[h]
            u_ref[:, sl] = us[h]
            w_ref[:, sl] = ws[h]

    row = pl.BlockSpec((CH, GW), lambda c: (c, 0))
    sq = pl.BlockSpec((1, HEADS, CH, CH), lambda c: (c, 0, 0, 0))
    big = jax.ShapeDtypeStruct((n, GW), F32)
    sqs = jax.ShapeDtypeStruct((nch, HEADS, CH, CH), F32)
    return pl.pallas_call(
        body, name="gdn_intra", grid=(nch,),
        in_specs=[row, row, row, pl.BlockSpec((CH, DH), lambda c: (c, 0)),
                  pl.BlockSpec((1, DH, CH), lambda c: (c, 0, 0))],
        out_specs=[row, row, sq, sq], out_shape=[big, big, sqs, sqs],
        compiler_params=_params(("parallel",)),
    )(q, k, v, bg, bgt)


def _gdn_scan(q, k, bg, u, w, p):
    n = q.shape[0]
    nch = n // CH

    def body(q_ref, k_ref, bg_ref, u_ref, w_ref, p_ref, o_ref, vn_ref, s_out, s_scr):
        @pl.when(pl.program_id(0) == 0)
        def _():
            s_scr[...] = jnp.zeros_like(s_scr)

        bg = bg_ref[...]
        hs = range(HEADS)
        sls = [slice(h * DH, (h + 1) * DH) for h in hs]
        gcols = [bg[:, HEADS + h:HEADS + h + 1] for h in hs]
        glasts = [g[CH - 1:CH, :] for g in gcols]
        ss = [s_scr[h] for h in hs]
        wss = [_dot(w_ref[:, sl], s, NN, P_SCAN) for sl, s in zip(sls, ss)]
        oqs = [_dot(q_ref[:, sl] * jnp.exp(g), s, NN, P_SCAN) for sl, s, g in zip(sls, ss, gcols)]
        vns = [u_ref[:, sl] - x for sl, x in zip(sls, wss)]
        ops = [_dot(p_ref[0, h], vn, NN, P_SCAN) for h, vn in zip(hs, vns)]
        sns = [_dot(k_ref[:, sl] * jnp.exp(gl - g), vn, TN, P_SCAN)
               for sl, gl, g, vn in zip(sls, glasts, gcols, vns)]
        for h, sl in enumerate(sls):
            s_out[0, :, sl] = ss[h]
            vn_ref[:, sl] = vns[h]
            o_ref[:, sl] = oqs[h] + ops[h]
            s_scr[h] = ss[h] * jnp.exp(glasts[h]) + sns[h]

    row = pl.BlockSpec((CH, GW), lambda c: (c, 0))
    big = jax.ShapeDtypeStruct((n, GW), F32)
    return pl.pallas_call(
        body, name="gdn_scan", grid=(nch,),
        in_specs=[row, row, pl.BlockSpec((CH, DH), lambda c: (c, 0)), row, row,
                  pl.BlockSpec((1, HEADS, CH, CH), lambda c: (c, 0, 0, 0))],
        out_specs=[row, row, pl.BlockSpec((1, DH, GW), lambda c: (c, 0, 0))],
        out_shape=[big, big, jax.ShapeDtypeStruct((nch, DH, GW), F32)],
        scratch_shapes=[pltpu.VMEM((HEADS, DH, DH), F32)],
        compiler_params=_params(("arbitrary",)),
    )(q, k, bg, u, w, p)


def _gdn_scan_bwd(q, k, bg, w, p, vn, s_in, do):
    n = q.shape[0]
    nch = n // CH
    rev = lambda c: nch - 1 - c

    def body(q_ref, k_ref, bg_ref, w_ref, p_ref, vn_ref, s_ref, do_ref,
             dqg_ref, dp_ref, du_ref, dw_ref, dks_ref, dgam_ref, ds_scr):
        @pl.when(pl.program_id(0) == 0)
        def _():
            ds_scr[...] = jnp.zeros_like(ds_scr)

        bg = bg_ref[...]
        lane = _lane((1, DH))
        hs = range(HEADS)
        sls = [slice(h * DH, (h + 1) * DH) for h in hs]
        gcols = [bg[:, HEADS + h:HEADS + h + 1] for h in hs]
        glasts = [g[CH - 1:CH, :] for g in gcols]
        ss = [s_ref[0, :, sl] for sl in sls]
        dss = [ds_scr[h] for h in hs]
        dos = [do_ref[:, sl] for sl in sls]
        vnl = [vn_ref[:, sl] for sl in sls]
        dqgs = [_dot(d, s, NT, P_SCANB) for d, s in zip(dos, ss)]
        dps = [_dot(d, vn, NT, P_SCANB) for d, vn in zip(dos, vnl)]
        dvn1 = [_dot(p_ref[0, h], d, TN, P_SCANB) for h, d in zip(hs, dos)]
        dvn2 = [_dot(k_ref[:, sl] * jnp.exp(gl - g), ds, NN, P_SCANB)
                for sl, gl, g, ds in zip(sls, glasts, gcols, dss)]
        dkss = [_dot(vn, ds, NT, P_SCANB) for vn, ds in zip(vnl, dss)]
        dsq = [_dot(q_ref[:, sl] * jnp.exp(g), d, TN, P_SCANB) for sl, g, d in zip(sls, gcols, dos)]
        dvns = [a + b for a, b in zip(dvn1, dvn2)]
        dws = [_dot(dvn, s, NT, P_SCANB) for dvn, s in zip(dvns, ss)]
        dsw = [_dot(w_ref[:, sl], dvn, TN, P_SCANB) for sl, dvn in zip(sls, dvns)]
        dgam = jnp.zeros((1, DH), F32)
        for h, sl in enumerate(sls):
            dqg_ref[:, sl] = dqgs[h]
            dp_ref[0, h] = dps[h]
            du_ref[:, sl] = dvns[h]
            dw_ref[:, sl] = -dws[h]
            dks_ref[:, sl] = dkss[h]
            tot = jnp.sum(jnp.sum(dss[h] * ss[h], axis=-1, keepdims=True), axis=0, keepdims=True)
            dgam = dgam + jnp.where(lane == h, tot, 0.0)
            ds_scr[h] = dss[h] * jnp.exp(glasts[h]) + dsq[h] - dsw[h]
        dgam_ref[0] = jnp.broadcast_to(dgam, (8, DH))

    row = pl.BlockSpec((CH, GW), lambda c: (rev(c), 0))
    sq = pl.BlockSpec((1, HEADS, CH, CH), lambda c: (rev(c), 0, 0, 0))
    big = jax.ShapeDtypeStruct((n, GW), F32)
    return pl.pallas_call(
        body, name="gdn_scan_bwd", grid=(nch,),
        in_specs=[row, row, pl.BlockSpec((CH, DH), lambda c: (rev(c), 0)), row, sq, row,
                  pl.BlockSpec((1, DH, GW), lambda c: (rev(c), 0, 0)), row],
        out_specs=[row, sq, row, row, row, pl.BlockSpec((1, 8, DH), lambda c: (rev(c), 0, 0))],
        out_shape=[big, jax.ShapeDtypeStruct((nch, HEADS, CH, CH), F32), big, big, big,
                   jax.ShapeDtypeStruct((nch, 8, DH), F32)],
        scratch_shapes=[pltpu.VMEM((HEADS, DH, DH), F32)],
        compiler_params=_params(("arbitrary",)),
    )(q, k, bg, w, p, vn, s_in, do)


def _gdn_intra_bwd(q, k, v, bg, bgt, t, u, w, p, dqg, dp, du, dw, dks, dgam):
    n = q.shape[0]
    nch = n // CH

    def body(q_ref, k_ref, v_ref, bg_ref, bgt_ref, t_ref, u_ref, w_ref, p_ref,
             dqg_ref, dp_ref, du_ref, dw_ref, dks_ref, dgam_ref, dq_ref, dk_ref, dv_ref, dbg_ref):
        bg, bgt = bg_ref[...], bgt_ref[0]
        dgam_all = dgam_ref[0]
        i, j = _ij()
        rows1 = lax.broadcasted_iota(jnp.int32, (CH, 1), 0)
        lane = _lane((CH, DH))
        dbg = jnp.zeros((CH, DH), F32)
        rsum = lambda x: jnp.sum(x, axis=-1, keepdims=True)
        hs = range(HEADS)
        sls = [slice(h * DH, (h + 1) * DH) for h in hs]
        qs = [q_ref[:, sl] for sl in sls]
        ks = [k_ref[:, sl] for sl in sls]
        vecs = [_head_vectors(bg, bgt, h) for h in hs]
        decs = [_decay(gcol, grow) for _, gcol, grow in vecs]
        ths = [t_ref[0, h] for h in hs]
        drus = [_dot(th, du_ref[:, sl], TN, P_BWD) for th, sl in zip(ths, sls)]
        drws = [_dot(th, dw_ref[:, sl], TN, P_BWD) for th, sl in zip(ths, sls)]
        kks = [_dot(kh, kh, NT, P_GRAM) for kh in ks]
        da1 = [_dot(dru, u_ref[:, sl], NT, P_BWD) for dru, sl in zip(drus, sls)]
        da2 = [_dot(drw, w_ref[:, sl], NT, P_BWD) for drw, sl in zip(drws, sls)]
        das = [jnp.where(i > j, -(x + y), 0.0) for x, y in zip(da1, da2)]
        dkks = [da * bcol * dec for da, (bcol, _, _), dec in zip(das, vecs, decs)]
        dqks = [dp_ref[0, h] * dec for h, dec in zip(hs, decs)]
        dq_ps = [_dot(dqk, kh, NN, P_BWD) for dqk, kh in zip(dqks, ks)]
        dk_ps = [_dot(dqk, qh, TN, P_BWD) for dqk, qh in zip(dqks, qs)]
        dk_as = [_dot(dkk, kh, NN, P_BWD) for dkk, kh in zip(dkks, ks)]
        dk_bs = [_dot(dkk, kh, TN, P_BWD) for dkk, kh in zip(dkks, ks)]
        for h, sl in enumerate(sls):
            qh, kh, vh = qs[h], ks[h], v_ref[:, sl]
            bcol, gcol, _ = vecs[h]
            dec, dru, drw, da, kk = decs[h], drus[h], drws[h], das[h], kks[h]
            gam = jnp.exp(gcol)
            glast = gcol[CH - 1:CH, :]
            e = jnp.exp(glast - gcol)
            kg = kh * gam
            dv_ref[:, sl] = bcol * dru
            dbeta = rsum(dru * vh) + rsum(drw * kg) + rsum(da * kk * dec)
            dgc = rsum(drw * kg) * bcol
            dqg = dqg_ref[:, sl]
            dksh = dks_ref[:, sl]
            dq_ref[:, sl] = gam * dqg + dq_ps[h]
            dk_ref[:, sl] = (bcol * gam) * drw + dk_ps[h] + dk_as[h] + dk_bs[h] + dksh * e
            tk = rsum(dksh * kh) * e
            mdec = da * (bcol * kk * dec) + dp_ref[0, h] * p_ref[0, h]
            col = rsum(jnp.where(i == j, jnp.sum(mdec, axis=0, keepdims=True), 0.0))
            dgc = dgc + rsum(mdec) - col
            dgc = dgc + rsum(dqg * qh) * gam - tk
            dglast = jnp.sum(tk, axis=0, keepdims=True) + dgam_all[0:1, h:h + 1] * jnp.exp(glast)
            dgc = dgc + jnp.where(rows1 == CH - 1, dglast, 0.0)
            dbg = dbg + jnp.where(lane == h, dbeta, 0.0) + jnp.where(lane == HEADS + h, dgc, 0.0)
        dbg_ref[...] = dbg

    row = pl.BlockSpec((CH, GW), lambda c: (c, 0))
    sq = pl.BlockSpec((1, HEADS, CH, CH), lambda c: (c, 0, 0, 0))
    small = pl.BlockSpec((CH, DH), lambda c: (c, 0))
    big = jax.ShapeDtypeStruct((n, GW), F32)
    return pl.pallas_call(
        body, name="gdn_intra_bwd", grid=(nch,),
        in_specs=[row, row, row, small, pl.BlockSpec((1, DH, CH), lambda c: (c, 0, 0)), sq, row, row, sq,
                  row, sq, row, row, row, pl.BlockSpec((1, 8, DH), lambda c: (c, 0, 0))],
        out_specs=[row, row, row, small],
        out_shape=[big, big, big, jax.ShapeDtypeStruct((n, DH), F32)],
        compiler_params=_params(("parallel",)),
    )(q, k, v, bg, bgt, t, u, w, p, dqg, dp, du, dw, dks, dgam)


def _local_step(x, tgt, w_cat, w_out, norm_in_w, cqw, ad, gdn_norm_w, conv_w, conv_b, final_norm_w):
    h = _rms_in(x, norm_in_w)
    proj = _matmul(h, w_cat, NN, F32, 512, 640, 1024, "mm_proj")
    q, k, v = _prep_qkv(proj, cqw)
    bg, bgt = _prep_bg(proj, ad)
    u, w, p, t = _gdn_intra(q, k, v, bg, bgt)
    o, vn, s_in = _gdn_scan(q, k, bg, u, w, p)
    mix = jnp.concatenate([_gdn_out(o, proj, gdn_norm_w), _conv_branch(proj, conv_w, conv_b)], axis=1)
    out = _matmul(mix, w_out, NN, F32, 512, 512, 2048, "mm_out", add=x)
    dout, dout_b, g_fn, loss = _final_loss(out, tgt, final_norm_w)

    dmix = _matmul(dout_b, w_out, NT, F32, 512, 1024, 1024, "mm_dmix")
    g_wout = _matmul(mix, dout_b, TN, BF16, 512, 512, 2048, "mm_gwout")
    do, dz, g_gn = _gdn_out_bwd(o, proj, gdn_norm_w, dmix)
    dgb, dgc, dhc, dzc, g_cw, g_cb = _conv_branch_bwd(proj, conv_w, conv_b, dmix)
    dqg, dp, du, dw, dks, dgam = _gdn_scan_bwd(q, k, bg, w, p, vn, s_in, do)
    dq, dk, dv, dbg = _gdn_intra_bwd(q, k, v, bg, bgt, t, u, w, p, dqg, dp, du, dw, dks, dgam)
    dpq, dpk, dpv, gq, gk, gv = _prep_qkv_bwd(proj, cqw, dq, dk, dv)
    dba, g_al, g_dt = _prep_bg_bwd(proj, ad, dbg)
    dproj = jnp.concatenate([dpq, dpk, dpv, dz, dba, dgb, dgc, dhc, dzc], axis=1)
    g_wcat = _matmul(h, dproj, TN, BF16, 512, 640, 2048, "mm_gwin")
    dh = _matmul(dproj, w_cat, NT, F32, 512, 1024, 640, "mm_dh")
    gx, g_nin = _rms_in_bwd(x, norm_in_w, dh, dout)
    g_cqw = jnp.concatenate([gq, gk, gv], axis=1)
    return gx, g_wcat, g_wout, dict(nin=g_nin, cb=g_cb, fn=g_fn, al=g_al, dt=g_dt, gn=g_gn, cqw=g_cqw,
                                    cw=g_cw, loss=loss)


def _place():
    x, y, c = lax.axis_index("x"), lax.axis_index("y"), lax.axis_index("c")
    chips = [(1 - x, y), (x, 1 - y), (1 - x, 1 - y)]
    return x, y, c, chips


def _all_gather_shards(shards):
    nsh = len(shards)

    def body(*refs):
        ins, outs = refs[:nsh], refs[nsh:2 * nsh]
        send_sems, recv_sems, loc_sems = refs[2 * nsh:]
        x, y, c, chips = _place()
        mine = 2 * x + y
        local = [pltpu.make_async_copy(ins[a], outs[a].at[mine], loc_sems.at[a]) for a in range(nsh)]
        for cp in local:
            cp.start()
        remote = []
        for jj, (px, py) in enumerate(chips):
            for a in range(nsh):
                remote.append(pltpu.make_async_remote_copy(
                    src_ref=ins[a], dst_ref=outs[a].at[mine],
                    send_sem=send_sems.at[jj * nsh + a], recv_sem=recv_sems.at[jj * nsh + a],
                    device_id=(px, py, c), device_id_type=MESH))
        for cp in remote:
            cp.start()
        for cp in remote:
            cp.wait_recv()
        for cp in remote:
            cp.wait_send()
        for cp in local:
            cp.wait()

    return pl.pallas_call(
        body, name="all_gather_shards",
        in_specs=[ANY] * nsh, out_specs=[ANY] * nsh,
        out_shape=[jax.ShapeDtypeStruct((4,) + s.shape, s.dtype) for s in shards],
        scratch_shapes=[pltpu.SemaphoreType.DMA((3 * nsh,)), pltpu.SemaphoreType.DMA((3 * nsh,)),
                        pltpu.SemaphoreType.DMA((nsh,))],
    )(*shards)


def _exchange_grads(blocks, pack):
    nb = len(blocks)

    def body(*refs):
        ins, pack_ref = refs[:nb], refs[nb]
        lands, packs = refs[nb + 1:2 * nb + 1], refs[2 * nb + 1]
        send_sems, recv_sems, psend, precv, loc_sems = refs[2 * nb + 2:]
        x, y, c, chips = _place()
        me = 4 * x + 2 * y + c
        local = [pltpu.make_async_copy(pack_ref, packs.at[me], loc_sems.at[nb])]
        local += [pltpu.make_async_copy(ins[a].at[2 * x + y], lands[a].at[3], loc_sems.at[a]) for a in range(nb)]
        for cp in local:
            cp.start()
        remote = []
        for jj, (px, py) in enumerate(chips):
            for a in range(nb):
                remote.append(pltpu.make_async_remote_copy(
                    src_ref=ins[a].at[2 * px + py], dst_ref=lands[a].at[jj],
                    send_sem=send_sems.at[jj * nb + a], recv_sem=recv_sems.at[jj * nb + a],
                    device_id=(px, py, c), device_id_type=MESH))
        for r in range(1, 8):
            dx, dy, dc = (r >> 2) & 1, (r >> 1) & 1, r & 1
            peer = (x + dx - 2 * x * dx, y + dy - 2 * y * dy, c + dc - 2 * c * dc)
            remote.append(pltpu.make_async_remote_copy(
                src_ref=pack_ref, dst_ref=packs.at[me], send_sem=psend.at[r - 1], recv_sem=precv.at[r - 1],
                device_id=peer, device_id_type=MESH))
        for cp in remote:
            cp.start()
        for cp in remote:
            cp.wait_recv()
        for cp in remote:
            cp.wait_send()
        for cp in local:
            cp.wait()

    out_shape = [jax.ShapeDtypeStruct((4,) + b.shape[1:], b.dtype) for b in blocks]
    out_shape.append(jax.ShapeDtypeStruct((8,) + pack.shape, pack.dtype))
    return pl.pallas_call(
        body, name="exchange_grads",
        in_specs=[ANY] * (nb + 1), out_specs=[ANY] * (nb + 1), out_shape=out_shape,
        scratch_shapes=[pltpu.SemaphoreType.DMA((3 * nb,)), pltpu.SemaphoreType.DMA((3 * nb,)),
                        pltpu.SemaphoreType.DMA((7,)), pltpu.SemaphoreType.DMA((7,)), pltpu.SemaphoreType.DMA((nb + 1,))],
    )(*blocks, pack)


def _swap_with_sibling(parts):
    npart = len(parts)

    def body(*refs):
        ins, outs = refs[:npart], refs[npart:2 * npart]
        send_sems, recv_sems = refs[2 * npart:]
        x, y, c, _ = _place()
        cps = [pltpu.make_async_remote_copy(
            src_ref=ins[a], dst_ref=outs[a], send_sem=send_sems.at[a], recv_sem=recv_sems.at[a],
            device_id=(x, y, 1 - c), device_id_type=MESH) for a in range(npart)]
        for cp in cps:
            cp.start()
        for cp in cps:
            cp.wait_recv()
        for cp in cps:
            cp.wait_send()

    return pl.pallas_call(
        body, name="swap_with_sibling",
        in_specs=[ANY] * npart, out_specs=[ANY] * npart,
        out_shape=[jax.ShapeDtypeStruct(p.shape, p.dtype) for p in parts],
        scratch_shapes=[pltpu.SemaphoreType.DMA((npart,)), pltpu.SemaphoreType.DMA((npart,))],
    )(*parts)


def _sum_blocks(land, rows, name):
    _, r, cdim = land.shape
    rows = min(rows, r)

    def body(land_ref, o_ref):
        acc = land_ref[3].astype(F32)
        for jj in range(3):
            acc = acc + land_ref[jj].astype(F32)
        o_ref[...] = acc

    return pl.pallas_call(
        body, name=name, grid=(r // rows,),
        in_specs=[pl.BlockSpec((4, rows, cdim), lambda i: (0, i, 0))],
        out_specs=pl.BlockSpec((rows, cdim), lambda i: (i, 0)),
        out_shape=jax.ShapeDtypeStruct((r, cdim), F32),
        compiler_params=_params(("parallel",), 40 * 2**20),
    )(land)


def _sum_packs(packs):
    def body(p_ref, o_ref):
        acc = p_ref[0]
        for d in range(1, 8):
            acc = acc + p_ref[d]
        o_ref[...] = acc

    return pl.pallas_call(
        body, name="sum_packs", out_shape=jax.ShapeDtypeStruct(packs.shape[1:], F32),
    )(packs)


def _adamw(w, m, v, g1, g2, rows, name):
    r, cdim = w.shape
    rows = min(rows, r)
    c1 = 1.0 / (1.0 - ADAM_B1 ** ADAM_STEP)
    c2 = 1.0 / (1.0 - ADAM_B2 ** ADAM_STEP)

    def body(*refs):
        if g2 is None:
            w_ref, m_ref, v_ref, g_ref, go, do, mo, vo = refs
            g = g_ref[...]
        else:
            w_ref, m_ref, v_ref, g_ref, g2_ref, go, do, mo, vo = refs
            g = g_ref[...] + g2_ref[...]
        mn = ADAM_B1 * m_ref[...] + (1.0 - ADAM_B1) * g
        vn = ADAM_B2 * v_ref[...] + (1.0 - ADAM_B2) * (g * g)
        go[...] = g
        mo[...] = mn
        vo[...] = vn
        do[...] = -ADAM_LR * ((mn * c1) / (jnp.sqrt(vn * c2) + ADAM_EPS) + ADAM_WD * w_ref[...])

    blk = pl.BlockSpec((rows, cdim), lambda i: (i, 0))
    args = [w, m, v, g1] + ([] if g2 is None else [g2])
    shp = jax.ShapeDtypeStruct((r, cdim), F32)
    return pl.pallas_call(
        body, name=name, grid=(r // rows,),
        in_specs=[blk] * len(args), out_specs=[blk] * 4, out_shape=[shp] * 4,
        compiler_params=_params(("parallel",), 18 * rows * cdim * 4 + 8 * 2**20),
    )(*args)


def _pad_lanes(a, width):
    return jnp.pad(a, ((0, 0), (0, width - a.shape[1])))


def _cat_layout(w_full):
    return jnp.concatenate([w_full[:, :4096], _pad_lanes(w_full[:, 4096:4112], DH), w_full[:, 4112:]], axis=1)


def _uncat_layout(g_cat):
    return jnp.concatenate([g_cat[:, :4096], g_cat[:, 4096:4112], g_cat[:, 4096 + DH:]], axis=1)


def _gathered_to_full(g):
    return jnp.transpose(g, (1, 0, 2)).reshape(g.shape[1], 4 * g.shape[2])


def _full_to_blocks(a):
    r, c4 = a.shape
    return jnp.transpose(a.reshape(r, 4, c4 // 4), (1, 0, 2))


def _row(a):
    return _pad_lanes(a.reshape(1, -1), 1024)


def _small_pack(nin, cb, fn, al, dt, gn, cqw_shard, cw_shard):
    ad = jnp.concatenate([al.reshape(1, -1), dt.reshape(1, -1)], axis=1)
    rows = [_row(nin), _row(cb), _row(fn), _row(ad), _row(gn), cqw_shard.reshape(3, 1024), _row(cw_shard)]
    out = jnp.concatenate(rows, axis=0)
    return jnp.pad(out, ((0, 16 - out.shape[0]), (0, 0)))


def kernel(x, norm_in_w, w_in, conv_qkv_w, A_log, dt_bias, gdn_norm_w, conv_w, conv_b, w_out, final_norm_w, loss_target, m_norm_in_w, m_w_in, m_conv_qkv_w, m_A_log, m_dt_bias, m_gdn_norm_w, m_conv_w, m_conv_b, m_w_out, m_final_norm_w, v_norm_in_w, v_w_in, v_conv_qkv_w, v_A_log, v_dt_bias, v_gdn_norm_w, v_conv_w, v_conv_b, v_w_out, v_final_norm_w):
    chip = 2 * lax.axis_index("x") + lax.axis_index("y")
    wi_b = _cast_bf16(w_in[0], 256, "cast_w_in")
    wo_b = _cast_bf16(w_out[0], 256, "cast_w_out")
    wi_g, wo_g, cq_g, cw_g = _all_gather_shards([wi_b, wo_b, conv_qkv_w[0], conv_w[0]])
    w_cat = _cat_layout(_gathered_to_full(wi_g))
    w_out_full = wo_g.reshape(2 * GW, -1)
    cqw = _gathered_to_full(cq_g)
    cw = _gathered_to_full(cw_g)
    ad = _pad_lanes(jnp.concatenate([jnp.zeros((2, HEADS), F32), jnp.concatenate([A_log, dt_bias], axis=0)], axis=1), DH)

    gx, g_wcat, g_wout, sm = _local_step(x[0], loss_target[0], w_cat, w_out_full, norm_in_w, cqw, ad,
                                         gdn_norm_w, cw, conv_b, final_norm_w.reshape(1, -1))

    ad_g = jnp.concatenate([sm["al"][:, HEADS:2 * HEADS], sm["dt"][:, HEADS:2 * HEADS]], axis=1)
    pack = jnp.concatenate([_row(sm["nin"]), _row(sm["cb"]), _row(sm["fn"]), _row(ad_g), _row(sm["gn"]),
                            sm["cqw"].reshape(12, 1024), sm["cw"], _row(sm["loss"])], axis=0)
    pack = jnp.pad(pack, ((0, PACK_ROWS - pack.shape[0]), (0, 0)))
    blk_in = _full_to_blocks(_uncat_layout(g_wcat))
    blk_out = g_wout.reshape(4, GW // 2, -1)
    land_in, land_out, packs = _exchange_grads([blk_in, blk_out], pack)
    part_in = _sum_blocks(land_in, 128, "sum_w_in")
    part_out = _sum_blocks(land_out, 128, "sum_w_out")
    sib_in, sib_out = _swap_with_sibling([part_in, part_out])
    tot = _sum_packs(packs)

    g_wi, d_wi, m_wi, v_wi = _adamw(w_in[0], m_w_in[0], v_w_in[0], part_in, sib_in, 64, "adamw_w_in")
    g_wo, d_wo, m_wo, v_wo = _adamw(w_out[0], m_w_out[0], v_w_out[0], part_out, sib_out, 128, "adamw_w_out")
    g_cq_sh = lax.dynamic_slice_in_dim(tot[R_CQ:R_CQ + 12].reshape(4, 3 * GW), chip * 768, 768, axis=1)
    g_cw_sh = lax.dynamic_slice_in_dim(tot[R_CW:R_CW + 3], chip * 256, 256, axis=1)
    sp = lambda nin, cb, fn, al, dt, gn, cq, cwv: _small_pack(nin, cb, fn, al, dt, gn, cq[0], cwv[0])
    g_s = _small_pack(tot[R_NIN], tot[R_CB], tot[R_FN], tot[R_AD, :HEADS], tot[R_AD, HEADS:2 * HEADS],
                      tot[R_GN, :DH], g_cq_sh, g_cw_sh)
    w_s = sp(norm_in_w, conv_b, final_norm_w, A_log, dt_bias, gdn_norm_w, conv_qkv_w, conv_w)
    m_s = sp(m_norm_in_w, m_conv_b, m_final_norm_w, m_A_log, m_dt_bias, m_gdn_norm_w, m_conv_qkv_w, m_conv_w)
    v_s = sp(v_norm_in_w, v_conv_b, v_final_norm_w, v_A_log, v_dt_bias, v_gdn_norm_w, v_conv_qkv_w, v_conv_w)
    small = _adamw(w_s, m_s, v_s, g_s, None, 16, "adamw_small")

    def unpack(a, big_in, big_out):
        return (a[0:1], big_in[None], a[5:8].reshape(1, 4, 768), a[3:4, :HEADS], a[3:4, HEADS:2 * HEADS],
                a[4:5, :DH], a[8, :768].reshape(1, 3, 256), a[1:2], big_out[None], a[2])

    loss = tot[R_LOSS, 0]
    return (loss, gx[None], *unpack(small[0], g_wi, g_wo), *unpack(small[1], d_wi, d_wo),
            *unpack(small[2], m_wi, m_wo), *unpack(small[3], v_wi, v_wo))
```

```python
import functools
import math

import jax
import jax.numpy as jnp
from jax import lax
from jax.experimental import pallas as pl
from jax.experimental.pallas import tpu as pltpu

F32 = jnp.float32
BF16 = jnp.bfloat16
MESH = pl.DeviceIdType.MESH
ANY = pl.BlockSpec(memory_space=pl.ANY)

HEADS = 8
DH = 128
CH = 64
GW = HEADS * DH
EPS = 1e-6
VMEM_V7X = 64 * 1024 * 1024

QB, KB, VB, ZB, BAB = 0, 8, 16, 24, 64
CVB = 8
A_LANE = 120
NPB = 65
PW = NPB * DH

SHARD_W = 2052
ALIGNED_BLOCKS = 17
ALIGNED_W = ALIGNED_BLOCKS * DH
SHIFTS = (0, 4, ALIGNED_W - 8, ALIGNED_W - 4)
EDGE0 = 66
WX_BLOCKS = 72
MIXED = (2, BAB, 32 + 4 * 7 + 1)


def _shard_blocks(chip, edges):
    if chip == 0:
        out = [3 * b for b in range(8)] + [3 * b + 1 for b in range(8)] + [(EDGE0, MIXED[0])]
    elif chip == 1:
        out = [(EDGE0 + 1, MIXED[0])] + [3 * b + 2 for b in range(1, 8)] + [ZB + b for b in range(8)]
        out += [(EDGE0 + 2, MIXED[1])]
    elif chip == 2:
        out = [32 + 4 * b for b in range(8)] + [33 + 4 * b for b in range(7)]
        out += [(EDGE0 + 4, MIXED[2]), (EDGE0 + 3, MIXED[1])]
    else:
        out = [34 + 4 * b for b in range(8)] + [35 + 4 * b for b in range(8)] + [(EDGE0 + 5, MIXED[2])]
    return [o if isinstance(o, int) else o[0 if edges else 1] for o in out]


def _by_chip(chip, vals):
    if all(v == vals[0] for v in vals):
        return vals[0]
    r = vals[3]
    for kk in (2, 1, 0):
        r = jnp.where(chip == kk, vals[kk], r)
    return r

ADAM_LR, ADAM_B1, ADAM_B2, ADAM_EPS, ADAM_WD, ADAM_STEP = 0.001, 0.9, 0.999, 1e-08, 0.01, 10

R_NIN, R_CB, R_FN, R_AD, R_GN, R_CQ, R_CW, R_LOSS, PACK_ROWS = 0, 1, 2, 3, 4, 5, 17, 20, 24

NN = ((1,), (0,))
NT = ((1,), (1,))
TN = ((0,), (0,))


def _dot(a, b, dims=NN, mode="lo"):
    dn = (dims, ((), ()))
    if mode == "hi":
        return lax.dot_general(a, b, dn, precision=lax.Precision.HIGHEST, preferred_element_type=F32)
    ah, bh = a.astype(BF16), b.astype(BF16)
    out = lax.dot_general(ah, bh, dn, preferred_element_type=F32)
    if mode == "x3":
        al = (a - ah.astype(F32)).astype(BF16)
        bl = (b - bh.astype(F32)).astype(BF16)
        out = out + lax.dot_general(ah, bl, dn, preferred_element_type=F32)
        out = out + lax.dot_general(al, bh, dn, preferred_element_type=F32)
    return out


P_GRAM, P_INV, P_SOL, P_SCAN, P_SCANB, P_BWD = "lo", "lo", "lo", "lo", "lo", "lo"
P_CUM = "x3"


def _params(sem=None, vmem=None):
    kw = {}
    if sem is not None:
        kw["dimension_semantics"] = sem
    if vmem is not None:
        kw["vmem_limit_bytes"] = int(min(max(vmem, 32 * 2**20), VMEM_V7X - 8 * 2**20))
    return pltpu.CompilerParams(**kw)


def _sigmoid(x):
    return 1.0 / (1.0 + jnp.exp(-x))


def _dsilu(x, s):
    return s * (1.0 + x * (1.0 - s))


def _rows(shape):
    return lax.broadcasted_iota(jnp.int32, shape, 0)


def _shift_down(x, s):
    if s == 0:
        return x
    return jnp.where(_rows(x.shape) >= s, pltpu.roll(x, s, 0), 0.0)


def _shift_up(x, s):
    if s == 0:
        return x
    n = x.shape[0]
    return jnp.where(_rows(x.shape) < n - s, pltpu.roll(x, n - s, 0), 0.0)


def _matmul(a, b, dims, out_dtype, tm, tn, tk, name, add=None, n=None, k=None):
    if dims == NN:
        (m, k), n = a.shape, (n or b.shape[1])
    elif dims == NT:
        m, k, n = a.shape[0], (k or a.shape[1]), b.shape[0]
    else:
        (k, m), n = a.shape, b.shape[1]
    tm, tn, tk = min(tm, m), min(tn, n), min(tk, k)
    assert m % tm == 0 and n % tn == 0 and k % tk == 0, (name, m, n, k, tm, tn, tk)
    nk = k // tk

    def body(*refs):
        if add is None:
            a_ref, b_ref, o_ref = refs[:3]
            add_ref = None
        else:
            a_ref, b_ref, add_ref, o_ref = refs[:4]
        part = _dot(a_ref[...], b_ref[...], dims)
        if nk == 1:
            if add_ref is not None:
                part = part + add_ref[...]
            o_ref[...] = part.astype(out_dtype)
            return
        acc = refs[-1]
        kk = pl.program_id(2)

        @pl.when(kk == 0)
        def _():
            acc[...] = part

        @pl.when(kk > 0)
        def _():
            acc[...] += part

        @pl.when(kk == nk - 1)
        def _():
            r = acc[...]
            if add_ref is not None:
                r = r + add_ref[...]
            o_ref[...] = r.astype(out_dtype)

    if dims == TN:
        a_spec = pl.BlockSpec((tk, tm), lambda i, j, kk: (kk, i))
    else:
        a_spec = pl.BlockSpec((tm, tk), lambda i, j, kk: (i, kk))
    if dims == NT:
        b_spec = pl.BlockSpec((tn, tk), lambda i, j, kk: (j, kk))
    else:
        b_spec = pl.BlockSpec((tk, tn), lambda i, j, kk: (kk, j))
    o_spec = pl.BlockSpec((tm, tn), lambda i, j, kk: (i, j))
    in_specs = [a_spec, b_spec]
    args = [a, b]
    if add is not None:
        in_specs.append(o_spec)
        args.append(add)
    osz = jnp.dtype(out_dtype).itemsize
    est = 2 * (tm * tk * a.dtype.itemsize + tk * tn * b.dtype.itemsize + tm * tn * osz)
    est += 3 * tm * tn * 4 + (2 * tm * tn * 4 if add is not None else 0)
    return pl.pallas_call(
        body, name=name, grid=(m // tm, n // tn, nk),
        in_specs=in_specs, out_specs=o_spec,
        out_shape=jax.ShapeDtypeStruct((m, n), out_dtype),
        scratch_shapes=[pltpu.VMEM((tm, tn), F32)] if nk > 1 else [],
        compiler_params=_params(("parallel", "parallel", "arbitrary"), est + 8 * 2**20),
    )(*args)


def _cast_bf16(a, rows, name):
    r, c = a.shape
    rows = min(rows, r)

    def body(a_ref, o_ref):
        o_ref[...] = a_ref[...].astype(BF16)

    return pl.pallas_call(
        body, name=name, grid=(r // rows,),
        in_specs=[pl.BlockSpec((rows, c), lambda i: (i, 0))],
        out_specs=pl.BlockSpec((rows, c), lambda i: (i, 0)),
        out_shape=jax.ShapeDtypeStruct((r, c), BF16),
        compiler_params=_params(("parallel",)),
    )(a)


def _align_shard(w):
    r, c = w.shape
    rows = min(128, r)

    def body(w_ref, o_ref, pad_ref):
        chip = 2 * lax.axis_index("x") + lax.axis_index("y")
        pad_ref[...] = jnp.zeros_like(pad_ref)
        pad_ref[:, 0:c] = w_ref[...]
        o_ref[...] = pltpu.roll(pad_ref[...], _by_chip(chip, SHIFTS), 1).astype(BF16)

    return pl.pallas_call(
        body, name="align_shard", grid=(r // rows,),
        in_specs=[pl.BlockSpec((rows, c), lambda i: (i, 0))],
        out_specs=pl.BlockSpec((rows, ALIGNED_W), lambda i: (i, 0)),
        out_shape=jax.ShapeDtypeStruct((r, ALIGNED_W), BF16),
        scratch_shapes=[pltpu.VMEM((rows, ALIGNED_W), F32)],
        compiler_params=_params(("parallel",)),
    )(w)


def _rms_in(x, w):
    n, d = x.shape
    tr = min(256, n)

    def body(x_ref, w_ref, h_ref):
        xv = x_ref[...]
        r = lax.rsqrt(jnp.mean(xv * xv, axis=-1, keepdims=True) + EPS)
        h_ref[...] = (xv * r * w_ref[...]).astype(BF16)

    return pl.pallas_call(
        body, name="rms_in", grid=(n // tr,),
        in_specs=[pl.BlockSpec((tr, d), lambda i: (i, 0)), pl.BlockSpec((1, d), lambda i: (0, 0))],
        out_specs=pl.BlockSpec((tr, d), lambda i: (i, 0)),
        out_shape=jax.ShapeDtypeStruct((n, d), BF16),
        compiler_params=_params(("parallel",)),
    )(x, w)


def _conv_silu(p, w_ref, taps):
    c = None
    for j in range(taps):
        t = _shift_down(p, taps - 1 - j) * w_ref[j:j + 1, :]
        c = t if c is None else c + t
    return c


def _prep_qkv(proj, cw):
    n = proj.shape[0]

    def body(p3, wq, wk, wv, q_ref, k_ref, v_ref):
        for kind, (w_ref, o_ref) in enumerate(((wq, q_ref), (wk, k_ref), (wv, v_ref))):
            c = _conv_silu(p3[:, kind * DH:(kind + 1) * DH], w_ref, 4)
            a = c * _sigmoid(c)
            if kind < 2:
                r = lax.rsqrt(jnp.sum(a * a, axis=-1, keepdims=True) + EPS)
                a = a * (r * (DH ** -0.5 if kind == 0 else 1.0))
            o_ref[...] = a

    col = pl.BlockSpec((n, DH), lambda h: (0, h))
    wcol = lambda base: pl.BlockSpec((4, DH), lambda h: (0, base + h))
    out = jax.ShapeDtypeStruct((n, GW), F32)
    return pl.pallas_call(
        body, name="prep_qkv", grid=(HEADS,),
        in_specs=[pl.BlockSpec((n, 3 * DH), lambda h: (0, h)), wcol(QB), wcol(KB), wcol(VB)],
        out_specs=[col] * 3, out_shape=[out] * 3,
        compiler_params=_params(("parallel",), 40 * 2**20),
    )(proj, cw, cw, cw)


def _prep_qkv_bwd(proj, cw, dq, dk, dv, dproj):
    n = proj.shape[0]

    def body(p3, wq, wk, wv, dq_ref, dk_ref, dv_ref, _, o3, gq, gk, gv):
        for kind, (w_ref, d_ref, g_ref) in enumerate(((wq, dq_ref, gq), (wk, dk_ref, gk), (wv, dv_ref, gv))):
            p = p3[:, kind * DH:(kind + 1) * DH]
            c = _conv_silu(p, w_ref, 4)
            s = _sigmoid(c)
            a = c * s
            d = d_ref[...]
            if kind < 2:
                r = lax.rsqrt(jnp.sum(a * a, axis=-1, keepdims=True) + EPS)
                sc = DH ** -0.5 if kind == 0 else 1.0
                d = (sc * r) * (d - a * ((r * r) * jnp.sum(d * a, axis=-1, keepdims=True)))
            dc = d * _dsilu(c, s)
            dp = None
            for j in range(4):
                g_ref[j:j + 1, :] = jnp.sum(dc * _shift_down(p, 3 - j), axis=0, keepdims=True)
                t = _shift_up(dc, 3 - j) * w_ref[j:j + 1, :]
                dp = t if dp is None else dp + t
            o3[:, kind * DH:(kind + 1) * DH] = dp.astype(BF16)

    col = pl.BlockSpec((n, DH), lambda h: (0, h))
    wcol = lambda base: pl.BlockSpec((4, DH), lambda h: (0, base + h))
    p3spec = pl.BlockSpec((n, 3 * DH), lambda h: (0, h))
    return pl.pallas_call(
        body, name="prep_qkv_bwd", grid=(HEADS,),
        in_specs=[p3spec, wcol(QB), wcol(KB), wcol(VB), col, col, col, ANY],
        out_specs=[p3spec] + [wcol(0)] * 3,
        out_shape=[jax.ShapeDtypeStruct(dproj.shape, BF16)] + [jax.ShapeDtypeStruct((4, GW), F32)] * 3,
        input_output_aliases={7: 0},
        compiler_params=_params(("parallel",), 48 * 2**20),
    )(proj, cw, cw, cw, dq, dk, dv, dproj)


def _tri(lower_incl):
    i = lax.broadcasted_iota(jnp.int32, (CH, CH), 0)
    j = lax.broadcasted_iota(jnp.int32, (CH, CH), 1)
    return jnp.where(i >= j, 1.0, 0.0) if lower_incl else jnp.where(j >= i, 1.0, 0.0)


def _lane(shape):
    return lax.broadcasted_iota(jnp.int32, shape, 1)


def _prep_bg(proj, ad):
    n = proj.shape[0]
    nch = n // CH

    def body(p_ref, ad_ref, bg_ref, bgt_ref):
        p = p_ref[...]
        lane = _lane(p.shape)
        beta = _sigmoid(p)
        xa = p + ad_ref[1:2, :]
        sp = jnp.maximum(xa, 0.0) + jnp.log(1.0 + jnp.exp(-jnp.abs(xa)))
        g = pltpu.roll(-jnp.exp(ad_ref[0:1, :]) * sp, DH - A_LANE + HEADS, 1)
        gc = _dot(_tri(True), g, NN, P_CUM)
        bg = jnp.where(lane < HEADS, beta, jnp.where(lane < 2 * HEADS, gc, 0.0))
        bg_ref[...] = bg
        bgt_ref[0] = bg.T

    return pl.pallas_call(
        body, name="prep_bg", grid=(nch,),
        in_specs=[pl.BlockSpec((CH, DH), lambda i: (i, BAB)), pl.BlockSpec((2, DH), lambda i: (0, 0))],
        out_specs=[pl.BlockSpec((CH, DH), lambda i: (i, 0)), pl.BlockSpec((1, DH, CH), lambda i: (i, 0, 0))],
        out_shape=[jax.ShapeDtypeStruct((n, DH), F32), jax.ShapeDtypeStruct((nch, DH, CH), F32)],
        compiler_params=_params(("parallel",)),
    )(proj, ad)


def _prep_bg_bwd(proj, ad, dbg, dproj):
    n = proj.shape[0]
    nch = n // CH

    def body(p_ref, ad_ref, d_ref, _, o_ref, ga_ref, gd_ref):
        p = p_ref[...]
        d = d_ref[...]
        lane = _lane(p.shape)
        beta = _sigmoid(p)
        xa = p + ad_ref[1:2, :]
        sp = jnp.maximum(xa, 0.0) + jnp.log(1.0 + jnp.exp(-jnp.abs(xa)))
        na = -jnp.exp(ad_ref[0:1, :])
        dg = pltpu.roll(_dot(_tri(False), d, NN, P_CUM), A_LANE - HEADS, 1)
        da = dg * na * _sigmoid(xa)
        is_g = lane >= A_LANE
        o_ref[...] = jnp.where(lane < HEADS, d * beta * (1.0 - beta), jnp.where(is_g, da, 0.0)).astype(BF16)
        ga = jnp.sum(jnp.where(is_g, dg * na * sp, 0.0), axis=0, keepdims=True)
        gd = jnp.sum(jnp.where(is_g, da, 0.0), axis=0, keepdims=True)

        @pl.when(pl.program_id(0) == 0)
        def _():
            ga_ref[...] = jnp.zeros_like(ga_ref)
            gd_ref[...] = jnp.zeros_like(gd_ref)

        ga_ref[...] += ga
        gd_ref[...] += gd

    one = pl.BlockSpec((1, DH), lambda i: (0, 0))
    return pl.pallas_call(
        body, name="prep_bg_bwd", grid=(nch,),
        in_specs=[pl.BlockSpec((CH, DH), lambda i: (i, BAB)), pl.BlockSpec((2, DH), lambda i: (0, 0)),
                  pl.BlockSpec((CH, DH), lambda i: (i, 0)), ANY],
        out_specs=[pl.BlockSpec((CH, DH), lambda i: (i, BAB)), one, one],
        out_shape=[jax.ShapeDtypeStruct(dproj.shape, BF16), jax.ShapeDtypeStruct((1, DH), F32),
                   jax.ShapeDtypeStruct((1, DH), F32)],
        input_output_aliases={3: 0},
        compiler_params=_params(("arbitrary",)),
    )(proj, ad, dbg, dproj)


def _gdn_out(o, proj, wg):
    n = o.shape[0]

    def body(o_ref, z_ref, w_ref, y_ref):
        ov, z = o_ref[...], z_ref[...]
        r = lax.rsqrt(jnp.mean(ov * ov, axis=-1, keepdims=True) + EPS)
        y_ref[...] = (ov * r * w_ref[...] * (z * _sigmoid(z))).astype(BF16)

    return pl.pallas_call(
        body, name="gdn_out", grid=(HEADS,),
        in_specs=[pl.BlockSpec((n, DH), lambda h: (0, h)), pl.BlockSpec((n, DH), lambda h: (0, ZB + h)),
                  pl.BlockSpec((1, DH), lambda h: (0, 0))],
        out_specs=pl.BlockSpec((n, DH), lambda h: (0, h)),
        out_shape=jax.ShapeDtypeStruct((n, 2 * GW), BF16),
        compiler_params=_params(("parallel",)),
    )(o, proj, wg)


def _gdn_out_bwd(o, proj, wg, dmix):
    n = o.shape[0]

    def body(o_ref, z_ref, w_ref, d_ref, do_ref, dz_ref, gw_ref):
        ov, z, d, w = o_ref[...], z_ref[...], d_ref[...], w_ref[...]
        r = lax.rsqrt(jnp.mean(ov * ov, axis=-1, keepdims=True) + EPS)
        nrm = ov * r
        s = _sigmoid(z)
        dz_ref[...] = (d * (nrm * w) * _dsilu(z, s)).astype(BF16)
        dn_w = d * (z * s)
        gw = jnp.sum(dn_w * nrm, axis=0, keepdims=True)
        dn = dn_w * w
        do_ref[...] = r * (dn - nrm * jnp.mean(dn * nrm, axis=-1, keepdims=True))

        @pl.when(pl.program_id(0) == 0)
        def _():
            gw_ref[...] = jnp.zeros_like(gw_ref)

        gw_ref[...] += gw

    return pl.pallas_call(
        body, name="gdn_out_bwd", grid=(HEADS,),
        in_specs=[pl.BlockSpec((n, DH), lambda h: (0, h)), pl.BlockSpec((n, DH), lambda h: (0, ZB + h)),
                  pl.BlockSpec((1, DH), lambda h: (0, 0)), pl.BlockSpec((n, DH), lambda h: (0, h))],
        out_specs=[pl.BlockSpec((n, DH), lambda h: (0, h)), pl.BlockSpec((n, DH), lambda h: (0, ZB + h)),
                   pl.BlockSpec((1, DH), lambda h: (0, 0))],
        out_shape=[jax.ShapeDtypeStruct((n, GW), F32), jax.ShapeDtypeStruct((n, PW), BF16),
                   jax.ShapeDtypeStruct((1, DH), F32)],
        compiler_params=_params(("arbitrary",)),
    )(o, proj, wg, dmix)


def _conv_branch(proj, w3, b, mix):
    n = proj.shape[0]

    def body(p4, w_ref, b_ref, _, y_ref):
        u = p4[:, DH:2 * DH] * p4[:, 2 * DH:3 * DH]
        cc = _conv_silu(u, w_ref, 3) + b_ref[...]
        z = p4[:, 3 * DH:4 * DH]
        y_ref[...] = (p4[:, 0:DH] * cc * (z * _sigmoid(z))).astype(BF16)

    return pl.pallas_call(
        body, name="conv_branch", grid=(HEADS,),
        in_specs=[pl.BlockSpec((n, 4 * DH), lambda h: (0, CVB + h)), pl.BlockSpec((3, DH), lambda h: (0, h)),
                  pl.BlockSpec((1, DH), lambda h: (0, h)), ANY],
        out_specs=pl.BlockSpec((n, DH), lambda h: (0, HEADS + h)),
        out_shape=jax.ShapeDtypeStruct(mix.shape, BF16),
        input_output_aliases={3: 0},
        compiler_params=_params(("parallel",), 40 * 2**20),
    )(proj, w3, b, mix)


def _conv_branch_bwd(proj, w3, b, dmix, dproj):
    n = proj.shape[0]

    def body(p4, w_ref, b_ref, d_ref, _, o4, gw_ref, gbias_ref):
        gb, gcv, hc, z = p4[:, 0:DH], p4[:, DH:2 * DH], p4[:, 2 * DH:3 * DH], p4[:, 3 * DH:4 * DH]
        d = d_ref[...]
        dgb, dgc, dhc, dzc = (o4.at[:, kk * DH:(kk + 1) * DH] for kk in range(4))
        u = gcv * hc
        cc = _conv_silu(u, w_ref, 3) + b_ref[...]
        s = _sigmoid(z)
        dzc[...] = (d * (gb * cc) * _dsilu(z, s)).astype(BF16)
        dp = d * (z * s)
        dgb[...] = (dp * cc).astype(BF16)
        dcc = dp * gb
        gbias_ref[...] = jnp.sum(dcc, axis=0, keepdims=True)
        du = None
        for j in range(3):
            gw_ref[j:j + 1, :] = jnp.sum(dcc * _shift_down(u, 2 - j), axis=0, keepdims=True)
            t = _shift_up(dcc, 2 - j) * w_ref[j:j + 1, :]
            du = t if du is None else du + t
        dgc[...] = (du * hc).astype(BF16)
        dhc[...] = (du * gcv).astype(BF16)

    p4spec = pl.BlockSpec((n, 4 * DH), lambda h: (0, CVB + h))
    return pl.pallas_call(
        body, name="conv_branch_bwd", grid=(HEADS,),
        in_specs=[p4spec, pl.BlockSpec((3, DH), lambda h: (0, h)), pl.BlockSpec((1, DH), lambda h: (0, h)),
                  pl.BlockSpec((n, DH), lambda h: (0, HEADS + h)), ANY],
        out_specs=[p4spec, pl.BlockSpec((3, DH), lambda h: (0, h)), pl.BlockSpec((1, DH), lambda h: (0, h))],
        out_shape=[jax.ShapeDtypeStruct(dproj.shape, BF16), jax.ShapeDtypeStruct((3, GW), F32),
                   jax.ShapeDtypeStruct((1, GW), F32)],
        input_output_aliases={4: 0},
        compiler_params=_params(("parallel",), 48 * 2**20),
    )(proj, w3, b, dmix, dproj)


def _final_loss(out, tgt, wf):
    n, d = out.shape
    tr = min(256, n)

    def body(o_ref, t_ref, w_ref, do_ref, dob_ref, gw_ref, loss_ref):
        ov, w = o_ref[...], w_ref[...]
        r = lax.rsqrt(jnp.mean(ov * ov, axis=-1, keepdims=True) + EPS)
        nrm = ov * r
        e = nrm * w - t_ref[...]
        dy = e * (1.0 / d)
        dn = dy * w
        dout = r * (dn - nrm * jnp.mean(dn * nrm, axis=-1, keepdims=True))
        do_ref[...] = dout
        dob_ref[...] = dout.astype(BF16)

        @pl.when(pl.program_id(0) == 0)
        def _():
            gw_ref[...] = jnp.zeros_like(gw_ref)
            loss_ref[...] = jnp.zeros_like(loss_ref)

        gw_ref[...] += jnp.sum(dy * nrm, axis=0, keepdims=True)
        loss_ref[...] += (0.5 / d) * jnp.sum(jnp.sum(e * e, axis=-1, keepdims=True), axis=0, keepdims=True)

    row = pl.BlockSpec((tr, d), lambda i: (i, 0))
    return pl.pallas_call(
        body, name="final_loss", grid=(n // tr,),
        in_specs=[row, row, pl.BlockSpec((1, d), lambda i: (0, 0))],
        out_specs=[row, row, pl.BlockSpec((1, d), lambda i: (0, 0)), pl.BlockSpec((1, 1), lambda i: (0, 0))],
        out_shape=[jax.ShapeDtypeStruct((n, d), F32), jax.ShapeDtypeStruct((n, d), BF16),
                   jax.ShapeDtypeStruct((1, d), F32), jax.ShapeDtypeStruct((1, 1), F32)],
        compiler_params=_params(("arbitrary",)),
    )(out, tgt, wf)


def _rms_in_bwd(x, w, dh, dout):
    n, d = x.shape
    tr = min(256, n)

    def body(x_ref, w_ref, dh_ref, do_ref, dx_ref, gw_ref):
        xv, dhv = x_ref[...], dh_ref[...]
        r = lax.rsqrt(jnp.mean(xv * xv, axis=-1, keepdims=True) + EPS)
        xn = xv * r
        dxn = dhv * w_ref[...]
        dx_ref[...] = r * (dxn - xn * jnp.mean(dxn * xn, axis=-1, keepdims=True)) + do_ref[...]

        @pl.when(pl.program_id(0) == 0)
        def _():
            gw_ref[...] = jnp.zeros_like(gw_ref)

        gw_ref[...] += jnp.sum(dhv * xn, axis=0, keepdims=True)

    row = pl.BlockSpec((tr, d), lambda i: (i, 0))
    one = pl.BlockSpec((1, d), lambda i: (0, 0))
    return pl.pallas_call(
        body, name="rms_in_bwd", grid=(n // tr,),
        in_specs=[row, one, row, row], out_specs=[row, one],
        out_shape=[jax.ShapeDtypeStruct((n, d), F32), jax.ShapeDtypeStruct((1, d), F32)],
        compiler_params=_params(("arbitrary",)),
    )(x, w, dh, dout)


def _ij():
    i = lax.broadcasted_iota(jnp.int32, (CH, CH), 0)
    j = lax.broadcasted_iota(jnp.int32, (CH, CH), 1)
    return i, j


def _unit_lower_inverse(mats):
    i, j = _ij()
    eye = jnp.where(i == j, 1.0, 0.0)
    same16 = (i // 16) == (j // 16)
    same32 = (i // 32) == (j // 32)
    mm = lambda xs, ys: [_dot(x, y, NN, P_INV) for x, y in zip(xs, ys)]
    n1 = [jnp.where(same16, -a, 0.0) for a in mats]
    n2 = mm(n1, n1)
    n4 = mm(n2, n2)
    n8 = mm(n4, n4)
    t = [eye + x1 + x2 + x3 for x1, x2, x3 in zip(n1, n2, mm(n1, n2))]
    t = [x + y for x, y in zip(t, mm(t, n4))]
    t = [x + y for x, y in zip(t, mm(t, n8))]
    a1 = [jnp.where(same32 & jnp.logical_not(same16), a, 0.0) for a in mats]
    t = [x - y for x, y in zip(t, mm(t, mm(a1, t)))]
    a2 = [jnp.where(same32, 0.0, a) for a in mats]
    t = [x - y for x, y in zip(t, mm(t, mm(a2, t)))]
    return t


def _head_vectors(bg, bgt, h):
    bcol = bg[:, h:h + 1]
    gcol = bg[:, HEADS + h:HEADS + h + 1]
    grow = bgt[HEADS + h:HEADS + h + 1, :]
    return bcol, gcol, grow


def _decay(gcol, grow):
    i, j = _ij()
    return jnp.where(i >= j, jnp.exp(jnp.where(i >= j, gcol - grow, 0.0)), 0.0)


def _gdn_intra(q, k, v, bg, bgt):
    n = q.shape[0]
    nch = n // CH

    def body(q_ref, k_ref, v_ref, bg_ref, bgt_ref, u_ref, w_ref, p_ref, t_ref):
        bg, bgt = bg_ref[...], bgt_ref[0]
        i, j = _ij()
        sls = [slice(h * DH, (h + 1) * DH) for h in range(HEADS)]
        ks = [k_ref[:, sl] for sl in sls]
        vecs = [_head_vectors(bg, bgt, h) for h in range(HEADS)]
        decs = [_decay(gcol, grow) for _, gcol, grow in vecs]
        kks = [_dot(kh, kh, NT, P_GRAM) for kh in ks]
        qks = [_dot(q_ref[:, sl], kh, NT, P_GRAM) for sl, kh in zip(sls, ks)]
        ts = _unit_lower_inverse([jnp.where(i > j, bcol * kk * dec, 0.0)
                                  for (bcol, _, _), kk, dec in zip(vecs, kks, decs)])
        us = [_dot(t, v_ref[:, sl] * bcol, NN, P_SOL) for t, sl, (bcol, _, _) in zip(ts, sls, vecs)]
        ws = [_dot(t, kh * (bcol * jnp.exp(gcol)), NN, P_SOL) for t, kh, (bcol, gcol, _) in zip(ts, ks, vecs)]
        for h, sl in enumerate(sls):
            p_ref[0, h] = qks[h] * decs[h]
            t_ref[0, h] = ts[h]
            u_ref[:, sl] = us[h]
            w_ref[:, sl] = ws[h]

    row = pl.BlockSpec((CH, GW), lambda c: (c, 0))
    sq = pl.BlockSpec((1, HEADS, CH, CH), lambda c: (c, 0, 0, 0))
    big = jax.ShapeDtypeStruct((n, GW), F32)
    sqs = jax.ShapeDtypeStruct((nch, HEADS, CH, CH), F32)
    return pl.pallas_call(
        body, name="gdn_intra", grid=(nch,),
        in_specs=[row, row, row, pl.BlockSpec((CH, DH), lambda c: (c, 0)),
                  pl.BlockSpec((1, DH, CH), lambda c: (c, 0, 0))],
        out_specs=[row, row, sq, sq], out_shape=[big, big, sqs, sqs],
        compiler_params=_params(("parallel",)),
    )(q, k, v, bg, bgt)


def _gdn_scan(q, k, bg, u, w, p):
    n = q.shape[0]
    nch = n // CH

    def body(q_ref, k_ref, bg_ref, u_ref, w_ref, p_ref, o_ref, vn_ref, s_out, s_scr):
        @pl.when(pl.program_id(0) == 0)
        def _():
            s_scr[...] = jnp.zeros_like(s_scr)

        bg = bg_ref[...]
        hs = range(HEADS)
        sls = [slice(h * DH, (h + 1) * DH) for h in hs]
        gcols = [bg[:, HEADS + h:HEADS + h + 1] for h in hs]
        glasts = [g[CH - 1:CH, :] for g in gcols]
        ss = [s_scr[h] for h in hs]
        wss = [_dot(w_ref[:, sl], s, NN, P_SCAN) for sl, s in zip(sls, ss)]
        oqs = [_dot(q_ref[:, sl] * jnp.exp(g), s, NN, P_SCAN) for sl, s, g in zip(sls, ss, gcols)]
        vns = [u_ref[:, sl] - x for sl, x in zip(sls, wss)]
        ops = [_dot(p_ref[0, h], vn, NN, P_SCAN) for h, vn in zip(hs, vns)]
        sns = [_dot(k_ref[:, sl] * jnp.exp(gl - g), vn, TN, P_SCAN)
               for sl, gl, g, vn in zip(sls, glasts, gcols, vns)]
        for h, sl in enumerate(sls):
            s_out[0, :, sl] = ss[h]
            vn_ref[:, sl] = vns[h]
            o_ref[:, sl] = oqs[h] + ops[h]
            s_scr[h] = ss[h] * jnp.exp(glasts[h]) + sns[h]

    row = pl.BlockSpec((CH, GW), lambda c: (c, 0))
    big = jax.ShapeDtypeStruct((n, GW), F32)
    return pl.pallas_call(
        body, name="gdn_scan", grid=(nch,),
        in_specs=[row, row, pl.BlockSpec((CH, DH), lambda c: (c, 0)), row, row,
                  pl.BlockSpec((1, HEADS, CH, CH), lambda c: (c, 0, 0, 0))],
        out_specs=[row, row, pl.BlockSpec((1, DH, GW), lambda c: (c, 0, 0))],
        out_shape=[big, big, jax.ShapeDtypeStruct((nch, DH, GW), F32)],
        scratch_shapes=[pltpu.VMEM((HEADS, DH, DH), F32)],
        compiler_params=_params(("arbitrary",)),
    )(q, k, bg, u, w, p)


def _gdn_scan_bwd(q, k, bg, w, p, vn, s_in, do):
    n = q.shape[0]
    nch = n // CH
    rev = lambda c: nch - 1 - c

    def body(q_ref, k_ref, bg_ref, w_ref, p_ref, vn_ref, s_ref, do_ref,
             dqg_ref, dp_ref, du_ref, dw_ref, dks_ref, dgam_ref, ds_scr):
        @pl.when(pl.program_id(0) == 0)
        def _():
            ds_scr[...] = jnp.zeros_like(ds_scr)

        bg = bg_ref[...]
        lane = _lane((1, DH))
        hs = range(HEADS)
        sls = [slice(h * DH, (h + 1) * DH) for h in hs]
        gcols = [bg[:, HEADS + h:HEADS + h + 1] for h in hs]
        glasts = [g[CH - 1:CH, :] for g in gcols]
        ss = [s_ref[0, :, sl] for sl in sls]
        dss = [ds_scr[h] for h in hs]
        dos = [do_ref[:, sl] for sl in sls]
        vnl = [vn_ref[:, sl] for sl in sls]
        dqgs = [_dot(d, s, NT, P_SCANB) for d, s in zip(dos, ss)]
        dps = [_dot(d, vn, NT, P_SCANB) for d, vn in zip(dos, vnl)]
        dvn1 = [_dot(p_ref[0, h], d, TN, P_SCANB) for h, d in zip(hs, dos)]
        dvn2 = [_dot(k_ref[:, sl] * jnp.exp(gl - g), ds, NN, P_SCANB)
                for sl, gl, g, ds in zip(sls, glasts, gcols, dss)]
        dkss = [_dot(vn, ds, NT, P_SCANB) for vn, ds in zip(vnl, dss)]
        dsq = [_dot(q_ref[:, sl] * jnp.exp(g), d, TN, P_SCANB) for sl, g, d in zip(sls, gcols, dos)]
        dvns = [a + b for a, b in zip(dvn1, dvn2)]
        dws = [_dot(dvn, s, NT, P_SCANB) for dvn, s in zip(dvns, ss)]
        dsw = [_dot(w_ref[:, sl], dvn, TN, P_SCANB) for sl, dvn in zip(sls, dvns)]
        dgam = jnp.zeros((1, DH), F32)
        for h, sl in enumerate(sls):
            dqg_ref[:, sl] = dqgs[h]
            dp_ref[0, h] = dps[h]
            du_ref[:, sl] = dvns[h]
            dw_ref[:, sl] = -dws[h]
            dks_ref[:, sl] = dkss[h]
            tot = jnp.sum(jnp.sum(dss[h] * ss[h], axis=-1, keepdims=True), axis=0, keepdims=True)
            dgam = dgam + jnp.where(lane == h, tot, 0.0)
            ds_scr[h] = dss[h] * jnp.exp(glasts[h]) + dsq[h] - dsw[h]
        dgam_ref[0] = jnp.broadcast_to(dgam, (8, DH))

    row = pl.BlockSpec((CH, GW), lambda c: (rev(c), 0))
    sq = pl.BlockSpec((1, HEADS, CH, CH), lambda c: (rev(c), 0, 0, 0))
    big = jax.ShapeDtypeStruct((n, GW), F32)
    return pl.pallas_call(
        body, name="gdn_scan_bwd", grid=(nch,),
        in_specs=[row, row, pl.BlockSpec((CH, DH), lambda c: (rev(c), 0)), row, sq, row,
                  pl.BlockSpec((1, DH, GW), lambda c: (rev(c), 0, 0)), row],
        out_specs=[row, sq, row, row, row, pl.BlockSpec((1, 8, DH), lambda c: (rev(c), 0, 0))],
        out_shape=[big, jax.ShapeDtypeStruct((nch, HEADS, CH, CH), F32), big, big, big,
                   jax.ShapeDtypeStruct((nch, 8, DH), F32)],
        scratch_shapes=[pltpu.VMEM((HEADS, DH, DH), F32)],
        compiler_params=_params(("arbitrary",)),
    )(q, k, bg, w, p, vn, s_in, do)


def _gdn_intra_bwd(q, k, v, bg, bgt, t, u, w, p, dqg, dp, du, dw, dks, dgam):
    n = q.shape[0]
    nch = n // CH

    def body(q_ref, k_ref, v_ref, bg_ref, bgt_ref, t_ref, u_ref, w_ref, p_ref,
             dqg_ref, dp_ref, du_ref, dw_ref, dks_ref, dgam_ref, dq_ref, dk_ref, dv_ref, dbg_ref):
        bg, bgt = bg_ref[...], bgt_ref[0]
        dgam_all = dgam_ref[0]
        i, j = _ij()
        rows1 = lax.broadcasted_iota(jnp.int32, (CH, 1), 0)
        lane = _lane((CH, DH))
        dbg = jnp.zeros((CH, DH), F32)
        rsum = lambda x: jnp.sum(x, axis=-1, keepdims=True)
        hs = range(HEADS)
        sls = [slice(h * DH, (h + 1) * DH) for h in hs]
        qs = [q_ref[:, sl] for sl in sls]
        ks = [k_ref[:, sl] for sl in sls]
        vecs = [_head_vectors(bg, bgt, h) for h in hs]
        decs = [_decay(gcol, grow) for _, gcol, grow in vecs]
        ths = [t_ref[0, h] for h in hs]
        drus = [_dot(th, du_ref[:, sl], TN, P_BWD) for th, sl in zip(ths, sls)]
        drws = [_dot(th, dw_ref[:, sl], TN, P_BWD) for th, sl in zip(ths, sls)]
        kks = [_dot(kh, kh, NT, P_GRAM) for kh in ks]
        da1 = [_dot(dru, u_ref[:, sl], NT, P_BWD) for dru, sl in zip(drus, sls)]
        da2 = [_dot(drw, w_ref[:, sl], NT, P_BWD) for drw, sl in zip(drws, sls)]
        das = [jnp.where(i > j, -(x + y), 0.0) for x, y in zip(da1, da2)]
        dkks = [da * bcol * dec for da, (bcol, _, _), dec in zip(das, vecs, decs)]
        dqks = [dp_ref[0, h] * dec for h, dec in zip(hs, decs)]
        dq_ps = [_dot(dqk, kh, NN, P_BWD) for dqk, kh in zip(dqks, ks)]
        dk_ps = [_dot(dqk, qh, TN, P_BWD) for dqk, qh in zip(dqks, qs)]
        dk_as = [_dot(dkk, kh, NN, P_BWD) for dkk, kh in zip(dkks, ks)]
        dk_bs = [_dot(dkk, kh, TN, P_BWD) for dkk, kh in zip(dkks, ks)]
        for h, sl in enumerate(sls):
            qh, kh, vh = qs[h], ks[h], v_ref[:, sl]
            bcol, gcol, _ = vecs[h]
            dec, dru, drw, da, kk = decs[h], drus[h], drws[h], das[h], kks[h]
            gam = jnp.exp(gcol)
            glast = gcol[CH - 1:CH, :]
            e = jnp.exp(glast - gcol)
            kg = kh * gam
            dv_ref[:, sl] = bcol * dru
            dbeta = rsum(dru * vh) + rsum(drw * kg) + rsum(da * kk * dec)
            dgc = rsum(drw * kg) * bcol
            dqg = dqg_ref[:, sl]
            dksh = dks_ref[:, sl]
            dq_ref[:, sl] = gam * dqg + dq_ps[h]
            dk_ref[:, sl] = (bcol * gam) * drw + dk_ps[h] + dk_as[h] + dk_bs[h] + dksh * e
            tk = rsum(dksh * kh) * e
            mdec = da * (bcol * kk * dec) + dp_ref[0, h] * p_ref[0, h]
            col = rsum(jnp.where(i == j, jnp.sum(mdec, axis=0, keepdims=True), 0.0))
            dgc = dgc + rsum(mdec) - col
            dgc = dgc + rsum(dqg * qh) * gam - tk
            dglast = jnp.sum(tk, axis=0, keepdims=True) + dgam_all[0:1, h:h + 1] * jnp.exp(glast)
            dgc = dgc + jnp.where(rows1 == CH - 1, dglast, 0.0)
            dbg = dbg + jnp.where(lane == h, dbeta, 0.0) + jnp.where(lane == HEADS + h, dgc, 0.0)
        dbg_ref[...] = dbg

    row = pl.BlockSpec((CH, GW), lambda c: (c, 0))
    sq = pl.BlockSpec((1, HEADS, CH, CH), lambda c: (c, 0, 0, 0))
    small = pl.BlockSpec((CH, DH), lambda c: (c, 0))
    big = jax.ShapeDtypeStruct((n, GW), F32)
    return pl.pallas_call(
        body, name="gdn_intra_bwd", grid=(nch,),
        in_specs=[row, row, row, small, pl.BlockSpec((1, DH, CH), lambda c: (c, 0, 0)), sq, row, row, sq,
                  row, sq, row, row, row, pl.BlockSpec((1, 8, DH), lambda c: (c, 0, 0))],
        out_specs=[row, row, row, small],
        out_shape=[big, big, big, jax.ShapeDtypeStruct((n, DH), F32)],
        compiler_params=_params(("parallel",)),
    )(q, k, v, bg, bgt, t, u, w, p, dqg, dp, du, dw, dks, dgam)


def _local_step(x, tgt, w_cat, w_out, norm_in_w, cqw, ad, gdn_norm_w, conv_w, conv_b, final_norm_w):
    h = _rms_in(x, norm_in_w)
    proj = _matmul(h, w_cat, NN, F32, 512, 640, 1024, "mm_proj", n=PW)
    q, k, v = _prep_qkv(proj, cqw)
    bg, bgt = _prep_bg(proj, ad)
    u, w, p, t = _gdn_intra(q, k, v, bg, bgt)
    o, vn, s_in = _gdn_scan(q, k, bg, u, w, p)
    mix = _conv_branch(proj, conv_w, conv_b, _gdn_out(o, proj, gdn_norm_w))
    out = _matmul(mix, w_out, NN, F32, 512, 512, 2048, "mm_out", add=x)
    dout, dout_b, g_fn, loss = _final_loss(out, tgt, final_norm_w)

    dmix = _matmul(dout_b, w_out, NT, F32, 512, 1024, 1024, "mm_dmix")
    g_wout = _matmul(mix, dout_b, TN, BF16, 512, 512, 2048, "mm_gwout")
    do, dproj, g_gn = _gdn_out_bwd(o, proj, gdn_norm_w, dmix)
    dproj, g_cw, g_cb = _conv_branch_bwd(proj, conv_w, conv_b, dmix, dproj)
    dqg, dp, du, dw, dks, dgam = _gdn_scan_bwd(q, k, bg, w, p, vn, s_in, do)
    dq, dk, dv, dbg = _gdn_intra_bwd(q, k, v, bg, bgt, t, u, w, p, dqg, dp, du, dw, dks, dgam)
    dproj, gq, gk, gv = _prep_qkv_bwd(proj, cqw, dq, dk, dv, dproj)
    dproj, g_al, g_dt = _prep_bg_bwd(proj, ad, dbg, dproj)
    g_wcat = _matmul(h, dproj, TN, BF16, 512, 640, 2048, "mm_gwin")
    dh = _matmul(dproj, w_cat, NT, F32, 512, 1024, 640, "mm_dh", k=PW)
    gx, g_nin = _rms_in_bwd(x, norm_in_w, dh, dout)
    return gx, g_wcat, g_wout, dict(nin=g_nin, cb=g_cb, fn=g_fn, al=g_al, dt=g_dt, gn=g_gn, cq=(gq, gk, gv),
                                    cw=g_cw, loss=loss)


def _place():
    x, y, c = lax.axis_index("x"), lax.axis_index("y"), lax.axis_index("c")
    chips = [(1 - x, y), (x, 1 - y), (1 - x, 1 - y)]
    return x, y, c, chips


def _blk(ref, b):
    if isinstance(b, int):
        return ref.at[:, b * DH:(b + 1) * DH]
    return ref.at[:, pl.ds(pl.multiple_of(b * DH, DH), DH)]


def _gather_weights(a_shard, wo, cq, cw):
    d = a_shard.shape[0]
    nb = ALIGNED_BLOCKS

    def body(a_ref, wo_ref, cq_ref, cw_ref, wx_ref, wog_ref, cqg_ref, cwg_ref, send_sems, recv_sems, loc_sems):
        x, y, c, chips = _place()
        mine = 2 * x + y
        dst = [_by_chip(mine, [_shard_blocks(s, True)[b] for s in range(4)]) for b in range(nb)]
        small = ((wo_ref, wog_ref), (cq_ref, cqg_ref), (cw_ref, cwg_ref))
        whole = wx_ref.at[:, 0:ALIGNED_W]
        local = [pltpu.make_async_copy(_blk(a_ref, b), _blk(wx_ref, dst[b]), loc_sems.at[0]) for b in range(nb)]
        local_small = [pltpu.make_async_copy(s_ref, g_ref.at[mine], loc_sems.at[1 + a])
                       for a, (s_ref, g_ref) in enumerate(small)]
        for cp in local + local_small:
            cp.start()
        blocks, singles, alls = [], [], []
        for jj, (px, py) in enumerate(chips):
            to = dict(device_id=(px, py, c), device_id_type=MESH)
            blocks += [pltpu.make_async_remote_copy(
                src_ref=_blk(a_ref, b), dst_ref=_blk(wx_ref, dst[b]),
                send_sem=send_sems.at[4 * jj], recv_sem=recv_sems.at[4 * jj], **to) for b in range(nb)]
            singles += [pltpu.make_async_remote_copy(
                src_ref=s_ref, dst_ref=g_ref.at[mine],
                send_sem=send_sems.at[4 * jj + 1 + a], recv_sem=recv_sems.at[4 * jj + 1 + a], **to)
                for a, (s_ref, g_ref) in enumerate(small)]
            alls.append(pltpu.make_async_remote_copy(
                src_ref=a_ref, dst_ref=whole, send_sem=send_sems.at[4 * jj], recv_sem=recv_sems.at[4 * jj], **to))
        for cp in blocks + singles:
            cp.start()
        for cp in alls + singles:
            cp.wait_recv()
        for cp in alls + singles:
            cp.wait_send()
        pltpu.make_async_copy(a_ref, whole, loc_sems.at[0]).wait()
        for cp in local_small:
            cp.wait()

    return pl.pallas_call(
        body, name="gather_weights",
        in_specs=[ANY] * 4, out_specs=[ANY] * 4,
        out_shape=[jax.ShapeDtypeStruct((d, WX_BLOCKS * DH), a_shard.dtype)]
        + [jax.ShapeDtypeStruct((4,) + s.shape, s.dtype) for s in (wo, cq, cw)],
        scratch_shapes=[pltpu.SemaphoreType.DMA((12,)), pltpu.SemaphoreType.DMA((12,)),
                        pltpu.SemaphoreType.DMA((4,))],
    )(a_shard, wo, cq, cw)


def _merge_edges(wx):
    d = wx.shape[0]

    def body(e_ref, o_ref):
        o_ref[...] = e_ref[:, 0:DH] + e_ref[:, DH:2 * DH]

    mixed = lambda i: jnp.where(i == 0, MIXED[0], jnp.where(i == 1, MIXED[1], MIXED[2]))
    return pl.pallas_call(
        body, name="merge_edges", grid=(3,),
        in_specs=[pl.BlockSpec((d, 2 * DH), lambda i: (0, EDGE0 // 2 + i))],
        out_specs=pl.BlockSpec((d, DH), lambda i: (0, mixed(i))),
        out_shape=jax.ShapeDtypeStruct(wx.shape, wx.dtype),
        input_output_aliases={0: 0},
        compiler_params=_params(("arbitrary",)),
    )(wx)


def _exchange_grads(g_cat, g_out, pack):
    d = g_cat.shape[0]
    nb = ALIGNED_BLOCKS

    def body(g_ref, go_ref, pack_ref, land_ref, lando_ref, packs, send_sems, recv_sems, psend, precv, loc_sems):
        x, y, c, chips = _place()
        me = 4 * x + 2 * y + c
        mine = 2 * x + y
        src = lambda chip, b: _by_chip(chip, [_shard_blocks(s, False)[b] for s in range(4)])
        local = [pltpu.make_async_copy(_blk(g_ref, src(mine, b)), _blk(land_ref.at[3], b), loc_sems.at[0])
                 for b in range(nb)]
        local_small = [pltpu.make_async_copy(go_ref.at[mine], lando_ref.at[3], loc_sems.at[1]),
                       pltpu.make_async_copy(pack_ref, packs.at[me], loc_sems.at[2])]
        for cp in local + local_small:
            cp.start()
        blocks, singles, alls = [], [], []
        for jj, (px, py) in enumerate(chips):
            to = dict(device_id=(px, py, c), device_id_type=MESH)
            peer = 2 * px + py
            blocks += [pltpu.make_async_remote_copy(
                src_ref=_blk(g_ref, src(peer, b)), dst_ref=_blk(land_ref.at[jj], b),
                send_sem=send_sems.at[2 * jj], recv_sem=recv_sems.at[2 * jj], **to) for b in range(nb)]
            singles.append(pltpu.make_async_remote_copy(
                src_ref=go_ref.at[peer], dst_ref=lando_ref.at[jj],
                send_sem=send_sems.at[2 * jj + 1], recv_sem=recv_sems.at[2 * jj + 1], **to))
            alls.append(pltpu.make_async_remote_copy(
                src_ref=land_ref.at[jj], dst_ref=land_ref.at[jj],
                send_sem=send_sems.at[2 * jj], recv_sem=recv_sems.at[2 * jj], **to))
        for r in range(1, 8):
            dx, dy, dc = (r >> 2) & 1, (r >> 1) & 1, r & 1
            peer = (x + dx - 2 * x * dx, y + dy - 2 * y * dy, c + dc - 2 * c * dc)
            singles.append(pltpu.make_async_remote_copy(
                src_ref=pack_ref, dst_ref=packs.at[me], send_sem=psend.at[r - 1], recv_sem=precv.at[r - 1],
                device_id=peer, device_id_type=MESH))
        for cp in blocks + singles:
            cp.start()
        for cp in alls + singles:
            cp.wait_recv()
        for cp in alls + singles:
            cp.wait_send()
        pltpu.make_async_copy(land_ref.at[3], land_ref.at[3], loc_sems.at[0]).wait()
        for cp in local_small:
            cp.wait()

    return pl.pallas_call(
        body, name="exchange_grads",
        in_specs=[ANY] * 3, out_specs=[ANY] * 3,
        out_shape=[jax.ShapeDtypeStruct((4, d, ALIGNED_W), g_cat.dtype),
                   jax.ShapeDtypeStruct(g_out.shape, g_out.dtype),
                   jax.ShapeDtypeStruct((8,) + pack.shape, pack.dtype)],
        scratch_shapes=[pltpu.SemaphoreType.DMA((6,)), pltpu.SemaphoreType.DMA((6,)),
                        pltpu.SemaphoreType.DMA((7,)), pltpu.SemaphoreType.DMA((7,)), pltpu.SemaphoreType.DMA((3,))],
    )(g_cat, g_out, pack)


def _swap_with_sibling(parts):
    npart = len(parts)

    def body(*refs):
        ins, outs = refs[:npart], refs[npart:2 * npart]
        send_sems, recv_sems = refs[2 * npart:]
        x, y, c, _ = _place()
        cps = [pltpu.make_async_remote_copy(
            src_ref=ins[a], dst_ref=outs[a], send_sem=send_sems.at[a], recv_sem=recv_sems.at[a],
            device_id=(x, y, 1 - c), device_id_type=MESH) for a in range(npart)]
        for cp in cps:
            cp.start()
        for cp in cps:
            cp.wait_recv()
        for cp in cps:
            cp.wait_send()

    return pl.pallas_call(
        body, name="swap_with_sibling",
        in_specs=[ANY] * npart, out_specs=[ANY] * npart,
        out_shape=[jax.ShapeDtypeStruct(p.shape, p.dtype) for p in parts],
        scratch_shapes=[pltpu.SemaphoreType.DMA((npart,)), pltpu.SemaphoreType.DMA((npart,))],
    )(*parts)


def _sum_blocks(land, rows, name):
    _, r, cdim = land.shape
    rows = min(rows, r)

    def body(land_ref, o_ref):
        acc = land_ref[3].astype(F32)
        for jj in range(3):
            acc = acc + land_ref[jj].astype(F32)
        o_ref[...] = acc

    return pl.pallas_call(
        body, name=name, grid=(r // rows,),
        in_specs=[pl.BlockSpec((4, rows, cdim), lambda i: (0, i, 0))],
        out_specs=pl.BlockSpec((rows, cdim), lambda i: (i, 0)),
        out_shape=jax.ShapeDtypeStruct((r, cdim), F32),
        compiler_params=_params(("parallel",), 40 * 2**20),
    )(land)


def _sum_packs(packs):
    def body(p_ref, o_ref):
        acc = p_ref[0]
        for d in range(1, 8):
            acc = acc + p_ref[d]
        o_ref[...] = acc

    return pl.pallas_call(
        body, name="sum_packs", out_shape=jax.ShapeDtypeStruct(packs.shape[1:], F32),
    )(packs)


def _adamw_update(g, w_ref, m_ref, v_ref, go, do, mo, vo):
    c1 = 1.0 / (1.0 - ADAM_B1 ** ADAM_STEP)
    c2 = 1.0 / (1.0 - ADAM_B2 ** ADAM_STEP)
    mn = ADAM_B1 * m_ref[...] + (1.0 - ADAM_B1) * g
    vn = ADAM_B2 * v_ref[...] + (1.0 - ADAM_B2) * (g * g)
    go[...] = g
    mo[...] = mn
    vo[...] = vn
    do[...] = -ADAM_LR * ((mn * c1) / (jnp.sqrt(vn * c2) + ADAM_EPS) + ADAM_WD * w_ref[...])


def _adamw(w, m, v, g1, g2, rows, name, aligned=False):
    r, cdim = w.shape
    gdim = g1.shape[1]
    rows = min(rows, r)

    def body(*refs):
        n_in = 4 if g2 is None else 5
        w_ref, m_ref, v_ref, g_ref = refs[:4]
        g = g_ref[...] if g2 is None else g_ref[...] + refs[4][...]
        if aligned:
            chip = 2 * lax.axis_index("x") + lax.axis_index("y")
            back = [(ALIGNED_W - s) % ALIGNED_W for s in SHIFTS]
            pad_ref = refs[-1]
            pad_ref[...] = pltpu.roll(g, _by_chip(chip, back), 1)
            g = pad_ref[:, 0:cdim]
        _adamw_update(g, w_ref, m_ref, v_ref, *refs[n_in:n_in + 4])

    blk = pl.BlockSpec((rows, cdim), lambda i: (i, 0))
    gblk = pl.BlockSpec((rows, gdim), lambda i: (i, 0))
    args = [w, m, v, g1] + ([] if g2 is None else [g2])
    shp = jax.ShapeDtypeStruct((r, cdim), F32)
    return pl.pallas_call(
        body, name=name, grid=(r // rows,),
        in_specs=[blk] * 3 + [gblk] * (len(args) - 3), out_specs=[blk] * 4, out_shape=[shp] * 4,
        scratch_shapes=[pltpu.VMEM((rows, gdim), F32)] if aligned else [],
        compiler_params=_params(("parallel",), 20 * rows * gdim * 4 + 8 * 2**20),
    )(*args)


def _pad_lanes(a, width):
    return jnp.pad(a, ((0, 0), (0, width - a.shape[1])))


def _gathered_to_full(g):
    return jnp.transpose(g, (1, 0, 2)).reshape(g.shape[1], 4 * g.shape[2])


def _row(a):
    return _pad_lanes(a.reshape(1, -1), 1024)


def _small_pack(nin, cb, fn, al, dt, gn, cqw_shard, cw_shard):
    ad = jnp.concatenate([al.reshape(1, -1), dt.reshape(1, -1)], axis=1)
    rows = [_row(nin), _row(cb), _row(fn), _row(ad), _row(gn), cqw_shard.reshape(3, 1024), _row(cw_shard)]
    out = jnp.concatenate(rows, axis=0)
    return jnp.pad(out, ((0, 16 - out.shape[0]), (0, 0)))


def kernel(x, norm_in_w, w_in, conv_qkv_w, A_log, dt_bias, gdn_norm_w, conv_w, conv_b, w_out, final_norm_w, loss_target, m_norm_in_w, m_w_in, m_conv_qkv_w, m_A_log, m_dt_bias, m_gdn_norm_w, m_conv_w, m_conv_b, m_w_out, m_final_norm_w, v_norm_in_w, v_w_in, v_conv_qkv_w, v_A_log, v_dt_bias, v_gdn_norm_w, v_conv_w, v_conv_b, v_w_out, v_final_norm_w):
    chip = 2 * lax.axis_index("x") + lax.axis_index("y")
    wo_b = _cast_bf16(w_out[0], 256, "cast_w_out")
    wx, wo_g, cq_g, cw_g = _gather_weights(_align_shard(w_in[0]), wo_b, conv_qkv_w[0], conv_w[0])
    w_cat = _merge_edges(wx)
    w_out_full = wo_g.reshape(2 * GW, -1)
    cqw = _gathered_to_full(cq_g)
    cw = _gathered_to_full(cw_g)
    ad = jnp.pad(jnp.concatenate([A_log, dt_bias], axis=0), ((0, 0), (A_LANE, 0)))

    gx, g_wcat, g_wout, sm = _local_step(x[0], loss_target[0], w_cat, w_out_full, norm_in_w, cqw, ad,
                                         gdn_norm_w, cw, conv_b, final_norm_w.reshape(1, -1))

    ad_g = jnp.concatenate([sm["al"][:, A_LANE:], sm["dt"][:, A_LANE:]], axis=1)
    pack = jnp.concatenate([_row(sm["nin"]), _row(sm["cb"]), _row(sm["fn"]), _row(ad_g), _row(sm["gn"]),
                            jnp.concatenate(sm["cq"], axis=1).reshape(12, 1024), sm["cw"], _row(sm["loss"])], axis=0)
    pack = jnp.pad(pack, ((0, PACK_ROWS - pack.shape[0]), (0, 0)))
    land_in, land_out, packs = _exchange_grads(g_wcat, g_wout.reshape(4, GW // 2, -1), pack)
    part_in = _sum_blocks(land_in, 128, "sum_w_in")
    part_out = _sum_blocks(land_out, 128, "sum_w_out")
    sib_in, sib_out = _swap_with_sibling([part_in, part_out])
    tot = _sum_packs(packs)

    g_wi, d_wi, m_wi, v_wi = _adamw(w_in[0], m_w_in[0], v_w_in[0], part_in, sib_in, 64, "adamw_w_in", aligned=True)
    g_wo, d_wo, m_wo, v_wo = _adamw(w_out[0], m_w_out[0], v_w_out[0], part_out, sib_out, 128, "adamw_w_out")
    g_cq_sh = lax.dynamic_slice_in_dim(tot[R_CQ:R_CQ + 12].reshape(4, 3 * GW), chip * 768, 768, axis=1)
    g_cw_sh = lax.dynamic_slice_in_dim(tot[R_CW:R_CW + 3], chip * 256, 256, axis=1)
    sp = lambda nin, cb, fn, al, dt, gn, cq, cwv: _small_pack(nin, cb, fn, al, dt, gn, cq[0], cwv[0])
    g_s = _small_pack(tot[R_NIN], tot[R_CB], tot[R_FN], tot[R_AD, :HEADS], tot[R_AD, HEADS:2 * HEADS],
                      tot[R_GN, :DH], g_cq_sh, g_cw_sh)
    w_s = sp(norm_in_w, conv_b, final_norm_w, A_log, dt_bias, gdn_norm_w, conv_qkv_w, conv_w)
    m_s = sp(m_norm_in_w, m_conv_b, m_final_norm_w, m_A_log, m_dt_bias, m_gdn_norm_w, m_conv_qkv_w, m_conv_w)
    v_s = sp(v_norm_in_w, v_conv_b, v_final_norm_w, v_A_log, v_dt_bias, v_gdn_norm_w, v_conv_qkv_w, v_conv_w)
    small = _adamw(w_s, m_s, v_s, g_s, None, 16, "adamw_small")

    def unpack(a, big_in, big_out):
        return (a[0:1], big_in[None], a[5:8].reshape(1, 4, 768), a[3:4, :HEADS], a[3:4, HEADS:2 * HEADS],
                a[4:5, :DH], a[8, :768].reshape(1, 3, 256), a[1:2], big_out[None], a[2])

    loss = tot[R_LOSS, 0]
    return (loss, gx[None], *unpack(small[0], g_wi, g_wo), *unpack(small[1], d_wi, d_wo),
            *unpack(small[2], m_wi, m_wo), *unpack(small[3], v_wi, v_wo))
```

```python
import functools
import math

import jax
import jax.numpy as jnp
from jax import lax
from jax.experimental import pallas as pl
from jax.experimental.pallas import tpu as pltpu

F32 = jnp.float32
BF16 = jnp.bfloat16
MESH = pl.DeviceIdType.MESH
ANY = pl.BlockSpec(memory_space=pl.ANY)

HEADS = 8
DH = 128
CH = 64
GW = HEADS * DH
EPS = 1e-6
VMEM_V7X = 64 * 1024 * 1024

QB, KB, VB, ZB, BAB = 0, 8, 16, 24, 32
A_LANE = 120
NG, NC = 33, 32
GW_COLS, CW_COLS = NG * DH, NC * DH

SHARD_W = 2052
ALIGNED_BLOCKS = 17
ALIGNED_W = ALIGNED_BLOCKS * DH
SHIFTS = (0, 4, ALIGNED_W - 8, ALIGNED_W - 4)
G_EDGE, C_EDGE = 34, 32
WG_BLOCKS, WC_BLOCKS = 38, 34
G_MIXED, C_MIXED = (2, BAB), (4 * 7 + 1,)


def _shard_blocks(chip, edges):
    g, c = "g", "c"
    if chip == 0:
        out = [(g, 3 * b) for b in range(8)] + [(g, 3 * b + 1) for b in range(8)] + [(g, G_EDGE, G_MIXED[0])]
    elif chip == 1:
        out = [(g, G_EDGE + 1, G_MIXED[0])] + [(g, 3 * b + 2) for b in range(1, 8)]
        out += [(g, ZB + b) for b in range(8)] + [(g, G_EDGE + 2, G_MIXED[1])]
    elif chip == 2:
        out = [(c, 4 * b) for b in range(8)] + [(c, 4 * b + 1) for b in range(7)]
        out += [(c, C_EDGE, C_MIXED[0]), (g, G_EDGE + 3, G_MIXED[1])]
    else:
        out = [(c, 4 * b + 2) for b in range(8)] + [(c, 4 * b + 3) for b in range(8)] + [(c, C_EDGE + 1, C_MIXED[0])]
    return [(o[0], o[1] if (edges or len(o) == 2) else o[2]) for o in out]


def _by_chip(chip, vals):
    if all(v == vals[0] for v in vals):
        return vals[0]
    r = vals[3]
    for kk in (2, 1, 0):
        r = jnp.where(chip == kk, vals[kk], r)
    return r

ADAM_LR, ADAM_B1, ADAM_B2, ADAM_EPS, ADAM_WD, ADAM_STEP = 0.001, 0.9, 0.999, 1e-08, 0.01, 10

R_NIN, R_CB, R_FN, R_AD, R_GN, R_CQ, R_CW, R_LOSS, PACK_ROWS = 0, 1, 2, 3, 4, 5, 17, 20, 24

NN = ((1,), (0,))
NT = ((1,), (1,))
TN = ((0,), (0,))


def _dot(a, b, dims=NN, mode="lo"):
    dn = (dims, ((), ()))
    if mode == "hi":
        return lax.dot_general(a, b, dn, precision=lax.Precision.HIGHEST, preferred_element_type=F32)
    ah, bh = a.astype(BF16), b.astype(BF16)
    out = lax.dot_general(ah, bh, dn, preferred_element_type=F32)
    if mode == "x3":
        al = (a - ah.astype(F32)).astype(BF16)
        bl = (b - bh.astype(F32)).astype(BF16)
        out = out + lax.dot_general(ah, bl, dn, preferred_element_type=F32)
        out = out + lax.dot_general(al, bh, dn, preferred_element_type=F32)
    return out


P_GRAM, P_INV, P_SOL, P_SCAN, P_SCANB, P_BWD = "lo", "lo", "lo", "lo", "lo", "lo"
P_CUM = "x3"


def _params(sem=None, vmem=None):
    kw = {}
    if sem is not None:
        kw["dimension_semantics"] = sem
    if vmem is not None:
        kw["vmem_limit_bytes"] = int(min(max(vmem, 32 * 2**20), VMEM_V7X - 8 * 2**20))
    return pltpu.CompilerParams(**kw)


def _sigmoid(x):
    return 1.0 / (1.0 + jnp.exp(-x))


def _dsilu(x, s):
    return s * (1.0 + x * (1.0 - s))


def _rows(shape):
    return lax.broadcasted_iota(jnp.int32, shape, 0)


def _shift_down(x, s):
    if s == 0:
        return x
    return jnp.where(_rows(x.shape) >= s, pltpu.roll(x, s, 0), 0.0)


def _shift_up(x, s):
    if s == 0:
        return x
    n = x.shape[0]
    return jnp.where(_rows(x.shape) < n - s, pltpu.roll(x, n - s, 0), 0.0)


def _matmul(a, b, dims, out_dtype, tm, tn, tk, name, add=None, n=None, k=None):
    if dims == NN:
        (m, k), n = a.shape, (n or b.shape[1])
    elif dims == NT:
        m, k, n = a.shape[0], (k or a.shape[1]), b.shape[0]
    else:
        (k, m), n = a.shape, b.shape[1]
    tm, tn, tk = min(tm, m), min(tn, n), min(tk, k)
    assert m % tm == 0 and n % tn == 0 and k % tk == 0, (name, m, n, k, tm, tn, tk)
    nk = k // tk

    def body(*refs):
        if add is None:
            a_ref, b_ref, o_ref = refs[:3]
            add_ref = None
        else:
            a_ref, b_ref, add_ref, o_ref = refs[:4]
        part = _dot(a_ref[...], b_ref[...], dims)
        if nk == 1:
            if add_ref is not None:
                part = part + add_ref[...]
            o_ref[...] = part.astype(out_dtype)
            return
        acc = refs[-1]
        kk = pl.program_id(2)

        @pl.when(kk == 0)
        def _():
            acc[...] = part

        @pl.when(kk > 0)
        def _():
            acc[...] += part

        @pl.when(kk == nk - 1)
        def _():
            r = acc[...]
            if add_ref is not None:
                r = r + add_ref[...]
            o_ref[...] = r.astype(out_dtype)

    if dims == TN:
        a_spec = pl.BlockSpec((tk, tm), lambda i, j, kk: (kk, i))
    else:
        a_spec = pl.BlockSpec((tm, tk), lambda i, j, kk: (i, kk))
    if dims == NT:
        b_spec = pl.BlockSpec((tn, tk), lambda i, j, kk: (j, kk))
    else:
        b_spec = pl.BlockSpec((tk, tn), lambda i, j, kk: (kk, j))
    o_spec = pl.BlockSpec((tm, tn), lambda i, j, kk: (i, j))
    in_specs = [a_spec, b_spec]
    args = [a, b]
    if add is not None:
        in_specs.append(o_spec)
        args.append(add)
    osz = jnp.dtype(out_dtype).itemsize
    est = 2 * (tm * tk * a.dtype.itemsize + tk * tn * b.dtype.itemsize + tm * tn * osz)
    est += 3 * tm * tn * 4 + (2 * tm * tn * 4 if add is not None else 0)
    return pl.pallas_call(
        body, name=name, grid=(m // tm, n // tn, nk),
        in_specs=in_specs, out_specs=o_spec,
        out_shape=jax.ShapeDtypeStruct((m, n), out_dtype),
        scratch_shapes=[pltpu.VMEM((tm, tn), F32)] if nk > 1 else [],
        compiler_params=_params(("parallel", "parallel", "arbitrary"), est + 8 * 2**20),
    )(*args)


def _cast_bf16(a, rows, name):
    r, c = a.shape
    rows = min(rows, r)

    def body(a_ref, o_ref):
        o_ref[...] = a_ref[...].astype(BF16)

    return pl.pallas_call(
        body, name=name, grid=(r // rows,),
        in_specs=[pl.BlockSpec((rows, c), lambda i: (i, 0))],
        out_specs=pl.BlockSpec((rows, c), lambda i: (i, 0)),
        out_shape=jax.ShapeDtypeStruct((r, c), BF16),
        compiler_params=_params(("parallel",)),
    )(a)


def _align_shard(w):
    r, c = w.shape
    rows = min(128, r)

    def body(w_ref, o_ref, pad_ref):
        chip = 2 * lax.axis_index("x") + lax.axis_index("y")
        pad_ref[...] = jnp.zeros_like(pad_ref)
        pad_ref[:, 0:c] = w_ref[...]
        o_ref[...] = pltpu.roll(pad_ref[...], _by_chip(chip, SHIFTS), 1).astype(BF16)

    return pl.pallas_call(
        body, name="align_shard", grid=(r // rows,),
        in_specs=[pl.BlockSpec((rows, c), lambda i: (i, 0))],
        out_specs=pl.BlockSpec((rows, ALIGNED_W), lambda i: (i, 0)),
        out_shape=jax.ShapeDtypeStruct((r, ALIGNED_W), BF16),
        scratch_shapes=[pltpu.VMEM((rows, ALIGNED_W), F32)],
        compiler_params=_params(("parallel",)),
    )(w)


def _rms_in(x, w):
    n, d = x.shape
    tr = min(256, n)

    def body(x_ref, w_ref, h_ref):
        xv = x_ref[...]
        r = lax.rsqrt(jnp.mean(xv * xv, axis=-1, keepdims=True) + EPS)
        h_ref[...] = (xv * r * w_ref[...]).astype(BF16)

    return pl.pallas_call(
        body, name="rms_in", grid=(n // tr,),
        in_specs=[pl.BlockSpec((tr, d), lambda i: (i, 0)), pl.BlockSpec((1, d), lambda i: (0, 0))],
        out_specs=pl.BlockSpec((tr, d), lambda i: (i, 0)),
        out_shape=jax.ShapeDtypeStruct((n, d), BF16),
        compiler_params=_params(("parallel",)),
    )(x, w)


def _conv_silu(p, w_ref, taps):
    c = None
    for j in range(taps):
        t = _shift_down(p, taps - 1 - j) * w_ref[j:j + 1, :]
        c = t if c is None else c + t
    return c


def _prep_qkv(proj, cw):
    n = proj.shape[0]

    def body(p3, wq, wk, wv, q_ref, k_ref, v_ref):
        for kind, (w_ref, o_ref) in enumerate(((wq, q_ref), (wk, k_ref), (wv, v_ref))):
            c = _conv_silu(p3[:, kind * DH:(kind + 1) * DH], w_ref, 4)
            a = c * _sigmoid(c)
            if kind < 2:
                r = lax.rsqrt(jnp.sum(a * a, axis=-1, keepdims=True) + EPS)
                a = a * (r * (DH ** -0.5 if kind == 0 else 1.0))
            o_ref[...] = a

    col = pl.BlockSpec((n, DH), lambda h: (0, h))
    wcol = lambda base: pl.BlockSpec((4, DH), lambda h: (0, base + h))
    out = jax.ShapeDtypeStruct((n, GW), F32)
    return pl.pallas_call(
        body, name="prep_qkv", grid=(HEADS,),
        in_specs=[pl.BlockSpec((n, 3 * DH), lambda h: (0, h)), wcol(QB), wcol(KB), wcol(VB)],
        out_specs=[col] * 3, out_shape=[out] * 3,
        compiler_params=_params(("parallel",), 40 * 2**20),
    )(proj, cw, cw, cw)


def _prep_qkv_bwd(proj, cw, dq, dk, dv, dproj):
    n = proj.shape[0]

    def body(p3, wq, wk, wv, dq_ref, dk_ref, dv_ref, _, o3, gq, gk, gv):
        for kind, (w_ref, d_ref, g_ref) in enumerate(((wq, dq_ref, gq), (wk, dk_ref, gk), (wv, dv_ref, gv))):
            p = p3[:, kind * DH:(kind + 1) * DH]
            c = _conv_silu(p, w_ref, 4)
            s = _sigmoid(c)
            a = c * s
            d = d_ref[...]
            if kind < 2:
                r = lax.rsqrt(jnp.sum(a * a, axis=-1, keepdims=True) + EPS)
                sc = DH ** -0.5 if kind == 0 else 1.0
                d = (sc * r) * (d - a * ((r * r) * jnp.sum(d * a, axis=-1, keepdims=True)))
            dc = d * _dsilu(c, s)
            dp = None
            for j in range(4):
                g_ref[j:j + 1, :] = jnp.sum(dc * _shift_down(p, 3 - j), axis=0, keepdims=True)
                t = _shift_up(dc, 3 - j) * w_ref[j:j + 1, :]
                dp = t if dp is None else dp + t
            o3[:, kind * DH:(kind + 1) * DH] = dp.astype(BF16)

    col = pl.BlockSpec((n, DH), lambda h: (0, h))
    wcol = lambda base: pl.BlockSpec((4, DH), lambda h: (0, base + h))
    p3spec = pl.BlockSpec((n, 3 * DH), lambda h: (0, h))
    return pl.pallas_call(
        body, name="prep_qkv_bwd", grid=(HEADS,),
        in_specs=[p3spec, wcol(QB), wcol(KB), wcol(VB), col, col, col, ANY],
        out_specs=[p3spec] + [wcol(0)] * 3,
        out_shape=[jax.ShapeDtypeStruct(dproj.shape, BF16)] + [jax.ShapeDtypeStruct((4, GW), F32)] * 3,
        input_output_aliases={7: 0},
        compiler_params=_params(("parallel",), 48 * 2**20),
    )(proj, cw, cw, cw, dq, dk, dv, dproj)


def _tri(lower_incl):
    i = lax.broadcasted_iota(jnp.int32, (CH, CH), 0)
    j = lax.broadcasted_iota(jnp.int32, (CH, CH), 1)
    return jnp.where(i >= j, 1.0, 0.0) if lower_incl else jnp.where(j >= i, 1.0, 0.0)


def _lane(shape):
    return lax.broadcasted_iota(jnp.int32, shape, 1)


def _prep_bg(proj, ad):
    n = proj.shape[0]
    nch = n // CH

    def body(p_ref, ad_ref, bg_ref, bgt_ref):
        p = p_ref[...]
        lane = _lane(p.shape)
        beta = _sigmoid(p)
        xa = p + ad_ref[1:2, :]
        sp = jnp.maximum(xa, 0.0) + jnp.log(1.0 + jnp.exp(-jnp.abs(xa)))
        g = pltpu.roll(-jnp.exp(ad_ref[0:1, :]) * sp, DH - A_LANE + HEADS, 1)
        gc = _dot(_tri(True), g, NN, P_CUM)
        bg = jnp.where(lane < HEADS, beta, jnp.where(lane < 2 * HEADS, gc, 0.0))
        bg_ref[...] = bg
        bgt_ref[0] = bg.T

    return pl.pallas_call(
        body, name="prep_bg", grid=(nch,),
        in_specs=[pl.BlockSpec((CH, DH), lambda i: (i, BAB)), pl.BlockSpec((2, DH), lambda i: (0, 0))],
        out_specs=[pl.BlockSpec((CH, DH), lambda i: (i, 0)), pl.BlockSpec((1, DH, CH), lambda i: (i, 0, 0))],
        out_shape=[jax.ShapeDtypeStruct((n, DH), F32), jax.ShapeDtypeStruct((nch, DH, CH), F32)],
        compiler_params=_params(("parallel",)),
    )(proj, ad)


def _prep_bg_bwd(proj, ad, dbg, dproj):
    n = proj.shape[0]
    nch = n // CH

    def body(p_ref, ad_ref, d_ref, _, o_ref, ga_ref, gd_ref):
        p = p_ref[...]
        d = d_ref[...]
        lane = _lane(p.shape)
        beta = _sigmoid(p)
        xa = p + ad_ref[1:2, :]
        sp = jnp.maximum(xa, 0.0) + jnp.log(1.0 + jnp.exp(-jnp.abs(xa)))
        na = -jnp.exp(ad_ref[0:1, :])
        dg = pltpu.roll(_dot(_tri(False), d, NN, P_CUM), A_LANE - HEADS, 1)
        da = dg * na * _sigmoid(xa)
        is_g = lane >= A_LANE
        o_ref[...] = jnp.where(lane < HEADS, d * beta * (1.0 - beta), jnp.where(is_g, da, 0.0)).astype(BF16)
        ga = jnp.sum(jnp.where(is_g, dg * na * sp, 0.0), axis=0, keepdims=True)
        gd = jnp.sum(jnp.where(is_g, da, 0.0), axis=0, keepdims=True)

        @pl.when(pl.program_id(0) == 0)
        def _():
            ga_ref[...] = jnp.zeros_like(ga_ref)
            gd_ref[...] = jnp.zeros_like(gd_ref)

        ga_ref[...] += ga
        gd_ref[...] += gd

    one = pl.BlockSpec((1, DH), lambda i: (0, 0))
    return pl.pallas_call(
        body, name="prep_bg_bwd", grid=(nch,),
        in_specs=[pl.BlockSpec((CH, DH), lambda i: (i, BAB)), pl.BlockSpec((2, DH), lambda i: (0, 0)),
                  pl.BlockSpec((CH, DH), lambda i: (i, 0)), ANY],
        out_specs=[pl.BlockSpec((CH, DH), lambda i: (i, BAB)), one, one],
        out_shape=[jax.ShapeDtypeStruct(dproj.shape, BF16), jax.ShapeDtypeStruct((1, DH), F32),
                   jax.ShapeDtypeStruct((1, DH), F32)],
        input_output_aliases={3: 0},
        compiler_params=_params(("arbitrary",)),
    )(proj, ad, dbg, dproj)


def _gdn_out(o, proj, wg):
    n = o.shape[0]

    def body(o_ref, z_ref, w_ref, y_ref):
        ov, z = o_ref[...], z_ref[...]
        r = lax.rsqrt(jnp.mean(ov * ov, axis=-1, keepdims=True) + EPS)
        y_ref[...] = (ov * r * w_ref[...] * (z * _sigmoid(z))).astype(BF16)

    return pl.pallas_call(
        body, name="gdn_out", grid=(HEADS,),
        in_specs=[pl.BlockSpec((n, DH), lambda h: (0, h)), pl.BlockSpec((n, DH), lambda h: (0, ZB + h)),
                  pl.BlockSpec((1, DH), lambda h: (0, 0))],
        out_specs=pl.BlockSpec((n, DH), lambda h: (0, h)),
        out_shape=jax.ShapeDtypeStruct((n, 2 * GW), BF16),
        compiler_params=_params(("parallel",)),
    )(o, proj, wg)


def _gdn_out_bwd(o, proj, wg, dmix):
    n = o.shape[0]

    def body(o_ref, z_ref, w_ref, d_ref, do_ref, dz_ref, gw_ref):
        ov, z, d, w = o_ref[...], z_ref[...], d_ref[...], w_ref[...]
        r = lax.rsqrt(jnp.mean(ov * ov, axis=-1, keepdims=True) + EPS)
        nrm = ov * r
        s = _sigmoid(z)
        dz_ref[...] = (d * (nrm * w) * _dsilu(z, s)).astype(BF16)
        dn_w = d * (z * s)
        gw = jnp.sum(dn_w * nrm, axis=0, keepdims=True)
        dn = dn_w * w
        do_ref[...] = r * (dn - nrm * jnp.mean(dn * nrm, axis=-1, keepdims=True))

        @pl.when(pl.program_id(0) == 0)
        def _():
            gw_ref[...] = jnp.zeros_like(gw_ref)

        gw_ref[...] += gw

    return pl.pallas_call(
        body, name="gdn_out_bwd", grid=(HEADS,),
        in_specs=[pl.BlockSpec((n, DH), lambda h: (0, h)), pl.BlockSpec((n, DH), lambda h: (0, ZB + h)),
                  pl.BlockSpec((1, DH), lambda h: (0, 0)), pl.BlockSpec((n, DH), lambda h: (0, h))],
        out_specs=[pl.BlockSpec((n, DH), lambda h: (0, h)), pl.BlockSpec((n, DH), lambda h: (0, ZB + h)),
                   pl.BlockSpec((1, DH), lambda h: (0, 0))],
        out_shape=[jax.ShapeDtypeStruct((n, GW), F32), jax.ShapeDtypeStruct((n, GW_COLS), BF16),
                   jax.ShapeDtypeStruct((1, DH), F32)],
        compiler_params=_params(("arbitrary",)),
    )(o, proj, wg, dmix)


def _conv_branch(proj, w3, b, mix):
    n = proj.shape[0]

    def body(p4, w_ref, b_ref, _, y_ref):
        u = p4[:, DH:2 * DH] * p4[:, 2 * DH:3 * DH]
        cc = _conv_silu(u, w_ref, 3) + b_ref[...]
        z = p4[:, 3 * DH:4 * DH]
        y_ref[...] = (p4[:, 0:DH] * cc * (z * _sigmoid(z))).astype(BF16)

    return pl.pallas_call(
        body, name="conv_branch", grid=(HEADS,),
        in_specs=[pl.BlockSpec((n, 4 * DH), lambda h: (0, h)), pl.BlockSpec((3, DH), lambda h: (0, h)),
                  pl.BlockSpec((1, DH), lambda h: (0, h)), ANY],
        out_specs=pl.BlockSpec((n, DH), lambda h: (0, HEADS + h)),
        out_shape=jax.ShapeDtypeStruct(mix.shape, BF16),
        input_output_aliases={3: 0},
        compiler_params=_params(("parallel",), 40 * 2**20),
    )(proj, w3, b, mix)


def _conv_branch_bwd(proj, w3, b, dmix):
    n = proj.shape[0]

    def body(p4, w_ref, b_ref, d_ref, o4, gw_ref, gbias_ref):
        gb, gcv, hc, z = p4[:, 0:DH], p4[:, DH:2 * DH], p4[:, 2 * DH:3 * DH], p4[:, 3 * DH:4 * DH]
        d = d_ref[...]
        dgb, dgc, dhc, dzc = (o4.at[:, kk * DH:(kk + 1) * DH] for kk in range(4))
        u = gcv * hc
        cc = _conv_silu(u, w_ref, 3) + b_ref[...]
        s = _sigmoid(z)
        dzc[...] = (d * (gb * cc) * _dsilu(z, s)).astype(BF16)
        dp = d * (z * s)
        dgb[...] = (dp * cc).astype(BF16)
        dcc = dp * gb
        gbias_ref[...] = jnp.sum(dcc, axis=0, keepdims=True)
        du = None
        for j in range(3):
            gw_ref[j:j + 1, :] = jnp.sum(dcc * _shift_down(u, 2 - j), axis=0, keepdims=True)
            t = _shift_up(dcc, 2 - j) * w_ref[j:j + 1, :]
            du = t if du is None else du + t
        dgc[...] = (du * hc).astype(BF16)
        dhc[...] = (du * gcv).astype(BF16)

    p4spec = pl.BlockSpec((n, 4 * DH), lambda h: (0, h))
    return pl.pallas_call(
        body, name="conv_branch_bwd", grid=(HEADS,),
        in_specs=[p4spec, pl.BlockSpec((3, DH), lambda h: (0, h)), pl.BlockSpec((1, DH), lambda h: (0, h)),
                  pl.BlockSpec((n, DH), lambda h: (0, HEADS + h))],
        out_specs=[p4spec, pl.BlockSpec((3, DH), lambda h: (0, h)), pl.BlockSpec((1, DH), lambda h: (0, h))],
        out_shape=[jax.ShapeDtypeStruct((n, CW_COLS), BF16), jax.ShapeDtypeStruct((3, GW), F32),
                   jax.ShapeDtypeStruct((1, GW), F32)],
        compiler_params=_params(("parallel",), 48 * 2**20),
    )(proj, w3, b, dmix)


def _final_loss(out, tgt, wf):
    n, d = out.shape
    tr = min(256, n)

    def body(o_ref, t_ref, w_ref, do_ref, dob_ref, gw_ref, loss_ref):
        ov, w = o_ref[...], w_ref[...]
        r = lax.rsqrt(jnp.mean(ov * ov, axis=-1, keepdims=True) + EPS)
        nrm = ov * r
        e = nrm * w - t_ref[...]
        dy = e * (1.0 / d)
        dn = dy * w
        dout = r * (dn - nrm * jnp.mean(dn * nrm, axis=-1, keepdims=True))
        do_ref[...] = dout
        dob_ref[...] = dout.astype(BF16)

        @pl.when(pl.program_id(0) == 0)
        def _():
            gw_ref[...] = jnp.zeros_like(gw_ref)
            loss_ref[...] = jnp.zeros_like(loss_ref)

        gw_ref[...] += jnp.sum(dy * nrm, axis=0, keepdims=True)
        loss_ref[...] += (0.5 / d) * jnp.sum(jnp.sum(e * e, axis=-1, keepdims=True), axis=0, keepdims=True)

    row = pl.BlockSpec((tr, d), lambda i: (i, 0))
    return pl.pallas_call(
        body, name="final_loss", grid=(n // tr,),
        in_specs=[row, row, pl.BlockSpec((1, d), lambda i: (0, 0))],
        out_specs=[row, row, pl.BlockSpec((1, d), lambda i: (0, 0)), pl.BlockSpec((1, 1), lambda i: (0, 0))],
        out_shape=[jax.ShapeDtypeStruct((n, d), F32), jax.ShapeDtypeStruct((n, d), BF16),
                   jax.ShapeDtypeStruct((1, d), F32), jax.ShapeDtypeStruct((1, 1), F32)],
        compiler_params=_params(("arbitrary",)),
    )(out, tgt, wf)


def _rms_in_bwd(x, w, dh, dout):
    n, d = x.shape
    tr = min(256, n)

    def body(x_ref, w_ref, dh_ref, do_ref, dx_ref, gw_ref):
        xv, dhv = x_ref[...], dh_ref[...]
        r = lax.rsqrt(jnp.mean(xv * xv, axis=-1, keepdims=True) + EPS)
        xn = xv * r
        dxn = dhv * w_ref[...]
        dx_ref[...] = r * (dxn - xn * jnp.mean(dxn * xn, axis=-1, keepdims=True)) + do_ref[...]

        @pl.when(pl.program_id(0) == 0)
        def _():
            gw_ref[...] = jnp.zeros_like(gw_ref)

        gw_ref[...] += jnp.sum(dhv * xn, axis=0, keepdims=True)

    row = pl.BlockSpec((tr, d), lambda i: (i, 0))
    one = pl.BlockSpec((1, d), lambda i: (0, 0))
    return pl.pallas_call(
        body, name="rms_in_bwd", grid=(n // tr,),
        in_specs=[row, one, row, row], out_specs=[row, one],
        out_shape=[jax.ShapeDtypeStruct((n, d), F32), jax.ShapeDtypeStruct((1, d), F32)],
        compiler_params=_params(("arbitrary",)),
    )(x, w, dh, dout)


def _ij():
    i = lax.broadcasted_iota(jnp.int32, (CH, CH), 0)
    j = lax.broadcasted_iota(jnp.int32, (CH, CH), 1)
    return i, j


def _unit_lower_inverse(mats):
    i, j = _ij()
    eye = jnp.where(i == j, 1.0, 0.0)
    same16 = (i // 16) == (j // 16)
    same32 = (i // 32) == (j // 32)
    mm = lambda xs, ys: [_dot(x, y, NN, P_INV) for x, y in zip(xs, ys)]
    n1 = [jnp.where(same16, -a, 0.0) for a in mats]
    n2 = mm(n1, n1)
    n4 = mm(n2, n2)
    n8 = mm(n4, n4)
    t = [eye + x1 + x2 + x3 for x1, x2, x3 in zip(n1, n2, mm(n1, n2))]
    t = [x + y for x, y in zip(t, mm(t, n4))]
    t = [x + y for x, y in zip(t, mm(t, n8))]
    a1 = [jnp.where(same32 & jnp.logical_not(same16), a, 0.0) for a in mats]
    t = [x - y for x, y in zip(t, mm(t, mm(a1, t)))]
    a2 = [jnp.where(same32, 0.0, a) for a in mats]
    t = [x - y for x, y in zip(t, mm(t, mm(a2, t)))]
    return t


def _head_vectors(bg, bgt, h):
    bcol = bg[:, h:h + 1]
    gcol = bg[:, HEADS + h:HEADS + h + 1]
    grow = bgt[HEADS + h:HEADS + h + 1, :]
    return bcol, gcol, grow


def _decay(gcol, grow):
    i, j = _ij()
    return jnp.where(i >= j, jnp.exp(jnp.where(i >= j, gcol - grow, 0.0)), 0.0)


def _gdn_intra(q, k, v, bg, bgt):
    n = q.shape[0]
    nch = n // CH

    def body(q_ref, k_ref, v_ref, bg_ref, bgt_ref, u_ref, w_ref, p_ref, t_ref):
        bg, bgt = bg_ref[...], bgt_ref[0]
        i, j = _ij()
        sls = [slice(h * DH, (h + 1) * DH) for h in range(HEADS)]
        ks = [k_ref[:, sl] for sl in sls]
        vecs = [_head_vectors(bg, bgt, h) for h in range(HEADS)]
        decs = [_decay(gcol, grow) for _, gcol, grow in vecs]
        kks = [_dot(kh, kh, NT, P_GRAM) for kh in ks]
        qks = [_dot(q_ref[:, sl], kh, NT, P_GRAM) for sl, kh in zip(sls, ks)]
        ts = _unit_lower_inverse([jnp.where(i > j, bcol * kk * dec, 0.0)
                                  for (bcol, _, _), kk, dec in zip(vecs, kks, decs)])
        us = [_dot(t, v_ref[:, sl] * bcol, NN, P_SOL) for t, sl, (bcol, _, _) in zip(ts, sls, vecs)]
        ws = [_dot(t, kh * (bcol * jnp.exp(gcol)), NN, P_SOL) for t, kh, (bcol, gcol, _) in zip(ts, ks, vecs)]
        for h, sl in enumerate(sls):
            p_ref[0, h] = qks[h] * decs[h]
            t_ref[0, h] = ts[h]
            u_ref[:, sl] = us[h]
            w_ref[:, sl] = ws[h]

    row = pl.BlockSpec((CH, GW), lambda c: (c, 0))
    sq = pl.BlockSpec((1, HEADS, CH, CH), lambda c: (c, 0, 0, 0))
    big = jax.ShapeDtypeStruct((n, GW), F32)
    sqs = jax.ShapeDtypeStruct((nch, HEADS, CH, CH), F32)
    return pl.pallas_call(
        body, name="gdn_intra", grid=(nch,),
        in_specs=[row, row, row, pl.BlockSpec((CH, DH), lambda c: (c, 0)),
                  pl.BlockSpec((1, DH, CH), lambda c: (c, 0, 0))],
        out_specs=[row, row, sq, sq], out_shape=[big, big, sqs, sqs],
        compiler_params=_params(("parallel",)),
    )(q, k, v, bg, bgt)


def _gdn_scan(q, k, bg, u, w, p):
    n = q.shape[0]
    nch = n // CH

    def body(q_ref, k_ref, bg_ref, u_ref, w_ref, p_ref, o_ref, vn_ref, s_out, s_scr):
        @pl.when(pl.program_id(0) == 0)
        def _():
            s_scr[...] = jnp.zeros_like(s_scr)

        bg = bg_ref[...]
        hs = range(HEADS)
        sls = [slice(h * DH, (h + 1) * DH) for h in hs]
        gcols = [bg[:, HEADS + h:HEADS + h + 1] for h in hs]
        glasts = [g[CH - 1:CH, :] for g in gcols]
        ss = [s_scr[h] for h in hs]
        wss = [_dot(w_ref[:, sl], s, NN, P_SCAN) for sl, s in zip(sls, ss)]
        oqs = [_dot(q_ref[:, sl] * jnp.exp(g), s, NN, P_SCAN) for sl, s, g in zip(sls, ss, gcols)]
        vns = [u_ref[:, sl] - x for sl, x in zip(sls, wss)]
        ops = [_dot(p_ref[0, h], vn, NN, P_SCAN) for h, vn in zip(hs, vns)]
        sns = [_dot(k_ref[:, sl] * jnp.exp(gl - g), vn, TN, P_SCAN)
               for sl, gl, g, vn in zip(sls, glasts, gcols, vns)]
        for h, sl in enumerate(sls):
            s_out[0, :, sl] = ss[h]
            vn_ref[:, sl] = vns[h]
            o_ref[:, sl] = oqs[h] + ops[h]
            s_scr[h] = ss[h] * jnp.exp(glasts[h]) + sns[h]

    row = pl.BlockSpec((CH, GW), lambda c: (c, 0))
    big = jax.ShapeDtypeStruct((n, GW), F32)
    return pl.pallas_call(
        body, name="gdn_scan", grid=(nch,),
        in_specs=[row, row, pl.BlockSpec((CH, DH), lambda c: (c, 0)), row, row,
                  pl.BlockSpec((1, HEADS, CH, CH), lambda c: (c, 0, 0, 0))],
        out_specs=[row, row, pl.BlockSpec((1, DH, GW), lambda c: (c, 0, 0))],
        out_shape=[big, big, jax.ShapeDtypeStruct((nch, DH, GW), F32)],
        scratch_shapes=[pltpu.VMEM((HEADS, DH, DH), F32)],
        compiler_params=_params(("arbitrary",)),
    )(q, k, bg, u, w, p)


def _gdn_scan_bwd(q, k, bg, w, p, vn, s_in, do):
    n = q.shape[0]
    nch = n // CH
    rev = lambda c: nch - 1 - c

    def body(q_ref, k_ref, bg_ref, w_ref, p_ref, vn_ref, s_ref, do_ref,
             dqg_ref, dp_ref, du_ref, dw_ref, dks_ref, dgam_ref, ds_scr):
        @pl.when(pl.program_id(0) == 0)
        def _():
            ds_scr[...] = jnp.zeros_like(ds_scr)

        bg = bg_ref[...]
        lane = _lane((1, DH))
        hs = range(HEADS)
        sls = [slice(h * DH, (h + 1) * DH) for h in hs]
        gcols = [bg[:, HEADS + h:HEADS + h + 1] for h in hs]
        glasts = [g[CH - 1:CH, :] for g in gcols]
        ss = [s_ref[0, :, sl] for sl in sls]
        dss = [ds_scr[h] for h in hs]
        dos = [do_ref[:, sl] for sl in sls]
        vnl = [vn_ref[:, sl] for sl in sls]
        dqgs = [_dot(d, s, NT, P_SCANB) for d, s in zip(dos, ss)]
        dps = [_dot(d, vn, NT, P_SCANB) for d, vn in zip(dos, vnl)]
        dvn1 = [_dot(p_ref[0, h], d, TN, P_SCANB) for h, d in zip(hs, dos)]
        dvn2 = [_dot(k_ref[:, sl] * jnp.exp(gl - g), ds, NN, P_SCANB)
                for sl, gl, g, ds in zip(sls, glasts, gcols, dss)]
        dkss = [_dot(vn, ds, NT, P_SCANB) for vn, ds in zip(vnl, dss)]
        dsq = [_dot(q_ref[:, sl] * jnp.exp(g), d, TN, P_SCANB) for sl, g, d in zip(sls, gcols, dos)]
        dvns = [a + b for a, b in zip(dvn1, dvn2)]
        dws = [_dot(dvn, s, NT, P_SCANB) for dvn, s in zip(dvns, ss)]
        dsw = [_dot(w_ref[:, sl], dvn, TN, P_SCANB) for sl, dvn in zip(sls, dvns)]
        dgam = jnp.zeros((1, DH), F32)
        for h, sl in enumerate(sls):
            dqg_ref[:, sl] = dqgs[h]
            dp_ref[0, h] = dps[h]
            du_ref[:, sl] = dvns[h]
            dw_ref[:, sl] = -dws[h]
            dks_ref[:, sl] = dkss[h]
            tot = jnp.sum(jnp.sum(dss[h] * ss[h], axis=-1, keepdims=True), axis=0, keepdims=True)
            dgam = dgam + jnp.where(lane == h, tot, 0.0)
            ds_scr[h] = dss[h] * jnp.exp(glasts[h]) + dsq[h] - dsw[h]
        dgam_ref[0] = jnp.broadcast_to(dgam, (8, DH))

    row = pl.BlockSpec((CH, GW), lambda c: (rev(c), 0))
    sq = pl.BlockSpec((1, HEADS, CH, CH), lambda c: (rev(c), 0, 0, 0))
    big = jax.ShapeDtypeStruct((n, GW), F32)
    return pl.pallas_call(
        body, name="gdn_scan_bwd", grid=(nch,),
        in_specs=[row, row, pl.BlockSpec((CH, DH), lambda c: (rev(c), 0)), row, sq, row,
                  pl.BlockSpec((1, DH, GW), lambda c: (rev(c), 0, 0)), row],
        out_specs=[row, sq, row, row, row, pl.BlockSpec((1, 8, DH), lambda c: (rev(c), 0, 0))],
        out_shape=[big, jax.ShapeDtypeStruct((nch, HEADS, CH, CH), F32), big, big, big,
                   jax.ShapeDtypeStruct((nch, 8, DH), F32)],
        scratch_shapes=[pltpu.VMEM((HEADS, DH, DH), F32)],
        compiler_params=_params(("arbitrary",)),
    )(q, k, bg, w, p, vn, s_in, do)


def _gdn_intra_bwd(q, k, v, bg, bgt, t, u, w, p, dqg, dp, du, dw, dks, dgam):
    n = q.shape[0]
    nch = n // CH

    def body(q_ref, k_ref, v_ref, bg_ref, bgt_ref, t_ref, u_ref, w_ref, p_ref,
             dqg_ref, dp_ref, du_ref, dw_ref, dks_ref, dgam_ref, dq_ref, dk_ref, dv_ref, dbg_ref):
        bg, bgt = bg_ref[...], bgt_ref[0]
        dgam_all = dgam_ref[0]
        i, j = _ij()
        rows1 = lax.broadcasted_iota(jnp.int32, (CH, 1), 0)
        lane = _lane((CH, DH))
        dbg = jnp.zeros((CH, DH), F32)
        rsum = lambda x: jnp.sum(x, axis=-1, keepdims=True)
        hs = range(HEADS)
        sls = [slice(h * DH, (h + 1) * DH) for h in hs]
        qs = [q_ref[:, sl] for sl in sls]
        ks = [k_ref[:, sl] for sl in sls]
        vecs = [_head_vectors(bg, bgt, h) for h in hs]
        decs = [_decay(gcol, grow) for _, gcol, grow in vecs]
        ths = [t_ref[0, h] for h in hs]
        drus = [_dot(th, du_ref[:, sl], TN, P_BWD) for th, sl in zip(ths, sls)]
        drws = [_dot(th, dw_ref[:, sl], TN, P_BWD) for th, sl in zip(ths, sls)]
        kks = [_dot(kh, kh, NT, P_GRAM) for kh in ks]
        da1 = [_dot(dru, u_ref[:, sl], NT, P_BWD) for dru, sl in zip(drus, sls)]
        da2 = [_dot(drw, w_ref[:, sl], NT, P_BWD) for drw, sl in zip(drws, sls)]
        das = [jnp.where(i > j, -(x + y), 0.0) for x, y in zip(da1, da2)]
        dkks = [da * bcol * dec for da, (bcol, _, _), dec in zip(das, vecs, decs)]
        dqks = [dp_ref[0, h] * dec for h, dec in zip(hs, decs)]
        dq_ps = [_dot(dqk, kh, NN, P_BWD) for dqk, kh in zip(dqks, ks)]
        dk_ps = [_dot(dqk, qh, TN, P_BWD) for dqk, qh in zip(dqks, qs)]
        dk_as = [_dot(dkk, kh, NN, P_BWD) for dkk, kh in zip(dkks, ks)]
        dk_bs = [_dot(dkk, kh, TN, P_BWD) for dkk, kh in zip(dkks, ks)]
        for h, sl in enumerate(sls):
            qh, kh, vh = qs[h], ks[h], v_ref[:, sl]
            bcol, gcol, _ = vecs[h]
            dec, dru, drw, da, kk = decs[h], drus[h], drws[h], das[h], kks[h]
            gam = jnp.exp(gcol)
            glast = gcol[CH - 1:CH, :]
            e = jnp.exp(glast - gcol)
            kg = kh * gam
            dv_ref[:, sl] = bcol * dru
            dbeta = rsum(dru * vh) + rsum(drw * kg) + rsum(da * kk * dec)
            dgc = rsum(drw * kg) * bcol
            dqg = dqg_ref[:, sl]
            dksh = dks_ref[:, sl]
            dq_ref[:, sl] = gam * dqg + dq_ps[h]
            dk_ref[:, sl] = (bcol * gam) * drw + dk_ps[h] + dk_as[h] + dk_bs[h] + dksh * e
            tk = rsum(dksh * kh) * e
            mdec = da * (bcol * kk * dec) + dp_ref[0, h] * p_ref[0, h]
            col = rsum(jnp.where(i == j, jnp.sum(mdec, axis=0, keepdims=True), 0.0))
            dgc = dgc + rsum(mdec) - col
            dgc = dgc + rsum(dqg * qh) * gam - tk
            dglast = jnp.sum(tk, axis=0, keepdims=True) + dgam_all[0:1, h:h + 1] * jnp.exp(glast)
            dgc = dgc + jnp.where(rows1 == CH - 1, dglast, 0.0)
            dbg = dbg + jnp.where(lane == h, dbeta, 0.0) + jnp.where(lane == HEADS + h, dgc, 0.0)
        dbg_ref[...] = dbg

    row = pl.BlockSpec((CH, GW), lambda c: (c, 0))
    sq = pl.BlockSpec((1, HEADS, CH, CH), lambda c: (c, 0, 0, 0))
    small = pl.BlockSpec((CH, DH), lambda c: (c, 0))
    big = jax.ShapeDtypeStruct((n, GW), F32)
    return pl.pallas_call(
        body, name="gdn_intra_bwd", grid=(nch,),
        in_specs=[row, row, row, small, pl.BlockSpec((1, DH, CH), lambda c: (c, 0, 0)), sq, row, row, sq,
                  row, sq, row, row, row, pl.BlockSpec((1, 8, DH), lambda c: (c, 0, 0))],
        out_specs=[row, row, row, small],
        out_shape=[big, big, big, jax.ShapeDtypeStruct((n, DH), F32)],
        compiler_params=_params(("parallel",)),
    )(q, k, v, bg, bgt, t, u, w, p, dqg, dp, du, dw, dks, dgam)


def _local_step(x, tgt, h, w_g, cqw, late, norm_in_w, ad, gdn_norm_w, conv_b, final_norm_w,
                on_grad_c=None, on_grad_g=None):
    proj_g = _matmul(h, w_g, NN, F32, 512, 1408, 1024, "mm_proj_g", n=GW_COLS)
    q, k, v = _prep_qkv(proj_g, cqw)
    bg, bgt = _prep_bg(proj_g, ad)
    u, w, p, t = _gdn_intra(q, k, v, bg, bgt)
    o, vn, s_in = _gdn_scan(q, k, bg, u, w, p)
    w_c, w_out, conv_w = late(o)
    proj_c = _matmul(h, w_c, NN, F32, 512, 1024, 1024, "mm_proj_c", n=CW_COLS)
    mix = _conv_branch(proj_c, conv_w, conv_b, _gdn_out(o, proj_g, gdn_norm_w))
    out = _matmul(mix, w_out, NN, F32, 512, 512, 2048, "mm_out", add=x)
    dout, dout_b, g_fn, loss = _final_loss(out, tgt, final_norm_w)

    dmix = _matmul(dout_b, w_out, NT, F32, 512, 1024, 1024, "mm_dmix")
    g_wout = _matmul(mix, dout_b, TN, BF16, 512, 512, 2048, "mm_gwout")
    do, dproj_g, g_gn = _gdn_out_bwd(o, proj_g, gdn_norm_w, dmix)
    dproj_c, g_cw, g_cb = _conv_branch_bwd(proj_c, conv_w, conv_b, dmix)
    g_c = _matmul(h, dproj_c, TN, BF16, 512, 1024, 2048, "mm_gwin_c")
    if on_grad_c is not None:
        do = on_grad_c(g_c, g_wout, do)
    dqg, dp, du, dw, dks, dgam = _gdn_scan_bwd(q, k, bg, w, p, vn, s_in, do)
    dq, dk, dv, dbg = _gdn_intra_bwd(q, k, v, bg, bgt, t, u, w, p, dqg, dp, du, dw, dks, dgam)
    dproj_g, gq, gk, gv = _prep_qkv_bwd(proj_g, cqw, dq, dk, dv, dproj_g)
    dproj_g, g_al, g_dt = _prep_bg_bwd(proj_g, ad, dbg, dproj_g)
    g_g = _matmul(h, dproj_g, TN, BF16, 512, 1408, 2048, "mm_gwin_g")
    if on_grad_g is not None:
        dproj_g = on_grad_g(g_g, dproj_g)
    dh = _matmul(dproj_g, w_g, NT, F32, 512, 1024, 1408, "mm_dh_g", k=GW_COLS)
    dh = _matmul(dproj_c, w_c, NT, F32, 512, 1024, 1024, "mm_dh_c", k=CW_COLS, add=dh)
    gx, g_nin = _rms_in_bwd(x, norm_in_w, dh, dout)
    small = dict(nin=g_nin, cb=g_cb, fn=g_fn, al=g_al, dt=g_dt, gn=g_gn, cq=(gq, gk, gv), cw=g_cw, loss=loss)
    return gx, small, (g_g, g_c, g_wout)


def _place():
    x, y, c = lax.axis_index("x"), lax.axis_index("y"), lax.axis_index("c")
    chips = [(1 - x, y), (x, 1 - y), (1 - x, 1 - y)]
    return x, y, c, chips


def _blk(ref, b):
    if isinstance(b, int):
        return ref.at[:, b * DH:(b + 1) * DH]
    return ref.at[:, pl.ds(pl.multiple_of(b * DH, DH), DH)]


HBM = pl.BlockSpec(memory_space=pltpu.HBM)
SEM = pl.BlockSpec(memory_space=pltpu.SEMAPHORE)
EFFECT = pltpu.SideEffectType.DATAFLOW_SIDE_EFFECTING


def _split_start(name, issue, bufs, n_sems):
    nbuf = len(bufs)

    def body(*refs):
        issue(refs[:nbuf], refs[nbuf], refs[nbuf + 1])
        refs[-1][...] = jnp.zeros_like(refs[-1])

    out = pl.pallas_call(
        body, name=name,
        out_shape=(pltpu.SemaphoreType.DMA((n_sems,)), pltpu.SemaphoreType.DMA((n_sems,)),
                   *[pltpu.HBM(b.shape, b.dtype) for b in bufs], jax.ShapeDtypeStruct((8, DH), F32)),
        in_specs=[HBM] * nbuf,
        out_specs=(SEM, SEM, *[HBM] * nbuf, pl.BlockSpec(memory_space=pltpu.VMEM)),
        input_output_aliases={a: 2 + a for a in range(nbuf)},
        compiler_params=pltpu.CompilerParams(has_side_effects=EFFECT),
    )(*[pltpu.with_memory_space_constraint(b, pltpu.HBM) for b in bufs])
    return out[0], out[1], list(out[2:2 + nbuf]), out[-1]


def _split_wait(name, await_, send_sems, recv_sems, bufs, after):
    nbuf = len(bufs)

    def body(*refs):
        await_(refs[:nbuf], refs[nbuf], refs[nbuf + 1])

    out = pl.pallas_call(
        body, name=name,
        out_shape=tuple(pltpu.HBM(b.shape, b.dtype) for b in bufs),
        in_specs=[HBM] * nbuf + [SEM, SEM, ANY], out_specs=tuple([HBM] * nbuf),
        input_output_aliases={a: a for a in range(nbuf)},
        compiler_params=pltpu.CompilerParams(has_side_effects=EFFECT),
    )(*bufs, send_sems, recv_sems, after)
    return list(out)


def _phase_blocks(chip, phase, edges):
    return [(b, blk) for b, (grp, blk) in enumerate(_shard_blocks(chip, edges)) if grp == phase]


def _cols(ref, nblk):
    return ref.at[:, 0:nblk * DH]


def _place_own(a_shard, wo, cq, cw):
    d = a_shard.shape[0]

    def body(a_ref, wo_ref, cq_ref, cw_ref, wg_ref, wc_ref, wog_ref, cqg_ref, cwg_ref, sems):
        x, y, _, _ = _place()
        mine = 2 * x + y
        small = [pltpu.make_async_copy(s_ref, g_ref.at[mine], sems.at[1 + a])
                 for a, (s_ref, g_ref) in enumerate(((wo_ref, wog_ref), (cq_ref, cqg_ref), (cw_ref, cwg_ref)))]
        for cp in small:
            cp.start()
        for s in range(4):
            @pl.when(mine == s)
            def _():
                cps = [pltpu.make_async_copy(_blk(a_ref, b), _blk(wg_ref if grp == "g" else wc_ref, blk), sems.at[0])
                       for b, (grp, blk) in enumerate(_shard_blocks(s, True))]
                for cp in cps:
                    cp.start()
                for cp in cps:
                    cp.wait()
        for cp in small:
            cp.wait()

    return pl.pallas_call(
        body, name="place_own", in_specs=[ANY] * 4, out_specs=[ANY] * 5,
        out_shape=[jax.ShapeDtypeStruct((d, WG_BLOCKS * DH), a_shard.dtype),
                   jax.ShapeDtypeStruct((d, WC_BLOCKS * DH), a_shard.dtype)]
        + [jax.ShapeDtypeStruct((4,) + s.shape, s.dtype) for s in (wo, cq, cw)],
        scratch_shapes=[pltpu.SemaphoreType.DMA((4,))],
    )(a_shard, wo, cq, cw)


def _gather_start(phase, a_shard, w_grp, singles):
    ns = len(singles)

    def issue(refs, send_sems, recv_sems):
        a_ref, w_ref = refs[0], refs[1]
        x, y, c, chips = _place()
        mine = 2 * x + y
        for jj, (px, py) in enumerate(chips):
            to = dict(device_id=(px, py, c), device_id_type=MESH)
            for a in range(ns):
                pltpu.make_async_remote_copy(
                    src_ref=refs[2 + 2 * a], dst_ref=refs[3 + 2 * a].at[mine],
                    send_sem=send_sems.at[(1 + ns) * jj + 1 + a], recv_sem=recv_sems.at[(1 + ns) * jj + 1 + a],
                    **to).start()
        for s in range(4):
            blocks = _phase_blocks(s, phase, True)
            if blocks:
                @pl.when(mine == s)
                def _():
                    for jj, (px, py) in enumerate(chips):
                        for b, blk in blocks:
                            pltpu.make_async_remote_copy(
                                src_ref=_blk(a_ref, b), dst_ref=_blk(w_ref, blk),
                                send_sem=send_sems.at[(1 + ns) * jj], recv_sem=recv_sems.at[(1 + ns) * jj],
                                device_id=(px, py, c), device_id_type=MESH).start()

    bufs = [a_shard, w_grp] + [t for pair in singles for t in pair]
    return _split_start("gather_start_" + phase, issue, bufs, 3 * (1 + ns))


def _gather_wait(phase, send_sems, recv_sems, bufs, after):
    ns = (len(bufs) - 2) // 2

    def await_(refs, send_sems, recv_sems):
        a_ref, w_ref = refs[0], refs[1]
        x, y, c, chips = _place()
        mine = 2 * x + y
        for jj, (px, py) in enumerate(chips):
            to = dict(device_id=(px, py, c), device_id_type=MESH)
            peer = 2 * px + py
            for a in range(ns):
                cp = pltpu.make_async_remote_copy(
                    src_ref=refs[2 + 2 * a], dst_ref=refs[3 + 2 * a].at[mine],
                    send_sem=send_sems.at[(1 + ns) * jj + 1 + a], recv_sem=recv_sems.at[(1 + ns) * jj + 1 + a], **to)
                cp.wait_recv()
                cp.wait_send()
            for s in range(4):
                nblk = len(_phase_blocks(s, phase, True))
                if nblk:
                    both = pltpu.make_async_remote_copy(
                        src_ref=_cols(a_ref, nblk), dst_ref=_cols(w_ref, nblk),
                        send_sem=send_sems.at[(1 + ns) * jj], recv_sem=recv_sems.at[(1 + ns) * jj], **to)

                    @pl.when(peer == s)
                    def _():
                        both.wait_recv()

                    @pl.when(mine == s)
                    def _():
                        both.wait_send()

    return _split_wait("gather_wait_" + phase, await_, send_sems, recv_sems, bufs, after)


def _merge_edges(w, edge0, mixed, name):
    d = w.shape[0]

    def body(e_ref, o_ref):
        o_ref[...] = e_ref[:, 0:DH] + e_ref[:, DH:2 * DH]

    def to_block(i):
        r = mixed[-1]
        for kk in range(len(mixed) - 2, -1, -1):
            r = jnp.where(i == kk, mixed[kk], r)
        return r

    return pl.pallas_call(
        body, name=name, grid=(len(mixed),),
        in_specs=[pl.BlockSpec((d, 2 * DH), lambda i: (0, edge0 // 2 + i))],
        out_specs=pl.BlockSpec((d, DH), lambda i: (0, to_block(i))),
        out_shape=jax.ShapeDtypeStruct(w.shape, w.dtype),
        input_output_aliases={0: 0},
        compiler_params=_params(("arbitrary",)),
    )(w)


def _scatter_start(phase, g_grp, land, singles):
    ns = len(singles)

    def issue(refs, send_sems, recv_sems):
        g_ref, land_ref = refs[0], refs[1]
        x, y, c, chips = _place()
        for jj, (px, py) in enumerate(chips):
            to = dict(device_id=(px, py, c), device_id_type=MESH)
            peer = 2 * px + py
            for a in range(ns):
                pltpu.make_async_remote_copy(
                    src_ref=refs[2 + 2 * a].at[peer], dst_ref=refs[3 + 2 * a].at[jj],
                    send_sem=send_sems.at[(1 + ns) * jj + 1 + a], recv_sem=recv_sems.at[(1 + ns) * jj + 1 + a],
                    **to).start()
            for s in range(4):
                blocks = _phase_blocks(s, phase, False)
                if blocks:
                    @pl.when(peer == s)
                    def _():
                        for b, blk in blocks:
                            pltpu.make_async_remote_copy(
                                src_ref=_blk(g_ref, blk), dst_ref=_blk(land_ref.at[jj], b),
                                send_sem=send_sems.at[(1 + ns) * jj], recv_sem=recv_sems.at[(1 + ns) * jj],
                                **to).start()

    bufs = [g_grp, land] + [t for pair in singles for t in pair]
    return _split_start("scatter_start_" + phase, issue, bufs, 3 * (1 + ns))


def _scatter_wait(phase, send_sems, recv_sems, bufs, after):
    ns = (len(bufs) - 2) // 2

    def await_(refs, send_sems, recv_sems):
        g_ref, land_ref = refs[0], refs[1]
        x, y, c, chips = _place()
        mine = 2 * x + y
        for jj, (px, py) in enumerate(chips):
            to = dict(device_id=(px, py, c), device_id_type=MESH)
            peer = 2 * px + py
            for a in range(ns):
                cp = pltpu.make_async_remote_copy(
                    src_ref=refs[2 + 2 * a].at[peer], dst_ref=refs[3 + 2 * a].at[jj],
                    send_sem=send_sems.at[(1 + ns) * jj + 1 + a], recv_sem=recv_sems.at[(1 + ns) * jj + 1 + a], **to)
                cp.wait_recv()
                cp.wait_send()
            for s in range(4):
                nblk = len(_phase_blocks(s, phase, False))
                if nblk:
                    both = pltpu.make_async_remote_copy(
                        src_ref=_cols(g_ref, nblk), dst_ref=_cols(land_ref.at[jj], nblk),
                        send_sem=send_sems.at[(1 + ns) * jj], recv_sem=recv_sems.at[(1 + ns) * jj], **to)

                    @pl.when(mine == s)
                    def _():
                        both.wait_recv()

                    @pl.when(peer == s)
                    def _():
                        both.wait_send()

    return _split_wait("scatter_wait_" + phase, await_, send_sems, recv_sems, bufs, after)


def _own_grad_blocks(g_g, g_c, g_out):
    d = g_g.shape[0]

    def body(gg_ref, gc_ref, go_ref, own_ref, owno_ref, sems):
        x, y, _, _ = _place()
        mine = 2 * x + y
        out = pltpu.make_async_copy(go_ref.at[mine], owno_ref, sems.at[1])
        out.start()
        for s in range(4):
            @pl.when(mine == s)
            def _():
                cps = [pltpu.make_async_copy(_blk(gg_ref if grp == "g" else gc_ref, blk), _blk(own_ref, b), sems.at[0])
                       for b, (grp, blk) in enumerate(_shard_blocks(s, False))]
                for cp in cps:
                    cp.start()
                for cp in cps:
                    cp.wait()
        out.wait()

    return pl.pallas_call(
        body, name="own_grad_blocks", in_specs=[ANY] * 3, out_specs=[ANY] * 2,
        out_shape=[jax.ShapeDtypeStruct((d, ALIGNED_W), g_g.dtype), jax.ShapeDtypeStruct(g_out.shape[1:], g_out.dtype)],
        scratch_shapes=[pltpu.SemaphoreType.DMA((2,))],
    )(g_g, g_c, g_out)


def _final_exchange(parts, pack):
    npart = len(parts)

    def body(*refs):
        ins, pack_ref = refs[:npart], refs[npart]
        outs, packs = refs[npart + 1:2 * npart + 1], refs[2 * npart + 1]
        send_sems, recv_sems, psend, precv, loc_sem = refs[2 * npart + 2:]
        x, y, c, _ = _place()
        me = 4 * x + 2 * y + c
        local = pltpu.make_async_copy(pack_ref, packs.at[me], loc_sem)
        local.start()
        cps = [pltpu.make_async_remote_copy(
            src_ref=ins[a], dst_ref=outs[a], send_sem=send_sems.at[a], recv_sem=recv_sems.at[a],
            device_id=(x, y, 1 - c), device_id_type=MESH) for a in range(npart)]
        for r in range(1, 8):
            dx, dy, dc = (r >> 2) & 1, (r >> 1) & 1, r & 1
            peer = (x + dx - 2 * x * dx, y + dy - 2 * y * dy, c + dc - 2 * c * dc)
            cps.append(pltpu.make_async_remote_copy(
                src_ref=pack_ref, dst_ref=packs.at[me], send_sem=psend.at[r - 1], recv_sem=precv.at[r - 1],
                device_id=peer, device_id_type=MESH))
        for cp in cps:
            cp.start()
        for cp in cps:
            cp.wait_recv()
        for cp in cps:
            cp.wait_send()
        local.wait()

    return pl.pallas_call(
        body, name="final_exchange",
        in_specs=[ANY] * (npart + 1), out_specs=[ANY] * (npart + 1),
        out_shape=[jax.ShapeDtypeStruct(p.shape, p.dtype) for p in parts]
        + [jax.ShapeDtypeStruct((8,) + pack.shape, pack.dtype)],
        scratch_shapes=[pltpu.SemaphoreType.DMA((npart,)), pltpu.SemaphoreType.DMA((npart,)),
                        pltpu.SemaphoreType.DMA((7,)), pltpu.SemaphoreType.DMA((7,)), pltpu.SemaphoreType.DMA],
    )(*parts, pack)


def _sum_blocks(own, land, rows, name):
    r, cdim = own.shape
    rows = min(rows, r)

    def body(own_ref, land_ref, o_ref):
        acc = own_ref[...].astype(F32)
        for jj in range(3):
            acc = acc + land_ref[jj].astype(F32)
        o_ref[...] = acc

    return pl.pallas_call(
        body, name=name, grid=(r // rows,),
        in_specs=[pl.BlockSpec((rows, cdim), lambda i: (i, 0)), pl.BlockSpec((3, rows, cdim), lambda i: (0, i, 0))],
        out_specs=pl.BlockSpec((rows, cdim), lambda i: (i, 0)),
        out_shape=jax.ShapeDtypeStruct((r, cdim), F32),
        compiler_params=_params(("parallel",), 40 * 2**20),
    )(own, land)


def _sum_packs(packs):
    def body(p_ref, o_ref):
        acc = p_ref[0]
        for d in range(1, 8):
            acc = acc + p_ref[d]
        o_ref[...] = acc

    return pl.pallas_call(
        body, name="sum_packs", out_shape=jax.ShapeDtypeStruct(packs.shape[1:], F32),
    )(packs)


def _adamw_update(g, w_ref, m_ref, v_ref, go, do, mo, vo):
    c1 = 1.0 / (1.0 - ADAM_B1 ** ADAM_STEP)
    c2 = 1.0 / (1.0 - ADAM_B2 ** ADAM_STEP)
    mn = ADAM_B1 * m_ref[...] + (1.0 - ADAM_B1) * g
    vn = ADAM_B2 * v_ref[...] + (1.0 - ADAM_B2) * (g * g)
    go[...] = g
    mo[...] = mn
    vo[...] = vn
    do[...] = -ADAM_LR * ((mn * c1) / (jnp.sqrt(vn * c2) + ADAM_EPS) + ADAM_WD * w_ref[...])


def _adamw(w, m, v, g1, g2, rows, name, aligned=False):
    r, cdim = w.shape
    gdim = g1.shape[1]
    rows = min(rows, r)

    def body(*refs):
        n_in = 4 if g2 is None else 5
        w_ref, m_ref, v_ref, g_ref = refs[:4]
        g = g_ref[...] if g2 is None else g_ref[...] + refs[4][...]
        if aligned:
            chip = 2 * lax.axis_index("x") + lax.axis_index("y")
            back = [(ALIGNED_W - s) % ALIGNED_W for s in SHIFTS]
            pad_ref = refs[-1]
            pad_ref[...] = pltpu.roll(g, _by_chip(chip, back), 1)
            g = pad_ref[:, 0:cdim]
        _adamw_update(g, w_ref, m_ref, v_ref, *refs[n_in:n_in + 4])

    blk = pl.BlockSpec((rows, cdim), lambda i: (i, 0))
    gblk = pl.BlockSpec((rows, gdim), lambda i: (i, 0))
    args = [w, m, v, g1] + ([] if g2 is None else [g2])
    shp = jax.ShapeDtypeStruct((r, cdim), F32)
    return pl.pallas_call(
        body, name=name, grid=(r // rows,),
        in_specs=[blk] * 3 + [gblk] * (len(args) - 3), out_specs=[blk] * 4, out_shape=[shp] * 4,
        scratch_shapes=[pltpu.VMEM((rows, gdim), F32)] if aligned else [],
        compiler_params=_params(("parallel",), 20 * rows * gdim * 4 + 8 * 2**20),
    )(*args)


def _pad_lanes(a, width):
    return jnp.pad(a, ((0, 0), (0, width - a.shape[1])))


def _gathered_to_full(g):
    return jnp.transpose(g, (1, 0, 2)).reshape(g.shape[1], 4 * g.shape[2])


def _row(a):
    return _pad_lanes(a.reshape(1, -1), 1024)


def _small_pack(nin, cb, fn, al, dt, gn, cqw_shard, cw_shard):
    ad = jnp.concatenate([al.reshape(1, -1), dt.reshape(1, -1)], axis=1)
    rows = [_row(nin), _row(cb), _row(fn), _row(ad), _row(gn), cqw_shard.reshape(3, 1024), _row(cw_shard)]
    out = jnp.concatenate(rows, axis=0)
    return jnp.pad(out, ((0, 16 - out.shape[0]), (0, 0)))


def kernel(x, norm_in_w, w_in, conv_qkv_w, A_log, dt_bias, gdn_norm_w, conv_w, conv_b, w_out, final_norm_w, loss_target, m_norm_in_w, m_w_in, m_conv_qkv_w, m_A_log, m_dt_bias, m_gdn_norm_w, m_conv_w, m_conv_b, m_w_out, m_final_norm_w, v_norm_in_w, v_w_in, v_conv_qkv_w, v_A_log, v_dt_bias, v_gdn_norm_w, v_conv_w, v_conv_b, v_w_out, v_final_norm_w):
    chip = 2 * lax.axis_index("x") + lax.axis_index("y")
    a_shard = _align_shard(w_in[0])
    wo_b = _cast_bf16(w_out[0], 256, "cast_w_out")
    wg0, wc0, wog0, cqg0, cwg0 = _place_own(a_shard, wo_b, conv_qkv_w[0], conv_w[0])
    ss_g, rs_g, bufs_g, tok_g = _gather_start("g", a_shard, wg0, [(conv_qkv_w[0], cqg0)])
    ss_c, rs_c, bufs_c, tok_c = _gather_start("c", bufs_g[0], wc0, [(conv_w[0], cwg0), (wo_b, wog0)])
    x0, _, _ = lax.optimization_barrier((x[0], tok_g, tok_c))
    h = _rms_in(x0, norm_in_w)
    a_thru, wg, _, cq_g = _gather_wait("g", ss_g, rs_g, [bufs_c[0]] + bufs_g[1:], h)
    w_g = _merge_edges(wg, G_EDGE, G_MIXED, "merge_edges_g")
    cqw = _gathered_to_full(cq_g)
    ad = jnp.pad(jnp.concatenate([A_log, dt_bias], axis=0), ((0, 0), (A_LANE, 0)))
    d_model = x.shape[-1]

    def late(o):
        _, wc, _, cw_g, _, wo_g = _gather_wait("c", ss_c, rs_c, [a_thru] + bufs_c[1:], o)
        return (_merge_edges(wc, C_EDGE, C_MIXED, "merge_edges_c"), wo_g.reshape(2 * GW, d_model),
                _gathered_to_full(cw_g))

    scat = {}

    def on_grad_c(g_c, g_wout, do):
        go4 = g_wout.reshape(4, GW // 2, d_model)
        land = lax.empty((3, d_model, ALIGNED_W), BF16)
        land_o = lax.empty((3, GW // 2, d_model), BF16)
        ss, rs, bufs, tok = _scatter_start("c", g_c, land, [(go4, land_o)])
        scat["c"] = (ss, rs, bufs)
        do, _ = lax.optimization_barrier((do, tok))
        return do

    def on_grad_g(g_g, dproj_g):
        ss, rs, bufs, tok = _scatter_start("g", g_g, scat["c"][2][1], [])
        scat["g"] = (ss, rs, bufs)
        dproj_g, _ = lax.optimization_barrier((dproj_g, tok))
        return dproj_g

    gx, sm, _ = _local_step(x0, loss_target[0], h, w_g, cqw, late, norm_in_w, ad, gdn_norm_w, conv_b,
                            final_norm_w.reshape(1, -1), on_grad_c, on_grad_g)

    ss, rs, bufs = scat["g"]
    g_g, land = _scatter_wait("g", ss, rs, bufs, gx)
    ss, rs, bufs = scat["c"]
    g_c, land, go4, land_o = _scatter_wait("c", ss, rs, [bufs[0], land, bufs[2], bufs[3]], gx)
    own_in, own_out = _own_grad_blocks(g_g, g_c, go4)
    part_in = _sum_blocks(own_in, land, 128, "sum_w_in")
    part_out = _sum_blocks(own_out, land_o, 128, "sum_w_out")
    ad_g = jnp.concatenate([sm["al"][:, A_LANE:], sm["dt"][:, A_LANE:]], axis=1)
    pack = jnp.concatenate([_row(sm["nin"]), _row(sm["cb"]), _row(sm["fn"]), _row(ad_g), _row(sm["gn"]),
                            jnp.concatenate(sm["cq"], axis=1).reshape(12, 1024), sm["cw"], _row(sm["loss"])], axis=0)
    pack = jnp.pad(pack, ((0, PACK_ROWS - pack.shape[0]), (0, 0)))
    sib_in, sib_out, packs = _final_exchange([part_in, part_out], pack)
    tot = _sum_packs(packs)

    g_wi, d_wi, m_wi, v_wi = _adamw(w_in[0], m_w_in[0], v_w_in[0], part_in, sib_in, 64, "adamw_w_in", aligned=True)
    g_wo, d_wo, m_wo, v_wo = _adamw(w_out[0], m_w_out[0], v_w_out[0], part_out, sib_out, 128, "adamw_w_out")
    g_cq_sh = lax.dynamic_slice_in_dim(tot[R_CQ:R_CQ + 12].reshape(4, 3 * GW), chip * 768, 768, axis=1)
    g_cw_sh = lax.dynamic_slice_in_dim(tot[R_CW:R_CW + 3], chip * 256, 256, axis=1)
    sp = lambda nin, cb, fn, al, dt, gn, cq, cwv: _small_pack(nin, cb, fn, al, dt, gn, cq[0], cwv[0])
    g_s = _small_pack(tot[R_NIN], tot[R_CB], tot[R_FN], tot[R_AD, :HEADS], tot[R_AD, HEADS:2 * HEADS],
                      tot[R_GN, :DH], g_cq_sh, g_cw_sh)
    w_s = sp(norm_in_w, conv_b, final_norm_w, A_log, dt_bias, gdn_norm_w, conv_qkv_w, conv_w)
    m_s = sp(m_norm_in_w, m_conv_b, m_final_norm_w, m_A_log, m_dt_bias, m_gdn_norm_w, m_conv_qkv_w, m_conv_w)
    v_s = sp(v_norm_in_w, v_conv_b, v_final_norm_w, v_A_log, v_dt_bias, v_gdn_norm_w, v_conv_qkv_w, v_conv_w)
    small = _adamw(w_s, m_s, v_s, g_s, None, 16, "adamw_small")

    def unpack(a, big_in, big_out):
        return (a[0:1], big_in[None], a[5:8].reshape(1, 4, 768), a[3:4, :HEADS], a[3:4, HEADS:2 * HEADS],
                a[4:5, :DH], a[8, :768].reshape(1, 3, 256), a[1:2], big_out[None], a[2])

    loss = tot[R_LOSS, 0]
    return (loss, gx[None], *unpack(small[0], g_wi, g_wo), *unpack(small[1], d_wi, d_wo),
            *unpack(small[2], m_wi, m_wo), *unpack(small[3], v_wi, v_wo))
```

```python
import functools
import math

import jax
import jax.numpy as jnp
from jax import lax
from jax.experimental import pallas as pl
from jax.experimental.pallas import tpu as pltpu

F32 = jnp.float32
BF16 = jnp.bfloat16
MESH = pl.DeviceIdType.MESH
ANY = pl.BlockSpec(memory_space=pl.ANY)

HEADS = 8
DH = 128
CH = 64
GW = HEADS * DH
EPS = 1e-6
VMEM_V7X = 64 * 1024 * 1024

QB, KB, VB, ZB, BAB = 0, 8, 16, 24, 32
A_LANE = 120
NG, NC = 33, 32
GW_COLS, CW_COLS = NG * DH, NC * DH

SHARD_W = 2052
ALIGNED_BLOCKS = 17
ALIGNED_W = ALIGNED_BLOCKS * DH
SHIFTS = (0, 4, ALIGNED_W - 8, ALIGNED_W - 4)
G_EDGE, C_EDGE = 34, 32
WG_BLOCKS, WC_BLOCKS = 38, 34
G_MIXED, C_MIXED = (2, BAB), (4 * 7 + 1,)


def _shard_blocks(chip, edges):
    g, c = "g", "c"
    if chip == 0:
        out = [(g, 3 * b) for b in range(8)] + [(g, 3 * b + 1) for b in range(8)] + [(g, G_EDGE, G_MIXED[0])]
    elif chip == 1:
        out = [(g, G_EDGE + 1, G_MIXED[0])] + [(g, 3 * b + 2) for b in range(1, 8)]
        out += [(g, ZB + b) for b in range(8)] + [(g, G_EDGE + 2, G_MIXED[1])]
    elif chip == 2:
        out = [(c, 4 * b) for b in range(8)] + [(c, 4 * b + 1) for b in range(7)]
        out += [(c, C_EDGE, C_MIXED[0]), (g, G_EDGE + 3, G_MIXED[1])]
    else:
        out = [(c, 4 * b + 2) for b in range(8)] + [(c, 4 * b + 3) for b in range(8)] + [(c, C_EDGE + 1, C_MIXED[0])]
    return [(o[0], o[1] if (edges or len(o) == 2) else o[2]) for o in out]


def _by_chip(chip, vals):
    if all(v == vals[0] for v in vals):
        return vals[0]
    r = vals[3]
    for kk in (2, 1, 0):
        r = jnp.where(chip == kk, vals[kk], r)
    return r

ADAM_LR, ADAM_B1, ADAM_B2, ADAM_EPS, ADAM_WD, ADAM_STEP = 0.001, 0.9, 0.999, 1e-08, 0.01, 10

R_NIN, R_CB, R_FN, R_AD, R_GN, R_CQ, R_CW, R_LOSS, PACK_ROWS = 0, 1, 2, 3, 4, 5, 17, 20, 24

NN = ((1,), (0,))
NT = ((1,), (1,))
TN = ((0,), (0,))


def _dot(a, b, dims=NN, mode="lo"):
    dn = (dims, ((), ()))
    if mode == "hi":
        return lax.dot_general(a, b, dn, precision=lax.Precision.HIGHEST, preferred_element_type=F32)
    ah, bh = a.astype(BF16), b.astype(BF16)
    out = lax.dot_general(ah, bh, dn, preferred_element_type=F32)
    if mode == "x3":
        al = (a - ah.astype(F32)).astype(BF16)
        bl = (b - bh.astype(F32)).astype(BF16)
        out = out + lax.dot_general(ah, bl, dn, preferred_element_type=F32)
        out = out + lax.dot_general(al, bh, dn, preferred_element_type=F32)
    return out


P_GRAM, P_INV, P_SOL, P_SCAN, P_SCANB, P_BWD = "lo", "lo", "lo", "lo", "lo", "lo"
P_CUM = "x3"


def _params(sem=None, vmem=None):
    kw = {}
    if sem is not None:
        kw["dimension_semantics"] = sem
    if vmem is not None:
        kw["vmem_limit_bytes"] = int(min(max(vmem, 32 * 2**20), VMEM_V7X - 8 * 2**20))
    return pltpu.CompilerParams(**kw)


def _sigmoid(x):
    return 1.0 / (1.0 + jnp.exp(-x))


def _dsilu(x, s):
    return s * (1.0 + x * (1.0 - s))


def _rows(shape):
    return lax.broadcasted_iota(jnp.int32, shape, 0)


def _shift_down(x, s):
    if s == 0:
        return x
    return jnp.where(_rows(x.shape) >= s, pltpu.roll(x, s, 0), 0.0)


def _shift_up(x, s):
    if s == 0:
        return x
    n = x.shape[0]
    return jnp.where(_rows(x.shape) < n - s, pltpu.roll(x, n - s, 0), 0.0)


def _matmul(a, b, dims, out_dtype, tm, tn, tk, name, add=None, n=None):
    if dims == NN:
        (m, k), n = a.shape, b.shape[1]
    elif dims == NT:
        (m, k), n = a.shape, (n or b.shape[0])
    else:
        (k, m), n = a.shape, b.shape[1]
    tm, tn, tk = min(tm, m), min(tn, n), min(tk, k)
    assert m % tm == 0 and n % tn == 0 and k % tk == 0, (name, m, n, k, tm, tn, tk)
    nk = k // tk

    def body(*refs):
        if add is None:
            a_ref, b_ref, o_ref = refs[:3]
            add_ref = None
        else:
            a_ref, b_ref, add_ref, o_ref = refs[:4]
        part = _dot(a_ref[...], b_ref[...], dims)
        if nk == 1:
            if add_ref is not None:
                part = part + add_ref[...]
            o_ref[...] = part.astype(out_dtype)
            return
        acc = refs[-1]
        kk = pl.program_id(2)

        @pl.when(kk == 0)
        def _():
            acc[...] = part

        @pl.when(kk > 0)
        def _():
            acc[...] += part

        @pl.when(kk == nk - 1)
        def _():
            r = acc[...]
            if add_ref is not None:
                r = r + add_ref[...]
            o_ref[...] = r.astype(out_dtype)

    if dims == TN:
        a_spec = pl.BlockSpec((tk, tm), lambda i, j, kk: (kk, i))
    else:
        a_spec = pl.BlockSpec((tm, tk), lambda i, j, kk: (i, kk))
    if dims == NT:
        b_spec = pl.BlockSpec((tn, tk), lambda i, j, kk: (j, kk))
    else:
        b_spec = pl.BlockSpec((tk, tn), lambda i, j, kk: (kk, j))
    o_spec = pl.BlockSpec((tm, tn), lambda i, j, kk: (i, j))
    in_specs = [a_spec, b_spec]
    args = [a, b]
    if add is not None:
        in_specs.append(o_spec)
        args.append(add)
    osz = jnp.dtype(out_dtype).itemsize
    est = 2 * (tm * tk * a.dtype.itemsize + tk * tn * b.dtype.itemsize + tm * tn * osz)
    est += 3 * tm * tn * 4 + (2 * tm * tn * 4 if add is not None else 0)
    return pl.pallas_call(
        body, name=name, grid=(m // tm, n // tn, nk),
        in_specs=in_specs, out_specs=o_spec,
        out_shape=jax.ShapeDtypeStruct((m, n), out_dtype),
        scratch_shapes=[pltpu.VMEM((tm, tn), F32)] if nk > 1 else [],
        compiler_params=_params(("parallel", "parallel", "arbitrary"), est + 8 * 2**20),
    )(*args)


def _cast_bf16(a, rows, name):
    r, c = a.shape
    rows = min(rows, r)

    def body(a_ref, o_ref):
        o_ref[...] = a_ref[...].astype(BF16)

    return pl.pallas_call(
        body, name=name, grid=(r // rows,),
        in_specs=[pl.BlockSpec((rows, c), lambda i: (i, 0))],
        out_specs=pl.BlockSpec((rows, c), lambda i: (i, 0)),
        out_shape=jax.ShapeDtypeStruct((r, c), BF16),
        compiler_params=_params(("parallel",)),
    )(a)


def _align_shard(wt):
    r, d = wt.shape
    cols = min(256, d)

    def body(w_ref, o_ref, pad_ref):
        chip = 2 * lax.axis_index("x") + lax.axis_index("y")
        pad_ref[...] = jnp.zeros_like(pad_ref)
        pad_ref[0:r, :] = w_ref[...]
        o_ref[...] = pltpu.roll(pad_ref[...], _by_chip(chip, SHIFTS), 0).astype(BF16)

    return pl.pallas_call(
        body, name="align_shard", grid=(d // cols,),
        in_specs=[pl.BlockSpec((r, cols), lambda i: (0, i))],
        out_specs=pl.BlockSpec((ALIGNED_W, cols), lambda i: (0, i)),
        out_shape=jax.ShapeDtypeStruct((ALIGNED_W, d), BF16),
        scratch_shapes=[pltpu.VMEM((ALIGNED_W, cols), F32)],
        compiler_params=_params(("parallel",)),
    )(wt)


def _rms_in(x, w):
    n, d = x.shape
    tr = min(256, n)

    def body(x_ref, w_ref, h_ref):
        xv = x_ref[...]
        r = lax.rsqrt(jnp.mean(xv * xv, axis=-1, keepdims=True) + EPS)
        h_ref[...] = (xv * r * w_ref[...]).astype(BF16)

    return pl.pallas_call(
        body, name="rms_in", grid=(n // tr,),
        in_specs=[pl.BlockSpec((tr, d), lambda i: (i, 0)), pl.BlockSpec((1, d), lambda i: (0, 0))],
        out_specs=pl.BlockSpec((tr, d), lambda i: (i, 0)),
        out_shape=jax.ShapeDtypeStruct((n, d), BF16),
        compiler_params=_params(("parallel",)),
    )(x, w)


def _conv_silu(p, w_ref, taps):
    c = None
    for j in range(taps):
        t = _shift_down(p, taps - 1 - j) * w_ref[j:j + 1, :]
        c = t if c is None else c + t
    return c


def _prep_qkv(proj, cw):
    n = proj.shape[0]

    def body(p3, wq, wk, wv, q_ref, k_ref, v_ref):
        for kind, (w_ref, o_ref) in enumerate(((wq, q_ref), (wk, k_ref), (wv, v_ref))):
            c = _conv_silu(p3[:, kind * DH:(kind + 1) * DH], w_ref, 4)
            a = c * _sigmoid(c)
            if kind < 2:
                r = lax.rsqrt(jnp.sum(a * a, axis=-1, keepdims=True) + EPS)
                a = a * (r * (DH ** -0.5 if kind == 0 else 1.0))
            o_ref[...] = a

    col = pl.BlockSpec((n, DH), lambda h: (0, h))
    wcol = lambda base: pl.BlockSpec((4, DH), lambda h: (0, base + h))
    out = jax.ShapeDtypeStruct((n, GW), F32)
    return pl.pallas_call(
        body, name="prep_qkv", grid=(HEADS,),
        in_specs=[pl.BlockSpec((n, 3 * DH), lambda h: (0, h)), wcol(QB), wcol(KB), wcol(VB)],
        out_specs=[col] * 3, out_shape=[out] * 3,
        compiler_params=_params(("parallel",), 40 * 2**20),
    )(proj, cw, cw, cw)


def _prep_qkv_bwd(proj, cw, dq, dk, dv, dproj):
    n = proj.shape[0]

    def body(p3, wq, wk, wv, dq_ref, dk_ref, dv_ref, _, o3, gq, gk, gv):
        for kind, (w_ref, d_ref, g_ref) in enumerate(((wq, dq_ref, gq), (wk, dk_ref, gk), (wv, dv_ref, gv))):
            p = p3[:, kind * DH:(kind + 1) * DH]
            c = _conv_silu(p, w_ref, 4)
            s = _sigmoid(c)
            a = c * s
            d = d_ref[...]
            if kind < 2:
                r = lax.rsqrt(jnp.sum(a * a, axis=-1, keepdims=True) + EPS)
                sc = DH ** -0.5 if kind == 0 else 1.0
                d = (sc * r) * (d - a * ((r * r) * jnp.sum(d * a, axis=-1, keepdims=True)))
            dc = d * _dsilu(c, s)
            dp = None
            for j in range(4):
                g_ref[j:j + 1, :] = jnp.sum(dc * _shift_down(p, 3 - j), axis=0, keepdims=True)
                t = _shift_up(dc, 3 - j) * w_ref[j:j + 1, :]
                dp = t if dp is None else dp + t
            o3[:, kind * DH:(kind + 1) * DH] = dp.astype(BF16)

    col = pl.BlockSpec((n, DH), lambda h: (0, h))
    wcol = lambda base: pl.BlockSpec((4, DH), lambda h: (0, base + h))
    p3spec = pl.BlockSpec((n, 3 * DH), lambda h: (0, h))
    return pl.pallas_call(
        body, name="prep_qkv_bwd", grid=(HEADS,),
        in_specs=[p3spec, wcol(QB), wcol(KB), wcol(VB), col, col, col, ANY],
        out_specs=[p3spec] + [wcol(0)] * 3,
        out_shape=[jax.ShapeDtypeStruct(dproj.shape, BF16)] + [jax.ShapeDtypeStruct((4, GW), F32)] * 3,
        input_output_aliases={7: 0},
        compiler_params=_params(("parallel",), 48 * 2**20),
    )(proj, cw, cw, cw, dq, dk, dv, dproj)


def _tri(lower_incl):
    i = lax.broadcasted_iota(jnp.int32, (CH, CH), 0)
    j = lax.broadcasted_iota(jnp.int32, (CH, CH), 1)
    return jnp.where(i >= j, 1.0, 0.0) if lower_incl else jnp.where(j >= i, 1.0, 0.0)


def _lane(shape):
    return lax.broadcasted_iota(jnp.int32, shape, 1)


def _prep_bg(proj, ad):
    n = proj.shape[0]
    nch = n // CH

    def body(p_ref, ad_ref, bg_ref, bgt_ref):
        p = p_ref[...]
        lane = _lane(p.shape)
        beta = _sigmoid(p)
        xa = p + ad_ref[1:2, :]
        sp = jnp.maximum(xa, 0.0) + jnp.log(1.0 + jnp.exp(-jnp.abs(xa)))
        g = pltpu.roll(-jnp.exp(ad_ref[0:1, :]) * sp, DH - A_LANE + HEADS, 1)
        gc = _dot(_tri(True), g, NN, P_CUM)
        bg = jnp.where(lane < HEADS, beta, jnp.where(lane < 2 * HEADS, gc, 0.0))
        bg_ref[...] = bg
        bgt_ref[0] = bg.T

    return pl.pallas_call(
        body, name="prep_bg", grid=(nch,),
        in_specs=[pl.BlockSpec((CH, DH), lambda i: (i, BAB)), pl.BlockSpec((2, DH), lambda i: (0, 0))],
        out_specs=[pl.BlockSpec((CH, DH), lambda i: (i, 0)), pl.BlockSpec((1, DH, CH), lambda i: (i, 0, 0))],
        out_shape=[jax.ShapeDtypeStruct((n, DH), F32), jax.ShapeDtypeStruct((nch, DH, CH), F32)],
        compiler_params=_params(("parallel",)),
    )(proj, ad)


def _prep_bg_bwd(proj, ad, dbg, dproj):
    n = proj.shape[0]
    nch = n // CH

    def body(p_ref, ad_ref, d_ref, _, o_ref, ga_ref, gd_ref):
        p = p_ref[...]
        d = d_ref[...]
        lane = _lane(p.shape)
        beta = _sigmoid(p)
        xa = p + ad_ref[1:2, :]
        sp = jnp.maximum(xa, 0.0) + jnp.log(1.0 + jnp.exp(-jnp.abs(xa)))
        na = -jnp.exp(ad_ref[0:1, :])
        dg = pltpu.roll(_dot(_tri(False), d, NN, P_CUM), A_LANE - HEADS, 1)
        da = dg * na * _sigmoid(xa)
        is_g = lane >= A_LANE
        o_ref[...] = jnp.where(lane < HEADS, d * beta * (1.0 - beta), jnp.where(is_g, da, 0.0)).astype(BF16)
        ga = jnp.sum(jnp.where(is_g, dg * na * sp, 0.0), axis=0, keepdims=True)
        gd = jnp.sum(jnp.where(is_g, da, 0.0), axis=0, keepdims=True)

        @pl.when(pl.program_id(0) == 0)
        def _():
            ga_ref[...] = jnp.zeros_like(ga_ref)
            gd_ref[...] = jnp.zeros_like(gd_ref)

        ga_ref[...] += ga
        gd_ref[...] += gd

    one = pl.BlockSpec((1, DH), lambda i: (0, 0))
    return pl.pallas_call(
        body, name="prep_bg_bwd", grid=(nch,),
        in_specs=[pl.BlockSpec((CH, DH), lambda i: (i, BAB)), pl.BlockSpec((2, DH), lambda i: (0, 0)),
                  pl.BlockSpec((CH, DH), lambda i: (i, 0)), ANY],
        out_specs=[pl.BlockSpec((CH, DH), lambda i: (i, BAB)), one, one],
        out_shape=[jax.ShapeDtypeStruct(dproj.shape, BF16), jax.ShapeDtypeStruct((1, DH), F32),
                   jax.ShapeDtypeStruct((1, DH), F32)],
        input_output_aliases={3: 0},
        compiler_params=_params(("arbitrary",)),
    )(proj, ad, dbg, dproj)


def _gdn_out(o, proj, wg):
    n = o.shape[0]

    def body(o_ref, z_ref, w_ref, y_ref):
        ov, z = o_ref[...], z_ref[...]
        r = lax.rsqrt(jnp.mean(ov * ov, axis=-1, keepdims=True) + EPS)
        y_ref[...] = (ov * r * w_ref[...] * (z * _sigmoid(z))).astype(BF16)

    return pl.pallas_call(
        body, name="gdn_out", grid=(HEADS,),
        in_specs=[pl.BlockSpec((n, DH), lambda h: (0, h)), pl.BlockSpec((n, DH), lambda h: (0, ZB + h)),
                  pl.BlockSpec((1, DH), lambda h: (0, 0))],
        out_specs=pl.BlockSpec((n, DH), lambda h: (0, h)),
        out_shape=jax.ShapeDtypeStruct((n, 2 * GW), BF16),
        compiler_params=_params(("parallel",)),
    )(o, proj, wg)


def _gdn_out_bwd(o, proj, wg, dmix):
    n = o.shape[0]

    def body(o_ref, z_ref, w_ref, d_ref, do_ref, dz_ref, gw_ref):
        ov, z, d, w = o_ref[...], z_ref[...], d_ref[...], w_ref[...]
        r = lax.rsqrt(jnp.mean(ov * ov, axis=-1, keepdims=True) + EPS)
        nrm = ov * r
        s = _sigmoid(z)
        dz_ref[...] = (d * (nrm * w) * _dsilu(z, s)).astype(BF16)
        dn_w = d * (z * s)
        gw = jnp.sum(dn_w * nrm, axis=0, keepdims=True)
        dn = dn_w * w
        do_ref[...] = r * (dn - nrm * jnp.mean(dn * nrm, axis=-1, keepdims=True))

        @pl.when(pl.program_id(0) == 0)
        def _():
            gw_ref[...] = jnp.zeros_like(gw_ref)

        gw_ref[...] += gw

    return pl.pallas_call(
        body, name="gdn_out_bwd", grid=(HEADS,),
        in_specs=[pl.BlockSpec((n, DH), lambda h: (0, h)), pl.BlockSpec((n, DH), lambda h: (0, ZB + h)),
                  pl.BlockSpec((1, DH), lambda h: (0, 0)), pl.BlockSpec((n, DH), lambda h: (0, h))],
        out_specs=[pl.BlockSpec((n, DH), lambda h: (0, h)), pl.BlockSpec((n, DH), lambda h: (0, ZB + h)),
                   pl.BlockSpec((1, DH), lambda h: (0, 0))],
        out_shape=[jax.ShapeDtypeStruct((n, GW), F32), jax.ShapeDtypeStruct((n, GW_COLS), BF16),
                   jax.ShapeDtypeStruct((1, DH), F32)],
        compiler_params=_params(("arbitrary",)),
    )(o, proj, wg, dmix)


def _conv_branch(proj, w3, b, mix):
    n = proj.shape[0]

    def body(p4, w_ref, b_ref, _, y_ref):
        u = p4[:, DH:2 * DH] * p4[:, 2 * DH:3 * DH]
        cc = _conv_silu(u, w_ref, 3) + b_ref[...]
        z = p4[:, 3 * DH:4 * DH]
        y_ref[...] = (p4[:, 0:DH] * cc * (z * _sigmoid(z))).astype(BF16)

    return pl.pallas_call(
        body, name="conv_branch", grid=(HEADS,),
        in_specs=[pl.BlockSpec((n, 4 * DH), lambda h: (0, h)), pl.BlockSpec((3, DH), lambda h: (0, h)),
                  pl.BlockSpec((1, DH), lambda h: (0, h)), ANY],
        out_specs=pl.BlockSpec((n, DH), lambda h: (0, HEADS + h)),
        out_shape=jax.ShapeDtypeStruct(mix.shape, BF16),
        input_output_aliases={3: 0},
        compiler_params=_params(("parallel",), 40 * 2**20),
    )(proj, w3, b, mix)


def _conv_branch_bwd(proj, w3, b, dmix):
    n = proj.shape[0]

    def body(p4, w_ref, b_ref, d_ref, o4, gw_ref, gbias_ref):
        gb, gcv, hc, z = p4[:, 0:DH], p4[:, DH:2 * DH], p4[:, 2 * DH:3 * DH], p4[:, 3 * DH:4 * DH]
        d = d_ref[...]
        dgb, dgc, dhc, dzc = (o4.at[:, kk * DH:(kk + 1) * DH] for kk in range(4))
        u = gcv * hc
        cc = _conv_silu(u, w_ref, 3) + b_ref[...]
        s = _sigmoid(z)
        dzc[...] = (d * (gb * cc) * _dsilu(z, s)).astype(BF16)
        dp = d * (z * s)
        dgb[...] = (dp * cc).astype(BF16)
        dcc = dp * gb
        gbias_ref[...] = jnp.sum(dcc, axis=0, keepdims=True)
        du = None
        for j in range(3):
            gw_ref[j:j + 1, :] = jnp.sum(dcc * _shift_down(u, 2 - j), axis=0, keepdims=True)
            t = _shift_up(dcc, 2 - j) * w_ref[j:j + 1, :]
            du = t if du is None else du + t
        dgc[...] = (du * hc).astype(BF16)
        dhc[...] = (du * gcv).astype(BF16)

    p4spec = pl.BlockSpec((n, 4 * DH), lambda h: (0, h))
    return pl.pallas_call(
        body, name="conv_branch_bwd", grid=(HEADS,),
        in_specs=[p4spec, pl.BlockSpec((3, DH), lambda h: (0, h)), pl.BlockSpec((1, DH), lambda h: (0, h)),
                  pl.BlockSpec((n, DH), lambda h: (0, HEADS + h))],
        out_specs=[p4spec, pl.BlockSpec((3, DH), lambda h: (0, h)), pl.BlockSpec((1, DH), lambda h: (0, h))],
        out_shape=[jax.ShapeDtypeStruct((n, CW_COLS), BF16), jax.ShapeDtypeStruct((3, GW), F32),
                   jax.ShapeDtypeStruct((1, GW), F32)],
        compiler_params=_params(("parallel",), 48 * 2**20),
    )(proj, w3, b, dmix)


def _final_loss(out, tgt, wf):
    n, d = out.shape
    tr = min(256, n)

    def body(o_ref, t_ref, w_ref, do_ref, dob_ref, gw_ref, loss_ref):
        ov, w = o_ref[...], w_ref[...]
        r = lax.rsqrt(jnp.mean(ov * ov, axis=-1, keepdims=True) + EPS)
        nrm = ov * r
        e = nrm * w - t_ref[...]
        dy = e * (1.0 / d)
        dn = dy * w
        dout = r * (dn - nrm * jnp.mean(dn * nrm, axis=-1, keepdims=True))
        do_ref[...] = dout
        dob_ref[...] = dout.astype(BF16)

        @pl.when(pl.program_id(0) == 0)
        def _():
            gw_ref[...] = jnp.zeros_like(gw_ref)
            loss_ref[...] = jnp.zeros_like(loss_ref)

        gw_ref[...] += jnp.sum(dy * nrm, axis=0, keepdims=True)
        loss_ref[...] += (0.5 / d) * jnp.sum(jnp.sum(e * e, axis=-1, keepdims=True), axis=0, keepdims=True)

    row = pl.BlockSpec((tr, d), lambda i: (i, 0))
    return pl.pallas_call(
        body, name="final_loss", grid=(n // tr,),
        in_specs=[row, row, pl.BlockSpec((1, d), lambda i: (0, 0))],
        out_specs=[row, row, pl.BlockSpec((1, d), lambda i: (0, 0)), pl.BlockSpec((1, 1), lambda i: (0, 0))],
        out_shape=[jax.ShapeDtypeStruct((n, d), F32), jax.ShapeDtypeStruct((n, d), BF16),
                   jax.ShapeDtypeStruct((1, d), F32), jax.ShapeDtypeStruct((1, 1), F32)],
        compiler_params=_params(("arbitrary",)),
    )(out, tgt, wf)


def _rms_in_bwd(x, w, dh, dout):
    n, d = x.shape
    tr = min(256, n)

    def body(x_ref, w_ref, dh_ref, do_ref, dx_ref, gw_ref):
        xv, dhv = x_ref[...], dh_ref[...]
        r = lax.rsqrt(jnp.mean(xv * xv, axis=-1, keepdims=True) + EPS)
        xn = xv * r
        dxn = dhv * w_ref[...]
        dx_ref[...] = r * (dxn - xn * jnp.mean(dxn * xn, axis=-1, keepdims=True)) + do_ref[...]

        @pl.when(pl.program_id(0) == 0)
        def _():
            gw_ref[...] = jnp.zeros_like(gw_ref)

        gw_ref[...] += jnp.sum(dhv * xn, axis=0, keepdims=True)

    row = pl.BlockSpec((tr, d), lambda i: (i, 0))
    one = pl.BlockSpec((1, d), lambda i: (0, 0))
    return pl.pallas_call(
        body, name="rms_in_bwd", grid=(n // tr,),
        in_specs=[row, one, row, row], out_specs=[row, one],
        out_shape=[jax.ShapeDtypeStruct((n, d), F32), jax.ShapeDtypeStruct((1, d), F32)],
        compiler_params=_params(("arbitrary",)),
    )(x, w, dh, dout)


def _ij():
    i = lax.broadcasted_iota(jnp.int32, (CH, CH), 0)
    j = lax.broadcasted_iota(jnp.int32, (CH, CH), 1)
    return i, j


def _unit_lower_inverse(mats):
    i, j = _ij()
    eye = jnp.where(i == j, 1.0, 0.0)
    same16 = (i // 16) == (j // 16)
    same32 = (i // 32) == (j // 32)
    mm = lambda xs, ys: [_dot(x, y, NN, P_INV) for x, y in zip(xs, ys)]
    n1 = [jnp.where(same16, -a, 0.0) for a in mats]
    n2 = mm(n1, n1)
    n4 = mm(n2, n2)
    n8 = mm(n4, n4)
    t = [eye + x1 + x2 + x3 for x1, x2, x3 in zip(n1, n2, mm(n1, n2))]
    t = [x + y for x, y in zip(t, mm(t, n4))]
    t = [x + y for x, y in zip(t, mm(t, n8))]
    a1 = [jnp.where(same32 & jnp.logical_not(same16), a, 0.0) for a in mats]
    t = [x - y for x, y in zip(t, mm(t, mm(a1, t)))]
    a2 = [jnp.where(same32, 0.0, a) for a in mats]
    t = [x - y for x, y in zip(t, mm(t, mm(a2, t)))]
    return t


def _head_vectors(bg, bgt, h):
    bcol = bg[:, h:h + 1]
    gcol = bg[:, HEADS + h:HEADS + h + 1]
    grow = bgt[HEADS + h:HEADS + h + 1, :]
    return bcol, gcol, grow


def _decay(gcol, grow):
    i, j = _ij()
    return jnp.where(i >= j, jnp.exp(jnp.where(i >= j, gcol - grow, 0.0)), 0.0)


def _gdn_intra(q, k, v, bg, bgt):
    n = q.shape[0]
    nch = n // CH

    def body(q_ref, k_ref, v_ref, bg_ref, bgt_ref, u_ref, w_ref, p_ref, t_ref):
        bg, bgt = bg_ref[...], bgt_ref[0]
        i, j = _ij()
        sls = [slice(h * DH, (h + 1) * DH) for h in range(HEADS)]
        ks = [k_ref[:, sl] for sl in sls]
        vecs = [_head_vectors(bg, bgt, h) for h in range(HEADS)]
        decs = [_decay(gcol, grow) for _, gcol, grow in vecs]
        kks = [_dot(kh, kh, NT, P_GRAM) for kh in ks]
        qks = [_dot(q_ref[:, sl], kh, NT, P_GRAM) for sl, kh in zip(sls, ks)]
        ts = _unit_lower_inverse([jnp.where(i > j, bcol * kk * dec, 0.0)
                                  for (bcol, _, _), kk, dec in zip(vecs, kks, decs)])
        us = [_dot(t, v_ref[:, sl] * bcol, NN, P_SOL) for t, sl, (bcol, _, _) in zip(ts, sls, vecs)]
        ws = [_dot(t, kh * (bcol * jnp.exp(gcol)), NN, P_SOL) for t, kh, (bcol, gcol, _) in zip(ts, ks, vecs)]
        for h, sl in enumerate(sls):
            p_ref[0, h] = qks[h] * decs[h]
            t_ref[0, h] = ts[h]
            u_ref[:, sl] = us[h]
            w_ref[:, sl] = ws[h]

    row = pl.BlockSpec((CH, GW), lambda c: (c, 0))
    sq = pl.BlockSpec((1, HEADS, CH, CH), lambda c: (c, 0, 0, 0))
    big = jax.ShapeDtypeStruct((n, GW), F32)
    sqs = jax.ShapeDtypeStruct((nch, HEADS, CH, CH), F32)
    return pl.pallas_call(
        body, name="gdn_intra", grid=(nch,),
        in_specs=[row, row, row, pl.BlockSpec((CH, DH), lambda c: (c, 0)),
                  pl.BlockSpec((1, DH, CH), lambda c: (c, 0, 0))],
        out_specs=[row, row, sq, sq], out_shape=[big, big, sqs, sqs],
        compiler_params=_params(("parallel",)),
    )(q, k, v, bg, bgt)


def _gdn_scan(q, k, bg, u, w, p):
    n = q.shape[0]
    nch = n // CH

    def body(q_ref, k_ref, bg_ref, u_ref, w_ref, p_ref, o_ref, vn_ref, s_out, s_scr):
        @pl.when(pl.program_id(0) == 0)
        def _():
            s_scr[...] = jnp.zeros_like(s_scr)

        bg = bg_ref[...]
        hs = range(HEADS)
        sls = [slice(h * DH, (h + 1) * DH) for h in hs]
        gcols = [bg[:, HEADS + h:HEADS + h + 1] for h in hs]
        glasts = [g[CH - 1:CH, :] for g in gcols]
        ss = [s_scr[h] for h in hs]
        wss = [_dot(w_ref[:, sl], s, NN, P_SCAN) for sl, s in zip(sls, ss)]
        oqs = [_dot(q_ref[:, sl] * jnp.exp(g), s, NN, P_SCAN) for sl, s, g in zip(sls, ss, gcols)]
        vns = [u_ref[:, sl] - x for sl, x in zip(sls, wss)]
        ops = [_dot(p_ref[0, h], vn, NN, P_SCAN) for h, vn in zip(hs, vns)]
        sns = [_dot(k_ref[:, sl] * jnp.exp(gl - g), vn, TN, P_SCAN)
               for sl, gl, g, vn in zip(sls, glasts, gcols, vns)]
        for h, sl in enumerate(sls):
            s_out[0, :, sl] = ss[h]
            vn_ref[:, sl] = vns[h]
            o_ref[:, sl] = oqs[h] + ops[h]
            s_scr[h] = ss[h] * jnp.exp(glasts[h]) + sns[h]

    row = pl.BlockSpec((CH, GW), lambda c: (c, 0))
    big = jax.ShapeDtypeStruct((n, GW), F32)
    return pl.pallas_call(
        body, name="gdn_scan", grid=(nch,),
        in_specs=[row, row, pl.BlockSpec((CH, DH), lambda c: (c, 0)), row, row,
                  pl.BlockSpec((1, HEADS, CH, CH), lambda c: (c, 0, 0, 0))],
        out_specs=[row, row, pl.BlockSpec((1, DH, GW), lambda c: (c, 0, 0))],
        out_shape=[big, big, jax.ShapeDtypeStruct((nch, DH, GW), F32)],
        scratch_shapes=[pltpu.VMEM((HEADS, DH, DH), F32)],
        compiler_params=_params(("arbitrary",)),
    )(q, k, bg, u, w, p)


def _gdn_scan_bwd(q, k, bg, w, p, vn, s_in, do):
    n = q.shape[0]
    nch = n // CH
    rev = lambda c: nch - 1 - c

    def body(q_ref, k_ref, bg_ref, w_ref, p_ref, vn_ref, s_ref, do_ref,
             dqg_ref, dp_ref, du_ref, dw_ref, dks_ref, dgam_ref, ds_scr):
        @pl.when(pl.program_id(0) == 0)
        def _():
            ds_scr[...] = jnp.zeros_like(ds_scr)

        bg = bg_ref[...]
        lane = _lane((1, DH))
        hs = range(HEADS)
        sls = [slice(h * DH, (h + 1) * DH) for h in hs]
        gcols = [bg[:, HEADS + h:HEADS + h + 1] for h in hs]
        glasts = [g[CH - 1:CH, :] for g in gcols]
        ss = [s_ref[0, :, sl] for sl in sls]
        dss = [ds_scr[h] for h in hs]
        dos = [do_ref[:, sl] for sl in sls]
        vnl = [vn_ref[:, sl] for sl in sls]
        dqgs = [_dot(d, s, NT, P_SCANB) for d, s in zip(dos, ss)]
        dps = [_dot(d, vn, NT, P_SCANB) for d, vn in zip(dos, vnl)]
        dvn1 = [_dot(p_ref[0, h], d, TN, P_SCANB) for h, d in zip(hs, dos)]
        dvn2 = [_dot(k_ref[:, sl] * jnp.exp(gl - g), ds, NN, P_SCANB)
                for sl, gl, g, ds in zip(sls, glasts, gcols, dss)]
        dkss = [_dot(vn, ds, NT, P_SCANB) for vn, ds in zip(vnl, dss)]
        dsq = [_dot(q_ref[:, sl] * jnp.exp(g), d, TN, P_SCANB) for sl, g, d in zip(sls, gcols, dos)]
        dvns = [a + b for a, b in zip(dvn1, dvn2)]
        dws = [_dot(dvn, s, NT, P_SCANB) for dvn, s in zip(dvns, ss)]
        dsw = [_dot(w_ref[:, sl], dvn, TN, P_SCANB) for sl, dvn in zip(sls, dvns)]
        dgam = jnp.zeros((1, DH), F32)
        for h, sl in enumerate(sls):
            dqg_ref[:, sl] = dqgs[h]
            dp_ref[0, h] = dps[h]
            du_ref[:, sl] = dvns[h]
            dw_ref[:, sl] = -dws[h]
            dks_ref[:, sl] = dkss[h]
            tot = jnp.sum(jnp.sum(dss[h] * ss[h], axis=-1, keepdims=True), axis=0, keepdims=True)
            dgam = dgam + jnp.where(lane == h, tot, 0.0)
            ds_scr[h] = dss[h] * jnp.exp(glasts[h]) + dsq[h] - dsw[h]
        dgam_ref[0] = jnp.broadcast_to(dgam, (8, DH))

    row = pl.BlockSpec((CH, GW), lambda c: (rev(c), 0))
    sq = pl.BlockSpec((1, HEADS, CH, CH), lambda c: (rev(c), 0, 0, 0))
    big = jax.ShapeDtypeStruct((n, GW), F32)
    return pl.pallas_call(
        body, name="gdn_scan_bwd", grid=(nch,),
        in_specs=[row, row, pl.BlockSpec((CH, DH), lambda c: (rev(c), 0)), row, sq, row,
                  pl.BlockSpec((1, DH, GW), lambda c: (rev(c), 0, 0)), row],
        out_specs=[row, sq, row, row, row, pl.BlockSpec((1, 8, DH), lambda c: (rev(c), 0, 0))],
        out_shape=[big, jax.ShapeDtypeStruct((nch, HEADS, CH, CH), F32), big, big, big,
                   jax.ShapeDtypeStruct((nch, 8, DH), F32)],
        scratch_shapes=[pltpu.VMEM((HEADS, DH, DH), F32)],
        compiler_params=_params(("arbitrary",)),
    )(q, k, bg, w, p, vn, s_in, do)


def _gdn_intra_bwd(q, k, v, bg, bgt, t, u, w, p, dqg, dp, du, dw, dks, dgam):
    n = q.shape[0]
    nch = n // CH

    def body(q_ref, k_ref, v_ref, bg_ref, bgt_ref, t_ref, u_ref, w_ref, p_ref,
             dqg_ref, dp_ref, du_ref, dw_ref, dks_ref, dgam_ref, dq_ref, dk_ref, dv_ref, dbg_ref):
        bg, bgt = bg_ref[...], bgt_ref[0]
        dgam_all = dgam_ref[0]
        i, j = _ij()
        rows1 = lax.broadcasted_iota(jnp.int32, (CH, 1), 0)
        lane = _lane((CH, DH))
        dbg = jnp.zeros((CH, DH), F32)
        rsum = lambda x: jnp.sum(x, axis=-1, keepdims=True)
        hs = range(HEADS)
        sls = [slice(h * DH, (h + 1) * DH) for h in hs]
        qs = [q_ref[:, sl] for sl in sls]
        ks = [k_ref[:, sl] for sl in sls]
        vecs = [_head_vectors(bg, bgt, h) for h in hs]
        decs = [_decay(gcol, grow) for _, gcol, grow in vecs]
        ths = [t_ref[0, h] for h in hs]
        drus = [_dot(th, du_ref[:, sl], TN, P_BWD) for th, sl in zip(ths, sls)]
        drws = [_dot(th, dw_ref[:, sl], TN, P_BWD) for th, sl in zip(ths, sls)]
        kks = [_dot(kh, kh, NT, P_GRAM) for kh in ks]
        da1 = [_dot(dru, u_ref[:, sl], NT, P_BWD) for dru, sl in zip(drus, sls)]
        da2 = [_dot(drw, w_ref[:, sl], NT, P_BWD) for drw, sl in zip(drws, sls)]
        das = [jnp.where(i > j, -(x + y), 0.0) for x, y in zip(da1, da2)]
        dkks = [da * bcol * dec for da, (bcol, _, _), dec in zip(das, vecs, decs)]
        dqks = [dp_ref[0, h] * dec for h, dec in zip(hs, decs)]
        dq_ps = [_dot(dqk, kh, NN, P_BWD) for dqk, kh in zip(dqks, ks)]
        dk_ps = [_dot(dqk, qh, TN, P_BWD) for dqk, qh in zip(dqks, qs)]
        dk_as = [_dot(dkk, kh, NN, P_BWD) for dkk, kh in zip(dkks, ks)]
        dk_bs = [_dot(dkk, kh, TN, P_BWD) for dkk, kh in zip(dkks, ks)]
        for h, sl in enumerate(sls):
            qh, kh, vh = qs[h], ks[h], v_ref[:, sl]
            bcol, gcol, _ = vecs[h]
            dec, dru, drw, da, kk = decs[h], drus[h], drws[h], das[h], kks[h]
            gam = jnp.exp(gcol)
            glast = gcol[CH - 1:CH, :]
            e = jnp.exp(glast - gcol)
            kg = kh * gam
            dv_ref[:, sl] = bcol * dru
            dbeta = rsum(dru * vh) + rsum(drw * kg) + rsum(da * kk * dec)
            dgc = rsum(drw * kg) * bcol
            dqg = dqg_ref[:, sl]
            dksh = dks_ref[:, sl]
            dq_ref[:, sl] = gam * dqg + dq_ps[h]
            dk_ref[:, sl] = (bcol * gam) * drw + dk_ps[h] + dk_as[h] + dk_bs[h] + dksh * e
            tk = rsum(dksh * kh) * e
            mdec = da * (bcol * kk * dec) + dp_ref[0, h] * p_ref[0, h]
            col = rsum(jnp.where(i == j, jnp.sum(mdec, axis=0, keepdims=True), 0.0))
            dgc = dgc + rsum(mdec) - col
            dgc = dgc + rsum(dqg * qh) * gam - tk
            dglast = jnp.sum(tk, axis=0, keepdims=True) + dgam_all[0:1, h:h + 1] * jnp.exp(glast)
            dgc = dgc + jnp.where(rows1 == CH - 1, dglast, 0.0)
            dbg = dbg + jnp.where(lane == h, dbeta, 0.0) + jnp.where(lane == HEADS + h, dgc, 0.0)
        dbg_ref[...] = dbg

    row = pl.BlockSpec((CH, GW), lambda c: (c, 0))
    sq = pl.BlockSpec((1, HEADS, CH, CH), lambda c: (c, 0, 0, 0))
    small = pl.BlockSpec((CH, DH), lambda c: (c, 0))
    big = jax.ShapeDtypeStruct((n, GW), F32)
    return pl.pallas_call(
        body, name="gdn_intra_bwd", grid=(nch,),
        in_specs=[row, row, row, small, pl.BlockSpec((1, DH, CH), lambda c: (c, 0, 0)), sq, row, row, sq,
                  row, sq, row, row, row, pl.BlockSpec((1, 8, DH), lambda c: (c, 0, 0))],
        out_specs=[row, row, row, small],
        out_shape=[big, big, big, jax.ShapeDtypeStruct((n, DH), F32)],
        compiler_params=_params(("parallel",)),
    )(q, k, v, bg, bgt, t, u, w, p, dqg, dp, du, dw, dks, dgam)


def _local_step(x, tgt, h, w_g, cqw, late, norm_in_w, ad, gdn_norm_w, conv_b, final_norm_w,
                on_grad_c=None, on_grad_g=None):
    proj_g = _matmul(h, w_g, NT, F32, 512, 1408, 1024, "mm_proj_g", n=GW_COLS)
    q, k, v = _prep_qkv(proj_g, cqw)
    bg, bgt = _prep_bg(proj_g, ad)
    u, w, p, t = _gdn_intra(q, k, v, bg, bgt)
    o, vn, s_in = _gdn_scan(q, k, bg, u, w, p)
    w_c, w_out, conv_w = late(o)
    proj_c = _matmul(h, w_c, NT, F32, 512, 1024, 1024, "mm_proj_c", n=CW_COLS)
    mix = _conv_branch(proj_c, conv_w, conv_b, _gdn_out(o, proj_g, gdn_norm_w))
    out = _matmul(mix, w_out, NN, F32, 512, 512, 2048, "mm_out", add=x)
    dout, dout_b, g_fn, loss = _final_loss(out, tgt, final_norm_w)

    dmix = _matmul(dout_b, w_out, NT, F32, 512, 1024, 1024, "mm_dmix")
    g_wout = _matmul(mix, dout_b, TN, BF16, 512, 512, 2048, "mm_gwout")
    do, dproj_g, g_gn = _gdn_out_bwd(o, proj_g, gdn_norm_w, dmix)
    dproj_c, g_cw, g_cb = _conv_branch_bwd(proj_c, conv_w, conv_b, dmix)
    g_c = _matmul(dproj_c, h, TN, BF16, 1024, 512, 2048, "mm_gwin_c")
    if on_grad_c is not None:
        do = on_grad_c(g_c, g_wout, do)
    dqg, dp, du, dw, dks, dgam = _gdn_scan_bwd(q, k, bg, w, p, vn, s_in, do)
    dq, dk, dv, dbg = _gdn_intra_bwd(q, k, v, bg, bgt, t, u, w, p, dqg, dp, du, dw, dks, dgam)
    dproj_g, gq, gk, gv = _prep_qkv_bwd(proj_g, cqw, dq, dk, dv, dproj_g)
    dproj_g, g_al, g_dt = _prep_bg_bwd(proj_g, ad, dbg, dproj_g)
    g_g = _matmul(dproj_g, h, TN, BF16, 1408, 512, 2048, "mm_gwin_g")
    if on_grad_g is not None:
        dproj_g = on_grad_g(g_g, dproj_g)
    dh = _matmul(dproj_g, w_g, NN, F32, 512, 1024, 1408, "mm_dh_g")
    dh = _matmul(dproj_c, w_c, NN, F32, 512, 1024, 1024, "mm_dh_c", add=dh)
    gx, g_nin = _rms_in_bwd(x, norm_in_w, dh, dout)
    small = dict(nin=g_nin, cb=g_cb, fn=g_fn, al=g_al, dt=g_dt, gn=g_gn, cq=(gq, gk, gv), cw=g_cw, loss=loss)
    return gx, small, (g_g, g_c, g_wout)


def _place():
    x, y, c = lax.axis_index("x"), lax.axis_index("y"), lax.axis_index("c")
    chips = [(1 - x, y), (x, 1 - y), (1 - x, 1 - y)]
    return x, y, c, chips


def _blk(ref, b):
    if isinstance(b, int):
        return ref.at[b * DH:(b + 1) * DH, :]
    return ref.at[pl.ds(pl.multiple_of(b * DH, DH), DH), :]


HBM = pl.BlockSpec(memory_space=pltpu.HBM)
SEM = pl.BlockSpec(memory_space=pltpu.SEMAPHORE)
EFFECT = pltpu.SideEffectType.DATAFLOW_SIDE_EFFECTING


def _split_start(name, issue, bufs, n_sems):
    nbuf = len(bufs)

    def body(*refs):
        issue(refs[:nbuf], refs[nbuf], refs[nbuf + 1])
        refs[-1][...] = jnp.zeros_like(refs[-1])

    out = pl.pallas_call(
        body, name=name,
        out_shape=(pltpu.SemaphoreType.DMA((n_sems,)), pltpu.SemaphoreType.DMA((n_sems,)),
                   *[pltpu.HBM(b.shape, b.dtype) for b in bufs], jax.ShapeDtypeStruct((8, DH), F32)),
        in_specs=[HBM] * nbuf,
        out_specs=(SEM, SEM, *[HBM] * nbuf, pl.BlockSpec(memory_space=pltpu.VMEM)),
        input_output_aliases={a: 2 + a for a in range(nbuf)},
        compiler_params=pltpu.CompilerParams(has_side_effects=EFFECT),
    )(*[pltpu.with_memory_space_constraint(b, pltpu.HBM) for b in bufs])
    return out[0], out[1], list(out[2:2 + nbuf]), out[-1]


def _split_wait(name, await_, send_sems, recv_sems, bufs, after):
    nbuf = len(bufs)

    def body(*refs):
        await_(refs[:nbuf], refs[nbuf], refs[nbuf + 1])

    out = pl.pallas_call(
        body, name=name,
        out_shape=tuple(pltpu.HBM(b.shape, b.dtype) for b in bufs),
        in_specs=[HBM] * nbuf + [SEM, SEM, ANY], out_specs=tuple([HBM] * nbuf),
        input_output_aliases={a: a for a in range(nbuf)},
        compiler_params=pltpu.CompilerParams(has_side_effects=EFFECT),
    )(*bufs, send_sems, recv_sems, after)
    return list(out)


def _phase_blocks(chip, phase, edges):
    return [(b, blk) for b, (grp, blk) in enumerate(_shard_blocks(chip, edges)) if grp == phase]


def _cols(ref, nblk):
    return ref.at[0:nblk * DH, :]


def _place_own(a_shard, wo, cq, cw):
    d = a_shard.shape[1]

    def body(a_ref, wo_ref, cq_ref, cw_ref, wg_ref, wc_ref, wog_ref, cqg_ref, cwg_ref, sems):
        x, y, _, _ = _place()
        mine = 2 * x + y
        small = [pltpu.make_async_copy(s_ref, g_ref.at[mine], sems.at[1 + a])
                 for a, (s_ref, g_ref) in enumerate(((wo_ref, wog_ref), (cq_ref, cqg_ref), (cw_ref, cwg_ref)))]
        for cp in small:
            cp.start()
        for s in range(4):
            @pl.when(mine == s)
            def _():
                cps = [pltpu.make_async_copy(_blk(a_ref, b), _blk(wg_ref if grp == "g" else wc_ref, blk), sems.at[0])
                       for b, (grp, blk) in enumerate(_shard_blocks(s, True))]
                for cp in cps:
                    cp.start()
                for cp in cps:
                    cp.wait()
        for cp in small:
            cp.wait()

    return pl.pallas_call(
        body, name="place_own", in_specs=[ANY] * 4, out_specs=[ANY] * 5,
        out_shape=[jax.ShapeDtypeStruct((WG_BLOCKS * DH, d), a_shard.dtype),
                   jax.ShapeDtypeStruct((WC_BLOCKS * DH, d), a_shard.dtype)]
        + [jax.ShapeDtypeStruct((4,) + s.shape, s.dtype) for s in (wo, cq, cw)],
        scratch_shapes=[pltpu.SemaphoreType.DMA((4,))],
    )(a_shard, wo, cq, cw)


def _gather_start(phase, a_shard, w_grp, singles):
    ns = len(singles)

    def issue(refs, send_sems, recv_sems):
        a_ref, w_ref = refs[0], refs[1]
        x, y, c, chips = _place()
        mine = 2 * x + y
        for jj, (px, py) in enumerate(chips):
            to = dict(device_id=(px, py, c), device_id_type=MESH)
            for a in range(ns):
                pltpu.make_async_remote_copy(
                    src_ref=refs[2 + 2 * a], dst_ref=refs[3 + 2 * a].at[mine],
                    send_sem=send_sems.at[(1 + ns) * jj + 1 + a], recv_sem=recv_sems.at[(1 + ns) * jj + 1 + a],
                    **to).start()
        for s in range(4):
            blocks = _phase_blocks(s, phase, True)
            if blocks:
                @pl.when(mine == s)
                def _():
                    for jj, (px, py) in enumerate(chips):
                        for b, blk in blocks:
                            pltpu.make_async_remote_copy(
                                src_ref=_blk(a_ref, b), dst_ref=_blk(w_ref, blk),
                                send_sem=send_sems.at[(1 + ns) * jj], recv_sem=recv_sems.at[(1 + ns) * jj],
                                device_id=(px, py, c), device_id_type=MESH).start()

    bufs = [a_shard, w_grp] + [t for pair in singles for t in pair]
    return _split_start("gather_start_" + phase, issue, bufs, 3 * (1 + ns))


def _gather_wait(phase, send_sems, recv_sems, bufs, after):
    ns = (len(bufs) - 2) // 2

    def await_(refs, send_sems, recv_sems):
        a_ref, w_ref = refs[0], refs[1]
        x, y, c, chips = _place()
        mine = 2 * x + y
        for jj, (px, py) in enumerate(chips):
            to = dict(device_id=(px, py, c), device_id_type=MESH)
            peer = 2 * px + py
            for a in range(ns):
                cp = pltpu.make_async_remote_copy(
                    src_ref=refs[2 + 2 * a], dst_ref=refs[3 + 2 * a].at[mine],
                    send_sem=send_sems.at[(1 + ns) * jj + 1 + a], recv_sem=recv_sems.at[(1 + ns) * jj + 1 + a], **to)
                cp.wait_recv()
                cp.wait_send()
            for s in range(4):
                nblk = len(_phase_blocks(s, phase, True))
                if nblk:
                    both = pltpu.make_async_remote_copy(
                        src_ref=_cols(a_ref, nblk), dst_ref=_cols(w_ref, nblk),
                        send_sem=send_sems.at[(1 + ns) * jj], recv_sem=recv_sems.at[(1 + ns) * jj], **to)

                    @pl.when(peer == s)
                    def _():
                        both.wait_recv()

                    @pl.when(mine == s)
                    def _():
                        both.wait_send()

    return _split_wait("gather_wait_" + phase, await_, send_sems, recv_sems, bufs, after)


def _merge_edges(w, edge0, mixed, name):
    d = w.shape[1]

    def body(e_ref, o_ref):
        o_ref[...] = e_ref[0:DH, :] + e_ref[DH:2 * DH, :]

    def to_block(i):
        r = mixed[-1]
        for kk in range(len(mixed) - 2, -1, -1):
            r = jnp.where(i == kk, mixed[kk], r)
        return r

    return pl.pallas_call(
        body, name=name, grid=(len(mixed),),
        in_specs=[pl.BlockSpec((2 * DH, d), lambda i: (edge0 // 2 + i, 0))],
        out_specs=pl.BlockSpec((DH, d), lambda i: (to_block(i), 0)),
        out_shape=jax.ShapeDtypeStruct(w.shape, w.dtype),
        input_output_aliases={0: 0},
        compiler_params=_params(("arbitrary",)),
    )(w)


def _scatter_start(phase, g_grp, land, singles):
    ns = len(singles)

    def issue(refs, send_sems, recv_sems):
        g_ref, land_ref = refs[0], refs[1]
        x, y, c, chips = _place()
        for jj, (px, py) in enumerate(chips):
            to = dict(device_id=(px, py, c), device_id_type=MESH)
            peer = 2 * px + py
            for a in range(ns):
                pltpu.make_async_remote_copy(
                    src_ref=refs[2 + 2 * a].at[peer], dst_ref=refs[3 + 2 * a].at[jj],
                    send_sem=send_sems.at[(1 + ns) * jj + 1 + a], recv_sem=recv_sems.at[(1 + ns) * jj + 1 + a],
                    **to).start()
            for s in range(4):
                blocks = _phase_blocks(s, phase, False)
                if blocks:
                    @pl.when(peer == s)
                    def _():
                        for b, blk in blocks:
                            pltpu.make_async_remote_copy(
                                src_ref=_blk(g_ref, blk), dst_ref=_blk(land_ref.at[jj], b),
                                send_sem=send_sems.at[(1 + ns) * jj], recv_sem=recv_sems.at[(1 + ns) * jj],
                                **to).start()

    bufs = [g_grp, land] + [t for pair in singles for t in pair]
    return _split_start("scatter_start_" + phase, issue, bufs, 3 * (1 + ns))


def _scatter_wait(phase, send_sems, recv_sems, bufs, after):
    ns = (len(bufs) - 2) // 2

    def await_(refs, send_sems, recv_sems):
        g_ref, land_ref = refs[0], refs[1]
        x, y, c, chips = _place()
        mine = 2 * x + y
        for jj, (px, py) in enumerate(chips):
            to = dict(device_id=(px, py, c), device_id_type=MESH)
            peer = 2 * px + py
            for a in range(ns):
                cp = pltpu.make_async_remote_copy(
                    src_ref=refs[2 + 2 * a].at[peer], dst_ref=refs[3 + 2 * a].at[jj],
                    send_sem=send_sems.at[(1 + ns) * jj + 1 + a], recv_sem=recv_sems.at[(1 + ns) * jj + 1 + a], **to)
                cp.wait_recv()
                cp.wait_send()
            for s in range(4):
                nblk = len(_phase_blocks(s, phase, False))
                if nblk:
                    both = pltpu.make_async_remote_copy(
                        src_ref=_cols(g_ref, nblk), dst_ref=_cols(land_ref.at[jj], nblk),
                        send_sem=send_sems.at[(1 + ns) * jj], recv_sem=recv_sems.at[(1 + ns) * jj], **to)

                    @pl.when(mine == s)
                    def _():
                        both.wait_recv()

                    @pl.when(peer == s)
                    def _():
                        both.wait_send()

    return _split_wait("scatter_wait_" + phase, await_, send_sems, recv_sems, bufs, after)


def _own_grad_blocks(g_g, g_c, g_out):
    d = g_g.shape[1]

    def body(gg_ref, gc_ref, go_ref, own_ref, owno_ref, sems):
        x, y, _, _ = _place()
        mine = 2 * x + y
        out = pltpu.make_async_copy(go_ref.at[mine], owno_ref, sems.at[1])
        out.start()
        for s in range(4):
            @pl.when(mine == s)
            def _():
                cps = [pltpu.make_async_copy(_blk(gg_ref if grp == "g" else gc_ref, blk), _blk(own_ref, b), sems.at[0])
                       for b, (grp, blk) in enumerate(_shard_blocks(s, False))]
                for cp in cps:
                    cp.start()
                for cp in cps:
                    cp.wait()
        out.wait()

    return pl.pallas_call(
        body, name="own_grad_blocks", in_specs=[ANY] * 3, out_specs=[ANY] * 2,
        out_shape=[jax.ShapeDtypeStruct((ALIGNED_W, d), g_g.dtype), jax.ShapeDtypeStruct(g_out.shape[1:], g_out.dtype)],
        scratch_shapes=[pltpu.SemaphoreType.DMA((2,))],
    )(g_g, g_c, g_out)


def _final_exchange(parts, pack):
    npart = len(parts)

    def body(*refs):
        ins, pack_ref = refs[:npart], refs[npart]
        outs, packs = refs[npart + 1:2 * npart + 1], refs[2 * npart + 1]
        send_sems, recv_sems, psend, precv, loc_sem = refs[2 * npart + 2:]
        x, y, c, _ = _place()
        me = 4 * x + 2 * y + c
        local = pltpu.make_async_copy(pack_ref, packs.at[me], loc_sem)
        local.start()
        cps = [pltpu.make_async_remote_copy(
            src_ref=ins[a], dst_ref=outs[a], send_sem=send_sems.at[a], recv_sem=recv_sems.at[a],
            device_id=(x, y, 1 - c), device_id_type=MESH) for a in range(npart)]
        for r in range(1, 8):
            dx, dy, dc = (r >> 2) & 1, (r >> 1) & 1, r & 1
            peer = (x + dx - 2 * x * dx, y + dy - 2 * y * dy, c + dc - 2 * c * dc)
            cps.append(pltpu.make_async_remote_copy(
                src_ref=pack_ref, dst_ref=packs.at[me], send_sem=psend.at[r - 1], recv_sem=precv.at[r - 1],
                device_id=peer, device_id_type=MESH))
        for cp in cps:
            cp.start()
        for cp in cps:
            cp.wait_recv()
        for cp in cps:
            cp.wait_send()
        local.wait()

    return pl.pallas_call(
        body, name="final_exchange",
        in_specs=[ANY] * (npart + 1), out_specs=[ANY] * (npart + 1),
        out_shape=[jax.ShapeDtypeStruct(p.shape, p.dtype) for p in parts]
        + [jax.ShapeDtypeStruct((8,) + pack.shape, pack.dtype)],
        scratch_shapes=[pltpu.SemaphoreType.DMA((npart,)), pltpu.SemaphoreType.DMA((npart,)),
                        pltpu.SemaphoreType.DMA((7,)), pltpu.SemaphoreType.DMA((7,)), pltpu.SemaphoreType.DMA],
    )(*parts, pack)


def _sum_blocks(own, land, rows, name):
    r, cdim = own.shape
    rows = min(rows, r)

    def body(own_ref, land_ref, o_ref):
        acc = own_ref[...].astype(F32)
        for jj in range(3):
            acc = acc + land_ref[jj].astype(F32)
        o_ref[...] = acc

    return pl.pallas_call(
        body, name=name, grid=(r // rows,),
        in_specs=[pl.BlockSpec((rows, cdim), lambda i: (i, 0)), pl.BlockSpec((3, rows, cdim), lambda i: (0, i, 0))],
        out_specs=pl.BlockSpec((rows, cdim), lambda i: (i, 0)),
        out_shape=jax.ShapeDtypeStruct((r, cdim), F32),
        compiler_params=_params(("parallel",), 40 * 2**20),
    )(own, land)


def _sum_packs(packs):
    def body(p_ref, o_ref):
        acc = p_ref[0]
        for d in range(1, 8):
            acc = acc + p_ref[d]
        o_ref[...] = acc

    return pl.pallas_call(
        body, name="sum_packs", out_shape=jax.ShapeDtypeStruct(packs.shape[1:], F32),
    )(packs)


def _adamw_update(g, w_ref, m_ref, v_ref, go, do, mo, vo):
    c1 = 1.0 / (1.0 - ADAM_B1 ** ADAM_STEP)
    c2 = 1.0 / (1.0 - ADAM_B2 ** ADAM_STEP)
    mn = ADAM_B1 * m_ref[...] + (1.0 - ADAM_B1) * g
    vn = ADAM_B2 * v_ref[...] + (1.0 - ADAM_B2) * (g * g)
    go[...] = g
    mo[...] = mn
    vo[...] = vn
    do[...] = -ADAM_LR * ((mn * c1) / (jnp.sqrt(vn * c2) + ADAM_EPS) + ADAM_WD * w_ref[...])


def _adamw(w, m, v, g1, g2, rows, name):
    r, cdim = w.shape
    rows = min(rows, r)

    def body(*refs):
        n_in = 4 if g2 is None else 5
        w_ref, m_ref, v_ref, g_ref = refs[:4]
        g = g_ref[...] if g2 is None else g_ref[...] + refs[4][...]
        _adamw_update(g, w_ref, m_ref, v_ref, *refs[n_in:n_in + 4])

    blk = pl.BlockSpec((rows, cdim), lambda i: (i, 0))
    args = [w, m, v, g1] + ([] if g2 is None else [g2])
    shp = jax.ShapeDtypeStruct((r, cdim), F32)
    return pl.pallas_call(
        body, name=name, grid=(r // rows,),
        in_specs=[blk] * len(args), out_specs=[blk] * 4, out_shape=[shp] * 4,
        compiler_params=_params(("parallel",), 20 * rows * cdim * 4 + 8 * 2**20),
    )(*args)


def _adamw_shard(wt, mt, vt, g1, g2):
    r, d = wt.shape
    cols = min(128, d)

    def body(w_ref, m_ref, v_ref, g_ref, g2_ref, go, do, mo, vo, pad_ref):
        chip = 2 * lax.axis_index("x") + lax.axis_index("y")
        back = [(ALIGNED_W - s) % ALIGNED_W for s in SHIFTS]
        pad_ref[...] = pltpu.roll(g_ref[...] + g2_ref[...], _by_chip(chip, back), 0)
        _adamw_update(pad_ref[0:r, :], w_ref, m_ref, v_ref, go, do, mo, vo)

    blk = pl.BlockSpec((r, cols), lambda i: (0, i))
    gblk = pl.BlockSpec((ALIGNED_W, cols), lambda i: (0, i))
    shp = jax.ShapeDtypeStruct((r, d), F32)
    return pl.pallas_call(
        body, name="adamw_w_in", grid=(d // cols,),
        in_specs=[blk] * 3 + [gblk] * 2, out_specs=[blk] * 4, out_shape=[shp] * 4,
        scratch_shapes=[pltpu.VMEM((ALIGNED_W, cols), F32)],
        compiler_params=_params(("parallel",), 24 * ALIGNED_W * cols * 4 + 8 * 2**20),
    )(wt, mt, vt, g1, g2)


def _pad_lanes(a, width):
    return jnp.pad(a, ((0, 0), (0, width - a.shape[1])))


def _gathered_to_full(g):
    return jnp.transpose(g, (1, 0, 2)).reshape(g.shape[1], 4 * g.shape[2])


def _row(a):
    return _pad_lanes(a.reshape(1, -1), 1024)


def _small_pack(nin, cb, fn, al, dt, gn, cqw_shard, cw_shard):
    ad = jnp.concatenate([al.reshape(1, -1), dt.reshape(1, -1)], axis=1)
    rows = [_row(nin), _row(cb), _row(fn), _row(ad), _row(gn), cqw_shard.reshape(3, 1024), _row(cw_shard)]
    out = jnp.concatenate(rows, axis=0)
    return jnp.pad(out, ((0, 16 - out.shape[0]), (0, 0)))


def kernel(x, norm_in_w, w_in, conv_qkv_w, A_log, dt_bias, gdn_norm_w, conv_w, conv_b, w_out, final_norm_w, loss_target, m_norm_in_w, m_w_in, m_conv_qkv_w, m_A_log, m_dt_bias, m_gdn_norm_w, m_conv_w, m_conv_b, m_w_out, m_final_norm_w, v_norm_in_w, v_w_in, v_conv_qkv_w, v_A_log, v_dt_bias, v_gdn_norm_w, v_conv_w, v_conv_b, v_w_out, v_final_norm_w):
    chip = 2 * lax.axis_index("x") + lax.axis_index("y")
    a_shard = _align_shard(jnp.transpose(w_in[0]))
    wo_b = _cast_bf16(w_out[0], 256, "cast_w_out")
    wg0, wc0, wog0, cqg0, cwg0 = _place_own(a_shard, wo_b, conv_qkv_w[0], conv_w[0])
    ss_g, rs_g, bufs_g, tok_g = _gather_start("g", a_shard, wg0, [(conv_qkv_w[0], cqg0)])
    ss_c, rs_c, bufs_c, tok_c = _gather_start("c", bufs_g[0], wc0, [(conv_w[0], cwg0), (wo_b, wog0)])
    x0, _, _ = lax.optimization_barrier((x[0], tok_g, tok_c))
    h = _rms_in(x0, norm_in_w)
    a_thru, wg, _, cq_g = _gather_wait("g", ss_g, rs_g, [bufs_c[0]] + bufs_g[1:], h)
    w_g = _merge_edges(wg, G_EDGE, G_MIXED, "merge_edges_g")
    cqw = _gathered_to_full(cq_g)
    ad = jnp.pad(jnp.concatenate([A_log, dt_bias], axis=0), ((0, 0), (A_LANE, 0)))
    d_model = x.shape[-1]

    def late(o):
        _, wc, _, cw_g, _, wo_g = _gather_wait("c", ss_c, rs_c, [a_thru] + bufs_c[1:], o)
        return (_merge_edges(wc, C_EDGE, C_MIXED, "merge_edges_c"), wo_g.reshape(2 * GW, d_model),
                _gathered_to_full(cw_g))

    scat = {}

    def on_grad_c(g_c, g_wout, do):
        go4 = g_wout.reshape(4, GW // 2, d_model)
        land = lax.empty((3, ALIGNED_W, d_model), BF16)
        land_o = lax.empty((3, GW // 2, d_model), BF16)
        ss, rs, bufs, tok = _scatter_start("c", g_c, land, [(go4, land_o)])
        scat["c"] = (ss, rs, bufs)
        do, _ = lax.optimization_barrier((do, tok))
        return do

    def on_grad_g(g_g, dproj_g):
        ss, rs, bufs, tok = _scatter_start("g", g_g, scat["c"][2][1], [])
        scat["g"] = (ss, rs, bufs)
        dproj_g, _ = lax.optimization_barrier((dproj_g, tok))
        return dproj_g

    gx, sm, _ = _local_step(x0, loss_target[0], h, w_g, cqw, late, norm_in_w, ad, gdn_norm_w, conv_b,
                            final_norm_w.reshape(1, -1), on_grad_c, on_grad_g)

    ss, rs, bufs = scat["g"]
    g_g, land = _scatter_wait("g", ss, rs, bufs, gx)
    ss, rs, bufs = scat["c"]
    g_c, land, go4, land_o = _scatter_wait("c", ss, rs, [bufs[0], land, bufs[2], bufs[3]], gx)
    own_in, own_out = _own_grad_blocks(g_g, g_c, go4)
    part_in = _sum_blocks(own_in, land, 128, "sum_w_in")
    part_out = _sum_blocks(own_out, land_o, 128, "sum_w_out")
    ad_g = jnp.concatenate([sm["al"][:, A_LANE:], sm["dt"][:, A_LANE:]], axis=1)
    pack = jnp.concatenate([_row(sm["nin"]), _row(sm["cb"]), _row(sm["fn"]), _row(ad_g), _row(sm["gn"]),
                            jnp.concatenate(sm["cq"], axis=1).reshape(12, 1024), sm["cw"], _row(sm["loss"])], axis=0)
    pack = jnp.pad(pack, ((0, PACK_ROWS - pack.shape[0]), (0, 0)))
    sib_in, sib_out, packs = _final_exchange([part_in, part_out], pack)
    tot = _sum_packs(packs)

    g_wi, d_wi, m_wi, v_wi = [jnp.transpose(a) for a in _adamw_shard(
        jnp.transpose(w_in[0]), jnp.transpose(m_w_in[0]), jnp.transpose(v_w_in[0]), part_in, sib_in)]
    g_wo, d_wo, m_wo, v_wo = _adamw(w_out[0], m_w_out[0], v_w_out[0], part_out, sib_out, 128, "adamw_w_out")
    g_cq_sh = lax.dynamic_slice_in_dim(tot[R_CQ:R_CQ + 12].reshape(4, 3 * GW), chip * 768, 768, axis=1)
    g_cw_sh = lax.dynamic_slice_in_dim(tot[R_CW:R_CW + 3], chip * 256, 256, axis=1)
    sp = lambda nin, cb, fn, al, dt, gn, cq, cwv: _small_pack(nin, cb, fn, al, dt, gn, cq[0], cwv[0])
    g_s = _small_pack(tot[R_NIN], tot[R_CB], tot[R_FN], tot[R_AD, :HEADS], tot[R_AD, HEADS:2 * HEADS],
                      tot[R_GN, :DH], g_cq_sh, g_cw_sh)
    w_s = sp(norm_in_w, conv_b, final_norm_w, A_log, dt_bias, gdn_norm_w, conv_qkv_w, conv_w)
    m_s = sp(m_norm_in_w, m_conv_b, m_final_norm_w, m_A_log, m_dt_bias, m_gdn_norm_w, m_conv_qkv_w, m_conv_w)
    v_s = sp(v_norm_in_w, v_conv_b, v_final_norm_w, v_A_log, v_dt_bias, v_gdn_norm_w, v_conv_qkv_w, v_conv_w)
    small = _adamw(w_s, m_s, v_s, g_s, None, 16, "adamw_small")

    def unpack(a, big_in, big_out):
        return (a[0:1], big_in[None], a[5:8].reshape(1, 4, 768), a[3:4, :HEADS], a[3:4, HEADS:2 * HEADS],
                a[4:5, :DH], a[8, :768].reshape(1, 3, 256), a[1:2], big_out[None], a[2])

    loss = tot[R_LOSS, 0]
    return (loss, gx[None], *unpack(small[0], g_wi, g_wo), *unpack(small[1], d_wi, d_wo),
            *unpack(small[2], m_wi, m_wo), *unpack(small[3], v_wi, v_wo))
```

```python
import functools
import math

import jax
import jax.numpy as jnp
from jax import lax
from jax.experimental import pallas as pl
from jax.experimental.pallas import tpu as pltpu

F32 = jnp.float32
BF16 = jnp.bfloat16
MESH = pl.DeviceIdType.MESH
ANY = pl.BlockSpec(memory_space=pl.ANY)

HEADS = 8
DH = 128
CH = 64
GW = HEADS * DH
EPS = 1e-6
VMEM_V7X = 64 * 1024 * 1024

QB, KB, VB, ZB, BAB = 0, 8, 16, 24, 32
A_LANE = 120
NG, NC = 33, 32
GW_COLS, CW_COLS = NG * DH, NC * DH

SHARD_W = 2052
ALIGNED_BLOCKS = 17
ALIGNED_W = ALIGNED_BLOCKS * DH
SHIFTS = (0, 4, ALIGNED_W - 8, ALIGNED_W - 4)
G_EDGE, C_EDGE = 34, 32
G_SPARE, C_SPARE = 33, 34
WG_BLOCKS, WC_BLOCKS = 38, 36
G_MIXED, C_MIXED = (2, BAB), (4 * 7 + 1,)


def _shard_blocks(chip, edges):
    g, c = "g", "c"
    if chip == 0:
        out = [(g, 3 * b) for b in range(8)] + [(g, 3 * b + 1) for b in range(8)] + [(g, G_EDGE, G_MIXED[0])]
    elif chip == 1:
        out = [(g, G_EDGE + 1, G_MIXED[0])] + [(g, 3 * b + 2) for b in range(1, 8)]
        out += [(g, ZB + b) for b in range(8)] + [(g, G_EDGE + 2, G_MIXED[1])]
    elif chip == 2:
        out = [(c, 4 * b) for b in range(8)] + [(c, 4 * b + 1) for b in range(7)]
        out += [(c, C_EDGE, C_MIXED[0]), (g, G_EDGE + 3, G_MIXED[1])]
    else:
        out = [(c, 4 * b + 2) for b in range(8)] + [(c, 4 * b + 3) for b in range(8)] + [(c, C_EDGE + 1, C_MIXED[0])]
    return [(o[0], o[1] if (edges or len(o) == 2) else o[2]) for o in out]


def _by_chip(chip, vals):
    if all(v == vals[0] for v in vals):
        return vals[0]
    r = vals[3]
    for kk in (2, 1, 0):
        r = jnp.where(chip == kk, vals[kk], r)
    return r

ADAM_LR, ADAM_B1, ADAM_B2, ADAM_EPS, ADAM_WD, ADAM_STEP = 0.001, 0.9, 0.999, 1e-08, 0.01, 10

R_NIN, R_CB, R_FN, R_AD, R_GN, R_CQ, R_CW, R_LOSS, PACK_ROWS = 0, 1, 2, 3, 4, 5, 17, 20, 24

NN = ((1,), (0,))
NT = ((1,), (1,))
TN = ((0,), (0,))


def _dot(a, b, dims=NN, mode="lo"):
    dn = (dims, ((), ()))
    if mode == "hi":
        return lax.dot_general(a, b, dn, precision=lax.Precision.HIGHEST, preferred_element_type=F32)
    ah, bh = a.astype(BF16), b.astype(BF16)
    out = lax.dot_general(ah, bh, dn, preferred_element_type=F32)
    if mode == "x3":
        al = (a - ah.astype(F32)).astype(BF16)
        bl = (b - bh.astype(F32)).astype(BF16)
        out = out + lax.dot_general(ah, bl, dn, preferred_element_type=F32)
        out = out + lax.dot_general(al, bh, dn, preferred_element_type=F32)
    return out


P_GRAM, P_INV, P_SOL, P_SCAN, P_SCANB, P_BWD = "lo", "lo", "lo", "lo", "lo", "lo"
P_CUM = "x3"


def _params(sem=None, vmem=None):
    kw = {}
    if sem is not None:
        kw["dimension_semantics"] = sem
    if vmem is not None:
        kw["vmem_limit_bytes"] = int(min(max(vmem, 32 * 2**20), VMEM_V7X - 8 * 2**20))
    return pltpu.CompilerParams(**kw)


def _sigmoid(x):
    return 1.0 / (1.0 + jnp.exp(-x))


def _dsilu(x, s):
    return s * (1.0 + x * (1.0 - s))


def _rows(shape):
    return lax.broadcasted_iota(jnp.int32, shape, 0)


def _shift_down(x, s):
    if s == 0:
        return x
    return jnp.where(_rows(x.shape) >= s, pltpu.roll(x, s, 0), 0.0)


def _shift_up(x, s):
    if s == 0:
        return x
    n = x.shape[0]
    return jnp.where(_rows(x.shape) < n - s, pltpu.roll(x, n - s, 0), 0.0)


def _matmul(a, b, dims, out_dtype, tm, tn, tk, name, add=None, n=None):
    if dims == NN:
        (m, k), n = a.shape, b.shape[1]
    elif dims == NT:
        (m, k), n = a.shape, (n or b.shape[0])
    else:
        (k, m), n = a.shape, b.shape[1]
    tm, tn, tk = min(tm, m), min(tn, n), min(tk, k)
    assert m % tm == 0 and n % tn == 0 and k % tk == 0, (name, m, n, k, tm, tn, tk)
    nk = k // tk

    def body(*refs):
        if add is None:
            a_ref, b_ref, o_ref = refs[:3]
            add_ref = None
        else:
            a_ref, b_ref, add_ref, o_ref = refs[:4]
        part = _dot(a_ref[...], b_ref[...], dims)
        if nk == 1:
            if add_ref is not None:
                part = part + add_ref[...]
            o_ref[...] = part.astype(out_dtype)
            return
        acc = refs[-1]
        kk = pl.program_id(2)

        @pl.when(kk == 0)
        def _():
            acc[...] = part

        @pl.when(kk > 0)
        def _():
            acc[...] += part

        @pl.when(kk == nk - 1)
        def _():
            r = acc[...]
            if add_ref is not None:
                r = r + add_ref[...]
            o_ref[...] = r.astype(out_dtype)

    if dims == TN:
        a_spec = pl.BlockSpec((tk, tm), lambda i, j, kk: (kk, i))
    else:
        a_spec = pl.BlockSpec((tm, tk), lambda i, j, kk: (i, kk))
    if dims == NT:
        b_spec = pl.BlockSpec((tn, tk), lambda i, j, kk: (j, kk))
    else:
        b_spec = pl.BlockSpec((tk, tn), lambda i, j, kk: (kk, j))
    o_spec = pl.BlockSpec((tm, tn), lambda i, j, kk: (i, j))
    in_specs = [a_spec, b_spec]
    args = [a, b]
    if add is not None:
        in_specs.append(o_spec)
        args.append(add)
    osz = jnp.dtype(out_dtype).itemsize
    est = 2 * (tm * tk * a.dtype.itemsize + tk * tn * b.dtype.itemsize + tm * tn * osz)
    est += 3 * tm * tn * 4 + (2 * tm * tn * 4 if add is not None else 0)
    return pl.pallas_call(
        body, name=name, grid=(m // tm, n // tn, nk),
        in_specs=in_specs, out_specs=o_spec,
        out_shape=jax.ShapeDtypeStruct((m, n), out_dtype),
        scratch_shapes=[pltpu.VMEM((tm, tn), F32)] if nk > 1 else [],
        compiler_params=_params(("parallel", "parallel", "arbitrary"), est + 8 * 2**20),
    )(*args)


def _cast_bf16(a, rows, name):
    r, c = a.shape
    rows = min(rows, r)

    def body(a_ref, o_ref):
        o_ref[...] = a_ref[...].astype(BF16)

    return pl.pallas_call(
        body, name=name, grid=(r // rows,),
        in_specs=[pl.BlockSpec((rows, c), lambda i: (i, 0))],
        out_specs=pl.BlockSpec((rows, c), lambda i: (i, 0)),
        out_shape=jax.ShapeDtypeStruct((r, c), BF16),
        compiler_params=_params(("parallel",)),
    )(a)


def _align_shard(wt):
    r, d = wt.shape
    cols = min(256, d)

    def body(w_ref, o_ref, pad_ref):
        chip = 2 * lax.axis_index("x") + lax.axis_index("y")
        pad_ref[...] = jnp.zeros_like(pad_ref)
        pad_ref[0:r, :] = w_ref[...]
        o_ref[...] = pltpu.roll(pad_ref[...], _by_chip(chip, SHIFTS), 0).astype(BF16)

    return pl.pallas_call(
        body, name="align_shard", grid=(d // cols,),
        in_specs=[pl.BlockSpec((r, cols), lambda i: (0, i))],
        out_specs=pl.BlockSpec((ALIGNED_W, cols), lambda i: (0, i)),
        out_shape=jax.ShapeDtypeStruct((ALIGNED_W, d), BF16),
        scratch_shapes=[pltpu.VMEM((ALIGNED_W, cols), F32)],
        compiler_params=_params(("parallel",)),
    )(wt)


def _rms_in(x, w):
    n, d = x.shape
    tr = min(256, n)

    def body(x_ref, w_ref, h_ref):
        xv = x_ref[...]
        r = lax.rsqrt(jnp.mean(xv * xv, axis=-1, keepdims=True) + EPS)
        h_ref[...] = (xv * r * w_ref[...]).astype(BF16)

    return pl.pallas_call(
        body, name="rms_in", grid=(n // tr,),
        in_specs=[pl.BlockSpec((tr, d), lambda i: (i, 0)), pl.BlockSpec((1, d), lambda i: (0, 0))],
        out_specs=pl.BlockSpec((tr, d), lambda i: (i, 0)),
        out_shape=jax.ShapeDtypeStruct((n, d), BF16),
        compiler_params=_params(("parallel",)),
    )(x, w)


def _conv_silu(p, w_ref, taps):
    c = None
    for j in range(taps):
        t = _shift_down(p, taps - 1 - j) * w_ref[j:j + 1, :]
        c = t if c is None else c + t
    return c


def _prep_qkv(proj, cw):
    n = proj.shape[0]

    def body(p3, wq, wk, wv, q_ref, k_ref, v_ref):
        for kind, (w_ref, o_ref) in enumerate(((wq, q_ref), (wk, k_ref), (wv, v_ref))):
            c = _conv_silu(p3[:, kind * DH:(kind + 1) * DH], w_ref, 4)
            a = c * _sigmoid(c)
            if kind < 2:
                r = lax.rsqrt(jnp.sum(a * a, axis=-1, keepdims=True) + EPS)
                a = a * (r * (DH ** -0.5 if kind == 0 else 1.0))
            o_ref[...] = a

    col = pl.BlockSpec((n, DH), lambda h: (0, h))
    wcol = lambda base: pl.BlockSpec((4, DH), lambda h: (0, base + h))
    out = jax.ShapeDtypeStruct((n, GW), F32)
    return pl.pallas_call(
        body, name="prep_qkv", grid=(HEADS,),
        in_specs=[pl.BlockSpec((n, 3 * DH), lambda h: (0, h)), wcol(QB), wcol(KB), wcol(VB)],
        out_specs=[col] * 3, out_shape=[out] * 3,
        compiler_params=_params(("parallel",), 40 * 2**20),
    )(proj, cw, cw, cw)


def _prep_qkv_bwd(proj, cw, dq, dk, dv, dproj):
    n = proj.shape[0]

    def body(p3, wq, wk, wv, dq_ref, dk_ref, dv_ref, _, o3, gq, gk, gv):
        for kind, (w_ref, d_ref, g_ref) in enumerate(((wq, dq_ref, gq), (wk, dk_ref, gk), (wv, dv_ref, gv))):
            p = p3[:, kind * DH:(kind + 1) * DH]
            c = _conv_silu(p, w_ref, 4)
            s = _sigmoid(c)
            a = c * s
            d = d_ref[...]
            if kind < 2:
                r = lax.rsqrt(jnp.sum(a * a, axis=-1, keepdims=True) + EPS)
                sc = DH ** -0.5 if kind == 0 else 1.0
                d = (sc * r) * (d - a * ((r * r) * jnp.sum(d * a, axis=-1, keepdims=True)))
            dc = d * _dsilu(c, s)
            dp = None
            for j in range(4):
                g_ref[j:j + 1, :] = jnp.sum(dc * _shift_down(p, 3 - j), axis=0, keepdims=True)
                t = _shift_up(dc, 3 - j) * w_ref[j:j + 1, :]
                dp = t if dp is None else dp + t
            o3[:, kind * DH:(kind + 1) * DH] = dp.astype(BF16)

    col = pl.BlockSpec((n, DH), lambda h: (0, h))
    wcol = lambda base: pl.BlockSpec((4, DH), lambda h: (0, base + h))
    p3spec = pl.BlockSpec((n, 3 * DH), lambda h: (0, h))
    return pl.pallas_call(
        body, name="prep_qkv_bwd", grid=(HEADS,),
        in_specs=[p3spec, wcol(QB), wcol(KB), wcol(VB), col, col, col, ANY],
        out_specs=[p3spec] + [wcol(0)] * 3,
        out_shape=[jax.ShapeDtypeStruct(dproj.shape, BF16)] + [jax.ShapeDtypeStruct((4, GW), F32)] * 3,
        input_output_aliases={7: 0},
        compiler_params=_params(("parallel",), 48 * 2**20),
    )(proj, cw, cw, cw, dq, dk, dv, dproj)


def _tri(lower_incl):
    i = lax.broadcasted_iota(jnp.int32, (CH, CH), 0)
    j = lax.broadcasted_iota(jnp.int32, (CH, CH), 1)
    return jnp.where(i >= j, 1.0, 0.0) if lower_incl else jnp.where(j >= i, 1.0, 0.0)


def _lane(shape):
    return lax.broadcasted_iota(jnp.int32, shape, 1)


def _prep_bg(proj, ad):
    n = proj.shape[0]
    nch = n // CH

    def body(p_ref, ad_ref, bg_ref, bgt_ref):
        p = p_ref[...]
        lane = _lane(p.shape)
        beta = _sigmoid(p)
        xa = p + ad_ref[1:2, :]
        sp = jnp.maximum(xa, 0.0) + jnp.log(1.0 + jnp.exp(-jnp.abs(xa)))
        g = pltpu.roll(-jnp.exp(ad_ref[0:1, :]) * sp, DH - A_LANE + HEADS, 1)
        gc = _dot(_tri(True), g, NN, P_CUM)
        bg = jnp.where(lane < HEADS, beta, jnp.where(lane < 2 * HEADS, gc, 0.0))
        bg_ref[...] = bg
        bgt_ref[0] = bg.T

    return pl.pallas_call(
        body, name="prep_bg", grid=(nch,),
        in_specs=[pl.BlockSpec((CH, DH), lambda i: (i, BAB)), pl.BlockSpec((2, DH), lambda i: (0, 0))],
        out_specs=[pl.BlockSpec((CH, DH), lambda i: (i, 0)), pl.BlockSpec((1, DH, CH), lambda i: (i, 0, 0))],
        out_shape=[jax.ShapeDtypeStruct((n, DH), F32), jax.ShapeDtypeStruct((nch, DH, CH), F32)],
        compiler_params=_params(("parallel",)),
    )(proj, ad)


def _prep_bg_bwd(proj, ad, dbg, dproj):
    n = proj.shape[0]
    nch = n // CH

    def body(p_ref, ad_ref, d_ref, _, o_ref, ga_ref, gd_ref):
        p = p_ref[...]
        d = d_ref[...]
        lane = _lane(p.shape)
        beta = _sigmoid(p)
        xa = p + ad_ref[1:2, :]
        sp = jnp.maximum(xa, 0.0) + jnp.log(1.0 + jnp.exp(-jnp.abs(xa)))
        na = -jnp.exp(ad_ref[0:1, :])
        dg = pltpu.roll(_dot(_tri(False), d, NN, P_CUM), A_LANE - HEADS, 1)
        da = dg * na * _sigmoid(xa)
        is_g = lane >= A_LANE
        o_ref[...] = jnp.where(lane < HEADS, d * beta * (1.0 - beta), jnp.where(is_g, da, 0.0)).astype(BF16)
        ga = jnp.sum(jnp.where(is_g, dg * na * sp, 0.0), axis=0, keepdims=True)
        gd = jnp.sum(jnp.where(is_g, da, 0.0), axis=0, keepdims=True)

        @pl.when(pl.program_id(0) == 0)
        def _():
            ga_ref[...] = jnp.zeros_like(ga_ref)
            gd_ref[...] = jnp.zeros_like(gd_ref)

        ga_ref[...] += ga
        gd_ref[...] += gd

    one = pl.BlockSpec((1, DH), lambda i: (0, 0))
    return pl.pallas_call(
        body, name="prep_bg_bwd", grid=(nch,),
        in_specs=[pl.BlockSpec((CH, DH), lambda i: (i, BAB)), pl.BlockSpec((2, DH), lambda i: (0, 0)),
                  pl.BlockSpec((CH, DH), lambda i: (i, 0)), ANY],
        out_specs=[pl.BlockSpec((CH, DH), lambda i: (i, BAB)), one, one],
        out_shape=[jax.ShapeDtypeStruct(dproj.shape, BF16), jax.ShapeDtypeStruct((1, DH), F32),
                   jax.ShapeDtypeStruct((1, DH), F32)],
        input_output_aliases={3: 0},
        compiler_params=_params(("arbitrary",)),
    )(proj, ad, dbg, dproj)


def _gdn_out(o, proj, wg):
    n = o.shape[0]

    def body(o_ref, z_ref, w_ref, y_ref):
        ov, z = o_ref[...], z_ref[...]
        r = lax.rsqrt(jnp.mean(ov * ov, axis=-1, keepdims=True) + EPS)
        y_ref[...] = (ov * r * w_ref[...] * (z * _sigmoid(z))).astype(BF16)

    return pl.pallas_call(
        body, name="gdn_out", grid=(HEADS,),
        in_specs=[pl.BlockSpec((n, DH), lambda h: (0, h)), pl.BlockSpec((n, DH), lambda h: (0, ZB + h)),
                  pl.BlockSpec((1, DH), lambda h: (0, 0))],
        out_specs=pl.BlockSpec((n, DH), lambda h: (0, h)),
        out_shape=jax.ShapeDtypeStruct((n, 2 * GW), BF16),
        compiler_params=_params(("parallel",)),
    )(o, proj, wg)


def _gdn_out_bwd(o, proj, wg, dmix):
    n = o.shape[0]

    def body(o_ref, z_ref, w_ref, d_ref, do_ref, dz_ref, gw_ref):
        ov, z, d, w = o_ref[...], z_ref[...], d_ref[...], w_ref[...]
        r = lax.rsqrt(jnp.mean(ov * ov, axis=-1, keepdims=True) + EPS)
        nrm = ov * r
        s = _sigmoid(z)
        dz_ref[...] = (d * (nrm * w) * _dsilu(z, s)).astype(BF16)
        dn_w = d * (z * s)
        gw = jnp.sum(dn_w * nrm, axis=0, keepdims=True)
        dn = dn_w * w
        do_ref[...] = r * (dn - nrm * jnp.mean(dn * nrm, axis=-1, keepdims=True))

        @pl.when(pl.program_id(0) == 0)
        def _():
            gw_ref[...] = jnp.zeros_like(gw_ref)

        gw_ref[...] += gw

    return pl.pallas_call(
        body, name="gdn_out_bwd", grid=(HEADS,),
        in_specs=[pl.BlockSpec((n, DH), lambda h: (0, h)), pl.BlockSpec((n, DH), lambda h: (0, ZB + h)),
                  pl.BlockSpec((1, DH), lambda h: (0, 0)), pl.BlockSpec((n, DH), lambda h: (0, h))],
        out_specs=[pl.BlockSpec((n, DH), lambda h: (0, h)), pl.BlockSpec((n, DH), lambda h: (0, ZB + h)),
                   pl.BlockSpec((1, DH), lambda h: (0, 0))],
        out_shape=[jax.ShapeDtypeStruct((n, GW), F32), jax.ShapeDtypeStruct((n, GW_COLS), BF16),
                   jax.ShapeDtypeStruct((1, DH), F32)],
        compiler_params=_params(("arbitrary",)),
    )(o, proj, wg, dmix)


def _conv_branch(proj, w3, b, mix):
    n = proj.shape[0]

    def body(p4, w_ref, b_ref, _, y_ref):
        u = p4[:, DH:2 * DH] * p4[:, 2 * DH:3 * DH]
        cc = _conv_silu(u, w_ref, 3) + b_ref[...]
        z = p4[:, 3 * DH:4 * DH]
        y_ref[...] = (p4[:, 0:DH] * cc * (z * _sigmoid(z))).astype(BF16)

    return pl.pallas_call(
        body, name="conv_branch", grid=(HEADS,),
        in_specs=[pl.BlockSpec((n, 4 * DH), lambda h: (0, h)), pl.BlockSpec((3, DH), lambda h: (0, h)),
                  pl.BlockSpec((1, DH), lambda h: (0, h)), ANY],
        out_specs=pl.BlockSpec((n, DH), lambda h: (0, HEADS + h)),
        out_shape=jax.ShapeDtypeStruct(mix.shape, BF16),
        input_output_aliases={3: 0},
        compiler_params=_params(("parallel",), 40 * 2**20),
    )(proj, w3, b, mix)


def _conv_branch_bwd(proj, w3, b, dmix):
    n = proj.shape[0]

    def body(p4, w_ref, b_ref, d_ref, o4, gw_ref, gbias_ref):
        gb, gcv, hc, z = p4[:, 0:DH], p4[:, DH:2 * DH], p4[:, 2 * DH:3 * DH], p4[:, 3 * DH:4 * DH]
        d = d_ref[...]
        dgb, dgc, dhc, dzc = (o4.at[:, kk * DH:(kk + 1) * DH] for kk in range(4))
        u = gcv * hc
        cc = _conv_silu(u, w_ref, 3) + b_ref[...]
        s = _sigmoid(z)
        dzc[...] = (d * (gb * cc) * _dsilu(z, s)).astype(BF16)
        dp = d * (z * s)
        dgb[...] = (dp * cc).astype(BF16)
        dcc = dp * gb
        gbias_ref[...] = jnp.sum(dcc, axis=0, keepdims=True)
        du = None
        for j in range(3):
            gw_ref[j:j + 1, :] = jnp.sum(dcc * _shift_down(u, 2 - j), axis=0, keepdims=True)
            t = _shift_up(dcc, 2 - j) * w_ref[j:j + 1, :]
            du = t if du is None else du + t
        dgc[...] = (du * hc).astype(BF16)
        dhc[...] = (du * gcv).astype(BF16)

    p4spec = pl.BlockSpec((n, 4 * DH), lambda h: (0, h))
    return pl.pallas_call(
        body, name="conv_branch_bwd", grid=(HEADS,),
        in_specs=[p4spec, pl.BlockSpec((3, DH), lambda h: (0, h)), pl.BlockSpec((1, DH), lambda h: (0, h)),
                  pl.BlockSpec((n, DH), lambda h: (0, HEADS + h))],
        out_specs=[p4spec, pl.BlockSpec((3, DH), lambda h: (0, h)), pl.BlockSpec((1, DH), lambda h: (0, h))],
        out_shape=[jax.ShapeDtypeStruct((n, CW_COLS), BF16), jax.ShapeDtypeStruct((3, GW), F32),
                   jax.ShapeDtypeStruct((1, GW), F32)],
        compiler_params=_params(("parallel",), 48 * 2**20),
    )(proj, w3, b, dmix)


def _final_loss(out, tgt, wf):
    n, d = out.shape
    tr = min(256, n)

    def body(o_ref, t_ref, w_ref, do_ref, dob_ref, gw_ref, loss_ref):
        ov, w = o_ref[...], w_ref[...]
        r = lax.rsqrt(jnp.mean(ov * ov, axis=-1, keepdims=True) + EPS)
        nrm = ov * r
        e = nrm * w - t_ref[...]
        dy = e * (1.0 / d)
        dn = dy * w
        dout = r * (dn - nrm * jnp.mean(dn * nrm, axis=-1, keepdims=True))
        do_ref[...] = dout
        dob_ref[...] = dout.astype(BF16)

        @pl.when(pl.program_id(0) == 0)
        def _():
            gw_ref[...] = jnp.zeros_like(gw_ref)
            loss_ref[...] = jnp.zeros_like(loss_ref)

        gw_ref[...] += jnp.sum(dy * nrm, axis=0, keepdims=True)
        loss_ref[...] += (0.5 / d) * jnp.sum(jnp.sum(e * e, axis=-1, keepdims=True), axis=0, keepdims=True)

    row = pl.BlockSpec((tr, d), lambda i: (i, 0))
    return pl.pallas_call(
        body, name="final_loss", grid=(n // tr,),
        in_specs=[row, row, pl.BlockSpec((1, d), lambda i: (0, 0))],
        out_specs=[row, row, pl.BlockSpec((1, d), lambda i: (0, 0)), pl.BlockSpec((1, 1), lambda i: (0, 0))],
        out_shape=[jax.ShapeDtypeStruct((n, d), F32), jax.ShapeDtypeStruct((n, d), BF16),
                   jax.ShapeDtypeStruct((1, d), F32), jax.ShapeDtypeStruct((1, 1), F32)],
        compiler_params=_params(("arbitrary",)),
    )(out, tgt, wf)


def _rms_in_bwd(x, w, dh, dout):
    n, d = x.shape
    tr = min(256, n)

    def body(x_ref, w_ref, dh_ref, do_ref, dx_ref, gw_ref):
        xv, dhv = x_ref[...], dh_ref[...]
        r = lax.rsqrt(jnp.mean(xv * xv, axis=-1, keepdims=True) + EPS)
        xn = xv * r
        dxn = dhv * w_ref[...]
        dx_ref[...] = r * (dxn - xn * jnp.mean(dxn * xn, axis=-1, keepdims=True)) + do_ref[...]

        @pl.when(pl.program_id(0) == 0)
        def _():
            gw_ref[...] = jnp.zeros_like(gw_ref)

        gw_ref[...] += jnp.sum(dhv * xn, axis=0, keepdims=True)

    row = pl.BlockSpec((tr, d), lambda i: (i, 0))
    one = pl.BlockSpec((1, d), lambda i: (0, 0))
    return pl.pallas_call(
        body, name="rms_in_bwd", grid=(n // tr,),
        in_specs=[row, one, row, row], out_specs=[row, one],
        out_shape=[jax.ShapeDtypeStruct((n, d), F32), jax.ShapeDtypeStruct((1, d), F32)],
        compiler_params=_params(("arbitrary",)),
    )(x, w, dh, dout)


def _ij():
    i = lax.broadcasted_iota(jnp.int32, (CH, CH), 0)
    j = lax.broadcasted_iota(jnp.int32, (CH, CH), 1)
    return i, j


def _unit_lower_inverse(mats):
    i, j = _ij()
    eye = jnp.where(i == j, 1.0, 0.0)
    same16 = (i // 16) == (j // 16)
    same32 = (i // 32) == (j // 32)
    mm = lambda xs, ys: [_dot(x, y, NN, P_INV) for x, y in zip(xs, ys)]
    n1 = [jnp.where(same16, -a, 0.0) for a in mats]
    n2 = mm(n1, n1)
    n4 = mm(n2, n2)
    n8 = mm(n4, n4)
    t = [eye + x1 + x2 + x3 for x1, x2, x3 in zip(n1, n2, mm(n1, n2))]
    t = [x + y for x, y in zip(t, mm(t, n4))]
    t = [x + y for x, y in zip(t, mm(t, n8))]
    a1 = [jnp.where(same32 & jnp.logical_not(same16), a, 0.0) for a in mats]
    t = [x - y for x, y in zip(t, mm(t, mm(a1, t)))]
    a2 = [jnp.where(same32, 0.0, a) for a in mats]
    t = [x - y for x, y in zip(t, mm(t, mm(a2, t)))]
    return t


def _head_vectors(bg, bgt, h):
    bcol = bg[:, h:h + 1]
    gcol = bg[:, HEADS + h:HEADS + h + 1]
    grow = bgt[HEADS + h:HEADS + h + 1, :]
    return bcol, gcol, grow


def _decay(gcol, grow):
    i, j = _ij()
    return jnp.where(i >= j, jnp.exp(jnp.where(i >= j, gcol - grow, 0.0)), 0.0)


def _gdn_intra(q, k, v, bg, bgt):
    n = q.shape[0]
    nch = n // CH

    def body(q_ref, k_ref, v_ref, bg_ref, bgt_ref, u_ref, w_ref, p_ref, t_ref):
        bg, bgt = bg_ref[...], bgt_ref[0]
        i, j = _ij()
        sls = [slice(h * DH, (h + 1) * DH) for h in range(HEADS)]
        ks = [k_ref[:, sl] for sl in sls]
        vecs = [_head_vectors(bg, bgt, h) for h in range(HEADS)]
        decs = [_decay(gcol, grow) for _, gcol, grow in vecs]
        kks = [_dot(kh, kh, NT, P_GRAM) for kh in ks]
        qks = [_dot(q_ref[:, sl], kh, NT, P_GRAM) for sl, kh in zip(sls, ks)]
        ts = _unit_lower_inverse([jnp.where(i > j, bcol * kk * dec, 0.0)
                                  for (bcol, _, _), kk, dec in zip(vecs, kks, decs)])
        us = [_dot(t, v_ref[:, sl] * bcol, NN, P_SOL) for t, sl, (bcol, _, _) in zip(ts, sls, vecs)]
        ws = [_dot(t, kh * (bcol * jnp.exp(gcol)), NN, P_SOL) for t, kh, (bcol, gcol, _) in zip(ts, ks, vecs)]
        for h, sl in enumerate(sls):
            p_ref[0, h] = qks[h] * decs[h]
            t_ref[0, h] = ts[h]
            u_ref[:, sl] = us[h]
            w_ref[:, sl] = ws[h]

    row = pl.BlockSpec((CH, GW), lambda c: (c, 0))
    sq = pl.BlockSpec((1, HEADS, CH, CH), lambda c: (c, 0, 0, 0))
    big = jax.ShapeDtypeStruct((n, GW), F32)
    sqs = jax.ShapeDtypeStruct((nch, HEADS, CH, CH), F32)
    return pl.pallas_call(
        body, name="gdn_intra", grid=(nch,),
        in_specs=[row, row, row, pl.BlockSpec((CH, DH), lambda c: (c, 0)),
                  pl.BlockSpec((1, DH, CH), lambda c: (c, 0, 0))],
        out_specs=[row, row, sq, sq], out_shape=[big, big, sqs, sqs],
        compiler_params=_params(("parallel",)),
    )(q, k, v, bg, bgt)


def _gdn_scan(q, k, bg, u, w, p):
    n = q.shape[0]
    nch = n // CH

    def body(q_ref, k_ref, bg_ref, u_ref, w_ref, p_ref, o_ref, vn_ref, s_out, s_scr):
        @pl.when(pl.program_id(0) == 0)
        def _():
            s_scr[...] = jnp.zeros_like(s_scr)

        bg = bg_ref[...]
        hs = range(HEADS)
        sls = [slice(h * DH, (h + 1) * DH) for h in hs]
        gcols = [bg[:, HEADS + h:HEADS + h + 1] for h in hs]
        glasts = [g[CH - 1:CH, :] for g in gcols]
        ss = [s_scr[h] for h in hs]
        wss = [_dot(w_ref[:, sl], s, NN, P_SCAN) for sl, s in zip(sls, ss)]
        oqs = [_dot(q_ref[:, sl] * jnp.exp(g), s, NN, P_SCAN) for sl, s, g in zip(sls, ss, gcols)]
        vns = [u_ref[:, sl] - x for sl, x in zip(sls, wss)]
        ops = [_dot(p_ref[0, h], vn, NN, P_SCAN) for h, vn in zip(hs, vns)]
        sns = [_dot(k_ref[:, sl] * jnp.exp(gl - g), vn, TN, P_SCAN)
               for sl, gl, g, vn in zip(sls, glasts, gcols, vns)]
        for h, sl in enumerate(sls):
            s_out[0, :, sl] = ss[h]
            vn_ref[:, sl] = vns[h]
            o_ref[:, sl] = oqs[h] + ops[h]
            s_scr[h] = ss[h] * jnp.exp(glasts[h]) + sns[h]

    row = pl.BlockSpec((CH, GW), lambda c: (c, 0))
    big = jax.ShapeDtypeStruct((n, GW), F32)
    return pl.pallas_call(
        body, name="gdn_scan", grid=(nch,),
        in_specs=[row, row, pl.BlockSpec((CH, DH), lambda c: (c, 0)), row, row,
                  pl.BlockSpec((1, HEADS, CH, CH), lambda c: (c, 0, 0, 0))],
        out_specs=[row, row, pl.BlockSpec((1, DH, GW), lambda c: (c, 0, 0))],
        out_shape=[big, big, jax.ShapeDtypeStruct((nch, DH, GW), F32)],
        scratch_shapes=[pltpu.VMEM((HEADS, DH, DH), F32)],
        compiler_params=_params(("arbitrary",)),
    )(q, k, bg, u, w, p)


def _gdn_scan_bwd(q, k, bg, w, p, vn, s_in, do):
    n = q.shape[0]
    nch = n // CH
    rev = lambda c: nch - 1 - c

    def body(q_ref, k_ref, bg_ref, w_ref, p_ref, vn_ref, s_ref, do_ref,
             dqg_ref, dp_ref, du_ref, dw_ref, dks_ref, dgam_ref, ds_scr):
        @pl.when(pl.program_id(0) == 0)
        def _():
            ds_scr[...] = jnp.zeros_like(ds_scr)

        bg = bg_ref[...]
        lane = _lane((1, DH))
        hs = range(HEADS)
        sls = [slice(h * DH, (h + 1) * DH) for h in hs]
        gcols = [bg[:, HEADS + h:HEADS + h + 1] for h in hs]
        glasts = [g[CH - 1:CH, :] for g in gcols]
        ss = [s_ref[0, :, sl] for sl in sls]
        dss = [ds_scr[h] for h in hs]
        dos = [do_ref[:, sl] for sl in sls]
        vnl = [vn_ref[:, sl] for sl in sls]
        dqgs = [_dot(d, s, NT, P_SCANB) for d, s in zip(dos, ss)]
        dps = [_dot(d, vn, NT, P_SCANB) for d, vn in zip(dos, vnl)]
        dvn1 = [_dot(p_ref[0, h], d, TN, P_SCANB) for h, d in zip(hs, dos)]
        dvn2 = [_dot(k_ref[:, sl] * jnp.exp(gl - g), ds, NN, P_SCANB)
                for sl, gl, g, ds in zip(sls, glasts, gcols, dss)]
        dkss = [_dot(vn, ds, NT, P_SCANB) for vn, ds in zip(vnl, dss)]
        dsq = [_dot(q_ref[:, sl] * jnp.exp(g), d, TN, P_SCANB) for sl, g, d in zip(sls, gcols, dos)]
        dvns = [a + b for a, b in zip(dvn1, dvn2)]
        dws = [_dot(dvn, s, NT, P_SCANB) for dvn, s in zip(dvns, ss)]
        dsw = [_dot(w_ref[:, sl], dvn, TN, P_SCANB) for sl, dvn in zip(sls, dvns)]
        dgam = jnp.zeros((1, DH), F32)
        for h, sl in enumerate(sls):
            dqg_ref[:, sl] = dqgs[h]
            dp_ref[0, h] = dps[h]
            du_ref[:, sl] = dvns[h]
            dw_ref[:, sl] = -dws[h]
            dks_ref[:, sl] = dkss[h]
            tot = jnp.sum(jnp.sum(dss[h] * ss[h], axis=-1, keepdims=True), axis=0, keepdims=True)
            dgam = dgam + jnp.where(lane == h, tot, 0.0)
            ds_scr[h] = dss[h] * jnp.exp(glasts[h]) + dsq[h] - dsw[h]
        dgam_ref[0] = jnp.broadcast_to(dgam, (8, DH))

    row = pl.BlockSpec((CH, GW), lambda c: (rev(c), 0))
    sq = pl.BlockSpec((1, HEADS, CH, CH), lambda c: (rev(c), 0, 0, 0))
    big = jax.ShapeDtypeStruct((n, GW), F32)
    return pl.pallas_call(
        body, name="gdn_scan_bwd", grid=(nch,),
        in_specs=[row, row, pl.BlockSpec((CH, DH), lambda c: (rev(c), 0)), row, sq, row,
                  pl.BlockSpec((1, DH, GW), lambda c: (rev(c), 0, 0)), row],
        out_specs=[row, sq, row, row, row, pl.BlockSpec((1, 8, DH), lambda c: (rev(c), 0, 0))],
        out_shape=[big, jax.ShapeDtypeStruct((nch, HEADS, CH, CH), F32), big, big, big,
                   jax.ShapeDtypeStruct((nch, 8, DH), F32)],
        scratch_shapes=[pltpu.VMEM((HEADS, DH, DH), F32)],
        compiler_params=_params(("arbitrary",)),
    )(q, k, bg, w, p, vn, s_in, do)


def _gdn_intra_bwd(q, k, v, bg, bgt, t, u, w, p, dqg, dp, du, dw, dks, dgam):
    n = q.shape[0]
    nch = n // CH

    def body(q_ref, k_ref, v_ref, bg_ref, bgt_ref, t_ref, u_ref, w_ref, p_ref,
             dqg_ref, dp_ref, du_ref, dw_ref, dks_ref, dgam_ref, dq_ref, dk_ref, dv_ref, dbg_ref):
        bg, bgt = bg_ref[...], bgt_ref[0]
        dgam_all = dgam_ref[0]
        i, j = _ij()
        rows1 = lax.broadcasted_iota(jnp.int32, (CH, 1), 0)
        lane = _lane((CH, DH))
        dbg = jnp.zeros((CH, DH), F32)
        rsum = lambda x: jnp.sum(x, axis=-1, keepdims=True)
        hs = range(HEADS)
        sls = [slice(h * DH, (h + 1) * DH) for h in hs]
        qs = [q_ref[:, sl] for sl in sls]
        ks = [k_ref[:, sl] for sl in sls]
        vecs = [_head_vectors(bg, bgt, h) for h in hs]
        decs = [_decay(gcol, grow) for _, gcol, grow in vecs]
        ths = [t_ref[0, h] for h in hs]
        drus = [_dot(th, du_ref[:, sl], TN, P_BWD) for th, sl in zip(ths, sls)]
        drws = [_dot(th, dw_ref[:, sl], TN, P_BWD) for th, sl in zip(ths, sls)]
        kks = [_dot(kh, kh, NT, P_GRAM) for kh in ks]
        da1 = [_dot(dru, u_ref[:, sl], NT, P_BWD) for dru, sl in zip(drus, sls)]
        da2 = [_dot(drw, w_ref[:, sl], NT, P_BWD) for drw, sl in zip(drws, sls)]
        das = [jnp.where(i > j, -(x + y), 0.0) for x, y in zip(da1, da2)]
        dkks = [da * bcol * dec for da, (bcol, _, _), dec in zip(das, vecs, decs)]
        dqks = [dp_ref[0, h] * dec for h, dec in zip(hs, decs)]
        dq_ps = [_dot(dqk, kh, NN, P_BWD) for dqk, kh in zip(dqks, ks)]
        dk_ps = [_dot(dqk, qh, TN, P_BWD) for dqk, qh in zip(dqks, qs)]
        dk_as = [_dot(dkk, kh, NN, P_BWD) for dkk, kh in zip(dkks, ks)]
        dk_bs = [_dot(dkk, kh, TN, P_BWD) for dkk, kh in zip(dkks, ks)]
        for h, sl in enumerate(sls):
            qh, kh, vh = qs[h], ks[h], v_ref[:, sl]
            bcol, gcol, _ = vecs[h]
            dec, dru, drw, da, kk = decs[h], drus[h], drws[h], das[h], kks[h]
            gam = jnp.exp(gcol)
            glast = gcol[CH - 1:CH, :]
            e = jnp.exp(glast - gcol)
            kg = kh * gam
            dv_ref[:, sl] = bcol * dru
            dbeta = rsum(dru * vh) + rsum(drw * kg) + rsum(da * kk * dec)
            dgc = rsum(drw * kg) * bcol
            dqg = dqg_ref[:, sl]
            dksh = dks_ref[:, sl]
            dq_ref[:, sl] = gam * dqg + dq_ps[h]
            dk_ref[:, sl] = (bcol * gam) * drw + dk_ps[h] + dk_as[h] + dk_bs[h] + dksh * e
            tk = rsum(dksh * kh) * e
            mdec = da * (bcol * kk * dec) + dp_ref[0, h] * p_ref[0, h]
            col = rsum(jnp.where(i == j, jnp.sum(mdec, axis=0, keepdims=True), 0.0))
            dgc = dgc + rsum(mdec) - col
            dgc = dgc + rsum(dqg * qh) * gam - tk
            dglast = jnp.sum(tk, axis=0, keepdims=True) + dgam_all[0:1, h:h + 1] * jnp.exp(glast)
            dgc = dgc + jnp.where(rows1 == CH - 1, dglast, 0.0)
            dbg = dbg + jnp.where(lane == h, dbeta, 0.0) + jnp.where(lane == HEADS + h, dgc, 0.0)
        dbg_ref[...] = dbg

    row = pl.BlockSpec((CH, GW), lambda c: (c, 0))
    sq = pl.BlockSpec((1, HEADS, CH, CH), lambda c: (c, 0, 0, 0))
    small = pl.BlockSpec((CH, DH), lambda c: (c, 0))
    big = jax.ShapeDtypeStruct((n, GW), F32)
    return pl.pallas_call(
        body, name="gdn_intra_bwd", grid=(nch,),
        in_specs=[row, row, row, small, pl.BlockSpec((1, DH, CH), lambda c: (c, 0, 0)), sq, row, row, sq,
                  row, sq, row, row, row, pl.BlockSpec((1, 8, DH), lambda c: (c, 0, 0))],
        out_specs=[row, row, row, small],
        out_shape=[big, big, big, jax.ShapeDtypeStruct((n, DH), F32)],
        compiler_params=_params(("parallel",)),
    )(q, k, v, bg, bgt, t, u, w, p, dqg, dp, du, dw, dks, dgam)


def _local_step(x, tgt, h, w_g, cqw, late, norm_in_w, ad, gdn_norm_w, conv_b, final_norm_w,
                on_grad_c=None, on_grad_g=None):
    proj_g = _matmul(h, w_g, NT, F32, 512, 1408, 1024, "mm_proj_g", n=GW_COLS)
    q, k, v = _prep_qkv(proj_g, cqw)
    bg, bgt = _prep_bg(proj_g, ad)
    u, w, p, t = _gdn_intra(q, k, v, bg, bgt)
    o, vn, s_in = _gdn_scan(q, k, bg, u, w, p)
    w_c, w_out, conv_w = late(o)
    proj_c = _matmul(h, w_c, NT, F32, 512, 1024, 1024, "mm_proj_c", n=CW_COLS)
    mix = _conv_branch(proj_c, conv_w, conv_b, _gdn_out(o, proj_g, gdn_norm_w))
    out = _matmul(mix, w_out, NN, F32, 512, 512, 2048, "mm_out", add=x)
    dout, dout_b, g_fn, loss = _final_loss(out, tgt, final_norm_w)

    dmix = _matmul(dout_b, w_out, NT, F32, 512, 1024, 1024, "mm_dmix")
    g_wout = _matmul(mix, dout_b, TN, BF16, 512, 512, 2048, "mm_gwout")
    do, dproj_g, g_gn = _gdn_out_bwd(o, proj_g, gdn_norm_w, dmix)
    dproj_c, g_cw, g_cb = _conv_branch_bwd(proj_c, conv_w, conv_b, dmix)
    g_c = _matmul(dproj_c, h, TN, BF16, 1024, 512, 2048, "mm_gwin_c")
    if on_grad_c is not None:
        do = on_grad_c(g_c, g_wout, do)
    dqg, dp, du, dw, dks, dgam = _gdn_scan_bwd(q, k, bg, w, p, vn, s_in, do)
    dq, dk, dv, dbg = _gdn_intra_bwd(q, k, v, bg, bgt, t, u, w, p, dqg, dp, du, dw, dks, dgam)
    dproj_g, gq, gk, gv = _prep_qkv_bwd(proj_g, cqw, dq, dk, dv, dproj_g)
    dproj_g, g_al, g_dt = _prep_bg_bwd(proj_g, ad, dbg, dproj_g)
    g_g = _matmul(dproj_g, h, TN, BF16, 1408, 512, 2048, "mm_gwin_g")
    if on_grad_g is not None:
        dproj_g = on_grad_g(g_g, dproj_g)
    dh = _matmul(dproj_g, w_g, NN, F32, 512, 1024, 1408, "mm_dh_g")
    dh = _matmul(dproj_c, w_c, NN, F32, 512, 1024, 1024, "mm_dh_c", add=dh)
    gx, g_nin = _rms_in_bwd(x, norm_in_w, dh, dout)
    small = dict(nin=g_nin, cb=g_cb, fn=g_fn, al=g_al, dt=g_dt, gn=g_gn, cq=(gq, gk, gv), cw=g_cw, loss=loss)
    return gx, small, (g_g, g_c, g_wout)


def _place():
    x, y, c = lax.axis_index("x"), lax.axis_index("y"), lax.axis_index("c")
    chips = [(1 - x, y), (x, 1 - y), (1 - x, 1 - y)]
    return x, y, c, chips


def _blk(ref, b):
    if isinstance(b, int):
        return ref.at[b * DH:(b + 1) * DH, :]
    return ref.at[pl.ds(pl.multiple_of(b * DH, DH), DH), :]


HBM = pl.BlockSpec(memory_space=pltpu.HBM)
SEM = pl.BlockSpec(memory_space=pltpu.SEMAPHORE)
EFFECT = pltpu.SideEffectType.DATAFLOW_SIDE_EFFECTING


def _split_start(name, issue, bufs, n_sems):
    nbuf = len(bufs)

    def body(*refs):
        issue(refs[:nbuf], refs[nbuf], refs[nbuf + 1])
        refs[-1][...] = jnp.zeros_like(refs[-1])

    out = pl.pallas_call(
        body, name=name,
        out_shape=(pltpu.SemaphoreType.DMA((n_sems,)), pltpu.SemaphoreType.DMA((n_sems,)),
                   *[pltpu.HBM(b.shape, b.dtype) for b in bufs], jax.ShapeDtypeStruct((8, DH), F32)),
        in_specs=[HBM] * nbuf,
        out_specs=(SEM, SEM, *[HBM] * nbuf, pl.BlockSpec(memory_space=pltpu.VMEM)),
        input_output_aliases={a: 2 + a for a in range(nbuf)},
        compiler_params=pltpu.CompilerParams(has_side_effects=EFFECT),
    )(*[pltpu.with_memory_space_constraint(b, pltpu.HBM) for b in bufs])
    return out[0], out[1], list(out[2:2 + nbuf]), out[-1]


def _split_wait(name, await_, send_sems, recv_sems, bufs, after):
    nbuf = len(bufs)

    def body(*refs):
        await_(refs[:nbuf], refs[nbuf], refs[nbuf + 1])

    out = pl.pallas_call(
        body, name=name,
        out_shape=tuple(pltpu.HBM(b.shape, b.dtype) for b in bufs),
        in_specs=[HBM] * nbuf + [SEM, SEM, ANY], out_specs=tuple([HBM] * nbuf),
        input_output_aliases={a: a for a in range(nbuf)},
        compiler_params=pltpu.CompilerParams(has_side_effects=EFFECT),
    )(*bufs, send_sems, recv_sems, after)
    return list(out)


def _phase_blocks(chip, phase, edges):
    return [(b, blk) for b, (grp, blk) in enumerate(_shard_blocks(chip, edges)) if grp == phase]


def _cols(ref, nblk):
    return ref.at[0:nblk * DH, :]


def _block_table(chip, edges, spare_g, spare_c):
    rows = []
    for s in range(4):
        sb = _shard_blocks(s, edges)
        rows.append([[blk if grp == "g" else spare_g for grp, blk in sb],
                     [blk if grp == "c" else spare_c for grp, blk in sb],
                     [int(grp == "g") for grp, _ in sb], [s] * ALIGNED_BLOCKS])
    return jnp.asarray(rows, jnp.int32)[chip]


def _place_own(a_shard, wo, cq, cw):
    d = a_shard.shape[1]
    chip = 2 * lax.axis_index("x") + lax.axis_index("y")

    def body(t_ref, a_ref, wo_ref, cq_ref, cw_ref, wg_ref, wc_ref, wog_ref, cqg_ref, cwg_ref):
        wg_ref[...] = a_ref[...]
        wc_ref[...] = a_ref[...]

        @pl.when(pl.program_id(0) == 0)
        def _():
            wog_ref[0] = wo_ref[...]
            cqg_ref[0] = cq_ref[...]
            cwg_ref[0] = cw_ref[...]

    whole = lambda s: pl.BlockSpec(s.shape, lambda b, t: (0,) * s.ndim)
    slot = lambda s: pl.BlockSpec((1,) + s.shape, lambda b, t: (t[3, 0],) + (0,) * s.ndim)
    return pl.pallas_call(
        body, name="place_own",
        grid_spec=pltpu.PrefetchScalarGridSpec(
            num_scalar_prefetch=1, grid=(ALIGNED_BLOCKS,),
            in_specs=[pl.BlockSpec((DH, d), lambda b, t: (b, 0)), whole(wo), whole(cq), whole(cw)],
            out_specs=[pl.BlockSpec((DH, d), lambda b, t: (t[0, b], 0)),
                       pl.BlockSpec((DH, d), lambda b, t: (t[1, b], 0)), slot(wo), slot(cq), slot(cw)]),
        out_shape=[jax.ShapeDtypeStruct((WG_BLOCKS * DH, d), a_shard.dtype),
                   jax.ShapeDtypeStruct((WC_BLOCKS * DH, d), a_shard.dtype)]
        + [jax.ShapeDtypeStruct((4,) + s.shape, s.dtype) for s in (wo, cq, cw)],
        compiler_params=_params(("arbitrary",)),
    )(_block_table(chip, True, G_SPARE, C_SPARE), a_shard, wo, cq, cw)


def _tie(x, token, name):
    def body(x_ref, t_ref, o_ref):
        del x_ref, t_ref, o_ref

    return pl.pallas_call(
        body, name=name, in_specs=[ANY, ANY], out_specs=ANY,
        out_shape=jax.ShapeDtypeStruct(x.shape, x.dtype), input_output_aliases={0: 0},
    )(x, token)


def _gather_start(phase, a_shard, w_grp, singles):
    ns = len(singles)

    def issue(refs, send_sems, recv_sems):
        a_ref, w_ref = refs[0], refs[1]
        x, y, c, chips = _place()
        mine = 2 * x + y
        for jj, (px, py) in enumerate(chips):
            to = dict(device_id=(px, py, c), device_id_type=MESH)
            for a in range(ns):
                pltpu.make_async_remote_copy(
                    src_ref=refs[2 + 2 * a], dst_ref=refs[3 + 2 * a].at[mine],
                    send_sem=send_sems.at[(1 + ns) * jj + 1 + a], recv_sem=recv_sems.at[(1 + ns) * jj + 1 + a],
                    **to).start()
        for s in range(4):
            blocks = _phase_blocks(s, phase, True)
            if blocks:
                @pl.when(mine == s)
                def _():
                    for jj, (px, py) in enumerate(chips):
                        for b, blk in blocks:
                            pltpu.make_async_remote_copy(
                                src_ref=_blk(a_ref, b), dst_ref=_blk(w_ref, blk),
                                send_sem=send_sems.at[(1 + ns) * jj], recv_sem=recv_sems.at[(1 + ns) * jj],
                                device_id=(px, py, c), device_id_type=MESH).start()

    bufs = [a_shard, w_grp] + [t for pair in singles for t in pair]
    return _split_start("gather_start_" + phase, issue, bufs, 3 * (1 + ns))


def _gather_wait(phase, send_sems, recv_sems, bufs, after):
    ns = (len(bufs) - 2) // 2

    def await_(refs, send_sems, recv_sems):
        a_ref, w_ref = refs[0], refs[1]
        x, y, c, chips = _place()
        mine = 2 * x + y
        for jj, (px, py) in enumerate(chips):
            to = dict(device_id=(px, py, c), device_id_type=MESH)
            peer = 2 * px + py
            for a in range(ns):
                cp = pltpu.make_async_remote_copy(
                    src_ref=refs[2 + 2 * a], dst_ref=refs[3 + 2 * a].at[mine],
                    send_sem=send_sems.at[(1 + ns) * jj + 1 + a], recv_sem=recv_sems.at[(1 + ns) * jj + 1 + a], **to)
                cp.wait_recv()
                cp.wait_send()
            for s in range(4):
                nblk = len(_phase_blocks(s, phase, True))
                if nblk:
                    both = pltpu.make_async_remote_copy(
                        src_ref=_cols(a_ref, nblk), dst_ref=_cols(w_ref, nblk),
                        send_sem=send_sems.at[(1 + ns) * jj], recv_sem=recv_sems.at[(1 + ns) * jj], **to)

                    @pl.when(peer == s)
                    def _():
                        both.wait_recv()

                    @pl.when(mine == s)
                    def _():
                        both.wait_send()

    return _split_wait("gather_wait_" + phase, await_, send_sems, recv_sems, bufs, after)


def _merge_edges(w, edge0, mixed, name):
    d = w.shape[1]

    def body(e_ref, o_ref):
        o_ref[...] = e_ref[0:DH, :] + e_ref[DH:2 * DH, :]

    def to_block(i):
        r = mixed[-1]
        for kk in range(len(mixed) - 2, -1, -1):
            r = jnp.where(i == kk, mixed[kk], r)
        return r

    return pl.pallas_call(
        body, name=name, grid=(len(mixed),),
        in_specs=[pl.BlockSpec((2 * DH, d), lambda i: (edge0 // 2 + i, 0))],
        out_specs=pl.BlockSpec((DH, d), lambda i: (to_block(i), 0)),
        out_shape=jax.ShapeDtypeStruct(w.shape, w.dtype),
        input_output_aliases={0: 0},
        compiler_params=_params(("arbitrary",)),
    )(w)


def _scatter_start(phase, g_grp, land, singles):
    ns = len(singles)

    def issue(refs, send_sems, recv_sems):
        g_ref, land_ref = refs[0], refs[1]
        x, y, c, chips = _place()
        for jj, (px, py) in enumerate(chips):
            to = dict(device_id=(px, py, c), device_id_type=MESH)
            peer = 2 * px + py
            for a in range(ns):
                pltpu.make_async_remote_copy(
                    src_ref=refs[2 + 2 * a].at[peer], dst_ref=refs[3 + 2 * a].at[jj],
                    send_sem=send_sems.at[(1 + ns) * jj + 1 + a], recv_sem=recv_sems.at[(1 + ns) * jj + 1 + a],
                    **to).start()
            for s in range(4):
                blocks = _phase_blocks(s, phase, False)
                if blocks:
                    @pl.when(peer == s)
                    def _():
                        for b, blk in blocks:
                            pltpu.make_async_remote_copy(
                                src_ref=_blk(g_ref, blk), dst_ref=_blk(land_ref.at[jj], b),
                                send_sem=send_sems.at[(1 + ns) * jj], recv_sem=recv_sems.at[(1 + ns) * jj],
                                **to).start()

    bufs = [g_grp, land] + [t for pair in singles for t in pair]
    return _split_start("scatter_start_" + phase, issue, bufs, 3 * (1 + ns))


def _scatter_wait(phase, send_sems, recv_sems, bufs, after):
    ns = (len(bufs) - 2) // 2

    def await_(refs, send_sems, recv_sems):
        g_ref, land_ref = refs[0], refs[1]
        x, y, c, chips = _place()
        mine = 2 * x + y
        for jj, (px, py) in enumerate(chips):
            to = dict(device_id=(px, py, c), device_id_type=MESH)
            peer = 2 * px + py
            for a in range(ns):
                cp = pltpu.make_async_remote_copy(
                    src_ref=refs[2 + 2 * a].at[peer], dst_ref=refs[3 + 2 * a].at[jj],
                    send_sem=send_sems.at[(1 + ns) * jj + 1 + a], recv_sem=recv_sems.at[(1 + ns) * jj + 1 + a], **to)
                cp.wait_recv()
                cp.wait_send()
            for s in range(4):
                nblk = len(_phase_blocks(s, phase, False))
                if nblk:
                    both = pltpu.make_async_remote_copy(
                        src_ref=_cols(g_ref, nblk), dst_ref=_cols(land_ref.at[jj], nblk),
                        send_sem=send_sems.at[(1 + ns) * jj], recv_sem=recv_sems.at[(1 + ns) * jj], **to)

                    @pl.when(mine == s)
                    def _():
                        both.wait_recv()

                    @pl.when(peer == s)
                    def _():
                        both.wait_send()

    return _split_wait("scatter_wait_" + phase, await_, send_sems, recv_sems, bufs, after)


def _sum_shard(g_g, g_c, land):
    d = g_g.shape[1]
    chip = 2 * lax.axis_index("x") + lax.axis_index("y")

    def body(t_ref, gg_ref, gc_ref, land_ref, o_ref):
        own = jnp.where(t_ref[2, pl.program_id(0)] == 1, gg_ref[...].astype(F32), gc_ref[...].astype(F32))
        for jj in range(3):
            own = own + land_ref[jj].astype(F32)
        o_ref[...] = own

    return pl.pallas_call(
        body, name="sum_w_in",
        grid_spec=pltpu.PrefetchScalarGridSpec(
            num_scalar_prefetch=1, grid=(ALIGNED_BLOCKS,),
            in_specs=[pl.BlockSpec((DH, d), lambda b, t: (t[0, b], 0)), pl.BlockSpec((DH, d), lambda b, t: (t[1, b], 0)),
                      pl.BlockSpec((3, DH, d), lambda b, t: (0, b, 0))],
            out_specs=pl.BlockSpec((DH, d), lambda b, t: (b, 0))),
        out_shape=jax.ShapeDtypeStruct((ALIGNED_W, d), F32),
        compiler_params=_params(("arbitrary",)),
    )(_block_table(chip, False, 0, 0), g_g, g_c, land)


def _sum_rows(stack, land, rows):
    _, r, d = stack.shape
    rows = min(rows, r)
    chip = 2 * lax.axis_index("x") + lax.axis_index("y")

    def body(t_ref, own_ref, land_ref, o_ref):
        acc = own_ref[0].astype(F32)
        for jj in range(3):
            acc = acc + land_ref[jj].astype(F32)
        o_ref[...] = acc

    return pl.pallas_call(
        body, name="sum_w_out",
        grid_spec=pltpu.PrefetchScalarGridSpec(
            num_scalar_prefetch=1, grid=(r // rows,),
            in_specs=[pl.BlockSpec((1, rows, d), lambda i, t: (t[0], i, 0)),
                      pl.BlockSpec((3, rows, d), lambda i, t: (0, i, 0))],
            out_specs=pl.BlockSpec((rows, d), lambda i, t: (i, 0))),
        out_shape=jax.ShapeDtypeStruct((r, d), F32),
        compiler_params=_params(("arbitrary",)),
    )(jnp.reshape(chip, (1,)).astype(jnp.int32), stack, land)


def _final_exchange(parts, pack):
    npart = len(parts)

    def body(*refs):
        ins, pack_ref = refs[:npart], refs[npart]
        outs, packs = refs[npart + 1:2 * npart + 1], refs[2 * npart + 1]
        send_sems, recv_sems, psend, precv, loc_sem = refs[2 * npart + 2:]
        x, y, c, _ = _place()
        me = 4 * x + 2 * y + c
        local = pltpu.make_async_copy(pack_ref, packs.at[me], loc_sem)
        local.start()
        cps = [pltpu.make_async_remote_copy(
            src_ref=ins[a], dst_ref=outs[a], send_sem=send_sems.at[a], recv_sem=recv_sems.at[a],
            device_id=(x, y, 1 - c), device_id_type=MESH) for a in range(npart)]
        for r in range(1, 8):
            dx, dy, dc = (r >> 2) & 1, (r >> 1) & 1, r & 1
            peer = (x + dx - 2 * x * dx, y + dy - 2 * y * dy, c + dc - 2 * c * dc)
            cps.append(pltpu.make_async_remote_copy(
                src_ref=pack_ref, dst_ref=packs.at[me], send_sem=psend.at[r - 1], recv_sem=precv.at[r - 1],
                device_id=peer, device_id_type=MESH))
        for cp in cps:
            cp.start()
        for cp in cps:
            cp.wait_recv()
        for cp in cps:
            cp.wait_send()
        local.wait()

    return pl.pallas_call(
        body, name="final_exchange",
        in_specs=[ANY] * (npart + 1), out_specs=[ANY] * (npart + 1),
        out_shape=[jax.ShapeDtypeStruct(p.shape, p.dtype) for p in parts]
        + [jax.ShapeDtypeStruct((8,) + pack.shape, pack.dtype)],
        scratch_shapes=[pltpu.SemaphoreType.DMA((npart,)), pltpu.SemaphoreType.DMA((npart,)),
                        pltpu.SemaphoreType.DMA((7,)), pltpu.SemaphoreType.DMA((7,)), pltpu.SemaphoreType.DMA],
    )(*parts, pack)


def _sum_packs(packs):
    def body(p_ref, o_ref):
        acc = p_ref[0]
        for d in range(1, 8):
            acc = acc + p_ref[d]
        o_ref[...] = acc

    return pl.pallas_call(
        body, name="sum_packs", out_shape=jax.ShapeDtypeStruct(packs.shape[1:], F32),
    )(packs)


def _adamw_update(g, w_ref, m_ref, v_ref, go, do, mo, vo):
    c1 = 1.0 / (1.0 - ADAM_B1 ** ADAM_STEP)
    c2 = 1.0 / (1.0 - ADAM_B2 ** ADAM_STEP)
    mn = ADAM_B1 * m_ref[...] + (1.0 - ADAM_B1) * g
    vn = ADAM_B2 * v_ref[...] + (1.0 - ADAM_B2) * (g * g)
    go[...] = g
    mo[...] = mn
    vo[...] = vn
    do[...] = -ADAM_LR * ((mn * c1) / (jnp.sqrt(vn * c2) + ADAM_EPS) + ADAM_WD * w_ref[...])


def _adamw(w, m, v, g1, g2, rows, name):
    r, cdim = w.shape
    rows = min(rows, r)

    def body(*refs):
        n_in = 4 if g2 is None else 5
        w_ref, m_ref, v_ref, g_ref = refs[:4]
        g = g_ref[...] if g2 is None else g_ref[...] + refs[4][...]
        _adamw_update(g, w_ref, m_ref, v_ref, *refs[n_in:n_in + 4])

    blk = pl.BlockSpec((rows, cdim), lambda i: (i, 0))
    args = [w, m, v, g1] + ([] if g2 is None else [g2])
    shp = jax.ShapeDtypeStruct((r, cdim), F32)
    return pl.pallas_call(
        body, name=name, grid=(r // rows,),
        in_specs=[blk] * len(args), out_specs=[blk] * 4, out_shape=[shp] * 4,
        compiler_params=_params(("parallel",), 20 * rows * cdim * 4 + 8 * 2**20),
    )(*args)


def _adamw_shard(wt, mt, vt, g1, g2):
    r, d = wt.shape
    cols = min(128, d)

    def body(w_ref, m_ref, v_ref, g_ref, g2_ref, go, do, mo, vo, pad_ref):
        chip = 2 * lax.axis_index("x") + lax.axis_index("y")
        back = [(ALIGNED_W - s) % ALIGNED_W for s in SHIFTS]
        pad_ref[...] = pltpu.roll(g_ref[...] + g2_ref[...], _by_chip(chip, back), 0)
        _adamw_update(pad_ref[0:r, :], w_ref, m_ref, v_ref, go, do, mo, vo)

    blk = pl.BlockSpec((r, cols), lambda i: (0, i))
    gblk = pl.BlockSpec((ALIGNED_W, cols), lambda i: (0, i))
    shp = jax.ShapeDtypeStruct((r, d), F32)
    return pl.pallas_call(
        body, name="adamw_w_in", grid=(d // cols,),
        in_specs=[blk] * 3 + [gblk] * 2, out_specs=[blk] * 4, out_shape=[shp] * 4,
        scratch_shapes=[pltpu.VMEM((ALIGNED_W, cols), F32)],
        compiler_params=_params(("parallel",), 24 * ALIGNED_W * cols * 4 + 8 * 2**20),
    )(wt, mt, vt, g1, g2)


def _pad_lanes(a, width):
    return jnp.pad(a, ((0, 0), (0, width - a.shape[1])))


def _gathered_to_full(g):
    return jnp.transpose(g, (1, 0, 2)).reshape(g.shape[1], 4 * g.shape[2])


def _row(a):
    return _pad_lanes(a.reshape(1, -1), 1024)


def _small_pack(nin, cb, fn, al, dt, gn, cqw_shard, cw_shard):
    ad = jnp.concatenate([al.reshape(1, -1), dt.reshape(1, -1)], axis=1)
    rows = [_row(nin), _row(cb), _row(fn), _row(ad), _row(gn), cqw_shard.reshape(3, 1024), _row(cw_shard)]
    out = jnp.concatenate(rows, axis=0)
    return jnp.pad(out, ((0, 16 - out.shape[0]), (0, 0)))


def kernel(x, norm_in_w, w_in, conv_qkv_w, A_log, dt_bias, gdn_norm_w, conv_w, conv_b, w_out, final_norm_w, loss_target, m_norm_in_w, m_w_in, m_conv_qkv_w, m_A_log, m_dt_bias, m_gdn_norm_w, m_conv_w, m_conv_b, m_w_out, m_final_norm_w, v_norm_in_w, v_w_in, v_conv_qkv_w, v_A_log, v_dt_bias, v_gdn_norm_w, v_conv_w, v_conv_b, v_w_out, v_final_norm_w):
    chip = 2 * lax.axis_index("x") + lax.axis_index("y")
    a_shard = _align_shard(jnp.transpose(w_in[0]))
    wo_b = _cast_bf16(w_out[0], 256, "cast_w_out")
    wg0, wc0, wog0, cqg0, cwg0 = _place_own(a_shard, wo_b, conv_qkv_w[0], conv_w[0])
    ss_g, rs_g, bufs_g, tok_g = _gather_start("g", a_shard, wg0, [(conv_qkv_w[0], cqg0)])
    ss_c, rs_c, bufs_c, tok_c = _gather_start("c", bufs_g[0], wc0, [(conv_w[0], cwg0), (wo_b, wog0)])
    x0 = x[0]
    h = _tie(_tie(_rms_in(x0, norm_in_w), tok_g, "after_gather_start_g"), tok_c, "after_gather_start_c")
    a_thru, wg, _, cq_g = _gather_wait("g", ss_g, rs_g, [bufs_c[0]] + bufs_g[1:], h)
    w_g = _merge_edges(wg, G_EDGE, G_MIXED, "merge_edges_g")
    cqw = _gathered_to_full(cq_g)
    ad = jnp.pad(jnp.concatenate([A_log, dt_bias], axis=0), ((0, 0), (A_LANE, 0)))
    d_model = x.shape[-1]

    def late(o):
        _, wc, _, cw_g, _, wo_g = _gather_wait("c", ss_c, rs_c, [a_thru] + bufs_c[1:], o)
        return (_merge_edges(wc, C_EDGE, C_MIXED, "merge_edges_c"), wo_g.reshape(2 * GW, d_model),
                _gathered_to_full(cw_g))

    scat = {}

    def on_grad_c(g_c, g_wout, do):
        go4 = g_wout.reshape(4, GW // 2, d_model)
        land = lax.empty((3, ALIGNED_W, d_model), BF16)
        land_o = lax.empty((3, GW // 2, d_model), BF16)
        ss, rs, bufs, tok = _scatter_start("c", g_c, land, [(go4, land_o)])
        scat["c"] = (ss, rs, bufs)
        return _tie(do, tok, "after_scatter_start_c")

    def on_grad_g(g_g, dproj_g):
        ss, rs, bufs, tok = _scatter_start("g", g_g, scat["c"][2][1], [])
        scat["g"] = (ss, rs, bufs)
        return _tie(dproj_g, tok, "after_scatter_start_g")

    gx, sm, _ = _local_step(x0, loss_target[0], h, w_g, cqw, late, norm_in_w, ad, gdn_norm_w, conv_b,
                            final_norm_w.reshape(1, -1), on_grad_c, on_grad_g)

    ss, rs, bufs = scat["g"]
    g_g, land = _scatter_wait("g", ss, rs, bufs, gx)
    ss, rs, bufs = scat["c"]
    g_c, land, go4, land_o = _scatter_wait("c", ss, rs, [bufs[0], land, bufs[2], bufs[3]], gx)
    part_in = _sum_shard(g_g, g_c, land)
    part_out = _sum_rows(go4, land_o, 128)
    ad_g = jnp.concatenate([sm["al"][:, A_LANE:], sm["dt"][:, A_LANE:]], axis=1)
    pack = jnp.concatenate([_row(sm["nin"]), _row(sm["cb"]), _row(sm["fn"]), _row(ad_g), _row(sm["gn"]),
                            jnp.concatenate(sm["cq"], axis=1).reshape(12, 1024), sm["cw"], _row(sm["loss"])], axis=0)
    pack = jnp.pad(pack, ((0, PACK_ROWS - pack.shape[0]), (0, 0)))
    sib_in, sib_out, packs = _final_exchange([part_in, part_out], pack)
    tot = _sum_packs(packs)

    g_wi, d_wi, m_wi, v_wi = [jnp.transpose(a) for a in _adamw_shard(
        jnp.transpose(w_in[0]), jnp.transpose(m_w_in[0]), jnp.transpose(v_w_in[0]), part_in, sib_in)]
    g_wo, d_wo, m_wo, v_wo = _adamw(w_out[0], m_w_out[0], v_w_out[0], part_out, sib_out, 128, "adamw_w_out")
    g_cq_sh = lax.dynamic_slice_in_dim(tot[R_CQ:R_CQ + 12].reshape(4, 3 * GW), chip * 768, 768, axis=1)
    g_cw_sh = lax.dynamic_slice_in_dim(tot[R_CW:R_CW + 3], chip * 256, 256, axis=1)
    sp = lambda nin, cb, fn, al, dt, gn, cq, cwv: _small_pack(nin, cb, fn, al, dt, gn, cq[0], cwv[0])
    g_s = _small_pack(tot[R_NIN], tot[R_CB], tot[R_FN], tot[R_AD, :HEADS], tot[R_AD, HEADS:2 * HEADS],
                      tot[R_GN, :DH], g_cq_sh, g_cw_sh)
    w_s = sp(norm_in_w, conv_b, final_norm_w, A_log, dt_bias, gdn_norm_w, conv_qkv_w, conv_w)
    m_s = sp(m_norm_in_w, m_conv_b, m_final_norm_w, m_A_log, m_dt_bias, m_gdn_norm_w, m_conv_qkv_w, m_conv_w)
    v_s = sp(v_norm_in_w, v_conv_b, v_final_norm_w, v_A_log, v_dt_bias, v_gdn_norm_w, v_conv_qkv_w, v_conv_w)
    small = _adamw(w_s, m_s, v_s, g_s, None, 16, "adamw_small")

    def unpack(a, big_in, big_out):
        return (a[0:1], big_in[None], a[5:8].reshape(1, 4, 768), a[3:4, :HEADS], a[3:4, HEADS:2 * HEADS],
                a[4:5, :DH], a[8, :768].reshape(1, 3, 256), a[1:2], big_out[None], a[2])

    loss = tot[R_LOSS, 0]
    return (loss, gx[None], *unpack(small[0], g_wi, g_wo), *unpack(small[1], d_wi, d_wo),
            *unpack(small[2], m_wi, m_wo), *unpack(small[3], v_wi, v_wo))
```

```python
import functools
import math

import jax
import jax.numpy as jnp
from jax import lax
from jax.experimental import pallas as pl
from jax.experimental.pallas import tpu as pltpu

F32 = jnp.float32
BF16 = jnp.bfloat16
MESH = pl.DeviceIdType.MESH
ANY = pl.BlockSpec(memory_space=pl.ANY)

HEADS = 8
DH = 128
CH = 64
GW = HEADS * DH
EPS = 1e-6
VMEM_V7X = 64 * 1024 * 1024

QB, KB, VB, ZB, BAB = 0, 8, 16, 24, 32
A_LANE = 120
NG, NC = 33, 32
GW_COLS, CW_COLS = NG * DH, NC * DH

SHARD_W = 2052
ALIGNED_BLOCKS = 17
ALIGNED_W = ALIGNED_BLOCKS * DH
SHIFTS = (0, 4, ALIGNED_W - 8, ALIGNED_W - 4)
G_EDGE, C_EDGE = 34, 32
G_SPARE, C_SPARE = 33, 34
WG_BLOCKS, WC_BLOCKS = 38, 36
G_MIXED, C_MIXED = (2, BAB), (4 * 7 + 1,)


def _shard_blocks(chip, edges):
    g, c = "g", "c"
    if chip == 0:
        out = [(g, 3 * b) for b in range(8)] + [(g, 3 * b + 1) for b in range(8)] + [(g, G_EDGE, G_MIXED[0])]
    elif chip == 1:
        out = [(g, G_EDGE + 1, G_MIXED[0])] + [(g, 3 * b + 2) for b in range(1, 8)]
        out += [(g, ZB + b) for b in range(8)] + [(g, G_EDGE + 2, G_MIXED[1])]
    elif chip == 2:
        out = [(c, 4 * b) for b in range(8)] + [(c, 4 * b + 1) for b in range(7)]
        out += [(c, C_EDGE, C_MIXED[0]), (g, G_EDGE + 3, G_MIXED[1])]
    else:
        out = [(c, 4 * b + 2) for b in range(8)] + [(c, 4 * b + 3) for b in range(8)] + [(c, C_EDGE + 1, C_MIXED[0])]
    return [(o[0], o[1] if (edges or len(o) == 2) else o[2]) for o in out]


def _by_chip(chip, vals):
    if all(v == vals[0] for v in vals):
        return vals[0]
    r = vals[3]
    for kk in (2, 1, 0):
        r = jnp.where(chip == kk, vals[kk], r)
    return r

ADAM_LR, ADAM_B1, ADAM_B2, ADAM_EPS, ADAM_WD, ADAM_STEP = 0.001, 0.9, 0.999, 1e-08, 0.01, 10

R_NIN, R_CB, R_FN, R_AD, R_GN, R_CQ, R_CW, R_LOSS, PACK_ROWS = 0, 1, 2, 3, 4, 5, 17, 20, 24

NN = ((1,), (0,))
NT = ((1,), (1,))
TN = ((0,), (0,))


def _dot(a, b, dims=NN, mode="lo"):
    dn = (dims, ((), ()))
    if mode == "hi":
        return lax.dot_general(a, b, dn, precision=lax.Precision.HIGHEST, preferred_element_type=F32)
    ah, bh = a.astype(BF16), b.astype(BF16)
    out = lax.dot_general(ah, bh, dn, preferred_element_type=F32)
    if mode == "x3":
        al = (a - ah.astype(F32)).astype(BF16)
        bl = (b - bh.astype(F32)).astype(BF16)
        out = out + lax.dot_general(ah, bl, dn, preferred_element_type=F32)
        out = out + lax.dot_general(al, bh, dn, preferred_element_type=F32)
    return out


P_GRAM, P_INV, P_SOL, P_SCAN, P_SCANB, P_BWD = "lo", "lo", "lo", "lo", "lo", "lo"
P_CUM = "x3"


def _params(sem=None, vmem=None):
    kw = {}
    if sem is not None:
        kw["dimension_semantics"] = sem
    if vmem is not None:
        kw["vmem_limit_bytes"] = int(min(max(vmem, 32 * 2**20), VMEM_V7X - 8 * 2**20))
    return pltpu.CompilerParams(**kw)


def _sigmoid(x):
    return 1.0 / (1.0 + jnp.exp(-x))


def _dsilu(x, s):
    return s * (1.0 + x * (1.0 - s))


def _rows(shape):
    return lax.broadcasted_iota(jnp.int32, shape, 0)


def _shift_down(x, s):
    if s == 0:
        return x
    return jnp.where(_rows(x.shape) >= s, pltpu.roll(x, s, 0), 0.0)


def _shift_up(x, s):
    if s == 0:
        return x
    n = x.shape[0]
    return jnp.where(_rows(x.shape) < n - s, pltpu.roll(x, n - s, 0), 0.0)


def _matmul(a, b, dims, out_dtype, tm, tn, tk, name, add=None, n=None):
    if dims == NN:
        (m, k), n = a.shape, b.shape[1]
    elif dims == NT:
        (m, k), n = a.shape, (n or b.shape[0])
    else:
        (k, m), n = a.shape, b.shape[1]
    tm, tn, tk = min(tm, m), min(tn, n), min(tk, k)
    assert m % tm == 0 and n % tn == 0 and k % tk == 0, (name, m, n, k, tm, tn, tk)
    nk = k // tk

    def body(*refs):
        if add is None:
            a_ref, b_ref, o_ref = refs[:3]
            add_ref = None
        else:
            a_ref, b_ref, add_ref, o_ref = refs[:4]
        part = _dot(a_ref[...], b_ref[...], dims)
        if nk == 1:
            if add_ref is not None:
                part = part + add_ref[...]
            o_ref[...] = part.astype(out_dtype)
            return
        acc = refs[-1]
        kk = pl.program_id(2)

        @pl.when(kk == 0)
        def _():
            acc[...] = part

        @pl.when(kk > 0)
        def _():
            acc[...] += part

        @pl.when(kk == nk - 1)
        def _():
            r = acc[...]
            if add_ref is not None:
                r = r + add_ref[...]
            o_ref[...] = r.astype(out_dtype)

    if dims == TN:
        a_spec = pl.BlockSpec((tk, tm), lambda i, j, kk: (kk, i))
    else:
        a_spec = pl.BlockSpec((tm, tk), lambda i, j, kk: (i, kk))
    if dims == NT:
        b_spec = pl.BlockSpec((tn, tk), lambda i, j, kk: (j, kk))
    else:
        b_spec = pl.BlockSpec((tk, tn), lambda i, j, kk: (kk, j))
    o_spec = pl.BlockSpec((tm, tn), lambda i, j, kk: (i, j))
    in_specs = [a_spec, b_spec]
    args = [a, b]
    if add is not None:
        in_specs.append(o_spec)
        args.append(add)
    osz = jnp.dtype(out_dtype).itemsize
    est = 2 * (tm * tk * a.dtype.itemsize + tk * tn * b.dtype.itemsize + tm * tn * osz)
    est += 3 * tm * tn * 4 + (2 * tm * tn * 4 if add is not None else 0)
    return pl.pallas_call(
        body, name=name, grid=(m // tm, n // tn, nk),
        in_specs=in_specs, out_specs=o_spec,
        out_shape=jax.ShapeDtypeStruct((m, n), out_dtype),
        scratch_shapes=[pltpu.VMEM((tm, tn), F32)] if nk > 1 else [],
        compiler_params=_params(("parallel", "parallel", "arbitrary"), est + 8 * 2**20),
    )(*args)


def _cast_bf16(a, rows, name):
    r, c = a.shape
    rows = min(rows, r)

    def body(a_ref, o_ref):
        o_ref[...] = a_ref[...].astype(BF16)

    return pl.pallas_call(
        body, name=name, grid=(r // rows,),
        in_specs=[pl.BlockSpec((rows, c), lambda i: (i, 0))],
        out_specs=pl.BlockSpec((rows, c), lambda i: (i, 0)),
        out_shape=jax.ShapeDtypeStruct((r, c), BF16),
        compiler_params=_params(("parallel",)),
    )(a)


def _align_shard(wt):
    r, d = wt.shape
    cols = min(256, d)

    def body(w_ref, o_ref, pad_ref):
        chip = 2 * lax.axis_index("x") + lax.axis_index("y")
        pad_ref[...] = jnp.zeros_like(pad_ref)
        pad_ref[0:r, :] = w_ref[...]
        o_ref[...] = pltpu.roll(pad_ref[...], _by_chip(chip, SHIFTS), 0).astype(BF16)

    return pl.pallas_call(
        body, name="align_shard", grid=(d // cols,),
        in_specs=[pl.BlockSpec((r, cols), lambda i: (0, i))],
        out_specs=pl.BlockSpec((ALIGNED_W, cols), lambda i: (0, i)),
        out_shape=jax.ShapeDtypeStruct((ALIGNED_W, d), BF16),
        scratch_shapes=[pltpu.VMEM((ALIGNED_W, cols), F32)],
        compiler_params=_params(("parallel",)),
    )(wt)


def _rms_in(x, w):
    n, d = x.shape
    tr = min(256, n)

    def body(x_ref, w_ref, h_ref):
        xv = x_ref[...]
        r = lax.rsqrt(jnp.mean(xv * xv, axis=-1, keepdims=True) + EPS)
        h_ref[...] = (xv * r * w_ref[...]).astype(BF16)

    return pl.pallas_call(
        body, name="rms_in", grid=(n // tr,),
        in_specs=[pl.BlockSpec((tr, d), lambda i: (i, 0)), pl.BlockSpec((1, d), lambda i: (0, 0))],
        out_specs=pl.BlockSpec((tr, d), lambda i: (i, 0)),
        out_shape=jax.ShapeDtypeStruct((n, d), BF16),
        compiler_params=_params(("parallel",)),
    )(x, w)


def _conv_silu(p, w_ref, taps):
    c = None
    for j in range(taps):
        t = _shift_down(p, taps - 1 - j) * w_ref[j:j + 1, :]
        c = t if c is None else c + t
    return c


def _prep_qkv(proj, cw):
    n = proj.shape[0]

    def body(p3, wq, wk, wv, q_ref, k_ref, v_ref):
        for kind, (w_ref, o_ref) in enumerate(((wq, q_ref), (wk, k_ref), (wv, v_ref))):
            c = _conv_silu(p3[:, kind * DH:(kind + 1) * DH], w_ref, 4)
            a = c * _sigmoid(c)
            if kind < 2:
                r = lax.rsqrt(jnp.sum(a * a, axis=-1, keepdims=True) + EPS)
                a = a * (r * (DH ** -0.5 if kind == 0 else 1.0))
            o_ref[...] = a

    col = pl.BlockSpec((n, DH), lambda h: (0, h))
    wcol = lambda base: pl.BlockSpec((4, DH), lambda h: (0, base + h))
    out = jax.ShapeDtypeStruct((n, GW), F32)
    return pl.pallas_call(
        body, name="prep_qkv", grid=(HEADS,),
        in_specs=[pl.BlockSpec((n, 3 * DH), lambda h: (0, h)), wcol(QB), wcol(KB), wcol(VB)],
        out_specs=[col] * 3, out_shape=[out] * 3,
        compiler_params=_params(("parallel",), 40 * 2**20),
    )(proj, cw, cw, cw)


def _prep_qkv_bwd(proj, cw, dq, dk, dv, dproj):
    n = proj.shape[0]

    def body(p3, wq, wk, wv, dq_ref, dk_ref, dv_ref, _, o3, gq, gk, gv):
        for kind, (w_ref, d_ref, g_ref) in enumerate(((wq, dq_ref, gq), (wk, dk_ref, gk), (wv, dv_ref, gv))):
            p = p3[:, kind * DH:(kind + 1) * DH]
            c = _conv_silu(p, w_ref, 4)
            s = _sigmoid(c)
            a = c * s
            d = d_ref[...]
            if kind < 2:
                r = lax.rsqrt(jnp.sum(a * a, axis=-1, keepdims=True) + EPS)
                sc = DH ** -0.5 if kind == 0 else 1.0
                d = (sc * r) * (d - a * ((r * r) * jnp.sum(d * a, axis=-1, keepdims=True)))
            dc = d * _dsilu(c, s)
            dp = None
            for j in range(4):
                g_ref[j:j + 1, :] = jnp.sum(dc * _shift_down(p, 3 - j), axis=0, keepdims=True)
                t = _shift_up(dc, 3 - j) * w_ref[j:j + 1, :]
                dp = t if dp is None else dp + t
            o3[:, kind * DH:(kind + 1) * DH] = dp.astype(BF16)

    col = pl.BlockSpec((n, DH), lambda h: (0, h))
    wcol = lambda base: pl.BlockSpec((4, DH), lambda h: (0, base + h))
    p3spec = pl.BlockSpec((n, 3 * DH), lambda h: (0, h))
    return pl.pallas_call(
        body, name="prep_qkv_bwd", grid=(HEADS,),
        in_specs=[p3spec, wcol(QB), wcol(KB), wcol(VB), col, col, col, ANY],
        out_specs=[p3spec] + [wcol(0)] * 3,
        out_shape=[jax.ShapeDtypeStruct(dproj.shape, BF16)] + [jax.ShapeDtypeStruct((4, GW), F32)] * 3,
        input_output_aliases={7: 0},
        compiler_params=_params(("parallel",), 48 * 2**20),
    )(proj, cw, cw, cw, dq, dk, dv, dproj)


def _tri(lower_incl):
    i = lax.broadcasted_iota(jnp.int32, (CH, CH), 0)
    j = lax.broadcasted_iota(jnp.int32, (CH, CH), 1)
    return jnp.where(i >= j, 1.0, 0.0) if lower_incl else jnp.where(j >= i, 1.0, 0.0)


def _lane(shape):
    return lax.broadcasted_iota(jnp.int32, shape, 1)


def _prep_bg(proj, ad):
    n = proj.shape[0]
    nch = n // CH

    def body(p_ref, ad_ref, bg_ref, bgt_ref):
        p = p_ref[...]
        lane = _lane(p.shape)
        beta = _sigmoid(p)
        xa = p + ad_ref[1:2, :]
        sp = jnp.maximum(xa, 0.0) + jnp.log(1.0 + jnp.exp(-jnp.abs(xa)))
        g = pltpu.roll(-jnp.exp(ad_ref[0:1, :]) * sp, DH - A_LANE + HEADS, 1)
        gc = _dot(_tri(True), g, NN, P_CUM)
        bg = jnp.where(lane < HEADS, beta, jnp.where(lane < 2 * HEADS, gc, 0.0))
        bg_ref[...] = bg
        bgt_ref[0] = bg.T

    return pl.pallas_call(
        body, name="prep_bg", grid=(nch,),
        in_specs=[pl.BlockSpec((CH, DH), lambda i: (i, BAB)), pl.BlockSpec((2, DH), lambda i: (0, 0))],
        out_specs=[pl.BlockSpec((CH, DH), lambda i: (i, 0)), pl.BlockSpec((1, DH, CH), lambda i: (i, 0, 0))],
        out_shape=[jax.ShapeDtypeStruct((n, DH), F32), jax.ShapeDtypeStruct((nch, DH, CH), F32)],
        compiler_params=_params(("parallel",)),
    )(proj, ad)


def _prep_bg_bwd(proj, ad, dbg, dproj):
    n = proj.shape[0]
    nch = n // CH

    def body(p_ref, ad_ref, d_ref, _, o_ref, ga_ref, gd_ref):
        p = p_ref[...]
        d = d_ref[...]
        lane = _lane(p.shape)
        beta = _sigmoid(p)
        xa = p + ad_ref[1:2, :]
        sp = jnp.maximum(xa, 0.0) + jnp.log(1.0 + jnp.exp(-jnp.abs(xa)))
        na = -jnp.exp(ad_ref[0:1, :])
        dg = pltpu.roll(_dot(_tri(False), d, NN, P_CUM), A_LANE - HEADS, 1)
        da = dg * na * _sigmoid(xa)
        is_g = lane >= A_LANE
        o_ref[...] = jnp.where(lane < HEADS, d * beta * (1.0 - beta), jnp.where(is_g, da, 0.0)).astype(BF16)
        ga = jnp.sum(jnp.where(is_g, dg * na * sp, 0.0), axis=0, keepdims=True)
        gd = jnp.sum(jnp.where(is_g, da, 0.0), axis=0, keepdims=True)

        @pl.when(pl.program_id(0) == 0)
        def _():
            ga_ref[...] = jnp.zeros_like(ga_ref)
            gd_ref[...] = jnp.zeros_like(gd_ref)

        ga_ref[...] += ga
        gd_ref[...] += gd

    one = pl.BlockSpec((1, DH), lambda i: (0, 0))
    return pl.pallas_call(
        body, name="prep_bg_bwd", grid=(nch,),
        in_specs=[pl.BlockSpec((CH, DH), lambda i: (i, BAB)), pl.BlockSpec((2, DH), lambda i: (0, 0)),
                  pl.BlockSpec((CH, DH), lambda i: (i, 0)), ANY],
        out_specs=[pl.BlockSpec((CH, DH), lambda i: (i, BAB)), one, one],
        out_shape=[jax.ShapeDtypeStruct(dproj.shape, BF16), jax.ShapeDtypeStruct((1, DH), F32),
                   jax.ShapeDtypeStruct((1, DH), F32)],
        input_output_aliases={3: 0},
        compiler_params=_params(("arbitrary",)),
    )(proj, ad, dbg, dproj)


def _gdn_out(o, proj, wg):
    n = o.shape[0]

    def body(o_ref, z_ref, w_ref, y_ref):
        ov, z = o_ref[...], z_ref[...]
        r = lax.rsqrt(jnp.mean(ov * ov, axis=-1, keepdims=True) + EPS)
        y_ref[...] = (ov * r * w_ref[...] * (z * _sigmoid(z))).astype(BF16)

    return pl.pallas_call(
        body, name="gdn_out", grid=(HEADS,),
        in_specs=[pl.BlockSpec((n, DH), lambda h: (0, h)), pl.BlockSpec((n, DH), lambda h: (0, ZB + h)),
                  pl.BlockSpec((1, DH), lambda h: (0, 0))],
        out_specs=pl.BlockSpec((n, DH), lambda h: (0, h)),
        out_shape=jax.ShapeDtypeStruct((n, 2 * GW), BF16),
        compiler_params=_params(("parallel",)),
    )(o, proj, wg)


def _gdn_out_bwd(o, proj, wg, dmix):
    n = o.shape[0]

    def body(o_ref, z_ref, w_ref, d_ref, do_ref, dz_ref, gw_ref):
        ov, z, d, w = o_ref[...], z_ref[...], d_ref[...], w_ref[...]
        r = lax.rsqrt(jnp.mean(ov * ov, axis=-1, keepdims=True) + EPS)
        nrm = ov * r
        s = _sigmoid(z)
        dz_ref[...] = (d * (nrm * w) * _dsilu(z, s)).astype(BF16)
        dn_w = d * (z * s)
        gw = jnp.sum(dn_w * nrm, axis=0, keepdims=True)
        dn = dn_w * w
        do_ref[...] = r * (dn - nrm * jnp.mean(dn * nrm, axis=-1, keepdims=True))

        @pl.when(pl.program_id(0) == 0)
        def _():
            gw_ref[...] = jnp.zeros_like(gw_ref)

        gw_ref[...] += gw

    return pl.pallas_call(
        body, name="gdn_out_bwd", grid=(HEADS,),
        in_specs=[pl.BlockSpec((n, DH), lambda h: (0, h)), pl.BlockSpec((n, DH), lambda h: (0, ZB + h)),
                  pl.BlockSpec((1, DH), lambda h: (0, 0)), pl.BlockSpec((n, DH), lambda h: (0, h))],
        out_specs=[pl.BlockSpec((n, DH), lambda h: (0, h)), pl.BlockSpec((n, DH), lambda h: (0, ZB + h)),
                   pl.BlockSpec((1, DH), lambda h: (0, 0))],
        out_shape=[jax.ShapeDtypeStruct((n, GW), F32), jax.ShapeDtypeStruct((n, GW_COLS), BF16),
                   jax.ShapeDtypeStruct((1, DH), F32)],
        compiler_params=_params(("arbitrary",)),
    )(o, proj, wg, dmix)


def _conv_branch(proj, w3, b, mix):
    n = proj.shape[0]

    def body(p4, w_ref, b_ref, _, y_ref):
        u = p4[:, DH:2 * DH] * p4[:, 2 * DH:3 * DH]
        cc = _conv_silu(u, w_ref, 3) + b_ref[...]
        z = p4[:, 3 * DH:4 * DH]
        y_ref[...] = (p4[:, 0:DH] * cc * (z * _sigmoid(z))).astype(BF16)

    return pl.pallas_call(
        body, name="conv_branch", grid=(HEADS,),
        in_specs=[pl.BlockSpec((n, 4 * DH), lambda h: (0, h)), pl.BlockSpec((3, DH), lambda h: (0, h)),
                  pl.BlockSpec((1, DH), lambda h: (0, h)), ANY],
        out_specs=pl.BlockSpec((n, DH), lambda h: (0, HEADS + h)),
        out_shape=jax.ShapeDtypeStruct(mix.shape, BF16),
        input_output_aliases={3: 0},
        compiler_params=_params(("parallel",), 40 * 2**20),
    )(proj, w3, b, mix)


def _conv_branch_bwd(proj, w3, b, dmix):
    n = proj.shape[0]

    def body(p4, w_ref, b_ref, d_ref, o4, gw_ref, gbias_ref):
        gb, gcv, hc, z = p4[:, 0:DH], p4[:, DH:2 * DH], p4[:, 2 * DH:3 * DH], p4[:, 3 * DH:4 * DH]
        d = d_ref[...]
        dgb, dgc, dhc, dzc = (o4.at[:, kk * DH:(kk + 1) * DH] for kk in range(4))
        u = gcv * hc
        cc = _conv_silu(u, w_ref, 3) + b_ref[...]
        s = _sigmoid(z)
        dzc[...] = (d * (gb * cc) * _dsilu(z, s)).astype(BF16)
        dp = d * (z * s)
        dgb[...] = (dp * cc).astype(BF16)
        dcc = dp * gb
        gbias_ref[...] = jnp.sum(dcc, axis=0, keepdims=True)
        du = None
        for j in range(3):
            gw_ref[j:j + 1, :] = jnp.sum(dcc * _shift_down(u, 2 - j), axis=0, keepdims=True)
            t = _shift_up(dcc, 2 - j) * w_ref[j:j + 1, :]
            du = t if du is None else du + t
        dgc[...] = (du * hc).astype(BF16)
        dhc[...] = (du * gcv).astype(BF16)

    p4spec = pl.BlockSpec((n, 4 * DH), lambda h: (0, h))
    return pl.pallas_call(
        body, name="conv_branch_bwd", grid=(HEADS,),
        in_specs=[p4spec, pl.BlockSpec((3, DH), lambda h: (0, h)), pl.BlockSpec((1, DH), lambda h: (0, h)),
                  pl.BlockSpec((n, DH), lambda h: (0, HEADS + h))],
        out_specs=[p4spec, pl.BlockSpec((3, DH), lambda h: (0, h)), pl.BlockSpec((1, DH), lambda h: (0, h))],
        out_shape=[jax.ShapeDtypeStruct((n, CW_COLS), BF16), jax.ShapeDtypeStruct((3, GW), F32),
                   jax.ShapeDtypeStruct((1, GW), F32)],
        compiler_params=_params(("parallel",), 48 * 2**20),
    )(proj, w3, b, dmix)


def _final_loss(out, tgt, wf):
    n, d = out.shape
    tr = min(256, n)

    def body(o_ref, t_ref, w_ref, do_ref, dob_ref, gw_ref, loss_ref):
        ov, w = o_ref[...], w_ref[...]
        r = lax.rsqrt(jnp.mean(ov * ov, axis=-1, keepdims=True) + EPS)
        nrm = ov * r
        e = nrm * w - t_ref[...]
        dy = e * (1.0 / d)
        dn = dy * w
        dout = r * (dn - nrm * jnp.mean(dn * nrm, axis=-1, keepdims=True))
        do_ref[...] = dout
        dob_ref[...] = dout.astype(BF16)

        @pl.when(pl.program_id(0) == 0)
        def _():
            gw_ref[...] = jnp.zeros_like(gw_ref)
            loss_ref[...] = jnp.zeros_like(loss_ref)

        gw_ref[...] += jnp.sum(dy * nrm, axis=0, keepdims=True)
        loss_ref[...] += (0.5 / d) * jnp.sum(jnp.sum(e * e, axis=-1, keepdims=True), axis=0, keepdims=True)

    row = pl.BlockSpec((tr, d), lambda i: (i, 0))
    return pl.pallas_call(
        body, name="final_loss", grid=(n // tr,),
        in_specs=[row, row, pl.BlockSpec((1, d), lambda i: (0, 0))],
        out_specs=[row, row, pl.BlockSpec((1, d), lambda i: (0, 0)), pl.BlockSpec((1, 1), lambda i: (0, 0))],
        out_shape=[jax.ShapeDtypeStruct((n, d), F32), jax.ShapeDtypeStruct((n, d), BF16),
                   jax.ShapeDtypeStruct((1, d), F32), jax.ShapeDtypeStruct((1, 1), F32)],
        compiler_params=_params(("arbitrary",)),
    )(out, tgt, wf)


def _rms_in_bwd(x, w, dh, dout):
    n, d = x.shape
    tr = min(256, n)

    def body(x_ref, w_ref, dh_ref, do_ref, dx_ref, gw_ref):
        xv, dhv = x_ref[...], dh_ref[...]
        r = lax.rsqrt(jnp.mean(xv * xv, axis=-1, keepdims=True) + EPS)
        xn = xv * r
        dxn = dhv * w_ref[...]
        dx_ref[...] = r * (dxn - xn * jnp.mean(dxn * xn, axis=-1, keepdims=True)) + do_ref[...]

        @pl.when(pl.program_id(0) == 0)
        def _():
            gw_ref[...] = jnp.zeros_like(gw_ref)

        gw_ref[...] += jnp.sum(dhv * xn, axis=0, keepdims=True)

    row = pl.BlockSpec((tr, d), lambda i: (i, 0))
    one = pl.BlockSpec((1, d), lambda i: (0, 0))
    return pl.pallas_call(
        body, name="rms_in_bwd", grid=(n // tr,),
        in_specs=[row, one, row, row], out_specs=[row, one],
        out_shape=[jax.ShapeDtypeStruct((n, d), F32), jax.ShapeDtypeStruct((1, d), F32)],
        compiler_params=_params(("arbitrary",)),
    )(x, w, dh, dout)


def _ij():
    i = lax.broadcasted_iota(jnp.int32, (CH, CH), 0)
    j = lax.broadcasted_iota(jnp.int32, (CH, CH), 1)
    return i, j


def _unit_lower_inverse(mats):
    i, j = _ij()
    eye = jnp.where(i == j, 1.0, 0.0)
    same16 = (i // 16) == (j // 16)
    same32 = (i // 32) == (j // 32)
    mm = lambda xs, ys: [_dot(x, y, NN, P_INV) for x, y in zip(xs, ys)]
    n1 = [jnp.where(same16, -a, 0.0) for a in mats]
    n2 = mm(n1, n1)
    n4 = mm(n2, n2)
    n8 = mm(n4, n4)
    t = [eye + x1 + x2 + x3 for x1, x2, x3 in zip(n1, n2, mm(n1, n2))]
    t = [x + y for x, y in zip(t, mm(t, n4))]
    t = [x + y for x, y in zip(t, mm(t, n8))]
    a1 = [jnp.where(same32 & jnp.logical_not(same16), a, 0.0) for a in mats]
    t = [x - y for x, y in zip(t, mm(t, mm(a1, t)))]
    a2 = [jnp.where(same32, 0.0, a) for a in mats]
    t = [x - y for x, y in zip(t, mm(t, mm(a2, t)))]
    return t


def _head_vectors(bg, bgt, h):
    bcol = bg[:, h:h + 1]
    gcol = bg[:, HEADS + h:HEADS + h + 1]
    grow = bgt[HEADS + h:HEADS + h + 1, :]
    return bcol, gcol, grow


def _decay(gcol, grow):
    i, j = _ij()
    return jnp.where(i >= j, jnp.exp(jnp.where(i >= j, gcol - grow, 0.0)), 0.0)


def _gdn_intra(q, k, v, bg, bgt):
    n = q.shape[0]
    nch = n // CH

    def body(q_ref, k_ref, v_ref, bg_ref, bgt_ref, u_ref, w_ref, p_ref, t_ref):
        bg, bgt = bg_ref[...], bgt_ref[0]
        i, j = _ij()
        sls = [slice(h * DH, (h + 1) * DH) for h in range(HEADS)]
        ks = [k_ref[:, sl] for sl in sls]
        vecs = [_head_vectors(bg, bgt, h) for h in range(HEADS)]
        decs = [_decay(gcol, grow) for _, gcol, grow in vecs]
        kks = [_dot(kh, kh, NT, P_GRAM) for kh in ks]
        qks = [_dot(q_ref[:, sl], kh, NT, P_GRAM) for sl, kh in zip(sls, ks)]
        ts = _unit_lower_inverse([jnp.where(i > j, bcol * kk * dec, 0.0)
                                  for (bcol, _, _), kk, dec in zip(vecs, kks, decs)])
        us = [_dot(t, v_ref[:, sl] * bcol, NN, P_SOL) for t, sl, (bcol, _, _) in zip(ts, sls, vecs)]
        ws = [_dot(t, kh * (bcol * jnp.exp(gcol)), NN, P_SOL) for t, kh, (bcol, gcol, _) in zip(ts, ks, vecs)]
        for h, sl in enumerate(sls):
            p_ref[0, h] = qks[h] * decs[h]
            t_ref[0, h] = ts[h]
            u_ref[:, sl] = us[h]
            w_ref[:, sl] = ws[h]

    row = pl.BlockSpec((CH, GW), lambda c: (c, 0))
    sq = pl.BlockSpec((1, HEADS, CH, CH), lambda c: (c, 0, 0, 0))
    big = jax.ShapeDtypeStruct((n, GW), F32)
    sqs = jax.ShapeDtypeStruct((nch, HEADS, CH, CH), F32)
    return pl.pallas_call(
        body, name="gdn_intra", grid=(nch,),
        in_specs=[row, row, row, pl.BlockSpec((CH, DH), lambda c: (c, 0)),
                  pl.BlockSpec((1, DH, CH), lambda c: (c, 0, 0))],
        out_specs=[row, row, sq, sq], out_shape=[big, big, sqs, sqs],
        compiler_params=_params(("parallel",)),
    )(q, k, v, bg, bgt)


def _gdn_scan(q, k, bg, u, w, p):
    n = q.shape[0]
    nch = n // CH

    def body(q_ref, k_ref, bg_ref, u_ref, w_ref, p_ref, o_ref, vn_ref, s_out, s_scr):
        @pl.when(pl.program_id(0) == 0)
        def _():
            s_scr[...] = jnp.zeros_like(s_scr)

        bg = bg_ref[...]
        hs = range(HEADS)
        sls = [slice(h * DH, (h + 1) * DH) for h in hs]
        gcols = [bg[:, HEADS + h:HEADS + h + 1] for h in hs]
        glasts = [g[CH - 1:CH, :] for g in gcols]
        ss = [s_scr[h] for h in hs]
        wss = [_dot(w_ref[:, sl], s, NN, P_SCAN) for sl, s in zip(sls, ss)]
        oqs = [_dot(q_ref[:, sl] * jnp.exp(g), s, NN, P_SCAN) for sl, s, g in zip(sls, ss, gcols)]
        vns = [u_ref[:, sl] - x for sl, x in zip(sls, wss)]
        ops = [_dot(p_ref[0, h], vn, NN, P_SCAN) for h, vn in zip(hs, vns)]
        sns = [_dot(k_ref[:, sl] * jnp.exp(gl - g), vn, TN, P_SCAN)
               for sl, gl, g, vn in zip(sls, glasts, gcols, vns)]
        for h, sl in enumerate(sls):
            s_out[0, :, sl] = ss[h]
            vn_ref[:, sl] = vns[h]
            o_ref[:, sl] = oqs[h] + ops[h]
            s_scr[h] = ss[h] * jnp.exp(glasts[h]) + sns[h]

    row = pl.BlockSpec((CH, GW), lambda c: (c, 0))
    big = jax.ShapeDtypeStruct((n, GW), F32)
    return pl.pallas_call(
        body, name="gdn_scan", grid=(nch,),
        in_specs=[row, row, pl.BlockSpec((CH, DH), lambda c: (c, 0)), row, row,
                  pl.BlockSpec((1, HEADS, CH, CH), lambda c: (c, 0, 0, 0))],
        out_specs=[row, row, pl.BlockSpec((1, DH, GW), lambda c: (c, 0, 0))],
        out_shape=[big, big, jax.ShapeDtypeStruct((nch, DH, GW), F32)],
        scratch_shapes=[pltpu.VMEM((HEADS, DH, DH), F32)],
        compiler_params=_params(("arbitrary",)),
    )(q, k, bg, u, w, p)


def _gdn_scan_bwd(q, k, bg, w, p, vn, s_in, do):
    n = q.shape[0]
    nch = n // CH
    rev = lambda c: nch - 1 - c

    def body(q_ref, k_ref, bg_ref, w_ref, p_ref, vn_ref, s_ref, do_ref,
             dqg_ref, dp_ref, du_ref, dw_ref, dks_ref, dgam_ref, ds_scr):
        @pl.when(pl.program_id(0) == 0)
        def _():
            ds_scr[...] = jnp.zeros_like(ds_scr)

        bg = bg_ref[...]
        lane = _lane((1, DH))
        hs = range(HEADS)
        sls = [slice(h * DH, (h + 1) * DH) for h in hs]
        gcols = [bg[:, HEADS + h:HEADS + h + 1] for h in hs]
        glasts = [g[CH - 1:CH, :] for g in gcols]
        ss = [s_ref[0, :, sl] for sl in sls]
        dss = [ds_scr[h] for h in hs]
        dos = [do_ref[:, sl] for sl in sls]
        vnl = [vn_ref[:, sl] for sl in sls]
        dqgs = [_dot(d, s, NT, P_SCANB) for d, s in zip(dos, ss)]
        dps = [_dot(d, vn, NT, P_SCANB) for d, vn in zip(dos, vnl)]
        dvn1 = [_dot(p_ref[0, h], d, TN, P_SCANB) for h, d in zip(hs, dos)]
        dvn2 = [_dot(k_ref[:, sl] * jnp.exp(gl - g), ds, NN, P_SCANB)
                for sl, gl, g, ds in zip(sls, glasts, gcols, dss)]
        dkss = [_dot(vn, ds, NT, P_SCANB) for vn, ds in zip(vnl, dss)]
        dsq = [_dot(q_ref[:, sl] * jnp.exp(g), d, TN, P_SCANB) for sl, g, d in zip(sls, gcols, dos)]
        dvns = [a + b for a, b in zip(dvn1, dvn2)]
        dws = [_dot(dvn, s, NT, P_SCANB) for dvn, s in zip(dvns, ss)]
        dsw = [_dot(w_ref[:, sl], dvn, TN, P_SCANB) for sl, dvn in zip(sls, dvns)]
        dgam = jnp.zeros((1, DH), F32)
        for h, sl in enumerate(sls):
            dqg_ref[:, sl] = dqgs[h]
            dp_ref[0, h] = dps[h]
            du_ref[:, sl] = dvns[h]
            dw_ref[:, sl] = -dws[h]
            dks_ref[:, sl] = dkss[h]
            tot = jnp.sum(jnp.sum(dss[h] * ss[h], axis=-1, keepdims=True), axis=0, keepdims=True)
            dgam = dgam + jnp.where(lane == h, tot, 0.0)
            ds_scr[h] = dss[h] * jnp.exp(glasts[h]) + dsq[h] - dsw[h]
        dgam_ref[0] = jnp.broadcast_to(dgam, (8, DH))

    row = pl.BlockSpec((CH, GW), lambda c: (rev(c), 0))
    sq = pl.BlockSpec((1, HEADS, CH, CH), lambda c: (rev(c), 0, 0, 0))
    big = jax.ShapeDtypeStruct((n, GW), F32)
    return pl.pallas_call(
        body, name="gdn_scan_bwd", grid=(nch,),
        in_specs=[row, row, pl.BlockSpec((CH, DH), lambda c: (rev(c), 0)), row, sq, row,
                  pl.BlockSpec((1, DH, GW), lambda c: (rev(c), 0, 0)), row],
        out_specs=[row, sq, row, row, row, pl.BlockSpec((1, 8, DH), lambda c: (rev(c), 0, 0))],
        out_shape=[big, jax.ShapeDtypeStruct((nch, HEADS, CH, CH), F32), big, big, big,
                   jax.ShapeDtypeStruct((nch, 8, DH), F32)],
        scratch_shapes=[pltpu.VMEM((HEADS, DH, DH), F32)],
        compiler_params=_params(("arbitrary",)),
    )(q, k, bg, w, p, vn, s_in, do)


def _gdn_intra_bwd(q, k, v, bg, bgt, t, u, w, p, dqg, dp, du, dw, dks, dgam):
    n = q.shape[0]
    nch = n // CH

    def body(q_ref, k_ref, v_ref, bg_ref, bgt_ref, t_ref, u_ref, w_ref, p_ref,
             dqg_ref, dp_ref, du_ref, dw_ref, dks_ref, dgam_ref, dq_ref, dk_ref, dv_ref, dbg_ref):
        bg, bgt = bg_ref[...], bgt_ref[0]
        dgam_all = dgam_ref[0]
        i, j = _ij()
        rows1 = lax.broadcasted_iota(jnp.int32, (CH, 1), 0)
        lane = _lane((CH, DH))
        dbg = jnp.zeros((CH, DH), F32)
        rsum = lambda x: jnp.sum(x, axis=-1, keepdims=True)
        hs = range(HEADS)
        sls = [slice(h * DH, (h + 1) * DH) for h in hs]
        qs = [q_ref[:, sl] for sl in sls]
        ks = [k_ref[:, sl] for sl in sls]
        vecs = [_head_vectors(bg, bgt, h) for h in hs]
        decs = [_decay(gcol, grow) for _, gcol, grow in vecs]
        ths = [t_ref[0, h] for h in hs]
        drus = [_dot(th, du_ref[:, sl], TN, P_BWD) for th, sl in zip(ths, sls)]
        drws = [_dot(th, dw_ref[:, sl], TN, P_BWD) for th, sl in zip(ths, sls)]
        kks = [_dot(kh, kh, NT, P_GRAM) for kh in ks]
        da1 = [_dot(dru, u_ref[:, sl], NT, P_BWD) for dru, sl in zip(drus, sls)]
        da2 = [_dot(drw, w_ref[:, sl], NT, P_BWD) for drw, sl in zip(drws, sls)]
        das = [jnp.where(i > j, -(x + y), 0.0) for x, y in zip(da1, da2)]
        dkks = [da * bcol * dec for da, (bcol, _, _), dec in zip(das, vecs, decs)]
        dqks = [dp_ref[0, h] * dec for h, dec in zip(hs, decs)]
        dq_ps = [_dot(dqk, kh, NN, P_BWD) for dqk, kh in zip(dqks, ks)]
        dk_ps = [_dot(dqk, qh, TN, P_BWD) for dqk, qh in zip(dqks, qs)]
        dk_as = [_dot(dkk, kh, NN, P_BWD) for dkk, kh in zip(dkks, ks)]
        dk_bs = [_dot(dkk, kh, TN, P_BWD) for dkk, kh in zip(dkks, ks)]
        for h, sl in enumerate(sls):
            qh, kh, vh = qs[h], ks[h], v_ref[:, sl]
            bcol, gcol, _ = vecs[h]
            dec, dru, drw, da, kk = decs[h], drus[h], drws[h], das[h], kks[h]
            gam = jnp.exp(gcol)
            glast = gcol[CH - 1:CH, :]
            e = jnp.exp(glast - gcol)
            kg = kh * gam
            dv_ref[:, sl] = bcol * dru
            dbeta = rsum(dru * vh) + rsum(drw * kg) + rsum(da * kk * dec)
            dgc = rsum(drw * kg) * bcol
            dqg = dqg_ref[:, sl]
            dksh = dks_ref[:, sl]
            dq_ref[:, sl] = gam * dqg + dq_ps[h]
            dk_ref[:, sl] = (bcol * gam) * drw + dk_ps[h] + dk_as[h] + dk_bs[h] + dksh * e
            tk = rsum(dksh * kh) * e
            mdec = da * (bcol * kk * dec) + dp_ref[0, h] * p_ref[0, h]
            col = rsum(jnp.where(i == j, jnp.sum(mdec, axis=0, keepdims=True), 0.0))
            dgc = dgc + rsum(mdec) - col
            dgc = dgc + rsum(dqg * qh) * gam - tk
            dglast = jnp.sum(tk, axis=0, keepdims=True) + dgam_all[0:1, h:h + 1] * jnp.exp(glast)
            dgc = dgc + jnp.where(rows1 == CH - 1, dglast, 0.0)
            dbg = dbg + jnp.where(lane == h, dbeta, 0.0) + jnp.where(lane == HEADS + h, dgc, 0.0)
        dbg_ref[...] = dbg

    row = pl.BlockSpec((CH, GW), lambda c: (c, 0))
    sq = pl.BlockSpec((1, HEADS, CH, CH), lambda c: (c, 0, 0, 0))
    small = pl.BlockSpec((CH, DH), lambda c: (c, 0))
    big = jax.ShapeDtypeStruct((n, GW), F32)
    return pl.pallas_call(
        body, name="gdn_intra_bwd", grid=(nch,),
        in_specs=[row, row, row, small, pl.BlockSpec((1, DH, CH), lambda c: (c, 0, 0)), sq, row, row, sq,
                  row, sq, row, row, row, pl.BlockSpec((1, 8, DH), lambda c: (c, 0, 0))],
        out_specs=[row, row, row, small],
        out_shape=[big, big, big, jax.ShapeDtypeStruct((n, DH), F32)],
        compiler_params=_params(("parallel",)),
    )(q, k, v, bg, bgt, t, u, w, p, dqg, dp, du, dw, dks, dgam)


def _local_step(x, tgt, h, w_g, cqw, late, norm_in_w, ad, gdn_norm_w, conv_b, final_norm_w,
                on_grad_c=None, on_grad_g=None):
    proj_g = _matmul(h, w_g, NT, F32, 512, 1408, 1024, "mm_proj_g", n=GW_COLS)
    q, k, v = _prep_qkv(proj_g, cqw)
    bg, bgt = _prep_bg(proj_g, ad)
    u, w, p, t = _gdn_intra(q, k, v, bg, bgt)
    o, vn, s_in = _gdn_scan(q, k, bg, u, w, p)
    w_c, w_out, conv_w = late(o)
    proj_c = _matmul(h, w_c, NT, F32, 512, 1024, 1024, "mm_proj_c", n=CW_COLS)
    mix = _conv_branch(proj_c, conv_w, conv_b, _gdn_out(o, proj_g, gdn_norm_w))
    out = _matmul(mix, w_out, NN, F32, 512, 512, 2048, "mm_out", add=x)
    dout, dout_b, g_fn, loss = _final_loss(out, tgt, final_norm_w)

    dmix = _matmul(dout_b, w_out, NT, F32, 512, 1024, 1024, "mm_dmix")
    g_wout = _matmul(mix, dout_b, TN, BF16, 512, 512, 2048, "mm_gwout")
    do, dproj_g, g_gn = _gdn_out_bwd(o, proj_g, gdn_norm_w, dmix)
    dproj_c, g_cw, g_cb = _conv_branch_bwd(proj_c, conv_w, conv_b, dmix)
    g_c = _matmul(dproj_c, h, TN, BF16, 1024, 512, 2048, "mm_gwin_c")
    if on_grad_c is not None:
        do = on_grad_c(g_c, g_wout, do)
    dqg, dp, du, dw, dks, dgam = _gdn_scan_bwd(q, k, bg, w, p, vn, s_in, do)
    dq, dk, dv, dbg = _gdn_intra_bwd(q, k, v, bg, bgt, t, u, w, p, dqg, dp, du, dw, dks, dgam)
    dproj_g, gq, gk, gv = _prep_qkv_bwd(proj_g, cqw, dq, dk, dv, dproj_g)
    dproj_g, g_al, g_dt = _prep_bg_bwd(proj_g, ad, dbg, dproj_g)
    g_g = _matmul(dproj_g, h, TN, BF16, 1408, 512, 2048, "mm_gwin_g")
    if on_grad_g is not None:
        dproj_g = on_grad_g(g_g, dproj_g)
    dh = _matmul(dproj_g, w_g, NN, F32, 512, 1024, 1408, "mm_dh_g")
    dh = _matmul(dproj_c, w_c, NN, F32, 512, 1024, 1024, "mm_dh_c", add=dh)
    gx, g_nin = _rms_in_bwd(x, norm_in_w, dh, dout)
    small = dict(nin=g_nin, cb=g_cb, fn=g_fn, al=g_al, dt=g_dt, gn=g_gn, cq=(gq, gk, gv), cw=g_cw, loss=loss)
    return gx, small, (g_g, g_c, g_wout)


def _place():
    x, y, c = lax.axis_index("x"), lax.axis_index("y"), lax.axis_index("c")
    chips = [(1 - x, y), (x, 1 - y), (1 - x, 1 - y)]
    return x, y, c, chips


def _blk(ref, b):
    if isinstance(b, int):
        return ref.at[b * DH:(b + 1) * DH, :]
    return ref.at[pl.ds(pl.multiple_of(b * DH, DH), DH), :]


HBM = pl.BlockSpec(memory_space=pltpu.HBM)
SEM = pl.BlockSpec(memory_space=pltpu.SEMAPHORE)
EFFECT = pltpu.SideEffectType.DATAFLOW_SIDE_EFFECTING


def _split_start(name, issue, bufs, n_sems):
    nbuf = len(bufs)

    def body(*refs):
        issue(refs[:nbuf], refs[nbuf], refs[nbuf + 1])
        refs[-1][...] = jnp.zeros_like(refs[-1])

    out = pl.pallas_call(
        body, name=name,
        out_shape=(pltpu.SemaphoreType.DMA((n_sems,)), pltpu.SemaphoreType.DMA((n_sems,)),
                   *[pltpu.HBM(b.shape, b.dtype) for b in bufs], jax.ShapeDtypeStruct((8, DH), F32)),
        in_specs=[HBM] * nbuf,
        out_specs=(SEM, SEM, *[HBM] * nbuf, pl.BlockSpec(memory_space=pltpu.VMEM)),
        input_output_aliases={a: 2 + a for a in range(nbuf)},
        compiler_params=pltpu.CompilerParams(has_side_effects=EFFECT),
    )(*[pltpu.with_memory_space_constraint(b, pltpu.HBM) for b in bufs])
    return out[0], out[1], list(out[2:2 + nbuf]), out[-1]


def _split_wait(name, await_, send_sems, recv_sems, bufs, after):
    nbuf = len(bufs)

    def body(*refs):
        await_(refs[:nbuf], refs[nbuf], refs[nbuf + 1])

    out = pl.pallas_call(
        body, name=name,
        out_shape=tuple(pltpu.HBM(b.shape, b.dtype) for b in bufs),
        in_specs=[HBM] * nbuf + [SEM, SEM, ANY], out_specs=tuple([HBM] * nbuf),
        input_output_aliases={a: a for a in range(nbuf)},
        compiler_params=pltpu.CompilerParams(has_side_effects=EFFECT),
    )(*bufs, send_sems, recv_sems, after)
    return list(out)


def _phase_blocks(chip, phase, edges, parity=None):
    return [(b, blk) for b, (grp, blk) in enumerate(_shard_blocks(chip, edges))
            if grp == phase and (parity is None or b % 2 == parity)]


def _cols(ref, nblk):
    return ref.at[0:nblk * DH, :]


def _block_table(chip, edges, spare_g, spare_c):
    rows = []
    for s in range(4):
        sb = _shard_blocks(s, edges)
        rows.append([[blk if grp == "g" else spare_g for grp, blk in sb],
                     [blk if grp == "c" else spare_c for grp, blk in sb],
                     [int(grp == "g") for grp, _ in sb], [s] * ALIGNED_BLOCKS])
    return jnp.asarray(rows, jnp.int32)[chip]


def _place_own(a_shard, wo, cq, cw):
    d = a_shard.shape[1]
    chip = 2 * lax.axis_index("x") + lax.axis_index("y")

    def body(t_ref, a_ref, wo_ref, cq_ref, cw_ref, wg_ref, wc_ref, wog_ref, cqg_ref, cwg_ref):
        wg_ref[...] = a_ref[...]
        wc_ref[...] = a_ref[...]

        @pl.when(pl.program_id(0) == 0)
        def _():
            wog_ref[0] = wo_ref[...]
            cqg_ref[0] = cq_ref[...]
            cwg_ref[0] = cw_ref[...]

    whole = lambda s: pl.BlockSpec(s.shape, lambda b, t: (0,) * s.ndim)
    slot = lambda s: pl.BlockSpec((1,) + s.shape, lambda b, t: (t[3, 0],) + (0,) * s.ndim)
    return pl.pallas_call(
        body, name="place_own",
        grid_spec=pltpu.PrefetchScalarGridSpec(
            num_scalar_prefetch=1, grid=(ALIGNED_BLOCKS,),
            in_specs=[pl.BlockSpec((DH, d), lambda b, t: (b, 0)), whole(wo), whole(cq), whole(cw)],
            out_specs=[pl.BlockSpec((DH, d), lambda b, t: (t[0, b], 0)),
                       pl.BlockSpec((DH, d), lambda b, t: (t[1, b], 0)), slot(wo), slot(cq), slot(cw)]),
        out_shape=[jax.ShapeDtypeStruct((WG_BLOCKS * DH, d), a_shard.dtype),
                   jax.ShapeDtypeStruct((WC_BLOCKS * DH, d), a_shard.dtype)]
        + [jax.ShapeDtypeStruct((4,) + s.shape, s.dtype) for s in (wo, cq, cw)],
        compiler_params=_params(("arbitrary",)),
    )(_block_table(chip, True, G_SPARE, C_SPARE), a_shard, wo, cq, cw)


def _tie(x, token, name):
    def body(x_ref, t_ref, o_ref):
        del x_ref, t_ref, o_ref

    return pl.pallas_call(
        body, name=name, in_specs=[ANY, ANY], out_specs=ANY,
        out_shape=jax.ShapeDtypeStruct(x.shape, x.dtype), input_output_aliases={0: 0},
    )(x, token)


def _gather_start(phase, a_shard, w_grp, singles):
    ns = len(singles)

    def issue(refs, send_sems, recv_sems):
        a_ref, w_ref = refs[0], refs[1]
        x, y, c, chips = _place()
        mine = 2 * x + y
        for jj, (px, py) in enumerate(chips):
            to = dict(device_id=(px, py, c), device_id_type=MESH)
            for a in range(ns):
                pltpu.make_async_remote_copy(
                    src_ref=refs[2 + 2 * a], dst_ref=refs[3 + 2 * a].at[mine],
                    send_sem=send_sems.at[(1 + ns) * jj + 1 + a], recv_sem=recv_sems.at[(1 + ns) * jj + 1 + a],
                    **to).start()
        for s in range(4):
            for par in range(2):
                blocks = _phase_blocks(s, phase, True, par)
                if blocks:
                    @pl.when((mine == s) & (c == par))
                    def _():
                        for jj, (px, py) in enumerate(chips):
                            for b, blk in blocks:
                                pltpu.make_async_remote_copy(
                                    src_ref=_blk(a_ref, b), dst_ref=_blk(w_ref, blk),
                                    send_sem=send_sems.at[(1 + ns) * jj], recv_sem=recv_sems.at[(1 + ns) * jj],
                                    device_id=(px, py, c), device_id_type=MESH).start()

    bufs = [a_shard, w_grp] + [t for pair in singles for t in pair]
    return _split_start("gather_start_" + phase, issue, bufs, 3 * (1 + ns))


def _gather_wait(phase, send_sems, recv_sems, bufs, after):
    ns = (len(bufs) - 2) // 2

    def await_(refs, send_sems, recv_sems):
        a_ref, w_ref = refs[0], refs[1]
        x, y, c, chips = _place()
        mine = 2 * x + y
        for jj, (px, py) in enumerate(chips):
            to = dict(device_id=(px, py, c), device_id_type=MESH)
            peer = 2 * px + py
            for a in range(ns):
                cp = pltpu.make_async_remote_copy(
                    src_ref=refs[2 + 2 * a], dst_ref=refs[3 + 2 * a].at[mine],
                    send_sem=send_sems.at[(1 + ns) * jj + 1 + a], recv_sem=recv_sems.at[(1 + ns) * jj + 1 + a], **to)
                cp.wait_recv()
                cp.wait_send()
            for s in range(4):
                for par in range(2):
                    nblk = len(_phase_blocks(s, phase, True, par))
                    if nblk:
                        both = pltpu.make_async_remote_copy(
                            src_ref=_cols(a_ref, nblk), dst_ref=_cols(w_ref, nblk),
                            send_sem=send_sems.at[(1 + ns) * jj], recv_sem=recv_sems.at[(1 + ns) * jj], **to)

                        @pl.when((peer == s) & (c == par))
                        def _():
                            both.wait_recv()

                        @pl.when((mine == s) & (c == par))
                        def _():
                            both.wait_send()

    return _split_wait("gather_wait_" + phase, await_, send_sems, recv_sems, bufs, after)


def _sibling_forward(phase, w_grp):
    def body(w_in_ref, w_ref, send_sems, recv_sems):
        del w_in_ref
        x, y, c, chips = _place()
        to = dict(device_id=(x, y, 1 - c), device_id_type=MESH)
        for jj, (px, py) in enumerate(chips):
            peer = 2 * px + py
            for s in range(4):
                for par in range(2):
                    mine_blocks = _phase_blocks(s, phase, True, par)
                    theirs = len(_phase_blocks(s, phase, True, 1 - par))
                    if not (mine_blocks or theirs):
                        continue

                    @pl.when((peer == s) & (c == par))
                    def _():
                        for _, blk in mine_blocks:
                            pltpu.make_async_remote_copy(
                                src_ref=_blk(w_ref, blk), dst_ref=_blk(w_ref, blk),
                                send_sem=send_sems.at[jj], recv_sem=recv_sems.at[jj], **to).start()
                        if theirs:
                            pltpu.make_async_remote_copy(
                                src_ref=_cols(w_ref, theirs), dst_ref=_cols(w_ref, theirs),
                                send_sem=send_sems.at[jj], recv_sem=recv_sems.at[jj], **to).wait_recv()
                        if mine_blocks:
                            pltpu.make_async_remote_copy(
                                src_ref=_cols(w_ref, len(mine_blocks)), dst_ref=_cols(w_ref, len(mine_blocks)),
                                send_sem=send_sems.at[jj], recv_sem=recv_sems.at[jj], **to).wait_send()

    return pl.pallas_call(
        body, name="sibling_forward_" + phase, in_specs=[ANY], out_specs=ANY,
        out_shape=jax.ShapeDtypeStruct(w_grp.shape, w_grp.dtype), input_output_aliases={0: 0},
        scratch_shapes=[pltpu.SemaphoreType.DMA((3,)), pltpu.SemaphoreType.DMA((3,))],
    )(w_grp)


def _merge_edges(w, edge0, mixed, name):
    d = w.shape[1]

    def body(e_ref, o_ref):
        o_ref[...] = e_ref[0:DH, :] + e_ref[DH:2 * DH, :]

    def to_block(i):
        r = mixed[-1]
        for kk in range(len(mixed) - 2, -1, -1):
            r = jnp.where(i == kk, mixed[kk], r)
        return r

    return pl.pallas_call(
        body, name=name, grid=(len(mixed),),
        in_specs=[pl.BlockSpec((2 * DH, d), lambda i: (edge0 // 2 + i, 0))],
        out_specs=pl.BlockSpec((DH, d), lambda i: (to_block(i), 0)),
        out_shape=jax.ShapeDtypeStruct(w.shape, w.dtype),
        input_output_aliases={0: 0},
        compiler_params=_params(("arbitrary",)),
    )(w)


def _scatter_start(phase, g_grp, land, singles):
    ns = len(singles)

    def issue(refs, send_sems, recv_sems):
        g_ref, land_ref = refs[0], refs[1]
        x, y, c, chips = _place()
        for jj, (px, py) in enumerate(chips):
            to = dict(device_id=(px, py, c), device_id_type=MESH)
            peer = 2 * px + py
            for a in range(ns):
                pltpu.make_async_remote_copy(
                    src_ref=refs[2 + 2 * a].at[peer], dst_ref=refs[3 + 2 * a].at[jj],
                    send_sem=send_sems.at[(1 + ns) * jj + 1 + a], recv_sem=recv_sems.at[(1 + ns) * jj + 1 + a],
                    **to).start()
            for s in range(4):
                blocks = _phase_blocks(s, phase, False)
                if blocks:
                    @pl.when(peer == s)
                    def _():
                        for b, blk in blocks:
                            pltpu.make_async_remote_copy(
                                src_ref=_blk(g_ref, blk), dst_ref=_blk(land_ref.at[jj], b),
                                send_sem=send_sems.at[(1 + ns) * jj], recv_sem=recv_sems.at[(1 + ns) * jj],
                                **to).start()

    bufs = [g_grp, land] + [t for pair in singles for t in pair]
    return _split_start("scatter_start_" + phase, issue, bufs, 3 * (1 + ns))


def _scatter_wait(phase, send_sems, recv_sems, bufs, after):
    ns = (len(bufs) - 2) // 2

    def await_(refs, send_sems, recv_sems):
        g_ref, land_ref = refs[0], refs[1]
        x, y, c, chips = _place()
        mine = 2 * x + y
        for jj, (px, py) in enumerate(chips):
            to = dict(device_id=(px, py, c), device_id_type=MESH)
            peer = 2 * px + py
            for a in range(ns):
                cp = pltpu.make_async_remote_copy(
                    src_ref=refs[2 + 2 * a].at[peer], dst_ref=refs[3 + 2 * a].at[jj],
                    send_sem=send_sems.at[(1 + ns) * jj + 1 + a], recv_sem=recv_sems.at[(1 + ns) * jj + 1 + a], **to)
                cp.wait_recv()
                cp.wait_send()
            for s in range(4):
                nblk = len(_phase_blocks(s, phase, False))
                if nblk:
                    both = pltpu.make_async_remote_copy(
                        src_ref=_cols(g_ref, nblk), dst_ref=_cols(land_ref.at[jj], nblk),
                        send_sem=send_sems.at[(1 + ns) * jj], recv_sem=recv_sems.at[(1 + ns) * jj], **to)

                    @pl.when(mine == s)
                    def _():
                        both.wait_recv()

                    @pl.when(peer == s)
                    def _():
                        both.wait_send()

    return _split_wait("scatter_wait_" + phase, await_, send_sems, recv_sems, bufs, after)


def _sum_shard(g_g, g_c, land):
    d = g_g.shape[1]
    chip = 2 * lax.axis_index("x") + lax.axis_index("y")

    def body(t_ref, gg_ref, gc_ref, land_ref, o_ref):
        own = jnp.where(t_ref[2, pl.program_id(0)] == 1, gg_ref[...].astype(F32), gc_ref[...].astype(F32))
        for jj in range(3):
            own = own + land_ref[jj].astype(F32)
        o_ref[...] = own

    return pl.pallas_call(
        body, name="sum_w_in",
        grid_spec=pltpu.PrefetchScalarGridSpec(
            num_scalar_prefetch=1, grid=(ALIGNED_BLOCKS,),
            in_specs=[pl.BlockSpec((DH, d), lambda b, t: (t[0, b], 0)), pl.BlockSpec((DH, d), lambda b, t: (t[1, b], 0)),
                      pl.BlockSpec((3, DH, d), lambda b, t: (0, b, 0))],
            out_specs=pl.BlockSpec((DH, d), lambda b, t: (b, 0))),
        out_shape=jax.ShapeDtypeStruct((ALIGNED_W, d), F32),
        compiler_params=_params(("arbitrary",)),
    )(_block_table(chip, False, 0, 0), g_g, g_c, land)


def _sum_rows(stack, land, rows):
    _, r, d = stack.shape
    rows = min(rows, r)
    chip = 2 * lax.axis_index("x") + lax.axis_index("y")

    def body(t_ref, own_ref, land_ref, o_ref):
        acc = own_ref[0].astype(F32)
        for jj in range(3):
            acc = acc + land_ref[jj].astype(F32)
        o_ref[...] = acc

    return pl.pallas_call(
        body, name="sum_w_out",
        grid_spec=pltpu.PrefetchScalarGridSpec(
            num_scalar_prefetch=1, grid=(r // rows,),
            in_specs=[pl.BlockSpec((1, rows, d), lambda i, t: (t[0], i, 0)),
                      pl.BlockSpec((3, rows, d), lambda i, t: (0, i, 0))],
            out_specs=pl.BlockSpec((rows, d), lambda i, t: (i, 0))),
        out_shape=jax.ShapeDtypeStruct((r, d), F32),
        compiler_params=_params(("arbitrary",)),
    )(jnp.reshape(chip, (1,)).astype(jnp.int32), stack, land)


def _final_exchange(parts, pack):
    npart = len(parts)

    def body(*refs):
        ins, pack_ref = refs[:npart], refs[npart]
        outs, packs = refs[npart + 1:2 * npart + 1], refs[2 * npart + 1]
        send_sems, recv_sems, psend, precv, loc_sem = refs[2 * npart + 2:]
        x, y, c, _ = _place()
        me = 4 * x + 2 * y + c
        local = pltpu.make_async_copy(pack_ref, packs.at[me], loc_sem)
        local.start()
        cps = [pltpu.make_async_remote_copy(
            src_ref=ins[a], dst_ref=outs[a], send_sem=send_sems.at[a], recv_sem=recv_sems.at[a],
            device_id=(x, y, 1 - c), device_id_type=MESH) for a in range(npart)]
        for r in range(1, 8):
            dx, dy, dc = (r >> 2) & 1, (r >> 1) & 1, r & 1
            peer = (x + dx - 2 * x * dx, y + dy - 2 * y * dy, c + dc - 2 * c * dc)
            cps.append(pltpu.make_async_remote_copy(
                src_ref=pack_ref, dst_ref=packs.at[me], send_sem=psend.at[r - 1], recv_sem=precv.at[r - 1],
                device_id=peer, device_id_type=MESH))
        for cp in cps:
            cp.start()
        for cp in cps:
            cp.wait_recv()
        for cp in cps:
            cp.wait_send()
        local.wait()

    return pl.pallas_call(
        body, name="final_exchange",
        in_specs=[ANY] * (npart + 1), out_specs=[ANY] * (npart + 1),
        out_shape=[jax.ShapeDtypeStruct(p.shape, p.dtype) for p in parts]
        + [jax.ShapeDtypeStruct((8,) + pack.shape, pack.dtype)],
        scratch_shapes=[pltpu.SemaphoreType.DMA((npart,)), pltpu.SemaphoreType.DMA((npart,)),
                        pltpu.SemaphoreType.DMA((7,)), pltpu.SemaphoreType.DMA((7,)), pltpu.SemaphoreType.DMA],
    )(*parts, pack)


def _sum_packs(packs):
    def body(p_ref, o_ref):
        acc = p_ref[0]
        for d in range(1, 8):
            acc = acc + p_ref[d]
        o_ref[...] = acc

    return pl.pallas_call(
        body, name="sum_packs", out_shape=jax.ShapeDtypeStruct(packs.shape[1:], F32),
    )(packs)


def _adamw_update(g, w_ref, m_ref, v_ref, go, do, mo, vo):
    c1 = 1.0 / (1.0 - ADAM_B1 ** ADAM_STEP)
    c2 = 1.0 / (1.0 - ADAM_B2 ** ADAM_STEP)
    mn = ADAM_B1 * m_ref[...] + (1.0 - ADAM_B1) * g
    vn = ADAM_B2 * v_ref[...] + (1.0 - ADAM_B2) * (g * g)
    go[...] = g
    mo[...] = mn
    vo[...] = vn
    do[...] = -ADAM_LR * ((mn * c1) / (jnp.sqrt(vn * c2) + ADAM_EPS) + ADAM_WD * w_ref[...])


def _adamw(w, m, v, g1, g2, rows, name):
    r, cdim = w.shape
    rows = min(rows, r)

    def body(*refs):
        n_in = 4 if g2 is None else 5
        w_ref, m_ref, v_ref, g_ref = refs[:4]
        g = g_ref[...] if g2 is None else g_ref[...] + refs[4][...]
        _adamw_update(g, w_ref, m_ref, v_ref, *refs[n_in:n_in + 4])

    blk = pl.BlockSpec((rows, cdim), lambda i: (i, 0))
    args = [w, m, v, g1] + ([] if g2 is None else [g2])
    shp = jax.ShapeDtypeStruct((r, cdim), F32)
    return pl.pallas_call(
        body, name=name, grid=(r // rows,),
        in_specs=[blk] * len(args), out_specs=[blk] * 4, out_shape=[shp] * 4,
        compiler_params=_params(("parallel",), 20 * rows * cdim * 4 + 8 * 2**20),
    )(*args)


def _adamw_shard(wt, mt, vt, g1, g2):
    r, d = wt.shape
    cols = min(128, d)

    def body(w_ref, m_ref, v_ref, g_ref, g2_ref, go, do, mo, vo, pad_ref):
        chip = 2 * lax.axis_index("x") + lax.axis_index("y")
        back = [(ALIGNED_W - s) % ALIGNED_W for s in SHIFTS]
        pad_ref[...] = pltpu.roll(g_ref[...] + g2_ref[...], _by_chip(chip, back), 0)
        _adamw_update(pad_ref[0:r, :], w_ref, m_ref, v_ref, go, do, mo, vo)

    blk = pl.BlockSpec((r, cols), lambda i: (0, i))
    gblk = pl.BlockSpec((ALIGNED_W, cols), lambda i: (0, i))
    shp = jax.ShapeDtypeStruct((r, d), F32)
    return pl.pallas_call(
        body, name="adamw_w_in", grid=(d // cols,),
        in_specs=[blk] * 3 + [gblk] * 2, out_specs=[blk] * 4, out_shape=[shp] * 4,
        scratch_shapes=[pltpu.VMEM((ALIGNED_W, cols), F32)],
        compiler_params=_params(("parallel",), 24 * ALIGNED_W * cols * 4 + 8 * 2**20),
    )(wt, mt, vt, g1, g2)


def _pad_lanes(a, width):
    return jnp.pad(a, ((0, 0), (0, width - a.shape[1])))


def _gathered_to_full(g):
    return jnp.transpose(g, (1, 0, 2)).reshape(g.shape[1], 4 * g.shape[2])


def _row(a):
    return _pad_lanes(a.reshape(1, -1), 1024)


def _small_pack(nin, cb, fn, al, dt, gn, cqw_shard, cw_shard):
    ad = jnp.concatenate([al.reshape(1, -1), dt.reshape(1, -1)], axis=1)
    rows = [_row(nin), _row(cb), _row(fn), _row(ad), _row(gn), cqw_shard.reshape(3, 1024), _row(cw_shard)]
    out = jnp.concatenate(rows, axis=0)
    return jnp.pad(out, ((0, 16 - out.shape[0]), (0, 0)))


def kernel(x, norm_in_w, w_in, conv_qkv_w, A_log, dt_bias, gdn_norm_w, conv_w, conv_b, w_out, final_norm_w, loss_target, m_norm_in_w, m_w_in, m_conv_qkv_w, m_A_log, m_dt_bias, m_gdn_norm_w, m_conv_w, m_conv_b, m_w_out, m_final_norm_w, v_norm_in_w, v_w_in, v_conv_qkv_w, v_A_log, v_dt_bias, v_gdn_norm_w, v_conv_w, v_conv_b, v_w_out, v_final_norm_w):
    chip = 2 * lax.axis_index("x") + lax.axis_index("y")
    a_shard = _align_shard(jnp.transpose(w_in[0]))
    wo_b = _cast_bf16(w_out[0], 256, "cast_w_out")
    wg0, wc0, wog0, cqg0, cwg0 = _place_own(a_shard, wo_b, conv_qkv_w[0], conv_w[0])
    ss_g, rs_g, bufs_g, tok_g = _gather_start("g", a_shard, wg0, [(conv_qkv_w[0], cqg0)])
    ss_c, rs_c, bufs_c, tok_c = _gather_start("c", bufs_g[0], wc0, [(conv_w[0], cwg0), (wo_b, wog0)])
    x0 = x[0]
    h = _tie(_tie(_rms_in(x0, norm_in_w), tok_g, "after_gather_start_g"), tok_c, "after_gather_start_c")
    a_thru, wg, _, cq_g = _gather_wait("g", ss_g, rs_g, [bufs_c[0]] + bufs_g[1:], h)
    w_g = _merge_edges(_sibling_forward("g", wg), G_EDGE, G_MIXED, "merge_edges_g")
    cqw = _gathered_to_full(cq_g)
    ad = jnp.pad(jnp.concatenate([A_log, dt_bias], axis=0), ((0, 0), (A_LANE, 0)))
    d_model = x.shape[-1]

    def late(o):
        _, wc, _, cw_g, _, wo_g = _gather_wait("c", ss_c, rs_c, [a_thru] + bufs_c[1:], o)
        return (_merge_edges(_sibling_forward("c", wc), C_EDGE, C_MIXED, "merge_edges_c"),
                wo_g.reshape(2 * GW, d_model),
                _gathered_to_full(cw_g))

    scat = {}

    def on_grad_c(g_c, g_wout, do):
        go4 = g_wout.reshape(4, GW // 2, d_model)
        land = lax.empty((3, ALIGNED_W, d_model), BF16)
        land_o = lax.empty((3, GW // 2, d_model), BF16)
        ss, rs, bufs, tok = _scatter_start("c", g_c, land, [(go4, land_o)])
        scat["c"] = (ss, rs, bufs)
        return _tie(do, tok, "after_scatter_start_c")

    def on_grad_g(g_g, dproj_g):
        ss, rs, bufs, tok = _scatter_start("g", g_g, scat["c"][2][1], [])
        scat["g"] = (ss, rs, bufs)
        return _tie(dproj_g, tok, "after_scatter_start_g")

    gx, sm, _ = _local_step(x0, loss_target[0], h, w_g, cqw, late, norm_in_w, ad, gdn_norm_w, conv_b,
                            final_norm_w.reshape(1, -1), on_grad_c, on_grad_g)

    ss, rs, bufs = scat["g"]
    g_g, land = _scatter_wait("g", ss, rs, bufs, gx)
    ss, rs, bufs = scat["c"]
    g_c, land, go4, land_o = _scatter_wait("c", ss, rs, [bufs[0], land, bufs[2], bufs[3]], gx)
    part_in = _sum_shard(g_g, g_c, land)
    part_out = _sum_rows(go4, land_o, 128)
    ad_g = jnp.concatenate([sm["al"][:, A_LANE:], sm["dt"][:, A_LANE:]], axis=1)
    pack = jnp.concatenate([_row(sm["nin"]), _row(sm["cb"]), _row(sm["fn"]), _row(ad_g), _row(sm["gn"]),
                            jnp.concatenate(sm["cq"], axis=1).reshape(12, 1024), sm["cw"], _row(sm["loss"])], axis=0)
    pack = jnp.pad(pack, ((0, PACK_ROWS - pack.shape[0]), (0, 0)))
    sib_in, sib_out, packs = _final_exchange([part_in, part_out], pack)
    tot = _sum_packs(packs)

    g_wi, d_wi, m_wi, v_wi = [jnp.transpose(a) for a in _adamw_shard(
        jnp.transpose(w_in[0]), jnp.transpose(m_w_in[0]), jnp.transpose(v_w_in[0]), part_in, sib_in)]
    g_wo, d_wo, m_wo, v_wo = _adamw(w_out[0], m_w_out[0], v_w_out[0], part_out, sib_out, 128, "adamw_w_out")
    g_cq_sh = lax.dynamic_slice_in_dim(tot[R_CQ:R_CQ + 12].reshape(4, 3 * GW), chip * 768, 768, axis=1)
    g_cw_sh = lax.dynamic_slice_in_dim(tot[R_CW:R_CW + 3], chip * 256, 256, axis=1)
    sp = lambda nin, cb, fn, al, dt, gn, cq, cwv: _small_pack(nin, cb, fn, al, dt, gn, cq[0], cwv[0])
    g_s = _small_pack(tot[R_NIN], tot[R_CB], tot[R_FN], tot[R_AD, :HEADS], tot[R_AD, HEADS:2 * HEADS],
                      tot[R_GN, :DH], g_cq_sh, g_cw_sh)
    w_s = sp(norm_in_w, conv_b, final_norm_w, A_log, dt_bias, gdn_norm_w, conv_qkv_w, conv_w)
    m_s = sp(m_norm_in_w, m_conv_b, m_final_norm_w, m_A_log, m_dt_bias, m_gdn_norm_w, m_conv_qkv_w, m_conv_w)
    v_s = sp(v_norm_in_w, v_conv_b, v_final_norm_w, v_A_log, v_dt_bias, v_gdn_norm_w, v_conv_qkv_w, v_conv_w)
    small = _adamw(w_s, m_s, v_s, g_s, None, 16, "adamw_small")

    def unpack(a, big_in, big_out):
        return (a[0:1], big_in[None], a[5:8].reshape(1, 4, 768), a[3:4, :HEADS], a[3:4, HEADS:2 * HEADS],
                a[4:5, :DH], a[8, :768].reshape(1, 3, 256), a[1:2], big_out[None], a[2])

    loss = tot[R_LOSS, 0]
    return (loss, gx[None], *unpack(small[0], g_wi, g_wo), *unpack(small[1], d_wi, d_wo),
            *unpack(small[2], m_wi, m_wo), *unpack(small[3], v_wi, v_wo))
```

```python
import functools
import math

import jax
import jax.numpy as jnp
from jax import lax
from jax.experimental import pallas as pl
from jax.experimental.pallas import tpu as pltpu

F32 = jnp.float32
BF16 = jnp.bfloat16
MESH = pl.DeviceIdType.MESH
ANY = pl.BlockSpec(memory_space=pl.ANY)

HEADS = 8
DH = 128
CH = 64
GW = HEADS * DH
EPS = 1e-6
VMEM_V7X = 64 * 1024 * 1024

QB, KB, VB, ZB, BAB = 0, 8, 16, 24, 32
A_LANE = 120
NG, NC = 33, 32
GW_COLS, CW_COLS = NG * DH, NC * DH

SHARD_W = 2052
ALIGNED_BLOCKS = 17
ALIGNED_W = ALIGNED_BLOCKS * DH
SHIFTS = (0, 4, ALIGNED_W - 8, ALIGNED_W - 4)
G_EDGE, C_EDGE = 34, 32
G_SPARE, C_SPARE = 33, 34
WG_BLOCKS, WC_BLOCKS = 38, 36
G_MIXED, C_MIXED = (2, BAB), (4 * 7 + 1,)


def _shard_blocks(chip, edges):
    g, c = "g", "c"
    if chip == 0:
        out = [(g, 3 * b) for b in range(8)] + [(g, 3 * b + 1) for b in range(8)] + [(g, G_EDGE, G_MIXED[0])]
    elif chip == 1:
        out = [(g, G_EDGE + 1, G_MIXED[0])] + [(g, 3 * b + 2) for b in range(1, 8)]
        out += [(g, ZB + b) for b in range(8)] + [(g, G_EDGE + 2, G_MIXED[1])]
    elif chip == 2:
        out = [(c, 4 * b) for b in range(8)] + [(c, 4 * b + 1) for b in range(7)]
        out += [(c, C_EDGE, C_MIXED[0]), (g, G_EDGE + 3, G_MIXED[1])]
    else:
        out = [(c, 4 * b + 2) for b in range(8)] + [(c, 4 * b + 3) for b in range(8)] + [(c, C_EDGE + 1, C_MIXED[0])]
    return [(o[0], o[1] if (edges or len(o) == 2) else o[2]) for o in out]


def _by_chip(chip, vals):
    if all(v == vals[0] for v in vals):
        return vals[0]
    r = vals[3]
    for kk in (2, 1, 0):
        r = jnp.where(chip == kk, vals[kk], r)
    return r

ADAM_LR, ADAM_B1, ADAM_B2, ADAM_EPS, ADAM_WD, ADAM_STEP = 0.001, 0.9, 0.999, 1e-08, 0.01, 10

R_NIN, R_CB, R_FN, R_AD, R_GN, R_CQ, R_CW, R_LOSS, PACK_ROWS = 0, 1, 2, 3, 4, 5, 17, 20, 24

NN = ((1,), (0,))
NT = ((1,), (1,))
TN = ((0,), (0,))


def _dot(a, b, dims=NN, mode="lo"):
    dn = (dims, ((), ()))
    if mode == "hi":
        return lax.dot_general(a, b, dn, precision=lax.Precision.HIGHEST, preferred_element_type=F32)
    ah, bh = a.astype(BF16), b.astype(BF16)
    out = lax.dot_general(ah, bh, dn, preferred_element_type=F32)
    if mode == "x3":
        al = (a - ah.astype(F32)).astype(BF16)
        bl = (b - bh.astype(F32)).astype(BF16)
        out = out + lax.dot_general(ah, bl, dn, preferred_element_type=F32)
        out = out + lax.dot_general(al, bh, dn, preferred_element_type=F32)
    return out


P_GRAM, P_INV, P_SOL, P_SCAN, P_SCANB, P_BWD = "lo", "lo", "lo", "lo", "lo", "lo"
P_CUM = "x3"


def _params(sem=None, vmem=None):
    kw = {}
    if sem is not None:
        kw["dimension_semantics"] = sem
    if vmem is not None:
        kw["vmem_limit_bytes"] = int(min(max(vmem, 32 * 2**20), VMEM_V7X - 8 * 2**20))
    return pltpu.CompilerParams(**kw)


def _sigmoid(x):
    return 1.0 / (1.0 + jnp.exp(-x))


def _dsilu(x, s):
    return s * (1.0 + x * (1.0 - s))


def _rows(shape):
    return lax.broadcasted_iota(jnp.int32, shape, 0)


def _shift_down(x, s):
    if s == 0:
        return x
    return jnp.where(_rows(x.shape) >= s, pltpu.roll(x, s, 0), 0.0)


def _shift_up(x, s):
    if s == 0:
        return x
    n = x.shape[0]
    return jnp.where(_rows(x.shape) < n - s, pltpu.roll(x, n - s, 0), 0.0)


def _matmul(a, b, dims, out_dtype, tm, tn, tk, name, add=None, n=None):
    if dims == NN:
        (m, k), n = a.shape, b.shape[1]
    elif dims == NT:
        (m, k), n = a.shape, (n or b.shape[0])
    else:
        (k, m), n = a.shape, b.shape[1]
    tm, tn, tk = min(tm, m), min(tn, n), min(tk, k)
    assert m % tm == 0 and n % tn == 0 and k % tk == 0, (name, m, n, k, tm, tn, tk)
    nk = k // tk

    def body(*refs):
        if add is None:
            a_ref, b_ref, o_ref = refs[:3]
            add_ref = None
        else:
            a_ref, b_ref, add_ref, o_ref = refs[:4]
        part = _dot(a_ref[...], b_ref[...], dims)
        if nk == 1:
            if add_ref is not None:
                part = part + add_ref[...]
            o_ref[...] = part.astype(out_dtype)
            return
        acc = refs[-1]
        kk = pl.program_id(2)

        @pl.when(kk == 0)
        def _():
            acc[...] = part

        @pl.when(kk > 0)
        def _():
            acc[...] += part

        @pl.when(kk == nk - 1)
        def _():
            r = acc[...]
            if add_ref is not None:
                r = r + add_ref[...]
            o_ref[...] = r.astype(out_dtype)

    if dims == TN:
        a_spec = pl.BlockSpec((tk, tm), lambda i, j, kk: (kk, i))
    else:
        a_spec = pl.BlockSpec((tm, tk), lambda i, j, kk: (i, kk))
    if dims == NT:
        b_spec = pl.BlockSpec((tn, tk), lambda i, j, kk: (j, kk))
    else:
        b_spec = pl.BlockSpec((tk, tn), lambda i, j, kk: (kk, j))
    o_spec = pl.BlockSpec((tm, tn), lambda i, j, kk: (i, j))
    in_specs = [a_spec, b_spec]
    args = [a, b]
    if add is not None:
        in_specs.append(o_spec)
        args.append(add)
    osz = jnp.dtype(out_dtype).itemsize
    est = 2 * (tm * tk * a.dtype.itemsize + tk * tn * b.dtype.itemsize + tm * tn * osz)
    est += 3 * tm * tn * 4 + (2 * tm * tn * 4 if add is not None else 0)
    return pl.pallas_call(
        body, name=name, grid=(m // tm, n // tn, nk),
        in_specs=in_specs, out_specs=o_spec,
        out_shape=jax.ShapeDtypeStruct((m, n), out_dtype),
        scratch_shapes=[pltpu.VMEM((tm, tn), F32)] if nk > 1 else [],
        compiler_params=_params(("parallel", "parallel", "arbitrary"), est + 8 * 2**20),
    )(*[pltpu.with_memory_space_constraint(t, pltpu.HBM) for t in args])


def _cast_bf16(a, rows, name):
    r, c = a.shape
    rows = min(rows, r)

    def body(a_ref, o_ref):
        o_ref[...] = a_ref[...].astype(BF16)

    return pl.pallas_call(
        body, name=name, grid=(r // rows,),
        in_specs=[pl.BlockSpec((rows, c), lambda i: (i, 0))],
        out_specs=pl.BlockSpec((rows, c), lambda i: (i, 0)),
        out_shape=jax.ShapeDtypeStruct((r, c), BF16),
        compiler_params=_params(("parallel",)),
    )(a)


def _align_shard(wt):
    r, d = wt.shape
    cols = min(256, d)

    def body(w_ref, o_ref, pad_ref):
        chip = 2 * lax.axis_index("x") + lax.axis_index("y")
        pad_ref[...] = jnp.zeros_like(pad_ref)
        pad_ref[0:r, :] = w_ref[...]
        o_ref[...] = pltpu.roll(pad_ref[...], _by_chip(chip, SHIFTS), 0).astype(BF16)

    return pl.pallas_call(
        body, name="align_shard", grid=(d // cols,),
        in_specs=[pl.BlockSpec((r, cols), lambda i: (0, i))],
        out_specs=pl.BlockSpec((ALIGNED_W, cols), lambda i: (0, i)),
        out_shape=jax.ShapeDtypeStruct((ALIGNED_W, d), BF16),
        scratch_shapes=[pltpu.VMEM((ALIGNED_W, cols), F32)],
        compiler_params=_params(("parallel",)),
    )(wt)


def _rms_in(x, w):
    n, d = x.shape
    tr = min(256, n)

    def body(x_ref, w_ref, h_ref):
        xv = x_ref[...]
        r = lax.rsqrt(jnp.mean(xv * xv, axis=-1, keepdims=True) + EPS)
        h_ref[...] = (xv * r * w_ref[...]).astype(BF16)

    return pl.pallas_call(
        body, name="rms_in", grid=(n // tr,),
        in_specs=[pl.BlockSpec((tr, d), lambda i: (i, 0)), pl.BlockSpec((1, d), lambda i: (0, 0))],
        out_specs=pl.BlockSpec((tr, d), lambda i: (i, 0)),
        out_shape=jax.ShapeDtypeStruct((n, d), BF16),
        compiler_params=_params(("parallel",)),
    )(x, w)


def _conv_silu(p, w_ref, taps):
    c = None
    for j in range(taps):
        t = _shift_down(p, taps - 1 - j) * w_ref[j:j + 1, :]
        c = t if c is None else c + t
    return c


def _prep_qkv(proj, cw):
    n = proj.shape[0]

    def body(p3, wq, wk, wv, q_ref, k_ref, v_ref):
        for kind, (w_ref, o_ref) in enumerate(((wq, q_ref), (wk, k_ref), (wv, v_ref))):
            c = _conv_silu(p3[:, kind * DH:(kind + 1) * DH], w_ref, 4)
            a = c * _sigmoid(c)
            if kind < 2:
                r = lax.rsqrt(jnp.sum(a * a, axis=-1, keepdims=True) + EPS)
                a = a * (r * (DH ** -0.5 if kind == 0 else 1.0))
            o_ref[...] = a

    col = pl.BlockSpec((n, DH), lambda h: (0, h))
    wcol = lambda base: pl.BlockSpec((4, DH), lambda h: (0, base + h))
    out = jax.ShapeDtypeStruct((n, GW), F32)
    return pl.pallas_call(
        body, name="prep_qkv", grid=(HEADS,),
        in_specs=[pl.BlockSpec((n, 3 * DH), lambda h: (0, h)), wcol(QB), wcol(KB), wcol(VB)],
        out_specs=[col] * 3, out_shape=[out] * 3,
        compiler_params=_params(("parallel",), 40 * 2**20),
    )(proj, cw, cw, cw)


def _prep_qkv_bwd(proj, cw, dq, dk, dv, dproj):
    n = proj.shape[0]

    def body(p3, wq, wk, wv, dq_ref, dk_ref, dv_ref, _, o3, gq, gk, gv):
        for kind, (w_ref, d_ref, g_ref) in enumerate(((wq, dq_ref, gq), (wk, dk_ref, gk), (wv, dv_ref, gv))):
            p = p3[:, kind * DH:(kind + 1) * DH]
            c = _conv_silu(p, w_ref, 4)
            s = _sigmoid(c)
            a = c * s
            d = d_ref[...]
            if kind < 2:
                r = lax.rsqrt(jnp.sum(a * a, axis=-1, keepdims=True) + EPS)
                sc = DH ** -0.5 if kind == 0 else 1.0
                d = (sc * r) * (d - a * ((r * r) * jnp.sum(d * a, axis=-1, keepdims=True)))
            dc = d * _dsilu(c, s)
            dp = None
            for j in range(4):
                g_ref[j:j + 1, :] = jnp.sum(dc * _shift_down(p, 3 - j), axis=0, keepdims=True)
                t = _shift_up(dc, 3 - j) * w_ref[j:j + 1, :]
                dp = t if dp is None else dp + t
            o3[:, kind * DH:(kind + 1) * DH] = dp.astype(BF16)

    col = pl.BlockSpec((n, DH), lambda h: (0, h))
    wcol = lambda base: pl.BlockSpec((4, DH), lambda h: (0, base + h))
    p3spec = pl.BlockSpec((n, 3 * DH), lambda h: (0, h))
    return pl.pallas_call(
        body, name="prep_qkv_bwd", grid=(HEADS,),
        in_specs=[p3spec, wcol(QB), wcol(KB), wcol(VB), col, col, col, ANY],
        out_specs=[p3spec] + [wcol(0)] * 3,
        out_shape=[jax.ShapeDtypeStruct(dproj.shape, BF16)] + [jax.ShapeDtypeStruct((4, GW), F32)] * 3,
        input_output_aliases={7: 0},
        compiler_params=_params(("parallel",), 48 * 2**20),
    )(proj, cw, cw, cw, dq, dk, dv, dproj)


def _tri(lower_incl):
    i = lax.broadcasted_iota(jnp.int32, (CH, CH), 0)
    j = lax.broadcasted_iota(jnp.int32, (CH, CH), 1)
    return jnp.where(i >= j, 1.0, 0.0) if lower_incl else jnp.where(j >= i, 1.0, 0.0)


def _lane(shape):
    return lax.broadcasted_iota(jnp.int32, shape, 1)


def _prep_bg(proj, ad):
    n = proj.shape[0]
    nch = n // CH

    def body(p_ref, ad_ref, bg_ref, bgt_ref):
        p = p_ref[...]
        lane = _lane(p.shape)
        beta = _sigmoid(p)
        xa = p + ad_ref[1:2, :]
        sp = jnp.maximum(xa, 0.0) + jnp.log(1.0 + jnp.exp(-jnp.abs(xa)))
        g = pltpu.roll(-jnp.exp(ad_ref[0:1, :]) * sp, DH - A_LANE + HEADS, 1)
        gc = _dot(_tri(True), g, NN, P_CUM)
        bg = jnp.where(lane < HEADS, beta, jnp.where(lane < 2 * HEADS, gc, 0.0))
        bg_ref[...] = bg
        bgt_ref[0] = bg.T

    return pl.pallas_call(
        body, name="prep_bg", grid=(nch,),
        in_specs=[pl.BlockSpec((CH, DH), lambda i: (i, BAB)), pl.BlockSpec((2, DH), lambda i: (0, 0))],
        out_specs=[pl.BlockSpec((CH, DH), lambda i: (i, 0)), pl.BlockSpec((1, DH, CH), lambda i: (i, 0, 0))],
        out_shape=[jax.ShapeDtypeStruct((n, DH), F32), jax.ShapeDtypeStruct((nch, DH, CH), F32)],
        compiler_params=_params(("parallel",)),
    )(proj, ad)


def _prep_bg_bwd(proj, ad, dbg, dproj):
    n = proj.shape[0]
    nch = n // CH

    def body(p_ref, ad_ref, d_ref, _, o_ref, ga_ref, gd_ref):
        p = p_ref[...]
        d = d_ref[...]
        lane = _lane(p.shape)
        beta = _sigmoid(p)
        xa = p + ad_ref[1:2, :]
        sp = jnp.maximum(xa, 0.0) + jnp.log(1.0 + jnp.exp(-jnp.abs(xa)))
        na = -jnp.exp(ad_ref[0:1, :])
        dg = pltpu.roll(_dot(_tri(False), d, NN, P_CUM), A_LANE - HEADS, 1)
        da = dg * na * _sigmoid(xa)
        is_g = lane >= A_LANE
        o_ref[...] = jnp.where(lane < HEADS, d * beta * (1.0 - beta), jnp.where(is_g, da, 0.0)).astype(BF16)
        ga = jnp.sum(jnp.where(is_g, dg * na * sp, 0.0), axis=0, keepdims=True)
        gd = jnp.sum(jnp.where(is_g, da, 0.0), axis=0, keepdims=True)

        @pl.when(pl.program_id(0) == 0)
        def _():
            ga_ref[...] = jnp.zeros_like(ga_ref)
            gd_ref[...] = jnp.zeros_like(gd_ref)

        ga_ref[...] += ga
        gd_ref[...] += gd

    one = pl.BlockSpec((1, DH), lambda i: (0, 0))
    return pl.pallas_call(
        body, name="prep_bg_bwd", grid=(nch,),
        in_specs=[pl.BlockSpec((CH, DH), lambda i: (i, BAB)), pl.BlockSpec((2, DH), lambda i: (0, 0)),
                  pl.BlockSpec((CH, DH), lambda i: (i, 0)), ANY],
        out_specs=[pl.BlockSpec((CH, DH), lambda i: (i, BAB)), one, one],
        out_shape=[jax.ShapeDtypeStruct(dproj.shape, BF16), jax.ShapeDtypeStruct((1, DH), F32),
                   jax.ShapeDtypeStruct((1, DH), F32)],
        input_output_aliases={3: 0},
        compiler_params=_params(("arbitrary",)),
    )(proj, ad, dbg, dproj)


def _gdn_out(o, proj, wg):
    n = o.shape[0]

    def body(o_ref, z_ref, w_ref, y_ref):
        ov, z = o_ref[...], z_ref[...]
        r = lax.rsqrt(jnp.mean(ov * ov, axis=-1, keepdims=True) + EPS)
        y_ref[...] = (ov * r * w_ref[...] * (z * _sigmoid(z))).astype(BF16)

    return pl.pallas_call(
        body, name="gdn_out", grid=(HEADS,),
        in_specs=[pl.BlockSpec((n, DH), lambda h: (0, h)), pl.BlockSpec((n, DH), lambda h: (0, ZB + h)),
                  pl.BlockSpec((1, DH), lambda h: (0, 0))],
        out_specs=pl.BlockSpec((n, DH), lambda h: (0, h)),
        out_shape=jax.ShapeDtypeStruct((n, 2 * GW), BF16),
        compiler_params=_params(("parallel",)),
    )(o, proj, wg)


def _gdn_out_bwd(o, proj, wg, dmix):
    n = o.shape[0]

    def body(o_ref, z_ref, w_ref, d_ref, do_ref, dz_ref, gw_ref):
        ov, z, d, w = o_ref[...], z_ref[...], d_ref[...], w_ref[...]
        r = lax.rsqrt(jnp.mean(ov * ov, axis=-1, keepdims=True) + EPS)
        nrm = ov * r
        s = _sigmoid(z)
        dz_ref[...] = (d * (nrm * w) * _dsilu(z, s)).astype(BF16)
        dn_w = d * (z * s)
        gw = jnp.sum(dn_w * nrm, axis=0, keepdims=True)
        dn = dn_w * w
        do_ref[...] = r * (dn - nrm * jnp.mean(dn * nrm, axis=-1, keepdims=True))

        @pl.when(pl.program_id(0) == 0)
        def _():
            gw_ref[...] = jnp.zeros_like(gw_ref)

        gw_ref[...] += gw

    return pl.pallas_call(
        body, name="gdn_out_bwd", grid=(HEADS,),
        in_specs=[pl.BlockSpec((n, DH), lambda h: (0, h)), pl.BlockSpec((n, DH), lambda h: (0, ZB + h)),
                  pl.BlockSpec((1, DH), lambda h: (0, 0)), pl.BlockSpec((n, DH), lambda h: (0, h))],
        out_specs=[pl.BlockSpec((n, DH), lambda h: (0, h)), pl.BlockSpec((n, DH), lambda h: (0, ZB + h)),
                   pl.BlockSpec((1, DH), lambda h: (0, 0))],
        out_shape=[jax.ShapeDtypeStruct((n, GW), F32), jax.ShapeDtypeStruct((n, GW_COLS), BF16),
                   jax.ShapeDtypeStruct((1, DH), F32)],
        compiler_params=_params(("arbitrary",)),
    )(o, proj, wg, dmix)


def _conv_branch(proj, w3, b, mix):
    n = proj.shape[0]

    def body(p4, w_ref, b_ref, _, y_ref):
        u = p4[:, DH:2 * DH] * p4[:, 2 * DH:3 * DH]
        cc = _conv_silu(u, w_ref, 3) + b_ref[...]
        z = p4[:, 3 * DH:4 * DH]
        y_ref[...] = (p4[:, 0:DH] * cc * (z * _sigmoid(z))).astype(BF16)

    return pl.pallas_call(
        body, name="conv_branch", grid=(HEADS,),
        in_specs=[pl.BlockSpec((n, 4 * DH), lambda h: (0, h)), pl.BlockSpec((3, DH), lambda h: (0, h)),
                  pl.BlockSpec((1, DH), lambda h: (0, h)), ANY],
        out_specs=pl.BlockSpec((n, DH), lambda h: (0, HEADS + h)),
        out_shape=jax.ShapeDtypeStruct(mix.shape, BF16),
        input_output_aliases={3: 0},
        compiler_params=_params(("parallel",), 40 * 2**20),
    )(proj, w3, b, mix)


def _conv_branch_bwd(proj, w3, b, dmix):
    n = proj.shape[0]

    def body(p4, w_ref, b_ref, d_ref, o4, gw_ref, gbias_ref):
        gb, gcv, hc, z = p4[:, 0:DH], p4[:, DH:2 * DH], p4[:, 2 * DH:3 * DH], p4[:, 3 * DH:4 * DH]
        d = d_ref[...]
        dgb, dgc, dhc, dzc = (o4.at[:, kk * DH:(kk + 1) * DH] for kk in range(4))
        u = gcv * hc
        cc = _conv_silu(u, w_ref, 3) + b_ref[...]
        s = _sigmoid(z)
        dzc[...] = (d * (gb * cc) * _dsilu(z, s)).astype(BF16)
        dp = d * (z * s)
        dgb[...] = (dp * cc).astype(BF16)
        dcc = dp * gb
        gbias_ref[...] = jnp.sum(dcc, axis=0, keepdims=True)
        du = None
        for j in range(3):
            gw_ref[j:j + 1, :] = jnp.sum(dcc * _shift_down(u, 2 - j), axis=0, keepdims=True)
            t = _shift_up(dcc, 2 - j) * w_ref[j:j + 1, :]
            du = t if du is None else du + t
        dgc[...] = (du * hc).astype(BF16)
        dhc[...] = (du * gcv).astype(BF16)

    p4spec = pl.BlockSpec((n, 4 * DH), lambda h: (0, h))
    return pl.pallas_call(
        body, name="conv_branch_bwd", grid=(HEADS,),
        in_specs=[p4spec, pl.BlockSpec((3, DH), lambda h: (0, h)), pl.BlockSpec((1, DH), lambda h: (0, h)),
                  pl.BlockSpec((n, DH), lambda h: (0, HEADS + h))],
        out_specs=[p4spec, pl.BlockSpec((3, DH), lambda h: (0, h)), pl.BlockSpec((1, DH), lambda h: (0, h))],
        out_shape=[jax.ShapeDtypeStruct((n, CW_COLS), BF16), jax.ShapeDtypeStruct((3, GW), F32),
                   jax.ShapeDtypeStruct((1, GW), F32)],
        compiler_params=_params(("parallel",), 48 * 2**20),
    )(proj, w3, b, dmix)


def _final_loss(out, tgt, wf):
    n, d = out.shape
    tr = min(256, n)

    def body(o_ref, t_ref, w_ref, do_ref, dob_ref, gw_ref, loss_ref):
        ov, w = o_ref[...], w_ref[...]
        r = lax.rsqrt(jnp.mean(ov * ov, axis=-1, keepdims=True) + EPS)
        nrm = ov * r
        e = nrm * w - t_ref[...]
        dy = e * (1.0 / d)
        dn = dy * w
        dout = r * (dn - nrm * jnp.mean(dn * nrm, axis=-1, keepdims=True))
        do_ref[...] = dout
        dob_ref[...] = dout.astype(BF16)

        @pl.when(pl.program_id(0) == 0)
        def _():
            gw_ref[...] = jnp.zeros_like(gw_ref)
            loss_ref[...] = jnp.zeros_like(loss_ref)

        gw_ref[...] += jnp.sum(dy * nrm, axis=0, keepdims=True)
        loss_ref[...] += (0.5 / d) * jnp.sum(jnp.sum(e * e, axis=-1, keepdims=True), axis=0, keepdims=True)

    row = pl.BlockSpec((tr, d), lambda i: (i, 0))
    return pl.pallas_call(
        body, name="final_loss", grid=(n // tr,),
        in_specs=[row, row, pl.BlockSpec((1, d), lambda i: (0, 0))],
        out_specs=[row, row, pl.BlockSpec((1, d), lambda i: (0, 0)), pl.BlockSpec((1, 1), lambda i: (0, 0))],
        out_shape=[jax.ShapeDtypeStruct((n, d), F32), jax.ShapeDtypeStruct((n, d), BF16),
                   jax.ShapeDtypeStruct((1, d), F32), jax.ShapeDtypeStruct((1, 1), F32)],
        compiler_params=_params(("arbitrary",)),
    )(out, tgt, wf)


def _rms_in_bwd(x, w, dh, dout):
    n, d = x.shape
    tr = min(256, n)

    def body(x_ref, w_ref, dh_ref, do_ref, dx_ref, gw_ref):
        xv, dhv = x_ref[...], dh_ref[...]
        r = lax.rsqrt(jnp.mean(xv * xv, axis=-1, keepdims=True) + EPS)
        xn = xv * r
        dxn = dhv * w_ref[...]
        dx_ref[...] = r * (dxn - xn * jnp.mean(dxn * xn, axis=-1, keepdims=True)) + do_ref[...]

        @pl.when(pl.program_id(0) == 0)
        def _():
            gw_ref[...] = jnp.zeros_like(gw_ref)

        gw_ref[...] += jnp.sum(dhv * xn, axis=0, keepdims=True)

    row = pl.BlockSpec((tr, d), lambda i: (i, 0))
    one = pl.BlockSpec((1, d), lambda i: (0, 0))
    return pl.pallas_call(
        body, name="rms_in_bwd", grid=(n // tr,),
        in_specs=[row, one, row, row], out_specs=[row, one],
        out_shape=[jax.ShapeDtypeStruct((n, d), F32), jax.ShapeDtypeStruct((1, d), F32)],
        compiler_params=_params(("arbitrary",)),
    )(x, w, dh, dout)


def _ij():
    i = lax.broadcasted_iota(jnp.int32, (CH, CH), 0)
    j = lax.broadcasted_iota(jnp.int32, (CH, CH), 1)
    return i, j


def _unit_lower_inverse(mats):
    i, j = _ij()
    eye = jnp.where(i == j, 1.0, 0.0)
    same16 = (i // 16) == (j // 16)
    same32 = (i // 32) == (j // 32)
    mm = lambda xs, ys: [_dot(x, y, NN, P_INV) for x, y in zip(xs, ys)]
    n1 = [jnp.where(same16, -a, 0.0) for a in mats]
    n2 = mm(n1, n1)
    n4 = mm(n2, n2)
    n8 = mm(n4, n4)
    t = [eye + x1 + x2 + x3 for x1, x2, x3 in zip(n1, n2, mm(n1, n2))]
    t = [x + y for x, y in zip(t, mm(t, n4))]
    t = [x + y for x, y in zip(t, mm(t, n8))]
    a1 = [jnp.where(same32 & jnp.logical_not(same16), a, 0.0) for a in mats]
    t = [x - y for x, y in zip(t, mm(t, mm(a1, t)))]
    a2 = [jnp.where(same32, 0.0, a) for a in mats]
    t = [x - y for x, y in zip(t, mm(t, mm(a2, t)))]
    return t


def _head_vectors(bg, bgt, h):
    bcol = bg[:, h:h + 1]
    gcol = bg[:, HEADS + h:HEADS + h + 1]
    grow = bgt[HEADS + h:HEADS + h + 1, :]
    return bcol, gcol, grow


def _decay(gcol, grow):
    i, j = _ij()
    return jnp.where(i >= j, jnp.exp(jnp.where(i >= j, gcol - grow, 0.0)), 0.0)


def _gdn_intra(q, k, v, bg, bgt):
    n = q.shape[0]
    nch = n // CH

    def body(q_ref, k_ref, v_ref, bg_ref, bgt_ref, u_ref, w_ref, p_ref, t_ref):
        bg, bgt = bg_ref[...], bgt_ref[0]
        i, j = _ij()
        sls = [slice(h * DH, (h + 1) * DH) for h in range(HEADS)]
        ks = [k_ref[:, sl] for sl in sls]
        vecs = [_head_vectors(bg, bgt, h) for h in range(HEADS)]
        decs = [_decay(gcol, grow) for _, gcol, grow in vecs]
        kks = [_dot(kh, kh, NT, P_GRAM) for kh in ks]
        qks = [_dot(q_ref[:, sl], kh, NT, P_GRAM) for sl, kh in zip(sls, ks)]
        ts = _unit_lower_inverse([jnp.where(i > j, bcol * kk * dec, 0.0)
                                  for (bcol, _, _), kk, dec in zip(vecs, kks, decs)])
        us = [_dot(t, v_ref[:, sl] * bcol, NN, P_SOL) for t, sl, (bcol, _, _) in zip(ts, sls, vecs)]
        ws = [_dot(t, kh * (bcol * jnp.exp(gcol)), NN, P_SOL) for t, kh, (bcol, gcol, _) in zip(ts, ks, vecs)]
        for h, sl in enumerate(sls):
            p_ref[0, h] = qks[h] * decs[h]
            t_ref[0, h] = ts[h]
            u_ref[:, sl] = us[h]
            w_ref[:, sl] = ws[h]

    row = pl.BlockSpec((CH, GW), lambda c: (c, 0))
    sq = pl.BlockSpec((1, HEADS, CH, CH), lambda c: (c, 0, 0, 0))
    big = jax.ShapeDtypeStruct((n, GW), F32)
    sqs = jax.ShapeDtypeStruct((nch, HEADS, CH, CH), F32)
    return pl.pallas_call(
        body, name="gdn_intra", grid=(nch,),
        in_specs=[row, row, row, pl.BlockSpec((CH, DH), lambda c: (c, 0)),
                  pl.BlockSpec((1, DH, CH), lambda c: (c, 0, 0))],
        out_specs=[row, row, sq, sq], out_shape=[big, big, sqs, sqs],
        compiler_params=_params(("parallel",)),
    )(q, k, v, bg, bgt)


def _gdn_scan(q, k, bg, u, w, p):
    n = q.shape[0]
    nch = n // CH

    def body(q_ref, k_ref, bg_ref, u_ref, w_ref, p_ref, o_ref, vn_ref, s_out, s_scr):
        @pl.when(pl.program_id(0) == 0)
        def _():
            s_scr[...] = jnp.zeros_like(s_scr)

        bg = bg_ref[...]
        hs = range(HEADS)
        sls = [slice(h * DH, (h + 1) * DH) for h in hs]
        gcols = [bg[:, HEADS + h:HEADS + h + 1] for h in hs]
        glasts = [g[CH - 1:CH, :] for g in gcols]
        ss = [s_scr[h] for h in hs]
        wss = [_dot(w_ref[:, sl], s, NN, P_SCAN) for sl, s in zip(sls, ss)]
        oqs = [_dot(q_ref[:, sl] * jnp.exp(g), s, NN, P_SCAN) for sl, s, g in zip(sls, ss, gcols)]
        vns = [u_ref[:, sl] - x for sl, x in zip(sls, wss)]
        ops = [_dot(p_ref[0, h], vn, NN, P_SCAN) for h, vn in zip(hs, vns)]
        sns = [_dot(k_ref[:, sl] * jnp.exp(gl - g), vn, TN, P_SCAN)
               for sl, gl, g, vn in zip(sls, glasts, gcols, vns)]
        for h, sl in enumerate(sls):
            s_out[0, :, sl] = ss[h]
            vn_ref[:, sl] = vns[h]
            o_ref[:, sl] = oqs[h] + ops[h]
            s_scr[h] = ss[h] * jnp.exp(glasts[h]) + sns[h]

    row = pl.BlockSpec((CH, GW), lambda c: (c, 0))
    big = jax.ShapeDtypeStruct((n, GW), F32)
    return pl.pallas_call(
        body, name="gdn_scan", grid=(nch,),
        in_specs=[row, row, pl.BlockSpec((CH, DH), lambda c: (c, 0)), row, row,
                  pl.BlockSpec((1, HEADS, CH, CH), lambda c: (c, 0, 0, 0))],
        out_specs=[row, row, pl.BlockSpec((1, DH, GW), lambda c: (c, 0, 0))],
        out_shape=[big, big, jax.ShapeDtypeStruct((nch, DH, GW), F32)],
        scratch_shapes=[pltpu.VMEM((HEADS, DH, DH), F32)],
        compiler_params=_params(("arbitrary",)),
    )(q, k, bg, u, w, p)


def _gdn_scan_bwd(q, k, bg, w, p, vn, s_in, do):
    n = q.shape[0]
    nch = n // CH
    rev = lambda c: nch - 1 - c

    def body(q_ref, k_ref, bg_ref, w_ref, p_ref, vn_ref, s_ref, do_ref,
             dqg_ref, dp_ref, du_ref, dw_ref, dks_ref, dgam_ref, ds_scr):
        @pl.when(pl.program_id(0) == 0)
        def _():
            ds_scr[...] = jnp.zeros_like(ds_scr)

        bg = bg_ref[...]
        lane = _lane((1, DH))
        hs = range(HEADS)
        sls = [slice(h * DH, (h + 1) * DH) for h in hs]
        gcols = [bg[:, HEADS + h:HEADS + h + 1] for h in hs]
        glasts = [g[CH - 1:CH, :] for g in gcols]
        ss = [s_ref[0, :, sl] for sl in sls]
        dss = [ds_scr[h] for h in hs]
        dos = [do_ref[:, sl] for sl in sls]
        vnl = [vn_ref[:, sl] for sl in sls]
        dqgs = [_dot(d, s, NT, P_SCANB) for d, s in zip(dos, ss)]
        dps = [_dot(d, vn, NT, P_SCANB) for d, vn in zip(dos, vnl)]
        dvn1 = [_dot(p_ref[0, h], d, TN, P_SCANB) for h, d in zip(hs, dos)]
        dvn2 = [_dot(k_ref[:, sl] * jnp.exp(gl - g), ds, NN, P_SCANB)
                for sl, gl, g, ds in zip(sls, glasts, gcols, dss)]
        dkss = [_dot(vn, ds, NT, P_SCANB) for vn, ds in zip(vnl, dss)]
        dsq = [_dot(q_ref[:, sl] * jnp.exp(g), d, TN, P_SCANB) for sl, g, d in zip(sls, gcols, dos)]
        dvns = [a + b for a, b in zip(dvn1, dvn2)]
        dws = [_dot(dvn, s, NT, P_SCANB) for dvn, s in zip(dvns, ss)]
        dsw = [_dot(w_ref[:, sl], dvn, TN, P_SCANB) for sl, dvn in zip(sls, dvns)]
        dgam = jnp.zeros((1, DH), F32)
        for h, sl in enumerate(sls):
            dqg_ref[:, sl] = dqgs[h]
            dp_ref[0, h] = dps[h]
            du_ref[:, sl] = dvns[h]
            dw_ref[:, sl] = -dws[h]
            dks_ref[:, sl] = dkss[h]
            tot = jnp.sum(jnp.sum(dss[h] * ss[h], axis=-1, keepdims=True), axis=0, keepdims=True)
            dgam = dgam + jnp.where(lane == h, tot, 0.0)
            ds_scr[h] = dss[h] * jnp.exp(glasts[h]) + dsq[h] - dsw[h]
        dgam_ref[0] = jnp.broadcast_to(dgam, (8, DH))

    row = pl.BlockSpec((CH, GW), lambda c: (rev(c), 0))
    sq = pl.BlockSpec((1, HEADS, CH, CH), lambda c: (rev(c), 0, 0, 0))
    big = jax.ShapeDtypeStruct((n, GW), F32)
    return pl.pallas_call(
        body, name="gdn_scan_bwd", grid=(nch,),
        in_specs=[row, row, pl.BlockSpec((CH, DH), lambda c: (rev(c), 0)), row, sq, row,
                  pl.BlockSpec((1, DH, GW), lambda c: (rev(c), 0, 0)), row],
        out_specs=[row, sq, row, row, row, pl.BlockSpec((1, 8, DH), lambda c: (rev(c), 0, 0))],
        out_shape=[big, jax.ShapeDtypeStruct((nch, HEADS, CH, CH), F32), big, big, big,
                   jax.ShapeDtypeStruct((nch, 8, DH), F32)],
        scratch_shapes=[pltpu.VMEM((HEADS, DH, DH), F32)],
        compiler_params=_params(("arbitrary",)),
    )(q, k, bg, w, p, vn, s_in, do)


def _gdn_intra_bwd(q, k, v, bg, bgt, t, u, w, p, dqg, dp, du, dw, dks, dgam):
    n = q.shape[0]
    nch = n // CH

    def body(q_ref, k_ref, v_ref, bg_ref, bgt_ref, t_ref, u_ref, w_ref, p_ref,
             dqg_ref, dp_ref, du_ref, dw_ref, dks_ref, dgam_ref, dq_ref, dk_ref, dv_ref, dbg_ref):
        bg, bgt = bg_ref[...], bgt_ref[0]
        dgam_all = dgam_ref[0]
        i, j = _ij()
        rows1 = lax.broadcasted_iota(jnp.int32, (CH, 1), 0)
        lane = _lane((CH, DH))
        dbg = jnp.zeros((CH, DH), F32)
        rsum = lambda x: jnp.sum(x, axis=-1, keepdims=True)
        hs = range(HEADS)
        sls = [slice(h * DH, (h + 1) * DH) for h in hs]
        qs = [q_ref[:, sl] for sl in sls]
        ks = [k_ref[:, sl] for sl in sls]
        vecs = [_head_vectors(bg, bgt, h) for h in hs]
        decs = [_decay(gcol, grow) for _, gcol, grow in vecs]
        ths = [t_ref[0, h] for h in hs]
        drus = [_dot(th, du_ref[:, sl], TN, P_BWD) for th, sl in zip(ths, sls)]
        drws = [_dot(th, dw_ref[:, sl], TN, P_BWD) for th, sl in zip(ths, sls)]
        kks = [_dot(kh, kh, NT, P_GRAM) for kh in ks]
        da1 = [_dot(dru, u_ref[:, sl], NT, P_BWD) for dru, sl in zip(drus, sls)]
        da2 = [_dot(drw, w_ref[:, sl], NT, P_BWD) for drw, sl in zip(drws, sls)]
        das = [jnp.where(i > j, -(x + y), 0.0) for x, y in zip(da1, da2)]
        dkks = [da * bcol * dec for da, (bcol, _, _), dec in zip(das, vecs, decs)]
        dqks = [dp_ref[0, h] * dec for h, dec in zip(hs, decs)]
        dq_ps = [_dot(dqk, kh, NN, P_BWD) for dqk, kh in zip(dqks, ks)]
        dk_ps = [_dot(dqk, qh, TN, P_BWD) for dqk, qh in zip(dqks, qs)]
        dk_as = [_dot(dkk, kh, NN, P_BWD) for dkk, kh in zip(dkks, ks)]
        dk_bs = [_dot(dkk, kh, TN, P_BWD) for dkk, kh in zip(dkks, ks)]
        for h, sl in enumerate(sls):
            qh, kh, vh = qs[h], ks[h], v_ref[:, sl]
            bcol, gcol, _ = vecs[h]
            dec, dru, drw, da, kk = decs[h], drus[h], drws[h], das[h], kks[h]
            gam = jnp.exp(gcol)
            glast = gcol[CH - 1:CH, :]
            e = jnp.exp(glast - gcol)
            kg = kh * gam
            dv_ref[:, sl] = bcol * dru
            dbeta = rsum(dru * vh) + rsum(drw * kg) + rsum(da * kk * dec)
            dgc = rsum(drw * kg) * bcol
            dqg = dqg_ref[:, sl]
            dksh = dks_ref[:, sl]
            dq_ref[:, sl] = gam * dqg + dq_ps[h]
            dk_ref[:, sl] = (bcol * gam) * drw + dk_ps[h] + dk_as[h] + dk_bs[h] + dksh * e
            tk = rsum(dksh * kh) * e
            mdec = da * (bcol * kk * dec) + dp_ref[0, h] * p_ref[0, h]
            col = rsum(jnp.where(i == j, jnp.sum(mdec, axis=0, keepdims=True), 0.0))
            dgc = dgc + rsum(mdec) - col
            dgc = dgc + rsum(dqg * qh) * gam - tk
            dglast = jnp.sum(tk, axis=0, keepdims=True) + dgam_all[0:1, h:h + 1] * jnp.exp(glast)
            dgc = dgc + jnp.where(rows1 == CH - 1, dglast, 0.0)
            dbg = dbg + jnp.where(lane == h, dbeta, 0.0) + jnp.where(lane == HEADS + h, dgc, 0.0)
        dbg_ref[...] = dbg

    row = pl.BlockSpec((CH, GW), lambda c: (c, 0))
    sq = pl.BlockSpec((1, HEADS, CH, CH), lambda c: (c, 0, 0, 0))
    small = pl.BlockSpec((CH, DH), lambda c: (c, 0))
    big = jax.ShapeDtypeStruct((n, GW), F32)
    return pl.pallas_call(
        body, name="gdn_intra_bwd", grid=(nch,),
        in_specs=[row, row, row, small, pl.BlockSpec((1, DH, CH), lambda c: (c, 0, 0)), sq, row, row, sq,
                  row, sq, row, row, row, pl.BlockSpec((1, 8, DH), lambda c: (c, 0, 0))],
        out_specs=[row, row, row, small],
        out_shape=[big, big, big, jax.ShapeDtypeStruct((n, DH), F32)],
        compiler_params=_params(("parallel",)),
    )(q, k, v, bg, bgt, t, u, w, p, dqg, dp, du, dw, dks, dgam)


def _local_step(x, tgt, h, w_g, cqw, late, norm_in_w, ad, gdn_norm_w, conv_b, final_norm_w,
                on_grad_c=None, on_grad_g=None):
    proj_g = _matmul(h, w_g, NT, F32, 512, 1408, 1024, "mm_proj_g", n=GW_COLS)
    q, k, v = _prep_qkv(proj_g, cqw)
    bg, bgt = _prep_bg(proj_g, ad)
    u, w, p, t = _gdn_intra(q, k, v, bg, bgt)
    o, vn, s_in = _gdn_scan(q, k, bg, u, w, p)
    w_c, w_out, conv_w = late(o)
    proj_c = _matmul(h, w_c, NT, F32, 512, 1024, 1024, "mm_proj_c", n=CW_COLS)
    mix = _conv_branch(proj_c, conv_w, conv_b, _gdn_out(o, proj_g, gdn_norm_w))
    out = _matmul(mix, w_out, NN, F32, 512, 512, 2048, "mm_out", add=x)
    dout, dout_b, g_fn, loss = _final_loss(out, tgt, final_norm_w)

    dmix = _matmul(dout_b, w_out, NT, F32, 512, 1024, 1024, "mm_dmix")
    g_wout = _matmul(mix, dout_b, TN, BF16, 512, 512, 2048, "mm_gwout")
    do, dproj_g, g_gn = _gdn_out_bwd(o, proj_g, gdn_norm_w, dmix)
    dproj_c, g_cw, g_cb = _conv_branch_bwd(proj_c, conv_w, conv_b, dmix)
    g_c = _matmul(dproj_c, h, TN, BF16, 1024, 512, 2048, "mm_gwin_c")
    if on_grad_c is not None:
        do = on_grad_c(g_c, g_wout, do)
    dqg, dp, du, dw, dks, dgam = _gdn_scan_bwd(q, k, bg, w, p, vn, s_in, do)
    dq, dk, dv, dbg = _gdn_intra_bwd(q, k, v, bg, bgt, t, u, w, p, dqg, dp, du, dw, dks, dgam)
    dproj_g, gq, gk, gv = _prep_qkv_bwd(proj_g, cqw, dq, dk, dv, dproj_g)
    dproj_g, g_al, g_dt = _prep_bg_bwd(proj_g, ad, dbg, dproj_g)
    g_g = _matmul(dproj_g, h, TN, BF16, 1408, 512, 2048, "mm_gwin_g")
    if on_grad_g is not None:
        dproj_g = on_grad_g(g_g, dproj_g)
    dh = _matmul(dproj_g, w_g, NN, F32, 512, 1024, 1408, "mm_dh_g")
    dh = _matmul(dproj_c, w_c, NN, F32, 512, 1024, 1024, "mm_dh_c", add=dh)
    gx, g_nin = _rms_in_bwd(x, norm_in_w, dh, dout)
    small = dict(nin=g_nin, cb=g_cb, fn=g_fn, al=g_al, dt=g_dt, gn=g_gn, cq=(gq, gk, gv), cw=g_cw, loss=loss)
    return gx, small, (g_g, g_c, g_wout)


def _place():
    x, y, c = lax.axis_index("x"), lax.axis_index("y"), lax.axis_index("c")
    chips = [(1 - x, y), (x, 1 - y), (1 - x, 1 - y)]
    return x, y, c, chips


def _blk(ref, b):
    if isinstance(b, int):
        return ref.at[b * DH:(b + 1) * DH, :]
    return ref.at[pl.ds(pl.multiple_of(b * DH, DH), DH), :]


HBM = pl.BlockSpec(memory_space=pltpu.HBM)
SEM = pl.BlockSpec(memory_space=pltpu.SEMAPHORE)
EFFECT = pltpu.SideEffectType.DATAFLOW_SIDE_EFFECTING


def _split_start(name, issue, bufs, n_sems):
    nbuf = len(bufs)

    def body(*refs):
        issue(refs[:nbuf], refs[nbuf], refs[nbuf + 1])
        refs[-1][...] = jnp.zeros_like(refs[-1])

    out = pl.pallas_call(
        body, name=name,
        out_shape=(pltpu.SemaphoreType.DMA((n_sems,)), pltpu.SemaphoreType.DMA((n_sems,)),
                   *[pltpu.HBM(b.shape, b.dtype) for b in bufs], jax.ShapeDtypeStruct((8, DH), F32)),
        in_specs=[HBM] * nbuf,
        out_specs=(SEM, SEM, *[HBM] * nbuf, pl.BlockSpec(memory_space=pltpu.VMEM)),
        input_output_aliases={a: 2 + a for a in range(nbuf)},
        compiler_params=pltpu.CompilerParams(has_side_effects=EFFECT),
    )(*[pltpu.with_memory_space_constraint(b, pltpu.HBM) for b in bufs])
    return out[0], out[1], list(out[2:2 + nbuf]), out[-1]


def _split_wait(name, await_, send_sems, recv_sems, bufs, after):
    nbuf = len(bufs)

    def body(*refs):
        await_(refs[:nbuf], refs[nbuf], refs[nbuf + 1])

    out = pl.pallas_call(
        body, name=name,
        out_shape=tuple(pltpu.HBM(b.shape, b.dtype) for b in bufs),
        in_specs=[HBM] * nbuf + [SEM, SEM, ANY], out_specs=tuple([HBM] * nbuf),
        input_output_aliases={a: a for a in range(nbuf)},
        compiler_params=pltpu.CompilerParams(has_side_effects=EFFECT),
    )(*bufs, send_sems, recv_sems, after)
    return list(out)


def _phase_blocks(chip, phase, edges, parity=None):
    return [(b, blk) for b, (grp, blk) in enumerate(_shard_blocks(chip, edges))
            if grp == phase and (parity is None or b % 2 == parity)]


def _cols(ref, nblk):
    return ref.at[0:nblk * DH, :]


def _block_table(chip, edges, spare_g, spare_c):
    rows = []
    for s in range(4):
        sb = _shard_blocks(s, edges)
        rows.append([[blk if grp == "g" else spare_g for grp, blk in sb],
                     [blk if grp == "c" else spare_c for grp, blk in sb],
                     [int(grp == "g") for grp, _ in sb], [s] * ALIGNED_BLOCKS])
    return jnp.asarray(rows, jnp.int32)[chip]


def _place_own(a_shard, wo, cq, cw, bufs):
    d = a_shard.shape[1]
    chip = 2 * lax.axis_index("x") + lax.axis_index("y")

    def body(t_ref, a_ref, wo_ref, cq_ref, cw_ref, *refs):
        wg_ref, wc_ref, wog_ref, cqg_ref, cwg_ref = refs[5:]
        wg_ref[...] = a_ref[...]
        wc_ref[...] = a_ref[...]

        @pl.when(pl.program_id(0) == 0)
        def _():
            wog_ref[0] = wo_ref[...]
            cqg_ref[0] = cq_ref[...]
            cwg_ref[0] = cw_ref[...]

    whole = lambda s: pl.BlockSpec(s.shape, lambda b, t: (0,) * s.ndim)
    slot = lambda s: pl.BlockSpec((1,) + s.shape, lambda b, t: (t[3, 0],) + (0,) * s.ndim)
    return pl.pallas_call(
        body, name="place_own",
        grid_spec=pltpu.PrefetchScalarGridSpec(
            num_scalar_prefetch=1, grid=(ALIGNED_BLOCKS,),
            in_specs=[pl.BlockSpec((DH, d), lambda b, t: (b, 0)), whole(wo), whole(cq), whole(cw)] + [ANY] * 5,
            out_specs=[pl.BlockSpec((DH, d), lambda b, t: (t[0, b], 0)),
                       pl.BlockSpec((DH, d), lambda b, t: (t[1, b], 0)), slot(wo), slot(cq), slot(cw)]),
        out_shape=[jax.ShapeDtypeStruct(b.shape, b.dtype) for b in bufs],
        input_output_aliases={5 + a: a for a in range(5)},
        compiler_params=_params(("arbitrary",)),
    )(_block_table(chip, True, G_SPARE, C_SPARE), a_shard, wo, cq, cw, *bufs)


def _tie(x, token, name):
    def body(x_ref, t_ref, o_ref):
        del x_ref, t_ref, o_ref

    return pl.pallas_call(
        body, name=name, in_specs=[ANY, ANY], out_specs=ANY,
        out_shape=jax.ShapeDtypeStruct(x.shape, x.dtype), input_output_aliases={0: 0},
    )(x, token)


def _gather_start(phase, a_shard, w_grp, singles):
    ns = len(singles)

    def issue(refs, send_sems, recv_sems):
        a_ref, w_ref = refs[0], refs[1]
        x, y, c, chips = _place()
        mine = 2 * x + y
        for jj, (px, py) in enumerate(chips):
            to = dict(device_id=(px, py, c), device_id_type=MESH)
            for a in range(ns):
                pltpu.make_async_remote_copy(
                    src_ref=refs[2 + 2 * a], dst_ref=refs[3 + 2 * a].at[mine],
                    send_sem=send_sems.at[(1 + ns) * jj + 1 + a], recv_sem=recv_sems.at[(1 + ns) * jj + 1 + a],
                    **to).start()
        for s in range(4):
            for par in range(2):
                blocks = _phase_blocks(s, phase, True, par)
                if blocks:
                    @pl.when((mine == s) & (c == par))
                    def _():
                        for jj, (px, py) in enumerate(chips):
                            for b, blk in blocks:
                                pltpu.make_async_remote_copy(
                                    src_ref=_blk(a_ref, b), dst_ref=_blk(w_ref, blk),
                                    send_sem=send_sems.at[(1 + ns) * jj], recv_sem=recv_sems.at[(1 + ns) * jj],
                                    device_id=(px, py, c), device_id_type=MESH).start()

    bufs = [a_shard, w_grp] + [t for pair in singles for t in pair]
    return _split_start("gather_start_" + phase, issue, bufs, 3 * (1 + ns))


def _gather_wait(phase, send_sems, recv_sems, bufs, after):
    ns = (len(bufs) - 2) // 2

    def await_(refs, send_sems, recv_sems):
        a_ref, w_ref = refs[0], refs[1]
        x, y, c, chips = _place()
        mine = 2 * x + y
        for jj, (px, py) in enumerate(chips):
            to = dict(device_id=(px, py, c), device_id_type=MESH)
            peer = 2 * px + py
            for a in range(ns):
                cp = pltpu.make_async_remote_copy(
                    src_ref=refs[2 + 2 * a], dst_ref=refs[3 + 2 * a].at[mine],
                    send_sem=send_sems.at[(1 + ns) * jj + 1 + a], recv_sem=recv_sems.at[(1 + ns) * jj + 1 + a], **to)
                cp.wait_recv()
                cp.wait_send()
            for s in range(4):
                for par in range(2):
                    nblk = len(_phase_blocks(s, phase, True, par))
                    if nblk:
                        both = pltpu.make_async_remote_copy(
                            src_ref=_cols(a_ref, nblk), dst_ref=_cols(w_ref, nblk),
                            send_sem=send_sems.at[(1 + ns) * jj], recv_sem=recv_sems.at[(1 + ns) * jj], **to)

                        @pl.when((peer == s) & (c == par))
                        def _():
                            both.wait_recv()

                        @pl.when((mine == s) & (c == par))
                        def _():
                            both.wait_send()

    return _split_wait("gather_wait_" + phase, await_, send_sems, recv_sems, bufs, after)


def _sibling_forward(phase, w_grp):
    def body(w_in_ref, w_ref, send_sems, recv_sems):
        del w_in_ref
        x, y, c, chips = _place()
        to = dict(device_id=(x, y, 1 - c), device_id_type=MESH)
        for jj, (px, py) in enumerate(chips):
            peer = 2 * px + py
            for s in range(4):
                for par in range(2):
                    mine_blocks = _phase_blocks(s, phase, True, par)
                    theirs = len(_phase_blocks(s, phase, True, 1 - par))
                    if not (mine_blocks or theirs):
                        continue

                    @pl.when((peer == s) & (c == par))
                    def _():
                        for _, blk in mine_blocks:
                            pltpu.make_async_remote_copy(
                                src_ref=_blk(w_ref, blk), dst_ref=_blk(w_ref, blk),
                                send_sem=send_sems.at[jj], recv_sem=recv_sems.at[jj], **to).start()
                        if theirs:
                            pltpu.make_async_remote_copy(
                                src_ref=_cols(w_ref, theirs), dst_ref=_cols(w_ref, theirs),
                                send_sem=send_sems.at[jj], recv_sem=recv_sems.at[jj], **to).wait_recv()
                        if mine_blocks:
                            pltpu.make_async_remote_copy(
                                src_ref=_cols(w_ref, len(mine_blocks)), dst_ref=_cols(w_ref, len(mine_blocks)),
                                send_sem=send_sems.at[jj], recv_sem=recv_sems.at[jj], **to).wait_send()

    return pl.pallas_call(
        body, name="sibling_forward_" + phase, in_specs=[ANY], out_specs=ANY,
        out_shape=jax.ShapeDtypeStruct(w_grp.shape, w_grp.dtype), input_output_aliases={0: 0},
        scratch_shapes=[pltpu.SemaphoreType.DMA((3,)), pltpu.SemaphoreType.DMA((3,))],
    )(w_grp)


def _merge_edges(w, edge0, mixed, name):
    d = w.shape[1]

    def body(e_ref, o_ref):
        o_ref[...] = e_ref[0:DH, :] + e_ref[DH:2 * DH, :]

    def to_block(i):
        r = mixed[-1]
        for kk in range(len(mixed) - 2, -1, -1):
            r = jnp.where(i == kk, mixed[kk], r)
        return r

    return pl.pallas_call(
        body, name=name, grid=(len(mixed),),
        in_specs=[pl.BlockSpec((2 * DH, d), lambda i: (edge0 // 2 + i, 0))],
        out_specs=pl.BlockSpec((DH, d), lambda i: (to_block(i), 0)),
        out_shape=jax.ShapeDtypeStruct(w.shape, w.dtype),
        input_output_aliases={0: 0},
        compiler_params=_params(("arbitrary",)),
    )(w)


def _scatter_start(phase, g_grp, land, singles):
    ns = len(singles)

    def issue(refs, send_sems, recv_sems):
        g_ref, land_ref = refs[0], refs[1]
        x, y, c, chips = _place()
        for jj, (px, py) in enumerate(chips):
            to = dict(device_id=(px, py, c), device_id_type=MESH)
            peer = 2 * px + py
            for a in range(ns):
                pltpu.make_async_remote_copy(
                    src_ref=refs[2 + 2 * a].at[peer], dst_ref=refs[3 + 2 * a].at[jj],
                    send_sem=send_sems.at[(1 + ns) * jj + 1 + a], recv_sem=recv_sems.at[(1 + ns) * jj + 1 + a],
                    **to).start()
            for s in range(4):
                blocks = _phase_blocks(s, phase, False)
                if blocks:
                    @pl.when(peer == s)
                    def _():
                        for b, blk in blocks:
                            pltpu.make_async_remote_copy(
                                src_ref=_blk(g_ref, blk), dst_ref=_blk(land_ref.at[jj], b),
                                send_sem=send_sems.at[(1 + ns) * jj], recv_sem=recv_sems.at[(1 + ns) * jj],
                                **to).start()

    bufs = [g_grp, land] + [t for pair in singles for t in pair]
    return _split_start("scatter_start_" + phase, issue, bufs, 3 * (1 + ns))


def _scatter_wait(phase, send_sems, recv_sems, bufs, after):
    ns = (len(bufs) - 2) // 2

    def await_(refs, send_sems, recv_sems):
        g_ref, land_ref = refs[0], refs[1]
        x, y, c, chips = _place()
        mine = 2 * x + y
        for jj, (px, py) in enumerate(chips):
            to = dict(device_id=(px, py, c), device_id_type=MESH)
            peer = 2 * px + py
            for a in range(ns):
                cp = pltpu.make_async_remote_copy(
                    src_ref=refs[2 + 2 * a].at[peer], dst_ref=refs[3 + 2 * a].at[jj],
                    send_sem=send_sems.at[(1 + ns) * jj + 1 + a], recv_sem=recv_sems.at[(1 + ns) * jj + 1 + a], **to)
                cp.wait_recv()
                cp.wait_send()
            for s in range(4):
                nblk = len(_phase_blocks(s, phase, False))
                if nblk:
                    both = pltpu.make_async_remote_copy(
                        src_ref=_cols(g_ref, nblk), dst_ref=_cols(land_ref.at[jj], nblk),
                        send_sem=send_sems.at[(1 + ns) * jj], recv_sem=recv_sems.at[(1 + ns) * jj], **to)

                    @pl.when(mine == s)
                    def _():
                        both.wait_recv()

                    @pl.when(peer == s)
                    def _():
                        both.wait_send()

    return _split_wait("scatter_wait_" + phase, await_, send_sems, recv_sems, bufs, after)


def _sum_shard(g_g, g_c, land):
    d = g_g.shape[1]
    chip = 2 * lax.axis_index("x") + lax.axis_index("y")

    def body(t_ref, gg_ref, gc_ref, land_ref, o_ref):
        own = jnp.where(t_ref[2, pl.program_id(0)] == 1, gg_ref[...].astype(F32), gc_ref[...].astype(F32))
        for jj in range(3):
            own = own + land_ref[jj].astype(F32)
        o_ref[...] = own

    return pl.pallas_call(
        body, name="sum_w_in",
        grid_spec=pltpu.PrefetchScalarGridSpec(
            num_scalar_prefetch=1, grid=(ALIGNED_BLOCKS,),
            in_specs=[pl.BlockSpec((DH, d), lambda b, t: (t[0, b], 0)), pl.BlockSpec((DH, d), lambda b, t: (t[1, b], 0)),
                      pl.BlockSpec((3, DH, d), lambda b, t: (0, b, 0))],
            out_specs=pl.BlockSpec((DH, d), lambda b, t: (b, 0))),
        out_shape=jax.ShapeDtypeStruct((ALIGNED_W, d), F32),
        compiler_params=_params(("arbitrary",)),
    )(_block_table(chip, False, 0, 0), g_g, g_c, land)


def _sum_rows(stack, land, rows):
    _, r, d = stack.shape
    rows = min(rows, r)
    chip = 2 * lax.axis_index("x") + lax.axis_index("y")

    def body(t_ref, own_ref, land_ref, o_ref):
        acc = own_ref[0].astype(F32)
        for jj in range(3):
            acc = acc + land_ref[jj].astype(F32)
        o_ref[...] = acc

    return pl.pallas_call(
        body, name="sum_w_out",
        grid_spec=pltpu.PrefetchScalarGridSpec(
            num_scalar_prefetch=1, grid=(r // rows,),
            in_specs=[pl.BlockSpec((1, rows, d), lambda i, t: (t[0], i, 0)),
                      pl.BlockSpec((3, rows, d), lambda i, t: (0, i, 0))],
            out_specs=pl.BlockSpec((rows, d), lambda i, t: (i, 0))),
        out_shape=jax.ShapeDtypeStruct((r, d), F32),
        compiler_params=_params(("arbitrary",)),
    )(jnp.reshape(chip, (1,)).astype(jnp.int32), stack, land)


def _final_exchange(parts, pack):
    npart = len(parts)

    def body(*refs):
        ins, pack_ref = refs[:npart], refs[npart]
        outs, packs = refs[npart + 1:2 * npart + 1], refs[2 * npart + 1]
        send_sems, recv_sems, psend, precv, loc_sem = refs[2 * npart + 2:]
        x, y, c, _ = _place()
        me = 4 * x + 2 * y + c
        local = pltpu.make_async_copy(pack_ref, packs.at[me], loc_sem)
        local.start()
        cps = [pltpu.make_async_remote_copy(
            src_ref=ins[a], dst_ref=outs[a], send_sem=send_sems.at[a], recv_sem=recv_sems.at[a],
            device_id=(x, y, 1 - c), device_id_type=MESH) for a in range(npart)]
        for r in range(1, 8):
            dx, dy, dc = (r >> 2) & 1, (r >> 1) & 1, r & 1
            peer = (x + dx - 2 * x * dx, y + dy - 2 * y * dy, c + dc - 2 * c * dc)
            cps.append(pltpu.make_async_remote_copy(
                src_ref=pack_ref, dst_ref=packs.at[me], send_sem=psend.at[r - 1], recv_sem=precv.at[r - 1],
                device_id=peer, device_id_type=MESH))
        for cp in cps:
            cp.start()
        for cp in cps:
            cp.wait_recv()
        for cp in cps:
            cp.wait_send()
        local.wait()

    return pl.pallas_call(
        body, name="final_exchange",
        in_specs=[ANY] * (npart + 1), out_specs=[ANY] * (npart + 1),
        out_shape=[jax.ShapeDtypeStruct(p.shape, p.dtype) for p in parts]
        + [jax.ShapeDtypeStruct((8,) + pack.shape, pack.dtype)],
        scratch_shapes=[pltpu.SemaphoreType.DMA((npart,)), pltpu.SemaphoreType.DMA((npart,)),
                        pltpu.SemaphoreType.DMA((7,)), pltpu.SemaphoreType.DMA((7,)), pltpu.SemaphoreType.DMA],
    )(*parts, pack)


def _sum_packs(packs):
    def body(p_ref, o_ref):
        acc = p_ref[0]
        for d in range(1, 8):
            acc = acc + p_ref[d]
        o_ref[...] = acc

    return pl.pallas_call(
        body, name="sum_packs", out_shape=jax.ShapeDtypeStruct(packs.shape[1:], F32),
    )(packs)


def _adamw_update(g, w_ref, m_ref, v_ref, go, do, mo, vo):
    c1 = 1.0 / (1.0 - ADAM_B1 ** ADAM_STEP)
    c2 = 1.0 / (1.0 - ADAM_B2 ** ADAM_STEP)
    mn = ADAM_B1 * m_ref[...] + (1.0 - ADAM_B1) * g
    vn = ADAM_B2 * v_ref[...] + (1.0 - ADAM_B2) * (g * g)
    go[...] = g
    mo[...] = mn
    vo[...] = vn
    do[...] = -ADAM_LR * ((mn * c1) / (jnp.sqrt(vn * c2) + ADAM_EPS) + ADAM_WD * w_ref[...])


def _adamw(w, m, v, g1, g2, rows, name):
    r, cdim = w.shape
    rows = min(rows, r)

    def body(*refs):
        n_in = 4 if g2 is None else 5
        w_ref, m_ref, v_ref, g_ref = refs[:4]
        g = g_ref[...] if g2 is None else g_ref[...] + refs[4][...]
        _adamw_update(g, w_ref, m_ref, v_ref, *refs[n_in:n_in + 4])

    blk = pl.BlockSpec((rows, cdim), lambda i: (i, 0))
    args = [w, m, v, g1] + ([] if g2 is None else [g2])
    shp = jax.ShapeDtypeStruct((r, cdim), F32)
    return pl.pallas_call(
        body, name=name, grid=(r // rows,),
        in_specs=[blk] * len(args), out_specs=[blk] * 4, out_shape=[shp] * 4,
        compiler_params=_params(("parallel",), 20 * rows * cdim * 4 + 8 * 2**20),
    )(*args)


def _adamw_shard(wt, mt, vt, g1, g2):
    r, d = wt.shape
    cols = min(128, d)

    def body(w_ref, m_ref, v_ref, g_ref, g2_ref, go, do, mo, vo, pad_ref):
        chip = 2 * lax.axis_index("x") + lax.axis_index("y")
        back = [(ALIGNED_W - s) % ALIGNED_W for s in SHIFTS]
        pad_ref[...] = pltpu.roll(g_ref[...] + g2_ref[...], _by_chip(chip, back), 0)
        _adamw_update(pad_ref[0:r, :], w_ref, m_ref, v_ref, go, do, mo, vo)

    blk = pl.BlockSpec((r, cols), lambda i: (0, i))
    gblk = pl.BlockSpec((ALIGNED_W, cols), lambda i: (0, i))
    shp = jax.ShapeDtypeStruct((r, d), F32)
    return pl.pallas_call(
        body, name="adamw_w_in", grid=(d // cols,),
        in_specs=[blk] * 3 + [gblk] * 2, out_specs=[blk] * 4, out_shape=[shp] * 4,
        scratch_shapes=[pltpu.VMEM((ALIGNED_W, cols), F32)],
        compiler_params=_params(("parallel",), 24 * ALIGNED_W * cols * 4 + 8 * 2**20),
    )(wt, mt, vt, g1, g2)


def _pad_lanes(a, width):
    return jnp.pad(a, ((0, 0), (0, width - a.shape[1])))


def _gathered_to_full(g):
    return jnp.transpose(g, (1, 0, 2)).reshape(g.shape[1], 4 * g.shape[2])


def _row(a):
    return _pad_lanes(a.reshape(1, -1), 1024)


def _small_pack(nin, cb, fn, al, dt, gn, cqw_shard, cw_shard):
    ad = jnp.concatenate([al.reshape(1, -1), dt.reshape(1, -1)], axis=1)
    rows = [_row(nin), _row(cb), _row(fn), _row(ad), _row(gn), cqw_shard.reshape(3, 1024), _row(cw_shard)]
    out = jnp.concatenate(rows, axis=0)
    return jnp.pad(out, ((0, 16 - out.shape[0]), (0, 0)))


def kernel(x, norm_in_w, w_in, conv_qkv_w, A_log, dt_bias, gdn_norm_w, conv_w, conv_b, w_out, final_norm_w, loss_target, m_norm_in_w, m_w_in, m_conv_qkv_w, m_A_log, m_dt_bias, m_gdn_norm_w, m_conv_w, m_conv_b, m_w_out, m_final_norm_w, v_norm_in_w, v_w_in, v_conv_qkv_w, v_A_log, v_dt_bias, v_gdn_norm_w, v_conv_w, v_conv_b, v_w_out, v_final_norm_w):
    chip = 2 * lax.axis_index("x") + lax.axis_index("y")
    a_shard = _align_shard(jnp.transpose(w_in[0]))
    wo_b = _cast_bf16(w_out[0], 256, "cast_w_out")
    d_model = x.shape[-1]
    stack = lambda s: lax.empty((4,) + s.shape, s.dtype)
    wg0 = lax.empty((WG_BLOCKS * DH, d_model), BF16)
    wc0 = lax.empty((WC_BLOCKS * DH, d_model), BF16)
    ss_g, rs_g, bufs_g, tok_g = _gather_start("g", a_shard, wg0, [(conv_qkv_w[0], stack(conv_qkv_w[0]))])
    ss_c, rs_c, bufs_c, tok_c = _gather_start("c", bufs_g[0], wc0,
                                              [(conv_w[0], stack(conv_w[0])), (wo_b, stack(wo_b))])
    wg1, wc1, wog1, cqg1, cwg1 = _place_own(bufs_c[0], bufs_c[4], bufs_g[2], bufs_c[2],
                                            [bufs_g[1], bufs_c[1], bufs_c[5], bufs_g[3], bufs_c[3]])
    x0 = x[0]
    h = _tie(_tie(_rms_in(x0, norm_in_w), tok_g, "after_gather_start_g"), tok_c, "after_gather_start_c")
    a_thru, wg, _, cq_g = _gather_wait("g", ss_g, rs_g, [bufs_c[0], wg1, bufs_g[2], cqg1], h)
    w_g = _merge_edges(_sibling_forward("g", wg), G_EDGE, G_MIXED, "merge_edges_g")
    cqw = _gathered_to_full(cq_g)
    ad = jnp.pad(jnp.concatenate([A_log, dt_bias], axis=0), ((0, 0), (A_LANE, 0)))

    def late(o):
        _, wc, _, cw_g, _, wo_g = _gather_wait("c", ss_c, rs_c,
                                               [a_thru, wc1, bufs_c[2], cwg1, bufs_c[4], wog1], o)
        return (_merge_edges(_sibling_forward("c", wc), C_EDGE, C_MIXED, "merge_edges_c"),
                wo_g.reshape(2 * GW, d_model),
                _gathered_to_full(cw_g))

    scat = {}

    def on_grad_c(g_c, g_wout, do):
        go4 = g_wout.reshape(4, GW // 2, d_model)
        land = lax.empty((3, ALIGNED_W, d_model), BF16)
        land_o = lax.empty((3, GW // 2, d_model), BF16)
        ss, rs, bufs, tok = _scatter_start("c", g_c, land, [(go4, land_o)])
        scat["c"] = (ss, rs, bufs)
        return _tie(do, tok, "after_scatter_start_c")

    def on_grad_g(g_g, dproj_g):
        ss, rs, bufs, tok = _scatter_start("g", g_g, scat["c"][2][1], [])
        scat["g"] = (ss, rs, bufs)
        return _tie(dproj_g, tok, "after_scatter_start_g")

    gx, sm, _ = _local_step(x0, loss_target[0], h, w_g, cqw, late, norm_in_w, ad, gdn_norm_w, conv_b,
                            final_norm_w.reshape(1, -1), on_grad_c, on_grad_g)

    ss, rs, bufs = scat["g"]
    g_g, land = _scatter_wait("g", ss, rs, bufs, gx)
    ss, rs, bufs = scat["c"]
    g_c, land, go4, land_o = _scatter_wait("c", ss, rs, [bufs[0], land, bufs[2], bufs[3]], gx)
    part_in = _sum_shard(g_g, g_c, land)
    part_out = _sum_rows(go4, land_o, 128)
    ad_g = jnp.concatenate([sm["al"][:, A_LANE:], sm["dt"][:, A_LANE:]], axis=1)
    pack = jnp.concatenate([_row(sm["nin"]), _row(sm["cb"]), _row(sm["fn"]), _row(ad_g), _row(sm["gn"]),
                            jnp.concatenate(sm["cq"], axis=1).reshape(12, 1024), sm["cw"], _row(sm["loss"])], axis=0)
    pack = jnp.pad(pack, ((0, PACK_ROWS - pack.shape[0]), (0, 0)))
    sib_in, sib_out, packs = _final_exchange([part_in, part_out], pack)
    tot = _sum_packs(packs)

    g_wi, d_wi, m_wi, v_wi = [jnp.transpose(a) for a in _adamw_shard(
        jnp.transpose(w_in[0]), jnp.transpose(m_w_in[0]), jnp.transpose(v_w_in[0]), part_in, sib_in)]
    g_wo, d_wo, m_wo, v_wo = _adamw(w_out[0], m_w_out[0], v_w_out[0], part_out, sib_out, 128, "adamw_w_out")
    g_cq_sh = lax.dynamic_slice_in_dim(tot[R_CQ:R_CQ + 12].reshape(4, 3 * GW), chip * 768, 768, axis=1)
    g_cw_sh = lax.dynamic_slice_in_dim(tot[R_CW:R_CW + 3], chip * 256, 256, axis=1)
    sp = lambda nin, cb, fn, al, dt, gn, cq, cwv: _small_pack(nin, cb, fn, al, dt, gn, cq[0], cwv[0])
    g_s = _small_pack(tot[R_NIN], tot[R_CB], tot[R_FN], tot[R_AD, :HEADS], tot[R_AD, HEADS:2 * HEADS],
                      tot[R_GN, :DH], g_cq_sh, g_cw_sh)
    w_s = sp(norm_in_w, conv_b, final_norm_w, A_log, dt_bias, gdn_norm_w, conv_qkv_w, conv_w)
    m_s = sp(m_norm_in_w, m_conv_b, m_final_norm_w, m_A_log, m_dt_bias, m_gdn_norm_w, m_conv_qkv_w, m_conv_w)
    v_s = sp(v_norm_in_w, v_conv_b, v_final_norm_w, v_A_log, v_dt_bias, v_gdn_norm_w, v_conv_qkv_w, v_conv_w)
    small = _adamw(w_s, m_s, v_s, g_s, None, 16, "adamw_small")

    def unpack(a, big_in, big_out):
        return (a[0:1], big_in[None], a[5:8].reshape(1, 4, 768), a[3:4, :HEADS], a[3:4, HEADS:2 * HEADS],
                a[4:5, :DH], a[8, :768].reshape(1, 3, 256), a[1:2], big_out[None], a[2])

    loss = tot[R_LOSS, 0]
    return (loss, gx[None], *unpack(small[0], g_wi, g_wo), *unpack(small[1], d_wi, d_wo),
            *unpack(small[2], m_wi, m_wo), *unpack(small[3], v_wi, v_wo))
```

```python
import functools
import math

import jax
import jax.numpy as jnp
from jax import lax
from jax.experimental import pallas as pl
from jax.experimental.pallas import tpu as pltpu

F32 = jnp.float32
BF16 = jnp.bfloat16
MESH = pl.DeviceIdType.MESH
ANY = pl.BlockSpec(memory_space=pl.ANY)

HEADS = 8
DH = 128
CH = 64
GW = HEADS * DH
EPS = 1e-6
VMEM_V7X = 64 * 1024 * 1024

QB, KB, VB, ZB, BAB = 0, 8, 16, 24, 32
A_LANE = 120
NG, NC = 33, 32
GW_COLS, CW_COLS = NG * DH, NC * DH

SHARD_W = 2052
ALIGNED_BLOCKS = 17
ALIGNED_W = ALIGNED_BLOCKS * DH
SHIFTS = (0, 4, ALIGNED_W - 8, ALIGNED_W - 4)
G_EDGE, C_EDGE = 34, 32
G_SPARE, C_SPARE = 33, 34
WG_BLOCKS, WC_BLOCKS = 38, 36
G_MIXED, C_MIXED = (2, BAB), (4 * 7 + 1,)


def _shard_blocks(chip, edges):
    g, c = "g", "c"
    if chip == 0:
        out = [(g, 3 * b) for b in range(8)] + [(g, 3 * b + 1) for b in range(8)] + [(g, G_EDGE, G_MIXED[0])]
    elif chip == 1:
        out = [(g, G_EDGE + 1, G_MIXED[0])] + [(g, 3 * b + 2) for b in range(1, 8)]
        out += [(g, ZB + b) for b in range(8)] + [(g, G_EDGE + 2, G_MIXED[1])]
    elif chip == 2:
        out = [(c, 4 * b) for b in range(8)] + [(c, 4 * b + 1) for b in range(7)]
        out += [(c, C_EDGE, C_MIXED[0]), (g, G_EDGE + 3, G_MIXED[1])]
    else:
        out = [(c, 4 * b + 2) for b in range(8)] + [(c, 4 * b + 3) for b in range(8)] + [(c, C_EDGE + 1, C_MIXED[0])]
    return [(o[0], o[1] if (edges or len(o) == 2) else o[2]) for o in out]


def _by_chip(chip, vals):
    if all(v == vals[0] for v in vals):
        return vals[0]
    r = vals[3]
    for kk in (2, 1, 0):
        r = jnp.where(chip == kk, vals[kk], r)
    return r

ADAM_LR, ADAM_B1, ADAM_B2, ADAM_EPS, ADAM_WD, ADAM_STEP = 0.001, 0.9, 0.999, 1e-08, 0.01, 10

R_NIN, R_CB, R_FN, R_AD, R_GN, R_CQ, R_CW, R_LOSS, PACK_ROWS = 0, 1, 2, 3, 4, 5, 17, 20, 24

NN = ((1,), (0,))
NT = ((1,), (1,))
TN = ((0,), (0,))


def _dot(a, b, dims=NN, mode="lo"):
    dn = (dims, ((), ()))
    if mode == "hi":
        return lax.dot_general(a, b, dn, precision=lax.Precision.HIGHEST, preferred_element_type=F32)
    ah, bh = a.astype(BF16), b.astype(BF16)
    out = lax.dot_general(ah, bh, dn, preferred_element_type=F32)
    if mode == "x3":
        al = (a - ah.astype(F32)).astype(BF16)
        bl = (b - bh.astype(F32)).astype(BF16)
        out = out + lax.dot_general(ah, bl, dn, preferred_element_type=F32)
        out = out + lax.dot_general(al, bh, dn, preferred_element_type=F32)
    return out


P_GRAM, P_INV, P_SOL, P_SCAN, P_SCANB, P_BWD = "lo", "lo", "lo", "lo", "lo", "lo"
P_CUM = "x3"


def _params(sem=None, vmem=None):
    kw = {}
    if sem is not None:
        kw["dimension_semantics"] = sem
    if vmem is not None:
        kw["vmem_limit_bytes"] = int(min(max(vmem, 32 * 2**20), VMEM_V7X - 8 * 2**20))
    return pltpu.CompilerParams(**kw)


def _sigmoid(x):
    return 1.0 / (1.0 + jnp.exp(-x))


def _dsilu(x, s):
    return s * (1.0 + x * (1.0 - s))


def _rows(shape):
    return lax.broadcasted_iota(jnp.int32, shape, 0)


def _shift_down(x, s):
    if s == 0:
        return x
    return jnp.where(_rows(x.shape) >= s, pltpu.roll(x, s, 0), 0.0)


def _shift_up(x, s):
    if s == 0:
        return x
    n = x.shape[0]
    return jnp.where(_rows(x.shape) < n - s, pltpu.roll(x, n - s, 0), 0.0)


def _matmul(a, b, dims, out_dtype, tm, tn, tk, name, add=None, n=None):
    if dims == NN:
        (m, k), n = a.shape, b.shape[1]
    elif dims == NT:
        (m, k), n = a.shape, (n or b.shape[0])
    else:
        (k, m), n = a.shape, b.shape[1]
    tm, tn, tk = min(tm, m), min(tn, n), min(tk, k)
    assert m % tm == 0 and n % tn == 0 and k % tk == 0, (name, m, n, k, tm, tn, tk)
    nk = k // tk

    def body(*refs):
        if add is None:
            a_ref, b_ref, o_ref = refs[:3]
            add_ref = None
        else:
            a_ref, b_ref, add_ref, o_ref = refs[:4]
        part = _dot(a_ref[...], b_ref[...], dims)
        if nk == 1:
            if add_ref is not None:
                part = part + add_ref[...]
            o_ref[...] = part.astype(out_dtype)
            return
        acc = refs[-1]
        kk = pl.program_id(2)

        @pl.when(kk == 0)
        def _():
            acc[...] = part

        @pl.when(kk > 0)
        def _():
            acc[...] += part

        @pl.when(kk == nk - 1)
        def _():
            r = acc[...]
            if add_ref is not None:
                r = r + add_ref[...]
            o_ref[...] = r.astype(out_dtype)

    if dims == TN:
        a_spec = pl.BlockSpec((tk, tm), lambda i, j, kk: (kk, i))
    else:
        a_spec = pl.BlockSpec((tm, tk), lambda i, j, kk: (i, kk))
    if dims == NT:
        b_spec = pl.BlockSpec((tn, tk), lambda i, j, kk: (j, kk))
    else:
        b_spec = pl.BlockSpec((tk, tn), lambda i, j, kk: (kk, j))
    o_spec = pl.BlockSpec((tm, tn), lambda i, j, kk: (i, j))
    in_specs = [a_spec, b_spec]
    args = [a, b]
    if add is not None:
        in_specs.append(o_spec)
        args.append(add)
    osz = jnp.dtype(out_dtype).itemsize
    est = 2 * (tm * tk * a.dtype.itemsize + tk * tn * b.dtype.itemsize + tm * tn * osz)
    est += 3 * tm * tn * 4 + (2 * tm * tn * 4 if add is not None else 0)
    return pl.pallas_call(
        body, name=name, grid=(m // tm, n // tn, nk),
        in_specs=in_specs, out_specs=o_spec,
        out_shape=jax.ShapeDtypeStruct((m, n), out_dtype),
        scratch_shapes=[pltpu.VMEM((tm, tn), F32)] if nk > 1 else [],
        compiler_params=_params(("parallel", "parallel", "arbitrary"), est + 8 * 2**20),
    )(*args)


def _cast_bf16(a, rows, name):
    r, c = a.shape
    rows = min(rows, r)

    def body(a_ref, o_ref):
        o_ref[...] = a_ref[...].astype(BF16)

    return pl.pallas_call(
        body, name=name, grid=(r // rows,),
        in_specs=[pl.BlockSpec((rows, c), lambda i: (i, 0))],
        out_specs=pl.BlockSpec((rows, c), lambda i: (i, 0)),
        out_shape=jax.ShapeDtypeStruct((r, c), BF16),
        compiler_params=_params(("parallel",)),
    )(a)


def _align_shard(wt):
    r, d = wt.shape
    cols = min(256, d)

    def body(w_ref, o_ref, pad_ref):
        chip = 2 * lax.axis_index("x") + lax.axis_index("y")
        pad_ref[...] = jnp.zeros_like(pad_ref)
        pad_ref[0:r, :] = w_ref[...]
        o_ref[...] = pltpu.roll(pad_ref[...], _by_chip(chip, SHIFTS), 0).astype(BF16)

    return pl.pallas_call(
        body, name="align_shard", grid=(d // cols,),
        in_specs=[pl.BlockSpec((r, cols), lambda i: (0, i))],
        out_specs=pl.BlockSpec((ALIGNED_W, cols), lambda i: (0, i)),
        out_shape=jax.ShapeDtypeStruct((ALIGNED_W, d), BF16),
        scratch_shapes=[pltpu.VMEM((ALIGNED_W, cols), F32)],
        compiler_params=_params(("parallel",)),
    )(wt)


def _rms_in(x, w):
    n, d = x.shape
    tr = min(256, n)

    def body(x_ref, w_ref, h_ref):
        xv = x_ref[...]
        r = lax.rsqrt(jnp.mean(xv * xv, axis=-1, keepdims=True) + EPS)
        h_ref[...] = (xv * r * w_ref[...]).astype(BF16)

    return pl.pallas_call(
        body, name="rms_in", grid=(n // tr,),
        in_specs=[pl.BlockSpec((tr, d), lambda i: (i, 0)), pl.BlockSpec((1, d), lambda i: (0, 0))],
        out_specs=pl.BlockSpec((tr, d), lambda i: (i, 0)),
        out_shape=jax.ShapeDtypeStruct((n, d), BF16),
        compiler_params=_params(("parallel",)),
    )(x, w)


def _conv_silu(p, w_ref, taps):
    c = None
    for j in range(taps):
        t = _shift_down(p, taps - 1 - j) * w_ref[j:j + 1, :]
        c = t if c is None else c + t
    return c


def _prep_qkv(proj, cw):
    n = proj.shape[0]

    def body(p3, wq, wk, wv, q_ref, k_ref, v_ref):
        for kind, (w_ref, o_ref) in enumerate(((wq, q_ref), (wk, k_ref), (wv, v_ref))):
            c = _conv_silu(p3[:, kind * DH:(kind + 1) * DH], w_ref, 4)
            a = c * _sigmoid(c)
            if kind < 2:
                r = lax.rsqrt(jnp.sum(a * a, axis=-1, keepdims=True) + EPS)
                a = a * (r * (DH ** -0.5 if kind == 0 else 1.0))
            o_ref[...] = a

    col = pl.BlockSpec((n, DH), lambda h: (0, h))
    wcol = lambda base: pl.BlockSpec((4, DH), lambda h: (0, base + h))
    out = jax.ShapeDtypeStruct((n, GW), F32)
    return pl.pallas_call(
        body, name="prep_qkv", grid=(HEADS,),
        in_specs=[pl.BlockSpec((n, 3 * DH), lambda h: (0, h)), wcol(QB), wcol(KB), wcol(VB)],
        out_specs=[col] * 3, out_shape=[out] * 3,
        compiler_params=_params(("parallel",), 40 * 2**20),
    )(proj, cw, cw, cw)


def _prep_qkv_bwd(proj, cw, dq, dk, dv, dproj):
    n = proj.shape[0]

    def body(p3, wq, wk, wv, dq_ref, dk_ref, dv_ref, _, o3, gq, gk, gv):
        for kind, (w_ref, d_ref, g_ref) in enumerate(((wq, dq_ref, gq), (wk, dk_ref, gk), (wv, dv_ref, gv))):
            p = p3[:, kind * DH:(kind + 1) * DH]
            c = _conv_silu(p, w_ref, 4)
            s = _sigmoid(c)
            a = c * s
            d = d_ref[...]
            if kind < 2:
                r = lax.rsqrt(jnp.sum(a * a, axis=-1, keepdims=True) + EPS)
                sc = DH ** -0.5 if kind == 0 else 1.0
                d = (sc * r) * (d - a * ((r * r) * jnp.sum(d * a, axis=-1, keepdims=True)))
            dc = d * _dsilu(c, s)
            dp = None
            for j in range(4):
                g_ref[j:j + 1, :] = jnp.sum(dc * _shift_down(p, 3 - j), axis=0, keepdims=True)
                t = _shift_up(dc, 3 - j) * w_ref[j:j + 1, :]
                dp = t if dp is None else dp + t
            o3[:, kind * DH:(kind + 1) * DH] = dp.astype(BF16)

    col = pl.BlockSpec((n, DH), lambda h: (0, h))
    wcol = lambda base: pl.BlockSpec((4, DH), lambda h: (0, base + h))
    p3spec = pl.BlockSpec((n, 3 * DH), lambda h: (0, h))
    return pl.pallas_call(
        body, name="prep_qkv_bwd", grid=(HEADS,),
        in_specs=[p3spec, wcol(QB), wcol(KB), wcol(VB), col, col, col, ANY],
        out_specs=[p3spec] + [wcol(0)] * 3,
        out_shape=[jax.ShapeDtypeStruct(dproj.shape, BF16)] + [jax.ShapeDtypeStruct((4, GW), F32)] * 3,
        input_output_aliases={7: 0},
        compiler_params=_params(("parallel",), 48 * 2**20),
    )(proj, cw, cw, cw, dq, dk, dv, dproj)


def _tri(lower_incl):
    i = lax.broadcasted_iota(jnp.int32, (CH, CH), 0)
    j = lax.broadcasted_iota(jnp.int32, (CH, CH), 1)
    return jnp.where(i >= j, 1.0, 0.0) if lower_incl else jnp.where(j >= i, 1.0, 0.0)


def _lane(shape):
    return lax.broadcasted_iota(jnp.int32, shape, 1)


def _prep_bg(proj, ad):
    n = proj.shape[0]
    nch = n // CH

    def body(p_ref, ad_ref, bg_ref, bgt_ref):
        p = p_ref[...]
        lane = _lane(p.shape)
        beta = _sigmoid(p)
        xa = p + ad_ref[1:2, :]
        sp = jnp.maximum(xa, 0.0) + jnp.log(1.0 + jnp.exp(-jnp.abs(xa)))
        g = pltpu.roll(-jnp.exp(ad_ref[0:1, :]) * sp, DH - A_LANE + HEADS, 1)
        gc = _dot(_tri(True), g, NN, P_CUM)
        bg = jnp.where(lane < HEADS, beta, jnp.where(lane < 2 * HEADS, gc, 0.0))
        bg_ref[...] = bg
        bgt_ref[0] = bg.T

    return pl.pallas_call(
        body, name="prep_bg", grid=(nch,),
        in_specs=[pl.BlockSpec((CH, DH), lambda i: (i, BAB)), pl.BlockSpec((2, DH), lambda i: (0, 0))],
        out_specs=[pl.BlockSpec((CH, DH), lambda i: (i, 0)), pl.BlockSpec((1, DH, CH), lambda i: (i, 0, 0))],
        out_shape=[jax.ShapeDtypeStruct((n, DH), F32), jax.ShapeDtypeStruct((nch, DH, CH), F32)],
        compiler_params=_params(("parallel",)),
    )(proj, ad)


def _prep_bg_bwd(proj, ad, dbg, dproj):
    n = proj.shape[0]
    nch = n // CH

    def body(p_ref, ad_ref, d_ref, _, o_ref, ga_ref, gd_ref):
        p = p_ref[...]
        d = d_ref[...]
        lane = _lane(p.shape)
        beta = _sigmoid(p)
        xa = p + ad_ref[1:2, :]
        sp = jnp.maximum(xa, 0.0) + jnp.log(1.0 + jnp.exp(-jnp.abs(xa)))
        na = -jnp.exp(ad_ref[0:1, :])
        dg = pltpu.roll(_dot(_tri(False), d, NN, P_CUM), A_LANE - HEADS, 1)
        da = dg * na * _sigmoid(xa)
        is_g = lane >= A_LANE
        o_ref[...] = jnp.where(lane < HEADS, d * beta * (1.0 - beta), jnp.where(is_g, da, 0.0)).astype(BF16)
        ga = jnp.sum(jnp.where(is_g, dg * na * sp, 0.0), axis=0, keepdims=True)
        gd = jnp.sum(jnp.where(is_g, da, 0.0), axis=0, keepdims=True)

        @pl.when(pl.program_id(0) == 0)
        def _():
            ga_ref[...] = jnp.zeros_like(ga_ref)
            gd_ref[...] = jnp.zeros_like(gd_ref)

        ga_ref[...] += ga
        gd_ref[...] += gd

    one = pl.BlockSpec((1, DH), lambda i: (0, 0))
    return pl.pallas_call(
        body, name="prep_bg_bwd", grid=(nch,),
        in_specs=[pl.BlockSpec((CH, DH), lambda i: (i, BAB)), pl.BlockSpec((2, DH), lambda i: (0, 0)),
                  pl.BlockSpec((CH, DH), lambda i: (i, 0)), ANY],
        out_specs=[pl.BlockSpec((CH, DH), lambda i: (i, BAB)), one, one],
        out_shape=[jax.ShapeDtypeStruct(dproj.shape, BF16), jax.ShapeDtypeStruct((1, DH), F32),
                   jax.ShapeDtypeStruct((1, DH), F32)],
        input_output_aliases={3: 0},
        compiler_params=_params(("arbitrary",)),
    )(proj, ad, dbg, dproj)


def _gdn_out(o, proj, wg):
    n = o.shape[0]

    def body(o_ref, z_ref, w_ref, y_ref):
        ov, z = o_ref[...], z_ref[...]
        r = lax.rsqrt(jnp.mean(ov * ov, axis=-1, keepdims=True) + EPS)
        y_ref[...] = (ov * r * w_ref[...] * (z * _sigmoid(z))).astype(BF16)

    return pl.pallas_call(
        body, name="gdn_out", grid=(HEADS,),
        in_specs=[pl.BlockSpec((n, DH), lambda h: (0, h)), pl.BlockSpec((n, DH), lambda h: (0, ZB + h)),
                  pl.BlockSpec((1, DH), lambda h: (0, 0))],
        out_specs=pl.BlockSpec((n, DH), lambda h: (0, h)),
        out_shape=jax.ShapeDtypeStruct((n, 2 * GW), BF16),
        compiler_params=_params(("parallel",)),
    )(o, proj, wg)


def _gdn_out_bwd(o, proj, wg, dmix):
    n = o.shape[0]

    def body(o_ref, z_ref, w_ref, d_ref, do_ref, dz_ref, gw_ref):
        ov, z, d, w = o_ref[...], z_ref[...], d_ref[...], w_ref[...]
        r = lax.rsqrt(jnp.mean(ov * ov, axis=-1, keepdims=True) + EPS)
        nrm = ov * r
        s = _sigmoid(z)
        dz_ref[...] = (d * (nrm * w) * _dsilu(z, s)).astype(BF16)
        dn_w = d * (z * s)
        gw = jnp.sum(dn_w * nrm, axis=0, keepdims=True)
        dn = dn_w * w
        do_ref[...] = r * (dn - nrm * jnp.mean(dn * nrm, axis=-1, keepdims=True))

        @pl.when(pl.program_id(0) == 0)
        def _():
            gw_ref[...] = jnp.zeros_like(gw_ref)

        gw_ref[...] += gw

    return pl.pallas_call(
        body, name="gdn_out_bwd", grid=(HEADS,),
        in_specs=[pl.BlockSpec((n, DH), lambda h: (0, h)), pl.BlockSpec((n, DH), lambda h: (0, ZB + h)),
                  pl.BlockSpec((1, DH), lambda h: (0, 0)), pl.BlockSpec((n, DH), lambda h: (0, h))],
        out_specs=[pl.BlockSpec((n, DH), lambda h: (0, h)), pl.BlockSpec((n, DH), lambda h: (0, ZB + h)),
                   pl.BlockSpec((1, DH), lambda h: (0, 0))],
        out_shape=[jax.ShapeDtypeStruct((n, GW), F32), jax.ShapeDtypeStruct((n, GW_COLS), BF16),
                   jax.ShapeDtypeStruct((1, DH), F32)],
        compiler_params=_params(("arbitrary",)),
    )(o, proj, wg, dmix)


def _conv_branch(proj, w3, b, mix):
    n = proj.shape[0]

    def body(p4, w_ref, b_ref, _, y_ref):
        u = p4[:, DH:2 * DH] * p4[:, 2 * DH:3 * DH]
        cc = _conv_silu(u, w_ref, 3) + b_ref[...]
        z = p4[:, 3 * DH:4 * DH]
        y_ref[...] = (p4[:, 0:DH] * cc * (z * _sigmoid(z))).astype(BF16)

    return pl.pallas_call(
        body, name="conv_branch", grid=(HEADS,),
        in_specs=[pl.BlockSpec((n, 4 * DH), lambda h: (0, h)), pl.BlockSpec((3, DH), lambda h: (0, h)),
                  pl.BlockSpec((1, DH), lambda h: (0, h)), ANY],
        out_specs=pl.BlockSpec((n, DH), lambda h: (0, HEADS + h)),
        out_shape=jax.ShapeDtypeStruct(mix.shape, BF16),
        input_output_aliases={3: 0},
        compiler_params=_params(("parallel",), 40 * 2**20),
    )(proj, w3, b, mix)


def _conv_branch_bwd(proj, w3, b, dmix):
    n = proj.shape[0]

    def body(p4, w_ref, b_ref, d_ref, o4, gw_ref, gbias_ref):
        gb, gcv, hc, z = p4[:, 0:DH], p4[:, DH:2 * DH], p4[:, 2 * DH:3 * DH], p4[:, 3 * DH:4 * DH]
        d = d_ref[...]
        dgb, dgc, dhc, dzc = (o4.at[:, kk * DH:(kk + 1) * DH] for kk in range(4))
        u = gcv * hc
        cc = _conv_silu(u, w_ref, 3) + b_ref[...]
        s = _sigmoid(z)
        dzc[...] = (d * (gb * cc) * _dsilu(z, s)).astype(BF16)
        dp = d * (z * s)
        dgb[...] = (dp * cc).astype(BF16)
        dcc = dp * gb
        gbias_ref[...] = jnp.sum(dcc, axis=0, keepdims=True)
        du = None
        for j in range(3):
            gw_ref[j:j + 1, :] = jnp.sum(dcc * _shift_down(u, 2 - j), axis=0, keepdims=True)
            t = _shift_up(dcc, 2 - j) * w_ref[j:j + 1, :]
            du = t if du is None else du + t
        dgc[...] = (du * hc).astype(BF16)
        dhc[...] = (du * gcv).astype(BF16)

    p4spec = pl.BlockSpec((n, 4 * DH), lambda h: (0, h))
    return pl.pallas_call(
        body, name="conv_branch_bwd", grid=(HEADS,),
        in_specs=[p4spec, pl.BlockSpec((3, DH), lambda h: (0, h)), pl.BlockSpec((1, DH), lambda h: (0, h)),
                  pl.BlockSpec((n, DH), lambda h: (0, HEADS + h))],
        out_specs=[p4spec, pl.BlockSpec((3, DH), lambda h: (0, h)), pl.BlockSpec((1, DH), lambda h: (0, h))],
        out_shape=[jax.ShapeDtypeStruct((n, CW_COLS), BF16), jax.ShapeDtypeStruct((3, GW), F32),
                   jax.ShapeDtypeStruct((1, GW), F32)],
        compiler_params=_params(("parallel",), 48 * 2**20),
    )(proj, w3, b, dmix)


def _final_loss(out, tgt, wf):
    n, d = out.shape
    tr = min(256, n)

    def body(o_ref, t_ref, w_ref, do_ref, dob_ref, gw_ref, loss_ref):
        ov, w = o_ref[...], w_ref[...]
        r = lax.rsqrt(jnp.mean(ov * ov, axis=-1, keepdims=True) + EPS)
        nrm = ov * r
        e = nrm * w - t_ref[...]
        dy = e * (1.0 / d)
        dn = dy * w
        dout = r * (dn - nrm * jnp.mean(dn * nrm, axis=-1, keepdims=True))
        do_ref[...] = dout
        dob_ref[...] = dout.astype(BF16)

        @pl.when(pl.program_id(0) == 0)
        def _():
            gw_ref[...] = jnp.zeros_like(gw_ref)
            loss_ref[...] = jnp.zeros_like(loss_ref)

        gw_ref[...] += jnp.sum(dy * nrm, axis=0, keepdims=True)
        loss_ref[...] += (0.5 / d) * jnp.sum(jnp.sum(e * e, axis=-1, keepdims=True), axis=0, keepdims=True)

    row = pl.BlockSpec((tr, d), lambda i: (i, 0))
    return pl.pallas_call(
        body, name="final_loss", grid=(n // tr,),
        in_specs=[row, row, pl.BlockSpec((1, d), lambda i: (0, 0))],
        out_specs=[row, row, pl.BlockSpec((1, d), lambda i: (0, 0)), pl.BlockSpec((1, 1), lambda i: (0, 0))],
        out_shape=[jax.ShapeDtypeStruct((n, d), F32), jax.ShapeDtypeStruct((n, d), BF16),
                   jax.ShapeDtypeStruct((1, d), F32), jax.ShapeDtypeStruct((1, 1), F32)],
        compiler_params=_params(("arbitrary",)),
    )(out, tgt, wf)


def _rms_in_bwd(x, w, dh, dout):
    n, d = x.shape
    tr = min(256, n)

    def body(x_ref, w_ref, dh_ref, do_ref, dx_ref, gw_ref):
        xv, dhv = x_ref[...], dh_ref[...]
        r = lax.rsqrt(jnp.mean(xv * xv, axis=-1, keepdims=True) + EPS)
        xn = xv * r
        dxn = dhv * w_ref[...]
        dx_ref[...] = r * (dxn - xn * jnp.mean(dxn * xn, axis=-1, keepdims=True)) + do_ref[...]

        @pl.when(pl.program_id(0) == 0)
        def _():
            gw_ref[...] = jnp.zeros_like(gw_ref)

        gw_ref[...] += jnp.sum(dhv * xn, axis=0, keepdims=True)

    row = pl.BlockSpec((tr, d), lambda i: (i, 0))
    one = pl.BlockSpec((1, d), lambda i: (0, 0))
    return pl.pallas_call(
        body, name="rms_in_bwd", grid=(n // tr,),
        in_specs=[row, one, row, row], out_specs=[row, one],
        out_shape=[jax.ShapeDtypeStruct((n, d), F32), jax.ShapeDtypeStruct((1, d), F32)],
        compiler_params=_params(("arbitrary",)),
    )(x, w, dh, dout)


def _ij():
    i = lax.broadcasted_iota(jnp.int32, (CH, CH), 0)
    j = lax.broadcasted_iota(jnp.int32, (CH, CH), 1)
    return i, j


def _unit_lower_inverse(mats):
    i, j = _ij()
    eye = jnp.where(i == j, 1.0, 0.0)
    same16 = (i // 16) == (j // 16)
    same32 = (i // 32) == (j // 32)
    mm = lambda xs, ys: [_dot(x, y, NN, P_INV) for x, y in zip(xs, ys)]
    n1 = [jnp.where(same16, -a, 0.0) for a in mats]
    n2 = mm(n1, n1)
    n4 = mm(n2, n2)
    n8 = mm(n4, n4)
    t = [eye + x1 + x2 + x3 for x1, x2, x3 in zip(n1, n2, mm(n1, n2))]
    t = [x + y for x, y in zip(t, mm(t, n4))]
    t = [x + y for x, y in zip(t, mm(t, n8))]
    a1 = [jnp.where(same32 & jnp.logical_not(same16), a, 0.0) for a in mats]
    t = [x - y for x, y in zip(t, mm(t, mm(a1, t)))]
    a2 = [jnp.where(same32, 0.0, a) for a in mats]
    t = [x - y for x, y in zip(t, mm(t, mm(a2, t)))]
    return t


def _head_vectors(bg, bgt, h):
    bcol = bg[:, h:h + 1]
    gcol = bg[:, HEADS + h:HEADS + h + 1]
    grow = bgt[HEADS + h:HEADS + h + 1, :]
    return bcol, gcol, grow


def _decay(gcol, grow):
    i, j = _ij()
    return jnp.where(i >= j, jnp.exp(jnp.where(i >= j, gcol - grow, 0.0)), 0.0)


def _gdn_intra(q, k, v, bg, bgt):
    n = q.shape[0]
    nch = n // CH

    def body(q_ref, k_ref, v_ref, bg_ref, bgt_ref, u_ref, w_ref, p_ref, t_ref):
        bg, bgt = bg_ref[...], bgt_ref[0]
        i, j = _ij()
        sls = [slice(h * DH, (h + 1) * DH) for h in range(HEADS)]
        ks = [k_ref[:, sl] for sl in sls]
        vecs = [_head_vectors(bg, bgt, h) for h in range(HEADS)]
        decs = [_decay(gcol, grow) for _, gcol, grow in vecs]
        kks = [_dot(kh, kh, NT, P_GRAM) for kh in ks]
        qks = [_dot(q_ref[:, sl], kh, NT, P_GRAM) for sl, kh in zip(sls, ks)]
        ts = _unit_lower_inverse([jnp.where(i > j, bcol * kk * dec, 0.0)
                                  for (bcol, _, _), kk, dec in zip(vecs, kks, decs)])
        us = [_dot(t, v_ref[:, sl] * bcol, NN, P_SOL) for t, sl, (bcol, _, _) in zip(ts, sls, vecs)]
        ws = [_dot(t, kh * (bcol * jnp.exp(gcol)), NN, P_SOL) for t, kh, (bcol, gcol, _) in zip(ts, ks, vecs)]
        for h, sl in enumerate(sls):
            p_ref[0, h] = qks[h] * decs[h]
            t_ref[0, h] = ts[h]
            u_ref[:, sl] = us[h]
            w_ref[:, sl] = ws[h]

    row = pl.BlockSpec((CH, GW), lambda c: (c, 0))
    sq = pl.BlockSpec((1, HEADS, CH, CH), lambda c: (c, 0, 0, 0))
    big = jax.ShapeDtypeStruct((n, GW), F32)
    sqs = jax.ShapeDtypeStruct((nch, HEADS, CH, CH), F32)
    return pl.pallas_call(
        body, name="gdn_intra", grid=(nch,),
        in_specs=[row, row, row, pl.BlockSpec((CH, DH), lambda c: (c, 0)),
                  pl.BlockSpec((1, DH, CH), lambda c: (c, 0, 0))],
        out_specs=[row, row, sq, sq], out_shape=[big, big, sqs, sqs],
        compiler_params=_params(("parallel",)),
    )(q, k, v, bg, bgt)


def _gdn_scan(q, k, bg, u, w, p):
    n = q.shape[0]
    nch = n // CH

    def body(q_ref, k_ref, bg_ref, u_ref, w_ref, p_ref, o_ref, vn_ref, s_out, s_scr):
        @pl.when(pl.program_id(0) == 0)
        def _():
            s_scr[...] = jnp.zeros_like(s_scr)

        bg = bg_ref[...]
        hs = range(HEADS)
        sls = [slice(h * DH, (h + 1) * DH) for h in hs]
        gcols = [bg[:, HEADS + h:HEADS + h + 1] for h in hs]
        glasts = [g[CH - 1:CH, :] for g in gcols]
        ss = [s_scr[h] for h in hs]
        wss = [_dot(w_ref[:, sl], s, NN, P_SCAN) for sl, s in zip(sls, ss)]
        oqs = [_dot(q_ref[:, sl] * jnp.exp(g), s, NN, P_SCAN) for sl, s, g in zip(sls, ss, gcols)]
        vns = [u_ref[:, sl] - x for sl, x in zip(sls, wss)]
        ops = [_dot(p_ref[0, h], vn, NN, P_SCAN) for h, vn in zip(hs, vns)]
        sns = [_dot(k_ref[:, sl] * jnp.exp(gl - g), vn, TN, P_SCAN)
               for sl, gl, g, vn in zip(sls, glasts, gcols, vns)]
        for h, sl in enumerate(sls):
            s_out[0, :, sl] = ss[h]
            vn_ref[:, sl] = vns[h]
            o_ref[:, sl] = oqs[h] + ops[h]
            s_scr[h] = ss[h] * jnp.exp(glasts[h]) + sns[h]

    row = pl.BlockSpec((CH, GW), lambda c: (c, 0))
    big = jax.ShapeDtypeStruct((n, GW), F32)
    return pl.pallas_call(
        body, name="gdn_scan", grid=(nch,),
        in_specs=[row, row, pl.BlockSpec((CH, DH), lambda c: (c, 0)), row, row,
                  pl.BlockSpec((1, HEADS, CH, CH), lambda c: (c, 0, 0, 0))],
        out_specs=[row, row, pl.BlockSpec((1, DH, GW), lambda c: (c, 0, 0))],
        out_shape=[big, big, jax.ShapeDtypeStruct((nch, DH, GW), F32)],
        scratch_shapes=[pltpu.VMEM((HEADS, DH, DH), F32)],
        compiler_params=_params(("arbitrary",)),
    )(q, k, bg, u, w, p)


def _gdn_scan_bwd(q, k, bg, w, p, vn, s_in, do):
    n = q.shape[0]
    nch = n // CH
    rev = lambda c: nch - 1 - c

    def body(q_ref, k_ref, bg_ref, w_ref, p_ref, vn_ref, s_ref, do_ref,
             dqg_ref, dp_ref, du_ref, dw_ref, dks_ref, dgam_ref, ds_scr):
        @pl.when(pl.program_id(0) == 0)
        def _():
            ds_scr[...] = jnp.zeros_like(ds_scr)

        bg = bg_ref[...]
        lane = _lane((1, DH))
        hs = range(HEADS)
        sls = [slice(h * DH, (h + 1) * DH) for h in hs]
        gcols = [bg[:, HEADS + h:HEADS + h + 1] for h in hs]
        glasts = [g[CH - 1:CH, :] for g in gcols]
        ss = [s_ref[0, :, sl] for sl in sls]
        dss = [ds_scr[h] for h in hs]
        dos = [do_ref[:, sl] for sl in sls]
        vnl = [vn_ref[:, sl] for sl in sls]
        dqgs = [_dot(d, s, NT, P_SCANB) for d, s in zip(dos, ss)]
        dps = [_dot(d, vn, NT, P_SCANB) for d, vn in zip(dos, vnl)]
        dvn1 = [_dot(p_ref[0, h], d, TN, P_SCANB) for h, d in zip(hs, dos)]
        dvn2 = [_dot(k_ref[:, sl] * jnp.exp(gl - g), ds, NN, P_SCANB)
                for sl, gl, g, ds in zip(sls, glasts, gcols, dss)]
        dkss = [_dot(vn, ds, NT, P_SCANB) for vn, ds in zip(vnl, dss)]
        dsq = [_dot(q_ref[:, sl] * jnp.exp(g), d, TN, P_SCANB) for sl, g, d in zip(sls, gcols, dos)]
        dvns = [a + b for a, b in zip(dvn1, dvn2)]
        dws = [_dot(dvn, s, NT, P_SCANB) for dvn, s in zip(dvns, ss)]
        dsw = [_dot(w_ref[:, sl], dvn, TN, P_SCANB) for sl, dvn in zip(sls, dvns)]
        dgam = jnp.zeros((1, DH), F32)
        for h, sl in enumerate(sls):
            dqg_ref[:, sl] = dqgs[h]
            dp_ref[0, h] = dps[h]
            du_ref[:, sl] = dvns[h]
            dw_ref[:, sl] = -dws[h]
            dks_ref[:, sl] = dkss[h]
            tot = jnp.sum(jnp.sum(dss[h] * ss[h], axis=-1, keepdims=True), axis=0, keepdims=True)
            dgam = dgam + jnp.where(lane == h, tot, 0.0)
            ds_scr[h] = dss[h] * jnp.exp(glasts[h]) + dsq[h] - dsw[h]
        dgam_ref[0] = jnp.broadcast_to(dgam, (8, DH))

    row = pl.BlockSpec((CH, GW), lambda c: (rev(c), 0))
    sq = pl.BlockSpec((1, HEADS, CH, CH), lambda c: (rev(c), 0, 0, 0))
    big = jax.ShapeDtypeStruct((n, GW), F32)
    return pl.pallas_call(
        body, name="gdn_scan_bwd", grid=(nch,),
        in_specs=[row, row, pl.BlockSpec((CH, DH), lambda c: (rev(c), 0)), row, sq, row,
                  pl.BlockSpec((1, DH, GW), lambda c: (rev(c), 0, 0)), row],
        out_specs=[row, sq, row, row, row, pl.BlockSpec((1, 8, DH), lambda c: (rev(c), 0, 0))],
        out_shape=[big, jax.ShapeDtypeStruct((nch, HEADS, CH, CH), F32), big, big, big,
                   jax.ShapeDtypeStruct((nch, 8, DH), F32)],
        scratch_shapes=[pltpu.VMEM((HEADS, DH, DH), F32)],
        compiler_params=_params(("arbitrary",)),
    )(q, k, bg, w, p, vn, s_in, do)


def _gdn_intra_bwd(q, k, v, bg, bgt, t, u, w, p, dqg, dp, du, dw, dks, dgam):
    n = q.shape[0]
    nch = n // CH

    def body(q_ref, k_ref, v_ref, bg_ref, bgt_ref, t_ref, u_ref, w_ref, p_ref,
             dqg_ref, dp_ref, du_ref, dw_ref, dks_ref, dgam_ref, dq_ref, dk_ref, dv_ref, dbg_ref):
        bg, bgt = bg_ref[...], bgt_ref[0]
        dgam_all = dgam_ref[0]
        i, j = _ij()
        rows1 = lax.broadcasted_iota(jnp.int32, (CH, 1), 0)
        lane = _lane((CH, DH))
        dbg = jnp.zeros((CH, DH), F32)
        rsum = lambda x: jnp.sum(x, axis=-1, keepdims=True)
        hs = range(HEADS)
        sls = [slice(h * DH, (h + 1) * DH) for h in hs]
        qs = [q_ref[:, sl] for sl in sls]
        ks = [k_ref[:, sl] for sl in sls]
        vecs = [_head_vectors(bg, bgt, h) for h in hs]
        decs = [_decay(gcol, grow) for _, gcol, grow in vecs]
        ths = [t_ref[0, h] for h in hs]
        drus = [_dot(th, du_ref[:, sl], TN, P_BWD) for th, sl in zip(ths, sls)]
        drws = [_dot(th, dw_ref[:, sl], TN, P_BWD) for th, sl in zip(ths, sls)]
        kks = [_dot(kh, kh, NT, P_GRAM) for kh in ks]
        da1 = [_dot(dru, u_ref[:, sl], NT, P_BWD) for dru, sl in zip(drus, sls)]
        da2 = [_dot(drw, w_ref[:, sl], NT, P_BWD) for drw, sl in zip(drws, sls)]
        das = [jnp.where(i > j, -(x + y), 0.0) for x, y in zip(da1, da2)]
        dkks = [da * bcol * dec for da, (bcol, _, _), dec in zip(das, vecs, decs)]
        dqks = [dp_ref[0, h] * dec for h, dec in zip(hs, decs)]
        dq_ps = [_dot(dqk, kh, NN, P_BWD) for dqk, kh in zip(dqks, ks)]
        dk_ps = [_dot(dqk, qh, TN, P_BWD) for dqk, qh in zip(dqks, qs)]
        dk_as = [_dot(dkk, kh, NN, P_BWD) for dkk, kh in zip(dkks, ks)]
        dk_bs = [_dot(dkk, kh, TN, P_BWD) for dkk, kh in zip(dkks, ks)]
        for h, sl in enumerate(sls):
            qh, kh, vh = qs[h], ks[h], v_ref[:, sl]
            bcol, gcol, _ = vecs[h]
            dec, dru, drw, da, kk = decs[h], drus[h], drws[h], das[h], kks[h]
            gam = jnp.exp(gcol)
            glast = gcol[CH - 1:CH, :]
            e = jnp.exp(glast - gcol)
            kg = kh * gam
            dv_ref[:, sl] = bcol * dru
            dbeta = rsum(dru * vh) + rsum(drw * kg) + rsum(da * kk * dec)
            dgc = rsum(drw * kg) * bcol
            dqg = dqg_ref[:, sl]
            dksh = dks_ref[:, sl]
            dq_ref[:, sl] = gam * dqg + dq_ps[h]
            dk_ref[:, sl] = (bcol * gam) * drw + dk_ps[h] + dk_as[h] + dk_bs[h] + dksh * e
            tk = rsum(dksh * kh) * e
            mdec = da * (bcol * kk * dec) + dp_ref[0, h] * p_ref[0, h]
            col = rsum(jnp.where(i == j, jnp.sum(mdec, axis=0, keepdims=True), 0.0))
            dgc = dgc + rsum(mdec) - col
            dgc = dgc + rsum(dqg * qh) * gam - tk
            dglast = jnp.sum(tk, axis=0, keepdims=True) + dgam_all[0:1, h:h + 1] * jnp.exp(glast)
            dgc = dgc + jnp.where(rows1 == CH - 1, dglast, 0.0)
            dbg = dbg + jnp.where(lane == h, dbeta, 0.0) + jnp.where(lane == HEADS + h, dgc, 0.0)
        dbg_ref[...] = dbg

    row = pl.BlockSpec((CH, GW), lambda c: (c, 0))
    sq = pl.BlockSpec((1, HEADS, CH, CH), lambda c: (c, 0, 0, 0))
    small = pl.BlockSpec((CH, DH), lambda c: (c, 0))
    big = jax.ShapeDtypeStruct((n, GW), F32)
    return pl.pallas_call(
        body, name="gdn_intra_bwd", grid=(nch,),
        in_specs=[row, row, row, small, pl.BlockSpec((1, DH, CH), lambda c: (c, 0, 0)), sq, row, row, sq,
                  row, sq, row, row, row, pl.BlockSpec((1, 8, DH), lambda c: (c, 0, 0))],
        out_specs=[row, row, row, small],
        out_shape=[big, big, big, jax.ShapeDtypeStruct((n, DH), F32)],
        compiler_params=_params(("parallel",)),
    )(q, k, v, bg, bgt, t, u, w, p, dqg, dp, du, dw, dks, dgam)


def _local_step(x, tgt, h, w_g, cqw, late, norm_in_w, ad, gdn_norm_w, conv_b, final_norm_w,
                on_grad_c=None, on_grad_g=None):
    proj_g = _matmul(h, w_g, NT, F32, 512, 1408, 1024, "mm_proj_g", n=GW_COLS)
    q, k, v = _prep_qkv(proj_g, cqw)
    bg, bgt = _prep_bg(proj_g, ad)
    u, w, p, t = _gdn_intra(q, k, v, bg, bgt)
    o, vn, s_in = _gdn_scan(q, k, bg, u, w, p)
    w_c, w_out, conv_w = late(o)
    proj_c = _matmul(h, w_c, NT, F32, 512, 1024, 1024, "mm_proj_c", n=CW_COLS)
    mix = _conv_branch(proj_c, conv_w, conv_b, _gdn_out(o, proj_g, gdn_norm_w))
    out = _matmul(mix, w_out, NN, F32, 512, 512, 2048, "mm_out", add=x)
    dout, dout_b, g_fn, loss = _final_loss(out, tgt, final_norm_w)

    dmix = _matmul(dout_b, w_out, NT, F32, 512, 1024, 1024, "mm_dmix")
    g_wout = _matmul(mix, dout_b, TN, BF16, 512, 512, 2048, "mm_gwout")
    do, dproj_g, g_gn = _gdn_out_bwd(o, proj_g, gdn_norm_w, dmix)
    dproj_c, g_cw, g_cb = _conv_branch_bwd(proj_c, conv_w, conv_b, dmix)
    g_c = _matmul(dproj_c, h, TN, BF16, 1024, 512, 2048, "mm_gwin_c")
    if on_grad_c is not None:
        do = on_grad_c(g_c, g_wout, do)
    dqg, dp, du, dw, dks, dgam = _gdn_scan_bwd(q, k, bg, w, p, vn, s_in, do)
    dq, dk, dv, dbg = _gdn_intra_bwd(q, k, v, bg, bgt, t, u, w, p, dqg, dp, du, dw, dks, dgam)
    dproj_g, gq, gk, gv = _prep_qkv_bwd(proj_g, cqw, dq, dk, dv, dproj_g)
    dproj_g, g_al, g_dt = _prep_bg_bwd(proj_g, ad, dbg, dproj_g)
    g_g = _matmul(dproj_g, h, TN, BF16, 1408, 512, 2048, "mm_gwin_g")
    if on_grad_g is not None:
        dproj_g = on_grad_g(g_g, dproj_g)
    dh = _matmul(dproj_g, w_g, NN, F32, 512, 1024, 1408, "mm_dh_g")
    dh = _matmul(dproj_c, w_c, NN, F32, 512, 1024, 1024, "mm_dh_c", add=dh)
    gx, g_nin = _rms_in_bwd(x, norm_in_w, dh, dout)
    small = dict(nin=g_nin, cb=g_cb, fn=g_fn, al=g_al, dt=g_dt, gn=g_gn, cq=(gq, gk, gv), cw=g_cw, loss=loss)
    return gx, small, (g_g, g_c, g_wout)


def _place():
    x, y, c = lax.axis_index("x"), lax.axis_index("y"), lax.axis_index("c")
    chips = [(1 - x, y), (x, 1 - y), (1 - x, 1 - y)]
    return x, y, c, chips


def _blk(ref, b):
    if isinstance(b, int):
        return ref.at[b * DH:(b + 1) * DH, :]
    return ref.at[pl.ds(pl.multiple_of(b * DH, DH), DH), :]


HBM = pl.BlockSpec(memory_space=pltpu.HBM)
SEM = pl.BlockSpec(memory_space=pltpu.SEMAPHORE)
EFFECT = pltpu.SideEffectType.DATAFLOW_SIDE_EFFECTING


def _split_start(name, issue, bufs, n_sems):
    nbuf = len(bufs)

    def body(*refs):
        issue(refs[:nbuf], refs[nbuf], refs[nbuf + 1])
        refs[-1][...] = jnp.zeros_like(refs[-1])

    out = pl.pallas_call(
        body, name=name,
        out_shape=(pltpu.SemaphoreType.DMA((n_sems,)), pltpu.SemaphoreType.DMA((n_sems,)),
                   *[pltpu.HBM(b.shape, b.dtype) for b in bufs], jax.ShapeDtypeStruct((8, DH), F32)),
        in_specs=[HBM] * nbuf,
        out_specs=(SEM, SEM, *[HBM] * nbuf, pl.BlockSpec(memory_space=pltpu.VMEM)),
        input_output_aliases={a: 2 + a for a in range(nbuf)},
        compiler_params=pltpu.CompilerParams(has_side_effects=EFFECT),
    )(*[pltpu.with_memory_space_constraint(b, pltpu.HBM) for b in bufs])
    return out[0], out[1], list(out[2:2 + nbuf]), out[-1]


def _split_wait(name, await_, send_sems, recv_sems, bufs, after):
    nbuf = len(bufs)

    def body(*refs):
        await_(refs[:nbuf], refs[nbuf], refs[nbuf + 1])

    out = pl.pallas_call(
        body, name=name,
        out_shape=tuple(pltpu.HBM(b.shape, b.dtype) for b in bufs),
        in_specs=[HBM] * nbuf + [SEM, SEM, ANY], out_specs=tuple([HBM] * nbuf),
        input_output_aliases={a: a for a in range(nbuf)},
        compiler_params=pltpu.CompilerParams(has_side_effects=EFFECT),
    )(*bufs, send_sems, recv_sems, after)
    return list(out)


def _phase_blocks(chip, phase, edges, parity=None):
    return [(b, blk) for b, (grp, blk) in enumerate(_shard_blocks(chip, edges))
            if grp == phase and (parity is None or b % 2 == parity)]


def _cols(ref, nblk):
    return ref.at[0:nblk * DH, :]


def _block_table(chip, edges, spare_g, spare_c):
    rows = []
    for s in range(4):
        sb = _shard_blocks(s, edges)
        rows.append([[blk if grp == "g" else spare_g for grp, blk in sb],
                     [blk if grp == "c" else spare_c for grp, blk in sb],
                     [int(grp == "g") for grp, _ in sb], [s] * ALIGNED_BLOCKS])
    return jnp.asarray(rows, jnp.int32)[chip]


def _place_own(a_shard, wo, cq, cw, bufs):
    d = a_shard.shape[1]
    chip = 2 * lax.axis_index("x") + lax.axis_index("y")

    def body(t_ref, a_ref, wo_ref, cq_ref, cw_ref, *refs):
        wg_ref, wc_ref, wog_ref, cqg_ref, cwg_ref = refs[5:]
        wg_ref[...] = a_ref[...]
        wc_ref[...] = a_ref[...]

        @pl.when(pl.program_id(0) == 0)
        def _():
            wog_ref[0] = wo_ref[...]
            cqg_ref[0] = cq_ref[...]
            cwg_ref[0] = cw_ref[...]

    whole = lambda s: pl.BlockSpec(s.shape, lambda b, t: (0,) * s.ndim)
    slot = lambda s: pl.BlockSpec((1,) + s.shape, lambda b, t: (t[3, 0],) + (0,) * s.ndim)
    return pl.pallas_call(
        body, name="place_own",
        grid_spec=pltpu.PrefetchScalarGridSpec(
            num_scalar_prefetch=1, grid=(ALIGNED_BLOCKS,),
            in_specs=[pl.BlockSpec((DH, d), lambda b, t: (b, 0)), whole(wo), whole(cq), whole(cw)] + [ANY] * 5,
            out_specs=[pl.BlockSpec((DH, d), lambda b, t: (t[0, b], 0)),
                       pl.BlockSpec((DH, d), lambda b, t: (t[1, b], 0)), slot(wo), slot(cq), slot(cw)]),
        out_shape=[jax.ShapeDtypeStruct(b.shape, b.dtype) for b in bufs],
        input_output_aliases={5 + a: a for a in range(5)},
        compiler_params=_params(("arbitrary",)),
    )(_block_table(chip, True, G_SPARE, C_SPARE), a_shard, wo, cq, cw, *bufs)


def _tie(x, token, name):
    def body(x_ref, t_ref, o_ref):
        del x_ref, t_ref, o_ref

    return pl.pallas_call(
        body, name=name, in_specs=[ANY, ANY], out_specs=ANY,
        out_shape=jax.ShapeDtypeStruct(x.shape, x.dtype), input_output_aliases={0: 0},
    )(x, token)


def _gather_start(phase, a_shard, w_grp, singles):
    ns = len(singles)

    def issue(refs, send_sems, recv_sems):
        a_ref, w_ref = refs[0], refs[1]
        x, y, c, chips = _place()
        mine = 2 * x + y
        for jj, (px, py) in enumerate(chips):
            to = dict(device_id=(px, py, c), device_id_type=MESH)
            for a in range(ns):
                pltpu.make_async_remote_copy(
                    src_ref=refs[2 + 2 * a], dst_ref=refs[3 + 2 * a].at[mine],
                    send_sem=send_sems.at[(1 + ns) * jj + 1 + a], recv_sem=recv_sems.at[(1 + ns) * jj + 1 + a],
                    **to).start()
        for s in range(4):
            for par in range(2):
                blocks = _phase_blocks(s, phase, True, par)
                if blocks:
                    @pl.when((mine == s) & (c == par))
                    def _():
                        for jj, (px, py) in enumerate(chips):
                            for b, blk in blocks:
                                pltpu.make_async_remote_copy(
                                    src_ref=_blk(a_ref, b), dst_ref=_blk(w_ref, blk),
                                    send_sem=send_sems.at[(1 + ns) * jj], recv_sem=recv_sems.at[(1 + ns) * jj],
                                    device_id=(px, py, c), device_id_type=MESH).start()

    bufs = [a_shard, w_grp] + [t for pair in singles for t in pair]
    return _split_start("gather_start_" + phase, issue, bufs, 3 * (1 + ns))


def _gather_wait(phase, send_sems, recv_sems, bufs, after):
    ns = (len(bufs) - 2) // 2

    def await_(refs, send_sems, recv_sems):
        a_ref, w_ref = refs[0], refs[1]
        x, y, c, chips = _place()
        mine = 2 * x + y
        for jj, (px, py) in enumerate(chips):
            to = dict(device_id=(px, py, c), device_id_type=MESH)
            peer = 2 * px + py
            for a in range(ns):
                cp = pltpu.make_async_remote_copy(
                    src_ref=refs[2 + 2 * a], dst_ref=refs[3 + 2 * a].at[mine],
                    send_sem=send_sems.at[(1 + ns) * jj + 1 + a], recv_sem=recv_sems.at[(1 + ns) * jj + 1 + a], **to)
                cp.wait_recv()
                cp.wait_send()
            for s in range(4):
                for par in range(2):
                    nblk = len(_phase_blocks(s, phase, True, par))
                    if nblk:
                        both = pltpu.make_async_remote_copy(
                            src_ref=_cols(a_ref, nblk), dst_ref=_cols(w_ref, nblk),
                            send_sem=send_sems.at[(1 + ns) * jj], recv_sem=recv_sems.at[(1 + ns) * jj], **to)

                        @pl.when((peer == s) & (c == par))
                        def _():
                            both.wait_recv()

                        @pl.when((mine == s) & (c == par))
                        def _():
                            both.wait_send()

    return _split_wait("gather_wait_" + phase, await_, send_sems, recv_sems, bufs, after)


def _sibling_forward(phase, w_grp):
    def body(w_in_ref, w_ref, send_sems, recv_sems):
        del w_in_ref
        x, y, c, chips = _place()
        to = dict(device_id=(x, y, 1 - c), device_id_type=MESH)
        for jj, (px, py) in enumerate(chips):
            peer = 2 * px + py
            for s in range(4):
                for par in range(2):
                    mine_blocks = _phase_blocks(s, phase, True, par)
                    theirs = len(_phase_blocks(s, phase, True, 1 - par))
                    if not (mine_blocks or theirs):
                        continue

                    @pl.when((peer == s) & (c == par))
                    def _():
                        for _, blk in mine_blocks:
                            pltpu.make_async_remote_copy(
                                src_ref=_blk(w_ref, blk), dst_ref=_blk(w_ref, blk),
                                send_sem=send_sems.at[jj], recv_sem=recv_sems.at[jj], **to).start()
                        if theirs:
                            pltpu.make_async_remote_copy(
                                src_ref=_cols(w_ref, theirs), dst_ref=_cols(w_ref, theirs),
                                send_sem=send_sems.at[jj], recv_sem=recv_sems.at[jj], **to).wait_recv()
                        if mine_blocks:
                            pltpu.make_async_remote_copy(
                                src_ref=_cols(w_ref, len(mine_blocks)), dst_ref=_cols(w_ref, len(mine_blocks)),
                                send_sem=send_sems.at[jj], recv_sem=recv_sems.at[jj], **to).wait_send()

    return pl.pallas_call(
        body, name="sibling_forward_" + phase, in_specs=[ANY], out_specs=ANY,
        out_shape=jax.ShapeDtypeStruct(w_grp.shape, w_grp.dtype), input_output_aliases={0: 0},
        scratch_shapes=[pltpu.SemaphoreType.DMA((3,)), pltpu.SemaphoreType.DMA((3,))],
    )(w_grp)


def _merge_edges(w, edge0, mixed, name):
    d = w.shape[1]

    def body(e_ref, o_ref):
        o_ref[...] = e_ref[0:DH, :] + e_ref[DH:2 * DH, :]

    def to_block(i):
        r = mixed[-1]
        for kk in range(len(mixed) - 2, -1, -1):
            r = jnp.where(i == kk, mixed[kk], r)
        return r

    return pl.pallas_call(
        body, name=name, grid=(len(mixed),),
        in_specs=[pl.BlockSpec((2 * DH, d), lambda i: (edge0 // 2 + i, 0))],
        out_specs=pl.BlockSpec((DH, d), lambda i: (to_block(i), 0)),
        out_shape=jax.ShapeDtypeStruct(w.shape, w.dtype),
        input_output_aliases={0: 0},
        compiler_params=_params(("arbitrary",)),
    )(w)


def _scatter_start(phase, g_grp, land, singles, halved=False):
    ns = len(singles)

    def issue(refs, send_sems, recv_sems):
        g_ref, land_ref = refs[0], refs[1]
        x, y, c, chips = _place()
        for jj, (px, py) in enumerate(chips):
            to = dict(device_id=(px, py, c), device_id_type=MESH)
            peer = 2 * px + py
            for a in range(ns):
                pltpu.make_async_remote_copy(
                    src_ref=refs[2 + 2 * a].at[peer], dst_ref=refs[3 + 2 * a].at[jj],
                    send_sem=send_sems.at[(1 + ns) * jj + 1 + a], recv_sem=recv_sems.at[(1 + ns) * jj + 1 + a],
                    **to).start()
            for s in range(4):
                for par in ((0, 1) if halved else (None,)):
                    blocks = _phase_blocks(s, phase, False, par)
                    if blocks:
                        @pl.when((peer == s) if par is None else ((peer == s) & (c == par)))
                        def _():
                            for b, blk in blocks:
                                pltpu.make_async_remote_copy(
                                    src_ref=_blk(g_ref, blk), dst_ref=_blk(land_ref.at[jj], b),
                                    send_sem=send_sems.at[(1 + ns) * jj], recv_sem=recv_sems.at[(1 + ns) * jj],
                                    **to).start()

    bufs = [g_grp, land] + [t for pair in singles for t in pair]
    return _split_start("scatter_start_" + phase, issue, bufs, 3 * (1 + ns))


def _scatter_wait(phase, send_sems, recv_sems, bufs, after, halved=False):
    ns = (len(bufs) - 2) // 2

    def await_(refs, send_sems, recv_sems):
        g_ref, land_ref = refs[0], refs[1]
        x, y, c, chips = _place()
        mine = 2 * x + y
        for jj, (px, py) in enumerate(chips):
            to = dict(device_id=(px, py, c), device_id_type=MESH)
            peer = 2 * px + py
            for a in range(ns):
                cp = pltpu.make_async_remote_copy(
                    src_ref=refs[2 + 2 * a].at[peer], dst_ref=refs[3 + 2 * a].at[jj],
                    send_sem=send_sems.at[(1 + ns) * jj + 1 + a], recv_sem=recv_sems.at[(1 + ns) * jj + 1 + a], **to)
                cp.wait_recv()
                cp.wait_send()
            for s in range(4):
                for par in ((0, 1) if halved else (None,)):
                    nblk = len(_phase_blocks(s, phase, False, par))
                    if nblk:
                        both = pltpu.make_async_remote_copy(
                            src_ref=_cols(g_ref, nblk), dst_ref=_cols(land_ref.at[jj], nblk),
                            send_sem=send_sems.at[(1 + ns) * jj], recv_sem=recv_sems.at[(1 + ns) * jj], **to)

                        @pl.when((mine == s) if par is None else ((mine == s) & (c == par)))
                        def _():
                            both.wait_recv()

                        @pl.when((peer == s) if par is None else ((peer == s) & (c == par)))
                        def _():
                            both.wait_send()

    return _split_wait("scatter_wait_" + phase, await_, send_sems, recv_sems, bufs, after)


def _needed_blocks(phase, parity):
    return sorted({blk for s in range(4) for _, blk in _phase_blocks(s, phase, False, parity)})


def _pair_reduce(phase, g_grp):
    n, d = g_grp.shape

    def swap(g_ref, sib_ref, send_sem, recv_sem):
        x, y, c, _ = _place()
        to = dict(device_id=(x, y, 1 - c), device_id_type=MESH)
        for par in range(2):
            give, get = _needed_blocks(phase, 1 - par), _needed_blocks(phase, par)

            @pl.when(c == par)
            def _():
                for blk in give:
                    pltpu.make_async_remote_copy(src_ref=_blk(g_ref, blk), dst_ref=_blk(sib_ref, blk),
                                                 send_sem=send_sem, recv_sem=recv_sem, **to).start()
                pltpu.make_async_remote_copy(src_ref=_cols(g_ref, len(get)), dst_ref=_cols(sib_ref, len(get)),
                                             send_sem=send_sem, recv_sem=recv_sem, **to).wait_recv()
                pltpu.make_async_remote_copy(src_ref=_cols(g_ref, len(give)), dst_ref=_cols(sib_ref, len(give)),
                                             send_sem=send_sem, recv_sem=recv_sem, **to).wait_send()

    sib = pl.pallas_call(
        swap, name="pair_swap_" + phase, in_specs=[ANY], out_specs=ANY,
        out_shape=jax.ShapeDtypeStruct((n, d), g_grp.dtype),
        scratch_shapes=[pltpu.SemaphoreType.DMA, pltpu.SemaphoreType.DMA],
    )(g_grp)

    lists = [_needed_blocks(phase, par) for par in range(2)]
    longest = max(len(t) for t in lists)
    table = jnp.asarray([t + [t[-1]] * (longest - len(t)) for t in lists], jnp.int32)[lax.axis_index("c")]

    def add(t_ref, a_ref, b_ref, o_ref):
        o_ref[...] = (a_ref[...].astype(F32) + b_ref[...].astype(F32)).astype(o_ref.dtype)

    blk = pl.BlockSpec((DH, d), lambda i, t: (t[i], 0))
    return pl.pallas_call(
        add, name="pair_add_" + phase,
        grid_spec=pltpu.PrefetchScalarGridSpec(num_scalar_prefetch=1, grid=(longest,),
                                               in_specs=[blk, blk], out_specs=blk),
        out_shape=jax.ShapeDtypeStruct((n, d), g_grp.dtype),
        compiler_params=_params(("arbitrary",)),
    )(table, g_grp, sib)


def _sum_shard(g_g, g_c, land):
    d = g_g.shape[1]
    chip = 2 * lax.axis_index("x") + lax.axis_index("y")

    def body(t_ref, gg_ref, gc_ref, land_ref, o_ref):
        b = pl.program_id(0)
        in_g = t_ref[2, b] == 1
        own = jnp.where(in_g, gg_ref[...].astype(F32), gc_ref[...].astype(F32))
        for jj in range(3):
            own = own + land_ref[jj].astype(F32)
        o_ref[...] = jnp.where(in_g & (b % 2 != lax.axis_index("c")), 0.0, own)

    return pl.pallas_call(
        body, name="sum_w_in",
        grid_spec=pltpu.PrefetchScalarGridSpec(
            num_scalar_prefetch=1, grid=(ALIGNED_BLOCKS,),
            in_specs=[pl.BlockSpec((DH, d), lambda b, t: (t[0, b], 0)), pl.BlockSpec((DH, d), lambda b, t: (t[1, b], 0)),
                      pl.BlockSpec((3, DH, d), lambda b, t: (0, b, 0))],
            out_specs=pl.BlockSpec((DH, d), lambda b, t: (b, 0))),
        out_shape=jax.ShapeDtypeStruct((ALIGNED_W, d), F32),
        compiler_params=_params(("arbitrary",)),
    )(_block_table(chip, False, 0, 0), g_g, g_c, land)


def _sum_rows(stack, land, rows):
    _, r, d = stack.shape
    rows = min(rows, r)
    chip = 2 * lax.axis_index("x") + lax.axis_index("y")

    def body(t_ref, own_ref, land_ref, o_ref):
        acc = own_ref[0].astype(F32)
        for jj in range(3):
            acc = acc + land_ref[jj].astype(F32)
        o_ref[...] = acc

    return pl.pallas_call(
        body, name="sum_w_out",
        grid_spec=pltpu.PrefetchScalarGridSpec(
            num_scalar_prefetch=1, grid=(r // rows,),
            in_specs=[pl.BlockSpec((1, rows, d), lambda i, t: (t[0], i, 0)),
                      pl.BlockSpec((3, rows, d), lambda i, t: (0, i, 0))],
            out_specs=pl.BlockSpec((rows, d), lambda i, t: (i, 0))),
        out_shape=jax.ShapeDtypeStruct((r, d), F32),
        compiler_params=_params(("arbitrary",)),
    )(jnp.reshape(chip, (1,)).astype(jnp.int32), stack, land)


def _final_exchange(parts, pack):
    npart = len(parts)

    def body(*refs):
        ins, pack_ref = refs[:npart], refs[npart]
        outs, packs = refs[npart + 1:2 * npart + 1], refs[2 * npart + 1]
        send_sems, recv_sems, psend, precv, loc_sem = refs[2 * npart + 2:]
        x, y, c, _ = _place()
        me = 4 * x + 2 * y + c
        local = pltpu.make_async_copy(pack_ref, packs.at[me], loc_sem)
        local.start()
        cps = [pltpu.make_async_remote_copy(
            src_ref=ins[a], dst_ref=outs[a], send_sem=send_sems.at[a], recv_sem=recv_sems.at[a],
            device_id=(x, y, 1 - c), device_id_type=MESH) for a in range(npart)]
        for r in range(1, 8):
            dx, dy, dc = (r >> 2) & 1, (r >> 1) & 1, r & 1
            peer = (x + dx - 2 * x * dx, y + dy - 2 * y * dy, c + dc - 2 * c * dc)
            cps.append(pltpu.make_async_remote_copy(
                src_ref=pack_ref, dst_ref=packs.at[me], send_sem=psend.at[r - 1], recv_sem=precv.at[r - 1],
                device_id=peer, device_id_type=MESH))
        for cp in cps:
            cp.start()
        for cp in cps:
            cp.wait_recv()
        for cp in cps:
            cp.wait_send()
        local.wait()

    return pl.pallas_call(
        body, name="final_exchange",
        in_specs=[ANY] * (npart + 1), out_specs=[ANY] * (npart + 1),
        out_shape=[jax.ShapeDtypeStruct(p.shape, p.dtype) for p in parts]
        + [jax.ShapeDtypeStruct((8,) + pack.shape, pack.dtype)],
        scratch_shapes=[pltpu.SemaphoreType.DMA((npart,)), pltpu.SemaphoreType.DMA((npart,)),
                        pltpu.SemaphoreType.DMA((7,)), pltpu.SemaphoreType.DMA((7,)), pltpu.SemaphoreType.DMA],
    )(*parts, pack)


def _sum_packs(packs):
    def body(p_ref, o_ref):
        acc = p_ref[0]
        for d in range(1, 8):
            acc = acc + p_ref[d]
        o_ref[...] = acc

    return pl.pallas_call(
        body, name="sum_packs", out_shape=jax.ShapeDtypeStruct(packs.shape[1:], F32),
    )(packs)


def _adamw_update(g, w_ref, m_ref, v_ref, go, do, mo, vo):
    c1 = 1.0 / (1.0 - ADAM_B1 ** ADAM_STEP)
    c2 = 1.0 / (1.0 - ADAM_B2 ** ADAM_STEP)
    mn = ADAM_B1 * m_ref[...] + (1.0 - ADAM_B1) * g
    vn = ADAM_B2 * v_ref[...] + (1.0 - ADAM_B2) * (g * g)
    go[...] = g
    mo[...] = mn
    vo[...] = vn
    do[...] = -ADAM_LR * ((mn * c1) / (jnp.sqrt(vn * c2) + ADAM_EPS) + ADAM_WD * w_ref[...])


def _adamw(w, m, v, g1, g2, rows, name):
    r, cdim = w.shape
    rows = min(rows, r)

    def body(*refs):
        n_in = 4 if g2 is None else 5
        w_ref, m_ref, v_ref, g_ref = refs[:4]
        g = g_ref[...] if g2 is None else g_ref[...] + refs[4][...]
        _adamw_update(g, w_ref, m_ref, v_ref, *refs[n_in:n_in + 4])

    blk = pl.BlockSpec((rows, cdim), lambda i: (i, 0))
    args = [w, m, v, g1] + ([] if g2 is None else [g2])
    shp = jax.ShapeDtypeStruct((r, cdim), F32)
    return pl.pallas_call(
        body, name=name, grid=(r // rows,),
        in_specs=[blk] * len(args), out_specs=[blk] * 4, out_shape=[shp] * 4,
        compiler_params=_params(("parallel",), 20 * rows * cdim * 4 + 8 * 2**20),
    )(*args)


def _adamw_shard(wt, mt, vt, g1, g2):
    r, d = wt.shape
    cols = min(128, d)

    def body(w_ref, m_ref, v_ref, g_ref, g2_ref, go, do, mo, vo, pad_ref):
        chip = 2 * lax.axis_index("x") + lax.axis_index("y")
        back = [(ALIGNED_W - s) % ALIGNED_W for s in SHIFTS]
        pad_ref[...] = pltpu.roll(g_ref[...] + g2_ref[...], _by_chip(chip, back), 0)
        _adamw_update(pad_ref[0:r, :], w_ref, m_ref, v_ref, go, do, mo, vo)

    blk = pl.BlockSpec((r, cols), lambda i: (0, i))
    gblk = pl.BlockSpec((ALIGNED_W, cols), lambda i: (0, i))
    shp = jax.ShapeDtypeStruct((r, d), F32)
    return pl.pallas_call(
        body, name="adamw_w_in", grid=(d // cols,),
        in_specs=[blk] * 3 + [gblk] * 2, out_specs=[blk] * 4, out_shape=[shp] * 4,
        scratch_shapes=[pltpu.VMEM((ALIGNED_W, cols), F32)],
        compiler_params=_params(("parallel",), 24 * ALIGNED_W * cols * 4 + 8 * 2**20),
    )(wt, mt, vt, g1, g2)


def _pad_lanes(a, width):
    return jnp.pad(a, ((0, 0), (0, width - a.shape[1])))


def _gathered_to_full(g):
    return jnp.transpose(g, (1, 0, 2)).reshape(g.shape[1], 4 * g.shape[2])


def _row(a):
    return _pad_lanes(a.reshape(1, -1), 1024)


def _small_pack(nin, cb, fn, al, dt, gn, cqw_shard, cw_shard):
    ad = jnp.concatenate([al.reshape(1, -1), dt.reshape(1, -1)], axis=1)
    rows = [_row(nin), _row(cb), _row(fn), _row(ad), _row(gn), cqw_shard.reshape(3, 1024), _row(cw_shard)]
    out = jnp.concatenate(rows, axis=0)
    return jnp.pad(out, ((0, 16 - out.shape[0]), (0, 0)))


def kernel(x, norm_in_w, w_in, conv_qkv_w, A_log, dt_bias, gdn_norm_w, conv_w, conv_b, w_out, final_norm_w, loss_target, m_norm_in_w, m_w_in, m_conv_qkv_w, m_A_log, m_dt_bias, m_gdn_norm_w, m_conv_w, m_conv_b, m_w_out, m_final_norm_w, v_norm_in_w, v_w_in, v_conv_qkv_w, v_A_log, v_dt_bias, v_gdn_norm_w, v_conv_w, v_conv_b, v_w_out, v_final_norm_w):
    chip = 2 * lax.axis_index("x") + lax.axis_index("y")
    a_shard = _align_shard(jnp.transpose(w_in[0]))
    wo_b = _cast_bf16(w_out[0], 256, "cast_w_out")
    d_model = x.shape[-1]
    stack = lambda s: lax.empty((4,) + s.shape, s.dtype)
    wg0 = lax.empty((WG_BLOCKS * DH, d_model), BF16)
    wc0 = lax.empty((WC_BLOCKS * DH, d_model), BF16)
    ss_g, rs_g, bufs_g, tok_g = _gather_start("g", a_shard, wg0, [(conv_qkv_w[0], stack(conv_qkv_w[0]))])
    ss_c, rs_c, bufs_c, tok_c = _gather_start("c", bufs_g[0], wc0,
                                              [(conv_w[0], stack(conv_w[0])), (wo_b, stack(wo_b))])
    wg1, wc1, wog1, cqg1, cwg1 = _place_own(bufs_c[0], bufs_c[4], bufs_g[2], bufs_c[2],
                                            [bufs_g[1], bufs_c[1], bufs_c[5], bufs_g[3], bufs_c[3]])
    x0 = x[0]
    h = _tie(_tie(_rms_in(x0, norm_in_w), tok_g, "after_gather_start_g"), tok_c, "after_gather_start_c")
    a_thru, wg, _, cq_g = _gather_wait("g", ss_g, rs_g, [bufs_c[0], wg1, bufs_g[2], cqg1], h)
    w_g = _merge_edges(_sibling_forward("g", wg), G_EDGE, G_MIXED, "merge_edges_g")
    cqw = _gathered_to_full(cq_g)
    ad = jnp.pad(jnp.concatenate([A_log, dt_bias], axis=0), ((0, 0), (A_LANE, 0)))

    def late(o):
        _, wc, _, cw_g, _, wo_g = _gather_wait("c", ss_c, rs_c,
                                               [a_thru, wc1, bufs_c[2], cwg1, bufs_c[4], wog1], o)
        return (_merge_edges(_sibling_forward("c", wc), C_EDGE, C_MIXED, "merge_edges_c"),
                wo_g.reshape(2 * GW, d_model),
                _gathered_to_full(cw_g))

    scat = {}

    def on_grad_c(g_c, g_wout, do):
        go4 = g_wout.reshape(4, GW // 2, d_model)
        land = lax.empty((3, ALIGNED_W, d_model), BF16)
        land_o = lax.empty((3, GW // 2, d_model), BF16)
        ss, rs, bufs, tok = _scatter_start("c", g_c, land, [(go4, land_o)])
        scat["c"] = (ss, rs, bufs)
        return _tie(do, tok, "after_scatter_start_c")

    def on_grad_g(g_g, dproj_g):
        ss, rs, bufs, tok = _scatter_start("g", _pair_reduce("g", g_g), scat["c"][2][1], [], halved=True)
        scat["g"] = (ss, rs, bufs)
        return _tie(dproj_g, tok, "after_scatter_start_g")

    gx, sm, _ = _local_step(x0, loss_target[0], h, w_g, cqw, late, norm_in_w, ad, gdn_norm_w, conv_b,
                            final_norm_w.reshape(1, -1), on_grad_c, on_grad_g)

    ss, rs, bufs = scat["g"]
    g_g, land = _scatter_wait("g", ss, rs, bufs, gx, halved=True)
    ss, rs, bufs = scat["c"]
    g_c, land, go4, land_o = _scatter_wait("c", ss, rs, [bufs[0], land, bufs[2], bufs[3]], gx)
    part_in = _sum_shard(g_g, g_c, land)
    part_out = _sum_rows(go4, land_o, 128)
    ad_g = jnp.concatenate([sm["al"][:, A_LANE:], sm["dt"][:, A_LANE:]], axis=1)
    pack = jnp.concatenate([_row(sm["nin"]), _row(sm["cb"]), _row(sm["fn"]), _row(ad_g), _row(sm["gn"]),
                            jnp.concatenate(sm["cq"], axis=1).reshape(12, 1024), sm["cw"], _row(sm["loss"])], axis=0)
    pack = jnp.pad(pack, ((0, PACK_ROWS - pack.shape[0]), (0, 0)))
    sib_in, sib_out, packs = _final_exchange([part_in, part_out], pack)
    tot = _sum_packs(packs)

    g_wi, d_wi, m_wi, v_wi = [jnp.transpose(a) for a in _adamw_shard(
        jnp.transpose(w_in[0]), jnp.transpose(m_w_in[0]), jnp.transpose(v_w_in[0]), part_in, sib_in)]
    g_wo, d_wo, m_wo, v_wo = _adamw(w_out[0], m_w_out[0], v_w_out[0], part_out, sib_out, 128, "adamw_w_out")
    g_cq_sh = lax.dynamic_slice_in_dim(tot[R_CQ:R_CQ + 12].reshape(4, 3 * GW), chip * 768, 768, axis=1)
    g_cw_sh = lax.dynamic_slice_in_dim(tot[R_CW:R_CW + 3], chip * 256, 256, axis=1)
    sp = lambda nin, cb, fn, al, dt, gn, cq, cwv: _small_pack(nin, cb, fn, al, dt, gn, cq[0], cwv[0])
    g_s = _small_pack(tot[R_NIN], tot[R_CB], tot[R_FN], tot[R_AD, :HEADS], tot[R_AD, HEADS:2 * HEADS],
                      tot[R_GN, :DH], g_cq_sh, g_cw_sh)
    w_s = sp(norm_in_w, conv_b, final_norm_w, A_log, dt_bias, gdn_norm_w, conv_qkv_w, conv_w)
    m_s = sp(m_norm_in_w, m_conv_b, m_final_norm_w, m_A_log, m_dt_bias, m_gdn_norm_w, m_conv_qkv_w, m_conv_w)
    v_s = sp(v_norm_in_w, v_conv_b, v_final_norm_w, v_A_log, v_dt_bias, v_gdn_norm_w, v_conv_qkv_w, v_conv_w)
    small = _adamw(w_s, m_s, v_s, g_s, None, 16, "adamw_small")

    def unpack(a, big_in, big_out):
        return (a[0:1], big_in[None], a[5:8].reshape(1, 4, 768), a[3:4, :HEADS], a[3:4, HEADS:2 * HEADS],
                a[4:5, :DH], a[8, :768].reshape(1, 3, 256), a[1:2], big_out[None], a[2])

    loss = tot[R_LOSS, 0]
    return (loss, gx[None], *unpack(small[0], g_wi, g_wo), *unpack(small[1], d_wi, d_wo),
            *unpack(small[2], m_wi, m_wo), *unpack(small[3], v_wi, v_wo))
```

```python
import functools
import math

import jax
import jax.numpy as jnp
from jax import lax
from jax.experimental import pallas as pl
from jax.experimental.pallas import tpu as pltpu

F32 = jnp.float32
BF16 = jnp.bfloat16
MESH = pl.DeviceIdType.MESH
ANY = pl.BlockSpec(memory_space=pl.ANY)

HEADS = 8
DH = 128
CH = 64
GW = HEADS * DH
EPS = 1e-6
VMEM_V7X = 64 * 1024 * 1024

QB, KB, VB, ZB, BAB = 0, 8, 16, 24, 32
A_LANE = 120
NG, NC = 33, 32
GW_COLS, CW_COLS = NG * DH, NC * DH

SHARD_W = 2052
ALIGNED_BLOCKS = 17
ALIGNED_W = ALIGNED_BLOCKS * DH
SHIFTS = (0, 4, ALIGNED_W - 8, ALIGNED_W - 4)
G_EDGE, C_EDGE = 34, 32
G_SPARE, C_SPARE = 33, 34
WG_BLOCKS, WC_BLOCKS = 38, 36
G_MIXED, C_MIXED = (2, BAB), (4 * 7 + 1,)


def _shard_blocks(chip, edges):
    g, c = "g", "c"
    if chip == 0:
        out = [(g, 3 * b) for b in range(8)] + [(g, 3 * b + 1) for b in range(8)] + [(g, G_EDGE, G_MIXED[0])]
    elif chip == 1:
        out = [(g, G_EDGE + 1, G_MIXED[0])] + [(g, 3 * b + 2) for b in range(1, 8)]
        out += [(g, ZB + b) for b in range(8)] + [(g, G_EDGE + 2, G_MIXED[1])]
    elif chip == 2:
        out = [(c, 4 * b) for b in range(8)] + [(c, 4 * b + 1) for b in range(7)]
        out += [(c, C_EDGE, C_MIXED[0]), (g, G_EDGE + 3, G_MIXED[1])]
    else:
        out = [(c, 4 * b + 2) for b in range(8)] + [(c, 4 * b + 3) for b in range(8)] + [(c, C_EDGE + 1, C_MIXED[0])]
    return [(o[0], o[1] if (edges or len(o) == 2) else o[2]) for o in out]


def _by_chip(chip, vals):
    if all(v == vals[0] for v in vals):
        return vals[0]
    r = vals[3]
    for kk in (2, 1, 0):
        r = jnp.where(chip == kk, vals[kk], r)
    return r

ADAM_LR, ADAM_B1, ADAM_B2, ADAM_EPS, ADAM_WD, ADAM_STEP = 0.001, 0.9, 0.999, 1e-08, 0.01, 10

R_NIN, R_CB, R_FN, R_AD, R_GN, R_CQ, R_CW, R_LOSS, PACK_ROWS = 0, 1, 2, 3, 4, 5, 17, 20, 24

NN = ((1,), (0,))
NT = ((1,), (1,))
TN = ((0,), (0,))


def _dot(a, b, dims=NN, mode="lo"):
    dn = (dims, ((), ()))
    if mode == "hi":
        return lax.dot_general(a, b, dn, precision=lax.Precision.HIGHEST, preferred_element_type=F32)
    ah, bh = a.astype(BF16), b.astype(BF16)
    out = lax.dot_general(ah, bh, dn, preferred_element_type=F32)
    if mode == "x3":
        al = (a - ah.astype(F32)).astype(BF16)
        bl = (b - bh.astype(F32)).astype(BF16)
        out = out + lax.dot_general(ah, bl, dn, preferred_element_type=F32)
        out = out + lax.dot_general(al, bh, dn, preferred_element_type=F32)
    return out


P_GRAM, P_INV, P_SOL, P_SCAN, P_SCANB, P_BWD = "lo", "lo", "lo", "lo", "lo", "lo"
P_CUM = "x3"


def _params(sem=None, vmem=None):
    kw = {}
    if sem is not None:
        kw["dimension_semantics"] = sem
    if vmem is not None:
        kw["vmem_limit_bytes"] = int(min(max(vmem, 32 * 2**20), VMEM_V7X - 8 * 2**20))
    return pltpu.CompilerParams(**kw)


def _sigmoid(x):
    return 1.0 / (1.0 + jnp.exp(-x))


def _dsilu(x, s):
    return s * (1.0 + x * (1.0 - s))


def _rows(shape):
    return lax.broadcasted_iota(jnp.int32, shape, 0)


def _shift_down(x, s):
    if s == 0:
        return x
    return jnp.where(_rows(x.shape) >= s, pltpu.roll(x, s, 0), 0.0)


def _shift_up(x, s):
    if s == 0:
        return x
    n = x.shape[0]
    return jnp.where(_rows(x.shape) < n - s, pltpu.roll(x, n - s, 0), 0.0)


def _matmul(a, b, dims, out_dtype, tm, tn, tk, name, add=None, n=None):
    if dims == NN:
        (m, k), n = a.shape, b.shape[1]
    elif dims == NT:
        (m, k), n = a.shape, (n or b.shape[0])
    else:
        (k, m), n = a.shape, b.shape[1]
    tm, tn, tk = min(tm, m), min(tn, n), min(tk, k)
    assert m % tm == 0 and n % tn == 0 and k % tk == 0, (name, m, n, k, tm, tn, tk)
    nk = k // tk

    def body(*refs):
        if add is None:
            a_ref, b_ref, o_ref = refs[:3]
            add_ref = None
        else:
            a_ref, b_ref, add_ref, o_ref = refs[:4]
        part = _dot(a_ref[...], b_ref[...], dims)
        if nk == 1:
            if add_ref is not None:
                part = part + add_ref[...]
            o_ref[...] = part.astype(out_dtype)
            return
        acc = refs[-1]
        kk = pl.program_id(2)

        @pl.when(kk == 0)
        def _():
            acc[...] = part

        @pl.when(kk > 0)
        def _():
            acc[...] += part

        @pl.when(kk == nk - 1)
        def _():
            r = acc[...]
            if add_ref is not None:
                r = r + add_ref[...]
            o_ref[...] = r.astype(out_dtype)

    if dims == TN:
        a_spec = pl.BlockSpec((tk, tm), lambda i, j, kk: (kk, i))
    else:
        a_spec = pl.BlockSpec((tm, tk), lambda i, j, kk: (i, kk))
    if dims == NT:
        b_spec = pl.BlockSpec((tn, tk), lambda i, j, kk: (j, kk))
    else:
        b_spec = pl.BlockSpec((tk, tn), lambda i, j, kk: (kk, j))
    o_spec = pl.BlockSpec((tm, tn), lambda i, j, kk: (i, j))
    in_specs = [a_spec, b_spec]
    args = [a, b]
    if add is not None:
        in_specs.append(o_spec)
        args.append(add)
    osz = jnp.dtype(out_dtype).itemsize
    est = 2 * (tm * tk * a.dtype.itemsize + tk * tn * b.dtype.itemsize + tm * tn * osz)
    est += 3 * tm * tn * 4 + (2 * tm * tn * 4 if add is not None else 0)
    return pl.pallas_call(
        body, name=name, grid=(m // tm, n // tn, nk),
        in_specs=in_specs, out_specs=o_spec,
        out_shape=jax.ShapeDtypeStruct((m, n), out_dtype),
        scratch_shapes=[pltpu.VMEM((tm, tn), F32)] if nk > 1 else [],
        compiler_params=_params(("parallel", "parallel", "arbitrary"), est + 8 * 2**20),
    )(*args)


def _cast_bf16(a, rows, name):
    r, c = a.shape
    rows = min(rows, r)

    def body(a_ref, o_ref):
        o_ref[...] = a_ref[...].astype(BF16)

    return pl.pallas_call(
        body, name=name, grid=(r // rows,),
        in_specs=[pl.BlockSpec((rows, c), lambda i: (i, 0))],
        out_specs=pl.BlockSpec((rows, c), lambda i: (i, 0)),
        out_shape=jax.ShapeDtypeStruct((r, c), BF16),
        compiler_params=_params(("parallel",)),
    )(a)


def _align_shard(wt):
    r, d = wt.shape
    cols = min(256, d)

    def body(w_ref, o_ref, pad_ref):
        chip = 2 * lax.axis_index("x") + lax.axis_index("y")
        pad_ref[...] = jnp.zeros_like(pad_ref)
        pad_ref[0:r, :] = w_ref[...]
        o_ref[...] = pltpu.roll(pad_ref[...], _by_chip(chip, SHIFTS), 0).astype(BF16)

    return pl.pallas_call(
        body, name="align_shard", grid=(d // cols,),
        in_specs=[pl.BlockSpec((r, cols), lambda i: (0, i))],
        out_specs=pl.BlockSpec((ALIGNED_W, cols), lambda i: (0, i)),
        out_shape=jax.ShapeDtypeStruct((ALIGNED_W, d), BF16),
        scratch_shapes=[pltpu.VMEM((ALIGNED_W, cols), F32)],
        compiler_params=_params(("parallel",)),
    )(wt)


def _rms_in(x, w):
    n, d = x.shape
    tr = min(256, n)

    def body(x_ref, w_ref, h_ref):
        xv = x_ref[...]
        r = lax.rsqrt(jnp.mean(xv * xv, axis=-1, keepdims=True) + EPS)
        h_ref[...] = (xv * r * w_ref[...]).astype(BF16)

    return pl.pallas_call(
        body, name="rms_in", grid=(n // tr,),
        in_specs=[pl.BlockSpec((tr, d), lambda i: (i, 0)), pl.BlockSpec((1, d), lambda i: (0, 0))],
        out_specs=pl.BlockSpec((tr, d), lambda i: (i, 0)),
        out_shape=jax.ShapeDtypeStruct((n, d), BF16),
        compiler_params=_params(("parallel",)),
    )(x, w)


def _conv_silu(p, w_ref, taps):
    c = None
    for j in range(taps):
        t = _shift_down(p, taps - 1 - j) * w_ref[j:j + 1, :]
        c = t if c is None else c + t
    return c


def _prep_qkv(proj, cw):
    n = proj.shape[0]

    def body(p3, wq, wk, wv, q_ref, k_ref, v_ref):
        for kind, (w_ref, o_ref) in enumerate(((wq, q_ref), (wk, k_ref), (wv, v_ref))):
            c = _conv_silu(p3[:, kind * DH:(kind + 1) * DH], w_ref, 4)
            a = c * _sigmoid(c)
            if kind < 2:
                r = lax.rsqrt(jnp.sum(a * a, axis=-1, keepdims=True) + EPS)
                a = a * (r * (DH ** -0.5 if kind == 0 else 1.0))
            o_ref[...] = a

    col = pl.BlockSpec((n, DH), lambda h: (0, h))
    wcol = lambda base: pl.BlockSpec((4, DH), lambda h: (0, base + h))
    out = jax.ShapeDtypeStruct((n, GW), F32)
    return pl.pallas_call(
        body, name="prep_qkv", grid=(HEADS,),
        in_specs=[pl.BlockSpec((n, 3 * DH), lambda h: (0, h)), wcol(QB), wcol(KB), wcol(VB)],
        out_specs=[col] * 3, out_shape=[out] * 3,
        compiler_params=_params(("parallel",), 40 * 2**20),
    )(proj, cw, cw, cw)


def _prep_qkv_bwd(proj, cw, dq, dk, dv, dproj):
    n = proj.shape[0]

    def body(p3, wq, wk, wv, dq_ref, dk_ref, dv_ref, _, o3, gq, gk, gv):
        for kind, (w_ref, d_ref, g_ref) in enumerate(((wq, dq_ref, gq), (wk, dk_ref, gk), (wv, dv_ref, gv))):
            p = p3[:, kind * DH:(kind + 1) * DH]
            c = _conv_silu(p, w_ref, 4)
            s = _sigmoid(c)
            a = c * s
            d = d_ref[...]
            if kind < 2:
                r = lax.rsqrt(jnp.sum(a * a, axis=-1, keepdims=True) + EPS)
                sc = DH ** -0.5 if kind == 0 else 1.0
                d = (sc * r) * (d - a * ((r * r) * jnp.sum(d * a, axis=-1, keepdims=True)))
            dc = d * _dsilu(c, s)
            dp = None
            for j in range(4):
                g_ref[j:j + 1, :] = jnp.sum(dc * _shift_down(p, 3 - j), axis=0, keepdims=True)
                t = _shift_up(dc, 3 - j) * w_ref[j:j + 1, :]
                dp = t if dp is None else dp + t
            o3[:, kind * DH:(kind + 1) * DH] = dp.astype(BF16)

    col = pl.BlockSpec((n, DH), lambda h: (0, h))
    wcol = lambda base: pl.BlockSpec((4, DH), lambda h: (0, base + h))
    p3spec = pl.BlockSpec((n, 3 * DH), lambda h: (0, h))
    return pl.pallas_call(
        body, name="prep_qkv_bwd", grid=(HEADS,),
        in_specs=[p3spec, wcol(QB), wcol(KB), wcol(VB), col, col, col, ANY],
        out_specs=[p3spec] + [wcol(0)] * 3,
        out_shape=[jax.ShapeDtypeStruct(dproj.shape, BF16)] + [jax.ShapeDtypeStruct((4, GW), F32)] * 3,
        input_output_aliases={7: 0},
        compiler_params=_params(("parallel",), 48 * 2**20),
    )(proj, cw, cw, cw, dq, dk, dv, dproj)


def _tri(lower_incl):
    i = lax.broadcasted_iota(jnp.int32, (CH, CH), 0)
    j = lax.broadcasted_iota(jnp.int32, (CH, CH), 1)
    return jnp.where(i >= j, 1.0, 0.0) if lower_incl else jnp.where(j >= i, 1.0, 0.0)


def _lane(shape):
    return lax.broadcasted_iota(jnp.int32, shape, 1)


def _prep_bg(proj, ad):
    n = proj.shape[0]
    nch = n // CH

    def body(p_ref, ad_ref, bg_ref, bgt_ref):
        p = p_ref[...]
        lane = _lane(p.shape)
        beta = _sigmoid(p)
        xa = p + ad_ref[1:2, :]
        sp = jnp.maximum(xa, 0.0) + jnp.log(1.0 + jnp.exp(-jnp.abs(xa)))
        g = pltpu.roll(-jnp.exp(ad_ref[0:1, :]) * sp, DH - A_LANE + HEADS, 1)
        gc = _dot(_tri(True), g, NN, P_CUM)
        bg = jnp.where(lane < HEADS, beta, jnp.where(lane < 2 * HEADS, gc, 0.0))
        bg_ref[...] = bg
        bgt_ref[0] = bg.T

    return pl.pallas_call(
        body, name="prep_bg", grid=(nch,),
        in_specs=[pl.BlockSpec((CH, DH), lambda i: (i, BAB)), pl.BlockSpec((2, DH), lambda i: (0, 0))],
        out_specs=[pl.BlockSpec((CH, DH), lambda i: (i, 0)), pl.BlockSpec((1, DH, CH), lambda i: (i, 0, 0))],
        out_shape=[jax.ShapeDtypeStruct((n, DH), F32), jax.ShapeDtypeStruct((nch, DH, CH), F32)],
        compiler_params=_params(("parallel",)),
    )(proj, ad)


def _prep_bg_bwd(proj, ad, dbg, dproj):
    n = proj.shape[0]
    nch = n // CH

    def body(p_ref, ad_ref, d_ref, _, o_ref, ga_ref, gd_ref):
        p = p_ref[...]
        d = d_ref[...]
        lane = _lane(p.shape)
        beta = _sigmoid(p)
        xa = p + ad_ref[1:2, :]
        sp = jnp.maximum(xa, 0.0) + jnp.log(1.0 + jnp.exp(-jnp.abs(xa)))
        na = -jnp.exp(ad_ref[0:1, :])
        dg = pltpu.roll(_dot(_tri(False), d, NN, P_CUM), A_LANE - HEADS, 1)
        da = dg * na * _sigmoid(xa)
        is_g = lane >= A_LANE
        o_ref[...] = jnp.where(lane < HEADS, d * beta * (1.0 - beta), jnp.where(is_g, da, 0.0)).astype(BF16)
        ga = jnp.sum(jnp.where(is_g, dg * na * sp, 0.0), axis=0, keepdims=True)
        gd = jnp.sum(jnp.where(is_g, da, 0.0), axis=0, keepdims=True)

        @pl.when(pl.program_id(0) == 0)
        def _():
            ga_ref[...] = jnp.zeros_like(ga_ref)
            gd_ref[...] = jnp.zeros_like(gd_ref)

        ga_ref[...] += ga
        gd_ref[...] += gd

    one = pl.BlockSpec((1, DH), lambda i: (0, 0))
    return pl.pallas_call(
        body, name="prep_bg_bwd", grid=(nch,),
        in_specs=[pl.BlockSpec((CH, DH), lambda i: (i, BAB)), pl.BlockSpec((2, DH), lambda i: (0, 0)),
                  pl.BlockSpec((CH, DH), lambda i: (i, 0)), ANY],
        out_specs=[pl.BlockSpec((CH, DH), lambda i: (i, BAB)), one, one],
        out_shape=[jax.ShapeDtypeStruct(dproj.shape, BF16), jax.ShapeDtypeStruct((1, DH), F32),
                   jax.ShapeDtypeStruct((1, DH), F32)],
        input_output_aliases={3: 0},
        compiler_params=_params(("arbitrary",)),
    )(proj, ad, dbg, dproj)


def _gdn_out(o, proj, wg):
    n = o.shape[0]

    def body(o_ref, z_ref, w_ref, y_ref):
        ov, z = o_ref[...], z_ref[...]
        r = lax.rsqrt(jnp.mean(ov * ov, axis=-1, keepdims=True) + EPS)
        y_ref[...] = (ov * r * w_ref[...] * (z * _sigmoid(z))).astype(BF16)

    return pl.pallas_call(
        body, name="gdn_out", grid=(HEADS,),
        in_specs=[pl.BlockSpec((n, DH), lambda h: (0, h)), pl.BlockSpec((n, DH), lambda h: (0, ZB + h)),
                  pl.BlockSpec((1, DH), lambda h: (0, 0))],
        out_specs=pl.BlockSpec((n, DH), lambda h: (0, h)),
        out_shape=jax.ShapeDtypeStruct((n, 2 * GW), BF16),
        compiler_params=_params(("parallel",)),
    )(o, proj, wg)


def _gdn_out_bwd(o, proj, wg, dmix):
    n = o.shape[0]

    def body(o_ref, z_ref, w_ref, d_ref, do_ref, dz_ref, gw_ref):
        ov, z, d, w = o_ref[...], z_ref[...], d_ref[...], w_ref[...]
        r = lax.rsqrt(jnp.mean(ov * ov, axis=-1, keepdims=True) + EPS)
        nrm = ov * r
        s = _sigmoid(z)
        dz_ref[...] = (d * (nrm * w) * _dsilu(z, s)).astype(BF16)
        dn_w = d * (z * s)
        gw = jnp.sum(dn_w * nrm, axis=0, keepdims=True)
        dn = dn_w * w
        do_ref[...] = r * (dn - nrm * jnp.mean(dn * nrm, axis=-1, keepdims=True))

        @pl.when(pl.program_id(0) == 0)
        def _():
            gw_ref[...] = jnp.zeros_like(gw_ref)

        gw_ref[...] += gw

    return pl.pallas_call(
        body, name="gdn_out_bwd", grid=(HEADS,),
        in_specs=[pl.BlockSpec((n, DH), lambda h: (0, h)), pl.BlockSpec((n, DH), lambda h: (0, ZB + h)),
                  pl.BlockSpec((1, DH), lambda h: (0, 0)), pl.BlockSpec((n, DH), lambda h: (0, h))],
        out_specs=[pl.BlockSpec((n, DH), lambda h: (0, h)), pl.BlockSpec((n, DH), lambda h: (0, ZB + h)),
                   pl.BlockSpec((1, DH), lambda h: (0, 0))],
        out_shape=[jax.ShapeDtypeStruct((n, GW), F32), jax.ShapeDtypeStruct((n, GW_COLS), BF16),
                   jax.ShapeDtypeStruct((1, DH), F32)],
        compiler_params=_params(("arbitrary",)),
    )(o, proj, wg, dmix)


def _conv_branch(proj, w3, b, mix):
    n = proj.shape[0]

    def body(p4, w_ref, b_ref, _, y_ref):
        u = p4[:, DH:2 * DH] * p4[:, 2 * DH:3 * DH]
        cc = _conv_silu(u, w_ref, 3) + b_ref[...]
        z = p4[:, 3 * DH:4 * DH]
        y_ref[...] = (p4[:, 0:DH] * cc * (z * _sigmoid(z))).astype(BF16)

    return pl.pallas_call(
        body, name="conv_branch", grid=(HEADS,),
        in_specs=[pl.BlockSpec((n, 4 * DH), lambda h: (0, h)), pl.BlockSpec((3, DH), lambda h: (0, h)),
                  pl.BlockSpec((1, DH), lambda h: (0, h)), ANY],
        out_specs=pl.BlockSpec((n, DH), lambda h: (0, HEADS + h)),
        out_shape=jax.ShapeDtypeStruct(mix.shape, BF16),
        input_output_aliases={3: 0},
        compiler_params=_params(("parallel",), 40 * 2**20),
    )(proj, w3, b, mix)


def _conv_branch_bwd(proj, w3, b, dmix):
    n = proj.shape[0]

    def body(p4, w_ref, b_ref, d_ref, o4, gw_ref, gbias_ref):
        gb, gcv, hc, z = p4[:, 0:DH], p4[:, DH:2 * DH], p4[:, 2 * DH:3 * DH], p4[:, 3 * DH:4 * DH]
        d = d_ref[...]
        dgb, dgc, dhc, dzc = (o4.at[:, kk * DH:(kk + 1) * DH] for kk in range(4))
        u = gcv * hc
        cc = _conv_silu(u, w_ref, 3) + b_ref[...]
        s = _sigmoid(z)
        dzc[...] = (d * (gb * cc) * _dsilu(z, s)).astype(BF16)
        dp = d * (z * s)
        dgb[...] = (dp * cc).astype(BF16)
        dcc = dp * gb
        gbias_ref[...] = jnp.sum(dcc, axis=0, keepdims=True)
        du = None
        for j in range(3):
            gw_ref[j:j + 1, :] = jnp.sum(dcc * _shift_down(u, 2 - j), axis=0, keepdims=True)
            t = _shift_up(dcc, 2 - j) * w_ref[j:j + 1, :]
            du = t if du is None else du + t
        dgc[...] = (du * hc).astype(BF16)
        dhc[...] = (du * gcv).astype(BF16)

    p4spec = pl.BlockSpec((n, 4 * DH), lambda h: (0, h))
    return pl.pallas_call(
        body, name="conv_branch_bwd", grid=(HEADS,),
        in_specs=[p4spec, pl.BlockSpec((3, DH), lambda h: (0, h)), pl.BlockSpec((1, DH), lambda h: (0, h)),
                  pl.BlockSpec((n, DH), lambda h: (0, HEADS + h))],
        out_specs=[p4spec, pl.BlockSpec((3, DH), lambda h: (0, h)), pl.BlockSpec((1, DH), lambda h: (0, h))],
        out_shape=[jax.ShapeDtypeStruct((n, CW_COLS), BF16), jax.ShapeDtypeStruct((3, GW), F32),
                   jax.ShapeDtypeStruct((1, GW), F32)],
        compiler_params=_params(("parallel",), 48 * 2**20),
    )(proj, w3, b, dmix)


def _final_loss(out, tgt, wf):
    n, d = out.shape
    tr = min(256, n)

    def body(o_ref, t_ref, w_ref, do_ref, dob_ref, gw_ref, loss_ref):
        ov, w = o_ref[...], w_ref[...]
        r = lax.rsqrt(jnp.mean(ov * ov, axis=-1, keepdims=True) + EPS)
        nrm = ov * r
        e = nrm * w - t_ref[...]
        dy = e * (1.0 / d)
        dn = dy * w
        dout = r * (dn - nrm * jnp.mean(dn * nrm, axis=-1, keepdims=True))
        do_ref[...] = dout
        dob_ref[...] = dout.astype(BF16)

        @pl.when(pl.program_id(0) == 0)
        def _():
            gw_ref[...] = jnp.zeros_like(gw_ref)
            loss_ref[...] = jnp.zeros_like(loss_ref)

        gw_ref[...] += jnp.sum(dy * nrm, axis=0, keepdims=True)
        loss_ref[...] += (0.5 / d) * jnp.sum(jnp.sum(e * e, axis=-1, keepdims=True), axis=0, keepdims=True)

    row = pl.BlockSpec((tr, d), lambda i: (i, 0))
    return pl.pallas_call(
        body, name="final_loss", grid=(n // tr,),
        in_specs=[row, row, pl.BlockSpec((1, d), lambda i: (0, 0))],
        out_specs=[row, row, pl.BlockSpec((1, d), lambda i: (0, 0)), pl.BlockSpec((1, 1), lambda i: (0, 0))],
        out_shape=[jax.ShapeDtypeStruct((n, d), F32), jax.ShapeDtypeStruct((n, d), BF16),
                   jax.ShapeDtypeStruct((1, d), F32), jax.ShapeDtypeStruct((1, 1), F32)],
        compiler_params=_params(("arbitrary",)),
    )(out, tgt, wf)


def _rms_in_bwd(x, w, dh, dout):
    n, d = x.shape
    tr = min(256, n)

    def body(x_ref, w_ref, dh_ref, do_ref, dx_ref, gw_ref):
        xv, dhv = x_ref[...], dh_ref[...]
        r = lax.rsqrt(jnp.mean(xv * xv, axis=-1, keepdims=True) + EPS)
        xn = xv * r
        dxn = dhv * w_ref[...]
        dx_ref[...] = r * (dxn - xn * jnp.mean(dxn * xn, axis=-1, keepdims=True)) + do_ref[...]

        @pl.when(pl.program_id(0) == 0)
        def _():
            gw_ref[...] = jnp.zeros_like(gw_ref)

        gw_ref[...] += jnp.sum(dhv * xn, axis=0, keepdims=True)

    row = pl.BlockSpec((tr, d), lambda i: (i, 0))
    one = pl.BlockSpec((1, d), lambda i: (0, 0))
    return pl.pallas_call(
        body, name="rms_in_bwd", grid=(n // tr,),
        in_specs=[row, one, row, row], out_specs=[row, one],
        out_shape=[jax.ShapeDtypeStruct((n, d), F32), jax.ShapeDtypeStruct((1, d), F32)],
        compiler_params=_params(("arbitrary",)),
    )(x, w, dh, dout)


def _ij():
    i = lax.broadcasted_iota(jnp.int32, (CH, CH), 0)
    j = lax.broadcasted_iota(jnp.int32, (CH, CH), 1)
    return i, j


def _unit_lower_inverse(mats):
    i, j = _ij()
    eye = jnp.where(i == j, 1.0, 0.0)
    same16 = (i // 16) == (j // 16)
    same32 = (i // 32) == (j // 32)
    mm = lambda xs, ys: [_dot(x, y, NN, P_INV) for x, y in zip(xs, ys)]
    n1 = [jnp.where(same16, -a, 0.0) for a in mats]
    n2 = mm(n1, n1)
    n4 = mm(n2, n2)
    n8 = mm(n4, n4)
    t = [eye + x1 + x2 + x3 for x1, x2, x3 in zip(n1, n2, mm(n1, n2))]
    t = [x + y for x, y in zip(t, mm(t, n4))]
    t = [x + y for x, y in zip(t, mm(t, n8))]
    a1 = [jnp.where(same32 & jnp.logical_not(same16), a, 0.0) for a in mats]
    t = [x - y for x, y in zip(t, mm(t, mm(a1, t)))]
    a2 = [jnp.where(same32, 0.0, a) for a in mats]
    t = [x - y for x, y in zip(t, mm(t, mm(a2, t)))]
    return t


def _head_vectors(bg, bgt, h):
    bcol = bg[:, h:h + 1]
    gcol = bg[:, HEADS + h:HEADS + h + 1]
    grow = bgt[HEADS + h:HEADS + h + 1, :]
    return bcol, gcol, grow


def _decay(gcol, grow):
    i, j = _ij()
    return jnp.where(i >= j, jnp.exp(jnp.where(i >= j, gcol - grow, 0.0)), 0.0)


def _gdn_intra(q, k, v, bg, bgt):
    n = q.shape[0]
    nch = n // CH
    cps = 4 if nch % 4 == 0 else 1

    def body(q_ref, k_ref, v_ref, bg_ref, bgt_ref, u_ref, w_ref, p_ref, t_ref):
        i, j = _ij()
        items = [(ci, h) for ci in range(cps) for h in range(HEADS)]
        at = lambda ref, ci, h: ref.at[ci * CH:(ci + 1) * CH, h * DH:(h + 1) * DH]
        bgs = [bg_ref[ci * CH:(ci + 1) * CH, :] for ci in range(cps)]
        ks = [at(k_ref, ci, h)[...] for ci, h in items]
        vecs = [_head_vectors(bgs[ci], bgt_ref[ci], h) for ci, h in items]
        decs = [_decay(gcol, grow) for _, gcol, grow in vecs]
        kks = [_dot(kh, kh, NT, P_GRAM) for kh in ks]
        qks = [_dot(at(q_ref, ci, h)[...], kh, NT, P_GRAM) for (ci, h), kh in zip(items, ks)]
        ts = _unit_lower_inverse([jnp.where(i > j, bcol * kk * dec, 0.0)
                                  for (bcol, _, _), kk, dec in zip(vecs, kks, decs)])
        us = [_dot(t, at(v_ref, ci, h)[...] * bcol, NN, P_SOL) for t, (ci, h), (bcol, _, _) in zip(ts, items, vecs)]
        ws = [_dot(t, kh * (bcol * jnp.exp(gcol)), NN, P_SOL) for t, kh, (bcol, gcol, _) in zip(ts, ks, vecs)]
        for n_, (ci, h) in enumerate(items):
            p_ref[ci, h] = qks[n_] * decs[n_]
            t_ref[ci, h] = ts[n_]
            at(u_ref, ci, h)[...] = us[n_]
            at(w_ref, ci, h)[...] = ws[n_]

    row = pl.BlockSpec((cps * CH, GW), lambda c: (c, 0))
    sq = pl.BlockSpec((cps, HEADS, CH, CH), lambda c: (c, 0, 0, 0))
    big = jax.ShapeDtypeStruct((n, GW), F32)
    sqs = jax.ShapeDtypeStruct((nch, HEADS, CH, CH), F32)
    return pl.pallas_call(
        body, name="gdn_intra", grid=(nch // cps,),
        in_specs=[row, row, row, pl.BlockSpec((cps * CH, DH), lambda c: (c, 0)),
                  pl.BlockSpec((cps, DH, CH), lambda c: (c, 0, 0))],
        out_specs=[row, row, sq, sq], out_shape=[big, big, sqs, sqs],
        compiler_params=_params(("parallel",)),
    )(q, k, v, bg, bgt)


def _gdn_scan(q, k, bg, u, w, p):
    n = q.shape[0]
    nch = n // CH

    def body(q_ref, k_ref, bg_ref, u_ref, w_ref, p_ref, o_ref, vn_ref, s_out, s_scr):
        @pl.when(pl.program_id(0) == 0)
        def _():
            s_scr[...] = jnp.zeros_like(s_scr)

        bg = bg_ref[...]
        hs = range(HEADS)
        sls = [slice(h * DH, (h + 1) * DH) for h in hs]
        gcols = [bg[:, HEADS + h:HEADS + h + 1] for h in hs]
        glasts = [g[CH - 1:CH, :] for g in gcols]
        ss = [s_scr[h] for h in hs]
        wss = [_dot(w_ref[:, sl], s, NN, P_SCAN) for sl, s in zip(sls, ss)]
        oqs = [_dot(q_ref[:, sl] * jnp.exp(g), s, NN, P_SCAN) for sl, s, g in zip(sls, ss, gcols)]
        vns = [u_ref[:, sl] - x for sl, x in zip(sls, wss)]
        ops = [_dot(p_ref[0, h], vn, NN, P_SCAN) for h, vn in zip(hs, vns)]
        sns = [_dot(k_ref[:, sl] * jnp.exp(gl - g), vn, TN, P_SCAN)
               for sl, gl, g, vn in zip(sls, glasts, gcols, vns)]
        for h, sl in enumerate(sls):
            s_out[0, :, sl] = ss[h]
            vn_ref[:, sl] = vns[h]
            o_ref[:, sl] = oqs[h] + ops[h]
            s_scr[h] = ss[h] * jnp.exp(glasts[h]) + sns[h]

    row = pl.BlockSpec((CH, GW), lambda c: (c, 0))
    big = jax.ShapeDtypeStruct((n, GW), F32)
    return pl.pallas_call(
        body, name="gdn_scan", grid=(nch,),
        in_specs=[row, row, pl.BlockSpec((CH, DH), lambda c: (c, 0)), row, row,
                  pl.BlockSpec((1, HEADS, CH, CH), lambda c: (c, 0, 0, 0))],
        out_specs=[row, row, pl.BlockSpec((1, DH, GW), lambda c: (c, 0, 0))],
        out_shape=[big, big, jax.ShapeDtypeStruct((nch, DH, GW), F32)],
        scratch_shapes=[pltpu.VMEM((HEADS, DH, DH), F32)],
        compiler_params=_params(("arbitrary",)),
    )(q, k, bg, u, w, p)


def _gdn_scan_bwd(q, k, bg, w, p, vn, s_in, do):
    n = q.shape[0]
    nch = n // CH
    rev = lambda c: nch - 1 - c

    def body(q_ref, k_ref, bg_ref, w_ref, p_ref, vn_ref, s_ref, do_ref,
             dqg_ref, dp_ref, du_ref, dw_ref, dks_ref, dgam_ref, ds_scr):
        @pl.when(pl.program_id(0) == 0)
        def _():
            ds_scr[...] = jnp.zeros_like(ds_scr)

        bg = bg_ref[...]
        lane = _lane((1, DH))
        hs = range(HEADS)
        sls = [slice(h * DH, (h + 1) * DH) for h in hs]
        gcols = [bg[:, HEADS + h:HEADS + h + 1] for h in hs]
        glasts = [g[CH - 1:CH, :] for g in gcols]
        ss = [s_ref[0, :, sl] for sl in sls]
        dss = [ds_scr[h] for h in hs]
        dos = [do_ref[:, sl] for sl in sls]
        vnl = [vn_ref[:, sl] for sl in sls]
        dqgs = [_dot(d, s, NT, P_SCANB) for d, s in zip(dos, ss)]
        dps = [_dot(d, vn, NT, P_SCANB) for d, vn in zip(dos, vnl)]
        dvn1 = [_dot(p_ref[0, h], d, TN, P_SCANB) for h, d in zip(hs, dos)]
        dvn2 = [_dot(k_ref[:, sl] * jnp.exp(gl - g), ds, NN, P_SCANB)
                for sl, gl, g, ds in zip(sls, glasts, gcols, dss)]
        dkss = [_dot(vn, ds, NT, P_SCANB) for vn, ds in zip(vnl, dss)]
        dsq = [_dot(q_ref[:, sl] * jnp.exp(g), d, TN, P_SCANB) for sl, g, d in zip(sls, gcols, dos)]
        dvns = [a + b for a, b in zip(dvn1, dvn2)]
        dws = [_dot(dvn, s, NT, P_SCANB) for dvn, s in zip(dvns, ss)]
        dsw = [_dot(w_ref[:, sl], dvn, TN, P_SCANB) for sl, dvn in zip(sls, dvns)]
        dgam = jnp.zeros((1, DH), F32)
        for h, sl in enumerate(sls):
            dqg_ref[:, sl] = dqgs[h]
            dp_ref[0, h] = dps[h]
            du_ref[:, sl] = dvns[h]
            dw_ref[:, sl] = -dws[h]
            dks_ref[:, sl] = dkss[h]
            tot = jnp.sum(jnp.sum(dss[h] * ss[h], axis=-1, keepdims=True), axis=0, keepdims=True)
            dgam = dgam + jnp.where(lane == h, tot, 0.0)
            ds_scr[h] = dss[h] * jnp.exp(glasts[h]) + dsq[h] - dsw[h]
        dgam_ref[0] = jnp.broadcast_to(dgam, (8, DH))

    row = pl.BlockSpec((CH, GW), lambda c: (rev(c), 0))
    sq = pl.BlockSpec((1, HEADS, CH, CH), lambda c: (rev(c), 0, 0, 0))
    big = jax.ShapeDtypeStruct((n, GW), F32)
    return pl.pallas_call(
        body, name="gdn_scan_bwd", grid=(nch,),
        in_specs=[row, row, pl.BlockSpec((CH, DH), lambda c: (rev(c), 0)), row, sq, row,
                  pl.BlockSpec((1, DH, GW), lambda c: (rev(c), 0, 0)), row],
        out_specs=[row, sq, row, row, row, pl.BlockSpec((1, 8, DH), lambda c: (rev(c), 0, 0))],
        out_shape=[big, jax.ShapeDtypeStruct((nch, HEADS, CH, CH), F32), big, big, big,
                   jax.ShapeDtypeStruct((nch, 8, DH), F32)],
        scratch_shapes=[pltpu.VMEM((HEADS, DH, DH), F32)],
        compiler_params=_params(("arbitrary",)),
    )(q, k, bg, w, p, vn, s_in, do)


def _gdn_intra_bwd(q, k, v, bg, bgt, t, u, w, p, dqg, dp, du, dw, dks, dgam):
    n = q.shape[0]
    nch = n // CH
    cps = 1

    def body(q_ref, k_ref, v_ref, bg_ref, bgt_ref, t_ref, u_ref, w_ref, p_ref,
             dqg_ref, dp_ref, du_ref, dw_ref, dks_ref, dgam_ref, dq_ref, dk_ref, dv_ref, dbg_ref):
        i, j = _ij()
        rows1 = lax.broadcasted_iota(jnp.int32, (CH, 1), 0)
        lane = _lane((CH, DH))
        rsum = lambda x: jnp.sum(x, axis=-1, keepdims=True)
        items = [(ci, h) for ci in range(cps) for h in range(HEADS)]
        at = lambda ref, it: ref.at[it[0] * CH:(it[0] + 1) * CH, it[1] * DH:(it[1] + 1) * DH]
        ld = lambda ref: [at(ref, it)[...] for it in items]
        bgs = [bg_ref[ci * CH:(ci + 1) * CH, :] for ci in range(cps)]
        qs, ks = ld(q_ref), ld(k_ref)
        vecs = [_head_vectors(bgs[ci], bgt_ref[ci], h) for ci, h in items]
        decs = [_decay(gcol, grow) for _, gcol, grow in vecs]
        ths = [t_ref[ci, h] for ci, h in items]
        drus = [_dot(th, x_, TN, P_BWD) for th, x_ in zip(ths, ld(du_ref))]
        drws = [_dot(th, x_, TN, P_BWD) for th, x_ in zip(ths, ld(dw_ref))]
        kks = [_dot(kh, kh, NT, P_GRAM) for kh in ks]
        da1 = [_dot(dru, x_, NT, P_BWD) for dru, x_ in zip(drus, ld(u_ref))]
        da2 = [_dot(drw, x_, NT, P_BWD) for drw, x_ in zip(drws, ld(w_ref))]
        das = [jnp.where(i > j, -(x_ + y_), 0.0) for x_, y_ in zip(da1, da2)]
        dkks = [da * bcol * dec for da, (bcol, _, _), dec in zip(das, vecs, decs)]
        dps = [dp_ref[ci, h] for ci, h in items]
        dqks = [dp_ * dec for dp_, dec in zip(dps, decs)]
        dq_ps = [_dot(dqk, kh, NN, P_BWD) for dqk, kh in zip(dqks, ks)]
        dk_ps = [_dot(dqk, qh, TN, P_BWD) for dqk, qh in zip(dqks, qs)]
        dk_as = [_dot(dkk, kh, NN, P_BWD) for dkk, kh in zip(dkks, ks)]
        dk_bs = [_dot(dkk, kh, TN, P_BWD) for dkk, kh in zip(dkks, ks)]
        bcols = [vc[0] for vc in vecs]
        gcols = [vc[1] for vc in vecs]
        gams = [jnp.exp(g) for g in gcols]
        glasts = [g[CH - 1:CH, :] for g in gcols]
        es = [jnp.exp(gl - g) for gl, g in zip(glasts, gcols)]
        kgs = [kh * gam for kh, gam in zip(ks, gams)]
        dqgs, dkss = ld(dqg_ref), ld(dks_ref)
        r_uv = [rsum(dru * x_) for dru, x_ in zip(drus, ld(v_ref))]
        r_wk = [rsum(drw * kg) for drw, kg in zip(drws, kgs)]
        r_ak = [rsum(da * kk * dec) for da, kk, dec in zip(das, kks, decs)]
        r_qq = [rsum(dqg * qh) for dqg, qh in zip(dqgs, qs)]
        tks = [rsum(dk_ * kh) * e for dk_, kh, e in zip(dkss, ks, es)]
        mdecs = [da * (bcol * kk * dec) + dp_ * p_ref[ci, h]
                 for (ci, h), da, bcol, kk, dec, dp_ in zip(items, das, bcols, kks, decs, dps)]
        r_md = [rsum(m) for m in mdecs]
        c_md = [rsum(jnp.where(i == j, jnp.sum(m, axis=0, keepdims=True), 0.0)) for m in mdecs]
        dbgs = [jnp.zeros((CH, DH), F32) for _ in range(cps)]
        for n_, (ci, h) in enumerate(items):
            at(dv_ref, (ci, h))[...] = bcols[n_] * drus[n_]
            at(dq_ref, (ci, h))[...] = gams[n_] * dqgs[n_] + dq_ps[n_]
            at(dk_ref, (ci, h))[...] = ((bcols[n_] * gams[n_]) * drws[n_] + dk_ps[n_] + dk_as[n_] + dk_bs[n_]
                                        + dkss[n_] * es[n_])
            dbeta = r_uv[n_] + r_wk[n_] + r_ak[n_]
            dglast = (jnp.sum(tks[n_], axis=0, keepdims=True)
                      + dgam_ref[ci, 0:1, h:h + 1] * jnp.exp(glasts[n_]))
            dgc = (r_wk[n_] * bcols[n_] + r_md[n_] - c_md[n_] + r_qq[n_] * gams[n_] - tks[n_]
                   + jnp.where(rows1 == CH - 1, dglast, 0.0))
            dbgs[ci] = dbgs[ci] + jnp.where(lane == h, dbeta, 0.0) + jnp.where(lane == HEADS + h, dgc, 0.0)
        for ci in range(cps):
            dbg_ref[ci * CH:(ci + 1) * CH, :] = dbgs[ci]

    row = pl.BlockSpec((cps * CH, GW), lambda c: (c, 0))
    sq = pl.BlockSpec((cps, HEADS, CH, CH), lambda c: (c, 0, 0, 0))
    small = pl.BlockSpec((cps * CH, DH), lambda c: (c, 0))
    big = jax.ShapeDtypeStruct((n, GW), F32)
    return pl.pallas_call(
        body, name="gdn_intra_bwd", grid=(nch // cps,),
        in_specs=[row, row, row, small, pl.BlockSpec((cps, DH, CH), lambda c: (c, 0, 0)), sq, row, row, sq,
                  row, sq, row, row, row, pl.BlockSpec((cps, 8, DH), lambda c: (c, 0, 0))],
        out_specs=[row, row, row, small],
        out_shape=[big, big, big, jax.ShapeDtypeStruct((n, DH), F32)],
        compiler_params=_params(("parallel",)),
    )(q, k, v, bg, bgt, t, u, w, p, dqg, dp, du, dw, dks, dgam)


def _local_step(x, tgt, h, w_g, cqw, late, norm_in_w, ad, gdn_norm_w, conv_b, final_norm_w,
                on_grad_c=None, on_grad_g=None):
    proj_g = _matmul(h, w_g, NT, F32, 512, 1408, 1024, "mm_proj_g", n=GW_COLS)
    q, k, v = _prep_qkv(proj_g, cqw)
    bg, bgt = _prep_bg(proj_g, ad)
    u, w, p, t = _gdn_intra(q, k, v, bg, bgt)
    o, vn, s_in = _gdn_scan(q, k, bg, u, w, p)
    w_c, w_out, conv_w = late(o)
    proj_c = _matmul(h, w_c, NT, F32, 512, 1024, 1024, "mm_proj_c", n=CW_COLS)
    mix = _conv_branch(proj_c, conv_w, conv_b, _gdn_out(o, proj_g, gdn_norm_w))
    out = _matmul(mix, w_out, NN, F32, 512, 512, 2048, "mm_out", add=x)
    dout, dout_b, g_fn, loss = _final_loss(out, tgt, final_norm_w)

    dmix = _matmul(dout_b, w_out, NT, F32, 512, 1024, 1024, "mm_dmix")
    g_wout = _matmul(mix, dout_b, TN, BF16, 512, 512, 2048, "mm_gwout")
    do, dproj_g, g_gn = _gdn_out_bwd(o, proj_g, gdn_norm_w, dmix)
    dproj_c, g_cw, g_cb = _conv_branch_bwd(proj_c, conv_w, conv_b, dmix)
    g_c = _matmul(dproj_c, h, TN, BF16, 1024, 512, 2048, "mm_gwin_c")
    if on_grad_c is not None:
        do = on_grad_c(g_c, g_wout, do)
    dqg, dp, du, dw, dks, dgam = _gdn_scan_bwd(q, k, bg, w, p, vn, s_in, do)
    dq, dk, dv, dbg = _gdn_intra_bwd(q, k, v, bg, bgt, t, u, w, p, dqg, dp, du, dw, dks, dgam)
    dproj_g, gq, gk, gv = _prep_qkv_bwd(proj_g, cqw, dq, dk, dv, dproj_g)
    dproj_g, g_al, g_dt = _prep_bg_bwd(proj_g, ad, dbg, dproj_g)
    g_g = _matmul(dproj_g, h, TN, BF16, 1408, 512, 2048, "mm_gwin_g")
    if on_grad_g is not None:
        dproj_g = on_grad_g(g_g, dproj_g)
    dh = _matmul(dproj_g, w_g, NN, F32, 512, 1024, 1408, "mm_dh_g")
    dh = _matmul(dproj_c, w_c, NN, F32, 512, 1024, 1024, "mm_dh_c", add=dh)
    gx, g_nin = _rms_in_bwd(x, norm_in_w, dh, dout)
    small = dict(nin=g_nin, cb=g_cb, fn=g_fn, al=g_al, dt=g_dt, gn=g_gn, cq=(gq, gk, gv), cw=g_cw, loss=loss)
    return gx, small, (g_g, g_c, g_wout)


def _place():
    x, y, c = lax.axis_index("x"), lax.axis_index("y"), lax.axis_index("c")
    chips = [(1 - x, y), (x, 1 - y), (1 - x, 1 - y)]
    return x, y, c, chips


def _blk(ref, b):
    if isinstance(b, int):
        return ref.at[b * DH:(b + 1) * DH, :]
    return ref.at[pl.ds(pl.multiple_of(b * DH, DH), DH), :]


HBM = pl.BlockSpec(memory_space=pltpu.HBM)
SEM = pl.BlockSpec(memory_space=pltpu.SEMAPHORE)
EFFECT = pltpu.SideEffectType.DATAFLOW_SIDE_EFFECTING


def _split_start(name, issue, bufs, n_sems):
    nbuf = len(bufs)

    def body(*refs):
        issue(refs[:nbuf], refs[nbuf], refs[nbuf + 1])
        refs[-1][...] = jnp.zeros_like(refs[-1])

    out = pl.pallas_call(
        body, name=name,
        out_shape=(pltpu.SemaphoreType.DMA((n_sems,)), pltpu.SemaphoreType.DMA((n_sems,)),
                   *[pltpu.HBM(b.shape, b.dtype) for b in bufs], jax.ShapeDtypeStruct((8, DH), F32)),
        in_specs=[HBM] * nbuf,
        out_specs=(SEM, SEM, *[HBM] * nbuf, pl.BlockSpec(memory_space=pltpu.VMEM)),
        input_output_aliases={a: 2 + a for a in range(nbuf)},
        compiler_params=pltpu.CompilerParams(has_side_effects=EFFECT),
    )(*[pltpu.with_memory_space_constraint(b, pltpu.HBM) for b in bufs])
    return out[0], out[1], list(out[2:2 + nbuf]), out[-1]


def _split_wait(name, await_, send_sems, recv_sems, bufs, after):
    nbuf = len(bufs)

    def body(*refs):
        await_(refs[:nbuf], refs[nbuf], refs[nbuf + 1])

    out = pl.pallas_call(
        body, name=name,
        out_shape=tuple(pltpu.HBM(b.shape, b.dtype) for b in bufs),
        in_specs=[HBM] * nbuf + [SEM, SEM, ANY], out_specs=tuple([HBM] * nbuf),
        input_output_aliases={a: a for a in range(nbuf)},
        compiler_params=pltpu.CompilerParams(has_side_effects=EFFECT),
    )(*bufs, send_sems, recv_sems, after)
    return list(out)


def _phase_blocks(chip, phase, edges, parity=None):
    return [(b, blk) for b, (grp, blk) in enumerate(_shard_blocks(chip, edges))
            if grp == phase and (parity is None or b % 2 == parity)]


def _cols(ref, nblk):
    return ref.at[0:nblk * DH, :]


def _block_table(chip, edges, spare_g, spare_c):
    rows = []
    for s in range(4):
        sb = _shard_blocks(s, edges)
        rows.append([[blk if grp == "g" else spare_g for grp, blk in sb],
                     [blk if grp == "c" else spare_c for grp, blk in sb],
                     [int(grp == "g") for grp, _ in sb], [s] * ALIGNED_BLOCKS])
    return jnp.asarray(rows, jnp.int32)[chip]


def _place_own(a_shard, wo, cq, cw, bufs):
    d = a_shard.shape[1]
    chip = 2 * lax.axis_index("x") + lax.axis_index("y")

    def body(t_ref, a_ref, wo_ref, cq_ref, cw_ref, *refs):
        wg_ref, wc_ref, wog_ref, cqg_ref, cwg_ref = refs[5:]
        wg_ref[...] = a_ref[...]
        wc_ref[...] = a_ref[...]

        @pl.when(pl.program_id(0) == 0)
        def _():
            wog_ref[0] = wo_ref[...]
            cqg_ref[0] = cq_ref[...]
            cwg_ref[0] = cw_ref[...]

    whole = lambda s: pl.BlockSpec(s.shape, lambda b, t: (0,) * s.ndim)
    slot = lambda s: pl.BlockSpec((1,) + s.shape, lambda b, t: (t[3, 0],) + (0,) * s.ndim)
    return pl.pallas_call(
        body, name="place_own",
        grid_spec=pltpu.PrefetchScalarGridSpec(
            num_scalar_prefetch=1, grid=(ALIGNED_BLOCKS,),
            in_specs=[pl.BlockSpec((DH, d), lambda b, t: (b, 0)), whole(wo), whole(cq), whole(cw)] + [ANY] * 5,
            out_specs=[pl.BlockSpec((DH, d), lambda b, t: (t[0, b], 0)),
                       pl.BlockSpec((DH, d), lambda b, t: (t[1, b], 0)), slot(wo), slot(cq), slot(cw)]),
        out_shape=[jax.ShapeDtypeStruct(b.shape, b.dtype) for b in bufs],
        input_output_aliases={5 + a: a for a in range(5)},
        compiler_params=_params(("arbitrary",)),
    )(_block_table(chip, True, G_SPARE, C_SPARE), a_shard, wo, cq, cw, *bufs)


def _tie(x, token, name):
    def body(x_ref, t_ref, o_ref):
        del x_ref, t_ref, o_ref

    return pl.pallas_call(
        body, name=name, in_specs=[ANY, ANY], out_specs=ANY,
        out_shape=jax.ShapeDtypeStruct(x.shape, x.dtype), input_output_aliases={0: 0},
    )(x, token)


def _gather_start(phase, a_shard, w_grp, singles):
    ns = len(singles)

    def issue(refs, send_sems, recv_sems):
        a_ref, w_ref = refs[0], refs[1]
        x, y, c, chips = _place()
        mine = 2 * x + y
        for jj, (px, py) in enumerate(chips):
            to = dict(device_id=(px, py, c), device_id_type=MESH)
            for a in range(ns):
                pltpu.make_async_remote_copy(
                    src_ref=refs[2 + 2 * a], dst_ref=refs[3 + 2 * a].at[mine],
                    send_sem=send_sems.at[(1 + ns) * jj + 1 + a], recv_sem=recv_sems.at[(1 + ns) * jj + 1 + a],
                    **to).start()
        for s in range(4):
            for par in range(2):
                blocks = _phase_blocks(s, phase, True, par)
                if blocks:
                    @pl.when((mine == s) & (c == par))
                    def _():
                        for jj, (px, py) in enumerate(chips):
                            for b, blk in blocks:
                                pltpu.make_async_remote_copy(
                                    src_ref=_blk(a_ref, b), dst_ref=_blk(w_ref, blk),
                                    send_sem=send_sems.at[(1 + ns) * jj], recv_sem=recv_sems.at[(1 + ns) * jj],
                                    device_id=(px, py, c), device_id_type=MESH).start()

    bufs = [a_shard, w_grp] + [t for pair in singles for t in pair]
    return _split_start("gather_start_" + phase, issue, bufs, 3 * (1 + ns))


def _gather_wait(phase, send_sems, recv_sems, bufs, after):
    ns = (len(bufs) - 2) // 2

    def await_(refs, send_sems, recv_sems):
        a_ref, w_ref = refs[0], refs[1]
        x, y, c, chips = _place()
        mine = 2 * x + y
        for jj, (px, py) in enumerate(chips):
            to = dict(device_id=(px, py, c), device_id_type=MESH)
            peer = 2 * px + py
            for a in range(ns):
                cp = pltpu.make_async_remote_copy(
                    src_ref=refs[2 + 2 * a], dst_ref=refs[3 + 2 * a].at[mine],
                    send_sem=send_sems.at[(1 + ns) * jj + 1 + a], recv_sem=recv_sems.at[(1 + ns) * jj + 1 + a], **to)
                cp.wait_recv()
                cp.wait_send()
            for s in range(4):
                for par in range(2):
                    nblk = len(_phase_blocks(s, phase, True, par))
                    if nblk:
                        both = pltpu.make_async_remote_copy(
                            src_ref=_cols(a_ref, nblk), dst_ref=_cols(w_ref, nblk),
                            send_sem=send_sems.at[(1 + ns) * jj], recv_sem=recv_sems.at[(1 + ns) * jj], **to)

                        @pl.when((peer == s) & (c == par))
                        def _():
                            both.wait_recv()

                        @pl.when((mine == s) & (c == par))
                        def _():
                            both.wait_send()

    return _split_wait("gather_wait_" + phase, await_, send_sems, recv_sems, bufs, after)


def _sibling_forward(phase, w_grp):
    def body(w_in_ref, w_ref, send_sems, recv_sems):
        del w_in_ref
        x, y, c, chips = _place()
        to = dict(device_id=(x, y, 1 - c), device_id_type=MESH)
        for jj, (px, py) in enumerate(chips):
            peer = 2 * px + py
            for s in range(4):
                for par in range(2):
                    mine_blocks = _phase_blocks(s, phase, True, par)
                    theirs = len(_phase_blocks(s, phase, True, 1 - par))
                    if not (mine_blocks or theirs):
                        continue

                    @pl.when((peer == s) & (c == par))
                    def _():
                        for _, blk in mine_blocks:
                            pltpu.make_async_remote_copy(
                                src_ref=_blk(w_ref, blk), dst_ref=_blk(w_ref, blk),
                                send_sem=send_sems.at[jj], recv_sem=recv_sems.at[jj], **to).start()
                        if theirs:
                            pltpu.make_async_remote_copy(
                                src_ref=_cols(w_ref, theirs), dst_ref=_cols(w_ref, theirs),
                                send_sem=send_sems.at[jj], recv_sem=recv_sems.at[jj], **to).wait_recv()
                        if mine_blocks:
                            pltpu.make_async_remote_copy(
                                src_ref=_cols(w_ref, len(mine_blocks)), dst_ref=_cols(w_ref, len(mine_blocks)),
                                send_sem=send_sems.at[jj], recv_sem=recv_sems.at[jj], **to).wait_send()

    return pl.pallas_call(
        body, name="sibling_forward_" + phase, in_specs=[ANY], out_specs=ANY,
        out_shape=jax.ShapeDtypeStruct(w_grp.shape, w_grp.dtype), input_output_aliases={0: 0},
        scratch_shapes=[pltpu.SemaphoreType.DMA((3,)), pltpu.SemaphoreType.DMA((3,))],
    )(w_grp)


def _merge_edges(w, edge0, mixed, name):
    d = w.shape[1]

    def body(e_ref, o_ref):
        o_ref[...] = e_ref[0:DH, :] + e_ref[DH:2 * DH, :]

    def to_block(i):
        r = mixed[-1]
        for kk in range(len(mixed) - 2, -1, -1):
            r = jnp.where(i == kk, mixed[kk], r)
        return r

    return pl.pallas_call(
        body, name=name, grid=(len(mixed),),
        in_specs=[pl.BlockSpec((2 * DH, d), lambda i: (edge0 // 2 + i, 0))],
        out_specs=pl.BlockSpec((DH, d), lambda i: (to_block(i), 0)),
        out_shape=jax.ShapeDtypeStruct(w.shape, w.dtype),
        input_output_aliases={0: 0},
        compiler_params=_params(("arbitrary",)),
    )(w)


def _scatter_start(phase, g_grp, land, singles, halved=False):
    ns = len(singles)

    def issue(refs, send_sems, recv_sems):
        g_ref, land_ref = refs[0], refs[1]
        x, y, c, chips = _place()
        for jj, (px, py) in enumerate(chips):
            to = dict(device_id=(px, py, c), device_id_type=MESH)
            peer = 2 * px + py
            for a in range(ns):
                pltpu.make_async_remote_copy(
                    src_ref=refs[2 + 2 * a].at[peer], dst_ref=refs[3 + 2 * a].at[jj],
                    send_sem=send_sems.at[(1 + ns) * jj + 1 + a], recv_sem=recv_sems.at[(1 + ns) * jj + 1 + a],
                    **to).start()
            for s in range(4):
                for par in ((0, 1) if halved else (None,)):
                    blocks = _phase_blocks(s, phase, False, par)
                    if blocks:
                        @pl.when((peer == s) if par is None else ((peer == s) & (c == par)))
                        def _():
                            for b, blk in blocks:
                                pltpu.make_async_remote_copy(
                                    src_ref=_blk(g_ref, blk), dst_ref=_blk(land_ref.at[jj], b),
                                    send_sem=send_sems.at[(1 + ns) * jj], recv_sem=recv_sems.at[(1 + ns) * jj],
                                    **to).start()

    bufs = [g_grp, land] + [t for pair in singles for t in pair]
    return _split_start("scatter_start_" + phase, issue, bufs, 3 * (1 + ns))


def _scatter_wait(phase, send_sems, recv_sems, bufs, after, halved=False):
    ns = (len(bufs) - 2) // 2

    def await_(refs, send_sems, recv_sems):
        g_ref, land_ref = refs[0], refs[1]
        x, y, c, chips = _place()
        mine = 2 * x + y
        for jj, (px, py) in enumerate(chips):
            to = dict(device_id=(px, py, c), device_id_type=MESH)
            peer = 2 * px + py
            for a in range(ns):
                cp = pltpu.make_async_remote_copy(
                    src_ref=refs[2 + 2 * a].at[peer], dst_ref=refs[3 + 2 * a].at[jj],
                    send_sem=send_sems.at[(1 + ns) * jj + 1 + a], recv_sem=recv_sems.at[(1 + ns) * jj + 1 + a], **to)
                cp.wait_recv()
                cp.wait_send()
            for s in range(4):
                for par in ((0, 1) if halved else (None,)):
                    nblk = len(_phase_blocks(s, phase, False, par))
                    if nblk:
                        both = pltpu.make_async_remote_copy(
                            src_ref=_cols(g_ref, nblk), dst_ref=_cols(land_ref.at[jj], nblk),
                            send_sem=send_sems.at[(1 + ns) * jj], recv_sem=recv_sems.at[(1 + ns) * jj], **to)

                        @pl.when((mine == s) if par is None else ((mine == s) & (c == par)))
                        def _():
                            both.wait_recv()

                        @pl.when((peer == s) if par is None else ((peer == s) & (c == par)))
                        def _():
                            both.wait_send()

    return _split_wait("scatter_wait_" + phase, await_, send_sems, recv_sems, bufs, after)


def _needed_blocks(phase, parity):
    return sorted({blk for s in range(4) for _, blk in _phase_blocks(s, phase, False, parity)})


def _pair_reduce(phase, g_grp):
    n, d = g_grp.shape

    def swap(g_ref, sib_ref, send_sem, recv_sem):
        x, y, c, _ = _place()
        to = dict(device_id=(x, y, 1 - c), device_id_type=MESH)
        for par in range(2):
            give, get = _needed_blocks(phase, 1 - par), _needed_blocks(phase, par)

            @pl.when(c == par)
            def _():
                for blk in give:
                    pltpu.make_async_remote_copy(src_ref=_blk(g_ref, blk), dst_ref=_blk(sib_ref, blk),
                                                 send_sem=send_sem, recv_sem=recv_sem, **to).start()
                pltpu.make_async_remote_copy(src_ref=_cols(g_ref, len(get)), dst_ref=_cols(sib_ref, len(get)),
                                             send_sem=send_sem, recv_sem=recv_sem, **to).wait_recv()
                pltpu.make_async_remote_copy(src_ref=_cols(g_ref, len(give)), dst_ref=_cols(sib_ref, len(give)),
                                             send_sem=send_sem, recv_sem=recv_sem, **to).wait_send()

    sib = pl.pallas_call(
        swap, name="pair_swap_" + phase, in_specs=[ANY], out_specs=ANY,
        out_shape=jax.ShapeDtypeStruct((n, d), g_grp.dtype),
        scratch_shapes=[pltpu.SemaphoreType.DMA, pltpu.SemaphoreType.DMA],
    )(g_grp)

    lists = [_needed_blocks(phase, par) for par in range(2)]
    longest = max(len(t) for t in lists)
    table = jnp.asarray([t + [t[-1]] * (longest - len(t)) for t in lists], jnp.int32)[lax.axis_index("c")]

    def add(t_ref, a_ref, b_ref, o_ref):
        o_ref[...] = (a_ref[...].astype(F32) + b_ref[...].astype(F32)).astype(o_ref.dtype)

    blk = pl.BlockSpec((DH, d), lambda i, t: (t[i], 0))
    return pl.pallas_call(
        add, name="pair_add_" + phase,
        grid_spec=pltpu.PrefetchScalarGridSpec(num_scalar_prefetch=1, grid=(longest,),
                                               in_specs=[blk, blk], out_specs=blk),
        out_shape=jax.ShapeDtypeStruct((n, d), g_grp.dtype),
        compiler_params=_params(("arbitrary",)),
    )(table, g_grp, sib)


def _sum_shard(g_g, g_c, land):
    d = g_g.shape[1]
    chip = 2 * lax.axis_index("x") + lax.axis_index("y")

    def body(t_ref, gg_ref, gc_ref, land_ref, o_ref):
        b = pl.program_id(0)
        in_g = t_ref[2, b] == 1
        own = jnp.where(in_g, gg_ref[...].astype(F32), gc_ref[...].astype(F32))
        for jj in range(3):
            own = own + land_ref[jj].astype(F32)
        o_ref[...] = jnp.where(in_g & (b % 2 != lax.axis_index("c")), 0.0, own)

    return pl.pallas_call(
        body, name="sum_w_in",
        grid_spec=pltpu.PrefetchScalarGridSpec(
            num_scalar_prefetch=1, grid=(ALIGNED_BLOCKS,),
            in_specs=[pl.BlockSpec((DH, d), lambda b, t: (t[0, b], 0)), pl.BlockSpec((DH, d), lambda b, t: (t[1, b], 0)),
                      pl.BlockSpec((3, DH, d), lambda b, t: (0, b, 0))],
            out_specs=pl.BlockSpec((DH, d), lambda b, t: (b, 0))),
        out_shape=jax.ShapeDtypeStruct((ALIGNED_W, d), F32),
        compiler_params=_params(("arbitrary",)),
    )(_block_table(chip, False, 0, 0), g_g, g_c, land)


def _sum_rows(stack, land, rows):
    _, r, d = stack.shape
    rows = min(rows, r)
    chip = 2 * lax.axis_index("x") + lax.axis_index("y")

    def body(t_ref, own_ref, land_ref, o_ref):
        acc = own_ref[0].astype(F32)
        for jj in range(3):
            acc = acc + land_ref[jj].astype(F32)
        o_ref[...] = acc

    return pl.pallas_call(
        body, name="sum_w_out",
        grid_spec=pltpu.PrefetchScalarGridSpec(
            num_scalar_prefetch=1, grid=(r // rows,),
            in_specs=[pl.BlockSpec((1, rows, d), lambda i, t: (t[0], i, 0)),
                      pl.BlockSpec((3, rows, d), lambda i, t: (0, i, 0))],
            out_specs=pl.BlockSpec((rows, d), lambda i, t: (i, 0))),
        out_shape=jax.ShapeDtypeStruct((r, d), F32),
        compiler_params=_params(("arbitrary",)),
    )(jnp.reshape(chip, (1,)).astype(jnp.int32), stack, land)


def _final_exchange(parts, pack):
    npart = len(parts)

    def body(*refs):
        ins, pack_ref = refs[:npart], refs[npart]
        outs, packs = refs[npart + 1:2 * npart + 1], refs[2 * npart + 1]
        send_sems, recv_sems, psend, precv, loc_sem = refs[2 * npart + 2:]
        x, y, c, _ = _place()
        me = 4 * x + 2 * y + c
        local = pltpu.make_async_copy(pack_ref, packs.at[me], loc_sem)
        local.start()
        cps = [pltpu.make_async_remote_copy(
            src_ref=ins[a], dst_ref=outs[a], send_sem=send_sems.at[a], recv_sem=recv_sems.at[a],
            device_id=(x, y, 1 - c), device_id_type=MESH) for a in range(npart)]
        for r in range(1, 8):
            dx, dy, dc = (r >> 2) & 1, (r >> 1) & 1, r & 1
            peer = (x + dx - 2 * x * dx, y + dy - 2 * y * dy, c + dc - 2 * c * dc)
            cps.append(pltpu.make_async_remote_copy(
                src_ref=pack_ref, dst_ref=packs.at[me], send_sem=psend.at[r - 1], recv_sem=precv.at[r - 1],
                device_id=peer, device_id_type=MESH))
        for cp in cps:
            cp.start()
        for cp in cps:
            cp.wait_recv()
        for cp in cps:
            cp.wait_send()
        local.wait()

    return pl.pallas_call(
        body, name="final_exchange",
        in_specs=[ANY] * (npart + 1), out_specs=[ANY] * (npart + 1),
        out_shape=[jax.ShapeDtypeStruct(p.shape, p.dtype) for p in parts]
        + [jax.ShapeDtypeStruct((8,) + pack.shape, pack.dtype)],
        scratch_shapes=[pltpu.SemaphoreType.DMA((npart,)), pltpu.SemaphoreType.DMA((npart,)),
                        pltpu.SemaphoreType.DMA((7,)), pltpu.SemaphoreType.DMA((7,)), pltpu.SemaphoreType.DMA],
    )(*parts, pack)


def _sum_packs(packs):
    def body(p_ref, o_ref):
        acc = p_ref[0]
        for d in range(1, 8):
            acc = acc + p_ref[d]
        o_ref[...] = acc

    return pl.pallas_call(
        body, name="sum_packs", out_shape=jax.ShapeDtypeStruct(packs.shape[1:], F32),
    )(packs)


def _adamw_update(g, w_ref, m_ref, v_ref, go, do, mo, vo):
    c1 = 1.0 / (1.0 - ADAM_B1 ** ADAM_STEP)
    c2 = 1.0 / (1.0 - ADAM_B2 ** ADAM_STEP)
    mn = ADAM_B1 * m_ref[...] + (1.0 - ADAM_B1) * g
    vn = ADAM_B2 * v_ref[...] + (1.0 - ADAM_B2) * (g * g)
    go[...] = g
    mo[...] = mn
    vo[...] = vn
    do[...] = -ADAM_LR * ((mn * c1) / (jnp.sqrt(vn * c2) + ADAM_EPS) + ADAM_WD * w_ref[...])


def _adamw(w, m, v, g1, g2, rows, name):
    r, cdim = w.shape
    rows = min(rows, r)

    def body(*refs):
        n_in = 4 if g2 is None else 5
        w_ref, m_ref, v_ref, g_ref = refs[:4]
        g = g_ref[...] if g2 is None else g_ref[...] + refs[4][...]
        _adamw_update(g, w_ref, m_ref, v_ref, *refs[n_in:n_in + 4])

    blk = pl.BlockSpec((rows, cdim), lambda i: (i, 0))
    args = [w, m, v, g1] + ([] if g2 is None else [g2])
    shp = jax.ShapeDtypeStruct((r, cdim), F32)
    return pl.pallas_call(
        body, name=name, grid=(r // rows,),
        in_specs=[blk] * len(args), out_specs=[blk] * 4, out_shape=[shp] * 4,
        compiler_params=_params(("parallel",), 20 * rows * cdim * 4 + 8 * 2**20),
    )(*args)


def _adamw_shard(wt, mt, vt, g1, g2):
    r, d = wt.shape
    cols = min(128, d)

    def body(w_ref, m_ref, v_ref, g_ref, g2_ref, go, do, mo, vo, pad_ref):
        chip = 2 * lax.axis_index("x") + lax.axis_index("y")
        back = [(ALIGNED_W - s) % ALIGNED_W for s in SHIFTS]
        pad_ref[...] = pltpu.roll(g_ref[...] + g2_ref[...], _by_chip(chip, back), 0)
        _adamw_update(pad_ref[0:r, :], w_ref, m_ref, v_ref, go, do, mo, vo)

    blk = pl.BlockSpec((r, cols), lambda i: (0, i))
    gblk = pl.BlockSpec((ALIGNED_W, cols), lambda i: (0, i))
    shp = jax.ShapeDtypeStruct((r, d), F32)
    return pl.pallas_call(
        body, name="adamw_w_in", grid=(d // cols,),
        in_specs=[blk] * 3 + [gblk] * 2, out_specs=[blk] * 4, out_shape=[shp] * 4,
        scratch_shapes=[pltpu.VMEM((ALIGNED_W, cols), F32)],
        compiler_params=_params(("parallel",), 24 * ALIGNED_W * cols * 4 + 8 * 2**20),
    )(wt, mt, vt, g1, g2)


def _pad_lanes(a, width):
    return jnp.pad(a, ((0, 0), (0, width - a.shape[1])))


def _gathered_to_full(g):
    return jnp.transpose(g, (1, 0, 2)).reshape(g.shape[1], 4 * g.shape[2])


def _row(a):
    return _pad_lanes(a.reshape(1, -1), 1024)


def _small_pack(nin, cb, fn, al, dt, gn, cqw_shard, cw_shard):
    ad = jnp.concatenate([al.reshape(1, -1), dt.reshape(1, -1)], axis=1)
    rows = [_row(nin), _row(cb), _row(fn), _row(ad), _row(gn), cqw_shard.reshape(3, 1024), _row(cw_shard)]
    out = jnp.concatenate(rows, axis=0)
    return jnp.pad(out, ((0, 16 - out.shape[0]), (0, 0)))


def kernel(x, norm_in_w, w_in, conv_qkv_w, A_log, dt_bias, gdn_norm_w, conv_w, conv_b, w_out, final_norm_w, loss_target, m_norm_in_w, m_w_in, m_conv_qkv_w, m_A_log, m_dt_bias, m_gdn_norm_w, m_conv_w, m_conv_b, m_w_out, m_final_norm_w, v_norm_in_w, v_w_in, v_conv_qkv_w, v_A_log, v_dt_bias, v_gdn_norm_w, v_conv_w, v_conv_b, v_w_out, v_final_norm_w):
    chip = 2 * lax.axis_index("x") + lax.axis_index("y")
    a_shard = _align_shard(jnp.transpose(w_in[0]))
    wo_b = _cast_bf16(w_out[0], 256, "cast_w_out")
    d_model = x.shape[-1]
    stack = lambda s: lax.empty((4,) + s.shape, s.dtype)
    wg0 = lax.empty((WG_BLOCKS * DH, d_model), BF16)
    wc0 = lax.empty((WC_BLOCKS * DH, d_model), BF16)
    ss_g, rs_g, bufs_g, tok_g = _gather_start("g", a_shard, wg0, [(conv_qkv_w[0], stack(conv_qkv_w[0]))])
    ss_c, rs_c, bufs_c, tok_c = _gather_start("c", bufs_g[0], wc0,
                                              [(conv_w[0], stack(conv_w[0])), (wo_b, stack(wo_b))])
    wg1, wc1, wog1, cqg1, cwg1 = _place_own(bufs_c[0], bufs_c[4], bufs_g[2], bufs_c[2],
                                            [bufs_g[1], bufs_c[1], bufs_c[5], bufs_g[3], bufs_c[3]])
    x0 = x[0]
    h = _tie(_tie(_rms_in(x0, norm_in_w), tok_g, "after_gather_start_g"), tok_c, "after_gather_start_c")
    a_thru, wg, _, cq_g = _gather_wait("g", ss_g, rs_g, [bufs_c[0], wg1, bufs_g[2], cqg1], h)
    w_g = _merge_edges(_sibling_forward("g", wg), G_EDGE, G_MIXED, "merge_edges_g")
    cqw = _gathered_to_full(cq_g)
    ad = jnp.pad(jnp.concatenate([A_log, dt_bias], axis=0), ((0, 0), (A_LANE, 0)))

    def late(o):
        _, wc, _, cw_g, _, wo_g = _gather_wait("c", ss_c, rs_c,
                                               [a_thru, wc1, bufs_c[2], cwg1, bufs_c[4], wog1], o)
        return (_merge_edges(_sibling_forward("c", wc), C_EDGE, C_MIXED, "merge_edges_c"),
                wo_g.reshape(2 * GW, d_model),
                _gathered_to_full(cw_g))

    scat = {}

    def on_grad_c(g_c, g_wout, do):
        go4 = g_wout.reshape(4, GW // 2, d_model)
        land = lax.empty((3, ALIGNED_W, d_model), BF16)
        land_o = lax.empty((3, GW // 2, d_model), BF16)
        ss, rs, bufs, tok = _scatter_start("c", g_c, land, [(go4, land_o)])
        scat["c"] = (ss, rs, bufs)
        return _tie(do, tok, "after_scatter_start_c")

    def on_grad_g(g_g, dproj_g):
        ss, rs, bufs, tok = _scatter_start("g", _pair_reduce("g", g_g), scat["c"][2][1], [], halved=True)
        scat["g"] = (ss, rs, bufs)
        return _tie(dproj_g, tok, "after_scatter_start_g")

    gx, sm, _ = _local_step(x0, loss_target[0], h, w_g, cqw, late, norm_in_w, ad, gdn_norm_w, conv_b,
                            final_norm_w.reshape(1, -1), on_grad_c, on_grad_g)

    ss, rs, bufs = scat["g"]
    g_g, land = _scatter_wait("g", ss, rs, bufs, gx, halved=True)
    ss, rs, bufs = scat["c"]
    g_c, land, go4, land_o = _scatter_wait("c", ss, rs, [bufs[0], land, bufs[2], bufs[3]], gx)
    part_in = _sum_shard(g_g, g_c, land)
    part_out = _sum_rows(go4, land_o, 128)
    ad_g = jnp.concatenate([sm["al"][:, A_LANE:], sm["dt"][:, A_LANE:]], axis=1)
    pack = jnp.concatenate([_row(sm["nin"]), _row(sm["cb"]), _row(sm["fn"]), _row(ad_g), _row(sm["gn"]),
                            jnp.concatenate(sm["cq"], axis=1).reshape(12, 1024), sm["cw"], _row(sm["loss"])], axis=0)
    pack = jnp.pad(pack, ((0, PACK_ROWS - pack.shape[0]), (0, 0)))
    sib_in, sib_out, packs = _final_exchange([part_in, part_out], pack)
    tot = _sum_packs(packs)

    g_wi, d_wi, m_wi, v_wi = [jnp.transpose(a) for a in _adamw_shard(
        jnp.transpose(w_in[0]), jnp.transpose(m_w_in[0]), jnp.transpose(v_w_in[0]), part_in, sib_in)]
    g_wo, d_wo, m_wo, v_wo = _adamw(w_out[0], m_w_out[0], v_w_out[0], part_out, sib_out, 128, "adamw_w_out")
    g_cq_sh = lax.dynamic_slice_in_dim(tot[R_CQ:R_CQ + 12].reshape(4, 3 * GW), chip * 768, 768, axis=1)
    g_cw_sh = lax.dynamic_slice_in_dim(tot[R_CW:R_CW + 3], chip * 256, 256, axis=1)
    sp = lambda nin, cb, fn, al, dt, gn, cq, cwv: _small_pack(nin, cb, fn, al, dt, gn, cq[0], cwv[0])
    g_s = _small_pack(tot[R_NIN], tot[R_CB], tot[R_FN], tot[R_AD, :HEADS], tot[R_AD, HEADS:2 * HEADS],
                      tot[R_GN, :DH], g_cq_sh, g_cw_sh)
    w_s = sp(norm_in_w, conv_b, final_norm_w, A_log, dt_bias, gdn_norm_w, conv_qkv_w, conv_w)
    m_s = sp(m_norm_in_w, m_conv_b, m_final_norm_w, m_A_log, m_dt_bias, m_gdn_norm_w, m_conv_qkv_w, m_conv_w)
    v_s = sp(v_norm_in_w, v_conv_b, v_final_norm_w, v_A_log, v_dt_bias, v_gdn_norm_w, v_conv_qkv_w, v_conv_w)
    small = _adamw(w_s, m_s, v_s, g_s, None, 16, "adamw_small")

    def unpack(a, big_in, big_out):
        return (a[0:1], big_in[None], a[5:8].reshape(1, 4, 768), a[3:4, :HEADS], a[3:4, HEADS:2 * HEADS],
                a[4:5, :DH], a[8, :768].reshape(1, 3, 256), a[1:2], big_out[None], a[2])

    loss = tot[R_LOSS, 0]
    return (loss, gx[None], *unpack(small[0], g_wi, g_wo), *unpack(small[1], d_wi, d_wo),
            *unpack(small[2], m_wi, m_wo), *unpack(small[3], v_wi, v_wo))
```

```python
import functools
import math

import jax
import jax.numpy as jnp
from jax import lax
from jax.experimental import pallas as pl
from jax.experimental.pallas import tpu as pltpu

F32 = jnp.float32
BF16 = jnp.bfloat16
MESH = pl.DeviceIdType.MESH
ANY = pl.BlockSpec(memory_space=pl.ANY)

HEADS = 8
DH = 128
CH = 64
GW = HEADS * DH
EPS = 1e-6
VMEM_V7X = 64 * 1024 * 1024

QB, KB, VB, ZB, BAB = 0, 8, 16, 24, 32
A_LANE = 120
NG, NC = 33, 32
GW_COLS, CW_COLS = NG * DH, NC * DH

SHARD_W = 2052
ALIGNED_BLOCKS = 17
ALIGNED_W = ALIGNED_BLOCKS * DH
SHIFTS = (0, 4, ALIGNED_W - 8, ALIGNED_W - 4)
G_EDGE, C_EDGE = 34, 32
G_SPARE, C_SPARE = 33, 34
WG_BLOCKS, WC_BLOCKS = 38, 36
G_MIXED, C_MIXED = (2, BAB), (4 * 7 + 1,)


def _shard_blocks(chip, edges):
    g, c = "g", "c"
    if chip == 0:
        out = [(g, 3 * b) for b in range(8)] + [(g, 3 * b + 1) for b in range(8)] + [(g, G_EDGE, G_MIXED[0])]
    elif chip == 1:
        out = [(g, G_EDGE + 1, G_MIXED[0])] + [(g, 3 * b + 2) for b in range(1, 8)]
        out += [(g, ZB + b) for b in range(8)] + [(g, G_EDGE + 2, G_MIXED[1])]
    elif chip == 2:
        out = [(c, 4 * b) for b in range(8)] + [(c, 4 * b + 1) for b in range(7)]
        out += [(c, C_EDGE, C_MIXED[0]), (g, G_EDGE + 3, G_MIXED[1])]
    else:
        out = [(c, 4 * b + 2) for b in range(8)] + [(c, 4 * b + 3) for b in range(8)] + [(c, C_EDGE + 1, C_MIXED[0])]
    return [(o[0], o[1] if (edges or len(o) == 2) else o[2]) for o in out]


def _by_chip(chip, vals):
    if all(v == vals[0] for v in vals):
        return vals[0]
    r = vals[3]
    for kk in (2, 1, 0):
        r = jnp.where(chip == kk, vals[kk], r)
    return r

ADAM_LR, ADAM_B1, ADAM_B2, ADAM_EPS, ADAM_WD, ADAM_STEP = 0.001, 0.9, 0.999, 1e-08, 0.01, 10

R_NIN, R_CB, R_FN, R_AD, R_GN, R_CQ, R_CW, R_LOSS, PACK_ROWS = 0, 1, 2, 3, 4, 5, 17, 20, 24

NN = ((1,), (0,))
NT = ((1,), (1,))
TN = ((0,), (0,))


def _dot(a, b, dims=NN, mode="lo"):
    dn = (dims, ((), ()))
    if mode == "hi":
        return lax.dot_general(a, b, dn, precision=lax.Precision.HIGHEST, preferred_element_type=F32)
    ah, bh = a.astype(BF16), b.astype(BF16)
    out = lax.dot_general(ah, bh, dn, preferred_element_type=F32)
    if mode == "x3":
        al = (a - ah.astype(F32)).astype(BF16)
        bl = (b - bh.astype(F32)).astype(BF16)
        out = out + lax.dot_general(ah, bl, dn, preferred_element_type=F32)
        out = out + lax.dot_general(al, bh, dn, preferred_element_type=F32)
    return out


P_GRAM, P_INV, P_SOL, P_SCAN, P_SCANB, P_BWD = "lo", "lo", "lo", "lo", "lo", "lo"
P_CUM = "x3"


def _params(sem=None, vmem=None):
    kw = {}
    if sem is not None:
        kw["dimension_semantics"] = sem
    if vmem is not None:
        kw["vmem_limit_bytes"] = int(min(max(vmem, 32 * 2**20), VMEM_V7X - 8 * 2**20))
    return pltpu.CompilerParams(**kw)


def _sigmoid(x):
    return 1.0 / (1.0 + jnp.exp(-x))


def _dsilu(x, s):
    return s * (1.0 + x * (1.0 - s))


def _rows(shape):
    return lax.broadcasted_iota(jnp.int32, shape, 0)


def _shift_down(x, s):
    if s == 0:
        return x
    return jnp.where(_rows(x.shape) >= s, pltpu.roll(x, s, 0), 0.0)


def _shift_up(x, s):
    if s == 0:
        return x
    n = x.shape[0]
    return jnp.where(_rows(x.shape) < n - s, pltpu.roll(x, n - s, 0), 0.0)


def _matmul(a, b, dims, out_dtype, tm, tn, tk, name, add=None, n=None):
    if dims == NN:
        (m, k), n = a.shape, b.shape[1]
    elif dims == NT:
        (m, k), n = a.shape, (n or b.shape[0])
    else:
        (k, m), n = a.shape, b.shape[1]
    tm, tn, tk = min(tm, m), min(tn, n), min(tk, k)
    assert m % tm == 0 and n % tn == 0 and k % tk == 0, (name, m, n, k, tm, tn, tk)
    nk = k // tk

    def body(*refs):
        if add is None:
            a_ref, b_ref, o_ref = refs[:3]
            add_ref = None
        else:
            a_ref, b_ref, add_ref, o_ref = refs[:4]
        part = _dot(a_ref[...], b_ref[...], dims)
        if nk == 1:
            if add_ref is not None:
                part = part + add_ref[...]
            o_ref[...] = part.astype(out_dtype)
            return
        acc = refs[-1]
        kk = pl.program_id(2)

        @pl.when(kk == 0)
        def _():
            acc[...] = part

        @pl.when(kk > 0)
        def _():
            acc[...] += part

        @pl.when(kk == nk - 1)
        def _():
            r = acc[...]
            if add_ref is not None:
                r = r + add_ref[...]
            o_ref[...] = r.astype(out_dtype)

    if dims == TN:
        a_spec = pl.BlockSpec((tk, tm), lambda i, j, kk: (kk, i))
    else:
        a_spec = pl.BlockSpec((tm, tk), lambda i, j, kk: (i, kk))
    if dims == NT:
        b_spec = pl.BlockSpec((tn, tk), lambda i, j, kk: (j, kk))
    else:
        b_spec = pl.BlockSpec((tk, tn), lambda i, j, kk: (kk, j))
    o_spec = pl.BlockSpec((tm, tn), lambda i, j, kk: (i, j))
    in_specs = [a_spec, b_spec]
    args = [a, b]
    if add is not None:
        in_specs.append(o_spec)
        args.append(add)
    osz = jnp.dtype(out_dtype).itemsize
    est = 2 * (tm * tk * a.dtype.itemsize + tk * tn * b.dtype.itemsize + tm * tn * osz)
    est += 3 * tm * tn * 4 + (2 * tm * tn * 4 if add is not None else 0)
    return pl.pallas_call(
        body, name=name, grid=(m // tm, n // tn, nk),
        in_specs=in_specs, out_specs=o_spec,
        out_shape=jax.ShapeDtypeStruct((m, n), out_dtype),
        scratch_shapes=[pltpu.VMEM((tm, tn), F32)] if nk > 1 else [],
        compiler_params=_params(("parallel", "parallel", "arbitrary"), est + 8 * 2**20),
    )(*args)


def _cast_bf16(a, rows, name):
    r, c = a.shape
    rows = min(rows, r)

    def body(a_ref, o_ref):
        o_ref[...] = a_ref[...].astype(BF16)

    return pl.pallas_call(
        body, name=name, grid=(r // rows,),
        in_specs=[pl.BlockSpec((rows, c), lambda i: (i, 0))],
        out_specs=pl.BlockSpec((rows, c), lambda i: (i, 0)),
        out_shape=jax.ShapeDtypeStruct((r, c), BF16),
        compiler_params=_params(("parallel",)),
    )(a)


def _align_shard(wt):
    r, d = wt.shape
    cols = min(256, d)

    def body(w_ref, o_ref, pad_ref):
        chip = 2 * lax.axis_index("x") + lax.axis_index("y")
        pad_ref[...] = jnp.zeros_like(pad_ref)
        pad_ref[0:r, :] = w_ref[...]
        o_ref[...] = pltpu.roll(pad_ref[...], _by_chip(chip, SHIFTS), 0).astype(BF16)

    return pl.pallas_call(
        body, name="align_shard", grid=(d // cols,),
        in_specs=[pl.BlockSpec((r, cols), lambda i: (0, i))],
        out_specs=pl.BlockSpec((ALIGNED_W, cols), lambda i: (0, i)),
        out_shape=jax.ShapeDtypeStruct((ALIGNED_W, d), BF16),
        scratch_shapes=[pltpu.VMEM((ALIGNED_W, cols), F32)],
        compiler_params=_params(("parallel",)),
    )(wt)


def _rms_in(x, w):
    n, d = x.shape
    tr = min(256, n)

    def body(x_ref, w_ref, h_ref):
        xv = x_ref[...]
        r = lax.rsqrt(jnp.mean(xv * xv, axis=-1, keepdims=True) + EPS)
        h_ref[...] = (xv * r * w_ref[...]).astype(BF16)

    return pl.pallas_call(
        body, name="rms_in", grid=(n // tr,),
        in_specs=[pl.BlockSpec((tr, d), lambda i: (i, 0)), pl.BlockSpec((1, d), lambda i: (0, 0))],
        out_specs=pl.BlockSpec((tr, d), lambda i: (i, 0)),
        out_shape=jax.ShapeDtypeStruct((n, d), BF16),
        compiler_params=_params(("parallel",)),
    )(x, w)


def _conv_silu(p, w_ref, taps):
    c = None
    for j in range(taps):
        t = _shift_down(p, taps - 1 - j) * w_ref[j:j + 1, :]
        c = t if c is None else c + t
    return c


def _prep_qkv(proj, cw):
    n = proj.shape[0]

    def body(p3, wq, wk, wv, q_ref, k_ref, v_ref):
        for kind, (w_ref, o_ref) in enumerate(((wq, q_ref), (wk, k_ref), (wv, v_ref))):
            c = _conv_silu(p3[:, kind * DH:(kind + 1) * DH], w_ref, 4)
            a = c * _sigmoid(c)
            if kind < 2:
                r = lax.rsqrt(jnp.sum(a * a, axis=-1, keepdims=True) + EPS)
                a = a * (r * (DH ** -0.5 if kind == 0 else 1.0))
            o_ref[...] = a

    col = pl.BlockSpec((n, DH), lambda h: (0, h))
    wcol = lambda base: pl.BlockSpec((4, DH), lambda h: (0, base + h))
    out = jax.ShapeDtypeStruct((n, GW), F32)
    return pl.pallas_call(
        body, name="prep_qkv", grid=(HEADS,),
        in_specs=[pl.BlockSpec((n, 3 * DH), lambda h: (0, h)), wcol(QB), wcol(KB), wcol(VB)],
        out_specs=[col] * 3, out_shape=[out] * 3,
        compiler_params=_params(("parallel",), 40 * 2**20),
    )(proj, cw, cw, cw)


def _prep_qkv_bwd(proj, cw, dq, dk, dv, dproj):
    n = proj.shape[0]

    def body(p3, wq, wk, wv, dq_ref, dk_ref, dv_ref, _, o3, gq, gk, gv):
        for kind, (w_ref, d_ref, g_ref) in enumerate(((wq, dq_ref, gq), (wk, dk_ref, gk), (wv, dv_ref, gv))):
            p = p3[:, kind * DH:(kind + 1) * DH]
            c = _conv_silu(p, w_ref, 4)
            s = _sigmoid(c)
            a = c * s
            d = d_ref[...]
            if kind < 2:
                r = lax.rsqrt(jnp.sum(a * a, axis=-1, keepdims=True) + EPS)
                sc = DH ** -0.5 if kind == 0 else 1.0
                d = (sc * r) * (d - a * ((r * r) * jnp.sum(d * a, axis=-1, keepdims=True)))
            dc = d * _dsilu(c, s)
            dp = None
            for j in range(4):
                g_ref[j:j + 1, :] = jnp.sum(dc * _shift_down(p, 3 - j), axis=0, keepdims=True)
                t = _shift_up(dc, 3 - j) * w_ref[j:j + 1, :]
                dp = t if dp is None else dp + t
            o3[:, kind * DH:(kind + 1) * DH] = dp.astype(BF16)

    col = pl.BlockSpec((n, DH), lambda h: (0, h))
    wcol = lambda base: pl.BlockSpec((4, DH), lambda h: (0, base + h))
    p3spec = pl.BlockSpec((n, 3 * DH), lambda h: (0, h))
    return pl.pallas_call(
        body, name="prep_qkv_bwd", grid=(HEADS,),
        in_specs=[p3spec, wcol(QB), wcol(KB), wcol(VB), col, col, col, ANY],
        out_specs=[p3spec] + [wcol(0)] * 3,
        out_shape=[jax.ShapeDtypeStruct(dproj.shape, BF16)] + [jax.ShapeDtypeStruct((4, GW), F32)] * 3,
        input_output_aliases={7: 0},
        compiler_params=_params(("parallel",), 48 * 2**20),
    )(proj, cw, cw, cw, dq, dk, dv, dproj)


CPB = 8
SCAN_CPS = 4


def _tri(lower, rows):
    i = lax.broadcasted_iota(jnp.int32, (rows, rows), 0)
    j = lax.broadcasted_iota(jnp.int32, (rows, rows), 1)
    return jnp.where((i // CH == j // CH) & ((i >= j) if lower else (j >= i)), 1.0, 0.0)


def _lane(shape):
    return lax.broadcasted_iota(jnp.int32, shape, 1)


def _prep_bg(proj, ad):
    n = proj.shape[0]
    nch = n // CH
    cpb = CPB if nch % CPB == 0 else 1
    rows = cpb * CH

    def body(p_ref, ad_ref, bg_ref, bgt_ref):
        p = p_ref[...]
        lane = _lane(p.shape)
        beta = _sigmoid(p)
        xa = p + ad_ref[1:2, :]
        sp = jnp.maximum(xa, 0.0) + jnp.log(1.0 + jnp.exp(-jnp.abs(xa)))
        g = pltpu.roll(-jnp.exp(ad_ref[0:1, :]) * sp, DH - A_LANE + HEADS, 1)
        gc = _dot(_tri(True, rows), g, NN, P_CUM)
        bg = jnp.where(lane < HEADS, beta, jnp.where(lane < 2 * HEADS, gc, 0.0))
        bg_ref[...] = bg
        for ci in range(cpb):
            bgt_ref[ci] = bg[ci * CH:(ci + 1) * CH, :].T

    return pl.pallas_call(
        body, name="prep_bg", grid=(nch // cpb,),
        in_specs=[pl.BlockSpec((rows, DH), lambda i: (i, BAB)), pl.BlockSpec((2, DH), lambda i: (0, 0))],
        out_specs=[pl.BlockSpec((rows, DH), lambda i: (i, 0)), pl.BlockSpec((cpb, DH, CH), lambda i: (i, 0, 0))],
        out_shape=[jax.ShapeDtypeStruct((n, DH), F32), jax.ShapeDtypeStruct((nch, DH, CH), F32)],
        compiler_params=_params(("parallel",)),
    )(proj, ad)


def _prep_bg_bwd(proj, ad, dbg, dproj):
    n = proj.shape[0]
    nch = n // CH
    cpb = CPB if nch % CPB == 0 else 1
    rows = cpb * CH

    def body(p_ref, ad_ref, d_ref, _, o_ref, ga_ref, gd_ref):
        p = p_ref[...]
        d = d_ref[...]
        lane = _lane(p.shape)
        beta = _sigmoid(p)
        xa = p + ad_ref[1:2, :]
        sp = jnp.maximum(xa, 0.0) + jnp.log(1.0 + jnp.exp(-jnp.abs(xa)))
        na = -jnp.exp(ad_ref[0:1, :])
        dg = pltpu.roll(_dot(_tri(False, rows), d, NN, P_CUM), A_LANE - HEADS, 1)
        da = dg * na * _sigmoid(xa)
        is_g = lane >= A_LANE
        o_ref[...] = jnp.where(lane < HEADS, d * beta * (1.0 - beta), jnp.where(is_g, da, 0.0)).astype(BF16)
        ga = jnp.sum(jnp.where(is_g, dg * na * sp, 0.0), axis=0, keepdims=True)
        gd = jnp.sum(jnp.where(is_g, da, 0.0), axis=0, keepdims=True)

        @pl.when(pl.program_id(0) == 0)
        def _():
            ga_ref[...] = jnp.zeros_like(ga_ref)
            gd_ref[...] = jnp.zeros_like(gd_ref)

        ga_ref[...] += ga
        gd_ref[...] += gd

    one = pl.BlockSpec((1, DH), lambda i: (0, 0))
    return pl.pallas_call(
        body, name="prep_bg_bwd", grid=(nch // cpb,),
        in_specs=[pl.BlockSpec((rows, DH), lambda i: (i, BAB)), pl.BlockSpec((2, DH), lambda i: (0, 0)),
                  pl.BlockSpec((rows, DH), lambda i: (i, 0)), ANY],
        out_specs=[pl.BlockSpec((rows, DH), lambda i: (i, BAB)), one, one],
        out_shape=[jax.ShapeDtypeStruct(dproj.shape, BF16), jax.ShapeDtypeStruct((1, DH), F32),
                   jax.ShapeDtypeStruct((1, DH), F32)],
        input_output_aliases={3: 0},
        compiler_params=_params(("arbitrary",)),
    )(proj, ad, dbg, dproj)


def _gdn_out(o, proj, wg):
    n = o.shape[0]

    def body(o_ref, z_ref, w_ref, y_ref):
        ov, z = o_ref[...], z_ref[...]
        r = lax.rsqrt(jnp.mean(ov * ov, axis=-1, keepdims=True) + EPS)
        y_ref[...] = (ov * r * w_ref[...] * (z * _sigmoid(z))).astype(BF16)

    return pl.pallas_call(
        body, name="gdn_out", grid=(HEADS,),
        in_specs=[pl.BlockSpec((n, DH), lambda h: (0, h)), pl.BlockSpec((n, DH), lambda h: (0, ZB + h)),
                  pl.BlockSpec((1, DH), lambda h: (0, 0))],
        out_specs=pl.BlockSpec((n, DH), lambda h: (0, h)),
        out_shape=jax.ShapeDtypeStruct((n, 2 * GW), BF16),
        compiler_params=_params(("parallel",)),
    )(o, proj, wg)


def _gdn_out_bwd(o, proj, wg, dmix):
    n = o.shape[0]

    def body(o_ref, z_ref, w_ref, d_ref, do_ref, dz_ref, gw_ref):
        ov, z, d, w = o_ref[...], z_ref[...], d_ref[...], w_ref[...]
        r = lax.rsqrt(jnp.mean(ov * ov, axis=-1, keepdims=True) + EPS)
        nrm = ov * r
        s = _sigmoid(z)
        dz_ref[...] = (d * (nrm * w) * _dsilu(z, s)).astype(BF16)
        dn_w = d * (z * s)
        gw = jnp.sum(dn_w * nrm, axis=0, keepdims=True)
        dn = dn_w * w
        do_ref[...] = r * (dn - nrm * jnp.mean(dn * nrm, axis=-1, keepdims=True))

        @pl.when(pl.program_id(0) == 0)
        def _():
            gw_ref[...] = jnp.zeros_like(gw_ref)

        gw_ref[...] += gw

    return pl.pallas_call(
        body, name="gdn_out_bwd", grid=(HEADS,),
        in_specs=[pl.BlockSpec((n, DH), lambda h: (0, h)), pl.BlockSpec((n, DH), lambda h: (0, ZB + h)),
                  pl.BlockSpec((1, DH), lambda h: (0, 0)), pl.BlockSpec((n, DH), lambda h: (0, h))],
        out_specs=[pl.BlockSpec((n, DH), lambda h: (0, h)), pl.BlockSpec((n, DH), lambda h: (0, ZB + h)),
                   pl.BlockSpec((1, DH), lambda h: (0, 0))],
        out_shape=[jax.ShapeDtypeStruct((n, GW), F32), jax.ShapeDtypeStruct((n, GW_COLS), BF16),
                   jax.ShapeDtypeStruct((1, DH), F32)],
        compiler_params=_params(("arbitrary",)),
    )(o, proj, wg, dmix)


def _conv_branch(proj, w3, b, mix):
    n = proj.shape[0]

    def body(p4, w_ref, b_ref, _, y_ref):
        u = p4[:, DH:2 * DH] * p4[:, 2 * DH:3 * DH]
        cc = _conv_silu(u, w_ref, 3) + b_ref[...]
        z = p4[:, 3 * DH:4 * DH]
        y_ref[...] = (p4[:, 0:DH] * cc * (z * _sigmoid(z))).astype(BF16)

    return pl.pallas_call(
        body, name="conv_branch", grid=(HEADS,),
        in_specs=[pl.BlockSpec((n, 4 * DH), lambda h: (0, h)), pl.BlockSpec((3, DH), lambda h: (0, h)),
                  pl.BlockSpec((1, DH), lambda h: (0, h)), ANY],
        out_specs=pl.BlockSpec((n, DH), lambda h: (0, HEADS + h)),
        out_shape=jax.ShapeDtypeStruct(mix.shape, BF16),
        input_output_aliases={3: 0},
        compiler_params=_params(("parallel",), 40 * 2**20),
    )(proj, w3, b, mix)


def _conv_branch_bwd(proj, w3, b, dmix):
    n = proj.shape[0]

    def body(p4, w_ref, b_ref, d_ref, o4, gw_ref, gbias_ref):
        gb, gcv, hc, z = p4[:, 0:DH], p4[:, DH:2 * DH], p4[:, 2 * DH:3 * DH], p4[:, 3 * DH:4 * DH]
        d = d_ref[...]
        dgb, dgc, dhc, dzc = (o4.at[:, kk * DH:(kk + 1) * DH] for kk in range(4))
        u = gcv * hc
        cc = _conv_silu(u, w_ref, 3) + b_ref[...]
        s = _sigmoid(z)
        dzc[...] = (d * (gb * cc) * _dsilu(z, s)).astype(BF16)
        dp = d * (z * s)
        dgb[...] = (dp * cc).astype(BF16)
        dcc = dp * gb
        gbias_ref[...] = jnp.sum(dcc, axis=0, keepdims=True)
        du = None
        for j in range(3):
            gw_ref[j:j + 1, :] = jnp.sum(dcc * _shift_down(u, 2 - j), axis=0, keepdims=True)
            t = _shift_up(dcc, 2 - j) * w_ref[j:j + 1, :]
            du = t if du is None else du + t
        dgc[...] = (du * hc).astype(BF16)
        dhc[...] = (du * gcv).astype(BF16)

    p4spec = pl.BlockSpec((n, 4 * DH), lambda h: (0, h))
    return pl.pallas_call(
        body, name="conv_branch_bwd", grid=(HEADS,),
        in_specs=[p4spec, pl.BlockSpec((3, DH), lambda h: (0, h)), pl.BlockSpec((1, DH), lambda h: (0, h)),
                  pl.BlockSpec((n, DH), lambda h: (0, HEADS + h))],
        out_specs=[p4spec, pl.BlockSpec((3, DH), lambda h: (0, h)), pl.BlockSpec((1, DH), lambda h: (0, h))],
        out_shape=[jax.ShapeDtypeStruct((n, CW_COLS), BF16), jax.ShapeDtypeStruct((3, GW), F32),
                   jax.ShapeDtypeStruct((1, GW), F32)],
        compiler_params=_params(("parallel",), 48 * 2**20),
    )(proj, w3, b, dmix)


def _final_loss(out, tgt, wf):
    n, d = out.shape
    tr = min(256, n)

    def body(o_ref, t_ref, w_ref, do_ref, dob_ref, gw_ref, loss_ref):
        ov, w = o_ref[...], w_ref[...]
        r = lax.rsqrt(jnp.mean(ov * ov, axis=-1, keepdims=True) + EPS)
        nrm = ov * r
        e = nrm * w - t_ref[...]
        dy = e * (1.0 / d)
        dn = dy * w
        dout = r * (dn - nrm * jnp.mean(dn * nrm, axis=-1, keepdims=True))
        do_ref[...] = dout
        dob_ref[...] = dout.astype(BF16)

        @pl.when(pl.program_id(0) == 0)
        def _():
            gw_ref[...] = jnp.zeros_like(gw_ref)
            loss_ref[...] = jnp.zeros_like(loss_ref)

        gw_ref[...] += jnp.sum(dy * nrm, axis=0, keepdims=True)
        loss_ref[...] += (0.5 / d) * jnp.sum(jnp.sum(e * e, axis=-1, keepdims=True), axis=0, keepdims=True)

    row = pl.BlockSpec((tr, d), lambda i: (i, 0))
    return pl.pallas_call(
        body, name="final_loss", grid=(n // tr,),
        in_specs=[row, row, pl.BlockSpec((1, d), lambda i: (0, 0))],
        out_specs=[row, row, pl.BlockSpec((1, d), lambda i: (0, 0)), pl.BlockSpec((1, 1), lambda i: (0, 0))],
        out_shape=[jax.ShapeDtypeStruct((n, d), F32), jax.ShapeDtypeStruct((n, d), BF16),
                   jax.ShapeDtypeStruct((1, d), F32), jax.ShapeDtypeStruct((1, 1), F32)],
        compiler_params=_params(("arbitrary",)),
    )(out, tgt, wf)


def _rms_in_bwd(x, w, dh, dout):
    n, d = x.shape
    tr = min(256, n)

    def body(x_ref, w_ref, dh_ref, do_ref, dx_ref, gw_ref):
        xv, dhv = x_ref[...], dh_ref[...]
        r = lax.rsqrt(jnp.mean(xv * xv, axis=-1, keepdims=True) + EPS)
        xn = xv * r
        dxn = dhv * w_ref[...]
        dx_ref[...] = r * (dxn - xn * jnp.mean(dxn * xn, axis=-1, keepdims=True)) + do_ref[...]

        @pl.when(pl.program_id(0) == 0)
        def _():
            gw_ref[...] = jnp.zeros_like(gw_ref)

        gw_ref[...] += jnp.sum(dhv * xn, axis=0, keepdims=True)

    row = pl.BlockSpec((tr, d), lambda i: (i, 0))
    one = pl.BlockSpec((1, d), lambda i: (0, 0))
    return pl.pallas_call(
        body, name="rms_in_bwd", grid=(n // tr,),
        in_specs=[row, one, row, row], out_specs=[row, one],
        out_shape=[jax.ShapeDtypeStruct((n, d), F32), jax.ShapeDtypeStruct((1, d), F32)],
        compiler_params=_params(("arbitrary",)),
    )(x, w, dh, dout)


def _ij():
    i = lax.broadcasted_iota(jnp.int32, (CH, CH), 0)
    j = lax.broadcasted_iota(jnp.int32, (CH, CH), 1)
    return i, j


def _unit_lower_inverse(mats):
    i, j = _ij()
    eye = jnp.where(i == j, 1.0, 0.0)
    same16 = (i // 16) == (j // 16)
    same32 = (i // 32) == (j // 32)
    mm = lambda xs, ys: [_dot(x, y, NN, P_INV) for x, y in zip(xs, ys)]
    n1 = [jnp.where(same16, -a, 0.0) for a in mats]
    n2 = mm(n1, n1)
    n4 = mm(n2, n2)
    n8 = mm(n4, n4)
    t = [eye + x1 + x2 + x3 for x1, x2, x3 in zip(n1, n2, mm(n1, n2))]
    t = [x + y for x, y in zip(t, mm(t, n4))]
    t = [x + y for x, y in zip(t, mm(t, n8))]
    a1 = [jnp.where(same32 & jnp.logical_not(same16), a, 0.0) for a in mats]
    t = [x - y for x, y in zip(t, mm(t, mm(a1, t)))]
    a2 = [jnp.where(same32, 0.0, a) for a in mats]
    t = [x - y for x, y in zip(t, mm(t, mm(a2, t)))]
    return t


def _head_vectors(bg, bgt, h):
    bcol = bg[:, h:h + 1]
    gcol = bg[:, HEADS + h:HEADS + h + 1]
    grow = bgt[HEADS + h:HEADS + h + 1, :]
    return bcol, gcol, grow


def _decay(gcol, grow):
    i, j = _ij()
    return jnp.where(i >= j, jnp.exp(jnp.where(i >= j, gcol - grow, 0.0)), 0.0)


def _gdn_intra(q, k, v, bg, bgt):
    n = q.shape[0]
    nch = n // CH
    cps = 4 if nch % 4 == 0 else 1

    def body(q_ref, k_ref, v_ref, bg_ref, bgt_ref, u_ref, w_ref, p_ref, t_ref):
        i, j = _ij()
        items = [(ci, h) for ci in range(cps) for h in range(HEADS)]
        at = lambda ref, ci, h: ref.at[ci * CH:(ci + 1) * CH, h * DH:(h + 1) * DH]
        bgs = [bg_ref[ci * CH:(ci + 1) * CH, :] for ci in range(cps)]
        ks = [at(k_ref, ci, h)[...] for ci, h in items]
        vecs = [_head_vectors(bgs[ci], bgt_ref[ci], h) for ci, h in items]
        decs = [_decay(gcol, grow) for _, gcol, grow in vecs]
        kks = [_dot(kh, kh, NT, P_GRAM) for kh in ks]
        qks = [_dot(at(q_ref, ci, h)[...], kh, NT, P_GRAM) for (ci, h), kh in zip(items, ks)]
        ts = _unit_lower_inverse([jnp.where(i > j, bcol * kk * dec, 0.0)
                                  for (bcol, _, _), kk, dec in zip(vecs, kks, decs)])
        us = [_dot(t, at(v_ref, ci, h)[...] * bcol, NN, P_SOL) for t, (ci, h), (bcol, _, _) in zip(ts, items, vecs)]
        ws = [_dot(t, kh * (bcol * jnp.exp(gcol)), NN, P_SOL) for t, kh, (bcol, gcol, _) in zip(ts, ks, vecs)]
        for n_, (ci, h) in enumerate(items):
            p_ref[ci, h] = qks[n_] * decs[n_]
            t_ref[ci, h] = ts[n_]
            at(u_ref, ci, h)[...] = us[n_]
            at(w_ref, ci, h)[...] = ws[n_]

    row = pl.BlockSpec((cps * CH, GW), lambda c: (c, 0))
    sq = pl.BlockSpec((cps, HEADS, CH, CH), lambda c: (c, 0, 0, 0))
    big = jax.ShapeDtypeStruct((n, GW), F32)
    sqs = jax.ShapeDtypeStruct((nch, HEADS, CH, CH), F32)
    return pl.pallas_call(
        body, name="gdn_intra", grid=(nch // cps,),
        in_specs=[row, row, row, pl.BlockSpec((cps * CH, DH), lambda c: (c, 0)),
                  pl.BlockSpec((cps, DH, CH), lambda c: (c, 0, 0))],
        out_specs=[row, row, sq, sq], out_shape=[big, big, sqs, sqs],
        compiler_params=_params(("parallel",)),
    )(q, k, v, bg, bgt)


def _gdn_scan(q, k, bg, u, w, p):
    n = q.shape[0]
    nch = n // CH
    cps = SCAN_CPS if nch % SCAN_CPS == 0 else 1

    def body(q_ref, k_ref, bg_ref, u_ref, w_ref, p_ref, o_ref, vn_ref, s_out, s_scr):
        @pl.when(pl.program_id(0) == 0)
        def _():
            s_scr[...] = jnp.zeros_like(s_scr)

        hs = range(HEADS)
        sls = [slice(h * DH, (h + 1) * DH) for h in hs]
        ss = [s_scr[h] for h in hs]
        for ci in range(cps):
            rs = slice(ci * CH, (ci + 1) * CH)
            bg = bg_ref[rs, :]
            gcols = [bg[:, HEADS + h:HEADS + h + 1] for h in hs]
            glasts = [g[CH - 1:CH, :] for g in gcols]
            wss = [_dot(w_ref[rs, sl], s, NN, P_SCAN) for sl, s in zip(sls, ss)]
            oqs = [_dot(q_ref[rs, sl] * jnp.exp(g), s, NN, P_SCAN) for sl, s, g in zip(sls, ss, gcols)]
            vns = [u_ref[rs, sl] - x for sl, x in zip(sls, wss)]
            ops = [_dot(p_ref[ci, h], vn, NN, P_SCAN) for h, vn in zip(hs, vns)]
            sns = [_dot(k_ref[rs, sl] * jnp.exp(gl - g), vn, TN, P_SCAN)
                   for sl, gl, g, vn in zip(sls, glasts, gcols, vns)]
            for h, sl in enumerate(sls):
                s_out[ci, :, sl] = ss[h]
                vn_ref[rs, sl] = vns[h]
                o_ref[rs, sl] = oqs[h] + ops[h]
            ss = [s * jnp.exp(gl) + sn for s, gl, sn in zip(ss, glasts, sns)]
        for h in hs:
            s_scr[h] = ss[h]

    row = pl.BlockSpec((cps * CH, GW), lambda c: (c, 0))
    big = jax.ShapeDtypeStruct((n, GW), F32)
    return pl.pallas_call(
        body, name="gdn_scan", grid=(nch // cps,),
        in_specs=[row, row, pl.BlockSpec((cps * CH, DH), lambda c: (c, 0)), row, row,
                  pl.BlockSpec((cps, HEADS, CH, CH), lambda c: (c, 0, 0, 0))],
        out_specs=[row, row, pl.BlockSpec((cps, DH, GW), lambda c: (c, 0, 0))],
        out_shape=[big, big, jax.ShapeDtypeStruct((nch, DH, GW), F32)],
        scratch_shapes=[pltpu.VMEM((HEADS, DH, DH), F32)],
        compiler_params=_params(("arbitrary",)),
    )(q, k, bg, u, w, p)


def _gdn_scan_bwd(q, k, bg, w, p, vn, s_in, do):
    n = q.shape[0]
    nch = n // CH
    cps = SCAN_CPS if nch % SCAN_CPS == 0 else 1
    rev = lambda c: nch // cps - 1 - c

    def body(q_ref, k_ref, bg_ref, w_ref, p_ref, vn_ref, s_ref, do_ref,
             dqg_ref, dp_ref, du_ref, dw_ref, dks_ref, dgam_ref, ds_scr):
        @pl.when(pl.program_id(0) == 0)
        def _():
            ds_scr[...] = jnp.zeros_like(ds_scr)

        lane = _lane((1, DH))
        hs = range(HEADS)
        sls = [slice(h * DH, (h + 1) * DH) for h in hs]
        dss = [ds_scr[h] for h in hs]
        for ci in reversed(range(cps)):
            rs = slice(ci * CH, (ci + 1) * CH)
            bg = bg_ref[rs, :]
            gcols = [bg[:, HEADS + h:HEADS + h + 1] for h in hs]
            glasts = [g[CH - 1:CH, :] for g in gcols]
            ss = [s_ref[ci, :, sl] for sl in sls]
            dos = [do_ref[rs, sl] for sl in sls]
            vnl = [vn_ref[rs, sl] for sl in sls]
            dqgs = [_dot(d, s, NT, P_SCANB) for d, s in zip(dos, ss)]
            dps = [_dot(d, vn, NT, P_SCANB) for d, vn in zip(dos, vnl)]
            dvn1 = [_dot(p_ref[ci, h], d, TN, P_SCANB) for h, d in zip(hs, dos)]
            dvn2 = [_dot(k_ref[rs, sl] * jnp.exp(gl - g), ds, NN, P_SCANB)
                    for sl, gl, g, ds in zip(sls, glasts, gcols, dss)]
            dkss = [_dot(vn, ds, NT, P_SCANB) for vn, ds in zip(vnl, dss)]
            dsq = [_dot(q_ref[rs, sl] * jnp.exp(g), d, TN, P_SCANB) for sl, g, d in zip(sls, gcols, dos)]
            dvns = [a + b for a, b in zip(dvn1, dvn2)]
            dws = [_dot(dvn, s, NT, P_SCANB) for dvn, s in zip(dvns, ss)]
            dsw = [_dot(w_ref[rs, sl], dvn, TN, P_SCANB) for sl, dvn in zip(sls, dvns)]
            dgam = jnp.zeros((1, DH), F32)
            for h, sl in enumerate(sls):
                dqg_ref[rs, sl] = dqgs[h]
                dp_ref[ci, h] = dps[h]
                du_ref[rs, sl] = dvns[h]
                dw_ref[rs, sl] = -dws[h]
                dks_ref[rs, sl] = dkss[h]
                tot = jnp.sum(jnp.sum(dss[h] * ss[h], axis=-1, keepdims=True), axis=0, keepdims=True)
                dgam = dgam + jnp.where(lane == h, tot, 0.0)
            dgam_ref[ci] = jnp.broadcast_to(dgam, (8, DH))
            dss = [ds * jnp.exp(gl) + a - b for ds, gl, a, b in zip(dss, glasts, dsq, dsw)]
        for h in hs:
            ds_scr[h] = dss[h]

    row = pl.BlockSpec((cps * CH, GW), lambda c: (rev(c), 0))
    sq = pl.BlockSpec((cps, HEADS, CH, CH), lambda c: (rev(c), 0, 0, 0))
    big = jax.ShapeDtypeStruct((n, GW), F32)
    return pl.pallas_call(
        body, name="gdn_scan_bwd", grid=(nch // cps,),
        in_specs=[row, row, pl.BlockSpec((cps * CH, DH), lambda c: (rev(c), 0)), row, sq, row,
                  pl.BlockSpec((cps, DH, GW), lambda c: (rev(c), 0, 0)), row],
        out_specs=[row, sq, row, row, row, pl.BlockSpec((cps, 8, DH), lambda c: (rev(c), 0, 0))],
        out_shape=[big, jax.ShapeDtypeStruct((nch, HEADS, CH, CH), F32), big, big, big,
                   jax.ShapeDtypeStruct((nch, 8, DH), F32)],
        scratch_shapes=[pltpu.VMEM((HEADS, DH, DH), F32)],
        compiler_params=_params(("arbitrary",)),
    )(q, k, bg, w, p, vn, s_in, do)


def _gdn_intra_bwd(q, k, v, bg, bgt, t, u, w, p, dqg, dp, du, dw, dks, dgam):
    n = q.shape[0]
    nch = n // CH
    cps = 1

    def body(q_ref, k_ref, v_ref, bg_ref, bgt_ref, t_ref, u_ref, w_ref, p_ref,
             dqg_ref, dp_ref, du_ref, dw_ref, dks_ref, dgam_ref, dq_ref, dk_ref, dv_ref, dbg_ref):
        i, j = _ij()
        rows1 = lax.broadcasted_iota(jnp.int32, (CH, 1), 0)
        lane = _lane((CH, DH))
        rsum = lambda x: jnp.sum(x, axis=-1, keepdims=True)
        items = [(ci, h) for ci in range(cps) for h in range(HEADS)]
        at = lambda ref, it: ref.at[it[0] * CH:(it[0] + 1) * CH, it[1] * DH:(it[1] + 1) * DH]
        ld = lambda ref: [at(ref, it)[...] for it in items]
        bgs = [bg_ref[ci * CH:(ci + 1) * CH, :] for ci in range(cps)]
        qs, ks = ld(q_ref), ld(k_ref)
        vecs = [_head_vectors(bgs[ci], bgt_ref[ci], h) for ci, h in items]
        decs = [_decay(gcol, grow) for _, gcol, grow in vecs]
        ths = [t_ref[ci, h] for ci, h in items]
        drus = [_dot(th, x_, TN, P_BWD) for th, x_ in zip(ths, ld(du_ref))]
        drws = [_dot(th, x_, TN, P_BWD) for th, x_ in zip(ths, ld(dw_ref))]
        kks = [_dot(kh, kh, NT, P_GRAM) for kh in ks]
        da1 = [_dot(dru, x_, NT, P_BWD) for dru, x_ in zip(drus, ld(u_ref))]
        da2 = [_dot(drw, x_, NT, P_BWD) for drw, x_ in zip(drws, ld(w_ref))]
        das = [jnp.where(i > j, -(x_ + y_), 0.0) for x_, y_ in zip(da1, da2)]
        dkks = [da * bcol * dec for da, (bcol, _, _), dec in zip(das, vecs, decs)]
        dps = [dp_ref[ci, h] for ci, h in items]
        dqks = [dp_ * dec for dp_, dec in zip(dps, decs)]
        dq_ps = [_dot(dqk, kh, NN, P_BWD) for dqk, kh in zip(dqks, ks)]
        dk_ps = [_dot(dqk, qh, TN, P_BWD) for dqk, qh in zip(dqks, qs)]
        dk_as = [_dot(dkk, kh, NN, P_BWD) for dkk, kh in zip(dkks, ks)]
        dk_bs = [_dot(dkk, kh, TN, P_BWD) for dkk, kh in zip(dkks, ks)]
        bcols = [vc[0] for vc in vecs]
        gcols = [vc[1] for vc in vecs]
        gams = [jnp.exp(g) for g in gcols]
        glasts = [g[CH - 1:CH, :] for g in gcols]
        es = [jnp.exp(gl - g) for gl, g in zip(glasts, gcols)]
        kgs = [kh * gam for kh, gam in zip(ks, gams)]
        dqgs, dkss = ld(dqg_ref), ld(dks_ref)
        r_uv = [rsum(dru * x_) for dru, x_ in zip(drus, ld(v_ref))]
        r_wk = [rsum(drw * kg) for drw, kg in zip(drws, kgs)]
        r_ak = [rsum(da * kk * dec) for da, kk, dec in zip(das, kks, decs)]
        r_qq = [rsum(dqg * qh) for dqg, qh in zip(dqgs, qs)]
        tks = [rsum(dk_ * kh) * e for dk_, kh, e in zip(dkss, ks, es)]
        mdecs = [da * (bcol * kk * dec) + dp_ * p_ref[ci, h]
                 for (ci, h), da, bcol, kk, dec, dp_ in zip(items, das, bcols, kks, decs, dps)]
        r_md = [rsum(m) for m in mdecs]
        c_md = [rsum(jnp.where(i == j, jnp.sum(m, axis=0, keepdims=True), 0.0)) for m in mdecs]
        dbgs = [jnp.zeros((CH, DH), F32) for _ in range(cps)]
        for n_, (ci, h) in enumerate(items):
            at(dv_ref, (ci, h))[...] = bcols[n_] * drus[n_]
            at(dq_ref, (ci, h))[...] = gams[n_] * dqgs[n_] + dq_ps[n_]
            at(dk_ref, (ci, h))[...] = ((bcols[n_] * gams[n_]) * drws[n_] + dk_ps[n_] + dk_as[n_] + dk_bs[n_]
                                        + dkss[n_] * es[n_])
            dbeta = r_uv[n_] + r_wk[n_] + r_ak[n_]
            dglast = (jnp.sum(tks[n_], axis=0, keepdims=True)
                      + dgam_ref[ci, 0:1, h:h + 1] * jnp.exp(glasts[n_]))
            dgc = (r_wk[n_] * bcols[n_] + r_md[n_] - c_md[n_] + r_qq[n_] * gams[n_] - tks[n_]
                   + jnp.where(rows1 == CH - 1, dglast, 0.0))
            dbgs[ci] = dbgs[ci] + jnp.where(lane == h, dbeta, 0.0) + jnp.where(lane == HEADS + h, dgc, 0.0)
        for ci in range(cps):
            dbg_ref[ci * CH:(ci + 1) * CH, :] = dbgs[ci]

    row = pl.BlockSpec((cps * CH, GW), lambda c: (c, 0))
    sq = pl.BlockSpec((cps, HEADS, CH, CH), lambda c: (c, 0, 0, 0))
    small = pl.BlockSpec((cps * CH, DH), lambda c: (c, 0))
    big = jax.ShapeDtypeStruct((n, GW), F32)
    return pl.pallas_call(
        body, name="gdn_intra_bwd", grid=(nch // cps,),
        in_specs=[row, row, row, small, pl.BlockSpec((cps, DH, CH), lambda c: (c, 0, 0)), sq, row, row, sq,
                  row, sq, row, row, row, pl.BlockSpec((cps, 8, DH), lambda c: (c, 0, 0))],
        out_specs=[row, row, row, small],
        out_shape=[big, big, big, jax.ShapeDtypeStruct((n, DH), F32)],
        compiler_params=_params(("parallel",)),
    )(q, k, v, bg, bgt, t, u, w, p, dqg, dp, du, dw, dks, dgam)


def _local_step(x, tgt, h, w_g, cqw, late, norm_in_w, ad, gdn_norm_w, conv_b, final_norm_w,
                on_grad_c=None, on_grad_g=None):
    proj_g = _matmul(h, w_g, NT, F32, 512, 1408, 1024, "mm_proj_g", n=GW_COLS)
    q, k, v = _prep_qkv(proj_g, cqw)
    bg, bgt = _prep_bg(proj_g, ad)
    u, w, p, t = _gdn_intra(q, k, v, bg, bgt)
    o, vn, s_in = _gdn_scan(q, k, bg, u, w, p)
    w_c, w_out, conv_w = late(o)
    proj_c = _matmul(h, w_c, NT, F32, 512, 1024, 1024, "mm_proj_c", n=CW_COLS)
    mix = _conv_branch(proj_c, conv_w, conv_b, _gdn_out(o, proj_g, gdn_norm_w))
    out = _matmul(mix, w_out, NN, F32, 512, 512, 2048, "mm_out", add=x)
    dout, dout_b, g_fn, loss = _final_loss(out, tgt, final_norm_w)

    dmix = _matmul(dout_b, w_out, NT, F32, 512, 1024, 1024, "mm_dmix")
    g_wout = _matmul(mix, dout_b, TN, BF16, 512, 512, 2048, "mm_gwout")
    do, dproj_g, g_gn = _gdn_out_bwd(o, proj_g, gdn_norm_w, dmix)
    dproj_c, g_cw, g_cb = _conv_branch_bwd(proj_c, conv_w, conv_b, dmix)
    g_c = _matmul(dproj_c, h, TN, BF16, 1024, 512, 2048, "mm_gwin_c")
    if on_grad_c is not None:
        do = on_grad_c(g_c, g_wout, do)
    dqg, dp, du, dw, dks, dgam = _gdn_scan_bwd(q, k, bg, w, p, vn, s_in, do)
    dq, dk, dv, dbg = _gdn_intra_bwd(q, k, v, bg, bgt, t, u, w, p, dqg, dp, du, dw, dks, dgam)
    dproj_g, gq, gk, gv = _prep_qkv_bwd(proj_g, cqw, dq, dk, dv, dproj_g)
    dproj_g, g_al, g_dt = _prep_bg_bwd(proj_g, ad, dbg, dproj_g)
    g_g = _matmul(dproj_g, h, TN, BF16, 1408, 512, 2048, "mm_gwin_g")
    if on_grad_g is not None:
        dproj_g = on_grad_g(g_g, dproj_g)
    dh = _matmul(dproj_g, w_g, NN, F32, 512, 1024, 1408, "mm_dh_g")
    dh = _matmul(dproj_c, w_c, NN, F32, 512, 1024, 1024, "mm_dh_c", add=dh)
    gx, g_nin = _rms_in_bwd(x, norm_in_w, dh, dout)
    small = dict(nin=g_nin, cb=g_cb, fn=g_fn, al=g_al, dt=g_dt, gn=g_gn, cq=(gq, gk, gv), cw=g_cw, loss=loss)
    return gx, small, (g_g, g_c, g_wout)


def _place():
    x, y, c = lax.axis_index("x"), lax.axis_index("y"), lax.axis_index("c")
    chips = [(1 - x, y), (x, 1 - y), (1 - x, 1 - y)]
    return x, y, c, chips


def _blk(ref, b):
    if isinstance(b, int):
        return ref.at[b * DH:(b + 1) * DH, :]
    return ref.at[pl.ds(pl.multiple_of(b * DH, DH), DH), :]


HBM = pl.BlockSpec(memory_space=pltpu.HBM)
SEM = pl.BlockSpec(memory_space=pltpu.SEMAPHORE)
EFFECT = pltpu.SideEffectType.DATAFLOW_SIDE_EFFECTING


def _split_start(name, issue, bufs, n_sems):
    nbuf = len(bufs)

    def body(*refs):
        issue(refs[:nbuf], refs[nbuf], refs[nbuf + 1])
        refs[-1][...] = jnp.zeros_like(refs[-1])

    out = pl.pallas_call(
        body, name=name,
        out_shape=(pltpu.SemaphoreType.DMA((n_sems,)), pltpu.SemaphoreType.DMA((n_sems,)),
                   *[pltpu.HBM(b.shape, b.dtype) for b in bufs], jax.ShapeDtypeStruct((8, DH), F32)),
        in_specs=[HBM] * nbuf,
        out_specs=(SEM, SEM, *[HBM] * nbuf, pl.BlockSpec(memory_space=pltpu.VMEM)),
        input_output_aliases={a: 2 + a for a in range(nbuf)},
        compiler_params=pltpu.CompilerParams(has_side_effects=EFFECT),
    )(*[pltpu.with_memory_space_constraint(b, pltpu.HBM) for b in bufs])
    return out[0], out[1], list(out[2:2 + nbuf]), out[-1]


def _split_wait(name, await_, send_sems, recv_sems, bufs, after):
    nbuf = len(bufs)

    def body(*refs):
        await_(refs[:nbuf], refs[nbuf], refs[nbuf + 1])

    out = pl.pallas_call(
        body, name=name,
        out_shape=tuple(pltpu.HBM(b.shape, b.dtype) for b in bufs),
        in_specs=[HBM] * nbuf + [SEM, SEM, ANY], out_specs=tuple([HBM] * nbuf),
        input_output_aliases={a: a for a in range(nbuf)},
        compiler_params=pltpu.CompilerParams(has_side_effects=EFFECT),
    )(*bufs, send_sems, recv_sems, after)
    return list(out)


def _phase_blocks(chip, phase, edges, parity=None):
    return [(b, blk) for b, (grp, blk) in enumerate(_shard_blocks(chip, edges))
            if grp == phase and (parity is None or b % 2 == parity)]


def _cols(ref, nblk):
    return ref.at[0:nblk * DH, :]


def _block_table(chip, edges, spare_g, spare_c):
    rows = []
    for s in range(4):
        sb = _shard_blocks(s, edges)
        rows.append([[blk if grp == "g" else spare_g for grp, blk in sb],
                     [blk if grp == "c" else spare_c for grp, blk in sb],
                     [int(grp == "g") for grp, _ in sb], [s] * ALIGNED_BLOCKS])
    return jnp.asarray(rows, jnp.int32)[chip]


def _place_own(a_shard, wo, cq, cw, bufs):
    d = a_shard.shape[1]
    chip = 2 * lax.axis_index("x") + lax.axis_index("y")

    def body(t_ref, a_ref, wo_ref, cq_ref, cw_ref, *refs):
        wg_ref, wc_ref, wog_ref, cqg_ref, cwg_ref = refs[5:]
        wg_ref[...] = a_ref[...]
        wc_ref[...] = a_ref[...]

        @pl.when(pl.program_id(0) == 0)
        def _():
            wog_ref[0] = wo_ref[...]
            cqg_ref[0] = cq_ref[...]
            cwg_ref[0] = cw_ref[...]

    whole = lambda s: pl.BlockSpec(s.shape, lambda b, t: (0,) * s.ndim)
    slot = lambda s: pl.BlockSpec((1,) + s.shape, lambda b, t: (t[3, 0],) + (0,) * s.ndim)
    return pl.pallas_call(
        body, name="place_own",
        grid_spec=pltpu.PrefetchScalarGridSpec(
            num_scalar_prefetch=1, grid=(ALIGNED_BLOCKS,),
            in_specs=[pl.BlockSpec((DH, d), lambda b, t: (b, 0)), whole(wo), whole(cq), whole(cw)] + [ANY] * 5,
            out_specs=[pl.BlockSpec((DH, d), lambda b, t: (t[0, b], 0)),
                       pl.BlockSpec((DH, d), lambda b, t: (t[1, b], 0)), slot(wo), slot(cq), slot(cw)]),
        out_shape=[jax.ShapeDtypeStruct(b.shape, b.dtype) for b in bufs],
        input_output_aliases={5 + a: a for a in range(5)},
        compiler_params=_params(("arbitrary",)),
    )(_block_table(chip, True, G_SPARE, C_SPARE), a_shard, wo, cq, cw, *bufs)


def _tie(x, token, name):
    def body(x_ref, t_ref, o_ref):
        del x_ref, t_ref, o_ref

    return pl.pallas_call(
        body, name=name, in_specs=[ANY, ANY], out_specs=ANY,
        out_shape=jax.ShapeDtypeStruct(x.shape, x.dtype), input_output_aliases={0: 0},
    )(x, token)


def _gather_start(phase, a_shard, w_grp, singles):
    ns = len(singles)

    def issue(refs, send_sems, recv_sems):
        a_ref, w_ref = refs[0], refs[1]
        x, y, c, chips = _place()
        mine = 2 * x + y
        for jj, (px, py) in enumerate(chips):
            to = dict(device_id=(px, py, c), device_id_type=MESH)
            for a in range(ns):
                pltpu.make_async_remote_copy(
                    src_ref=refs[2 + 2 * a], dst_ref=refs[3 + 2 * a].at[mine],
                    send_sem=send_sems.at[(1 + ns) * jj + 1 + a], recv_sem=recv_sems.at[(1 + ns) * jj + 1 + a],
                    **to).start()
        for s in range(4):
            for par in range(2):
                blocks = _phase_blocks(s, phase, True, par)
                if blocks:
                    @pl.when((mine == s) & (c == par))
                    def _():
                        for jj, (px, py) in enumerate(chips):
                            for b, blk in blocks:
                                pltpu.make_async_remote_copy(
                                    src_ref=_blk(a_ref, b), dst_ref=_blk(w_ref, blk),
                                    send_sem=send_sems.at[(1 + ns) * jj], recv_sem=recv_sems.at[(1 + ns) * jj],
                                    device_id=(px, py, c), device_id_type=MESH).start()

    bufs = [a_shard, w_grp] + [t for pair in singles for t in pair]
    return _split_start("gather_start_" + phase, issue, bufs, 3 * (1 + ns))


def _gather_wait(phase, send_sems, recv_sems, bufs, after):
    ns = (len(bufs) - 2) // 2

    def await_(refs, send_sems, recv_sems):
        a_ref, w_ref = refs[0], refs[1]
        x, y, c, chips = _place()
        mine = 2 * x + y
        for jj, (px, py) in enumerate(chips):
            to = dict(device_id=(px, py, c), device_id_type=MESH)
            peer = 2 * px + py
            for a in range(ns):
                cp = pltpu.make_async_remote_copy(
                    src_ref=refs[2 + 2 * a], dst_ref=refs[3 + 2 * a].at[mine],
                    send_sem=send_sems.at[(1 + ns) * jj + 1 + a], recv_sem=recv_sems.at[(1 + ns) * jj + 1 + a], **to)
                cp.wait_recv()
                cp.wait_send()
            for s in range(4):
                for par in range(2):
                    nblk = len(_phase_blocks(s, phase, True, par))
                    if nblk:
                        both = pltpu.make_async_remote_copy(
                            src_ref=_cols(a_ref, nblk), dst_ref=_cols(w_ref, nblk),
                            send_sem=send_sems.at[(1 + ns) * jj], recv_sem=recv_sems.at[(1 + ns) * jj], **to)

                        @pl.when((peer == s) & (c == par))
                        def _():
                            both.wait_recv()

                        @pl.when((mine == s) & (c == par))
                        def _():
                            both.wait_send()

    return _split_wait("gather_wait_" + phase, await_, send_sems, recv_sems, bufs, after)


def _sibling_forward(phase, w_grp):
    def body(w_in_ref, w_ref, send_sems, recv_sems):
        del w_in_ref
        x, y, c, chips = _place()
        to = dict(device_id=(x, y, 1 - c), device_id_type=MESH)
        for jj, (px, py) in enumerate(chips):
            peer = 2 * px + py
            for s in range(4):
                for par in range(2):
                    mine_blocks = _phase_blocks(s, phase, True, par)
                    theirs = len(_phase_blocks(s, phase, True, 1 - par))
                    if not (mine_blocks or theirs):
                        continue

                    @pl.when((peer == s) & (c == par))
                    def _():
                        for _, blk in mine_blocks:
                            pltpu.make_async_remote_copy(
                                src_ref=_blk(w_ref, blk), dst_ref=_blk(w_ref, blk),
                                send_sem=send_sems.at[jj], recv_sem=recv_sems.at[jj], **to).start()
                        if theirs:
                            pltpu.make_async_remote_copy(
                                src_ref=_cols(w_ref, theirs), dst_ref=_cols(w_ref, theirs),
                                send_sem=send_sems.at[jj], recv_sem=recv_sems.at[jj], **to).wait_recv()
                        if mine_blocks:
                            pltpu.make_async_remote_copy(
                                src_ref=_cols(w_ref, len(mine_blocks)), dst_ref=_cols(w_ref, len(mine_blocks)),
                                send_sem=send_sems.at[jj], recv_sem=recv_sems.at[jj], **to).wait_send()

    return pl.pallas_call(
        body, name="sibling_forward_" + phase, in_specs=[ANY], out_specs=ANY,
        out_shape=jax.ShapeDtypeStruct(w_grp.shape, w_grp.dtype), input_output_aliases={0: 0},
        scratch_shapes=[pltpu.SemaphoreType.DMA((3,)), pltpu.SemaphoreType.DMA((3,))],
    )(w_grp)


def _merge_edges(w, edge0, mixed, name):
    d = w.shape[1]

    def body(e_ref, o_ref):
        o_ref[...] = e_ref[0:DH, :] + e_ref[DH:2 * DH, :]

    def to_block(i):
        r = mixed[-1]
        for kk in range(len(mixed) - 2, -1, -1):
            r = jnp.where(i == kk, mixed[kk], r)
        return r

    return pl.pallas_call(
        body, name=name, grid=(len(mixed),),
        in_specs=[pl.BlockSpec((2 * DH, d), lambda i: (edge0 // 2 + i, 0))],
        out_specs=pl.BlockSpec((DH, d), lambda i: (to_block(i), 0)),
        out_shape=jax.ShapeDtypeStruct(w.shape, w.dtype),
        input_output_aliases={0: 0},
        compiler_params=_params(("arbitrary",)),
    )(w)


def _scatter_start(phase, g_grp, land, singles, halved=False):
    ns = len(singles)

    def issue(refs, send_sems, recv_sems):
        g_ref, land_ref = refs[0], refs[1]
        x, y, c, chips = _place()
        for jj, (px, py) in enumerate(chips):
            to = dict(device_id=(px, py, c), device_id_type=MESH)
            peer = 2 * px + py
            for a in range(ns):
                pltpu.make_async_remote_copy(
                    src_ref=refs[2 + 2 * a].at[peer], dst_ref=refs[3 + 2 * a].at[jj],
                    send_sem=send_sems.at[(1 + ns) * jj + 1 + a], recv_sem=recv_sems.at[(1 + ns) * jj + 1 + a],
                    **to).start()
            for s in range(4):
                for par in ((0, 1) if halved else (None,)):
                    blocks = _phase_blocks(s, phase, False, par)
                    if blocks:
                        @pl.when((peer == s) if par is None else ((peer == s) & (c == par)))
                        def _():
                            for b, blk in blocks:
                                pltpu.make_async_remote_copy(
                                    src_ref=_blk(g_ref, blk), dst_ref=_blk(land_ref.at[jj], b),
                                    send_sem=send_sems.at[(1 + ns) * jj], recv_sem=recv_sems.at[(1 + ns) * jj],
                                    **to).start()

    bufs = [g_grp, land] + [t for pair in singles for t in pair]
    return _split_start("scatter_start_" + phase, issue, bufs, 3 * (1 + ns))


def _scatter_wait(phase, send_sems, recv_sems, bufs, after, halved=False):
    ns = (len(bufs) - 2) // 2

    def await_(refs, send_sems, recv_sems):
        g_ref, land_ref = refs[0], refs[1]
        x, y, c, chips = _place()
        mine = 2 * x + y
        for jj, (px, py) in enumerate(chips):
            to = dict(device_id=(px, py, c), device_id_type=MESH)
            peer = 2 * px + py
            for a in range(ns):
                cp = pltpu.make_async_remote_copy(
                    src_ref=refs[2 + 2 * a].at[peer], dst_ref=refs[3 + 2 * a].at[jj],
                    send_sem=send_sems.at[(1 + ns) * jj + 1 + a], recv_sem=recv_sems.at[(1 + ns) * jj + 1 + a], **to)
                cp.wait_recv()
                cp.wait_send()
            for s in range(4):
                for par in ((0, 1) if halved else (None,)):
                    nblk = len(_phase_blocks(s, phase, False, par))
                    if nblk:
                        both = pltpu.make_async_remote_copy(
                            src_ref=_cols(g_ref, nblk), dst_ref=_cols(land_ref.at[jj], nblk),
                            send_sem=send_sems.at[(1 + ns) * jj], recv_sem=recv_sems.at[(1 + ns) * jj], **to)

                        @pl.when((mine == s) if par is None else ((mine == s) & (c == par)))
                        def _():
                            both.wait_recv()

                        @pl.when((peer == s) if par is None else ((peer == s) & (c == par)))
                        def _():
                            both.wait_send()

    return _split_wait("scatter_wait_" + phase, await_, send_sems, recv_sems, bufs, after)


def _needed_blocks(phase, parity):
    return sorted({blk for s in range(4) for _, blk in _phase_blocks(s, phase, False, parity)})


def _pair_reduce(phase, g_grp):
    n, d = g_grp.shape

    def swap(g_ref, sib_ref, send_sem, recv_sem):
        x, y, c, _ = _place()
        to = dict(device_id=(x, y, 1 - c), device_id_type=MESH)
        for par in range(2):
            give, get = _needed_blocks(phase, 1 - par), _needed_blocks(phase, par)

            @pl.when(c == par)
            def _():
                for blk in give:
                    pltpu.make_async_remote_copy(src_ref=_blk(g_ref, blk), dst_ref=_blk(sib_ref, blk),
                                                 send_sem=send_sem, recv_sem=recv_sem, **to).start()
                pltpu.make_async_remote_copy(src_ref=_cols(g_ref, len(get)), dst_ref=_cols(sib_ref, len(get)),
                                             send_sem=send_sem, recv_sem=recv_sem, **to).wait_recv()
                pltpu.make_async_remote_copy(src_ref=_cols(g_ref, len(give)), dst_ref=_cols(sib_ref, len(give)),
                                             send_sem=send_sem, recv_sem=recv_sem, **to).wait_send()

    sib = pl.pallas_call(
        swap, name="pair_swap_" + phase, in_specs=[ANY], out_specs=ANY,
        out_shape=jax.ShapeDtypeStruct((n, d), g_grp.dtype),
        scratch_shapes=[pltpu.SemaphoreType.DMA, pltpu.SemaphoreType.DMA],
    )(g_grp)

    lists = [_needed_blocks(phase, par) for par in range(2)]
    longest = max(len(t) for t in lists)
    table = jnp.asarray([t + [t[-1]] * (longest - len(t)) for t in lists], jnp.int32)[lax.axis_index("c")]

    def add(t_ref, a_ref, b_ref, o_ref):
        o_ref[...] = (a_ref[...].astype(F32) + b_ref[...].astype(F32)).astype(o_ref.dtype)

    blk = pl.BlockSpec((DH, d), lambda i, t: (t[i], 0))
    return pl.pallas_call(
        add, name="pair_add_" + phase,
        grid_spec=pltpu.PrefetchScalarGridSpec(num_scalar_prefetch=1, grid=(longest,),
                                               in_specs=[blk, blk], out_specs=blk),
        out_shape=jax.ShapeDtypeStruct((n, d), g_grp.dtype),
        compiler_params=_params(("arbitrary",)),
    )(table, g_grp, sib)


def _sum_shard(g_g, g_c, land):
    d = g_g.shape[1]
    chip = 2 * lax.axis_index("x") + lax.axis_index("y")

    def body(t_ref, gg_ref, gc_ref, land_ref, o_ref):
        b = pl.program_id(0)
        in_g = t_ref[2, b] == 1
        own = jnp.where(in_g, gg_ref[...].astype(F32), gc_ref[...].astype(F32))
        for jj in range(3):
            own = own + land_ref[jj].astype(F32)
        o_ref[...] = jnp.where(in_g & (b % 2 != lax.axis_index("c")), 0.0, own)

    return pl.pallas_call(
        body, name="sum_w_in",
        grid_spec=pltpu.PrefetchScalarGridSpec(
            num_scalar_prefetch=1, grid=(ALIGNED_BLOCKS,),
            in_specs=[pl.BlockSpec((DH, d), lambda b, t: (t[0, b], 0)), pl.BlockSpec((DH, d), lambda b, t: (t[1, b], 0)),
                      pl.BlockSpec((3, DH, d), lambda b, t: (0, b, 0))],
            out_specs=pl.BlockSpec((DH, d), lambda b, t: (b, 0))),
        out_shape=jax.ShapeDtypeStruct((ALIGNED_W, d), F32),
        compiler_params=_params(("arbitrary",)),
    )(_block_table(chip, False, 0, 0), g_g, g_c, land)


def _sum_rows(stack, land, rows):
    _, r, d = stack.shape
    rows = min(rows, r)
    chip = 2 * lax.axis_index("x") + lax.axis_index("y")

    def body(t_ref, own_ref, land_ref, o_ref):
        acc = own_ref[0].astype(F32)
        for jj in range(3):
            acc = acc + land_ref[jj].astype(F32)
        o_ref[...] = acc

    return pl.pallas_call(
        body, name="sum_w_out",
        grid_spec=pltpu.PrefetchScalarGridSpec(
            num_scalar_prefetch=1, grid=(r // rows,),
            in_specs=[pl.BlockSpec((1, rows, d), lambda i, t: (t[0], i, 0)),
                      pl.BlockSpec((3, rows, d), lambda i, t: (0, i, 0))],
            out_specs=pl.BlockSpec((rows, d), lambda i, t: (i, 0))),
        out_shape=jax.ShapeDtypeStruct((r, d), F32),
        compiler_params=_params(("arbitrary",)),
    )(jnp.reshape(chip, (1,)).astype(jnp.int32), stack, land)


def _final_exchange(parts, pack):
    npart = len(parts)

    def body(*refs):
        ins, pack_ref = refs[:npart], refs[npart]
        outs, packs = refs[npart + 1:2 * npart + 1], refs[2 * npart + 1]
        send_sems, recv_sems, psend, precv, loc_sem = refs[2 * npart + 2:]
        x, y, c, _ = _place()
        me = 4 * x + 2 * y + c
        local = pltpu.make_async_copy(pack_ref, packs.at[me], loc_sem)
        local.start()
        cps = [pltpu.make_async_remote_copy(
            src_ref=ins[a], dst_ref=outs[a], send_sem=send_sems.at[a], recv_sem=recv_sems.at[a],
            device_id=(x, y, 1 - c), device_id_type=MESH) for a in range(npart)]
        for r in range(1, 8):
            dx, dy, dc = (r >> 2) & 1, (r >> 1) & 1, r & 1
            peer = (x + dx - 2 * x * dx, y + dy - 2 * y * dy, c + dc - 2 * c * dc)
            cps.append(pltpu.make_async_remote_copy(
                src_ref=pack_ref, dst_ref=packs.at[me], send_sem=psend.at[r - 1], recv_sem=precv.at[r - 1],
                device_id=peer, device_id_type=MESH))
        for cp in cps:
            cp.start()
        for cp in cps:
            cp.wait_recv()
        for cp in cps:
            cp.wait_send()
        local.wait()

    return pl.pallas_call(
        body, name="final_exchange",
        in_specs=[ANY] * (npart + 1), out_specs=[ANY] * (npart + 1),
        out_shape=[jax.ShapeDtypeStruct(p.shape, p.dtype) for p in parts]
        + [jax.ShapeDtypeStruct((8,) + pack.shape, pack.dtype)],
        scratch_shapes=[pltpu.SemaphoreType.DMA((npart,)), pltpu.SemaphoreType.DMA((npart,)),
                        pltpu.SemaphoreType.DMA((7,)), pltpu.SemaphoreType.DMA((7,)), pltpu.SemaphoreType.DMA],
    )(*parts, pack)


def _sum_packs(packs):
    def body(p_ref, o_ref):
        acc = p_ref[0]
        for d in range(1, 8):
            acc = acc + p_ref[d]
        o_ref[...] = acc

    return pl.pallas_call(
        body, name="sum_packs", out_shape=jax.ShapeDtypeStruct(packs.shape[1:], F32),
    )(packs)


def _adamw_update(g, w_ref, m_ref, v_ref, go, do, mo, vo):
    c1 = 1.0 / (1.0 - ADAM_B1 ** ADAM_STEP)
    c2 = 1.0 / (1.0 - ADAM_B2 ** ADAM_STEP)
    mn = ADAM_B1 * m_ref[...] + (1.0 - ADAM_B1) * g
    vn = ADAM_B2 * v_ref[...] + (1.0 - ADAM_B2) * (g * g)
    go[...] = g
    mo[...] = mn
    vo[...] = vn
    do[...] = -ADAM_LR * ((mn * c1) / (jnp.sqrt(vn * c2) + ADAM_EPS) + ADAM_WD * w_ref[...])


def _adamw(w, m, v, g1, g2, rows, name):
    r, cdim = w.shape
    rows = min(rows, r)

    def body(*refs):
        n_in = 4 if g2 is None else 5
        w_ref, m_ref, v_ref, g_ref = refs[:4]
        g = g_ref[...] if g2 is None else g_ref[...] + refs[4][...]
        _adamw_update(g, w_ref, m_ref, v_ref, *refs[n_in:n_in + 4])

    blk = pl.BlockSpec((rows, cdim), lambda i: (i, 0))
    args = [w, m, v, g1] + ([] if g2 is None else [g2])
    shp = jax.ShapeDtypeStruct((r, cdim), F32)
    return pl.pallas_call(
        body, name=name, grid=(r // rows,),
        in_specs=[blk] * len(args), out_specs=[blk] * 4, out_shape=[shp] * 4,
        compiler_params=_params(("parallel",), 20 * rows * cdim * 4 + 8 * 2**20),
    )(*args)


def _adamw_shard(wt, mt, vt, g1, g2):
    r, d = wt.shape
    cols = min(128, d)

    def body(w_ref, m_ref, v_ref, g_ref, g2_ref, go, do, mo, vo, pad_ref):
        chip = 2 * lax.axis_index("x") + lax.axis_index("y")
        back = [(ALIGNED_W - s) % ALIGNED_W for s in SHIFTS]
        pad_ref[...] = pltpu.roll(g_ref[...] + g2_ref[...], _by_chip(chip, back), 0)
        _adamw_update(pad_ref[0:r, :], w_ref, m_ref, v_ref, go, do, mo, vo)

    blk = pl.BlockSpec((r, cols), lambda i: (0, i))
    gblk = pl.BlockSpec((ALIGNED_W, cols), lambda i: (0, i))
    shp = jax.ShapeDtypeStruct((r, d), F32)
    return pl.pallas_call(
        body, name="adamw_w_in", grid=(d // cols,),
        in_specs=[blk] * 3 + [gblk] * 2, out_specs=[blk] * 4, out_shape=[shp] * 4,
        scratch_shapes=[pltpu.VMEM((ALIGNED_W, cols), F32)],
        compiler_params=_params(("parallel",), 24 * ALIGNED_W * cols * 4 + 8 * 2**20),
    )(wt, mt, vt, g1, g2)


def _pad_lanes(a, width):
    return jnp.pad(a, ((0, 0), (0, width - a.shape[1])))


def _gathered_to_full(g):
    return jnp.transpose(g, (1, 0, 2)).reshape(g.shape[1], 4 * g.shape[2])


def _row(a):
    return _pad_lanes(a.reshape(1, -1), 1024)


def _small_pack(nin, cb, fn, al, dt, gn, cqw_shard, cw_shard):
    ad = jnp.concatenate([al.reshape(1, -1), dt.reshape(1, -1)], axis=1)
    rows = [_row(nin), _row(cb), _row(fn), _row(ad), _row(gn), cqw_shard.reshape(3, 1024), _row(cw_shard)]
    out = jnp.concatenate(rows, axis=0)
    return jnp.pad(out, ((0, 16 - out.shape[0]), (0, 0)))


def kernel(x, norm_in_w, w_in, conv_qkv_w, A_log, dt_bias, gdn_norm_w, conv_w, conv_b, w_out, final_norm_w, loss_target, m_norm_in_w, m_w_in, m_conv_qkv_w, m_A_log, m_dt_bias, m_gdn_norm_w, m_conv_w, m_conv_b, m_w_out, m_final_norm_w, v_norm_in_w, v_w_in, v_conv_qkv_w, v_A_log, v_dt_bias, v_gdn_norm_w, v_conv_w, v_conv_b, v_w_out, v_final_norm_w):
    chip = 2 * lax.axis_index("x") + lax.axis_index("y")
    a_shard = _align_shard(jnp.transpose(w_in[0]))
    wo_b = _cast_bf16(w_out[0], 256, "cast_w_out")
    d_model = x.shape[-1]
    stack = lambda s: lax.empty((4,) + s.shape, s.dtype)
    wg0 = lax.empty((WG_BLOCKS * DH, d_model), BF16)
    wc0 = lax.empty((WC_BLOCKS * DH, d_model), BF16)
    ss_g, rs_g, bufs_g, tok_g = _gather_start("g", a_shard, wg0, [(conv_qkv_w[0], stack(conv_qkv_w[0]))])
    ss_c, rs_c, bufs_c, tok_c = _gather_start("c", bufs_g[0], wc0,
                                              [(conv_w[0], stack(conv_w[0])), (wo_b, stack(wo_b))])
    wg1, wc1, wog1, cqg1, cwg1 = _place_own(bufs_c[0], bufs_c[4], bufs_g[2], bufs_c[2],
                                            [bufs_g[1], bufs_c[1], bufs_c[5], bufs_g[3], bufs_c[3]])
    x0 = x[0]
    h = _tie(_tie(_rms_in(x0, norm_in_w), tok_g, "after_gather_start_g"), tok_c, "after_gather_start_c")
    a_thru, wg, _, cq_g = _gather_wait("g", ss_g, rs_g, [bufs_c[0], wg1, bufs_g[2], cqg1], h)
    w_g = _merge_edges(_sibling_forward("g", wg), G_EDGE, G_MIXED, "merge_edges_g")
    cqw = _gathered_to_full(cq_g)
    ad = jnp.pad(jnp.concatenate([A_log, dt_bias], axis=0), ((0, 0), (A_LANE, 0)))

    def late(o):
        _, wc, _, cw_g, _, wo_g = _gather_wait("c", ss_c, rs_c,
                                               [a_thru, wc1, bufs_c[2], cwg1, bufs_c[4], wog1], o)
        return (_merge_edges(_sibling_forward("c", wc), C_EDGE, C_MIXED, "merge_edges_c"),
                wo_g.reshape(2 * GW, d_model),
                _gathered_to_full(cw_g))

    scat = {}

    def on_grad_c(g_c, g_wout, do):
        go4 = g_wout.reshape(4, GW // 2, d_model)
        land = lax.empty((3, ALIGNED_W, d_model), BF16)
        land_o = lax.empty((3, GW // 2, d_model), BF16)
        ss, rs, bufs, tok = _scatter_start("c", g_c, land, [(go4, land_o)])
        scat["c"] = (ss, rs, bufs)
        return _tie(do, tok, "after_scatter_start_c")

    def on_grad_g(g_g, dproj_g):
        ss, rs, bufs, tok = _scatter_start("g", _pair_reduce("g", g_g), scat["c"][2][1], [], halved=True)
        scat["g"] = (ss, rs, bufs)
        return _tie(dproj_g, tok, "after_scatter_start_g")

    gx, sm, _ = _local_step(x0, loss_target[0], h, w_g, cqw, late, norm_in_w, ad, gdn_norm_w, conv_b,
                            final_norm_w.reshape(1, -1), on_grad_c, on_grad_g)

    ss, rs, bufs = scat["g"]
    g_g, land = _scatter_wait("g", ss, rs, bufs, gx, halved=True)
    ss, rs, bufs = scat["c"]
    g_c, land, go4, land_o = _scatter_wait("c", ss, rs, [bufs[0], land, bufs[2], bufs[3]], gx)
    part_in = _sum_shard(g_g, g_c, land)
    part_out = _sum_rows(go4, land_o, 128)
    ad_g = jnp.concatenate([sm["al"][:, A_LANE:], sm["dt"][:, A_LANE:]], axis=1)
    pack = jnp.concatenate([_row(sm["nin"]), _row(sm["cb"]), _row(sm["fn"]), _row(ad_g), _row(sm["gn"]),
                            jnp.concatenate(sm["cq"], axis=1).reshape(12, 1024), sm["cw"], _row(sm["loss"])], axis=0)
    pack = jnp.pad(pack, ((0, PACK_ROWS - pack.shape[0]), (0, 0)))
    sib_in, sib_out, packs = _final_exchange([part_in, part_out], pack)
    tot = _sum_packs(packs)

    g_wi, d_wi, m_wi, v_wi = [jnp.transpose(a) for a in _adamw_shard(
        jnp.transpose(w_in[0]), jnp.transpose(m_w_in[0]), jnp.transpose(v_w_in[0]), part_in, sib_in)]
    g_wo, d_wo, m_wo, v_wo = _adamw(w_out[0], m_w_out[0], v_w_out[0], part_out, sib_out, 128, "adamw_w_out")
    g_cq_sh = lax.dynamic_slice_in_dim(tot[R_CQ:R_CQ + 12].reshape(4, 3 * GW), chip * 768, 768, axis=1)
    g_cw_sh = lax.dynamic_slice_in_dim(tot[R_CW:R_CW + 3], chip * 256, 256, axis=1)
    sp = lambda nin, cb, fn, al, dt, gn, cq, cwv: _small_pack(nin, cb, fn, al, dt, gn, cq[0], cwv[0])
    g_s = _small_pack(tot[R_NIN], tot[R_CB], tot[R_FN], tot[R_AD, :HEADS], tot[R_AD, HEADS:2 * HEADS],
                      tot[R_GN, :DH], g_cq_sh, g_cw_sh)
    w_s = sp(norm_in_w, conv_b, final_norm_w, A_log, dt_bias, gdn_norm_w, conv_qkv_w, conv_w)
    m_s = sp(m_norm_in_w, m_conv_b, m_final_norm_w, m_A_log, m_dt_bias, m_gdn_norm_w, m_conv_qkv_w, m_conv_w)
    v_s = sp(v_norm_in_w, v_conv_b, v_final_norm_w, v_A_log, v_dt_bias, v_gdn_norm_w, v_conv_qkv_w, v_conv_w)
    small = _adamw(w_s, m_s, v_s, g_s, None, 16, "adamw_small")

    def unpack(a, big_in, big_out):
        return (a[0:1], big_in[None], a[5:8].reshape(1, 4, 768), a[3:4, :HEADS], a[3:4, HEADS:2 * HEADS],
                a[4:5, :DH], a[8, :768].reshape(1, 3, 256), a[1:2], big_out[None], a[2])

    loss = tot[R_LOSS, 0]
    return (loss, gx[None], *unpack(small[0], g_wi, g_wo), *unpack(small[1], d_wi, d_wo),
            *unpack(small[2], m_wi, m_wo), *unpack(small[3], v_wi, v_wo))
```

```python
import functools
import math

import jax
import jax.numpy as jnp
from jax import lax
from jax.experimental import pallas as pl
from jax.experimental.pallas import tpu as pltpu

F32 = jnp.float32
BF16 = jnp.bfloat16
MESH = pl.DeviceIdType.MESH
ANY = pl.BlockSpec(memory_space=pl.ANY)

HEADS = 8
DH = 128
CH = 64
GW = HEADS * DH
EPS = 1e-6
VMEM_V7X = 64 * 1024 * 1024

QB, KB, VB, ZB, BAB = 0, 8, 16, 24, 32
A_LANE = 120
NG, NC = 33, 32
GW_COLS, CW_COLS = NG * DH, NC * DH

SHARD_W = 2052
ALIGNED_BLOCKS = 17
ALIGNED_W = ALIGNED_BLOCKS * DH
SHIFTS = (0, 4, ALIGNED_W - 8, ALIGNED_W - 4)
G_EDGE, C_EDGE = 34, 32
G_SPARE, C_SPARE = 33, 34
WG_BLOCKS, WC_BLOCKS = 38, 36
G_MIXED, C_MIXED = (2, BAB), (4 * 7 + 1,)


def _shard_blocks(chip, edges):
    g, c = "g", "c"
    if chip == 0:
        out = [(g, 3 * b) for b in range(8)] + [(g, 3 * b + 1) for b in range(8)] + [(g, G_EDGE, G_MIXED[0])]
    elif chip == 1:
        out = [(g, G_EDGE + 1, G_MIXED[0])] + [(g, 3 * b + 2) for b in range(1, 8)]
        out += [(g, ZB + b) for b in range(8)] + [(g, G_EDGE + 2, G_MIXED[1])]
    elif chip == 2:
        out = [(c, 4 * b) for b in range(8)] + [(c, 4 * b + 1) for b in range(7)]
        out += [(c, C_EDGE, C_MIXED[0]), (g, G_EDGE + 3, G_MIXED[1])]
    else:
        out = [(c, 4 * b + 2) for b in range(8)] + [(c, 4 * b + 3) for b in range(8)] + [(c, C_EDGE + 1, C_MIXED[0])]
    return [(o[0], o[1] if (edges or len(o) == 2) else o[2]) for o in out]


def _by_chip(chip, vals):
    if all(v == vals[0] for v in vals):
        return vals[0]
    r = vals[3]
    for kk in (2, 1, 0):
        r = jnp.where(chip == kk, vals[kk], r)
    return r

ADAM_LR, ADAM_B1, ADAM_B2, ADAM_EPS, ADAM_WD, ADAM_STEP = 0.001, 0.9, 0.999, 1e-08, 0.01, 10

R_NIN, R_CB, R_FN, R_AD, R_GN, R_CQ, R_CW, R_LOSS, PACK_ROWS = 0, 1, 2, 3, 4, 5, 17, 20, 24

NN = ((1,), (0,))
NT = ((1,), (1,))
TN = ((0,), (0,))


def _dot(a, b, dims=NN, mode="lo"):
    dn = (dims, ((), ()))
    if mode == "hi":
        return lax.dot_general(a, b, dn, precision=lax.Precision.HIGHEST, preferred_element_type=F32)
    ah, bh = a.astype(BF16), b.astype(BF16)
    out = lax.dot_general(ah, bh, dn, preferred_element_type=F32)
    if mode == "x3":
        al = (a - ah.astype(F32)).astype(BF16)
        bl = (b - bh.astype(F32)).astype(BF16)
        out = out + lax.dot_general(ah, bl, dn, preferred_element_type=F32)
        out = out + lax.dot_general(al, bh, dn, preferred_element_type=F32)
    return out


P_GRAM, P_INV, P_SOL, P_SCAN, P_SCANB, P_BWD = "lo", "lo", "lo", "lo", "lo", "lo"
P_CUM = "x3"


def _params(sem=None, vmem=None):
    kw = {}
    if sem is not None:
        kw["dimension_semantics"] = sem
    if vmem is not None:
        kw["vmem_limit_bytes"] = int(min(max(vmem, 32 * 2**20), VMEM_V7X - 8 * 2**20))
    return pltpu.CompilerParams(**kw)


def _sigmoid(x):
    return 1.0 / (1.0 + jnp.exp(-x))


def _dsilu(x, s):
    return s * (1.0 + x * (1.0 - s))


def _rows(shape):
    return lax.broadcasted_iota(jnp.int32, shape, 0)


def _shift_down(x, s):
    if s == 0:
        return x
    return jnp.where(_rows(x.shape) >= s, pltpu.roll(x, s, 0), 0.0)


def _shift_up(x, s):
    if s == 0:
        return x
    n = x.shape[0]
    return jnp.where(_rows(x.shape) < n - s, pltpu.roll(x, n - s, 0), 0.0)


def _matmul(a, b, dims, out_dtype, tm, tn, tk, name, add=None, n=None):
    if dims == NN:
        (m, k), n = a.shape, b.shape[1]
    elif dims == NT:
        (m, k), n = a.shape, (n or b.shape[0])
    else:
        (k, m), n = a.shape, b.shape[1]
    tm, tn, tk = min(tm, m), min(tn, n), min(tk, k)
    assert m % tm == 0 and n % tn == 0 and k % tk == 0, (name, m, n, k, tm, tn, tk)
    nk = k // tk

    def body(*refs):
        if add is None:
            a_ref, b_ref, o_ref = refs[:3]
            add_ref = None
        else:
            a_ref, b_ref, add_ref, o_ref = refs[:4]
        part = _dot(a_ref[...], b_ref[...], dims)
        if nk == 1:
            if add_ref is not None:
                part = part + add_ref[...]
            o_ref[...] = part.astype(out_dtype)
            return
        acc = refs[-1]
        kk = pl.program_id(2)

        @pl.when(kk == 0)
        def _():
            acc[...] = part

        @pl.when(kk > 0)
        def _():
            acc[...] += part

        @pl.when(kk == nk - 1)
        def _():
            r = acc[...]
            if add_ref is not None:
                r = r + add_ref[...]
            o_ref[...] = r.astype(out_dtype)

    if dims == TN:
        a_spec = pl.BlockSpec((tk, tm), lambda i, j, kk: (kk, i))
    else:
        a_spec = pl.BlockSpec((tm, tk), lambda i, j, kk: (i, kk))
    if dims == NT:
        b_spec = pl.BlockSpec((tn, tk), lambda i, j, kk: (j, kk))
    else:
        b_spec = pl.BlockSpec((tk, tn), lambda i, j, kk: (kk, j))
    o_spec = pl.BlockSpec((tm, tn), lambda i, j, kk: (i, j))
    in_specs = [a_spec, b_spec]
    args = [a, b]
    if add is not None:
        in_specs.append(o_spec)
        args.append(add)
    osz = jnp.dtype(out_dtype).itemsize
    est = 2 * (tm * tk * a.dtype.itemsize + tk * tn * b.dtype.itemsize + tm * tn * osz)
    est += 3 * tm * tn * 4 + (2 * tm * tn * 4 if add is not None else 0)
    return pl.pallas_call(
        body, name=name, grid=(m // tm, n // tn, nk),
        in_specs=in_specs, out_specs=o_spec,
        out_shape=jax.ShapeDtypeStruct((m, n), out_dtype),
        scratch_shapes=[pltpu.VMEM((tm, tn), F32)] if nk > 1 else [],
        compiler_params=_params(("parallel", "parallel", "arbitrary"), est + 8 * 2**20),
    )(*args)


def _cast_bf16(a, rows, name):
    r, c = a.shape
    rows = min(rows, r)

    def body(a_ref, o_ref):
        o_ref[...] = a_ref[...].astype(BF16)

    return pl.pallas_call(
        body, name=name, grid=(r // rows,),
        in_specs=[pl.BlockSpec((rows, c), lambda i: (i, 0))],
        out_specs=pl.BlockSpec((rows, c), lambda i: (i, 0)),
        out_shape=jax.ShapeDtypeStruct((r, c), BF16),
        compiler_params=_params(("parallel",)),
    )(a)


def _align_shard(wt):
    r, d = wt.shape
    cols = min(256, d)

    def body(w_ref, o_ref, pad_ref):
        chip = 2 * lax.axis_index("x") + lax.axis_index("y")
        pad_ref[...] = jnp.zeros_like(pad_ref)
        pad_ref[0:r, :] = w_ref[...]
        o_ref[...] = pltpu.roll(pad_ref[...], _by_chip(chip, SHIFTS), 0).astype(BF16)

    return pl.pallas_call(
        body, name="align_shard", grid=(d // cols,),
        in_specs=[pl.BlockSpec((r, cols), lambda i: (0, i))],
        out_specs=pl.BlockSpec((ALIGNED_W, cols), lambda i: (0, i)),
        out_shape=jax.ShapeDtypeStruct((ALIGNED_W, d), BF16),
        scratch_shapes=[pltpu.VMEM((ALIGNED_W, cols), F32)],
        compiler_params=_params(("parallel",)),
    )(wt)


def _rms_in(x, w):
    n, d = x.shape
    tr = min(256, n)

    def body(x_ref, w_ref, h_ref):
        xv = x_ref[...]
        r = lax.rsqrt(jnp.mean(xv * xv, axis=-1, keepdims=True) + EPS)
        h_ref[...] = (xv * r * w_ref[...]).astype(BF16)

    return pl.pallas_call(
        body, name="rms_in", grid=(n // tr,),
        in_specs=[pl.BlockSpec((tr, d), lambda i: (i, 0)), pl.BlockSpec((1, d), lambda i: (0, 0))],
        out_specs=pl.BlockSpec((tr, d), lambda i: (i, 0)),
        out_shape=jax.ShapeDtypeStruct((n, d), BF16),
        compiler_params=_params(("parallel",)),
    )(x, w)


def _conv_silu(p, w_ref, taps):
    c = None
    for j in range(taps):
        t = _shift_down(p, taps - 1 - j) * w_ref[j:j + 1, :]
        c = t if c is None else c + t
    return c


def _prep_qkv(proj, cw):
    n = proj.shape[0]

    def body(p3, wq, wk, wv, q_ref, k_ref, v_ref):
        for kind, (w_ref, o_ref) in enumerate(((wq, q_ref), (wk, k_ref), (wv, v_ref))):
            c = _conv_silu(p3[:, kind * DH:(kind + 1) * DH], w_ref, 4)
            a = c * _sigmoid(c)
            if kind < 2:
                r = lax.rsqrt(jnp.sum(a * a, axis=-1, keepdims=True) + EPS)
                a = a * (r * (DH ** -0.5 if kind == 0 else 1.0))
            o_ref[...] = a

    col = pl.BlockSpec((n, DH), lambda h: (0, h))
    wcol = lambda base: pl.BlockSpec((4, DH), lambda h: (0, base + h))
    out = jax.ShapeDtypeStruct((n, GW), F32)
    return pl.pallas_call(
        body, name="prep_qkv", grid=(HEADS,),
        in_specs=[pl.BlockSpec((n, 3 * DH), lambda h: (0, h)), wcol(QB), wcol(KB), wcol(VB)],
        out_specs=[col] * 3, out_shape=[out] * 3,
        compiler_params=_params(("parallel",), 40 * 2**20),
    )(proj, cw, cw, cw)


def _prep_qkv_bwd(proj, cw, dq, dk, dv, dproj):
    n = proj.shape[0]

    def body(p3, wq, wk, wv, dq_ref, dk_ref, dv_ref, _, o3, gq, gk, gv):
        for kind, (w_ref, d_ref, g_ref) in enumerate(((wq, dq_ref, gq), (wk, dk_ref, gk), (wv, dv_ref, gv))):
            p = p3[:, kind * DH:(kind + 1) * DH]
            shifted = [_shift_down(p, 3 - j) for j in range(4)]
            c = shifted[0] * w_ref[0:1, :]
            for j in range(1, 4):
                c = c + shifted[j] * w_ref[j:j + 1, :]
            s = _sigmoid(c)
            a = c * s
            d = d_ref[...]
            if kind < 2:
                r = lax.rsqrt(jnp.sum(a * a, axis=-1, keepdims=True) + EPS)
                sc = DH ** -0.5 if kind == 0 else 1.0
                d = (sc * r) * (d - a * ((r * r) * jnp.sum(d * a, axis=-1, keepdims=True)))
            dc = d * _dsilu(c, s)
            dp = None
            for j in range(4):
                g_ref[j:j + 1, :] = jnp.sum(dc * shifted[j], axis=0, keepdims=True)
                t = _shift_up(dc, 3 - j) * w_ref[j:j + 1, :]
                dp = t if dp is None else dp + t
            o3[:, kind * DH:(kind + 1) * DH] = dp.astype(BF16)

    col = pl.BlockSpec((n, DH), lambda h: (0, h))
    wcol = lambda base: pl.BlockSpec((4, DH), lambda h: (0, base + h))
    p3spec = pl.BlockSpec((n, 3 * DH), lambda h: (0, h))
    return pl.pallas_call(
        body, name="prep_qkv_bwd", grid=(HEADS,),
        in_specs=[p3spec, wcol(QB), wcol(KB), wcol(VB), col, col, col, ANY],
        out_specs=[p3spec] + [wcol(0)] * 3,
        out_shape=[jax.ShapeDtypeStruct(dproj.shape, BF16)] + [jax.ShapeDtypeStruct((4, GW), F32)] * 3,
        input_output_aliases={7: 0},
        compiler_params=_params(("parallel",), 48 * 2**20),
    )(proj, cw, cw, cw, dq, dk, dv, dproj)


CPB = 8
SCAN_CPS = 4


def _tri(lower, rows):
    i = lax.broadcasted_iota(jnp.int32, (rows, rows), 0)
    j = lax.broadcasted_iota(jnp.int32, (rows, rows), 1)
    return jnp.where((i // CH == j // CH) & ((i >= j) if lower else (j >= i)), 1.0, 0.0)


def _lane(shape):
    return lax.broadcasted_iota(jnp.int32, shape, 1)


def _prep_bg(proj, ad):
    n = proj.shape[0]
    nch = n // CH
    cpb = CPB if nch % CPB == 0 else 1
    rows = cpb * CH

    def body(p_ref, ad_ref, bg_ref, bgt_ref):
        p = p_ref[...]
        lane = _lane(p.shape)
        beta = _sigmoid(p)
        xa = p + ad_ref[1:2, :]
        sp = jnp.maximum(xa, 0.0) + jnp.log(1.0 + jnp.exp(-jnp.abs(xa)))
        g = pltpu.roll(-jnp.exp(ad_ref[0:1, :]) * sp, DH - A_LANE + HEADS, 1)
        gc = _dot(_tri(True, rows), g, NN, P_CUM)
        bg = jnp.where(lane < HEADS, beta, jnp.where(lane < 2 * HEADS, gc, 0.0))
        bg_ref[...] = bg
        for ci in range(cpb):
            bgt_ref[ci] = bg[ci * CH:(ci + 1) * CH, :].T

    return pl.pallas_call(
        body, name="prep_bg", grid=(nch // cpb,),
        in_specs=[pl.BlockSpec((rows, DH), lambda i: (i, BAB)), pl.BlockSpec((2, DH), lambda i: (0, 0))],
        out_specs=[pl.BlockSpec((rows, DH), lambda i: (i, 0)), pl.BlockSpec((cpb, DH, CH), lambda i: (i, 0, 0))],
        out_shape=[jax.ShapeDtypeStruct((n, DH), F32), jax.ShapeDtypeStruct((nch, DH, CH), F32)],
        compiler_params=_params(("parallel",)),
    )(proj, ad)


def _prep_bg_bwd(proj, ad, dbg, dproj):
    n = proj.shape[0]
    nch = n // CH
    cpb = CPB if nch % CPB == 0 else 1
    rows = cpb * CH

    def body(p_ref, ad_ref, d_ref, _, o_ref, ga_ref, gd_ref):
        p = p_ref[...]
        d = d_ref[...]
        lane = _lane(p.shape)
        beta = _sigmoid(p)
        xa = p + ad_ref[1:2, :]
        sp = jnp.maximum(xa, 0.0) + jnp.log(1.0 + jnp.exp(-jnp.abs(xa)))
        na = -jnp.exp(ad_ref[0:1, :])
        dg = pltpu.roll(_dot(_tri(False, rows), d, NN, P_CUM), A_LANE - HEADS, 1)
        da = dg * na * _sigmoid(xa)
        is_g = lane >= A_LANE
        o_ref[...] = jnp.where(lane < HEADS, d * beta * (1.0 - beta), jnp.where(is_g, da, 0.0)).astype(BF16)
        ga = jnp.sum(jnp.where(is_g, dg * na * sp, 0.0), axis=0, keepdims=True)
        gd = jnp.sum(jnp.where(is_g, da, 0.0), axis=0, keepdims=True)

        @pl.when(pl.program_id(0) == 0)
        def _():
            ga_ref[...] = jnp.zeros_like(ga_ref)
            gd_ref[...] = jnp.zeros_like(gd_ref)

        ga_ref[...] += ga
        gd_ref[...] += gd

    one = pl.BlockSpec((1, DH), lambda i: (0, 0))
    return pl.pallas_call(
        body, name="prep_bg_bwd", grid=(nch // cpb,),
        in_specs=[pl.BlockSpec((rows, DH), lambda i: (i, BAB)), pl.BlockSpec((2, DH), lambda i: (0, 0)),
                  pl.BlockSpec((rows, DH), lambda i: (i, 0)), ANY],
        out_specs=[pl.BlockSpec((rows, DH), lambda i: (i, BAB)), one, one],
        out_shape=[jax.ShapeDtypeStruct(dproj.shape, BF16), jax.ShapeDtypeStruct((1, DH), F32),
                   jax.ShapeDtypeStruct((1, DH), F32)],
        input_output_aliases={3: 0},
        compiler_params=_params(("arbitrary",)),
    )(proj, ad, dbg, dproj)


def _gdn_out(o, proj, wg):
    n = o.shape[0]

    def body(o_ref, z_ref, w_ref, y_ref):
        ov, z = o_ref[...], z_ref[...]
        r = lax.rsqrt(jnp.mean(ov * ov, axis=-1, keepdims=True) + EPS)
        y_ref[...] = (ov * r * w_ref[...] * (z * _sigmoid(z))).astype(BF16)

    return pl.pallas_call(
        body, name="gdn_out", grid=(HEADS,),
        in_specs=[pl.BlockSpec((n, DH), lambda h: (0, h)), pl.BlockSpec((n, DH), lambda h: (0, ZB + h)),
                  pl.BlockSpec((1, DH), lambda h: (0, 0))],
        out_specs=pl.BlockSpec((n, DH), lambda h: (0, h)),
        out_shape=jax.ShapeDtypeStruct((n, 2 * GW), BF16),
        compiler_params=_params(("parallel",)),
    )(o, proj, wg)


def _gdn_out_bwd(o, proj, wg, dmix):
    n = o.shape[0]

    def body(o_ref, z_ref, w_ref, d_ref, do_ref, dz_ref, gw_ref):
        ov, z, d, w = o_ref[...], z_ref[...], d_ref[...], w_ref[...]
        r = lax.rsqrt(jnp.mean(ov * ov, axis=-1, keepdims=True) + EPS)
        nrm = ov * r
        s = _sigmoid(z)
        dz_ref[...] = (d * (nrm * w) * _dsilu(z, s)).astype(BF16)
        dn_w = d * (z * s)
        gw = jnp.sum(dn_w * nrm, axis=0, keepdims=True)
        dn = dn_w * w
        do_ref[...] = r * (dn - nrm * jnp.mean(dn * nrm, axis=-1, keepdims=True))

        @pl.when(pl.program_id(0) == 0)
        def _():
            gw_ref[...] = jnp.zeros_like(gw_ref)

        gw_ref[...] += gw

    return pl.pallas_call(
        body, name="gdn_out_bwd", grid=(HEADS,),
        in_specs=[pl.BlockSpec((n, DH), lambda h: (0, h)), pl.BlockSpec((n, DH), lambda h: (0, ZB + h)),
                  pl.BlockSpec((1, DH), lambda h: (0, 0)), pl.BlockSpec((n, DH), lambda h: (0, h))],
        out_specs=[pl.BlockSpec((n, DH), lambda h: (0, h)), pl.BlockSpec((n, DH), lambda h: (0, ZB + h)),
                   pl.BlockSpec((1, DH), lambda h: (0, 0))],
        out_shape=[jax.ShapeDtypeStruct((n, GW), F32), jax.ShapeDtypeStruct((n, GW_COLS), BF16),
                   jax.ShapeDtypeStruct((1, DH), F32)],
        compiler_params=_params(("arbitrary",)),
    )(o, proj, wg, dmix)


def _conv_branch(proj, w3, b, mix):
    n = proj.shape[0]

    def body(p4, w_ref, b_ref, _, y_ref):
        u = p4[:, DH:2 * DH] * p4[:, 2 * DH:3 * DH]
        cc = _conv_silu(u, w_ref, 3) + b_ref[...]
        z = p4[:, 3 * DH:4 * DH]
        y_ref[...] = (p4[:, 0:DH] * cc * (z * _sigmoid(z))).astype(BF16)

    return pl.pallas_call(
        body, name="conv_branch", grid=(HEADS,),
        in_specs=[pl.BlockSpec((n, 4 * DH), lambda h: (0, h)), pl.BlockSpec((3, DH), lambda h: (0, h)),
                  pl.BlockSpec((1, DH), lambda h: (0, h)), ANY],
        out_specs=pl.BlockSpec((n, DH), lambda h: (0, HEADS + h)),
        out_shape=jax.ShapeDtypeStruct(mix.shape, BF16),
        input_output_aliases={3: 0},
        compiler_params=_params(("parallel",), 40 * 2**20),
    )(proj, w3, b, mix)


def _conv_branch_bwd(proj, w3, b, dmix):
    n = proj.shape[0]

    def body(p4, w_ref, b_ref, d_ref, o4, gw_ref, gbias_ref):
        gb, gcv, hc, z = p4[:, 0:DH], p4[:, DH:2 * DH], p4[:, 2 * DH:3 * DH], p4[:, 3 * DH:4 * DH]
        d = d_ref[...]
        dgb, dgc, dhc, dzc = (o4.at[:, kk * DH:(kk + 1) * DH] for kk in range(4))
        u = gcv * hc
        cc = _conv_silu(u, w_ref, 3) + b_ref[...]
        s = _sigmoid(z)
        dzc[...] = (d * (gb * cc) * _dsilu(z, s)).astype(BF16)
        dp = d * (z * s)
        dgb[...] = (dp * cc).astype(BF16)
        dcc = dp * gb
        gbias_ref[...] = jnp.sum(dcc, axis=0, keepdims=True)
        du = None
        for j in range(3):
            gw_ref[j:j + 1, :] = jnp.sum(dcc * _shift_down(u, 2 - j), axis=0, keepdims=True)
            t = _shift_up(dcc, 2 - j) * w_ref[j:j + 1, :]
            du = t if du is None else du + t
        dgc[...] = (du * hc).astype(BF16)
        dhc[...] = (du * gcv).astype(BF16)

    p4spec = pl.BlockSpec((n, 4 * DH), lambda h: (0, h))
    return pl.pallas_call(
        body, name="conv_branch_bwd", grid=(HEADS,),
        in_specs=[p4spec, pl.BlockSpec((3, DH), lambda h: (0, h)), pl.BlockSpec((1, DH), lambda h: (0, h)),
                  pl.BlockSpec((n, DH), lambda h: (0, HEADS + h))],
        out_specs=[p4spec, pl.BlockSpec((3, DH), lambda h: (0, h)), pl.BlockSpec((1, DH), lambda h: (0, h))],
        out_shape=[jax.ShapeDtypeStruct((n, CW_COLS), BF16), jax.ShapeDtypeStruct((3, GW), F32),
                   jax.ShapeDtypeStruct((1, GW), F32)],
        compiler_params=_params(("parallel",), 48 * 2**20),
    )(proj, w3, b, dmix)


def _final_loss(out, tgt, wf):
    n, d = out.shape
    tr = min(256, n)

    def body(o_ref, t_ref, w_ref, do_ref, dob_ref, gw_ref, loss_ref):
        ov, w = o_ref[...], w_ref[...]
        r = lax.rsqrt(jnp.mean(ov * ov, axis=-1, keepdims=True) + EPS)
        nrm = ov * r
        e = nrm * w - t_ref[...]
        dy = e * (1.0 / d)
        dn = dy * w
        dout = r * (dn - nrm * jnp.mean(dn * nrm, axis=-1, keepdims=True))
        do_ref[...] = dout
        dob_ref[...] = dout.astype(BF16)

        @pl.when(pl.program_id(0) == 0)
        def _():
            gw_ref[...] = jnp.zeros_like(gw_ref)
            loss_ref[...] = jnp.zeros_like(loss_ref)

        gw_ref[...] += jnp.sum(dy * nrm, axis=0, keepdims=True)
        loss_ref[...] += (0.5 / d) * jnp.sum(jnp.sum(e * e, axis=-1, keepdims=True), axis=0, keepdims=True)

    row = pl.BlockSpec((tr, d), lambda i: (i, 0))
    return pl.pallas_call(
        body, name="final_loss", grid=(n // tr,),
        in_specs=[row, row, pl.BlockSpec((1, d), lambda i: (0, 0))],
        out_specs=[row, row, pl.BlockSpec((1, d), lambda i: (0, 0)), pl.BlockSpec((1, 1), lambda i: (0, 0))],
        out_shape=[jax.ShapeDtypeStruct((n, d), F32), jax.ShapeDtypeStruct((n, d), BF16),
                   jax.ShapeDtypeStruct((1, d), F32), jax.ShapeDtypeStruct((1, 1), F32)],
        compiler_params=_params(("arbitrary",)),
    )(out, tgt, wf)


def _rms_in_bwd(x, w, dh, dout):
    n, d = x.shape
    tr = min(256, n)

    def body(x_ref, w_ref, dh_ref, do_ref, dx_ref, gw_ref):
        xv, dhv = x_ref[...], dh_ref[...]
        r = lax.rsqrt(jnp.mean(xv * xv, axis=-1, keepdims=True) + EPS)
        xn = xv * r
        dxn = dhv * w_ref[...]
        dx_ref[...] = r * (dxn - xn * jnp.mean(dxn * xn, axis=-1, keepdims=True)) + do_ref[...]

        @pl.when(pl.program_id(0) == 0)
        def _():
            gw_ref[...] = jnp.zeros_like(gw_ref)

        gw_ref[...] += jnp.sum(dhv * xn, axis=0, keepdims=True)

    row = pl.BlockSpec((tr, d), lambda i: (i, 0))
    one = pl.BlockSpec((1, d), lambda i: (0, 0))
    return pl.pallas_call(
        body, name="rms_in_bwd", grid=(n // tr,),
        in_specs=[row, one, row, row], out_specs=[row, one],
        out_shape=[jax.ShapeDtypeStruct((n, d), F32), jax.ShapeDtypeStruct((1, d), F32)],
        compiler_params=_params(("arbitrary",)),
    )(x, w, dh, dout)


def _ij():
    i = lax.broadcasted_iota(jnp.int32, (CH, CH), 0)
    j = lax.broadcasted_iota(jnp.int32, (CH, CH), 1)
    return i, j


def _unit_lower_inverse(mats):
    i, j = _ij()
    eye = jnp.where(i == j, 1.0, 0.0)
    same16 = (i // 16) == (j // 16)
    same32 = (i // 32) == (j // 32)
    mm = lambda xs, ys: [_dot(x, y, NN, P_INV) for x, y in zip(xs, ys)]
    n1 = [jnp.where(same16, -a, 0.0) for a in mats]
    n2 = mm(n1, n1)
    n4 = mm(n2, n2)
    n8 = mm(n4, n4)
    t = [eye + x1 + x2 + x3 for x1, x2, x3 in zip(n1, n2, mm(n1, n2))]
    t = [x + y for x, y in zip(t, mm(t, n4))]
    t = [x + y for x, y in zip(t, mm(t, n8))]
    a1 = [jnp.where(same32 & jnp.logical_not(same16), a, 0.0) for a in mats]
    t = [x - y for x, y in zip(t, mm(t, mm(a1, t)))]
    a2 = [jnp.where(same32, 0.0, a) for a in mats]
    t = [x - y for x, y in zip(t, mm(t, mm(a2, t)))]
    return t


def _head_vectors(bg, bgt, h):
    bcol = bg[:, h:h + 1]
    gcol = bg[:, HEADS + h:HEADS + h + 1]
    grow = bgt[HEADS + h:HEADS + h + 1, :]
    return bcol, gcol, grow


def _decay(gcol, grow):
    i, j = _ij()
    return jnp.where(i >= j, jnp.exp(jnp.where(i >= j, gcol - grow, 0.0)), 0.0)


def _gdn_intra(q, k, v, bg, bgt):
    n = q.shape[0]
    nch = n // CH
    cps = 4 if nch % 4 == 0 else 1

    def body(q_ref, k_ref, v_ref, bg_ref, bgt_ref, u_ref, w_ref, p_ref, t_ref):
        i, j = _ij()
        items = [(ci, h) for ci in range(cps) for h in range(HEADS)]
        at = lambda ref, ci, h: ref.at[ci * CH:(ci + 1) * CH, h * DH:(h + 1) * DH]
        bgs = [bg_ref[ci * CH:(ci + 1) * CH, :] for ci in range(cps)]
        ks = [at(k_ref, ci, h)[...] for ci, h in items]
        vecs = [_head_vectors(bgs[ci], bgt_ref[ci], h) for ci, h in items]
        decs = [_decay(gcol, grow) for _, gcol, grow in vecs]
        kks = [_dot(kh, kh, NT, P_GRAM) for kh in ks]
        qks = [_dot(at(q_ref, ci, h)[...], kh, NT, P_GRAM) for (ci, h), kh in zip(items, ks)]
        ts = _unit_lower_inverse([jnp.where(i > j, bcol * kk * dec, 0.0)
                                  for (bcol, _, _), kk, dec in zip(vecs, kks, decs)])
        us = [_dot(t, at(v_ref, ci, h)[...] * bcol, NN, P_SOL) for t, (ci, h), (bcol, _, _) in zip(ts, items, vecs)]
        ws = [_dot(t, kh * (bcol * jnp.exp(gcol)), NN, P_SOL) for t, kh, (bcol, gcol, _) in zip(ts, ks, vecs)]
        for n_, (ci, h) in enumerate(items):
            p_ref[ci, h] = qks[n_] * decs[n_]
            t_ref[ci, h] = ts[n_]
            at(u_ref, ci, h)[...] = us[n_]
            at(w_ref, ci, h)[...] = ws[n_]

    row = pl.BlockSpec((cps * CH, GW), lambda c: (c, 0))
    sq = pl.BlockSpec((cps, HEADS, CH, CH), lambda c: (c, 0, 0, 0))
    big = jax.ShapeDtypeStruct((n, GW), F32)
    sqs = jax.ShapeDtypeStruct((nch, HEADS, CH, CH), F32)
    return pl.pallas_call(
        body, name="gdn_intra", grid=(nch // cps,),
        in_specs=[row, row, row, pl.BlockSpec((cps * CH, DH), lambda c: (c, 0)),
                  pl.BlockSpec((cps, DH, CH), lambda c: (c, 0, 0))],
        out_specs=[row, row, sq, sq], out_shape=[big, big, sqs, sqs],
        compiler_params=_params(("parallel",)),
    )(q, k, v, bg, bgt)


def _gdn_scan(q, k, bg, u, w, p):
    n = q.shape[0]
    nch = n // CH
    cps = SCAN_CPS if nch % SCAN_CPS == 0 else 1

    def body(q_ref, k_ref, bg_ref, u_ref, w_ref, p_ref, o_ref, vn_ref, s_out, s_scr):
        @pl.when(pl.program_id(0) == 0)
        def _():
            s_scr[...] = jnp.zeros_like(s_scr)

        hs = range(HEADS)
        sls = [slice(h * DH, (h + 1) * DH) for h in hs]
        ss = [s_scr[h] for h in hs]
        for ci in range(cps):
            rs = slice(ci * CH, (ci + 1) * CH)
            bg = bg_ref[rs, :]
            gcols = [bg[:, HEADS + h:HEADS + h + 1] for h in hs]
            glasts = [g[CH - 1:CH, :] for g in gcols]
            wss = [_dot(w_ref[rs, sl], s, NN, P_SCAN) for sl, s in zip(sls, ss)]
            oqs = [_dot(q_ref[rs, sl] * jnp.exp(g), s, NN, P_SCAN) for sl, s, g in zip(sls, ss, gcols)]
            vns = [u_ref[rs, sl] - x for sl, x in zip(sls, wss)]
            ops = [_dot(p_ref[ci, h], vn, NN, P_SCAN) for h, vn in zip(hs, vns)]
            sns = [_dot(k_ref[rs, sl] * jnp.exp(gl - g), vn, TN, P_SCAN)
                   for sl, gl, g, vn in zip(sls, glasts, gcols, vns)]
            for h, sl in enumerate(sls):
                s_out[ci, :, sl] = ss[h]
                vn_ref[rs, sl] = vns[h]
                o_ref[rs, sl] = oqs[h] + ops[h]
            ss = [s * jnp.exp(gl) + sn for s, gl, sn in zip(ss, glasts, sns)]
        for h in hs:
            s_scr[h] = ss[h]

    row = pl.BlockSpec((cps * CH, GW), lambda c: (c, 0))
    big = jax.ShapeDtypeStruct((n, GW), F32)
    return pl.pallas_call(
        body, name="gdn_scan", grid=(nch // cps,),
        in_specs=[row, row, pl.BlockSpec((cps * CH, DH), lambda c: (c, 0)), row, row,
                  pl.BlockSpec((cps, HEADS, CH, CH), lambda c: (c, 0, 0, 0))],
        out_specs=[row, row, pl.BlockSpec((cps, DH, GW), lambda c: (c, 0, 0))],
        out_shape=[big, big, jax.ShapeDtypeStruct((nch, DH, GW), F32)],
        scratch_shapes=[pltpu.VMEM((HEADS, DH, DH), F32)],
        compiler_params=_params(("arbitrary",)),
    )(q, k, bg, u, w, p)


def _gdn_scan_bwd(q, k, bg, w, p, vn, s_in, do):
    n = q.shape[0]
    nch = n // CH
    cps = SCAN_CPS if nch % SCAN_CPS == 0 else 1
    rev = lambda c: nch // cps - 1 - c

    def body(q_ref, k_ref, bg_ref, w_ref, p_ref, vn_ref, s_ref, do_ref,
             dqg_ref, dp_ref, du_ref, dw_ref, dks_ref, dgam_ref, ds_scr):
        @pl.when(pl.program_id(0) == 0)
        def _():
            ds_scr[...] = jnp.zeros_like(ds_scr)

        lane = _lane((1, DH))
        hs = range(HEADS)
        sls = [slice(h * DH, (h + 1) * DH) for h in hs]
        dss = [ds_scr[h] for h in hs]
        for ci in reversed(range(cps)):
            rs = slice(ci * CH, (ci + 1) * CH)
            bg = bg_ref[rs, :]
            gcols = [bg[:, HEADS + h:HEADS + h + 1] for h in hs]
            glasts = [g[CH - 1:CH, :] for g in gcols]
            ss = [s_ref[ci, :, sl] for sl in sls]
            dos = [do_ref[rs, sl] for sl in sls]
            vnl = [vn_ref[rs, sl] for sl in sls]
            dqgs = [_dot(d, s, NT, P_SCANB) for d, s in zip(dos, ss)]
            dps = [_dot(d, vn, NT, P_SCANB) for d, vn in zip(dos, vnl)]
            dvn1 = [_dot(p_ref[ci, h], d, TN, P_SCANB) for h, d in zip(hs, dos)]
            dvn2 = [_dot(k_ref[rs, sl] * jnp.exp(gl - g), ds, NN, P_SCANB)
                    for sl, gl, g, ds in zip(sls, glasts, gcols, dss)]
            dkss = [_dot(vn, ds, NT, P_SCANB) for vn, ds in zip(vnl, dss)]
            dsq = [_dot(q_ref[rs, sl] * jnp.exp(g), d, TN, P_SCANB) for sl, g, d in zip(sls, gcols, dos)]
            dvns = [a + b for a, b in zip(dvn1, dvn2)]
            dws = [_dot(dvn, s, NT, P_SCANB) for dvn, s in zip(dvns, ss)]
            dsw = [_dot(w_ref[rs, sl], dvn, TN, P_SCANB) for sl, dvn in zip(sls, dvns)]
            dgam = jnp.zeros((1, DH), F32)
            for h, sl in enumerate(sls):
                dqg_ref[rs, sl] = dqgs[h]
                dp_ref[ci, h] = dps[h]
                du_ref[rs, sl] = dvns[h]
                dw_ref[rs, sl] = -dws[h]
                dks_ref[rs, sl] = dkss[h]
                tot = jnp.sum(jnp.sum(dss[h] * ss[h], axis=-1, keepdims=True), axis=0, keepdims=True)
                dgam = dgam + jnp.where(lane == h, tot, 0.0)
            dgam_ref[ci] = jnp.broadcast_to(dgam, (8, DH))
            dss = [ds * jnp.exp(gl) + a - b for ds, gl, a, b in zip(dss, glasts, dsq, dsw)]
        for h in hs:
            ds_scr[h] = dss[h]

    row = pl.BlockSpec((cps * CH, GW), lambda c: (rev(c), 0))
    sq = pl.BlockSpec((cps, HEADS, CH, CH), lambda c: (rev(c), 0, 0, 0))
    big = jax.ShapeDtypeStruct((n, GW), F32)
    return pl.pallas_call(
        body, name="gdn_scan_bwd", grid=(nch // cps,),
        in_specs=[row, row, pl.BlockSpec((cps * CH, DH), lambda c: (rev(c), 0)), row, sq, row,
                  pl.BlockSpec((cps, DH, GW), lambda c: (rev(c), 0, 0)), row],
        out_specs=[row, sq, row, row, row, pl.BlockSpec((cps, 8, DH), lambda c: (rev(c), 0, 0))],
        out_shape=[big, jax.ShapeDtypeStruct((nch, HEADS, CH, CH), F32), big, big, big,
                   jax.ShapeDtypeStruct((nch, 8, DH), F32)],
        scratch_shapes=[pltpu.VMEM((HEADS, DH, DH), F32)],
        compiler_params=_params(("arbitrary",)),
    )(q, k, bg, w, p, vn, s_in, do)


def _gdn_intra_bwd(q, k, v, bg, bgt, t, u, w, p, dqg, dp, du, dw, dks, dgam):
    n = q.shape[0]
    nch = n // CH
    cps = 1

    def body(q_ref, k_ref, v_ref, bg_ref, bgt_ref, t_ref, u_ref, w_ref, p_ref,
             dqg_ref, dp_ref, du_ref, dw_ref, dks_ref, dgam_ref, dq_ref, dk_ref, dv_ref, dbg_ref):
        i, j = _ij()
        rows1 = lax.broadcasted_iota(jnp.int32, (CH, 1), 0)
        lane = _lane((CH, DH))
        rsum = lambda x: jnp.sum(x, axis=-1, keepdims=True)
        items = [(ci, h) for ci in range(cps) for h in range(HEADS)]
        at = lambda ref, it: ref.at[it[0] * CH:(it[0] + 1) * CH, it[1] * DH:(it[1] + 1) * DH]
        ld = lambda ref: [at(ref, it)[...] for it in items]
        bgs = [bg_ref[ci * CH:(ci + 1) * CH, :] for ci in range(cps)]
        qs, ks = ld(q_ref), ld(k_ref)
        vecs = [_head_vectors(bgs[ci], bgt_ref[ci], h) for ci, h in items]
        decs = [_decay(gcol, grow) for _, gcol, grow in vecs]
        ths = [t_ref[ci, h] for ci, h in items]
        drus = [_dot(th, x_, TN, P_BWD) for th, x_ in zip(ths, ld(du_ref))]
        drws = [_dot(th, x_, TN, P_BWD) for th, x_ in zip(ths, ld(dw_ref))]
        kks = [_dot(kh, kh, NT, P_GRAM) for kh in ks]
        da1 = [_dot(dru, x_, NT, P_BWD) for dru, x_ in zip(drus, ld(u_ref))]
        da2 = [_dot(drw, x_, NT, P_BWD) for drw, x_ in zip(drws, ld(w_ref))]
        das = [jnp.where(i > j, -(x_ + y_), 0.0) for x_, y_ in zip(da1, da2)]
        dkks = [da * bcol * dec for da, (bcol, _, _), dec in zip(das, vecs, decs)]
        dps = [dp_ref[ci, h] for ci, h in items]
        dqks = [dp_ * dec for dp_, dec in zip(dps, decs)]
        dq_ps = [_dot(dqk, kh, NN, P_BWD) for dqk, kh in zip(dqks, ks)]
        dk_ps = [_dot(dqk, qh, TN, P_BWD) for dqk, qh in zip(dqks, qs)]
        dk_as = [_dot(dkk, kh, NN, P_BWD) for dkk, kh in zip(dkks, ks)]
        dk_bs = [_dot(dkk, kh, TN, P_BWD) for dkk, kh in zip(dkks, ks)]
        bcols = [vc[0] for vc in vecs]
        gcols = [vc[1] for vc in vecs]
        gams = [jnp.exp(g) for g in gcols]
        glasts = [g[CH - 1:CH, :] for g in gcols]
        es = [jnp.exp(gl - g) for gl, g in zip(glasts, gcols)]
        kgs = [kh * gam for kh, gam in zip(ks, gams)]
        dqgs, dkss = ld(dqg_ref), ld(dks_ref)
        r_uv = [rsum(dru * x_) for dru, x_ in zip(drus, ld(v_ref))]
        r_wk = [rsum(drw * kg) for drw, kg in zip(drws, kgs)]
        r_ak = [rsum(da * kk * dec) for da, kk, dec in zip(das, kks, decs)]
        r_qq = [rsum(dqg * qh) for dqg, qh in zip(dqgs, qs)]
        tks = [rsum(dk_ * kh) * e for dk_, kh, e in zip(dkss, ks, es)]
        mdecs = [da * (bcol * kk * dec) + dp_ * p_ref[ci, h]
                 for (ci, h), da, bcol, kk, dec, dp_ in zip(items, das, bcols, kks, decs, dps)]
        r_md = [rsum(m) for m in mdecs]
        c_md = [rsum(jnp.where(i == j, jnp.sum(m, axis=0, keepdims=True), 0.0)) for m in mdecs]
        dbgs = [jnp.zeros((CH, DH), F32) for _ in range(cps)]
        for n_, (ci, h) in enumerate(items):
            at(dv_ref, (ci, h))[...] = bcols[n_] * drus[n_]
            at(dq_ref, (ci, h))[...] = gams[n_] * dqgs[n_] + dq_ps[n_]
            at(dk_ref, (ci, h))[...] = ((bcols[n_] * gams[n_]) * drws[n_] + dk_ps[n_] + dk_as[n_] + dk_bs[n_]
                                        + dkss[n_] * es[n_])
            dbeta = r_uv[n_] + r_wk[n_] + r_ak[n_]
            dglast = (jnp.sum(tks[n_], axis=0, keepdims=True)
                      + dgam_ref[ci, 0:1, h:h + 1] * jnp.exp(glasts[n_]))
            dgc = (r_wk[n_] * bcols[n_] + r_md[n_] - c_md[n_] + r_qq[n_] * gams[n_] - tks[n_]
                   + jnp.where(rows1 == CH - 1, dglast, 0.0))
            dbgs[ci] = dbgs[ci] + jnp.where(lane == h, dbeta, 0.0) + jnp.where(lane == HEADS + h, dgc, 0.0)
        for ci in range(cps):
            dbg_ref[ci * CH:(ci + 1) * CH, :] = dbgs[ci]

    row = pl.BlockSpec((cps * CH, GW), lambda c: (c, 0))
    sq = pl.BlockSpec((cps, HEADS, CH, CH), lambda c: (c, 0, 0, 0))
    small = pl.BlockSpec((cps * CH, DH), lambda c: (c, 0))
    big = jax.ShapeDtypeStruct((n, GW), F32)
    return pl.pallas_call(
        body, name="gdn_intra_bwd", grid=(nch // cps,),
        in_specs=[row, row, row, small, pl.BlockSpec((cps, DH, CH), lambda c: (c, 0, 0)), sq, row, row, sq,
                  row, sq, row, row, row, pl.BlockSpec((cps, 8, DH), lambda c: (c, 0, 0))],
        out_specs=[row, row, row, small],
        out_shape=[big, big, big, jax.ShapeDtypeStruct((n, DH), F32)],
        compiler_params=_params(("parallel",)),
    )(q, k, v, bg, bgt, t, u, w, p, dqg, dp, du, dw, dks, dgam)


def _local_step(x, tgt, h, w_g, cqw, late, norm_in_w, ad, gdn_norm_w, conv_b, final_norm_w,
                on_grad_c=None, on_grad_g=None, on_q=None):
    proj_g = _matmul(h, w_g, NT, F32, 512, 1408, 1024, "mm_proj_g", n=GW_COLS)
    q, k, v = _prep_qkv(proj_g, cqw)
    if on_q is not None:
        q = on_q(q)
    bg, bgt = _prep_bg(proj_g, ad)
    u, w, p, t = _gdn_intra(q, k, v, bg, bgt)
    o, vn, s_in = _gdn_scan(q, k, bg, u, w, p)
    w_c, w_out, conv_w = late(o)
    proj_c = _matmul(h, w_c, NT, F32, 512, 1024, 1024, "mm_proj_c", n=CW_COLS)
    mix = _conv_branch(proj_c, conv_w, conv_b, _gdn_out(o, proj_g, gdn_norm_w))
    out = _matmul(mix, w_out, NN, F32, 512, 512, 2048, "mm_out", add=x)
    dout, dout_b, g_fn, loss = _final_loss(out, tgt, final_norm_w)

    dmix = _matmul(dout_b, w_out, NT, F32, 512, 1024, 1024, "mm_dmix")
    g_wout = _matmul(mix, dout_b, TN, BF16, 512, 512, 2048, "mm_gwout")
    do, dproj_g, g_gn = _gdn_out_bwd(o, proj_g, gdn_norm_w, dmix)
    dproj_c, g_cw, g_cb = _conv_branch_bwd(proj_c, conv_w, conv_b, dmix)
    g_c = _matmul(dproj_c, h, TN, BF16, 1024, 512, 2048, "mm_gwin_c")
    if on_grad_c is not None:
        do = on_grad_c(g_c, g_wout, do)
    dqg, dp, du, dw, dks, dgam = _gdn_scan_bwd(q, k, bg, w, p, vn, s_in, do)
    dq, dk, dv, dbg = _gdn_intra_bwd(q, k, v, bg, bgt, t, u, w, p, dqg, dp, du, dw, dks, dgam)
    dproj_g, gq, gk, gv = _prep_qkv_bwd(proj_g, cqw, dq, dk, dv, dproj_g)
    dproj_g, g_al, g_dt = _prep_bg_bwd(proj_g, ad, dbg, dproj_g)
    g_g = _matmul(dproj_g, h, TN, BF16, 1408, 512, 2048, "mm_gwin_g")
    if on_grad_g is not None:
        dproj_g = on_grad_g(g_g, dproj_g)
    dh = _matmul(dproj_g, w_g, NN, F32, 1024, 1024, 1408, "mm_dh_g")
    dh = _matmul(dproj_c, w_c, NN, F32, 1024, 1024, 1024, "mm_dh_c", add=dh)
    gx, g_nin = _rms_in_bwd(x, norm_in_w, dh, dout)
    small = dict(nin=g_nin, cb=g_cb, fn=g_fn, al=g_al, dt=g_dt, gn=g_gn, cq=(gq, gk, gv), cw=g_cw, loss=loss)
    return gx, small, (g_g, g_c, g_wout)


def _place():
    x, y, c = lax.axis_index("x"), lax.axis_index("y"), lax.axis_index("c")
    chips = [(1 - x, y), (x, 1 - y), (1 - x, 1 - y)]
    return x, y, c, chips


def _blk(ref, b):
    if isinstance(b, int):
        return ref.at[b * DH:(b + 1) * DH, :]
    return ref.at[pl.ds(pl.multiple_of(b * DH, DH), DH), :]


HBM = pl.BlockSpec(memory_space=pltpu.HBM)
SEM = pl.BlockSpec(memory_space=pltpu.SEMAPHORE)
EFFECT = pltpu.SideEffectType.DATAFLOW_SIDE_EFFECTING


def _split_start(name, issue, bufs, n_sems):
    nbuf = len(bufs)

    def body(*refs):
        issue(refs[:nbuf], refs[nbuf], refs[nbuf + 1])
        refs[-1][...] = jnp.zeros_like(refs[-1])

    out = pl.pallas_call(
        body, name=name,
        out_shape=(pltpu.SemaphoreType.DMA((n_sems,)), pltpu.SemaphoreType.DMA((n_sems,)),
                   *[pltpu.HBM(b.shape, b.dtype) for b in bufs], jax.ShapeDtypeStruct((8, DH), F32)),
        in_specs=[HBM] * nbuf,
        out_specs=(SEM, SEM, *[HBM] * nbuf, pl.BlockSpec(memory_space=pltpu.VMEM)),
        input_output_aliases={a: 2 + a for a in range(nbuf)},
        compiler_params=pltpu.CompilerParams(has_side_effects=EFFECT),
    )(*[pltpu.with_memory_space_constraint(b, pltpu.HBM) for b in bufs])
    return out[0], out[1], list(out[2:2 + nbuf]), out[-1]


def _split_wait(name, await_, send_sems, recv_sems, bufs, after):
    nbuf = len(bufs)

    def body(*refs):
        await_(refs[:nbuf], refs[nbuf], refs[nbuf + 1])

    out = pl.pallas_call(
        body, name=name,
        out_shape=tuple(pltpu.HBM(b.shape, b.dtype) for b in bufs),
        in_specs=[HBM] * nbuf + [SEM, SEM, ANY], out_specs=tuple([HBM] * nbuf),
        input_output_aliases={a: a for a in range(nbuf)},
        compiler_params=pltpu.CompilerParams(has_side_effects=EFFECT),
    )(*bufs, send_sems, recv_sems, after)
    return list(out)


def _phase_blocks(chip, phase, edges, parity=None):
    return [(b, blk) for b, (grp, blk) in enumerate(_shard_blocks(chip, edges))
            if grp == phase and (parity is None or b % 2 == parity)]


def _cols(ref, nblk):
    return ref.at[0:nblk * DH, :]


def _block_table(chip, edges, spare_g, spare_c):
    rows = []
    for s in range(4):
        sb = _shard_blocks(s, edges)
        rows.append([[blk if grp == "g" else spare_g for grp, blk in sb],
                     [blk if grp == "c" else spare_c for grp, blk in sb],
                     [int(grp == "g") for grp, _ in sb], [s] * ALIGNED_BLOCKS])
    return jnp.asarray(rows, jnp.int32)[chip]


def _place_own(a_shard, wo, cq, cw, bufs):
    d = a_shard.shape[1]
    chip = 2 * lax.axis_index("x") + lax.axis_index("y")

    def body(t_ref, a_ref, wo_ref, cq_ref, cw_ref, *refs):
        wg_ref, wc_ref, wog_ref, cqg_ref, cwg_ref = refs[5:]
        wg_ref[...] = a_ref[...]
        wc_ref[...] = a_ref[...]

        @pl.when(pl.program_id(0) == 0)
        def _():
            wog_ref[0] = wo_ref[...]
            cqg_ref[0] = cq_ref[...]
            cwg_ref[0] = cw_ref[...]

    whole = lambda s: pl.BlockSpec(s.shape, lambda b, t: (0,) * s.ndim)
    slot = lambda s: pl.BlockSpec((1,) + s.shape, lambda b, t: (t[3, 0],) + (0,) * s.ndim)
    return pl.pallas_call(
        body, name="place_own",
        grid_spec=pltpu.PrefetchScalarGridSpec(
            num_scalar_prefetch=1, grid=(ALIGNED_BLOCKS,),
            in_specs=[pl.BlockSpec((DH, d), lambda b, t: (b, 0)), whole(wo), whole(cq), whole(cw)] + [ANY] * 5,
            out_specs=[pl.BlockSpec((DH, d), lambda b, t: (t[0, b], 0)),
                       pl.BlockSpec((DH, d), lambda b, t: (t[1, b], 0)), slot(wo), slot(cq), slot(cw)]),
        out_shape=[jax.ShapeDtypeStruct(b.shape, b.dtype) for b in bufs],
        input_output_aliases={5 + a: a for a in range(5)},
        compiler_params=_params(("arbitrary",)),
    )(_block_table(chip, True, G_SPARE, C_SPARE), a_shard, wo, cq, cw, *bufs)


def _tie(x, token, name):
    def body(x_ref, t_ref, o_ref):
        del x_ref, t_ref, o_ref

    return pl.pallas_call(
        body, name=name, in_specs=[ANY, ANY], out_specs=ANY,
        out_shape=jax.ShapeDtypeStruct(x.shape, x.dtype), input_output_aliases={0: 0},
    )(x, token)


def _gather_start(phase, a_shard, w_grp, singles):
    ns = len(singles)

    def issue(refs, send_sems, recv_sems):
        a_ref, w_ref = refs[0], refs[1]
        x, y, c, chips = _place()
        mine = 2 * x + y
        for jj, (px, py) in enumerate(chips):
            to = dict(device_id=(px, py, c), device_id_type=MESH)
            for a in range(ns):
                pltpu.make_async_remote_copy(
                    src_ref=refs[2 + 2 * a], dst_ref=refs[3 + 2 * a].at[mine],
                    send_sem=send_sems.at[(1 + ns) * jj + 1 + a], recv_sem=recv_sems.at[(1 + ns) * jj + 1 + a],
                    **to).start()
        for s in range(4):
            for par in range(2):
                blocks = _phase_blocks(s, phase, True, par)
                if blocks:
                    @pl.when((mine == s) & (c == par))
                    def _():
                        for jj, (px, py) in enumerate(chips):
                            for b, blk in blocks:
                                pltpu.make_async_remote_copy(
                                    src_ref=_blk(a_ref, b), dst_ref=_blk(w_ref, blk),
                                    send_sem=send_sems.at[(1 + ns) * jj], recv_sem=recv_sems.at[(1 + ns) * jj],
                                    device_id=(px, py, c), device_id_type=MESH).start()

    bufs = [a_shard, w_grp] + [t for pair in singles for t in pair]
    return _split_start("gather_start_" + phase, issue, bufs, 3 * (1 + ns))


def _gather_wait(phase, send_sems, recv_sems, bufs, after):
    ns = (len(bufs) - 2) // 2

    def await_(refs, send_sems, recv_sems):
        a_ref, w_ref = refs[0], refs[1]
        x, y, c, chips = _place()
        mine = 2 * x + y
        for jj, (px, py) in enumerate(chips):
            to = dict(device_id=(px, py, c), device_id_type=MESH)
            peer = 2 * px + py
            for a in range(ns):
                cp = pltpu.make_async_remote_copy(
                    src_ref=refs[2 + 2 * a], dst_ref=refs[3 + 2 * a].at[mine],
                    send_sem=send_sems.at[(1 + ns) * jj + 1 + a], recv_sem=recv_sems.at[(1 + ns) * jj + 1 + a], **to)
                cp.wait_recv()
                cp.wait_send()
            for s in range(4):
                for par in range(2):
                    nblk = len(_phase_blocks(s, phase, True, par))
                    if nblk:
                        both = pltpu.make_async_remote_copy(
                            src_ref=_cols(a_ref, nblk), dst_ref=_cols(w_ref, nblk),
                            send_sem=send_sems.at[(1 + ns) * jj], recv_sem=recv_sems.at[(1 + ns) * jj], **to)

                        @pl.when((peer == s) & (c == par))
                        def _():
                            both.wait_recv()

                        @pl.when((mine == s) & (c == par))
                        def _():
                            both.wait_send()

    return _split_wait("gather_wait_" + phase, await_, send_sems, recv_sems, bufs, after)


def _sibling_forward_parts(phase):
    def each(w_ref, send_sems, recv_sems, start):
        x, y, c, chips = _place()
        to = dict(device_id=(x, y, 1 - c), device_id_type=MESH)
        for jj, (px, py) in enumerate(chips):
            peer = 2 * px + py
            for s in range(4):
                for par in range(2):
                    mine_blocks = _phase_blocks(s, phase, True, par)
                    theirs = len(_phase_blocks(s, phase, True, 1 - par))
                    if not (mine_blocks or theirs):
                        continue

                    @pl.when((peer == s) & (c == par))
                    def _():
                        if start:
                            for _, blk in mine_blocks:
                                pltpu.make_async_remote_copy(
                                    src_ref=_blk(w_ref, blk), dst_ref=_blk(w_ref, blk),
                                    send_sem=send_sems.at[jj], recv_sem=recv_sems.at[jj], **to).start()
                            return
                        if theirs:
                            pltpu.make_async_remote_copy(
                                src_ref=_cols(w_ref, theirs), dst_ref=_cols(w_ref, theirs),
                                send_sem=send_sems.at[jj], recv_sem=recv_sems.at[jj], **to).wait_recv()
                        if mine_blocks:
                            pltpu.make_async_remote_copy(
                                src_ref=_cols(w_ref, len(mine_blocks)), dst_ref=_cols(w_ref, len(mine_blocks)),
                                send_sem=send_sems.at[jj], recv_sem=recv_sems.at[jj], **to).wait_send()

    issue = lambda refs, send_sems, recv_sems: each(refs[0], send_sems, recv_sems, True)
    await_ = lambda refs, send_sems, recv_sems: each(refs[0], send_sems, recv_sems, False)
    return issue, await_


def _sibling_forward(phase, w_grp):
    issue, await_ = _sibling_forward_parts(phase)

    def body(w_in_ref, w_ref, send_sems, recv_sems):
        del w_in_ref
        issue([w_ref], send_sems, recv_sems)
        await_([w_ref], send_sems, recv_sems)

    return pl.pallas_call(
        body, name="sibling_forward_" + phase, in_specs=[ANY], out_specs=ANY,
        out_shape=jax.ShapeDtypeStruct(w_grp.shape, w_grp.dtype), input_output_aliases={0: 0},
        scratch_shapes=[pltpu.SemaphoreType.DMA((3,)), pltpu.SemaphoreType.DMA((3,))],
    )(w_grp)


def _merge_edges(w, edge0, mixed, name):
    d = w.shape[1]

    def body(e_ref, o_ref):
        o_ref[...] = e_ref[0:DH, :] + e_ref[DH:2 * DH, :]

    def to_block(i):
        r = mixed[-1]
        for kk in range(len(mixed) - 2, -1, -1):
            r = jnp.where(i == kk, mixed[kk], r)
        return r

    return pl.pallas_call(
        body, name=name, grid=(len(mixed),),
        in_specs=[pl.BlockSpec((2 * DH, d), lambda i: (edge0 // 2 + i, 0))],
        out_specs=pl.BlockSpec((DH, d), lambda i: (to_block(i), 0)),
        out_shape=jax.ShapeDtypeStruct(w.shape, w.dtype),
        input_output_aliases={0: 0},
        compiler_params=_params(("arbitrary",)),
    )(w)


def _scatter_start(phase, g_grp, land, singles, halved=False):
    ns = len(singles)

    def issue(refs, send_sems, recv_sems):
        g_ref, land_ref = refs[0], refs[1]
        x, y, c, chips = _place()
        for jj, (px, py) in enumerate(chips):
            to = dict(device_id=(px, py, c), device_id_type=MESH)
            peer = 2 * px + py
            for a in range(ns):
                pltpu.make_async_remote_copy(
                    src_ref=refs[2 + 2 * a].at[peer], dst_ref=refs[3 + 2 * a].at[jj],
                    send_sem=send_sems.at[(1 + ns) * jj + 1 + a], recv_sem=recv_sems.at[(1 + ns) * jj + 1 + a],
                    **to).start()
            for s in range(4):
                for par in ((0, 1) if halved else (None,)):
                    blocks = _phase_blocks(s, phase, False, par)
                    if blocks:
                        @pl.when((peer == s) if par is None else ((peer == s) & (c == par)))
                        def _():
                            for b, blk in blocks:
                                pltpu.make_async_remote_copy(
                                    src_ref=_blk(g_ref, blk), dst_ref=_blk(land_ref.at[jj], b),
                                    send_sem=send_sems.at[(1 + ns) * jj], recv_sem=recv_sems.at[(1 + ns) * jj],
                                    **to).start()

    bufs = [g_grp, land] + [t for pair in singles for t in pair]
    return _split_start("scatter_start_" + phase, issue, bufs, 3 * (1 + ns))


def _scatter_wait(phase, send_sems, recv_sems, bufs, after, halved=False):
    ns = (len(bufs) - 2) // 2

    def await_(refs, send_sems, recv_sems):
        g_ref, land_ref = refs[0], refs[1]
        x, y, c, chips = _place()
        mine = 2 * x + y
        for jj, (px, py) in enumerate(chips):
            to = dict(device_id=(px, py, c), device_id_type=MESH)
            peer = 2 * px + py
            for a in range(ns):
                cp = pltpu.make_async_remote_copy(
                    src_ref=refs[2 + 2 * a].at[peer], dst_ref=refs[3 + 2 * a].at[jj],
                    send_sem=send_sems.at[(1 + ns) * jj + 1 + a], recv_sem=recv_sems.at[(1 + ns) * jj + 1 + a], **to)
                cp.wait_recv()
                cp.wait_send()
            for s in range(4):
                for par in ((0, 1) if halved else (None,)):
                    nblk = len(_phase_blocks(s, phase, False, par))
                    if nblk:
                        both = pltpu.make_async_remote_copy(
                            src_ref=_cols(g_ref, nblk), dst_ref=_cols(land_ref.at[jj], nblk),
                            send_sem=send_sems.at[(1 + ns) * jj], recv_sem=recv_sems.at[(1 + ns) * jj], **to)

                        @pl.when((mine == s) if par is None else ((mine == s) & (c == par)))
                        def _():
                            both.wait_recv()

                        @pl.when((peer == s) if par is None else ((peer == s) & (c == par)))
                        def _():
                            both.wait_send()

    return _split_wait("scatter_wait_" + phase, await_, send_sems, recv_sems, bufs, after)


def _needed_blocks(phase, parity):
    return sorted({blk for s in range(4) for _, blk in _phase_blocks(s, phase, False, parity)})


def _pair_reduce(phase, g_grp):
    n, d = g_grp.shape

    def swap(g_ref, sib_ref, send_sem, recv_sem):
        x, y, c, _ = _place()
        to = dict(device_id=(x, y, 1 - c), device_id_type=MESH)
        for par in range(2):
            give, get = _needed_blocks(phase, 1 - par), _needed_blocks(phase, par)

            @pl.when(c == par)
            def _():
                for blk in give:
                    pltpu.make_async_remote_copy(src_ref=_blk(g_ref, blk), dst_ref=_blk(sib_ref, blk),
                                                 send_sem=send_sem, recv_sem=recv_sem, **to).start()
                pltpu.make_async_remote_copy(src_ref=_cols(g_ref, len(get)), dst_ref=_cols(sib_ref, len(get)),
                                             send_sem=send_sem, recv_sem=recv_sem, **to).wait_recv()
                pltpu.make_async_remote_copy(src_ref=_cols(g_ref, len(give)), dst_ref=_cols(sib_ref, len(give)),
                                             send_sem=send_sem, recv_sem=recv_sem, **to).wait_send()

    sib = pl.pallas_call(
        swap, name="pair_swap_" + phase, in_specs=[ANY], out_specs=ANY,
        out_shape=jax.ShapeDtypeStruct((n, d), g_grp.dtype),
        scratch_shapes=[pltpu.SemaphoreType.DMA, pltpu.SemaphoreType.DMA],
    )(g_grp)

    lists = [_needed_blocks(phase, par) for par in range(2)]
    longest = max(len(t) for t in lists)
    table = jnp.asarray([t + [t[-1]] * (longest - len(t)) for t in lists], jnp.int32)[lax.axis_index("c")]

    def add(t_ref, a_ref, b_ref, o_ref):
        o_ref[...] = (a_ref[...].astype(F32) + b_ref[...].astype(F32)).astype(o_ref.dtype)

    blk = pl.BlockSpec((DH, d), lambda i, t: (t[i], 0))
    return pl.pallas_call(
        add, name="pair_add_" + phase,
        grid_spec=pltpu.PrefetchScalarGridSpec(num_scalar_prefetch=1, grid=(longest,),
                                               in_specs=[blk, blk], out_specs=blk),
        out_shape=jax.ShapeDtypeStruct((n, d), g_grp.dtype),
        compiler_params=_params(("arbitrary",)),
    )(table, g_grp, sib)


def _sum_shard(g_g, g_c, land):
    d = g_g.shape[1]
    chip = 2 * lax.axis_index("x") + lax.axis_index("y")

    def body(t_ref, gg_ref, gc_ref, land_ref, o_ref):
        b = pl.program_id(0)
        in_g = t_ref[2, b] == 1
        own = jnp.where(in_g, gg_ref[...].astype(F32), gc_ref[...].astype(F32))
        for jj in range(3):
            own = own + land_ref[jj].astype(F32)
        o_ref[...] = jnp.where(in_g & (b % 2 != lax.axis_index("c")), 0.0, own)

    return pl.pallas_call(
        body, name="sum_w_in",
        grid_spec=pltpu.PrefetchScalarGridSpec(
            num_scalar_prefetch=1, grid=(ALIGNED_BLOCKS,),
            in_specs=[pl.BlockSpec((DH, d), lambda b, t: (t[0, b], 0)), pl.BlockSpec((DH, d), lambda b, t: (t[1, b], 0)),
                      pl.BlockSpec((3, DH, d), lambda b, t: (0, b, 0))],
            out_specs=pl.BlockSpec((DH, d), lambda b, t: (b, 0))),
        out_shape=jax.ShapeDtypeStruct((ALIGNED_W, d), F32),
        compiler_params=_params(("arbitrary",)),
    )(_block_table(chip, False, 0, 0), g_g, g_c, land)


def _sum_rows(stack, land, rows):
    _, r, d = stack.shape
    rows = min(rows, r)
    chip = 2 * lax.axis_index("x") + lax.axis_index("y")

    def body(t_ref, own_ref, land_ref, o_ref):
        acc = own_ref[0].astype(F32)
        for jj in range(3):
            acc = acc + land_ref[jj].astype(F32)
        o_ref[...] = acc

    return pl.pallas_call(
        body, name="sum_w_out",
        grid_spec=pltpu.PrefetchScalarGridSpec(
            num_scalar_prefetch=1, grid=(r // rows,),
            in_specs=[pl.BlockSpec((1, rows, d), lambda i, t: (t[0], i, 0)),
                      pl.BlockSpec((3, rows, d), lambda i, t: (0, i, 0))],
            out_specs=pl.BlockSpec((rows, d), lambda i, t: (i, 0))),
        out_shape=jax.ShapeDtypeStruct((r, d), F32),
        compiler_params=_params(("arbitrary",)),
    )(jnp.reshape(chip, (1,)).astype(jnp.int32), stack, land)


def _final_exchange(parts, pack):
    npart = len(parts)

    def body(*refs):
        ins, pack_ref = refs[:npart], refs[npart]
        outs, packs = refs[npart + 1:2 * npart + 1], refs[2 * npart + 1]
        send_sems, recv_sems, psend, precv, loc_sem = refs[2 * npart + 2:]
        x, y, c, _ = _place()
        me = 4 * x + 2 * y + c
        local = pltpu.make_async_copy(pack_ref, packs.at[me], loc_sem)
        local.start()
        cps = [pltpu.make_async_remote_copy(
            src_ref=ins[a], dst_ref=outs[a], send_sem=send_sems.at[a], recv_sem=recv_sems.at[a],
            device_id=(x, y, 1 - c), device_id_type=MESH) for a in range(npart)]
        for r in range(1, 8):
            dx, dy, dc = (r >> 2) & 1, (r >> 1) & 1, r & 1
            peer = (x + dx - 2 * x * dx, y + dy - 2 * y * dy, c + dc - 2 * c * dc)
            cps.append(pltpu.make_async_remote_copy(
                src_ref=pack_ref, dst_ref=packs.at[me], send_sem=psend.at[r - 1], recv_sem=precv.at[r - 1],
                device_id=peer, device_id_type=MESH))
        for cp in cps:
            cp.start()
        for cp in cps:
            cp.wait_recv()
        for cp in cps:
            cp.wait_send()
        local.wait()

    return pl.pallas_call(
        body, name="final_exchange",
        in_specs=[ANY] * (npart + 1), out_specs=[ANY] * (npart + 1),
        out_shape=[jax.ShapeDtypeStruct(p.shape, p.dtype) for p in parts]
        + [jax.ShapeDtypeStruct((8,) + pack.shape, pack.dtype)],
        scratch_shapes=[pltpu.SemaphoreType.DMA((npart,)), pltpu.SemaphoreType.DMA((npart,)),
                        pltpu.SemaphoreType.DMA((7,)), pltpu.SemaphoreType.DMA((7,)), pltpu.SemaphoreType.DMA],
    )(*parts, pack)


def _sum_packs(packs):
    def body(p_ref, o_ref):
        acc = p_ref[0]
        for d in range(1, 8):
            acc = acc + p_ref[d]
        o_ref[...] = acc

    return pl.pallas_call(
        body, name="sum_packs", out_shape=jax.ShapeDtypeStruct(packs.shape[1:], F32),
    )(packs)


def _adamw_update(g, w_ref, m_ref, v_ref, go, do, mo, vo):
    c1 = 1.0 / (1.0 - ADAM_B1 ** ADAM_STEP)
    c2 = 1.0 / (1.0 - ADAM_B2 ** ADAM_STEP)
    mn = ADAM_B1 * m_ref[...] + (1.0 - ADAM_B1) * g
    vn = ADAM_B2 * v_ref[...] + (1.0 - ADAM_B2) * (g * g)
    go[...] = g
    mo[...] = mn
    vo[...] = vn
    do[...] = -ADAM_LR * ((mn * c1) / (jnp.sqrt(vn * c2) + ADAM_EPS) + ADAM_WD * w_ref[...])


def _adamw(w, m, v, g1, g2, rows, name):
    r, cdim = w.shape
    rows = min(rows, r)

    def body(*refs):
        n_in = 4 if g2 is None else 5
        w_ref, m_ref, v_ref, g_ref = refs[:4]
        g = g_ref[...] if g2 is None else g_ref[...] + refs[4][...]
        _adamw_update(g, w_ref, m_ref, v_ref, *refs[n_in:n_in + 4])

    blk = pl.BlockSpec((rows, cdim), lambda i: (i, 0))
    args = [w, m, v, g1] + ([] if g2 is None else [g2])
    shp = jax.ShapeDtypeStruct((r, cdim), F32)
    return pl.pallas_call(
        body, name=name, grid=(r // rows,),
        in_specs=[blk] * len(args), out_specs=[blk] * 4, out_shape=[shp] * 4,
        compiler_params=_params(("parallel",), 20 * rows * cdim * 4 + 8 * 2**20),
    )(*args)


def _adamw_shard(wt, mt, vt, g1, g2):
    r, d = wt.shape
    cols = min(128, d)

    def body(w_ref, m_ref, v_ref, g_ref, g2_ref, go, do, mo, vo, pad_ref):
        chip = 2 * lax.axis_index("x") + lax.axis_index("y")
        back = [(ALIGNED_W - s) % ALIGNED_W for s in SHIFTS]
        pad_ref[...] = pltpu.roll(g_ref[...] + g2_ref[...], _by_chip(chip, back), 0)
        _adamw_update(pad_ref[0:r, :], w_ref, m_ref, v_ref, go, do, mo, vo)

    blk = pl.BlockSpec((r, cols), lambda i: (0, i))
    gblk = pl.BlockSpec((ALIGNED_W, cols), lambda i: (0, i))
    shp = jax.ShapeDtypeStruct((r, d), F32)
    return pl.pallas_call(
        body, name="adamw_w_in", grid=(d // cols,),
        in_specs=[blk] * 3 + [gblk] * 2, out_specs=[blk] * 4, out_shape=[shp] * 4,
        scratch_shapes=[pltpu.VMEM((ALIGNED_W, cols), F32)],
        compiler_params=_params(("parallel",), 24 * ALIGNED_W * cols * 4 + 8 * 2**20),
    )(wt, mt, vt, g1, g2)


def _pad_lanes(a, width):
    return jnp.pad(a, ((0, 0), (0, width - a.shape[1])))


def _gathered_to_full(g):
    return jnp.transpose(g, (1, 0, 2)).reshape(g.shape[1], 4 * g.shape[2])


def _row(a):
    return _pad_lanes(a.reshape(1, -1), 1024)


def _small_pack(nin, cb, fn, al, dt, gn, cqw_shard, cw_shard):
    ad = jnp.concatenate([al.reshape(1, -1), dt.reshape(1, -1)], axis=1)
    rows = [_row(nin), _row(cb), _row(fn), _row(ad), _row(gn), cqw_shard.reshape(3, 1024), _row(cw_shard)]
    out = jnp.concatenate(rows, axis=0)
    return jnp.pad(out, ((0, 16 - out.shape[0]), (0, 0)))


def kernel(x, norm_in_w, w_in, conv_qkv_w, A_log, dt_bias, gdn_norm_w, conv_w, conv_b, w_out, final_norm_w, loss_target, m_norm_in_w, m_w_in, m_conv_qkv_w, m_A_log, m_dt_bias, m_gdn_norm_w, m_conv_w, m_conv_b, m_w_out, m_final_norm_w, v_norm_in_w, v_w_in, v_conv_qkv_w, v_A_log, v_dt_bias, v_gdn_norm_w, v_conv_w, v_conv_b, v_w_out, v_final_norm_w):
    chip = 2 * lax.axis_index("x") + lax.axis_index("y")
    a_shard = _align_shard(jnp.transpose(w_in[0]))
    wo_b = _cast_bf16(w_out[0], 256, "cast_w_out")
    d_model = x.shape[-1]
    stack = lambda s: lax.empty((4,) + s.shape, s.dtype)
    wg0 = lax.empty((WG_BLOCKS * DH, d_model), BF16)
    wc0 = lax.empty((WC_BLOCKS * DH, d_model), BF16)
    ss_g, rs_g, bufs_g, tok_g = _gather_start("g", a_shard, wg0, [(conv_qkv_w[0], stack(conv_qkv_w[0]))])
    ss_c, rs_c, bufs_c, tok_c = _gather_start("c", bufs_g[0], wc0,
                                              [(conv_w[0], stack(conv_w[0])), (wo_b, stack(wo_b))])
    wg1, wc1, wog1, cqg1, cwg1 = _place_own(bufs_c[0], bufs_c[4], bufs_g[2], bufs_c[2],
                                            [bufs_g[1], bufs_c[1], bufs_c[5], bufs_g[3], bufs_c[3]])
    x0 = x[0]
    h = _rms_in(_tie(_tie(x0, tok_g, "after_gather_start_g"), tok_c, "after_gather_start_c"), norm_in_w)
    a_thru, wg, _, cq_g = _gather_wait("g", ss_g, rs_g, [bufs_c[0], wg1, bufs_g[2], cqg1], h)
    w_g = _merge_edges(_sibling_forward("g", wg), G_EDGE, G_MIXED, "merge_edges_g")
    cqw = _gathered_to_full(cq_g)
    ad = jnp.pad(jnp.concatenate([A_log, dt_bias], axis=0), ((0, 0), (A_LANE, 0)))
    fwd_c = {}

    def on_q(q):
        _, wc, _, cw_g, _, wo_g = _gather_wait("c", ss_c, rs_c,
                                               [a_thru, wc1, bufs_c[2], cwg1, bufs_c[4], wog1], q)
        issue, _ = _sibling_forward_parts("c")
        ss, rs, (wc,), tok = _split_start("sibling_forward_start_c", issue, [wc], 3)
        fwd_c.update(ss=ss, rs=rs, wc=wc, cw_g=cw_g, wo_g=wo_g)
        return _tie(q, tok, "after_sibling_forward_start_c")

    def late(o):
        _, await_ = _sibling_forward_parts("c")
        (wc,) = _split_wait("sibling_forward_wait_c", await_, fwd_c["ss"], fwd_c["rs"], [fwd_c["wc"]], o)
        return (_merge_edges(wc, C_EDGE, C_MIXED, "merge_edges_c"), fwd_c["wo_g"].reshape(2 * GW, d_model),
                _gathered_to_full(fwd_c["cw_g"]))

    scat = {}

    def on_grad_c(g_c, g_wout, do):
        go4 = g_wout.reshape(4, GW // 2, d_model)
        land = lax.empty((3, ALIGNED_W, d_model), BF16)
        land_o = lax.empty((3, GW // 2, d_model), BF16)
        ss, rs, bufs, tok = _scatter_start("c", g_c, land, [(go4, land_o)])
        scat["c"] = (ss, rs, bufs)
        return _tie(do, tok, "after_scatter_start_c")

    def on_grad_g(g_g, dproj_g):
        ss, rs, bufs, tok = _scatter_start("g", _pair_reduce("g", g_g), scat["c"][2][1], [], halved=True)
        scat["g"] = (ss, rs, bufs)
        return _tie(dproj_g, tok, "after_scatter_start_g")

    gx, sm, _ = _local_step(x0, loss_target[0], h, w_g, cqw, late, norm_in_w, ad, gdn_norm_w, conv_b,
                            final_norm_w.reshape(1, -1), on_grad_c, on_grad_g, on_q)

    ss, rs, bufs = scat["g"]
    g_g, land = _scatter_wait("g", ss, rs, bufs, gx, halved=True)
    ss, rs, bufs = scat["c"]
    g_c, land, go4, land_o = _scatter_wait("c", ss, rs, [bufs[0], land, bufs[2], bufs[3]], gx)
    part_in = _sum_shard(g_g, g_c, land)
    part_out = _sum_rows(go4, land_o, 128)
    ad_g = jnp.concatenate([sm["al"][:, A_LANE:], sm["dt"][:, A_LANE:]], axis=1)
    pack = jnp.concatenate([_row(sm["nin"]), _row(sm["cb"]), _row(sm["fn"]), _row(ad_g), _row(sm["gn"]),
                            jnp.concatenate(sm["cq"], axis=1).reshape(12, 1024), sm["cw"], _row(sm["loss"])], axis=0)
    pack = jnp.pad(pack, ((0, PACK_ROWS - pack.shape[0]), (0, 0)))
    sib_in, sib_out, packs = _final_exchange([part_in, part_out], pack)
    tot = _sum_packs(packs)

    g_wi, d_wi, m_wi, v_wi = [jnp.transpose(a) for a in _adamw_shard(
        jnp.transpose(w_in[0]), jnp.transpose(m_w_in[0]), jnp.transpose(v_w_in[0]), part_in, sib_in)]
    g_wo, d_wo, m_wo, v_wo = _adamw(w_out[0], m_w_out[0], v_w_out[0], part_out, sib_out, 128, "adamw_w_out")
    g_cq_sh = lax.dynamic_slice_in_dim(tot[R_CQ:R_CQ + 12].reshape(4, 3 * GW), chip * 768, 768, axis=1)
    g_cw_sh = lax.dynamic_slice_in_dim(tot[R_CW:R_CW + 3], chip * 256, 256, axis=1)
    sp = lambda nin, cb, fn, al, dt, gn, cq, cwv: _small_pack(nin, cb, fn, al, dt, gn, cq[0], cwv[0])
    g_s = _small_pack(tot[R_NIN], tot[R_CB], tot[R_FN], tot[R_AD, :HEADS], tot[R_AD, HEADS:2 * HEADS],
                      tot[R_GN, :DH], g_cq_sh, g_cw_sh)
    w_s = sp(norm_in_w, conv_b, final_norm_w, A_log, dt_bias, gdn_norm_w, conv_qkv_w, conv_w)
    m_s = sp(m_norm_in_w, m_conv_b, m_final_norm_w, m_A_log, m_dt_bias, m_gdn_norm_w, m_conv_qkv_w, m_conv_w)
    v_s = sp(v_norm_in_w, v_conv_b, v_final_norm_w, v_A_log, v_dt_bias, v_gdn_norm_w, v_conv_qkv_w, v_conv_w)
    small = _adamw(w_s, m_s, v_s, g_s, None, 16, "adamw_small")

    def unpack(a, big_in, big_out):
        return (a[0:1], big_in[None], a[5:8].reshape(1, 4, 768), a[3:4, :HEADS], a[3:4, HEADS:2 * HEADS],
                a[4:5, :DH], a[8, :768].reshape(1, 3, 256), a[1:2], big_out[None], a[2])

    loss = tot[R_LOSS, 0]
    return (loss, gx[None], *unpack(small[0], g_wi, g_wo), *unpack(small[1], d_wi, d_wo),
            *unpack(small[2], m_wi, m_wo), *unpack(small[3], v_wi, v_wo))
```

```python
import functools
import math

import jax
import jax.numpy as jnp
from jax import lax
from jax.experimental import pallas as pl
from jax.experimental.pallas import tpu as pltpu

F32 = jnp.float32
BF16 = jnp.bfloat16
MESH = pl.DeviceIdType.MESH
ANY = pl.BlockSpec(memory_space=pl.ANY)

HEADS = 8
DH = 128
CH = 64
GW = HEADS * DH
EPS = 1e-6
VMEM_V7X = 64 * 1024 * 1024

QB, KB, VB, ZB, BAB = 0, 8, 16, 24, 32
A_LANE = 120
NG, NC = 33, 32
GW_COLS, CW_COLS = NG * DH, NC * DH

SHARD_W = 2052
ALIGNED_BLOCKS = 17
ALIGNED_W = ALIGNED_BLOCKS * DH
SHIFTS = (0, 4, ALIGNED_W - 8, ALIGNED_W - 4)
G_EDGE, C_EDGE = 34, 32
G_SPARE, C_SPARE = 33, 34
WG_BLOCKS, WC_BLOCKS = 38, 36
G_MIXED, C_MIXED = (2, BAB), (4 * 7 + 1,)


def _shard_blocks(chip, edges):
    g, c = "g", "c"
    if chip == 0:
        out = [(g, 3 * b) for b in range(8)] + [(g, 3 * b + 1) for b in range(8)] + [(g, G_EDGE, G_MIXED[0])]
    elif chip == 1:
        out = [(g, G_EDGE + 1, G_MIXED[0])] + [(g, 3 * b + 2) for b in range(1, 8)]
        out += [(g, ZB + b) for b in range(8)] + [(g, G_EDGE + 2, G_MIXED[1])]
    elif chip == 2:
        out = [(c, 4 * b) for b in range(8)] + [(c, 4 * b + 1) for b in range(7)]
        out += [(c, C_EDGE, C_MIXED[0]), (g, G_EDGE + 3, G_MIXED[1])]
    else:
        out = [(c, 4 * b + 2) for b in range(8)] + [(c, 4 * b + 3) for b in range(8)] + [(c, C_EDGE + 1, C_MIXED[0])]
    return [(o[0], o[1] if (edges or len(o) == 2) else o[2]) for o in out]


def _by_chip(chip, vals):
    if all(v == vals[0] for v in vals):
        return vals[0]
    r = vals[3]
    for kk in (2, 1, 0):
        r = jnp.where(chip == kk, vals[kk], r)
    return r

ADAM_LR, ADAM_B1, ADAM_B2, ADAM_EPS, ADAM_WD, ADAM_STEP = 0.001, 0.9, 0.999, 1e-08, 0.01, 10

R_NIN, R_CB, R_FN, R_AD, R_GN, R_CQ, R_CW, R_LOSS, PACK_ROWS = 0, 1, 2, 3, 4, 5, 17, 20, 24

NN = ((1,), (0,))
NT = ((1,), (1,))
TN = ((0,), (0,))


def _dot(a, b, dims=NN, mode="lo"):
    dn = (dims, ((), ()))
    if mode == "hi":
        return lax.dot_general(a, b, dn, precision=lax.Precision.HIGHEST, preferred_element_type=F32)
    ah, bh = a.astype(BF16), b.astype(BF16)
    out = lax.dot_general(ah, bh, dn, preferred_element_type=F32)
    if mode == "x3":
        al = (a - ah.astype(F32)).astype(BF16)
        bl = (b - bh.astype(F32)).astype(BF16)
        out = out + lax.dot_general(ah, bl, dn, preferred_element_type=F32)
        out = out + lax.dot_general(al, bh, dn, preferred_element_type=F32)
    return out


P_GRAM, P_INV, P_SOL, P_SCAN, P_SCANB, P_BWD = "lo", "lo", "lo", "lo", "lo", "lo"
P_CUM = "x3"


def _params(sem=None, vmem=None):
    kw = {}
    if sem is not None:
        kw["dimension_semantics"] = sem
    if vmem is not None:
        kw["vmem_limit_bytes"] = int(min(max(vmem, 32 * 2**20), VMEM_V7X - 8 * 2**20))
    return pltpu.CompilerParams(**kw)


def _in_hbm(*arrays):
    return [pltpu.with_memory_space_constraint(a, pltpu.HBM) for a in arrays]


def _sigmoid(x):
    return 1.0 / (1.0 + jnp.exp(-x))


def _dsilu(x, s):
    return s * (1.0 + x * (1.0 - s))


def _rows(shape):
    return lax.broadcasted_iota(jnp.int32, shape, 0)


def _shift_down(x, s):
    if s == 0:
        return x
    return jnp.where(_rows(x.shape) >= s, pltpu.roll(x, s, 0), 0.0)


def _shift_up(x, s):
    if s == 0:
        return x
    n = x.shape[0]
    return jnp.where(_rows(x.shape) < n - s, pltpu.roll(x, n - s, 0), 0.0)


def _matmul(a, b, dims, out_dtype, tm, tn, tk, name, add=None, n=None):
    if dims == NN:
        (m, k), n = a.shape, b.shape[1]
    elif dims == NT:
        (m, k), n = a.shape, (n or b.shape[0])
    else:
        (k, m), n = a.shape, b.shape[1]
    tm, tn, tk = min(tm, m), min(tn, n), min(tk, k)
    assert m % tm == 0 and n % tn == 0 and k % tk == 0, (name, m, n, k, tm, tn, tk)
    nk = k // tk

    def body(*refs):
        if add is None:
            a_ref, b_ref, o_ref = refs[:3]
            add_ref = None
        else:
            a_ref, b_ref, add_ref, o_ref = refs[:4]
        part = _dot(a_ref[...], b_ref[...], dims)
        if nk == 1:
            if add_ref is not None:
                part = part + add_ref[...]
            o_ref[...] = part.astype(out_dtype)
            return
        acc = refs[-1]
        kk = pl.program_id(2)

        @pl.when(kk == 0)
        def _():
            acc[...] = part

        @pl.when(kk > 0)
        def _():
            acc[...] += part

        @pl.when(kk == nk - 1)
        def _():
            r = acc[...]
            if add_ref is not None:
                r = r + add_ref[...]
            o_ref[...] = r.astype(out_dtype)

    if dims == TN:
        a_spec = pl.BlockSpec((tk, tm), lambda i, j, kk: (kk, i))
    else:
        a_spec = pl.BlockSpec((tm, tk), lambda i, j, kk: (i, kk))
    if dims == NT:
        b_spec = pl.BlockSpec((tn, tk), lambda i, j, kk: (j, kk))
    else:
        b_spec = pl.BlockSpec((tk, tn), lambda i, j, kk: (kk, j))
    o_spec = pl.BlockSpec((tm, tn), lambda i, j, kk: (i, j))
    in_specs = [a_spec, b_spec]
    args = [a, b]
    if add is not None:
        in_specs.append(o_spec)
        args.append(add)
    osz = jnp.dtype(out_dtype).itemsize
    est = 2 * (tm * tk * a.dtype.itemsize + tk * tn * b.dtype.itemsize + tm * tn * osz)
    est += 3 * tm * tn * 4 + (2 * tm * tn * 4 if add is not None else 0)
    return pl.pallas_call(
        body, name=name, grid=(m // tm, n // tn, nk),
        in_specs=in_specs, out_specs=o_spec,
        out_shape=jax.ShapeDtypeStruct((m, n), out_dtype),
        scratch_shapes=[pltpu.VMEM((tm, tn), F32)] if nk > 1 else [],
        compiler_params=_params(("parallel", "parallel", "arbitrary"), est + 8 * 2**20),
    )(*args)


def _cast_bf16(a, rows, name):
    r, c = a.shape
    rows = min(rows, r)

    def body(a_ref, o_ref):
        o_ref[...] = a_ref[...].astype(BF16)

    return pl.pallas_call(
        body, name=name, grid=(r // rows,),
        in_specs=[pl.BlockSpec((rows, c), lambda i: (i, 0))],
        out_specs=pl.BlockSpec((rows, c), lambda i: (i, 0)),
        out_shape=jax.ShapeDtypeStruct((r, c), BF16),
        compiler_params=_params(("parallel",)),
    )(a)


def _align_shard(wt):
    r, d = wt.shape
    cols = min(256, d)

    def body(w_ref, o_ref, pad_ref):
        chip = 2 * lax.axis_index("x") + lax.axis_index("y")
        pad_ref[...] = jnp.zeros_like(pad_ref)
        pad_ref[0:r, :] = w_ref[...]
        o_ref[...] = pltpu.roll(pad_ref[...], _by_chip(chip, SHIFTS), 0).astype(BF16)

    return pl.pallas_call(
        body, name="align_shard", grid=(d // cols,),
        in_specs=[pl.BlockSpec((r, cols), lambda i: (0, i))],
        out_specs=pl.BlockSpec((ALIGNED_W, cols), lambda i: (0, i)),
        out_shape=jax.ShapeDtypeStruct((ALIGNED_W, d), BF16),
        scratch_shapes=[pltpu.VMEM((ALIGNED_W, cols), F32)],
        compiler_params=_params(("parallel",)),
    )(wt)


def _rms_in(x, w):
    n, d = x.shape
    tr = min(256, n)

    def body(x_ref, w_ref, h_ref):
        xv = x_ref[...]
        r = lax.rsqrt(jnp.mean(xv * xv, axis=-1, keepdims=True) + EPS)
        h_ref[...] = (xv * r * w_ref[...]).astype(BF16)

    return pl.pallas_call(
        body, name="rms_in", grid=(n // tr,),
        in_specs=[pl.BlockSpec((tr, d), lambda i: (i, 0)), pl.BlockSpec((1, d), lambda i: (0, 0))],
        out_specs=pl.BlockSpec((tr, d), lambda i: (i, 0)),
        out_shape=jax.ShapeDtypeStruct((n, d), BF16),
        compiler_params=_params(("parallel",)),
    )(x, w)


def _conv_silu(p, w_ref, taps):
    c = None
    for j in range(taps):
        t = _shift_down(p, taps - 1 - j) * w_ref[j:j + 1, :]
        c = t if c is None else c + t
    return c


def _prep_qkv(proj, cw):
    n = proj.shape[0]

    def body(p3, wq, wk, wv, q_ref, k_ref, v_ref):
        for kind, (w_ref, o_ref) in enumerate(((wq, q_ref), (wk, k_ref), (wv, v_ref))):
            c = _conv_silu(p3[:, kind * DH:(kind + 1) * DH], w_ref, 4)
            a = c * _sigmoid(c)
            if kind < 2:
                r = lax.rsqrt(jnp.sum(a * a, axis=-1, keepdims=True) + EPS)
                a = a * (r * (DH ** -0.5 if kind == 0 else 1.0))
            o_ref[...] = a

    col = pl.BlockSpec((n, DH), lambda h: (0, h))
    wcol = lambda base: pl.BlockSpec((4, DH), lambda h: (0, base + h))
    out = jax.ShapeDtypeStruct((n, GW), F32)
    return pl.pallas_call(
        body, name="prep_qkv", grid=(HEADS,),
        in_specs=[pl.BlockSpec((n, 3 * DH), lambda h: (0, h)), wcol(QB), wcol(KB), wcol(VB)],
        out_specs=[col] * 3, out_shape=[out] * 3,
        compiler_params=_params(("parallel",), 40 * 2**20),
    )(proj, cw, cw, cw)


def _prep_qkv_bwd(proj, cw, dq, dk, dv, dproj):
    n = proj.shape[0]

    def body(p3, wq, wk, wv, dq_ref, dk_ref, dv_ref, _, o3, gq, gk, gv):
        for kind, (w_ref, d_ref, g_ref) in enumerate(((wq, dq_ref, gq), (wk, dk_ref, gk), (wv, dv_ref, gv))):
            p = p3[:, kind * DH:(kind + 1) * DH]
            shifted = [_shift_down(p, 3 - j) for j in range(4)]
            c = shifted[0] * w_ref[0:1, :]
            for j in range(1, 4):
                c = c + shifted[j] * w_ref[j:j + 1, :]
            s = _sigmoid(c)
            a = c * s
            d = d_ref[...]
            if kind < 2:
                r = lax.rsqrt(jnp.sum(a * a, axis=-1, keepdims=True) + EPS)
                sc = DH ** -0.5 if kind == 0 else 1.0
                d = (sc * r) * (d - a * ((r * r) * jnp.sum(d * a, axis=-1, keepdims=True)))
            dc = d * _dsilu(c, s)
            dp = None
            for j in range(4):
                g_ref[j:j + 1, :] = jnp.sum(dc * shifted[j], axis=0, keepdims=True)
                t = _shift_up(dc, 3 - j) * w_ref[j:j + 1, :]
                dp = t if dp is None else dp + t
            o3[:, kind * DH:(kind + 1) * DH] = dp.astype(BF16)

    col = pl.BlockSpec((n, DH), lambda h: (0, h))
    wcol = lambda base: pl.BlockSpec((4, DH), lambda h: (0, base + h))
    p3spec = pl.BlockSpec((n, 3 * DH), lambda h: (0, h))
    return pl.pallas_call(
        body, name="prep_qkv_bwd", grid=(HEADS,),
        in_specs=[p3spec, wcol(QB), wcol(KB), wcol(VB), col, col, col, ANY],
        out_specs=[p3spec] + [wcol(0)] * 3,
        out_shape=[jax.ShapeDtypeStruct(dproj.shape, BF16)] + [jax.ShapeDtypeStruct((4, GW), F32)] * 3,
        input_output_aliases={7: 0},
        compiler_params=_params(("parallel",), 48 * 2**20),
    )(proj, cw, cw, cw, dq, dk, dv, dproj)


CPB = 8
SCAN_CPS = 4


def _tri(lower, rows):
    i = lax.broadcasted_iota(jnp.int32, (rows, rows), 0)
    j = lax.broadcasted_iota(jnp.int32, (rows, rows), 1)
    return jnp.where((i // CH == j // CH) & ((i >= j) if lower else (j >= i)), 1.0, 0.0)


def _lane(shape):
    return lax.broadcasted_iota(jnp.int32, shape, 1)


def _prep_bg(proj, ad):
    n = proj.shape[0]
    nch = n // CH
    cpb = CPB if nch % CPB == 0 else 1
    rows = cpb * CH

    def body(p_ref, ad_ref, bg_ref, bgt_ref):
        p = p_ref[...]
        lane = _lane(p.shape)
        beta = _sigmoid(p)
        xa = p + ad_ref[1:2, :]
        sp = jnp.maximum(xa, 0.0) + jnp.log(1.0 + jnp.exp(-jnp.abs(xa)))
        g = pltpu.roll(-jnp.exp(ad_ref[0:1, :]) * sp, DH - A_LANE + HEADS, 1)
        gc = _dot(_tri(True, rows), g, NN, P_CUM)
        bg = jnp.where(lane < HEADS, beta, jnp.where(lane < 2 * HEADS, gc, 0.0))
        bg_ref[...] = bg
        for ci in range(cpb):
            bgt_ref[ci] = bg[ci * CH:(ci + 1) * CH, :].T

    return pl.pallas_call(
        body, name="prep_bg", grid=(nch // cpb,),
        in_specs=[pl.BlockSpec((rows, DH), lambda i: (i, BAB)), pl.BlockSpec((2, DH), lambda i: (0, 0))],
        out_specs=[pl.BlockSpec((rows, DH), lambda i: (i, 0)), pl.BlockSpec((cpb, DH, CH), lambda i: (i, 0, 0))],
        out_shape=[jax.ShapeDtypeStruct((n, DH), F32), jax.ShapeDtypeStruct((nch, DH, CH), F32)],
        compiler_params=_params(("parallel",)),
    )(*_in_hbm(proj, ad))


def _prep_bg_bwd(proj, ad, dbg, dproj):
    n = proj.shape[0]
    nch = n // CH
    cpb = CPB if nch % CPB == 0 else 1
    rows = cpb * CH

    def body(p_ref, ad_ref, d_ref, _, o_ref, ga_ref, gd_ref):
        p = p_ref[...]
        d = d_ref[...]
        lane = _lane(p.shape)
        beta = _sigmoid(p)
        xa = p + ad_ref[1:2, :]
        sp = jnp.maximum(xa, 0.0) + jnp.log(1.0 + jnp.exp(-jnp.abs(xa)))
        na = -jnp.exp(ad_ref[0:1, :])
        dg = pltpu.roll(_dot(_tri(False, rows), d, NN, P_CUM), A_LANE - HEADS, 1)
        da = dg * na * _sigmoid(xa)
        is_g = lane >= A_LANE
        o_ref[...] = jnp.where(lane < HEADS, d * beta * (1.0 - beta), jnp.where(is_g, da, 0.0)).astype(BF16)
        ga = jnp.sum(jnp.where(is_g, dg * na * sp, 0.0), axis=0, keepdims=True)
        gd = jnp.sum(jnp.where(is_g, da, 0.0), axis=0, keepdims=True)

        @pl.when(pl.program_id(0) == 0)
        def _():
            ga_ref[...] = jnp.zeros_like(ga_ref)
            gd_ref[...] = jnp.zeros_like(gd_ref)

        ga_ref[...] += ga
        gd_ref[...] += gd

    one = pl.BlockSpec((1, DH), lambda i: (0, 0))
    return pl.pallas_call(
        body, name="prep_bg_bwd", grid=(nch // cpb,),
        in_specs=[pl.BlockSpec((rows, DH), lambda i: (i, BAB)), pl.BlockSpec((2, DH), lambda i: (0, 0)),
                  pl.BlockSpec((rows, DH), lambda i: (i, 0)), ANY],
        out_specs=[pl.BlockSpec((rows, DH), lambda i: (i, BAB)), one, one],
        out_shape=[jax.ShapeDtypeStruct(dproj.shape, BF16), jax.ShapeDtypeStruct((1, DH), F32),
                   jax.ShapeDtypeStruct((1, DH), F32)],
        input_output_aliases={3: 0},
        compiler_params=_params(("arbitrary",)),
    )(proj, ad, dbg, dproj)


def _gdn_out(o, proj, wg):
    n = o.shape[0]

    def body(o_ref, z_ref, w_ref, y_ref):
        ov, z = o_ref[...], z_ref[...]
        r = lax.rsqrt(jnp.mean(ov * ov, axis=-1, keepdims=True) + EPS)
        y_ref[...] = (ov * r * w_ref[...] * (z * _sigmoid(z))).astype(BF16)

    return pl.pallas_call(
        body, name="gdn_out", grid=(HEADS,),
        in_specs=[pl.BlockSpec((n, DH), lambda h: (0, h)), pl.BlockSpec((n, DH), lambda h: (0, ZB + h)),
                  pl.BlockSpec((1, DH), lambda h: (0, 0))],
        out_specs=pl.BlockSpec((n, DH), lambda h: (0, h)),
        out_shape=jax.ShapeDtypeStruct((n, 2 * GW), BF16),
        compiler_params=_params(("parallel",)),
    )(o, proj, wg)


def _gdn_out_bwd(o, proj, wg, dmix):
    n = o.shape[0]

    def body(o_ref, z_ref, w_ref, d_ref, do_ref, dz_ref, gw_ref):
        ov, z, d, w = o_ref[...], z_ref[...], d_ref[...], w_ref[...]
        r = lax.rsqrt(jnp.mean(ov * ov, axis=-1, keepdims=True) + EPS)
        nrm = ov * r
        s = _sigmoid(z)
        dz_ref[...] = (d * (nrm * w) * _dsilu(z, s)).astype(BF16)
        dn_w = d * (z * s)
        gw = jnp.sum(dn_w * nrm, axis=0, keepdims=True)
        dn = dn_w * w
        do_ref[...] = r * (dn - nrm * jnp.mean(dn * nrm, axis=-1, keepdims=True))

        @pl.when(pl.program_id(0) == 0)
        def _():
            gw_ref[...] = jnp.zeros_like(gw_ref)

        gw_ref[...] += gw

    return pl.pallas_call(
        body, name="gdn_out_bwd", grid=(HEADS,),
        in_specs=[pl.BlockSpec((n, DH), lambda h: (0, h)), pl.BlockSpec((n, DH), lambda h: (0, ZB + h)),
                  pl.BlockSpec((1, DH), lambda h: (0, 0)), pl.BlockSpec((n, DH), lambda h: (0, h))],
        out_specs=[pl.BlockSpec((n, DH), lambda h: (0, h)), pl.BlockSpec((n, DH), lambda h: (0, ZB + h)),
                   pl.BlockSpec((1, DH), lambda h: (0, 0))],
        out_shape=[jax.ShapeDtypeStruct((n, GW), F32), jax.ShapeDtypeStruct((n, GW_COLS), BF16),
                   jax.ShapeDtypeStruct((1, DH), F32)],
        compiler_params=_params(("arbitrary",)),
    )(o, proj, wg, dmix)


def _conv_branch(proj, w3, b, mix):
    n = proj.shape[0]

    def body(p4, w_ref, b_ref, _, y_ref):
        u = p4[:, DH:2 * DH] * p4[:, 2 * DH:3 * DH]
        cc = _conv_silu(u, w_ref, 3) + b_ref[...]
        z = p4[:, 3 * DH:4 * DH]
        y_ref[...] = (p4[:, 0:DH] * cc * (z * _sigmoid(z))).astype(BF16)

    return pl.pallas_call(
        body, name="conv_branch", grid=(HEADS,),
        in_specs=[pl.BlockSpec((n, 4 * DH), lambda h: (0, h)), pl.BlockSpec((3, DH), lambda h: (0, h)),
                  pl.BlockSpec((1, DH), lambda h: (0, h)), ANY],
        out_specs=pl.BlockSpec((n, DH), lambda h: (0, HEADS + h)),
        out_shape=jax.ShapeDtypeStruct(mix.shape, BF16),
        input_output_aliases={3: 0},
        compiler_params=_params(("parallel",), 40 * 2**20),
    )(*_in_hbm(proj, w3, b, mix))


def _conv_branch_bwd(proj, w3, b, dmix):
    n = proj.shape[0]

    def body(p4, w_ref, b_ref, d_ref, o4, gw_ref, gbias_ref):
        gb, gcv, hc, z = p4[:, 0:DH], p4[:, DH:2 * DH], p4[:, 2 * DH:3 * DH], p4[:, 3 * DH:4 * DH]
        d = d_ref[...]
        dgb, dgc, dhc, dzc = (o4.at[:, kk * DH:(kk + 1) * DH] for kk in range(4))
        u = gcv * hc
        cc = _conv_silu(u, w_ref, 3) + b_ref[...]
        s = _sigmoid(z)
        dzc[...] = (d * (gb * cc) * _dsilu(z, s)).astype(BF16)
        dp = d * (z * s)
        dgb[...] = (dp * cc).astype(BF16)
        dcc = dp * gb
        gbias_ref[...] = jnp.sum(dcc, axis=0, keepdims=True)
        du = None
        for j in range(3):
            gw_ref[j:j + 1, :] = jnp.sum(dcc * _shift_down(u, 2 - j), axis=0, keepdims=True)
            t = _shift_up(dcc, 2 - j) * w_ref[j:j + 1, :]
            du = t if du is None else du + t
        dgc[...] = (du * hc).astype(BF16)
        dhc[...] = (du * gcv).astype(BF16)

    p4spec = pl.BlockSpec((n, 4 * DH), lambda h: (0, h))
    return pl.pallas_call(
        body, name="conv_branch_bwd", grid=(HEADS,),
        in_specs=[p4spec, pl.BlockSpec((3, DH), lambda h: (0, h)), pl.BlockSpec((1, DH), lambda h: (0, h)),
                  pl.BlockSpec((n, DH), lambda h: (0, HEADS + h))],
        out_specs=[p4spec, pl.BlockSpec((3, DH), lambda h: (0, h)), pl.BlockSpec((1, DH), lambda h: (0, h))],
        out_shape=[jax.ShapeDtypeStruct((n, CW_COLS), BF16), jax.ShapeDtypeStruct((3, GW), F32),
                   jax.ShapeDtypeStruct((1, GW), F32)],
        compiler_params=_params(("parallel",), 48 * 2**20),
    )(proj, w3, b, dmix)


def _final_loss(out, tgt, wf):
    n, d = out.shape
    tr = min(256, n)

    def body(o_ref, t_ref, w_ref, do_ref, dob_ref, gw_ref, loss_ref):
        ov, w = o_ref[...], w_ref[...]
        r = lax.rsqrt(jnp.mean(ov * ov, axis=-1, keepdims=True) + EPS)
        nrm = ov * r
        e = nrm * w - t_ref[...]
        dy = e * (1.0 / d)
        dn = dy * w
        dout = r * (dn - nrm * jnp.mean(dn * nrm, axis=-1, keepdims=True))
        do_ref[...] = dout
        dob_ref[...] = dout.astype(BF16)

        @pl.when(pl.program_id(0) == 0)
        def _():
            gw_ref[...] = jnp.zeros_like(gw_ref)
            loss_ref[...] = jnp.zeros_like(loss_ref)

        gw_ref[...] += jnp.sum(dy * nrm, axis=0, keepdims=True)
        loss_ref[...] += (0.5 / d) * jnp.sum(jnp.sum(e * e, axis=-1, keepdims=True), axis=0, keepdims=True)

    row = pl.BlockSpec((tr, d), lambda i: (i, 0))
    return pl.pallas_call(
        body, name="final_loss", grid=(n // tr,),
        in_specs=[row, row, pl.BlockSpec((1, d), lambda i: (0, 0))],
        out_specs=[row, row, pl.BlockSpec((1, d), lambda i: (0, 0)), pl.BlockSpec((1, 1), lambda i: (0, 0))],
        out_shape=[jax.ShapeDtypeStruct((n, d), F32), jax.ShapeDtypeStruct((n, d), BF16),
                   jax.ShapeDtypeStruct((1, d), F32), jax.ShapeDtypeStruct((1, 1), F32)],
        compiler_params=_params(("arbitrary",)),
    )(*_in_hbm(out, tgt, wf))


def _rms_in_bwd(x, w, dh, dout):
    n, d = x.shape
    tr = min(256, n)

    def body(x_ref, w_ref, dh_ref, do_ref, dx_ref, gw_ref):
        xv, dhv = x_ref[...], dh_ref[...]
        r = lax.rsqrt(jnp.mean(xv * xv, axis=-1, keepdims=True) + EPS)
        xn = xv * r
        dxn = dhv * w_ref[...]
        dx_ref[...] = r * (dxn - xn * jnp.mean(dxn * xn, axis=-1, keepdims=True)) + do_ref[...]

        @pl.when(pl.program_id(0) == 0)
        def _():
            gw_ref[...] = jnp.zeros_like(gw_ref)

        gw_ref[...] += jnp.sum(dhv * xn, axis=0, keepdims=True)

    row = pl.BlockSpec((tr, d), lambda i: (i, 0))
    one = pl.BlockSpec((1, d), lambda i: (0, 0))
    return pl.pallas_call(
        body, name="rms_in_bwd", grid=(n // tr,),
        in_specs=[row, one, row, row], out_specs=[row, one],
        out_shape=[jax.ShapeDtypeStruct((n, d), F32), jax.ShapeDtypeStruct((1, d), F32)],
        compiler_params=_params(("arbitrary",)),
    )(*_in_hbm(x, w, dh, dout))


def _ij():
    i = lax.broadcasted_iota(jnp.int32, (CH, CH), 0)
    j = lax.broadcasted_iota(jnp.int32, (CH, CH), 1)
    return i, j


def _unit_lower_inverse(mats):
    i, j = _ij()
    eye = jnp.where(i == j, 1.0, 0.0)
    same16 = (i // 16) == (j // 16)
    same32 = (i // 32) == (j // 32)
    mm = lambda xs, ys: [_dot(x, y, NN, P_INV) for x, y in zip(xs, ys)]
    n1 = [jnp.where(same16, -a, 0.0) for a in mats]
    n2 = mm(n1, n1)
    n4 = mm(n2, n2)
    n8 = mm(n4, n4)
    t = [eye + x1 + x2 + x3 for x1, x2, x3 in zip(n1, n2, mm(n1, n2))]
    t = [x + y for x, y in zip(t, mm(t, n4))]
    t = [x + y for x, y in zip(t, mm(t, n8))]
    a1 = [jnp.where(same32 & jnp.logical_not(same16), a, 0.0) for a in mats]
    t = [x - y for x, y in zip(t, mm(t, mm(a1, t)))]
    a2 = [jnp.where(same32, 0.0, a) for a in mats]
    t = [x - y for x, y in zip(t, mm(t, mm(a2, t)))]
    return t


def _head_vectors(bg, bgt, h):
    bcol = bg[:, h:h + 1]
    gcol = bg[:, HEADS + h:HEADS + h + 1]
    grow = bgt[HEADS + h:HEADS + h + 1, :]
    return bcol, gcol, grow


def _decay(gcol, grow):
    i, j = _ij()
    return jnp.where(i >= j, jnp.exp(jnp.where(i >= j, gcol - grow, 0.0)), 0.0)


def _gdn_intra(q, k, v, bg, bgt):
    n = q.shape[0]
    nch = n // CH
    cps = 4 if nch % 4 == 0 else 1

    def body(q_ref, k_ref, v_ref, bg_ref, bgt_ref, u_ref, w_ref, p_ref, t_ref):
        i, j = _ij()
        items = [(ci, h) for ci in range(cps) for h in range(HEADS)]
        at = lambda ref, ci, h: ref.at[ci * CH:(ci + 1) * CH, h * DH:(h + 1) * DH]
        bgs = [bg_ref[ci * CH:(ci + 1) * CH, :] for ci in range(cps)]
        ks = [at(k_ref, ci, h)[...] for ci, h in items]
        vecs = [_head_vectors(bgs[ci], bgt_ref[ci], h) for ci, h in items]
        decs = [_decay(gcol, grow) for _, gcol, grow in vecs]
        kks = [_dot(kh, kh, NT, P_GRAM) for kh in ks]
        qks = [_dot(at(q_ref, ci, h)[...], kh, NT, P_GRAM) for (ci, h), kh in zip(items, ks)]
        ts = _unit_lower_inverse([jnp.where(i > j, bcol * kk * dec, 0.0)
                                  for (bcol, _, _), kk, dec in zip(vecs, kks, decs)])
        us = [_dot(t, at(v_ref, ci, h)[...] * bcol, NN, P_SOL) for t, (ci, h), (bcol, _, _) in zip(ts, items, vecs)]
        ws = [_dot(t, kh * (bcol * jnp.exp(gcol)), NN, P_SOL) for t, kh, (bcol, gcol, _) in zip(ts, ks, vecs)]
        for n_, (ci, h) in enumerate(items):
            p_ref[ci, h] = qks[n_] * decs[n_]
            t_ref[ci, h] = ts[n_]
            at(u_ref, ci, h)[...] = us[n_]
            at(w_ref, ci, h)[...] = ws[n_]

    row = pl.BlockSpec((cps * CH, GW), lambda c: (c, 0))
    sq = pl.BlockSpec((cps, HEADS, CH, CH), lambda c: (c, 0, 0, 0))
    big = jax.ShapeDtypeStruct((n, GW), F32)
    sqs = jax.ShapeDtypeStruct((nch, HEADS, CH, CH), F32)
    return pl.pallas_call(
        body, name="gdn_intra", grid=(nch // cps,),
        in_specs=[row, row, row, pl.BlockSpec((cps * CH, DH), lambda c: (c, 0)),
                  pl.BlockSpec((cps, DH, CH), lambda c: (c, 0, 0))],
        out_specs=[row, row, sq, sq], out_shape=[big, big, sqs, sqs],
        compiler_params=_params(("parallel",)),
    )(q, k, v, bg, bgt)


def _gdn_scan(q, k, bg, u, w, p):
    n = q.shape[0]
    nch = n // CH
    cps = SCAN_CPS if nch % SCAN_CPS == 0 else 1

    def body(q_ref, k_ref, bg_ref, u_ref, w_ref, p_ref, o_ref, vn_ref, s_out, s_scr):
        @pl.when(pl.program_id(0) == 0)
        def _():
            s_scr[...] = jnp.zeros_like(s_scr)

        hs = range(HEADS)
        sls = [slice(h * DH, (h + 1) * DH) for h in hs]
        ss = [s_scr[h] for h in hs]
        for ci in range(cps):
            rs = slice(ci * CH, (ci + 1) * CH)
            bg = bg_ref[rs, :]
            gcols = [bg[:, HEADS + h:HEADS + h + 1] for h in hs]
            glasts = [g[CH - 1:CH, :] for g in gcols]
            wss = [_dot(w_ref[rs, sl], s, NN, P_SCAN) for sl, s in zip(sls, ss)]
            oqs = [_dot(q_ref[rs, sl] * jnp.exp(g), s, NN, P_SCAN) for sl, s, g in zip(sls, ss, gcols)]
            vns = [u_ref[rs, sl] - x for sl, x in zip(sls, wss)]
            ops = [_dot(p_ref[ci, h], vn, NN, P_SCAN) for h, vn in zip(hs, vns)]
            sns = [_dot(k_ref[rs, sl] * jnp.exp(gl - g), vn, TN, P_SCAN)
                   for sl, gl, g, vn in zip(sls, glasts, gcols, vns)]
            for h, sl in enumerate(sls):
                s_out[ci, :, sl] = ss[h]
                vn_ref[rs, sl] = vns[h]
                o_ref[rs, sl] = oqs[h] + ops[h]
            ss = [s * jnp.exp(gl) + sn for s, gl, sn in zip(ss, glasts, sns)]
        for h in hs:
            s_scr[h] = ss[h]

    row = pl.BlockSpec((cps * CH, GW), lambda c: (c, 0))
    big = jax.ShapeDtypeStruct((n, GW), F32)
    return pl.pallas_call(
        body, name="gdn_scan", grid=(nch // cps,),
        in_specs=[row, row, pl.BlockSpec((cps * CH, DH), lambda c: (c, 0)), row, row,
                  pl.BlockSpec((cps, HEADS, CH, CH), lambda c: (c, 0, 0, 0))],
        out_specs=[row, row, pl.BlockSpec((cps, DH, GW), lambda c: (c, 0, 0))],
        out_shape=[big, big, jax.ShapeDtypeStruct((nch, DH, GW), F32)],
        scratch_shapes=[pltpu.VMEM((HEADS, DH, DH), F32)],
        compiler_params=_params(("arbitrary",)),
    )(q, k, bg, u, w, p)


def _gdn_scan_bwd(q, k, bg, w, p, vn, s_in, do):
    n = q.shape[0]
    nch = n // CH
    cps = SCAN_CPS if nch % SCAN_CPS == 0 else 1
    rev = lambda c: nch // cps - 1 - c

    def body(q_ref, k_ref, bg_ref, w_ref, p_ref, vn_ref, s_ref, do_ref,
             dqg_ref, dp_ref, du_ref, dw_ref, dks_ref, dgam_ref, ds_scr):
        @pl.when(pl.program_id(0) == 0)
        def _():
            ds_scr[...] = jnp.zeros_like(ds_scr)

        lane = _lane((1, DH))
        hs = range(HEADS)
        sls = [slice(h * DH, (h + 1) * DH) for h in hs]
        dss = [ds_scr[h] for h in hs]
        for ci in reversed(range(cps)):
            rs = slice(ci * CH, (ci + 1) * CH)
            bg = bg_ref[rs, :]
            gcols = [bg[:, HEADS + h:HEADS + h + 1] for h in hs]
            glasts = [g[CH - 1:CH, :] for g in gcols]
            ss = [s_ref[ci, :, sl] for sl in sls]
            dos = [do_ref[rs, sl] for sl in sls]
            vnl = [vn_ref[rs, sl] for sl in sls]
            dqgs = [_dot(d, s, NT, P_SCANB) for d, s in zip(dos, ss)]
            dps = [_dot(d, vn, NT, P_SCANB) for d, vn in zip(dos, vnl)]
            dvn1 = [_dot(p_ref[ci, h], d, TN, P_SCANB) for h, d in zip(hs, dos)]
            dvn2 = [_dot(k_ref[rs, sl] * jnp.exp(gl - g), ds, NN, P_SCANB)
                    for sl, gl, g, ds in zip(sls, glasts, gcols, dss)]
            dkss = [_dot(vn, ds, NT, P_SCANB) for vn, ds in zip(vnl, dss)]
            dsq = [_dot(q_ref[rs, sl] * jnp.exp(g), d, TN, P_SCANB) for sl, g, d in zip(sls, gcols, dos)]
            dvns = [a + b for a, b in zip(dvn1, dvn2)]
            dws = [_dot(dvn, s, NT, P_SCANB) for dvn, s in zip(dvns, ss)]
            dsw = [_dot(w_ref[rs, sl], dvn, TN, P_SCANB) for sl, dvn in zip(sls, dvns)]
            dgam = jnp.zeros((1, DH), F32)
            for h, sl in enumerate(sls):
                dqg_ref[rs, sl] = dqgs[h]
                dp_ref[ci, h] = dps[h]
                du_ref[rs, sl] = dvns[h]
                dw_ref[rs, sl] = -dws[h]
                dks_ref[rs, sl] = dkss[h]
                tot = jnp.sum(jnp.sum(dss[h] * ss[h], axis=-1, keepdims=True), axis=0, keepdims=True)
                dgam = dgam + jnp.where(lane == h, tot, 0.0)
            dgam_ref[ci] = jnp.broadcast_to(dgam, (8, DH))
            dss = [ds * jnp.exp(gl) + a - b for ds, gl, a, b in zip(dss, glasts, dsq, dsw)]
        for h in hs:
            ds_scr[h] = dss[h]

    row = pl.BlockSpec((cps * CH, GW), lambda c: (rev(c), 0))
    sq = pl.BlockSpec((cps, HEADS, CH, CH), lambda c: (rev(c), 0, 0, 0))
    big = jax.ShapeDtypeStruct((n, GW), F32)
    return pl.pallas_call(
        body, name="gdn_scan_bwd", grid=(nch // cps,),
        in_specs=[row, row, pl.BlockSpec((cps * CH, DH), lambda c: (rev(c), 0)), row, sq, row,
                  pl.BlockSpec((cps, DH, GW), lambda c: (rev(c), 0, 0)), row],
        out_specs=[row, sq, row, row, row, pl.BlockSpec((cps, 8, DH), lambda c: (rev(c), 0, 0))],
        out_shape=[big, jax.ShapeDtypeStruct((nch, HEADS, CH, CH), F32), big, big, big,
                   jax.ShapeDtypeStruct((nch, 8, DH), F32)],
        scratch_shapes=[pltpu.VMEM((HEADS, DH, DH), F32)],
        compiler_params=_params(("arbitrary",)),
    )(q, k, bg, w, p, vn, s_in, do)


def _gdn_intra_bwd(q, k, v, bg, bgt, t, u, w, p, dqg, dp, du, dw, dks, dgam):
    n = q.shape[0]
    nch = n // CH
    cps = 1

    def body(q_ref, k_ref, v_ref, bg_ref, bgt_ref, t_ref, u_ref, w_ref, p_ref,
             dqg_ref, dp_ref, du_ref, dw_ref, dks_ref, dgam_ref, dq_ref, dk_ref, dv_ref, dbg_ref):
        i, j = _ij()
        rows1 = lax.broadcasted_iota(jnp.int32, (CH, 1), 0)
        lane = _lane((CH, DH))
        rsum = lambda x: jnp.sum(x, axis=-1, keepdims=True)
        items = [(ci, h) for ci in range(cps) for h in range(HEADS)]
        at = lambda ref, it: ref.at[it[0] * CH:(it[0] + 1) * CH, it[1] * DH:(it[1] + 1) * DH]
        ld = lambda ref: [at(ref, it)[...] for it in items]
        bgs = [bg_ref[ci * CH:(ci + 1) * CH, :] for ci in range(cps)]
        qs, ks = ld(q_ref), ld(k_ref)
        vecs = [_head_vectors(bgs[ci], bgt_ref[ci], h) for ci, h in items]
        decs = [_decay(gcol, grow) for _, gcol, grow in vecs]
        ths = [t_ref[ci, h] for ci, h in items]
        drus = [_dot(th, x_, TN, P_BWD) for th, x_ in zip(ths, ld(du_ref))]
        drws = [_dot(th, x_, TN, P_BWD) for th, x_ in zip(ths, ld(dw_ref))]
        kks = [_dot(kh, kh, NT, P_GRAM) for kh in ks]
        da1 = [_dot(dru, x_, NT, P_BWD) for dru, x_ in zip(drus, ld(u_ref))]
        da2 = [_dot(drw, x_, NT, P_BWD) for drw, x_ in zip(drws, ld(w_ref))]
        das = [jnp.where(i > j, -(x_ + y_), 0.0) for x_, y_ in zip(da1, da2)]
        dkks = [da * bcol * dec for da, (bcol, _, _), dec in zip(das, vecs, decs)]
        dps = [dp_ref[ci, h] for ci, h in items]
        dqks = [dp_ * dec for dp_, dec in zip(dps, decs)]
        dq_ps = [_dot(dqk, kh, NN, P_BWD) for dqk, kh in zip(dqks, ks)]
        dk_ps = [_dot(dqk, qh, TN, P_BWD) for dqk, qh in zip(dqks, qs)]
        dk_as = [_dot(dkk, kh, NN, P_BWD) for dkk, kh in zip(dkks, ks)]
        dk_bs = [_dot(dkk, kh, TN, P_BWD) for dkk, kh in zip(dkks, ks)]
        bcols = [vc[0] for vc in vecs]
        gcols = [vc[1] for vc in vecs]
        gams = [jnp.exp(g) for g in gcols]
        glasts = [g[CH - 1:CH, :] for g in gcols]
        es = [jnp.exp(gl - g) for gl, g in zip(glasts, gcols)]
        kgs = [kh * gam for kh, gam in zip(ks, gams)]
        dqgs, dkss = ld(dqg_ref), ld(dks_ref)
        r_uv = [rsum(dru * x_) for dru, x_ in zip(drus, ld(v_ref))]
        r_wk = [rsum(drw * kg) for drw, kg in zip(drws, kgs)]
        r_ak = [rsum(da * kk * dec) for da, kk, dec in zip(das, kks, decs)]
        r_qq = [rsum(dqg * qh) for dqg, qh in zip(dqgs, qs)]
        tks = [rsum(dk_ * kh) * e for dk_, kh, e in zip(dkss, ks, es)]
        mdecs = [da * (bcol * kk * dec) + dp_ * p_ref[ci, h]
                 for (ci, h), da, bcol, kk, dec, dp_ in zip(items, das, bcols, kks, decs, dps)]
        r_md = [rsum(m) for m in mdecs]
        c_md = [rsum(jnp.where(i == j, jnp.sum(m, axis=0, keepdims=True), 0.0)) for m in mdecs]
        dbgs = [jnp.zeros((CH, DH), F32) for _ in range(cps)]
        for n_, (ci, h) in enumerate(items):
            at(dv_ref, (ci, h))[...] = bcols[n_] * drus[n_]
            at(dq_ref, (ci, h))[...] = gams[n_] * dqgs[n_] + dq_ps[n_]
            at(dk_ref, (ci, h))[...] = ((bcols[n_] * gams[n_]) * drws[n_] + dk_ps[n_] + dk_as[n_] + dk_bs[n_]
                                        + dkss[n_] * es[n_])
            dbeta = r_uv[n_] + r_wk[n_] + r_ak[n_]
            dglast = (jnp.sum(tks[n_], axis=0, keepdims=True)
                      + dgam_ref[ci, 0:1, h:h + 1] * jnp.exp(glasts[n_]))
            dgc = (r_wk[n_] * bcols[n_] + r_md[n_] - c_md[n_] + r_qq[n_] * gams[n_] - tks[n_]
                   + jnp.where(rows1 == CH - 1, dglast, 0.0))
            dbgs[ci] = dbgs[ci] + jnp.where(lane == h, dbeta, 0.0) + jnp.where(lane == HEADS + h, dgc, 0.0)
        for ci in range(cps):
            dbg_ref[ci * CH:(ci + 1) * CH, :] = dbgs[ci]

    row = pl.BlockSpec((cps * CH, GW), lambda c: (c, 0))
    sq = pl.BlockSpec((cps, HEADS, CH, CH), lambda c: (c, 0, 0, 0))
    small = pl.BlockSpec((cps * CH, DH), lambda c: (c, 0))
    big = jax.ShapeDtypeStruct((n, GW), F32)
    return pl.pallas_call(
        body, name="gdn_intra_bwd", grid=(nch // cps,),
        in_specs=[row, row, row, small, pl.BlockSpec((cps, DH, CH), lambda c: (c, 0, 0)), sq, row, row, sq,
                  row, sq, row, row, row, pl.BlockSpec((cps, 8, DH), lambda c: (c, 0, 0))],
        out_specs=[row, row, row, small],
        out_shape=[big, big, big, jax.ShapeDtypeStruct((n, DH), F32)],
        compiler_params=_params(("parallel",)),
    )(q, k, v, bg, bgt, t, u, w, p, dqg, dp, du, dw, dks, dgam)


def _local_step(x, tgt, h, w_g, cqw, late, norm_in_w, ad, gdn_norm_w, conv_b, final_norm_w,
                on_grad_c=None, on_grad_g=None, on_q=None):
    proj_g = _matmul(h, w_g, NT, F32, 512, 1408, 1024, "mm_proj_g", n=GW_COLS)
    q, k, v = _prep_qkv(proj_g, cqw)
    if on_q is not None:
        q = on_q(q)
    bg, bgt = _prep_bg(proj_g, ad)
    u, w, p, t = _gdn_intra(q, k, v, bg, bgt)
    o, vn, s_in = _gdn_scan(q, k, bg, u, w, p)
    w_c, w_out, conv_w = late(o)
    proj_c = _matmul(h, w_c, NT, F32, 512, 1024, 1024, "mm_proj_c", n=CW_COLS)
    mix = _conv_branch(proj_c, conv_w, conv_b, _gdn_out(o, proj_g, gdn_norm_w))
    out = _matmul(mix, w_out, NN, F32, 512, 512, 2048, "mm_out", add=x)
    dout, dout_b, g_fn, loss = _final_loss(out, tgt, final_norm_w)

    dmix = _matmul(dout_b, w_out, NT, F32, 512, 1024, 1024, "mm_dmix")
    g_wout = _matmul(mix, dout_b, TN, BF16, 512, 512, 2048, "mm_gwout")
    do, dproj_g, g_gn = _gdn_out_bwd(o, proj_g, gdn_norm_w, dmix)
    dproj_c, g_cw, g_cb = _conv_branch_bwd(proj_c, conv_w, conv_b, dmix)
    g_c = _matmul(dproj_c, h, TN, BF16, 1024, 512, 2048, "mm_gwin_c")
    if on_grad_c is not None:
        do = on_grad_c(g_c, g_wout, do)
    dqg, dp, du, dw, dks, dgam = _gdn_scan_bwd(q, k, bg, w, p, vn, s_in, do)
    dq, dk, dv, dbg = _gdn_intra_bwd(q, k, v, bg, bgt, t, u, w, p, dqg, dp, du, dw, dks, dgam)
    dproj_g, gq, gk, gv = _prep_qkv_bwd(proj_g, cqw, dq, dk, dv, dproj_g)
    dproj_g, g_al, g_dt = _prep_bg_bwd(proj_g, ad, dbg, dproj_g)
    g_g = _matmul(dproj_g, h, TN, BF16, 1408, 512, 2048, "mm_gwin_g")
    if on_grad_g is not None:
        dproj_g = on_grad_g(g_g, dproj_g)
    dh = _matmul(dproj_g, w_g, NN, F32, 1024, 1024, 1408, "mm_dh_g")
    dh = _matmul(dproj_c, w_c, NN, F32, 1024, 1024, 1024, "mm_dh_c", add=dh)
    gx, g_nin = _rms_in_bwd(x, norm_in_w, dh, dout)
    small = dict(nin=g_nin, cb=g_cb, fn=g_fn, al=g_al, dt=g_dt, gn=g_gn, cq=(gq, gk, gv), cw=g_cw, loss=loss)
    return gx, small, (g_g, g_c, g_wout)


def _place():
    x, y, c = lax.axis_index("x"), lax.axis_index("y"), lax.axis_index("c")
    chips = [(1 - x, y), (x, 1 - y), (1 - x, 1 - y)]
    return x, y, c, chips


def _blk(ref, b):
    if isinstance(b, int):
        return ref.at[b * DH:(b + 1) * DH, :]
    return ref.at[pl.ds(pl.multiple_of(b * DH, DH), DH), :]


HBM = pl.BlockSpec(memory_space=pltpu.HBM)
SEM = pl.BlockSpec(memory_space=pltpu.SEMAPHORE)
EFFECT = pltpu.SideEffectType.DATAFLOW_SIDE_EFFECTING


def _split_start(name, issue, bufs, n_sems):
    nbuf = len(bufs)

    def body(*refs):
        issue(refs[:nbuf], refs[nbuf], refs[nbuf + 1])
        refs[-1][...] = jnp.zeros_like(refs[-1])

    out = pl.pallas_call(
        body, name=name,
        out_shape=(pltpu.SemaphoreType.DMA((n_sems,)), pltpu.SemaphoreType.DMA((n_sems,)),
                   *[pltpu.HBM(b.shape, b.dtype) for b in bufs], jax.ShapeDtypeStruct((8, DH), F32)),
        in_specs=[HBM] * nbuf,
        out_specs=(SEM, SEM, *[HBM] * nbuf, pl.BlockSpec(memory_space=pltpu.VMEM)),
        input_output_aliases={a: 2 + a for a in range(nbuf)},
        compiler_params=pltpu.CompilerParams(has_side_effects=EFFECT),
    )(*[pltpu.with_memory_space_constraint(b, pltpu.HBM) for b in bufs])
    return out[0], out[1], list(out[2:2 + nbuf]), out[-1]


def _split_wait(name, await_, send_sems, recv_sems, bufs, after):
    nbuf = len(bufs)

    def body(*refs):
        await_(refs[:nbuf], refs[nbuf], refs[nbuf + 1])

    out = pl.pallas_call(
        body, name=name,
        out_shape=tuple(pltpu.HBM(b.shape, b.dtype) for b in bufs),
        in_specs=[HBM] * nbuf + [SEM, SEM, ANY], out_specs=tuple([HBM] * nbuf),
        input_output_aliases={a: a for a in range(nbuf)},
        compiler_params=pltpu.CompilerParams(has_side_effects=EFFECT),
    )(*bufs, send_sems, recv_sems, after)
    return list(out)


def _phase_blocks(chip, phase, edges, parity=None):
    return [(b, blk) for b, (grp, blk) in enumerate(_shard_blocks(chip, edges))
            if grp == phase and (parity is None or b % 2 == parity)]


def _cols(ref, nblk):
    return ref.at[0:nblk * DH, :]


def _block_table(chip, edges, spare_g, spare_c):
    rows = []
    for s in range(4):
        sb = _shard_blocks(s, edges)
        rows.append([[blk if grp == "g" else spare_g for grp, blk in sb],
                     [blk if grp == "c" else spare_c for grp, blk in sb],
                     [int(grp == "g") for grp, _ in sb], [s] * ALIGNED_BLOCKS])
    return jnp.asarray(rows, jnp.int32)[chip]


def _place_own(a_shard, wo, cq, cw, bufs):
    d = a_shard.shape[1]
    chip = 2 * lax.axis_index("x") + lax.axis_index("y")

    def body(t_ref, a_ref, wo_ref, cq_ref, cw_ref, *refs):
        wg_ref, wc_ref, wog_ref, cqg_ref, cwg_ref = refs[5:]
        wg_ref[...] = a_ref[...]
        wc_ref[...] = a_ref[...]

        @pl.when(pl.program_id(0) == 0)
        def _():
            wog_ref[0] = wo_ref[...]
            cqg_ref[0] = cq_ref[...]
            cwg_ref[0] = cw_ref[...]

    whole = lambda s: pl.BlockSpec(s.shape, lambda b, t: (0,) * s.ndim)
    slot = lambda s: pl.BlockSpec((1,) + s.shape, lambda b, t: (t[3, 0],) + (0,) * s.ndim)
    return pl.pallas_call(
        body, name="place_own",
        grid_spec=pltpu.PrefetchScalarGridSpec(
            num_scalar_prefetch=1, grid=(ALIGNED_BLOCKS,),
            in_specs=[pl.BlockSpec((DH, d), lambda b, t: (b, 0)), whole(wo), whole(cq), whole(cw)] + [ANY] * 5,
            out_specs=[pl.BlockSpec((DH, d), lambda b, t: (t[0, b], 0)),
                       pl.BlockSpec((DH, d), lambda b, t: (t[1, b], 0)), slot(wo), slot(cq), slot(cw)]),
        out_shape=[jax.ShapeDtypeStruct(b.shape, b.dtype) for b in bufs],
        input_output_aliases={5 + a: a for a in range(5)},
        compiler_params=_params(("arbitrary",)),
    )(_block_table(chip, True, G_SPARE, C_SPARE), a_shard, wo, cq, cw, *bufs)


def _tie(x, token, name):
    def body(x_ref, t_ref, o_ref):
        del x_ref, t_ref, o_ref

    return pl.pallas_call(
        body, name=name, in_specs=[ANY, ANY], out_specs=ANY,
        out_shape=jax.ShapeDtypeStruct(x.shape, x.dtype), input_output_aliases={0: 0},
    )(x, token)


def _gather_start(phase, a_shard, w_grp, singles):
    ns = len(singles)

    def issue(refs, send_sems, recv_sems):
        a_ref, w_ref = refs[0], refs[1]
        x, y, c, chips = _place()
        mine = 2 * x + y
        for jj, (px, py) in enumerate(chips):
            to = dict(device_id=(px, py, c), device_id_type=MESH)
            for a in range(ns):
                pltpu.make_async_remote_copy(
                    src_ref=refs[2 + 2 * a], dst_ref=refs[3 + 2 * a].at[mine],
                    send_sem=send_sems.at[(1 + ns) * jj + 1 + a], recv_sem=recv_sems.at[(1 + ns) * jj + 1 + a],
                    **to).start()
        for s in range(4):
            for par in range(2):
                blocks = _phase_blocks(s, phase, True, par)
                if blocks:
                    @pl.when((mine == s) & (c == par))
                    def _():
                        for jj, (px, py) in enumerate(chips):
                            for b, blk in blocks:
                                pltpu.make_async_remote_copy(
                                    src_ref=_blk(a_ref, b), dst_ref=_blk(w_ref, blk),
                                    send_sem=send_sems.at[(1 + ns) * jj], recv_sem=recv_sems.at[(1 + ns) * jj],
                                    device_id=(px, py, c), device_id_type=MESH).start()

    bufs = [a_shard, w_grp] + [t for pair in singles for t in pair]
    return _split_start("gather_start_" + phase, issue, bufs, 3 * (1 + ns))


def _gather_wait(phase, send_sems, recv_sems, bufs, after):
    ns = (len(bufs) - 2) // 2

    def await_(refs, send_sems, recv_sems):
        a_ref, w_ref = refs[0], refs[1]
        x, y, c, chips = _place()
        mine = 2 * x + y
        for jj, (px, py) in enumerate(chips):
            to = dict(device_id=(px, py, c), device_id_type=MESH)
            peer = 2 * px + py
            for a in range(ns):
                cp = pltpu.make_async_remote_copy(
                    src_ref=refs[2 + 2 * a], dst_ref=refs[3 + 2 * a].at[mine],
                    send_sem=send_sems.at[(1 + ns) * jj + 1 + a], recv_sem=recv_sems.at[(1 + ns) * jj + 1 + a], **to)
                cp.wait_recv()
                cp.wait_send()
            for s in range(4):
                for par in range(2):
                    nblk = len(_phase_blocks(s, phase, True, par))
                    if nblk:
                        both = pltpu.make_async_remote_copy(
                            src_ref=_cols(a_ref, nblk), dst_ref=_cols(w_ref, nblk),
                            send_sem=send_sems.at[(1 + ns) * jj], recv_sem=recv_sems.at[(1 + ns) * jj], **to)

                        @pl.when((peer == s) & (c == par))
                        def _():
                            both.wait_recv()

                        @pl.when((mine == s) & (c == par))
                        def _():
                            both.wait_send()

    return _split_wait("gather_wait_" + phase, await_, send_sems, recv_sems, bufs, after)


def _sibling_forward_parts(phase):
    def each(w_ref, send_sems, recv_sems, start):
        x, y, c, chips = _place()
        to = dict(device_id=(x, y, 1 - c), device_id_type=MESH)
        for jj, (px, py) in enumerate(chips):
            peer = 2 * px + py
            for s in range(4):
                for par in range(2):
                    mine_blocks = _phase_blocks(s, phase, True, par)
                    theirs = len(_phase_blocks(s, phase, True, 1 - par))
                    if not (mine_blocks or theirs):
                        continue

                    @pl.when((peer == s) & (c == par))
                    def _():
                        if start:
                            for _, blk in mine_blocks:
                                pltpu.make_async_remote_copy(
                                    src_ref=_blk(w_ref, blk), dst_ref=_blk(w_ref, blk),
                                    send_sem=send_sems.at[jj], recv_sem=recv_sems.at[jj], **to).start()
                            return
                        if theirs:
                            pltpu.make_async_remote_copy(
                                src_ref=_cols(w_ref, theirs), dst_ref=_cols(w_ref, theirs),
                                send_sem=send_sems.at[jj], recv_sem=recv_sems.at[jj], **to).wait_recv()
                        if mine_blocks:
                            pltpu.make_async_remote_copy(
                                src_ref=_cols(w_ref, len(mine_blocks)), dst_ref=_cols(w_ref, len(mine_blocks)),
                                send_sem=send_sems.at[jj], recv_sem=recv_sems.at[jj], **to).wait_send()

    issue = lambda refs, send_sems, recv_sems: each(refs[0], send_sems, recv_sems, True)
    await_ = lambda refs, send_sems, recv_sems: each(refs[0], send_sems, recv_sems, False)
    return issue, await_


def _sibling_forward(phase, w_grp):
    issue, await_ = _sibling_forward_parts(phase)

    def body(w_in_ref, w_ref, send_sems, recv_sems):
        del w_in_ref
        issue([w_ref], send_sems, recv_sems)
        await_([w_ref], send_sems, recv_sems)

    return pl.pallas_call(
        body, name="sibling_forward_" + phase, in_specs=[ANY], out_specs=ANY,
        out_shape=jax.ShapeDtypeStruct(w_grp.shape, w_grp.dtype), input_output_aliases={0: 0},
        scratch_shapes=[pltpu.SemaphoreType.DMA((3,)), pltpu.SemaphoreType.DMA((3,))],
    )(w_grp)


def _merge_edges(w, edge0, mixed, name):
    d = w.shape[1]

    def body(e_ref, o_ref):
        o_ref[...] = e_ref[0:DH, :] + e_ref[DH:2 * DH, :]

    def to_block(i):
        r = mixed[-1]
        for kk in range(len(mixed) - 2, -1, -1):
            r = jnp.where(i == kk, mixed[kk], r)
        return r

    return pl.pallas_call(
        body, name=name, grid=(len(mixed),),
        in_specs=[pl.BlockSpec((2 * DH, d), lambda i: (edge0 // 2 + i, 0))],
        out_specs=pl.BlockSpec((DH, d), lambda i: (to_block(i), 0)),
        out_shape=jax.ShapeDtypeStruct(w.shape, w.dtype),
        input_output_aliases={0: 0},
        compiler_params=_params(("arbitrary",)),
    )(w)


def _scatter_start(phase, g_grp, land, singles, halved=False):
    ns = len(singles)

    def issue(refs, send_sems, recv_sems):
        g_ref, land_ref = refs[0], refs[1]
        x, y, c, chips = _place()
        for jj, (px, py) in enumerate(chips):
            to = dict(device_id=(px, py, c), device_id_type=MESH)
            peer = 2 * px + py
            for a in range(ns):
                pltpu.make_async_remote_copy(
                    src_ref=refs[2 + 2 * a].at[peer], dst_ref=refs[3 + 2 * a].at[jj],
                    send_sem=send_sems.at[(1 + ns) * jj + 1 + a], recv_sem=recv_sems.at[(1 + ns) * jj + 1 + a],
                    **to).start()
            for s in range(4):
                for par in ((0, 1) if halved else (None,)):
                    blocks = _phase_blocks(s, phase, False, par)
                    if blocks:
                        @pl.when((peer == s) if par is None else ((peer == s) & (c == par)))
                        def _():
                            for b, blk in blocks:
                                pltpu.make_async_remote_copy(
                                    src_ref=_blk(g_ref, blk), dst_ref=_blk(land_ref.at[jj], b),
                                    send_sem=send_sems.at[(1 + ns) * jj], recv_sem=recv_sems.at[(1 + ns) * jj],
                                    **to).start()

    bufs = [g_grp, land] + [t for pair in singles for t in pair]
    return _split_start("scatter_start_" + phase, issue, bufs, 3 * (1 + ns))


def _scatter_wait(phase, send_sems, recv_sems, bufs, after, halved=False):
    ns = (len(bufs) - 2) // 2

    def await_(refs, send_sems, recv_sems):
        g_ref, land_ref = refs[0], refs[1]
        x, y, c, chips = _place()
        mine = 2 * x + y
        for jj, (px, py) in enumerate(chips):
            to = dict(device_id=(px, py, c), device_id_type=MESH)
            peer = 2 * px + py
            for a in range(ns):
                cp = pltpu.make_async_remote_copy(
                    src_ref=refs[2 + 2 * a].at[peer], dst_ref=refs[3 + 2 * a].at[jj],
                    send_sem=send_sems.at[(1 + ns) * jj + 1 + a], recv_sem=recv_sems.at[(1 + ns) * jj + 1 + a], **to)
                cp.wait_recv()
                cp.wait_send()
            for s in range(4):
                for par in ((0, 1) if halved else (None,)):
                    nblk = len(_phase_blocks(s, phase, False, par))
                    if nblk:
                        both = pltpu.make_async_remote_copy(
                            src_ref=_cols(g_ref, nblk), dst_ref=_cols(land_ref.at[jj], nblk),
                            send_sem=send_sems.at[(1 + ns) * jj], recv_sem=recv_sems.at[(1 + ns) * jj], **to)

                        @pl.when((mine == s) if par is None else ((mine == s) & (c == par)))
                        def _():
                            both.wait_recv()

                        @pl.when((peer == s) if par is None else ((peer == s) & (c == par)))
                        def _():
                            both.wait_send()

    return _split_wait("scatter_wait_" + phase, await_, send_sems, recv_sems, bufs, after)


def _needed_blocks(phase, parity):
    return sorted({blk for s in range(4) for _, blk in _phase_blocks(s, phase, False, parity)})


def _pair_reduce(phase, g_grp):
    n, d = g_grp.shape

    def swap(g_ref, sib_ref, send_sem, recv_sem):
        x, y, c, _ = _place()
        to = dict(device_id=(x, y, 1 - c), device_id_type=MESH)
        for par in range(2):
            give, get = _needed_blocks(phase, 1 - par), _needed_blocks(phase, par)

            @pl.when(c == par)
            def _():
                for blk in give:
                    pltpu.make_async_remote_copy(src_ref=_blk(g_ref, blk), dst_ref=_blk(sib_ref, blk),
                                                 send_sem=send_sem, recv_sem=recv_sem, **to).start()
                pltpu.make_async_remote_copy(src_ref=_cols(g_ref, len(get)), dst_ref=_cols(sib_ref, len(get)),
                                             send_sem=send_sem, recv_sem=recv_sem, **to).wait_recv()
                pltpu.make_async_remote_copy(src_ref=_cols(g_ref, len(give)), dst_ref=_cols(sib_ref, len(give)),
                                             send_sem=send_sem, recv_sem=recv_sem, **to).wait_send()

    sib = pl.pallas_call(
        swap, name="pair_swap_" + phase, in_specs=[ANY], out_specs=ANY,
        out_shape=jax.ShapeDtypeStruct((n, d), g_grp.dtype),
        scratch_shapes=[pltpu.SemaphoreType.DMA, pltpu.SemaphoreType.DMA],
    )(*_in_hbm(g_grp))

    lists = [_needed_blocks(phase, par) for par in range(2)]
    longest = max(len(t) for t in lists)
    table = jnp.asarray([t + [t[-1]] * (longest - len(t)) for t in lists], jnp.int32)[lax.axis_index("c")]

    def add(t_ref, a_ref, b_ref, o_ref):
        o_ref[...] = (a_ref[...].astype(F32) + b_ref[...].astype(F32)).astype(o_ref.dtype)

    blk = pl.BlockSpec((DH, d), lambda i, t: (t[i], 0))
    return pl.pallas_call(
        add, name="pair_add_" + phase,
        grid_spec=pltpu.PrefetchScalarGridSpec(num_scalar_prefetch=1, grid=(longest,),
                                               in_specs=[blk, blk], out_specs=blk),
        out_shape=jax.ShapeDtypeStruct((n, d), g_grp.dtype),
        compiler_params=_params(("arbitrary",)),
    )(table, g_grp, sib)


def _sum_shard(g_g, g_c, land):
    d = g_g.shape[1]
    chip = 2 * lax.axis_index("x") + lax.axis_index("y")

    def body(t_ref, gg_ref, gc_ref, land_ref, o_ref):
        b = pl.program_id(0)
        in_g = t_ref[2, b] == 1
        own = jnp.where(in_g, gg_ref[...].astype(F32), gc_ref[...].astype(F32))
        for jj in range(3):
            own = own + land_ref[jj].astype(F32)
        o_ref[...] = jnp.where(in_g & (b % 2 != lax.axis_index("c")), 0.0, own)

    return pl.pallas_call(
        body, name="sum_w_in",
        grid_spec=pltpu.PrefetchScalarGridSpec(
            num_scalar_prefetch=1, grid=(ALIGNED_BLOCKS,),
            in_specs=[pl.BlockSpec((DH, d), lambda b, t: (t[0, b], 0)), pl.BlockSpec((DH, d), lambda b, t: (t[1, b], 0)),
                      pl.BlockSpec((3, DH, d), lambda b, t: (0, b, 0))],
            out_specs=pl.BlockSpec((DH, d), lambda b, t: (b, 0))),
        out_shape=jax.ShapeDtypeStruct((ALIGNED_W, d), F32),
        compiler_params=_params(("arbitrary",)),
    )(_block_table(chip, False, 0, 0), g_g, g_c, land)


def _sum_rows(stack, land, rows):
    _, r, d = stack.shape
    rows = min(rows, r)
    chip = 2 * lax.axis_index("x") + lax.axis_index("y")

    def body(t_ref, own_ref, land_ref, o_ref):
        acc = own_ref[0].astype(F32)
        for jj in range(3):
            acc = acc + land_ref[jj].astype(F32)
        o_ref[...] = acc

    return pl.pallas_call(
        body, name="sum_w_out",
        grid_spec=pltpu.PrefetchScalarGridSpec(
            num_scalar_prefetch=1, grid=(r // rows,),
            in_specs=[pl.BlockSpec((1, rows, d), lambda i, t: (t[0], i, 0)),
                      pl.BlockSpec((3, rows, d), lambda i, t: (0, i, 0))],
            out_specs=pl.BlockSpec((rows, d), lambda i, t: (i, 0))),
        out_shape=jax.ShapeDtypeStruct((r, d), F32),
        compiler_params=_params(("arbitrary",)),
    )(jnp.reshape(chip, (1,)).astype(jnp.int32), stack, land)


def _final_exchange(parts, pack):
    npart = len(parts)

    def body(*refs):
        ins, pack_ref = refs[:npart], refs[npart]
        outs, packs = refs[npart + 1:2 * npart + 1], refs[2 * npart + 1]
        send_sems, recv_sems, psend, precv, loc_sem = refs[2 * npart + 2:]
        x, y, c, _ = _place()
        me = 4 * x + 2 * y + c
        local = pltpu.make_async_copy(pack_ref, packs.at[me], loc_sem)
        local.start()
        cps = [pltpu.make_async_remote_copy(
            src_ref=ins[a], dst_ref=outs[a], send_sem=send_sems.at[a], recv_sem=recv_sems.at[a],
            device_id=(x, y, 1 - c), device_id_type=MESH) for a in range(npart)]
        for r in range(1, 8):
            dx, dy, dc = (r >> 2) & 1, (r >> 1) & 1, r & 1
            peer = (x + dx - 2 * x * dx, y + dy - 2 * y * dy, c + dc - 2 * c * dc)
            cps.append(pltpu.make_async_remote_copy(
                src_ref=pack_ref, dst_ref=packs.at[me], send_sem=psend.at[r - 1], recv_sem=precv.at[r - 1],
                device_id=peer, device_id_type=MESH))
        for cp in cps:
            cp.start()
        for cp in cps:
            cp.wait_recv()
        for cp in cps:
            cp.wait_send()
        local.wait()

    return pl.pallas_call(
        body, name="final_exchange",
        in_specs=[ANY] * (npart + 1), out_specs=[ANY] * (npart + 1),
        out_shape=[jax.ShapeDtypeStruct(p.shape, p.dtype) for p in parts]
        + [jax.ShapeDtypeStruct((8,) + pack.shape, pack.dtype)],
        scratch_shapes=[pltpu.SemaphoreType.DMA((npart,)), pltpu.SemaphoreType.DMA((npart,)),
                        pltpu.SemaphoreType.DMA((7,)), pltpu.SemaphoreType.DMA((7,)), pltpu.SemaphoreType.DMA],
    )(*parts, pack)


def _sum_packs(packs):
    def body(p_ref, o_ref):
        acc = p_ref[0]
        for d in range(1, 8):
            acc = acc + p_ref[d]
        o_ref[...] = acc

    return pl.pallas_call(
        body, name="sum_packs", out_shape=jax.ShapeDtypeStruct(packs.shape[1:], F32),
    )(packs)


def _adamw_update(g, w_ref, m_ref, v_ref, go, do, mo, vo):
    c1 = 1.0 / (1.0 - ADAM_B1 ** ADAM_STEP)
    c2 = 1.0 / (1.0 - ADAM_B2 ** ADAM_STEP)
    mn = ADAM_B1 * m_ref[...] + (1.0 - ADAM_B1) * g
    vn = ADAM_B2 * v_ref[...] + (1.0 - ADAM_B2) * (g * g)
    go[...] = g
    mo[...] = mn
    vo[...] = vn
    do[...] = -ADAM_LR * ((mn * c1) / (jnp.sqrt(vn * c2) + ADAM_EPS) + ADAM_WD * w_ref[...])


def _adamw(w, m, v, g1, g2, rows, name):
    r, cdim = w.shape
    rows = min(rows, r)

    def body(*refs):
        n_in = 4 if g2 is None else 5
        w_ref, m_ref, v_ref, g_ref = refs[:4]
        g = g_ref[...] if g2 is None else g_ref[...] + refs[4][...]
        _adamw_update(g, w_ref, m_ref, v_ref, *refs[n_in:n_in + 4])

    blk = pl.BlockSpec((rows, cdim), lambda i: (i, 0))
    args = [w, m, v, g1] + ([] if g2 is None else [g2])
    shp = jax.ShapeDtypeStruct((r, cdim), F32)
    return pl.pallas_call(
        body, name=name, grid=(r // rows,),
        in_specs=[blk] * len(args), out_specs=[blk] * 4, out_shape=[shp] * 4,
        compiler_params=_params(("parallel",), 20 * rows * cdim * 4 + 8 * 2**20),
    )(*_in_hbm(*args))


def _adamw_shard(wt, mt, vt, g1, g2):
    r, d = wt.shape
    cols = min(128, d)

    def body(w_ref, m_ref, v_ref, g_ref, g2_ref, go, do, mo, vo, pad_ref):
        chip = 2 * lax.axis_index("x") + lax.axis_index("y")
        back = [(ALIGNED_W - s) % ALIGNED_W for s in SHIFTS]
        pad_ref[...] = pltpu.roll(g_ref[...] + g2_ref[...], _by_chip(chip, back), 0)
        _adamw_update(pad_ref[0:r, :], w_ref, m_ref, v_ref, go, do, mo, vo)

    blk = pl.BlockSpec((r, cols), lambda i: (0, i))
    gblk = pl.BlockSpec((ALIGNED_W, cols), lambda i: (0, i))
    shp = jax.ShapeDtypeStruct((r, d), F32)
    return pl.pallas_call(
        body, name="adamw_w_in", grid=(d // cols,),
        in_specs=[blk] * 3 + [gblk] * 2, out_specs=[blk] * 4, out_shape=[shp] * 4,
        scratch_shapes=[pltpu.VMEM((ALIGNED_W, cols), F32)],
        compiler_params=_params(("parallel",), 24 * ALIGNED_W * cols * 4 + 8 * 2**20),
    )(wt, mt, vt, g1, g2)


def _pad_lanes(a, width):
    return jnp.pad(a, ((0, 0), (0, width - a.shape[1])))


def _gathered_to_full(g):
    return jnp.transpose(g, (1, 0, 2)).reshape(g.shape[1], 4 * g.shape[2])


def _row(a):
    return _pad_lanes(a.reshape(1, -1), 1024)


def _small_pack(nin, cb, fn, al, dt, gn, cqw_shard, cw_shard):
    ad = jnp.concatenate([al.reshape(1, -1), dt.reshape(1, -1)], axis=1)
    rows = [_row(nin), _row(cb), _row(fn), _row(ad), _row(gn), cqw_shard.reshape(3, 1024), _row(cw_shard)]
    out = jnp.concatenate(rows, axis=0)
    return jnp.pad(out, ((0, 16 - out.shape[0]), (0, 0)))


def kernel(x, norm_in_w, w_in, conv_qkv_w, A_log, dt_bias, gdn_norm_w, conv_w, conv_b, w_out, final_norm_w, loss_target, m_norm_in_w, m_w_in, m_conv_qkv_w, m_A_log, m_dt_bias, m_gdn_norm_w, m_conv_w, m_conv_b, m_w_out, m_final_norm_w, v_norm_in_w, v_w_in, v_conv_qkv_w, v_A_log, v_dt_bias, v_gdn_norm_w, v_conv_w, v_conv_b, v_w_out, v_final_norm_w):
    chip = 2 * lax.axis_index("x") + lax.axis_index("y")
    a_shard = _align_shard(jnp.transpose(w_in[0]))
    wo_b = _cast_bf16(w_out[0], 256, "cast_w_out")
    d_model = x.shape[-1]
    stack = lambda s: lax.empty((4,) + s.shape, s.dtype)
    wg0 = lax.empty((WG_BLOCKS * DH, d_model), BF16)
    wc0 = lax.empty((WC_BLOCKS * DH, d_model), BF16)
    ss_g, rs_g, bufs_g, tok_g = _gather_start("g", a_shard, wg0, [(conv_qkv_w[0], stack(conv_qkv_w[0]))])
    ss_c, rs_c, bufs_c, tok_c = _gather_start("c", bufs_g[0], wc0,
                                              [(conv_w[0], stack(conv_w[0])), (wo_b, stack(wo_b))])
    wg1, wc1, wog1, cqg1, cwg1 = _place_own(bufs_c[0], bufs_c[4], bufs_g[2], bufs_c[2],
                                            [bufs_g[1], bufs_c[1], bufs_c[5], bufs_g[3], bufs_c[3]])
    x0 = x[0]
    h = _rms_in(x0, _tie(_tie(norm_in_w, tok_g, "after_gather_start_g"), tok_c, "after_gather_start_c"))
    a_thru, wg, _, cq_g = _gather_wait("g", ss_g, rs_g, [bufs_c[0], wg1, bufs_g[2], cqg1], h)
    w_g = _merge_edges(_sibling_forward("g", wg), G_EDGE, G_MIXED, "merge_edges_g")
    cqw = _gathered_to_full(cq_g)
    ad = jnp.pad(jnp.concatenate([A_log, dt_bias], axis=0), ((0, 0), (A_LANE, 0)))
    fwd_c = {}

    def on_q(q):
        _, wc, _, cw_g, _, wo_g = _gather_wait("c", ss_c, rs_c,
                                               [a_thru, wc1, bufs_c[2], cwg1, bufs_c[4], wog1], q)
        issue, _ = _sibling_forward_parts("c")
        ss, rs, (wc,), tok = _split_start("sibling_forward_start_c", issue, [wc], 3)
        fwd_c.update(ss=ss, rs=rs, wc=wc, cw_g=cw_g, wo_g=wo_g)
        return _tie(q, tok, "after_sibling_forward_start_c")

    def late(o):
        _, await_ = _sibling_forward_parts("c")
        (wc,) = _split_wait("sibling_forward_wait_c", await_, fwd_c["ss"], fwd_c["rs"], [fwd_c["wc"]], o)
        return (_merge_edges(wc, C_EDGE, C_MIXED, "merge_edges_c"), fwd_c["wo_g"].reshape(2 * GW, d_model),
                _gathered_to_full(fwd_c["cw_g"]))

    scat = {}

    def on_grad_c(g_c, g_wout, do):
        go4 = g_wout.reshape(4, GW // 2, d_model)
        land = lax.empty((3, ALIGNED_W, d_model), BF16)
        land_o = lax.empty((3, GW // 2, d_model), BF16)
        ss, rs, bufs, tok = _scatter_start("c", g_c, land, [(go4, land_o)])
        scat["c"] = (ss, rs, bufs)
        return _tie(do, tok, "after_scatter_start_c")

    def on_grad_g(g_g, dproj_g):
        ss, rs, bufs, tok = _scatter_start("g", _pair_reduce("g", g_g), scat["c"][2][1], [], halved=True)
        scat["g"] = (ss, rs, bufs)
        return _tie(dproj_g, tok, "after_scatter_start_g")

    gx, sm, _ = _local_step(x0, loss_target[0], h, w_g, cqw, late, norm_in_w, ad, gdn_norm_w, conv_b,
                            final_norm_w.reshape(1, -1), on_grad_c, on_grad_g, on_q)

    ss, rs, bufs = scat["g"]
    g_g, land = _scatter_wait("g", ss, rs, bufs, gx, halved=True)
    ss, rs, bufs = scat["c"]
    g_c, land, go4, land_o = _scatter_wait("c", ss, rs, [bufs[0], land, bufs[2], bufs[3]], gx)
    part_in = _sum_shard(g_g, g_c, land)
    part_out = _sum_rows(go4, land_o, 128)
    ad_g = jnp.concatenate([sm["al"][:, A_LANE:], sm["dt"][:, A_LANE:]], axis=1)
    pack = jnp.concatenate([_row(sm["nin"]), _row(sm["cb"]), _row(sm["fn"]), _row(ad_g), _row(sm["gn"]),
                            jnp.concatenate(sm["cq"], axis=1).reshape(12, 1024), sm["cw"], _row(sm["loss"])], axis=0)
    pack = jnp.pad(pack, ((0, PACK_ROWS - pack.shape[0]), (0, 0)))
    sib_in, sib_out, packs = _final_exchange([part_in, part_out], pack)
    tot = _sum_packs(packs)

    g_wi, d_wi, m_wi, v_wi = [jnp.transpose(a) for a in _adamw_shard(
        jnp.transpose(w_in[0]), jnp.transpose(m_w_in[0]), jnp.transpose(v_w_in[0]), part_in, sib_in)]
    g_wo, d_wo, m_wo, v_wo = _adamw(w_out[0], m_w_out[0], v_w_out[0], part_out, sib_out, 128, "adamw_w_out")
    g_cq_sh = lax.dynamic_slice_in_dim(tot[R_CQ:R_CQ + 12].reshape(4, 3 * GW), chip * 768, 768, axis=1)
    g_cw_sh = lax.dynamic_slice_in_dim(tot[R_CW:R_CW + 3], chip * 256, 256, axis=1)
    sp = lambda nin, cb, fn, al, dt, gn, cq, cwv: _small_pack(nin, cb, fn, al, dt, gn, cq[0], cwv[0])
    g_s = _small_pack(tot[R_NIN], tot[R_CB], tot[R_FN], tot[R_AD, :HEADS], tot[R_AD, HEADS:2 * HEADS],
                      tot[R_GN, :DH], g_cq_sh, g_cw_sh)
    w_s = sp(norm_in_w, conv_b, final_norm_w, A_log, dt_bias, gdn_norm_w, conv_qkv_w, conv_w)
    m_s = sp(m_norm_in_w, m_conv_b, m_final_norm_w, m_A_log, m_dt_bias, m_gdn_norm_w, m_conv_qkv_w, m_conv_w)
    v_s = sp(v_norm_in_w, v_conv_b, v_final_norm_w, v_A_log, v_dt_bias, v_gdn_norm_w, v_conv_qkv_w, v_conv_w)
    small = _adamw(w_s, m_s, v_s, g_s, None, 16, "adamw_small")

    def unpack(a, big_in, big_out):
        return (a[0:1], big_in[None], a[5:8].reshape(1, 4, 768), a[3:4, :HEADS], a[3:4, HEADS:2 * HEADS],
                a[4:5, :DH], a[8, :768].reshape(1, 3, 256), a[1:2], big_out[None], a[2])

    loss = tot[R_LOSS, 0]
    return (loss, gx[None], *unpack(small[0], g_wi, g_wo), *unpack(small[1], d_wi, d_wo),
            *unpack(small[2], m_wi, m_wo), *unpack(small[3], v_wi, v_wo))
```

```python
import functools
import math

import jax
import jax.numpy as jnp
from jax import lax
from jax.experimental import pallas as pl
from jax.experimental.pallas import tpu as pltpu

F32 = jnp.float32
BF16 = jnp.bfloat16
MESH = pl.DeviceIdType.MESH
ANY = pl.BlockSpec(memory_space=pl.ANY)

HEADS = 8
DH = 128
CH = 64
GW = HEADS * DH
EPS = 1e-6
VMEM_V7X = 64 * 1024 * 1024

QB, KB, VB, ZB, BAB = 0, 8, 16, 24, 32
A_LANE = 120
NG, NC = 33, 32
GW_COLS, CW_COLS = NG * DH, NC * DH

SHARD_W = 2052
ALIGNED_BLOCKS = 17
ALIGNED_W = ALIGNED_BLOCKS * DH
SHIFTS = (0, 4, ALIGNED_W - 8, ALIGNED_W - 4)
G_EDGE, C_EDGE = 34, 32
G_SPARE, C_SPARE = 33, 34
WG_BLOCKS, WC_BLOCKS = 38, 36
G_MIXED, C_MIXED = (2, BAB), (4 * 7 + 1,)


def _shard_blocks(chip, edges):
    g, c = "g", "c"
    if chip == 0:
        out = [(g, 3 * b) for b in range(8)] + [(g, 3 * b + 1) for b in range(8)] + [(g, G_EDGE, G_MIXED[0])]
    elif chip == 1:
        out = [(g, G_EDGE + 1, G_MIXED[0])] + [(g, 3 * b + 2) for b in range(1, 8)]
        out += [(g, ZB + b) for b in range(8)] + [(g, G_EDGE + 2, G_MIXED[1])]
    elif chip == 2:
        out = [(c, 4 * b) for b in range(8)] + [(c, 4 * b + 1) for b in range(7)]
        out += [(c, C_EDGE, C_MIXED[0]), (g, G_EDGE + 3, G_MIXED[1])]
    else:
        out = [(c, 4 * b + 2) for b in range(8)] + [(c, 4 * b + 3) for b in range(8)] + [(c, C_EDGE + 1, C_MIXED[0])]
    return [(o[0], o[1] if (edges or len(o) == 2) else o[2]) for o in out]


def _by_chip(chip, vals):
    if all(v == vals[0] for v in vals):
        return vals[0]
    r = vals[3]
    for kk in (2, 1, 0):
        r = jnp.where(chip == kk, vals[kk], r)
    return r

ADAM_LR, ADAM_B1, ADAM_B2, ADAM_EPS, ADAM_WD, ADAM_STEP = 0.001, 0.9, 0.999, 1e-08, 0.01, 10

R_NIN, R_CB, R_FN, R_AD, R_GN, R_CQ, R_CW, R_LOSS, PACK_ROWS = 0, 1, 2, 3, 4, 5, 17, 20, 24

NN = ((1,), (0,))
NT = ((1,), (1,))
TN = ((0,), (0,))


def _dot(a, b, dims=NN, mode="lo"):
    dn = (dims, ((), ()))
    if mode == "hi":
        return lax.dot_general(a, b, dn, precision=lax.Precision.HIGHEST, preferred_element_type=F32)
    ah, bh = a.astype(BF16), b.astype(BF16)
    out = lax.dot_general(ah, bh, dn, preferred_element_type=F32)
    if mode == "x3":
        al = (a - ah.astype(F32)).astype(BF16)
        bl = (b - bh.astype(F32)).astype(BF16)
        out = out + lax.dot_general(ah, bl, dn, preferred_element_type=F32)
        out = out + lax.dot_general(al, bh, dn, preferred_element_type=F32)
    return out


P_GRAM, P_INV, P_SOL, P_SCAN, P_SCANB, P_BWD = "lo", "lo", "lo", "lo", "lo", "lo"
P_CUM = "x3"


def _params(sem=None, vmem=None):
    kw = {}
    if sem is not None:
        kw["dimension_semantics"] = sem
    if vmem is not None:
        kw["vmem_limit_bytes"] = int(min(max(vmem, 32 * 2**20), VMEM_V7X - 8 * 2**20))
    return pltpu.CompilerParams(**kw)


def _in_hbm(*arrays):
    return [pltpu.with_memory_space_constraint(a, pltpu.HBM) for a in arrays]


def _sigmoid(x):
    return 1.0 / (1.0 + jnp.exp(-x))


def _dsilu(x, s):
    return s * (1.0 + x * (1.0 - s))


def _rows(shape):
    return lax.broadcasted_iota(jnp.int32, shape, 0)


def _shift_down(x, s):
    if s == 0:
        return x
    return jnp.where(_rows(x.shape) >= s, pltpu.roll(x, s, 0), 0.0)


def _shift_up(x, s):
    if s == 0:
        return x
    n = x.shape[0]
    return jnp.where(_rows(x.shape) < n - s, pltpu.roll(x, n - s, 0), 0.0)


def _matmul(a, b, dims, out_dtype, tm, tn, tk, name, add=None, n=None, b_outer=False):
    if dims == NN:
        (m, k), n = a.shape, b.shape[1]
    elif dims == NT:
        (m, k), n = a.shape, (n or b.shape[0])
    else:
        (k, m), n = a.shape, b.shape[1]
    tm, tn, tk = min(tm, m), min(tn, n), min(tk, k)
    assert m % tm == 0 and n % tn == 0 and k % tk == 0, (name, m, n, k, tm, tn, tk)
    nk = k // tk

    def body(*refs):
        if add is None:
            a_ref, b_ref, o_ref = refs[:3]
            add_ref = None
        else:
            a_ref, b_ref, add_ref, o_ref = refs[:4]
        part = _dot(a_ref[...], b_ref[...], dims)
        if nk == 1:
            if add_ref is not None:
                part = part + add_ref[...]
            o_ref[...] = part.astype(out_dtype)
            return
        acc = refs[-1]
        kk = pl.program_id(2)

        @pl.when(kk == 0)
        def _():
            acc[...] = part

        @pl.when(kk > 0)
        def _():
            acc[...] += part

        @pl.when(kk == nk - 1)
        def _():
            r = acc[...]
            if add_ref is not None:
                r = r + add_ref[...]
            o_ref[...] = r.astype(out_dtype)

    ij = (lambda g0, g1: (g1, g0)) if b_outer else (lambda g0, g1: (g0, g1))

    def spec(shape, pick):
        return pl.BlockSpec(shape, lambda g0, g1, kk: pick(*ij(g0, g1), kk))

    a_spec = spec((tk, tm), lambda i, j, kk: (kk, i)) if dims == TN else spec((tm, tk), lambda i, j, kk: (i, kk))
    b_spec = spec((tn, tk), lambda i, j, kk: (j, kk)) if dims == NT else spec((tk, tn), lambda i, j, kk: (kk, j))
    o_spec = spec((tm, tn), lambda i, j, kk: (i, j))
    in_specs = [a_spec, b_spec]
    args = [a, b]
    if add is not None:
        in_specs.append(o_spec)
        args.append(add)
    osz = jnp.dtype(out_dtype).itemsize
    est = 2 * (tm * tk * a.dtype.itemsize + tk * tn * b.dtype.itemsize + tm * tn * osz)
    est += 3 * tm * tn * 4 + (2 * tm * tn * 4 if add is not None else 0)
    return pl.pallas_call(
        body, name=name, grid=(n // tn, m // tm, nk) if b_outer else (m // tm, n // tn, nk),
        in_specs=in_specs, out_specs=o_spec,
        out_shape=jax.ShapeDtypeStruct((m, n), out_dtype),
        scratch_shapes=[pltpu.VMEM((tm, tn), F32)] if nk > 1 else [],
        compiler_params=_params(("parallel", "parallel", "arbitrary"), est + 8 * 2**20),
    )(*args)


def _cast_bf16(a, rows, name):
    r, c = a.shape
    rows = min(rows, r)

    def body(a_ref, o_ref):
        o_ref[...] = a_ref[...].astype(BF16)

    return pl.pallas_call(
        body, name=name, grid=(r // rows,),
        in_specs=[pl.BlockSpec((rows, c), lambda i: (i, 0))],
        out_specs=pl.BlockSpec((rows, c), lambda i: (i, 0)),
        out_shape=jax.ShapeDtypeStruct((r, c), BF16),
        compiler_params=_params(("parallel",)),
    )(a)


def _align_shard(wt):
    r, d = wt.shape
    cols = min(256, d)

    def body(w_ref, o_ref, pad_ref):
        chip = 2 * lax.axis_index("x") + lax.axis_index("y")
        pad_ref[...] = jnp.zeros_like(pad_ref)
        pad_ref[0:r, :] = w_ref[...]
        o_ref[...] = pltpu.roll(pad_ref[...], _by_chip(chip, SHIFTS), 0).astype(BF16)

    return pl.pallas_call(
        body, name="align_shard", grid=(d // cols,),
        in_specs=[pl.BlockSpec((r, cols), lambda i: (0, i))],
        out_specs=pl.BlockSpec((ALIGNED_W, cols), lambda i: (0, i)),
        out_shape=jax.ShapeDtypeStruct((ALIGNED_W, d), BF16),
        scratch_shapes=[pltpu.VMEM((ALIGNED_W, cols), F32)],
        compiler_params=_params(("parallel",)),
    )(wt)


def _rms_in(x, w):
    n, d = x.shape
    tr = min(256, n)

    def body(x_ref, w_ref, h_ref):
        xv = x_ref[...]
        r = lax.rsqrt(jnp.mean(xv * xv, axis=-1, keepdims=True) + EPS)
        h_ref[...] = (xv * r * w_ref[...]).astype(BF16)

    return pl.pallas_call(
        body, name="rms_in", grid=(n // tr,),
        in_specs=[pl.BlockSpec((tr, d), lambda i: (i, 0)), pl.BlockSpec((1, d), lambda i: (0, 0))],
        out_specs=pl.BlockSpec((tr, d), lambda i: (i, 0)),
        out_shape=jax.ShapeDtypeStruct((n, d), BF16),
        compiler_params=_params(("parallel",)),
    )(x, w)


def _conv_silu(p, w_ref, taps):
    c = None
    for j in range(taps):
        t = _shift_down(p, taps - 1 - j) * w_ref[j:j + 1, :]
        c = t if c is None else c + t
    return c


def _prep_qkv(proj, cw):
    n = proj.shape[0]

    def body(p3, wq, wk, wv, q_ref, k_ref, v_ref):
        for kind, (w_ref, o_ref) in enumerate(((wq, q_ref), (wk, k_ref), (wv, v_ref))):
            c = _conv_silu(p3[:, kind * DH:(kind + 1) * DH], w_ref, 4)
            a = c * _sigmoid(c)
            if kind < 2:
                r = lax.rsqrt(jnp.sum(a * a, axis=-1, keepdims=True) + EPS)
                a = a * (r * (DH ** -0.5 if kind == 0 else 1.0))
            o_ref[...] = a

    col = pl.BlockSpec((n, DH), lambda h: (0, h))
    wcol = lambda base: pl.BlockSpec((4, DH), lambda h: (0, base + h))
    out = jax.ShapeDtypeStruct((n, GW), F32)
    return pl.pallas_call(
        body, name="prep_qkv", grid=(HEADS,),
        in_specs=[pl.BlockSpec((n, 3 * DH), lambda h: (0, h)), wcol(QB), wcol(KB), wcol(VB)],
        out_specs=[col] * 3, out_shape=[out] * 3,
        compiler_params=_params(("parallel",), 40 * 2**20),
    )(proj, cw, cw, cw)


def _prep_qkv_bwd(proj, cw, dq, dk, dv, dproj):
    n = proj.shape[0]

    def body(p3, wq, wk, wv, dq_ref, dk_ref, dv_ref, _, o3, gq, gk, gv):
        for kind, (w_ref, d_ref, g_ref) in enumerate(((wq, dq_ref, gq), (wk, dk_ref, gk), (wv, dv_ref, gv))):
            p = p3[:, kind * DH:(kind + 1) * DH]
            shifted = [_shift_down(p, 3 - j) for j in range(4)]
            c = shifted[0] * w_ref[0:1, :]
            for j in range(1, 4):
                c = c + shifted[j] * w_ref[j:j + 1, :]
            s = _sigmoid(c)
            a = c * s
            d = d_ref[...]
            if kind < 2:
                r = lax.rsqrt(jnp.sum(a * a, axis=-1, keepdims=True) + EPS)
                sc = DH ** -0.5 if kind == 0 else 1.0
                d = (sc * r) * (d - a * ((r * r) * jnp.sum(d * a, axis=-1, keepdims=True)))
            dc = d * _dsilu(c, s)
            dp = None
            for j in range(4):
                g_ref[j:j + 1, :] = jnp.sum(dc * shifted[j], axis=0, keepdims=True)
                t = _shift_up(dc, 3 - j) * w_ref[j:j + 1, :]
                dp = t if dp is None else dp + t
            o3[:, kind * DH:(kind + 1) * DH] = dp.astype(BF16)

    col = pl.BlockSpec((n, DH), lambda h: (0, h))
    wcol = lambda base: pl.BlockSpec((4, DH), lambda h: (0, base + h))
    p3spec = pl.BlockSpec((n, 3 * DH), lambda h: (0, h))
    return pl.pallas_call(
        body, name="prep_qkv_bwd", grid=(HEADS,),
        in_specs=[p3spec, wcol(QB), wcol(KB), wcol(VB), col, col, col, ANY],
        out_specs=[p3spec] + [wcol(0)] * 3,
        out_shape=[jax.ShapeDtypeStruct(dproj.shape, BF16)] + [jax.ShapeDtypeStruct((4, GW), F32)] * 3,
        input_output_aliases={7: 0},
        compiler_params=_params(("parallel",), 48 * 2**20),
    )(proj, cw, cw, cw, dq, dk, dv, dproj)


CPB = 8
SCAN_CPS = 4


def _tri(lower, rows):
    i = lax.broadcasted_iota(jnp.int32, (rows, rows), 0)
    j = lax.broadcasted_iota(jnp.int32, (rows, rows), 1)
    return jnp.where((i // CH == j // CH) & ((i >= j) if lower else (j >= i)), 1.0, 0.0)


def _lane(shape):
    return lax.broadcasted_iota(jnp.int32, shape, 1)


def _prep_bg(proj, ad):
    n = proj.shape[0]
    nch = n // CH
    cpb = CPB if nch % CPB == 0 else 1
    rows = cpb * CH

    def body(p_ref, ad_ref, bg_ref, bgt_ref):
        p = p_ref[...]
        lane = _lane(p.shape)
        beta = _sigmoid(p)
        xa = p + ad_ref[1:2, :]
        sp = jnp.maximum(xa, 0.0) + jnp.log(1.0 + jnp.exp(-jnp.abs(xa)))
        g = pltpu.roll(-jnp.exp(ad_ref[0:1, :]) * sp, DH - A_LANE + HEADS, 1)
        gc = _dot(_tri(True, rows), g, NN, P_CUM)
        bg = jnp.where(lane < HEADS, beta, jnp.where(lane < 2 * HEADS, gc, 0.0))
        bg_ref[...] = bg
        for ci in range(cpb):
            bgt_ref[ci] = bg[ci * CH:(ci + 1) * CH, :].T

    return pl.pallas_call(
        body, name="prep_bg", grid=(nch // cpb,),
        in_specs=[pl.BlockSpec((rows, DH), lambda i: (i, BAB)), pl.BlockSpec((2, DH), lambda i: (0, 0))],
        out_specs=[pl.BlockSpec((rows, DH), lambda i: (i, 0)), pl.BlockSpec((cpb, DH, CH), lambda i: (i, 0, 0))],
        out_shape=[jax.ShapeDtypeStruct((n, DH), F32), jax.ShapeDtypeStruct((nch, DH, CH), F32)],
        compiler_params=_params(("parallel",)),
    )(*_in_hbm(proj, ad))


def _prep_bg_bwd(proj, ad, dbg, dproj):
    n = proj.shape[0]
    nch = n // CH
    cpb = CPB if nch % CPB == 0 else 1
    rows = cpb * CH

    def body(p_ref, ad_ref, d_ref, _, o_ref, ga_ref, gd_ref):
        p = p_ref[...]
        d = d_ref[...]
        lane = _lane(p.shape)
        beta = _sigmoid(p)
        xa = p + ad_ref[1:2, :]
        sp = jnp.maximum(xa, 0.0) + jnp.log(1.0 + jnp.exp(-jnp.abs(xa)))
        na = -jnp.exp(ad_ref[0:1, :])
        dg = pltpu.roll(_dot(_tri(False, rows), d, NN, P_CUM), A_LANE - HEADS, 1)
        da = dg * na * _sigmoid(xa)
        is_g = lane >= A_LANE
        o_ref[...] = jnp.where(lane < HEADS, d * beta * (1.0 - beta), jnp.where(is_g, da, 0.0)).astype(BF16)
        ga = jnp.sum(jnp.where(is_g, dg * na * sp, 0.0), axis=0, keepdims=True)
        gd = jnp.sum(jnp.where(is_g, da, 0.0), axis=0, keepdims=True)

        @pl.when(pl.program_id(0) == 0)
        def _():
            ga_ref[...] = jnp.zeros_like(ga_ref)
            gd_ref[...] = jnp.zeros_like(gd_ref)

        ga_ref[...] += ga
        gd_ref[...] += gd

    one = pl.BlockSpec((1, DH), lambda i: (0, 0))
    return pl.pallas_call(
        body, name="prep_bg_bwd", grid=(nch // cpb,),
        in_specs=[pl.BlockSpec((rows, DH), lambda i: (i, BAB)), pl.BlockSpec((2, DH), lambda i: (0, 0)),
                  pl.BlockSpec((rows, DH), lambda i: (i, 0)), ANY],
        out_specs=[pl.BlockSpec((rows, DH), lambda i: (i, BAB)), one, one],
        out_shape=[jax.ShapeDtypeStruct(dproj.shape, BF16), jax.ShapeDtypeStruct((1, DH), F32),
                   jax.ShapeDtypeStruct((1, DH), F32)],
        input_output_aliases={3: 0},
        compiler_params=_params(("arbitrary",)),
    )(proj, ad, dbg, dproj)


def _gdn_out(o, proj, wg):
    n = o.shape[0]

    def body(o_ref, z_ref, w_ref, y_ref):
        ov, z = o_ref[...], z_ref[...]
        r = lax.rsqrt(jnp.mean(ov * ov, axis=-1, keepdims=True) + EPS)
        y_ref[...] = (ov * r * w_ref[...] * (z * _sigmoid(z))).astype(BF16)

    return pl.pallas_call(
        body, name="gdn_out", grid=(HEADS,),
        in_specs=[pl.BlockSpec((n, DH), lambda h: (0, h)), pl.BlockSpec((n, DH), lambda h: (0, ZB + h)),
                  pl.BlockSpec((1, DH), lambda h: (0, 0))],
        out_specs=pl.BlockSpec((n, DH), lambda h: (0, h)),
        out_shape=jax.ShapeDtypeStruct((n, 2 * GW), BF16),
        compiler_params=_params(("parallel",)),
    )(o, proj, wg)


def _gdn_out_bwd(o, proj, wg, dmix):
    n = o.shape[0]

    def body(o_ref, z_ref, w_ref, d_ref, do_ref, dz_ref, gw_ref):
        ov, z, d, w = o_ref[...], z_ref[...], d_ref[...], w_ref[...]
        r = lax.rsqrt(jnp.mean(ov * ov, axis=-1, keepdims=True) + EPS)
        nrm = ov * r
        s = _sigmoid(z)
        dz_ref[...] = (d * (nrm * w) * _dsilu(z, s)).astype(BF16)
        dn_w = d * (z * s)
        gw = jnp.sum(dn_w * nrm, axis=0, keepdims=True)
        dn = dn_w * w
        do_ref[...] = r * (dn - nrm * jnp.mean(dn * nrm, axis=-1, keepdims=True))

        @pl.when(pl.program_id(0) == 0)
        def _():
            gw_ref[...] = jnp.zeros_like(gw_ref)

        gw_ref[...] += gw

    return pl.pallas_call(
        body, name="gdn_out_bwd", grid=(HEADS,),
        in_specs=[pl.BlockSpec((n, DH), lambda h: (0, h)), pl.BlockSpec((n, DH), lambda h: (0, ZB + h)),
                  pl.BlockSpec((1, DH), lambda h: (0, 0)), pl.BlockSpec((n, DH), lambda h: (0, h))],
        out_specs=[pl.BlockSpec((n, DH), lambda h: (0, h)), pl.BlockSpec((n, DH), lambda h: (0, ZB + h)),
                   pl.BlockSpec((1, DH), lambda h: (0, 0))],
        out_shape=[jax.ShapeDtypeStruct((n, GW), F32), jax.ShapeDtypeStruct((n, GW_COLS), BF16),
                   jax.ShapeDtypeStruct((1, DH), F32)],
        compiler_params=_params(("arbitrary",)),
    )(o, proj, wg, dmix)


def _conv_branch(proj, w3, b, mix):
    n = proj.shape[0]

    def body(p4, w_ref, b_ref, _, y_ref):
        u = p4[:, DH:2 * DH] * p4[:, 2 * DH:3 * DH]
        cc = _conv_silu(u, w_ref, 3) + b_ref[...]
        z = p4[:, 3 * DH:4 * DH]
        y_ref[...] = (p4[:, 0:DH] * cc * (z * _sigmoid(z))).astype(BF16)

    return pl.pallas_call(
        body, name="conv_branch", grid=(HEADS,),
        in_specs=[pl.BlockSpec((n, 4 * DH), lambda h: (0, h)), pl.BlockSpec((3, DH), lambda h: (0, h)),
                  pl.BlockSpec((1, DH), lambda h: (0, h)), ANY],
        out_specs=pl.BlockSpec((n, DH), lambda h: (0, HEADS + h)),
        out_shape=jax.ShapeDtypeStruct(mix.shape, BF16),
        input_output_aliases={3: 0},
        compiler_params=_params(("parallel",), 40 * 2**20),
    )(*_in_hbm(proj, w3, b, mix))


def _conv_branch_bwd(proj, w3, b, dmix):
    n = proj.shape[0]

    def body(p4, w_ref, b_ref, d_ref, o4, gw_ref, gbias_ref):
        gb, gcv, hc, z = p4[:, 0:DH], p4[:, DH:2 * DH], p4[:, 2 * DH:3 * DH], p4[:, 3 * DH:4 * DH]
        d = d_ref[...]
        dgb, dgc, dhc, dzc = (o4.at[:, kk * DH:(kk + 1) * DH] for kk in range(4))
        u = gcv * hc
        cc = _conv_silu(u, w_ref, 3) + b_ref[...]
        s = _sigmoid(z)
        dzc[...] = (d * (gb * cc) * _dsilu(z, s)).astype(BF16)
        dp = d * (z * s)
        dgb[...] = (dp * cc).astype(BF16)
        dcc = dp * gb
        gbias_ref[...] = jnp.sum(dcc, axis=0, keepdims=True)
        du = None
        for j in range(3):
            gw_ref[j:j + 1, :] = jnp.sum(dcc * _shift_down(u, 2 - j), axis=0, keepdims=True)
            t = _shift_up(dcc, 2 - j) * w_ref[j:j + 1, :]
            du = t if du is None else du + t
        dgc[...] = (du * hc).astype(BF16)
        dhc[...] = (du * gcv).astype(BF16)

    p4spec = pl.BlockSpec((n, 4 * DH), lambda h: (0, h))
    return pl.pallas_call(
        body, name="conv_branch_bwd", grid=(HEADS,),
        in_specs=[p4spec, pl.BlockSpec((3, DH), lambda h: (0, h)), pl.BlockSpec((1, DH), lambda h: (0, h)),
                  pl.BlockSpec((n, DH), lambda h: (0, HEADS + h))],
        out_specs=[p4spec, pl.BlockSpec((3, DH), lambda h: (0, h)), pl.BlockSpec((1, DH), lambda h: (0, h))],
        out_shape=[jax.ShapeDtypeStruct((n, CW_COLS), BF16), jax.ShapeDtypeStruct((3, GW), F32),
                   jax.ShapeDtypeStruct((1, GW), F32)],
        compiler_params=_params(("parallel",), 48 * 2**20),
    )(proj, w3, b, dmix)


def _final_loss(out, tgt, wf):
    n, d = out.shape
    tr = min(256, n)

    def body(o_ref, t_ref, w_ref, do_ref, dob_ref, gw_ref, loss_ref):
        ov, w = o_ref[...], w_ref[...]
        r = lax.rsqrt(jnp.mean(ov * ov, axis=-1, keepdims=True) + EPS)
        nrm = ov * r
        e = nrm * w - t_ref[...]
        dy = e * (1.0 / d)
        dn = dy * w
        dout = r * (dn - nrm * jnp.mean(dn * nrm, axis=-1, keepdims=True))
        do_ref[...] = dout
        dob_ref[...] = dout.astype(BF16)

        @pl.when(pl.program_id(0) == 0)
        def _():
            gw_ref[...] = jnp.zeros_like(gw_ref)
            loss_ref[...] = jnp.zeros_like(loss_ref)

        gw_ref[...] += jnp.sum(dy * nrm, axis=0, keepdims=True)
        loss_ref[...] += (0.5 / d) * jnp.sum(jnp.sum(e * e, axis=-1, keepdims=True), axis=0, keepdims=True)

    row = pl.BlockSpec((tr, d), lambda i: (i, 0))
    return pl.pallas_call(
        body, name="final_loss", grid=(n // tr,),
        in_specs=[row, row, pl.BlockSpec((1, d), lambda i: (0, 0))],
        out_specs=[row, row, pl.BlockSpec((1, d), lambda i: (0, 0)), pl.BlockSpec((1, 1), lambda i: (0, 0))],
        out_shape=[jax.ShapeDtypeStruct((n, d), F32), jax.ShapeDtypeStruct((n, d), BF16),
                   jax.ShapeDtypeStruct((1, d), F32), jax.ShapeDtypeStruct((1, 1), F32)],
        compiler_params=_params(("arbitrary",)),
    )(*_in_hbm(out, tgt, wf))


def _rms_in_bwd(x, w, dh, dout):
    n, d = x.shape
    tr = min(256, n)

    def body(x_ref, w_ref, dh_ref, do_ref, dx_ref, gw_ref):
        xv, dhv = x_ref[...], dh_ref[...]
        r = lax.rsqrt(jnp.mean(xv * xv, axis=-1, keepdims=True) + EPS)
        xn = xv * r
        dxn = dhv * w_ref[...]
        dx_ref[...] = r * (dxn - xn * jnp.mean(dxn * xn, axis=-1, keepdims=True)) + do_ref[...]

        @pl.when(pl.program_id(0) == 0)
        def _():
            gw_ref[...] = jnp.zeros_like(gw_ref)

        gw_ref[...] += jnp.sum(dhv * xn, axis=0, keepdims=True)

    row = pl.BlockSpec((tr, d), lambda i: (i, 0))
    one = pl.BlockSpec((1, d), lambda i: (0, 0))
    return pl.pallas_call(
        body, name="rms_in_bwd", grid=(n // tr,),
        in_specs=[row, one, row, row], out_specs=[row, one],
        out_shape=[jax.ShapeDtypeStruct((n, d), F32), jax.ShapeDtypeStruct((1, d), F32)],
        compiler_params=_params(("arbitrary",)),
    )(*_in_hbm(x, w, dh, dout))


def _ij():
    i = lax.broadcasted_iota(jnp.int32, (CH, CH), 0)
    j = lax.broadcasted_iota(jnp.int32, (CH, CH), 1)
    return i, j


def _unit_lower_inverse(mats):
    i, j = _ij()
    eye = jnp.where(i == j, 1.0, 0.0)
    same16 = (i // 16) == (j // 16)
    same32 = (i // 32) == (j // 32)
    mm = lambda xs, ys: [_dot(x, y, NN, P_INV) for x, y in zip(xs, ys)]
    n1 = [jnp.where(same16, -a, 0.0) for a in mats]
    n2 = mm(n1, n1)
    n4 = mm(n2, n2)
    n8 = mm(n4, n4)
    t = [eye + x1 + x2 + x3 for x1, x2, x3 in zip(n1, n2, mm(n1, n2))]
    t = [x + y for x, y in zip(t, mm(t, n4))]
    t = [x + y for x, y in zip(t, mm(t, n8))]
    a1 = [jnp.where(same32 & jnp.logical_not(same16), a, 0.0) for a in mats]
    t = [x - y for x, y in zip(t, mm(t, mm(a1, t)))]
    a2 = [jnp.where(same32, 0.0, a) for a in mats]
    t = [x - y for x, y in zip(t, mm(t, mm(a2, t)))]
    return t


def _head_vectors(bg, bgt, h):
    bcol = bg[:, h:h + 1]
    gcol = bg[:, HEADS + h:HEADS + h + 1]
    grow = bgt[HEADS + h:HEADS + h + 1, :]
    return bcol, gcol, grow


def _decay(gcol, grow):
    i, j = _ij()
    return jnp.where(i >= j, jnp.exp(jnp.where(i >= j, gcol - grow, 0.0)), 0.0)


def _gdn_intra(q, k, v, bg, bgt):
    n = q.shape[0]
    nch = n // CH
    cps = 4 if nch % 4 == 0 else 1

    def body(q_ref, k_ref, v_ref, bg_ref, bgt_ref, u_ref, w_ref, p_ref, t_ref):
        i, j = _ij()
        items = [(ci, h) for ci in range(cps) for h in range(HEADS)]
        at = lambda ref, ci, h: ref.at[ci * CH:(ci + 1) * CH, h * DH:(h + 1) * DH]
        bgs = [bg_ref[ci * CH:(ci + 1) * CH, :] for ci in range(cps)]
        ks = [at(k_ref, ci, h)[...] for ci, h in items]
        vecs = [_head_vectors(bgs[ci], bgt_ref[ci], h) for ci, h in items]
        decs = [_decay(gcol, grow) for _, gcol, grow in vecs]
        kks = [_dot(kh, kh, NT, P_GRAM) for kh in ks]
        qks = [_dot(at(q_ref, ci, h)[...], kh, NT, P_GRAM) for (ci, h), kh in zip(items, ks)]
        ts = _unit_lower_inverse([jnp.where(i > j, bcol * kk * dec, 0.0)
                                  for (bcol, _, _), kk, dec in zip(vecs, kks, decs)])
        us = [_dot(t, at(v_ref, ci, h)[...] * bcol, NN, P_SOL) for t, (ci, h), (bcol, _, _) in zip(ts, items, vecs)]
        ws = [_dot(t, kh * (bcol * jnp.exp(gcol)), NN, P_SOL) for t, kh, (bcol, gcol, _) in zip(ts, ks, vecs)]
        for n_, (ci, h) in enumerate(items):
            p_ref[ci, h] = qks[n_] * decs[n_]
            t_ref[ci, h] = ts[n_]
            at(u_ref, ci, h)[...] = us[n_]
            at(w_ref, ci, h)[...] = ws[n_]

    row = pl.BlockSpec((cps * CH, GW), lambda c: (c, 0))
    sq = pl.BlockSpec((cps, HEADS, CH, CH), lambda c: (c, 0, 0, 0))
    big = jax.ShapeDtypeStruct((n, GW), F32)
    sqs = jax.ShapeDtypeStruct((nch, HEADS, CH, CH), F32)
    return pl.pallas_call(
        body, name="gdn_intra", grid=(nch // cps,),
        in_specs=[row, row, row, pl.BlockSpec((cps * CH, DH), lambda c: (c, 0)),
                  pl.BlockSpec((cps, DH, CH), lambda c: (c, 0, 0))],
        out_specs=[row, row, sq, sq], out_shape=[big, big, sqs, sqs],
        compiler_params=_params(("parallel",)),
    )(q, k, v, bg, bgt)


def _gdn_scan(q, k, bg, u, w, p):
    n = q.shape[0]
    nch = n // CH
    cps = SCAN_CPS if nch % SCAN_CPS == 0 else 1

    def body(q_ref, k_ref, bg_ref, u_ref, w_ref, p_ref, o_ref, vn_ref, s_out, s_scr):
        @pl.when(pl.program_id(0) == 0)
        def _():
            s_scr[...] = jnp.zeros_like(s_scr)

        hs = range(HEADS)
        sls = [slice(h * DH, (h + 1) * DH) for h in hs]
        ss = [s_scr[h] for h in hs]
        for ci in range(cps):
            rs = slice(ci * CH, (ci + 1) * CH)
            bg = bg_ref[rs, :]
            gcols = [bg[:, HEADS + h:HEADS + h + 1] for h in hs]
            glasts = [g[CH - 1:CH, :] for g in gcols]
            wss = [_dot(w_ref[rs, sl], s, NN, P_SCAN) for sl, s in zip(sls, ss)]
            oqs = [_dot(q_ref[rs, sl] * jnp.exp(g), s, NN, P_SCAN) for sl, s, g in zip(sls, ss, gcols)]
            vns = [u_ref[rs, sl] - x for sl, x in zip(sls, wss)]
            ops = [_dot(p_ref[ci, h], vn, NN, P_SCAN) for h, vn in zip(hs, vns)]
            sns = [_dot(k_ref[rs, sl] * jnp.exp(gl - g), vn, TN, P_SCAN)
                   for sl, gl, g, vn in zip(sls, glasts, gcols, vns)]
            for h, sl in enumerate(sls):
                s_out[ci, :, sl] = ss[h]
                vn_ref[rs, sl] = vns[h]
                o_ref[rs, sl] = oqs[h] + ops[h]
            ss = [s * jnp.exp(gl) + sn for s, gl, sn in zip(ss, glasts, sns)]
        for h in hs:
            s_scr[h] = ss[h]

    row = pl.BlockSpec((cps * CH, GW), lambda c: (c, 0))
    big = jax.ShapeDtypeStruct((n, GW), F32)
    return pl.pallas_call(
        body, name="gdn_scan", grid=(nch // cps,),
        in_specs=[row, row, pl.BlockSpec((cps * CH, DH), lambda c: (c, 0)), row, row,
                  pl.BlockSpec((cps, HEADS, CH, CH), lambda c: (c, 0, 0, 0))],
        out_specs=[row, row, pl.BlockSpec((cps, DH, GW), lambda c: (c, 0, 0))],
        out_shape=[big, big, jax.ShapeDtypeStruct((nch, DH, GW), F32)],
        scratch_shapes=[pltpu.VMEM((HEADS, DH, DH), F32)],
        compiler_params=_params(("arbitrary",)),
    )(q, k, bg, u, w, p)


def _gdn_scan_bwd(q, k, bg, w, p, vn, s_in, do):
    n = q.shape[0]
    nch = n // CH
    cps = SCAN_CPS if nch % SCAN_CPS == 0 else 1
    rev = lambda c: nch // cps - 1 - c

    def body(q_ref, k_ref, bg_ref, w_ref, p_ref, vn_ref, s_ref, do_ref,
             dqg_ref, dp_ref, du_ref, dw_ref, dks_ref, dgam_ref, ds_scr):
        @pl.when(pl.program_id(0) == 0)
        def _():
            ds_scr[...] = jnp.zeros_like(ds_scr)

        lane = _lane((1, DH))
        hs = range(HEADS)
        sls = [slice(h * DH, (h + 1) * DH) for h in hs]
        dss = [ds_scr[h] for h in hs]
        for ci in reversed(range(cps)):
            rs = slice(ci * CH, (ci + 1) * CH)
            bg = bg_ref[rs, :]
            gcols = [bg[:, HEADS + h:HEADS + h + 1] for h in hs]
            glasts = [g[CH - 1:CH, :] for g in gcols]
            ss = [s_ref[ci, :, sl] for sl in sls]
            dos = [do_ref[rs, sl] for sl in sls]
            vnl = [vn_ref[rs, sl] for sl in sls]
            dqgs = [_dot(d, s, NT, P_SCANB) for d, s in zip(dos, ss)]
            dps = [_dot(d, vn, NT, P_SCANB) for d, vn in zip(dos, vnl)]
            dvn1 = [_dot(p_ref[ci, h], d, TN, P_SCANB) for h, d in zip(hs, dos)]
            dvn2 = [_dot(k_ref[rs, sl] * jnp.exp(gl - g), ds, NN, P_SCANB)
                    for sl, gl, g, ds in zip(sls, glasts, gcols, dss)]
            dkss = [_dot(vn, ds, NT, P_SCANB) for vn, ds in zip(vnl, dss)]
            dsq = [_dot(q_ref[rs, sl] * jnp.exp(g), d, TN, P_SCANB) for sl, g, d in zip(sls, gcols, dos)]
            dvns = [a + b for a, b in zip(dvn1, dvn2)]
            dws = [_dot(dvn, s, NT, P_SCANB) for dvn, s in zip(dvns, ss)]
            dsw = [_dot(w_ref[rs, sl], dvn, TN, P_SCANB) for sl, dvn in zip(sls, dvns)]
            dgam = jnp.zeros((1, DH), F32)
            for h, sl in enumerate(sls):
                dqg_ref[rs, sl] = dqgs[h]
                dp_ref[ci, h] = dps[h]
                du_ref[rs, sl] = dvns[h]
                dw_ref[rs, sl] = -dws[h]
                dks_ref[rs, sl] = dkss[h]
                tot = jnp.sum(jnp.sum(dss[h] * ss[h], axis=-1, keepdims=True), axis=0, keepdims=True)
                dgam = dgam + jnp.where(lane == h, tot, 0.0)
            dgam_ref[ci] = jnp.broadcast_to(dgam, (8, DH))
            dss = [ds * jnp.exp(gl) + a - b for ds, gl, a, b in zip(dss, glasts, dsq, dsw)]
        for h in hs:
            ds_scr[h] = dss[h]

    row = pl.BlockSpec((cps * CH, GW), lambda c: (rev(c), 0))
    sq = pl.BlockSpec((cps, HEADS, CH, CH), lambda c: (rev(c), 0, 0, 0))
    big = jax.ShapeDtypeStruct((n, GW), F32)
    return pl.pallas_call(
        body, name="gdn_scan_bwd", grid=(nch // cps,),
        in_specs=[row, row, pl.BlockSpec((cps * CH, DH), lambda c: (rev(c), 0)), row, sq, row,
                  pl.BlockSpec((cps, DH, GW), lambda c: (rev(c), 0, 0)), row],
        out_specs=[row, sq, row, row, row, pl.BlockSpec((cps, 8, DH), lambda c: (rev(c), 0, 0))],
        out_shape=[big, jax.ShapeDtypeStruct((nch, HEADS, CH, CH), F32), big, big, big,
                   jax.ShapeDtypeStruct((nch, 8, DH), F32)],
        scratch_shapes=[pltpu.VMEM((HEADS, DH, DH), F32)],
        compiler_params=_params(("arbitrary",)),
    )(q, k, bg, w, p, vn, s_in, do)


def _gdn_intra_bwd(q, k, v, bg, bgt, t, u, w, p, dqg, dp, du, dw, dks, dgam):
    n = q.shape[0]
    nch = n // CH
    cps = 1

    def body(q_ref, k_ref, v_ref, bg_ref, bgt_ref, t_ref, u_ref, w_ref, p_ref,
             dqg_ref, dp_ref, du_ref, dw_ref, dks_ref, dgam_ref, dq_ref, dk_ref, dv_ref, dbg_ref):
        i, j = _ij()
        rows1 = lax.broadcasted_iota(jnp.int32, (CH, 1), 0)
        lane = _lane((CH, DH))
        rsum = lambda x: jnp.sum(x, axis=-1, keepdims=True)
        items = [(ci, h) for ci in range(cps) for h in range(HEADS)]
        at = lambda ref, it: ref.at[it[0] * CH:(it[0] + 1) * CH, it[1] * DH:(it[1] + 1) * DH]
        ld = lambda ref: [at(ref, it)[...] for it in items]
        bgs = [bg_ref[ci * CH:(ci + 1) * CH, :] for ci in range(cps)]
        qs, ks = ld(q_ref), ld(k_ref)
        vecs = [_head_vectors(bgs[ci], bgt_ref[ci], h) for ci, h in items]
        decs = [_decay(gcol, grow) for _, gcol, grow in vecs]
        ths = [t_ref[ci, h] for ci, h in items]
        drus = [_dot(th, x_, TN, P_BWD) for th, x_ in zip(ths, ld(du_ref))]
        drws = [_dot(th, x_, TN, P_BWD) for th, x_ in zip(ths, ld(dw_ref))]
        kks = [_dot(kh, kh, NT, P_GRAM) for kh in ks]
        da1 = [_dot(dru, x_, NT, P_BWD) for dru, x_ in zip(drus, ld(u_ref))]
        da2 = [_dot(drw, x_, NT, P_BWD) for drw, x_ in zip(drws, ld(w_ref))]
        das = [jnp.where(i > j, -(x_ + y_), 0.0) for x_, y_ in zip(da1, da2)]
        dkks = [da * bcol * dec for da, (bcol, _, _), dec in zip(das, vecs, decs)]
        dps = [dp_ref[ci, h] for ci, h in items]
        dqks = [dp_ * dec for dp_, dec in zip(dps, decs)]
        dq_ps = [_dot(dqk, kh, NN, P_BWD) for dqk, kh in zip(dqks, ks)]
        dk_ps = [_dot(dqk, qh, TN, P_BWD) for dqk, qh in zip(dqks, qs)]
        dk_as = [_dot(dkk, kh, NN, P_BWD) for dkk, kh in zip(dkks, ks)]
        dk_bs = [_dot(dkk, kh, TN, P_BWD) for dkk, kh in zip(dkks, ks)]
        bcols = [vc[0] for vc in vecs]
        gcols = [vc[1] for vc in vecs]
        gams = [jnp.exp(g) for g in gcols]
        glasts = [g[CH - 1:CH, :] for g in gcols]
        es = [jnp.exp(gl - g) for gl, g in zip(glasts, gcols)]
        kgs = [kh * gam for kh, gam in zip(ks, gams)]
        dqgs, dkss = ld(dqg_ref), ld(dks_ref)
        r_uv = [rsum(dru * x_) for dru, x_ in zip(drus, ld(v_ref))]
        r_wk = [rsum(drw * kg) for drw, kg in zip(drws, kgs)]
        r_ak = [rsum(da * kk * dec) for da, kk, dec in zip(das, kks, decs)]
        r_qq = [rsum(dqg * qh) for dqg, qh in zip(dqgs, qs)]
        tks = [rsum(dk_ * kh) * e for dk_, kh, e in zip(dkss, ks, es)]
        mdecs = [da * (bcol * kk * dec) + dp_ * p_ref[ci, h]
                 for (ci, h), da, bcol, kk, dec, dp_ in zip(items, das, bcols, kks, decs, dps)]
        r_md = [rsum(m) for m in mdecs]
        c_md = [rsum(jnp.where(i == j, jnp.sum(m, axis=0, keepdims=True), 0.0)) for m in mdecs]
        dbgs = [jnp.zeros((CH, DH), F32) for _ in range(cps)]
        for n_, (ci, h) in enumerate(items):
            at(dv_ref, (ci, h))[...] = bcols[n_] * drus[n_]
            at(dq_ref, (ci, h))[...] = gams[n_] * dqgs[n_] + dq_ps[n_]
            at(dk_ref, (ci, h))[...] = ((bcols[n_] * gams[n_]) * drws[n_] + dk_ps[n_] + dk_as[n_] + dk_bs[n_]
                                        + dkss[n_] * es[n_])
            dbeta = r_uv[n_] + r_wk[n_] + r_ak[n_]
            dglast = (jnp.sum(tks[n_], axis=0, keepdims=True)
                      + dgam_ref[ci, 0:1, h:h + 1] * jnp.exp(glasts[n_]))
            dgc = (r_wk[n_] * bcols[n_] + r_md[n_] - c_md[n_] + r_qq[n_] * gams[n_] - tks[n_]
                   + jnp.where(rows1 == CH - 1, dglast, 0.0))
            dbgs[ci] = dbgs[ci] + jnp.where(lane == h, dbeta, 0.0) + jnp.where(lane == HEADS + h, dgc, 0.0)
        for ci in range(cps):
            dbg_ref[ci * CH:(ci + 1) * CH, :] = dbgs[ci]

    row = pl.BlockSpec((cps * CH, GW), lambda c: (c, 0))
    sq = pl.BlockSpec((cps, HEADS, CH, CH), lambda c: (c, 0, 0, 0))
    small = pl.BlockSpec((cps * CH, DH), lambda c: (c, 0))
    big = jax.ShapeDtypeStruct((n, GW), F32)
    return pl.pallas_call(
        body, name="gdn_intra_bwd", grid=(nch // cps,),
        in_specs=[row, row, row, small, pl.BlockSpec((cps, DH, CH), lambda c: (c, 0, 0)), sq, row, row, sq,
                  row, sq, row, row, row, pl.BlockSpec((cps, 8, DH), lambda c: (c, 0, 0))],
        out_specs=[row, row, row, small],
        out_shape=[big, big, big, jax.ShapeDtypeStruct((n, DH), F32)],
        compiler_params=_params(("parallel",)),
    )(q, k, v, bg, bgt, t, u, w, p, dqg, dp, du, dw, dks, dgam)


def _local_step(x, tgt, h, w_g, cqw, late, norm_in_w, ad, gdn_norm_w, conv_b, final_norm_w,
                on_grad_c=None, on_grad_g=None, on_q=None):
    proj_g = _matmul(h, w_g, NT, F32, 512, 1408, 1024, "mm_proj_g", n=GW_COLS, b_outer=True)
    q, k, v = _prep_qkv(proj_g, cqw)
    if on_q is not None:
        q = on_q(q)
    bg, bgt = _prep_bg(proj_g, ad)
    u, w, p, t = _gdn_intra(q, k, v, bg, bgt)
    o, vn, s_in = _gdn_scan(q, k, bg, u, w, p)
    w_c, w_out, conv_w = late(o)
    proj_c = _matmul(h, w_c, NT, F32, 512, 1024, 1024, "mm_proj_c", n=CW_COLS, b_outer=True)
    mix = _conv_branch(proj_c, conv_w, conv_b, _gdn_out(o, proj_g, gdn_norm_w))
    out = _matmul(mix, w_out, NN, F32, 512, 512, 2048, "mm_out", add=x)
    dout, dout_b, g_fn, loss = _final_loss(out, tgt, final_norm_w)

    dmix = _matmul(dout_b, w_out, NT, F32, 512, 1024, 1024, "mm_dmix", b_outer=True)
    g_wout = _matmul(mix, dout_b, TN, BF16, 512, 512, 2048, "mm_gwout")
    do, dproj_g, g_gn = _gdn_out_bwd(o, proj_g, gdn_norm_w, dmix)
    dproj_c, g_cw, g_cb = _conv_branch_bwd(proj_c, conv_w, conv_b, dmix)
    g_c = _matmul(dproj_c, h, TN, BF16, 1024, 512, 2048, "mm_gwin_c")
    if on_grad_c is not None:
        do = on_grad_c(g_c, g_wout, do)
    dqg, dp, du, dw, dks, dgam = _gdn_scan_bwd(q, k, bg, w, p, vn, s_in, do)
    dq, dk, dv, dbg = _gdn_intra_bwd(q, k, v, bg, bgt, t, u, w, p, dqg, dp, du, dw, dks, dgam)
    dproj_g, gq, gk, gv = _prep_qkv_bwd(proj_g, cqw, dq, dk, dv, dproj_g)
    dproj_g, g_al, g_dt = _prep_bg_bwd(proj_g, ad, dbg, dproj_g)
    g_g = _matmul(dproj_g, h, TN, BF16, 1408, 512, 2048, "mm_gwin_g")
    if on_grad_g is not None:
        dproj_g = on_grad_g(g_g, dproj_g)
    dh = _matmul(dproj_g, w_g, NN, F32, 1024, 1024, 1408, "mm_dh_g")
    dh = _matmul(dproj_c, w_c, NN, F32, 1024, 1024, 1024, "mm_dh_c", add=dh)
    gx, g_nin = _rms_in_bwd(x, norm_in_w, dh, dout)
    small = dict(nin=g_nin, cb=g_cb, fn=g_fn, al=g_al, dt=g_dt, gn=g_gn, cq=(gq, gk, gv), cw=g_cw, loss=loss)
    return gx, small, (g_g, g_c, g_wout)


def _place():
    x, y, c = lax.axis_index("x"), lax.axis_index("y"), lax.axis_index("c")
    chips = [(1 - x, y), (x, 1 - y), (1 - x, 1 - y)]
    return x, y, c, chips


def _blk(ref, b):
    if isinstance(b, int):
        return ref.at[b * DH:(b + 1) * DH, :]
    return ref.at[pl.ds(pl.multiple_of(b * DH, DH), DH), :]


HBM = pl.BlockSpec(memory_space=pltpu.HBM)
SEM = pl.BlockSpec(memory_space=pltpu.SEMAPHORE)
EFFECT = pltpu.SideEffectType.DATAFLOW_SIDE_EFFECTING


def _split_start(name, issue, bufs, n_sems):
    nbuf = len(bufs)

    def body(*refs):
        issue(refs[:nbuf], refs[nbuf], refs[nbuf + 1])
        refs[-1][...] = jnp.zeros_like(refs[-1])

    out = pl.pallas_call(
        body, name=name,
        out_shape=(pltpu.SemaphoreType.DMA((n_sems,)), pltpu.SemaphoreType.DMA((n_sems,)),
                   *[pltpu.HBM(b.shape, b.dtype) for b in bufs], jax.ShapeDtypeStruct((8, DH), F32)),
        in_specs=[HBM] * nbuf,
        out_specs=(SEM, SEM, *[HBM] * nbuf, pl.BlockSpec(memory_space=pltpu.VMEM)),
        input_output_aliases={a: 2 + a for a in range(nbuf)},
        compiler_params=pltpu.CompilerParams(has_side_effects=EFFECT),
    )(*[pltpu.with_memory_space_constraint(b, pltpu.HBM) for b in bufs])
    return out[0], out[1], list(out[2:2 + nbuf]), out[-1]


def _split_wait(name, await_, send_sems, recv_sems, bufs, after):
    nbuf = len(bufs)

    def body(*refs):
        await_(refs[:nbuf], refs[nbuf], refs[nbuf + 1])

    out = pl.pallas_call(
        body, name=name,
        out_shape=tuple(pltpu.HBM(b.shape, b.dtype) for b in bufs),
        in_specs=[HBM] * nbuf + [SEM, SEM, ANY], out_specs=tuple([HBM] * nbuf),
        input_output_aliases={a: a for a in range(nbuf)},
        compiler_params=pltpu.CompilerParams(has_side_effects=EFFECT),
    )(*bufs, send_sems, recv_sems, after)
    return list(out)


def _phase_blocks(chip, phase, edges, parity=None):
    return [(b, blk) for b, (grp, blk) in enumerate(_shard_blocks(chip, edges))
            if grp == phase and (parity is None or b % 2 == parity)]


def _cols(ref, nblk):
    return ref.at[0:nblk * DH, :]


def _block_table(chip, edges, spare_g, spare_c):
    rows = []
    for s in range(4):
        sb = _shard_blocks(s, edges)
        rows.append([[blk if grp == "g" else spare_g for grp, blk in sb],
                     [blk if grp == "c" else spare_c for grp, blk in sb],
                     [int(grp == "g") for grp, _ in sb], [s] * ALIGNED_BLOCKS])
    return jnp.asarray(rows, jnp.int32)[chip]


def _place_own(a_shard, wo, cq, cw, bufs):
    d = a_shard.shape[1]
    chip = 2 * lax.axis_index("x") + lax.axis_index("y")

    def body(t_ref, a_ref, wo_ref, cq_ref, cw_ref, *refs):
        wg_ref, wc_ref, wog_ref, cqg_ref, cwg_ref = refs[5:]
        wg_ref[...] = a_ref[...]
        wc_ref[...] = a_ref[...]

        @pl.when(pl.program_id(0) == 0)
        def _():
            wog_ref[0] = wo_ref[...]
            cqg_ref[0] = cq_ref[...]
            cwg_ref[0] = cw_ref[...]

    whole = lambda s: pl.BlockSpec(s.shape, lambda b, t: (0,) * s.ndim)
    slot = lambda s: pl.BlockSpec((1,) + s.shape, lambda b, t: (t[3, 0],) + (0,) * s.ndim)
    return pl.pallas_call(
        body, name="place_own",
        grid_spec=pltpu.PrefetchScalarGridSpec(
            num_scalar_prefetch=1, grid=(ALIGNED_BLOCKS,),
            in_specs=[pl.BlockSpec((DH, d), lambda b, t: (b, 0)), whole(wo), whole(cq), whole(cw)] + [ANY] * 5,
            out_specs=[pl.BlockSpec((DH, d), lambda b, t: (t[0, b], 0)),
                       pl.BlockSpec((DH, d), lambda b, t: (t[1, b], 0)), slot(wo), slot(cq), slot(cw)]),
        out_shape=[jax.ShapeDtypeStruct(b.shape, b.dtype) for b in bufs],
        input_output_aliases={5 + a: a for a in range(5)},
        compiler_params=_params(("arbitrary",)),
    )(_block_table(chip, True, G_SPARE, C_SPARE), a_shard, wo, cq, cw, *bufs)


def _tie(x, token, name):
    def body(x_ref, t_ref, o_ref):
        del x_ref, t_ref, o_ref

    return pl.pallas_call(
        body, name=name, in_specs=[ANY, ANY], out_specs=ANY,
        out_shape=jax.ShapeDtypeStruct(x.shape, x.dtype), input_output_aliases={0: 0},
    )(x, token)


def _gather_start(phase, a_shard, w_grp, singles):
    ns = len(singles)

    def issue(refs, send_sems, recv_sems):
        a_ref, w_ref = refs[0], refs[1]
        x, y, c, chips = _place()
        mine = 2 * x + y
        for jj, (px, py) in enumerate(chips):
            to = dict(device_id=(px, py, c), device_id_type=MESH)
            for a in range(ns):
                pltpu.make_async_remote_copy(
                    src_ref=refs[2 + 2 * a], dst_ref=refs[3 + 2 * a].at[mine],
                    send_sem=send_sems.at[(1 + ns) * jj + 1 + a], recv_sem=recv_sems.at[(1 + ns) * jj + 1 + a],
                    **to).start()
        for s in range(4):
            for par in range(2):
                blocks = _phase_blocks(s, phase, True, par)
                if blocks:
                    @pl.when((mine == s) & (c == par))
                    def _():
                        for jj, (px, py) in enumerate(chips):
                            for b, blk in blocks:
                                pltpu.make_async_remote_copy(
                                    src_ref=_blk(a_ref, b), dst_ref=_blk(w_ref, blk),
                                    send_sem=send_sems.at[(1 + ns) * jj], recv_sem=recv_sems.at[(1 + ns) * jj],
                                    device_id=(px, py, c), device_id_type=MESH).start()

    bufs = [a_shard, w_grp] + [t for pair in singles for t in pair]
    return _split_start("gather_start_" + phase, issue, bufs, 3 * (1 + ns))


def _gather_wait(phase, send_sems, recv_sems, bufs, after):
    ns = (len(bufs) - 2) // 2

    def await_(refs, send_sems, recv_sems):
        a_ref, w_ref = refs[0], refs[1]
        x, y, c, chips = _place()
        mine = 2 * x + y
        for jj, (px, py) in enumerate(chips):
            to = dict(device_id=(px, py, c), device_id_type=MESH)
            peer = 2 * px + py
            for a in range(ns):
                cp = pltpu.make_async_remote_copy(
                    src_ref=refs[2 + 2 * a], dst_ref=refs[3 + 2 * a].at[mine],
                    send_sem=send_sems.at[(1 + ns) * jj + 1 + a], recv_sem=recv_sems.at[(1 + ns) * jj + 1 + a], **to)
                cp.wait_recv()
                cp.wait_send()
            for s in range(4):
                for par in range(2):
                    nblk = len(_phase_blocks(s, phase, True, par))
                    if nblk:
                        both = pltpu.make_async_remote_copy(
                            src_ref=_cols(a_ref, nblk), dst_ref=_cols(w_ref, nblk),
                            send_sem=send_sems.at[(1 + ns) * jj], recv_sem=recv_sems.at[(1 + ns) * jj], **to)

                        @pl.when((peer == s) & (c == par))
                        def _():
                            both.wait_recv()

                        @pl.when((mine == s) & (c == par))
                        def _():
                            both.wait_send()

    return _split_wait("gather_wait_" + phase, await_, send_sems, recv_sems, bufs, after)


def _sibling_forward_parts(phase):
    def each(w_ref, send_sems, recv_sems, start):
        x, y, c, chips = _place()
        to = dict(device_id=(x, y, 1 - c), device_id_type=MESH)
        for jj, (px, py) in enumerate(chips):
            peer = 2 * px + py
            for s in range(4):
                for par in range(2):
                    mine_blocks = _phase_blocks(s, phase, True, par)
                    theirs = len(_phase_blocks(s, phase, True, 1 - par))
                    if not (mine_blocks or theirs):
                        continue

                    @pl.when((peer == s) & (c == par))
                    def _():
                        if start:
                            for _, blk in mine_blocks:
                                pltpu.make_async_remote_copy(
                                    src_ref=_blk(w_ref, blk), dst_ref=_blk(w_ref, blk),
                                    send_sem=send_sems.at[jj], recv_sem=recv_sems.at[jj], **to).start()
                            return
                        if theirs:
                            pltpu.make_async_remote_copy(
                                src_ref=_cols(w_ref, theirs), dst_ref=_cols(w_ref, theirs),
                                send_sem=send_sems.at[jj], recv_sem=recv_sems.at[jj], **to).wait_recv()
                        if mine_blocks:
                            pltpu.make_async_remote_copy(
                                src_ref=_cols(w_ref, len(mine_blocks)), dst_ref=_cols(w_ref, len(mine_blocks)),
                                send_sem=send_sems.at[jj], recv_sem=recv_sems.at[jj], **to).wait_send()

    issue = lambda refs, send_sems, recv_sems: each(refs[0], send_sems, recv_sems, True)
    await_ = lambda refs, send_sems, recv_sems: each(refs[0], send_sems, recv_sems, False)
    return issue, await_


def _sibling_forward(phase, w_grp):
    issue, await_ = _sibling_forward_parts(phase)

    def body(w_in_ref, w_ref, send_sems, recv_sems):
        del w_in_ref
        issue([w_ref], send_sems, recv_sems)
        await_([w_ref], send_sems, recv_sems)

    return pl.pallas_call(
        body, name="sibling_forward_" + phase, in_specs=[ANY], out_specs=ANY,
        out_shape=jax.ShapeDtypeStruct(w_grp.shape, w_grp.dtype), input_output_aliases={0: 0},
        scratch_shapes=[pltpu.SemaphoreType.DMA((3,)), pltpu.SemaphoreType.DMA((3,))],
    )(w_grp)


def _merge_edges(w, edge0, mixed, name):
    d = w.shape[1]

    def body(e_ref, o_ref):
        o_ref[...] = e_ref[0:DH, :] + e_ref[DH:2 * DH, :]

    def to_block(i):
        r = mixed[-1]
        for kk in range(len(mixed) - 2, -1, -1):
            r = jnp.where(i == kk, mixed[kk], r)
        return r

    return pl.pallas_call(
        body, name=name, grid=(len(mixed),),
        in_specs=[pl.BlockSpec((2 * DH, d), lambda i: (edge0 // 2 + i, 0))],
        out_specs=pl.BlockSpec((DH, d), lambda i: (to_block(i), 0)),
        out_shape=jax.ShapeDtypeStruct(w.shape, w.dtype),
        input_output_aliases={0: 0},
        compiler_params=_params(("arbitrary",)),
    )(w)


def _scatter_start(phase, g_grp, land, singles, halved=False):
    ns = len(singles)

    def issue(refs, send_sems, recv_sems):
        g_ref, land_ref = refs[0], refs[1]
        x, y, c, chips = _place()
        for jj, (px, py) in enumerate(chips):
            to = dict(device_id=(px, py, c), device_id_type=MESH)
            peer = 2 * px + py
            for a in range(ns):
                pltpu.make_async_remote_copy(
                    src_ref=refs[2 + 2 * a].at[peer], dst_ref=refs[3 + 2 * a].at[jj],
                    send_sem=send_sems.at[(1 + ns) * jj + 1 + a], recv_sem=recv_sems.at[(1 + ns) * jj + 1 + a],
                    **to).start()
            for s in range(4):
                for par in ((0, 1) if halved else (None,)):
                    blocks = _phase_blocks(s, phase, False, par)
                    if blocks:
                        @pl.when((peer == s) if par is None else ((peer == s) & (c == par)))
                        def _():
                            for b, blk in blocks:
                                pltpu.make_async_remote_copy(
                                    src_ref=_blk(g_ref, blk), dst_ref=_blk(land_ref.at[jj], b),
                                    send_sem=send_sems.at[(1 + ns) * jj], recv_sem=recv_sems.at[(1 + ns) * jj],
                                    **to).start()

    bufs = [g_grp, land] + [t for pair in singles for t in pair]
    return _split_start("scatter_start_" + phase, issue, bufs, 3 * (1 + ns))


def _scatter_wait(phase, send_sems, recv_sems, bufs, after, halved=False):
    ns = (len(bufs) - 2) // 2

    def await_(refs, send_sems, recv_sems):
        g_ref, land_ref = refs[0], refs[1]
        x, y, c, chips = _place()
        mine = 2 * x + y
        for jj, (px, py) in enumerate(chips):
            to = dict(device_id=(px, py, c), device_id_type=MESH)
            peer = 2 * px + py
            for a in range(ns):
                cp = pltpu.make_async_remote_copy(
                    src_ref=refs[2 + 2 * a].at[peer], dst_ref=refs[3 + 2 * a].at[jj],
                    send_sem=send_sems.at[(1 + ns) * jj + 1 + a], recv_sem=recv_sems.at[(1 + ns) * jj + 1 + a], **to)
                cp.wait_recv()
                cp.wait_send()
            for s in range(4):
                for par in ((0, 1) if halved else (None,)):
                    nblk = len(_phase_blocks(s, phase, False, par))
                    if nblk:
                        both = pltpu.make_async_remote_copy(
                            src_ref=_cols(g_ref, nblk), dst_ref=_cols(land_ref.at[jj], nblk),
                            send_sem=send_sems.at[(1 + ns) * jj], recv_sem=recv_sems.at[(1 + ns) * jj], **to)

                        @pl.when((mine == s) if par is None else ((mine == s) & (c == par)))
                        def _():
                            both.wait_recv()

                        @pl.when((peer == s) if par is None else ((peer == s) & (c == par)))
                        def _():
                            both.wait_send()

    return _split_wait("scatter_wait_" + phase, await_, send_sems, recv_sems, bufs, after)


def _needed_blocks(phase, parity):
    return sorted({blk for s in range(4) for _, blk in _phase_blocks(s, phase, False, parity)})


def _pair_reduce(phase, g_grp):
    n, d = g_grp.shape

    def swap(g_ref, sib_ref, send_sem, recv_sem):
        x, y, c, _ = _place()
        to = dict(device_id=(x, y, 1 - c), device_id_type=MESH)
        for par in range(2):
            give, get = _needed_blocks(phase, 1 - par), _needed_blocks(phase, par)

            @pl.when(c == par)
            def _():
                for blk in give:
                    pltpu.make_async_remote_copy(src_ref=_blk(g_ref, blk), dst_ref=_blk(sib_ref, blk),
                                                 send_sem=send_sem, recv_sem=recv_sem, **to).start()
                pltpu.make_async_remote_copy(src_ref=_cols(g_ref, len(get)), dst_ref=_cols(sib_ref, len(get)),
                                             send_sem=send_sem, recv_sem=recv_sem, **to).wait_recv()
                pltpu.make_async_remote_copy(src_ref=_cols(g_ref, len(give)), dst_ref=_cols(sib_ref, len(give)),
                                             send_sem=send_sem, recv_sem=recv_sem, **to).wait_send()

    sib = pl.pallas_call(
        swap, name="pair_swap_" + phase, in_specs=[ANY], out_specs=ANY,
        out_shape=jax.ShapeDtypeStruct((n, d), g_grp.dtype),
        scratch_shapes=[pltpu.SemaphoreType.DMA, pltpu.SemaphoreType.DMA],
    )(*_in_hbm(g_grp))

    lists = [_needed_blocks(phase, par) for par in range(2)]
    longest = max(len(t) for t in lists)
    table = jnp.asarray([t + [t[-1]] * (longest - len(t)) for t in lists], jnp.int32)[lax.axis_index("c")]

    def add(t_ref, a_ref, b_ref, o_ref):
        o_ref[...] = (a_ref[...].astype(F32) + b_ref[...].astype(F32)).astype(o_ref.dtype)

    blk = pl.BlockSpec((DH, d), lambda i, t: (t[i], 0))
    return pl.pallas_call(
        add, name="pair_add_" + phase,
        grid_spec=pltpu.PrefetchScalarGridSpec(num_scalar_prefetch=1, grid=(longest,),
                                               in_specs=[blk, blk], out_specs=blk),
        out_shape=jax.ShapeDtypeStruct((n, d), g_grp.dtype),
        compiler_params=_params(("arbitrary",)),
    )(table, g_grp, sib)


def _sum_shard(g_g, g_c, land):
    d = g_g.shape[1]
    chip = 2 * lax.axis_index("x") + lax.axis_index("y")

    def body(t_ref, gg_ref, gc_ref, land_ref, o_ref):
        b = pl.program_id(0)
        in_g = t_ref[2, b] == 1
        own = jnp.where(in_g, gg_ref[...].astype(F32), gc_ref[...].astype(F32))
        for jj in range(3):
            own = own + land_ref[jj].astype(F32)
        o_ref[...] = jnp.where(in_g & (b % 2 != lax.axis_index("c")), 0.0, own)

    return pl.pallas_call(
        body, name="sum_w_in",
        grid_spec=pltpu.PrefetchScalarGridSpec(
            num_scalar_prefetch=1, grid=(ALIGNED_BLOCKS,),
            in_specs=[pl.BlockSpec((DH, d), lambda b, t: (t[0, b], 0)), pl.BlockSpec((DH, d), lambda b, t: (t[1, b], 0)),
                      pl.BlockSpec((3, DH, d), lambda b, t: (0, b, 0))],
            out_specs=pl.BlockSpec((DH, d), lambda b, t: (b, 0))),
        out_shape=jax.ShapeDtypeStruct((ALIGNED_W, d), F32),
        compiler_params=_params(("arbitrary",)),
    )(_block_table(chip, False, 0, 0), g_g, g_c, land)


def _sum_rows(stack, land, rows):
    _, r, d = stack.shape
    rows = min(rows, r)
    chip = 2 * lax.axis_index("x") + lax.axis_index("y")

    def body(t_ref, own_ref, land_ref, o_ref):
        acc = own_ref[0].astype(F32)
        for jj in range(3):
            acc = acc + land_ref[jj].astype(F32)
        o_ref[...] = acc

    return pl.pallas_call(
        body, name="sum_w_out",
        grid_spec=pltpu.PrefetchScalarGridSpec(
            num_scalar_prefetch=1, grid=(r // rows,),
            in_specs=[pl.BlockSpec((1, rows, d), lambda i, t: (t[0], i, 0)),
                      pl.BlockSpec((3, rows, d), lambda i, t: (0, i, 0))],
            out_specs=pl.BlockSpec((rows, d), lambda i, t: (i, 0))),
        out_shape=jax.ShapeDtypeStruct((r, d), F32),
        compiler_params=_params(("arbitrary",)),
    )(jnp.reshape(chip, (1,)).astype(jnp.int32), stack, land)


def _final_exchange(parts, pack):
    npart = len(parts)

    def body(*refs):
        ins, pack_ref = refs[:npart], refs[npart]
        outs, packs = refs[npart + 1:2 * npart + 1], refs[2 * npart + 1]
        send_sems, recv_sems, psend, precv, loc_sem = refs[2 * npart + 2:]
        x, y, c, _ = _place()
        me = 4 * x + 2 * y + c
        local = pltpu.make_async_copy(pack_ref, packs.at[me], loc_sem)
        local.start()
        cps = [pltpu.make_async_remote_copy(
            src_ref=ins[a], dst_ref=outs[a], send_sem=send_sems.at[a], recv_sem=recv_sems.at[a],
            device_id=(x, y, 1 - c), device_id_type=MESH) for a in range(npart)]
        for r in range(1, 8):
            dx, dy, dc = (r >> 2) & 1, (r >> 1) & 1, r & 1
            peer = (x + dx - 2 * x * dx, y + dy - 2 * y * dy, c + dc - 2 * c * dc)
            cps.append(pltpu.make_async_remote_copy(
                src_ref=pack_ref, dst_ref=packs.at[me], send_sem=psend.at[r - 1], recv_sem=precv.at[r - 1],
                device_id=peer, device_id_type=MESH))
        for cp in cps:
            cp.start()
        for cp in cps:
            cp.wait_recv()
        for cp in cps:
            cp.wait_send()
        local.wait()

    return pl.pallas_call(
        body, name="final_exchange",
        in_specs=[ANY] * (npart + 1), out_specs=[ANY] * (npart + 1),
        out_shape=[jax.ShapeDtypeStruct(p.shape, p.dtype) for p in parts]
        + [jax.ShapeDtypeStruct((8,) + pack.shape, pack.dtype)],
        scratch_shapes=[pltpu.SemaphoreType.DMA((npart,)), pltpu.SemaphoreType.DMA((npart,)),
                        pltpu.SemaphoreType.DMA((7,)), pltpu.SemaphoreType.DMA((7,)), pltpu.SemaphoreType.DMA],
    )(*parts, pack)


def _sum_packs(packs):
    def body(p_ref, o_ref):
        acc = p_ref[0]
        for d in range(1, 8):
            acc = acc + p_ref[d]
        o_ref[...] = acc

    return pl.pallas_call(
        body, name="sum_packs", out_shape=jax.ShapeDtypeStruct(packs.shape[1:], F32),
    )(packs)


def _adamw_update(g, w_ref, m_ref, v_ref, go, do, mo, vo):
    c1 = 1.0 / (1.0 - ADAM_B1 ** ADAM_STEP)
    c2 = 1.0 / (1.0 - ADAM_B2 ** ADAM_STEP)
    mn = ADAM_B1 * m_ref[...] + (1.0 - ADAM_B1) * g
    vn = ADAM_B2 * v_ref[...] + (1.0 - ADAM_B2) * (g * g)
    go[...] = g
    mo[...] = mn
    vo[...] = vn
    do[...] = -ADAM_LR * ((mn * c1) / (jnp.sqrt(vn * c2) + ADAM_EPS) + ADAM_WD * w_ref[...])


def _adamw(w, m, v, g1, g2, rows, name):
    r, cdim = w.shape
    rows = min(rows, r)

    def body(*refs):
        n_in = 4 if g2 is None else 5
        w_ref, m_ref, v_ref, g_ref = refs[:4]
        g = g_ref[...] if g2 is None else g_ref[...] + refs[4][...]
        _adamw_update(g, w_ref, m_ref, v_ref, *refs[n_in:n_in + 4])

    blk = pl.BlockSpec((rows, cdim), lambda i: (i, 0))
    args = [w, m, v, g1] + ([] if g2 is None else [g2])
    shp = jax.ShapeDtypeStruct((r, cdim), F32)
    return pl.pallas_call(
        body, name=name, grid=(r // rows,),
        in_specs=[blk] * len(args), out_specs=[blk] * 4, out_shape=[shp] * 4,
        compiler_params=_params(("parallel",), 20 * rows * cdim * 4 + 8 * 2**20),
    )(*_in_hbm(*args))


def _adamw_shard(wt, mt, vt, g1, g2):
    r, d = wt.shape
    cols = min(128, d)

    def body(w_ref, m_ref, v_ref, g_ref, g2_ref, go, do, mo, vo, pad_ref):
        chip = 2 * lax.axis_index("x") + lax.axis_index("y")
        back = [(ALIGNED_W - s) % ALIGNED_W for s in SHIFTS]
        pad_ref[...] = pltpu.roll(g_ref[...] + g2_ref[...], _by_chip(chip, back), 0)
        _adamw_update(pad_ref[0:r, :], w_ref, m_ref, v_ref, go, do, mo, vo)

    blk = pl.BlockSpec((r, cols), lambda i: (0, i))
    gblk = pl.BlockSpec((ALIGNED_W, cols), lambda i: (0, i))
    shp = jax.ShapeDtypeStruct((r, d), F32)
    return pl.pallas_call(
        body, name="adamw_w_in", grid=(d // cols,),
        in_specs=[blk] * 3 + [gblk] * 2, out_specs=[blk] * 4, out_shape=[shp] * 4,
        scratch_shapes=[pltpu.VMEM((ALIGNED_W, cols), F32)],
        compiler_params=_params(("parallel",), 24 * ALIGNED_W * cols * 4 + 8 * 2**20),
    )(wt, mt, vt, g1, g2)


def _pad_lanes(a, width):
    return jnp.pad(a, ((0, 0), (0, width - a.shape[1])))


def _gathered_to_full(g):
    return jnp.transpose(g, (1, 0, 2)).reshape(g.shape[1], 4 * g.shape[2])


def _row(a):
    return _pad_lanes(a.reshape(1, -1), 1024)


def _small_pack(nin, cb, fn, al, dt, gn, cqw_shard, cw_shard):
    ad = jnp.concatenate([al.reshape(1, -1), dt.reshape(1, -1)], axis=1)
    rows = [_row(nin), _row(cb), _row(fn), _row(ad), _row(gn), cqw_shard.reshape(3, 1024), _row(cw_shard)]
    out = jnp.concatenate(rows, axis=0)
    return jnp.pad(out, ((0, 16 - out.shape[0]), (0, 0)))


def kernel(x, norm_in_w, w_in, conv_qkv_w, A_log, dt_bias, gdn_norm_w, conv_w, conv_b, w_out, final_norm_w, loss_target, m_norm_in_w, m_w_in, m_conv_qkv_w, m_A_log, m_dt_bias, m_gdn_norm_w, m_conv_w, m_conv_b, m_w_out, m_final_norm_w, v_norm_in_w, v_w_in, v_conv_qkv_w, v_A_log, v_dt_bias, v_gdn_norm_w, v_conv_w, v_conv_b, v_w_out, v_final_norm_w):
    chip = 2 * lax.axis_index("x") + lax.axis_index("y")
    a_shard = _align_shard(jnp.transpose(w_in[0]))
    wo_b = _cast_bf16(w_out[0], 256, "cast_w_out")
    d_model = x.shape[-1]
    stack = lambda s: lax.empty((4,) + s.shape, s.dtype)
    wg0 = lax.empty((WG_BLOCKS * DH, d_model), BF16)
    wc0 = lax.empty((WC_BLOCKS * DH, d_model), BF16)
    ss_g, rs_g, bufs_g, tok_g = _gather_start("g", a_shard, wg0, [(conv_qkv_w[0], stack(conv_qkv_w[0]))])
    ss_c, rs_c, bufs_c, tok_c = _gather_start("c", bufs_g[0], wc0,
                                              [(conv_w[0], stack(conv_w[0])), (wo_b, stack(wo_b))])
    wg1, wc1, wog1, cqg1, cwg1 = _place_own(bufs_c[0], bufs_c[4], bufs_g[2], bufs_c[2],
                                            [bufs_g[1], bufs_c[1], bufs_c[5], bufs_g[3], bufs_c[3]])
    x0 = x[0]
    h = _rms_in(x0, _tie(_tie(norm_in_w, tok_g, "after_gather_start_g"), tok_c, "after_gather_start_c"))
    a_thru, wg, _, cq_g = _gather_wait("g", ss_g, rs_g, [bufs_c[0], wg1, bufs_g[2], cqg1], h)
    w_g = _merge_edges(_sibling_forward("g", wg), G_EDGE, G_MIXED, "merge_edges_g")
    cqw = _gathered_to_full(cq_g)
    ad = jnp.pad(jnp.concatenate([A_log, dt_bias], axis=0), ((0, 0), (A_LANE, 0)))
    fwd_c = {}

    def on_q(q):
        _, wc, _, cw_g, _, wo_g = _gather_wait("c", ss_c, rs_c,
                                               [a_thru, wc1, bufs_c[2], cwg1, bufs_c[4], wog1], q)
        issue, _ = _sibling_forward_parts("c")
        ss, rs, (wc,), tok = _split_start("sibling_forward_start_c", issue, [wc], 3)
        fwd_c.update(ss=ss, rs=rs, wc=wc, cw_g=cw_g, wo_g=wo_g)
        return _tie(q, tok, "after_sibling_forward_start_c")

    def late(o):
        _, await_ = _sibling_forward_parts("c")
        (wc,) = _split_wait("sibling_forward_wait_c", await_, fwd_c["ss"], fwd_c["rs"], [fwd_c["wc"]], o)
        return (_merge_edges(wc, C_EDGE, C_MIXED, "merge_edges_c"), fwd_c["wo_g"].reshape(2 * GW, d_model),
                _gathered_to_full(fwd_c["cw_g"]))

    scat = {}

    def on_grad_c(g_c, g_wout, do):
        go4 = g_wout.reshape(4, GW // 2, d_model)
        land = lax.empty((3, ALIGNED_W, d_model), BF16)
        land_o = lax.empty((3, GW // 2, d_model), BF16)
        ss, rs, bufs, tok = _scatter_start("c", g_c, land, [(go4, land_o)])
        scat["c"] = (ss, rs, bufs)
        return _tie(do, tok, "after_scatter_start_c")

    def on_grad_g(g_g, dproj_g):
        ss, rs, bufs, tok = _scatter_start("g", _pair_reduce("g", g_g), scat["c"][2][1], [], halved=True)
        scat["g"] = (ss, rs, bufs)
        return _tie(dproj_g, tok, "after_scatter_start_g")

    gx, sm, _ = _local_step(x0, loss_target[0], h, w_g, cqw, late, norm_in_w, ad, gdn_norm_w, conv_b,
                            final_norm_w.reshape(1, -1), on_grad_c, on_grad_g, on_q)

    ss, rs, bufs = scat["g"]
    g_g, land = _scatter_wait("g", ss, rs, bufs, gx, halved=True)
    ss, rs, bufs = scat["c"]
    g_c, land, go4, land_o = _scatter_wait("c", ss, rs, [bufs[0], land, bufs[2], bufs[3]], gx)
    part_in = _sum_shard(g_g, g_c, land)
    part_out = _sum_rows(go4, land_o, 128)
    ad_g = jnp.concatenate([sm["al"][:, A_LANE:], sm["dt"][:, A_LANE:]], axis=1)
    pack = jnp.concatenate([_row(sm["nin"]), _row(sm["cb"]), _row(sm["fn"]), _row(ad_g), _row(sm["gn"]),
                            jnp.concatenate(sm["cq"], axis=1).reshape(12, 1024), sm["cw"], _row(sm["loss"])], axis=0)
    pack = jnp.pad(pack, ((0, PACK_ROWS - pack.shape[0]), (0, 0)))
    sib_in, sib_out, packs = _final_exchange([part_in, part_out], pack)
    tot = _sum_packs(packs)

    g_wi, d_wi, m_wi, v_wi = [jnp.transpose(a) for a in _adamw_shard(
        jnp.transpose(w_in[0]), jnp.transpose(m_w_in[0]), jnp.transpose(v_w_in[0]), part_in, sib_in)]
    g_wo, d_wo, m_wo, v_wo = _adamw(w_out[0], m_w_out[0], v_w_out[0], part_out, sib_out, 128, "adamw_w_out")
    g_cq_sh = lax.dynamic_slice_in_dim(tot[R_CQ:R_CQ + 12].reshape(4, 3 * GW), chip * 768, 768, axis=1)
    g_cw_sh = lax.dynamic_slice_in_dim(tot[R_CW:R_CW + 3], chip * 256, 256, axis=1)
    sp = lambda nin, cb, fn, al, dt, gn, cq, cwv: _small_pack(nin, cb, fn, al, dt, gn, cq[0], cwv[0])
    g_s = _small_pack(tot[R_NIN], tot[R_CB], tot[R_FN], tot[R_AD, :HEADS], tot[R_AD, HEADS:2 * HEADS],
                      tot[R_GN, :DH], g_cq_sh, g_cw_sh)
    w_s = sp(norm_in_w, conv_b, final_norm_w, A_log, dt_bias, gdn_norm_w, conv_qkv_w, conv_w)
    m_s = sp(m_norm_in_w, m_conv_b, m_final_norm_w, m_A_log, m_dt_bias, m_gdn_norm_w, m_conv_qkv_w, m_conv_w)
    v_s = sp(v_norm_in_w, v_conv_b, v_final_norm_w, v_A_log, v_dt_bias, v_gdn_norm_w, v_conv_qkv_w, v_conv_w)
    small = _adamw(w_s, m_s, v_s, g_s, None, 16, "adamw_small")

    def unpack(a, big_in, big_out):
        return (a[0:1], big_in[None], a[5:8].reshape(1, 4, 768), a[3:4, :HEADS], a[3:4, HEADS:2 * HEADS],
                a[4:5, :DH], a[8, :768].reshape(1, 3, 256), a[1:2], big_out[None], a[2])

    loss = tot[R_LOSS, 0]
    return (loss, gx[None], *unpack(small[0], g_wi, g_wo), *unpack(small[1], d_wi, d_wo),
            *unpack(small[2], m_wi, m_wo), *unpack(small[3], v_wi, v_wo))
```

```python
import functools
import math

import jax
import jax.numpy as jnp
from jax import lax
from jax.experimental import pallas as pl
from jax.experimental.pallas import tpu as pltpu

F32 = jnp.float32
BF16 = jnp.bfloat16
MESH = pl.DeviceIdType.MESH
ANY = pl.BlockSpec(memory_space=pl.ANY)

HEADS = 8
DH = 128
CH = 64
GW = HEADS * DH
EPS = 1e-6
VMEM_V7X = 64 * 1024 * 1024

QB, KB, VB, ZB, BAB = 0, 8, 16, 24, 32
A_LANE = 120
NG, NC = 33, 32
GW_COLS, CW_COLS = NG * DH, NC * DH

SHARD_W = 2052
ALIGNED_BLOCKS = 17
ALIGNED_W = ALIGNED_BLOCKS * DH
SHIFTS = (0, 4, ALIGNED_W - 8, ALIGNED_W - 4)
G_EDGE, C_EDGE = 34, 32
G_SPARE, C_SPARE = 33, 34
WG_BLOCKS, WC_BLOCKS = 38, 36
G_MIXED, C_MIXED = (2, BAB), (4 * 7 + 1,)


def _shard_blocks(chip, edges):
    g, c = "g", "c"
    if chip == 0:
        out = [(g, 3 * b) for b in range(8)] + [(g, 3 * b + 1) for b in range(8)] + [(g, G_EDGE, G_MIXED[0])]
    elif chip == 1:
        out = [(g, G_EDGE + 1, G_MIXED[0])] + [(g, 3 * b + 2) for b in range(1, 8)]
        out += [(g, ZB + b) for b in range(8)] + [(g, G_EDGE + 2, G_MIXED[1])]
    elif chip == 2:
        out = [(c, 4 * b) for b in range(8)] + [(c, 4 * b + 1) for b in range(7)]
        out += [(c, C_EDGE, C_MIXED[0]), (g, G_EDGE + 3, G_MIXED[1])]
    else:
        out = [(c, 4 * b + 2) for b in range(8)] + [(c, 4 * b + 3) for b in range(8)] + [(c, C_EDGE + 1, C_MIXED[0])]
    return [(o[0], o[1] if (edges or len(o) == 2) else o[2]) for o in out]


def _by_chip(chip, vals):
    if all(v == vals[0] for v in vals):
        return vals[0]
    r = vals[3]
    for kk in (2, 1, 0):
        r = jnp.where(chip == kk, vals[kk], r)
    return r

ADAM_LR, ADAM_B1, ADAM_B2, ADAM_EPS, ADAM_WD, ADAM_STEP = 0.001, 0.9, 0.999, 1e-08, 0.01, 10

R_NIN, R_CB, R_FN, R_AD, R_GN, R_CQ, R_CW, R_LOSS, PACK_ROWS = 0, 1, 2, 3, 4, 5, 17, 20, 24

NN = ((1,), (0,))
NT = ((1,), (1,))
TN = ((0,), (0,))


def _dot(a, b, dims=NN, mode="lo"):
    dn = (dims, ((), ()))
    if mode == "hi":
        return lax.dot_general(a, b, dn, precision=lax.Precision.HIGHEST, preferred_element_type=F32)
    ah, bh = a.astype(BF16), b.astype(BF16)
    out = lax.dot_general(ah, bh, dn, preferred_element_type=F32)
    if mode == "x3":
        al = (a - ah.astype(F32)).astype(BF16)
        bl = (b - bh.astype(F32)).astype(BF16)
        out = out + lax.dot_general(ah, bl, dn, preferred_element_type=F32)
        out = out + lax.dot_general(al, bh, dn, preferred_element_type=F32)
    return out


P_GRAM, P_INV, P_SOL, P_SCAN, P_SCANB, P_BWD = "lo", "lo", "lo", "lo", "lo", "lo"
P_CUM = "x3"


def _params(sem=None, vmem=None):
    kw = {}
    if sem is not None:
        kw["dimension_semantics"] = sem
    if vmem is not None:
        kw["vmem_limit_bytes"] = int(min(max(vmem, 32 * 2**20), VMEM_V7X - 8 * 2**20))
    return pltpu.CompilerParams(**kw)


def _in_hbm(*arrays):
    return [pltpu.with_memory_space_constraint(a, pltpu.HBM) for a in arrays]


def _sigmoid(x):
    return 1.0 / (1.0 + jnp.exp(-x))


def _dsilu(x, s):
    return s * (1.0 + x * (1.0 - s))


def _rows(shape):
    return lax.broadcasted_iota(jnp.int32, shape, 0)


def _shift_down(x, s):
    if s == 0:
        return x
    return jnp.where(_rows(x.shape) >= s, pltpu.roll(x, s, 0), 0.0)


def _shift_up(x, s):
    if s == 0:
        return x
    n = x.shape[0]
    return jnp.where(_rows(x.shape) < n - s, pltpu.roll(x, n - s, 0), 0.0)


def _matmul(a, b, dims, out_dtype, tm, tn, tk, name, add=None, n=None, b_outer=False):
    if dims == NN:
        (m, k), n = a.shape, b.shape[1]
    elif dims == NT:
        (m, k), n = a.shape, (n or b.shape[0])
    else:
        (k, m), n = a.shape, b.shape[1]
    tm, tn, tk = min(tm, m), min(tn, n), min(tk, k)
    assert m % tm == 0 and n % tn == 0 and k % tk == 0, (name, m, n, k, tm, tn, tk)
    nk = k // tk

    def body(*refs):
        if add is None:
            a_ref, b_ref, o_ref = refs[:3]
            add_ref = None
        else:
            a_ref, b_ref, add_ref, o_ref = refs[:4]
        part = _dot(a_ref[...], b_ref[...], dims)
        if nk == 1:
            if add_ref is not None:
                part = part + add_ref[...]
            o_ref[...] = part.astype(out_dtype)
            return
        acc = refs[-1]
        kk = pl.program_id(2)

        @pl.when(kk == 0)
        def _():
            acc[...] = part

        @pl.when(kk > 0)
        def _():
            acc[...] += part

        @pl.when(kk == nk - 1)
        def _():
            r = acc[...]
            if add_ref is not None:
                r = r + add_ref[...]
            o_ref[...] = r.astype(out_dtype)

    ij = (lambda g0, g1: (g1, g0)) if b_outer else (lambda g0, g1: (g0, g1))

    def spec(shape, pick):
        return pl.BlockSpec(shape, lambda g0, g1, kk: pick(*ij(g0, g1), kk))

    a_spec = spec((tk, tm), lambda i, j, kk: (kk, i)) if dims == TN else spec((tm, tk), lambda i, j, kk: (i, kk))
    b_spec = spec((tn, tk), lambda i, j, kk: (j, kk)) if dims == NT else spec((tk, tn), lambda i, j, kk: (kk, j))
    o_spec = spec((tm, tn), lambda i, j, kk: (i, j))
    in_specs = [a_spec, b_spec]
    args = [a, b]
    if add is not None:
        in_specs.append(o_spec)
        args.append(add)
    osz = jnp.dtype(out_dtype).itemsize
    est = 2 * (tm * tk * a.dtype.itemsize + tk * tn * b.dtype.itemsize + tm * tn * osz)
    est += 3 * tm * tn * 4 + (2 * tm * tn * 4 if add is not None else 0)
    return pl.pallas_call(
        body, name=name, grid=(n // tn, m // tm, nk) if b_outer else (m // tm, n // tn, nk),
        in_specs=in_specs, out_specs=o_spec,
        out_shape=jax.ShapeDtypeStruct((m, n), out_dtype),
        scratch_shapes=[pltpu.VMEM((tm, tn), F32)] if nk > 1 else [],
        compiler_params=_params(("parallel", "parallel", "arbitrary"), est + 8 * 2**20),
    )(*args)


def _cast_bf16(a, rows, name):
    r, c = a.shape
    rows = min(rows, r)

    def body(a_ref, o_ref):
        o_ref[...] = a_ref[...].astype(BF16)

    return pl.pallas_call(
        body, name=name, grid=(r // rows,),
        in_specs=[pl.BlockSpec((rows, c), lambda i: (i, 0))],
        out_specs=pl.BlockSpec((rows, c), lambda i: (i, 0)),
        out_shape=jax.ShapeDtypeStruct((r, c), BF16),
        compiler_params=_params(("parallel",)),
    )(a)


def _align_shard(wt):
    r, _, d = wt.shape
    cols = min(256, d)

    def body(w_ref, o_ref, pad_ref):
        chip = 2 * lax.axis_index("x") + lax.axis_index("y")
        pad_ref[...] = jnp.zeros_like(pad_ref)
        pad_ref[0:r, :] = w_ref[:, 0, :]
        o_ref[...] = pltpu.roll(pad_ref[...], _by_chip(chip, SHIFTS), 0).astype(BF16)

    return pl.pallas_call(
        body, name="align_shard", grid=(d // cols,),
        in_specs=[pl.BlockSpec((r, 1, cols), lambda i: (0, 0, i))],
        out_specs=pl.BlockSpec((ALIGNED_W, cols), lambda i: (0, i)),
        out_shape=jax.ShapeDtypeStruct((ALIGNED_W, d), BF16),
        scratch_shapes=[pltpu.VMEM((ALIGNED_W, cols), F32)],
        compiler_params=_params(("parallel",)),
    )(wt)


def _rms_in(x, w):
    n, d = x.shape
    tr = min(256, n)

    def body(x_ref, w_ref, h_ref):
        xv = x_ref[...]
        r = lax.rsqrt(jnp.mean(xv * xv, axis=-1, keepdims=True) + EPS)
        h_ref[...] = (xv * r * w_ref[...]).astype(BF16)

    return pl.pallas_call(
        body, name="rms_in", grid=(n // tr,),
        in_specs=[pl.BlockSpec((tr, d), lambda i: (i, 0)), pl.BlockSpec((1, d), lambda i: (0, 0))],
        out_specs=pl.BlockSpec((tr, d), lambda i: (i, 0)),
        out_shape=jax.ShapeDtypeStruct((n, d), BF16),
        compiler_params=_params(("parallel",)),
    )(x, w)


def _conv_silu(p, w_ref, taps):
    c = None
    for j in range(taps):
        t = _shift_down(p, taps - 1 - j) * w_ref[j:j + 1, :]
        c = t if c is None else c + t
    return c


def _prep_qkv(proj, cw):
    n = proj.shape[0]

    def body(p3, wq, wk, wv, q_ref, k_ref, v_ref):
        for kind, (w_ref, o_ref) in enumerate(((wq, q_ref), (wk, k_ref), (wv, v_ref))):
            c = _conv_silu(p3[:, kind * DH:(kind + 1) * DH], w_ref, 4)
            a = c * _sigmoid(c)
            if kind < 2:
                r = lax.rsqrt(jnp.sum(a * a, axis=-1, keepdims=True) + EPS)
                a = a * (r * (DH ** -0.5 if kind == 0 else 1.0))
            o_ref[...] = a

    col = pl.BlockSpec((n, DH), lambda h: (0, h))
    wcol = lambda base: pl.BlockSpec((4, DH), lambda h: (0, base + h))
    out = jax.ShapeDtypeStruct((n, GW), F32)
    return pl.pallas_call(
        body, name="prep_qkv", grid=(HEADS,),
        in_specs=[pl.BlockSpec((n, 3 * DH), lambda h: (0, h)), wcol(QB), wcol(KB), wcol(VB)],
        out_specs=[col] * 3, out_shape=[out] * 3,
        compiler_params=_params(("parallel",), 40 * 2**20),
    )(proj, cw, cw, cw)


def _prep_qkv_bwd(proj, cw, dq, dk, dv, dproj):
    n = proj.shape[0]

    def body(p3, wq, wk, wv, dq_ref, dk_ref, dv_ref, _, o3, gq, gk, gv):
        for kind, (w_ref, d_ref, g_ref) in enumerate(((wq, dq_ref, gq), (wk, dk_ref, gk), (wv, dv_ref, gv))):
            p = p3[:, kind * DH:(kind + 1) * DH]
            shifted = [_shift_down(p, 3 - j) for j in range(4)]
            c = shifted[0] * w_ref[0:1, :]
            for j in range(1, 4):
                c = c + shifted[j] * w_ref[j:j + 1, :]
            s = _sigmoid(c)
            a = c * s
            d = d_ref[...]
            if kind < 2:
                r = lax.rsqrt(jnp.sum(a * a, axis=-1, keepdims=True) + EPS)
                sc = DH ** -0.5 if kind == 0 else 1.0
                d = (sc * r) * (d - a * ((r * r) * jnp.sum(d * a, axis=-1, keepdims=True)))
            dc = d * _dsilu(c, s)
            dp = None
            for j in range(4):
                g_ref[j:j + 1, :] = jnp.sum(dc * shifted[j], axis=0, keepdims=True)
                t = _shift_up(dc, 3 - j) * w_ref[j:j + 1, :]
                dp = t if dp is None else dp + t
            o3[:, kind * DH:(kind + 1) * DH] = dp.astype(BF16)

    col = pl.BlockSpec((n, DH), lambda h: (0, h))
    wcol = lambda base: pl.BlockSpec((4, DH), lambda h: (0, base + h))
    p3spec = pl.BlockSpec((n, 3 * DH), lambda h: (0, h))
    return pl.pallas_call(
        body, name="prep_qkv_bwd", grid=(HEADS,),
        in_specs=[p3spec, wcol(QB), wcol(KB), wcol(VB), col, col, col, ANY],
        out_specs=[p3spec] + [wcol(0)] * 3,
        out_shape=[jax.ShapeDtypeStruct(dproj.shape, BF16)] + [jax.ShapeDtypeStruct((4, GW), F32)] * 3,
        input_output_aliases={7: 0},
        compiler_params=_params(("parallel",), 48 * 2**20),
    )(proj, cw, cw, cw, dq, dk, dv, dproj)


CPB = 8
SCAN_CPS = 4


def _tri(lower, rows):
    i = lax.broadcasted_iota(jnp.int32, (rows, rows), 0)
    j = lax.broadcasted_iota(jnp.int32, (rows, rows), 1)
    return jnp.where((i // CH == j // CH) & ((i >= j) if lower else (j >= i)), 1.0, 0.0)


def _lane(shape):
    return lax.broadcasted_iota(jnp.int32, shape, 1)


def _prep_bg(proj, ad):
    n = proj.shape[0]
    nch = n // CH
    cpb = CPB if nch % CPB == 0 else 1
    rows = cpb * CH

    def body(p_ref, ad_ref, bg_ref, bgt_ref):
        p = p_ref[...]
        lane = _lane(p.shape)
        beta = _sigmoid(p)
        xa = p + ad_ref[1:2, :]
        sp = jnp.maximum(xa, 0.0) + jnp.log(1.0 + jnp.exp(-jnp.abs(xa)))
        g = pltpu.roll(-jnp.exp(ad_ref[0:1, :]) * sp, DH - A_LANE + HEADS, 1)
        gc = _dot(_tri(True, rows), g, NN, P_CUM)
        bg = jnp.where(lane < HEADS, beta, jnp.where(lane < 2 * HEADS, gc, 0.0))
        bg_ref[...] = bg
        for ci in range(cpb):
            bgt_ref[ci] = bg[ci * CH:(ci + 1) * CH, :].T

    return pl.pallas_call(
        body, name="prep_bg", grid=(nch // cpb,),
        in_specs=[pl.BlockSpec((rows, DH), lambda i: (i, BAB)), pl.BlockSpec((2, DH), lambda i: (0, 0))],
        out_specs=[pl.BlockSpec((rows, DH), lambda i: (i, 0)), pl.BlockSpec((cpb, DH, CH), lambda i: (i, 0, 0))],
        out_shape=[jax.ShapeDtypeStruct((n, DH), F32), jax.ShapeDtypeStruct((nch, DH, CH), F32)],
        compiler_params=_params(("parallel",)),
    )(*_in_hbm(proj, ad))


def _prep_bg_bwd(proj, ad, dbg, dproj):
    n = proj.shape[0]
    nch = n // CH
    cpb = CPB if nch % CPB == 0 else 1
    rows = cpb * CH

    def body(p_ref, ad_ref, d_ref, _, o_ref, ga_ref, gd_ref):
        p = p_ref[...]
        d = d_ref[...]
        lane = _lane(p.shape)
        beta = _sigmoid(p)
        xa = p + ad_ref[1:2, :]
        sp = jnp.maximum(xa, 0.0) + jnp.log(1.0 + jnp.exp(-jnp.abs(xa)))
        na = -jnp.exp(ad_ref[0:1, :])
        dg = pltpu.roll(_dot(_tri(False, rows), d, NN, P_CUM), A_LANE - HEADS, 1)
        da = dg * na * _sigmoid(xa)
        is_g = lane >= A_LANE
        o_ref[...] = jnp.where(lane < HEADS, d * beta * (1.0 - beta), jnp.where(is_g, da, 0.0)).astype(BF16)
        ga = jnp.sum(jnp.where(is_g, dg * na * sp, 0.0), axis=0, keepdims=True)
        gd = jnp.sum(jnp.where(is_g, da, 0.0), axis=0, keepdims=True)

        @pl.when(pl.program_id(0) == 0)
        def _():
            ga_ref[...] = jnp.zeros_like(ga_ref)
            gd_ref[...] = jnp.zeros_like(gd_ref)

        ga_ref[...] += ga
        gd_ref[...] += gd

    one = pl.BlockSpec((1, DH), lambda i: (0, 0))
    return pl.pallas_call(
        body, name="prep_bg_bwd", grid=(nch // cpb,),
        in_specs=[pl.BlockSpec((rows, DH), lambda i: (i, BAB)), pl.BlockSpec((2, DH), lambda i: (0, 0)),
                  pl.BlockSpec((rows, DH), lambda i: (i, 0)), ANY],
        out_specs=[pl.BlockSpec((rows, DH), lambda i: (i, BAB)), one, one],
        out_shape=[jax.ShapeDtypeStruct(dproj.shape, BF16), jax.ShapeDtypeStruct((1, DH), F32),
                   jax.ShapeDtypeStruct((1, DH), F32)],
        input_output_aliases={3: 0},
        compiler_params=_params(("arbitrary",)),
    )(proj, ad, dbg, dproj)


def _gdn_out(o, proj, wg):
    n = o.shape[0]

    def body(o_ref, z_ref, w_ref, y_ref):
        ov, z = o_ref[...], z_ref[...]
        r = lax.rsqrt(jnp.mean(ov * ov, axis=-1, keepdims=True) + EPS)
        y_ref[...] = (ov * r * w_ref[...] * (z * _sigmoid(z))).astype(BF16)

    return pl.pallas_call(
        body, name="gdn_out", grid=(HEADS,),
        in_specs=[pl.BlockSpec((n, DH), lambda h: (0, h)), pl.BlockSpec((n, DH), lambda h: (0, ZB + h)),
                  pl.BlockSpec((1, DH), lambda h: (0, 0))],
        out_specs=pl.BlockSpec((n, DH), lambda h: (0, h)),
        out_shape=jax.ShapeDtypeStruct((n, 2 * GW), BF16),
        compiler_params=_params(("parallel",)),
    )(o, proj, wg)


def _gdn_out_bwd(o, proj, wg, dmix):
    n = o.shape[0]

    def body(o_ref, z_ref, w_ref, d_ref, do_ref, dz_ref, gw_ref):
        ov, z, d, w = o_ref[...], z_ref[...], d_ref[...], w_ref[...]
        r = lax.rsqrt(jnp.mean(ov * ov, axis=-1, keepdims=True) + EPS)
        nrm = ov * r
        s = _sigmoid(z)
        dz_ref[...] = (d * (nrm * w) * _dsilu(z, s)).astype(BF16)
        dn_w = d * (z * s)
        gw = jnp.sum(dn_w * nrm, axis=0, keepdims=True)
        dn = dn_w * w
        do_ref[...] = r * (dn - nrm * jnp.mean(dn * nrm, axis=-1, keepdims=True))

        @pl.when(pl.program_id(0) == 0)
        def _():
            gw_ref[...] = jnp.zeros_like(gw_ref)

        gw_ref[...] += gw

    return pl.pallas_call(
        body, name="gdn_out_bwd", grid=(HEADS,),
        in_specs=[pl.BlockSpec((n, DH), lambda h: (0, h)), pl.BlockSpec((n, DH), lambda h: (0, ZB + h)),
                  pl.BlockSpec((1, DH), lambda h: (0, 0)), pl.BlockSpec((n, DH), lambda h: (0, h))],
        out_specs=[pl.BlockSpec((n, DH), lambda h: (0, h)), pl.BlockSpec((n, DH), lambda h: (0, ZB + h)),
                   pl.BlockSpec((1, DH), lambda h: (0, 0))],
        out_shape=[jax.ShapeDtypeStruct((n, GW), F32), jax.ShapeDtypeStruct((n, GW_COLS), BF16),
                   jax.ShapeDtypeStruct((1, DH), F32)],
        compiler_params=_params(("arbitrary",)),
    )(o, proj, wg, dmix)


def _conv_branch(proj, w3, b, mix):
    n = proj.shape[0]

    def body(p4, w_ref, b_ref, _, y_ref):
        u = p4[:, DH:2 * DH] * p4[:, 2 * DH:3 * DH]
        cc = _conv_silu(u, w_ref, 3) + b_ref[...]
        z = p4[:, 3 * DH:4 * DH]
        y_ref[...] = (p4[:, 0:DH] * cc * (z * _sigmoid(z))).astype(BF16)

    return pl.pallas_call(
        body, name="conv_branch", grid=(HEADS,),
        in_specs=[pl.BlockSpec((n, 4 * DH), lambda h: (0, h)), pl.BlockSpec((3, DH), lambda h: (0, h)),
                  pl.BlockSpec((1, DH), lambda h: (0, h)), ANY],
        out_specs=pl.BlockSpec((n, DH), lambda h: (0, HEADS + h)),
        out_shape=jax.ShapeDtypeStruct(mix.shape, BF16),
        input_output_aliases={3: 0},
        compiler_params=_params(("parallel",), 40 * 2**20),
    )(*_in_hbm(proj, w3, b, mix))


def _conv_branch_bwd(proj, w3, b, dmix):
    n = proj.shape[0]

    def body(p4, w_ref, b_ref, d_ref, o4, gw_ref, gbias_ref):
        gb, gcv, hc, z = p4[:, 0:DH], p4[:, DH:2 * DH], p4[:, 2 * DH:3 * DH], p4[:, 3 * DH:4 * DH]
        d = d_ref[...]
        dgb, dgc, dhc, dzc = (o4.at[:, kk * DH:(kk + 1) * DH] for kk in range(4))
        u = gcv * hc
        cc = _conv_silu(u, w_ref, 3) + b_ref[...]
        s = _sigmoid(z)
        dzc[...] = (d * (gb * cc) * _dsilu(z, s)).astype(BF16)
        dp = d * (z * s)
        dgb[...] = (dp * cc).astype(BF16)
        dcc = dp * gb
        gbias_ref[...] = jnp.sum(dcc, axis=0, keepdims=True)
        du = None
        for j in range(3):
            gw_ref[j:j + 1, :] = jnp.sum(dcc * _shift_down(u, 2 - j), axis=0, keepdims=True)
            t = _shift_up(dcc, 2 - j) * w_ref[j:j + 1, :]
            du = t if du is None else du + t
        dgc[...] = (du * hc).astype(BF16)
        dhc[...] = (du * gcv).astype(BF16)

    p4spec = pl.BlockSpec((n, 4 * DH), lambda h: (0, h))
    return pl.pallas_call(
        body, name="conv_branch_bwd", grid=(HEADS,),
        in_specs=[p4spec, pl.BlockSpec((3, DH), lambda h: (0, h)), pl.BlockSpec((1, DH), lambda h: (0, h)),
                  pl.BlockSpec((n, DH), lambda h: (0, HEADS + h))],
        out_specs=[p4spec, pl.BlockSpec((3, DH), lambda h: (0, h)), pl.BlockSpec((1, DH), lambda h: (0, h))],
        out_shape=[jax.ShapeDtypeStruct((n, CW_COLS), BF16), jax.ShapeDtypeStruct((3, GW), F32),
                   jax.ShapeDtypeStruct((1, GW), F32)],
        compiler_params=_params(("parallel",), 48 * 2**20),
    )(proj, w3, b, dmix)


def _final_loss(out, tgt, wf):
    n, d = out.shape
    tr = min(256, n)

    def body(o_ref, t_ref, w_ref, do_ref, dob_ref, gw_ref, loss_ref):
        ov, w = o_ref[...], w_ref[...]
        r = lax.rsqrt(jnp.mean(ov * ov, axis=-1, keepdims=True) + EPS)
        nrm = ov * r
        e = nrm * w - t_ref[...]
        dy = e * (1.0 / d)
        dn = dy * w
        dout = r * (dn - nrm * jnp.mean(dn * nrm, axis=-1, keepdims=True))
        do_ref[...] = dout
        dob_ref[...] = dout.astype(BF16)

        @pl.when(pl.program_id(0) == 0)
        def _():
            gw_ref[...] = jnp.zeros_like(gw_ref)
            loss_ref[...] = jnp.zeros_like(loss_ref)

        gw_ref[...] += jnp.sum(dy * nrm, axis=0, keepdims=True)
        loss_ref[...] += (0.5 / d) * jnp.sum(jnp.sum(e * e, axis=-1, keepdims=True), axis=0, keepdims=True)

    row = pl.BlockSpec((tr, d), lambda i: (i, 0))
    return pl.pallas_call(
        body, name="final_loss", grid=(n // tr,),
        in_specs=[row, row, pl.BlockSpec((1, d), lambda i: (0, 0))],
        out_specs=[row, row, pl.BlockSpec((1, d), lambda i: (0, 0)), pl.BlockSpec((1, 1), lambda i: (0, 0))],
        out_shape=[jax.ShapeDtypeStruct((n, d), F32), jax.ShapeDtypeStruct((n, d), BF16),
                   jax.ShapeDtypeStruct((1, d), F32), jax.ShapeDtypeStruct((1, 1), F32)],
        compiler_params=_params(("arbitrary",)),
    )(*_in_hbm(out, tgt, wf))


def _rms_in_bwd(x, w, dh, dout):
    n, d = x.shape
    tr = min(256, n)

    def body(x_ref, w_ref, dh_ref, do_ref, dx_ref, gw_ref):
        xv, dhv = x_ref[...], dh_ref[...]
        r = lax.rsqrt(jnp.mean(xv * xv, axis=-1, keepdims=True) + EPS)
        xn = xv * r
        dxn = dhv * w_ref[...]
        dx_ref[...] = r * (dxn - xn * jnp.mean(dxn * xn, axis=-1, keepdims=True)) + do_ref[...]

        @pl.when(pl.program_id(0) == 0)
        def _():
            gw_ref[...] = jnp.zeros_like(gw_ref)

        gw_ref[...] += jnp.sum(dhv * xn, axis=0, keepdims=True)

    row = pl.BlockSpec((tr, d), lambda i: (i, 0))
    one = pl.BlockSpec((1, d), lambda i: (0, 0))
    return pl.pallas_call(
        body, name="rms_in_bwd", grid=(n // tr,),
        in_specs=[row, one, row, row], out_specs=[row, one],
        out_shape=[jax.ShapeDtypeStruct((n, d), F32), jax.ShapeDtypeStruct((1, d), F32)],
        compiler_params=_params(("arbitrary",)),
    )(*_in_hbm(x, w, dh, dout))


def _ij():
    i = lax.broadcasted_iota(jnp.int32, (CH, CH), 0)
    j = lax.broadcasted_iota(jnp.int32, (CH, CH), 1)
    return i, j


def _unit_lower_inverse(mats):
    i, j = _ij()
    eye = jnp.where(i == j, 1.0, 0.0)
    same16 = (i // 16) == (j // 16)
    same32 = (i // 32) == (j // 32)
    mm = lambda xs, ys: [_dot(x, y, NN, P_INV) for x, y in zip(xs, ys)]
    n1 = [jnp.where(same16, -a, 0.0) for a in mats]
    n2 = mm(n1, n1)
    n4 = mm(n2, n2)
    n8 = mm(n4, n4)
    t = [eye + x1 + x2 + x3 for x1, x2, x3 in zip(n1, n2, mm(n1, n2))]
    t = [x + y for x, y in zip(t, mm(t, n4))]
    t = [x + y for x, y in zip(t, mm(t, n8))]
    a1 = [jnp.where(same32 & jnp.logical_not(same16), a, 0.0) for a in mats]
    t = [x - y for x, y in zip(t, mm(t, mm(a1, t)))]
    a2 = [jnp.where(same32, 0.0, a) for a in mats]
    t = [x - y for x, y in zip(t, mm(t, mm(a2, t)))]
    return t


def _head_vectors(bg, bgt, h):
    bcol = bg[:, h:h + 1]
    gcol = bg[:, HEADS + h:HEADS + h + 1]
    grow = bgt[HEADS + h:HEADS + h + 1, :]
    return bcol, gcol, grow


def _decay(gcol, grow):
    i, j = _ij()
    return jnp.where(i >= j, jnp.exp(jnp.where(i >= j, gcol - grow, 0.0)), 0.0)


def _gdn_intra(q, k, v, bg, bgt):
    n = q.shape[0]
    nch = n // CH
    cps = 4 if nch % 4 == 0 else 1

    def body(q_ref, k_ref, v_ref, bg_ref, bgt_ref, u_ref, w_ref, p_ref, t_ref):
        i, j = _ij()
        items = [(ci, h) for ci in range(cps) for h in range(HEADS)]
        at = lambda ref, ci, h: ref.at[ci * CH:(ci + 1) * CH, h * DH:(h + 1) * DH]
        bgs = [bg_ref[ci * CH:(ci + 1) * CH, :] for ci in range(cps)]
        ks = [at(k_ref, ci, h)[...] for ci, h in items]
        vecs = [_head_vectors(bgs[ci], bgt_ref[ci], h) for ci, h in items]
        decs = [_decay(gcol, grow) for _, gcol, grow in vecs]
        kks = [_dot(kh, kh, NT, P_GRAM) for kh in ks]
        qks = [_dot(at(q_ref, ci, h)[...], kh, NT, P_GRAM) for (ci, h), kh in zip(items, ks)]
        ts = _unit_lower_inverse([jnp.where(i > j, bcol * kk * dec, 0.0)
                                  for (bcol, _, _), kk, dec in zip(vecs, kks, decs)])
        us = [_dot(t, at(v_ref, ci, h)[...] * bcol, NN, P_SOL) for t, (ci, h), (bcol, _, _) in zip(ts, items, vecs)]
        ws = [_dot(t, kh * (bcol * jnp.exp(gcol)), NN, P_SOL) for t, kh, (bcol, gcol, _) in zip(ts, ks, vecs)]
        for n_, (ci, h) in enumerate(items):
            p_ref[ci, h] = qks[n_] * decs[n_]
            t_ref[ci, h] = ts[n_]
            at(u_ref, ci, h)[...] = us[n_]
            at(w_ref, ci, h)[...] = ws[n_]

    row = pl.BlockSpec((cps * CH, GW), lambda c: (c, 0))
    sq = pl.BlockSpec((cps, HEADS, CH, CH), lambda c: (c, 0, 0, 0))
    big = jax.ShapeDtypeStruct((n, GW), F32)
    sqs = jax.ShapeDtypeStruct((nch, HEADS, CH, CH), F32)
    return pl.pallas_call(
        body, name="gdn_intra", grid=(nch // cps,),
        in_specs=[row, row, row, pl.BlockSpec((cps * CH, DH), lambda c: (c, 0)),
                  pl.BlockSpec((cps, DH, CH), lambda c: (c, 0, 0))],
        out_specs=[row, row, sq, sq], out_shape=[big, big, sqs, sqs],
        compiler_params=_params(("parallel",)),
    )(q, k, v, bg, bgt)


def _gdn_scan(q, k, bg, u, w, p):
    n = q.shape[0]
    nch = n // CH
    cps = SCAN_CPS if nch % SCAN_CPS == 0 else 1

    def body(q_ref, k_ref, bg_ref, u_ref, w_ref, p_ref, o_ref, vn_ref, s_out, s_scr):
        @pl.when(pl.program_id(0) == 0)
        def _():
            s_scr[...] = jnp.zeros_like(s_scr)

        hs = range(HEADS)
        sls = [slice(h * DH, (h + 1) * DH) for h in hs]
        ss = [s_scr[h] for h in hs]
        for ci in range(cps):
            rs = slice(ci * CH, (ci + 1) * CH)
            bg = bg_ref[rs, :]
            gcols = [bg[:, HEADS + h:HEADS + h + 1] for h in hs]
            glasts = [g[CH - 1:CH, :] for g in gcols]
            wss = [_dot(w_ref[rs, sl], s, NN, P_SCAN) for sl, s in zip(sls, ss)]
            oqs = [_dot(q_ref[rs, sl] * jnp.exp(g), s, NN, P_SCAN) for sl, s, g in zip(sls, ss, gcols)]
            vns = [u_ref[rs, sl] - x for sl, x in zip(sls, wss)]
            ops = [_dot(p_ref[ci, h], vn, NN, P_SCAN) for h, vn in zip(hs, vns)]
            sns = [_dot(k_ref[rs, sl] * jnp.exp(gl - g), vn, TN, P_SCAN)
                   for sl, gl, g, vn in zip(sls, glasts, gcols, vns)]
            for h, sl in enumerate(sls):
                s_out[ci, :, sl] = ss[h]
                vn_ref[rs, sl] = vns[h]
                o_ref[rs, sl] = oqs[h] + ops[h]
            ss = [s * jnp.exp(gl) + sn for s, gl, sn in zip(ss, glasts, sns)]
        for h in hs:
            s_scr[h] = ss[h]

    row = pl.BlockSpec((cps * CH, GW), lambda c: (c, 0))
    big = jax.ShapeDtypeStruct((n, GW), F32)
    return pl.pallas_call(
        body, name="gdn_scan", grid=(nch // cps,),
        in_specs=[row, row, pl.BlockSpec((cps * CH, DH), lambda c: (c, 0)), row, row,
                  pl.BlockSpec((cps, HEADS, CH, CH), lambda c: (c, 0, 0, 0))],
        out_specs=[row, row, pl.BlockSpec((cps, DH, GW), lambda c: (c, 0, 0))],
        out_shape=[big, big, jax.ShapeDtypeStruct((nch, DH, GW), F32)],
        scratch_shapes=[pltpu.VMEM((HEADS, DH, DH), F32)],
        compiler_params=_params(("arbitrary",)),
    )(q, k, bg, u, w, p)


def _gdn_scan_bwd(q, k, bg, w, p, vn, s_in, do):
    n = q.shape[0]
    nch = n // CH
    cps = SCAN_CPS if nch % SCAN_CPS == 0 else 1
    rev = lambda c: nch // cps - 1 - c

    def body(q_ref, k_ref, bg_ref, w_ref, p_ref, vn_ref, s_ref, do_ref,
             dqg_ref, dp_ref, du_ref, dw_ref, dks_ref, dgam_ref, ds_scr):
        @pl.when(pl.program_id(0) == 0)
        def _():
            ds_scr[...] = jnp.zeros_like(ds_scr)

        lane = _lane((1, DH))
        hs = range(HEADS)
        sls = [slice(h * DH, (h + 1) * DH) for h in hs]
        dss = [ds_scr[h] for h in hs]
        for ci in reversed(range(cps)):
            rs = slice(ci * CH, (ci + 1) * CH)
            bg = bg_ref[rs, :]
            gcols = [bg[:, HEADS + h:HEADS + h + 1] for h in hs]
            glasts = [g[CH - 1:CH, :] for g in gcols]
            ss = [s_ref[ci, :, sl] for sl in sls]
            dos = [do_ref[rs, sl] for sl in sls]
            vnl = [vn_ref[rs, sl] for sl in sls]
            dqgs = [_dot(d, s, NT, P_SCANB) for d, s in zip(dos, ss)]
            dps = [_dot(d, vn, NT, P_SCANB) for d, vn in zip(dos, vnl)]
            dvn1 = [_dot(p_ref[ci, h], d, TN, P_SCANB) for h, d in zip(hs, dos)]
            dvn2 = [_dot(k_ref[rs, sl] * jnp.exp(gl - g), ds, NN, P_SCANB)
                    for sl, gl, g, ds in zip(sls, glasts, gcols, dss)]
            dkss = [_dot(vn, ds, NT, P_SCANB) for vn, ds in zip(vnl, dss)]
            dsq = [_dot(q_ref[rs, sl] * jnp.exp(g), d, TN, P_SCANB) for sl, g, d in zip(sls, gcols, dos)]
            dvns = [a + b for a, b in zip(dvn1, dvn2)]
            dws = [_dot(dvn, s, NT, P_SCANB) for dvn, s in zip(dvns, ss)]
            dsw = [_dot(w_ref[rs, sl], dvn, TN, P_SCANB) for sl, dvn in zip(sls, dvns)]
            dgam = jnp.zeros((1, DH), F32)
            for h, sl in enumerate(sls):
                dqg_ref[rs, sl] = dqgs[h]
                dp_ref[ci, h] = dps[h]
                du_ref[rs, sl] = dvns[h]
                dw_ref[rs, sl] = -dws[h]
                dks_ref[rs, sl] = dkss[h]
                tot = jnp.sum(jnp.sum(dss[h] * ss[h], axis=-1, keepdims=True), axis=0, keepdims=True)
                dgam = dgam + jnp.where(lane == h, tot, 0.0)
            dgam_ref[ci] = jnp.broadcast_to(dgam, (8, DH))
            dss = [ds * jnp.exp(gl) + a - b for ds, gl, a, b in zip(dss, glasts, dsq, dsw)]
        for h in hs:
            ds_scr[h] = dss[h]

    row = pl.BlockSpec((cps * CH, GW), lambda c: (rev(c), 0))
    sq = pl.BlockSpec((cps, HEADS, CH, CH), lambda c: (rev(c), 0, 0, 0))
    big = jax.ShapeDtypeStruct((n, GW), F32)
    return pl.pallas_call(
        body, name="gdn_scan_bwd", grid=(nch // cps,),
        in_specs=[row, row, pl.BlockSpec((cps * CH, DH), lambda c: (rev(c), 0)), row, sq, row,
                  pl.BlockSpec((cps, DH, GW), lambda c: (rev(c), 0, 0)), row],
        out_specs=[row, sq, row, row, row, pl.BlockSpec((cps, 8, DH), lambda c: (rev(c), 0, 0))],
        out_shape=[big, jax.ShapeDtypeStruct((nch, HEADS, CH, CH), F32), big, big, big,
                   jax.ShapeDtypeStruct((nch, 8, DH), F32)],
        scratch_shapes=[pltpu.VMEM((HEADS, DH, DH), F32)],
        compiler_params=_params(("arbitrary",)),
    )(q, k, bg, w, p, vn, s_in, do)


def _gdn_intra_bwd(q, k, v, bg, bgt, t, u, w, p, dqg, dp, du, dw, dks, dgam):
    n = q.shape[0]
    nch = n // CH
    cps = 1

    def body(q_ref, k_ref, v_ref, bg_ref, bgt_ref, t_ref, u_ref, w_ref, p_ref,
             dqg_ref, dp_ref, du_ref, dw_ref, dks_ref, dgam_ref, dq_ref, dk_ref, dv_ref, dbg_ref):
        i, j = _ij()
        rows1 = lax.broadcasted_iota(jnp.int32, (CH, 1), 0)
        lane = _lane((CH, DH))
        rsum = lambda x: jnp.sum(x, axis=-1, keepdims=True)
        items = [(ci, h) for ci in range(cps) for h in range(HEADS)]
        at = lambda ref, it: ref.at[it[0] * CH:(it[0] + 1) * CH, it[1] * DH:(it[1] + 1) * DH]
        ld = lambda ref: [at(ref, it)[...] for it in items]
        bgs = [bg_ref[ci * CH:(ci + 1) * CH, :] for ci in range(cps)]
        qs, ks = ld(q_ref), ld(k_ref)
        vecs = [_head_vectors(bgs[ci], bgt_ref[ci], h) for ci, h in items]
        decs = [_decay(gcol, grow) for _, gcol, grow in vecs]
        ths = [t_ref[ci, h] for ci, h in items]
        drus = [_dot(th, x_, TN, P_BWD) for th, x_ in zip(ths, ld(du_ref))]
        drws = [_dot(th, x_, TN, P_BWD) for th, x_ in zip(ths, ld(dw_ref))]
        kks = [_dot(kh, kh, NT, P_GRAM) for kh in ks]
        da1 = [_dot(dru, x_, NT, P_BWD) for dru, x_ in zip(drus, ld(u_ref))]
        da2 = [_dot(drw, x_, NT, P_BWD) for drw, x_ in zip(drws, ld(w_ref))]
        das = [jnp.where(i > j, -(x_ + y_), 0.0) for x_, y_ in zip(da1, da2)]
        dkks = [da * bcol * dec for da, (bcol, _, _), dec in zip(das, vecs, decs)]
        dps = [dp_ref[ci, h] for ci, h in items]
        dqks = [dp_ * dec for dp_, dec in zip(dps, decs)]
        dq_ps = [_dot(dqk, kh, NN, P_BWD) for dqk, kh in zip(dqks, ks)]
        dk_ps = [_dot(dqk, qh, TN, P_BWD) for dqk, qh in zip(dqks, qs)]
        dk_as = [_dot(dkk, kh, NN, P_BWD) for dkk, kh in zip(dkks, ks)]
        dk_bs = [_dot(dkk, kh, TN, P_BWD) for dkk, kh in zip(dkks, ks)]
        bcols = [vc[0] for vc in vecs]
        gcols = [vc[1] for vc in vecs]
        gams = [jnp.exp(g) for g in gcols]
        glasts = [g[CH - 1:CH, :] for g in gcols]
        es = [jnp.exp(gl - g) for gl, g in zip(glasts, gcols)]
        kgs = [kh * gam for kh, gam in zip(ks, gams)]
        dqgs, dkss = ld(dqg_ref), ld(dks_ref)
        r_uv = [rsum(dru * x_) for dru, x_ in zip(drus, ld(v_ref))]
        r_wk = [rsum(drw * kg) for drw, kg in zip(drws, kgs)]
        r_ak = [rsum(da * kk * dec) for da, kk, dec in zip(das, kks, decs)]
        r_qq = [rsum(dqg * qh) for dqg, qh in zip(dqgs, qs)]
        tks = [rsum(dk_ * kh) * e for dk_, kh, e in zip(dkss, ks, es)]
        mdecs = [da * (bcol * kk * dec) + dp_ * p_ref[ci, h]
                 for (ci, h), da, bcol, kk, dec, dp_ in zip(items, das, bcols, kks, decs, dps)]
        r_md = [rsum(m) for m in mdecs]
        c_md = [rsum(jnp.where(i == j, jnp.sum(m, axis=0, keepdims=True), 0.0)) for m in mdecs]
        dbgs = [jnp.zeros((CH, DH), F32) for _ in range(cps)]
        for n_, (ci, h) in enumerate(items):
            at(dv_ref, (ci, h))[...] = bcols[n_] * drus[n_]
            at(dq_ref, (ci, h))[...] = gams[n_] * dqgs[n_] + dq_ps[n_]
            at(dk_ref, (ci, h))[...] = ((bcols[n_] * gams[n_]) * drws[n_] + dk_ps[n_] + dk_as[n_] + dk_bs[n_]
                                        + dkss[n_] * es[n_])
            dbeta = r_uv[n_] + r_wk[n_] + r_ak[n_]
            dglast = (jnp.sum(tks[n_], axis=0, keepdims=True)
                      + dgam_ref[ci, 0:1, h:h + 1] * jnp.exp(glasts[n_]))
            dgc = (r_wk[n_] * bcols[n_] + r_md[n_] - c_md[n_] + r_qq[n_] * gams[n_] - tks[n_]
                   + jnp.where(rows1 == CH - 1, dglast, 0.0))
            dbgs[ci] = dbgs[ci] + jnp.where(lane == h, dbeta, 0.0) + jnp.where(lane == HEADS + h, dgc, 0.0)
        for ci in range(cps):
            dbg_ref[ci * CH:(ci + 1) * CH, :] = dbgs[ci]

    row = pl.BlockSpec((cps * CH, GW), lambda c: (c, 0))
    sq = pl.BlockSpec((cps, HEADS, CH, CH), lambda c: (c, 0, 0, 0))
    small = pl.BlockSpec((cps * CH, DH), lambda c: (c, 0))
    big = jax.ShapeDtypeStruct((n, GW), F32)
    return pl.pallas_call(
        body, name="gdn_intra_bwd", grid=(nch // cps,),
        in_specs=[row, row, row, small, pl.BlockSpec((cps, DH, CH), lambda c: (c, 0, 0)), sq, row, row, sq,
                  row, sq, row, row, row, pl.BlockSpec((cps, 8, DH), lambda c: (c, 0, 0))],
        out_specs=[row, row, row, small],
        out_shape=[big, big, big, jax.ShapeDtypeStruct((n, DH), F32)],
        compiler_params=_params(("parallel",)),
    )(q, k, v, bg, bgt, t, u, w, p, dqg, dp, du, dw, dks, dgam)


def _local_step(x, tgt, h, w_g, cqw, late, norm_in_w, ad, gdn_norm_w, conv_b, final_norm_w,
                on_grad_c=None, on_grad_g=None, on_q=None):
    proj_g = _matmul(h, w_g, NT, F32, 512, 1408, 1024, "mm_proj_g", n=GW_COLS, b_outer=True)
    q, k, v = _prep_qkv(proj_g, cqw)
    if on_q is not None:
        q = on_q(q)
    bg, bgt = _prep_bg(proj_g, ad)
    u, w, p, t = _gdn_intra(q, k, v, bg, bgt)
    o, vn, s_in = _gdn_scan(q, k, bg, u, w, p)
    w_c, w_out, conv_w = late(o)
    proj_c = _matmul(h, w_c, NT, F32, 512, 1024, 1024, "mm_proj_c", n=CW_COLS, b_outer=True)
    mix = _conv_branch(proj_c, conv_w, conv_b, _gdn_out(o, proj_g, gdn_norm_w))
    out = _matmul(mix, w_out, NN, F32, 512, 512, 2048, "mm_out", add=x)
    dout, dout_b, g_fn, loss = _final_loss(out, tgt, final_norm_w)

    dmix = _matmul(dout_b, w_out, NT, F32, 512, 1024, 1024, "mm_dmix", b_outer=True)
    g_wout = _matmul(mix, dout_b, TN, BF16, 512, 512, 2048, "mm_gwout")
    do, dproj_g, g_gn = _gdn_out_bwd(o, proj_g, gdn_norm_w, dmix)
    dproj_c, g_cw, g_cb = _conv_branch_bwd(proj_c, conv_w, conv_b, dmix)
    g_c = _matmul(dproj_c, h, TN, BF16, 1024, 512, 2048, "mm_gwin_c")
    if on_grad_c is not None:
        do = on_grad_c(g_c, g_wout, do)
    dqg, dp, du, dw, dks, dgam = _gdn_scan_bwd(q, k, bg, w, p, vn, s_in, do)
    dq, dk, dv, dbg = _gdn_intra_bwd(q, k, v, bg, bgt, t, u, w, p, dqg, dp, du, dw, dks, dgam)
    dproj_g, gq, gk, gv = _prep_qkv_bwd(proj_g, cqw, dq, dk, dv, dproj_g)
    dproj_g, g_al, g_dt = _prep_bg_bwd(proj_g, ad, dbg, dproj_g)
    g_g = _matmul(dproj_g, h, TN, BF16, 1408, 512, 2048, "mm_gwin_g")
    if on_grad_g is not None:
        dproj_g = on_grad_g(g_g, dproj_g)
    dh = _matmul(dproj_g, w_g, NN, F32, 1024, 1024, 1408, "mm_dh_g")
    dh = _matmul(dproj_c, w_c, NN, F32, 1024, 1024, 1024, "mm_dh_c", add=dh)
    gx, g_nin = _rms_in_bwd(x, norm_in_w, dh, dout)
    small = dict(nin=g_nin, cb=g_cb, fn=g_fn, al=g_al, dt=g_dt, gn=g_gn, cq=(gq, gk, gv), cw=g_cw, loss=loss)
    return gx, small, (g_g, g_c, g_wout)


def _place():
    x, y, c = lax.axis_index("x"), lax.axis_index("y"), lax.axis_index("c")
    chips = [(1 - x, y), (x, 1 - y), (1 - x, 1 - y)]
    return x, y, c, chips


def _blk(ref, b):
    if isinstance(b, int):
        return ref.at[b * DH:(b + 1) * DH, :]
    return ref.at[pl.ds(pl.multiple_of(b * DH, DH), DH), :]


HBM = pl.BlockSpec(memory_space=pltpu.HBM)
SEM = pl.BlockSpec(memory_space=pltpu.SEMAPHORE)
EFFECT = pltpu.SideEffectType.DATAFLOW_SIDE_EFFECTING


def _split_start(name, issue, bufs, n_sems):
    nbuf = len(bufs)

    def body(*refs):
        issue(refs[:nbuf], refs[nbuf], refs[nbuf + 1])
        refs[-1][...] = jnp.zeros_like(refs[-1])

    out = pl.pallas_call(
        body, name=name,
        out_shape=(pltpu.SemaphoreType.DMA((n_sems,)), pltpu.SemaphoreType.DMA((n_sems,)),
                   *[pltpu.HBM(b.shape, b.dtype) for b in bufs], jax.ShapeDtypeStruct((8, DH), F32)),
        in_specs=[HBM] * nbuf,
        out_specs=(SEM, SEM, *[HBM] * nbuf, pl.BlockSpec(memory_space=pltpu.VMEM)),
        input_output_aliases={a: 2 + a for a in range(nbuf)},
        compiler_params=pltpu.CompilerParams(has_side_effects=EFFECT),
    )(*[pltpu.with_memory_space_constraint(b, pltpu.HBM) for b in bufs])
    return out[0], out[1], list(out[2:2 + nbuf]), out[-1]


def _split_wait(name, await_, send_sems, recv_sems, bufs, after):
    nbuf = len(bufs)

    def body(*refs):
        await_(refs[:nbuf], refs[nbuf], refs[nbuf + 1])

    out = pl.pallas_call(
        body, name=name,
        out_shape=tuple(pltpu.HBM(b.shape, b.dtype) for b in bufs),
        in_specs=[HBM] * nbuf + [SEM, SEM, ANY], out_specs=tuple([HBM] * nbuf),
        input_output_aliases={a: a for a in range(nbuf)},
        compiler_params=pltpu.CompilerParams(has_side_effects=EFFECT),
    )(*bufs, send_sems, recv_sems, after)
    return list(out)


def _phase_blocks(chip, phase, edges, parity=None):
    return [(b, blk) for b, (grp, blk) in enumerate(_shard_blocks(chip, edges))
            if grp == phase and (parity is None or b % 2 == parity)]


def _cols(ref, nblk):
    return ref.at[0:nblk * DH, :]


def _block_table(chip, edges, spare_g, spare_c):
    rows = []
    for s in range(4):
        sb = _shard_blocks(s, edges)
        rows.append([[blk if grp == "g" else spare_g for grp, blk in sb],
                     [blk if grp == "c" else spare_c for grp, blk in sb],
                     [int(grp == "g") for grp, _ in sb], [s] * ALIGNED_BLOCKS])
    return jnp.asarray(rows, jnp.int32)[chip]


def _place_own(a_shard, wo, cq, cw, bufs):
    d = a_shard.shape[1]
    chip = 2 * lax.axis_index("x") + lax.axis_index("y")

    def body(t_ref, a_ref, wo_ref, cq_ref, cw_ref, *refs):
        wg_ref, wc_ref, wog_ref, cqg_ref, cwg_ref = refs[5:]
        wg_ref[...] = a_ref[...]
        wc_ref[...] = a_ref[...]

        @pl.when(pl.program_id(0) == 0)
        def _():
            wog_ref[0] = wo_ref[...]
            cqg_ref[0] = cq_ref[...]
            cwg_ref[0] = cw_ref[...]

    whole = lambda s: pl.BlockSpec(s.shape, lambda b, t: (0,) * s.ndim)
    slot = lambda s: pl.BlockSpec((1,) + s.shape, lambda b, t: (t[3, 0],) + (0,) * s.ndim)
    return pl.pallas_call(
        body, name="place_own",
        grid_spec=pltpu.PrefetchScalarGridSpec(
            num_scalar_prefetch=1, grid=(ALIGNED_BLOCKS,),
            in_specs=[pl.BlockSpec((DH, d), lambda b, t: (b, 0)), whole(wo), whole(cq), whole(cw)] + [ANY] * 5,
            out_specs=[pl.BlockSpec((DH, d), lambda b, t: (t[0, b], 0)),
                       pl.BlockSpec((DH, d), lambda b, t: (t[1, b], 0)), slot(wo), slot(cq), slot(cw)]),
        out_shape=[jax.ShapeDtypeStruct(b.shape, b.dtype) for b in bufs],
        input_output_aliases={5 + a: a for a in range(5)},
        compiler_params=_params(("arbitrary",)),
    )(_block_table(chip, True, G_SPARE, C_SPARE), a_shard, wo, cq, cw, *bufs)


def _tie(x, token, name):
    def body(x_ref, t_ref, o_ref):
        del x_ref, t_ref, o_ref

    return pl.pallas_call(
        body, name=name, in_specs=[ANY, ANY], out_specs=ANY,
        out_shape=jax.ShapeDtypeStruct(x.shape, x.dtype), input_output_aliases={0: 0},
    )(x, token)


def _gather_start(phase, a_shard, w_grp, singles):
    ns = len(singles)

    def issue(refs, send_sems, recv_sems):
        a_ref, w_ref = refs[0], refs[1]
        x, y, c, chips = _place()
        mine = 2 * x + y
        for jj, (px, py) in enumerate(chips):
            to = dict(device_id=(px, py, c), device_id_type=MESH)
            for a in range(ns):
                pltpu.make_async_remote_copy(
                    src_ref=refs[2 + 2 * a], dst_ref=refs[3 + 2 * a].at[mine],
                    send_sem=send_sems.at[(1 + ns) * jj + 1 + a], recv_sem=recv_sems.at[(1 + ns) * jj + 1 + a],
                    **to).start()
        for s in range(4):
            for par in range(2):
                blocks = _phase_blocks(s, phase, True, par)
                if blocks:
                    @pl.when((mine == s) & (c == par))
                    def _():
                        for jj, (px, py) in enumerate(chips):
                            for b, blk in blocks:
                                pltpu.make_async_remote_copy(
                                    src_ref=_blk(a_ref, b), dst_ref=_blk(w_ref, blk),
                                    send_sem=send_sems.at[(1 + ns) * jj], recv_sem=recv_sems.at[(1 + ns) * jj],
                                    device_id=(px, py, c), device_id_type=MESH).start()

    bufs = [a_shard, w_grp] + [t for pair in singles for t in pair]
    return _split_start("gather_start_" + phase, issue, bufs, 3 * (1 + ns))


def _gather_wait(phase, send_sems, recv_sems, bufs, after):
    ns = (len(bufs) - 2) // 2

    def await_(refs, send_sems, recv_sems):
        a_ref, w_ref = refs[0], refs[1]
        x, y, c, chips = _place()
        mine = 2 * x + y
        for jj, (px, py) in enumerate(chips):
            to = dict(device_id=(px, py, c), device_id_type=MESH)
            peer = 2 * px + py
            for a in range(ns):
                cp = pltpu.make_async_remote_copy(
                    src_ref=refs[2 + 2 * a], dst_ref=refs[3 + 2 * a].at[mine],
                    send_sem=send_sems.at[(1 + ns) * jj + 1 + a], recv_sem=recv_sems.at[(1 + ns) * jj + 1 + a], **to)
                cp.wait_recv()
                cp.wait_send()
            for s in range(4):
                for par in range(2):
                    nblk = len(_phase_blocks(s, phase, True, par))
                    if nblk:
                        both = pltpu.make_async_remote_copy(
                            src_ref=_cols(a_ref, nblk), dst_ref=_cols(w_ref, nblk),
                            send_sem=send_sems.at[(1 + ns) * jj], recv_sem=recv_sems.at[(1 + ns) * jj], **to)

                        @pl.when((peer == s) & (c == par))
                        def _():
                            both.wait_recv()

                        @pl.when((mine == s) & (c == par))
                        def _():
                            both.wait_send()

    return _split_wait("gather_wait_" + phase, await_, send_sems, recv_sems, bufs, after)


def _sibling_forward_parts(phase):
    def each(w_ref, send_sems, recv_sems, start):
        x, y, c, chips = _place()
        to = dict(device_id=(x, y, 1 - c), device_id_type=MESH)
        for jj, (px, py) in enumerate(chips):
            peer = 2 * px + py
            for s in range(4):
                for par in range(2):
                    mine_blocks = _phase_blocks(s, phase, True, par)
                    theirs = len(_phase_blocks(s, phase, True, 1 - par))
                    if not (mine_blocks or theirs):
                        continue

                    @pl.when((peer == s) & (c == par))
                    def _():
                        if start:
                            for _, blk in mine_blocks:
                                pltpu.make_async_remote_copy(
                                    src_ref=_blk(w_ref, blk), dst_ref=_blk(w_ref, blk),
                                    send_sem=send_sems.at[jj], recv_sem=recv_sems.at[jj], **to).start()
                            return
                        if theirs:
                            pltpu.make_async_remote_copy(
                                src_ref=_cols(w_ref, theirs), dst_ref=_cols(w_ref, theirs),
                                send_sem=send_sems.at[jj], recv_sem=recv_sems.at[jj], **to).wait_recv()
                        if mine_blocks:
                            pltpu.make_async_remote_copy(
                                src_ref=_cols(w_ref, len(mine_blocks)), dst_ref=_cols(w_ref, len(mine_blocks)),
                                send_sem=send_sems.at[jj], recv_sem=recv_sems.at[jj], **to).wait_send()

    issue = lambda refs, send_sems, recv_sems: each(refs[0], send_sems, recv_sems, True)
    await_ = lambda refs, send_sems, recv_sems: each(refs[0], send_sems, recv_sems, False)
    return issue, await_


def _sibling_forward(phase, w_grp):
    issue, await_ = _sibling_forward_parts(phase)

    def body(w_in_ref, w_ref, send_sems, recv_sems):
        del w_in_ref
        issue([w_ref], send_sems, recv_sems)
        await_([w_ref], send_sems, recv_sems)

    return pl.pallas_call(
        body, name="sibling_forward_" + phase, in_specs=[ANY], out_specs=ANY,
        out_shape=jax.ShapeDtypeStruct(w_grp.shape, w_grp.dtype), input_output_aliases={0: 0},
        scratch_shapes=[pltpu.SemaphoreType.DMA((3,)), pltpu.SemaphoreType.DMA((3,))],
    )(w_grp)


def _merge_edges(w, edge0, mixed, name):
    d = w.shape[1]

    def body(e_ref, o_ref):
        o_ref[...] = e_ref[0:DH, :] + e_ref[DH:2 * DH, :]

    def to_block(i):
        r = mixed[-1]
        for kk in range(len(mixed) - 2, -1, -1):
            r = jnp.where(i == kk, mixed[kk], r)
        return r

    return pl.pallas_call(
        body, name=name, grid=(len(mixed),),
        in_specs=[pl.BlockSpec((2 * DH, d), lambda i: (edge0 // 2 + i, 0))],
        out_specs=pl.BlockSpec((DH, d), lambda i: (to_block(i), 0)),
        out_shape=jax.ShapeDtypeStruct(w.shape, w.dtype),
        input_output_aliases={0: 0},
        compiler_params=_params(("arbitrary",)),
    )(w)


def _scatter_start(phase, g_grp, land, singles, halved=False):
    ns = len(singles)

    def issue(refs, send_sems, recv_sems):
        g_ref, land_ref = refs[0], refs[1]
        x, y, c, chips = _place()
        for jj, (px, py) in enumerate(chips):
            to = dict(device_id=(px, py, c), device_id_type=MESH)
            peer = 2 * px + py
            for a in range(ns):
                pltpu.make_async_remote_copy(
                    src_ref=refs[2 + 2 * a].at[peer], dst_ref=refs[3 + 2 * a].at[jj],
                    send_sem=send_sems.at[(1 + ns) * jj + 1 + a], recv_sem=recv_sems.at[(1 + ns) * jj + 1 + a],
                    **to).start()
            for s in range(4):
                for par in ((0, 1) if halved else (None,)):
                    blocks = _phase_blocks(s, phase, False, par)
                    if blocks:
                        @pl.when((peer == s) if par is None else ((peer == s) & (c == par)))
                        def _():
                            for b, blk in blocks:
                                pltpu.make_async_remote_copy(
                                    src_ref=_blk(g_ref, blk), dst_ref=_blk(land_ref.at[jj], b),
                                    send_sem=send_sems.at[(1 + ns) * jj], recv_sem=recv_sems.at[(1 + ns) * jj],
                                    **to).start()

    bufs = [g_grp, land] + [t for pair in singles for t in pair]
    return _split_start("scatter_start_" + phase, issue, bufs, 3 * (1 + ns))


def _scatter_wait(phase, send_sems, recv_sems, bufs, after, halved=False):
    ns = (len(bufs) - 2) // 2

    def await_(refs, send_sems, recv_sems):
        g_ref, land_ref = refs[0], refs[1]
        x, y, c, chips = _place()
        mine = 2 * x + y
        for jj, (px, py) in enumerate(chips):
            to = dict(device_id=(px, py, c), device_id_type=MESH)
            peer = 2 * px + py
            for a in range(ns):
                cp = pltpu.make_async_remote_copy(
                    src_ref=refs[2 + 2 * a].at[peer], dst_ref=refs[3 + 2 * a].at[jj],
                    send_sem=send_sems.at[(1 + ns) * jj + 1 + a], recv_sem=recv_sems.at[(1 + ns) * jj + 1 + a], **to)
                cp.wait_recv()
                cp.wait_send()
            for s in range(4):
                for par in ((0, 1) if halved else (None,)):
                    nblk = len(_phase_blocks(s, phase, False, par))
                    if nblk:
                        both = pltpu.make_async_remote_copy(
                            src_ref=_cols(g_ref, nblk), dst_ref=_cols(land_ref.at[jj], nblk),
                            send_sem=send_sems.at[(1 + ns) * jj], recv_sem=recv_sems.at[(1 + ns) * jj], **to)

                        @pl.when((mine == s) if par is None else ((mine == s) & (c == par)))
                        def _():
                            both.wait_recv()

                        @pl.when((peer == s) if par is None else ((peer == s) & (c == par)))
                        def _():
                            both.wait_send()

    return _split_wait("scatter_wait_" + phase, await_, send_sems, recv_sems, bufs, after)


def _needed_blocks(phase, parity):
    return sorted({blk for s in range(4) for _, blk in _phase_blocks(s, phase, False, parity)})


def _pair_reduce(phase, g_grp):
    n, d = g_grp.shape

    def swap(g_ref, sib_ref, send_sem, recv_sem):
        x, y, c, _ = _place()
        to = dict(device_id=(x, y, 1 - c), device_id_type=MESH)
        for par in range(2):
            give, get = _needed_blocks(phase, 1 - par), _needed_blocks(phase, par)

            @pl.when(c == par)
            def _():
                for blk in give:
                    pltpu.make_async_remote_copy(src_ref=_blk(g_ref, blk), dst_ref=_blk(sib_ref, blk),
                                                 send_sem=send_sem, recv_sem=recv_sem, **to).start()
                pltpu.make_async_remote_copy(src_ref=_cols(g_ref, len(get)), dst_ref=_cols(sib_ref, len(get)),
                                             send_sem=send_sem, recv_sem=recv_sem, **to).wait_recv()
                pltpu.make_async_remote_copy(src_ref=_cols(g_ref, len(give)), dst_ref=_cols(sib_ref, len(give)),
                                             send_sem=send_sem, recv_sem=recv_sem, **to).wait_send()

    sib = pl.pallas_call(
        swap, name="pair_swap_" + phase, in_specs=[ANY], out_specs=ANY,
        out_shape=jax.ShapeDtypeStruct((n, d), g_grp.dtype),
        scratch_shapes=[pltpu.SemaphoreType.DMA, pltpu.SemaphoreType.DMA],
    )(*_in_hbm(g_grp))

    lists = [_needed_blocks(phase, par) for par in range(2)]
    longest = max(len(t) for t in lists)
    table = jnp.asarray([t + [t[-1]] * (longest - len(t)) for t in lists], jnp.int32)[lax.axis_index("c")]

    def add(t_ref, a_ref, b_ref, o_ref):
        o_ref[...] = (a_ref[...].astype(F32) + b_ref[...].astype(F32)).astype(o_ref.dtype)

    blk = pl.BlockSpec((DH, d), lambda i, t: (t[i], 0))
    return pl.pallas_call(
        add, name="pair_add_" + phase,
        grid_spec=pltpu.PrefetchScalarGridSpec(num_scalar_prefetch=1, grid=(longest,),
                                               in_specs=[blk, blk], out_specs=blk),
        out_shape=jax.ShapeDtypeStruct((n, d), g_grp.dtype),
        compiler_params=_params(("arbitrary",)),
    )(table, g_grp, sib)


def _sum_shard(g_g, g_c, land):
    d = g_g.shape[1]
    chip = 2 * lax.axis_index("x") + lax.axis_index("y")

    def body(t_ref, gg_ref, gc_ref, land_ref, o_ref):
        b = pl.program_id(0)
        in_g = t_ref[2, b] == 1
        own = jnp.where(in_g, gg_ref[...].astype(F32), gc_ref[...].astype(F32))
        for jj in range(3):
            own = own + land_ref[jj].astype(F32)
        o_ref[...] = jnp.where(in_g & (b % 2 != lax.axis_index("c")), 0.0, own)

    return pl.pallas_call(
        body, name="sum_w_in",
        grid_spec=pltpu.PrefetchScalarGridSpec(
            num_scalar_prefetch=1, grid=(ALIGNED_BLOCKS,),
            in_specs=[pl.BlockSpec((DH, d), lambda b, t: (t[0, b], 0)), pl.BlockSpec((DH, d), lambda b, t: (t[1, b], 0)),
                      pl.BlockSpec((3, DH, d), lambda b, t: (0, b, 0))],
            out_specs=pl.BlockSpec((DH, d), lambda b, t: (b, 0))),
        out_shape=jax.ShapeDtypeStruct((ALIGNED_W, d), F32),
        compiler_params=_params(("arbitrary",)),
    )(_block_table(chip, False, 0, 0), g_g, g_c, land)


def _sum_rows(stack, land, rows):
    _, r, d = stack.shape
    rows = min(rows, r)
    chip = 2 * lax.axis_index("x") + lax.axis_index("y")

    def body(t_ref, own_ref, land_ref, o_ref):
        acc = own_ref[0].astype(F32)
        for jj in range(3):
            acc = acc + land_ref[jj].astype(F32)
        o_ref[...] = acc

    return pl.pallas_call(
        body, name="sum_w_out",
        grid_spec=pltpu.PrefetchScalarGridSpec(
            num_scalar_prefetch=1, grid=(r // rows,),
            in_specs=[pl.BlockSpec((1, rows, d), lambda i, t: (t[0], i, 0)),
                      pl.BlockSpec((3, rows, d), lambda i, t: (0, i, 0))],
            out_specs=pl.BlockSpec((rows, d), lambda i, t: (i, 0))),
        out_shape=jax.ShapeDtypeStruct((r, d), F32),
        compiler_params=_params(("arbitrary",)),
    )(jnp.reshape(chip, (1,)).astype(jnp.int32), stack, land)


def _final_exchange(parts, pack):
    npart = len(parts)

    def body(*refs):
        ins, pack_ref = refs[:npart], refs[npart]
        outs, packs = refs[npart + 1:2 * npart + 1], refs[2 * npart + 1]
        send_sems, recv_sems, psend, precv, loc_sem = refs[2 * npart + 2:]
        x, y, c, _ = _place()
        me = 4 * x + 2 * y + c
        local = pltpu.make_async_copy(pack_ref, packs.at[me], loc_sem)
        local.start()
        cps = [pltpu.make_async_remote_copy(
            src_ref=ins[a], dst_ref=outs[a], send_sem=send_sems.at[a], recv_sem=recv_sems.at[a],
            device_id=(x, y, 1 - c), device_id_type=MESH) for a in range(npart)]
        for r in range(1, 8):
            dx, dy, dc = (r >> 2) & 1, (r >> 1) & 1, r & 1
            peer = (x + dx - 2 * x * dx, y + dy - 2 * y * dy, c + dc - 2 * c * dc)
            cps.append(pltpu.make_async_remote_copy(
                src_ref=pack_ref, dst_ref=packs.at[me], send_sem=psend.at[r - 1], recv_sem=precv.at[r - 1],
                device_id=peer, device_id_type=MESH))
        for cp in cps:
            cp.start()
        for cp in cps:
            cp.wait_recv()
        for cp in cps:
            cp.wait_send()
        local.wait()

    return pl.pallas_call(
        body, name="final_exchange",
        in_specs=[ANY] * (npart + 1), out_specs=[ANY] * (npart + 1),
        out_shape=[jax.ShapeDtypeStruct(p.shape, p.dtype) for p in parts]
        + [jax.ShapeDtypeStruct((8,) + pack.shape, pack.dtype)],
        scratch_shapes=[pltpu.SemaphoreType.DMA((npart,)), pltpu.SemaphoreType.DMA((npart,)),
                        pltpu.SemaphoreType.DMA((7,)), pltpu.SemaphoreType.DMA((7,)), pltpu.SemaphoreType.DMA],
    )(*parts, pack)


def _sum_packs(packs):
    def body(p_ref, o_ref):
        acc = p_ref[0]
        for d in range(1, 8):
            acc = acc + p_ref[d]
        o_ref[...] = acc

    return pl.pallas_call(
        body, name="sum_packs", out_shape=jax.ShapeDtypeStruct(packs.shape[1:], F32),
    )(packs)


def _adamw_update(g, w_ref, m_ref, v_ref, go, do, mo, vo):
    c1 = 1.0 / (1.0 - ADAM_B1 ** ADAM_STEP)
    c2 = 1.0 / (1.0 - ADAM_B2 ** ADAM_STEP)
    mn = ADAM_B1 * m_ref[...] + (1.0 - ADAM_B1) * g
    vn = ADAM_B2 * v_ref[...] + (1.0 - ADAM_B2) * (g * g)
    go[...] = g
    mo[...] = mn
    vo[...] = vn
    do[...] = -ADAM_LR * ((mn * c1) / (jnp.sqrt(vn * c2) + ADAM_EPS) + ADAM_WD * w_ref[...])


def _adamw(w, m, v, g1, g2, rows, name):
    r, cdim = w.shape
    rows = min(rows, r)

    def body(*refs):
        n_in = 4 if g2 is None else 5
        w_ref, m_ref, v_ref, g_ref = refs[:4]
        g = g_ref[...] if g2 is None else g_ref[...] + refs[4][...]
        _adamw_update(g, w_ref, m_ref, v_ref, *refs[n_in:n_in + 4])

    blk = pl.BlockSpec((rows, cdim), lambda i: (i, 0))
    args = [w, m, v, g1] + ([] if g2 is None else [g2])
    shp = jax.ShapeDtypeStruct((r, cdim), F32)
    return pl.pallas_call(
        body, name=name, grid=(r // rows,),
        in_specs=[blk] * len(args), out_specs=[blk] * 4, out_shape=[shp] * 4,
        compiler_params=_params(("parallel",), 20 * rows * cdim * 4 + 8 * 2**20),
    )(*_in_hbm(*args))


def _adamw_shard(wt, mt, vt, g1, g2):
    r, d = wt.shape
    cols = min(128, d)

    def body(w_ref, m_ref, v_ref, g_ref, g2_ref, go, do, mo, vo, pad_ref):
        chip = 2 * lax.axis_index("x") + lax.axis_index("y")
        back = [(ALIGNED_W - s) % ALIGNED_W for s in SHIFTS]
        pad_ref[...] = pltpu.roll(g_ref[...] + g2_ref[...], _by_chip(chip, back), 0)
        outs = [o.at[:, 0, :] for o in (go, do, mo, vo)]
        _adamw_update(pad_ref[0:r, :], w_ref, m_ref, v_ref, *outs)

    blk = pl.BlockSpec((r, cols), lambda i: (0, i))
    gblk = pl.BlockSpec((ALIGNED_W, cols), lambda i: (0, i))
    oblk = pl.BlockSpec((r, 1, cols), lambda i: (0, 0, i))
    shp = jax.ShapeDtypeStruct((r, 1, d), F32)
    return pl.pallas_call(
        body, name="adamw_w_in", grid=(d // cols,),
        in_specs=[blk] * 3 + [gblk] * 2, out_specs=[oblk] * 4, out_shape=[shp] * 4,
        scratch_shapes=[pltpu.VMEM((ALIGNED_W, cols), F32)],
        compiler_params=_params(("parallel",), 24 * ALIGNED_W * cols * 4 + 8 * 2**20),
    )(wt, mt, vt, g1, g2)


def _pad_lanes(a, width):
    return jnp.pad(a, ((0, 0), (0, width - a.shape[1])))


def _gathered_to_full(g):
    return jnp.transpose(g, (1, 0, 2)).reshape(g.shape[1], 4 * g.shape[2])


def _row(a):
    return _pad_lanes(a.reshape(1, -1), 1024)


def _small_pack(nin, cb, fn, al, dt, gn, cqw_shard, cw_shard):
    ad = jnp.concatenate([al.reshape(1, -1), dt.reshape(1, -1)], axis=1)
    rows = [_row(nin), _row(cb), _row(fn), _row(ad), _row(gn), cqw_shard.reshape(3, 1024), _row(cw_shard)]
    out = jnp.concatenate(rows, axis=0)
    return jnp.pad(out, ((0, 16 - out.shape[0]), (0, 0)))


def kernel(x, norm_in_w, w_in, conv_qkv_w, A_log, dt_bias, gdn_norm_w, conv_w, conv_b, w_out, final_norm_w, loss_target, m_norm_in_w, m_w_in, m_conv_qkv_w, m_A_log, m_dt_bias, m_gdn_norm_w, m_conv_w, m_conv_b, m_w_out, m_final_norm_w, v_norm_in_w, v_w_in, v_conv_qkv_w, v_A_log, v_dt_bias, v_gdn_norm_w, v_conv_w, v_conv_b, v_w_out, v_final_norm_w):
    chip = 2 * lax.axis_index("x") + lax.axis_index("y")
    a_shard = _align_shard(jnp.transpose(w_in, (2, 0, 1)))
    wo_b = _cast_bf16(w_out[0], 256, "cast_w_out")
    d_model = x.shape[-1]
    stack = lambda s: lax.empty((4,) + s.shape, s.dtype)
    wg0 = lax.empty((WG_BLOCKS * DH, d_model), BF16)
    wc0 = lax.empty((WC_BLOCKS * DH, d_model), BF16)
    ss_g, rs_g, bufs_g, tok_g = _gather_start("g", a_shard, wg0, [(conv_qkv_w[0], stack(conv_qkv_w[0]))])
    ss_c, rs_c, bufs_c, tok_c = _gather_start("c", bufs_g[0], wc0,
                                              [(conv_w[0], stack(conv_w[0])), (wo_b, stack(wo_b))])
    wg1, wc1, wog1, cqg1, cwg1 = _place_own(bufs_c[0], bufs_c[4], bufs_g[2], bufs_c[2],
                                            [bufs_g[1], bufs_c[1], bufs_c[5], bufs_g[3], bufs_c[3]])
    x0 = x[0]
    h = _rms_in(x0, _tie(_tie(norm_in_w, tok_g, "after_gather_start_g"), tok_c, "after_gather_start_c"))
    a_thru, wg, _, cq_g = _gather_wait("g", ss_g, rs_g, [bufs_c[0], wg1, bufs_g[2], cqg1], h)
    w_g = _merge_edges(_sibling_forward("g", wg), G_EDGE, G_MIXED, "merge_edges_g")
    cqw = _gathered_to_full(cq_g)
    ad = jnp.pad(jnp.concatenate([A_log, dt_bias], axis=0), ((0, 0), (A_LANE, 0)))
    fwd_c = {}

    def on_q(q):
        _, wc, _, cw_g, _, wo_g = _gather_wait("c", ss_c, rs_c,
                                               [a_thru, wc1, bufs_c[2], cwg1, bufs_c[4], wog1], q)
        issue, _ = _sibling_forward_parts("c")
        ss, rs, (wc,), tok = _split_start("sibling_forward_start_c", issue, [wc], 3)
        fwd_c.update(ss=ss, rs=rs, wc=wc, cw_g=cw_g, wo_g=wo_g)
        return _tie(q, tok, "after_sibling_forward_start_c")

    def late(o):
        _, await_ = _sibling_forward_parts("c")
        (wc,) = _split_wait("sibling_forward_wait_c", await_, fwd_c["ss"], fwd_c["rs"], [fwd_c["wc"]], o)
        return (_merge_edges(wc, C_EDGE, C_MIXED, "merge_edges_c"), fwd_c["wo_g"].reshape(2 * GW, d_model),
                _gathered_to_full(fwd_c["cw_g"]))

    scat = {}

    def on_grad_c(g_c, g_wout, do):
        go4 = g_wout.reshape(4, GW // 2, d_model)
        land = lax.empty((3, ALIGNED_W, d_model), BF16)
        land_o = lax.empty((3, GW // 2, d_model), BF16)
        ss, rs, bufs, tok = _scatter_start("c", g_c, land, [(go4, land_o)])
        scat["c"] = (ss, rs, bufs)
        return _tie(do, tok, "after_scatter_start_c")

    def on_grad_g(g_g, dproj_g):
        ss, rs, bufs, tok = _scatter_start("g", _pair_reduce("g", g_g), scat["c"][2][1], [], halved=True)
        scat["g"] = (ss, rs, bufs)
        return _tie(dproj_g, tok, "after_scatter_start_g")

    gx, sm, _ = _local_step(x0, loss_target[0], h, w_g, cqw, late, norm_in_w, ad, gdn_norm_w, conv_b,
                            final_norm_w.reshape(1, -1), on_grad_c, on_grad_g, on_q)

    ss, rs, bufs = scat["g"]
    g_g, land = _scatter_wait("g", ss, rs, bufs, gx, halved=True)
    ss, rs, bufs = scat["c"]
    g_c, land, go4, land_o = _scatter_wait("c", ss, rs, [bufs[0], land, bufs[2], bufs[3]], gx)
    part_in = _sum_shard(g_g, g_c, land)
    part_out = _sum_rows(go4, land_o, 128)
    ad_g = jnp.concatenate([sm["al"][:, A_LANE:], sm["dt"][:, A_LANE:]], axis=1)
    pack = jnp.concatenate([_row(sm["nin"]), _row(sm["cb"]), _row(sm["fn"]), _row(ad_g), _row(sm["gn"]),
                            jnp.concatenate(sm["cq"], axis=1).reshape(12, 1024), sm["cw"], _row(sm["loss"])], axis=0)
    pack = jnp.pad(pack, ((0, PACK_ROWS - pack.shape[0]), (0, 0)))
    sib_in, sib_out, packs = _final_exchange([part_in, part_out], pack)
    tot = _sum_packs(packs)

    g_wi, d_wi, m_wi, v_wi = [jnp.transpose(a, (1, 2, 0))[0] for a in _adamw_shard(
        jnp.transpose(w_in[0]), jnp.transpose(m_w_in[0]), jnp.transpose(v_w_in[0]), part_in, sib_in)]
    g_wo, d_wo, m_wo, v_wo = _adamw(w_out[0], m_w_out[0], v_w_out[0], part_out, sib_out, 128, "adamw_w_out")
    g_cq_sh = lax.dynamic_slice_in_dim(tot[R_CQ:R_CQ + 12].reshape(4, 3 * GW), chip * 768, 768, axis=1)
    g_cw_sh = lax.dynamic_slice_in_dim(tot[R_CW:R_CW + 3], chip * 256, 256, axis=1)
    sp = lambda nin, cb, fn, al, dt, gn, cq, cwv: _small_pack(nin, cb, fn, al, dt, gn, cq[0], cwv[0])
    g_s = _small_pack(tot[R_NIN], tot[R_CB], tot[R_FN], tot[R_AD, :HEADS], tot[R_AD, HEADS:2 * HEADS],
                      tot[R_GN, :DH], g_cq_sh, g_cw_sh)
    w_s = sp(norm_in_w, conv_b, final_norm_w, A_log, dt_bias, gdn_norm_w, conv_qkv_w, conv_w)
    m_s = sp(m_norm_in_w, m_conv_b, m_final_norm_w, m_A_log, m_dt_bias, m_gdn_norm_w, m_conv_qkv_w, m_conv_w)
    v_s = sp(v_norm_in_w, v_conv_b, v_final_norm_w, v_A_log, v_dt_bias, v_gdn_norm_w, v_conv_qkv_w, v_conv_w)
    small = _adamw(w_s, m_s, v_s, g_s, None, 16, "adamw_small")

    def unpack(a, big_in, big_out):
        return (a[0:1], big_in[None], a[5:8].reshape(1, 4, 768), a[3:4, :HEADS], a[3:4, HEADS:2 * HEADS],
                a[4:5, :DH], a[8, :768].reshape(1, 3, 256), a[1:2], big_out[None], a[2])

    loss = tot[R_LOSS, 0]
    return (loss, gx[None], *unpack(small[0], g_wi, g_wo), *unpack(small[1], d_wi, d_wo),
            *unpack(small[2], m_wi, m_wo), *unpack(small[3], v_wi, v_wo))
```

```python
import functools
import math

import jax
import jax.numpy as jnp
from jax import lax
from jax.experimental import pallas as pl
from jax.experimental.pallas import tpu as pltpu

F32 = jnp.float32
BF16 = jnp.bfloat16
MESH = pl.DeviceIdType.MESH
ANY = pl.BlockSpec(memory_space=pl.ANY)

HEADS = 8
DH = 128
CH = 64
GW = HEADS * DH
EPS = 1e-6
VMEM_V7X = 64 * 1024 * 1024

QB, KB, VB, ZB, BAB = 0, 8, 16, 24, 32
A_LANE = 120
NG, NC = 33, 32
GW_COLS, CW_COLS = NG * DH, NC * DH

SHARD_W = 2052
ALIGNED_BLOCKS = 17
ALIGNED_W = ALIGNED_BLOCKS * DH
SHIFTS = (0, 4, ALIGNED_W - 8, ALIGNED_W - 4)
G_EDGE, C_EDGE = 34, 32
G_SPARE, C_SPARE = 33, 34
WG_BLOCKS, WC_BLOCKS = 38, 36
G_MIXED, C_MIXED = (2, BAB), (4 * 7 + 1,)


def _shard_blocks(chip, edges):
    g, c = "g", "c"
    if chip == 0:
        out = [(g, 3 * b) for b in range(8)] + [(g, 3 * b + 1) for b in range(8)] + [(g, G_EDGE, G_MIXED[0])]
    elif chip == 1:
        out = [(g, G_EDGE + 1, G_MIXED[0])] + [(g, 3 * b + 2) for b in range(1, 8)]
        out += [(g, ZB + b) for b in range(8)] + [(g, G_EDGE + 2, G_MIXED[1])]
    elif chip == 2:
        out = [(c, 4 * b) for b in range(8)] + [(c, 4 * b + 1) for b in range(7)]
        out += [(c, C_EDGE, C_MIXED[0]), (g, G_EDGE + 3, G_MIXED[1])]
    else:
        out = [(c, 4 * b + 2) for b in range(8)] + [(c, 4 * b + 3) for b in range(8)] + [(c, C_EDGE + 1, C_MIXED[0])]
    return [(o[0], o[1] if (edges or len(o) == 2) else o[2]) for o in out]


def _by_chip(chip, vals):
    if all(v == vals[0] for v in vals):
        return vals[0]
    r = vals[3]
    for kk in (2, 1, 0):
        r = jnp.where(chip == kk, vals[kk], r)
    return r

ADAM_LR, ADAM_B1, ADAM_B2, ADAM_EPS, ADAM_WD, ADAM_STEP = 0.001, 0.9, 0.999, 1e-08, 0.01, 10

R_NIN, R_CB, R_FN, R_AD, R_GN, R_CQ, R_CW, R_LOSS, PACK_ROWS = 0, 1, 2, 3, 4, 5, 17, 20, 24

NN = ((1,), (0,))
NT = ((1,), (1,))
TN = ((0,), (0,))


def _dot(a, b, dims=NN, mode="lo"):
    dn = (dims, ((), ()))
    if mode == "hi":
        return lax.dot_general(a, b, dn, precision=lax.Precision.HIGHEST, preferred_element_type=F32)
    ah, bh = a.astype(BF16), b.astype(BF16)
    out = lax.dot_general(ah, bh, dn, preferred_element_type=F32)
    if mode == "x3":
        al = (a - ah.astype(F32)).astype(BF16)
        bl = (b - bh.astype(F32)).astype(BF16)
        out = out + lax.dot_general(ah, bl, dn, preferred_element_type=F32)
        out = out + lax.dot_general(al, bh, dn, preferred_element_type=F32)
    return out


P_GRAM, P_INV, P_SOL, P_SCAN, P_SCANB, P_BWD = "lo", "lo", "lo", "lo", "lo", "lo"
P_CUM = "x3"


def _params(sem=None, vmem=None):
    kw = {}
    if sem is not None:
        kw["dimension_semantics"] = sem
    if vmem is not None:
        kw["vmem_limit_bytes"] = int(min(max(vmem, 32 * 2**20), VMEM_V7X - 8 * 2**20))
    return pltpu.CompilerParams(**kw)


def _in_hbm(*arrays):
    return [pltpu.with_memory_space_constraint(a, pltpu.HBM) for a in arrays]


def _sigmoid(x):
    return 1.0 / (1.0 + jnp.exp(-x))


def _dsilu(x, s):
    return s * (1.0 + x * (1.0 - s))


def _rows(shape):
    return lax.broadcasted_iota(jnp.int32, shape, 0)


def _shift_down(x, s):
    if s == 0:
        return x
    return jnp.where(_rows(x.shape) >= s, pltpu.roll(x, s, 0), 0.0)


def _shift_up(x, s):
    if s == 0:
        return x
    n = x.shape[0]
    return jnp.where(_rows(x.shape) < n - s, pltpu.roll(x, n - s, 0), 0.0)


def _matmul(a, b, dims, out_dtype, tm, tn, tk, name, add=None, n=None, b_outer=False):
    if dims == NN:
        (m, k), n = a.shape, b.shape[1]
    elif dims == NT:
        (m, k), n = a.shape, (n or b.shape[0])
    else:
        (k, m), n = a.shape, b.shape[1]
    tm, tn, tk = min(tm, m), min(tn, n), min(tk, k)
    assert m % tm == 0 and n % tn == 0 and k % tk == 0, (name, m, n, k, tm, tn, tk)
    nk = k // tk

    def body(*refs):
        if add is None:
            a_ref, b_ref, o_ref = refs[:3]
            add_ref = None
        else:
            a_ref, b_ref, add_ref, o_ref = refs[:4]
        part = _dot(a_ref[...], b_ref[...], dims)
        if nk == 1:
            if add_ref is not None:
                part = part + add_ref[...]
            o_ref[...] = part.astype(out_dtype)
            return
        acc = refs[-1]
        kk = pl.program_id(2)

        @pl.when(kk == 0)
        def _():
            acc[...] = part

        @pl.when(kk > 0)
        def _():
            acc[...] += part

        @pl.when(kk == nk - 1)
        def _():
            r = acc[...]
            if add_ref is not None:
                r = r + add_ref[...]
            o_ref[...] = r.astype(out_dtype)

    ij = (lambda g0, g1: (g1, g0)) if b_outer else (lambda g0, g1: (g0, g1))

    def spec(shape, pick):
        return pl.BlockSpec(shape, lambda g0, g1, kk: pick(*ij(g0, g1), kk))

    a_spec = spec((tk, tm), lambda i, j, kk: (kk, i)) if dims == TN else spec((tm, tk), lambda i, j, kk: (i, kk))
    b_spec = spec((tn, tk), lambda i, j, kk: (j, kk)) if dims == NT else spec((tk, tn), lambda i, j, kk: (kk, j))
    o_spec = spec((tm, tn), lambda i, j, kk: (i, j))
    in_specs = [a_spec, b_spec]
    args = [a, b]
    if add is not None:
        in_specs.append(o_spec)
        args.append(add)
    osz = jnp.dtype(out_dtype).itemsize
    est = 2 * (tm * tk * a.dtype.itemsize + tk * tn * b.dtype.itemsize + tm * tn * osz)
    est += 3 * tm * tn * 4 + (2 * tm * tn * 4 if add is not None else 0)
    return pl.pallas_call(
        body, name=name, grid=(n // tn, m // tm, nk) if b_outer else (m // tm, n // tn, nk),
        in_specs=in_specs, out_specs=o_spec,
        out_shape=jax.ShapeDtypeStruct((m, n), out_dtype),
        scratch_shapes=[pltpu.VMEM((tm, tn), F32)] if nk > 1 else [],
        compiler_params=_params(("parallel", "parallel", "arbitrary"), est + 8 * 2**20),
    )(*args)


def _cast_bf16(a, rows, name):
    r, c = a.shape
    rows = min(rows, r)

    def body(a_ref, o_ref):
        o_ref[...] = a_ref[...].astype(BF16)

    return pl.pallas_call(
        body, name=name, grid=(r // rows,),
        in_specs=[pl.BlockSpec((rows, c), lambda i: (i, 0))],
        out_specs=pl.BlockSpec((rows, c), lambda i: (i, 0)),
        out_shape=jax.ShapeDtypeStruct((r, c), BF16),
        compiler_params=_params(("parallel",)),
    )(a)


def _align_shard(wt):
    r, _, d = wt.shape
    cols = min(256, d)

    def body(w_ref, o_ref, pad_ref):
        chip = 2 * lax.axis_index("x") + lax.axis_index("y")
        pad_ref[...] = jnp.zeros_like(pad_ref)
        pad_ref[0:r, :] = w_ref[:, 0, :]
        o_ref[...] = pltpu.roll(pad_ref[...], _by_chip(chip, SHIFTS), 0).astype(BF16)

    return pl.pallas_call(
        body, name="align_shard", grid=(d // cols,),
        in_specs=[pl.BlockSpec((r, 1, cols), lambda i: (0, 0, i))],
        out_specs=pl.BlockSpec((ALIGNED_W, cols), lambda i: (0, i)),
        out_shape=jax.ShapeDtypeStruct((ALIGNED_W, d), BF16),
        scratch_shapes=[pltpu.VMEM((ALIGNED_W, cols), F32)],
        compiler_params=_params(("parallel",)),
    )(wt)


def _rms_in(x, w):
    n, d = x.shape
    tr = min(256, n)

    def body(x_ref, w_ref, h_ref):
        xv = x_ref[...]
        r = lax.rsqrt(jnp.mean(xv * xv, axis=-1, keepdims=True) + EPS)
        h_ref[...] = (xv * r * w_ref[...]).astype(BF16)

    return pl.pallas_call(
        body, name="rms_in", grid=(n // tr,),
        in_specs=[pl.BlockSpec((tr, d), lambda i: (i, 0)), pl.BlockSpec((1, d), lambda i: (0, 0))],
        out_specs=pl.BlockSpec((tr, d), lambda i: (i, 0)),
        out_shape=jax.ShapeDtypeStruct((n, d), BF16),
        compiler_params=_params(("parallel",)),
    )(x, w)


def _conv_silu(p, w_ref, taps):
    c = None
    for j in range(taps):
        t = _shift_down(p, taps - 1 - j) * w_ref[j:j + 1, :]
        c = t if c is None else c + t
    return c


def _prep_qkv(proj, cw):
    n = proj.shape[0]

    def body(p3, wq, wk, wv, q_ref, k_ref, v_ref):
        for kind, (w_ref, o_ref) in enumerate(((wq, q_ref), (wk, k_ref), (wv, v_ref))):
            c = _conv_silu(p3[:, kind * DH:(kind + 1) * DH], w_ref, 4)
            a = c * _sigmoid(c)
            if kind < 2:
                r = lax.rsqrt(jnp.sum(a * a, axis=-1, keepdims=True) + EPS)
                a = a * (r * (DH ** -0.5 if kind == 0 else 1.0))
            o_ref[...] = a

    col = pl.BlockSpec((n, DH), lambda h: (0, h))
    wcol = lambda base: pl.BlockSpec((4, DH), lambda h: (0, base + h))
    out = jax.ShapeDtypeStruct((n, GW), F32)
    return pl.pallas_call(
        body, name="prep_qkv", grid=(HEADS,),
        in_specs=[pl.BlockSpec((n, 3 * DH), lambda h: (0, h)), wcol(QB), wcol(KB), wcol(VB)],
        out_specs=[col] * 3, out_shape=[out] * 3,
        compiler_params=_params(("parallel",), 40 * 2**20),
    )(proj, cw, cw, cw)


def _prep_qkv_bwd(proj, cw, dq, dk, dv, dproj):
    n = proj.shape[0]

    def body(p3, wq, wk, wv, dq_ref, dk_ref, dv_ref, _, o3, gq, gk, gv):
        for kind, (w_ref, d_ref, g_ref) in enumerate(((wq, dq_ref, gq), (wk, dk_ref, gk), (wv, dv_ref, gv))):
            p = p3[:, kind * DH:(kind + 1) * DH]
            shifted = [_shift_down(p, 3 - j) for j in range(4)]
            c = shifted[0] * w_ref[0:1, :]
            for j in range(1, 4):
                c = c + shifted[j] * w_ref[j:j + 1, :]
            s = _sigmoid(c)
            a = c * s
            d = d_ref[...]
            if kind < 2:
                r = lax.rsqrt(jnp.sum(a * a, axis=-1, keepdims=True) + EPS)
                sc = DH ** -0.5 if kind == 0 else 1.0
                d = (sc * r) * (d - a * ((r * r) * jnp.sum(d * a, axis=-1, keepdims=True)))
            dc = d * _dsilu(c, s)
            dp = None
            for j in range(4):
                g_ref[j:j + 1, :] = jnp.sum(dc * shifted[j], axis=0, keepdims=True)
                t = _shift_up(dc, 3 - j) * w_ref[j:j + 1, :]
                dp = t if dp is None else dp + t
            o3[:, kind * DH:(kind + 1) * DH] = dp.astype(BF16)

    col = pl.BlockSpec((n, DH), lambda h: (0, h))
    wcol = lambda base: pl.BlockSpec((4, DH), lambda h: (0, base + h))
    p3spec = pl.BlockSpec((n, 3 * DH), lambda h: (0, h))
    return pl.pallas_call(
        body, name="prep_qkv_bwd", grid=(HEADS,),
        in_specs=[p3spec, wcol(QB), wcol(KB), wcol(VB), col, col, col, ANY],
        out_specs=[p3spec] + [wcol(0)] * 3,
        out_shape=[jax.ShapeDtypeStruct(dproj.shape, BF16)] + [jax.ShapeDtypeStruct((4, GW), F32)] * 3,
        input_output_aliases={7: 0},
        compiler_params=_params(("parallel",), 48 * 2**20),
    )(proj, cw, cw, cw, dq, dk, dv, dproj)


CPB = 8
SCAN_CPS = 4


def _tri(lower, rows):
    i = lax.broadcasted_iota(jnp.int32, (rows, rows), 0)
    j = lax.broadcasted_iota(jnp.int32, (rows, rows), 1)
    return jnp.where((i // CH == j // CH) & ((i >= j) if lower else (j >= i)), 1.0, 0.0)


def _lane(shape):
    return lax.broadcasted_iota(jnp.int32, shape, 1)


def _prep_bg(proj, ad):
    n = proj.shape[0]
    nch = n // CH
    cpb = CPB if nch % CPB == 0 else 1
    rows = cpb * CH

    def body(p_ref, ad_ref, bg_ref, bgt_ref):
        p = p_ref[...]
        lane = _lane(p.shape)
        beta = _sigmoid(p)
        xa = p + ad_ref[1:2, :]
        sp = jnp.maximum(xa, 0.0) + jnp.log(1.0 + jnp.exp(-jnp.abs(xa)))
        g = pltpu.roll(-jnp.exp(ad_ref[0:1, :]) * sp, DH - A_LANE + HEADS, 1)
        gc = _dot(_tri(True, rows), g, NN, P_CUM)
        bg = jnp.where(lane < HEADS, beta, jnp.where(lane < 2 * HEADS, gc, 0.0))
        bg_ref[...] = bg
        for ci in range(cpb):
            bgt_ref[ci] = bg[ci * CH:(ci + 1) * CH, :].T

    return pl.pallas_call(
        body, name="prep_bg", grid=(nch // cpb,),
        in_specs=[pl.BlockSpec((rows, DH), lambda i: (i, BAB)), pl.BlockSpec((2, DH), lambda i: (0, 0))],
        out_specs=[pl.BlockSpec((rows, DH), lambda i: (i, 0)), pl.BlockSpec((cpb, DH, CH), lambda i: (i, 0, 0))],
        out_shape=[jax.ShapeDtypeStruct((n, DH), F32), jax.ShapeDtypeStruct((nch, DH, CH), F32)],
        compiler_params=_params(("parallel",)),
    )(*_in_hbm(proj, ad))


def _prep_bg_bwd(proj, ad, dbg, dproj):
    n = proj.shape[0]
    nch = n // CH
    cpb = CPB if nch % CPB == 0 else 1
    rows = cpb * CH

    def body(p_ref, ad_ref, d_ref, _, o_ref, ga_ref, gd_ref):
        p = p_ref[...]
        d = d_ref[...]
        lane = _lane(p.shape)
        beta = _sigmoid(p)
        xa = p + ad_ref[1:2, :]
        sp = jnp.maximum(xa, 0.0) + jnp.log(1.0 + jnp.exp(-jnp.abs(xa)))
        na = -jnp.exp(ad_ref[0:1, :])
        dg = pltpu.roll(_dot(_tri(False, rows), d, NN, P_CUM), A_LANE - HEADS, 1)
        da = dg * na * _sigmoid(xa)
        is_g = lane >= A_LANE
        o_ref[...] = jnp.where(lane < HEADS, d * beta * (1.0 - beta), jnp.where(is_g, da, 0.0)).astype(BF16)
        ga = jnp.sum(jnp.where(is_g, dg * na * sp, 0.0), axis=0, keepdims=True)
        gd = jnp.sum(jnp.where(is_g, da, 0.0), axis=0, keepdims=True)

        @pl.when(pl.program_id(0) == 0)
        def _():
            ga_ref[...] = jnp.zeros_like(ga_ref)
            gd_ref[...] = jnp.zeros_like(gd_ref)

        ga_ref[...] += ga
        gd_ref[...] += gd

    one = pl.BlockSpec((1, DH), lambda i: (0, 0))
    return pl.pallas_call(
        body, name="prep_bg_bwd", grid=(nch // cpb,),
        in_specs=[pl.BlockSpec((rows, DH), lambda i: (i, BAB)), pl.BlockSpec((2, DH), lambda i: (0, 0)),
                  pl.BlockSpec((rows, DH), lambda i: (i, 0)), ANY],
        out_specs=[pl.BlockSpec((rows, DH), lambda i: (i, BAB)), one, one],
        out_shape=[jax.ShapeDtypeStruct(dproj.shape, BF16), jax.ShapeDtypeStruct((1, DH), F32),
                   jax.ShapeDtypeStruct((1, DH), F32)],
        input_output_aliases={3: 0},
        compiler_params=_params(("arbitrary",)),
    )(proj, ad, dbg, dproj)


def _gdn_out(o, proj, wg):
    n = o.shape[0]

    def body(o_ref, z_ref, w_ref, y_ref):
        ov, z = o_ref[...], z_ref[...]
        r = lax.rsqrt(jnp.mean(ov * ov, axis=-1, keepdims=True) + EPS)
        y_ref[...] = (ov * r * w_ref[...] * (z * _sigmoid(z))).astype(BF16)

    return pl.pallas_call(
        body, name="gdn_out", grid=(HEADS,),
        in_specs=[pl.BlockSpec((n, DH), lambda h: (0, h)), pl.BlockSpec((n, DH), lambda h: (0, ZB + h)),
                  pl.BlockSpec((1, DH), lambda h: (0, 0))],
        out_specs=pl.BlockSpec((n, DH), lambda h: (0, h)),
        out_shape=jax.ShapeDtypeStruct((n, 2 * GW), BF16),
        compiler_params=_params(("parallel",)),
    )(o, proj, wg)


def _gdn_out_bwd(o, proj, wg, dmix):
    n = o.shape[0]

    def body(o_ref, z_ref, w_ref, d_ref, do_ref, dz_ref, gw_ref):
        ov, z, d, w = o_ref[...], z_ref[...], d_ref[...], w_ref[...]
        r = lax.rsqrt(jnp.mean(ov * ov, axis=-1, keepdims=True) + EPS)
        nrm = ov * r
        s = _sigmoid(z)
        dz_ref[...] = (d * (nrm * w) * _dsilu(z, s)).astype(BF16)
        dn_w = d * (z * s)
        gw = jnp.sum(dn_w * nrm, axis=0, keepdims=True)
        dn = dn_w * w
        do_ref[...] = r * (dn - nrm * jnp.mean(dn * nrm, axis=-1, keepdims=True))

        @pl.when(pl.program_id(0) == 0)
        def _():
            gw_ref[...] = jnp.zeros_like(gw_ref)

        gw_ref[...] += gw

    return pl.pallas_call(
        body, name="gdn_out_bwd", grid=(HEADS,),
        in_specs=[pl.BlockSpec((n, DH), lambda h: (0, h)), pl.BlockSpec((n, DH), lambda h: (0, ZB + h)),
                  pl.BlockSpec((1, DH), lambda h: (0, 0)), pl.BlockSpec((n, DH), lambda h: (0, h))],
        out_specs=[pl.BlockSpec((n, DH), lambda h: (0, h)), pl.BlockSpec((n, DH), lambda h: (0, ZB + h)),
                   pl.BlockSpec((1, DH), lambda h: (0, 0))],
        out_shape=[jax.ShapeDtypeStruct((n, GW), F32), jax.ShapeDtypeStruct((n, GW_COLS), BF16),
                   jax.ShapeDtypeStruct((1, DH), F32)],
        compiler_params=_params(("arbitrary",)),
    )(o, proj, wg, dmix)


def _conv_branch(proj, w3, b, mix):
    n = proj.shape[0]

    def body(p4, w_ref, b_ref, _, y_ref):
        u = p4[:, DH:2 * DH] * p4[:, 2 * DH:3 * DH]
        cc = _conv_silu(u, w_ref, 3) + b_ref[...]
        z = p4[:, 3 * DH:4 * DH]
        y_ref[...] = (p4[:, 0:DH] * cc * (z * _sigmoid(z))).astype(BF16)

    return pl.pallas_call(
        body, name="conv_branch", grid=(HEADS,),
        in_specs=[pl.BlockSpec((n, 4 * DH), lambda h: (0, h)), pl.BlockSpec((3, DH), lambda h: (0, h)),
                  pl.BlockSpec((1, DH), lambda h: (0, h)), ANY],
        out_specs=pl.BlockSpec((n, DH), lambda h: (0, HEADS + h)),
        out_shape=jax.ShapeDtypeStruct(mix.shape, BF16),
        input_output_aliases={3: 0},
        compiler_params=_params(("parallel",), 40 * 2**20),
    )(*_in_hbm(proj, w3, b, mix))


def _conv_branch_bwd(proj, w3, b, dmix):
    n = proj.shape[0]

    def body(p4, w_ref, b_ref, d_ref, o4, gw_ref, gbias_ref):
        gb, gcv, hc, z = p4[:, 0:DH], p4[:, DH:2 * DH], p4[:, 2 * DH:3 * DH], p4[:, 3 * DH:4 * DH]
        d = d_ref[...]
        dgb, dgc, dhc, dzc = (o4.at[:, kk * DH:(kk + 1) * DH] for kk in range(4))
        u = gcv * hc
        cc = _conv_silu(u, w_ref, 3) + b_ref[...]
        s = _sigmoid(z)
        dzc[...] = (d * (gb * cc) * _dsilu(z, s)).astype(BF16)
        dp = d * (z * s)
        dgb[...] = (dp * cc).astype(BF16)
        dcc = dp * gb
        gbias_ref[...] = jnp.sum(dcc, axis=0, keepdims=True)
        du = None
        for j in range(3):
            gw_ref[j:j + 1, :] = jnp.sum(dcc * _shift_down(u, 2 - j), axis=0, keepdims=True)
            t = _shift_up(dcc, 2 - j) * w_ref[j:j + 1, :]
            du = t if du is None else du + t
        dgc[...] = (du * hc).astype(BF16)
        dhc[...] = (du * gcv).astype(BF16)

    p4spec = pl.BlockSpec((n, 4 * DH), lambda h: (0, h))
    return pl.pallas_call(
        body, name="conv_branch_bwd", grid=(HEADS,),
        in_specs=[p4spec, pl.BlockSpec((3, DH), lambda h: (0, h)), pl.BlockSpec((1, DH), lambda h: (0, h)),
                  pl.BlockSpec((n, DH), lambda h: (0, HEADS + h))],
        out_specs=[p4spec, pl.BlockSpec((3, DH), lambda h: (0, h)), pl.BlockSpec((1, DH), lambda h: (0, h))],
        out_shape=[jax.ShapeDtypeStruct((n, CW_COLS), BF16), jax.ShapeDtypeStruct((3, GW), F32),
                   jax.ShapeDtypeStruct((1, GW), F32)],
        compiler_params=_params(("parallel",), 48 * 2**20),
    )(proj, w3, b, dmix)


def _final_loss(out, tgt, wf):
    n, d = out.shape
    tr = min(256, n)

    def body(o_ref, t_ref, w_ref, do_ref, dob_ref, gw_ref, loss_ref):
        ov, w = o_ref[...], w_ref[...]
        r = lax.rsqrt(jnp.mean(ov * ov, axis=-1, keepdims=True) + EPS)
        nrm = ov * r
        e = nrm * w - t_ref[...]
        dy = e * (1.0 / d)
        dn = dy * w
        dout = r * (dn - nrm * jnp.mean(dn * nrm, axis=-1, keepdims=True))
        do_ref[...] = dout
        dob_ref[...] = dout.astype(BF16)

        @pl.when(pl.program_id(0) == 0)
        def _():
            gw_ref[...] = jnp.zeros_like(gw_ref)
            loss_ref[...] = jnp.zeros_like(loss_ref)

        gw_ref[...] += jnp.sum(dy * nrm, axis=0, keepdims=True)
        loss_ref[...] += (0.5 / d) * jnp.sum(jnp.sum(e * e, axis=-1, keepdims=True), axis=0, keepdims=True)

    row = pl.BlockSpec((tr, d), lambda i: (i, 0))
    return pl.pallas_call(
        body, name="final_loss", grid=(n // tr,),
        in_specs=[row, row, pl.BlockSpec((1, d), lambda i: (0, 0))],
        out_specs=[row, row, pl.BlockSpec((1, d), lambda i: (0, 0)), pl.BlockSpec((1, 1), lambda i: (0, 0))],
        out_shape=[jax.ShapeDtypeStruct((n, d), F32), jax.ShapeDtypeStruct((n, d), BF16),
                   jax.ShapeDtypeStruct((1, d), F32), jax.ShapeDtypeStruct((1, 1), F32)],
        compiler_params=_params(("arbitrary",)),
    )(*_in_hbm(out, tgt, wf))


def _rms_in_bwd(x, w, dh, dout):
    n, d = x.shape
    tr = min(256, n)

    def body(x_ref, w_ref, dh_ref, do_ref, dx_ref, gw_ref):
        xv, dhv = x_ref[...], dh_ref[...]
        r = lax.rsqrt(jnp.mean(xv * xv, axis=-1, keepdims=True) + EPS)
        xn = xv * r
        dxn = dhv * w_ref[...]
        dx_ref[...] = r * (dxn - xn * jnp.mean(dxn * xn, axis=-1, keepdims=True)) + do_ref[...]

        @pl.when(pl.program_id(0) == 0)
        def _():
            gw_ref[...] = jnp.zeros_like(gw_ref)

        gw_ref[...] += jnp.sum(dhv * xn, axis=0, keepdims=True)

    row = pl.BlockSpec((tr, d), lambda i: (i, 0))
    one = pl.BlockSpec((1, d), lambda i: (0, 0))
    return pl.pallas_call(
        body, name="rms_in_bwd", grid=(n // tr,),
        in_specs=[row, one, row, row], out_specs=[row, one],
        out_shape=[jax.ShapeDtypeStruct((n, d), F32), jax.ShapeDtypeStruct((1, d), F32)],
        compiler_params=_params(("arbitrary",)),
    )(*_in_hbm(x, w, dh, dout))


def _ij():
    i = lax.broadcasted_iota(jnp.int32, (CH, CH), 0)
    j = lax.broadcasted_iota(jnp.int32, (CH, CH), 1)
    return i, j


def _unit_lower_inverse(mats):
    i, j = _ij()
    eye = jnp.where(i == j, 1.0, 0.0)
    same16 = (i // 16) == (j // 16)
    same32 = (i // 32) == (j // 32)
    mm = lambda xs, ys: [_dot(x, y, NN, P_INV) for x, y in zip(xs, ys)]
    n1 = [jnp.where(same16, -a, 0.0) for a in mats]
    n2 = mm(n1, n1)
    n4 = mm(n2, n2)
    n8 = mm(n4, n4)
    t = [eye + x1 + x2 + x3 for x1, x2, x3 in zip(n1, n2, mm(n1, n2))]
    t = [x + y for x, y in zip(t, mm(t, n4))]
    t = [x + y for x, y in zip(t, mm(t, n8))]
    a1 = [jnp.where(same32 & jnp.logical_not(same16), a, 0.0) for a in mats]
    t = [x - y for x, y in zip(t, mm(t, mm(a1, t)))]
    a2 = [jnp.where(same32, 0.0, a) for a in mats]
    t = [x - y for x, y in zip(t, mm(t, mm(a2, t)))]
    return t


def _head_vectors(bg, bgt, h):
    bcol = bg[:, h:h + 1]
    gcol = bg[:, HEADS + h:HEADS + h + 1]
    grow = bgt[HEADS + h:HEADS + h + 1, :]
    return bcol, gcol, grow


def _decay(gcol, grow):
    i, j = _ij()
    return jnp.where(i >= j, jnp.exp(jnp.where(i >= j, gcol - grow, 0.0)), 0.0)


def _gdn_intra(q, k, v, bg, bgt):
    n = q.shape[0]
    nch = n // CH
    cps = 4 if nch % 4 == 0 else 1

    def body(q_ref, k_ref, v_ref, bg_ref, bgt_ref, u_ref, w_ref, p_ref, t_ref):
        i, j = _ij()
        items = [(ci, h) for ci in range(cps) for h in range(HEADS)]
        at = lambda ref, ci, h: ref.at[ci * CH:(ci + 1) * CH, h * DH:(h + 1) * DH]
        bgs = [bg_ref[ci * CH:(ci + 1) * CH, :] for ci in range(cps)]
        ks = [at(k_ref, ci, h)[...] for ci, h in items]
        vecs = [_head_vectors(bgs[ci], bgt_ref[ci], h) for ci, h in items]
        decs = [_decay(gcol, grow) for _, gcol, grow in vecs]
        kks = [_dot(kh, kh, NT, P_GRAM) for kh in ks]
        qks = [_dot(at(q_ref, ci, h)[...], kh, NT, P_GRAM) for (ci, h), kh in zip(items, ks)]
        ts = _unit_lower_inverse([jnp.where(i > j, bcol * kk * dec, 0.0)
                                  for (bcol, _, _), kk, dec in zip(vecs, kks, decs)])
        us = [_dot(t, at(v_ref, ci, h)[...] * bcol, NN, P_SOL) for t, (ci, h), (bcol, _, _) in zip(ts, items, vecs)]
        ws = [_dot(t, kh * (bcol * jnp.exp(gcol)), NN, P_SOL) for t, kh, (bcol, gcol, _) in zip(ts, ks, vecs)]
        for n_, (ci, h) in enumerate(items):
            p_ref[ci, h] = qks[n_] * decs[n_]
            t_ref[ci, h] = ts[n_]
            at(u_ref, ci, h)[...] = us[n_]
            at(w_ref, ci, h)[...] = ws[n_]

    row = pl.BlockSpec((cps * CH, GW), lambda c: (c, 0))
    sq = pl.BlockSpec((cps, HEADS, CH, CH), lambda c: (c, 0, 0, 0))
    big = jax.ShapeDtypeStruct((n, GW), F32)
    sqs = jax.ShapeDtypeStruct((nch, HEADS, CH, CH), F32)
    return pl.pallas_call(
        body, name="gdn_intra", grid=(nch // cps,),
        in_specs=[row, row, row, pl.BlockSpec((cps * CH, DH), lambda c: (c, 0)),
                  pl.BlockSpec((cps, DH, CH), lambda c: (c, 0, 0))],
        out_specs=[row, row, sq, sq], out_shape=[big, big, sqs, sqs],
        compiler_params=_params(("parallel",)),
    )(q, k, v, bg, bgt)


def _gdn_scan(q, k, bg, u, w, p):
    n = q.shape[0]
    nch = n // CH
    cps = SCAN_CPS if nch % SCAN_CPS == 0 else 1

    def body(q_ref, k_ref, bg_ref, u_ref, w_ref, p_ref, o_ref, vn_ref, s_out, s_scr):
        @pl.when(pl.program_id(0) == 0)
        def _():
            s_scr[...] = jnp.zeros_like(s_scr)

        hs = range(HEADS)
        sls = [slice(h * DH, (h + 1) * DH) for h in hs]
        ss = [s_scr[h] for h in hs]
        for ci in range(cps):
            rs = slice(ci * CH, (ci + 1) * CH)
            bg = bg_ref[rs, :]
            gcols = [bg[:, HEADS + h:HEADS + h + 1] for h in hs]
            glasts = [g[CH - 1:CH, :] for g in gcols]
            wss = [_dot(w_ref[rs, sl], s, NN, P_SCAN) for sl, s in zip(sls, ss)]
            oqs = [_dot(q_ref[rs, sl] * jnp.exp(g), s, NN, P_SCAN) for sl, s, g in zip(sls, ss, gcols)]
            vns = [u_ref[rs, sl] - x for sl, x in zip(sls, wss)]
            ops = [_dot(p_ref[ci, h], vn, NN, P_SCAN) for h, vn in zip(hs, vns)]
            sns = [_dot(k_ref[rs, sl] * jnp.exp(gl - g), vn, TN, P_SCAN)
                   for sl, gl, g, vn in zip(sls, glasts, gcols, vns)]
            for h, sl in enumerate(sls):
                s_out[ci, :, sl] = ss[h]
                vn_ref[rs, sl] = vns[h]
                o_ref[rs, sl] = oqs[h] + ops[h]
            ss = [s * jnp.exp(gl) + sn for s, gl, sn in zip(ss, glasts, sns)]
        for h in hs:
            s_scr[h] = ss[h]

    row = pl.BlockSpec((cps * CH, GW), lambda c: (c, 0))
    big = jax.ShapeDtypeStruct((n, GW), F32)
    return pl.pallas_call(
        body, name="gdn_scan", grid=(nch // cps,),
        in_specs=[row, row, pl.BlockSpec((cps * CH, DH), lambda c: (c, 0)), row, row,
                  pl.BlockSpec((cps, HEADS, CH, CH), lambda c: (c, 0, 0, 0))],
        out_specs=[row, row, pl.BlockSpec((cps, DH, GW), lambda c: (c, 0, 0))],
        out_shape=[big, big, jax.ShapeDtypeStruct((nch, DH, GW), F32)],
        scratch_shapes=[pltpu.VMEM((HEADS, DH, DH), F32)],
        compiler_params=_params(("arbitrary",)),
    )(q, k, bg, u, w, p)


def _gdn_scan_bwd(q, k, bg, w, p, vn, s_in, do):
    n = q.shape[0]
    nch = n // CH
    cps = SCAN_CPS if nch % SCAN_CPS == 0 else 1
    rev = lambda c: nch // cps - 1 - c

    def body(q_ref, k_ref, bg_ref, w_ref, p_ref, vn_ref, s_ref, do_ref,
             dqg_ref, dp_ref, du_ref, dw_ref, dks_ref, dgam_ref, ds_scr):
        @pl.when(pl.program_id(0) == 0)
        def _():
            ds_scr[...] = jnp.zeros_like(ds_scr)

        lane = _lane((1, DH))
        hs = range(HEADS)
        sls = [slice(h * DH, (h + 1) * DH) for h in hs]
        dss = [ds_scr[h] for h in hs]
        for ci in reversed(range(cps)):
            rs = slice(ci * CH, (ci + 1) * CH)
            bg = bg_ref[rs, :]
            gcols = [bg[:, HEADS + h:HEADS + h + 1] for h in hs]
            glasts = [g[CH - 1:CH, :] for g in gcols]
            ss = [s_ref[ci, :, sl] for sl in sls]
            dos = [do_ref[rs, sl] for sl in sls]
            vnl = [vn_ref[rs, sl] for sl in sls]
            dqgs = [_dot(d, s, NT, P_SCANB) for d, s in zip(dos, ss)]
            dps = [_dot(d, vn, NT, P_SCANB) for d, vn in zip(dos, vnl)]
            dvn1 = [_dot(p_ref[ci, h], d, TN, P_SCANB) for h, d in zip(hs, dos)]
            dvn2 = [_dot(k_ref[rs, sl] * jnp.exp(gl - g), ds, NN, P_SCANB)
                    for sl, gl, g, ds in zip(sls, glasts, gcols, dss)]
            dkss = [_dot(vn, ds, NT, P_SCANB) for vn, ds in zip(vnl, dss)]
            dsq = [_dot(q_ref[rs, sl] * jnp.exp(g), d, TN, P_SCANB) for sl, g, d in zip(sls, gcols, dos)]
            dvns = [a + b for a, b in zip(dvn1, dvn2)]
            dws = [_dot(dvn, s, NT, P_SCANB) for dvn, s in zip(dvns, ss)]
            dsw = [_dot(w_ref[rs, sl], dvn, TN, P_SCANB) for sl, dvn in zip(sls, dvns)]
            dgam = jnp.zeros((1, DH), F32)
            for h, sl in enumerate(sls):
                dqg_ref[rs, sl] = dqgs[h]
                dp_ref[ci, h] = dps[h]
                du_ref[rs, sl] = dvns[h]
                dw_ref[rs, sl] = -dws[h]
                dks_ref[rs, sl] = dkss[h]
                tot = jnp.sum(jnp.sum(dss[h] * ss[h], axis=-1, keepdims=True), axis=0, keepdims=True)
                dgam = dgam + jnp.where(lane == h, tot, 0.0)
            dgam_ref[ci] = jnp.broadcast_to(dgam, (8, DH))
            dss = [ds * jnp.exp(gl) + a - b for ds, gl, a, b in zip(dss, glasts, dsq, dsw)]
        for h in hs:
            ds_scr[h] = dss[h]

    row = pl.BlockSpec((cps * CH, GW), lambda c: (rev(c), 0))
    sq = pl.BlockSpec((cps, HEADS, CH, CH), lambda c: (rev(c), 0, 0, 0))
    big = jax.ShapeDtypeStruct((n, GW), F32)
    return pl.pallas_call(
        body, name="gdn_scan_bwd", grid=(nch // cps,),
        in_specs=[row, row, pl.BlockSpec((cps * CH, DH), lambda c: (rev(c), 0)), row, sq, row,
                  pl.BlockSpec((cps, DH, GW), lambda c: (rev(c), 0, 0)), row],
        out_specs=[row, sq, row, row, row, pl.BlockSpec((cps, 8, DH), lambda c: (rev(c), 0, 0))],
        out_shape=[big, jax.ShapeDtypeStruct((nch, HEADS, CH, CH), F32), big, big, big,
                   jax.ShapeDtypeStruct((nch, 8, DH), F32)],
        scratch_shapes=[pltpu.VMEM((HEADS, DH, DH), F32)],
        compiler_params=_params(("arbitrary",)),
    )(q, k, bg, w, p, vn, s_in, do)


def _gdn_intra_bwd(q, k, v, bg, bgt, t, u, w, p, dqg, dp, du, dw, dks, dgam):
    n = q.shape[0]
    nch = n // CH
    cps = 1

    def body(q_ref, k_ref, v_ref, bg_ref, bgt_ref, t_ref, u_ref, w_ref, p_ref,
             dqg_ref, dp_ref, du_ref, dw_ref, dks_ref, dgam_ref, dq_ref, dk_ref, dv_ref, dbg_ref):
        i, j = _ij()
        rows1 = lax.broadcasted_iota(jnp.int32, (CH, 1), 0)
        lane = _lane((CH, DH))
        rsum = lambda x: jnp.sum(x, axis=-1, keepdims=True)
        items = [(ci, h) for ci in range(cps) for h in range(HEADS)]
        at = lambda ref, it: ref.at[it[0] * CH:(it[0] + 1) * CH, it[1] * DH:(it[1] + 1) * DH]
        ld = lambda ref: [at(ref, it)[...] for it in items]
        bgs = [bg_ref[ci * CH:(ci + 1) * CH, :] for ci in range(cps)]
        qs, ks = ld(q_ref), ld(k_ref)
        vecs = [_head_vectors(bgs[ci], bgt_ref[ci], h) for ci, h in items]
        decs = [_decay(gcol, grow) for _, gcol, grow in vecs]
        ths = [t_ref[ci, h] for ci, h in items]
        drus = [_dot(th, x_, TN, P_BWD) for th, x_ in zip(ths, ld(du_ref))]
        drws = [_dot(th, x_, TN, P_BWD) for th, x_ in zip(ths, ld(dw_ref))]
        kks = [_dot(kh, kh, NT, P_GRAM) for kh in ks]
        da1 = [_dot(dru, x_, NT, P_BWD) for dru, x_ in zip(drus, ld(u_ref))]
        da2 = [_dot(drw, x_, NT, P_BWD) for drw, x_ in zip(drws, ld(w_ref))]
        das = [jnp.where(i > j, -(x_ + y_), 0.0) for x_, y_ in zip(da1, da2)]
        dkks = [da * bcol * dec for da, (bcol, _, _), dec in zip(das, vecs, decs)]
        dps = [dp_ref[ci, h] for ci, h in items]
        dqks = [dp_ * dec for dp_, dec in zip(dps, decs)]
        dq_ps = [_dot(dqk, kh, NN, P_BWD) for dqk, kh in zip(dqks, ks)]
        dk_ps = [_dot(dqk, qh, TN, P_BWD) for dqk, qh in zip(dqks, qs)]
        dk_as = [_dot(dkk, kh, NN, P_BWD) for dkk, kh in zip(dkks, ks)]
        dk_bs = [_dot(dkk, kh, TN, P_BWD) for dkk, kh in zip(dkks, ks)]
        bcols = [vc[0] for vc in vecs]
        gcols = [vc[1] for vc in vecs]
        gams = [jnp.exp(g) for g in gcols]
        glasts = [g[CH - 1:CH, :] for g in gcols]
        es = [jnp.exp(gl - g) for gl, g in zip(glasts, gcols)]
        kgs = [kh * gam for kh, gam in zip(ks, gams)]
        dqgs, dkss = ld(dqg_ref), ld(dks_ref)
        r_uv = [rsum(dru * x_) for dru, x_ in zip(drus, ld(v_ref))]
        r_wk = [rsum(drw * kg) for drw, kg in zip(drws, kgs)]
        r_ak = [rsum(da * kk * dec) for da, kk, dec in zip(das, kks, decs)]
        r_qq = [rsum(dqg * qh) for dqg, qh in zip(dqgs, qs)]
        tks = [rsum(dk_ * kh) * e for dk_, kh, e in zip(dkss, ks, es)]
        mdecs = [da * (bcol * kk * dec) + dp_ * p_ref[ci, h]
                 for (ci, h), da, bcol, kk, dec, dp_ in zip(items, das, bcols, kks, decs, dps)]
        r_md = [rsum(m) for m in mdecs]
        c_md = [rsum(jnp.where(i == j, jnp.sum(m, axis=0, keepdims=True), 0.0)) for m in mdecs]
        dbgs = [jnp.zeros((CH, DH), F32) for _ in range(cps)]
        for n_, (ci, h) in enumerate(items):
            at(dv_ref, (ci, h))[...] = bcols[n_] * drus[n_]
            at(dq_ref, (ci, h))[...] = gams[n_] * dqgs[n_] + dq_ps[n_]
            at(dk_ref, (ci, h))[...] = ((bcols[n_] * gams[n_]) * drws[n_] + dk_ps[n_] + dk_as[n_] + dk_bs[n_]
                                        + dkss[n_] * es[n_])
            dbeta = r_uv[n_] + r_wk[n_] + r_ak[n_]
            dglast = (jnp.sum(tks[n_], axis=0, keepdims=True)
                      + dgam_ref[ci, 0:1, h:h + 1] * jnp.exp(glasts[n_]))
            dgc = (r_wk[n_] * bcols[n_] + r_md[n_] - c_md[n_] + r_qq[n_] * gams[n_] - tks[n_]
                   + jnp.where(rows1 == CH - 1, dglast, 0.0))
            dbgs[ci] = dbgs[ci] + jnp.where(lane == h, dbeta, 0.0) + jnp.where(lane == HEADS + h, dgc, 0.0)
        for ci in range(cps):
            dbg_ref[ci * CH:(ci + 1) * CH, :] = dbgs[ci]

    row = pl.BlockSpec((cps * CH, GW), lambda c: (c, 0))
    sq = pl.BlockSpec((cps, HEADS, CH, CH), lambda c: (c, 0, 0, 0))
    small = pl.BlockSpec((cps * CH, DH), lambda c: (c, 0))
    big = jax.ShapeDtypeStruct((n, GW), F32)
    return pl.pallas_call(
        body, name="gdn_intra_bwd", grid=(nch // cps,),
        in_specs=[row, row, row, small, pl.BlockSpec((cps, DH, CH), lambda c: (c, 0, 0)), sq, row, row, sq,
                  row, sq, row, row, row, pl.BlockSpec((cps, 8, DH), lambda c: (c, 0, 0))],
        out_specs=[row, row, row, small],
        out_shape=[big, big, big, jax.ShapeDtypeStruct((n, DH), F32)],
        compiler_params=_params(("parallel",)),
    )(q, k, v, bg, bgt, t, u, w, p, dqg, dp, du, dw, dks, dgam)


def _local_step(x, tgt, h, w_g, cqw, late, norm_in_w, ad, gdn_norm_w, conv_b, final_norm_w,
                on_grad_c=None, on_grad_g=None, on_q=None):
    proj_g = _matmul(h, w_g, NT, F32, 512, 1408, 1024, "mm_proj_g", n=GW_COLS, b_outer=True)
    q, k, v = _prep_qkv(proj_g, cqw)
    if on_q is not None:
        q = on_q(q)
    bg, bgt = _prep_bg(proj_g, ad)
    u, w, p, t = _gdn_intra(q, k, v, bg, bgt)
    o, vn, s_in = _gdn_scan(q, k, bg, u, w, p)
    w_c, w_out, conv_w = late(o)
    proj_c = _matmul(h, w_c, NT, F32, 512, 1024, 1024, "mm_proj_c", n=CW_COLS, b_outer=True)
    mix = _conv_branch(proj_c, conv_w, conv_b, _gdn_out(o, proj_g, gdn_norm_w))
    out = _matmul(mix, w_out, NN, F32, 512, 512, 2048, "mm_out", add=x)
    dout, dout_b, g_fn, loss = _final_loss(out, tgt, final_norm_w)

    dmix = _matmul(dout_b, w_out, NT, F32, 512, 1024, 1024, "mm_dmix", b_outer=True)
    g_wout = _matmul(mix, dout_b, TN, BF16, 512, 512, 2048, "mm_gwout")
    do, dproj_g, g_gn = _gdn_out_bwd(o, proj_g, gdn_norm_w, dmix)
    dproj_c, g_cw, g_cb = _conv_branch_bwd(proj_c, conv_w, conv_b, dmix)
    g_c = _matmul(dproj_c, h, TN, BF16, 1024, 512, 2048, "mm_gwin_c")
    if on_grad_c is not None:
        do = on_grad_c(g_c, g_wout, do)
    dqg, dp, du, dw, dks, dgam = _gdn_scan_bwd(q, k, bg, w, p, vn, s_in, do)
    dq, dk, dv, dbg = _gdn_intra_bwd(q, k, v, bg, bgt, t, u, w, p, dqg, dp, du, dw, dks, dgam)
    dproj_g, gq, gk, gv = _prep_qkv_bwd(proj_g, cqw, dq, dk, dv, dproj_g)
    dproj_g, g_al, g_dt = _prep_bg_bwd(proj_g, ad, dbg, dproj_g)
    g_g = _matmul(dproj_g, h, TN, BF16, 1408, 512, 2048, "mm_gwin_g")
    if on_grad_g is not None:
        dproj_g = on_grad_g(g_g, dproj_g)
    dh = _matmul(dproj_g, w_g, NN, F32, 1024, 1024, 1408, "mm_dh_g")
    dh = _matmul(dproj_c, w_c, NN, F32, 1024, 1024, 1024, "mm_dh_c", add=dh)
    gx, g_nin = _rms_in_bwd(x, norm_in_w, dh, dout)
    small = dict(nin=g_nin, cb=g_cb, fn=g_fn, al=g_al, dt=g_dt, gn=g_gn, cq=(gq, gk, gv), cw=g_cw, loss=loss)
    return gx, small, (g_g, g_c, g_wout)


def _place():
    x, y, c = lax.axis_index("x"), lax.axis_index("y"), lax.axis_index("c")
    chips = [(1 - x, y), (x, 1 - y), (1 - x, 1 - y)]
    return x, y, c, chips


def _blk(ref, b):
    if isinstance(b, int):
        return ref.at[b * DH:(b + 1) * DH, :]
    return ref.at[pl.ds(pl.multiple_of(b * DH, DH), DH), :]


HBM = pl.BlockSpec(memory_space=pltpu.HBM)
SEM = pl.BlockSpec(memory_space=pltpu.SEMAPHORE)
EFFECT = pltpu.SideEffectType.DATAFLOW_SIDE_EFFECTING


def _split_start(name, issue, bufs, n_sems):
    nbuf = len(bufs)

    def body(*refs):
        issue(refs[:nbuf], refs[nbuf], refs[nbuf + 1])
        refs[-1][...] = jnp.zeros_like(refs[-1])

    out = pl.pallas_call(
        body, name=name,
        out_shape=(pltpu.SemaphoreType.DMA((n_sems,)), pltpu.SemaphoreType.DMA((n_sems,)),
                   *[pltpu.HBM(b.shape, b.dtype) for b in bufs], jax.ShapeDtypeStruct((8, DH), F32)),
        in_specs=[HBM] * nbuf,
        out_specs=(SEM, SEM, *[HBM] * nbuf, pl.BlockSpec(memory_space=pltpu.VMEM)),
        input_output_aliases={a: 2 + a for a in range(nbuf)},
        compiler_params=pltpu.CompilerParams(has_side_effects=EFFECT),
    )(*[pltpu.with_memory_space_constraint(b, pltpu.HBM) for b in bufs])
    return out[0], out[1], list(out[2:2 + nbuf]), out[-1]


def _split_wait(name, await_, send_sems, recv_sems, bufs, after):
    nbuf = len(bufs)

    def body(*refs):
        await_(refs[:nbuf], refs[nbuf], refs[nbuf + 1])

    out = pl.pallas_call(
        body, name=name,
        out_shape=tuple(pltpu.HBM(b.shape, b.dtype) for b in bufs),
        in_specs=[HBM] * nbuf + [SEM, SEM, ANY], out_specs=tuple([HBM] * nbuf),
        input_output_aliases={a: a for a in range(nbuf)},
        compiler_params=pltpu.CompilerParams(has_side_effects=EFFECT),
    )(*bufs, send_sems, recv_sems, after)
    return list(out)


def _phase_blocks(chip, phase, edges, parity=None):
    return [(b, blk) for b, (grp, blk) in enumerate(_shard_blocks(chip, edges))
            if grp == phase and (parity is None or b % 2 == parity)]


def _cols(ref, nblk):
    return ref.at[0:nblk * DH, :]


def _block_table(chip, edges, spare_g, spare_c):
    rows = []
    for s in range(4):
        sb = _shard_blocks(s, edges)
        rows.append([[blk if grp == "g" else spare_g for grp, blk in sb],
                     [blk if grp == "c" else spare_c for grp, blk in sb],
                     [int(grp == "g") for grp, _ in sb], [s] * ALIGNED_BLOCKS])
    return jnp.asarray(rows, jnp.int32)[chip]


def _place_own(a_shard, wo, cq, cw, bufs):
    d = a_shard.shape[1]
    chip = 2 * lax.axis_index("x") + lax.axis_index("y")

    def body(t_ref, a_ref, wo_ref, cq_ref, cw_ref, *refs):
        wg_ref, wc_ref, wog_ref, cqg_ref, cwg_ref = refs[5:]
        wg_ref[...] = a_ref[...]
        wc_ref[...] = a_ref[...]

        @pl.when(pl.program_id(0) == 0)
        def _():
            wog_ref[0] = wo_ref[...]
            cqg_ref[0] = cq_ref[...]
            cwg_ref[0] = cw_ref[...]

    whole = lambda s: pl.BlockSpec(s.shape, lambda b, t: (0,) * s.ndim)
    slot = lambda s: pl.BlockSpec((1,) + s.shape, lambda b, t: (t[3, 0],) + (0,) * s.ndim)
    return pl.pallas_call(
        body, name="place_own",
        grid_spec=pltpu.PrefetchScalarGridSpec(
            num_scalar_prefetch=1, grid=(ALIGNED_BLOCKS,),
            in_specs=[pl.BlockSpec((DH, d), lambda b, t: (b, 0)), whole(wo), whole(cq), whole(cw)] + [ANY] * 5,
            out_specs=[pl.BlockSpec((DH, d), lambda b, t: (t[0, b], 0)),
                       pl.BlockSpec((DH, d), lambda b, t: (t[1, b], 0)), slot(wo), slot(cq), slot(cw)]),
        out_shape=[jax.ShapeDtypeStruct(b.shape, b.dtype) for b in bufs],
        input_output_aliases={5 + a: a for a in range(5)},
        compiler_params=_params(("arbitrary",)),
    )(_block_table(chip, True, G_SPARE, C_SPARE), a_shard, wo, cq, cw, *bufs)


def _tie(x, token, name):
    def body(x_ref, t_ref, o_ref):
        del x_ref, t_ref, o_ref

    return pl.pallas_call(
        body, name=name, in_specs=[ANY, ANY], out_specs=ANY,
        out_shape=jax.ShapeDtypeStruct(x.shape, x.dtype), input_output_aliases={0: 0},
    )(x, token)


def _gather_start(phase, a_shard, w_grp, singles):
    ns = len(singles)

    def issue(refs, send_sems, recv_sems):
        a_ref, w_ref = refs[0], refs[1]
        x, y, c, chips = _place()
        mine = 2 * x + y
        for jj, (px, py) in enumerate(chips):
            to = dict(device_id=(px, py, c), device_id_type=MESH)
            for a in range(ns):
                pltpu.make_async_remote_copy(
                    src_ref=refs[2 + 2 * a], dst_ref=refs[3 + 2 * a].at[mine],
                    send_sem=send_sems.at[(1 + ns) * jj + 1 + a], recv_sem=recv_sems.at[(1 + ns) * jj + 1 + a],
                    **to).start()
        for s in range(4):
            for par in range(2):
                blocks = _phase_blocks(s, phase, True, par)
                if blocks:
                    @pl.when((mine == s) & (c == par))
                    def _():
                        for jj, (px, py) in enumerate(chips):
                            for b, blk in blocks:
                                pltpu.make_async_remote_copy(
                                    src_ref=_blk(a_ref, b), dst_ref=_blk(w_ref, blk),
                                    send_sem=send_sems.at[(1 + ns) * jj], recv_sem=recv_sems.at[(1 + ns) * jj],
                                    device_id=(px, py, c), device_id_type=MESH).start()

    bufs = [a_shard, w_grp] + [t for pair in singles for t in pair]
    return _split_start("gather_start_" + phase, issue, bufs, 3 * (1 + ns))


def _gather_wait(phase, send_sems, recv_sems, bufs, after):
    ns = (len(bufs) - 2) // 2

    def await_(refs, send_sems, recv_sems):
        a_ref, w_ref = refs[0], refs[1]
        x, y, c, chips = _place()
        mine = 2 * x + y
        for jj, (px, py) in enumerate(chips):
            to = dict(device_id=(px, py, c), device_id_type=MESH)
            peer = 2 * px + py
            for a in range(ns):
                cp = pltpu.make_async_remote_copy(
                    src_ref=refs[2 + 2 * a], dst_ref=refs[3 + 2 * a].at[mine],
                    send_sem=send_sems.at[(1 + ns) * jj + 1 + a], recv_sem=recv_sems.at[(1 + ns) * jj + 1 + a], **to)
                cp.wait_recv()
                cp.wait_send()
            for s in range(4):
                for par in range(2):
                    nblk = len(_phase_blocks(s, phase, True, par))
                    if nblk:
                        both = pltpu.make_async_remote_copy(
                            src_ref=_cols(a_ref, nblk), dst_ref=_cols(w_ref, nblk),
                            send_sem=send_sems.at[(1 + ns) * jj], recv_sem=recv_sems.at[(1 + ns) * jj], **to)

                        @pl.when((peer == s) & (c == par))
                        def _():
                            both.wait_recv()

                        @pl.when((mine == s) & (c == par))
                        def _():
                            both.wait_send()

    return _split_wait("gather_wait_" + phase, await_, send_sems, recv_sems, bufs, after)


def _sibling_forward_parts(phase):
    def each(w_ref, send_sems, recv_sems, start):
        x, y, c, chips = _place()
        to = dict(device_id=(x, y, 1 - c), device_id_type=MESH)
        for jj, (px, py) in enumerate(chips):
            peer = 2 * px + py
            for s in range(4):
                for par in range(2):
                    mine_blocks = _phase_blocks(s, phase, True, par)
                    theirs = len(_phase_blocks(s, phase, True, 1 - par))
                    if not (mine_blocks or theirs):
                        continue

                    @pl.when((peer == s) & (c == par))
                    def _():
                        if start:
                            for _, blk in mine_blocks:
                                pltpu.make_async_remote_copy(
                                    src_ref=_blk(w_ref, blk), dst_ref=_blk(w_ref, blk),
                                    send_sem=send_sems.at[jj], recv_sem=recv_sems.at[jj], **to).start()
                            return
                        if theirs:
                            pltpu.make_async_remote_copy(
                                src_ref=_cols(w_ref, theirs), dst_ref=_cols(w_ref, theirs),
                                send_sem=send_sems.at[jj], recv_sem=recv_sems.at[jj], **to).wait_recv()
                        if mine_blocks:
                            pltpu.make_async_remote_copy(
                                src_ref=_cols(w_ref, len(mine_blocks)), dst_ref=_cols(w_ref, len(mine_blocks)),
                                send_sem=send_sems.at[jj], recv_sem=recv_sems.at[jj], **to).wait_send()

    issue = lambda refs, send_sems, recv_sems: each(refs[0], send_sems, recv_sems, True)
    await_ = lambda refs, send_sems, recv_sems: each(refs[0], send_sems, recv_sems, False)
    return issue, await_


def _sibling_forward(phase, w_grp):
    issue, await_ = _sibling_forward_parts(phase)

    def body(w_in_ref, w_ref, send_sems, recv_sems):
        del w_in_ref
        issue([w_ref], send_sems, recv_sems)
        await_([w_ref], send_sems, recv_sems)

    return pl.pallas_call(
        body, name="sibling_forward_" + phase, in_specs=[ANY], out_specs=ANY,
        out_shape=jax.ShapeDtypeStruct(w_grp.shape, w_grp.dtype), input_output_aliases={0: 0},
        scratch_shapes=[pltpu.SemaphoreType.DMA((3,)), pltpu.SemaphoreType.DMA((3,))],
    )(w_grp)


def _merge_edges(w, edge0, mixed, name):
    d = w.shape[1]

    def body(e_ref, o_ref):
        o_ref[...] = e_ref[0:DH, :] + e_ref[DH:2 * DH, :]

    def to_block(i):
        r = mixed[-1]
        for kk in range(len(mixed) - 2, -1, -1):
            r = jnp.where(i == kk, mixed[kk], r)
        return r

    return pl.pallas_call(
        body, name=name, grid=(len(mixed),),
        in_specs=[pl.BlockSpec((2 * DH, d), lambda i: (edge0 // 2 + i, 0))],
        out_specs=pl.BlockSpec((DH, d), lambda i: (to_block(i), 0)),
        out_shape=jax.ShapeDtypeStruct(w.shape, w.dtype),
        input_output_aliases={0: 0},
        compiler_params=_params(("arbitrary",)),
    )(w)


def _scatter_start(phase, g_grp, land, singles, halved=False):
    ns = len(singles)

    def issue(refs, send_sems, recv_sems):
        g_ref, land_ref = refs[0], refs[1]
        x, y, c, chips = _place()
        for jj, (px, py) in enumerate(chips):
            to = dict(device_id=(px, py, c), device_id_type=MESH)
            peer = 2 * px + py
            for a in range(ns):
                pltpu.make_async_remote_copy(
                    src_ref=refs[2 + 2 * a].at[peer], dst_ref=refs[3 + 2 * a].at[jj],
                    send_sem=send_sems.at[(1 + ns) * jj + 1 + a], recv_sem=recv_sems.at[(1 + ns) * jj + 1 + a],
                    **to).start()
            for s in range(4):
                for par in ((0, 1) if halved else (None,)):
                    blocks = _phase_blocks(s, phase, False, par)
                    if blocks:
                        @pl.when((peer == s) if par is None else ((peer == s) & (c == par)))
                        def _():
                            for b, blk in blocks:
                                pltpu.make_async_remote_copy(
                                    src_ref=_blk(g_ref, blk), dst_ref=_blk(land_ref.at[jj], b),
                                    send_sem=send_sems.at[(1 + ns) * jj], recv_sem=recv_sems.at[(1 + ns) * jj],
                                    **to).start()

    bufs = [g_grp, land] + [t for pair in singles for t in pair]
    return _split_start("scatter_start_" + phase, issue, bufs, 3 * (1 + ns))


def _scatter_wait(phase, send_sems, recv_sems, bufs, after, halved=False):
    ns = (len(bufs) - 2) // 2

    def await_(refs, send_sems, recv_sems):
        g_ref, land_ref = refs[0], refs[1]
        x, y, c, chips = _place()
        mine = 2 * x + y
        for jj, (px, py) in enumerate(chips):
            to = dict(device_id=(px, py, c), device_id_type=MESH)
            peer = 2 * px + py
            for a in range(ns):
                cp = pltpu.make_async_remote_copy(
                    src_ref=refs[2 + 2 * a].at[peer], dst_ref=refs[3 + 2 * a].at[jj],
                    send_sem=send_sems.at[(1 + ns) * jj + 1 + a], recv_sem=recv_sems.at[(1 + ns) * jj + 1 + a], **to)
                cp.wait_recv()
                cp.wait_send()
            for s in range(4):
                for par in ((0, 1) if halved else (None,)):
                    nblk = len(_phase_blocks(s, phase, False, par))
                    if nblk:
                        both = pltpu.make_async_remote_copy(
                            src_ref=_cols(g_ref, nblk), dst_ref=_cols(land_ref.at[jj], nblk),
                            send_sem=send_sems.at[(1 + ns) * jj], recv_sem=recv_sems.at[(1 + ns) * jj], **to)

                        @pl.when((mine == s) if par is None else ((mine == s) & (c == par)))
                        def _():
                            both.wait_recv()

                        @pl.when((peer == s) if par is None else ((peer == s) & (c == par)))
                        def _():
                            both.wait_send()

    return _split_wait("scatter_wait_" + phase, await_, send_sems, recv_sems, bufs, after)


def _needed_blocks(phase, parity):
    return sorted({blk for s in range(4) for _, blk in _phase_blocks(s, phase, False, parity)})


def _pair_reduce(phase, g_grp):
    n, d = g_grp.shape

    def swap(g_ref, sib_ref, send_sem, recv_sem):
        x, y, c, _ = _place()
        to = dict(device_id=(x, y, 1 - c), device_id_type=MESH)
        for par in range(2):
            give, get = _needed_blocks(phase, 1 - par), _needed_blocks(phase, par)

            @pl.when(c == par)
            def _():
                for blk in give:
                    pltpu.make_async_remote_copy(src_ref=_blk(g_ref, blk), dst_ref=_blk(sib_ref, blk),
                                                 send_sem=send_sem, recv_sem=recv_sem, **to).start()
                pltpu.make_async_remote_copy(src_ref=_cols(g_ref, len(get)), dst_ref=_cols(sib_ref, len(get)),
                                             send_sem=send_sem, recv_sem=recv_sem, **to).wait_recv()
                pltpu.make_async_remote_copy(src_ref=_cols(g_ref, len(give)), dst_ref=_cols(sib_ref, len(give)),
                                             send_sem=send_sem, recv_sem=recv_sem, **to).wait_send()

    sib = pl.pallas_call(
        swap, name="pair_swap_" + phase, in_specs=[ANY], out_specs=ANY,
        out_shape=jax.ShapeDtypeStruct((n, d), g_grp.dtype),
        scratch_shapes=[pltpu.SemaphoreType.DMA, pltpu.SemaphoreType.DMA],
    )(*_in_hbm(g_grp))

    lists = [_needed_blocks(phase, par) for par in range(2)]
    longest = max(len(t) for t in lists)
    table = jnp.asarray([t + [t[-1]] * (longest - len(t)) for t in lists], jnp.int32)[lax.axis_index("c")]

    def add(t_ref, a_ref, b_ref, o_ref):
        o_ref[...] = (a_ref[...].astype(F32) + b_ref[...].astype(F32)).astype(o_ref.dtype)

    blk = pl.BlockSpec((DH, d), lambda i, t: (t[i], 0))
    return pl.pallas_call(
        add, name="pair_add_" + phase,
        grid_spec=pltpu.PrefetchScalarGridSpec(num_scalar_prefetch=1, grid=(longest,),
                                               in_specs=[blk, blk], out_specs=blk),
        out_shape=jax.ShapeDtypeStruct((n, d), g_grp.dtype),
        compiler_params=_params(("arbitrary",)),
    )(table, g_grp, sib)


def _sum_shard(g_g, g_c, land):
    d = g_g.shape[1]
    chip = 2 * lax.axis_index("x") + lax.axis_index("y")

    def body(t_ref, gg_ref, gc_ref, land_ref, o_ref):
        b = pl.program_id(0)
        in_g = t_ref[2, b] == 1
        own = jnp.where(in_g, gg_ref[...].astype(F32), gc_ref[...].astype(F32))
        for jj in range(3):
            own = own + land_ref[jj].astype(F32)
        o_ref[...] = jnp.where(in_g & (b % 2 != lax.axis_index("c")), 0.0, own)

    return pl.pallas_call(
        body, name="sum_w_in",
        grid_spec=pltpu.PrefetchScalarGridSpec(
            num_scalar_prefetch=1, grid=(ALIGNED_BLOCKS,),
            in_specs=[pl.BlockSpec((DH, d), lambda b, t: (t[0, b], 0)), pl.BlockSpec((DH, d), lambda b, t: (t[1, b], 0)),
                      pl.BlockSpec((3, DH, d), lambda b, t: (0, b, 0))],
            out_specs=pl.BlockSpec((DH, d), lambda b, t: (b, 0))),
        out_shape=jax.ShapeDtypeStruct((ALIGNED_W, d), F32),
        compiler_params=_params(("arbitrary",)),
    )(_block_table(chip, False, 0, 0), g_g, g_c, land)


def _sum_rows(stack, land, rows):
    _, r, d = stack.shape
    rows = min(rows, r)
    chip = 2 * lax.axis_index("x") + lax.axis_index("y")

    def body(t_ref, own_ref, land_ref, o_ref):
        acc = own_ref[0].astype(F32)
        for jj in range(3):
            acc = acc + land_ref[jj].astype(F32)
        o_ref[...] = acc

    return pl.pallas_call(
        body, name="sum_w_out",
        grid_spec=pltpu.PrefetchScalarGridSpec(
            num_scalar_prefetch=1, grid=(r // rows,),
            in_specs=[pl.BlockSpec((1, rows, d), lambda i, t: (t[0], i, 0)),
                      pl.BlockSpec((3, rows, d), lambda i, t: (0, i, 0))],
            out_specs=pl.BlockSpec((rows, d), lambda i, t: (i, 0))),
        out_shape=jax.ShapeDtypeStruct((r, d), F32),
        compiler_params=_params(("arbitrary",)),
    )(jnp.reshape(chip, (1,)).astype(jnp.int32), stack, land)


def _final_exchange(parts, pack):
    npart = len(parts)

    def body(*refs):
        ins, pack_ref = refs[:npart], refs[npart]
        outs, packs = refs[npart + 1:2 * npart + 1], refs[2 * npart + 1]
        send_sems, recv_sems, psend, precv, loc_sem = refs[2 * npart + 2:]
        x, y, c, _ = _place()
        me = 4 * x + 2 * y + c
        local = pltpu.make_async_copy(pack_ref, packs.at[me], loc_sem)
        local.start()
        cps = [pltpu.make_async_remote_copy(
            src_ref=ins[a], dst_ref=outs[a], send_sem=send_sems.at[a], recv_sem=recv_sems.at[a],
            device_id=(x, y, 1 - c), device_id_type=MESH) for a in range(npart)]
        for r in range(1, 8):
            dx, dy, dc = (r >> 2) & 1, (r >> 1) & 1, r & 1
            peer = (x + dx - 2 * x * dx, y + dy - 2 * y * dy, c + dc - 2 * c * dc)
            cps.append(pltpu.make_async_remote_copy(
                src_ref=pack_ref, dst_ref=packs.at[me], send_sem=psend.at[r - 1], recv_sem=precv.at[r - 1],
                device_id=peer, device_id_type=MESH))
        for cp in cps:
            cp.start()
        for cp in cps:
            cp.wait_recv()
        for cp in cps:
            cp.wait_send()
        local.wait()

    return pl.pallas_call(
        body, name="final_exchange",
        in_specs=[ANY] * (npart + 1), out_specs=[ANY] * (npart + 1),
        out_shape=[jax.ShapeDtypeStruct(p.shape, p.dtype) for p in parts]
        + [jax.ShapeDtypeStruct((8,) + pack.shape, pack.dtype)],
        scratch_shapes=[pltpu.SemaphoreType.DMA((npart,)), pltpu.SemaphoreType.DMA((npart,)),
                        pltpu.SemaphoreType.DMA((7,)), pltpu.SemaphoreType.DMA((7,)), pltpu.SemaphoreType.DMA],
    )(*parts, pack)


def _sibling_swap(part):
    def body(in_ref, out_ref, send_sem, recv_sem):
        x, y, c, _ = _place()
        cp = pltpu.make_async_remote_copy(src_ref=in_ref, dst_ref=out_ref, send_sem=send_sem, recv_sem=recv_sem,
                                          device_id=(x, y, 1 - c), device_id_type=MESH)
        cp.start()
        cp.wait_recv()
        cp.wait_send()

    return pl.pallas_call(
        body, name="sibling_swap", in_specs=[ANY], out_specs=ANY,
        out_shape=jax.ShapeDtypeStruct(part.shape, part.dtype),
        scratch_shapes=[pltpu.SemaphoreType.DMA, pltpu.SemaphoreType.DMA],
    )(part)


def _sum_packs(packs):
    def body(p_ref, o_ref):
        acc = p_ref[0]
        for d in range(1, 8):
            acc = acc + p_ref[d]
        o_ref[...] = acc

    return pl.pallas_call(
        body, name="sum_packs", out_shape=jax.ShapeDtypeStruct(packs.shape[1:], F32),
    )(packs)


def _adamw_update(g, w_ref, m_ref, v_ref, go, do, mo, vo):
    c1 = 1.0 / (1.0 - ADAM_B1 ** ADAM_STEP)
    c2 = 1.0 / (1.0 - ADAM_B2 ** ADAM_STEP)
    mn = ADAM_B1 * m_ref[...] + (1.0 - ADAM_B1) * g
    vn = ADAM_B2 * v_ref[...] + (1.0 - ADAM_B2) * (g * g)
    go[...] = g
    mo[...] = mn
    vo[...] = vn
    do[...] = -ADAM_LR * ((mn * c1) / (jnp.sqrt(vn * c2) + ADAM_EPS) + ADAM_WD * w_ref[...])


def _adamw(w, m, v, g1, g2, rows, name):
    r, cdim = w.shape
    rows = min(rows, r)

    def body(*refs):
        n_in = 4 if g2 is None else 5
        w_ref, m_ref, v_ref, g_ref = refs[:4]
        g = g_ref[...] if g2 is None else g_ref[...] + refs[4][...]
        _adamw_update(g, w_ref, m_ref, v_ref, *refs[n_in:n_in + 4])

    blk = pl.BlockSpec((rows, cdim), lambda i: (i, 0))
    args = [w, m, v, g1] + ([] if g2 is None else [g2])
    shp = jax.ShapeDtypeStruct((r, cdim), F32)
    return pl.pallas_call(
        body, name=name, grid=(r // rows,),
        in_specs=[blk] * len(args), out_specs=[blk] * 4, out_shape=[shp] * 4,
        compiler_params=_params(("parallel",), 20 * rows * cdim * 4 + 8 * 2**20),
    )(*_in_hbm(*args))


def _adamw_shard(wt, mt, vt, g1, g2):
    r, d = wt.shape
    cols = min(128, d)

    def body(w_ref, m_ref, v_ref, g_ref, g2_ref, go, do, mo, vo, pad_ref):
        chip = 2 * lax.axis_index("x") + lax.axis_index("y")
        back = [(ALIGNED_W - s) % ALIGNED_W for s in SHIFTS]
        pad_ref[...] = pltpu.roll(g_ref[...] + g2_ref[...], _by_chip(chip, back), 0)
        outs = [o.at[:, 0, :] for o in (go, do, mo, vo)]
        _adamw_update(pad_ref[0:r, :], w_ref, m_ref, v_ref, *outs)

    blk = pl.BlockSpec((r, cols), lambda i: (0, i))
    gblk = pl.BlockSpec((ALIGNED_W, cols), lambda i: (0, i))
    oblk = pl.BlockSpec((r, 1, cols), lambda i: (0, 0, i))
    shp = jax.ShapeDtypeStruct((r, 1, d), F32)
    return pl.pallas_call(
        body, name="adamw_w_in", grid=(d // cols,),
        in_specs=[blk] * 3 + [gblk] * 2, out_specs=[oblk] * 4, out_shape=[shp] * 4,
        scratch_shapes=[pltpu.VMEM((ALIGNED_W, cols), F32)],
        compiler_params=_params(("parallel",), 24 * ALIGNED_W * cols * 4 + 8 * 2**20),
    )(wt, mt, vt, g1, g2)


def _pad_lanes(a, width):
    return jnp.pad(a, ((0, 0), (0, width - a.shape[1])))


def _gathered_to_full(g):
    return jnp.transpose(g, (1, 0, 2)).reshape(g.shape[1], 4 * g.shape[2])


def _row(a):
    return _pad_lanes(a.reshape(1, -1), 1024)


def _small_pack(nin, cb, fn, al, dt, gn, cqw_shard, cw_shard):
    ad = jnp.concatenate([al.reshape(1, -1), dt.reshape(1, -1)], axis=1)
    rows = [_row(nin), _row(cb), _row(fn), _row(ad), _row(gn), cqw_shard.reshape(3, 1024), _row(cw_shard)]
    out = jnp.concatenate(rows, axis=0)
    return jnp.pad(out, ((0, 16 - out.shape[0]), (0, 0)))


def kernel(x, norm_in_w, w_in, conv_qkv_w, A_log, dt_bias, gdn_norm_w, conv_w, conv_b, w_out, final_norm_w, loss_target, m_norm_in_w, m_w_in, m_conv_qkv_w, m_A_log, m_dt_bias, m_gdn_norm_w, m_conv_w, m_conv_b, m_w_out, m_final_norm_w, v_norm_in_w, v_w_in, v_conv_qkv_w, v_A_log, v_dt_bias, v_gdn_norm_w, v_conv_w, v_conv_b, v_w_out, v_final_norm_w):
    chip = 2 * lax.axis_index("x") + lax.axis_index("y")
    a_shard = _align_shard(jnp.transpose(w_in, (2, 0, 1)))
    wo_b = _cast_bf16(w_out[0], 256, "cast_w_out")
    d_model = x.shape[-1]
    stack = lambda s: lax.empty((4,) + s.shape, s.dtype)
    wg0 = lax.empty((WG_BLOCKS * DH, d_model), BF16)
    wc0 = lax.empty((WC_BLOCKS * DH, d_model), BF16)
    ss_g, rs_g, bufs_g, tok_g = _gather_start("g", a_shard, wg0, [(conv_qkv_w[0], stack(conv_qkv_w[0]))])
    ss_c, rs_c, bufs_c, tok_c = _gather_start("c", bufs_g[0], wc0,
                                              [(conv_w[0], stack(conv_w[0])), (wo_b, stack(wo_b))])
    wg1, wc1, wog1, cqg1, cwg1 = _place_own(bufs_c[0], bufs_c[4], bufs_g[2], bufs_c[2],
                                            [bufs_g[1], bufs_c[1], bufs_c[5], bufs_g[3], bufs_c[3]])
    x0 = x[0]
    h = _rms_in(x0, _tie(_tie(norm_in_w, tok_g, "after_gather_start_g"), tok_c, "after_gather_start_c"))
    a_thru, wg, _, cq_g = _gather_wait("g", ss_g, rs_g, [bufs_c[0], wg1, bufs_g[2], cqg1], h)
    w_g = _merge_edges(_sibling_forward("g", wg), G_EDGE, G_MIXED, "merge_edges_g")
    cqw = _gathered_to_full(cq_g)
    ad = jnp.pad(jnp.concatenate([A_log, dt_bias], axis=0), ((0, 0), (A_LANE, 0)))
    fwd_c = {}

    def on_q(q):
        _, wc, _, cw_g, _, wo_g = _gather_wait("c", ss_c, rs_c,
                                               [a_thru, wc1, bufs_c[2], cwg1, bufs_c[4], wog1], q)
        issue, _ = _sibling_forward_parts("c")
        ss, rs, (wc,), tok = _split_start("sibling_forward_start_c", issue, [wc], 3)
        fwd_c.update(ss=ss, rs=rs, wc=wc, cw_g=cw_g, wo_g=wo_g)
        return _tie(q, tok, "after_sibling_forward_start_c")

    def late(o):
        _, await_ = _sibling_forward_parts("c")
        (wc,) = _split_wait("sibling_forward_wait_c", await_, fwd_c["ss"], fwd_c["rs"], [fwd_c["wc"]], o)
        return (_merge_edges(wc, C_EDGE, C_MIXED, "merge_edges_c"), fwd_c["wo_g"].reshape(2 * GW, d_model),
                _gathered_to_full(fwd_c["cw_g"]))

    scat = {}

    def on_grad_c(g_c, g_wout, do):
        go4 = g_wout.reshape(4, GW // 2, d_model)
        land = lax.empty((3, ALIGNED_W, d_model), BF16)
        land_o = lax.empty((3, GW // 2, d_model), BF16)
        ss, rs, bufs, tok = _scatter_start("c", g_c, land, [(go4, land_o)])
        scat["c"] = (ss, rs, bufs)
        return _tie(do, tok, "after_scatter_start_c")

    def on_grad_g(g_g, dproj_g):
        ss, rs, bufs, tok = _scatter_start("g", _pair_reduce("g", g_g), scat["c"][2][1], [], halved=True)
        scat["g"] = (ss, rs, bufs)
        return _tie(dproj_g, tok, "after_scatter_start_g")

    gx, sm, _ = _local_step(x0, loss_target[0], h, w_g, cqw, late, norm_in_w, ad, gdn_norm_w, conv_b,
                            final_norm_w.reshape(1, -1), on_grad_c, on_grad_g, on_q)

    ss, rs, bufs = scat["c"]
    g_c, land, go4, land_o = _scatter_wait("c", ss, rs, [bufs[0], scat["g"][2][1], bufs[2], bufs[3]], gx)
    part_out = _sum_rows(go4, land_o, 128)
    ad_g = jnp.concatenate([sm["al"][:, A_LANE:], sm["dt"][:, A_LANE:]], axis=1)
    pack = jnp.concatenate([_row(sm["nin"]), _row(sm["cb"]), _row(sm["fn"]), _row(ad_g), _row(sm["gn"]),
                            jnp.concatenate(sm["cq"], axis=1).reshape(12, 1024), sm["cw"], _row(sm["loss"])], axis=0)
    pack = jnp.pad(pack, ((0, PACK_ROWS - pack.shape[0]), (0, 0)))
    sib_out, packs = _final_exchange([part_out], pack)
    tot = _sum_packs(packs)
    g_wo, d_wo, m_wo, v_wo = _adamw(w_out[0], m_w_out[0], v_w_out[0], part_out, sib_out, 128, "adamw_w_out")
    g_cq_sh = lax.dynamic_slice_in_dim(tot[R_CQ:R_CQ + 12].reshape(4, 3 * GW), chip * 768, 768, axis=1)
    g_cw_sh = lax.dynamic_slice_in_dim(tot[R_CW:R_CW + 3], chip * 256, 256, axis=1)
    sp = lambda nin, cb, fn, al, dt, gn, cq, cwv: _small_pack(nin, cb, fn, al, dt, gn, cq[0], cwv[0])
    g_s = _small_pack(tot[R_NIN], tot[R_CB], tot[R_FN], tot[R_AD, :HEADS], tot[R_AD, HEADS:2 * HEADS],
                      tot[R_GN, :DH], g_cq_sh, g_cw_sh)
    w_s = sp(norm_in_w, conv_b, final_norm_w, A_log, dt_bias, gdn_norm_w, conv_qkv_w, conv_w)
    m_s = sp(m_norm_in_w, m_conv_b, m_final_norm_w, m_A_log, m_dt_bias, m_gdn_norm_w, m_conv_qkv_w, m_conv_w)
    v_s = sp(v_norm_in_w, v_conv_b, v_final_norm_w, v_A_log, v_dt_bias, v_gdn_norm_w, v_conv_qkv_w, v_conv_w)
    small = _adamw(w_s, m_s, v_s, g_s, None, 16, "adamw_small")

    ss, rs, bufs = scat["g"]
    g_g, land = _scatter_wait("g", ss, rs, [bufs[0], land], small[0], halved=True)
    part_in = _sum_shard(g_g, g_c, land)
    g_wi, d_wi, m_wi, v_wi = [jnp.transpose(a, (1, 2, 0))[0] for a in _adamw_shard(
        jnp.transpose(w_in[0]), jnp.transpose(m_w_in[0]), jnp.transpose(v_w_in[0]), part_in, _sibling_swap(part_in))]

    def unpack(a, big_in, big_out):
        return (a[0:1], big_in[None], a[5:8].reshape(1, 4, 768), a[3:4, :HEADS], a[3:4, HEADS:2 * HEADS],
                a[4:5, :DH], a[8, :768].reshape(1, 3, 256), a[1:2], big_out[None], a[2])

    loss = tot[R_LOSS, 0]
    return (loss, gx[None], *unpack(small[0], g_wi, g_wo), *unpack(small[1], d_wi, d_wo),
            *unpack(small[2], m_wi, m_wo), *unpack(small[3], v_wi, v_wo))
```

```python
import functools
import math

import jax
import jax.numpy as jnp
from jax import lax
from jax.experimental import pallas as pl
from jax.experimental.pallas import tpu as pltpu

F32 = jnp.float32
BF16 = jnp.bfloat16
MESH = pl.DeviceIdType.MESH
ANY = pl.BlockSpec(memory_space=pl.ANY)

HEADS = 8
DH = 128
CH = 64
GW = HEADS * DH
EPS = 1e-6
VMEM_V7X = 64 * 1024 * 1024

QB, KB, VB, ZB, BAB = 0, 8, 16, 24, 32
A_LANE = 120
NG, NC = 33, 32
GW_COLS, CW_COLS = NG * DH, NC * DH

SHARD_W = 2052
ALIGNED_BLOCKS = 17
ALIGNED_W = ALIGNED_BLOCKS * DH
SHIFTS = (0, 4, ALIGNED_W - 8, ALIGNED_W - 4)
G_EDGE, C_EDGE = 34, 32
G_SPARE, C_SPARE = 33, 34
WG_BLOCKS, WC_BLOCKS = 38, 36
G_MIXED, C_MIXED = (2, BAB), (4 * 7 + 1,)


def _shard_blocks(chip, edges):
    g, c = "g", "c"
    if chip == 0:
        out = [(g, 3 * b) for b in range(8)] + [(g, 3 * b + 1) for b in range(8)] + [(g, G_EDGE, G_MIXED[0])]
    elif chip == 1:
        out = [(g, G_EDGE + 1, G_MIXED[0])] + [(g, 3 * b + 2) for b in range(1, 8)]
        out += [(g, ZB + b) for b in range(8)] + [(g, G_EDGE + 2, G_MIXED[1])]
    elif chip == 2:
        out = [(c, 4 * b) for b in range(8)] + [(c, 4 * b + 1) for b in range(7)]
        out += [(c, C_EDGE, C_MIXED[0]), (g, G_EDGE + 3, G_MIXED[1])]
    else:
        out = [(c, 4 * b + 2) for b in range(8)] + [(c, 4 * b + 3) for b in range(8)] + [(c, C_EDGE + 1, C_MIXED[0])]
    return [(o[0], o[1] if (edges or len(o) == 2) else o[2]) for o in out]


def _by_chip(chip, vals):
    if all(v == vals[0] for v in vals):
        return vals[0]
    r = vals[3]
    for kk in (2, 1, 0):
        r = jnp.where(chip == kk, vals[kk], r)
    return r

ADAM_LR, ADAM_B1, ADAM_B2, ADAM_EPS, ADAM_WD, ADAM_STEP = 0.001, 0.9, 0.999, 1e-08, 0.01, 10

R_NIN, R_CB, R_FN, R_AD, R_GN, R_CQ, R_CW, R_LOSS, PACK_ROWS = 0, 1, 2, 3, 4, 5, 17, 20, 24

NN = ((1,), (0,))
NT = ((1,), (1,))
TN = ((0,), (0,))


def _dot(a, b, dims=NN, mode="lo"):
    dn = (dims, ((), ()))
    if mode == "hi":
        return lax.dot_general(a, b, dn, precision=lax.Precision.HIGHEST, preferred_element_type=F32)
    ah, bh = a.astype(BF16), b.astype(BF16)
    out = lax.dot_general(ah, bh, dn, preferred_element_type=F32)
    if mode == "x3":
        al = (a - ah.astype(F32)).astype(BF16)
        bl = (b - bh.astype(F32)).astype(BF16)
        out = out + lax.dot_general(ah, bl, dn, preferred_element_type=F32)
        out = out + lax.dot_general(al, bh, dn, preferred_element_type=F32)
    return out


P_GRAM, P_INV, P_SOL, P_SCAN, P_SCANB, P_BWD = "lo", "lo", "lo", "lo", "lo", "lo"
P_CUM = "x3"


def _params(sem=None, vmem=None):
    kw = {}
    if sem is not None:
        kw["dimension_semantics"] = sem
    if vmem is not None:
        kw["vmem_limit_bytes"] = int(min(max(vmem, 32 * 2**20), VMEM_V7X - 8 * 2**20))
    return pltpu.CompilerParams(**kw)


def _in_hbm(*arrays):
    return [pltpu.with_memory_space_constraint(a, pltpu.HBM) for a in arrays]


def _sigmoid(x):
    return 1.0 / (1.0 + jnp.exp(-x))


def _dsilu(x, s):
    return s * (1.0 + x * (1.0 - s))


def _rows(shape):
    return lax.broadcasted_iota(jnp.int32, shape, 0)


def _shift_down(x, s):
    if s == 0:
        return x
    return jnp.where(_rows(x.shape) >= s, pltpu.roll(x, s, 0), 0.0)


def _shift_up(x, s):
    if s == 0:
        return x
    n = x.shape[0]
    return jnp.where(_rows(x.shape) < n - s, pltpu.roll(x, n - s, 0), 0.0)


def _matmul(a, b, dims, out_dtype, tm, tn, tk, name, add=None, n=None, b_outer=False):
    if dims == NN:
        (m, k), n = a.shape, b.shape[1]
    elif dims == NT:
        (m, k), n = a.shape, (n or b.shape[0])
    else:
        (k, m), n = a.shape, b.shape[1]
    tm, tn, tk = min(tm, m), min(tn, n), min(tk, k)
    assert m % tm == 0 and n % tn == 0 and k % tk == 0, (name, m, n, k, tm, tn, tk)
    nk = k // tk

    def body(*refs):
        if add is None:
            a_ref, b_ref, o_ref = refs[:3]
            add_ref = None
        else:
            a_ref, b_ref, add_ref, o_ref = refs[:4]
        part = _dot(a_ref[...], b_ref[...], dims)
        if nk == 1:
            if add_ref is not None:
                part = part + add_ref[...]
            o_ref[...] = part.astype(out_dtype)
            return
        acc = refs[-1]
        kk = pl.program_id(2)

        @pl.when(kk == 0)
        def _():
            acc[...] = part

        @pl.when(kk > 0)
        def _():
            acc[...] += part

        @pl.when(kk == nk - 1)
        def _():
            r = acc[...]
            if add_ref is not None:
                r = r + add_ref[...]
            o_ref[...] = r.astype(out_dtype)

    ij = (lambda g0, g1: (g1, g0)) if b_outer else (lambda g0, g1: (g0, g1))

    def spec(shape, pick):
        return pl.BlockSpec(shape, lambda g0, g1, kk: pick(*ij(g0, g1), kk))

    a_spec = spec((tk, tm), lambda i, j, kk: (kk, i)) if dims == TN else spec((tm, tk), lambda i, j, kk: (i, kk))
    b_spec = spec((tn, tk), lambda i, j, kk: (j, kk)) if dims == NT else spec((tk, tn), lambda i, j, kk: (kk, j))
    o_spec = spec((tm, tn), lambda i, j, kk: (i, j))
    in_specs = [a_spec, b_spec]
    args = [a, b]
    if add is not None:
        in_specs.append(o_spec)
        args.append(add)
    osz = jnp.dtype(out_dtype).itemsize
    est = 2 * (tm * tk * a.dtype.itemsize + tk * tn * b.dtype.itemsize + tm * tn * osz)
    est += 3 * tm * tn * 4 + (2 * tm * tn * 4 if add is not None else 0)
    return pl.pallas_call(
        body, name=name, grid=(n // tn, m // tm, nk) if b_outer else (m // tm, n // tn, nk),
        in_specs=in_specs, out_specs=o_spec,
        out_shape=jax.ShapeDtypeStruct((m, n), out_dtype),
        scratch_shapes=[pltpu.VMEM((tm, tn), F32)] if nk > 1 else [],
        compiler_params=_params(("parallel", "parallel", "arbitrary"), est + 8 * 2**20),
    )(*args)


def _cast_bf16(a, rows, name):
    r, c = a.shape
    rows = min(rows, r)

    def body(a_ref, o_ref):
        o_ref[...] = a_ref[...].astype(BF16)

    return pl.pallas_call(
        body, name=name, grid=(r // rows,),
        in_specs=[pl.BlockSpec((rows, c), lambda i: (i, 0))],
        out_specs=pl.BlockSpec((rows, c), lambda i: (i, 0)),
        out_shape=jax.ShapeDtypeStruct((r, c), BF16),
        compiler_params=_params(("parallel",)),
    )(a)


def _align_shard(wt):
    r, _, d = wt.shape
    cols = min(256, d)

    def body(w_ref, o_ref, pad_ref):
        chip = 2 * lax.axis_index("x") + lax.axis_index("y")
        pad_ref[...] = jnp.zeros_like(pad_ref)
        pad_ref[0:r, :] = w_ref[:, 0, :]
        o_ref[...] = pltpu.roll(pad_ref[...], _by_chip(chip, SHIFTS), 0).astype(BF16)

    return pl.pallas_call(
        body, name="align_shard", grid=(d // cols,),
        in_specs=[pl.BlockSpec((r, 1, cols), lambda i: (0, 0, i))],
        out_specs=pl.BlockSpec((ALIGNED_W, cols), lambda i: (0, i)),
        out_shape=jax.ShapeDtypeStruct((ALIGNED_W, d), BF16),
        scratch_shapes=[pltpu.VMEM((ALIGNED_W, cols), F32)],
        compiler_params=_params(("parallel",)),
    )(wt)


def _rms_in(x, w):
    n, d = x.shape
    tr = min(256, n)

    def body(x_ref, w_ref, h_ref):
        xv = x_ref[...]
        r = lax.rsqrt(jnp.mean(xv * xv, axis=-1, keepdims=True) + EPS)
        h_ref[...] = (xv * r * w_ref[...]).astype(BF16)

    return pl.pallas_call(
        body, name="rms_in", grid=(n // tr,),
        in_specs=[pl.BlockSpec((tr, d), lambda i: (i, 0)), pl.BlockSpec((1, d), lambda i: (0, 0))],
        out_specs=pl.BlockSpec((tr, d), lambda i: (i, 0)),
        out_shape=jax.ShapeDtypeStruct((n, d), BF16),
        compiler_params=_params(("parallel",)),
    )(x, w)


def _conv_silu(p, w_ref, taps):
    c = None
    for j in range(taps):
        t = _shift_down(p, taps - 1 - j) * w_ref[j:j + 1, :]
        c = t if c is None else c + t
    return c


def _prep_qkv(proj, cw):
    n = proj.shape[0]

    def body(p3, wq, wk, wv, q_ref, k_ref, v_ref):
        for kind, (w_ref, o_ref) in enumerate(((wq, q_ref), (wk, k_ref), (wv, v_ref))):
            c = _conv_silu(p3[:, kind * DH:(kind + 1) * DH], w_ref, 4)
            a = c * _sigmoid(c)
            if kind < 2:
                r = lax.rsqrt(jnp.sum(a * a, axis=-1, keepdims=True) + EPS)
                a = a * (r * (DH ** -0.5 if kind == 0 else 1.0))
            o_ref[...] = a

    col = pl.BlockSpec((n, DH), lambda h: (0, h))
    wcol = lambda base: pl.BlockSpec((4, DH), lambda h: (0, base + h))
    out = jax.ShapeDtypeStruct((n, GW), F32)
    return pl.pallas_call(
        body, name="prep_qkv", grid=(HEADS,),
        in_specs=[pl.BlockSpec((n, 3 * DH), lambda h: (0, h)), wcol(QB), wcol(KB), wcol(VB)],
        out_specs=[col] * 3, out_shape=[out] * 3,
        compiler_params=_params(("parallel",), 40 * 2**20),
    )(proj, cw, cw, cw)


def _prep_qkv_bwd(proj, cw, dq, dk, dv, dproj):
    n = proj.shape[0]

    def body(p3, wq, wk, wv, dq_ref, dk_ref, dv_ref, _, o3, gq, gk, gv):
        for kind, (w_ref, d_ref, g_ref) in enumerate(((wq, dq_ref, gq), (wk, dk_ref, gk), (wv, dv_ref, gv))):
            p = p3[:, kind * DH:(kind + 1) * DH]
            shifted = [_shift_down(p, 3 - j) for j in range(4)]
            c = shifted[0] * w_ref[0:1, :]
            for j in range(1, 4):
                c = c + shifted[j] * w_ref[j:j + 1, :]
            s = _sigmoid(c)
            a = c * s
            d = d_ref[...]
            if kind < 2:
                r = lax.rsqrt(jnp.sum(a * a, axis=-1, keepdims=True) + EPS)
                sc = DH ** -0.5 if kind == 0 else 1.0
                d = (sc * r) * (d - a * ((r * r) * jnp.sum(d * a, axis=-1, keepdims=True)))
            dc = d * _dsilu(c, s)
            dp = None
            for j in range(4):
                g_ref[j:j + 1, :] = jnp.sum(dc * shifted[j], axis=0, keepdims=True)
                t = _shift_up(dc, 3 - j) * w_ref[j:j + 1, :]
                dp = t if dp is None else dp + t
            o3[:, kind * DH:(kind + 1) * DH] = dp.astype(BF16)

    col = pl.BlockSpec((n, DH), lambda h: (0, h))
    wcol = lambda base: pl.BlockSpec((4, DH), lambda h: (0, base + h))
    p3spec = pl.BlockSpec((n, 3 * DH), lambda h: (0, h))
    return pl.pallas_call(
        body, name="prep_qkv_bwd", grid=(HEADS,),
        in_specs=[p3spec, wcol(QB), wcol(KB), wcol(VB), col, col, col, ANY],
        out_specs=[p3spec] + [wcol(0)] * 3,
        out_shape=[jax.ShapeDtypeStruct(dproj.shape, BF16)] + [jax.ShapeDtypeStruct((4, GW), F32)] * 3,
        input_output_aliases={7: 0},
        compiler_params=_params(("parallel",), 48 * 2**20),
    )(proj, cw, cw, cw, dq, dk, dv, dproj)


CPB = 8
SCAN_CPS = 4


def _tri(lower, rows):
    i = lax.broadcasted_iota(jnp.int32, (rows, rows), 0)
    j = lax.broadcasted_iota(jnp.int32, (rows, rows), 1)
    return jnp.where((i // CH == j // CH) & ((i >= j) if lower else (j >= i)), 1.0, 0.0)


def _lane(shape):
    return lax.broadcasted_iota(jnp.int32, shape, 1)


def _prep_bg(proj, ad):
    n = proj.shape[0]
    nch = n // CH
    cpb = CPB if nch % CPB == 0 else 1
    rows = cpb * CH

    def body(p_ref, ad_ref, bg_ref, bgt_ref):
        p = p_ref[...]
        lane = _lane(p.shape)
        beta = _sigmoid(p)
        xa = p + ad_ref[1:2, :]
        sp = jnp.maximum(xa, 0.0) + jnp.log(1.0 + jnp.exp(-jnp.abs(xa)))
        g = pltpu.roll(-jnp.exp(ad_ref[0:1, :]) * sp, DH - A_LANE + HEADS, 1)
        gc = _dot(_tri(True, rows), g, NN, P_CUM)
        bg = jnp.where(lane < HEADS, beta, jnp.where(lane < 2 * HEADS, gc, 0.0))
        bg_ref[...] = bg
        for ci in range(cpb):
            bgt_ref[ci] = bg[ci * CH:(ci + 1) * CH, :].T

    return pl.pallas_call(
        body, name="prep_bg", grid=(nch // cpb,),
        in_specs=[pl.BlockSpec((rows, DH), lambda i: (i, BAB)), pl.BlockSpec((2, DH), lambda i: (0, 0))],
        out_specs=[pl.BlockSpec((rows, DH), lambda i: (i, 0)), pl.BlockSpec((cpb, DH, CH), lambda i: (i, 0, 0))],
        out_shape=[jax.ShapeDtypeStruct((n, DH), F32), jax.ShapeDtypeStruct((nch, DH, CH), F32)],
        compiler_params=_params(("parallel",)),
    )(*_in_hbm(proj, ad))


def _prep_bg_bwd(proj, ad, dbg, dproj):
    n = proj.shape[0]
    nch = n // CH
    cpb = CPB if nch % CPB == 0 else 1
    rows = cpb * CH

    def body(p_ref, ad_ref, d_ref, _, o_ref, ga_ref, gd_ref):
        p = p_ref[...]
        d = d_ref[...]
        lane = _lane(p.shape)
        beta = _sigmoid(p)
        xa = p + ad_ref[1:2, :]
        sp = jnp.maximum(xa, 0.0) + jnp.log(1.0 + jnp.exp(-jnp.abs(xa)))
        na = -jnp.exp(ad_ref[0:1, :])
        dg = pltpu.roll(_dot(_tri(False, rows), d, NN, P_CUM), A_LANE - HEADS, 1)
        da = dg * na * _sigmoid(xa)
        is_g = lane >= A_LANE
        o_ref[...] = jnp.where(lane < HEADS, d * beta * (1.0 - beta), jnp.where(is_g, da, 0.0)).astype(BF16)
        ga = jnp.sum(jnp.where(is_g, dg * na * sp, 0.0), axis=0, keepdims=True)
        gd = jnp.sum(jnp.where(is_g, da, 0.0), axis=0, keepdims=True)

        @pl.when(pl.program_id(0) == 0)
        def _():
            ga_ref[...] = jnp.zeros_like(ga_ref)
            gd_ref[...] = jnp.zeros_like(gd_ref)

        ga_ref[...] += ga
        gd_ref[...] += gd

    one = pl.BlockSpec((1, DH), lambda i: (0, 0))
    return pl.pallas_call(
        body, name="prep_bg_bwd", grid=(nch // cpb,),
        in_specs=[pl.BlockSpec((rows, DH), lambda i: (i, BAB)), pl.BlockSpec((2, DH), lambda i: (0, 0)),
                  pl.BlockSpec((rows, DH), lambda i: (i, 0)), ANY],
        out_specs=[pl.BlockSpec((rows, DH), lambda i: (i, BAB)), one, one],
        out_shape=[jax.ShapeDtypeStruct(dproj.shape, BF16), jax.ShapeDtypeStruct((1, DH), F32),
                   jax.ShapeDtypeStruct((1, DH), F32)],
        input_output_aliases={3: 0},
        compiler_params=_params(("arbitrary",)),
    )(proj, ad, dbg, dproj)


def _gdn_out(o, proj, wg):
    n = o.shape[0]

    def body(o_ref, z_ref, w_ref, y_ref):
        ov, z = o_ref[...], z_ref[...]
        r = lax.rsqrt(jnp.mean(ov * ov, axis=-1, keepdims=True) + EPS)
        y_ref[...] = (ov * r * w_ref[...] * (z * _sigmoid(z))).astype(BF16)

    return pl.pallas_call(
        body, name="gdn_out", grid=(HEADS,),
        in_specs=[pl.BlockSpec((n, DH), lambda h: (0, h)), pl.BlockSpec((n, DH), lambda h: (0, ZB + h)),
                  pl.BlockSpec((1, DH), lambda h: (0, 0))],
        out_specs=pl.BlockSpec((n, DH), lambda h: (0, h)),
        out_shape=jax.ShapeDtypeStruct((n, 2 * GW), BF16),
        compiler_params=_params(("parallel",)),
    )(o, proj, wg)


def _gdn_out_bwd(o, proj, wg, dmix):
    n = o.shape[0]

    def body(o_ref, z_ref, w_ref, d_ref, do_ref, dz_ref, gw_ref):
        ov, z, d, w = o_ref[...], z_ref[...], d_ref[...], w_ref[...]
        r = lax.rsqrt(jnp.mean(ov * ov, axis=-1, keepdims=True) + EPS)
        nrm = ov * r
        s = _sigmoid(z)
        dz_ref[...] = (d * (nrm * w) * _dsilu(z, s)).astype(BF16)
        dn_w = d * (z * s)
        gw = jnp.sum(dn_w * nrm, axis=0, keepdims=True)
        dn = dn_w * w
        do_ref[...] = r * (dn - nrm * jnp.mean(dn * nrm, axis=-1, keepdims=True))

        @pl.when(pl.program_id(0) == 0)
        def _():
            gw_ref[...] = jnp.zeros_like(gw_ref)

        gw_ref[...] += gw

    return pl.pallas_call(
        body, name="gdn_out_bwd", grid=(HEADS,),
        in_specs=[pl.BlockSpec((n, DH), lambda h: (0, h)), pl.BlockSpec((n, DH), lambda h: (0, ZB + h)),
                  pl.BlockSpec((1, DH), lambda h: (0, 0)), pl.BlockSpec((n, DH), lambda h: (0, h))],
        out_specs=[pl.BlockSpec((n, DH), lambda h: (0, h)), pl.BlockSpec((n, DH), lambda h: (0, ZB + h)),
                   pl.BlockSpec((1, DH), lambda h: (0, 0))],
        out_shape=[jax.ShapeDtypeStruct((n, GW), F32), jax.ShapeDtypeStruct((n, GW_COLS), BF16),
                   jax.ShapeDtypeStruct((1, DH), F32)],
        compiler_params=_params(("arbitrary",)),
    )(o, proj, wg, dmix)


def _conv_branch(proj, w3, b, mix):
    n = proj.shape[0]

    def body(p4, w_ref, b_ref, _, y_ref):
        u = p4[:, DH:2 * DH] * p4[:, 2 * DH:3 * DH]
        cc = _conv_silu(u, w_ref, 3) + b_ref[...]
        z = p4[:, 3 * DH:4 * DH]
        y_ref[...] = (p4[:, 0:DH] * cc * (z * _sigmoid(z))).astype(BF16)

    return pl.pallas_call(
        body, name="conv_branch", grid=(HEADS,),
        in_specs=[pl.BlockSpec((n, 4 * DH), lambda h: (0, h)), pl.BlockSpec((3, DH), lambda h: (0, h)),
                  pl.BlockSpec((1, DH), lambda h: (0, h)), ANY],
        out_specs=pl.BlockSpec((n, DH), lambda h: (0, HEADS + h)),
        out_shape=jax.ShapeDtypeStruct(mix.shape, BF16),
        input_output_aliases={3: 0},
        compiler_params=_params(("parallel",), 40 * 2**20),
    )(*_in_hbm(proj, w3, b, mix))


def _conv_branch_bwd(proj, w3, b, dmix):
    n = proj.shape[0]

    def body(p4, w_ref, b_ref, d_ref, o4, gw_ref, gbias_ref):
        gb, gcv, hc, z = p4[:, 0:DH], p4[:, DH:2 * DH], p4[:, 2 * DH:3 * DH], p4[:, 3 * DH:4 * DH]
        d = d_ref[...]
        dgb, dgc, dhc, dzc = (o4.at[:, kk * DH:(kk + 1) * DH] for kk in range(4))
        u = gcv * hc
        cc = _conv_silu(u, w_ref, 3) + b_ref[...]
        s = _sigmoid(z)
        dzc[...] = (d * (gb * cc) * _dsilu(z, s)).astype(BF16)
        dp = d * (z * s)
        dgb[...] = (dp * cc).astype(BF16)
        dcc = dp * gb
        gbias_ref[...] = jnp.sum(dcc, axis=0, keepdims=True)
        du = None
        for j in range(3):
            gw_ref[j:j + 1, :] = jnp.sum(dcc * _shift_down(u, 2 - j), axis=0, keepdims=True)
            t = _shift_up(dcc, 2 - j) * w_ref[j:j + 1, :]
            du = t if du is None else du + t
        dgc[...] = (du * hc).astype(BF16)
        dhc[...] = (du * gcv).astype(BF16)

    p4spec = pl.BlockSpec((n, 4 * DH), lambda h: (0, h))
    return pl.pallas_call(
        body, name="conv_branch_bwd", grid=(HEADS,),
        in_specs=[p4spec, pl.BlockSpec((3, DH), lambda h: (0, h)), pl.BlockSpec((1, DH), lambda h: (0, h)),
                  pl.BlockSpec((n, DH), lambda h: (0, HEADS + h))],
        out_specs=[p4spec, pl.BlockSpec((3, DH), lambda h: (0, h)), pl.BlockSpec((1, DH), lambda h: (0, h))],
        out_shape=[jax.ShapeDtypeStruct((n, CW_COLS), BF16), jax.ShapeDtypeStruct((3, GW), F32),
                   jax.ShapeDtypeStruct((1, GW), F32)],
        compiler_params=_params(("parallel",), 48 * 2**20),
    )(proj, w3, b, dmix)


def _final_loss(out, tgt, wf):
    n, d = out.shape
    tr = min(256, n)

    def body(o_ref, t_ref, w_ref, do_ref, dob_ref, gw_ref, loss_ref):
        ov, w = o_ref[...], w_ref[...]
        r = lax.rsqrt(jnp.mean(ov * ov, axis=-1, keepdims=True) + EPS)
        nrm = ov * r
        e = nrm * w - t_ref[...]
        dy = e * (1.0 / d)
        dn = dy * w
        dout = r * (dn - nrm * jnp.mean(dn * nrm, axis=-1, keepdims=True))
        do_ref[...] = dout
        dob_ref[...] = dout.astype(BF16)

        @pl.when(pl.program_id(0) == 0)
        def _():
            gw_ref[...] = jnp.zeros_like(gw_ref)
            loss_ref[...] = jnp.zeros_like(loss_ref)

        gw_ref[...] += jnp.sum(dy * nrm, axis=0, keepdims=True)
        loss_ref[...] += (0.5 / d) * jnp.sum(jnp.sum(e * e, axis=-1, keepdims=True), axis=0, keepdims=True)

    row = pl.BlockSpec((tr, d), lambda i: (i, 0))
    return pl.pallas_call(
        body, name="final_loss", grid=(n // tr,),
        in_specs=[row, row, pl.BlockSpec((1, d), lambda i: (0, 0))],
        out_specs=[row, row, pl.BlockSpec((1, d), lambda i: (0, 0)), pl.BlockSpec((1, 1), lambda i: (0, 0))],
        out_shape=[jax.ShapeDtypeStruct((n, d), F32), jax.ShapeDtypeStruct((n, d), BF16),
                   jax.ShapeDtypeStruct((1, d), F32), jax.ShapeDtypeStruct((1, 1), F32)],
        compiler_params=_params(("arbitrary",)),
    )(*_in_hbm(out, tgt, wf))


def _rms_in_bwd(x, w, dh, dout):
    n, d = x.shape
    tr = min(256, n)

    def body(x_ref, w_ref, dh_ref, do_ref, dx_ref, gw_ref):
        xv, dhv = x_ref[...], dh_ref[...]
        r = lax.rsqrt(jnp.mean(xv * xv, axis=-1, keepdims=True) + EPS)
        xn = xv * r
        dxn = dhv * w_ref[...]
        dx_ref[...] = r * (dxn - xn * jnp.mean(dxn * xn, axis=-1, keepdims=True)) + do_ref[...]

        @pl.when(pl.program_id(0) == 0)
        def _():
            gw_ref[...] = jnp.zeros_like(gw_ref)

        gw_ref[...] += jnp.sum(dhv * xn, axis=0, keepdims=True)

    row = pl.BlockSpec((tr, d), lambda i: (i, 0))
    one = pl.BlockSpec((1, d), lambda i: (0, 0))
    return pl.pallas_call(
        body, name="rms_in_bwd", grid=(n // tr,),
        in_specs=[row, one, row, row], out_specs=[row, one],
        out_shape=[jax.ShapeDtypeStruct((n, d), F32), jax.ShapeDtypeStruct((1, d), F32)],
        compiler_params=_params(("arbitrary",)),
    )(*_in_hbm(x, w, dh, dout))


def _ij():
    i = lax.broadcasted_iota(jnp.int32, (CH, CH), 0)
    j = lax.broadcasted_iota(jnp.int32, (CH, CH), 1)
    return i, j


def _unit_lower_inverse(mats):
    i, j = _ij()
    eye = jnp.where(i == j, 1.0, 0.0)
    same16 = (i // 16) == (j // 16)
    same32 = (i // 32) == (j // 32)
    mm = lambda xs, ys: [_dot(x, y, NN, P_INV) for x, y in zip(xs, ys)]
    n1 = [jnp.where(same16, -a, 0.0) for a in mats]
    n2 = mm(n1, n1)
    n4 = mm(n2, n2)
    n8 = mm(n4, n4)
    t = [eye + x1 + x2 + x3 for x1, x2, x3 in zip(n1, n2, mm(n1, n2))]
    t = [x + y for x, y in zip(t, mm(t, n4))]
    t = [x + y for x, y in zip(t, mm(t, n8))]
    a1 = [jnp.where(same32 & jnp.logical_not(same16), a, 0.0) for a in mats]
    t = [x - y for x, y in zip(t, mm(t, mm(a1, t)))]
    a2 = [jnp.where(same32, 0.0, a) for a in mats]
    t = [x - y for x, y in zip(t, mm(t, mm(a2, t)))]
    return t


def _head_vectors(bg, bgt, h):
    bcol = bg[:, h:h + 1]
    gcol = bg[:, HEADS + h:HEADS + h + 1]
    grow = bgt[HEADS + h:HEADS + h + 1, :]
    return bcol, gcol, grow


def _decay(gcol, grow):
    i, j = _ij()
    return jnp.where(i >= j, jnp.exp(jnp.where(i >= j, gcol - grow, 0.0)), 0.0)


def _gdn_intra(q, k, v, bg, bgt):
    n = q.shape[0]
    nch = n // CH
    cps = 4 if nch % 4 == 0 else 1

    def body(q_ref, k_ref, v_ref, bg_ref, bgt_ref, u_ref, w_ref, p_ref, t_ref):
        i, j = _ij()
        items = [(ci, h) for ci in range(cps) for h in range(HEADS)]
        at = lambda ref, ci, h: ref.at[ci * CH:(ci + 1) * CH, h * DH:(h + 1) * DH]
        bgs = [bg_ref[ci * CH:(ci + 1) * CH, :] for ci in range(cps)]
        ks = [at(k_ref, ci, h)[...] for ci, h in items]
        vecs = [_head_vectors(bgs[ci], bgt_ref[ci], h) for ci, h in items]
        decs = [_decay(gcol, grow) for _, gcol, grow in vecs]
        kks = [_dot(kh, kh, NT, P_GRAM) for kh in ks]
        qks = [_dot(at(q_ref, ci, h)[...], kh, NT, P_GRAM) for (ci, h), kh in zip(items, ks)]
        ts = _unit_lower_inverse([jnp.where(i > j, bcol * kk * dec, 0.0)
                                  for (bcol, _, _), kk, dec in zip(vecs, kks, decs)])
        us = [_dot(t, at(v_ref, ci, h)[...] * bcol, NN, P_SOL) for t, (ci, h), (bcol, _, _) in zip(ts, items, vecs)]
        ws = [_dot(t, kh * (bcol * jnp.exp(gcol)), NN, P_SOL) for t, kh, (bcol, gcol, _) in zip(ts, ks, vecs)]
        for n_, (ci, h) in enumerate(items):
            p_ref[ci, h] = qks[n_] * decs[n_]
            t_ref[ci, h] = ts[n_]
            at(u_ref, ci, h)[...] = us[n_]
            at(w_ref, ci, h)[...] = ws[n_]

    row = pl.BlockSpec((cps * CH, GW), lambda c: (c, 0))
    sq = pl.BlockSpec((cps, HEADS, CH, CH), lambda c: (c, 0, 0, 0))
    big = jax.ShapeDtypeStruct((n, GW), F32)
    sqs = jax.ShapeDtypeStruct((nch, HEADS, CH, CH), F32)
    return pl.pallas_call(
        body, name="gdn_intra", grid=(nch // cps,),
        in_specs=[row, row, row, pl.BlockSpec((cps * CH, DH), lambda c: (c, 0)),
                  pl.BlockSpec((cps, DH, CH), lambda c: (c, 0, 0))],
        out_specs=[row, row, sq, sq], out_shape=[big, big, sqs, sqs],
        compiler_params=_params(("parallel",)),
    )(q, k, v, bg, bgt)


def _gdn_scan(q, k, bg, u, w, p):
    n = q.shape[0]
    nch = n // CH
    cps = SCAN_CPS if nch % SCAN_CPS == 0 else 1

    def body(q_ref, k_ref, bg_ref, u_ref, w_ref, p_ref, o_ref, vn_ref, s_out, s_scr):
        @pl.when(pl.program_id(0) == 0)
        def _():
            s_scr[...] = jnp.zeros_like(s_scr)

        hs = range(HEADS)
        sls = [slice(h * DH, (h + 1) * DH) for h in hs]
        ss = [s_scr[h] for h in hs]
        for ci in range(cps):
            rs = slice(ci * CH, (ci + 1) * CH)
            bg = bg_ref[rs, :]
            gcols = [bg[:, HEADS + h:HEADS + h + 1] for h in hs]
            glasts = [g[CH - 1:CH, :] for g in gcols]
            wss = [_dot(w_ref[rs, sl], s, NN, P_SCAN) for sl, s in zip(sls, ss)]
            oqs = [_dot(q_ref[rs, sl] * jnp.exp(g), s, NN, P_SCAN) for sl, s, g in zip(sls, ss, gcols)]
            vns = [u_ref[rs, sl] - x for sl, x in zip(sls, wss)]
            ops = [_dot(p_ref[ci, h], vn, NN, P_SCAN) for h, vn in zip(hs, vns)]
            sns = [_dot(k_ref[rs, sl] * jnp.exp(gl - g), vn, TN, P_SCAN)
                   for sl, gl, g, vn in zip(sls, glasts, gcols, vns)]
            for h, sl in enumerate(sls):
                s_out[ci, :, sl] = ss[h]
                vn_ref[rs, sl] = vns[h]
                o_ref[rs, sl] = oqs[h] + ops[h]
            ss = [s * jnp.exp(gl) + sn for s, gl, sn in zip(ss, glasts, sns)]
        for h in hs:
            s_scr[h] = ss[h]

    row = pl.BlockSpec((cps * CH, GW), lambda c: (c, 0))
    big = jax.ShapeDtypeStruct((n, GW), F32)
    return pl.pallas_call(
        body, name="gdn_scan", grid=(nch // cps,),
        in_specs=[row, row, pl.BlockSpec((cps * CH, DH), lambda c: (c, 0)), row, row,
                  pl.BlockSpec((cps, HEADS, CH, CH), lambda c: (c, 0, 0, 0))],
        out_specs=[row, row, pl.BlockSpec((cps, DH, GW), lambda c: (c, 0, 0))],
        out_shape=[big, big, jax.ShapeDtypeStruct((nch, DH, GW), F32)],
        scratch_shapes=[pltpu.VMEM((HEADS, DH, DH), F32)],
        compiler_params=_params(("arbitrary",)),
    )(q, k, bg, u, w, p)


def _gdn_scan_bwd(q, k, bg, w, p, vn, s_in, do):
    n = q.shape[0]
    nch = n // CH
    cps = SCAN_CPS if nch % SCAN_CPS == 0 else 1
    rev = lambda c: nch // cps - 1 - c

    def body(q_ref, k_ref, bg_ref, w_ref, p_ref, vn_ref, s_ref, do_ref,
             dqg_ref, dp_ref, du_ref, dw_ref, dks_ref, dgam_ref, ds_scr):
        @pl.when(pl.program_id(0) == 0)
        def _():
            ds_scr[...] = jnp.zeros_like(ds_scr)

        lane = _lane((1, DH))
        hs = range(HEADS)
        sls = [slice(h * DH, (h + 1) * DH) for h in hs]
        dss = [ds_scr[h] for h in hs]
        for ci in reversed(range(cps)):
            rs = slice(ci * CH, (ci + 1) * CH)
            bg = bg_ref[rs, :]
            gcols = [bg[:, HEADS + h:HEADS + h + 1] for h in hs]
            glasts = [g[CH - 1:CH, :] for g in gcols]
            ss = [s_ref[ci, :, sl] for sl in sls]
            dos = [do_ref[rs, sl] for sl in sls]
            vnl = [vn_ref[rs, sl] for sl in sls]
            dqgs = [_dot(d, s, NT, P_SCANB) for d, s in zip(dos, ss)]
            dps = [_dot(d, vn, NT, P_SCANB) for d, vn in zip(dos, vnl)]
            dvn1 = [_dot(p_ref[ci, h], d, TN, P_SCANB) for h, d in zip(hs, dos)]
            dvn2 = [_dot(k_ref[rs, sl] * jnp.exp(gl - g), ds, NN, P_SCANB)
                    for sl, gl, g, ds in zip(sls, glasts, gcols, dss)]
            dkss = [_dot(vn, ds, NT, P_SCANB) for vn, ds in zip(vnl, dss)]
            dsq = [_dot(q_ref[rs, sl] * jnp.exp(g), d, TN, P_SCANB) for sl, g, d in zip(sls, gcols, dos)]
            dvns = [a + b for a, b in zip(dvn1, dvn2)]
            dws = [_dot(dvn, s, NT, P_SCANB) for dvn, s in zip(dvns, ss)]
            dsw = [_dot(w_ref[rs, sl], dvn, TN, P_SCANB) for sl, dvn in zip(sls, dvns)]
            dgam = jnp.zeros((1, DH), F32)
            for h, sl in enumerate(sls):
                dqg_ref[rs, sl] = dqgs[h]
                dp_ref[ci, h] = dps[h]
                du_ref[rs, sl] = dvns[h]
                dw_ref[rs, sl] = -dws[h]
                dks_ref[rs, sl] = dkss[h]
                tot = jnp.sum(jnp.sum(dss[h] * ss[h], axis=-1, keepdims=True), axis=0, keepdims=True)
                dgam = dgam + jnp.where(lane == h, tot, 0.0)
            dgam_ref[ci] = jnp.broadcast_to(dgam, (8, DH))
            dss = [ds * jnp.exp(gl) + a - b for ds, gl, a, b in zip(dss, glasts, dsq, dsw)]
        for h in hs:
            ds_scr[h] = dss[h]

    row = pl.BlockSpec((cps * CH, GW), lambda c: (rev(c), 0))
    sq = pl.BlockSpec((cps, HEADS, CH, CH), lambda c: (rev(c), 0, 0, 0))
    big = jax.ShapeDtypeStruct((n, GW), F32)
    return pl.pallas_call(
        body, name="gdn_scan_bwd", grid=(nch // cps,),
        in_specs=[row, row, pl.BlockSpec((cps * CH, DH), lambda c: (rev(c), 0)), row, sq, row,
                  pl.BlockSpec((cps, DH, GW), lambda c: (rev(c), 0, 0)), row],
        out_specs=[row, sq, row, row, row, pl.BlockSpec((cps, 8, DH), lambda c: (rev(c), 0, 0))],
        out_shape=[big, jax.ShapeDtypeStruct((nch, HEADS, CH, CH), F32), big, big, big,
                   jax.ShapeDtypeStruct((nch, 8, DH), F32)],
        scratch_shapes=[pltpu.VMEM((HEADS, DH, DH), F32)],
        compiler_params=_params(("arbitrary",)),
    )(q, k, bg, w, p, vn, s_in, do)


def _gdn_intra_bwd(q, k, v, bg, bgt, t, u, w, p, dqg, dp, du, dw, dks, dgam):
    n = q.shape[0]
    nch = n // CH
    cps = 1

    def body(q_ref, k_ref, v_ref, bg_ref, bgt_ref, t_ref, u_ref, w_ref, p_ref,
             dqg_ref, dp_ref, du_ref, dw_ref, dks_ref, dgam_ref, dq_ref, dk_ref, dv_ref, dbg_ref):
        i, j = _ij()
        rows1 = lax.broadcasted_iota(jnp.int32, (CH, 1), 0)
        lane = _lane((CH, DH))
        rsum = lambda x: jnp.sum(x, axis=-1, keepdims=True)
        items = [(ci, h) for ci in range(cps) for h in range(HEADS)]
        at = lambda ref, it: ref.at[it[0] * CH:(it[0] + 1) * CH, it[1] * DH:(it[1] + 1) * DH]
        ld = lambda ref: [at(ref, it)[...] for it in items]
        bgs = [bg_ref[ci * CH:(ci + 1) * CH, :] for ci in range(cps)]
        qs, ks = ld(q_ref), ld(k_ref)
        vecs = [_head_vectors(bgs[ci], bgt_ref[ci], h) for ci, h in items]
        decs = [_decay(gcol, grow) for _, gcol, grow in vecs]
        ths = [t_ref[ci, h] for ci, h in items]
        drus = [_dot(th, x_, TN, P_BWD) for th, x_ in zip(ths, ld(du_ref))]
        drws = [_dot(th, x_, TN, P_BWD) for th, x_ in zip(ths, ld(dw_ref))]
        kks = [_dot(kh, kh, NT, P_GRAM) for kh in ks]
        da1 = [_dot(dru, x_, NT, P_BWD) for dru, x_ in zip(drus, ld(u_ref))]
        da2 = [_dot(drw, x_, NT, P_BWD) for drw, x_ in zip(drws, ld(w_ref))]
        das = [jnp.where(i > j, -(x_ + y_), 0.0) for x_, y_ in zip(da1, da2)]
        dkks = [da * bcol * dec for da, (bcol, _, _), dec in zip(das, vecs, decs)]
        dps = [dp_ref[ci, h] for ci, h in items]
        dqks = [dp_ * dec for dp_, dec in zip(dps, decs)]
        dq_ps = [_dot(dqk, kh, NN, P_BWD) for dqk, kh in zip(dqks, ks)]
        dk_ps = [_dot(dqk, qh, TN, P_BWD) for dqk, qh in zip(dqks, qs)]
        dk_as = [_dot(dkk, kh, NN, P_BWD) for dkk, kh in zip(dkks, ks)]
        dk_bs = [_dot(dkk, kh, TN, P_BWD) for dkk, kh in zip(dkks, ks)]
        bcols = [vc[0] for vc in vecs]
        gcols = [vc[1] for vc in vecs]
        gams = [jnp.exp(g) for g in gcols]
        glasts = [g[CH - 1:CH, :] for g in gcols]
        es = [jnp.exp(gl - g) for gl, g in zip(glasts, gcols)]
        kgs = [kh * gam for kh, gam in zip(ks, gams)]
        dqgs, dkss = ld(dqg_ref), ld(dks_ref)
        r_uv = [rsum(dru * x_) for dru, x_ in zip(drus, ld(v_ref))]
        r_wk = [rsum(drw * kg) for drw, kg in zip(drws, kgs)]
        r_ak = [rsum(da * kk * dec) for da, kk, dec in zip(das, kks, decs)]
        r_qq = [rsum(dqg * qh) for dqg, qh in zip(dqgs, qs)]
        tks = [rsum(dk_ * kh) * e for dk_, kh, e in zip(dkss, ks, es)]
        mdecs = [da * (bcol * kk * dec) + dp_ * p_ref[ci, h]
                 for (ci, h), da, bcol, kk, dec, dp_ in zip(items, das, bcols, kks, decs, dps)]
        r_md = [rsum(m) for m in mdecs]
        c_md = [rsum(jnp.where(i == j, jnp.sum(m, axis=0, keepdims=True), 0.0)) for m in mdecs]
        dbgs = [jnp.zeros((CH, DH), F32) for _ in range(cps)]
        for n_, (ci, h) in enumerate(items):
            at(dv_ref, (ci, h))[...] = bcols[n_] * drus[n_]
            at(dq_ref, (ci, h))[...] = gams[n_] * dqgs[n_] + dq_ps[n_]
            at(dk_ref, (ci, h))[...] = ((bcols[n_] * gams[n_]) * drws[n_] + dk_ps[n_] + dk_as[n_] + dk_bs[n_]
                                        + dkss[n_] * es[n_])
            dbeta = r_uv[n_] + r_wk[n_] + r_ak[n_]
            dglast = (jnp.sum(tks[n_], axis=0, keepdims=True)
                      + dgam_ref[ci, 0:1, h:h + 1] * jnp.exp(glasts[n_]))
            dgc = (r_wk[n_] * bcols[n_] + r_md[n_] - c_md[n_] + r_qq[n_] * gams[n_] - tks[n_]
                   + jnp.where(rows1 == CH - 1, dglast, 0.0))
            dbgs[ci] = dbgs[ci] + jnp.where(lane == h, dbeta, 0.0) + jnp.where(lane == HEADS + h, dgc, 0.0)
        for ci in range(cps):
            dbg_ref[ci * CH:(ci + 1) * CH, :] = dbgs[ci]

    row = pl.BlockSpec((cps * CH, GW), lambda c: (c, 0))
    sq = pl.BlockSpec((cps, HEADS, CH, CH), lambda c: (c, 0, 0, 0))
    small = pl.BlockSpec((cps * CH, DH), lambda c: (c, 0))
    big = jax.ShapeDtypeStruct((n, GW), F32)
    return pl.pallas_call(
        body, name="gdn_intra_bwd", grid=(nch // cps,),
        in_specs=[row, row, row, small, pl.BlockSpec((cps, DH, CH), lambda c: (c, 0, 0)), sq, row, row, sq,
                  row, sq, row, row, row, pl.BlockSpec((cps, 8, DH), lambda c: (c, 0, 0))],
        out_specs=[row, row, row, small],
        out_shape=[big, big, big, jax.ShapeDtypeStruct((n, DH), F32)],
        compiler_params=_params(("parallel",)),
    )(q, k, v, bg, bgt, t, u, w, p, dqg, dp, du, dw, dks, dgam)


def _local_step(x, tgt, h, w_g, cqw, late, norm_in_w, ad, gdn_norm_w, conv_b, final_norm_w,
                on_grad_c=None, on_grad_g=None, on_q=None):
    proj_g = _matmul(h, w_g, NT, F32, 512, 1408, 1024, "mm_proj_g", n=GW_COLS, b_outer=True)
    q, k, v = _prep_qkv(proj_g, cqw)
    if on_q is not None:
        q = on_q(q)
    bg, bgt = _prep_bg(proj_g, ad)
    u, w, p, t = _gdn_intra(q, k, v, bg, bgt)
    o, vn, s_in = _gdn_scan(q, k, bg, u, w, p)
    w_c, w_out, conv_w = late(o)
    proj_c = _matmul(h, w_c, NT, F32, 512, 1024, 1024, "mm_proj_c", n=CW_COLS, b_outer=True)
    mix = _conv_branch(proj_c, conv_w, conv_b, _gdn_out(o, proj_g, gdn_norm_w))
    out = _matmul(mix, w_out, NN, F32, 512, 512, 2048, "mm_out", add=x)
    dout, dout_b, g_fn, loss = _final_loss(out, tgt, final_norm_w)

    dmix = _matmul(dout_b, w_out, NT, F32, 512, 1024, 1024, "mm_dmix", b_outer=True)
    g_wout = _matmul(mix, dout_b, TN, BF16, 512, 512, 2048, "mm_gwout")
    do, dproj_g, g_gn = _gdn_out_bwd(o, proj_g, gdn_norm_w, dmix)
    dproj_c, g_cw, g_cb = _conv_branch_bwd(proj_c, conv_w, conv_b, dmix)
    g_c = _matmul(dproj_c, h, TN, BF16, 1024, 512, 2048, "mm_gwin_c")
    if on_grad_c is not None:
        do = on_grad_c(g_c, g_wout, do)
    dqg, dp, du, dw, dks, dgam = _gdn_scan_bwd(q, k, bg, w, p, vn, s_in, do)
    dq, dk, dv, dbg = _gdn_intra_bwd(q, k, v, bg, bgt, t, u, w, p, dqg, dp, du, dw, dks, dgam)
    dproj_g, gq, gk, gv = _prep_qkv_bwd(proj_g, cqw, dq, dk, dv, dproj_g)
    dproj_g, g_al, g_dt = _prep_bg_bwd(proj_g, ad, dbg, dproj_g)
    g_g = _matmul(dproj_g, h, TN, BF16, 1408, 512, 2048, "mm_gwin_g")
    if on_grad_g is not None:
        dproj_g = on_grad_g(g_g, dproj_g)
    dh = _matmul(dproj_g, w_g, NN, F32, 1024, 1024, 1408, "mm_dh_g")
    dh = _matmul(dproj_c, w_c, NN, F32, 1024, 1024, 1024, "mm_dh_c", add=dh)
    gx, g_nin = _rms_in_bwd(x, norm_in_w, dh, dout)
    small = dict(nin=g_nin, cb=g_cb, fn=g_fn, al=g_al, dt=g_dt, gn=g_gn, cq=(gq, gk, gv), cw=g_cw, loss=loss)
    return gx, small, (g_g, g_c, g_wout)


def _place():
    x, y, c = lax.axis_index("x"), lax.axis_index("y"), lax.axis_index("c")
    chips = [(1 - x, y), (x, 1 - y), (1 - x, 1 - y)]
    return x, y, c, chips


def _blk(ref, b):
    if isinstance(b, int):
        return ref.at[b * DH:(b + 1) * DH, :]
    return ref.at[pl.ds(pl.multiple_of(b * DH, DH), DH), :]


HBM = pl.BlockSpec(memory_space=pltpu.HBM)
SEM = pl.BlockSpec(memory_space=pltpu.SEMAPHORE)
EFFECT = pltpu.SideEffectType.DATAFLOW_SIDE_EFFECTING


def _split_start(name, issue, bufs, n_sems):
    nbuf = len(bufs)

    def body(*refs):
        issue(refs[:nbuf], refs[nbuf], refs[nbuf + 1])
        refs[-1][...] = jnp.zeros_like(refs[-1])

    out = pl.pallas_call(
        body, name=name,
        out_shape=(pltpu.SemaphoreType.DMA((n_sems,)), pltpu.SemaphoreType.DMA((n_sems,)),
                   *[pltpu.HBM(b.shape, b.dtype) for b in bufs], jax.ShapeDtypeStruct((8, DH), F32)),
        in_specs=[HBM] * nbuf,
        out_specs=(SEM, SEM, *[HBM] * nbuf, pl.BlockSpec(memory_space=pltpu.VMEM)),
        input_output_aliases={a: 2 + a for a in range(nbuf)},
        compiler_params=pltpu.CompilerParams(has_side_effects=EFFECT),
    )(*[pltpu.with_memory_space_constraint(b, pltpu.HBM) for b in bufs])
    return out[0], out[1], list(out[2:2 + nbuf]), out[-1]


def _split_wait(name, await_, send_sems, recv_sems, bufs, after):
    nbuf = len(bufs)
    after = list(after) if isinstance(after, (list, tuple)) else [after]

    def body(*refs):
        await_(refs[:nbuf], refs[nbuf], refs[nbuf + 1])

    out = pl.pallas_call(
        body, name=name,
        out_shape=tuple(pltpu.HBM(b.shape, b.dtype) for b in bufs),
        in_specs=[HBM] * nbuf + [SEM, SEM] + [ANY] * len(after), out_specs=tuple([HBM] * nbuf),
        input_output_aliases={a: a for a in range(nbuf)},
        compiler_params=pltpu.CompilerParams(has_side_effects=EFFECT),
    )(*bufs, send_sems, recv_sems, *after)
    return list(out)


def _phase_blocks(chip, phase, edges, parity=None):
    return [(b, blk) for b, (grp, blk) in enumerate(_shard_blocks(chip, edges))
            if grp == phase and (parity is None or b % 2 == parity)]


def _cols(ref, nblk):
    return ref.at[0:nblk * DH, :]


def _block_table(chip, edges, spare_g, spare_c):
    rows = []
    for s in range(4):
        sb = _shard_blocks(s, edges)
        rows.append([[blk if grp == "g" else spare_g for grp, blk in sb],
                     [blk if grp == "c" else spare_c for grp, blk in sb],
                     [int(grp == "g") for grp, _ in sb], [s] * ALIGNED_BLOCKS])
    return jnp.asarray(rows, jnp.int32)[chip]


def _place_own(a_shard, wo, cq, cw, bufs):
    d = a_shard.shape[1]
    chip = 2 * lax.axis_index("x") + lax.axis_index("y")

    def body(t_ref, a_ref, wo_ref, cq_ref, cw_ref, *refs):
        wg_ref, wc_ref, wog_ref, cqg_ref, cwg_ref = refs[5:]
        wg_ref[...] = a_ref[...]
        wc_ref[...] = a_ref[...]

        @pl.when(pl.program_id(0) == 0)
        def _():
            wog_ref[0] = wo_ref[...]
            cqg_ref[0] = cq_ref[...]
            cwg_ref[0] = cw_ref[...]

    whole = lambda s: pl.BlockSpec(s.shape, lambda b, t: (0,) * s.ndim)
    slot = lambda s: pl.BlockSpec((1,) + s.shape, lambda b, t: (t[3, 0],) + (0,) * s.ndim)
    return pl.pallas_call(
        body, name="place_own",
        grid_spec=pltpu.PrefetchScalarGridSpec(
            num_scalar_prefetch=1, grid=(ALIGNED_BLOCKS,),
            in_specs=[pl.BlockSpec((DH, d), lambda b, t: (b, 0)), whole(wo), whole(cq), whole(cw)] + [ANY] * 5,
            out_specs=[pl.BlockSpec((DH, d), lambda b, t: (t[0, b], 0)),
                       pl.BlockSpec((DH, d), lambda b, t: (t[1, b], 0)), slot(wo), slot(cq), slot(cw)]),
        out_shape=[jax.ShapeDtypeStruct(b.shape, b.dtype) for b in bufs],
        input_output_aliases={5 + a: a for a in range(5)},
        compiler_params=_params(("arbitrary",)),
    )(_block_table(chip, True, G_SPARE, C_SPARE), a_shard, wo, cq, cw, *bufs)


def _tie(x, token, name):
    def body(x_ref, t_ref, o_ref):
        del x_ref, t_ref, o_ref

    return pl.pallas_call(
        body, name=name, in_specs=[ANY, ANY], out_specs=ANY,
        out_shape=jax.ShapeDtypeStruct(x.shape, x.dtype), input_output_aliases={0: 0},
    )(x, token)


def _gather_start(phase, a_shard, w_grp, singles):
    ns = len(singles)

    def issue(refs, send_sems, recv_sems):
        a_ref, w_ref = refs[0], refs[1]
        x, y, c, chips = _place()
        mine = 2 * x + y
        for jj, (px, py) in enumerate(chips):
            to = dict(device_id=(px, py, c), device_id_type=MESH)
            for a in range(ns):
                pltpu.make_async_remote_copy(
                    src_ref=refs[2 + 2 * a], dst_ref=refs[3 + 2 * a].at[mine],
                    send_sem=send_sems.at[(1 + ns) * jj + 1 + a], recv_sem=recv_sems.at[(1 + ns) * jj + 1 + a],
                    **to).start()
        for s in range(4):
            for par in range(2):
                blocks = _phase_blocks(s, phase, True, par)
                if blocks:
                    @pl.when((mine == s) & (c == par))
                    def _():
                        for jj, (px, py) in enumerate(chips):
                            for b, blk in blocks:
                                pltpu.make_async_remote_copy(
                                    src_ref=_blk(a_ref, b), dst_ref=_blk(w_ref, blk),
                                    send_sem=send_sems.at[(1 + ns) * jj], recv_sem=recv_sems.at[(1 + ns) * jj],
                                    device_id=(px, py, c), device_id_type=MESH).start()

    bufs = [a_shard, w_grp] + [t for pair in singles for t in pair]
    return _split_start("gather_start_" + phase, issue, bufs, 3 * (1 + ns))


def _gather_wait(phase, send_sems, recv_sems, bufs, after):
    ns = (len(bufs) - 2) // 2

    def await_(refs, send_sems, recv_sems):
        a_ref, w_ref = refs[0], refs[1]
        x, y, c, chips = _place()
        mine = 2 * x + y
        for jj, (px, py) in enumerate(chips):
            to = dict(device_id=(px, py, c), device_id_type=MESH)
            peer = 2 * px + py
            for a in range(ns):
                cp = pltpu.make_async_remote_copy(
                    src_ref=refs[2 + 2 * a], dst_ref=refs[3 + 2 * a].at[mine],
                    send_sem=send_sems.at[(1 + ns) * jj + 1 + a], recv_sem=recv_sems.at[(1 + ns) * jj + 1 + a], **to)
                cp.wait_recv()
                cp.wait_send()
            for s in range(4):
                for par in range(2):
                    nblk = len(_phase_blocks(s, phase, True, par))
                    if nblk:
                        both = pltpu.make_async_remote_copy(
                            src_ref=_cols(a_ref, nblk), dst_ref=_cols(w_ref, nblk),
                            send_sem=send_sems.at[(1 + ns) * jj], recv_sem=recv_sems.at[(1 + ns) * jj], **to)

                        @pl.when((peer == s) & (c == par))
                        def _():
                            both.wait_recv()

                        @pl.when((mine == s) & (c == par))
                        def _():
                            both.wait_send()

    return _split_wait("gather_wait_" + phase, await_, send_sems, recv_sems, bufs, after)


def _sibling_forward_parts(phase):
    def each(w_ref, send_sems, recv_sems, start):
        x, y, c, chips = _place()
        to = dict(device_id=(x, y, 1 - c), device_id_type=MESH)
        for jj, (px, py) in enumerate(chips):
            peer = 2 * px + py
            for s in range(4):
                for par in range(2):
                    mine_blocks = _phase_blocks(s, phase, True, par)
                    theirs = len(_phase_blocks(s, phase, True, 1 - par))
                    if not (mine_blocks or theirs):
                        continue

                    @pl.when((peer == s) & (c == par))
                    def _():
                        if start:
                            for _, blk in mine_blocks:
                                pltpu.make_async_remote_copy(
                                    src_ref=_blk(w_ref, blk), dst_ref=_blk(w_ref, blk),
                                    send_sem=send_sems.at[jj], recv_sem=recv_sems.at[jj], **to).start()
                            return
                        if theirs:
                            pltpu.make_async_remote_copy(
                                src_ref=_cols(w_ref, theirs), dst_ref=_cols(w_ref, theirs),
                                send_sem=send_sems.at[jj], recv_sem=recv_sems.at[jj], **to).wait_recv()
                        if mine_blocks:
                            pltpu.make_async_remote_copy(
                                src_ref=_cols(w_ref, len(mine_blocks)), dst_ref=_cols(w_ref, len(mine_blocks)),
                                send_sem=send_sems.at[jj], recv_sem=recv_sems.at[jj], **to).wait_send()

    issue = lambda refs, send_sems, recv_sems: each(refs[0], send_sems, recv_sems, True)
    await_ = lambda refs, send_sems, recv_sems: each(refs[0], send_sems, recv_sems, False)
    return issue, await_


def _sibling_forward(phase, w_grp):
    issue, await_ = _sibling_forward_parts(phase)

    def body(w_in_ref, w_ref, send_sems, recv_sems):
        del w_in_ref
        issue([w_ref], send_sems, recv_sems)
        await_([w_ref], send_sems, recv_sems)

    return pl.pallas_call(
        body, name="sibling_forward_" + phase, in_specs=[ANY], out_specs=ANY,
        out_shape=jax.ShapeDtypeStruct(w_grp.shape, w_grp.dtype), input_output_aliases={0: 0},
        scratch_shapes=[pltpu.SemaphoreType.DMA((3,)), pltpu.SemaphoreType.DMA((3,))],
    )(w_grp)


def _merge_edges(w, edge0, mixed, name):
    d = w.shape[1]

    def body(e_ref, o_ref):
        o_ref[...] = e_ref[0:DH, :] + e_ref[DH:2 * DH, :]

    def to_block(i):
        r = mixed[-1]
        for kk in range(len(mixed) - 2, -1, -1):
            r = jnp.where(i == kk, mixed[kk], r)
        return r

    return pl.pallas_call(
        body, name=name, grid=(len(mixed),),
        in_specs=[pl.BlockSpec((2 * DH, d), lambda i: (edge0 // 2 + i, 0))],
        out_specs=pl.BlockSpec((DH, d), lambda i: (to_block(i), 0)),
        out_shape=jax.ShapeDtypeStruct(w.shape, w.dtype),
        input_output_aliases={0: 0},
        compiler_params=_params(("arbitrary",)),
    )(w)


def _scatter_start(phase, g_grp, land, singles, halved=False):
    ns = len(singles)

    def issue(refs, send_sems, recv_sems):
        g_ref, land_ref = refs[0], refs[1]
        x, y, c, chips = _place()
        for jj, (px, py) in enumerate(chips):
            to = dict(device_id=(px, py, c), device_id_type=MESH)
            peer = 2 * px + py
            for a in range(ns):
                pltpu.make_async_remote_copy(
                    src_ref=refs[2 + 2 * a].at[peer], dst_ref=refs[3 + 2 * a].at[jj],
                    send_sem=send_sems.at[(1 + ns) * jj + 1 + a], recv_sem=recv_sems.at[(1 + ns) * jj + 1 + a],
                    **to).start()
            for s in range(4):
                for par in ((0, 1) if halved else (None,)):
                    blocks = _phase_blocks(s, phase, False, par)
                    if blocks:
                        @pl.when((peer == s) if par is None else ((peer == s) & (c == par)))
                        def _():
                            for b, blk in blocks:
                                pltpu.make_async_remote_copy(
                                    src_ref=_blk(g_ref, blk), dst_ref=_blk(land_ref.at[jj], b),
                                    send_sem=send_sems.at[(1 + ns) * jj], recv_sem=recv_sems.at[(1 + ns) * jj],
                                    **to).start()

    bufs = [g_grp, land] + [t for pair in singles for t in pair]
    return _split_start("scatter_start_" + phase, issue, bufs, 3 * (1 + ns))


def _scatter_wait(phase, send_sems, recv_sems, bufs, after, halved=False):
    ns = (len(bufs) - 2) // 2

    def await_(refs, send_sems, recv_sems):
        g_ref, land_ref = refs[0], refs[1]
        x, y, c, chips = _place()
        mine = 2 * x + y
        for jj, (px, py) in enumerate(chips):
            to = dict(device_id=(px, py, c), device_id_type=MESH)
            peer = 2 * px + py
            for a in range(ns):
                cp = pltpu.make_async_remote_copy(
                    src_ref=refs[2 + 2 * a].at[peer], dst_ref=refs[3 + 2 * a].at[jj],
                    send_sem=send_sems.at[(1 + ns) * jj + 1 + a], recv_sem=recv_sems.at[(1 + ns) * jj + 1 + a], **to)
                cp.wait_recv()
                cp.wait_send()
            for s in range(4):
                for par in ((0, 1) if halved else (None,)):
                    nblk = len(_phase_blocks(s, phase, False, par))
                    if nblk:
                        both = pltpu.make_async_remote_copy(
                            src_ref=_cols(g_ref, nblk), dst_ref=_cols(land_ref.at[jj], nblk),
                            send_sem=send_sems.at[(1 + ns) * jj], recv_sem=recv_sems.at[(1 + ns) * jj], **to)

                        @pl.when((mine == s) if par is None else ((mine == s) & (c == par)))
                        def _():
                            both.wait_recv()

                        @pl.when((peer == s) if par is None else ((peer == s) & (c == par)))
                        def _():
                            both.wait_send()

    return _split_wait("scatter_wait_" + phase, await_, send_sems, recv_sems, bufs, after)


def _needed_blocks(phase, parity):
    return sorted({blk for s in range(4) for _, blk in _phase_blocks(s, phase, False, parity)})


def _pair_reduce(phase, g_grp):
    n, d = g_grp.shape

    def swap(g_ref, sib_ref, send_sem, recv_sem):
        x, y, c, _ = _place()
        to = dict(device_id=(x, y, 1 - c), device_id_type=MESH)
        for par in range(2):
            give, get = _needed_blocks(phase, 1 - par), _needed_blocks(phase, par)

            @pl.when(c == par)
            def _():
                for blk in give:
                    pltpu.make_async_remote_copy(src_ref=_blk(g_ref, blk), dst_ref=_blk(sib_ref, blk),
                                                 send_sem=send_sem, recv_sem=recv_sem, **to).start()
                pltpu.make_async_remote_copy(src_ref=_cols(g_ref, len(get)), dst_ref=_cols(sib_ref, len(get)),
                                             send_sem=send_sem, recv_sem=recv_sem, **to).wait_recv()
                pltpu.make_async_remote_copy(src_ref=_cols(g_ref, len(give)), dst_ref=_cols(sib_ref, len(give)),
                                             send_sem=send_sem, recv_sem=recv_sem, **to).wait_send()

    sib = pl.pallas_call(
        swap, name="pair_swap_" + phase, in_specs=[ANY], out_specs=ANY,
        out_shape=jax.ShapeDtypeStruct((n, d), g_grp.dtype),
        scratch_shapes=[pltpu.SemaphoreType.DMA, pltpu.SemaphoreType.DMA],
    )(*_in_hbm(g_grp))

    lists = [_needed_blocks(phase, par) for par in range(2)]
    longest = max(len(t) for t in lists)
    table = jnp.asarray([t + [t[-1]] * (longest - len(t)) for t in lists], jnp.int32)[lax.axis_index("c")]

    def add(t_ref, a_ref, b_ref, o_ref):
        o_ref[...] = (a_ref[...].astype(F32) + b_ref[...].astype(F32)).astype(o_ref.dtype)

    blk = pl.BlockSpec((DH, d), lambda i, t: (t[i], 0))
    return pl.pallas_call(
        add, name="pair_add_" + phase,
        grid_spec=pltpu.PrefetchScalarGridSpec(num_scalar_prefetch=1, grid=(longest,),
                                               in_specs=[blk, blk], out_specs=blk),
        out_shape=jax.ShapeDtypeStruct((n, d), g_grp.dtype),
        compiler_params=_params(("arbitrary",)),
    )(table, g_grp, sib)


def _sum_shard(g_g, g_c, land):
    d = g_g.shape[1]
    chip = 2 * lax.axis_index("x") + lax.axis_index("y")

    def body(t_ref, gg_ref, gc_ref, land_ref, o_ref):
        b = pl.program_id(0)
        in_g = t_ref[2, b] == 1
        own = jnp.where(in_g, gg_ref[...].astype(F32), gc_ref[...].astype(F32))
        for jj in range(3):
            own = own + land_ref[jj].astype(F32)
        o_ref[...] = jnp.where(in_g & (b % 2 != lax.axis_index("c")), 0.0, own)

    return pl.pallas_call(
        body, name="sum_w_in",
        grid_spec=pltpu.PrefetchScalarGridSpec(
            num_scalar_prefetch=1, grid=(ALIGNED_BLOCKS,),
            in_specs=[pl.BlockSpec((DH, d), lambda b, t: (t[0, b], 0)), pl.BlockSpec((DH, d), lambda b, t: (t[1, b], 0)),
                      pl.BlockSpec((3, DH, d), lambda b, t: (0, b, 0))],
            out_specs=pl.BlockSpec((DH, d), lambda b, t: (b, 0))),
        out_shape=jax.ShapeDtypeStruct((ALIGNED_W, d), F32),
        compiler_params=_params(("arbitrary",)),
    )(_block_table(chip, False, 0, 0), g_g, g_c, land)


def _sum_rows(stack, land, rows):
    _, r, d = stack.shape
    rows = min(rows, r)
    chip = 2 * lax.axis_index("x") + lax.axis_index("y")

    def body(t_ref, own_ref, land_ref, o_ref):
        acc = own_ref[0].astype(F32)
        for jj in range(3):
            acc = acc + land_ref[jj].astype(F32)
        o_ref[...] = acc

    return pl.pallas_call(
        body, name="sum_w_out",
        grid_spec=pltpu.PrefetchScalarGridSpec(
            num_scalar_prefetch=1, grid=(r // rows,),
            in_specs=[pl.BlockSpec((1, rows, d), lambda i, t: (t[0], i, 0)),
                      pl.BlockSpec((3, rows, d), lambda i, t: (0, i, 0))],
            out_specs=pl.BlockSpec((rows, d), lambda i, t: (i, 0))),
        out_shape=jax.ShapeDtypeStruct((r, d), F32),
        compiler_params=_params(("arbitrary",)),
    )(jnp.reshape(chip, (1,)).astype(jnp.int32), stack, land)


def _final_exchange(parts, pack):
    npart = len(parts)

    def body(*refs):
        ins, pack_ref = refs[:npart], refs[npart]
        outs, packs = refs[npart + 1:2 * npart + 1], refs[2 * npart + 1]
        send_sems, recv_sems, psend, precv, loc_sem = refs[2 * npart + 2:]
        x, y, c, _ = _place()
        me = 4 * x + 2 * y + c
        local = pltpu.make_async_copy(pack_ref, packs.at[me], loc_sem)
        local.start()
        cps = [pltpu.make_async_remote_copy(
            src_ref=ins[a], dst_ref=outs[a], send_sem=send_sems.at[a], recv_sem=recv_sems.at[a],
            device_id=(x, y, 1 - c), device_id_type=MESH) for a in range(npart)]
        for r in range(1, 8):
            dx, dy, dc = (r >> 2) & 1, (r >> 1) & 1, r & 1
            peer = (x + dx - 2 * x * dx, y + dy - 2 * y * dy, c + dc - 2 * c * dc)
            cps.append(pltpu.make_async_remote_copy(
                src_ref=pack_ref, dst_ref=packs.at[me], send_sem=psend.at[r - 1], recv_sem=precv.at[r - 1],
                device_id=peer, device_id_type=MESH))
        for cp in cps:
            cp.start()
        for cp in cps:
            cp.wait_recv()
        for cp in cps:
            cp.wait_send()
        local.wait()

    return pl.pallas_call(
        body, name="final_exchange",
        in_specs=[ANY] * (npart + 1), out_specs=[ANY] * (npart + 1),
        out_shape=[jax.ShapeDtypeStruct(p.shape, p.dtype) for p in parts]
        + [jax.ShapeDtypeStruct((8,) + pack.shape, pack.dtype)],
        scratch_shapes=[pltpu.SemaphoreType.DMA((npart,)), pltpu.SemaphoreType.DMA((npart,)),
                        pltpu.SemaphoreType.DMA((7,)), pltpu.SemaphoreType.DMA((7,)), pltpu.SemaphoreType.DMA],
    )(*parts, pack)


def _sibling_swap(part):
    def body(in_ref, out_ref, send_sem, recv_sem):
        x, y, c, _ = _place()
        cp = pltpu.make_async_remote_copy(src_ref=in_ref, dst_ref=out_ref, send_sem=send_sem, recv_sem=recv_sem,
                                          device_id=(x, y, 1 - c), device_id_type=MESH)
        cp.start()
        cp.wait_recv()
        cp.wait_send()

    return pl.pallas_call(
        body, name="sibling_swap", in_specs=[ANY], out_specs=ANY,
        out_shape=jax.ShapeDtypeStruct(part.shape, part.dtype),
        scratch_shapes=[pltpu.SemaphoreType.DMA, pltpu.SemaphoreType.DMA],
    )(part)


def _sum_packs(packs):
    def body(p_ref, o_ref):
        acc = p_ref[0]
        for d in range(1, 8):
            acc = acc + p_ref[d]
        o_ref[...] = acc

    return pl.pallas_call(
        body, name="sum_packs", out_shape=jax.ShapeDtypeStruct(packs.shape[1:], F32),
    )(packs)


def _adamw_update(g, w_ref, m_ref, v_ref, go, do, mo, vo):
    c1 = 1.0 / (1.0 - ADAM_B1 ** ADAM_STEP)
    c2 = 1.0 / (1.0 - ADAM_B2 ** ADAM_STEP)
    mn = ADAM_B1 * m_ref[...] + (1.0 - ADAM_B1) * g
    vn = ADAM_B2 * v_ref[...] + (1.0 - ADAM_B2) * (g * g)
    go[...] = g
    mo[...] = mn
    vo[...] = vn
    do[...] = -ADAM_LR * ((mn * c1) / (jnp.sqrt(vn * c2) + ADAM_EPS) + ADAM_WD * w_ref[...])


def _adamw(w, m, v, g1, g2, rows, name):
    r, cdim = w.shape
    rows = min(rows, r)

    def body(*refs):
        n_in = 4 if g2 is None else 5
        w_ref, m_ref, v_ref, g_ref = refs[:4]
        g = g_ref[...] if g2 is None else g_ref[...] + refs[4][...]
        _adamw_update(g, w_ref, m_ref, v_ref, *refs[n_in:n_in + 4])

    blk = pl.BlockSpec((rows, cdim), lambda i: (i, 0))
    args = [w, m, v, g1] + ([] if g2 is None else [g2])
    shp = jax.ShapeDtypeStruct((r, cdim), F32)
    return pl.pallas_call(
        body, name=name, grid=(r // rows,),
        in_specs=[blk] * len(args), out_specs=[blk] * 4, out_shape=[shp] * 4,
        compiler_params=_params(("parallel",), 20 * rows * cdim * 4 + 8 * 2**20),
    )(*_in_hbm(*args))


def _adamw_shard(wt, mt, vt, g1, g2):
    r, d = wt.shape
    cols = min(128, d)

    def body(w_ref, m_ref, v_ref, g_ref, g2_ref, go, do, mo, vo, pad_ref):
        chip = 2 * lax.axis_index("x") + lax.axis_index("y")
        back = [(ALIGNED_W - s) % ALIGNED_W for s in SHIFTS]
        pad_ref[...] = pltpu.roll(g_ref[...] + g2_ref[...], _by_chip(chip, back), 0)
        outs = [o.at[:, 0, :] for o in (go, do, mo, vo)]
        _adamw_update(pad_ref[0:r, :], w_ref, m_ref, v_ref, *outs)

    blk = pl.BlockSpec((r, cols), lambda i: (0, i))
    gblk = pl.BlockSpec((ALIGNED_W, cols), lambda i: (0, i))
    oblk = pl.BlockSpec((r, 1, cols), lambda i: (0, 0, i))
    shp = jax.ShapeDtypeStruct((r, 1, d), F32)
    return pl.pallas_call(
        body, name="adamw_w_in", grid=(d // cols,),
        in_specs=[blk] * 3 + [gblk] * 2, out_specs=[oblk] * 4, out_shape=[shp] * 4,
        scratch_shapes=[pltpu.VMEM((ALIGNED_W, cols), F32)],
        compiler_params=_params(("parallel",), 24 * ALIGNED_W * cols * 4 + 8 * 2**20),
    )(wt, mt, vt, g1, g2)


def _pad_lanes(a, width):
    return jnp.pad(a, ((0, 0), (0, width - a.shape[1])))


def _gathered_to_full(g):
    return jnp.transpose(g, (1, 0, 2)).reshape(g.shape[1], 4 * g.shape[2])


def _row(a):
    return _pad_lanes(a.reshape(1, -1), 1024)


def _small_pack(nin, cb, fn, al, dt, gn, cqw_shard, cw_shard):
    ad = jnp.concatenate([al.reshape(1, -1), dt.reshape(1, -1)], axis=1)
    rows = [_row(nin), _row(cb), _row(fn), _row(ad), _row(gn), cqw_shard.reshape(3, 1024), _row(cw_shard)]
    out = jnp.concatenate(rows, axis=0)
    return jnp.pad(out, ((0, 16 - out.shape[0]), (0, 0)))


def kernel(x, norm_in_w, w_in, conv_qkv_w, A_log, dt_bias, gdn_norm_w, conv_w, conv_b, w_out, final_norm_w, loss_target, m_norm_in_w, m_w_in, m_conv_qkv_w, m_A_log, m_dt_bias, m_gdn_norm_w, m_conv_w, m_conv_b, m_w_out, m_final_norm_w, v_norm_in_w, v_w_in, v_conv_qkv_w, v_A_log, v_dt_bias, v_gdn_norm_w, v_conv_w, v_conv_b, v_w_out, v_final_norm_w):
    chip = 2 * lax.axis_index("x") + lax.axis_index("y")
    a_shard = _align_shard(jnp.transpose(w_in, (2, 0, 1)))
    wo_b = _cast_bf16(w_out[0], 256, "cast_w_out")
    d_model = x.shape[-1]
    stack = lambda s: lax.empty((4,) + s.shape, s.dtype)
    wg0 = lax.empty((WG_BLOCKS * DH, d_model), BF16)
    wc0 = lax.empty((WC_BLOCKS * DH, d_model), BF16)
    ss_g, rs_g, bufs_g, tok_g = _gather_start("g", a_shard, wg0, [(conv_qkv_w[0], stack(conv_qkv_w[0]))])
    ss_c, rs_c, bufs_c, tok_c = _gather_start("c", bufs_g[0], wc0,
                                              [(conv_w[0], stack(conv_w[0])), (wo_b, stack(wo_b))])
    wg1, wc1, wog1, cqg1, cwg1 = _place_own(bufs_c[0], bufs_c[4], bufs_g[2], bufs_c[2],
                                            [bufs_g[1], bufs_c[1], bufs_c[5], bufs_g[3], bufs_c[3]])
    x0 = x[0]
    h = _rms_in(x0, _tie(_tie(norm_in_w, tok_g, "after_gather_start_g"), tok_c, "after_gather_start_c"))
    adam_in = [jnp.transpose(a[0]) for a in (w_in, m_w_in, v_w_in)]
    sp = lambda nin, cb, fn, al, dt, gn, cq, cwv: _small_pack(nin, cb, fn, al, dt, gn, cq[0], cwv[0])
    w_s = sp(norm_in_w, conv_b, final_norm_w, A_log, dt_bias, gdn_norm_w, conv_qkv_w, conv_w)
    m_s = sp(m_norm_in_w, m_conv_b, m_final_norm_w, m_A_log, m_dt_bias, m_gdn_norm_w, m_conv_qkv_w, m_conv_w)
    v_s = sp(v_norm_in_w, v_conv_b, v_final_norm_w, v_A_log, v_dt_bias, v_gdn_norm_w, v_conv_qkv_w, v_conv_w)
    a_thru, wg, _, cq_g = _gather_wait("g", ss_g, rs_g, [bufs_c[0], wg1, bufs_g[2], cqg1],
                                       [h, w_s, m_s, v_s] + adam_in[1:])
    w_g = _merge_edges(_sibling_forward("g", wg), G_EDGE, G_MIXED, "merge_edges_g")
    cqw = _gathered_to_full(cq_g)
    ad = jnp.pad(jnp.concatenate([A_log, dt_bias], axis=0), ((0, 0), (A_LANE, 0)))
    fwd_c = {}

    def on_q(q):
        _, wc, _, cw_g, _, wo_g = _gather_wait("c", ss_c, rs_c,
                                               [a_thru, wc1, bufs_c[2], cwg1, bufs_c[4], wog1], q)
        issue, _ = _sibling_forward_parts("c")
        ss, rs, (wc,), tok = _split_start("sibling_forward_start_c", issue, [wc], 3)
        fwd_c.update(ss=ss, rs=rs, wc=wc, cw_g=cw_g, wo_g=wo_g)
        return _tie(q, tok, "after_sibling_forward_start_c")

    def late(o):
        _, await_ = _sibling_forward_parts("c")
        (wc,) = _split_wait("sibling_forward_wait_c", await_, fwd_c["ss"], fwd_c["rs"], [fwd_c["wc"]], o)
        return (_merge_edges(wc, C_EDGE, C_MIXED, "merge_edges_c"), fwd_c["wo_g"].reshape(2 * GW, d_model),
                _gathered_to_full(fwd_c["cw_g"]))

    scat = {}

    def on_grad_c(g_c, g_wout, do):
        go4 = g_wout.reshape(4, GW // 2, d_model)
        land = lax.empty((3, ALIGNED_W, d_model), BF16)
        land_o = lax.empty((3, GW // 2, d_model), BF16)
        ss, rs, bufs, tok = _scatter_start("c", g_c, land, [(go4, land_o)])
        scat["c"] = (ss, rs, bufs)
        return _tie(do, tok, "after_scatter_start_c")

    def on_grad_g(g_g, dproj_g):
        ss, rs, bufs, tok = _scatter_start("g", _pair_reduce("g", g_g), scat["c"][2][1], [], halved=True)
        scat["g"] = (ss, rs, bufs)
        return _tie(dproj_g, tok, "after_scatter_start_g")

    gx, sm, _ = _local_step(x0, loss_target[0], h, w_g, cqw, late, norm_in_w, ad, gdn_norm_w, conv_b,
                            final_norm_w.reshape(1, -1), on_grad_c, on_grad_g, on_q)

    ss, rs, bufs = scat["c"]
    g_c, land, go4, land_o = _scatter_wait("c", ss, rs, [bufs[0], scat["g"][2][1], bufs[2], bufs[3]], gx)
    part_out = _sum_rows(go4, land_o, 128)
    ad_g = jnp.concatenate([sm["al"][:, A_LANE:], sm["dt"][:, A_LANE:]], axis=1)
    pack = jnp.concatenate([_row(sm["nin"]), _row(sm["cb"]), _row(sm["fn"]), _row(ad_g), _row(sm["gn"]),
                            jnp.concatenate(sm["cq"], axis=1).reshape(12, 1024), sm["cw"], _row(sm["loss"])], axis=0)
    pack = jnp.pad(pack, ((0, PACK_ROWS - pack.shape[0]), (0, 0)))
    sib_out, packs = _final_exchange([part_out], pack)
    tot = _sum_packs(packs)
    g_wo, d_wo, m_wo, v_wo = _adamw(w_out[0], m_w_out[0], v_w_out[0], part_out, sib_out, 128, "adamw_w_out")
    g_cq_sh = lax.dynamic_slice_in_dim(tot[R_CQ:R_CQ + 12].reshape(4, 3 * GW), chip * 768, 768, axis=1)
    g_cw_sh = lax.dynamic_slice_in_dim(tot[R_CW:R_CW + 3], chip * 256, 256, axis=1)
    g_s = _small_pack(tot[R_NIN], tot[R_CB], tot[R_FN], tot[R_AD, :HEADS], tot[R_AD, HEADS:2 * HEADS],
                      tot[R_GN, :DH], g_cq_sh, g_cw_sh)
    small = _adamw(w_s, m_s, v_s, g_s, None, 16, "adamw_small")

    ss, rs, bufs = scat["g"]
    g_g, land = _scatter_wait("g", ss, rs, [bufs[0], land], [small[0], d_wo], halved=True)
    part_in = _sum_shard(g_g, g_c, land)
    g_wi, d_wi, m_wi, v_wi = [jnp.transpose(a, (1, 2, 0))[0] for a in _adamw_shard(
        *adam_in, part_in, _sibling_swap(part_in))]

    def unpack(a, big_in, big_out):
        return (a[0:1], big_in[None], a[5:8].reshape(1, 4, 768), a[3:4, :HEADS], a[3:4, HEADS:2 * HEADS],
                a[4:5, :DH], a[8, :768].reshape(1, 3, 256), a[1:2], big_out[None], a[2])

    loss = tot[R_LOSS, 0]
    return (loss, gx[None], *unpack(small[0], g_wi, g_wo), *unpack(small[1], d_wi, d_wo),
            *unpack(small[2], m_wi, m_wo), *unpack(small[3], v_wi, v_wo))
```

```python
import functools
import math

import jax
import jax.numpy as jnp
from jax import lax
from jax.experimental import pallas as pl
from jax.experimental.pallas import tpu as pltpu

F32 = jnp.float32
BF16 = jnp.bfloat16
MESH = pl.DeviceIdType.MESH
ANY = pl.BlockSpec(memory_space=pl.ANY)

HEADS = 8
DH = 128
CH = 64
GW = HEADS * DH
EPS = 1e-6
VMEM_V7X = 64 * 1024 * 1024

QB, KB, VB, ZB, BAB = 0, 8, 16, 24, 32
A_LANE = 120
NG, NC = 33, 32
GW_COLS, CW_COLS = NG * DH, NC * DH

SHARD_W = 2052
ALIGNED_BLOCKS = 17
ALIGNED_W = ALIGNED_BLOCKS * DH
SHIFTS = (0, 4, ALIGNED_W - 8, ALIGNED_W - 4)
G_EDGE, C_EDGE = 34, 32
G_SPARE, C_SPARE = 33, 34
WG_BLOCKS, WC_BLOCKS = 38, 36
G_MIXED, C_MIXED = (2, BAB), (4 * 7 + 1,)


def _shard_blocks(chip, edges):
    g, c = "g", "c"
    if chip == 0:
        out = [(g, 3 * b) for b in range(8)] + [(g, 3 * b + 1) for b in range(8)] + [(g, G_EDGE, G_MIXED[0])]
    elif chip == 1:
        out = [(g, G_EDGE + 1, G_MIXED[0])] + [(g, 3 * b + 2) for b in range(1, 8)]
        out += [(g, ZB + b) for b in range(8)] + [(g, G_EDGE + 2, G_MIXED[1])]
    elif chip == 2:
        out = [(c, 4 * b) for b in range(8)] + [(c, 4 * b + 1) for b in range(7)]
        out += [(c, C_EDGE, C_MIXED[0]), (g, G_EDGE + 3, G_MIXED[1])]
    else:
        out = [(c, 4 * b + 2) for b in range(8)] + [(c, 4 * b + 3) for b in range(8)] + [(c, C_EDGE + 1, C_MIXED[0])]
    return [(o[0], o[1] if (edges or len(o) == 2) else o[2]) for o in out]


def _by_chip(chip, vals):
    if all(v == vals[0] for v in vals):
        return vals[0]
    r = vals[3]
    for kk in (2, 1, 0):
        r = jnp.where(chip == kk, vals[kk], r)
    return r

ADAM_LR, ADAM_B1, ADAM_B2, ADAM_EPS, ADAM_WD, ADAM_STEP = 0.001, 0.9, 0.999, 1e-08, 0.01, 10

R_NIN, R_CB, R_FN, R_AD, R_GN, R_CQ, R_CW, R_LOSS, PACK_ROWS = 0, 1, 2, 3, 4, 5, 17, 20, 24

NN = ((1,), (0,))
NT = ((1,), (1,))
TN = ((0,), (0,))


def _dot(a, b, dims=NN, mode="lo"):
    dn = (dims, ((), ()))
    if mode == "hi":
        return lax.dot_general(a, b, dn, precision=lax.Precision.HIGHEST, preferred_element_type=F32)
    ah, bh = a.astype(BF16), b.astype(BF16)
    out = lax.dot_general(ah, bh, dn, preferred_element_type=F32)
    if mode == "x3":
        al = (a - ah.astype(F32)).astype(BF16)
        bl = (b - bh.astype(F32)).astype(BF16)
        out = out + lax.dot_general(ah, bl, dn, preferred_element_type=F32)
        out = out + lax.dot_general(al, bh, dn, preferred_element_type=F32)
    return out


P_GRAM, P_INV, P_SOL, P_SCAN, P_SCANB, P_BWD = "lo", "lo", "lo", "lo", "lo", "lo"
P_CUM = "x3"


def _params(sem=None, vmem=None):
    kw = {}
    if sem is not None:
        kw["dimension_semantics"] = sem
    if vmem is not None:
        kw["vmem_limit_bytes"] = int(min(max(vmem, 32 * 2**20), VMEM_V7X - 8 * 2**20))
    return pltpu.CompilerParams(**kw)


def _in_hbm(*arrays):
    return [pltpu.with_memory_space_constraint(a, pltpu.HBM) for a in arrays]


def _sigmoid(x):
    return 1.0 / (1.0 + jnp.exp(-x))


def _dsilu(x, s):
    return s * (1.0 + x * (1.0 - s))


def _rows(shape):
    return lax.broadcasted_iota(jnp.int32, shape, 0)


def _shift_down(x, s):
    if s == 0:
        return x
    return jnp.where(_rows(x.shape) >= s, pltpu.roll(x, s, 0), 0.0)


def _shift_up(x, s):
    if s == 0:
        return x
    n = x.shape[0]
    return jnp.where(_rows(x.shape) < n - s, pltpu.roll(x, n - s, 0), 0.0)


def _matmul(a, b, dims, out_dtype, tm, tn, tk, name, add=None, n=None, b_outer=False):
    if dims == NN:
        (m, k), n = a.shape, b.shape[1]
    elif dims == NT:
        (m, k), n = a.shape, (n or b.shape[0])
    else:
        (k, m), n = a.shape, b.shape[1]
    tm, tn, tk = min(tm, m), min(tn, n), min(tk, k)
    assert m % tm == 0 and n % tn == 0 and k % tk == 0, (name, m, n, k, tm, tn, tk)
    nk = k // tk

    def body(*refs):
        if add is None:
            a_ref, b_ref, o_ref = refs[:3]
            add_ref = None
        else:
            a_ref, b_ref, add_ref, o_ref = refs[:4]
        part = _dot(a_ref[...], b_ref[...], dims)
        if nk == 1:
            if add_ref is not None:
                part = part + add_ref[...]
            o_ref[...] = part.astype(out_dtype)
            return
        acc = refs[-1]
        kk = pl.program_id(2)

        @pl.when(kk == 0)
        def _():
            acc[...] = part

        @pl.when(kk > 0)
        def _():
            acc[...] += part

        @pl.when(kk == nk - 1)
        def _():
            r = acc[...]
            if add_ref is not None:
                r = r + add_ref[...]
            o_ref[...] = r.astype(out_dtype)

    ij = (lambda g0, g1: (g1, g0)) if b_outer else (lambda g0, g1: (g0, g1))

    def spec(shape, pick):
        return pl.BlockSpec(shape, lambda g0, g1, kk: pick(*ij(g0, g1), kk))

    a_spec = spec((tk, tm), lambda i, j, kk: (kk, i)) if dims == TN else spec((tm, tk), lambda i, j, kk: (i, kk))
    b_spec = spec((tn, tk), lambda i, j, kk: (j, kk)) if dims == NT else spec((tk, tn), lambda i, j, kk: (kk, j))
    o_spec = spec((tm, tn), lambda i, j, kk: (i, j))
    in_specs = [a_spec, b_spec]
    args = [a, b]
    if add is not None:
        in_specs.append(o_spec)
        args.append(add)
    osz = jnp.dtype(out_dtype).itemsize
    est = 2 * (tm * tk * a.dtype.itemsize + tk * tn * b.dtype.itemsize + tm * tn * osz)
    est += 3 * tm * tn * 4 + (2 * tm * tn * 4 if add is not None else 0)
    return pl.pallas_call(
        body, name=name, grid=(n // tn, m // tm, nk) if b_outer else (m // tm, n // tn, nk),
        in_specs=in_specs, out_specs=o_spec,
        out_shape=jax.ShapeDtypeStruct((m, n), out_dtype),
        scratch_shapes=[pltpu.VMEM((tm, tn), F32)] if nk > 1 else [],
        compiler_params=_params(("parallel", "parallel", "arbitrary"), est + 8 * 2**20),
    )(*args)


def _cast_bf16(a, rows, name):
    r, c = a.shape
    rows = min(rows, r)

    def body(a_ref, o_ref):
        o_ref[...] = a_ref[...].astype(BF16)

    return pl.pallas_call(
        body, name=name, grid=(r // rows,),
        in_specs=[pl.BlockSpec((rows, c), lambda i: (i, 0))],
        out_specs=pl.BlockSpec((rows, c), lambda i: (i, 0)),
        out_shape=jax.ShapeDtypeStruct((r, c), BF16),
        compiler_params=_params(("parallel",)),
    )(a)


def _align_shard(wt):
    r, _, d = wt.shape
    cols = min(256, d)

    def body(w_ref, o_ref, pad_ref):
        chip = 2 * lax.axis_index("x") + lax.axis_index("y")
        pad_ref[...] = jnp.zeros_like(pad_ref)
        pad_ref[0:r, :] = w_ref[:, 0, :]
        o_ref[...] = pltpu.roll(pad_ref[...], _by_chip(chip, SHIFTS), 0).astype(BF16)

    return pl.pallas_call(
        body, name="align_shard", grid=(d // cols,),
        in_specs=[pl.BlockSpec((r, 1, cols), lambda i: (0, 0, i))],
        out_specs=pl.BlockSpec((ALIGNED_W, cols), lambda i: (0, i)),
        out_shape=jax.ShapeDtypeStruct((ALIGNED_W, d), BF16),
        scratch_shapes=[pltpu.VMEM((ALIGNED_W, cols), F32)],
        compiler_params=_params(("parallel",)),
    )(wt)


def _rms_in(x, w):
    n, d = x.shape
    tr = min(256, n)

    def body(x_ref, w_ref, h_ref):
        xv = x_ref[...]
        r = lax.rsqrt(jnp.mean(xv * xv, axis=-1, keepdims=True) + EPS)
        h_ref[...] = (xv * r * w_ref[...]).astype(BF16)

    return pl.pallas_call(
        body, name="rms_in", grid=(n // tr,),
        in_specs=[pl.BlockSpec((tr, d), lambda i: (i, 0)), pl.BlockSpec((1, d), lambda i: (0, 0))],
        out_specs=pl.BlockSpec((tr, d), lambda i: (i, 0)),
        out_shape=jax.ShapeDtypeStruct((n, d), BF16),
        compiler_params=_params(("parallel",)),
    )(x, w)


def _conv_silu(p, w_ref, taps):
    c = None
    for j in range(taps):
        t = _shift_down(p, taps - 1 - j) * w_ref[j:j + 1, :]
        c = t if c is None else c + t
    return c


def _prep_qkv(proj, cw):
    n = proj.shape[0]

    def body(p3, wq, wk, wv, q_ref, k_ref, v_ref):
        for kind, (w_ref, o_ref) in enumerate(((wq, q_ref), (wk, k_ref), (wv, v_ref))):
            c = _conv_silu(p3[:, kind * DH:(kind + 1) * DH], w_ref, 4)
            a = c * _sigmoid(c)
            if kind < 2:
                r = lax.rsqrt(jnp.sum(a * a, axis=-1, keepdims=True) + EPS)
                a = a * (r * (DH ** -0.5 if kind == 0 else 1.0))
            o_ref[...] = a

    col = pl.BlockSpec((n, DH), lambda h: (0, h))
    wcol = lambda base: pl.BlockSpec((4, DH), lambda h: (0, base + h))
    out = jax.ShapeDtypeStruct((n, GW), F32)
    return pl.pallas_call(
        body, name="prep_qkv", grid=(HEADS,),
        in_specs=[pl.BlockSpec((n, 3 * DH), lambda h: (0, h)), wcol(QB), wcol(KB), wcol(VB)],
        out_specs=[col] * 3, out_shape=[out] * 3,
        compiler_params=_params(("parallel",), 40 * 2**20),
    )(proj, cw, cw, cw)


def _prep_qkv_bwd(proj, cw, dq, dk, dv, dproj):
    n = proj.shape[0]

    def body(p3, wq, wk, wv, dq_ref, dk_ref, dv_ref, _, o3, gq, gk, gv):
        for kind, (w_ref, d_ref, g_ref) in enumerate(((wq, dq_ref, gq), (wk, dk_ref, gk), (wv, dv_ref, gv))):
            p = p3[:, kind * DH:(kind + 1) * DH]
            shifted = [_shift_down(p, 3 - j) for j in range(4)]
            c = shifted[0] * w_ref[0:1, :]
            for j in range(1, 4):
                c = c + shifted[j] * w_ref[j:j + 1, :]
            s = _sigmoid(c)
            a = c * s
            d = d_ref[...]
            if kind < 2:
                r = lax.rsqrt(jnp.sum(a * a, axis=-1, keepdims=True) + EPS)
                sc = DH ** -0.5 if kind == 0 else 1.0
                d = (sc * r) * (d - a * ((r * r) * jnp.sum(d * a, axis=-1, keepdims=True)))
            dc = d * _dsilu(c, s)
            dp = None
            for j in range(4):
                g_ref[j:j + 1, :] = jnp.sum(dc * shifted[j], axis=0, keepdims=True)
                t = _shift_up(dc, 3 - j) * w_ref[j:j + 1, :]
                dp = t if dp is None else dp + t
            o3[:, kind * DH:(kind + 1) * DH] = dp.astype(BF16)

    col = pl.BlockSpec((n, DH), lambda h: (0, h))
    wcol = lambda base: pl.BlockSpec((4, DH), lambda h: (0, base + h))
    p3spec = pl.BlockSpec((n, 3 * DH), lambda h: (0, h))
    return pl.pallas_call(
        body, name="prep_qkv_bwd", grid=(HEADS,),
        in_specs=[p3spec, wcol(QB), wcol(KB), wcol(VB), col, col, col, ANY],
        out_specs=[p3spec] + [wcol(0)] * 3,
        out_shape=[jax.ShapeDtypeStruct(dproj.shape, BF16)] + [jax.ShapeDtypeStruct((4, GW), F32)] * 3,
        input_output_aliases={7: 0},
        compiler_params=_params(("parallel",), 48 * 2**20),
    )(proj, cw, cw, cw, dq, dk, dv, dproj)


CPB = 8
SCAN_CPS = 4


def _tri(lower, rows):
    i = lax.broadcasted_iota(jnp.int32, (rows, rows), 0)
    j = lax.broadcasted_iota(jnp.int32, (rows, rows), 1)
    return jnp.where((i // CH == j // CH) & ((i >= j) if lower else (j >= i)), 1.0, 0.0)


def _lane(shape):
    return lax.broadcasted_iota(jnp.int32, shape, 1)


def _prep_bg(proj, ad):
    n = proj.shape[0]
    nch = n // CH
    cpb = CPB if nch % CPB == 0 else 1
    rows = cpb * CH

    def body(p_ref, ad_ref, bg_ref, bgt_ref):
        p = p_ref[...]
        lane = _lane(p.shape)
        beta = _sigmoid(p)
        xa = p + ad_ref[1:2, :]
        sp = jnp.maximum(xa, 0.0) + jnp.log(1.0 + jnp.exp(-jnp.abs(xa)))
        g = pltpu.roll(-jnp.exp(ad_ref[0:1, :]) * sp, DH - A_LANE + HEADS, 1)
        gc = _dot(_tri(True, rows), g, NN, P_CUM)
        bg = jnp.where(lane < HEADS, beta, jnp.where(lane < 2 * HEADS, gc, 0.0))
        bg_ref[...] = bg
        for ci in range(cpb):
            bgt_ref[ci] = bg[ci * CH:(ci + 1) * CH, :].T

    return pl.pallas_call(
        body, name="prep_bg", grid=(nch // cpb,),
        in_specs=[pl.BlockSpec((rows, DH), lambda i: (i, BAB)), pl.BlockSpec((2, DH), lambda i: (0, 0))],
        out_specs=[pl.BlockSpec((rows, DH), lambda i: (i, 0)), pl.BlockSpec((cpb, DH, CH), lambda i: (i, 0, 0))],
        out_shape=[jax.ShapeDtypeStruct((n, DH), F32), jax.ShapeDtypeStruct((nch, DH, CH), F32)],
        compiler_params=_params(("parallel",)),
    )(*_in_hbm(proj, ad))


def _prep_bg_bwd(proj, ad, dbg, dproj):
    n = proj.shape[0]
    nch = n // CH
    cpb = CPB if nch % CPB == 0 else 1
    rows = cpb * CH

    def body(p_ref, ad_ref, d_ref, _, o_ref, ga_ref, gd_ref):
        p = p_ref[...]
        d = d_ref[...]
        lane = _lane(p.shape)
        beta = _sigmoid(p)
        xa = p + ad_ref[1:2, :]
        sp = jnp.maximum(xa, 0.0) + jnp.log(1.0 + jnp.exp(-jnp.abs(xa)))
        na = -jnp.exp(ad_ref[0:1, :])
        dg = pltpu.roll(_dot(_tri(False, rows), d, NN, P_CUM), A_LANE - HEADS, 1)
        da = dg * na * _sigmoid(xa)
        is_g = lane >= A_LANE
        o_ref[...] = jnp.where(lane < HEADS, d * beta * (1.0 - beta), jnp.where(is_g, da, 0.0)).astype(BF16)
        ga = jnp.sum(jnp.where(is_g, dg * na * sp, 0.0), axis=0, keepdims=True)
        gd = jnp.sum(jnp.where(is_g, da, 0.0), axis=0, keepdims=True)

        @pl.when(pl.program_id(0) == 0)
        def _():
            ga_ref[...] = jnp.zeros_like(ga_ref)
            gd_ref[...] = jnp.zeros_like(gd_ref)

        ga_ref[...] += ga
        gd_ref[...] += gd

    one = pl.BlockSpec((1, DH), lambda i: (0, 0))
    return pl.pallas_call(
        body, name="prep_bg_bwd", grid=(nch // cpb,),
        in_specs=[pl.BlockSpec((rows, DH), lambda i: (i, BAB)), pl.BlockSpec((2, DH), lambda i: (0, 0)),
                  pl.BlockSpec((rows, DH), lambda i: (i, 0)), ANY],
        out_specs=[pl.BlockSpec((rows, DH), lambda i: (i, BAB)), one, one],
        out_shape=[jax.ShapeDtypeStruct(dproj.shape, BF16), jax.ShapeDtypeStruct((1, DH), F32),
                   jax.ShapeDtypeStruct((1, DH), F32)],
        input_output_aliases={3: 0},
        compiler_params=_params(("arbitrary",)),
    )(proj, ad, dbg, dproj)


def _gdn_out(o, proj, wg):
    n = o.shape[0]

    def body(o_ref, z_ref, w_ref, y_ref):
        ov, z = o_ref[...], z_ref[...]
        r = lax.rsqrt(jnp.mean(ov * ov, axis=-1, keepdims=True) + EPS)
        y_ref[...] = (ov * r * w_ref[...] * (z * _sigmoid(z))).astype(BF16)

    return pl.pallas_call(
        body, name="gdn_out", grid=(HEADS,),
        in_specs=[pl.BlockSpec((n, DH), lambda h: (0, h)), pl.BlockSpec((n, DH), lambda h: (0, ZB + h)),
                  pl.BlockSpec((1, DH), lambda h: (0, 0))],
        out_specs=pl.BlockSpec((n, DH), lambda h: (0, h)),
        out_shape=jax.ShapeDtypeStruct((n, 2 * GW), BF16),
        compiler_params=_params(("parallel",)),
    )(o, proj, wg)


def _gdn_out_bwd(o, proj, wg, dmix):
    n = o.shape[0]

    def body(o_ref, z_ref, w_ref, d_ref, do_ref, dz_ref, gw_ref):
        ov, z, d, w = o_ref[...], z_ref[...], d_ref[...], w_ref[...]
        r = lax.rsqrt(jnp.mean(ov * ov, axis=-1, keepdims=True) + EPS)
        nrm = ov * r
        s = _sigmoid(z)
        dz_ref[...] = (d * (nrm * w) * _dsilu(z, s)).astype(BF16)
        dn_w = d * (z * s)
        gw = jnp.sum(dn_w * nrm, axis=0, keepdims=True)
        dn = dn_w * w
        do_ref[...] = r * (dn - nrm * jnp.mean(dn * nrm, axis=-1, keepdims=True))

        @pl.when(pl.program_id(0) == 0)
        def _():
            gw_ref[...] = jnp.zeros_like(gw_ref)

        gw_ref[...] += gw

    return pl.pallas_call(
        body, name="gdn_out_bwd", grid=(HEADS,),
        in_specs=[pl.BlockSpec((n, DH), lambda h: (0, h)), pl.BlockSpec((n, DH), lambda h: (0, ZB + h)),
                  pl.BlockSpec((1, DH), lambda h: (0, 0)), pl.BlockSpec((n, DH), lambda h: (0, h))],
        out_specs=[pl.BlockSpec((n, DH), lambda h: (0, h)), pl.BlockSpec((n, DH), lambda h: (0, ZB + h)),
                   pl.BlockSpec((1, DH), lambda h: (0, 0))],
        out_shape=[jax.ShapeDtypeStruct((n, GW), F32), jax.ShapeDtypeStruct((n, GW_COLS), BF16),
                   jax.ShapeDtypeStruct((1, DH), F32)],
        compiler_params=_params(("arbitrary",)),
    )(o, proj, wg, dmix)


def _conv_branch(proj, w3, b, mix):
    n = proj.shape[0]

    def body(p4, w_ref, b_ref, _, y_ref):
        u = p4[:, DH:2 * DH] * p4[:, 2 * DH:3 * DH]
        cc = _conv_silu(u, w_ref, 3) + b_ref[...]
        z = p4[:, 3 * DH:4 * DH]
        y_ref[...] = (p4[:, 0:DH] * cc * (z * _sigmoid(z))).astype(BF16)

    return pl.pallas_call(
        body, name="conv_branch", grid=(HEADS,),
        in_specs=[pl.BlockSpec((n, 4 * DH), lambda h: (0, h)), pl.BlockSpec((3, DH), lambda h: (0, h)),
                  pl.BlockSpec((1, DH), lambda h: (0, h)), ANY],
        out_specs=pl.BlockSpec((n, DH), lambda h: (0, HEADS + h)),
        out_shape=jax.ShapeDtypeStruct(mix.shape, BF16),
        input_output_aliases={3: 0},
        compiler_params=_params(("parallel",), 40 * 2**20),
    )(*_in_hbm(proj, w3, b, mix))


def _conv_branch_bwd(proj, w3, b, dmix):
    n = proj.shape[0]

    def body(p4, w_ref, b_ref, d_ref, o4, gw_ref, gbias_ref):
        gb, gcv, hc, z = p4[:, 0:DH], p4[:, DH:2 * DH], p4[:, 2 * DH:3 * DH], p4[:, 3 * DH:4 * DH]
        d = d_ref[...]
        dgb, dgc, dhc, dzc = (o4.at[:, kk * DH:(kk + 1) * DH] for kk in range(4))
        u = gcv * hc
        cc = _conv_silu(u, w_ref, 3) + b_ref[...]
        s = _sigmoid(z)
        dzc[...] = (d * (gb * cc) * _dsilu(z, s)).astype(BF16)
        dp = d * (z * s)
        dgb[...] = (dp * cc).astype(BF16)
        dcc = dp * gb
        gbias_ref[...] = jnp.sum(dcc, axis=0, keepdims=True)
        du = None
        for j in range(3):
            gw_ref[j:j + 1, :] = jnp.sum(dcc * _shift_down(u, 2 - j), axis=0, keepdims=True)
            t = _shift_up(dcc, 2 - j) * w_ref[j:j + 1, :]
            du = t if du is None else du + t
        dgc[...] = (du * hc).astype(BF16)
        dhc[...] = (du * gcv).astype(BF16)

    p4spec = pl.BlockSpec((n, 4 * DH), lambda h: (0, h))
    return pl.pallas_call(
        body, name="conv_branch_bwd", grid=(HEADS,),
        in_specs=[p4spec, pl.BlockSpec((3, DH), lambda h: (0, h)), pl.BlockSpec((1, DH), lambda h: (0, h)),
                  pl.BlockSpec((n, DH), lambda h: (0, HEADS + h))],
        out_specs=[p4spec, pl.BlockSpec((3, DH), lambda h: (0, h)), pl.BlockSpec((1, DH), lambda h: (0, h))],
        out_shape=[jax.ShapeDtypeStruct((n, CW_COLS), BF16), jax.ShapeDtypeStruct((3, GW), F32),
                   jax.ShapeDtypeStruct((1, GW), F32)],
        compiler_params=_params(("parallel",), 48 * 2**20),
    )(proj, w3, b, dmix)


def _final_loss(out, tgt, wf):
    n, d = out.shape
    tr = min(256, n)

    def body(o_ref, t_ref, w_ref, do_ref, dob_ref, gw_ref, loss_ref):
        ov, w = o_ref[...], w_ref[...]
        r = lax.rsqrt(jnp.mean(ov * ov, axis=-1, keepdims=True) + EPS)
        nrm = ov * r
        e = nrm * w - t_ref[...]
        dy = e * (1.0 / d)
        dn = dy * w
        dout = r * (dn - nrm * jnp.mean(dn * nrm, axis=-1, keepdims=True))
        do_ref[...] = dout
        dob_ref[...] = dout.astype(BF16)

        @pl.when(pl.program_id(0) == 0)
        def _():
            gw_ref[...] = jnp.zeros_like(gw_ref)
            loss_ref[...] = jnp.zeros_like(loss_ref)

        gw_ref[...] += jnp.sum(dy * nrm, axis=0, keepdims=True)
        loss_ref[...] += (0.5 / d) * jnp.sum(jnp.sum(e * e, axis=-1, keepdims=True), axis=0, keepdims=True)

    row = pl.BlockSpec((tr, d), lambda i: (i, 0))
    return pl.pallas_call(
        body, name="final_loss", grid=(n // tr,),
        in_specs=[row, row, pl.BlockSpec((1, d), lambda i: (0, 0))],
        out_specs=[row, row, pl.BlockSpec((1, d), lambda i: (0, 0)), pl.BlockSpec((1, 1), lambda i: (0, 0))],
        out_shape=[jax.ShapeDtypeStruct((n, d), F32), jax.ShapeDtypeStruct((n, d), BF16),
                   jax.ShapeDtypeStruct((1, d), F32), jax.ShapeDtypeStruct((1, 1), F32)],
        compiler_params=_params(("arbitrary",)),
    )(*_in_hbm(out, tgt, wf))


def _rms_in_bwd(x, w, dh, dout):
    n, d = x.shape
    tr = min(256, n)

    def body(x_ref, w_ref, dh_ref, do_ref, dx_ref, gw_ref):
        xv, dhv = x_ref[...], dh_ref[...]
        r = lax.rsqrt(jnp.mean(xv * xv, axis=-1, keepdims=True) + EPS)
        xn = xv * r
        dxn = dhv * w_ref[...]
        dx_ref[...] = r * (dxn - xn * jnp.mean(dxn * xn, axis=-1, keepdims=True)) + do_ref[...]

        @pl.when(pl.program_id(0) == 0)
        def _():
            gw_ref[...] = jnp.zeros_like(gw_ref)

        gw_ref[...] += jnp.sum(dhv * xn, axis=0, keepdims=True)

    row = pl.BlockSpec((tr, d), lambda i: (i, 0))
    one = pl.BlockSpec((1, d), lambda i: (0, 0))
    return pl.pallas_call(
        body, name="rms_in_bwd", grid=(n // tr,),
        in_specs=[row, one, row, row], out_specs=[row, one],
        out_shape=[jax.ShapeDtypeStruct((n, d), F32), jax.ShapeDtypeStruct((1, d), F32)],
        compiler_params=_params(("arbitrary",)),
    )(*_in_hbm(x, w, dh, dout))


def _ij():
    i = lax.broadcasted_iota(jnp.int32, (CH, CH), 0)
    j = lax.broadcasted_iota(jnp.int32, (CH, CH), 1)
    return i, j


def _unit_lower_inverse(mats):
    i, j = _ij()
    eye = jnp.where(i == j, 1.0, 0.0)
    same16 = (i // 16) == (j // 16)
    same32 = (i // 32) == (j // 32)
    mm = lambda xs, ys: [_dot(x, y, NN, P_INV) for x, y in zip(xs, ys)]
    n1 = [jnp.where(same16, -a, 0.0) for a in mats]
    n2 = mm(n1, n1)
    n4 = mm(n2, n2)
    n8 = mm(n4, n4)
    t = [eye + x1 + x2 + x3 for x1, x2, x3 in zip(n1, n2, mm(n1, n2))]
    t = [x + y for x, y in zip(t, mm(t, n4))]
    t = [x + y for x, y in zip(t, mm(t, n8))]
    a1 = [jnp.where(same32 & jnp.logical_not(same16), a, 0.0) for a in mats]
    t = [x - y for x, y in zip(t, mm(t, mm(a1, t)))]
    a2 = [jnp.where(same32, 0.0, a) for a in mats]
    t = [x - y for x, y in zip(t, mm(t, mm(a2, t)))]
    return t


def _head_vectors(bg, bgt, h):
    bcol = bg[:, h:h + 1]
    gcol = bg[:, HEADS + h:HEADS + h + 1]
    grow = bgt[HEADS + h:HEADS + h + 1, :]
    return bcol, gcol, grow


def _decay(gcol, grow):
    i, j = _ij()
    return jnp.where(i >= j, jnp.exp(jnp.where(i >= j, gcol - grow, 0.0)), 0.0)


def _gdn_intra(q, k, v, bg, bgt):
    n = q.shape[0]
    nch = n // CH
    cps = 4 if nch % 4 == 0 else 1

    def body(q_ref, k_ref, v_ref, bg_ref, bgt_ref, u_ref, w_ref, p_ref, t_ref):
        i, j = _ij()
        items = [(ci, h) for ci in range(cps) for h in range(HEADS)]
        at = lambda ref, ci, h: ref.at[ci * CH:(ci + 1) * CH, h * DH:(h + 1) * DH]
        bgs = [bg_ref[ci * CH:(ci + 1) * CH, :] for ci in range(cps)]
        ks = [at(k_ref, ci, h)[...] for ci, h in items]
        vecs = [_head_vectors(bgs[ci], bgt_ref[ci], h) for ci, h in items]
        decs = [_decay(gcol, grow) for _, gcol, grow in vecs]
        kks = [_dot(kh, kh, NT, P_GRAM) for kh in ks]
        qks = [_dot(at(q_ref, ci, h)[...], kh, NT, P_GRAM) for (ci, h), kh in zip(items, ks)]
        ts = _unit_lower_inverse([jnp.where(i > j, bcol * kk * dec, 0.0)
                                  for (bcol, _, _), kk, dec in zip(vecs, kks, decs)])
        us = [_dot(t, at(v_ref, ci, h)[...] * bcol, NN, P_SOL) for t, (ci, h), (bcol, _, _) in zip(ts, items, vecs)]
        ws = [_dot(t, kh * (bcol * jnp.exp(gcol)), NN, P_SOL) for t, kh, (bcol, gcol, _) in zip(ts, ks, vecs)]
        for n_, (ci, h) in enumerate(items):
            p_ref[ci, h] = qks[n_] * decs[n_]
            t_ref[ci, h] = ts[n_]
            at(u_ref, ci, h)[...] = us[n_]
            at(w_ref, ci, h)[...] = ws[n_]

    row = pl.BlockSpec((cps * CH, GW), lambda c: (c, 0))
    sq = pl.BlockSpec((cps, HEADS, CH, CH), lambda c: (c, 0, 0, 0))
    big = jax.ShapeDtypeStruct((n, GW), F32)
    sqs = jax.ShapeDtypeStruct((nch, HEADS, CH, CH), F32)
    return pl.pallas_call(
        body, name="gdn_intra", grid=(nch // cps,),
        in_specs=[row, row, row, pl.BlockSpec((cps * CH, DH), lambda c: (c, 0)),
                  pl.BlockSpec((cps, DH, CH), lambda c: (c, 0, 0))],
        out_specs=[row, row, sq, sq], out_shape=[big, big, sqs, sqs],
        compiler_params=_params(("parallel",)),
    )(q, k, v, bg, bgt)


def _gdn_scan(q, k, bg, u, w, p):
    n = q.shape[0]
    nch = n // CH
    cps = SCAN_CPS if nch % SCAN_CPS == 0 else 1

    def body(q_ref, k_ref, bg_ref, u_ref, w_ref, p_ref, o_ref, vn_ref, s_out, s_scr):
        @pl.when(pl.program_id(0) == 0)
        def _():
            s_scr[...] = jnp.zeros_like(s_scr)

        hs = range(HEADS)
        sls = [slice(h * DH, (h + 1) * DH) for h in hs]
        ss = [s_scr[h] for h in hs]
        for ci in range(cps):
            rs = slice(ci * CH, (ci + 1) * CH)
            bg = bg_ref[rs, :]
            gcols = [bg[:, HEADS + h:HEADS + h + 1] for h in hs]
            glasts = [g[CH - 1:CH, :] for g in gcols]
            wss = [_dot(w_ref[rs, sl], s, NN, P_SCAN) for sl, s in zip(sls, ss)]
            oqs = [_dot(q_ref[rs, sl] * jnp.exp(g), s, NN, P_SCAN) for sl, s, g in zip(sls, ss, gcols)]
            vns = [u_ref[rs, sl] - x for sl, x in zip(sls, wss)]
            ops = [_dot(p_ref[ci, h], vn, NN, P_SCAN) for h, vn in zip(hs, vns)]
            sns = [_dot(k_ref[rs, sl] * jnp.exp(gl - g), vn, TN, P_SCAN)
                   for sl, gl, g, vn in zip(sls, glasts, gcols, vns)]
            for h, sl in enumerate(sls):
                s_out[ci, :, sl] = ss[h]
                vn_ref[rs, sl] = vns[h]
                o_ref[rs, sl] = oqs[h] + ops[h]
            ss = [s * jnp.exp(gl) + sn for s, gl, sn in zip(ss, glasts, sns)]
        for h in hs:
            s_scr[h] = ss[h]

    row = pl.BlockSpec((cps * CH, GW), lambda c: (c, 0))
    big = jax.ShapeDtypeStruct((n, GW), F32)
    return pl.pallas_call(
        body, name="gdn_scan", grid=(nch // cps,),
        in_specs=[row, row, pl.BlockSpec((cps * CH, DH), lambda c: (c, 0)), row, row,
                  pl.BlockSpec((cps, HEADS, CH, CH), lambda c: (c, 0, 0, 0))],
        out_specs=[row, row, pl.BlockSpec((cps, DH, GW), lambda c: (c, 0, 0))],
        out_shape=[big, big, jax.ShapeDtypeStruct((nch, DH, GW), F32)],
        scratch_shapes=[pltpu.VMEM((HEADS, DH, DH), F32)],
        compiler_params=_params(("arbitrary",)),
    )(q, k, bg, u, w, p)


def _gdn_scan_bwd(q, k, bg, w, p, vn, s_in, do):
    n = q.shape[0]
    nch = n // CH
    cps = SCAN_CPS if nch % SCAN_CPS == 0 else 1
    rev = lambda c: nch // cps - 1 - c

    def body(q_ref, k_ref, bg_ref, w_ref, p_ref, vn_ref, s_ref, do_ref,
             dqg_ref, dp_ref, du_ref, dw_ref, dks_ref, dgam_ref, ds_scr):
        @pl.when(pl.program_id(0) == 0)
        def _():
            ds_scr[...] = jnp.zeros_like(ds_scr)

        lane = _lane((1, DH))
        hs = range(HEADS)
        sls = [slice(h * DH, (h + 1) * DH) for h in hs]
        dss = [ds_scr[h] for h in hs]
        for ci in reversed(range(cps)):
            rs = slice(ci * CH, (ci + 1) * CH)
            bg = bg_ref[rs, :]
            gcols = [bg[:, HEADS + h:HEADS + h + 1] for h in hs]
            glasts = [g[CH - 1:CH, :] for g in gcols]
            ss = [s_ref[ci, :, sl] for sl in sls]
            dos = [do_ref[rs, sl] for sl in sls]
            vnl = [vn_ref[rs, sl] for sl in sls]
            dqgs = [_dot(d, s, NT, P_SCANB) for d, s in zip(dos, ss)]
            dps = [_dot(d, vn, NT, P_SCANB) for d, vn in zip(dos, vnl)]
            dvn1 = [_dot(p_ref[ci, h], d, TN, P_SCANB) for h, d in zip(hs, dos)]
            dvn2 = [_dot(k_ref[rs, sl] * jnp.exp(gl - g), ds, NN, P_SCANB)
                    for sl, gl, g, ds in zip(sls, glasts, gcols, dss)]
            dkss = [_dot(vn, ds, NT, P_SCANB) for vn, ds in zip(vnl, dss)]
            dsq = [_dot(q_ref[rs, sl] * jnp.exp(g), d, TN, P_SCANB) for sl, g, d in zip(sls, gcols, dos)]
            dvns = [a + b for a, b in zip(dvn1, dvn2)]
            dws = [_dot(dvn, s, NT, P_SCANB) for dvn, s in zip(dvns, ss)]
            dsw = [_dot(w_ref[rs, sl], dvn, TN, P_SCANB) for sl, dvn in zip(sls, dvns)]
            dgam = jnp.zeros((1, DH), F32)
            for h, sl in enumerate(sls):
                dqg_ref[rs, sl] = dqgs[h]
                dp_ref[ci, h] = dps[h]
                du_ref[rs, sl] = dvns[h]
                dw_ref[rs, sl] = -dws[h]
                dks_ref[rs, sl] = dkss[h]
                tot = jnp.sum(jnp.sum(dss[h] * ss[h], axis=-1, keepdims=True), axis=0, keepdims=True)
                dgam = dgam + jnp.where(lane == h, tot, 0.0)
            dgam_ref[ci] = jnp.broadcast_to(dgam, (8, DH))
            dss = [ds * jnp.exp(gl) + a - b for ds, gl, a, b in zip(dss, glasts, dsq, dsw)]
        for h in hs:
            ds_scr[h] = dss[h]

    row = pl.BlockSpec((cps * CH, GW), lambda c: (rev(c), 0))
    sq = pl.BlockSpec((cps, HEADS, CH, CH), lambda c: (rev(c), 0, 0, 0))
    big = jax.ShapeDtypeStruct((n, GW), F32)
    return pl.pallas_call(
        body, name="gdn_scan_bwd", grid=(nch // cps,),
        in_specs=[row, row, pl.BlockSpec((cps * CH, DH), lambda c: (rev(c), 0)), row, sq, row,
                  pl.BlockSpec((cps, DH, GW), lambda c: (rev(c), 0, 0)), row],
        out_specs=[row, sq, row, row, row, pl.BlockSpec((cps, 8, DH), lambda c: (rev(c), 0, 0))],
        out_shape=[big, jax.ShapeDtypeStruct((nch, HEADS, CH, CH), F32), big, big, big,
                   jax.ShapeDtypeStruct((nch, 8, DH), F32)],
        scratch_shapes=[pltpu.VMEM((HEADS, DH, DH), F32)],
        compiler_params=_params(("arbitrary",)),
    )(q, k, bg, w, p, vn, s_in, do)


def _gdn_intra_bwd(q, k, v, bg, bgt, t, u, w, p, dqg, dp, du, dw, dks, dgam):
    n = q.shape[0]
    nch = n // CH
    cps = 1

    def body(q_ref, k_ref, v_ref, bg_ref, bgt_ref, t_ref, u_ref, w_ref, p_ref,
             dqg_ref, dp_ref, du_ref, dw_ref, dks_ref, dgam_ref, dq_ref, dk_ref, dv_ref, dbg_ref):
        i, j = _ij()
        rows1 = lax.broadcasted_iota(jnp.int32, (CH, 1), 0)
        lane = _lane((CH, DH))
        rsum = lambda x: jnp.sum(x, axis=-1, keepdims=True)
        items = [(ci, h) for ci in range(cps) for h in range(HEADS)]
        at = lambda ref, it: ref.at[it[0] * CH:(it[0] + 1) * CH, it[1] * DH:(it[1] + 1) * DH]
        ld = lambda ref: [at(ref, it)[...] for it in items]
        bgs = [bg_ref[ci * CH:(ci + 1) * CH, :] for ci in range(cps)]
        qs, ks = ld(q_ref), ld(k_ref)
        vecs = [_head_vectors(bgs[ci], bgt_ref[ci], h) for ci, h in items]
        decs = [_decay(gcol, grow) for _, gcol, grow in vecs]
        ths = [t_ref[ci, h] for ci, h in items]
        drus = [_dot(th, x_, TN, P_BWD) for th, x_ in zip(ths, ld(du_ref))]
        drws = [_dot(th, x_, TN, P_BWD) for th, x_ in zip(ths, ld(dw_ref))]
        kks = [_dot(kh, kh, NT, P_GRAM) for kh in ks]
        da1 = [_dot(dru, x_, NT, P_BWD) for dru, x_ in zip(drus, ld(u_ref))]
        da2 = [_dot(drw, x_, NT, P_BWD) for drw, x_ in zip(drws, ld(w_ref))]
        das = [jnp.where(i > j, -(x_ + y_), 0.0) for x_, y_ in zip(da1, da2)]
        dkks = [da * bcol * dec for da, (bcol, _, _), dec in zip(das, vecs, decs)]
        dps = [dp_ref[ci, h] for ci, h in items]
        dqks = [dp_ * dec for dp_, dec in zip(dps, decs)]
        dq_ps = [_dot(dqk, kh, NN, P_BWD) for dqk, kh in zip(dqks, ks)]
        dk_ps = [_dot(dqk, qh, TN, P_BWD) for dqk, qh in zip(dqks, qs)]
        dk_as = [_dot(dkk, kh, NN, P_BWD) for dkk, kh in zip(dkks, ks)]
        dk_bs = [_dot(dkk, kh, TN, P_BWD) for dkk, kh in zip(dkks, ks)]
        bcols = [vc[0] for vc in vecs]
        gcols = [vc[1] for vc in vecs]
        gams = [jnp.exp(g) for g in gcols]
        glasts = [g[CH - 1:CH, :] for g in gcols]
        es = [jnp.exp(gl - g) for gl, g in zip(glasts, gcols)]
        kgs = [kh * gam for kh, gam in zip(ks, gams)]
        dqgs, dkss = ld(dqg_ref), ld(dks_ref)
        r_uv = [rsum(dru * x_) for dru, x_ in zip(drus, ld(v_ref))]
        r_wk = [rsum(drw * kg) for drw, kg in zip(drws, kgs)]
        r_ak = [rsum(da * kk * dec) for da, kk, dec in zip(das, kks, decs)]
        r_qq = [rsum(dqg * qh) for dqg, qh in zip(dqgs, qs)]
        tks = [rsum(dk_ * kh) * e for dk_, kh, e in zip(dkss, ks, es)]
        mdecs = [da * (bcol * kk * dec) + dp_ * p_ref[ci, h]
                 for (ci, h), da, bcol, kk, dec, dp_ in zip(items, das, bcols, kks, decs, dps)]
        r_md = [rsum(m) for m in mdecs]
        c_md = [rsum(jnp.where(i == j, jnp.sum(m, axis=0, keepdims=True), 0.0)) for m in mdecs]
        dbgs = [jnp.zeros((CH, DH), F32) for _ in range(cps)]
        for n_, (ci, h) in enumerate(items):
            at(dv_ref, (ci, h))[...] = bcols[n_] * drus[n_]
            at(dq_ref, (ci, h))[...] = gams[n_] * dqgs[n_] + dq_ps[n_]
            at(dk_ref, (ci, h))[...] = ((bcols[n_] * gams[n_]) * drws[n_] + dk_ps[n_] + dk_as[n_] + dk_bs[n_]
                                        + dkss[n_] * es[n_])
            dbeta = r_uv[n_] + r_wk[n_] + r_ak[n_]
            dglast = (jnp.sum(tks[n_], axis=0, keepdims=True)
                      + dgam_ref[ci, 0:1, h:h + 1] * jnp.exp(glasts[n_]))
            dgc = (r_wk[n_] * bcols[n_] + r_md[n_] - c_md[n_] + r_qq[n_] * gams[n_] - tks[n_]
                   + jnp.where(rows1 == CH - 1, dglast, 0.0))
            dbgs[ci] = dbgs[ci] + jnp.where(lane == h, dbeta, 0.0) + jnp.where(lane == HEADS + h, dgc, 0.0)
        for ci in range(cps):
            dbg_ref[ci * CH:(ci + 1) * CH, :] = dbgs[ci]

    row = pl.BlockSpec((cps * CH, GW), lambda c: (c, 0))
    sq = pl.BlockSpec((cps, HEADS, CH, CH), lambda c: (c, 0, 0, 0))
    small = pl.BlockSpec((cps * CH, DH), lambda c: (c, 0))
    big = jax.ShapeDtypeStruct((n, GW), F32)
    return pl.pallas_call(
        body, name="gdn_intra_bwd", grid=(nch // cps,),
        in_specs=[row, row, row, small, pl.BlockSpec((cps, DH, CH), lambda c: (c, 0, 0)), sq, row, row, sq,
                  row, sq, row, row, row, pl.BlockSpec((cps, 8, DH), lambda c: (c, 0, 0))],
        out_specs=[row, row, row, small],
        out_shape=[big, big, big, jax.ShapeDtypeStruct((n, DH), F32)],
        compiler_params=_params(("parallel",)),
    )(q, k, v, bg, bgt, t, u, w, p, dqg, dp, du, dw, dks, dgam)


def _local_step(x, tgt, h, w_g, cqw, late, norm_in_w, ad, gdn_norm_w, conv_b, final_norm_w,
                on_grad_c=None, on_grad_g=None, on_q=None):
    proj_g = _matmul(h, w_g, NT, F32, 512, 1408, 1024, "mm_proj_g", n=GW_COLS, b_outer=True)
    q, k, v = _prep_qkv(proj_g, cqw)
    if on_q is not None:
        q = on_q(q)
    bg, bgt = _prep_bg(proj_g, ad)
    u, w, p, t = _gdn_intra(q, k, v, bg, bgt)
    o, vn, s_in = _gdn_scan(q, k, bg, u, w, p)
    w_c, w_out, conv_w = late(o)
    proj_c = _matmul(h, w_c, NT, F32, 512, 1024, 1024, "mm_proj_c", n=CW_COLS, b_outer=True)
    mix = _conv_branch(proj_c, conv_w, conv_b, _gdn_out(o, proj_g, gdn_norm_w))
    out = _matmul(mix, w_out, NN, F32, 512, 512, 2048, "mm_out", add=x)
    dout, dout_b, g_fn, loss = _final_loss(out, tgt, final_norm_w)

    dmix = _matmul(dout_b, w_out, NT, F32, 512, 1024, 1024, "mm_dmix", b_outer=True)
    g_wout = _matmul(mix, dout_b, TN, BF16, 512, 512, 2048, "mm_gwout")
    do, dproj_g, g_gn = _gdn_out_bwd(o, proj_g, gdn_norm_w, dmix)
    dproj_c, g_cw, g_cb = _conv_branch_bwd(proj_c, conv_w, conv_b, dmix)
    g_c = _matmul(dproj_c, h, TN, BF16, 1024, 512, 2048, "mm_gwin_c")
    if on_grad_c is not None:
        do = on_grad_c(g_c, g_wout, do)
    dqg, dp, du, dw, dks, dgam = _gdn_scan_bwd(q, k, bg, w, p, vn, s_in, do)
    dq, dk, dv, dbg = _gdn_intra_bwd(q, k, v, bg, bgt, t, u, w, p, dqg, dp, du, dw, dks, dgam)
    dproj_g, gq, gk, gv = _prep_qkv_bwd(proj_g, cqw, dq, dk, dv, dproj_g)
    dproj_g, g_al, g_dt = _prep_bg_bwd(proj_g, ad, dbg, dproj_g)
    g_g = _matmul(dproj_g, h, TN, BF16, 1408, 512, 2048, "mm_gwin_g")
    if on_grad_g is not None:
        dproj_g = on_grad_g(g_g, dproj_g)
    dh = _matmul(dproj_g, w_g, NN, F32, 1024, 1024, 1408, "mm_dh_g")
    dh = _matmul(dproj_c, w_c, NN, F32, 1024, 1024, 1024, "mm_dh_c", add=dh)
    gx, g_nin = _rms_in_bwd(x, norm_in_w, dh, dout)
    small = dict(nin=g_nin, cb=g_cb, fn=g_fn, al=g_al, dt=g_dt, gn=g_gn, cq=(gq, gk, gv), cw=g_cw, loss=loss)
    return gx, small, (g_g, g_c, g_wout)


def _place():
    x, y, c = lax.axis_index("x"), lax.axis_index("y"), lax.axis_index("c")
    chips = [(1 - x, y), (x, 1 - y), (1 - x, 1 - y)]
    return x, y, c, chips


def _blk(ref, b):
    if isinstance(b, int):
        return ref.at[b * DH:(b + 1) * DH, :]
    return ref.at[pl.ds(pl.multiple_of(b * DH, DH), DH), :]


HBM = pl.BlockSpec(memory_space=pltpu.HBM)
SEM = pl.BlockSpec(memory_space=pltpu.SEMAPHORE)
EFFECT = pltpu.SideEffectType.DATAFLOW_SIDE_EFFECTING


def _split_start(name, issue, bufs, n_sems):
    nbuf = len(bufs)

    def body(*refs):
        issue(refs[:nbuf], refs[nbuf], refs[nbuf + 1])
        refs[-1][...] = jnp.zeros_like(refs[-1])

    out = pl.pallas_call(
        body, name=name,
        out_shape=(pltpu.SemaphoreType.DMA((n_sems,)), pltpu.SemaphoreType.DMA((n_sems,)),
                   *[pltpu.HBM(b.shape, b.dtype) for b in bufs], jax.ShapeDtypeStruct((8, DH), F32)),
        in_specs=[HBM] * nbuf,
        out_specs=(SEM, SEM, *[HBM] * nbuf, pl.BlockSpec(memory_space=pltpu.VMEM)),
        input_output_aliases={a: 2 + a for a in range(nbuf)},
        compiler_params=pltpu.CompilerParams(has_side_effects=EFFECT),
    )(*[pltpu.with_memory_space_constraint(b, pltpu.HBM) for b in bufs])
    return out[0], out[1], list(out[2:2 + nbuf]), out[-1]


def _split_wait(name, await_, send_sems, recv_sems, bufs, after):
    nbuf = len(bufs)
    after = list(after) if isinstance(after, (list, tuple)) else [after]

    def body(*refs):
        await_(refs[:nbuf], refs[nbuf], refs[nbuf + 1])

    out = pl.pallas_call(
        body, name=name,
        out_shape=tuple(pltpu.HBM(b.shape, b.dtype) for b in bufs),
        in_specs=[HBM] * nbuf + [SEM, SEM] + [ANY] * len(after), out_specs=tuple([HBM] * nbuf),
        input_output_aliases={a: a for a in range(nbuf)},
        compiler_params=pltpu.CompilerParams(has_side_effects=EFFECT),
    )(*bufs, send_sems, recv_sems, *after)
    return list(out)


def _phase_blocks(chip, phase, edges, parity=None):
    return [(b, blk) for b, (grp, blk) in enumerate(_shard_blocks(chip, edges))
            if grp == phase and (parity is None or b % 2 == parity)]


def _cols(ref, nblk):
    return ref.at[0:nblk * DH, :]


def _block_table(chip, edges, spare_g, spare_c):
    rows = []
    for s in range(4):
        sb = _shard_blocks(s, edges)
        rows.append([[blk if grp == "g" else spare_g for grp, blk in sb],
                     [blk if grp == "c" else spare_c for grp, blk in sb],
                     [int(grp == "g") for grp, _ in sb], [s] * ALIGNED_BLOCKS])
    return jnp.asarray(rows, jnp.int32)[chip]


def _place_own(a_shard, wo, cq, cw, bufs):
    d = a_shard.shape[1]
    chip = 2 * lax.axis_index("x") + lax.axis_index("y")

    def body(t_ref, a_ref, wo_ref, cq_ref, cw_ref, *refs):
        wg_ref, wc_ref, wog_ref, cqg_ref, cwg_ref = refs[5:]
        wg_ref[...] = a_ref[...]
        wc_ref[...] = a_ref[...]

        @pl.when(pl.program_id(0) == 0)
        def _():
            wog_ref[0] = wo_ref[...]
            cqg_ref[0] = cq_ref[...]
            cwg_ref[0] = cw_ref[...]

    whole = lambda s: pl.BlockSpec(s.shape, lambda b, t: (0,) * s.ndim)
    slot = lambda s: pl.BlockSpec((1,) + s.shape, lambda b, t: (t[3, 0],) + (0,) * s.ndim)
    return pl.pallas_call(
        body, name="place_own",
        grid_spec=pltpu.PrefetchScalarGridSpec(
            num_scalar_prefetch=1, grid=(ALIGNED_BLOCKS,),
            in_specs=[pl.BlockSpec((DH, d), lambda b, t: (b, 0)), whole(wo), whole(cq), whole(cw)] + [ANY] * 5,
            out_specs=[pl.BlockSpec((DH, d), lambda b, t: (t[0, b], 0)),
                       pl.BlockSpec((DH, d), lambda b, t: (t[1, b], 0)), slot(wo), slot(cq), slot(cw)]),
        out_shape=[jax.ShapeDtypeStruct(b.shape, b.dtype) for b in bufs],
        input_output_aliases={5 + a: a for a in range(5)},
        compiler_params=_params(("arbitrary",)),
    )(_block_table(chip, True, G_SPARE, C_SPARE), a_shard, wo, cq, cw, *bufs)


def _tie(x, token, name):
    def body(x_ref, t_ref, o_ref):
        del x_ref, t_ref, o_ref

    return pl.pallas_call(
        body, name=name, in_specs=[ANY, ANY], out_specs=ANY,
        out_shape=jax.ShapeDtypeStruct(x.shape, x.dtype), input_output_aliases={0: 0},
    )(x, token)


def _gather_start(phase, a_shard, w_grp, singles):
    ns = len(singles)

    def issue(refs, send_sems, recv_sems):
        a_ref, w_ref = refs[0], refs[1]
        x, y, c, chips = _place()
        mine = 2 * x + y
        for jj, (px, py) in enumerate(chips):
            to = dict(device_id=(px, py, c), device_id_type=MESH)
            for a in range(ns):
                pltpu.make_async_remote_copy(
                    src_ref=refs[2 + 2 * a], dst_ref=refs[3 + 2 * a].at[mine],
                    send_sem=send_sems.at[(1 + ns) * jj + 1 + a], recv_sem=recv_sems.at[(1 + ns) * jj + 1 + a],
                    **to).start()
        for s in range(4):
            for par in range(2):
                blocks = _phase_blocks(s, phase, True, par)
                if blocks:
                    @pl.when((mine == s) & (c == par))
                    def _():
                        for jj, (px, py) in enumerate(chips):
                            for b, blk in blocks:
                                pltpu.make_async_remote_copy(
                                    src_ref=_blk(a_ref, b), dst_ref=_blk(w_ref, blk),
                                    send_sem=send_sems.at[(1 + ns) * jj], recv_sem=recv_sems.at[(1 + ns) * jj],
                                    device_id=(px, py, c), device_id_type=MESH).start()

    bufs = [a_shard, w_grp] + [t for pair in singles for t in pair]
    return _split_start("gather_start_" + phase, issue, bufs, 3 * (1 + ns))


def _gather_wait(phase, send_sems, recv_sems, bufs, after):
    ns = (len(bufs) - 2) // 2

    def await_(refs, send_sems, recv_sems):
        a_ref, w_ref = refs[0], refs[1]
        x, y, c, chips = _place()
        mine = 2 * x + y
        for jj, (px, py) in enumerate(chips):
            to = dict(device_id=(px, py, c), device_id_type=MESH)
            peer = 2 * px + py
            for a in range(ns):
                cp = pltpu.make_async_remote_copy(
                    src_ref=refs[2 + 2 * a], dst_ref=refs[3 + 2 * a].at[mine],
                    send_sem=send_sems.at[(1 + ns) * jj + 1 + a], recv_sem=recv_sems.at[(1 + ns) * jj + 1 + a], **to)
                cp.wait_recv()
                cp.wait_send()
            for s in range(4):
                for par in range(2):
                    nblk = len(_phase_blocks(s, phase, True, par))
                    if nblk:
                        both = pltpu.make_async_remote_copy(
                            src_ref=_cols(a_ref, nblk), dst_ref=_cols(w_ref, nblk),
                            send_sem=send_sems.at[(1 + ns) * jj], recv_sem=recv_sems.at[(1 + ns) * jj], **to)

                        @pl.when((peer == s) & (c == par))
                        def _():
                            both.wait_recv()

                        @pl.when((mine == s) & (c == par))
                        def _():
                            both.wait_send()

    return _split_wait("gather_wait_" + phase, await_, send_sems, recv_sems, bufs, after)


def _sibling_forward_parts(phase):
    def each(w_ref, send_sems, recv_sems, start):
        x, y, c, chips = _place()
        to = dict(device_id=(x, y, 1 - c), device_id_type=MESH)
        for jj, (px, py) in enumerate(chips):
            peer = 2 * px + py
            for s in range(4):
                for par in range(2):
                    mine_blocks = _phase_blocks(s, phase, True, par)
                    theirs = len(_phase_blocks(s, phase, True, 1 - par))
                    if not (mine_blocks or theirs):
                        continue

                    @pl.when((peer == s) & (c == par))
                    def _():
                        if start:
                            for _, blk in mine_blocks:
                                pltpu.make_async_remote_copy(
                                    src_ref=_blk(w_ref, blk), dst_ref=_blk(w_ref, blk),
                                    send_sem=send_sems.at[jj], recv_sem=recv_sems.at[jj], **to).start()
                            return
                        if theirs:
                            pltpu.make_async_remote_copy(
                                src_ref=_cols(w_ref, theirs), dst_ref=_cols(w_ref, theirs),
                                send_sem=send_sems.at[jj], recv_sem=recv_sems.at[jj], **to).wait_recv()
                        if mine_blocks:
                            pltpu.make_async_remote_copy(
                                src_ref=_cols(w_ref, len(mine_blocks)), dst_ref=_cols(w_ref, len(mine_blocks)),
                                send_sem=send_sems.at[jj], recv_sem=recv_sems.at[jj], **to).wait_send()

    issue = lambda refs, send_sems, recv_sems: each(refs[0], send_sems, recv_sems, True)
    await_ = lambda refs, send_sems, recv_sems: each(refs[0], send_sems, recv_sems, False)
    return issue, await_


def _sibling_forward(phase, w_grp):
    issue, await_ = _sibling_forward_parts(phase)

    def body(w_in_ref, w_ref, send_sems, recv_sems):
        del w_in_ref
        issue([w_ref], send_sems, recv_sems)
        await_([w_ref], send_sems, recv_sems)

    return pl.pallas_call(
        body, name="sibling_forward_" + phase, in_specs=[ANY], out_specs=ANY,
        out_shape=jax.ShapeDtypeStruct(w_grp.shape, w_grp.dtype), input_output_aliases={0: 0},
        scratch_shapes=[pltpu.SemaphoreType.DMA((3,)), pltpu.SemaphoreType.DMA((3,))],
    )(w_grp)


def _merge_edges(w, edge0, mixed, name):
    d = w.shape[1]

    def body(e_ref, o_ref):
        o_ref[...] = e_ref[0:DH, :] + e_ref[DH:2 * DH, :]

    def to_block(i):
        r = mixed[-1]
        for kk in range(len(mixed) - 2, -1, -1):
            r = jnp.where(i == kk, mixed[kk], r)
        return r

    return pl.pallas_call(
        body, name=name, grid=(len(mixed),),
        in_specs=[pl.BlockSpec((2 * DH, d), lambda i: (edge0 // 2 + i, 0))],
        out_specs=pl.BlockSpec((DH, d), lambda i: (to_block(i), 0)),
        out_shape=jax.ShapeDtypeStruct(w.shape, w.dtype),
        input_output_aliases={0: 0},
        compiler_params=_params(("arbitrary",)),
    )(w)


def _scatter_start(phase, g_grp, land, singles, halved=False):
    ns = len(singles)

    def issue(refs, send_sems, recv_sems):
        g_ref, land_ref = refs[0], refs[1]
        x, y, c, chips = _place()
        for jj, (px, py) in enumerate(chips):
            to = dict(device_id=(px, py, c), device_id_type=MESH)
            peer = 2 * px + py
            for a in range(ns):
                pltpu.make_async_remote_copy(
                    src_ref=refs[2 + 2 * a].at[peer], dst_ref=refs[3 + 2 * a].at[jj],
                    send_sem=send_sems.at[(1 + ns) * jj + 1 + a], recv_sem=recv_sems.at[(1 + ns) * jj + 1 + a],
                    **to).start()
            for s in range(4):
                for par in ((0, 1) if halved else (None,)):
                    blocks = _phase_blocks(s, phase, False, par)
                    if blocks:
                        @pl.when((peer == s) if par is None else ((peer == s) & (c == par)))
                        def _():
                            for b, blk in blocks:
                                pltpu.make_async_remote_copy(
                                    src_ref=_blk(g_ref, blk), dst_ref=_blk(land_ref.at[jj], b),
                                    send_sem=send_sems.at[(1 + ns) * jj], recv_sem=recv_sems.at[(1 + ns) * jj],
                                    **to).start()

    bufs = [g_grp, land] + [t for pair in singles for t in pair]
    return _split_start("scatter_start_" + phase, issue, bufs, 3 * (1 + ns))


def _scatter_wait(phase, send_sems, recv_sems, bufs, after, halved=False):
    ns = (len(bufs) - 2) // 2

    def await_(refs, send_sems, recv_sems):
        g_ref, land_ref = refs[0], refs[1]
        x, y, c, chips = _place()
        mine = 2 * x + y
        for jj, (px, py) in enumerate(chips):
            to = dict(device_id=(px, py, c), device_id_type=MESH)
            peer = 2 * px + py
            for a in range(ns):
                cp = pltpu.make_async_remote_copy(
                    src_ref=refs[2 + 2 * a].at[peer], dst_ref=refs[3 + 2 * a].at[jj],
                    send_sem=send_sems.at[(1 + ns) * jj + 1 + a], recv_sem=recv_sems.at[(1 + ns) * jj + 1 + a], **to)
                cp.wait_recv()
                cp.wait_send()
            for s in range(4):
                for par in ((0, 1) if halved else (None,)):
                    nblk = len(_phase_blocks(s, phase, False, par))
                    if nblk:
                        both = pltpu.make_async_remote_copy(
                            src_ref=_cols(g_ref, nblk), dst_ref=_cols(land_ref.at[jj], nblk),
                            send_sem=send_sems.at[(1 + ns) * jj], recv_sem=recv_sems.at[(1 + ns) * jj], **to)

                        @pl.when((mine == s) if par is None else ((mine == s) & (c == par)))
                        def _():
                            both.wait_recv()

                        @pl.when((peer == s) if par is None else ((peer == s) & (c == par)))
                        def _():
                            both.wait_send()

    return _split_wait("scatter_wait_" + phase, await_, send_sems, recv_sems, bufs, after)


def _needed_blocks(phase, parity):
    return sorted({blk for s in range(4) for _, blk in _phase_blocks(s, phase, False, parity)})


def _pair_reduce(phase, g_grp):
    n, d = g_grp.shape

    def swap(g_ref, sib_ref, send_sem, recv_sem):
        x, y, c, _ = _place()
        to = dict(device_id=(x, y, 1 - c), device_id_type=MESH)
        for par in range(2):
            give, get = _needed_blocks(phase, 1 - par), _needed_blocks(phase, par)

            @pl.when(c == par)
            def _():
                for blk in give:
                    pltpu.make_async_remote_copy(src_ref=_blk(g_ref, blk), dst_ref=_blk(sib_ref, blk),
                                                 send_sem=send_sem, recv_sem=recv_sem, **to).start()
                pltpu.make_async_remote_copy(src_ref=_cols(g_ref, len(get)), dst_ref=_cols(sib_ref, len(get)),
                                             send_sem=send_sem, recv_sem=recv_sem, **to).wait_recv()
                pltpu.make_async_remote_copy(src_ref=_cols(g_ref, len(give)), dst_ref=_cols(sib_ref, len(give)),
                                             send_sem=send_sem, recv_sem=recv_sem, **to).wait_send()

    sib = pl.pallas_call(
        swap, name="pair_swap_" + phase, in_specs=[ANY], out_specs=ANY,
        out_shape=jax.ShapeDtypeStruct((n, d), g_grp.dtype),
        scratch_shapes=[pltpu.SemaphoreType.DMA, pltpu.SemaphoreType.DMA],
    )(*_in_hbm(g_grp))

    lists = [_needed_blocks(phase, par) for par in range(2)]
    longest = max(len(t) for t in lists)
    table = jnp.asarray([t + [t[-1]] * (longest - len(t)) for t in lists], jnp.int32)[lax.axis_index("c")]

    def add(t_ref, a_ref, b_ref, o_ref):
        o_ref[...] = (a_ref[...].astype(F32) + b_ref[...].astype(F32)).astype(o_ref.dtype)

    blk = pl.BlockSpec((DH, d), lambda i, t: (t[i], 0))
    return pl.pallas_call(
        add, name="pair_add_" + phase,
        grid_spec=pltpu.PrefetchScalarGridSpec(num_scalar_prefetch=1, grid=(longest,),
                                               in_specs=[blk, blk], out_specs=blk),
        out_shape=jax.ShapeDtypeStruct((n, d), g_grp.dtype),
        compiler_params=_params(("arbitrary",)),
    )(table, g_grp, sib)


def _sum_shard(g_g, g_c, land):
    d = g_g.shape[1]
    chip = 2 * lax.axis_index("x") + lax.axis_index("y")

    def body(t_ref, gg_ref, gc_ref, land_ref, o_ref):
        b = pl.program_id(0)
        in_g = t_ref[2, b] == 1
        own = jnp.where(in_g, gg_ref[...].astype(F32), gc_ref[...].astype(F32))
        for jj in range(3):
            own = own + land_ref[jj].astype(F32)
        o_ref[...] = jnp.where(in_g & (b % 2 != lax.axis_index("c")), 0.0, own)

    return pl.pallas_call(
        body, name="sum_w_in",
        grid_spec=pltpu.PrefetchScalarGridSpec(
            num_scalar_prefetch=1, grid=(ALIGNED_BLOCKS,),
            in_specs=[pl.BlockSpec((DH, d), lambda b, t: (t[0, b], 0)), pl.BlockSpec((DH, d), lambda b, t: (t[1, b], 0)),
                      pl.BlockSpec((3, DH, d), lambda b, t: (0, b, 0))],
            out_specs=pl.BlockSpec((DH, d), lambda b, t: (b, 0))),
        out_shape=jax.ShapeDtypeStruct((ALIGNED_W, d), F32),
        compiler_params=_params(("arbitrary",)),
    )(_block_table(chip, False, 0, 0), g_g, g_c, land)


def _sum_rows(stack, land, rows):
    _, r, d = stack.shape
    rows = min(rows, r)
    chip = 2 * lax.axis_index("x") + lax.axis_index("y")

    def body(t_ref, own_ref, land_ref, o_ref):
        acc = own_ref[0].astype(F32)
        for jj in range(3):
            acc = acc + land_ref[jj].astype(F32)
        o_ref[...] = acc

    return pl.pallas_call(
        body, name="sum_w_out",
        grid_spec=pltpu.PrefetchScalarGridSpec(
            num_scalar_prefetch=1, grid=(r // rows,),
            in_specs=[pl.BlockSpec((1, rows, d), lambda i, t: (t[0], i, 0)),
                      pl.BlockSpec((3, rows, d), lambda i, t: (0, i, 0))],
            out_specs=pl.BlockSpec((rows, d), lambda i, t: (i, 0))),
        out_shape=jax.ShapeDtypeStruct((r, d), F32),
        compiler_params=_params(("arbitrary",)),
    )(jnp.reshape(chip, (1,)).astype(jnp.int32), stack, land)


def _exchange_parts(n_swap, with_pack):
    def copies(refs, send_sems, recv_sems):
        x, y, c, _ = _place()
        me = 4 * x + 2 * y + c
        cps = [pltpu.make_async_remote_copy(
            src_ref=refs[2 * a], dst_ref=refs[2 * a + 1], send_sem=send_sems.at[a], recv_sem=recv_sems.at[a],
            device_id=(x, y, 1 - c), device_id_type=MESH) for a in range(n_swap)]
        if with_pack:
            pack_ref, packs = refs[2 * n_swap], refs[2 * n_swap + 1]
            for r in range(1, 8):
                dx, dy, dc = (r >> 2) & 1, (r >> 1) & 1, r & 1
                peer = (x + dx - 2 * x * dx, y + dy - 2 * y * dy, c + dc - 2 * c * dc)
                cps.append(pltpu.make_async_remote_copy(
                    src_ref=pack_ref, dst_ref=packs.at[me], send_sem=send_sems.at[n_swap + r - 1],
                    recv_sem=recv_sems.at[n_swap + r - 1], device_id=peer, device_id_type=MESH))
        return cps

    def issue(refs, send_sems, recv_sems):
        for cp in copies(refs, send_sems, recv_sems):
            cp.start()

    def await_(refs, send_sems, recv_sems):
        cps = copies(refs, send_sems, recv_sems)
        for cp in cps:
            cp.wait_recv()
        for cp in cps:
            cp.wait_send()

    return issue, await_, n_swap + (7 if with_pack else 0)


def _sum_packs(pack, packs):
    x, y, c = lax.axis_index("x"), lax.axis_index("y"), lax.axis_index("c")
    me = jnp.reshape(4 * x + 2 * y + c, (1,)).astype(jnp.int32)

    def body(me_ref, own_ref, p_ref, o_ref):
        acc = jnp.where(me_ref[0] == 0, own_ref[...], p_ref[0])
        for d in range(1, 8):
            acc = acc + jnp.where(me_ref[0] == d, own_ref[...], p_ref[d])
        o_ref[...] = acc

    full = lambda s: pl.BlockSpec(s.shape, lambda i, t: (0,) * s.ndim)
    return pl.pallas_call(
        body, name="sum_packs",
        grid_spec=pltpu.PrefetchScalarGridSpec(num_scalar_prefetch=1, grid=(1,), in_specs=[full(pack), full(packs)],
                                               out_specs=full(pack)),
        out_shape=jax.ShapeDtypeStruct(pack.shape, F32),
    )(me, pack, packs)


def _adamw_update(g, w_ref, m_ref, v_ref, go, do, mo, vo):
    c1 = 1.0 / (1.0 - ADAM_B1 ** ADAM_STEP)
    c2 = 1.0 / (1.0 - ADAM_B2 ** ADAM_STEP)
    mn = ADAM_B1 * m_ref[...] + (1.0 - ADAM_B1) * g
    vn = ADAM_B2 * v_ref[...] + (1.0 - ADAM_B2) * (g * g)
    go[...] = g
    mo[...] = mn
    vo[...] = vn
    do[...] = -ADAM_LR * ((mn * c1) / (jnp.sqrt(vn * c2) + ADAM_EPS) + ADAM_WD * w_ref[...])


def _adamw(w, m, v, g1, g2, rows, name):
    r, cdim = w.shape
    rows = min(rows, r)

    def body(*refs):
        n_in = 4 if g2 is None else 5
        w_ref, m_ref, v_ref, g_ref = refs[:4]
        g = g_ref[...] if g2 is None else g_ref[...] + refs[4][...]
        _adamw_update(g, w_ref, m_ref, v_ref, *refs[n_in:n_in + 4])

    blk = pl.BlockSpec((rows, cdim), lambda i: (i, 0))
    args = [w, m, v, g1] + ([] if g2 is None else [g2])
    shp = jax.ShapeDtypeStruct((r, cdim), F32)
    return pl.pallas_call(
        body, name=name, grid=(r // rows,),
        in_specs=[blk] * len(args), out_specs=[blk] * 4, out_shape=[shp] * 4,
        compiler_params=_params(("parallel",), 20 * rows * cdim * 4 + 8 * 2**20),
    )(*_in_hbm(*args))


def _adamw_shard(wt, mt, vt, g1, g2):
    r, d = wt.shape
    cols = min(128, d)

    def body(w_ref, m_ref, v_ref, g_ref, g2_ref, go, do, mo, vo, pad_ref):
        chip = 2 * lax.axis_index("x") + lax.axis_index("y")
        back = [(ALIGNED_W - s) % ALIGNED_W for s in SHIFTS]
        pad_ref[...] = pltpu.roll(g_ref[...] + g2_ref[...], _by_chip(chip, back), 0)
        outs = [o.at[:, 0, :] for o in (go, do, mo, vo)]
        _adamw_update(pad_ref[0:r, :], w_ref, m_ref, v_ref, *outs)

    blk = pl.BlockSpec((r, cols), lambda i: (0, i))
    gblk = pl.BlockSpec((ALIGNED_W, cols), lambda i: (0, i))
    oblk = pl.BlockSpec((r, 1, cols), lambda i: (0, 0, i))
    shp = jax.ShapeDtypeStruct((r, 1, d), F32)
    return pl.pallas_call(
        body, name="adamw_w_in", grid=(d // cols,),
        in_specs=[blk] * 3 + [gblk] * 2, out_specs=[oblk] * 4, out_shape=[shp] * 4,
        scratch_shapes=[pltpu.VMEM((ALIGNED_W, cols), F32)],
        compiler_params=_params(("parallel",), 24 * ALIGNED_W * cols * 4 + 8 * 2**20),
    )(wt, mt, vt, g1, g2)


def _pad_lanes(a, width):
    return jnp.pad(a, ((0, 0), (0, width - a.shape[1])))


def _gathered_to_full(g):
    return jnp.transpose(g, (1, 0, 2)).reshape(g.shape[1], 4 * g.shape[2])


def _row(a):
    return _pad_lanes(a.reshape(1, -1), 1024)


def _small_pack(nin, cb, fn, al, dt, gn, cqw_shard, cw_shard):
    ad = jnp.concatenate([al.reshape(1, -1), dt.reshape(1, -1)], axis=1)
    rows = [_row(nin), _row(cb), _row(fn), _row(ad), _row(gn), cqw_shard.reshape(3, 1024), _row(cw_shard)]
    out = jnp.concatenate(rows, axis=0)
    return jnp.pad(out, ((0, 16 - out.shape[0]), (0, 0)))


def kernel(x, norm_in_w, w_in, conv_qkv_w, A_log, dt_bias, gdn_norm_w, conv_w, conv_b, w_out, final_norm_w, loss_target, m_norm_in_w, m_w_in, m_conv_qkv_w, m_A_log, m_dt_bias, m_gdn_norm_w, m_conv_w, m_conv_b, m_w_out, m_final_norm_w, v_norm_in_w, v_w_in, v_conv_qkv_w, v_A_log, v_dt_bias, v_gdn_norm_w, v_conv_w, v_conv_b, v_w_out, v_final_norm_w):
    chip = 2 * lax.axis_index("x") + lax.axis_index("y")
    a_shard = _align_shard(jnp.transpose(w_in, (2, 0, 1)))
    wo_b = _cast_bf16(w_out[0], 256, "cast_w_out")
    d_model = x.shape[-1]
    stack = lambda s: lax.empty((4,) + s.shape, s.dtype)
    wg0 = lax.empty((WG_BLOCKS * DH, d_model), BF16)
    wc0 = lax.empty((WC_BLOCKS * DH, d_model), BF16)
    ss_g, rs_g, bufs_g, tok_g = _gather_start("g", a_shard, wg0, [(conv_qkv_w[0], stack(conv_qkv_w[0]))])
    ss_c, rs_c, bufs_c, tok_c = _gather_start("c", bufs_g[0], wc0,
                                              [(conv_w[0], stack(conv_w[0])), (wo_b, stack(wo_b))])
    wg1, wc1, wog1, cqg1, cwg1 = _place_own(bufs_c[0], bufs_c[4], bufs_g[2], bufs_c[2],
                                            [bufs_g[1], bufs_c[1], bufs_c[5], bufs_g[3], bufs_c[3]])
    x0 = x[0]
    h = _rms_in(x0, _tie(_tie(norm_in_w, tok_g, "after_gather_start_g"), tok_c, "after_gather_start_c"))
    adam_in = [jnp.transpose(a[0]) for a in (w_in, m_w_in, v_w_in)]
    sp = lambda nin, cb, fn, al, dt, gn, cq, cwv: _small_pack(nin, cb, fn, al, dt, gn, cq[0], cwv[0])
    w_s = sp(norm_in_w, conv_b, final_norm_w, A_log, dt_bias, gdn_norm_w, conv_qkv_w, conv_w)
    m_s = sp(m_norm_in_w, m_conv_b, m_final_norm_w, m_A_log, m_dt_bias, m_gdn_norm_w, m_conv_qkv_w, m_conv_w)
    v_s = sp(v_norm_in_w, v_conv_b, v_final_norm_w, v_A_log, v_dt_bias, v_gdn_norm_w, v_conv_qkv_w, v_conv_w)
    a_thru, wg, _, cq_g = _gather_wait("g", ss_g, rs_g, [bufs_c[0], wg1, bufs_g[2], cqg1],
                                       [h, w_s, m_s, v_s] + adam_in[1:])
    w_g = _merge_edges(_sibling_forward("g", wg), G_EDGE, G_MIXED, "merge_edges_g")
    cqw = _gathered_to_full(cq_g)
    ad = jnp.pad(jnp.concatenate([A_log, dt_bias], axis=0), ((0, 0), (A_LANE, 0)))
    fwd_c = {}

    def on_q(q):
        _, wc, _, cw_g, _, wo_g = _gather_wait("c", ss_c, rs_c,
                                               [a_thru, wc1, bufs_c[2], cwg1, bufs_c[4], wog1], q)
        issue, _ = _sibling_forward_parts("c")
        ss, rs, (wc,), tok = _split_start("sibling_forward_start_c", issue, [wc], 3)
        fwd_c.update(ss=ss, rs=rs, wc=wc, cw_g=cw_g, wo_g=wo_g)
        return _tie(q, tok, "after_sibling_forward_start_c")

    def late(o):
        _, await_ = _sibling_forward_parts("c")
        (wc,) = _split_wait("sibling_forward_wait_c", await_, fwd_c["ss"], fwd_c["rs"], [fwd_c["wc"]], o)
        return (_merge_edges(wc, C_EDGE, C_MIXED, "merge_edges_c"), fwd_c["wo_g"].reshape(2 * GW, d_model),
                _gathered_to_full(fwd_c["cw_g"]))

    scat = {}

    def on_grad_c(g_c, g_wout, do):
        go4 = g_wout.reshape(4, GW // 2, d_model)
        land = lax.empty((3, ALIGNED_W, d_model), BF16)
        land_o = lax.empty((3, GW // 2, d_model), BF16)
        ss, rs, bufs, tok = _scatter_start("c", g_c, land, [(go4, land_o)])
        scat["c"] = (ss, rs, bufs)
        return _tie(do, tok, "after_scatter_start_c")

    def on_grad_g(g_g, dproj_g):
        ss, rs, bufs, tok = _scatter_start("g", _pair_reduce("g", g_g), scat["c"][2][1], [], halved=True)
        scat["g"] = (ss, rs, bufs)
        return _tie(dproj_g, tok, "after_scatter_start_g")

    gx, sm, _ = _local_step(x0, loss_target[0], h, w_g, cqw, late, norm_in_w, ad, gdn_norm_w, conv_b,
                            final_norm_w.reshape(1, -1), on_grad_c, on_grad_g, on_q)

    ss, rs, bufs = scat["c"]
    g_c, land, go4, land_o = _scatter_wait("c", ss, rs, [bufs[0], scat["g"][2][1], bufs[2], bufs[3]], gx)
    part_out = _sum_rows(go4, land_o, 128)
    ad_g = jnp.concatenate([sm["al"][:, A_LANE:], sm["dt"][:, A_LANE:]], axis=1)
    pack = jnp.concatenate([_row(sm["nin"]), _row(sm["cb"]), _row(sm["fn"]), _row(ad_g), _row(sm["gn"]),
                            jnp.concatenate(sm["cq"], axis=1).reshape(12, 1024), sm["cw"], _row(sm["loss"])], axis=0)
    pack = jnp.pad(pack, ((0, PACK_ROWS - pack.shape[0]), (0, 0)))
    issue, await_a, nsem = _exchange_parts(1, True)
    ss_a, rs_a, bufs_a, tok_a = _split_start(
        "exchange_start_small", issue,
        [part_out, lax.empty(part_out.shape, F32), pack, lax.empty((8,) + pack.shape, F32)], nsem)
    ss, rs, bufs = scat["g"]
    g_g, land = _scatter_wait("g", ss, rs, [bufs[0], land], [gx, tok_a], halved=True)
    part_in = _sum_shard(g_g, g_c, land)
    issue, await_b, nsem = _exchange_parts(1, False)
    ss_b, rs_b, bufs_b, tok_b = _split_start("exchange_start_w_in", issue,
                                             [part_in, lax.empty(part_in.shape, F32)], nsem)
    part_out, sib_out, pack, packs = _split_wait("exchange_wait_small", await_a, ss_a, rs_a, bufs_a, tok_b)
    tot = _sum_packs(pack, packs)
    g_wo, d_wo, m_wo, v_wo = _adamw(w_out[0], m_w_out[0], v_w_out[0], part_out, sib_out, 128, "adamw_w_out")
    g_cq_sh = lax.dynamic_slice_in_dim(tot[R_CQ:R_CQ + 12].reshape(4, 3 * GW), chip * 768, 768, axis=1)
    g_cw_sh = lax.dynamic_slice_in_dim(tot[R_CW:R_CW + 3], chip * 256, 256, axis=1)
    g_s = _small_pack(tot[R_NIN], tot[R_CB], tot[R_FN], tot[R_AD, :HEADS], tot[R_AD, HEADS:2 * HEADS],
                      tot[R_GN, :DH], g_cq_sh, g_cw_sh)
    small = _adamw(w_s, m_s, v_s, g_s, None, 16, "adamw_small")
    part_in, sib_in = _split_wait("exchange_wait_w_in", await_b, ss_b, rs_b, bufs_b, [small[0], d_wo])
    g_wi, d_wi, m_wi, v_wi = [jnp.transpose(a, (1, 2, 0))[0] for a in _adamw_shard(*adam_in, part_in, sib_in)]

    def unpack(a, big_in, big_out):
        return (a[0:1], big_in[None], a[5:8].reshape(1, 4, 768), a[3:4, :HEADS], a[3:4, HEADS:2 * HEADS],
                a[4:5, :DH], a[8, :768].reshape(1, 3, 256), a[1:2], big_out[None], a[2])

    loss = tot[R_LOSS, 0]
    return (loss, gx[None], *unpack(small[0], g_wi, g_wo), *unpack(small[1], d_wi, d_wo),
            *unpack(small[2], m_wi, m_wo), *unpack(small[3], v_wi, v_wo))
```

```python
import functools
import math

import jax
import jax.numpy as jnp
from jax import lax
from jax.experimental import pallas as pl
from jax.experimental.pallas import tpu as pltpu

F32 = jnp.float32
BF16 = jnp.bfloat16
MESH = pl.DeviceIdType.MESH
ANY = pl.BlockSpec(memory_space=pl.ANY)

HEADS = 8
DH = 128
CH = 64
GW = HEADS * DH
EPS = 1e-6
VMEM_V7X = 64 * 1024 * 1024

QB, KB, VB, ZB, BAB = 0, 8, 16, 24, 32
A_LANE = 120
NG, NC = 33, 32
GW_COLS, CW_COLS = NG * DH, NC * DH

SHARD_W = 2052
ALIGNED_BLOCKS = 17
ALIGNED_W = ALIGNED_BLOCKS * DH
SHIFTS = (0, 4, ALIGNED_W - 8, ALIGNED_W - 4)
G_EDGE, C_EDGE = 34, 32
G_SPARE, C_SPARE = 33, 34
WG_BLOCKS, WC_BLOCKS = 38, 36
G_MIXED, C_MIXED = (2, BAB), (4 * 7 + 1,)


def _shard_blocks(chip, edges):
    g, c = "g", "c"
    if chip == 0:
        out = [(g, 3 * b) for b in range(8)] + [(g, 3 * b + 1) for b in range(8)] + [(g, G_EDGE, G_MIXED[0])]
    elif chip == 1:
        out = [(g, G_EDGE + 1, G_MIXED[0])] + [(g, 3 * b + 2) for b in range(1, 8)]
        out += [(g, ZB + b) for b in range(8)] + [(g, G_EDGE + 2, G_MIXED[1])]
    elif chip == 2:
        out = [(c, 4 * b) for b in range(8)] + [(c, 4 * b + 1) for b in range(7)]
        out += [(c, C_EDGE, C_MIXED[0]), (g, G_EDGE + 3, G_MIXED[1])]
    else:
        out = [(c, 4 * b + 2) for b in range(8)] + [(c, 4 * b + 3) for b in range(8)] + [(c, C_EDGE + 1, C_MIXED[0])]
    return [(o[0], o[1] if (edges or len(o) == 2) else o[2]) for o in out]


def _by_chip(chip, vals):
    if all(v == vals[0] for v in vals):
        return vals[0]
    r = vals[3]
    for kk in (2, 1, 0):
        r = jnp.where(chip == kk, vals[kk], r)
    return r

ADAM_LR, ADAM_B1, ADAM_B2, ADAM_EPS, ADAM_WD, ADAM_STEP = 0.001, 0.9, 0.999, 1e-08, 0.01, 10

R_NIN, R_CB, R_FN, R_AD, R_GN, R_CQ, R_CW, R_LOSS, PACK_ROWS = 0, 1, 2, 3, 4, 5, 17, 20, 24

NN = ((1,), (0,))
NT = ((1,), (1,))
TN = ((0,), (0,))


def _dot(a, b, dims=NN, mode="lo"):
    dn = (dims, ((), ()))
    if mode == "hi":
        return lax.dot_general(a, b, dn, precision=lax.Precision.HIGHEST, preferred_element_type=F32)
    ah, bh = a.astype(BF16), b.astype(BF16)
    out = lax.dot_general(ah, bh, dn, preferred_element_type=F32)
    if mode == "x3":
        al = (a - ah.astype(F32)).astype(BF16)
        bl = (b - bh.astype(F32)).astype(BF16)
        out = out + lax.dot_general(ah, bl, dn, preferred_element_type=F32)
        out = out + lax.dot_general(al, bh, dn, preferred_element_type=F32)
    return out


P_GRAM, P_INV, P_SOL, P_SCAN, P_SCANB, P_BWD = "lo", "lo", "lo", "lo", "lo", "lo"
P_CUM = "x3"


def _params(sem=None, vmem=None):
    kw = {}
    if sem is not None:
        kw["dimension_semantics"] = sem
    if vmem is not None:
        kw["vmem_limit_bytes"] = int(min(max(vmem, 32 * 2**20), VMEM_V7X - 8 * 2**20))
    return pltpu.CompilerParams(**kw)


def _in_hbm(*arrays):
    return [pltpu.with_memory_space_constraint(a, pltpu.HBM) for a in arrays]


def _sigmoid(x):
    return 1.0 / (1.0 + jnp.exp(-x))


def _dsilu(x, s):
    return s * (1.0 + x * (1.0 - s))


def _rows(shape):
    return lax.broadcasted_iota(jnp.int32, shape, 0)


def _shift_down(x, s):
    if s == 0:
        return x
    return jnp.where(_rows(x.shape) >= s, pltpu.roll(x, s, 0), 0.0)


def _shift_up(x, s):
    if s == 0:
        return x
    n = x.shape[0]
    return jnp.where(_rows(x.shape) < n - s, pltpu.roll(x, n - s, 0), 0.0)


def _matmul(a, b, dims, out_dtype, tm, tn, tk, name, add=None, n=None, b_outer=False):
    if dims == NN:
        (m, k), n = a.shape, b.shape[1]
    elif dims == NT:
        (m, k), n = a.shape, (n or b.shape[0])
    else:
        (k, m), n = a.shape, b.shape[1]
    tm, tn, tk = min(tm, m), min(tn, n), min(tk, k)
    assert m % tm == 0 and n % tn == 0 and k % tk == 0, (name, m, n, k, tm, tn, tk)
    nk = k // tk

    def body(*refs):
        if add is None:
            a_ref, b_ref, o_ref = refs[:3]
            add_ref = None
        else:
            a_ref, b_ref, add_ref, o_ref = refs[:4]
        part = _dot(a_ref[...], b_ref[...], dims)
        if nk == 1:
            if add_ref is not None:
                part = part + add_ref[...]
            o_ref[...] = part.astype(out_dtype)
            return
        acc = refs[-1]
        kk = pl.program_id(2)

        @pl.when(kk == 0)
        def _():
            acc[...] = part

        @pl.when(kk > 0)
        def _():
            acc[...] += part

        @pl.when(kk == nk - 1)
        def _():
            r = acc[...]
            if add_ref is not None:
                r = r + add_ref[...]
            o_ref[...] = r.astype(out_dtype)

    ij = (lambda g0, g1: (g1, g0)) if b_outer else (lambda g0, g1: (g0, g1))

    def spec(shape, pick):
        return pl.BlockSpec(shape, lambda g0, g1, kk: pick(*ij(g0, g1), kk))

    a_spec = spec((tk, tm), lambda i, j, kk: (kk, i)) if dims == TN else spec((tm, tk), lambda i, j, kk: (i, kk))
    b_spec = spec((tn, tk), lambda i, j, kk: (j, kk)) if dims == NT else spec((tk, tn), lambda i, j, kk: (kk, j))
    o_spec = spec((tm, tn), lambda i, j, kk: (i, j))
    in_specs = [a_spec, b_spec]
    args = [a, b]
    if add is not None:
        in_specs.append(o_spec)
        args.append(add)
    osz = jnp.dtype(out_dtype).itemsize
    est = 2 * (tm * tk * a.dtype.itemsize + tk * tn * b.dtype.itemsize + tm * tn * osz)
    est += 3 * tm * tn * 4 + (2 * tm * tn * 4 if add is not None else 0)
    return pl.pallas_call(
        body, name=name, grid=(n // tn, m // tm, nk) if b_outer else (m // tm, n // tn, nk),
        in_specs=in_specs, out_specs=o_spec,
        out_shape=jax.ShapeDtypeStruct((m, n), out_dtype),
        scratch_shapes=[pltpu.VMEM((tm, tn), F32)] if nk > 1 else [],
        compiler_params=_params(("parallel", "parallel", "arbitrary"), est + 8 * 2**20),
    )(*args)


def _cast_bf16(a, rows, name):
    r, c = a.shape
    rows = min(rows, r)

    def body(a_ref, o_ref):
        o_ref[...] = a_ref[...].astype(BF16)

    return pl.pallas_call(
        body, name=name, grid=(r // rows,),
        in_specs=[pl.BlockSpec((rows, c), lambda i: (i, 0))],
        out_specs=pl.BlockSpec((rows, c), lambda i: (i, 0)),
        out_shape=jax.ShapeDtypeStruct((r, c), BF16),
        compiler_params=_params(("parallel",)),
    )(a)


def _align_shard(wt):
    r, _, d = wt.shape
    cols = min(256, d)

    def body(w_ref, o_ref, pad_ref):
        chip = 2 * lax.axis_index("x") + lax.axis_index("y")
        pad_ref[...] = jnp.zeros_like(pad_ref)
        pad_ref[0:r, :] = w_ref[:, 0, :]
        o_ref[...] = pltpu.roll(pad_ref[...], _by_chip(chip, SHIFTS), 0).astype(BF16)

    return pl.pallas_call(
        body, name="align_shard", grid=(d // cols,),
        in_specs=[pl.BlockSpec((r, 1, cols), lambda i: (0, 0, i))],
        out_specs=pl.BlockSpec((ALIGNED_W, cols), lambda i: (0, i)),
        out_shape=jax.ShapeDtypeStruct((ALIGNED_W, d), BF16),
        scratch_shapes=[pltpu.VMEM((ALIGNED_W, cols), F32)],
        compiler_params=_params(("parallel",)),
    )(wt)


def _rms_in(x, w):
    n, d = x.shape
    tr = min(256, n)

    def body(x_ref, w_ref, h_ref):
        xv = x_ref[...]
        r = lax.rsqrt(jnp.mean(xv * xv, axis=-1, keepdims=True) + EPS)
        h_ref[...] = (xv * r * w_ref[...]).astype(BF16)

    return pl.pallas_call(
        body, name="rms_in", grid=(n // tr,),
        in_specs=[pl.BlockSpec((tr, d), lambda i: (i, 0)), pl.BlockSpec((1, d), lambda i: (0, 0))],
        out_specs=pl.BlockSpec((tr, d), lambda i: (i, 0)),
        out_shape=jax.ShapeDtypeStruct((n, d), BF16),
        compiler_params=_params(("parallel",)),
    )(x, w)


def _conv_silu(p, w_ref, taps):
    c = None
    for j in range(taps):
        t = _shift_down(p, taps - 1 - j) * w_ref[j:j + 1, :]
        c = t if c is None else c + t
    return c


def _prep_qkv(proj, cw):
    n = proj.shape[0]

    def body(p3, wq, wk, wv, q_ref, k_ref, v_ref):
        for kind, (w_ref, o_ref) in enumerate(((wq, q_ref), (wk, k_ref), (wv, v_ref))):
            c = _conv_silu(p3[:, kind * DH:(kind + 1) * DH], w_ref, 4)
            a = c * _sigmoid(c)
            if kind < 2:
                r = lax.rsqrt(jnp.sum(a * a, axis=-1, keepdims=True) + EPS)
                a = a * (r * (DH ** -0.5 if kind == 0 else 1.0))
            o_ref[...] = a

    col = pl.BlockSpec((n, DH), lambda h: (0, h))
    wcol = lambda base: pl.BlockSpec((4, DH), lambda h: (0, base + h))
    out = jax.ShapeDtypeStruct((n, GW), F32)
    return pl.pallas_call(
        body, name="prep_qkv", grid=(HEADS,),
        in_specs=[pl.BlockSpec((n, 3 * DH), lambda h: (0, h)), wcol(QB), wcol(KB), wcol(VB)],
        out_specs=[col] * 3, out_shape=[out] * 3,
        compiler_params=_params(("parallel",), 40 * 2**20),
    )(proj, cw, cw, cw)


def _prep_qkv_bwd(proj, cw, dq, dk, dv, dproj):
    n = proj.shape[0]

    def body(p3, wq, wk, wv, dq_ref, dk_ref, dv_ref, _, o3, gq, gk, gv):
        for kind, (w_ref, d_ref, g_ref) in enumerate(((wq, dq_ref, gq), (wk, dk_ref, gk), (wv, dv_ref, gv))):
            p = p3[:, kind * DH:(kind + 1) * DH]
            shifted = [_shift_down(p, 3 - j) for j in range(4)]
            c = shifted[0] * w_ref[0:1, :]
            for j in range(1, 4):
                c = c + shifted[j] * w_ref[j:j + 1, :]
            s = _sigmoid(c)
            a = c * s
            d = d_ref[...]
            if kind < 2:
                r = lax.rsqrt(jnp.sum(a * a, axis=-1, keepdims=True) + EPS)
                sc = DH ** -0.5 if kind == 0 else 1.0
                d = (sc * r) * (d - a * ((r * r) * jnp.sum(d * a, axis=-1, keepdims=True)))
            dc = d * _dsilu(c, s)
            dp = None
            for j in range(4):
                g_ref[j:j + 1, :] = jnp.sum(dc * shifted[j], axis=0, keepdims=True)
                t = _shift_up(dc, 3 - j) * w_ref[j:j + 1, :]
                dp = t if dp is None else dp + t
            o3[:, kind * DH:(kind + 1) * DH] = dp.astype(BF16)

    col = pl.BlockSpec((n, DH), lambda h: (0, h))
    wcol = lambda base: pl.BlockSpec((4, DH), lambda h: (0, base + h))
    p3spec = pl.BlockSpec((n, 3 * DH), lambda h: (0, h))
    return pl.pallas_call(
        body, name="prep_qkv_bwd", grid=(HEADS,),
        in_specs=[p3spec, wcol(QB), wcol(KB), wcol(VB), col, col, col, ANY],
        out_specs=[p3spec] + [wcol(0)] * 3,
        out_shape=[jax.ShapeDtypeStruct(dproj.shape, BF16)] + [jax.ShapeDtypeStruct((4, GW), F32)] * 3,
        input_output_aliases={7: 0},
        compiler_params=_params(("parallel",), 48 * 2**20),
    )(proj, cw, cw, cw, dq, dk, dv, dproj)


CPB = 8
SCAN_CPS = 4


def _tri(lower, rows):
    i = lax.broadcasted_iota(jnp.int32, (rows, rows), 0)
    j = lax.broadcasted_iota(jnp.int32, (rows, rows), 1)
    return jnp.where((i // CH == j // CH) & ((i >= j) if lower else (j >= i)), 1.0, 0.0)


def _lane(shape):
    return lax.broadcasted_iota(jnp.int32, shape, 1)


def _prep_bg(proj, ad):
    n = proj.shape[0]
    nch = n // CH
    cpb = CPB if nch % CPB == 0 else 1
    rows = cpb * CH

    def body(p_ref, ad_ref, bg_ref, bgt_ref):
        p = p_ref[...]
        lane = _lane(p.shape)
        beta = _sigmoid(p)
        xa = p + ad_ref[1:2, :]
        sp = jnp.maximum(xa, 0.0) + jnp.log(1.0 + jnp.exp(-jnp.abs(xa)))
        g = pltpu.roll(-jnp.exp(ad_ref[0:1, :]) * sp, DH - A_LANE + HEADS, 1)
        gc = _dot(_tri(True, rows), g, NN, P_CUM)
        bg = jnp.where(lane < HEADS, beta, jnp.where(lane < 2 * HEADS, gc, 0.0))
        bg_ref[...] = bg
        for ci in range(cpb):
            bgt_ref[ci] = bg[ci * CH:(ci + 1) * CH, :].T

    return pl.pallas_call(
        body, name="prep_bg", grid=(nch // cpb,),
        in_specs=[pl.BlockSpec((rows, DH), lambda i: (i, BAB)), pl.BlockSpec((2, DH), lambda i: (0, 0))],
        out_specs=[pl.BlockSpec((rows, DH), lambda i: (i, 0)), pl.BlockSpec((cpb, DH, CH), lambda i: (i, 0, 0))],
        out_shape=[jax.ShapeDtypeStruct((n, DH), F32), jax.ShapeDtypeStruct((nch, DH, CH), F32)],
        compiler_params=_params(("parallel",)),
    )(*_in_hbm(proj, ad))


def _prep_bg_bwd(proj, ad, dbg, dproj):
    n = proj.shape[0]
    nch = n // CH
    cpb = CPB if nch % CPB == 0 else 1
    rows = cpb * CH

    def body(p_ref, ad_ref, d_ref, _, o_ref, ga_ref, gd_ref):
        p = p_ref[...]
        d = d_ref[...]
        lane = _lane(p.shape)
        beta = _sigmoid(p)
        xa = p + ad_ref[1:2, :]
        sp = jnp.maximum(xa, 0.0) + jnp.log(1.0 + jnp.exp(-jnp.abs(xa)))
        na = -jnp.exp(ad_ref[0:1, :])
        dg = pltpu.roll(_dot(_tri(False, rows), d, NN, P_CUM), A_LANE - HEADS, 1)
        da = dg * na * _sigmoid(xa)
        is_g = lane >= A_LANE
        o_ref[...] = jnp.where(lane < HEADS, d * beta * (1.0 - beta), jnp.where(is_g, da, 0.0)).astype(BF16)
        ga = jnp.sum(jnp.where(is_g, dg * na * sp, 0.0), axis=0, keepdims=True)
        gd = jnp.sum(jnp.where(is_g, da, 0.0), axis=0, keepdims=True)

        @pl.when(pl.program_id(0) == 0)
        def _():
            ga_ref[...] = jnp.zeros_like(ga_ref)
            gd_ref[...] = jnp.zeros_like(gd_ref)

        ga_ref[...] += ga
        gd_ref[...] += gd

    one = pl.BlockSpec((1, DH), lambda i: (0, 0))
    return pl.pallas_call(
        body, name="prep_bg_bwd", grid=(nch // cpb,),
        in_specs=[pl.BlockSpec((rows, DH), lambda i: (i, BAB)), pl.BlockSpec((2, DH), lambda i: (0, 0)),
                  pl.BlockSpec((rows, DH), lambda i: (i, 0)), ANY],
        out_specs=[pl.BlockSpec((rows, DH), lambda i: (i, BAB)), one, one],
        out_shape=[jax.ShapeDtypeStruct(dproj.shape, BF16), jax.ShapeDtypeStruct((1, DH), F32),
                   jax.ShapeDtypeStruct((1, DH), F32)],
        input_output_aliases={3: 0},
        compiler_params=_params(("arbitrary",)),
    )(proj, ad, dbg, dproj)


def _gdn_out(o, proj, wg):
    n = o.shape[0]

    def body(o_ref, z_ref, w_ref, y_ref):
        ov, z = o_ref[...], z_ref[...]
        r = lax.rsqrt(jnp.mean(ov * ov, axis=-1, keepdims=True) + EPS)
        y_ref[...] = (ov * r * w_ref[...] * (z * _sigmoid(z))).astype(BF16)

    return pl.pallas_call(
        body, name="gdn_out", grid=(HEADS,),
        in_specs=[pl.BlockSpec((n, DH), lambda h: (0, h)), pl.BlockSpec((n, DH), lambda h: (0, ZB + h)),
                  pl.BlockSpec((1, DH), lambda h: (0, 0))],
        out_specs=pl.BlockSpec((n, DH), lambda h: (0, h)),
        out_shape=jax.ShapeDtypeStruct((n, 2 * GW), BF16),
        compiler_params=_params(("parallel",)),
    )(o, proj, wg)


def _gdn_out_bwd(o, proj, wg, dmix):
    n = o.shape[0]

    def body(o_ref, z_ref, w_ref, d_ref, do_ref, dz_ref, gw_ref):
        ov, z, d, w = o_ref[...], z_ref[...], d_ref[...], w_ref[...]
        r = lax.rsqrt(jnp.mean(ov * ov, axis=-1, keepdims=True) + EPS)
        nrm = ov * r
        s = _sigmoid(z)
        dz_ref[...] = (d * (nrm * w) * _dsilu(z, s)).astype(BF16)
        dn_w = d * (z * s)
        gw = jnp.sum(dn_w * nrm, axis=0, keepdims=True)
        dn = dn_w * w
        do_ref[...] = r * (dn - nrm * jnp.mean(dn * nrm, axis=-1, keepdims=True))

        @pl.when(pl.program_id(0) == 0)
        def _():
            gw_ref[...] = jnp.zeros_like(gw_ref)

        gw_ref[...] += gw

    return pl.pallas_call(
        body, name="gdn_out_bwd", grid=(HEADS,),
        in_specs=[pl.BlockSpec((n, DH), lambda h: (0, h)), pl.BlockSpec((n, DH), lambda h: (0, ZB + h)),
                  pl.BlockSpec((1, DH), lambda h: (0, 0)), pl.BlockSpec((n, DH), lambda h: (0, h))],
        out_specs=[pl.BlockSpec((n, DH), lambda h: (0, h)), pl.BlockSpec((n, DH), lambda h: (0, ZB + h)),
                   pl.BlockSpec((1, DH), lambda h: (0, 0))],
        out_shape=[jax.ShapeDtypeStruct((n, GW), F32), jax.ShapeDtypeStruct((n, GW_COLS), BF16),
                   jax.ShapeDtypeStruct((1, DH), F32)],
        compiler_params=_params(("arbitrary",)),
    )(o, proj, wg, dmix)


def _conv_branch(proj, w3, b, mix):
    n = proj.shape[0]

    def body(p4, w_ref, b_ref, _, y_ref):
        u = p4[:, DH:2 * DH] * p4[:, 2 * DH:3 * DH]
        cc = _conv_silu(u, w_ref, 3) + b_ref[...]
        z = p4[:, 3 * DH:4 * DH]
        y_ref[...] = (p4[:, 0:DH] * cc * (z * _sigmoid(z))).astype(BF16)

    return pl.pallas_call(
        body, name="conv_branch", grid=(HEADS,),
        in_specs=[pl.BlockSpec((n, 4 * DH), lambda h: (0, h)), pl.BlockSpec((3, DH), lambda h: (0, h)),
                  pl.BlockSpec((1, DH), lambda h: (0, h)), ANY],
        out_specs=pl.BlockSpec((n, DH), lambda h: (0, HEADS + h)),
        out_shape=jax.ShapeDtypeStruct(mix.shape, BF16),
        input_output_aliases={3: 0},
        compiler_params=_params(("parallel",), 40 * 2**20),
    )(*_in_hbm(proj, w3, b, mix))


def _conv_branch_bwd(proj, w3, b, dmix):
    n = proj.shape[0]

    def body(p4, w_ref, b_ref, d_ref, o4, gw_ref, gbias_ref):
        gb, gcv, hc, z = p4[:, 0:DH], p4[:, DH:2 * DH], p4[:, 2 * DH:3 * DH], p4[:, 3 * DH:4 * DH]
        d = d_ref[...]
        dgb, dgc, dhc, dzc = (o4.at[:, kk * DH:(kk + 1) * DH] for kk in range(4))
        u = gcv * hc
        cc = _conv_silu(u, w_ref, 3) + b_ref[...]
        s = _sigmoid(z)
        dzc[...] = (d * (gb * cc) * _dsilu(z, s)).astype(BF16)
        dp = d * (z * s)
        dgb[...] = (dp * cc).astype(BF16)
        dcc = dp * gb
        gbias_ref[...] = jnp.sum(dcc, axis=0, keepdims=True)
        du = None
        for j in range(3):
            gw_ref[j:j + 1, :] = jnp.sum(dcc * _shift_down(u, 2 - j), axis=0, keepdims=True)
            t = _shift_up(dcc, 2 - j) * w_ref[j:j + 1, :]
            du = t if du is None else du + t
        dgc[...] = (du * hc).astype(BF16)
        dhc[...] = (du * gcv).astype(BF16)

    p4spec = pl.BlockSpec((n, 4 * DH), lambda h: (0, h))
    return pl.pallas_call(
        body, name="conv_branch_bwd", grid=(HEADS,),
        in_specs=[p4spec, pl.BlockSpec((3, DH), lambda h: (0, h)), pl.BlockSpec((1, DH), lambda h: (0, h)),
                  pl.BlockSpec((n, DH), lambda h: (0, HEADS + h))],
        out_specs=[p4spec, pl.BlockSpec((3, DH), lambda h: (0, h)), pl.BlockSpec((1, DH), lambda h: (0, h))],
        out_shape=[jax.ShapeDtypeStruct((n, CW_COLS), BF16), jax.ShapeDtypeStruct((3, GW), F32),
                   jax.ShapeDtypeStruct((1, GW), F32)],
        compiler_params=_params(("parallel",), 48 * 2**20),
    )(proj, w3, b, dmix)


def _out_loss(mix, w_out, x, tgt, wf):
    n, d = x.shape
    kdim = mix.shape[1]
    tr = min(256, n)

    def body(m_ref, wo_ref, x_ref, t_ref, w_ref, do_ref, dob_ref, gw_ref, loss_ref):
        ov = _dot(m_ref[...], wo_ref[...], NN) + x_ref[...]
        w = w_ref[...]
        r = lax.rsqrt(jnp.mean(ov * ov, axis=-1, keepdims=True) + EPS)
        nrm = ov * r
        e = nrm * w - t_ref[...]
        dy = e * (1.0 / d)
        dn = dy * w
        dout = r * (dn - nrm * jnp.mean(dn * nrm, axis=-1, keepdims=True))
        do_ref[...] = dout
        dob_ref[...] = dout.astype(BF16)

        @pl.when(pl.program_id(0) == 0)
        def _():
            gw_ref[...] = jnp.zeros_like(gw_ref)
            loss_ref[...] = jnp.zeros_like(loss_ref)

        gw_ref[...] += jnp.sum(dy * nrm, axis=0, keepdims=True)
        loss_ref[...] += (0.5 / d) * jnp.sum(jnp.sum(e * e, axis=-1, keepdims=True), axis=0, keepdims=True)

    row = pl.BlockSpec((tr, d), lambda i: (i, 0))
    return pl.pallas_call(
        body, name="out_loss", grid=(n // tr,),
        in_specs=[pl.BlockSpec((tr, kdim), lambda i: (i, 0)), pl.BlockSpec((kdim, d), lambda i: (0, 0)), row, row,
                  pl.BlockSpec((1, d), lambda i: (0, 0))],
        out_specs=[row, row, pl.BlockSpec((1, d), lambda i: (0, 0)), pl.BlockSpec((1, 1), lambda i: (0, 0))],
        out_shape=[jax.ShapeDtypeStruct((n, d), F32), jax.ShapeDtypeStruct((n, d), BF16),
                   jax.ShapeDtypeStruct((1, d), F32), jax.ShapeDtypeStruct((1, 1), F32)],
        compiler_params=_params(("arbitrary",), 40 * 2**20),
    )(mix, w_out, x, tgt, wf)


def _rms_in_bwd(x, w, dh, dout):
    n, d = x.shape
    tr = min(256, n)

    def body(x_ref, w_ref, dh_ref, do_ref, dx_ref, gw_ref):
        xv, dhv = x_ref[...], dh_ref[...]
        r = lax.rsqrt(jnp.mean(xv * xv, axis=-1, keepdims=True) + EPS)
        xn = xv * r
        dxn = dhv * w_ref[...]
        dx_ref[...] = r * (dxn - xn * jnp.mean(dxn * xn, axis=-1, keepdims=True)) + do_ref[...]

        @pl.when(pl.program_id(0) == 0)
        def _():
            gw_ref[...] = jnp.zeros_like(gw_ref)

        gw_ref[...] += jnp.sum(dhv * xn, axis=0, keepdims=True)

    row = pl.BlockSpec((tr, d), lambda i: (i, 0))
    one = pl.BlockSpec((1, d), lambda i: (0, 0))
    return pl.pallas_call(
        body, name="rms_in_bwd", grid=(n // tr,),
        in_specs=[row, one, row, row], out_specs=[row, one],
        out_shape=[jax.ShapeDtypeStruct((n, d), F32), jax.ShapeDtypeStruct((1, d), F32)],
        compiler_params=_params(("arbitrary",)),
    )(*_in_hbm(x, w, dh, dout))


def _ij():
    i = lax.broadcasted_iota(jnp.int32, (CH, CH), 0)
    j = lax.broadcasted_iota(jnp.int32, (CH, CH), 1)
    return i, j


def _unit_lower_inverse(mats):
    i, j = _ij()
    eye = jnp.where(i == j, 1.0, 0.0)
    same16 = (i // 16) == (j // 16)
    same32 = (i // 32) == (j // 32)
    mm = lambda xs, ys: [_dot(x, y, NN, P_INV) for x, y in zip(xs, ys)]
    n1 = [jnp.where(same16, -a, 0.0) for a in mats]
    n2 = mm(n1, n1)
    n4 = mm(n2, n2)
    n8 = mm(n4, n4)
    t = [eye + x1 + x2 + x3 for x1, x2, x3 in zip(n1, n2, mm(n1, n2))]
    t = [x + y for x, y in zip(t, mm(t, n4))]
    t = [x + y for x, y in zip(t, mm(t, n8))]
    a1 = [jnp.where(same32 & jnp.logical_not(same16), a, 0.0) for a in mats]
    t = [x - y for x, y in zip(t, mm(t, mm(a1, t)))]
    a2 = [jnp.where(same32, 0.0, a) for a in mats]
    t = [x - y for x, y in zip(t, mm(t, mm(a2, t)))]
    return t


def _head_vectors(bg, bgt, h):
    bcol = bg[:, h:h + 1]
    gcol = bg[:, HEADS + h:HEADS + h + 1]
    grow = bgt[HEADS + h:HEADS + h + 1, :]
    return bcol, gcol, grow


def _decay(gcol, grow):
    i, j = _ij()
    return jnp.where(i >= j, jnp.exp(jnp.where(i >= j, gcol - grow, 0.0)), 0.0)


def _gdn_intra(q, k, v, bg, bgt):
    n = q.shape[0]
    nch = n // CH
    cps = 4 if nch % 4 == 0 else 1

    def body(q_ref, k_ref, v_ref, bg_ref, bgt_ref, u_ref, w_ref, p_ref, t_ref):
        i, j = _ij()
        items = [(ci, h) for ci in range(cps) for h in range(HEADS)]
        at = lambda ref, ci, h: ref.at[ci * CH:(ci + 1) * CH, h * DH:(h + 1) * DH]
        bgs = [bg_ref[ci * CH:(ci + 1) * CH, :] for ci in range(cps)]
        ks = [at(k_ref, ci, h)[...] for ci, h in items]
        vecs = [_head_vectors(bgs[ci], bgt_ref[ci], h) for ci, h in items]
        decs = [_decay(gcol, grow) for _, gcol, grow in vecs]
        kks = [_dot(kh, kh, NT, P_GRAM) for kh in ks]
        qks = [_dot(at(q_ref, ci, h)[...], kh, NT, P_GRAM) for (ci, h), kh in zip(items, ks)]
        ts = _unit_lower_inverse([jnp.where(i > j, bcol * kk * dec, 0.0)
                                  for (bcol, _, _), kk, dec in zip(vecs, kks, decs)])
        us = [_dot(t, at(v_ref, ci, h)[...] * bcol, NN, P_SOL) for t, (ci, h), (bcol, _, _) in zip(ts, items, vecs)]
        ws = [_dot(t, kh * (bcol * jnp.exp(gcol)), NN, P_SOL) for t, kh, (bcol, gcol, _) in zip(ts, ks, vecs)]
        for n_, (ci, h) in enumerate(items):
            p_ref[ci, h] = qks[n_] * decs[n_]
            t_ref[ci, h] = ts[n_]
            at(u_ref, ci, h)[...] = us[n_]
            at(w_ref, ci, h)[...] = ws[n_]

    row = pl.BlockSpec((cps * CH, GW), lambda c: (c, 0))
    sq = pl.BlockSpec((cps, HEADS, CH, CH), lambda c: (c, 0, 0, 0))
    big = jax.ShapeDtypeStruct((n, GW), F32)
    sqs = jax.ShapeDtypeStruct((nch, HEADS, CH, CH), F32)
    return pl.pallas_call(
        body, name="gdn_intra", grid=(nch // cps,),
        in_specs=[row, row, row, pl.BlockSpec((cps * CH, DH), lambda c: (c, 0)),
                  pl.BlockSpec((cps, DH, CH), lambda c: (c, 0, 0))],
        out_specs=[row, row, sq, sq], out_shape=[big, big, sqs, sqs],
        compiler_params=_params(("parallel",)),
    )(q, k, v, bg, bgt)


def _gdn_scan(q, k, bg, u, w, p):
    n = q.shape[0]
    nch = n // CH
    cps = SCAN_CPS if nch % SCAN_CPS == 0 else 1

    def body(q_ref, k_ref, bg_ref, u_ref, w_ref, p_ref, o_ref, vn_ref, s_out, s_scr):
        @pl.when(pl.program_id(0) == 0)
        def _():
            s_scr[...] = jnp.zeros_like(s_scr)

        hs = range(HEADS)
        sls = [slice(h * DH, (h + 1) * DH) for h in hs]
        ss = [s_scr[h] for h in hs]
        for ci in range(cps):
            rs = slice(ci * CH, (ci + 1) * CH)
            bg = bg_ref[rs, :]
            gcols = [bg[:, HEADS + h:HEADS + h + 1] for h in hs]
            glasts = [g[CH - 1:CH, :] for g in gcols]
            wss = [_dot(w_ref[rs, sl], s, NN, P_SCAN) for sl, s in zip(sls, ss)]
            oqs = [_dot(q_ref[rs, sl] * jnp.exp(g), s, NN, P_SCAN) for sl, s, g in zip(sls, ss, gcols)]
            vns = [u_ref[rs, sl] - x for sl, x in zip(sls, wss)]
            ops = [_dot(p_ref[ci, h], vn, NN, P_SCAN) for h, vn in zip(hs, vns)]
            sns = [_dot(k_ref[rs, sl] * jnp.exp(gl - g), vn, TN, P_SCAN)
                   for sl, gl, g, vn in zip(sls, glasts, gcols, vns)]
            for h, sl in enumerate(sls):
                s_out[ci, :, sl] = ss[h]
                vn_ref[rs, sl] = vns[h]
                o_ref[rs, sl] = oqs[h] + ops[h]
            ss = [s * jnp.exp(gl) + sn for s, gl, sn in zip(ss, glasts, sns)]
        for h in hs:
            s_scr[h] = ss[h]

    row = pl.BlockSpec((cps * CH, GW), lambda c: (c, 0))
    big = jax.ShapeDtypeStruct((n, GW), F32)
    return pl.pallas_call(
        body, name="gdn_scan", grid=(nch // cps,),
        in_specs=[row, row, pl.BlockSpec((cps * CH, DH), lambda c: (c, 0)), row, row,
                  pl.BlockSpec((cps, HEADS, CH, CH), lambda c: (c, 0, 0, 0))],
        out_specs=[row, row, pl.BlockSpec((cps, DH, GW), lambda c: (c, 0, 0))],
        out_shape=[big, big, jax.ShapeDtypeStruct((nch, DH, GW), F32)],
        scratch_shapes=[pltpu.VMEM((HEADS, DH, DH), F32)],
        compiler_params=_params(("arbitrary",)),
    )(q, k, bg, u, w, p)


def _gdn_scan_bwd(q, k, bg, w, p, vn, s_in, do):
    n = q.shape[0]
    nch = n // CH
    cps = SCAN_CPS if nch % SCAN_CPS == 0 else 1
    rev = lambda c: nch // cps - 1 - c

    def body(q_ref, k_ref, bg_ref, w_ref, p_ref, vn_ref, s_ref, do_ref,
             dqg_ref, dp_ref, du_ref, dw_ref, dks_ref, dgam_ref, ds_scr):
        @pl.when(pl.program_id(0) == 0)
        def _():
            ds_scr[...] = jnp.zeros_like(ds_scr)

        lane = _lane((1, DH))
        hs = range(HEADS)
        sls = [slice(h * DH, (h + 1) * DH) for h in hs]
        dss = [ds_scr[h] for h in hs]
        for ci in reversed(range(cps)):
            rs = slice(ci * CH, (ci + 1) * CH)
            bg = bg_ref[rs, :]
            gcols = [bg[:, HEADS + h:HEADS + h + 1] for h in hs]
            glasts = [g[CH - 1:CH, :] for g in gcols]
            ss = [s_ref[ci, :, sl] for sl in sls]
            dos = [do_ref[rs, sl] for sl in sls]
            vnl = [vn_ref[rs, sl] for sl in sls]
            dqgs = [_dot(d, s, NT, P_SCANB) for d, s in zip(dos, ss)]
            dps = [_dot(d, vn, NT, P_SCANB) for d, vn in zip(dos, vnl)]
            dvn1 = [_dot(p_ref[ci, h], d, TN, P_SCANB) for h, d in zip(hs, dos)]
            dvn2 = [_dot(k_ref[rs, sl] * jnp.exp(gl - g), ds, NN, P_SCANB)
                    for sl, gl, g, ds in zip(sls, glasts, gcols, dss)]
            dkss = [_dot(vn, ds, NT, P_SCANB) for vn, ds in zip(vnl, dss)]
            dsq = [_dot(q_ref[rs, sl] * jnp.exp(g), d, TN, P_SCANB) for sl, g, d in zip(sls, gcols, dos)]
            dvns = [a + b for a, b in zip(dvn1, dvn2)]
            dws = [_dot(dvn, s, NT, P_SCANB) for dvn, s in zip(dvns, ss)]
            dsw = [_dot(w_ref[rs, sl], dvn, TN, P_SCANB) for sl, dvn in zip(sls, dvns)]
            dgam = jnp.zeros((1, DH), F32)
            for h, sl in enumerate(sls):
                dqg_ref[rs, sl] = dqgs[h]
                dp_ref[ci, h] = dps[h]
                du_ref[rs, sl] = dvns[h]
                dw_ref[rs, sl] = -dws[h]
                dks_ref[rs, sl] = dkss[h]
                tot = jnp.sum(jnp.sum(dss[h] * ss[h], axis=-1, keepdims=True), axis=0, keepdims=True)
                dgam = dgam + jnp.where(lane == h, tot, 0.0)
            dgam_ref[ci] = jnp.broadcast_to(dgam, (8, DH))
            dss = [ds * jnp.exp(gl) + a - b for ds, gl, a, b in zip(dss, glasts, dsq, dsw)]
        for h in hs:
            ds_scr[h] = dss[h]

    row = pl.BlockSpec((cps * CH, GW), lambda c: (rev(c), 0))
    sq = pl.BlockSpec((cps, HEADS, CH, CH), lambda c: (rev(c), 0, 0, 0))
    big = jax.ShapeDtypeStruct((n, GW), F32)
    return pl.pallas_call(
        body, name="gdn_scan_bwd", grid=(nch // cps,),
        in_specs=[row, row, pl.BlockSpec((cps * CH, DH), lambda c: (rev(c), 0)), row, sq, row,
                  pl.BlockSpec((cps, DH, GW), lambda c: (rev(c), 0, 0)), row],
        out_specs=[row, sq, row, row, row, pl.BlockSpec((cps, 8, DH), lambda c: (rev(c), 0, 0))],
        out_shape=[big, jax.ShapeDtypeStruct((nch, HEADS, CH, CH), F32), big, big, big,
                   jax.ShapeDtypeStruct((nch, 8, DH), F32)],
        scratch_shapes=[pltpu.VMEM((HEADS, DH, DH), F32)],
        compiler_params=_params(("arbitrary",)),
    )(q, k, bg, w, p, vn, s_in, do)


def _gdn_intra_bwd(q, k, v, bg, bgt, t, u, w, p, dqg, dp, du, dw, dks, dgam):
    n = q.shape[0]
    nch = n // CH
    cps = 1

    def body(q_ref, k_ref, v_ref, bg_ref, bgt_ref, t_ref, u_ref, w_ref, p_ref,
             dqg_ref, dp_ref, du_ref, dw_ref, dks_ref, dgam_ref, dq_ref, dk_ref, dv_ref, dbg_ref):
        i, j = _ij()
        rows1 = lax.broadcasted_iota(jnp.int32, (CH, 1), 0)
        lane = _lane((CH, DH))
        rsum = lambda x: jnp.sum(x, axis=-1, keepdims=True)
        items = [(ci, h) for ci in range(cps) for h in range(HEADS)]
        at = lambda ref, it: ref.at[it[0] * CH:(it[0] + 1) * CH, it[1] * DH:(it[1] + 1) * DH]
        ld = lambda ref: [at(ref, it)[...] for it in items]
        bgs = [bg_ref[ci * CH:(ci + 1) * CH, :] for ci in range(cps)]
        qs, ks = ld(q_ref), ld(k_ref)
        vecs = [_head_vectors(bgs[ci], bgt_ref[ci], h) for ci, h in items]
        decs = [_decay(gcol, grow) for _, gcol, grow in vecs]
        ths = [t_ref[ci, h] for ci, h in items]
        drus = [_dot(th, x_, TN, P_BWD) for th, x_ in zip(ths, ld(du_ref))]
        drws = [_dot(th, x_, TN, P_BWD) for th, x_ in zip(ths, ld(dw_ref))]
        kks = [_dot(kh, kh, NT, P_GRAM) for kh in ks]
        da1 = [_dot(dru, x_, NT, P_BWD) for dru, x_ in zip(drus, ld(u_ref))]
        da2 = [_dot(drw, x_, NT, P_BWD) for drw, x_ in zip(drws, ld(w_ref))]
        das = [jnp.where(i > j, -(x_ + y_), 0.0) for x_, y_ in zip(da1, da2)]
        dkks = [da * bcol * dec for da, (bcol, _, _), dec in zip(das, vecs, decs)]
        dps = [dp_ref[ci, h] for ci, h in items]
        dqks = [dp_ * dec for dp_, dec in zip(dps, decs)]
        dq_ps = [_dot(dqk, kh, NN, P_BWD) for dqk, kh in zip(dqks, ks)]
        dk_ps = [_dot(dqk, qh, TN, P_BWD) for dqk, qh in zip(dqks, qs)]
        dk_as = [_dot(dkk, kh, NN, P_BWD) for dkk, kh in zip(dkks, ks)]
        dk_bs = [_dot(dkk, kh, TN, P_BWD) for dkk, kh in zip(dkks, ks)]
        bcols = [vc[0] for vc in vecs]
        gcols = [vc[1] for vc in vecs]
        gams = [jnp.exp(g) for g in gcols]
        glasts = [g[CH - 1:CH, :] for g in gcols]
        es = [jnp.exp(gl - g) for gl, g in zip(glasts, gcols)]
        kgs = [kh * gam for kh, gam in zip(ks, gams)]
        dqgs, dkss = ld(dqg_ref), ld(dks_ref)
        r_uv = [rsum(dru * x_) for dru, x_ in zip(drus, ld(v_ref))]
        r_wk = [rsum(drw * kg) for drw, kg in zip(drws, kgs)]
        r_ak = [rsum(da * kk * dec) for da, kk, dec in zip(das, kks, decs)]
        r_qq = [rsum(dqg * qh) for dqg, qh in zip(dqgs, qs)]
        tks = [rsum(dk_ * kh) * e for dk_, kh, e in zip(dkss, ks, es)]
        mdecs = [da * (bcol * kk * dec) + dp_ * p_ref[ci, h]
                 for (ci, h), da, bcol, kk, dec, dp_ in zip(items, das, bcols, kks, decs, dps)]
        r_md = [rsum(m) for m in mdecs]
        c_md = [rsum(jnp.where(i == j, jnp.sum(m, axis=0, keepdims=True), 0.0)) for m in mdecs]
        dbgs = [jnp.zeros((CH, DH), F32) for _ in range(cps)]
        for n_, (ci, h) in enumerate(items):
            at(dv_ref, (ci, h))[...] = bcols[n_] * drus[n_]
            at(dq_ref, (ci, h))[...] = gams[n_] * dqgs[n_] + dq_ps[n_]
            at(dk_ref, (ci, h))[...] = ((bcols[n_] * gams[n_]) * drws[n_] + dk_ps[n_] + dk_as[n_] + dk_bs[n_]
                                        + dkss[n_] * es[n_])
            dbeta = r_uv[n_] + r_wk[n_] + r_ak[n_]
            dglast = (jnp.sum(tks[n_], axis=0, keepdims=True)
                      + dgam_ref[ci, 0:1, h:h + 1] * jnp.exp(glasts[n_]))
            dgc = (r_wk[n_] * bcols[n_] + r_md[n_] - c_md[n_] + r_qq[n_] * gams[n_] - tks[n_]
                   + jnp.where(rows1 == CH - 1, dglast, 0.0))
            dbgs[ci] = dbgs[ci] + jnp.where(lane == h, dbeta, 0.0) + jnp.where(lane == HEADS + h, dgc, 0.0)
        for ci in range(cps):
            dbg_ref[ci * CH:(ci + 1) * CH, :] = dbgs[ci]

    row = pl.BlockSpec((cps * CH, GW), lambda c: (c, 0))
    sq = pl.BlockSpec((cps, HEADS, CH, CH), lambda c: (c, 0, 0, 0))
    small = pl.BlockSpec((cps * CH, DH), lambda c: (c, 0))
    big = jax.ShapeDtypeStruct((n, GW), F32)
    return pl.pallas_call(
        body, name="gdn_intra_bwd", grid=(nch // cps,),
        in_specs=[row, row, row, small, pl.BlockSpec((cps, DH, CH), lambda c: (c, 0, 0)), sq, row, row, sq,
                  row, sq, row, row, row, pl.BlockSpec((cps, 8, DH), lambda c: (c, 0, 0))],
        out_specs=[row, row, row, small],
        out_shape=[big, big, big, jax.ShapeDtypeStruct((n, DH), F32)],
        compiler_params=_params(("parallel",)),
    )(q, k, v, bg, bgt, t, u, w, p, dqg, dp, du, dw, dks, dgam)


def _local_step(x, tgt, h, w_g, cqw, late, norm_in_w, ad, gdn_norm_w, conv_b, final_norm_w,
                on_grad_c=None, on_grad_g=None, on_q=None):
    proj_g = _matmul(h, w_g, NT, F32, 512, 1408, 1024, "mm_proj_g", n=GW_COLS, b_outer=True)
    q, k, v = _prep_qkv(proj_g, cqw)
    if on_q is not None:
        q = on_q(q)
    bg, bgt = _prep_bg(proj_g, ad)
    u, w, p, t = _gdn_intra(q, k, v, bg, bgt)
    o, vn, s_in = _gdn_scan(q, k, bg, u, w, p)
    w_c, w_out, conv_w = late(o)
    proj_c = _matmul(h, w_c, NT, F32, 512, 1024, 1024, "mm_proj_c", n=CW_COLS, b_outer=True)
    mix = _conv_branch(proj_c, conv_w, conv_b, _gdn_out(o, proj_g, gdn_norm_w))
    dout, dout_b, g_fn, loss = _out_loss(mix, w_out, x, tgt, final_norm_w)

    dmix = _matmul(dout_b, w_out, NT, F32, 512, 1024, 1024, "mm_dmix", b_outer=True)
    g_wout = _matmul(mix, dout_b, TN, BF16, 512, 512, 2048, "mm_gwout")
    do, dproj_g, g_gn = _gdn_out_bwd(o, proj_g, gdn_norm_w, dmix)
    dproj_c, g_cw, g_cb = _conv_branch_bwd(proj_c, conv_w, conv_b, dmix)
    g_c = _matmul(dproj_c, h, TN, BF16, 1024, 512, 2048, "mm_gwin_c")
    if on_grad_c is not None:
        do = on_grad_c(g_c, g_wout, do)
    dqg, dp, du, dw, dks, dgam = _gdn_scan_bwd(q, k, bg, w, p, vn, s_in, do)
    dq, dk, dv, dbg = _gdn_intra_bwd(q, k, v, bg, bgt, t, u, w, p, dqg, dp, du, dw, dks, dgam)
    dproj_g, gq, gk, gv = _prep_qkv_bwd(proj_g, cqw, dq, dk, dv, dproj_g)
    dproj_g, g_al, g_dt = _prep_bg_bwd(proj_g, ad, dbg, dproj_g)
    g_g = _matmul(dproj_g, h, TN, BF16, 1408, 512, 2048, "mm_gwin_g")
    if on_grad_g is not None:
        dproj_g = on_grad_g(g_g, dproj_g)
    dh = _matmul(dproj_g, w_g, NN, F32, 1024, 1024, 1408, "mm_dh_g")
    dh = _matmul(dproj_c, w_c, NN, F32, 1024, 1024, 1024, "mm_dh_c", add=dh)
    gx, g_nin = _rms_in_bwd(x, norm_in_w, dh, dout)
    small = dict(nin=g_nin, cb=g_cb, fn=g_fn, al=g_al, dt=g_dt, gn=g_gn, cq=(gq, gk, gv), cw=g_cw, loss=loss)
    return gx, small, (g_g, g_c, g_wout)


def _place():
    x, y, c = lax.axis_index("x"), lax.axis_index("y"), lax.axis_index("c")
    chips = [(1 - x, y), (x, 1 - y), (1 - x, 1 - y)]
    return x, y, c, chips


def _blk(ref, b):
    if isinstance(b, int):
        return ref.at[b * DH:(b + 1) * DH, :]
    return ref.at[pl.ds(pl.multiple_of(b * DH, DH), DH), :]


HBM = pl.BlockSpec(memory_space=pltpu.HBM)
SEM = pl.BlockSpec(memory_space=pltpu.SEMAPHORE)
EFFECT = pltpu.SideEffectType.DATAFLOW_SIDE_EFFECTING


def _split_start(name, issue, bufs, n_sems):
    nbuf = len(bufs)

    def body(*refs):
        issue(refs[:nbuf], refs[nbuf], refs[nbuf + 1])
        refs[-1][...] = jnp.zeros_like(refs[-1])

    out = pl.pallas_call(
        body, name=name,
        out_shape=(pltpu.SemaphoreType.DMA((n_sems,)), pltpu.SemaphoreType.DMA((n_sems,)),
                   *[pltpu.HBM(b.shape, b.dtype) for b in bufs], jax.ShapeDtypeStruct((8, DH), F32)),
        in_specs=[HBM] * nbuf,
        out_specs=(SEM, SEM, *[HBM] * nbuf, pl.BlockSpec(memory_space=pltpu.VMEM)),
        input_output_aliases={a: 2 + a for a in range(nbuf)},
        compiler_params=pltpu.CompilerParams(has_side_effects=EFFECT),
    )(*[pltpu.with_memory_space_constraint(b, pltpu.HBM) for b in bufs])
    return out[0], out[1], list(out[2:2 + nbuf]), out[-1]


def _split_wait(name, await_, send_sems, recv_sems, bufs, after):
    nbuf = len(bufs)
    after = list(after) if isinstance(after, (list, tuple)) else [after]

    def body(*refs):
        await_(refs[:nbuf], refs[nbuf], refs[nbuf + 1])

    out = pl.pallas_call(
        body, name=name,
        out_shape=tuple(pltpu.HBM(b.shape, b.dtype) for b in bufs),
        in_specs=[HBM] * nbuf + [SEM, SEM] + [ANY] * len(after), out_specs=tuple([HBM] * nbuf),
        input_output_aliases={a: a for a in range(nbuf)},
        compiler_params=pltpu.CompilerParams(has_side_effects=EFFECT),
    )(*bufs, send_sems, recv_sems, *after)
    return list(out)


def _phase_blocks(chip, phase, edges, parity=None):
    return [(b, blk) for b, (grp, blk) in enumerate(_shard_blocks(chip, edges))
            if grp == phase and (parity is None or b % 2 == parity)]


def _cols(ref, nblk):
    return ref.at[0:nblk * DH, :]


def _block_table(chip, edges, spare_g, spare_c):
    rows = []
    for s in range(4):
        sb = _shard_blocks(s, edges)
        rows.append([[blk if grp == "g" else spare_g for grp, blk in sb],
                     [blk if grp == "c" else spare_c for grp, blk in sb],
                     [int(grp == "g") for grp, _ in sb], [s] * ALIGNED_BLOCKS])
    return jnp.asarray(rows, jnp.int32)[chip]


def _place_own(a_shard, wo, cq, cw, bufs):
    d = a_shard.shape[1]
    chip = 2 * lax.axis_index("x") + lax.axis_index("y")

    def body(t_ref, a_ref, wo_ref, cq_ref, cw_ref, *refs):
        wg_ref, wc_ref, wog_ref, cqg_ref, cwg_ref = refs[5:]
        wg_ref[...] = a_ref[...]
        wc_ref[...] = a_ref[...]

        @pl.when(pl.program_id(0) == 0)
        def _():
            wog_ref[0] = wo_ref[...]
            cqg_ref[0] = cq_ref[...]
            cwg_ref[0] = cw_ref[...]

    whole = lambda s: pl.BlockSpec(s.shape, lambda b, t: (0,) * s.ndim)
    slot = lambda s: pl.BlockSpec((1,) + s.shape, lambda b, t: (t[3, 0],) + (0,) * s.ndim)
    return pl.pallas_call(
        body, name="place_own",
        grid_spec=pltpu.PrefetchScalarGridSpec(
            num_scalar_prefetch=1, grid=(ALIGNED_BLOCKS,),
            in_specs=[pl.BlockSpec((DH, d), lambda b, t: (b, 0)), whole(wo), whole(cq), whole(cw)] + [ANY] * 5,
            out_specs=[pl.BlockSpec((DH, d), lambda b, t: (t[0, b], 0)),
                       pl.BlockSpec((DH, d), lambda b, t: (t[1, b], 0)), slot(wo), slot(cq), slot(cw)]),
        out_shape=[jax.ShapeDtypeStruct(b.shape, b.dtype) for b in bufs],
        input_output_aliases={5 + a: a for a in range(5)},
        compiler_params=_params(("arbitrary",)),
    )(_block_table(chip, True, G_SPARE, C_SPARE), a_shard, wo, cq, cw, *bufs)


def _tie(x, token, name):
    def body(x_ref, t_ref, o_ref):
        del x_ref, t_ref, o_ref

    return pl.pallas_call(
        body, name=name, in_specs=[ANY, ANY], out_specs=ANY,
        out_shape=jax.ShapeDtypeStruct(x.shape, x.dtype), input_output_aliases={0: 0},
    )(x, token)


def _gather_start(phase, a_shard, w_grp, singles):
    ns = len(singles)

    def issue(refs, send_sems, recv_sems):
        a_ref, w_ref = refs[0], refs[1]
        x, y, c, chips = _place()
        mine = 2 * x + y
        for jj, (px, py) in enumerate(chips):
            to = dict(device_id=(px, py, c), device_id_type=MESH)
            for a in range(ns):
                pltpu.make_async_remote_copy(
                    src_ref=refs[2 + 2 * a], dst_ref=refs[3 + 2 * a].at[mine],
                    send_sem=send_sems.at[(1 + ns) * jj + 1 + a], recv_sem=recv_sems.at[(1 + ns) * jj + 1 + a],
                    **to).start()
        for s in range(4):
            for par in range(2):
                blocks = _phase_blocks(s, phase, True, par)
                if blocks:
                    @pl.when((mine == s) & (c == par))
                    def _():
                        for jj, (px, py) in enumerate(chips):
                            for b, blk in blocks:
                                pltpu.make_async_remote_copy(
                                    src_ref=_blk(a_ref, b), dst_ref=_blk(w_ref, blk),
                                    send_sem=send_sems.at[(1 + ns) * jj], recv_sem=recv_sems.at[(1 + ns) * jj],
                                    device_id=(px, py, c), device_id_type=MESH).start()

    bufs = [a_shard, w_grp] + [t for pair in singles for t in pair]
    return _split_start("gather_start_" + phase, issue, bufs, 3 * (1 + ns))


def _gather_wait(phase, send_sems, recv_sems, bufs, after):
    ns = (len(bufs) - 2) // 2

    def await_(refs, send_sems, recv_sems):
        a_ref, w_ref = refs[0], refs[1]
        x, y, c, chips = _place()
        mine = 2 * x + y
        for jj, (px, py) in enumerate(chips):
            to = dict(device_id=(px, py, c), device_id_type=MESH)
            peer = 2 * px + py
            for a in range(ns):
                cp = pltpu.make_async_remote_copy(
                    src_ref=refs[2 + 2 * a], dst_ref=refs[3 + 2 * a].at[mine],
                    send_sem=send_sems.at[(1 + ns) * jj + 1 + a], recv_sem=recv_sems.at[(1 + ns) * jj + 1 + a], **to)
                cp.wait_recv()
                cp.wait_send()
            for s in range(4):
                for par in range(2):
                    nblk = len(_phase_blocks(s, phase, True, par))
                    if nblk:
                        both = pltpu.make_async_remote_copy(
                            src_ref=_cols(a_ref, nblk), dst_ref=_cols(w_ref, nblk),
                            send_sem=send_sems.at[(1 + ns) * jj], recv_sem=recv_sems.at[(1 + ns) * jj], **to)

                        @pl.when((peer == s) & (c == par))
                        def _():
                            both.wait_recv()

                        @pl.when((mine == s) & (c == par))
                        def _():
                            both.wait_send()

    return _split_wait("gather_wait_" + phase, await_, send_sems, recv_sems, bufs, after)


def _sibling_forward_parts(phase):
    def each(w_ref, send_sems, recv_sems, start):
        x, y, c, chips = _place()
        to = dict(device_id=(x, y, 1 - c), device_id_type=MESH)
        for jj, (px, py) in enumerate(chips):
            peer = 2 * px + py
            for s in range(4):
                for par in range(2):
                    mine_blocks = _phase_blocks(s, phase, True, par)
                    theirs = len(_phase_blocks(s, phase, True, 1 - par))
                    if not (mine_blocks or theirs):
                        continue

                    @pl.when((peer == s) & (c == par))
                    def _():
                        if start:
                            for _, blk in mine_blocks:
                                pltpu.make_async_remote_copy(
                                    src_ref=_blk(w_ref, blk), dst_ref=_blk(w_ref, blk),
                                    send_sem=send_sems.at[jj], recv_sem=recv_sems.at[jj], **to).start()
                            return
                        if theirs:
                            pltpu.make_async_remote_copy(
                                src_ref=_cols(w_ref, theirs), dst_ref=_cols(w_ref, theirs),
                                send_sem=send_sems.at[jj], recv_sem=recv_sems.at[jj], **to).wait_recv()
                        if mine_blocks:
                            pltpu.make_async_remote_copy(
                                src_ref=_cols(w_ref, len(mine_blocks)), dst_ref=_cols(w_ref, len(mine_blocks)),
                                send_sem=send_sems.at[jj], recv_sem=recv_sems.at[jj], **to).wait_send()

    issue = lambda refs, send_sems, recv_sems: each(refs[0], send_sems, recv_sems, True)
    await_ = lambda refs, send_sems, recv_sems: each(refs[0], send_sems, recv_sems, False)
    return issue, await_


def _sibling_forward(phase, w_grp):
    issue, await_ = _sibling_forward_parts(phase)

    def body(w_in_ref, w_ref, send_sems, recv_sems):
        del w_in_ref
        issue([w_ref], send_sems, recv_sems)
        await_([w_ref], send_sems, recv_sems)

    return pl.pallas_call(
        body, name="sibling_forward_" + phase, in_specs=[ANY], out_specs=ANY,
        out_shape=jax.ShapeDtypeStruct(w_grp.shape, w_grp.dtype), input_output_aliases={0: 0},
        scratch_shapes=[pltpu.SemaphoreType.DMA((3,)), pltpu.SemaphoreType.DMA((3,))],
    )(w_grp)


def _merge_edges(w, edge0, mixed, name):
    d = w.shape[1]

    def body(e_ref, o_ref):
        o_ref[...] = e_ref[0:DH, :] + e_ref[DH:2 * DH, :]

    def to_block(i):
        r = mixed[-1]
        for kk in range(len(mixed) - 2, -1, -1):
            r = jnp.where(i == kk, mixed[kk], r)
        return r

    return pl.pallas_call(
        body, name=name, grid=(len(mixed),),
        in_specs=[pl.BlockSpec((2 * DH, d), lambda i: (edge0 // 2 + i, 0))],
        out_specs=pl.BlockSpec((DH, d), lambda i: (to_block(i), 0)),
        out_shape=jax.ShapeDtypeStruct(w.shape, w.dtype),
        input_output_aliases={0: 0},
        compiler_params=_params(("arbitrary",)),
    )(w)


def _scatter_start(phase, g_grp, land, singles, halved=False):
    ns = len(singles)

    def issue(refs, send_sems, recv_sems):
        g_ref, land_ref = refs[0], refs[1]
        x, y, c, chips = _place()
        for jj, (px, py) in enumerate(chips):
            to = dict(device_id=(px, py, c), device_id_type=MESH)
            peer = 2 * px + py
            for a in range(ns):
                pltpu.make_async_remote_copy(
                    src_ref=refs[2 + 2 * a].at[peer], dst_ref=refs[3 + 2 * a].at[jj],
                    send_sem=send_sems.at[(1 + ns) * jj + 1 + a], recv_sem=recv_sems.at[(1 + ns) * jj + 1 + a],
                    **to).start()
            for s in range(4):
                for par in ((0, 1) if halved else (None,)):
                    blocks = _phase_blocks(s, phase, False, par)
                    if blocks:
                        @pl.when((peer == s) if par is None else ((peer == s) & (c == par)))
                        def _():
                            for b, blk in blocks:
                                pltpu.make_async_remote_copy(
                                    src_ref=_blk(g_ref, blk), dst_ref=_blk(land_ref.at[jj], b),
                                    send_sem=send_sems.at[(1 + ns) * jj], recv_sem=recv_sems.at[(1 + ns) * jj],
                                    **to).start()

    bufs = [g_grp, land] + [t for pair in singles for t in pair]
    return _split_start("scatter_start_" + phase, issue, bufs, 3 * (1 + ns))


def _scatter_wait(phase, send_sems, recv_sems, bufs, after, halved=False):
    ns = (len(bufs) - 2) // 2

    def await_(refs, send_sems, recv_sems):
        g_ref, land_ref = refs[0], refs[1]
        x, y, c, chips = _place()
        mine = 2 * x + y
        for jj, (px, py) in enumerate(chips):
            to = dict(device_id=(px, py, c), device_id_type=MESH)
            peer = 2 * px + py
            for a in range(ns):
                cp = pltpu.make_async_remote_copy(
                    src_ref=refs[2 + 2 * a].at[peer], dst_ref=refs[3 + 2 * a].at[jj],
                    send_sem=send_sems.at[(1 + ns) * jj + 1 + a], recv_sem=recv_sems.at[(1 + ns) * jj + 1 + a], **to)
                cp.wait_recv()
                cp.wait_send()
            for s in range(4):
                for par in ((0, 1) if halved else (None,)):
                    nblk = len(_phase_blocks(s, phase, False, par))
                    if nblk:
                        both = pltpu.make_async_remote_copy(
                            src_ref=_cols(g_ref, nblk), dst_ref=_cols(land_ref.at[jj], nblk),
                            send_sem=send_sems.at[(1 + ns) * jj], recv_sem=recv_sems.at[(1 + ns) * jj], **to)

                        @pl.when((mine == s) if par is None else ((mine == s) & (c == par)))
                        def _():
                            both.wait_recv()

                        @pl.when((peer == s) if par is None else ((peer == s) & (c == par)))
                        def _():
                            both.wait_send()

    return _split_wait("scatter_wait_" + phase, await_, send_sems, recv_sems, bufs, after)


def _needed_blocks(phase, parity):
    return sorted({blk for s in range(4) for _, blk in _phase_blocks(s, phase, False, parity)})


def _pair_reduce(phase, g_grp):
    n, d = g_grp.shape

    def swap(g_ref, sib_ref, send_sem, recv_sem):
        x, y, c, _ = _place()
        to = dict(device_id=(x, y, 1 - c), device_id_type=MESH)
        for par in range(2):
            give, get = _needed_blocks(phase, 1 - par), _needed_blocks(phase, par)

            @pl.when(c == par)
            def _():
                for blk in give:
                    pltpu.make_async_remote_copy(src_ref=_blk(g_ref, blk), dst_ref=_blk(sib_ref, blk),
                                                 send_sem=send_sem, recv_sem=recv_sem, **to).start()
                pltpu.make_async_remote_copy(src_ref=_cols(g_ref, len(get)), dst_ref=_cols(sib_ref, len(get)),
                                             send_sem=send_sem, recv_sem=recv_sem, **to).wait_recv()
                pltpu.make_async_remote_copy(src_ref=_cols(g_ref, len(give)), dst_ref=_cols(sib_ref, len(give)),
                                             send_sem=send_sem, recv_sem=recv_sem, **to).wait_send()

    sib = pl.pallas_call(
        swap, name="pair_swap_" + phase, in_specs=[ANY], out_specs=ANY,
        out_shape=jax.ShapeDtypeStruct((n, d), g_grp.dtype),
        scratch_shapes=[pltpu.SemaphoreType.DMA, pltpu.SemaphoreType.DMA],
    )(*_in_hbm(g_grp))

    lists = [_needed_blocks(phase, par) for par in range(2)]
    longest = max(len(t) for t in lists)
    table = jnp.asarray([t + [t[-1]] * (longest - len(t)) for t in lists], jnp.int32)[lax.axis_index("c")]

    def add(t_ref, a_ref, b_ref, o_ref):
        o_ref[...] = (a_ref[...].astype(F32) + b_ref[...].astype(F32)).astype(o_ref.dtype)

    blk = pl.BlockSpec((DH, d), lambda i, t: (t[i], 0))
    return pl.pallas_call(
        add, name="pair_add_" + phase,
        grid_spec=pltpu.PrefetchScalarGridSpec(num_scalar_prefetch=1, grid=(longest,),
                                               in_specs=[blk, blk], out_specs=blk),
        out_shape=jax.ShapeDtypeStruct((n, d), g_grp.dtype),
        compiler_params=_params(("arbitrary",)),
    )(table, g_grp, sib)


def _sum_shard(g_g, g_c, land):
    d = g_g.shape[1]
    chip = 2 * lax.axis_index("x") + lax.axis_index("y")

    def body(t_ref, gg_ref, gc_ref, land_ref, o_ref):
        b = pl.program_id(0)
        in_g = t_ref[2, b] == 1
        own = jnp.where(in_g, gg_ref[...].astype(F32), gc_ref[...].astype(F32))
        for jj in range(3):
            own = own + land_ref[jj].astype(F32)
        o_ref[...] = jnp.where(in_g & (b % 2 != lax.axis_index("c")), 0.0, own)

    return pl.pallas_call(
        body, name="sum_w_in",
        grid_spec=pltpu.PrefetchScalarGridSpec(
            num_scalar_prefetch=1, grid=(ALIGNED_BLOCKS,),
            in_specs=[pl.BlockSpec((DH, d), lambda b, t: (t[0, b], 0)), pl.BlockSpec((DH, d), lambda b, t: (t[1, b], 0)),
                      pl.BlockSpec((3, DH, d), lambda b, t: (0, b, 0))],
            out_specs=pl.BlockSpec((DH, d), lambda b, t: (b, 0))),
        out_shape=jax.ShapeDtypeStruct((ALIGNED_W, d), F32),
        compiler_params=_params(("arbitrary",)),
    )(_block_table(chip, False, 0, 0), g_g, g_c, land)


def _sum_rows(stack, land, rows):
    _, r, d = stack.shape
    rows = min(rows, r)
    chip = 2 * lax.axis_index("x") + lax.axis_index("y")

    def body(t_ref, own_ref, land_ref, o_ref):
        acc = own_ref[0].astype(F32)
        for jj in range(3):
            acc = acc + land_ref[jj].astype(F32)
        o_ref[...] = acc

    return pl.pallas_call(
        body, name="sum_w_out",
        grid_spec=pltpu.PrefetchScalarGridSpec(
            num_scalar_prefetch=1, grid=(r // rows,),
            in_specs=[pl.BlockSpec((1, rows, d), lambda i, t: (t[0], i, 0)),
                      pl.BlockSpec((3, rows, d), lambda i, t: (0, i, 0))],
            out_specs=pl.BlockSpec((rows, d), lambda i, t: (i, 0))),
        out_shape=jax.ShapeDtypeStruct((r, d), F32),
        compiler_params=_params(("arbitrary",)),
    )(jnp.reshape(chip, (1,)).astype(jnp.int32), stack, land)


def _exchange_parts(n_swap, with_pack):
    def copies(refs, send_sems, recv_sems):
        x, y, c, _ = _place()
        me = 4 * x + 2 * y + c
        cps = [pltpu.make_async_remote_copy(
            src_ref=refs[2 * a], dst_ref=refs[2 * a + 1], send_sem=send_sems.at[a], recv_sem=recv_sems.at[a],
            device_id=(x, y, 1 - c), device_id_type=MESH) for a in range(n_swap)]
        if with_pack:
            pack_ref, packs = refs[2 * n_swap], refs[2 * n_swap + 1]
            for r in range(1, 8):
                dx, dy, dc = (r >> 2) & 1, (r >> 1) & 1, r & 1
                peer = (x + dx - 2 * x * dx, y + dy - 2 * y * dy, c + dc - 2 * c * dc)
                cps.append(pltpu.make_async_remote_copy(
                    src_ref=pack_ref, dst_ref=packs.at[me], send_sem=send_sems.at[n_swap + r - 1],
                    recv_sem=recv_sems.at[n_swap + r - 1], device_id=peer, device_id_type=MESH))
        return cps

    def issue(refs, send_sems, recv_sems):
        for cp in copies(refs, send_sems, recv_sems):
            cp.start()

    def await_(refs, send_sems, recv_sems):
        cps = copies(refs, send_sems, recv_sems)
        for cp in cps:
            cp.wait_recv()
        for cp in cps:
            cp.wait_send()

    return issue, await_, n_swap + (7 if with_pack else 0)


def _sum_packs(pack, packs):
    x, y, c = lax.axis_index("x"), lax.axis_index("y"), lax.axis_index("c")
    me = jnp.reshape(4 * x + 2 * y + c, (1,)).astype(jnp.int32)

    def body(me_ref, own_ref, p_ref, o_ref):
        acc = jnp.where(me_ref[0] == 0, own_ref[...], p_ref[0])
        for d in range(1, 8):
            acc = acc + jnp.where(me_ref[0] == d, own_ref[...], p_ref[d])
        o_ref[...] = acc

    full = lambda s: pl.BlockSpec(s.shape, lambda i, t: (0,) * s.ndim)
    return pl.pallas_call(
        body, name="sum_packs",
        grid_spec=pltpu.PrefetchScalarGridSpec(num_scalar_prefetch=1, grid=(1,), in_specs=[full(pack), full(packs)],
                                               out_specs=full(pack)),
        out_shape=jax.ShapeDtypeStruct(pack.shape, F32),
    )(me, pack, packs)


def _adamw_update(g, w_ref, m_ref, v_ref, go, do, mo, vo):
    c1 = 1.0 / (1.0 - ADAM_B1 ** ADAM_STEP)
    c2 = 1.0 / (1.0 - ADAM_B2 ** ADAM_STEP)
    mn = ADAM_B1 * m_ref[...] + (1.0 - ADAM_B1) * g
    vn = ADAM_B2 * v_ref[...] + (1.0 - ADAM_B2) * (g * g)
    go[...] = g
    mo[...] = mn
    vo[...] = vn
    do[...] = -ADAM_LR * ((mn * c1) / (jnp.sqrt(vn * c2) + ADAM_EPS) + ADAM_WD * w_ref[...])


def _adamw(w, m, v, g1, g2, rows, name):
    r, cdim = w.shape
    rows = min(rows, r)

    def body(*refs):
        n_in = 4 if g2 is None else 5
        w_ref, m_ref, v_ref, g_ref = refs[:4]
        g = g_ref[...] if g2 is None else g_ref[...] + refs[4][...]
        _adamw_update(g, w_ref, m_ref, v_ref, *refs[n_in:n_in + 4])

    blk = pl.BlockSpec((rows, cdim), lambda i: (i, 0))
    args = [w, m, v, g1] + ([] if g2 is None else [g2])
    shp = jax.ShapeDtypeStruct((r, cdim), F32)
    return pl.pallas_call(
        body, name=name, grid=(r // rows,),
        in_specs=[blk] * len(args), out_specs=[blk] * 4, out_shape=[shp] * 4,
        compiler_params=_params(("parallel",), 20 * rows * cdim * 4 + 8 * 2**20),
    )(*_in_hbm(*args))


def _adamw_shard(wt, mt, vt, g1, g2):
    r, d = wt.shape
    cols = min(128, d)

    def body(w_ref, m_ref, v_ref, g_ref, g2_ref, go, do, mo, vo, pad_ref):
        chip = 2 * lax.axis_index("x") + lax.axis_index("y")
        back = [(ALIGNED_W - s) % ALIGNED_W for s in SHIFTS]
        pad_ref[...] = pltpu.roll(g_ref[...] + g2_ref[...], _by_chip(chip, back), 0)
        outs = [o.at[:, 0, :] for o in (go, do, mo, vo)]
        _adamw_update(pad_ref[0:r, :], w_ref, m_ref, v_ref, *outs)

    blk = pl.BlockSpec((r, cols), lambda i: (0, i))
    gblk = pl.BlockSpec((ALIGNED_W, cols), lambda i: (0, i))
    oblk = pl.BlockSpec((r, 1, cols), lambda i: (0, 0, i))
    shp = jax.ShapeDtypeStruct((r, 1, d), F32)
    return pl.pallas_call(
        body, name="adamw_w_in", grid=(d // cols,),
        in_specs=[blk] * 3 + [gblk] * 2, out_specs=[oblk] * 4, out_shape=[shp] * 4,
        scratch_shapes=[pltpu.VMEM((ALIGNED_W, cols), F32)],
        compiler_params=_params(("parallel",), 24 * ALIGNED_W * cols * 4 + 8 * 2**20),
    )(wt, mt, vt, g1, g2)


def _pad_lanes(a, width):
    return jnp.pad(a, ((0, 0), (0, width - a.shape[1])))


def _gathered_to_full(g):
    return jnp.transpose(g, (1, 0, 2)).reshape(g.shape[1], 4 * g.shape[2])


def _row(a):
    return _pad_lanes(a.reshape(1, -1), 1024)


def _small_pack(nin, cb, fn, al, dt, gn, cqw_shard, cw_shard):
    ad = jnp.concatenate([al.reshape(1, -1), dt.reshape(1, -1)], axis=1)
    rows = [_row(nin), _row(cb), _row(fn), _row(ad), _row(gn), cqw_shard.reshape(3, 1024), _row(cw_shard)]
    out = jnp.concatenate(rows, axis=0)
    return jnp.pad(out, ((0, 16 - out.shape[0]), (0, 0)))


def kernel(x, norm_in_w, w_in, conv_qkv_w, A_log, dt_bias, gdn_norm_w, conv_w, conv_b, w_out, final_norm_w, loss_target, m_norm_in_w, m_w_in, m_conv_qkv_w, m_A_log, m_dt_bias, m_gdn_norm_w, m_conv_w, m_conv_b, m_w_out, m_final_norm_w, v_norm_in_w, v_w_in, v_conv_qkv_w, v_A_log, v_dt_bias, v_gdn_norm_w, v_conv_w, v_conv_b, v_w_out, v_final_norm_w):
    chip = 2 * lax.axis_index("x") + lax.axis_index("y")
    a_shard = _align_shard(jnp.transpose(w_in, (2, 0, 1)))
    wo_b = _cast_bf16(w_out[0], 256, "cast_w_out")
    d_model = x.shape[-1]
    stack = lambda s: lax.empty((4,) + s.shape, s.dtype)
    wg0 = lax.empty((WG_BLOCKS * DH, d_model), BF16)
    wc0 = lax.empty((WC_BLOCKS * DH, d_model), BF16)
    ss_g, rs_g, bufs_g, tok_g = _gather_start("g", a_shard, wg0, [(conv_qkv_w[0], stack(conv_qkv_w[0]))])
    ss_c, rs_c, bufs_c, tok_c = _gather_start("c", bufs_g[0], wc0,
                                              [(conv_w[0], stack(conv_w[0])), (wo_b, stack(wo_b))])
    wg1, wc1, wog1, cqg1, cwg1 = _place_own(bufs_c[0], bufs_c[4], bufs_g[2], bufs_c[2],
                                            [bufs_g[1], bufs_c[1], bufs_c[5], bufs_g[3], bufs_c[3]])
    x0 = x[0]
    h = _rms_in(x0, _tie(_tie(norm_in_w, tok_g, "after_gather_start_g"), tok_c, "after_gather_start_c"))
    adam_in = [jnp.transpose(a[0]) for a in (w_in, m_w_in, v_w_in)]
    sp = lambda nin, cb, fn, al, dt, gn, cq, cwv: _small_pack(nin, cb, fn, al, dt, gn, cq[0], cwv[0])
    w_s = sp(norm_in_w, conv_b, final_norm_w, A_log, dt_bias, gdn_norm_w, conv_qkv_w, conv_w)
    m_s = sp(m_norm_in_w, m_conv_b, m_final_norm_w, m_A_log, m_dt_bias, m_gdn_norm_w, m_conv_qkv_w, m_conv_w)
    v_s = sp(v_norm_in_w, v_conv_b, v_final_norm_w, v_A_log, v_dt_bias, v_gdn_norm_w, v_conv_qkv_w, v_conv_w)
    a_thru, wg, _, cq_g = _gather_wait("g", ss_g, rs_g, [bufs_c[0], wg1, bufs_g[2], cqg1],
                                       [h, w_s, m_s, v_s] + adam_in[1:])
    w_g = _merge_edges(_sibling_forward("g", wg), G_EDGE, G_MIXED, "merge_edges_g")
    cqw = _gathered_to_full(cq_g)
    ad = jnp.pad(jnp.concatenate([A_log, dt_bias], axis=0), ((0, 0), (A_LANE, 0)))
    fwd_c = {}

    def on_q(q):
        _, wc, _, cw_g, _, wo_g = _gather_wait("c", ss_c, rs_c,
                                               [a_thru, wc1, bufs_c[2], cwg1, bufs_c[4], wog1], q)
        issue, _ = _sibling_forward_parts("c")
        ss, rs, (wc,), tok = _split_start("sibling_forward_start_c", issue, [wc], 3)
        fwd_c.update(ss=ss, rs=rs, wc=wc, cw_g=cw_g, wo_g=wo_g)
        return _tie(q, tok, "after_sibling_forward_start_c")

    def late(o):
        _, await_ = _sibling_forward_parts("c")
        (wc,) = _split_wait("sibling_forward_wait_c", await_, fwd_c["ss"], fwd_c["rs"], [fwd_c["wc"]], o)
        return (_merge_edges(wc, C_EDGE, C_MIXED, "merge_edges_c"), fwd_c["wo_g"].reshape(2 * GW, d_model),
                _gathered_to_full(fwd_c["cw_g"]))

    scat = {}

    def on_grad_c(g_c, g_wout, do):
        go4 = g_wout.reshape(4, GW // 2, d_model)
        land = lax.empty((3, ALIGNED_W, d_model), BF16)
        land_o = lax.empty((3, GW // 2, d_model), BF16)
        ss, rs, bufs, tok = _scatter_start("c", g_c, land, [(go4, land_o)])
        scat["c"] = (ss, rs, bufs)
        return _tie(do, tok, "after_scatter_start_c")

    def on_grad_g(g_g, dproj_g):
        ss, rs, bufs, tok = _scatter_start("g", _pair_reduce("g", g_g), scat["c"][2][1], [], halved=True)
        scat["g"] = (ss, rs, bufs)
        return _tie(dproj_g, tok, "after_scatter_start_g")

    gx, sm, _ = _local_step(x0, loss_target[0], h, w_g, cqw, late, norm_in_w, ad, gdn_norm_w, conv_b,
                            final_norm_w.reshape(1, -1), on_grad_c, on_grad_g, on_q)

    ss, rs, bufs = scat["c"]
    g_c, land, go4, land_o = _scatter_wait("c", ss, rs, [bufs[0], scat["g"][2][1], bufs[2], bufs[3]], gx)
    part_out = _sum_rows(go4, land_o, 128)
    ad_g = jnp.concatenate([sm["al"][:, A_LANE:], sm["dt"][:, A_LANE:]], axis=1)
    pack = jnp.concatenate([_row(sm["nin"]), _row(sm["cb"]), _row(sm["fn"]), _row(ad_g), _row(sm["gn"]),
                            jnp.concatenate(sm["cq"], axis=1).reshape(12, 1024), sm["cw"], _row(sm["loss"])], axis=0)
    pack = jnp.pad(pack, ((0, PACK_ROWS - pack.shape[0]), (0, 0)))
    issue, await_a, nsem = _exchange_parts(1, True)
    ss_a, rs_a, bufs_a, tok_a = _split_start(
        "exchange_start_small", issue,
        [part_out, lax.empty(part_out.shape, F32), pack, lax.empty((8,) + pack.shape, F32)], nsem)
    ss, rs, bufs = scat["g"]
    g_g, land = _scatter_wait("g", ss, rs, [bufs[0], land], [gx, tok_a], halved=True)
    part_in = _sum_shard(g_g, g_c, land)
    issue, await_b, nsem = _exchange_parts(1, False)
    ss_b, rs_b, bufs_b, tok_b = _split_start("exchange_start_w_in", issue,
                                             [part_in, lax.empty(part_in.shape, F32)], nsem)
    part_out, sib_out, pack, packs = _split_wait("exchange_wait_small", await_a, ss_a, rs_a, bufs_a, tok_b)
    tot = _sum_packs(pack, packs)
    g_wo, d_wo, m_wo, v_wo = _adamw(w_out[0], m_w_out[0], v_w_out[0], part_out, sib_out, 128, "adamw_w_out")
    g_cq_sh = lax.dynamic_slice_in_dim(tot[R_CQ:R_CQ + 12].reshape(4, 3 * GW), chip * 768, 768, axis=1)
    g_cw_sh = lax.dynamic_slice_in_dim(tot[R_CW:R_CW + 3], chip * 256, 256, axis=1)
    g_s = _small_pack(tot[R_NIN], tot[R_CB], tot[R_FN], tot[R_AD, :HEADS], tot[R_AD, HEADS:2 * HEADS],
                      tot[R_GN, :DH], g_cq_sh, g_cw_sh)
    small = _adamw(w_s, m_s, v_s, g_s, None, 16, "adamw_small")
    part_in, sib_in = _split_wait("exchange_wait_w_in", await_b, ss_b, rs_b, bufs_b, [small[0], d_wo])
    g_wi, d_wi, m_wi, v_wi = [jnp.transpose(a, (1, 2, 0))[0] for a in _adamw_shard(*adam_in, part_in, sib_in)]

    def unpack(a, big_in, big_out):
        return (a[0:1], big_in[None], a[5:8].reshape(1, 4, 768), a[3:4, :HEADS], a[3:4, HEADS:2 * HEADS],
                a[4:5, :DH], a[8, :768].reshape(1, 3, 256), a[1:2], big_out[None], a[2])

    loss = tot[R_LOSS, 0]
    return (loss, gx[None], *unpack(small[0], g_wi, g_wo), *unpack(small[1], d_wi, d_wo),
            *unpack(small[2], m_wi, m_wo), *unpack(small[3], v_wi, v_wo))
```

```python
import functools
import math

import jax
import jax.numpy as jnp
from jax import lax
from jax.experimental import pallas as pl
from jax.experimental.pallas import tpu as pltpu

F32 = jnp.float32
BF16 = jnp.bfloat16
MESH = pl.DeviceIdType.MESH
ANY = pl.BlockSpec(memory_space=pl.ANY)

HEADS = 8
DH = 128
CH = 64
GW = HEADS * DH
EPS = 1e-6
VMEM_V7X = 64 * 1024 * 1024

QB, KB, VB, ZB, BAB = 0, 8, 16, 24, 32
A_LANE = 120
NG, NC = 33, 32
GW_COLS, CW_COLS = NG * DH, NC * DH

SHARD_W = 2052
ALIGNED_BLOCKS = 17
ALIGNED_W = ALIGNED_BLOCKS * DH
SHIFTS = (0, 4, ALIGNED_W - 8, ALIGNED_W - 4)
G_EDGE, C_EDGE = 34, 32
G_SPARE, C_SPARE = 33, 34
WG_BLOCKS, WC_BLOCKS = 38, 36
G_MIXED, C_MIXED = (2, BAB), (4 * 7 + 1,)


def _shard_blocks(chip, edges):
    g, c = "g", "c"
    if chip == 0:
        out = [(g, 3 * b) for b in range(8)] + [(g, 3 * b + 1) for b in range(8)] + [(g, G_EDGE, G_MIXED[0])]
    elif chip == 1:
        out = [(g, G_EDGE + 1, G_MIXED[0])] + [(g, 3 * b + 2) for b in range(1, 8)]
        out += [(g, ZB + b) for b in range(8)] + [(g, G_EDGE + 2, G_MIXED[1])]
    elif chip == 2:
        out = [(c, 4 * b) for b in range(8)] + [(c, 4 * b + 1) for b in range(7)]
        out += [(c, C_EDGE, C_MIXED[0]), (g, G_EDGE + 3, G_MIXED[1])]
    else:
        out = [(c, 4 * b + 2) for b in range(8)] + [(c, 4 * b + 3) for b in range(8)] + [(c, C_EDGE + 1, C_MIXED[0])]
    return [(o[0], o[1] if (edges or len(o) == 2) else o[2]) for o in out]


def _by_chip(chip, vals):
    if all(v == vals[0] for v in vals):
        return vals[0]
    r = vals[3]
    for kk in (2, 1, 0):
        r = jnp.where(chip == kk, vals[kk], r)
    return r

ADAM_LR, ADAM_B1, ADAM_B2, ADAM_EPS, ADAM_WD, ADAM_STEP = 0.001, 0.9, 0.999, 1e-08, 0.01, 10

R_NIN, R_CB, R_FN, R_AD, R_GN, R_CQ, R_CW, R_LOSS, PACK_ROWS = 0, 1, 2, 3, 4, 5, 17, 20, 24

NN = ((1,), (0,))
NT = ((1,), (1,))
TN = ((0,), (0,))


def _dot(a, b, dims=NN, mode="lo"):
    dn = (dims, ((), ()))
    if mode == "hi":
        return lax.dot_general(a, b, dn, precision=lax.Precision.HIGHEST, preferred_element_type=F32)
    ah, bh = a.astype(BF16), b.astype(BF16)
    out = lax.dot_general(ah, bh, dn, preferred_element_type=F32)
    if mode == "x3":
        al = (a - ah.astype(F32)).astype(BF16)
        bl = (b - bh.astype(F32)).astype(BF16)
        out = out + lax.dot_general(ah, bl, dn, preferred_element_type=F32)
        out = out + lax.dot_general(al, bh, dn, preferred_element_type=F32)
    return out


P_GRAM, P_INV, P_SOL, P_SCAN, P_SCANB, P_BWD = "lo", "lo", "lo", "lo", "lo", "lo"
P_CUM = "x3"


def _params(sem=None, vmem=None):
    kw = {}
    if sem is not None:
        kw["dimension_semantics"] = sem
    if vmem is not None:
        kw["vmem_limit_bytes"] = int(min(max(vmem, 32 * 2**20), VMEM_V7X - 8 * 2**20))
    return pltpu.CompilerParams(**kw)


def _in_hbm(*arrays):
    return [pltpu.with_memory_space_constraint(a, pltpu.HBM) for a in arrays]


def _sigmoid(x):
    return 1.0 / (1.0 + jnp.exp(-x))


def _dsilu(x, s):
    return s * (1.0 + x * (1.0 - s))


def _rows(shape):
    return lax.broadcasted_iota(jnp.int32, shape, 0)


def _shift_down(x, s):
    if s == 0:
        return x
    return jnp.where(_rows(x.shape) >= s, pltpu.roll(x, s, 0), 0.0)


def _shift_up(x, s):
    if s == 0:
        return x
    n = x.shape[0]
    return jnp.where(_rows(x.shape) < n - s, pltpu.roll(x, n - s, 0), 0.0)


def _matmul(a, b, dims, out_dtype, tm, tn, tk, name, add=None, n=None, b_outer=False):
    if dims == NN:
        (m, k), n = a.shape, b.shape[1]
    elif dims == NT:
        (m, k), n = a.shape, (n or b.shape[0])
    else:
        (k, m), n = a.shape, b.shape[1]
    tm, tn, tk = min(tm, m), min(tn, n), min(tk, k)
    assert m % tm == 0 and n % tn == 0 and k % tk == 0, (name, m, n, k, tm, tn, tk)
    nk = k // tk

    def body(*refs):
        if add is None:
            a_ref, b_ref, o_ref = refs[:3]
            add_ref = None
        else:
            a_ref, b_ref, add_ref, o_ref = refs[:4]
        part = _dot(a_ref[...], b_ref[...], dims)
        if nk == 1:
            if add_ref is not None:
                part = part + add_ref[...]
            o_ref[...] = part.astype(out_dtype)
            return
        acc = refs[-1]
        kk = pl.program_id(2)

        @pl.when(kk == 0)
        def _():
            acc[...] = part

        @pl.when(kk > 0)
        def _():
            acc[...] += part

        @pl.when(kk == nk - 1)
        def _():
            r = acc[...]
            if add_ref is not None:
                r = r + add_ref[...]
            o_ref[...] = r.astype(out_dtype)

    ij = (lambda g0, g1: (g1, g0)) if b_outer else (lambda g0, g1: (g0, g1))

    def spec(shape, pick):
        return pl.BlockSpec(shape, lambda g0, g1, kk: pick(*ij(g0, g1), kk))

    a_spec = spec((tk, tm), lambda i, j, kk: (kk, i)) if dims == TN else spec((tm, tk), lambda i, j, kk: (i, kk))
    b_spec = spec((tn, tk), lambda i, j, kk: (j, kk)) if dims == NT else spec((tk, tn), lambda i, j, kk: (kk, j))
    o_spec = spec((tm, tn), lambda i, j, kk: (i, j))
    in_specs = [a_spec, b_spec]
    args = [a, b]
    if add is not None:
        in_specs.append(o_spec)
        args.append(add)
    osz = jnp.dtype(out_dtype).itemsize
    est = 2 * (tm * tk * a.dtype.itemsize + tk * tn * b.dtype.itemsize + tm * tn * osz)
    est += 3 * tm * tn * 4 + (2 * tm * tn * 4 if add is not None else 0)
    return pl.pallas_call(
        body, name=name, grid=(n // tn, m // tm, nk) if b_outer else (m // tm, n // tn, nk),
        in_specs=in_specs, out_specs=o_spec,
        out_shape=jax.ShapeDtypeStruct((m, n), out_dtype),
        scratch_shapes=[pltpu.VMEM((tm, tn), F32)] if nk > 1 else [],
        compiler_params=_params(("parallel", "parallel", "arbitrary"), est + 8 * 2**20),
    )(*args)


def _cast_bf16(a, rows, name):
    r, c = a.shape
    rows = min(rows, r)

    def body(a_ref, o_ref):
        o_ref[...] = a_ref[...].astype(BF16)

    return pl.pallas_call(
        body, name=name, grid=(r // rows,),
        in_specs=[pl.BlockSpec((rows, c), lambda i: (i, 0))],
        out_specs=pl.BlockSpec((rows, c), lambda i: (i, 0)),
        out_shape=jax.ShapeDtypeStruct((r, c), BF16),
        compiler_params=_params(("parallel",)),
    )(a)


def _align_shard(wt):
    r, _, d = wt.shape
    cols = min(256, d)

    def body(w_ref, o_ref, pad_ref):
        chip = 2 * lax.axis_index("x") + lax.axis_index("y")
        pad_ref[...] = jnp.zeros_like(pad_ref)
        pad_ref[0:r, :] = w_ref[:, 0, :]
        o_ref[...] = pltpu.roll(pad_ref[...], _by_chip(chip, SHIFTS), 0).astype(BF16)

    return pl.pallas_call(
        body, name="align_shard", grid=(d // cols,),
        in_specs=[pl.BlockSpec((r, 1, cols), lambda i: (0, 0, i))],
        out_specs=pl.BlockSpec((ALIGNED_W, cols), lambda i: (0, i)),
        out_shape=jax.ShapeDtypeStruct((ALIGNED_W, d), BF16),
        scratch_shapes=[pltpu.VMEM((ALIGNED_W, cols), F32)],
        compiler_params=_params(("parallel",)),
    )(wt)


def _rms_in(x, w):
    n, d = x.shape
    tr = min(256, n)

    def body(x_ref, w_ref, h_ref):
        xv = x_ref[...]
        r = lax.rsqrt(jnp.mean(xv * xv, axis=-1, keepdims=True) + EPS)
        h_ref[...] = (xv * r * w_ref[...]).astype(BF16)

    return pl.pallas_call(
        body, name="rms_in", grid=(n // tr,),
        in_specs=[pl.BlockSpec((tr, d), lambda i: (i, 0)), pl.BlockSpec((1, d), lambda i: (0, 0))],
        out_specs=pl.BlockSpec((tr, d), lambda i: (i, 0)),
        out_shape=jax.ShapeDtypeStruct((n, d), BF16),
        compiler_params=_params(("parallel",)),
    )(x, w)


def _conv_silu(p, w_ref, taps):
    c = None
    for j in range(taps):
        t = _shift_down(p, taps - 1 - j) * w_ref[j:j + 1, :]
        c = t if c is None else c + t
    return c


def _prep_qkv(proj, cw):
    n = proj.shape[0]

    def body(p3, wq, wk, wv, q_ref, k_ref, v_ref):
        for kind, (w_ref, o_ref) in enumerate(((wq, q_ref), (wk, k_ref), (wv, v_ref))):
            c = _conv_silu(p3[:, kind * DH:(kind + 1) * DH], w_ref, 4)
            a = c * _sigmoid(c)
            if kind < 2:
                r = lax.rsqrt(jnp.sum(a * a, axis=-1, keepdims=True) + EPS)
                a = a * (r * (DH ** -0.5 if kind == 0 else 1.0))
            o_ref[...] = a

    col = pl.BlockSpec((n, DH), lambda h: (0, h))
    wcol = lambda base: pl.BlockSpec((4, DH), lambda h: (0, base + h))
    out = jax.ShapeDtypeStruct((n, GW), F32)
    return pl.pallas_call(
        body, name="prep_qkv", grid=(HEADS,),
        in_specs=[pl.BlockSpec((n, 3 * DH), lambda h: (0, h)), wcol(QB), wcol(KB), wcol(VB)],
        out_specs=[col] * 3, out_shape=[out] * 3,
        compiler_params=_params(("parallel",), 40 * 2**20),
    )(proj, cw, cw, cw)


def _prep_qkv_bwd(proj, cw, dq, dk, dv, dproj):
    n = proj.shape[0]

    def body(p3, wq, wk, wv, dq_ref, dk_ref, dv_ref, _, o3, gq, gk, gv):
        for kind, (w_ref, d_ref, g_ref) in enumerate(((wq, dq_ref, gq), (wk, dk_ref, gk), (wv, dv_ref, gv))):
            p = p3[:, kind * DH:(kind + 1) * DH]
            shifted = [_shift_down(p, 3 - j) for j in range(4)]
            c = shifted[0] * w_ref[0:1, :]
            for j in range(1, 4):
                c = c + shifted[j] * w_ref[j:j + 1, :]
            s = _sigmoid(c)
            a = c * s
            d = d_ref[...]
            if kind < 2:
                r = lax.rsqrt(jnp.sum(a * a, axis=-1, keepdims=True) + EPS)
                sc = DH ** -0.5 if kind == 0 else 1.0
                d = (sc * r) * (d - a * ((r * r) * jnp.sum(d * a, axis=-1, keepdims=True)))
            dc = d * _dsilu(c, s)
            dp = None
            for j in range(4):
                g_ref[j:j + 1, :] = jnp.sum(dc * shifted[j], axis=0, keepdims=True)
                t = _shift_up(dc, 3 - j) * w_ref[j:j + 1, :]
                dp = t if dp is None else dp + t
            o3[:, kind * DH:(kind + 1) * DH] = dp.astype(BF16)

    col = pl.BlockSpec((n, DH), lambda h: (0, h))
    wcol = lambda base: pl.BlockSpec((4, DH), lambda h: (0, base + h))
    p3spec = pl.BlockSpec((n, 3 * DH), lambda h: (0, h))
    return pl.pallas_call(
        body, name="prep_qkv_bwd", grid=(HEADS,),
        in_specs=[p3spec, wcol(QB), wcol(KB), wcol(VB), col, col, col, ANY],
        out_specs=[p3spec] + [wcol(0)] * 3,
        out_shape=[jax.ShapeDtypeStruct(dproj.shape, BF16)] + [jax.ShapeDtypeStruct((4, GW), F32)] * 3,
        input_output_aliases={7: 0},
        compiler_params=_params(("parallel",), 48 * 2**20),
    )(proj, cw, cw, cw, dq, dk, dv, dproj)


CPB = 8
SCAN_CPS = 4


def _tri(lower, rows):
    i = lax.broadcasted_iota(jnp.int32, (rows, rows), 0)
    j = lax.broadcasted_iota(jnp.int32, (rows, rows), 1)
    return jnp.where((i // CH == j // CH) & ((i >= j) if lower else (j >= i)), 1.0, 0.0)


def _lane(shape):
    return lax.broadcasted_iota(jnp.int32, shape, 1)


def _prep_bg(proj, ad):
    n = proj.shape[0]
    nch = n // CH
    cpb = CPB if nch % CPB == 0 else 1
    rows = cpb * CH

    def body(p_ref, ad_ref, bg_ref, bgt_ref):
        p = p_ref[...]
        lane = _lane(p.shape)
        beta = _sigmoid(p)
        xa = p + ad_ref[1:2, :]
        sp = jnp.maximum(xa, 0.0) + jnp.log(1.0 + jnp.exp(-jnp.abs(xa)))
        g = pltpu.roll(-jnp.exp(ad_ref[0:1, :]) * sp, DH - A_LANE + HEADS, 1)
        gc = _dot(_tri(True, rows), g, NN, P_CUM)
        bg = jnp.where(lane < HEADS, beta, jnp.where(lane < 2 * HEADS, gc, 0.0))
        bg_ref[...] = bg
        for ci in range(cpb):
            bgt_ref[ci] = bg[ci * CH:(ci + 1) * CH, :].T

    return pl.pallas_call(
        body, name="prep_bg", grid=(nch // cpb,),
        in_specs=[pl.BlockSpec((rows, DH), lambda i: (i, BAB)), pl.BlockSpec((2, DH), lambda i: (0, 0))],
        out_specs=[pl.BlockSpec((rows, DH), lambda i: (i, 0)), pl.BlockSpec((cpb, DH, CH), lambda i: (i, 0, 0))],
        out_shape=[jax.ShapeDtypeStruct((n, DH), F32), jax.ShapeDtypeStruct((nch, DH, CH), F32)],
        compiler_params=_params(("parallel",)),
    )(*_in_hbm(proj, ad))


def _prep_bg_bwd(proj, ad, dbg, dproj):
    n = proj.shape[0]
    nch = n // CH
    cpb = CPB if nch % CPB == 0 else 1
    rows = cpb * CH

    def body(p_ref, ad_ref, d_ref, _, o_ref, ga_ref, gd_ref):
        p = p_ref[...]
        d = d_ref[...]
        lane = _lane(p.shape)
        beta = _sigmoid(p)
        xa = p + ad_ref[1:2, :]
        sp = jnp.maximum(xa, 0.0) + jnp.log(1.0 + jnp.exp(-jnp.abs(xa)))
        na = -jnp.exp(ad_ref[0:1, :])
        dg = pltpu.roll(_dot(_tri(False, rows), d, NN, P_CUM), A_LANE - HEADS, 1)
        da = dg * na * _sigmoid(xa)
        is_g = lane >= A_LANE
        o_ref[...] = jnp.where(lane < HEADS, d * beta * (1.0 - beta), jnp.where(is_g, da, 0.0)).astype(BF16)
        ga = jnp.sum(jnp.where(is_g, dg * na * sp, 0.0), axis=0, keepdims=True)
        gd = jnp.sum(jnp.where(is_g, da, 0.0), axis=0, keepdims=True)

        @pl.when(pl.program_id(0) == 0)
        def _():
            ga_ref[...] = jnp.zeros_like(ga_ref)
            gd_ref[...] = jnp.zeros_like(gd_ref)

        ga_ref[...] += ga
        gd_ref[...] += gd

    one = pl.BlockSpec((1, DH), lambda i: (0, 0))
    return pl.pallas_call(
        body, name="prep_bg_bwd", grid=(nch // cpb,),
        in_specs=[pl.BlockSpec((rows, DH), lambda i: (i, BAB)), pl.BlockSpec((2, DH), lambda i: (0, 0)),
                  pl.BlockSpec((rows, DH), lambda i: (i, 0)), ANY],
        out_specs=[pl.BlockSpec((rows, DH), lambda i: (i, BAB)), one, one],
        out_shape=[jax.ShapeDtypeStruct(dproj.shape, BF16), jax.ShapeDtypeStruct((1, DH), F32),
                   jax.ShapeDtypeStruct((1, DH), F32)],
        input_output_aliases={3: 0},
        compiler_params=_params(("arbitrary",)),
    )(proj, ad, dbg, dproj)


def _gdn_out(o, proj, wg):
    n = o.shape[0]

    def body(o_ref, z_ref, w_ref, y_ref):
        ov, z = o_ref[...], z_ref[...]
        r = lax.rsqrt(jnp.mean(ov * ov, axis=-1, keepdims=True) + EPS)
        y_ref[...] = (ov * r * w_ref[...] * (z * _sigmoid(z))).astype(BF16)

    return pl.pallas_call(
        body, name="gdn_out", grid=(HEADS,),
        in_specs=[pl.BlockSpec((n, DH), lambda h: (0, h)), pl.BlockSpec((n, DH), lambda h: (0, ZB + h)),
                  pl.BlockSpec((1, DH), lambda h: (0, 0))],
        out_specs=pl.BlockSpec((n, DH), lambda h: (0, h)),
        out_shape=jax.ShapeDtypeStruct((n, 2 * GW), BF16),
        compiler_params=_params(("parallel",)),
    )(o, proj, wg)


def _gdn_out_bwd(o, proj, wg, dout_b, w_out):
    n = o.shape[0]
    d_model = dout_b.shape[1]

    def body(o_ref, z_ref, w_ref, g_ref, wo_ref, do_ref, dz_ref, gw_ref):
        ov, z, w = o_ref[...], z_ref[...], w_ref[...]
        d = _dot(g_ref[...], wo_ref[...], NT)
        r = lax.rsqrt(jnp.mean(ov * ov, axis=-1, keepdims=True) + EPS)
        nrm = ov * r
        s = _sigmoid(z)
        dz_ref[...] = (d * (nrm * w) * _dsilu(z, s)).astype(BF16)
        dn_w = d * (z * s)
        gw = jnp.sum(dn_w * nrm, axis=0, keepdims=True)
        dn = dn_w * w
        do_ref[...] = r * (dn - nrm * jnp.mean(dn * nrm, axis=-1, keepdims=True))

        @pl.when(pl.program_id(0) == 0)
        def _():
            gw_ref[...] = jnp.zeros_like(gw_ref)

        gw_ref[...] += gw

    return pl.pallas_call(
        body, name="gdn_out_bwd", grid=(HEADS,),
        in_specs=[pl.BlockSpec((n, DH), lambda h: (0, h)), pl.BlockSpec((n, DH), lambda h: (0, ZB + h)),
                  pl.BlockSpec((1, DH), lambda h: (0, 0)), pl.BlockSpec((n, d_model), lambda h: (0, 0)),
                  pl.BlockSpec((DH, d_model), lambda h: (h, 0))],
        out_specs=[pl.BlockSpec((n, DH), lambda h: (0, h)), pl.BlockSpec((n, DH), lambda h: (0, ZB + h)),
                   pl.BlockSpec((1, DH), lambda h: (0, 0))],
        out_shape=[jax.ShapeDtypeStruct((n, GW), F32), jax.ShapeDtypeStruct((n, GW_COLS), BF16),
                   jax.ShapeDtypeStruct((1, DH), F32)],
        compiler_params=_params(("arbitrary",), 40 * 2**20),
    )(o, proj, wg, dout_b, w_out)


def _conv_branch(proj, w3, b, mix):
    n = proj.shape[0]

    def body(p4, w_ref, b_ref, _, y_ref):
        u = p4[:, DH:2 * DH] * p4[:, 2 * DH:3 * DH]
        cc = _conv_silu(u, w_ref, 3) + b_ref[...]
        z = p4[:, 3 * DH:4 * DH]
        y_ref[...] = (p4[:, 0:DH] * cc * (z * _sigmoid(z))).astype(BF16)

    return pl.pallas_call(
        body, name="conv_branch", grid=(HEADS,),
        in_specs=[pl.BlockSpec((n, 4 * DH), lambda h: (0, h)), pl.BlockSpec((3, DH), lambda h: (0, h)),
                  pl.BlockSpec((1, DH), lambda h: (0, h)), ANY],
        out_specs=pl.BlockSpec((n, DH), lambda h: (0, HEADS + h)),
        out_shape=jax.ShapeDtypeStruct(mix.shape, BF16),
        input_output_aliases={3: 0},
        compiler_params=_params(("parallel",), 40 * 2**20),
    )(*_in_hbm(proj, w3, b, mix))


def _conv_branch_bwd(proj, w3, b, dout_b, w_out):
    n = proj.shape[0]
    d_model = dout_b.shape[1]

    def body(p4, w_ref, b_ref, g_ref, wo_ref, o4, gw_ref, gbias_ref):
        gb, gcv, hc, z = p4[:, 0:DH], p4[:, DH:2 * DH], p4[:, 2 * DH:3 * DH], p4[:, 3 * DH:4 * DH]
        d = _dot(g_ref[...], wo_ref[...], NT)
        dgb, dgc, dhc, dzc = (o4.at[:, kk * DH:(kk + 1) * DH] for kk in range(4))
        u = gcv * hc
        cc = _conv_silu(u, w_ref, 3) + b_ref[...]
        s = _sigmoid(z)
        dzc[...] = (d * (gb * cc) * _dsilu(z, s)).astype(BF16)
        dp = d * (z * s)
        dgb[...] = (dp * cc).astype(BF16)
        dcc = dp * gb
        gbias_ref[...] = jnp.sum(dcc, axis=0, keepdims=True)
        du = None
        for j in range(3):
            gw_ref[j:j + 1, :] = jnp.sum(dcc * _shift_down(u, 2 - j), axis=0, keepdims=True)
            t = _shift_up(dcc, 2 - j) * w_ref[j:j + 1, :]
            du = t if du is None else du + t
        dgc[...] = (du * hc).astype(BF16)
        dhc[...] = (du * gcv).astype(BF16)

    p4spec = pl.BlockSpec((n, 4 * DH), lambda h: (0, h))
    return pl.pallas_call(
        body, name="conv_branch_bwd", grid=(HEADS,),
        in_specs=[p4spec, pl.BlockSpec((3, DH), lambda h: (0, h)), pl.BlockSpec((1, DH), lambda h: (0, h)),
                  pl.BlockSpec((n, d_model), lambda h: (0, 0)), pl.BlockSpec((DH, d_model), lambda h: (HEADS + h, 0))],
        out_specs=[p4spec, pl.BlockSpec((3, DH), lambda h: (0, h)), pl.BlockSpec((1, DH), lambda h: (0, h))],
        out_shape=[jax.ShapeDtypeStruct((n, CW_COLS), BF16), jax.ShapeDtypeStruct((3, GW), F32),
                   jax.ShapeDtypeStruct((1, GW), F32)],
        compiler_params=_params(("parallel",), 52 * 2**20),
    )(proj, w3, b, dout_b, w_out)


def _out_loss(mix, w_out, x, tgt, wf):
    n, d = x.shape
    kdim = mix.shape[1]
    tr = min(256, n)

    def body(m_ref, wo_ref, x_ref, t_ref, w_ref, do_ref, dob_ref, gw_ref, loss_ref):
        ov = _dot(m_ref[...], wo_ref[...], NN) + x_ref[...]
        w = w_ref[...]
        r = lax.rsqrt(jnp.mean(ov * ov, axis=-1, keepdims=True) + EPS)
        nrm = ov * r
        e = nrm * w - t_ref[...]
        dy = e * (1.0 / d)
        dn = dy * w
        dout = r * (dn - nrm * jnp.mean(dn * nrm, axis=-1, keepdims=True))
        do_ref[...] = dout
        dob_ref[...] = dout.astype(BF16)

        @pl.when(pl.program_id(0) == 0)
        def _():
            gw_ref[...] = jnp.zeros_like(gw_ref)
            loss_ref[...] = jnp.zeros_like(loss_ref)

        gw_ref[...] += jnp.sum(dy * nrm, axis=0, keepdims=True)
        loss_ref[...] += (0.5 / d) * jnp.sum(jnp.sum(e * e, axis=-1, keepdims=True), axis=0, keepdims=True)

    row = pl.BlockSpec((tr, d), lambda i: (i, 0))
    return pl.pallas_call(
        body, name="out_loss", grid=(n // tr,),
        in_specs=[pl.BlockSpec((tr, kdim), lambda i: (i, 0)), pl.BlockSpec((kdim, d), lambda i: (0, 0)), row, row,
                  pl.BlockSpec((1, d), lambda i: (0, 0))],
        out_specs=[row, row, pl.BlockSpec((1, d), lambda i: (0, 0)), pl.BlockSpec((1, 1), lambda i: (0, 0))],
        out_shape=[jax.ShapeDtypeStruct((n, d), F32), jax.ShapeDtypeStruct((n, d), BF16),
                   jax.ShapeDtypeStruct((1, d), F32), jax.ShapeDtypeStruct((1, 1), F32)],
        compiler_params=_params(("arbitrary",), 40 * 2**20),
    )(mix, w_out, x, tgt, wf)


def _dh_rms_bwd(dproj, w_t, dh0, x, w, dout, tk):
    n, d = x.shape
    kdim = dproj.shape[1]
    tm = min(512, n)
    tk = min(tk, kdim)
    nk = kdim // tk

    def body(a_ref, b_ref, dh0_ref, x_ref, w_ref, do_ref, dx_ref, gw_ref, acc):
        i, kk = pl.program_id(0), pl.program_id(1)
        part = _dot(a_ref[...], b_ref[...], NN)

        @pl.when(kk == 0)
        def _():
            acc[...] = part + dh0_ref[...]

        @pl.when(kk > 0)
        def _():
            acc[...] += part

        @pl.when((i == 0) & (kk == 0))
        def _():
            gw_ref[...] = jnp.zeros_like(gw_ref)

        @pl.when(kk == nk - 1)
        def _():
            xv, dhv = x_ref[...], acc[...]
            r = lax.rsqrt(jnp.mean(xv * xv, axis=-1, keepdims=True) + EPS)
            xn = xv * r
            dxn = dhv * w_ref[...]
            dx_ref[...] = r * (dxn - xn * jnp.mean(dxn * xn, axis=-1, keepdims=True)) + do_ref[...]
            gw_ref[...] += jnp.sum(dhv * xn, axis=0, keepdims=True)

    row = pl.BlockSpec((tm, d), lambda i, kk: (i, 0))
    one = pl.BlockSpec((1, d), lambda i, kk: (0, 0))
    return pl.pallas_call(
        body, name="dh_rms_bwd", grid=(n // tm, nk),
        in_specs=[pl.BlockSpec((tm, tk), lambda i, kk: (i, kk)), pl.BlockSpec((tk, d), lambda i, kk: (kk, 0)),
                  row, row, one, row],
        out_specs=[row, one],
        out_shape=[jax.ShapeDtypeStruct((n, d), F32), jax.ShapeDtypeStruct((1, d), F32)],
        scratch_shapes=[pltpu.VMEM((tm, d), F32)],
        compiler_params=_params(("arbitrary", "arbitrary"), 48 * 2**20),
    )(dproj, w_t, dh0, x, w, dout)


def _ij():
    i = lax.broadcasted_iota(jnp.int32, (CH, CH), 0)
    j = lax.broadcasted_iota(jnp.int32, (CH, CH), 1)
    return i, j


def _unit_lower_inverse(mats):
    i, j = _ij()
    eye = jnp.where(i == j, 1.0, 0.0)
    same16 = (i // 16) == (j // 16)
    same32 = (i // 32) == (j // 32)
    mm = lambda xs, ys: [_dot(x, y, NN, P_INV) for x, y in zip(xs, ys)]
    n1 = [jnp.where(same16, -a, 0.0) for a in mats]
    n2 = mm(n1, n1)
    n4 = mm(n2, n2)
    n8 = mm(n4, n4)
    t = [eye + x1 + x2 + x3 for x1, x2, x3 in zip(n1, n2, mm(n1, n2))]
    t = [x + y for x, y in zip(t, mm(t, n4))]
    t = [x + y for x, y in zip(t, mm(t, n8))]
    a1 = [jnp.where(same32 & jnp.logical_not(same16), a, 0.0) for a in mats]
    t = [x - y for x, y in zip(t, mm(t, mm(a1, t)))]
    a2 = [jnp.where(same32, 0.0, a) for a in mats]
    t = [x - y for x, y in zip(t, mm(t, mm(a2, t)))]
    return t


def _head_vectors(bg, bgt, h):
    bcol = bg[:, h:h + 1]
    gcol = bg[:, HEADS + h:HEADS + h + 1]
    grow = bgt[HEADS + h:HEADS + h + 1, :]
    return bcol, gcol, grow


def _decay(gcol, grow):
    i, j = _ij()
    return jnp.where(i >= j, jnp.exp(jnp.where(i >= j, gcol - grow, 0.0)), 0.0)


def _gdn_intra(q, k, v, bg, bgt):
    n = q.shape[0]
    nch = n // CH
    cps = 4 if nch % 4 == 0 else 1

    def body(q_ref, k_ref, v_ref, bg_ref, bgt_ref, u_ref, w_ref, p_ref, t_ref):
        i, j = _ij()
        items = [(ci, h) for ci in range(cps) for h in range(HEADS)]
        at = lambda ref, ci, h: ref.at[ci * CH:(ci + 1) * CH, h * DH:(h + 1) * DH]
        bgs = [bg_ref[ci * CH:(ci + 1) * CH, :] for ci in range(cps)]
        ks = [at(k_ref, ci, h)[...] for ci, h in items]
        vecs = [_head_vectors(bgs[ci], bgt_ref[ci], h) for ci, h in items]
        decs = [_decay(gcol, grow) for _, gcol, grow in vecs]
        kks = [_dot(kh, kh, NT, P_GRAM) for kh in ks]
        qks = [_dot(at(q_ref, ci, h)[...], kh, NT, P_GRAM) for (ci, h), kh in zip(items, ks)]
        ts = _unit_lower_inverse([jnp.where(i > j, bcol * kk * dec, 0.0)
                                  for (bcol, _, _), kk, dec in zip(vecs, kks, decs)])
        us = [_dot(t, at(v_ref, ci, h)[...] * bcol, NN, P_SOL) for t, (ci, h), (bcol, _, _) in zip(ts, items, vecs)]
        ws = [_dot(t, kh * (bcol * jnp.exp(gcol)), NN, P_SOL) for t, kh, (bcol, gcol, _) in zip(ts, ks, vecs)]
        for n_, (ci, h) in enumerate(items):
            p_ref[ci, h] = qks[n_] * decs[n_]
            t_ref[ci, h] = ts[n_]
            at(u_ref, ci, h)[...] = us[n_]
            at(w_ref, ci, h)[...] = ws[n_]

    row = pl.BlockSpec((cps * CH, GW), lambda c: (c, 0))
    sq = pl.BlockSpec((cps, HEADS, CH, CH), lambda c: (c, 0, 0, 0))
    big = jax.ShapeDtypeStruct((n, GW), F32)
    sqs = jax.ShapeDtypeStruct((nch, HEADS, CH, CH), F32)
    return pl.pallas_call(
        body, name="gdn_intra", grid=(nch // cps,),
        in_specs=[row, row, row, pl.BlockSpec((cps * CH, DH), lambda c: (c, 0)),
                  pl.BlockSpec((cps, DH, CH), lambda c: (c, 0, 0))],
        out_specs=[row, row, sq, sq], out_shape=[big, big, sqs, sqs],
        compiler_params=_params(("parallel",)),
    )(q, k, v, bg, bgt)


def _gdn_scan(q, k, bg, u, w, p):
    n = q.shape[0]
    nch = n // CH
    cps = SCAN_CPS if nch % SCAN_CPS == 0 else 1

    def body(q_ref, k_ref, bg_ref, u_ref, w_ref, p_ref, o_ref, vn_ref, s_out, s_scr):
        @pl.when(pl.program_id(0) == 0)
        def _():
            s_scr[...] = jnp.zeros_like(s_scr)

        hs = range(HEADS)
        sls = [slice(h * DH, (h + 1) * DH) for h in hs]
        ss = [s_scr[h] for h in hs]
        for ci in range(cps):
            rs = slice(ci * CH, (ci + 1) * CH)
            bg = bg_ref[rs, :]
            gcols = [bg[:, HEADS + h:HEADS + h + 1] for h in hs]
            glasts = [g[CH - 1:CH, :] for g in gcols]
            wss = [_dot(w_ref[rs, sl], s, NN, P_SCAN) for sl, s in zip(sls, ss)]
            oqs = [_dot(q_ref[rs, sl] * jnp.exp(g), s, NN, P_SCAN) for sl, s, g in zip(sls, ss, gcols)]
            vns = [u_ref[rs, sl] - x for sl, x in zip(sls, wss)]
            ops = [_dot(p_ref[ci, h], vn, NN, P_SCAN) for h, vn in zip(hs, vns)]
            sns = [_dot(k_ref[rs, sl] * jnp.exp(gl - g), vn, TN, P_SCAN)
                   for sl, gl, g, vn in zip(sls, glasts, gcols, vns)]
            for h, sl in enumerate(sls):
                s_out[ci, :, sl] = ss[h]
                vn_ref[rs, sl] = vns[h]
                o_ref[rs, sl] = oqs[h] + ops[h]
            ss = [s * jnp.exp(gl) + sn for s, gl, sn in zip(ss, glasts, sns)]
        for h in hs:
            s_scr[h] = ss[h]

    row = pl.BlockSpec((cps * CH, GW), lambda c: (c, 0))
    big = jax.ShapeDtypeStruct((n, GW), F32)
    return pl.pallas_call(
        body, name="gdn_scan", grid=(nch // cps,),
        in_specs=[row, row, pl.BlockSpec((cps * CH, DH), lambda c: (c, 0)), row, row,
                  pl.BlockSpec((cps, HEADS, CH, CH), lambda c: (c, 0, 0, 0))],
        out_specs=[row, row, pl.BlockSpec((cps, DH, GW), lambda c: (c, 0, 0))],
        out_shape=[big, big, jax.ShapeDtypeStruct((nch, DH, GW), F32)],
        scratch_shapes=[pltpu.VMEM((HEADS, DH, DH), F32)],
        compiler_params=_params(("arbitrary",)),
    )(q, k, bg, u, w, p)


def _gdn_scan_bwd(q, k, bg, w, p, vn, s_in, do):
    n = q.shape[0]
    nch = n // CH
    cps = SCAN_CPS if nch % SCAN_CPS == 0 else 1
    rev = lambda c: nch // cps - 1 - c

    def body(q_ref, k_ref, bg_ref, w_ref, p_ref, vn_ref, s_ref, do_ref,
             dqg_ref, dp_ref, du_ref, dw_ref, dks_ref, dgam_ref, ds_scr):
        @pl.when(pl.program_id(0) == 0)
        def _():
            ds_scr[...] = jnp.zeros_like(ds_scr)

        lane = _lane((1, DH))
        hs = range(HEADS)
        sls = [slice(h * DH, (h + 1) * DH) for h in hs]
        dss = [ds_scr[h] for h in hs]
        for ci in reversed(range(cps)):
            rs = slice(ci * CH, (ci + 1) * CH)
            bg = bg_ref[rs, :]
            gcols = [bg[:, HEADS + h:HEADS + h + 1] for h in hs]
            glasts = [g[CH - 1:CH, :] for g in gcols]
            ss = [s_ref[ci, :, sl] for sl in sls]
            dos = [do_ref[rs, sl] for sl in sls]
            vnl = [vn_ref[rs, sl] for sl in sls]
            dqgs = [_dot(d, s, NT, P_SCANB) for d, s in zip(dos, ss)]
            dps = [_dot(d, vn, NT, P_SCANB) for d, vn in zip(dos, vnl)]
            dvn1 = [_dot(p_ref[ci, h], d, TN, P_SCANB) for h, d in zip(hs, dos)]
            dvn2 = [_dot(k_ref[rs, sl] * jnp.exp(gl - g), ds, NN, P_SCANB)
                    for sl, gl, g, ds in zip(sls, glasts, gcols, dss)]
            dkss = [_dot(vn, ds, NT, P_SCANB) for vn, ds in zip(vnl, dss)]
            dsq = [_dot(q_ref[rs, sl] * jnp.exp(g), d, TN, P_SCANB) for sl, g, d in zip(sls, gcols, dos)]
            dvns = [a + b for a, b in zip(dvn1, dvn2)]
            dws = [_dot(dvn, s, NT, P_SCANB) for dvn, s in zip(dvns, ss)]
            dsw = [_dot(w_ref[rs, sl], dvn, TN, P_SCANB) for sl, dvn in zip(sls, dvns)]
            dgam = jnp.zeros((1, DH), F32)
            for h, sl in enumerate(sls):
                dqg_ref[rs, sl] = dqgs[h]
                dp_ref[ci, h] = dps[h]
                du_ref[rs, sl] = dvns[h]
                dw_ref[rs, sl] = -dws[h]
                dks_ref[rs, sl] = dkss[h]
                tot = jnp.sum(jnp.sum(dss[h] * ss[h], axis=-1, keepdims=True), axis=0, keepdims=True)
                dgam = dgam + jnp.where(lane == h, tot, 0.0)
            dgam_ref[ci] = jnp.broadcast_to(dgam, (8, DH))
            dss = [ds * jnp.exp(gl) + a - b for ds, gl, a, b in zip(dss, glasts, dsq, dsw)]
        for h in hs:
            ds_scr[h] = dss[h]

    row = pl.BlockSpec((cps * CH, GW), lambda c: (rev(c), 0))
    sq = pl.BlockSpec((cps, HEADS, CH, CH), lambda c: (rev(c), 0, 0, 0))
    big = jax.ShapeDtypeStruct((n, GW), F32)
    return pl.pallas_call(
        body, name="gdn_scan_bwd", grid=(nch // cps,),
        in_specs=[row, row, pl.BlockSpec((cps * CH, DH), lambda c: (rev(c), 0)), row, sq, row,
                  pl.BlockSpec((cps, DH, GW), lambda c: (rev(c), 0, 0)), row],
        out_specs=[row, sq, row, row, row, pl.BlockSpec((cps, 8, DH), lambda c: (rev(c), 0, 0))],
        out_shape=[big, jax.ShapeDtypeStruct((nch, HEADS, CH, CH), F32), big, big, big,
                   jax.ShapeDtypeStruct((nch, 8, DH), F32)],
        scratch_shapes=[pltpu.VMEM((HEADS, DH, DH), F32)],
        compiler_params=_params(("arbitrary",)),
    )(q, k, bg, w, p, vn, s_in, do)


def _gdn_intra_bwd(q, k, v, bg, bgt, t, u, w, p, dqg, dp, du, dw, dks, dgam):
    n = q.shape[0]
    nch = n // CH
    cps = 1

    def body(q_ref, k_ref, v_ref, bg_ref, bgt_ref, t_ref, u_ref, w_ref, p_ref,
             dqg_ref, dp_ref, du_ref, dw_ref, dks_ref, dgam_ref, dq_ref, dk_ref, dv_ref, dbg_ref):
        i, j = _ij()
        rows1 = lax.broadcasted_iota(jnp.int32, (CH, 1), 0)
        lane = _lane((CH, DH))
        rsum = lambda x: jnp.sum(x, axis=-1, keepdims=True)
        items = [(ci, h) for ci in range(cps) for h in range(HEADS)]
        at = lambda ref, it: ref.at[it[0] * CH:(it[0] + 1) * CH, it[1] * DH:(it[1] + 1) * DH]
        ld = lambda ref: [at(ref, it)[...] for it in items]
        bgs = [bg_ref[ci * CH:(ci + 1) * CH, :] for ci in range(cps)]
        qs, ks = ld(q_ref), ld(k_ref)
        vecs = [_head_vectors(bgs[ci], bgt_ref[ci], h) for ci, h in items]
        decs = [_decay(gcol, grow) for _, gcol, grow in vecs]
        ths = [t_ref[ci, h] for ci, h in items]
        drus = [_dot(th, x_, TN, P_BWD) for th, x_ in zip(ths, ld(du_ref))]
        drws = [_dot(th, x_, TN, P_BWD) for th, x_ in zip(ths, ld(dw_ref))]
        kks = [_dot(kh, kh, NT, P_GRAM) for kh in ks]
        da1 = [_dot(dru, x_, NT, P_BWD) for dru, x_ in zip(drus, ld(u_ref))]
        da2 = [_dot(drw, x_, NT, P_BWD) for drw, x_ in zip(drws, ld(w_ref))]
        das = [jnp.where(i > j, -(x_ + y_), 0.0) for x_, y_ in zip(da1, da2)]
        dkks = [da * bcol * dec for da, (bcol, _, _), dec in zip(das, vecs, decs)]
        dps = [dp_ref[ci, h] for ci, h in items]
        dqks = [dp_ * dec for dp_, dec in zip(dps, decs)]
        dq_ps = [_dot(dqk, kh, NN, P_BWD) for dqk, kh in zip(dqks, ks)]
        dk_ps = [_dot(dqk, qh, TN, P_BWD) for dqk, qh in zip(dqks, qs)]
        dk_as = [_dot(dkk, kh, NN, P_BWD) for dkk, kh in zip(dkks, ks)]
        dk_bs = [_dot(dkk, kh, TN, P_BWD) for dkk, kh in zip(dkks, ks)]
        bcols = [vc[0] for vc in vecs]
        gcols = [vc[1] for vc in vecs]
        gams = [jnp.exp(g) for g in gcols]
        glasts = [g[CH - 1:CH, :] for g in gcols]
        es = [jnp.exp(gl - g) for gl, g in zip(glasts, gcols)]
        kgs = [kh * gam for kh, gam in zip(ks, gams)]
        dqgs, dkss = ld(dqg_ref), ld(dks_ref)
        r_uv = [rsum(dru * x_) for dru, x_ in zip(drus, ld(v_ref))]
        r_wk = [rsum(drw * kg) for drw, kg in zip(drws, kgs)]
        r_ak = [rsum(da * kk * dec) for da, kk, dec in zip(das, kks, decs)]
        r_qq = [rsum(dqg * qh) for dqg, qh in zip(dqgs, qs)]
        tks = [rsum(dk_ * kh) * e for dk_, kh, e in zip(dkss, ks, es)]
        mdecs = [da * (bcol * kk * dec) + dp_ * p_ref[ci, h]
                 for (ci, h), da, bcol, kk, dec, dp_ in zip(items, das, bcols, kks, decs, dps)]
        r_md = [rsum(m) for m in mdecs]
        c_md = [rsum(jnp.where(i == j, jnp.sum(m, axis=0, keepdims=True), 0.0)) for m in mdecs]
        dbgs = [jnp.zeros((CH, DH), F32) for _ in range(cps)]
        for n_, (ci, h) in enumerate(items):
            at(dv_ref, (ci, h))[...] = bcols[n_] * drus[n_]
            at(dq_ref, (ci, h))[...] = gams[n_] * dqgs[n_] + dq_ps[n_]
            at(dk_ref, (ci, h))[...] = ((bcols[n_] * gams[n_]) * drws[n_] + dk_ps[n_] + dk_as[n_] + dk_bs[n_]
                                        + dkss[n_] * es[n_])
            dbeta = r_uv[n_] + r_wk[n_] + r_ak[n_]
            dglast = (jnp.sum(tks[n_], axis=0, keepdims=True)
                      + dgam_ref[ci, 0:1, h:h + 1] * jnp.exp(glasts[n_]))
            dgc = (r_wk[n_] * bcols[n_] + r_md[n_] - c_md[n_] + r_qq[n_] * gams[n_] - tks[n_]
                   + jnp.where(rows1 == CH - 1, dglast, 0.0))
            dbgs[ci] = dbgs[ci] + jnp.where(lane == h, dbeta, 0.0) + jnp.where(lane == HEADS + h, dgc, 0.0)
        for ci in range(cps):
            dbg_ref[ci * CH:(ci + 1) * CH, :] = dbgs[ci]

    row = pl.BlockSpec((cps * CH, GW), lambda c: (c, 0))
    sq = pl.BlockSpec((cps, HEADS, CH, CH), lambda c: (c, 0, 0, 0))
    small = pl.BlockSpec((cps * CH, DH), lambda c: (c, 0))
    big = jax.ShapeDtypeStruct((n, GW), F32)
    return pl.pallas_call(
        body, name="gdn_intra_bwd", grid=(nch // cps,),
        in_specs=[row, row, row, small, pl.BlockSpec((cps, DH, CH), lambda c: (c, 0, 0)), sq, row, row, sq,
                  row, sq, row, row, row, pl.BlockSpec((cps, 8, DH), lambda c: (c, 0, 0))],
        out_specs=[row, row, row, small],
        out_shape=[big, big, big, jax.ShapeDtypeStruct((n, DH), F32)],
        compiler_params=_params(("parallel",)),
    )(q, k, v, bg, bgt, t, u, w, p, dqg, dp, du, dw, dks, dgam)


def _local_step(x, tgt, h, w_g, cqw, late, norm_in_w, ad, gdn_norm_w, conv_b, final_norm_w,
                on_grad_c=None, on_grad_g=None, on_q=None):
    proj_g = _matmul(h, w_g, NT, F32, 512, 1408, 1024, "mm_proj_g", n=GW_COLS, b_outer=True)
    q, k, v = _prep_qkv(proj_g, cqw)
    if on_q is not None:
        q = on_q(q)
    bg, bgt = _prep_bg(proj_g, ad)
    u, w, p, t = _gdn_intra(q, k, v, bg, bgt)
    o, vn, s_in = _gdn_scan(q, k, bg, u, w, p)
    w_c, w_out, conv_w = late(o)
    proj_c = _matmul(h, w_c, NT, F32, 512, 1024, 1024, "mm_proj_c", n=CW_COLS, b_outer=True)
    mix = _conv_branch(proj_c, conv_w, conv_b, _gdn_out(o, proj_g, gdn_norm_w))
    dout, dout_b, g_fn, loss = _out_loss(mix, w_out, x, tgt, final_norm_w)

    g_wout = _matmul(mix, dout_b, TN, BF16, 512, 512, 2048, "mm_gwout")
    do, dproj_g, g_gn = _gdn_out_bwd(o, proj_g, gdn_norm_w, dout_b, w_out)
    dproj_c, g_cw, g_cb = _conv_branch_bwd(proj_c, conv_w, conv_b, dout_b, w_out)
    g_c = _matmul(dproj_c, h, TN, BF16, 1024, 512, 2048, "mm_gwin_c")
    if on_grad_c is not None:
        do = on_grad_c(g_c, g_wout, do)
    dqg, dp, du, dw, dks, dgam = _gdn_scan_bwd(q, k, bg, w, p, vn, s_in, do)
    dq, dk, dv, dbg = _gdn_intra_bwd(q, k, v, bg, bgt, t, u, w, p, dqg, dp, du, dw, dks, dgam)
    dproj_g, gq, gk, gv = _prep_qkv_bwd(proj_g, cqw, dq, dk, dv, dproj_g)
    dproj_g, g_al, g_dt = _prep_bg_bwd(proj_g, ad, dbg, dproj_g)
    g_g = _matmul(dproj_g, h, TN, BF16, 1408, 512, 2048, "mm_gwin_g")
    if on_grad_g is not None:
        dproj_g = on_grad_g(g_g, dproj_g)
    dh = _matmul(dproj_g, w_g, NN, F32, 1024, 1024, 1408, "mm_dh_g")
    gx, g_nin = _dh_rms_bwd(dproj_c, w_c, dh, x, norm_in_w, dout, 1024)
    small = dict(nin=g_nin, cb=g_cb, fn=g_fn, al=g_al, dt=g_dt, gn=g_gn, cq=(gq, gk, gv), cw=g_cw, loss=loss)
    return gx, small, (g_g, g_c, g_wout)


def _place():
    x, y, c = lax.axis_index("x"), lax.axis_index("y"), lax.axis_index("c")
    chips = [(1 - x, y), (x, 1 - y), (1 - x, 1 - y)]
    return x, y, c, chips


def _blk(ref, b):
    if isinstance(b, int):
        return ref.at[b * DH:(b + 1) * DH, :]
    return ref.at[pl.ds(pl.multiple_of(b * DH, DH), DH), :]


HBM = pl.BlockSpec(memory_space=pltpu.HBM)
SEM = pl.BlockSpec(memory_space=pltpu.SEMAPHORE)
EFFECT = pltpu.SideEffectType.DATAFLOW_SIDE_EFFECTING


def _split_start(name, issue, bufs, n_sems):
    nbuf = len(bufs)

    def body(*refs):
        issue(refs[:nbuf], refs[nbuf], refs[nbuf + 1])
        refs[-1][...] = jnp.zeros_like(refs[-1])

    out = pl.pallas_call(
        body, name=name,
        out_shape=(pltpu.SemaphoreType.DMA((n_sems,)), pltpu.SemaphoreType.DMA((n_sems,)),
                   *[pltpu.HBM(b.shape, b.dtype) for b in bufs], jax.ShapeDtypeStruct((8, DH), F32)),
        in_specs=[HBM] * nbuf,
        out_specs=(SEM, SEM, *[HBM] * nbuf, pl.BlockSpec(memory_space=pltpu.VMEM)),
        input_output_aliases={a: 2 + a for a in range(nbuf)},
        compiler_params=pltpu.CompilerParams(has_side_effects=EFFECT),
    )(*[pltpu.with_memory_space_constraint(b, pltpu.HBM) for b in bufs])
    return out[0], out[1], list(out[2:2 + nbuf]), out[-1]


def _split_wait(name, await_, send_sems, recv_sems, bufs, after):
    nbuf = len(bufs)
    after = list(after) if isinstance(after, (list, tuple)) else [after]

    def body(*refs):
        await_(refs[:nbuf], refs[nbuf], refs[nbuf + 1])

    out = pl.pallas_call(
        body, name=name,
        out_shape=tuple(pltpu.HBM(b.shape, b.dtype) for b in bufs),
        in_specs=[HBM] * nbuf + [SEM, SEM] + [ANY] * len(after), out_specs=tuple([HBM] * nbuf),
        input_output_aliases={a: a for a in range(nbuf)},
        compiler_params=pltpu.CompilerParams(has_side_effects=EFFECT),
    )(*bufs, send_sems, recv_sems, *after)
    return list(out)


def _phase_blocks(chip, phase, edges, parity=None):
    return [(b, blk) for b, (grp, blk) in enumerate(_shard_blocks(chip, edges))
            if grp == phase and (parity is None or b % 2 == parity)]


def _cols(ref, nblk):
    return ref.at[0:nblk * DH, :]


def _block_table(chip, edges, spare_g, spare_c):
    rows = []
    for s in range(4):
        sb = _shard_blocks(s, edges)
        rows.append([[blk if grp == "g" else spare_g for grp, blk in sb],
                     [blk if grp == "c" else spare_c for grp, blk in sb],
                     [int(grp == "g") for grp, _ in sb], [s] * ALIGNED_BLOCKS])
    return jnp.asarray(rows, jnp.int32)[chip]


def _place_own(a_shard, wo, cq, cw, bufs):
    d = a_shard.shape[1]
    chip = 2 * lax.axis_index("x") + lax.axis_index("y")

    def body(t_ref, a_ref, wo_ref, cq_ref, cw_ref, *refs):
        wg_ref, wc_ref, wog_ref, cqg_ref, cwg_ref = refs[5:]
        wg_ref[...] = a_ref[...]
        wc_ref[...] = a_ref[...]

        @pl.when(pl.program_id(0) == 0)
        def _():
            wog_ref[0] = wo_ref[...]
            cqg_ref[0] = cq_ref[...]
            cwg_ref[0] = cw_ref[...]

    whole = lambda s: pl.BlockSpec(s.shape, lambda b, t: (0,) * s.ndim)
    slot = lambda s: pl.BlockSpec((1,) + s.shape, lambda b, t: (t[3, 0],) + (0,) * s.ndim)
    return pl.pallas_call(
        body, name="place_own",
        grid_spec=pltpu.PrefetchScalarGridSpec(
            num_scalar_prefetch=1, grid=(ALIGNED_BLOCKS,),
            in_specs=[pl.BlockSpec((DH, d), lambda b, t: (b, 0)), whole(wo), whole(cq), whole(cw)] + [ANY] * 5,
            out_specs=[pl.BlockSpec((DH, d), lambda b, t: (t[0, b], 0)),
                       pl.BlockSpec((DH, d), lambda b, t: (t[1, b], 0)), slot(wo), slot(cq), slot(cw)]),
        out_shape=[jax.ShapeDtypeStruct(b.shape, b.dtype) for b in bufs],
        input_output_aliases={5 + a: a for a in range(5)},
        compiler_params=_params(("arbitrary",)),
    )(_block_table(chip, True, G_SPARE, C_SPARE), a_shard, wo, cq, cw, *bufs)


def _tie(x, token, name):
    def body(x_ref, t_ref, o_ref):
        del x_ref, t_ref, o_ref

    return pl.pallas_call(
        body, name=name, in_specs=[ANY, ANY], out_specs=ANY,
        out_shape=jax.ShapeDtypeStruct(x.shape, x.dtype), input_output_aliases={0: 0},
    )(x, token)


def _gather_start(phase, a_shard, w_grp, singles):
    ns = len(singles)

    def issue(refs, send_sems, recv_sems):
        a_ref, w_ref = refs[0], refs[1]
        x, y, c, chips = _place()
        mine = 2 * x + y
        for jj, (px, py) in enumerate(chips):
            to = dict(device_id=(px, py, c), device_id_type=MESH)
            for a in range(ns):
                pltpu.make_async_remote_copy(
                    src_ref=refs[2 + 2 * a], dst_ref=refs[3 + 2 * a].at[mine],
                    send_sem=send_sems.at[(1 + ns) * jj + 1 + a], recv_sem=recv_sems.at[(1 + ns) * jj + 1 + a],
                    **to).start()
        for s in range(4):
            for par in range(2):
                blocks = _phase_blocks(s, phase, True, par)
                if blocks:
                    @pl.when((mine == s) & (c == par))
                    def _():
                        for jj, (px, py) in enumerate(chips):
                            for b, blk in blocks:
                                pltpu.make_async_remote_copy(
                                    src_ref=_blk(a_ref, b), dst_ref=_blk(w_ref, blk),
                                    send_sem=send_sems.at[(1 + ns) * jj], recv_sem=recv_sems.at[(1 + ns) * jj],
                                    device_id=(px, py, c), device_id_type=MESH).start()

    bufs = [a_shard, w_grp] + [t for pair in singles for t in pair]
    return _split_start("gather_start_" + phase, issue, bufs, 3 * (1 + ns))


def _gather_wait(phase, send_sems, recv_sems, bufs, after):
    ns = (len(bufs) - 2) // 2

    def await_(refs, send_sems, recv_sems):
        a_ref, w_ref = refs[0], refs[1]
        x, y, c, chips = _place()
        mine = 2 * x + y
        for jj, (px, py) in enumerate(chips):
            to = dict(device_id=(px, py, c), device_id_type=MESH)
            peer = 2 * px + py
            for a in range(ns):
                cp = pltpu.make_async_remote_copy(
                    src_ref=refs[2 + 2 * a], dst_ref=refs[3 + 2 * a].at[mine],
                    send_sem=send_sems.at[(1 + ns) * jj + 1 + a], recv_sem=recv_sems.at[(1 + ns) * jj + 1 + a], **to)
                cp.wait_recv()
                cp.wait_send()
            for s in range(4):
                for par in range(2):
                    nblk = len(_phase_blocks(s, phase, True, par))
                    if nblk:
                        both = pltpu.make_async_remote_copy(
                            src_ref=_cols(a_ref, nblk), dst_ref=_cols(w_ref, nblk),
                            send_sem=send_sems.at[(1 + ns) * jj], recv_sem=recv_sems.at[(1 + ns) * jj], **to)

                        @pl.when((peer == s) & (c == par))
                        def _():
                            both.wait_recv()

                        @pl.when((mine == s) & (c == par))
                        def _():
                            both.wait_send()

    return _split_wait("gather_wait_" + phase, await_, send_sems, recv_sems, bufs, after)


def _sibling_forward_parts(phase):
    def each(w_ref, send_sems, recv_sems, start):
        x, y, c, chips = _place()
        to = dict(device_id=(x, y, 1 - c), device_id_type=MESH)
        for jj, (px, py) in enumerate(chips):
            peer = 2 * px + py
            for s in range(4):
                for par in range(2):
                    mine_blocks = _phase_blocks(s, phase, True, par)
                    theirs = len(_phase_blocks(s, phase, True, 1 - par))
                    if not (mine_blocks or theirs):
                        continue

                    @pl.when((peer == s) & (c == par))
                    def _():
                        if start:
                            for _, blk in mine_blocks:
                                pltpu.make_async_remote_copy(
                                    src_ref=_blk(w_ref, blk), dst_ref=_blk(w_ref, blk),
                                    send_sem=send_sems.at[jj], recv_sem=recv_sems.at[jj], **to).start()
                            return
                        if theirs:
                            pltpu.make_async_remote_copy(
                                src_ref=_cols(w_ref, theirs), dst_ref=_cols(w_ref, theirs),
                                send_sem=send_sems.at[jj], recv_sem=recv_sems.at[jj], **to).wait_recv()
                        if mine_blocks:
                            pltpu.make_async_remote_copy(
                                src_ref=_cols(w_ref, len(mine_blocks)), dst_ref=_cols(w_ref, len(mine_blocks)),
                                send_sem=send_sems.at[jj], recv_sem=recv_sems.at[jj], **to).wait_send()

    issue = lambda refs, send_sems, recv_sems: each(refs[0], send_sems, recv_sems, True)
    await_ = lambda refs, send_sems, recv_sems: each(refs[0], send_sems, recv_sems, False)
    return issue, await_


def _sibling_forward(phase, w_grp):
    issue, await_ = _sibling_forward_parts(phase)

    def body(w_in_ref, w_ref, send_sems, recv_sems):
        del w_in_ref
        issue([w_ref], send_sems, recv_sems)
        await_([w_ref], send_sems, recv_sems)

    return pl.pallas_call(
        body, name="sibling_forward_" + phase, in_specs=[ANY], out_specs=ANY,
        out_shape=jax.ShapeDtypeStruct(w_grp.shape, w_grp.dtype), input_output_aliases={0: 0},
        scratch_shapes=[pltpu.SemaphoreType.DMA((3,)), pltpu.SemaphoreType.DMA((3,))],
    )(w_grp)


def _merge_edges(w, edge0, mixed, name):
    d = w.shape[1]

    def body(e_ref, o_ref):
        o_ref[...] = e_ref[0:DH, :] + e_ref[DH:2 * DH, :]

    def to_block(i):
        r = mixed[-1]
        for kk in range(len(mixed) - 2, -1, -1):
            r = jnp.where(i == kk, mixed[kk], r)
        return r

    return pl.pallas_call(
        body, name=name, grid=(len(mixed),),
        in_specs=[pl.BlockSpec((2 * DH, d), lambda i: (edge0 // 2 + i, 0))],
        out_specs=pl.BlockSpec((DH, d), lambda i: (to_block(i), 0)),
        out_shape=jax.ShapeDtypeStruct(w.shape, w.dtype),
        input_output_aliases={0: 0},
        compiler_params=_params(("arbitrary",)),
    )(w)


def _scatter_start(phase, g_grp, land, singles, halved=False):
    ns = len(singles)

    def issue(refs, send_sems, recv_sems):
        g_ref, land_ref = refs[0], refs[1]
        x, y, c, chips = _place()
        for jj, (px, py) in enumerate(chips):
            to = dict(device_id=(px, py, c), device_id_type=MESH)
            peer = 2 * px + py
            for a in range(ns):
                pltpu.make_async_remote_copy(
                    src_ref=refs[2 + 2 * a].at[peer], dst_ref=refs[3 + 2 * a].at[jj],
                    send_sem=send_sems.at[(1 + ns) * jj + 1 + a], recv_sem=recv_sems.at[(1 + ns) * jj + 1 + a],
                    **to).start()
            for s in range(4):
                for par in ((0, 1) if halved else (None,)):
                    blocks = _phase_blocks(s, phase, False, par)
                    if blocks:
                        @pl.when((peer == s) if par is None else ((peer == s) & (c == par)))
                        def _():
                            for b, blk in blocks:
                                pltpu.make_async_remote_copy(
                                    src_ref=_blk(g_ref, blk), dst_ref=_blk(land_ref.at[jj], b),
                                    send_sem=send_sems.at[(1 + ns) * jj], recv_sem=recv_sems.at[(1 + ns) * jj],
                                    **to).start()

    bufs = [g_grp, land] + [t for pair in singles for t in pair]
    return _split_start("scatter_start_" + phase, issue, bufs, 3 * (1 + ns))


def _scatter_wait(phase, send_sems, recv_sems, bufs, after, halved=False):
    ns = (len(bufs) - 2) // 2

    def await_(refs, send_sems, recv_sems):
        g_ref, land_ref = refs[0], refs[1]
        x, y, c, chips = _place()
        mine = 2 * x + y
        for jj, (px, py) in enumerate(chips):
            to = dict(device_id=(px, py, c), device_id_type=MESH)
            peer = 2 * px + py
            for a in range(ns):
                cp = pltpu.make_async_remote_copy(
                    src_ref=refs[2 + 2 * a].at[peer], dst_ref=refs[3 + 2 * a].at[jj],
                    send_sem=send_sems.at[(1 + ns) * jj + 1 + a], recv_sem=recv_sems.at[(1 + ns) * jj + 1 + a], **to)
                cp.wait_recv()
                cp.wait_send()
            for s in range(4):
                for par in ((0, 1) if halved else (None,)):
                    nblk = len(_phase_blocks(s, phase, False, par))
                    if nblk:
                        both = pltpu.make_async_remote_copy(
                            src_ref=_cols(g_ref, nblk), dst_ref=_cols(land_ref.at[jj], nblk),
                            send_sem=send_sems.at[(1 + ns) * jj], recv_sem=recv_sems.at[(1 + ns) * jj], **to)

                        @pl.when((mine == s) if par is None else ((mine == s) & (c == par)))
                        def _():
                            both.wait_recv()

                        @pl.when((peer == s) if par is None else ((peer == s) & (c == par)))
                        def _():
                            both.wait_send()

    return _split_wait("scatter_wait_" + phase, await_, send_sems, recv_sems, bufs, after)


def _needed_blocks(phase, parity):
    return sorted({blk for s in range(4) for _, blk in _phase_blocks(s, phase, False, parity)})


def _pair_reduce(phase, g_grp):
    n, d = g_grp.shape

    def swap(g_ref, sib_ref, send_sem, recv_sem):
        x, y, c, _ = _place()
        to = dict(device_id=(x, y, 1 - c), device_id_type=MESH)
        for par in range(2):
            give, get = _needed_blocks(phase, 1 - par), _needed_blocks(phase, par)

            @pl.when(c == par)
            def _():
                for blk in give:
                    pltpu.make_async_remote_copy(src_ref=_blk(g_ref, blk), dst_ref=_blk(sib_ref, blk),
                                                 send_sem=send_sem, recv_sem=recv_sem, **to).start()
                pltpu.make_async_remote_copy(src_ref=_cols(g_ref, len(get)), dst_ref=_cols(sib_ref, len(get)),
                                             send_sem=send_sem, recv_sem=recv_sem, **to).wait_recv()
                pltpu.make_async_remote_copy(src_ref=_cols(g_ref, len(give)), dst_ref=_cols(sib_ref, len(give)),
                                             send_sem=send_sem, recv_sem=recv_sem, **to).wait_send()

    sib = pl.pallas_call(
        swap, name="pair_swap_" + phase, in_specs=[ANY], out_specs=ANY,
        out_shape=jax.ShapeDtypeStruct((n, d), g_grp.dtype),
        scratch_shapes=[pltpu.SemaphoreType.DMA, pltpu.SemaphoreType.DMA],
    )(*_in_hbm(g_grp))

    lists = [_needed_blocks(phase, par) for par in range(2)]
    longest = max(len(t) for t in lists)
    table = jnp.asarray([t + [t[-1]] * (longest - len(t)) for t in lists], jnp.int32)[lax.axis_index("c")]

    def add(t_ref, a_ref, b_ref, o_ref):
        o_ref[...] = (a_ref[...].astype(F32) + b_ref[...].astype(F32)).astype(o_ref.dtype)

    blk = pl.BlockSpec((DH, d), lambda i, t: (t[i], 0))
    return pl.pallas_call(
        add, name="pair_add_" + phase,
        grid_spec=pltpu.PrefetchScalarGridSpec(num_scalar_prefetch=1, grid=(longest,),
                                               in_specs=[blk, blk], out_specs=blk),
        out_shape=jax.ShapeDtypeStruct((n, d), g_grp.dtype),
        compiler_params=_params(("arbitrary",)),
    )(table, g_grp, sib)


def _sum_shard(g_g, g_c, land):
    d = g_g.shape[1]
    chip = 2 * lax.axis_index("x") + lax.axis_index("y")

    def body(t_ref, gg_ref, gc_ref, land_ref, o_ref):
        b = pl.program_id(0)
        in_g = t_ref[2, b] == 1
        own = jnp.where(in_g, gg_ref[...].astype(F32), gc_ref[...].astype(F32))
        for jj in range(3):
            own = own + land_ref[jj].astype(F32)
        o_ref[...] = jnp.where(in_g & (b % 2 != lax.axis_index("c")), 0.0, own)

    return pl.pallas_call(
        body, name="sum_w_in",
        grid_spec=pltpu.PrefetchScalarGridSpec(
            num_scalar_prefetch=1, grid=(ALIGNED_BLOCKS,),
            in_specs=[pl.BlockSpec((DH, d), lambda b, t: (t[0, b], 0)), pl.BlockSpec((DH, d), lambda b, t: (t[1, b], 0)),
                      pl.BlockSpec((3, DH, d), lambda b, t: (0, b, 0))],
            out_specs=pl.BlockSpec((DH, d), lambda b, t: (b, 0))),
        out_shape=jax.ShapeDtypeStruct((ALIGNED_W, d), F32),
        compiler_params=_params(("arbitrary",)),
    )(_block_table(chip, False, 0, 0), g_g, g_c, land)


def _sum_rows(stack, land, rows):
    _, r, d = stack.shape
    rows = min(rows, r)
    chip = 2 * lax.axis_index("x") + lax.axis_index("y")

    def body(t_ref, own_ref, land_ref, o_ref):
        acc = own_ref[0].astype(F32)
        for jj in range(3):
            acc = acc + land_ref[jj].astype(F32)
        o_ref[...] = acc

    return pl.pallas_call(
        body, name="sum_w_out",
        grid_spec=pltpu.PrefetchScalarGridSpec(
            num_scalar_prefetch=1, grid=(r // rows,),
            in_specs=[pl.BlockSpec((1, rows, d), lambda i, t: (t[0], i, 0)),
                      pl.BlockSpec((3, rows, d), lambda i, t: (0, i, 0))],
            out_specs=pl.BlockSpec((rows, d), lambda i, t: (i, 0))),
        out_shape=jax.ShapeDtypeStruct((r, d), F32),
        compiler_params=_params(("arbitrary",)),
    )(jnp.reshape(chip, (1,)).astype(jnp.int32), stack, land)


def _exchange_parts(n_swap, with_pack):
    def copies(refs, send_sems, recv_sems):
        x, y, c, _ = _place()
        me = 4 * x + 2 * y + c
        cps = [pltpu.make_async_remote_copy(
            src_ref=refs[2 * a], dst_ref=refs[2 * a + 1], send_sem=send_sems.at[a], recv_sem=recv_sems.at[a],
            device_id=(x, y, 1 - c), device_id_type=MESH) for a in range(n_swap)]
        if with_pack:
            pack_ref, packs = refs[2 * n_swap], refs[2 * n_swap + 1]
            for r in range(1, 8):
                dx, dy, dc = (r >> 2) & 1, (r >> 1) & 1, r & 1
                peer = (x + dx - 2 * x * dx, y + dy - 2 * y * dy, c + dc - 2 * c * dc)
                cps.append(pltpu.make_async_remote_copy(
                    src_ref=pack_ref, dst_ref=packs.at[me], send_sem=send_sems.at[n_swap + r - 1],
                    recv_sem=recv_sems.at[n_swap + r - 1], device_id=peer, device_id_type=MESH))
        return cps

    def issue(refs, send_sems, recv_sems):
        for cp in copies(refs, send_sems, recv_sems):
            cp.start()

    def await_(refs, send_sems, recv_sems):
        cps = copies(refs, send_sems, recv_sems)
        for cp in cps:
            cp.wait_recv()
        for cp in cps:
            cp.wait_send()

    return issue, await_, n_swap + (7 if with_pack else 0)


def _sum_packs(pack, packs):
    x, y, c = lax.axis_index("x"), lax.axis_index("y"), lax.axis_index("c")
    me = jnp.reshape(4 * x + 2 * y + c, (1,)).astype(jnp.int32)

    def body(me_ref, own_ref, p_ref, o_ref):
        acc = jnp.where(me_ref[0] == 0, own_ref[...], p_ref[0])
        for d in range(1, 8):
            acc = acc + jnp.where(me_ref[0] == d, own_ref[...], p_ref[d])
        o_ref[...] = acc

    full = lambda s: pl.BlockSpec(s.shape, lambda i, t: (0,) * s.ndim)
    return pl.pallas_call(
        body, name="sum_packs",
        grid_spec=pltpu.PrefetchScalarGridSpec(num_scalar_prefetch=1, grid=(1,), in_specs=[full(pack), full(packs)],
                                               out_specs=full(pack)),
        out_shape=jax.ShapeDtypeStruct(pack.shape, F32),
    )(me, pack, packs)


def _adamw_update(g, w_ref, m_ref, v_ref, go, do, mo, vo):
    c1 = 1.0 / (1.0 - ADAM_B1 ** ADAM_STEP)
    c2 = 1.0 / (1.0 - ADAM_B2 ** ADAM_STEP)
    mn = ADAM_B1 * m_ref[...] + (1.0 - ADAM_B1) * g
    vn = ADAM_B2 * v_ref[...] + (1.0 - ADAM_B2) * (g * g)
    go[...] = g
    mo[...] = mn
    vo[...] = vn
    do[...] = -ADAM_LR * ((mn * c1) / (jnp.sqrt(vn * c2) + ADAM_EPS) + ADAM_WD * w_ref[...])


def _adamw(w, m, v, g1, g2, rows, name):
    r, cdim = w.shape
    rows = min(rows, r)

    def body(*refs):
        n_in = 4 if g2 is None else 5
        w_ref, m_ref, v_ref, g_ref = refs[:4]
        g = g_ref[...] if g2 is None else g_ref[...] + refs[4][...]
        _adamw_update(g, w_ref, m_ref, v_ref, *refs[n_in:n_in + 4])

    blk = pl.BlockSpec((rows, cdim), lambda i: (i, 0))
    args = [w, m, v, g1] + ([] if g2 is None else [g2])
    shp = jax.ShapeDtypeStruct((r, cdim), F32)
    return pl.pallas_call(
        body, name=name, grid=(r // rows,),
        in_specs=[blk] * len(args), out_specs=[blk] * 4, out_shape=[shp] * 4,
        compiler_params=_params(("parallel",), 20 * rows * cdim * 4 + 8 * 2**20),
    )(*_in_hbm(*args))


def _adamw_shard(wt, mt, vt, g1, g2):
    r, d = wt.shape
    cols = min(128, d)

    def body(w_ref, m_ref, v_ref, g_ref, g2_ref, go, do, mo, vo, pad_ref):
        chip = 2 * lax.axis_index("x") + lax.axis_index("y")
        back = [(ALIGNED_W - s) % ALIGNED_W for s in SHIFTS]
        pad_ref[...] = pltpu.roll(g_ref[...] + g2_ref[...], _by_chip(chip, back), 0)
        outs = [o.at[:, 0, :] for o in (go, do, mo, vo)]
        _adamw_update(pad_ref[0:r, :], w_ref, m_ref, v_ref, *outs)

    blk = pl.BlockSpec((r, cols), lambda i: (0, i))
    gblk = pl.BlockSpec((ALIGNED_W, cols), lambda i: (0, i))
    oblk = pl.BlockSpec((r, 1, cols), lambda i: (0, 0, i))
    shp = jax.ShapeDtypeStruct((r, 1, d), F32)
    return pl.pallas_call(
        body, name="adamw_w_in", grid=(d // cols,),
        in_specs=[blk] * 3 + [gblk] * 2, out_specs=[oblk] * 4, out_shape=[shp] * 4,
        scratch_shapes=[pltpu.VMEM((ALIGNED_W, cols), F32)],
        compiler_params=_params(("parallel",), 24 * ALIGNED_W * cols * 4 + 8 * 2**20),
    )(wt, mt, vt, g1, g2)


def _pad_lanes(a, width):
    return jnp.pad(a, ((0, 0), (0, width - a.shape[1])))


def _gathered_to_full(g):
    return jnp.transpose(g, (1, 0, 2)).reshape(g.shape[1], 4 * g.shape[2])


def _row(a):
    return _pad_lanes(a.reshape(1, -1), 1024)


def _small_pack(nin, cb, fn, al, dt, gn, cqw_shard, cw_shard):
    ad = jnp.concatenate([al.reshape(1, -1), dt.reshape(1, -1)], axis=1)
    rows = [_row(nin), _row(cb), _row(fn), _row(ad), _row(gn), cqw_shard.reshape(3, 1024), _row(cw_shard)]
    out = jnp.concatenate(rows, axis=0)
    return jnp.pad(out, ((0, 16 - out.shape[0]), (0, 0)))


def kernel(x, norm_in_w, w_in, conv_qkv_w, A_log, dt_bias, gdn_norm_w, conv_w, conv_b, w_out, final_norm_w, loss_target, m_norm_in_w, m_w_in, m_conv_qkv_w, m_A_log, m_dt_bias, m_gdn_norm_w, m_conv_w, m_conv_b, m_w_out, m_final_norm_w, v_norm_in_w, v_w_in, v_conv_qkv_w, v_A_log, v_dt_bias, v_gdn_norm_w, v_conv_w, v_conv_b, v_w_out, v_final_norm_w):
    chip = 2 * lax.axis_index("x") + lax.axis_index("y")
    a_shard = _align_shard(jnp.transpose(w_in, (2, 0, 1)))
    wo_b = _cast_bf16(w_out[0], 256, "cast_w_out")
    d_model = x.shape[-1]
    stack = lambda s: lax.empty((4,) + s.shape, s.dtype)
    wg0 = lax.empty((WG_BLOCKS * DH, d_model), BF16)
    wc0 = lax.empty((WC_BLOCKS * DH, d_model), BF16)
    ss_g, rs_g, bufs_g, tok_g = _gather_start("g", a_shard, wg0, [(conv_qkv_w[0], stack(conv_qkv_w[0]))])
    ss_c, rs_c, bufs_c, tok_c = _gather_start("c", bufs_g[0], wc0,
                                              [(conv_w[0], stack(conv_w[0])), (wo_b, stack(wo_b))])
    wg1, wc1, wog1, cqg1, cwg1 = _place_own(bufs_c[0], bufs_c[4], bufs_g[2], bufs_c[2],
                                            [bufs_g[1], bufs_c[1], bufs_c[5], bufs_g[3], bufs_c[3]])
    x0 = x[0]
    h = _rms_in(x0, _tie(_tie(norm_in_w, tok_g, "after_gather_start_g"), tok_c, "after_gather_start_c"))
    adam_in = [jnp.transpose(a[0]) for a in (w_in, m_w_in, v_w_in)]
    sp = lambda nin, cb, fn, al, dt, gn, cq, cwv: _small_pack(nin, cb, fn, al, dt, gn, cq[0], cwv[0])
    w_s = sp(norm_in_w, conv_b, final_norm_w, A_log, dt_bias, gdn_norm_w, conv_qkv_w, conv_w)
    m_s = sp(m_norm_in_w, m_conv_b, m_final_norm_w, m_A_log, m_dt_bias, m_gdn_norm_w, m_conv_qkv_w, m_conv_w)
    v_s = sp(v_norm_in_w, v_conv_b, v_final_norm_w, v_A_log, v_dt_bias, v_gdn_norm_w, v_conv_qkv_w, v_conv_w)
    a_thru, wg, _, cq_g = _gather_wait("g", ss_g, rs_g, [bufs_c[0], wg1, bufs_g[2], cqg1],
                                       [h, w_s, m_s, v_s] + adam_in[1:])
    w_g = _merge_edges(_sibling_forward("g", wg), G_EDGE, G_MIXED, "merge_edges_g")
    cqw = _gathered_to_full(cq_g)
    ad = jnp.pad(jnp.concatenate([A_log, dt_bias], axis=0), ((0, 0), (A_LANE, 0)))
    fwd_c = {}

    def on_q(q):
        _, wc, _, cw_g, _, wo_g = _gather_wait("c", ss_c, rs_c,
                                               [a_thru, wc1, bufs_c[2], cwg1, bufs_c[4], wog1], q)
        issue, _ = _sibling_forward_parts("c")
        ss, rs, (wc,), tok = _split_start("sibling_forward_start_c", issue, [wc], 3)
        fwd_c.update(ss=ss, rs=rs, wc=wc, cw_g=cw_g, wo_g=wo_g)
        return _tie(q, tok, "after_sibling_forward_start_c")

    def late(o):
        _, await_ = _sibling_forward_parts("c")
        (wc,) = _split_wait("sibling_forward_wait_c", await_, fwd_c["ss"], fwd_c["rs"], [fwd_c["wc"]], o)
        return (_merge_edges(wc, C_EDGE, C_MIXED, "merge_edges_c"), fwd_c["wo_g"].reshape(2 * GW, d_model),
                _gathered_to_full(fwd_c["cw_g"]))

    scat = {}

    def on_grad_c(g_c, g_wout, do):
        go4 = g_wout.reshape(4, GW // 2, d_model)
        land = lax.empty((3, ALIGNED_W, d_model), BF16)
        land_o = lax.empty((3, GW // 2, d_model), BF16)
        ss, rs, bufs, tok = _scatter_start("c", g_c, land, [(go4, land_o)])
        scat["c"] = (ss, rs, bufs)
        return _tie(do, tok, "after_scatter_start_c")

    def on_grad_g(g_g, dproj_g):
        ss, rs, bufs, tok = _scatter_start("g", _pair_reduce("g", g_g), scat["c"][2][1], [], halved=True)
        scat["g"] = (ss, rs, bufs)
        return _tie(dproj_g, tok, "after_scatter_start_g")

    gx, sm, _ = _local_step(x0, loss_target[0], h, w_g, cqw, late, norm_in_w, ad, gdn_norm_w, conv_b,
                            final_norm_w.reshape(1, -1), on_grad_c, on_grad_g, on_q)

    ss, rs, bufs = scat["c"]
    g_c, land, go4, land_o = _scatter_wait("c", ss, rs, [bufs[0], scat["g"][2][1], bufs[2], bufs[3]], gx)
    part_out = _sum_rows(go4, land_o, 128)
    ad_g = jnp.concatenate([sm["al"][:, A_LANE:], sm["dt"][:, A_LANE:]], axis=1)
    pack = jnp.concatenate([_row(sm["nin"]), _row(sm["cb"]), _row(sm["fn"]), _row(ad_g), _row(sm["gn"]),
                            jnp.concatenate(sm["cq"], axis=1).reshape(12, 1024), sm["cw"], _row(sm["loss"])], axis=0)
    pack = jnp.pad(pack, ((0, PACK_ROWS - pack.shape[0]), (0, 0)))
    issue, await_a, nsem = _exchange_parts(1, True)
    ss_a, rs_a, bufs_a, tok_a = _split_start(
        "exchange_start_small", issue,
        [part_out, lax.empty(part_out.shape, F32), pack, lax.empty((8,) + pack.shape, F32)], nsem)
    ss, rs, bufs = scat["g"]
    g_g, land = _scatter_wait("g", ss, rs, [bufs[0], land], [gx, tok_a], halved=True)
    part_in = _sum_shard(g_g, g_c, land)
    issue, await_b, nsem = _exchange_parts(1, False)
    ss_b, rs_b, bufs_b, tok_b = _split_start("exchange_start_w_in", issue,
                                             [part_in, lax.empty(part_in.shape, F32)], nsem)
    part_out, sib_out, pack, packs = _split_wait("exchange_wait_small", await_a, ss_a, rs_a, bufs_a, tok_b)
    tot = _sum_packs(pack, packs)
    g_wo, d_wo, m_wo, v_wo = _adamw(w_out[0], m_w_out[0], v_w_out[0], part_out, sib_out, 128, "adamw_w_out")
    g_cq_sh = lax.dynamic_slice_in_dim(tot[R_CQ:R_CQ + 12].reshape(4, 3 * GW), chip * 768, 768, axis=1)
    g_cw_sh = lax.dynamic_slice_in_dim(tot[R_CW:R_CW + 3], chip * 256, 256, axis=1)
    g_s = _small_pack(tot[R_NIN], tot[R_CB], tot[R_FN], tot[R_AD, :HEADS], tot[R_AD, HEADS:2 * HEADS],
                      tot[R_GN, :DH], g_cq_sh, g_cw_sh)
    small = _adamw(w_s, m_s, v_s, g_s, None, 16, "adamw_small")
    part_in, sib_in = _split_wait("exchange_wait_w_in", await_b, ss_b, rs_b, bufs_b, [small[0], d_wo])
    g_wi, d_wi, m_wi, v_wi = [jnp.transpose(a, (1, 2, 0))[0] for a in _adamw_shard(*adam_in, part_in, sib_in)]

    def unpack(a, big_in, big_out):
        return (a[0:1], big_in[None], a[5:8].reshape(1, 4, 768), a[3:4, :HEADS], a[3:4, HEADS:2 * HEADS],
                a[4:5, :DH], a[8, :768].reshape(1, 3, 256), a[1:2], big_out[None], a[2])

    loss = tot[R_LOSS, 0]
    return (loss, gx[None], *unpack(small[0], g_wi, g_wo), *unpack(small[1], d_wi, d_wo),
            *unpack(small[2], m_wi, m_wo), *unpack(small[3], v_wi, v_wo))
```

```python
import functools
import math

import jax
import jax.numpy as jnp
from jax import lax
from jax.experimental import pallas as pl
from jax.experimental.pallas import tpu as pltpu

F32 = jnp.float32
BF16 = jnp.bfloat16
MESH = pl.DeviceIdType.MESH
ANY = pl.BlockSpec(memory_space=pl.ANY)

HEADS = 8
DH = 128
CH = 64
GW = HEADS * DH
EPS = 1e-6
VMEM_V7X = 64 * 1024 * 1024

QB, KB, VB, ZB, BAB = 0, 8, 16, 24, 32
A_LANE = 120
NG, NC = 33, 32
GW_COLS, CW_COLS = NG * DH, NC * DH

SHARD_W = 2052
ALIGNED_BLOCKS = 17
ALIGNED_W = ALIGNED_BLOCKS * DH
SHIFTS = (0, 4, ALIGNED_W - 8, ALIGNED_W - 4)
G_EDGE, C_EDGE = 34, 32
G_SPARE, C_SPARE = 33, 34
WG_BLOCKS, WC_BLOCKS = 38, 36
G_MIXED, C_MIXED = (2, BAB), (4 * 7 + 1,)


def _shard_blocks(chip, edges):
    g, c = "g", "c"
    if chip == 0:
        out = [(g, 3 * b) for b in range(8)] + [(g, 3 * b + 1) for b in range(8)] + [(g, G_EDGE, G_MIXED[0])]
    elif chip == 1:
        out = [(g, G_EDGE + 1, G_MIXED[0])] + [(g, 3 * b + 2) for b in range(1, 8)]
        out += [(g, ZB + b) for b in range(8)] + [(g, G_EDGE + 2, G_MIXED[1])]
    elif chip == 2:
        out = [(c, 4 * b) for b in range(8)] + [(c, 4 * b + 1) for b in range(7)]
        out += [(c, C_EDGE, C_MIXED[0]), (g, G_EDGE + 3, G_MIXED[1])]
    else:
        out = [(c, 4 * b + 2) for b in range(8)] + [(c, 4 * b + 3) for b in range(8)] + [(c, C_EDGE + 1, C_MIXED[0])]
    return [(o[0], o[1] if (edges or len(o) == 2) else o[2]) for o in out]


def _by_chip(chip, vals):
    if all(v == vals[0] for v in vals):
        return vals[0]
    r = vals[3]
    for kk in (2, 1, 0):
        r = jnp.where(chip == kk, vals[kk], r)
    return r

ADAM_LR, ADAM_B1, ADAM_B2, ADAM_EPS, ADAM_WD, ADAM_STEP = 0.001, 0.9, 0.999, 1e-08, 0.01, 10

R_NIN, R_CB, R_FN, R_AD, R_GN, R_CQ, R_CW, R_LOSS, PACK_ROWS = 0, 1, 2, 3, 4, 5, 17, 20, 24

NN = ((1,), (0,))
NT = ((1,), (1,))
TN = ((0,), (0,))


def _dot(a, b, dims=NN, mode="lo"):
    dn = (dims, ((), ()))
    if mode == "hi":
        return lax.dot_general(a, b, dn, precision=lax.Precision.HIGHEST, preferred_element_type=F32)
    ah, bh = a.astype(BF16), b.astype(BF16)
    out = lax.dot_general(ah, bh, dn, preferred_element_type=F32)
    if mode == "x3":
        al = (a - ah.astype(F32)).astype(BF16)
        bl = (b - bh.astype(F32)).astype(BF16)
        out = out + lax.dot_general(ah, bl, dn, preferred_element_type=F32)
        out = out + lax.dot_general(al, bh, dn, preferred_element_type=F32)
    return out


P_GRAM, P_INV, P_SOL, P_SCAN, P_SCANB, P_BWD = "lo", "lo", "lo", "lo", "lo", "lo"
P_CUM = "x3"


def _params(sem=None, vmem=None):
    kw = {}
    if sem is not None:
        kw["dimension_semantics"] = sem
    if vmem is not None:
        kw["vmem_limit_bytes"] = int(min(max(vmem, 32 * 2**20), VMEM_V7X - 8 * 2**20))
    return pltpu.CompilerParams(**kw)


def _in_hbm(*arrays):
    return [pltpu.with_memory_space_constraint(a, pltpu.HBM) for a in arrays]


def _sigmoid(x):
    return 1.0 / (1.0 + jnp.exp(-x))


def _dsilu(x, s):
    return s * (1.0 + x * (1.0 - s))


def _rows(shape):
    return lax.broadcasted_iota(jnp.int32, shape, 0)


def _shift_down(x, s):
    if s == 0:
        return x
    return jnp.where(_rows(x.shape) >= s, pltpu.roll(x, s, 0), 0.0)


def _shift_up(x, s):
    if s == 0:
        return x
    n = x.shape[0]
    return jnp.where(_rows(x.shape) < n - s, pltpu.roll(x, n - s, 0), 0.0)


def _matmul(a, b, dims, out_dtype, tm, tn, tk, name, add=None, n=None, b_outer=False):
    if dims == NN:
        (m, k), n = a.shape, b.shape[1]
    elif dims == NT:
        (m, k), n = a.shape, (n or b.shape[0])
    else:
        (k, m), n = a.shape, b.shape[1]
    tm, tn, tk = min(tm, m), min(tn, n), min(tk, k)
    assert m % tm == 0 and n % tn == 0 and k % tk == 0, (name, m, n, k, tm, tn, tk)
    nk = k // tk

    def body(*refs):
        if add is None:
            a_ref, b_ref, o_ref = refs[:3]
            add_ref = None
        else:
            a_ref, b_ref, add_ref, o_ref = refs[:4]
        part = _dot(a_ref[...], b_ref[...], dims)
        if nk == 1:
            if add_ref is not None:
                part = part + add_ref[...]
            o_ref[...] = part.astype(out_dtype)
            return
        acc = refs[-1]
        kk = pl.program_id(2)

        @pl.when(kk == 0)
        def _():
            acc[...] = part

        @pl.when(kk > 0)
        def _():
            acc[...] += part

        @pl.when(kk == nk - 1)
        def _():
            r = acc[...]
            if add_ref is not None:
                r = r + add_ref[...]
            o_ref[...] = r.astype(out_dtype)

    ij = (lambda g0, g1: (g1, g0)) if b_outer else (lambda g0, g1: (g0, g1))

    def spec(shape, pick):
        return pl.BlockSpec(shape, lambda g0, g1, kk: pick(*ij(g0, g1), kk))

    a_spec = spec((tk, tm), lambda i, j, kk: (kk, i)) if dims == TN else spec((tm, tk), lambda i, j, kk: (i, kk))
    b_spec = spec((tn, tk), lambda i, j, kk: (j, kk)) if dims == NT else spec((tk, tn), lambda i, j, kk: (kk, j))
    o_spec = spec((tm, tn), lambda i, j, kk: (i, j))
    in_specs = [a_spec, b_spec]
    args = [a, b]
    if add is not None:
        in_specs.append(o_spec)
        args.append(add)
    osz = jnp.dtype(out_dtype).itemsize
    est = 2 * (tm * tk * a.dtype.itemsize + tk * tn * b.dtype.itemsize + tm * tn * osz)
    est += 3 * tm * tn * 4 + (2 * tm * tn * 4 if add is not None else 0)
    return pl.pallas_call(
        body, name=name, grid=(n // tn, m // tm, nk) if b_outer else (m // tm, n // tn, nk),
        in_specs=in_specs, out_specs=o_spec,
        out_shape=jax.ShapeDtypeStruct((m, n), out_dtype),
        scratch_shapes=[pltpu.VMEM((tm, tn), F32)] if nk > 1 else [],
        compiler_params=_params(("parallel", "parallel", "arbitrary"), est + 8 * 2**20),
    )(*args)


def _cast_bf16(a, rows, name):
    r, c = a.shape
    rows = min(rows, r)

    def body(a_ref, o_ref):
        o_ref[...] = a_ref[...].astype(BF16)

    return pl.pallas_call(
        body, name=name, grid=(r // rows,),
        in_specs=[pl.BlockSpec((rows, c), lambda i: (i, 0))],
        out_specs=pl.BlockSpec((rows, c), lambda i: (i, 0)),
        out_shape=jax.ShapeDtypeStruct((r, c), BF16),
        compiler_params=_params(("parallel",)),
    )(a)


def _align_shard(wt):
    r, _, d = wt.shape
    cols = min(256, d)

    def body(w_ref, o_ref, pad_ref):
        chip = 2 * lax.axis_index("x") + lax.axis_index("y")
        pad_ref[...] = jnp.zeros_like(pad_ref)
        pad_ref[0:r, :] = w_ref[:, 0, :]
        o_ref[...] = pltpu.roll(pad_ref[...], _by_chip(chip, SHIFTS), 0).astype(BF16)

    return pl.pallas_call(
        body, name="align_shard", grid=(d // cols,),
        in_specs=[pl.BlockSpec((r, 1, cols), lambda i: (0, 0, i))],
        out_specs=pl.BlockSpec((ALIGNED_W, cols), lambda i: (0, i)),
        out_shape=jax.ShapeDtypeStruct((ALIGNED_W, d), BF16),
        scratch_shapes=[pltpu.VMEM((ALIGNED_W, cols), F32)],
        compiler_params=_params(("parallel",)),
    )(wt)


def _rms_in(x, w):
    n, d = x.shape
    tr = min(256, n)

    def body(x_ref, w_ref, h_ref):
        xv = x_ref[...]
        r = lax.rsqrt(jnp.mean(xv * xv, axis=-1, keepdims=True) + EPS)
        h_ref[...] = (xv * r * w_ref[...]).astype(BF16)

    return pl.pallas_call(
        body, name="rms_in", grid=(n // tr,),
        in_specs=[pl.BlockSpec((tr, d), lambda i: (i, 0)), pl.BlockSpec((1, d), lambda i: (0, 0))],
        out_specs=pl.BlockSpec((tr, d), lambda i: (i, 0)),
        out_shape=jax.ShapeDtypeStruct((n, d), BF16),
        compiler_params=_params(("parallel",)),
    )(x, w)


def _conv_silu(p, w_ref, taps):
    c = None
    for j in range(taps):
        t = _shift_down(p, taps - 1 - j) * w_ref[j:j + 1, :]
        c = t if c is None else c + t
    return c


def _prep_qkv(proj, cw):
    n = proj.shape[0]

    def body(p3, wq, wk, wv, q_ref, k_ref, v_ref):
        for kind, (w_ref, o_ref) in enumerate(((wq, q_ref), (wk, k_ref), (wv, v_ref))):
            c = _conv_silu(p3[:, kind * DH:(kind + 1) * DH], w_ref, 4)
            a = c * _sigmoid(c)
            if kind < 2:
                r = lax.rsqrt(jnp.sum(a * a, axis=-1, keepdims=True) + EPS)
                a = a * (r * (DH ** -0.5 if kind == 0 else 1.0))
            o_ref[...] = a

    col = pl.BlockSpec((n, DH), lambda h: (0, h))
    wcol = lambda base: pl.BlockSpec((4, DH), lambda h: (0, base + h))
    out = jax.ShapeDtypeStruct((n, GW), F32)
    return pl.pallas_call(
        body, name="prep_qkv", grid=(HEADS,),
        in_specs=[pl.BlockSpec((n, 3 * DH), lambda h: (0, h)), wcol(QB), wcol(KB), wcol(VB)],
        out_specs=[col] * 3, out_shape=[out] * 3,
        compiler_params=_params(("parallel",), 40 * 2**20),
    )(proj, cw, cw, cw)


def _prep_qkv_bwd(proj, cw, dq, dk, dv, dproj):
    n = proj.shape[0]

    def body(p3, wq, wk, wv, dq_ref, dk_ref, dv_ref, _, o3, gq, gk, gv):
        for kind, (w_ref, d_ref, g_ref) in enumerate(((wq, dq_ref, gq), (wk, dk_ref, gk), (wv, dv_ref, gv))):
            p = p3[:, kind * DH:(kind + 1) * DH]
            shifted = [_shift_down(p, 3 - j) for j in range(4)]
            c = shifted[0] * w_ref[0:1, :]
            for j in range(1, 4):
                c = c + shifted[j] * w_ref[j:j + 1, :]
            s = _sigmoid(c)
            a = c * s
            d = d_ref[...]
            if kind < 2:
                r = lax.rsqrt(jnp.sum(a * a, axis=-1, keepdims=True) + EPS)
                sc = DH ** -0.5 if kind == 0 else 1.0
                d = (sc * r) * (d - a * ((r * r) * jnp.sum(d * a, axis=-1, keepdims=True)))
            dc = d * _dsilu(c, s)
            dp = None
            for j in range(4):
                g_ref[j:j + 1, :] = jnp.sum(dc * shifted[j], axis=0, keepdims=True)
                t = _shift_up(dc, 3 - j) * w_ref[j:j + 1, :]
                dp = t if dp is None else dp + t
            o3[:, kind * DH:(kind + 1) * DH] = dp.astype(BF16)

    col = pl.BlockSpec((n, DH), lambda h: (0, h))
    wcol = lambda base: pl.BlockSpec((4, DH), lambda h: (0, base + h))
    p3spec = pl.BlockSpec((n, 3 * DH), lambda h: (0, h))
    return pl.pallas_call(
        body, name="prep_qkv_bwd", grid=(HEADS,),
        in_specs=[p3spec, wcol(QB), wcol(KB), wcol(VB), col, col, col, ANY],
        out_specs=[p3spec] + [wcol(0)] * 3,
        out_shape=[jax.ShapeDtypeStruct(dproj.shape, BF16)] + [jax.ShapeDtypeStruct((4, GW), F32)] * 3,
        input_output_aliases={7: 0},
        compiler_params=_params(("parallel",), 48 * 2**20),
    )(proj, cw, cw, cw, dq, dk, dv, dproj)


CPB = 8
SCAN_CPS = 4


def _tri(lower, rows):
    i = lax.broadcasted_iota(jnp.int32, (rows, rows), 0)
    j = lax.broadcasted_iota(jnp.int32, (rows, rows), 1)
    return jnp.where((i // CH == j // CH) & ((i >= j) if lower else (j >= i)), 1.0, 0.0)


def _lane(shape):
    return lax.broadcasted_iota(jnp.int32, shape, 1)


def _prep_bg(proj, ad):
    n = proj.shape[0]
    nch = n // CH
    cpb = CPB if nch % CPB == 0 else 1
    rows = cpb * CH

    def body(p_ref, ad_ref, bg_ref, bgt_ref):
        p = p_ref[...]
        lane = _lane(p.shape)
        beta = _sigmoid(p)
        xa = p + ad_ref[1:2, :]
        sp = jnp.maximum(xa, 0.0) + jnp.log(1.0 + jnp.exp(-jnp.abs(xa)))
        g = pltpu.roll(-jnp.exp(ad_ref[0:1, :]) * sp, DH - A_LANE + HEADS, 1)
        gc = _dot(_tri(True, rows), g, NN, P_CUM)
        bg = jnp.where(lane < HEADS, beta, jnp.where(lane < 2 * HEADS, gc, 0.0))
        bg_ref[...] = bg
        for ci in range(cpb):
            bgt_ref[ci] = bg[ci * CH:(ci + 1) * CH, :].T

    return pl.pallas_call(
        body, name="prep_bg", grid=(nch // cpb,),
        in_specs=[pl.BlockSpec((rows, DH), lambda i: (i, BAB)), pl.BlockSpec((2, DH), lambda i: (0, 0))],
        out_specs=[pl.BlockSpec((rows, DH), lambda i: (i, 0)), pl.BlockSpec((cpb, DH, CH), lambda i: (i, 0, 0))],
        out_shape=[jax.ShapeDtypeStruct((n, DH), F32), jax.ShapeDtypeStruct((nch, DH, CH), F32)],
        compiler_params=_params(("parallel",)),
    )(*_in_hbm(proj, ad))


def _prep_bg_bwd(proj, ad, dbg, dproj):
    n = proj.shape[0]
    nch = n // CH
    cpb = CPB if nch % CPB == 0 else 1
    rows = cpb * CH

    def body(p_ref, ad_ref, d_ref, _, o_ref, ga_ref, gd_ref):
        p = p_ref[...]
        d = d_ref[...]
        lane = _lane(p.shape)
        beta = _sigmoid(p)
        xa = p + ad_ref[1:2, :]
        sp = jnp.maximum(xa, 0.0) + jnp.log(1.0 + jnp.exp(-jnp.abs(xa)))
        na = -jnp.exp(ad_ref[0:1, :])
        dg = pltpu.roll(_dot(_tri(False, rows), d, NN, P_CUM), A_LANE - HEADS, 1)
        da = dg * na * _sigmoid(xa)
        is_g = lane >= A_LANE
        o_ref[...] = jnp.where(lane < HEADS, d * beta * (1.0 - beta), jnp.where(is_g, da, 0.0)).astype(BF16)
        ga = jnp.sum(jnp.where(is_g, dg * na * sp, 0.0), axis=0, keepdims=True)
        gd = jnp.sum(jnp.where(is_g, da, 0.0), axis=0, keepdims=True)

        @pl.when(pl.program_id(0) == 0)
        def _():
            ga_ref[...] = jnp.zeros_like(ga_ref)
            gd_ref[...] = jnp.zeros_like(gd_ref)

        ga_ref[...] += ga
        gd_ref[...] += gd

    one = pl.BlockSpec((1, DH), lambda i: (0, 0))
    return pl.pallas_call(
        body, name="prep_bg_bwd", grid=(nch // cpb,),
        in_specs=[pl.BlockSpec((rows, DH), lambda i: (i, BAB)), pl.BlockSpec((2, DH), lambda i: (0, 0)),
                  pl.BlockSpec((rows, DH), lambda i: (i, 0)), ANY],
        out_specs=[pl.BlockSpec((rows, DH), lambda i: (i, BAB)), one, one],
        out_shape=[jax.ShapeDtypeStruct(dproj.shape, BF16), jax.ShapeDtypeStruct((1, DH), F32),
                   jax.ShapeDtypeStruct((1, DH), F32)],
        input_output_aliases={3: 0},
        compiler_params=_params(("arbitrary",)),
    )(proj, ad, dbg, dproj)


def _gdn_out(o, proj, wg):
    n = o.shape[0]

    def body(o_ref, z_ref, w_ref, y_ref):
        ov, z = o_ref[...], z_ref[...]
        r = lax.rsqrt(jnp.mean(ov * ov, axis=-1, keepdims=True) + EPS)
        y_ref[...] = (ov * r * w_ref[...] * (z * _sigmoid(z))).astype(BF16)

    return pl.pallas_call(
        body, name="gdn_out", grid=(HEADS,),
        in_specs=[pl.BlockSpec((n, DH), lambda h: (0, h)), pl.BlockSpec((n, DH), lambda h: (0, ZB + h)),
                  pl.BlockSpec((1, DH), lambda h: (0, 0))],
        out_specs=pl.BlockSpec((n, DH), lambda h: (0, h)),
        out_shape=jax.ShapeDtypeStruct((n, 2 * GW), BF16),
        compiler_params=_params(("parallel",)),
    )(o, proj, wg)


def _gdn_out_bwd(o, proj, wg, dout_b, w_out):
    n = o.shape[0]
    d_model = dout_b.shape[1]

    def body(o_ref, z_ref, w_ref, g_ref, wo_ref, do_ref, dz_ref, gw_ref):
        ov, z, w = o_ref[...], z_ref[...], w_ref[...]
        d = _dot(g_ref[...], wo_ref[...], NT)
        r = lax.rsqrt(jnp.mean(ov * ov, axis=-1, keepdims=True) + EPS)
        nrm = ov * r
        s = _sigmoid(z)
        dz_ref[...] = (d * (nrm * w) * _dsilu(z, s)).astype(BF16)
        dn_w = d * (z * s)
        gw = jnp.sum(dn_w * nrm, axis=0, keepdims=True)
        dn = dn_w * w
        do_ref[...] = r * (dn - nrm * jnp.mean(dn * nrm, axis=-1, keepdims=True))

        @pl.when(pl.program_id(0) == 0)
        def _():
            gw_ref[...] = jnp.zeros_like(gw_ref)

        gw_ref[...] += gw

    return pl.pallas_call(
        body, name="gdn_out_bwd", grid=(HEADS,),
        in_specs=[pl.BlockSpec((n, DH), lambda h: (0, h)), pl.BlockSpec((n, DH), lambda h: (0, ZB + h)),
                  pl.BlockSpec((1, DH), lambda h: (0, 0)), pl.BlockSpec((n, d_model), lambda h: (0, 0)),
                  pl.BlockSpec((DH, d_model), lambda h: (h, 0))],
        out_specs=[pl.BlockSpec((n, DH), lambda h: (0, h)), pl.BlockSpec((n, DH), lambda h: (0, ZB + h)),
                   pl.BlockSpec((1, DH), lambda h: (0, 0))],
        out_shape=[jax.ShapeDtypeStruct((n, GW), F32), jax.ShapeDtypeStruct((n, GW_COLS), BF16),
                   jax.ShapeDtypeStruct((1, DH), F32)],
        compiler_params=_params(("arbitrary",), 40 * 2**20),
    )(o, proj, wg, dout_b, w_out)


def _conv_branch(proj, w3, b, mix):
    n = proj.shape[0]

    def body(p4, w_ref, b_ref, _, y_ref):
        u = p4[:, DH:2 * DH] * p4[:, 2 * DH:3 * DH]
        cc = _conv_silu(u, w_ref, 3) + b_ref[...]
        z = p4[:, 3 * DH:4 * DH]
        y_ref[...] = (p4[:, 0:DH] * cc * (z * _sigmoid(z))).astype(BF16)

    return pl.pallas_call(
        body, name="conv_branch", grid=(HEADS,),
        in_specs=[pl.BlockSpec((n, 4 * DH), lambda h: (0, h)), pl.BlockSpec((3, DH), lambda h: (0, h)),
                  pl.BlockSpec((1, DH), lambda h: (0, h)), ANY],
        out_specs=pl.BlockSpec((n, DH), lambda h: (0, HEADS + h)),
        out_shape=jax.ShapeDtypeStruct(mix.shape, BF16),
        input_output_aliases={3: 0},
        compiler_params=_params(("parallel",), 40 * 2**20),
    )(*_in_hbm(proj, w3, b, mix))


def _conv_branch_bwd(proj, w3, b, dout_b, w_out):
    n = proj.shape[0]
    d_model = dout_b.shape[1]

    def body(p4, w_ref, b_ref, g_ref, wo_ref, o4, gw_ref, gbias_ref):
        gb, gcv, hc, z = p4[:, 0:DH], p4[:, DH:2 * DH], p4[:, 2 * DH:3 * DH], p4[:, 3 * DH:4 * DH]
        d = _dot(g_ref[...], wo_ref[...], NT)
        dgb, dgc, dhc, dzc = (o4.at[:, kk * DH:(kk + 1) * DH] for kk in range(4))
        u = gcv * hc
        cc = _conv_silu(u, w_ref, 3) + b_ref[...]
        s = _sigmoid(z)
        dzc[...] = (d * (gb * cc) * _dsilu(z, s)).astype(BF16)
        dp = d * (z * s)
        dgb[...] = (dp * cc).astype(BF16)
        dcc = dp * gb
        gbias_ref[...] = jnp.sum(dcc, axis=0, keepdims=True)
        du = None
        for j in range(3):
            gw_ref[j:j + 1, :] = jnp.sum(dcc * _shift_down(u, 2 - j), axis=0, keepdims=True)
            t = _shift_up(dcc, 2 - j) * w_ref[j:j + 1, :]
            du = t if du is None else du + t
        dgc[...] = (du * hc).astype(BF16)
        dhc[...] = (du * gcv).astype(BF16)

    p4spec = pl.BlockSpec((n, 4 * DH), lambda h: (0, h))
    return pl.pallas_call(
        body, name="conv_branch_bwd", grid=(HEADS,),
        in_specs=[p4spec, pl.BlockSpec((3, DH), lambda h: (0, h)), pl.BlockSpec((1, DH), lambda h: (0, h)),
                  pl.BlockSpec((n, d_model), lambda h: (0, 0)), pl.BlockSpec((DH, d_model), lambda h: (HEADS + h, 0))],
        out_specs=[p4spec, pl.BlockSpec((3, DH), lambda h: (0, h)), pl.BlockSpec((1, DH), lambda h: (0, h))],
        out_shape=[jax.ShapeDtypeStruct((n, CW_COLS), BF16), jax.ShapeDtypeStruct((3, GW), F32),
                   jax.ShapeDtypeStruct((1, GW), F32)],
        compiler_params=_params(("parallel",), 52 * 2**20),
    )(proj, w3, b, dout_b, w_out)


def _out_loss(mix, w_out, x, tgt, wf):
    n, d = x.shape
    kdim = mix.shape[1]
    tr = min(256, n)

    def body(m_ref, wo_ref, x_ref, t_ref, w_ref, do_ref, dob_ref, gw_ref, loss_ref):
        ov = _dot(m_ref[...], wo_ref[...], NN) + x_ref[...]
        w = w_ref[...]
        r = lax.rsqrt(jnp.mean(ov * ov, axis=-1, keepdims=True) + EPS)
        nrm = ov * r
        e = nrm * w - t_ref[...]
        dy = e * (1.0 / d)
        dn = dy * w
        dout = r * (dn - nrm * jnp.mean(dn * nrm, axis=-1, keepdims=True))
        do_ref[...] = dout
        dob_ref[...] = dout.astype(BF16)

        @pl.when(pl.program_id(0) == 0)
        def _():
            gw_ref[...] = jnp.zeros_like(gw_ref)
            loss_ref[...] = jnp.zeros_like(loss_ref)

        gw_ref[...] += jnp.sum(dy * nrm, axis=0, keepdims=True)
        loss_ref[...] += (0.5 / d) * jnp.sum(jnp.sum(e * e, axis=-1, keepdims=True), axis=0, keepdims=True)

    row = pl.BlockSpec((tr, d), lambda i: (i, 0))
    return pl.pallas_call(
        body, name="out_loss", grid=(n // tr,),
        in_specs=[pl.BlockSpec((tr, kdim), lambda i: (i, 0)), pl.BlockSpec((kdim, d), lambda i: (0, 0)), row, row,
                  pl.BlockSpec((1, d), lambda i: (0, 0))],
        out_specs=[row, row, pl.BlockSpec((1, d), lambda i: (0, 0)), pl.BlockSpec((1, 1), lambda i: (0, 0))],
        out_shape=[jax.ShapeDtypeStruct((n, d), F32), jax.ShapeDtypeStruct((n, d), BF16),
                   jax.ShapeDtypeStruct((1, d), F32), jax.ShapeDtypeStruct((1, 1), F32)],
        compiler_params=_params(("arbitrary",), 40 * 2**20),
    )(mix, w_out, x, tgt, wf)


def _dh_rms_bwd(dproj, w_t, dh0, x, w, dout, tk):
    n, d = x.shape
    kdim = dproj.shape[1]
    tm = min(512, n)
    tk = min(tk, kdim)
    nk = kdim // tk

    def body(a_ref, b_ref, dh0_ref, x_ref, w_ref, do_ref, dx_ref, gw_ref, acc):
        i, kk = pl.program_id(0), pl.program_id(1)
        part = _dot(a_ref[...], b_ref[...], NN)

        @pl.when(kk == 0)
        def _():
            acc[...] = part + dh0_ref[...]

        @pl.when(kk > 0)
        def _():
            acc[...] += part

        @pl.when((i == 0) & (kk == 0))
        def _():
            gw_ref[...] = jnp.zeros_like(gw_ref)

        @pl.when(kk == nk - 1)
        def _():
            xv, dhv = x_ref[...], acc[...]
            r = lax.rsqrt(jnp.mean(xv * xv, axis=-1, keepdims=True) + EPS)
            xn = xv * r
            dxn = dhv * w_ref[...]
            dx_ref[...] = r * (dxn - xn * jnp.mean(dxn * xn, axis=-1, keepdims=True)) + do_ref[...]
            gw_ref[...] += jnp.sum(dhv * xn, axis=0, keepdims=True)

    row = pl.BlockSpec((tm, d), lambda i, kk: (i, 0))
    one = pl.BlockSpec((1, d), lambda i, kk: (0, 0))
    return pl.pallas_call(
        body, name="dh_rms_bwd", grid=(n // tm, nk),
        in_specs=[pl.BlockSpec((tm, tk), lambda i, kk: (i, kk)), pl.BlockSpec((tk, d), lambda i, kk: (kk, 0)),
                  row, row, one, row],
        out_specs=[row, one],
        out_shape=[jax.ShapeDtypeStruct((n, d), F32), jax.ShapeDtypeStruct((1, d), F32)],
        scratch_shapes=[pltpu.VMEM((tm, d), F32)],
        compiler_params=_params(("arbitrary", "arbitrary"), 48 * 2**20),
    )(dproj, w_t, dh0, x, w, dout)


def _ij():
    i = lax.broadcasted_iota(jnp.int32, (CH, CH), 0)
    j = lax.broadcasted_iota(jnp.int32, (CH, CH), 1)
    return i, j


def _unit_lower_inverse(mats):
    i, j = _ij()
    eye = jnp.where(i == j, 1.0, 0.0)
    same16 = (i // 16) == (j // 16)
    same32 = (i // 32) == (j // 32)
    mm = lambda xs, ys: [_dot(x, y, NN, P_INV) for x, y in zip(xs, ys)]
    n1 = [jnp.where(same16, -a, 0.0) for a in mats]
    n2 = mm(n1, n1)
    n4 = mm(n2, n2)
    n8 = mm(n4, n4)
    t = [eye + x1 + x2 + x3 for x1, x2, x3 in zip(n1, n2, mm(n1, n2))]
    t = [x + y for x, y in zip(t, mm(t, n4))]
    t = [x + y for x, y in zip(t, mm(t, n8))]
    a1 = [jnp.where(same32 & jnp.logical_not(same16), a, 0.0) for a in mats]
    t = [x - y for x, y in zip(t, mm(t, mm(a1, t)))]
    a2 = [jnp.where(same32, 0.0, a) for a in mats]
    t = [x - y for x, y in zip(t, mm(t, mm(a2, t)))]
    return t


def _head_vectors(bg, bgt, h):
    bcol = bg[:, h:h + 1]
    gcol = bg[:, HEADS + h:HEADS + h + 1]
    grow = bgt[HEADS + h:HEADS + h + 1, :]
    return bcol, gcol, grow


def _decay(gcol, grow):
    i, j = _ij()
    return jnp.where(i >= j, jnp.exp(jnp.where(i >= j, gcol - grow, 0.0)), 0.0)


def _gdn_intra(q, k, v, bg, bgt):
    n = q.shape[0]
    nch = n // CH
    cps = 4 if nch % 4 == 0 else 1

    def body(q_ref, k_ref, v_ref, bg_ref, bgt_ref, u_ref, w_ref, p_ref, t_ref):
        i, j = _ij()
        items = [(ci, h) for ci in range(cps) for h in range(HEADS)]
        at = lambda ref, ci, h: ref.at[ci * CH:(ci + 1) * CH, h * DH:(h + 1) * DH]
        bgs = [bg_ref[ci * CH:(ci + 1) * CH, :] for ci in range(cps)]
        ks = [at(k_ref, ci, h)[...] for ci, h in items]
        vecs = [_head_vectors(bgs[ci], bgt_ref[ci], h) for ci, h in items]
        decs = [_decay(gcol, grow) for _, gcol, grow in vecs]
        kks = [_dot(kh, kh, NT, P_GRAM) for kh in ks]
        qks = [_dot(at(q_ref, ci, h)[...], kh, NT, P_GRAM) for (ci, h), kh in zip(items, ks)]
        ts = _unit_lower_inverse([jnp.where(i > j, bcol * kk * dec, 0.0)
                                  for (bcol, _, _), kk, dec in zip(vecs, kks, decs)])
        us = [_dot(t, at(v_ref, ci, h)[...] * bcol, NN, P_SOL) for t, (ci, h), (bcol, _, _) in zip(ts, items, vecs)]
        ws = [_dot(t, kh * (bcol * jnp.exp(gcol)), NN, P_SOL) for t, kh, (bcol, gcol, _) in zip(ts, ks, vecs)]
        for n_, (ci, h) in enumerate(items):
            p_ref[ci, h] = qks[n_] * decs[n_]
            t_ref[ci, h] = ts[n_]
            at(u_ref, ci, h)[...] = us[n_]
            at(w_ref, ci, h)[...] = ws[n_]

    row = pl.BlockSpec((cps * CH, GW), lambda c: (c, 0))
    sq = pl.BlockSpec((cps, HEADS, CH, CH), lambda c: (c, 0, 0, 0))
    big = jax.ShapeDtypeStruct((n, GW), F32)
    sqs = jax.ShapeDtypeStruct((nch, HEADS, CH, CH), F32)
    return pl.pallas_call(
        body, name="gdn_intra", grid=(nch // cps,),
        in_specs=[row, row, row, pl.BlockSpec((cps * CH, DH), lambda c: (c, 0)),
                  pl.BlockSpec((cps, DH, CH), lambda c: (c, 0, 0))],
        out_specs=[row, row, sq, sq], out_shape=[big, big, sqs, sqs],
        compiler_params=_params(("parallel",)),
    )(q, k, v, bg, bgt)


def _gdn_scan(q, k, bg, u, w, p):
    n = q.shape[0]
    nch = n // CH
    cps = SCAN_CPS if nch % SCAN_CPS == 0 else 1

    def body(q_ref, k_ref, bg_ref, u_ref, w_ref, p_ref, o_ref, vn_ref, s_out, s_scr):
        @pl.when(pl.program_id(0) == 0)
        def _():
            s_scr[...] = jnp.zeros_like(s_scr)

        hs = range(HEADS)
        sls = [slice(h * DH, (h + 1) * DH) for h in hs]
        ss = [s_scr[h] for h in hs]
        for ci in range(cps):
            rs = slice(ci * CH, (ci + 1) * CH)
            bg = bg_ref[rs, :]
            gcols = [bg[:, HEADS + h:HEADS + h + 1] for h in hs]
            glasts = [g[CH - 1:CH, :] for g in gcols]
            wss = [_dot(w_ref[rs, sl], s, NN, P_SCAN) for sl, s in zip(sls, ss)]
            oqs = [_dot(q_ref[rs, sl] * jnp.exp(g), s, NN, P_SCAN) for sl, s, g in zip(sls, ss, gcols)]
            vns = [u_ref[rs, sl] - x for sl, x in zip(sls, wss)]
            ops = [_dot(p_ref[ci, h], vn, NN, P_SCAN) for h, vn in zip(hs, vns)]
            sns = [_dot(k_ref[rs, sl] * jnp.exp(gl - g), vn, TN, P_SCAN)
                   for sl, gl, g, vn in zip(sls, glasts, gcols, vns)]
            for h, sl in enumerate(sls):
                s_out[ci, :, sl] = ss[h]
                vn_ref[rs, sl] = vns[h]
                o_ref[rs, sl] = oqs[h] + ops[h]
            ss = [s * jnp.exp(gl) + sn for s, gl, sn in zip(ss, glasts, sns)]
        for h in hs:
            s_scr[h] = ss[h]

    row = pl.BlockSpec((cps * CH, GW), lambda c: (c, 0))
    big = jax.ShapeDtypeStruct((n, GW), F32)
    return pl.pallas_call(
        body, name="gdn_scan", grid=(nch // cps,),
        in_specs=[row, row, pl.BlockSpec((cps * CH, DH), lambda c: (c, 0)), row, row,
                  pl.BlockSpec((cps, HEADS, CH, CH), lambda c: (c, 0, 0, 0))],
        out_specs=[row, row, pl.BlockSpec((cps, DH, GW), lambda c: (c, 0, 0))],
        out_shape=[big, big, jax.ShapeDtypeStruct((nch, DH, GW), F32)],
        scratch_shapes=[pltpu.VMEM((HEADS, DH, DH), F32)],
        compiler_params=_params(("arbitrary",)),
    )(q, k, bg, u, w, p)


def _gdn_scan_bwd(q, k, bg, w, p, vn, s_in, do):
    n = q.shape[0]
    nch = n // CH
    cps = SCAN_CPS if nch % SCAN_CPS == 0 else 1
    rev = lambda c: nch // cps - 1 - c

    def body(q_ref, k_ref, bg_ref, w_ref, p_ref, vn_ref, s_ref, do_ref,
             dqg_ref, dp_ref, du_ref, dw_ref, dks_ref, dgam_ref, ds_scr):
        @pl.when(pl.program_id(0) == 0)
        def _():
            ds_scr[...] = jnp.zeros_like(ds_scr)

        lane = _lane((1, DH))
        hs = range(HEADS)
        sls = [slice(h * DH, (h + 1) * DH) for h in hs]
        dss = [ds_scr[h] for h in hs]
        for ci in reversed(range(cps)):
            rs = slice(ci * CH, (ci + 1) * CH)
            bg = bg_ref[rs, :]
            gcols = [bg[:, HEADS + h:HEADS + h + 1] for h in hs]
            glasts = [g[CH - 1:CH, :] for g in gcols]
            ss = [s_ref[ci, :, sl] for sl in sls]
            dos = [do_ref[rs, sl] for sl in sls]
            vnl = [vn_ref[rs, sl] for sl in sls]
            dqgs = [_dot(d, s, NT, P_SCANB) for d, s in zip(dos, ss)]
            dps = [_dot(d, vn, NT, P_SCANB) for d, vn in zip(dos, vnl)]
            dvn1 = [_dot(p_ref[ci, h], d, TN, P_SCANB) for h, d in zip(hs, dos)]
            dvn2 = [_dot(k_ref[rs, sl] * jnp.exp(gl - g), ds, NN, P_SCANB)
                    for sl, gl, g, ds in zip(sls, glasts, gcols, dss)]
            dkss = [_dot(vn, ds, NT, P_SCANB) for vn, ds in zip(vnl, dss)]
            dsq = [_dot(q_ref[rs, sl] * jnp.exp(g), d, TN, P_SCANB) for sl, g, d in zip(sls, gcols, dos)]
            dvns = [a + b for a, b in zip(dvn1, dvn2)]
            dws = [_dot(dvn, s, NT, P_SCANB) for dvn, s in zip(dvns, ss)]
            dsw = [_dot(w_ref[rs, sl], dvn, TN, P_SCANB) for sl, dvn in zip(sls, dvns)]
            dgam = jnp.zeros((1, DH), F32)
            for h, sl in enumerate(sls):
                dqg_ref[rs, sl] = dqgs[h]
                dp_ref[ci, h] = dps[h]
                du_ref[rs, sl] = dvns[h]
                dw_ref[rs, sl] = -dws[h]
                dks_ref[rs, sl] = dkss[h]
                tot = jnp.sum(jnp.sum(dss[h] * ss[h], axis=-1, keepdims=True), axis=0, keepdims=True)
                dgam = dgam + jnp.where(lane == h, tot, 0.0)
            dgam_ref[ci] = jnp.broadcast_to(dgam, (8, DH))
            dss = [ds * jnp.exp(gl) + a - b for ds, gl, a, b in zip(dss, glasts, dsq, dsw)]
        for h in hs:
            ds_scr[h] = dss[h]

    row = pl.BlockSpec((cps * CH, GW), lambda c: (rev(c), 0))
    sq = pl.BlockSpec((cps, HEADS, CH, CH), lambda c: (rev(c), 0, 0, 0))
    big = jax.ShapeDtypeStruct((n, GW), F32)
    return pl.pallas_call(
        body, name="gdn_scan_bwd", grid=(nch // cps,),
        in_specs=[row, row, pl.BlockSpec((cps * CH, DH), lambda c: (rev(c), 0)), row, sq, row,
                  pl.BlockSpec((cps, DH, GW), lambda c: (rev(c), 0, 0)), row],
        out_specs=[row, sq, row, row, row, pl.BlockSpec((cps, 8, DH), lambda c: (rev(c), 0, 0))],
        out_shape=[big, jax.ShapeDtypeStruct((nch, HEADS, CH, CH), F32), big, big, big,
                   jax.ShapeDtypeStruct((nch, 8, DH), F32)],
        scratch_shapes=[pltpu.VMEM((HEADS, DH, DH), F32)],
        compiler_params=_params(("arbitrary",)),
    )(q, k, bg, w, p, vn, s_in, do)


def _gdn_intra_bwd(q, k, v, bg, bgt, t, u, w, p, dqg, dp, du, dw, dks, dgam):
    n = q.shape[0]
    nch = n // CH
    cps = 2 if nch % 2 == 0 else 1

    def body(q_ref, k_ref, v_ref, bg_ref, bgt_ref, t_ref, u_ref, w_ref, p_ref,
             dqg_ref, dp_ref, du_ref, dw_ref, dks_ref, dgam_ref, dq_ref, dk_ref, dv_ref, dbg_ref):
        i, j = _ij()
        rows1 = lax.broadcasted_iota(jnp.int32, (CH, 1), 0)
        lane = _lane((CH, DH))
        rsum = lambda x: jnp.sum(x, axis=-1, keepdims=True)
        items = [(ci, h) for ci in range(cps) for h in range(HEADS)]
        at = lambda ref, it: ref.at[it[0] * CH:(it[0] + 1) * CH, it[1] * DH:(it[1] + 1) * DH]
        ld = lambda ref: [at(ref, it)[...] for it in items]
        bgs = [bg_ref[ci * CH:(ci + 1) * CH, :] for ci in range(cps)]
        qs, ks = ld(q_ref), ld(k_ref)
        vecs = [_head_vectors(bgs[ci], bgt_ref[ci], h) for ci, h in items]
        decs = [_decay(gcol, grow) for _, gcol, grow in vecs]
        ths = [t_ref[ci, h] for ci, h in items]
        drus = [_dot(th, x_, TN, P_BWD) for th, x_ in zip(ths, ld(du_ref))]
        drws = [_dot(th, x_, TN, P_BWD) for th, x_ in zip(ths, ld(dw_ref))]
        kks = [_dot(kh, kh, NT, P_GRAM) for kh in ks]
        da1 = [_dot(dru, x_, NT, P_BWD) for dru, x_ in zip(drus, ld(u_ref))]
        da2 = [_dot(drw, x_, NT, P_BWD) for drw, x_ in zip(drws, ld(w_ref))]
        das = [jnp.where(i > j, -(x_ + y_), 0.0) for x_, y_ in zip(da1, da2)]
        dkks = [da * bcol * dec for da, (bcol, _, _), dec in zip(das, vecs, decs)]
        dps = [dp_ref[ci, h] for ci, h in items]
        dqks = [dp_ * dec for dp_, dec in zip(dps, decs)]
        dq_ps = [_dot(dqk, kh, NN, P_BWD) for dqk, kh in zip(dqks, ks)]
        dk_ps = [_dot(dqk, qh, TN, P_BWD) for dqk, qh in zip(dqks, qs)]
        dk_as = [_dot(dkk, kh, NN, P_BWD) for dkk, kh in zip(dkks, ks)]
        dk_bs = [_dot(dkk, kh, TN, P_BWD) for dkk, kh in zip(dkks, ks)]
        bcols = [vc[0] for vc in vecs]
        gcols = [vc[1] for vc in vecs]
        gams = [jnp.exp(g) for g in gcols]
        glasts = [g[CH - 1:CH, :] for g in gcols]
        es = [jnp.exp(gl - g) for gl, g in zip(glasts, gcols)]
        kgs = [kh * gam for kh, gam in zip(ks, gams)]
        dqgs, dkss = ld(dqg_ref), ld(dks_ref)
        wks = [drw * kg for drw, kg in zip(drws, kgs)]
        kss = [dk_ * (kh * e) for dk_, kh, e in zip(dkss, ks, es)]
        r_beta = [rsum(dru * x_ + wk) for dru, x_, wk in zip(drus, ld(v_ref), wks)]
        r_ak = [rsum(da * kk * dec) for da, kk, dec in zip(das, kks, decs)]
        r_gc = [rsum(wk * bcol + dqg * (qh * gam) - ks_)
                for wk, bcol, dqg, qh, gam, ks_ in zip(wks, bcols, dqgs, qs, gams, kss)]
        tk_tot = [jnp.sum(jnp.sum(ks_, axis=0, keepdims=True), axis=-1, keepdims=True) for ks_ in kss]
        mdecs = [da * (bcol * kk * dec) + dp_ * p_ref[ci, h]
                 for (ci, h), da, bcol, kk, dec, dp_ in zip(items, das, bcols, kks, decs, dps)]
        r_md = [rsum(m) for m in mdecs]
        c_md = [rsum(jnp.where(i == j, jnp.sum(m, axis=0, keepdims=True), 0.0)) for m in mdecs]
        dbgs = [jnp.zeros((CH, DH), F32) for _ in range(cps)]
        for n_, (ci, h) in enumerate(items):
            at(dv_ref, (ci, h))[...] = bcols[n_] * drus[n_]
            at(dq_ref, (ci, h))[...] = gams[n_] * dqgs[n_] + dq_ps[n_]
            at(dk_ref, (ci, h))[...] = ((bcols[n_] * gams[n_]) * drws[n_] + dk_ps[n_] + dk_as[n_] + dk_bs[n_]
                                        + dkss[n_] * es[n_])
            dbeta = r_beta[n_] + r_ak[n_]
            dglast = tk_tot[n_] + dgam_ref[ci, 0:1, h:h + 1] * jnp.exp(glasts[n_])
            dgc = r_gc[n_] + r_md[n_] - c_md[n_] + jnp.where(rows1 == CH - 1, dglast, 0.0)
            dbgs[ci] = dbgs[ci] + jnp.where(lane == h, dbeta, 0.0) + jnp.where(lane == HEADS + h, dgc, 0.0)
        for ci in range(cps):
            dbg_ref[ci * CH:(ci + 1) * CH, :] = dbgs[ci]

    row = pl.BlockSpec((cps * CH, GW), lambda c: (c, 0))
    sq = pl.BlockSpec((cps, HEADS, CH, CH), lambda c: (c, 0, 0, 0))
    small = pl.BlockSpec((cps * CH, DH), lambda c: (c, 0))
    big = jax.ShapeDtypeStruct((n, GW), F32)
    return pl.pallas_call(
        body, name="gdn_intra_bwd", grid=(nch // cps,),
        in_specs=[row, row, row, small, pl.BlockSpec((cps, DH, CH), lambda c: (c, 0, 0)), sq, row, row, sq,
                  row, sq, row, row, row, pl.BlockSpec((cps, 8, DH), lambda c: (c, 0, 0))],
        out_specs=[row, row, row, small],
        out_shape=[big, big, big, jax.ShapeDtypeStruct((n, DH), F32)],
        compiler_params=_params(("parallel",)),
    )(q, k, v, bg, bgt, t, u, w, p, dqg, dp, du, dw, dks, dgam)


def _local_step(x, tgt, h, w_g, cqw, late, norm_in_w, ad, gdn_norm_w, conv_b, final_norm_w,
                on_grad_c=None, on_grad_g=None, on_q=None):
    proj_g = _matmul(h, w_g, NT, F32, 512, 1408, 1024, "mm_proj_g", n=GW_COLS, b_outer=True)
    q, k, v = _prep_qkv(proj_g, cqw)
    if on_q is not None:
        q = on_q(q)
    bg, bgt = _prep_bg(proj_g, ad)
    u, w, p, t = _gdn_intra(q, k, v, bg, bgt)
    o, vn, s_in = _gdn_scan(q, k, bg, u, w, p)
    w_c, w_out, conv_w = late(o)
    proj_c = _matmul(h, w_c, NT, F32, 512, 1024, 1024, "mm_proj_c", n=CW_COLS, b_outer=True)
    mix = _conv_branch(proj_c, conv_w, conv_b, _gdn_out(o, proj_g, gdn_norm_w))
    dout, dout_b, g_fn, loss = _out_loss(mix, w_out, x, tgt, final_norm_w)

    g_wout = _matmul(mix, dout_b, TN, BF16, 512, 512, 2048, "mm_gwout")
    do, dproj_g, g_gn = _gdn_out_bwd(o, proj_g, gdn_norm_w, dout_b, w_out)
    dproj_c, g_cw, g_cb = _conv_branch_bwd(proj_c, conv_w, conv_b, dout_b, w_out)
    g_c = _matmul(dproj_c, h, TN, BF16, 1024, 512, 2048, "mm_gwin_c")
    if on_grad_c is not None:
        do = on_grad_c(g_c, g_wout, do)
    dqg, dp, du, dw, dks, dgam = _gdn_scan_bwd(q, k, bg, w, p, vn, s_in, do)
    dq, dk, dv, dbg = _gdn_intra_bwd(q, k, v, bg, bgt, t, u, w, p, dqg, dp, du, dw, dks, dgam)
    dproj_g, gq, gk, gv = _prep_qkv_bwd(proj_g, cqw, dq, dk, dv, dproj_g)
    dproj_g, g_al, g_dt = _prep_bg_bwd(proj_g, ad, dbg, dproj_g)
    g_g = _matmul(dproj_g, h, TN, BF16, 1408, 512, 2048, "mm_gwin_g")
    if on_grad_g is not None:
        dproj_g = on_grad_g(g_g, dproj_g)
    dh = _matmul(dproj_g, w_g, NN, F32, 1024, 1024, 1408, "mm_dh_g")
    gx, g_nin = _dh_rms_bwd(dproj_c, w_c, dh, x, norm_in_w, dout, 1024)
    small = dict(nin=g_nin, cb=g_cb, fn=g_fn, al=g_al, dt=g_dt, gn=g_gn, cq=(gq, gk, gv), cw=g_cw, loss=loss)
    return gx, small, (g_g, g_c, g_wout)


def _place():
    x, y, c = lax.axis_index("x"), lax.axis_index("y"), lax.axis_index("c")
    chips = [(1 - x, y), (x, 1 - y), (1 - x, 1 - y)]
    return x, y, c, chips


def _blk(ref, b):
    if isinstance(b, int):
        return ref.at[b * DH:(b + 1) * DH, :]
    return ref.at[pl.ds(pl.multiple_of(b * DH, DH), DH), :]


HBM = pl.BlockSpec(memory_space=pltpu.HBM)
SEM = pl.BlockSpec(memory_space=pltpu.SEMAPHORE)
EFFECT = pltpu.SideEffectType.DATAFLOW_SIDE_EFFECTING


def _split_start(name, issue, bufs, n_sems):
    nbuf = len(bufs)

    def body(*refs):
        issue(refs[:nbuf], refs[nbuf], refs[nbuf + 1])
        refs[-1][...] = jnp.zeros_like(refs[-1])

    out = pl.pallas_call(
        body, name=name,
        out_shape=(pltpu.SemaphoreType.DMA((n_sems,)), pltpu.SemaphoreType.DMA((n_sems,)),
                   *[pltpu.HBM(b.shape, b.dtype) for b in bufs], jax.ShapeDtypeStruct((8, DH), F32)),
        in_specs=[HBM] * nbuf,
        out_specs=(SEM, SEM, *[HBM] * nbuf, pl.BlockSpec(memory_space=pltpu.VMEM)),
        input_output_aliases={a: 2 + a for a in range(nbuf)},
        compiler_params=pltpu.CompilerParams(has_side_effects=EFFECT),
    )(*[pltpu.with_memory_space_constraint(b, pltpu.HBM) for b in bufs])
    return out[0], out[1], list(out[2:2 + nbuf]), out[-1]


def _split_wait(name, await_, send_sems, recv_sems, bufs, after):
    nbuf = len(bufs)
    after = list(after) if isinstance(after, (list, tuple)) else [after]

    def body(*refs):
        await_(refs[:nbuf], refs[nbuf], refs[nbuf + 1])

    out = pl.pallas_call(
        body, name=name,
        out_shape=tuple(pltpu.HBM(b.shape, b.dtype) for b in bufs),
        in_specs=[HBM] * nbuf + [SEM, SEM] + [ANY] * len(after), out_specs=tuple([HBM] * nbuf),
        input_output_aliases={a: a for a in range(nbuf)},
        compiler_params=pltpu.CompilerParams(has_side_effects=EFFECT),
    )(*bufs, send_sems, recv_sems, *after)
    return list(out)


def _phase_blocks(chip, phase, edges, parity=None):
    return [(b, blk) for b, (grp, blk) in enumerate(_shard_blocks(chip, edges))
            if grp == phase and (parity is None or b % 2 == parity)]


def _cols(ref, nblk):
    return ref.at[0:nblk * DH, :]


def _block_table(chip, edges, spare_g, spare_c):
    rows = []
    for s in range(4):
        sb = _shard_blocks(s, edges)
        rows.append([[blk if grp == "g" else spare_g for grp, blk in sb],
                     [blk if grp == "c" else spare_c for grp, blk in sb],
                     [int(grp == "g") for grp, _ in sb], [s] * ALIGNED_BLOCKS])
    return jnp.asarray(rows, jnp.int32)[chip]


def _place_own(a_shard, wo, cq, cw, bufs):
    d = a_shard.shape[1]
    chip = 2 * lax.axis_index("x") + lax.axis_index("y")

    def body(t_ref, a_ref, wo_ref, cq_ref, cw_ref, *refs):
        wg_ref, wc_ref, wog_ref, cqg_ref, cwg_ref = refs[5:]
        wg_ref[...] = a_ref[...]
        wc_ref[...] = a_ref[...]

        @pl.when(pl.program_id(0) == 0)
        def _():
            wog_ref[0] = wo_ref[...]
            cqg_ref[0] = cq_ref[...]
            cwg_ref[0] = cw_ref[...]

    whole = lambda s: pl.BlockSpec(s.shape, lambda b, t: (0,) * s.ndim)
    slot = lambda s: pl.BlockSpec((1,) + s.shape, lambda b, t: (t[3, 0],) + (0,) * s.ndim)
    return pl.pallas_call(
        body, name="place_own",
        grid_spec=pltpu.PrefetchScalarGridSpec(
            num_scalar_prefetch=1, grid=(ALIGNED_BLOCKS,),
            in_specs=[pl.BlockSpec((DH, d), lambda b, t: (b, 0)), whole(wo), whole(cq), whole(cw)] + [ANY] * 5,
            out_specs=[pl.BlockSpec((DH, d), lambda b, t: (t[0, b], 0)),
                       pl.BlockSpec((DH, d), lambda b, t: (t[1, b], 0)), slot(wo), slot(cq), slot(cw)]),
        out_shape=[jax.ShapeDtypeStruct(b.shape, b.dtype) for b in bufs],
        input_output_aliases={5 + a: a for a in range(5)},
        compiler_params=_params(("arbitrary",)),
    )(_block_table(chip, True, G_SPARE, C_SPARE), a_shard, wo, cq, cw, *bufs)


def _tie(x, token, name):
    def body(x_ref, t_ref, o_ref):
        del x_ref, t_ref, o_ref

    return pl.pallas_call(
        body, name=name, in_specs=[ANY, ANY], out_specs=ANY,
        out_shape=jax.ShapeDtypeStruct(x.shape, x.dtype), input_output_aliases={0: 0},
    )(x, token)


def _gather_start(phase, a_shard, w_grp, singles):
    ns = len(singles)

    def issue(refs, send_sems, recv_sems):
        a_ref, w_ref = refs[0], refs[1]
        x, y, c, chips = _place()
        mine = 2 * x + y
        for jj, (px, py) in enumerate(chips):
            to = dict(device_id=(px, py, c), device_id_type=MESH)
            for a in range(ns):
                pltpu.make_async_remote_copy(
                    src_ref=refs[2 + 2 * a], dst_ref=refs[3 + 2 * a].at[mine],
                    send_sem=send_sems.at[(1 + ns) * jj + 1 + a], recv_sem=recv_sems.at[(1 + ns) * jj + 1 + a],
                    **to).start()
        for s in range(4):
            for par in range(2):
                blocks = _phase_blocks(s, phase, True, par)
                if blocks:
                    @pl.when((mine == s) & (c == par))
                    def _():
                        for jj, (px, py) in enumerate(chips):
                            for b, blk in blocks:
                                pltpu.make_async_remote_copy(
                                    src_ref=_blk(a_ref, b), dst_ref=_blk(w_ref, blk),
                                    send_sem=send_sems.at[(1 + ns) * jj], recv_sem=recv_sems.at[(1 + ns) * jj],
                                    device_id=(px, py, c), device_id_type=MESH).start()

    bufs = [a_shard, w_grp] + [t for pair in singles for t in pair]
    return _split_start("gather_start_" + phase, issue, bufs, 3 * (1 + ns))


def _gather_wait(phase, send_sems, recv_sems, bufs, after):
    ns = (len(bufs) - 2) // 2

    def await_(refs, send_sems, recv_sems):
        a_ref, w_ref = refs[0], refs[1]
        x, y, c, chips = _place()
        mine = 2 * x + y
        for jj, (px, py) in enumerate(chips):
            to = dict(device_id=(px, py, c), device_id_type=MESH)
            peer = 2 * px + py
            for a in range(ns):
                cp = pltpu.make_async_remote_copy(
                    src_ref=refs[2 + 2 * a], dst_ref=refs[3 + 2 * a].at[mine],
                    send_sem=send_sems.at[(1 + ns) * jj + 1 + a], recv_sem=recv_sems.at[(1 + ns) * jj + 1 + a], **to)
                cp.wait_recv()
                cp.wait_send()
            for s in range(4):
                for par in range(2):
                    nblk = len(_phase_blocks(s, phase, True, par))
                    if nblk:
                        both = pltpu.make_async_remote_copy(
                            src_ref=_cols(a_ref, nblk), dst_ref=_cols(w_ref, nblk),
                            send_sem=send_sems.at[(1 + ns) * jj], recv_sem=recv_sems.at[(1 + ns) * jj], **to)

                        @pl.when((peer == s) & (c == par))
                        def _():
                            both.wait_recv()

                        @pl.when((mine == s) & (c == par))
                        def _():
                            both.wait_send()

    return _split_wait("gather_wait_" + phase, await_, send_sems, recv_sems, bufs, after)


def _sibling_forward_parts(phase):
    def each(w_ref, send_sems, recv_sems, start):
        x, y, c, chips = _place()
        to = dict(device_id=(x, y, 1 - c), device_id_type=MESH)
        for jj, (px, py) in enumerate(chips):
            peer = 2 * px + py
            for s in range(4):
                for par in range(2):
                    mine_blocks = _phase_blocks(s, phase, True, par)
                    theirs = len(_phase_blocks(s, phase, True, 1 - par))
                    if not (mine_blocks or theirs):
                        continue

                    @pl.when((peer == s) & (c == par))
                    def _():
                        if start:
                            for _, blk in mine_blocks:
                                pltpu.make_async_remote_copy(
                                    src_ref=_blk(w_ref, blk), dst_ref=_blk(w_ref, blk),
                                    send_sem=send_sems.at[jj], recv_sem=recv_sems.at[jj], **to).start()
                            return
                        if theirs:
                            pltpu.make_async_remote_copy(
                                src_ref=_cols(w_ref, theirs), dst_ref=_cols(w_ref, theirs),
                                send_sem=send_sems.at[jj], recv_sem=recv_sems.at[jj], **to).wait_recv()
                        if mine_blocks:
                            pltpu.make_async_remote_copy(
                                src_ref=_cols(w_ref, len(mine_blocks)), dst_ref=_cols(w_ref, len(mine_blocks)),
                                send_sem=send_sems.at[jj], recv_sem=recv_sems.at[jj], **to).wait_send()

    issue = lambda refs, send_sems, recv_sems: each(refs[0], send_sems, recv_sems, True)
    await_ = lambda refs, send_sems, recv_sems: each(refs[0], send_sems, recv_sems, False)
    return issue, await_


def _sibling_forward(phase, w_grp):
    issue, await_ = _sibling_forward_parts(phase)

    def body(w_in_ref, w_ref, send_sems, recv_sems):
        del w_in_ref
        issue([w_ref], send_sems, recv_sems)
        await_([w_ref], send_sems, recv_sems)

    return pl.pallas_call(
        body, name="sibling_forward_" + phase, in_specs=[ANY], out_specs=ANY,
        out_shape=jax.ShapeDtypeStruct(w_grp.shape, w_grp.dtype), input_output_aliases={0: 0},
        scratch_shapes=[pltpu.SemaphoreType.DMA((3,)), pltpu.SemaphoreType.DMA((3,))],
    )(w_grp)


def _merge_edges(w, edge0, mixed, name):
    d = w.shape[1]

    def body(e_ref, o_ref):
        o_ref[...] = e_ref[0:DH, :] + e_ref[DH:2 * DH, :]

    def to_block(i):
        r = mixed[-1]
        for kk in range(len(mixed) - 2, -1, -1):
            r = jnp.where(i == kk, mixed[kk], r)
        return r

    return pl.pallas_call(
        body, name=name, grid=(len(mixed),),
        in_specs=[pl.BlockSpec((2 * DH, d), lambda i: (edge0 // 2 + i, 0))],
        out_specs=pl.BlockSpec((DH, d), lambda i: (to_block(i), 0)),
        out_shape=jax.ShapeDtypeStruct(w.shape, w.dtype),
        input_output_aliases={0: 0},
        compiler_params=_params(("arbitrary",)),
    )(w)


def _scatter_start(phase, g_grp, land, singles, halved=False):
    ns = len(singles)

    def issue(refs, send_sems, recv_sems):
        g_ref, land_ref = refs[0], refs[1]
        x, y, c, chips = _place()
        for jj, (px, py) in enumerate(chips):
            to = dict(device_id=(px, py, c), device_id_type=MESH)
            peer = 2 * px + py
            for a in range(ns):
                pltpu.make_async_remote_copy(
                    src_ref=refs[2 + 2 * a].at[peer], dst_ref=refs[3 + 2 * a].at[jj],
                    send_sem=send_sems.at[(1 + ns) * jj + 1 + a], recv_sem=recv_sems.at[(1 + ns) * jj + 1 + a],
                    **to).start()
            for s in range(4):
                for par in ((0, 1) if halved else (None,)):
                    blocks = _phase_blocks(s, phase, False, par)
                    if blocks:
                        @pl.when((peer == s) if par is None else ((peer == s) & (c == par)))
                        def _():
                            for b, blk in blocks:
                                pltpu.make_async_remote_copy(
                                    src_ref=_blk(g_ref, blk), dst_ref=_blk(land_ref.at[jj], b),
                                    send_sem=send_sems.at[(1 + ns) * jj], recv_sem=recv_sems.at[(1 + ns) * jj],
                                    **to).start()

    bufs = [g_grp, land] + [t for pair in singles for t in pair]
    return _split_start("scatter_start_" + phase, issue, bufs, 3 * (1 + ns))


def _scatter_wait(phase, send_sems, recv_sems, bufs, after, halved=False):
    ns = (len(bufs) - 2) // 2

    def await_(refs, send_sems, recv_sems):
        g_ref, land_ref = refs[0], refs[1]
        x, y, c, chips = _place()
        mine = 2 * x + y
        for jj, (px, py) in enumerate(chips):
            to = dict(device_id=(px, py, c), device_id_type=MESH)
            peer = 2 * px + py
            for a in range(ns):
                cp = pltpu.make_async_remote_copy(
                    src_ref=refs[2 + 2 * a].at[peer], dst_ref=refs[3 + 2 * a].at[jj],
                    send_sem=send_sems.at[(1 + ns) * jj + 1 + a], recv_sem=recv_sems.at[(1 + ns) * jj + 1 + a], **to)
                cp.wait_recv()
                cp.wait_send()
            for s in range(4):
                for par in ((0, 1) if halved else (None,)):
                    nblk = len(_phase_blocks(s, phase, False, par))
                    if nblk:
                        both = pltpu.make_async_remote_copy(
                            src_ref=_cols(g_ref, nblk), dst_ref=_cols(land_ref.at[jj], nblk),
                            send_sem=send_sems.at[(1 + ns) * jj], recv_sem=recv_sems.at[(1 + ns) * jj], **to)

                        @pl.when((mine == s) if par is None else ((mine == s) & (c == par)))
                        def _():
                            both.wait_recv()

                        @pl.when((peer == s) if par is None else ((peer == s) & (c == par)))
                        def _():
                            both.wait_send()

    return _split_wait("scatter_wait_" + phase, await_, send_sems, recv_sems, bufs, after)


def _needed_blocks(phase, parity):
    return sorted({blk for s in range(4) for _, blk in _phase_blocks(s, phase, False, parity)})


def _pair_reduce(phase, g_grp):
    n, d = g_grp.shape

    def swap(g_ref, sib_ref, send_sem, recv_sem):
        x, y, c, _ = _place()
        to = dict(device_id=(x, y, 1 - c), device_id_type=MESH)
        for par in range(2):
            give, get = _needed_blocks(phase, 1 - par), _needed_blocks(phase, par)

            @pl.when(c == par)
            def _():
                for blk in give:
                    pltpu.make_async_remote_copy(src_ref=_blk(g_ref, blk), dst_ref=_blk(sib_ref, blk),
                                                 send_sem=send_sem, recv_sem=recv_sem, **to).start()
                pltpu.make_async_remote_copy(src_ref=_cols(g_ref, len(get)), dst_ref=_cols(sib_ref, len(get)),
                                             send_sem=send_sem, recv_sem=recv_sem, **to).wait_recv()
                pltpu.make_async_remote_copy(src_ref=_cols(g_ref, len(give)), dst_ref=_cols(sib_ref, len(give)),
                                             send_sem=send_sem, recv_sem=recv_sem, **to).wait_send()

    sib = pl.pallas_call(
        swap, name="pair_swap_" + phase, in_specs=[ANY], out_specs=ANY,
        out_shape=jax.ShapeDtypeStruct((n, d), g_grp.dtype),
        scratch_shapes=[pltpu.SemaphoreType.DMA, pltpu.SemaphoreType.DMA],
    )(*_in_hbm(g_grp))

    lists = [_needed_blocks(phase, par) for par in range(2)]
    longest = max(len(t) for t in lists)
    table = jnp.asarray([t + [t[-1]] * (longest - len(t)) for t in lists], jnp.int32)[lax.axis_index("c")]

    def add(t_ref, a_ref, b_ref, o_ref):
        o_ref[...] = (a_ref[...].astype(F32) + b_ref[...].astype(F32)).astype(o_ref.dtype)

    blk = pl.BlockSpec((DH, d), lambda i, t: (t[i], 0))
    return pl.pallas_call(
        add, name="pair_add_" + phase,
        grid_spec=pltpu.PrefetchScalarGridSpec(num_scalar_prefetch=1, grid=(longest,),
                                               in_specs=[blk, blk], out_specs=blk),
        out_shape=jax.ShapeDtypeStruct((n, d), g_grp.dtype),
        compiler_params=_params(("arbitrary",)),
    )(table, g_grp, sib)


def _sum_shard(g_g, g_c, land):
    d = g_g.shape[1]
    chip = 2 * lax.axis_index("x") + lax.axis_index("y")

    def body(t_ref, gg_ref, gc_ref, land_ref, o_ref):
        b = pl.program_id(0)
        in_g = t_ref[2, b] == 1
        own = jnp.where(in_g, gg_ref[...].astype(F32), gc_ref[...].astype(F32))
        for jj in range(3):
            own = own + land_ref[jj].astype(F32)
        o_ref[...] = jnp.where(in_g & (b % 2 != lax.axis_index("c")), 0.0, own)

    return pl.pallas_call(
        body, name="sum_w_in",
        grid_spec=pltpu.PrefetchScalarGridSpec(
            num_scalar_prefetch=1, grid=(ALIGNED_BLOCKS,),
            in_specs=[pl.BlockSpec((DH, d), lambda b, t: (t[0, b], 0)), pl.BlockSpec((DH, d), lambda b, t: (t[1, b], 0)),
                      pl.BlockSpec((3, DH, d), lambda b, t: (0, b, 0))],
            out_specs=pl.BlockSpec((DH, d), lambda b, t: (b, 0))),
        out_shape=jax.ShapeDtypeStruct((ALIGNED_W, d), F32),
        compiler_params=_params(("arbitrary",)),
    )(_block_table(chip, False, 0, 0), g_g, g_c, land)


def _sum_rows(stack, land, rows):
    _, r, d = stack.shape
    rows = min(rows, r)
    chip = 2 * lax.axis_index("x") + lax.axis_index("y")

    def body(t_ref, own_ref, land_ref, o_ref):
        acc = own_ref[0].astype(F32)
        for jj in range(3):
            acc = acc + land_ref[jj].astype(F32)
        o_ref[...] = acc

    return pl.pallas_call(
        body, name="sum_w_out",
        grid_spec=pltpu.PrefetchScalarGridSpec(
            num_scalar_prefetch=1, grid=(r // rows,),
            in_specs=[pl.BlockSpec((1, rows, d), lambda i, t: (t[0], i, 0)),
                      pl.BlockSpec((3, rows, d), lambda i, t: (0, i, 0))],
            out_specs=pl.BlockSpec((rows, d), lambda i, t: (i, 0))),
        out_shape=jax.ShapeDtypeStruct((r, d), F32),
        compiler_params=_params(("arbitrary",)),
    )(jnp.reshape(chip, (1,)).astype(jnp.int32), stack, land)


def _exchange_parts(n_swap, with_pack):
    def copies(refs, send_sems, recv_sems):
        x, y, c, _ = _place()
        me = 4 * x + 2 * y + c
        cps = [pltpu.make_async_remote_copy(
            src_ref=refs[2 * a], dst_ref=refs[2 * a + 1], send_sem=send_sems.at[a], recv_sem=recv_sems.at[a],
            device_id=(x, y, 1 - c), device_id_type=MESH) for a in range(n_swap)]
        if with_pack:
            pack_ref, packs = refs[2 * n_swap], refs[2 * n_swap + 1]
            for r in range(1, 8):
                dx, dy, dc = (r >> 2) & 1, (r >> 1) & 1, r & 1
                peer = (x + dx - 2 * x * dx, y + dy - 2 * y * dy, c + dc - 2 * c * dc)
                cps.append(pltpu.make_async_remote_copy(
                    src_ref=pack_ref, dst_ref=packs.at[me], send_sem=send_sems.at[n_swap + r - 1],
                    recv_sem=recv_sems.at[n_swap + r - 1], device_id=peer, device_id_type=MESH))
        return cps

    def issue(refs, send_sems, recv_sems):
        for cp in copies(refs, send_sems, recv_sems):
            cp.start()

    def await_(refs, send_sems, recv_sems):
        cps = copies(refs, send_sems, recv_sems)
        for cp in cps:
            cp.wait_recv()
        for cp in cps:
            cp.wait_send()

    return issue, await_, n_swap + (7 if with_pack else 0)


def _sum_packs(pack, packs):
    x, y, c = lax.axis_index("x"), lax.axis_index("y"), lax.axis_index("c")
    me = jnp.reshape(4 * x + 2 * y + c, (1,)).astype(jnp.int32)

    def body(me_ref, own_ref, p_ref, o_ref):
        acc = jnp.where(me_ref[0] == 0, own_ref[...], p_ref[0])
        for d in range(1, 8):
            acc = acc + jnp.where(me_ref[0] == d, own_ref[...], p_ref[d])
        o_ref[...] = acc

    full = lambda s: pl.BlockSpec(s.shape, lambda i, t: (0,) * s.ndim)
    return pl.pallas_call(
        body, name="sum_packs",
        grid_spec=pltpu.PrefetchScalarGridSpec(num_scalar_prefetch=1, grid=(1,), in_specs=[full(pack), full(packs)],
                                               out_specs=full(pack)),
        out_shape=jax.ShapeDtypeStruct(pack.shape, F32),
    )(me, pack, packs)


def _adamw_update(g, w_ref, m_ref, v_ref, go, do, mo, vo):
    c1 = 1.0 / (1.0 - ADAM_B1 ** ADAM_STEP)
    c2 = 1.0 / (1.0 - ADAM_B2 ** ADAM_STEP)
    mn = ADAM_B1 * m_ref[...] + (1.0 - ADAM_B1) * g
    vn = ADAM_B2 * v_ref[...] + (1.0 - ADAM_B2) * (g * g)
    go[...] = g
    mo[...] = mn
    vo[...] = vn
    do[...] = -ADAM_LR * ((mn * c1) / (jnp.sqrt(vn * c2) + ADAM_EPS) + ADAM_WD * w_ref[...])


def _adamw(w, m, v, g1, g2, rows, name):
    r, cdim = w.shape
    rows = min(rows, r)

    def body(*refs):
        n_in = 4 if g2 is None else 5
        w_ref, m_ref, v_ref, g_ref = refs[:4]
        g = g_ref[...] if g2 is None else g_ref[...] + refs[4][...]
        _adamw_update(g, w_ref, m_ref, v_ref, *refs[n_in:n_in + 4])

    blk = pl.BlockSpec((rows, cdim), lambda i: (i, 0))
    args = [w, m, v, g1] + ([] if g2 is None else [g2])
    shp = jax.ShapeDtypeStruct((r, cdim), F32)
    return pl.pallas_call(
        body, name=name, grid=(r // rows,),
        in_specs=[blk] * len(args), out_specs=[blk] * 4, out_shape=[shp] * 4,
        compiler_params=_params(("parallel",), 20 * rows * cdim * 4 + 8 * 2**20),
    )(*_in_hbm(*args))


def _adamw_shard(wt, mt, vt, g1, g2):
    r, d = wt.shape
    cols = min(128, d)

    def body(w_ref, m_ref, v_ref, g_ref, g2_ref, go, do, mo, vo, pad_ref):
        chip = 2 * lax.axis_index("x") + lax.axis_index("y")
        back = [(ALIGNED_W - s) % ALIGNED_W for s in SHIFTS]
        pad_ref[...] = pltpu.roll(g_ref[...] + g2_ref[...], _by_chip(chip, back), 0)
        outs = [o.at[:, 0, :] for o in (go, do, mo, vo)]
        _adamw_update(pad_ref[0:r, :], w_ref, m_ref, v_ref, *outs)

    blk = pl.BlockSpec((r, cols), lambda i: (0, i))
    gblk = pl.BlockSpec((ALIGNED_W, cols), lambda i: (0, i))
    oblk = pl.BlockSpec((r, 1, cols), lambda i: (0, 0, i))
    shp = jax.ShapeDtypeStruct((r, 1, d), F32)
    return pl.pallas_call(
        body, name="adamw_w_in", grid=(d // cols,),
        in_specs=[blk] * 3 + [gblk] * 2, out_specs=[oblk] * 4, out_shape=[shp] * 4,
        scratch_shapes=[pltpu.VMEM((ALIGNED_W, cols), F32)],
        compiler_params=_params(("parallel",), 24 * ALIGNED_W * cols * 4 + 8 * 2**20),
    )(wt, mt, vt, g1, g2)


def _pad_lanes(a, width):
    return jnp.pad(a, ((0, 0), (0, width - a.shape[1])))


def _gathered_to_full(g):
    return jnp.transpose(g, (1, 0, 2)).reshape(g.shape[1], 4 * g.shape[2])


def _row(a):
    return _pad_lanes(a.reshape(1, -1), 1024)


def _small_pack(nin, cb, fn, al, dt, gn, cqw_shard, cw_shard):
    ad = jnp.concatenate([al.reshape(1, -1), dt.reshape(1, -1)], axis=1)
    rows = [_row(nin), _row(cb), _row(fn), _row(ad), _row(gn), cqw_shard.reshape(3, 1024), _row(cw_shard)]
    out = jnp.concatenate(rows, axis=0)
    return jnp.pad(out, ((0, 16 - out.shape[0]), (0, 0)))


def kernel(x, norm_in_w, w_in, conv_qkv_w, A_log, dt_bias, gdn_norm_w, conv_w, conv_b, w_out, final_norm_w, loss_target, m_norm_in_w, m_w_in, m_conv_qkv_w, m_A_log, m_dt_bias, m_gdn_norm_w, m_conv_w, m_conv_b, m_w_out, m_final_norm_w, v_norm_in_w, v_w_in, v_conv_qkv_w, v_A_log, v_dt_bias, v_gdn_norm_w, v_conv_w, v_conv_b, v_w_out, v_final_norm_w):
    chip = 2 * lax.axis_index("x") + lax.axis_index("y")
    a_shard = _align_shard(jnp.transpose(w_in, (2, 0, 1)))
    wo_b = _cast_bf16(w_out[0], 256, "cast_w_out")
    d_model = x.shape[-1]
    stack = lambda s: lax.empty((4,) + s.shape, s.dtype)
    wg0 = lax.empty((WG_BLOCKS * DH, d_model), BF16)
    wc0 = lax.empty((WC_BLOCKS * DH, d_model), BF16)
    ss_g, rs_g, bufs_g, tok_g = _gather_start("g", a_shard, wg0, [(conv_qkv_w[0], stack(conv_qkv_w[0]))])
    ss_c, rs_c, bufs_c, tok_c = _gather_start("c", bufs_g[0], wc0,
                                              [(conv_w[0], stack(conv_w[0])), (wo_b, stack(wo_b))])
    wg1, wc1, wog1, cqg1, cwg1 = _place_own(bufs_c[0], bufs_c[4], bufs_g[2], bufs_c[2],
                                            [bufs_g[1], bufs_c[1], bufs_c[5], bufs_g[3], bufs_c[3]])
    x0 = x[0]
    h = _rms_in(x0, _tie(_tie(norm_in_w, tok_g, "after_gather_start_g"), tok_c, "after_gather_start_c"))
    adam_in = [jnp.transpose(a[0]) for a in (w_in, m_w_in, v_w_in)]
    sp = lambda nin, cb, fn, al, dt, gn, cq, cwv: _small_pack(nin, cb, fn, al, dt, gn, cq[0], cwv[0])
    w_s = sp(norm_in_w, conv_b, final_norm_w, A_log, dt_bias, gdn_norm_w, conv_qkv_w, conv_w)
    m_s = sp(m_norm_in_w, m_conv_b, m_final_norm_w, m_A_log, m_dt_bias, m_gdn_norm_w, m_conv_qkv_w, m_conv_w)
    v_s = sp(v_norm_in_w, v_conv_b, v_final_norm_w, v_A_log, v_dt_bias, v_gdn_norm_w, v_conv_qkv_w, v_conv_w)
    a_thru, wg, _, cq_g = _gather_wait("g", ss_g, rs_g, [bufs_c[0], wg1, bufs_g[2], cqg1],
                                       [h, w_s, m_s, v_s] + adam_in[1:])
    w_g = _merge_edges(_sibling_forward("g", wg), G_EDGE, G_MIXED, "merge_edges_g")
    cqw = _gathered_to_full(cq_g)
    ad = jnp.pad(jnp.concatenate([A_log, dt_bias], axis=0), ((0, 0), (A_LANE, 0)))
    fwd_c = {}

    def on_q(q):
        _, wc, _, cw_g, _, wo_g = _gather_wait("c", ss_c, rs_c,
                                               [a_thru, wc1, bufs_c[2], cwg1, bufs_c[4], wog1], q)
        issue, _ = _sibling_forward_parts("c")
        ss, rs, (wc,), tok = _split_start("sibling_forward_start_c", issue, [wc], 3)
        fwd_c.update(ss=ss, rs=rs, wc=wc, cw_g=cw_g, wo_g=wo_g)
        return _tie(q, tok, "after_sibling_forward_start_c")

    def late(o):
        _, await_ = _sibling_forward_parts("c")
        (wc,) = _split_wait("sibling_forward_wait_c", await_, fwd_c["ss"], fwd_c["rs"], [fwd_c["wc"]], o)
        return (_merge_edges(wc, C_EDGE, C_MIXED, "merge_edges_c"), fwd_c["wo_g"].reshape(2 * GW, d_model),
                _gathered_to_full(fwd_c["cw_g"]))

    scat = {}

    def on_grad_c(g_c, g_wout, do):
        go4 = g_wout.reshape(4, GW // 2, d_model)
        land = lax.empty((3, ALIGNED_W, d_model), BF16)
        land_o = lax.empty((3, GW // 2, d_model), BF16)
        ss, rs, bufs, tok = _scatter_start("c", g_c, land, [(go4, land_o)])
        scat["c"] = (ss, rs, bufs)
        return _tie(do, tok, "after_scatter_start_c")

    def on_grad_g(g_g, dproj_g):
        ss, rs, bufs, tok = _scatter_start("g", _pair_reduce("g", g_g), scat["c"][2][1], [], halved=True)
        scat["g"] = (ss, rs, bufs)
        return _tie(dproj_g, tok, "after_scatter_start_g")

    gx, sm, _ = _local_step(x0, loss_target[0], h, w_g, cqw, late, norm_in_w, ad, gdn_norm_w, conv_b,
                            final_norm_w.reshape(1, -1), on_grad_c, on_grad_g, on_q)

    ss, rs, bufs = scat["c"]
    g_c, land, go4, land_o = _scatter_wait("c", ss, rs, [bufs[0], scat["g"][2][1], bufs[2], bufs[3]], gx)
    part_out = _sum_rows(go4, land_o, 128)
    ad_g = jnp.concatenate([sm["al"][:, A_LANE:], sm["dt"][:, A_LANE:]], axis=1)
    pack = jnp.concatenate([_row(sm["nin"]), _row(sm["cb"]), _row(sm["fn"]), _row(ad_g), _row(sm["gn"]),
                            jnp.concatenate(sm["cq"], axis=1).reshape(12, 1024), sm["cw"], _row(sm["loss"])], axis=0)
    pack = jnp.pad(pack, ((0, PACK_ROWS - pack.shape[0]), (0, 0)))
    issue, await_a, nsem = _exchange_parts(1, True)
    ss_a, rs_a, bufs_a, tok_a = _split_start(
        "exchange_start_small", issue,
        [part_out, lax.empty(part_out.shape, F32), pack, lax.empty((8,) + pack.shape, F32)], nsem)
    ss, rs, bufs = scat["g"]
    g_g, land = _scatter_wait("g", ss, rs, [bufs[0], land], [gx, tok_a], halved=True)
    part_in = _sum_shard(g_g, g_c, land)
    issue, await_b, nsem = _exchange_parts(1, False)
    ss_b, rs_b, bufs_b, tok_b = _split_start("exchange_start_w_in", issue,
                                             [part_in, lax.empty(part_in.shape, F32)], nsem)
    part_out, sib_out, pack, packs = _split_wait("exchange_wait_small", await_a, ss_a, rs_a, bufs_a, tok_b)
    tot = _sum_packs(pack, packs)
    g_wo, d_wo, m_wo, v_wo = _adamw(w_out[0], m_w_out[0], v_w_out[0], part_out, sib_out, 128, "adamw_w_out")
    g_cq_sh = lax.dynamic_slice_in_dim(tot[R_CQ:R_CQ + 12].reshape(4, 3 * GW), chip * 768, 768, axis=1)
    g_cw_sh = lax.dynamic_slice_in_dim(tot[R_CW:R_CW + 3], chip * 256, 256, axis=1)
    g_s = _small_pack(tot[R_NIN], tot[R_CB], tot[R_FN], tot[R_AD, :HEADS], tot[R_AD, HEADS:2 * HEADS],
                      tot[R_GN, :DH], g_cq_sh, g_cw_sh)
    small = _adamw(w_s, m_s, v_s, g_s, None, 16, "adamw_small")
    part_in, sib_in = _split_wait("exchange_wait_w_in", await_b, ss_b, rs_b, bufs_b, [small[0], d_wo])
    g_wi, d_wi, m_wi, v_wi = [jnp.transpose(a, (1, 2, 0))[0] for a in _adamw_shard(*adam_in, part_in, sib_in)]

    def unpack(a, big_in, big_out):
        return (a[0:1], big_in[None], a[5:8].reshape(1, 4, 768), a[3:4, :HEADS], a[3:4, HEADS:2 * HEADS],
                a[4:5, :DH], a[8, :768].reshape(1, 3, 256), a[1:2], big_out[None], a[2])

    loss = tot[R_LOSS, 0]
    return (loss, gx[None], *unpack(small[0], g_wi, g_wo), *unpack(small[1], d_wi, d_wo),
            *unpack(small[2], m_wi, m_wo), *unpack(small[3], v_wi, v_wo))
```

```python
import functools
import math

import jax
import jax.numpy as jnp
from jax import lax
from jax.experimental import pallas as pl
from jax.experimental.pallas import tpu as pltpu

F32 = jnp.float32
BF16 = jnp.bfloat16
MESH = pl.DeviceIdType.MESH
ANY = pl.BlockSpec(memory_space=pl.ANY)

HEADS = 8
DH = 128
CH = 64
GW = HEADS * DH
EPS = 1e-6
VMEM_V7X = 64 * 1024 * 1024

QB, KB, VB, ZB, BAB = 0, 8, 16, 24, 32
A_LANE = 120
NG, NC = 33, 32
GW_COLS, CW_COLS = NG * DH, NC * DH

SHARD_W = 2052
ALIGNED_BLOCKS = 17
ALIGNED_W = ALIGNED_BLOCKS * DH
SHIFTS = (0, 4, ALIGNED_W - 8, ALIGNED_W - 4)
G_EDGE, C_EDGE = 34, 32
G_SPARE, C_SPARE = 33, 34
WG_BLOCKS, WC_BLOCKS = 38, 36
G_MIXED, C_MIXED = (2, BAB), (4 * 7 + 1,)


def _shard_blocks(chip, edges):
    g, c = "g", "c"
    if chip == 0:
        out = [(g, 3 * b) for b in range(8)] + [(g, 3 * b + 1) for b in range(8)] + [(g, G_EDGE, G_MIXED[0])]
    elif chip == 1:
        out = [(g, G_EDGE + 1, G_MIXED[0])] + [(g, 3 * b + 2) for b in range(1, 8)]
        out += [(g, ZB + b) for b in range(8)] + [(g, G_EDGE + 2, G_MIXED[1])]
    elif chip == 2:
        out = [(c, 4 * b) for b in range(8)] + [(c, 4 * b + 1) for b in range(7)]
        out += [(c, C_EDGE, C_MIXED[0]), (g, G_EDGE + 3, G_MIXED[1])]
    else:
        out = [(c, 4 * b + 2) for b in range(8)] + [(c, 4 * b + 3) for b in range(8)] + [(c, C_EDGE + 1, C_MIXED[0])]
    return [(o[0], o[1] if (edges or len(o) == 2) else o[2]) for o in out]


def _by_chip(chip, vals):
    if all(v == vals[0] for v in vals):
        return vals[0]
    r = vals[3]
    for kk in (2, 1, 0):
        r = jnp.where(chip == kk, vals[kk], r)
    return r

ADAM_LR, ADAM_B1, ADAM_B2, ADAM_EPS, ADAM_WD, ADAM_STEP = 0.001, 0.9, 0.999, 1e-08, 0.01, 10

R_NIN, R_CB, R_FN, R_AD, R_GN, R_CQ, R_CW, R_LOSS, PACK_ROWS = 0, 1, 2, 3, 4, 5, 17, 20, 24

NN = ((1,), (0,))
NT = ((1,), (1,))
TN = ((0,), (0,))


def _dot(a, b, dims=NN, mode="lo"):
    dn = (dims, ((), ()))
    if mode == "hi":
        return lax.dot_general(a, b, dn, precision=lax.Precision.HIGHEST, preferred_element_type=F32)
    ah, bh = a.astype(BF16), b.astype(BF16)
    out = lax.dot_general(ah, bh, dn, preferred_element_type=F32)
    if mode == "x3":
        al = (a - ah.astype(F32)).astype(BF16)
        bl = (b - bh.astype(F32)).astype(BF16)
        out = out + lax.dot_general(ah, bl, dn, preferred_element_type=F32)
        out = out + lax.dot_general(al, bh, dn, preferred_element_type=F32)
    return out


P_GRAM, P_INV, P_SOL, P_SCAN, P_SCANB, P_BWD = "lo", "lo", "lo", "lo", "lo", "lo"
P_CUM = "x3"


def _params(sem=None, vmem=None):
    kw = {}
    if sem is not None:
        kw["dimension_semantics"] = sem
    if vmem is not None:
        kw["vmem_limit_bytes"] = int(min(max(vmem, 32 * 2**20), VMEM_V7X - 8 * 2**20))
    return pltpu.CompilerParams(**kw)


def _in_hbm(*arrays):
    return [pltpu.with_memory_space_constraint(a, pltpu.HBM) for a in arrays]


def _sigmoid(x):
    return 1.0 / (1.0 + jnp.exp(-x))


def _dsilu(x, s):
    return s * (1.0 + x * (1.0 - s))


def _rows(shape):
    return lax.broadcasted_iota(jnp.int32, shape, 0)


def _shift_down(x, s):
    if s == 0:
        return x
    return jnp.where(_rows(x.shape) >= s, pltpu.roll(x, s, 0), 0.0)


def _shift_up(x, s):
    if s == 0:
        return x
    n = x.shape[0]
    return jnp.where(_rows(x.shape) < n - s, pltpu.roll(x, n - s, 0), 0.0)


def _matmul(a, b, dims, out_dtype, tm, tn, tk, name, add=None, n=None, b_outer=False):
    if dims == NN:
        (m, k), n = a.shape, b.shape[1]
    elif dims == NT:
        (m, k), n = a.shape, (n or b.shape[0])
    else:
        (k, m), n = a.shape, b.shape[1]
    tm, tn, tk = min(tm, m), min(tn, n), min(tk, k)
    assert m % tm == 0 and n % tn == 0 and k % tk == 0, (name, m, n, k, tm, tn, tk)
    nk = k // tk

    def body(*refs):
        if add is None:
            a_ref, b_ref, o_ref = refs[:3]
            add_ref = None
        else:
            a_ref, b_ref, add_ref, o_ref = refs[:4]
        part = _dot(a_ref[...], b_ref[...], dims)
        if nk == 1:
            if add_ref is not None:
                part = part + add_ref[...]
            o_ref[...] = part.astype(out_dtype)
            return
        acc = refs[-1]
        kk = pl.program_id(2)

        @pl.when(kk == 0)
        def _():
            acc[...] = part

        @pl.when(kk > 0)
        def _():
            acc[...] += part

        @pl.when(kk == nk - 1)
        def _():
            r = acc[...]
            if add_ref is not None:
                r = r + add_ref[...]
            o_ref[...] = r.astype(out_dtype)

    ij = (lambda g0, g1: (g1, g0)) if b_outer else (lambda g0, g1: (g0, g1))

    def spec(shape, pick):
        return pl.BlockSpec(shape, lambda g0, g1, kk: pick(*ij(g0, g1), kk))

    a_spec = spec((tk, tm), lambda i, j, kk: (kk, i)) if dims == TN else spec((tm, tk), lambda i, j, kk: (i, kk))
    b_spec = spec((tn, tk), lambda i, j, kk: (j, kk)) if dims == NT else spec((tk, tn), lambda i, j, kk: (kk, j))
    o_spec = spec((tm, tn), lambda i, j, kk: (i, j))
    in_specs = [a_spec, b_spec]
    args = [a, b]
    if add is not None:
        in_specs.append(o_spec)
        args.append(add)
    osz = jnp.dtype(out_dtype).itemsize
    est = 2 * (tm * tk * a.dtype.itemsize + tk * tn * b.dtype.itemsize + tm * tn * osz)
    est += 3 * tm * tn * 4 + (2 * tm * tn * 4 if add is not None else 0)
    return pl.pallas_call(
        body, name=name, grid=(n // tn, m // tm, nk) if b_outer else (m // tm, n // tn, nk),
        in_specs=in_specs, out_specs=o_spec,
        out_shape=jax.ShapeDtypeStruct((m, n), out_dtype),
        scratch_shapes=[pltpu.VMEM((tm, tn), F32)] if nk > 1 else [],
        compiler_params=_params(("parallel", "parallel", "arbitrary"), est + 8 * 2**20),
    )(*args)


def _cast_bf16(a, rows, name):
    r, c = a.shape
    rows = min(rows, r)

    def body(a_ref, o_ref):
        o_ref[...] = a_ref[...].astype(BF16)

    return pl.pallas_call(
        body, name=name, grid=(r // rows,),
        in_specs=[pl.BlockSpec((rows, c), lambda i: (i, 0))],
        out_specs=pl.BlockSpec((rows, c), lambda i: (i, 0)),
        out_shape=jax.ShapeDtypeStruct((r, c), BF16),
        compiler_params=_params(("parallel",)),
    )(a)


def _align_shard(wt):
    r, _, d = wt.shape
    cols = min(256, d)

    def body(w_ref, o_ref, pad_ref):
        chip = 2 * lax.axis_index("x") + lax.axis_index("y")
        pad_ref[...] = jnp.zeros_like(pad_ref)
        pad_ref[0:r, :] = w_ref[:, 0, :]
        o_ref[...] = pltpu.roll(pad_ref[...], _by_chip(chip, SHIFTS), 0).astype(BF16)

    return pl.pallas_call(
        body, name="align_shard", grid=(d // cols,),
        in_specs=[pl.BlockSpec((r, 1, cols), lambda i: (0, 0, i))],
        out_specs=pl.BlockSpec((ALIGNED_W, cols), lambda i: (0, i)),
        out_shape=jax.ShapeDtypeStruct((ALIGNED_W, d), BF16),
        scratch_shapes=[pltpu.VMEM((ALIGNED_W, cols), F32)],
        compiler_params=_params(("parallel",)),
    )(wt)


def _rms_in(x, w):
    n, d = x.shape
    tr = min(256, n)

    def body(x_ref, w_ref, h_ref):
        xv = x_ref[...]
        r = lax.rsqrt(jnp.mean(xv * xv, axis=-1, keepdims=True) + EPS)
        h_ref[...] = (xv * r * w_ref[...]).astype(BF16)

    return pl.pallas_call(
        body, name="rms_in", grid=(n // tr,),
        in_specs=[pl.BlockSpec((tr, d), lambda i: (i, 0)), pl.BlockSpec((1, d), lambda i: (0, 0))],
        out_specs=pl.BlockSpec((tr, d), lambda i: (i, 0)),
        out_shape=jax.ShapeDtypeStruct((n, d), BF16),
        compiler_params=_params(("parallel",)),
    )(x, w)


def _conv_silu(p, w_ref, taps):
    c = None
    for j in range(taps):
        t = _shift_down(p, taps - 1 - j) * w_ref[j:j + 1, :]
        c = t if c is None else c + t
    return c


def _prep_qkv(proj, cw):
    n = proj.shape[0]

    def body(p3, wq, wk, wv, q_ref, k_ref, v_ref):
        for kind, (w_ref, o_ref) in enumerate(((wq, q_ref), (wk, k_ref), (wv, v_ref))):
            c = _conv_silu(p3[:, kind * DH:(kind + 1) * DH], w_ref, 4)
            a = c * _sigmoid(c)
            if kind < 2:
                r = lax.rsqrt(jnp.sum(a * a, axis=-1, keepdims=True) + EPS)
                a = a * (r * (DH ** -0.5 if kind == 0 else 1.0))
            o_ref[...] = a

    col = pl.BlockSpec((n, DH), lambda h: (0, h))
    wcol = lambda base: pl.BlockSpec((4, DH), lambda h: (0, base + h))
    out = jax.ShapeDtypeStruct((n, GW), F32)
    return pl.pallas_call(
        body, name="prep_qkv", grid=(HEADS,),
        in_specs=[pl.BlockSpec((n, 3 * DH), lambda h: (0, h)), wcol(QB), wcol(KB), wcol(VB)],
        out_specs=[col] * 3, out_shape=[out] * 3,
        compiler_params=_params(("parallel",), 40 * 2**20),
    )(proj, cw, cw, cw)


def _prep_qkv_bwd(proj, cw, dq, dk, dv, dproj):
    n = proj.shape[0]

    def body(p3, wq, wk, wv, dq_ref, dk_ref, dv_ref, _, o3, gq, gk, gv):
        for kind, (w_ref, d_ref, g_ref) in enumerate(((wq, dq_ref, gq), (wk, dk_ref, gk), (wv, dv_ref, gv))):
            p = p3[:, kind * DH:(kind + 1) * DH]
            shifted = [_shift_down(p, 3 - j) for j in range(4)]
            c = shifted[0] * w_ref[0:1, :]
            for j in range(1, 4):
                c = c + shifted[j] * w_ref[j:j + 1, :]
            s = _sigmoid(c)
            a = c * s
            d = d_ref[...]
            if kind < 2:
                r = lax.rsqrt(jnp.sum(a * a, axis=-1, keepdims=True) + EPS)
                sc = DH ** -0.5 if kind == 0 else 1.0
                d = (sc * r) * (d - a * ((r * r) * jnp.sum(d * a, axis=-1, keepdims=True)))
            dc = d * _dsilu(c, s)
            dp = None
            for j in range(4):
                g_ref[j:j + 1, :] = jnp.sum(dc * shifted[j], axis=0, keepdims=True)
                t = _shift_up(dc, 3 - j) * w_ref[j:j + 1, :]
                dp = t if dp is None else dp + t
            o3[:, kind * DH:(kind + 1) * DH] = dp.astype(BF16)

    col = pl.BlockSpec((n, DH), lambda h: (0, h))
    wcol = lambda base: pl.BlockSpec((4, DH), lambda h: (0, base + h))
    p3spec = pl.BlockSpec((n, 3 * DH), lambda h: (0, h))
    return pl.pallas_call(
        body, name="prep_qkv_bwd", grid=(HEADS,),
        in_specs=[p3spec, wcol(QB), wcol(KB), wcol(VB), col, col, col, ANY],
        out_specs=[p3spec] + [wcol(0)] * 3,
        out_shape=[jax.ShapeDtypeStruct(dproj.shape, BF16)] + [jax.ShapeDtypeStruct((4, GW), F32)] * 3,
        input_output_aliases={7: 0},
        compiler_params=_params(("parallel",), 48 * 2**20),
    )(proj, cw, cw, cw, dq, dk, dv, dproj)


CPB = 8
SCAN_CPS = 4


def _tri(lower, rows):
    i = lax.broadcasted_iota(jnp.int32, (rows, rows), 0)
    j = lax.broadcasted_iota(jnp.int32, (rows, rows), 1)
    return jnp.where((i // CH == j // CH) & ((i >= j) if lower else (j >= i)), 1.0, 0.0)


def _lane(shape):
    return lax.broadcasted_iota(jnp.int32, shape, 1)


def _prep_bg(proj, ad):
    n = proj.shape[0]
    nch = n // CH
    cpb = CPB if nch % CPB == 0 else 1
    rows = cpb * CH

    def body(p_ref, ad_ref, bg_ref, bgt_ref):
        p = p_ref[...]
        lane = _lane(p.shape)
        beta = _sigmoid(p)
        xa = p + ad_ref[1:2, :]
        sp = jnp.maximum(xa, 0.0) + jnp.log(1.0 + jnp.exp(-jnp.abs(xa)))
        g = pltpu.roll(-jnp.exp(ad_ref[0:1, :]) * sp, DH - A_LANE + HEADS, 1)
        gc = _dot(_tri(True, rows), g, NN, P_CUM)
        bg = jnp.where(lane < HEADS, beta, jnp.where(lane < 2 * HEADS, gc, 0.0))
        bg_ref[...] = bg
        for ci in range(cpb):
            bgt_ref[ci] = bg[ci * CH:(ci + 1) * CH, :].T

    return pl.pallas_call(
        body, name="prep_bg", grid=(nch // cpb,),
        in_specs=[pl.BlockSpec((rows, DH), lambda i: (i, BAB)), pl.BlockSpec((2, DH), lambda i: (0, 0))],
        out_specs=[pl.BlockSpec((rows, DH), lambda i: (i, 0)), pl.BlockSpec((cpb, DH, CH), lambda i: (i, 0, 0))],
        out_shape=[jax.ShapeDtypeStruct((n, DH), F32), jax.ShapeDtypeStruct((nch, DH, CH), F32)],
        compiler_params=_params(("parallel",)),
    )(*_in_hbm(proj, ad))


def _prep_bg_bwd(proj, ad, dbg, dproj):
    n = proj.shape[0]
    nch = n // CH
    cpb = CPB if nch % CPB == 0 else 1
    rows = cpb * CH

    def body(p_ref, ad_ref, d_ref, _, o_ref, ga_ref, gd_ref):
        p = p_ref[...]
        d = d_ref[...]
        lane = _lane(p.shape)
        beta = _sigmoid(p)
        xa = p + ad_ref[1:2, :]
        sp = jnp.maximum(xa, 0.0) + jnp.log(1.0 + jnp.exp(-jnp.abs(xa)))
        na = -jnp.exp(ad_ref[0:1, :])
        dg = pltpu.roll(_dot(_tri(False, rows), d, NN, P_CUM), A_LANE - HEADS, 1)
        da = dg * na * _sigmoid(xa)
        is_g = lane >= A_LANE
        o_ref[...] = jnp.where(lane < HEADS, d * beta * (1.0 - beta), jnp.where(is_g, da, 0.0)).astype(BF16)
        ga = jnp.sum(jnp.where(is_g, dg * na * sp, 0.0), axis=0, keepdims=True)
        gd = jnp.sum(jnp.where(is_g, da, 0.0), axis=0, keepdims=True)

        @pl.when(pl.program_id(0) == 0)
        def _():
            ga_ref[...] = jnp.zeros_like(ga_ref)
            gd_ref[...] = jnp.zeros_like(gd_ref)

        ga_ref[...] += ga
        gd_ref[...] += gd

    one = pl.BlockSpec((1, DH), lambda i: (0, 0))
    return pl.pallas_call(
        body, name="prep_bg_bwd", grid=(nch // cpb,),
        in_specs=[pl.BlockSpec((rows, DH), lambda i: (i, BAB)), pl.BlockSpec((2, DH), lambda i: (0, 0)),
                  pl.BlockSpec((rows, DH), lambda i: (i, 0)), ANY],
        out_specs=[pl.BlockSpec((rows, DH), lambda i: (i, BAB)), one, one],
        out_shape=[jax.ShapeDtypeStruct(dproj.shape, BF16), jax.ShapeDtypeStruct((1, DH), F32),
                   jax.ShapeDtypeStruct((1, DH), F32)],
        input_output_aliases={3: 0},
        compiler_params=_params(("arbitrary",)),
    )(proj, ad, dbg, dproj)


def _gdn_out(o, proj, wg):
    n = o.shape[0]

    def body(o_ref, z_ref, w_ref, y_ref):
        ov, z = o_ref[...], z_ref[...]
        r = lax.rsqrt(jnp.mean(ov * ov, axis=-1, keepdims=True) + EPS)
        y_ref[...] = (ov * r * w_ref[...] * (z * _sigmoid(z))).astype(BF16)

    return pl.pallas_call(
        body, name="gdn_out", grid=(HEADS,),
        in_specs=[pl.BlockSpec((n, DH), lambda h: (0, h)), pl.BlockSpec((n, DH), lambda h: (0, ZB + h)),
                  pl.BlockSpec((1, DH), lambda h: (0, 0))],
        out_specs=pl.BlockSpec((n, DH), lambda h: (0, h)),
        out_shape=jax.ShapeDtypeStruct((n, 2 * GW), BF16),
        compiler_params=_params(("parallel",)),
    )(o, proj, wg)


def _gdn_out_bwd(o, proj, wg, dout_b, w_out):
    n = o.shape[0]
    d_model = dout_b.shape[1]

    def body(o_ref, z_ref, w_ref, g_ref, wo_ref, do_ref, dz_ref, gw_ref):
        ov, z, w = o_ref[...], z_ref[...], w_ref[...]
        d = _dot(g_ref[...], wo_ref[...], NT)
        r = lax.rsqrt(jnp.mean(ov * ov, axis=-1, keepdims=True) + EPS)
        nrm = ov * r
        s = _sigmoid(z)
        dz_ref[...] = (d * (nrm * w) * _dsilu(z, s)).astype(BF16)
        dn_w = d * (z * s)
        gw = jnp.sum(dn_w * nrm, axis=0, keepdims=True)
        dn = dn_w * w
        do_ref[...] = r * (dn - nrm * jnp.mean(dn * nrm, axis=-1, keepdims=True))

        @pl.when(pl.program_id(0) == 0)
        def _():
            gw_ref[...] = jnp.zeros_like(gw_ref)

        gw_ref[...] += gw

    return pl.pallas_call(
        body, name="gdn_out_bwd", grid=(HEADS,),
        in_specs=[pl.BlockSpec((n, DH), lambda h: (0, h)), pl.BlockSpec((n, DH), lambda h: (0, ZB + h)),
                  pl.BlockSpec((1, DH), lambda h: (0, 0)), pl.BlockSpec((n, d_model), lambda h: (0, 0)),
                  pl.BlockSpec((DH, d_model), lambda h: (h, 0))],
        out_specs=[pl.BlockSpec((n, DH), lambda h: (0, h)), pl.BlockSpec((n, DH), lambda h: (0, ZB + h)),
                   pl.BlockSpec((1, DH), lambda h: (0, 0))],
        out_shape=[jax.ShapeDtypeStruct((n, GW), F32), jax.ShapeDtypeStruct((n, GW_COLS), BF16),
                   jax.ShapeDtypeStruct((1, DH), F32)],
        compiler_params=_params(("arbitrary",), 40 * 2**20),
    )(o, proj, wg, dout_b, w_out)


def _conv_branch(proj, w3, b, mix):
    n = proj.shape[0]

    def body(p4, w_ref, b_ref, _, y_ref):
        u = p4[:, DH:2 * DH] * p4[:, 2 * DH:3 * DH]
        cc = _conv_silu(u, w_ref, 3) + b_ref[...]
        z = p4[:, 3 * DH:4 * DH]
        y_ref[...] = (p4[:, 0:DH] * cc * (z * _sigmoid(z))).astype(BF16)

    return pl.pallas_call(
        body, name="conv_branch", grid=(HEADS,),
        in_specs=[pl.BlockSpec((n, 4 * DH), lambda h: (0, h)), pl.BlockSpec((3, DH), lambda h: (0, h)),
                  pl.BlockSpec((1, DH), lambda h: (0, h)), ANY],
        out_specs=pl.BlockSpec((n, DH), lambda h: (0, HEADS + h)),
        out_shape=jax.ShapeDtypeStruct(mix.shape, BF16),
        input_output_aliases={3: 0},
        compiler_params=_params(("parallel",), 40 * 2**20),
    )(*_in_hbm(proj, w3, b, mix))


def _conv_branch_bwd(proj, w3, b, dout_b, w_out):
    n = proj.shape[0]
    d_model = dout_b.shape[1]

    def body(p4, w_ref, b_ref, g_ref, wo_ref, o4, gw_ref, gbias_ref):
        gb, gcv, hc, z = p4[:, 0:DH], p4[:, DH:2 * DH], p4[:, 2 * DH:3 * DH], p4[:, 3 * DH:4 * DH]
        d = _dot(g_ref[...], wo_ref[...], NT)
        dgb, dgc, dhc, dzc = (o4.at[:, kk * DH:(kk + 1) * DH] for kk in range(4))
        u = gcv * hc
        cc = _conv_silu(u, w_ref, 3) + b_ref[...]
        s = _sigmoid(z)
        dzc[...] = (d * (gb * cc) * _dsilu(z, s)).astype(BF16)
        dp = d * (z * s)
        dgb[...] = (dp * cc).astype(BF16)
        dcc = dp * gb
        gbias_ref[...] = jnp.sum(dcc, axis=0, keepdims=True)
        du = None
        for j in range(3):
            gw_ref[j:j + 1, :] = jnp.sum(dcc * _shift_down(u, 2 - j), axis=0, keepdims=True)
            t = _shift_up(dcc, 2 - j) * w_ref[j:j + 1, :]
            du = t if du is None else du + t
        dgc[...] = (du * hc).astype(BF16)
        dhc[...] = (du * gcv).astype(BF16)

    p4spec = pl.BlockSpec((n, 4 * DH), lambda h: (0, h))
    return pl.pallas_call(
        body, name="conv_branch_bwd", grid=(HEADS,),
        in_specs=[p4spec, pl.BlockSpec((3, DH), lambda h: (0, h)), pl.BlockSpec((1, DH), lambda h: (0, h)),
                  pl.BlockSpec((n, d_model), lambda h: (0, 0)), pl.BlockSpec((DH, d_model), lambda h: (HEADS + h, 0))],
        out_specs=[p4spec, pl.BlockSpec((3, DH), lambda h: (0, h)), pl.BlockSpec((1, DH), lambda h: (0, h))],
        out_shape=[jax.ShapeDtypeStruct((n, CW_COLS), BF16), jax.ShapeDtypeStruct((3, GW), F32),
                   jax.ShapeDtypeStruct((1, GW), F32)],
        compiler_params=_params(("parallel",), 52 * 2**20),
    )(proj, w3, b, dout_b, w_out)


def _out_loss(mix, w_out, x, tgt, wf):
    n, d = x.shape
    kdim = mix.shape[1]
    tr = min(256, n)

    def body(m_ref, wo_ref, x_ref, t_ref, w_ref, do_ref, dob_ref, gw_ref, loss_ref):
        ov = _dot(m_ref[...], wo_ref[...], NN) + x_ref[...]
        w = w_ref[...]
        r = lax.rsqrt(jnp.mean(ov * ov, axis=-1, keepdims=True) + EPS)
        nrm = ov * r
        e = nrm * w - t_ref[...]
        dy = e * (1.0 / d)
        dn = dy * w
        dout = r * (dn - nrm * jnp.mean(dn * nrm, axis=-1, keepdims=True))
        do_ref[...] = dout
        dob_ref[...] = dout.astype(BF16)

        @pl.when(pl.program_id(0) == 0)
        def _():
            gw_ref[...] = jnp.zeros_like(gw_ref)
            loss_ref[...] = jnp.zeros_like(loss_ref)

        gw_ref[...] += jnp.sum(dy * nrm, axis=0, keepdims=True)
        loss_ref[...] += (0.5 / d) * jnp.sum(jnp.sum(e * e, axis=-1, keepdims=True), axis=0, keepdims=True)

    row = pl.BlockSpec((tr, d), lambda i: (i, 0))
    return pl.pallas_call(
        body, name="out_loss", grid=(n // tr,),
        in_specs=[pl.BlockSpec((tr, kdim), lambda i: (i, 0)), pl.BlockSpec((kdim, d), lambda i: (0, 0)), row, row,
                  pl.BlockSpec((1, d), lambda i: (0, 0))],
        out_specs=[row, row, pl.BlockSpec((1, d), lambda i: (0, 0)), pl.BlockSpec((1, 1), lambda i: (0, 0))],
        out_shape=[jax.ShapeDtypeStruct((n, d), F32), jax.ShapeDtypeStruct((n, d), BF16),
                   jax.ShapeDtypeStruct((1, d), F32), jax.ShapeDtypeStruct((1, 1), F32)],
        compiler_params=_params(("arbitrary",), 40 * 2**20),
    )(mix, w_out, x, tgt, wf)


def _dh_rms_bwd(dproj, w_t, dh0, x, w, dout, tk):
    n, d = x.shape
    kdim = dproj.shape[1]
    tm = min(512, n)
    tk = min(tk, kdim)
    nk = kdim // tk

    def body(a_ref, b_ref, dh0_ref, x_ref, w_ref, do_ref, dx_ref, gw_ref, acc):
        i, kk = pl.program_id(0), pl.program_id(1)
        part = _dot(a_ref[...], b_ref[...], NN)

        @pl.when(kk == 0)
        def _():
            acc[...] = part + dh0_ref[...]

        @pl.when(kk > 0)
        def _():
            acc[...] += part

        @pl.when((i == 0) & (kk == 0))
        def _():
            gw_ref[...] = jnp.zeros_like(gw_ref)

        @pl.when(kk == nk - 1)
        def _():
            xv, dhv = x_ref[...], acc[...]
            r = lax.rsqrt(jnp.mean(xv * xv, axis=-1, keepdims=True) + EPS)
            xn = xv * r
            dxn = dhv * w_ref[...]
            dx_ref[...] = r * (dxn - xn * jnp.mean(dxn * xn, axis=-1, keepdims=True)) + do_ref[...]
            gw_ref[...] += jnp.sum(dhv * xn, axis=0, keepdims=True)

    row = pl.BlockSpec((tm, d), lambda i, kk: (i, 0))
    one = pl.BlockSpec((1, d), lambda i, kk: (0, 0))
    return pl.pallas_call(
        body, name="dh_rms_bwd", grid=(n // tm, nk),
        in_specs=[pl.BlockSpec((tm, tk), lambda i, kk: (i, kk)), pl.BlockSpec((tk, d), lambda i, kk: (kk, 0)),
                  row, row, one, row],
        out_specs=[row, one],
        out_shape=[jax.ShapeDtypeStruct((n, d), F32), jax.ShapeDtypeStruct((1, d), F32)],
        scratch_shapes=[pltpu.VMEM((tm, d), F32)],
        compiler_params=_params(("arbitrary", "arbitrary"), 48 * 2**20),
    )(dproj, w_t, dh0, x, w, dout)


def _ij():
    i = lax.broadcasted_iota(jnp.int32, (CH, CH), 0)
    j = lax.broadcasted_iota(jnp.int32, (CH, CH), 1)
    return i, j


def _unit_lower_inverse(mats):
    i, j = _ij()
    eye = jnp.where(i == j, 1.0, 0.0)
    same16 = (i // 16) == (j // 16)
    same32 = (i // 32) == (j // 32)
    mm = lambda xs, ys: [_dot(x, y, NN, P_INV) for x, y in zip(xs, ys)]
    n1 = [jnp.where(same16, -a, 0.0) for a in mats]
    n2 = mm(n1, n1)
    n4 = mm(n2, n2)
    n8 = mm(n4, n4)
    t = [eye + x1 + x2 + x3 for x1, x2, x3 in zip(n1, n2, mm(n1, n2))]
    t = [x + y for x, y in zip(t, mm(t, n4))]
    t = [x + y for x, y in zip(t, mm(t, n8))]
    a1 = [jnp.where(same32 & jnp.logical_not(same16), a, 0.0) for a in mats]
    t = [x - y for x, y in zip(t, mm(t, mm(a1, t)))]
    a2 = [jnp.where(same32, 0.0, a) for a in mats]
    t = [x - y for x, y in zip(t, mm(t, mm(a2, t)))]
    return t


def _head_vectors(bg, bgt, h):
    bcol = bg[:, h:h + 1]
    gcol = bg[:, HEADS + h:HEADS + h + 1]
    grow = bgt[HEADS + h:HEADS + h + 1, :]
    return bcol, gcol, grow


def _decay(gcol, grow):
    i, j = _ij()
    return jnp.where(i >= j, jnp.exp(jnp.where(i >= j, gcol - grow, 0.0)), 0.0)


def _gdn_intra(q, k, v, bg, bgt):
    n = q.shape[0]
    nch = n // CH
    cps = 4 if nch % 4 == 0 else 1

    def body(q_ref, k_ref, v_ref, bg_ref, bgt_ref, u_ref, w_ref, p_ref, t_ref):
        i, j = _ij()
        items = [(ci, h) for ci in range(cps) for h in range(HEADS)]
        at = lambda ref, ci, h: ref.at[ci * CH:(ci + 1) * CH, h * DH:(h + 1) * DH]
        bgs = [bg_ref[ci * CH:(ci + 1) * CH, :] for ci in range(cps)]
        ks = [at(k_ref, ci, h)[...] for ci, h in items]
        vecs = [_head_vectors(bgs[ci], bgt_ref[ci], h) for ci, h in items]
        decs = [_decay(gcol, grow) for _, gcol, grow in vecs]
        kks = [_dot(kh, kh, NT, P_GRAM) for kh in ks]
        qks = [_dot(at(q_ref, ci, h)[...], kh, NT, P_GRAM) for (ci, h), kh in zip(items, ks)]
        ts = _unit_lower_inverse([jnp.where(i > j, bcol * kk * dec, 0.0)
                                  for (bcol, _, _), kk, dec in zip(vecs, kks, decs)])
        us = [_dot(t, at(v_ref, ci, h)[...] * bcol, NN, P_SOL) for t, (ci, h), (bcol, _, _) in zip(ts, items, vecs)]
        ws = [_dot(t, kh * (bcol * jnp.exp(gcol)), NN, P_SOL) for t, kh, (bcol, gcol, _) in zip(ts, ks, vecs)]
        for n_, (ci, h) in enumerate(items):
            p_ref[ci, h] = qks[n_] * decs[n_]
            t_ref[ci, h] = ts[n_].astype(BF16)
            at(u_ref, ci, h)[...] = us[n_]
            at(w_ref, ci, h)[...] = ws[n_].astype(BF16)

    row = pl.BlockSpec((cps * CH, GW), lambda c: (c, 0))
    sq = pl.BlockSpec((cps, HEADS, CH, CH), lambda c: (c, 0, 0, 0))
    big = jax.ShapeDtypeStruct((n, GW), F32)
    sqs = jax.ShapeDtypeStruct((nch, HEADS, CH, CH), F32)
    return pl.pallas_call(
        body, name="gdn_intra", grid=(nch // cps,),
        in_specs=[row, row, row, pl.BlockSpec((cps * CH, DH), lambda c: (c, 0)),
                  pl.BlockSpec((cps, DH, CH), lambda c: (c, 0, 0))],
        out_specs=[row, row, sq, sq],
        out_shape=[big, jax.ShapeDtypeStruct((n, GW), BF16), sqs, jax.ShapeDtypeStruct(sqs.shape, BF16)],
        compiler_params=_params(("parallel",)),
    )(q, k, v, bg, bgt)


def _gdn_scan(q, k, bg, u, w, p):
    n = q.shape[0]
    nch = n // CH
    cps = SCAN_CPS if nch % SCAN_CPS == 0 else 1

    def body(q_ref, k_ref, bg_ref, u_ref, w_ref, p_ref, o_ref, vn_ref, s_out, s_scr):
        @pl.when(pl.program_id(0) == 0)
        def _():
            s_scr[...] = jnp.zeros_like(s_scr)

        hs = range(HEADS)
        sls = [slice(h * DH, (h + 1) * DH) for h in hs]
        ss = [s_scr[h] for h in hs]
        for ci in range(cps):
            rs = slice(ci * CH, (ci + 1) * CH)
            bg = bg_ref[rs, :]
            gcols = [bg[:, HEADS + h:HEADS + h + 1] for h in hs]
            glasts = [g[CH - 1:CH, :] for g in gcols]
            wss = [_dot(w_ref[rs, sl], s, NN, P_SCAN) for sl, s in zip(sls, ss)]
            oqs = [_dot(q_ref[rs, sl] * jnp.exp(g), s, NN, P_SCAN) for sl, s, g in zip(sls, ss, gcols)]
            vns = [u_ref[rs, sl] - x for sl, x in zip(sls, wss)]
            ops = [_dot(p_ref[ci, h], vn, NN, P_SCAN) for h, vn in zip(hs, vns)]
            sns = [_dot(k_ref[rs, sl] * jnp.exp(gl - g), vn, TN, P_SCAN)
                   for sl, gl, g, vn in zip(sls, glasts, gcols, vns)]
            for h, sl in enumerate(sls):
                s_out[ci, :, sl] = ss[h].astype(BF16)
                vn_ref[rs, sl] = vns[h].astype(BF16)
                o_ref[rs, sl] = oqs[h] + ops[h]
            ss = [s * jnp.exp(gl) + sn for s, gl, sn in zip(ss, glasts, sns)]
        for h in hs:
            s_scr[h] = ss[h]

    row = pl.BlockSpec((cps * CH, GW), lambda c: (c, 0))
    big = jax.ShapeDtypeStruct((n, GW), F32)
    return pl.pallas_call(
        body, name="gdn_scan", grid=(nch // cps,),
        in_specs=[row, row, pl.BlockSpec((cps * CH, DH), lambda c: (c, 0)), row, row,
                  pl.BlockSpec((cps, HEADS, CH, CH), lambda c: (c, 0, 0, 0))],
        out_specs=[row, row, pl.BlockSpec((cps, DH, GW), lambda c: (c, 0, 0))],
        out_shape=[big, jax.ShapeDtypeStruct((n, GW), BF16), jax.ShapeDtypeStruct((nch, DH, GW), BF16)],
        scratch_shapes=[pltpu.VMEM((HEADS, DH, DH), F32)],
        compiler_params=_params(("arbitrary",)),
    )(q, k, bg, u, w, p)


def _gdn_scan_bwd(q, k, bg, w, p, vn, s_in, do):
    n = q.shape[0]
    nch = n // CH
    cps = SCAN_CPS if nch % SCAN_CPS == 0 else 1
    rev = lambda c: nch // cps - 1 - c

    def body(q_ref, k_ref, bg_ref, w_ref, p_ref, vn_ref, s_ref, do_ref,
             dqg_ref, dp_ref, du_ref, dw_ref, dks_ref, dgam_ref, ds_scr):
        @pl.when(pl.program_id(0) == 0)
        def _():
            ds_scr[...] = jnp.zeros_like(ds_scr)

        lane = _lane((1, DH))
        hs = range(HEADS)
        sls = [slice(h * DH, (h + 1) * DH) for h in hs]
        dss = [ds_scr[h] for h in hs]
        for ci in reversed(range(cps)):
            rs = slice(ci * CH, (ci + 1) * CH)
            bg = bg_ref[rs, :]
            gcols = [bg[:, HEADS + h:HEADS + h + 1] for h in hs]
            glasts = [g[CH - 1:CH, :] for g in gcols]
            ss = [s_ref[ci, :, sl] for sl in sls]
            dos = [do_ref[rs, sl] for sl in sls]
            vnl = [vn_ref[rs, sl] for sl in sls]
            dqgs = [_dot(d, s, NT, P_SCANB) for d, s in zip(dos, ss)]
            dps = [_dot(d, vn, NT, P_SCANB) for d, vn in zip(dos, vnl)]
            dvn1 = [_dot(p_ref[ci, h], d, TN, P_SCANB) for h, d in zip(hs, dos)]
            dvn2 = [_dot(k_ref[rs, sl] * jnp.exp(gl - g), ds, NN, P_SCANB)
                    for sl, gl, g, ds in zip(sls, glasts, gcols, dss)]
            dkss = [_dot(vn, ds, NT, P_SCANB) for vn, ds in zip(vnl, dss)]
            dsq = [_dot(q_ref[rs, sl] * jnp.exp(g), d, TN, P_SCANB) for sl, g, d in zip(sls, gcols, dos)]
            dvns = [a + b for a, b in zip(dvn1, dvn2)]
            dws = [_dot(dvn, s, NT, P_SCANB) for dvn, s in zip(dvns, ss)]
            dsw = [_dot(w_ref[rs, sl], dvn, TN, P_SCANB) for sl, dvn in zip(sls, dvns)]
            dgam = jnp.zeros((1, DH), F32)
            for h, sl in enumerate(sls):
                dqg_ref[rs, sl] = dqgs[h]
                dp_ref[ci, h] = dps[h]
                du_ref[rs, sl] = dvns[h].astype(BF16)
                dw_ref[rs, sl] = (-dws[h]).astype(BF16)
                dks_ref[rs, sl] = dkss[h]
                tot = jnp.sum(jnp.sum(dss[h] * ss[h], axis=-1, keepdims=True), axis=0, keepdims=True)
                dgam = dgam + jnp.where(lane == h, tot, 0.0)
            dgam_ref[ci] = jnp.broadcast_to(dgam, (8, DH))
            dss = [ds * jnp.exp(gl) + a - b for ds, gl, a, b in zip(dss, glasts, dsq, dsw)]
        for h in hs:
            ds_scr[h] = dss[h]

    row = pl.BlockSpec((cps * CH, GW), lambda c: (rev(c), 0))
    sq = pl.BlockSpec((cps, HEADS, CH, CH), lambda c: (rev(c), 0, 0, 0))
    big = jax.ShapeDtypeStruct((n, GW), F32)
    return pl.pallas_call(
        body, name="gdn_scan_bwd", grid=(nch // cps,),
        in_specs=[row, row, pl.BlockSpec((cps * CH, DH), lambda c: (rev(c), 0)), row, sq, row,
                  pl.BlockSpec((cps, DH, GW), lambda c: (rev(c), 0, 0)), row],
        out_specs=[row, sq, row, row, row, pl.BlockSpec((cps, 8, DH), lambda c: (rev(c), 0, 0))],
        out_shape=[big, jax.ShapeDtypeStruct((nch, HEADS, CH, CH), F32), jax.ShapeDtypeStruct((n, GW), BF16),
                   jax.ShapeDtypeStruct((n, GW), BF16), big,
                   jax.ShapeDtypeStruct((nch, 8, DH), F32)],
        scratch_shapes=[pltpu.VMEM((HEADS, DH, DH), F32)],
        compiler_params=_params(("arbitrary",)),
    )(q, k, bg, w, p, vn, s_in, do)


def _gdn_intra_bwd(q, k, v, bg, bgt, t, u, w, p, dqg, dp, du, dw, dks, dgam):
    n = q.shape[0]
    nch = n // CH
    cps = 2 if nch % 2 == 0 else 1

    def body(q_ref, k_ref, v_ref, bg_ref, bgt_ref, t_ref, u_ref, w_ref, p_ref,
             dqg_ref, dp_ref, du_ref, dw_ref, dks_ref, dgam_ref, dq_ref, dk_ref, dv_ref, dbg_ref):
        i, j = _ij()
        rows1 = lax.broadcasted_iota(jnp.int32, (CH, 1), 0)
        lane = _lane((CH, DH))
        rsum = lambda x: jnp.sum(x, axis=-1, keepdims=True)
        items = [(ci, h) for ci in range(cps) for h in range(HEADS)]
        at = lambda ref, it: ref.at[it[0] * CH:(it[0] + 1) * CH, it[1] * DH:(it[1] + 1) * DH]
        ld = lambda ref: [at(ref, it)[...] for it in items]
        bgs = [bg_ref[ci * CH:(ci + 1) * CH, :] for ci in range(cps)]
        qs, ks = ld(q_ref), ld(k_ref)
        vecs = [_head_vectors(bgs[ci], bgt_ref[ci], h) for ci, h in items]
        decs = [_decay(gcol, grow) for _, gcol, grow in vecs]
        ths = [t_ref[ci, h] for ci, h in items]
        drus = [_dot(th, x_, TN, P_BWD) for th, x_ in zip(ths, ld(du_ref))]
        drws = [_dot(th, x_, TN, P_BWD) for th, x_ in zip(ths, ld(dw_ref))]
        kks = [_dot(kh, kh, NT, P_GRAM) for kh in ks]
        da1 = [_dot(dru, x_, NT, P_BWD) for dru, x_ in zip(drus, ld(u_ref))]
        da2 = [_dot(drw, x_, NT, P_BWD) for drw, x_ in zip(drws, ld(w_ref))]
        das = [jnp.where(i > j, -(x_ + y_), 0.0) for x_, y_ in zip(da1, da2)]
        dkks = [da * bcol * dec for da, (bcol, _, _), dec in zip(das, vecs, decs)]
        dps = [dp_ref[ci, h] for ci, h in items]
        dqks = [dp_ * dec for dp_, dec in zip(dps, decs)]
        dq_ps = [_dot(dqk, kh, NN, P_BWD) for dqk, kh in zip(dqks, ks)]
        dk_ps = [_dot(dqk, qh, TN, P_BWD) for dqk, qh in zip(dqks, qs)]
        dk_as = [_dot(dkk, kh, NN, P_BWD) for dkk, kh in zip(dkks, ks)]
        dk_bs = [_dot(dkk, kh, TN, P_BWD) for dkk, kh in zip(dkks, ks)]
        bcols = [vc[0] for vc in vecs]
        gcols = [vc[1] for vc in vecs]
        gams = [jnp.exp(g) for g in gcols]
        glasts = [g[CH - 1:CH, :] for g in gcols]
        es = [jnp.exp(gl - g) for gl, g in zip(glasts, gcols)]
        kgs = [kh * gam for kh, gam in zip(ks, gams)]
        dqgs, dkss = ld(dqg_ref), ld(dks_ref)
        wks = [drw * kg for drw, kg in zip(drws, kgs)]
        kss = [dk_ * (kh * e) for dk_, kh, e in zip(dkss, ks, es)]
        r_beta = [rsum(dru * x_ + wk) for dru, x_, wk in zip(drus, ld(v_ref), wks)]
        r_ak = [rsum(da * kk * dec) for da, kk, dec in zip(das, kks, decs)]
        r_gc = [rsum(wk * bcol + dqg * (qh * gam) - ks_)
                for wk, bcol, dqg, qh, gam, ks_ in zip(wks, bcols, dqgs, qs, gams, kss)]
        tk_tot = [jnp.sum(jnp.sum(ks_, axis=0, keepdims=True), axis=-1, keepdims=True) for ks_ in kss]
        mdecs = [da * (bcol * kk * dec) + dp_ * p_ref[ci, h]
                 for (ci, h), da, bcol, kk, dec, dp_ in zip(items, das, bcols, kks, decs, dps)]
        r_md = [rsum(m) for m in mdecs]
        c_md = [rsum(jnp.where(i == j, jnp.sum(m, axis=0, keepdims=True), 0.0)) for m in mdecs]
        dbgs = [jnp.zeros((CH, DH), F32) for _ in range(cps)]
        for n_, (ci, h) in enumerate(items):
            at(dv_ref, (ci, h))[...] = bcols[n_] * drus[n_]
            at(dq_ref, (ci, h))[...] = gams[n_] * dqgs[n_] + dq_ps[n_]
            at(dk_ref, (ci, h))[...] = ((bcols[n_] * gams[n_]) * drws[n_] + dk_ps[n_] + dk_as[n_] + dk_bs[n_]
                                        + dkss[n_] * es[n_])
            dbeta = r_beta[n_] + r_ak[n_]
            dglast = tk_tot[n_] + dgam_ref[ci, 0:1, h:h + 1] * jnp.exp(glasts[n_])
            dgc = r_gc[n_] + r_md[n_] - c_md[n_] + jnp.where(rows1 == CH - 1, dglast, 0.0)
            dbgs[ci] = dbgs[ci] + jnp.where(lane == h, dbeta, 0.0) + jnp.where(lane == HEADS + h, dgc, 0.0)
        for ci in range(cps):
            dbg_ref[ci * CH:(ci + 1) * CH, :] = dbgs[ci]

    row = pl.BlockSpec((cps * CH, GW), lambda c: (c, 0))
    sq = pl.BlockSpec((cps, HEADS, CH, CH), lambda c: (c, 0, 0, 0))
    small = pl.BlockSpec((cps * CH, DH), lambda c: (c, 0))
    big = jax.ShapeDtypeStruct((n, GW), F32)
    return pl.pallas_call(
        body, name="gdn_intra_bwd", grid=(nch // cps,),
        in_specs=[row, row, row, small, pl.BlockSpec((cps, DH, CH), lambda c: (c, 0, 0)), sq, row, row, sq,
                  row, sq, row, row, row, pl.BlockSpec((cps, 8, DH), lambda c: (c, 0, 0))],
        out_specs=[row, row, row, small],
        out_shape=[big, big, big, jax.ShapeDtypeStruct((n, DH), F32)],
        compiler_params=_params(("parallel",)),
    )(q, k, v, bg, bgt, t, u, w, p, dqg, dp, du, dw, dks, dgam)


def _local_step(x, tgt, h, w_g, cqw, late, norm_in_w, ad, gdn_norm_w, conv_b, final_norm_w,
                on_grad_c=None, on_grad_g=None, on_q=None):
    proj_g = _matmul(h, w_g, NT, F32, 512, 1408, 1024, "mm_proj_g", n=GW_COLS, b_outer=True)
    q, k, v = _prep_qkv(proj_g, cqw)
    if on_q is not None:
        q = on_q(q)
    bg, bgt = _prep_bg(proj_g, ad)
    u, w, p, t = _gdn_intra(q, k, v, bg, bgt)
    o, vn, s_in = _gdn_scan(q, k, bg, u, w, p)
    w_c, w_out, conv_w = late(o)
    proj_c = _matmul(h, w_c, NT, F32, 512, 1024, 1024, "mm_proj_c", n=CW_COLS, b_outer=True)
    mix = _conv_branch(proj_c, conv_w, conv_b, _gdn_out(o, proj_g, gdn_norm_w))
    dout, dout_b, g_fn, loss = _out_loss(mix, w_out, x, tgt, final_norm_w)

    g_wout = _matmul(mix, dout_b, TN, BF16, 512, 512, 2048, "mm_gwout")
    do, dproj_g, g_gn = _gdn_out_bwd(o, proj_g, gdn_norm_w, dout_b, w_out)
    dproj_c, g_cw, g_cb = _conv_branch_bwd(proj_c, conv_w, conv_b, dout_b, w_out)
    g_c = _matmul(dproj_c, h, TN, BF16, 1024, 512, 2048, "mm_gwin_c")
    if on_grad_c is not None:
        do = on_grad_c(g_c, g_wout, do)
    dqg, dp, du, dw, dks, dgam = _gdn_scan_bwd(q, k, bg, w, p, vn, s_in, do)
    dq, dk, dv, dbg = _gdn_intra_bwd(q, k, v, bg, bgt, t, u, w, p, dqg, dp, du, dw, dks, dgam)
    dproj_g, gq, gk, gv = _prep_qkv_bwd(proj_g, cqw, dq, dk, dv, dproj_g)
    dproj_g, g_al, g_dt = _prep_bg_bwd(proj_g, ad, dbg, dproj_g)
    g_g = _matmul(dproj_g, h, TN, BF16, 1408, 512, 2048, "mm_gwin_g")
    if on_grad_g is not None:
        dproj_g = on_grad_g(g_g, dproj_g)
    dh = _matmul(dproj_g, w_g, NN, F32, 1024, 1024, 1408, "mm_dh_g")
    gx, g_nin = _dh_rms_bwd(dproj_c, w_c, dh, x, norm_in_w, dout, 1024)
    small = dict(nin=g_nin, cb=g_cb, fn=g_fn, al=g_al, dt=g_dt, gn=g_gn, cq=(gq, gk, gv), cw=g_cw, loss=loss)
    return gx, small, (g_g, g_c, g_wout)


def _place():
    x, y, c = lax.axis_index("x"), lax.axis_index("y"), lax.axis_index("c")
    chips = [(1 - x, y), (x, 1 - y), (1 - x, 1 - y)]
    return x, y, c, chips


def _blk(ref, b):
    if isinstance(b, int):
        return ref.at[b * DH:(b + 1) * DH, :]
    return ref.at[pl.ds(pl.multiple_of(b * DH, DH), DH), :]


HBM = pl.BlockSpec(memory_space=pltpu.HBM)
SEM = pl.BlockSpec(memory_space=pltpu.SEMAPHORE)
EFFECT = pltpu.SideEffectType.DATAFLOW_SIDE_EFFECTING


def _split_start(name, issue, bufs, n_sems):
    nbuf = len(bufs)

    def body(*refs):
        issue(refs[:nbuf], refs[nbuf], refs[nbuf + 1])
        refs[-1][...] = jnp.zeros_like(refs[-1])

    out = pl.pallas_call(
        body, name=name,
        out_shape=(pltpu.SemaphoreType.DMA((n_sems,)), pltpu.SemaphoreType.DMA((n_sems,)),
                   *[pltpu.HBM(b.shape, b.dtype) for b in bufs], jax.ShapeDtypeStruct((8, DH), F32)),
        in_specs=[HBM] * nbuf,
        out_specs=(SEM, SEM, *[HBM] * nbuf, pl.BlockSpec(memory_space=pltpu.VMEM)),
        input_output_aliases={a: 2 + a for a in range(nbuf)},
        compiler_params=pltpu.CompilerParams(has_side_effects=EFFECT),
    )(*[pltpu.with_memory_space_constraint(b, pltpu.HBM) for b in bufs])
    return out[0], out[1], list(out[2:2 + nbuf]), out[-1]


def _split_wait(name, await_, send_sems, recv_sems, bufs, after):
    nbuf = len(bufs)
    after = list(after) if isinstance(after, (list, tuple)) else [after]

    def body(*refs):
        await_(refs[:nbuf], refs[nbuf], refs[nbuf + 1])

    out = pl.pallas_call(
        body, name=name,
        out_shape=tuple(pltpu.HBM(b.shape, b.dtype) for b in bufs),
        in_specs=[HBM] * nbuf + [SEM, SEM] + [ANY] * len(after), out_specs=tuple([HBM] * nbuf),
        input_output_aliases={a: a for a in range(nbuf)},
        compiler_params=pltpu.CompilerParams(has_side_effects=EFFECT),
    )(*bufs, send_sems, recv_sems, *after)
    return list(out)


def _phase_blocks(chip, phase, edges, parity=None):
    return [(b, blk) for b, (grp, blk) in enumerate(_shard_blocks(chip, edges))
            if grp == phase and (parity is None or b % 2 == parity)]


def _cols(ref, nblk):
    return ref.at[0:nblk * DH, :]


def _block_table(chip, edges, spare_g, spare_c):
    rows = []
    for s in range(4):
        sb = _shard_blocks(s, edges)
        rows.append([[blk if grp == "g" else spare_g for grp, blk in sb],
                     [blk if grp == "c" else spare_c for grp, blk in sb],
                     [int(grp == "g") for grp, _ in sb], [s] * ALIGNED_BLOCKS])
    return jnp.asarray(rows, jnp.int32)[chip]


def _place_own(a_shard, wo, cq, cw, bufs):
    d = a_shard.shape[1]
    chip = 2 * lax.axis_index("x") + lax.axis_index("y")

    def body(t_ref, a_ref, wo_ref, cq_ref, cw_ref, *refs):
        wg_ref, wc_ref, wog_ref, cqg_ref, cwg_ref = refs[5:]
        wg_ref[...] = a_ref[...]
        wc_ref[...] = a_ref[...]

        @pl.when(pl.program_id(0) == 0)
        def _():
            wog_ref[0] = wo_ref[...]
            cqg_ref[0] = cq_ref[...]
            cwg_ref[0] = cw_ref[...]

    whole = lambda s: pl.BlockSpec(s.shape, lambda b, t: (0,) * s.ndim)
    slot = lambda s: pl.BlockSpec((1,) + s.shape, lambda b, t: (t[3, 0],) + (0,) * s.ndim)
    return pl.pallas_call(
        body, name="place_own",
        grid_spec=pltpu.PrefetchScalarGridSpec(
            num_scalar_prefetch=1, grid=(ALIGNED_BLOCKS,),
            in_specs=[pl.BlockSpec((DH, d), lambda b, t: (b, 0)), whole(wo), whole(cq), whole(cw)] + [ANY] * 5,
            out_specs=[pl.BlockSpec((DH, d), lambda b, t: (t[0, b], 0)),
                       pl.BlockSpec((DH, d), lambda b, t: (t[1, b], 0)), slot(wo), slot(cq), slot(cw)]),
        out_shape=[jax.ShapeDtypeStruct(b.shape, b.dtype) for b in bufs],
        input_output_aliases={5 + a: a for a in range(5)},
        compiler_params=_params(("arbitrary",)),
    )(_block_table(chip, True, G_SPARE, C_SPARE), a_shard, wo, cq, cw, *bufs)


def _tie(x, token, name):
    def body(x_ref, t_ref, o_ref):
        del x_ref, t_ref, o_ref

    return pl.pallas_call(
        body, name=name, in_specs=[ANY, ANY], out_specs=ANY,
        out_shape=jax.ShapeDtypeStruct(x.shape, x.dtype), input_output_aliases={0: 0},
    )(x, token)


def _gather_start(phase, a_shard, w_grp, singles):
    ns = len(singles)

    def issue(refs, send_sems, recv_sems):
        a_ref, w_ref = refs[0], refs[1]
        x, y, c, chips = _place()
        mine = 2 * x + y
        for jj, (px, py) in enumerate(chips):
            to = dict(device_id=(px, py, c), device_id_type=MESH)
            for a in range(ns):
                pltpu.make_async_remote_copy(
                    src_ref=refs[2 + 2 * a], dst_ref=refs[3 + 2 * a].at[mine],
                    send_sem=send_sems.at[(1 + ns) * jj + 1 + a], recv_sem=recv_sems.at[(1 + ns) * jj + 1 + a],
                    **to).start()
        for s in range(4):
            for par in range(2):
                blocks = _phase_blocks(s, phase, True, par)
                if blocks:
                    @pl.when((mine == s) & (c == par))
                    def _():
                        for jj, (px, py) in enumerate(chips):
                            for b, blk in blocks:
                                pltpu.make_async_remote_copy(
                                    src_ref=_blk(a_ref, b), dst_ref=_blk(w_ref, blk),
                                    send_sem=send_sems.at[(1 + ns) * jj], recv_sem=recv_sems.at[(1 + ns) * jj],
                                    device_id=(px, py, c), device_id_type=MESH).start()

    bufs = [a_shard, w_grp] + [t for pair in singles for t in pair]
    return _split_start("gather_start_" + phase, issue, bufs, 3 * (1 + ns))


def _gather_wait(phase, send_sems, recv_sems, bufs, after):
    ns = (len(bufs) - 2) // 2

    def await_(refs, send_sems, recv_sems):
        a_ref, w_ref = refs[0], refs[1]
        x, y, c, chips = _place()
        mine = 2 * x + y
        for jj, (px, py) in enumerate(chips):
            to = dict(device_id=(px, py, c), device_id_type=MESH)
            peer = 2 * px + py
            for a in range(ns):
                cp = pltpu.make_async_remote_copy(
                    src_ref=refs[2 + 2 * a], dst_ref=refs[3 + 2 * a].at[mine],
                    send_sem=send_sems.at[(1 + ns) * jj + 1 + a], recv_sem=recv_sems.at[(1 + ns) * jj + 1 + a], **to)
                cp.wait_recv()
                cp.wait_send()
            for s in range(4):
                for par in range(2):
                    nblk = len(_phase_blocks(s, phase, True, par))
                    if nblk:
                        both = pltpu.make_async_remote_copy(
                            src_ref=_cols(a_ref, nblk), dst_ref=_cols(w_ref, nblk),
                            send_sem=send_sems.at[(1 + ns) * jj], recv_sem=recv_sems.at[(1 + ns) * jj], **to)

                        @pl.when((peer == s) & (c == par))
                        def _():
                            both.wait_recv()

                        @pl.when((mine == s) & (c == par))
                        def _():
                            both.wait_send()

    return _split_wait("gather_wait_" + phase, await_, send_sems, recv_sems, bufs, after)


def _sibling_forward_parts(phase):
    def each(w_ref, send_sems, recv_sems, start):
        x, y, c, chips = _place()
        to = dict(device_id=(x, y, 1 - c), device_id_type=MESH)
        for jj, (px, py) in enumerate(chips):
            peer = 2 * px + py
            for s in range(4):
                for par in range(2):
                    mine_blocks = _phase_blocks(s, phase, True, par)
                    theirs = len(_phase_blocks(s, phase, True, 1 - par))
                    if not (mine_blocks or theirs):
                        continue

                    @pl.when((peer == s) & (c == par))
                    def _():
                        if start:
                            for _, blk in mine_blocks:
                                pltpu.make_async_remote_copy(
                                    src_ref=_blk(w_ref, blk), dst_ref=_blk(w_ref, blk),
                                    send_sem=send_sems.at[jj], recv_sem=recv_sems.at[jj], **to).start()
                            return
                        if theirs:
                            pltpu.make_async_remote_copy(
                                src_ref=_cols(w_ref, theirs), dst_ref=_cols(w_ref, theirs),
                                send_sem=send_sems.at[jj], recv_sem=recv_sems.at[jj], **to).wait_recv()
                        if mine_blocks:
                            pltpu.make_async_remote_copy(
                                src_ref=_cols(w_ref, len(mine_blocks)), dst_ref=_cols(w_ref, len(mine_blocks)),
                                send_sem=send_sems.at[jj], recv_sem=recv_sems.at[jj], **to).wait_send()

    issue = lambda refs, send_sems, recv_sems: each(refs[0], send_sems, recv_sems, True)
    await_ = lambda refs, send_sems, recv_sems: each(refs[0], send_sems, recv_sems, False)
    return issue, await_


def _sibling_forward(phase, w_grp):
    issue, await_ = _sibling_forward_parts(phase)

    def body(w_in_ref, w_ref, send_sems, recv_sems):
        del w_in_ref
        issue([w_ref], send_sems, recv_sems)
        await_([w_ref], send_sems, recv_sems)

    return pl.pallas_call(
        body, name="sibling_forward_" + phase, in_specs=[ANY], out_specs=ANY,
        out_shape=jax.ShapeDtypeStruct(w_grp.shape, w_grp.dtype), input_output_aliases={0: 0},
        scratch_shapes=[pltpu.SemaphoreType.DMA((3,)), pltpu.SemaphoreType.DMA((3,))],
    )(w_grp)


def _merge_edges(w, edge0, mixed, name):
    d = w.shape[1]

    def body(e_ref, o_ref):
        o_ref[...] = e_ref[0:DH, :] + e_ref[DH:2 * DH, :]

    def to_block(i):
        r = mixed[-1]
        for kk in range(len(mixed) - 2, -1, -1):
            r = jnp.where(i == kk, mixed[kk], r)
        return r

    return pl.pallas_call(
        body, name=name, grid=(len(mixed),),
        in_specs=[pl.BlockSpec((2 * DH, d), lambda i: (edge0 // 2 + i, 0))],
        out_specs=pl.BlockSpec((DH, d), lambda i: (to_block(i), 0)),
        out_shape=jax.ShapeDtypeStruct(w.shape, w.dtype),
        input_output_aliases={0: 0},
        compiler_params=_params(("arbitrary",)),
    )(w)


def _scatter_start(phase, g_grp, land, singles, halved=False):
    ns = len(singles)

    def issue(refs, send_sems, recv_sems):
        g_ref, land_ref = refs[0], refs[1]
        x, y, c, chips = _place()
        for jj, (px, py) in enumerate(chips):
            to = dict(device_id=(px, py, c), device_id_type=MESH)
            peer = 2 * px + py
            for a in range(ns):
                pltpu.make_async_remote_copy(
                    src_ref=refs[2 + 2 * a].at[peer], dst_ref=refs[3 + 2 * a].at[jj],
                    send_sem=send_sems.at[(1 + ns) * jj + 1 + a], recv_sem=recv_sems.at[(1 + ns) * jj + 1 + a],
                    **to).start()
            for s in range(4):
                for par in ((0, 1) if halved else (None,)):
                    blocks = _phase_blocks(s, phase, False, par)
                    if blocks:
                        @pl.when((peer == s) if par is None else ((peer == s) & (c == par)))
                        def _():
                            for b, blk in blocks:
                                pltpu.make_async_remote_copy(
                                    src_ref=_blk(g_ref, blk), dst_ref=_blk(land_ref.at[jj], b),
                                    send_sem=send_sems.at[(1 + ns) * jj], recv_sem=recv_sems.at[(1 + ns) * jj],
                                    **to).start()

    bufs = [g_grp, land] + [t for pair in singles for t in pair]
    return _split_start("scatter_start_" + phase, issue, bufs, 3 * (1 + ns))


def _scatter_wait(phase, send_sems, recv_sems, bufs, after, halved=False):
    ns = (len(bufs) - 2) // 2

    def await_(refs, send_sems, recv_sems):
        g_ref, land_ref = refs[0], refs[1]
        x, y, c, chips = _place()
        mine = 2 * x + y
        for jj, (px, py) in enumerate(chips):
            to = dict(device_id=(px, py, c), device_id_type=MESH)
            peer = 2 * px + py
            for a in range(ns):
                cp = pltpu.make_async_remote_copy(
                    src_ref=refs[2 + 2 * a].at[peer], dst_ref=refs[3 + 2 * a].at[jj],
                    send_sem=send_sems.at[(1 + ns) * jj + 1 + a], recv_sem=recv_sems.at[(1 + ns) * jj + 1 + a], **to)
                cp.wait_recv()
                cp.wait_send()
            for s in range(4):
                for par in ((0, 1) if halved else (None,)):
                    nblk = len(_phase_blocks(s, phase, False, par))
                    if nblk:
                        both = pltpu.make_async_remote_copy(
                            src_ref=_cols(g_ref, nblk), dst_ref=_cols(land_ref.at[jj], nblk),
                            send_sem=send_sems.at[(1 + ns) * jj], recv_sem=recv_sems.at[(1 + ns) * jj], **to)

                        @pl.when((mine == s) if par is None else ((mine == s) & (c == par)))
                        def _():
                            both.wait_recv()

                        @pl.when((peer == s) if par is None else ((peer == s) & (c == par)))
                        def _():
                            both.wait_send()

    return _split_wait("scatter_wait_" + phase, await_, send_sems, recv_sems, bufs, after)


def _needed_blocks(phase, parity):
    return sorted({blk for s in range(4) for _, blk in _phase_blocks(s, phase, False, parity)})


def _pair_reduce(phase, g_grp):
    n, d = g_grp.shape

    def swap(g_ref, sib_ref, send_sem, recv_sem):
        x, y, c, _ = _place()
        to = dict(device_id=(x, y, 1 - c), device_id_type=MESH)
        for par in range(2):
            give, get = _needed_blocks(phase, 1 - par), _needed_blocks(phase, par)

            @pl.when(c == par)
            def _():
                for blk in give:
                    pltpu.make_async_remote_copy(src_ref=_blk(g_ref, blk), dst_ref=_blk(sib_ref, blk),
                                                 send_sem=send_sem, recv_sem=recv_sem, **to).start()
                pltpu.make_async_remote_copy(src_ref=_cols(g_ref, len(get)), dst_ref=_cols(sib_ref, len(get)),
                                             send_sem=send_sem, recv_sem=recv_sem, **to).wait_recv()
                pltpu.make_async_remote_copy(src_ref=_cols(g_ref, len(give)), dst_ref=_cols(sib_ref, len(give)),
                                             send_sem=send_sem, recv_sem=recv_sem, **to).wait_send()

    sib = pl.pallas_call(
        swap, name="pair_swap_" + phase, in_specs=[ANY], out_specs=ANY,
        out_shape=jax.ShapeDtypeStruct((n, d), g_grp.dtype),
        scratch_shapes=[pltpu.SemaphoreType.DMA, pltpu.SemaphoreType.DMA],
    )(*_in_hbm(g_grp))

    lists = [_needed_blocks(phase, par) for par in range(2)]
    longest = max(len(t) for t in lists)
    table = jnp.asarray([t + [t[-1]] * (longest - len(t)) for t in lists], jnp.int32)[lax.axis_index("c")]

    def add(t_ref, a_ref, b_ref, o_ref):
        o_ref[...] = (a_ref[...].astype(F32) + b_ref[...].astype(F32)).astype(o_ref.dtype)

    blk = pl.BlockSpec((DH, d), lambda i, t: (t[i], 0))
    return pl.pallas_call(
        add, name="pair_add_" + phase,
        grid_spec=pltpu.PrefetchScalarGridSpec(num_scalar_prefetch=1, grid=(longest,),
                                               in_specs=[blk, blk], out_specs=blk),
        out_shape=jax.ShapeDtypeStruct((n, d), g_grp.dtype),
        compiler_params=_params(("arbitrary",)),
    )(table, g_grp, sib)


def _sum_shard(g_g, g_c, land):
    d = g_g.shape[1]
    chip = 2 * lax.axis_index("x") + lax.axis_index("y")

    def body(t_ref, gg_ref, gc_ref, land_ref, o_ref):
        b = pl.program_id(0)
        in_g = t_ref[2, b] == 1
        own = jnp.where(in_g, gg_ref[...].astype(F32), gc_ref[...].astype(F32))
        for jj in range(3):
            own = own + land_ref[jj].astype(F32)
        o_ref[...] = jnp.where(in_g & (b % 2 != lax.axis_index("c")), 0.0, own)

    return pl.pallas_call(
        body, name="sum_w_in",
        grid_spec=pltpu.PrefetchScalarGridSpec(
            num_scalar_prefetch=1, grid=(ALIGNED_BLOCKS,),
            in_specs=[pl.BlockSpec((DH, d), lambda b, t: (t[0, b], 0)), pl.BlockSpec((DH, d), lambda b, t: (t[1, b], 0)),
                      pl.BlockSpec((3, DH, d), lambda b, t: (0, b, 0))],
            out_specs=pl.BlockSpec((DH, d), lambda b, t: (b, 0))),
        out_shape=jax.ShapeDtypeStruct((ALIGNED_W, d), F32),
        compiler_params=_params(("arbitrary",)),
    )(_block_table(chip, False, 0, 0), g_g, g_c, land)


def _sum_rows(stack, land, rows):
    _, r, d = stack.shape
    rows = min(rows, r)
    chip = 2 * lax.axis_index("x") + lax.axis_index("y")

    def body(t_ref, own_ref, land_ref, o_ref):
        acc = own_ref[0].astype(F32)
        for jj in range(3):
            acc = acc + land_ref[jj].astype(F32)
        o_ref[...] = acc

    return pl.pallas_call(
        body, name="sum_w_out",
        grid_spec=pltpu.PrefetchScalarGridSpec(
            num_scalar_prefetch=1, grid=(r // rows,),
            in_specs=[pl.BlockSpec((1, rows, d), lambda i, t: (t[0], i, 0)),
                      pl.BlockSpec((3, rows, d), lambda i, t: (0, i, 0))],
            out_specs=pl.BlockSpec((rows, d), lambda i, t: (i, 0))),
        out_shape=jax.ShapeDtypeStruct((r, d), F32),
        compiler_params=_params(("arbitrary",)),
    )(jnp.reshape(chip, (1,)).astype(jnp.int32), stack, land)


def _exchange_parts(n_swap, with_pack):
    def copies(refs, send_sems, recv_sems):
        x, y, c, _ = _place()
        me = 4 * x + 2 * y + c
        cps = [pltpu.make_async_remote_copy(
            src_ref=refs[2 * a], dst_ref=refs[2 * a + 1], send_sem=send_sems.at[a], recv_sem=recv_sems.at[a],
            device_id=(x, y, 1 - c), device_id_type=MESH) for a in range(n_swap)]
        if with_pack:
            pack_ref, packs = refs[2 * n_swap], refs[2 * n_swap + 1]
            for r in range(1, 8):
                dx, dy, dc = (r >> 2) & 1, (r >> 1) & 1, r & 1
                peer = (x + dx - 2 * x * dx, y + dy - 2 * y * dy, c + dc - 2 * c * dc)
                cps.append(pltpu.make_async_remote_copy(
                    src_ref=pack_ref, dst_ref=packs.at[me], send_sem=send_sems.at[n_swap + r - 1],
                    recv_sem=recv_sems.at[n_swap + r - 1], device_id=peer, device_id_type=MESH))
        return cps

    def issue(refs, send_sems, recv_sems):
        for cp in copies(refs, send_sems, recv_sems):
            cp.start()

    def await_(refs, send_sems, recv_sems):
        cps = copies(refs, send_sems, recv_sems)
        for cp in cps:
            cp.wait_recv()
        for cp in cps:
            cp.wait_send()

    return issue, await_, n_swap + (7 if with_pack else 0)


def _sum_packs(pack, packs):
    x, y, c = lax.axis_index("x"), lax.axis_index("y"), lax.axis_index("c")
    me = jnp.reshape(4 * x + 2 * y + c, (1,)).astype(jnp.int32)

    def body(me_ref, own_ref, p_ref, o_ref):
        acc = jnp.where(me_ref[0] == 0, own_ref[...], p_ref[0])
        for d in range(1, 8):
            acc = acc + jnp.where(me_ref[0] == d, own_ref[...], p_ref[d])
        o_ref[...] = acc

    full = lambda s: pl.BlockSpec(s.shape, lambda i, t: (0,) * s.ndim)
    return pl.pallas_call(
        body, name="sum_packs",
        grid_spec=pltpu.PrefetchScalarGridSpec(num_scalar_prefetch=1, grid=(1,), in_specs=[full(pack), full(packs)],
                                               out_specs=full(pack)),
        out_shape=jax.ShapeDtypeStruct(pack.shape, F32),
    )(me, pack, packs)


def _adamw_update(g, w_ref, m_ref, v_ref, go, do, mo, vo):
    c1 = 1.0 / (1.0 - ADAM_B1 ** ADAM_STEP)
    c2 = 1.0 / (1.0 - ADAM_B2 ** ADAM_STEP)
    mn = ADAM_B1 * m_ref[...] + (1.0 - ADAM_B1) * g
    vn = ADAM_B2 * v_ref[...] + (1.0 - ADAM_B2) * (g * g)
    go[...] = g
    mo[...] = mn
    vo[...] = vn
    do[...] = -ADAM_LR * ((mn * c1) / (jnp.sqrt(vn * c2) + ADAM_EPS) + ADAM_WD * w_ref[...])


def _adamw(w, m, v, g1, g2, rows, name):
    r, cdim = w.shape
    rows = min(rows, r)

    def body(*refs):
        n_in = 4 if g2 is None else 5
        w_ref, m_ref, v_ref, g_ref = refs[:4]
        g = g_ref[...] if g2 is None else g_ref[...] + refs[4][...]
        _adamw_update(g, w_ref, m_ref, v_ref, *refs[n_in:n_in + 4])

    blk = pl.BlockSpec((rows, cdim), lambda i: (i, 0))
    args = [w, m, v, g1] + ([] if g2 is None else [g2])
    shp = jax.ShapeDtypeStruct((r, cdim), F32)
    return pl.pallas_call(
        body, name=name, grid=(r // rows,),
        in_specs=[blk] * len(args), out_specs=[blk] * 4, out_shape=[shp] * 4,
        compiler_params=_params(("parallel",), 20 * rows * cdim * 4 + 8 * 2**20),
    )(*_in_hbm(*args))


def _adamw_shard(wt, mt, vt, g1, g2):
    r, d = wt.shape
    cols = min(128, d)

    def body(w_ref, m_ref, v_ref, g_ref, g2_ref, go, do, mo, vo, pad_ref):
        chip = 2 * lax.axis_index("x") + lax.axis_index("y")
        back = [(ALIGNED_W - s) % ALIGNED_W for s in SHIFTS]
        pad_ref[...] = pltpu.roll(g_ref[...] + g2_ref[...], _by_chip(chip, back), 0)
        outs = [o.at[:, 0, :] for o in (go, do, mo, vo)]
        _adamw_update(pad_ref[0:r, :], w_ref, m_ref, v_ref, *outs)

    blk = pl.BlockSpec((r, cols), lambda i: (0, i))
    gblk = pl.BlockSpec((ALIGNED_W, cols), lambda i: (0, i))
    oblk = pl.BlockSpec((r, 1, cols), lambda i: (0, 0, i))
    shp = jax.ShapeDtypeStruct((r, 1, d), F32)
    return pl.pallas_call(
        body, name="adamw_w_in", grid=(d // cols,),
        in_specs=[blk] * 3 + [gblk] * 2, out_specs=[oblk] * 4, out_shape=[shp] * 4,
        scratch_shapes=[pltpu.VMEM((ALIGNED_W, cols), F32)],
        compiler_params=_params(("parallel",), 24 * ALIGNED_W * cols * 4 + 8 * 2**20),
    )(wt, mt, vt, g1, g2)


def _pad_lanes(a, width):
    return jnp.pad(a, ((0, 0), (0, width - a.shape[1])))


def _gathered_to_full(g):
    return jnp.transpose(g, (1, 0, 2)).reshape(g.shape[1], 4 * g.shape[2])


def _row(a):
    return _pad_lanes(a.reshape(1, -1), 1024)


def _small_pack(nin, cb, fn, al, dt, gn, cqw_shard, cw_shard):
    ad = jnp.concatenate([al.reshape(1, -1), dt.reshape(1, -1)], axis=1)
    rows = [_row(nin), _row(cb), _row(fn), _row(ad), _row(gn), cqw_shard.reshape(3, 1024), _row(cw_shard)]
    out = jnp.concatenate(rows, axis=0)
    return jnp.pad(out, ((0, 16 - out.shape[0]), (0, 0)))


def kernel(x, norm_in_w, w_in, conv_qkv_w, A_log, dt_bias, gdn_norm_w, conv_w, conv_b, w_out, final_norm_w, loss_target, m_norm_in_w, m_w_in, m_conv_qkv_w, m_A_log, m_dt_bias, m_gdn_norm_w, m_conv_w, m_conv_b, m_w_out, m_final_norm_w, v_norm_in_w, v_w_in, v_conv_qkv_w, v_A_log, v_dt_bias, v_gdn_norm_w, v_conv_w, v_conv_b, v_w_out, v_final_norm_w):
    chip = 2 * lax.axis_index("x") + lax.axis_index("y")
    a_shard = _align_shard(jnp.transpose(w_in, (2, 0, 1)))
    wo_b = _cast_bf16(w_out[0], 256, "cast_w_out")
    d_model = x.shape[-1]
    stack = lambda s: lax.empty((4,) + s.shape, s.dtype)
    wg0 = lax.empty((WG_BLOCKS * DH, d_model), BF16)
    wc0 = lax.empty((WC_BLOCKS * DH, d_model), BF16)
    ss_g, rs_g, bufs_g, tok_g = _gather_start("g", a_shard, wg0, [(conv_qkv_w[0], stack(conv_qkv_w[0]))])
    ss_c, rs_c, bufs_c, tok_c = _gather_start("c", bufs_g[0], wc0,
                                              [(conv_w[0], stack(conv_w[0])), (wo_b, stack(wo_b))])
    wg1, wc1, wog1, cqg1, cwg1 = _place_own(bufs_c[0], bufs_c[4], bufs_g[2], bufs_c[2],
                                            [bufs_g[1], bufs_c[1], bufs_c[5], bufs_g[3], bufs_c[3]])
    x0 = x[0]
    h = _rms_in(x0, _tie(_tie(norm_in_w, tok_g, "after_gather_start_g"), tok_c, "after_gather_start_c"))
    adam_in = [jnp.transpose(a[0]) for a in (w_in, m_w_in, v_w_in)]
    sp = lambda nin, cb, fn, al, dt, gn, cq, cwv: _small_pack(nin, cb, fn, al, dt, gn, cq[0], cwv[0])
    w_s = sp(norm_in_w, conv_b, final_norm_w, A_log, dt_bias, gdn_norm_w, conv_qkv_w, conv_w)
    m_s = sp(m_norm_in_w, m_conv_b, m_final_norm_w, m_A_log, m_dt_bias, m_gdn_norm_w, m_conv_qkv_w, m_conv_w)
    v_s = sp(v_norm_in_w, v_conv_b, v_final_norm_w, v_A_log, v_dt_bias, v_gdn_norm_w, v_conv_qkv_w, v_conv_w)
    a_thru, wg, _, cq_g = _gather_wait("g", ss_g, rs_g, [bufs_c[0], wg1, bufs_g[2], cqg1],
                                       [h, w_s, m_s, v_s] + adam_in[1:])
    w_g = _merge_edges(_sibling_forward("g", wg), G_EDGE, G_MIXED, "merge_edges_g")
    cqw = _gathered_to_full(cq_g)
    ad = jnp.pad(jnp.concatenate([A_log, dt_bias], axis=0), ((0, 0), (A_LANE, 0)))
    fwd_c = {}

    def on_q(q):
        _, wc, _, cw_g, _, wo_g = _gather_wait("c", ss_c, rs_c,
                                               [a_thru, wc1, bufs_c[2], cwg1, bufs_c[4], wog1], q)
        issue, _ = _sibling_forward_parts("c")
        ss, rs, (wc,), tok = _split_start("sibling_forward_start_c", issue, [wc], 3)
        fwd_c.update(ss=ss, rs=rs, wc=wc, cw_g=cw_g, wo_g=wo_g)
        return _tie(q, tok, "after_sibling_forward_start_c")

    def late(o):
        _, await_ = _sibling_forward_parts("c")
        (wc,) = _split_wait("sibling_forward_wait_c", await_, fwd_c["ss"], fwd_c["rs"], [fwd_c["wc"]], o)
        return (_merge_edges(wc, C_EDGE, C_MIXED, "merge_edges_c"), fwd_c["wo_g"].reshape(2 * GW, d_model),
                _gathered_to_full(fwd_c["cw_g"]))

    scat = {}

    def on_grad_c(g_c, g_wout, do):
        go4 = g_wout.reshape(4, GW // 2, d_model)
        land = lax.empty((3, ALIGNED_W, d_model), BF16)
        land_o = lax.empty((3, GW // 2, d_model), BF16)
        ss, rs, bufs, tok = _scatter_start("c", g_c, land, [(go4, land_o)])
        scat["c"] = (ss, rs, bufs)
        return _tie(do, tok, "after_scatter_start_c")

    def on_grad_g(g_g, dproj_g):
        ss, rs, bufs, tok = _scatter_start("g", _pair_reduce("g", g_g), scat["c"][2][1], [], halved=True)
        scat["g"] = (ss, rs, bufs)
        return _tie(dproj_g, tok, "after_scatter_start_g")

    gx, sm, _ = _local_step(x0, loss_target[0], h, w_g, cqw, late, norm_in_w, ad, gdn_norm_w, conv_b,
                            final_norm_w.reshape(1, -1), on_grad_c, on_grad_g, on_q)

    ss, rs, bufs = scat["c"]
    g_c, land, go4, land_o = _scatter_wait("c", ss, rs, [bufs[0], scat["g"][2][1], bufs[2], bufs[3]], gx)
    part_out = _sum_rows(go4, land_o, 128)
    ad_g = jnp.concatenate([sm["al"][:, A_LANE:], sm["dt"][:, A_LANE:]], axis=1)
    pack = jnp.concatenate([_row(sm["nin"]), _row(sm["cb"]), _row(sm["fn"]), _row(ad_g), _row(sm["gn"]),
                            jnp.concatenate(sm["cq"], axis=1).reshape(12, 1024), sm["cw"], _row(sm["loss"])], axis=0)
    pack = jnp.pad(pack, ((0, PACK_ROWS - pack.shape[0]), (0, 0)))
    issue, await_a, nsem = _exchange_parts(1, True)
    ss_a, rs_a, bufs_a, tok_a = _split_start(
        "exchange_start_small", issue,
        [part_out, lax.empty(part_out.shape, F32), pack, lax.empty((8,) + pack.shape, F32)], nsem)
    ss, rs, bufs = scat["g"]
    g_g, land = _scatter_wait("g", ss, rs, [bufs[0], land], [gx, tok_a], halved=True)
    part_in = _sum_shard(g_g, g_c, land)
    issue, await_b, nsem = _exchange_parts(1, False)
    ss_b, rs_b, bufs_b, tok_b = _split_start("exchange_start_w_in", issue,
                                             [part_in, lax.empty(part_in.shape, F32)], nsem)
    part_out, sib_out, pack, packs = _split_wait("exchange_wait_small", await_a, ss_a, rs_a, bufs_a, tok_b)
    tot = _sum_packs(pack, packs)
    g_wo, d_wo, m_wo, v_wo = _adamw(w_out[0], m_w_out[0], v_w_out[0], part_out, sib_out, 128, "adamw_w_out")
    g_cq_sh = lax.dynamic_slice_in_dim(tot[R_CQ:R_CQ + 12].reshape(4, 3 * GW), chip * 768, 768, axis=1)
    g_cw_sh = lax.dynamic_slice_in_dim(tot[R_CW:R_CW + 3], chip * 256, 256, axis=1)
    g_s = _small_pack(tot[R_NIN], tot[R_CB], tot[R_FN], tot[R_AD, :HEADS], tot[R_AD, HEADS:2 * HEADS],
                      tot[R_GN, :DH], g_cq_sh, g_cw_sh)
    small = _adamw(w_s, m_s, v_s, g_s, None, 16, "adamw_small")
    part_in, sib_in = _split_wait("exchange_wait_w_in", await_b, ss_b, rs_b, bufs_b, [small[0], d_wo])
    g_wi, d_wi, m_wi, v_wi = [jnp.transpose(a, (1, 2, 0))[0] for a in _adamw_shard(*adam_in, part_in, sib_in)]

    def unpack(a, big_in, big_out):
        return (a[0:1], big_in[None], a[5:8].reshape(1, 4, 768), a[3:4, :HEADS], a[3:4, HEADS:2 * HEADS],
                a[4:5, :DH], a[8, :768].reshape(1, 3, 256), a[1:2], big_out[None], a[2])

    loss = tot[R_LOSS, 0]
    return (loss, gx[None], *unpack(small[0], g_wi, g_wo), *unpack(small[1], d_wi, d_wo),
            *unpack(small[2], m_wi, m_wo), *unpack(small[3], v_wi, v_wo))
```

```python
import functools
import math

import jax
import jax.numpy as jnp
from jax import lax
from jax.experimental import pallas as pl
from jax.experimental.pallas import tpu as pltpu

F32 = jnp.float32
BF16 = jnp.bfloat16
MESH = pl.DeviceIdType.MESH
ANY = pl.BlockSpec(memory_space=pl.ANY)

HEADS = 8
DH = 128
CH = 64
GW = HEADS * DH
EPS = 1e-6
VMEM_V7X = 64 * 1024 * 1024

QB, KB, VB, ZB, BAB = 0, 8, 16, 24, 32
A_LANE = 120
NG, NC = 33, 32
GW_COLS, CW_COLS = NG * DH, NC * DH

SHARD_W = 2052
ALIGNED_BLOCKS = 17
ALIGNED_W = ALIGNED_BLOCKS * DH
SHIFTS = (0, 4, ALIGNED_W - 8, ALIGNED_W - 4)
G_EDGE, C_EDGE = 34, 32
G_SPARE, C_SPARE = 33, 34
WG_BLOCKS, WC_BLOCKS = 38, 36
G_MIXED, C_MIXED = (2, BAB), (4 * 7 + 1,)


def _shard_blocks(chip, edges):
    g, c = "g", "c"
    if chip == 0:
        out = [(g, 3 * b) for b in range(8)] + [(g, 3 * b + 1) for b in range(8)] + [(g, G_EDGE, G_MIXED[0])]
    elif chip == 1:
        out = [(g, G_EDGE + 1, G_MIXED[0])] + [(g, 3 * b + 2) for b in range(1, 8)]
        out += [(g, ZB + b) for b in range(8)] + [(g, G_EDGE + 2, G_MIXED[1])]
    elif chip == 2:
        out = [(c, 4 * b) for b in range(8)] + [(c, 4 * b + 1) for b in range(7)]
        out += [(c, C_EDGE, C_MIXED[0]), (g, G_EDGE + 3, G_MIXED[1])]
    else:
        out = [(c, 4 * b + 2) for b in range(8)] + [(c, 4 * b + 3) for b in range(8)] + [(c, C_EDGE + 1, C_MIXED[0])]
    return [(o[0], o[1] if (edges or len(o) == 2) else o[2]) for o in out]


def _by_chip(chip, vals):
    if all(v == vals[0] for v in vals):
        return vals[0]
    r = vals[3]
    for kk in (2, 1, 0):
        r = jnp.where(chip == kk, vals[kk], r)
    return r

ADAM_LR, ADAM_B1, ADAM_B2, ADAM_EPS, ADAM_WD, ADAM_STEP = 0.001, 0.9, 0.999, 1e-08, 0.01, 10

R_NIN, R_CB, R_FN, R_AD, R_GN, R_CQ, R_CW, R_LOSS, PACK_ROWS = 0, 1, 2, 3, 4, 5, 17, 20, 24

NN = ((1,), (0,))
NT = ((1,), (1,))
TN = ((0,), (0,))


def _dot(a, b, dims=NN, mode="lo"):
    dn = (dims, ((), ()))
    if mode == "hi":
        return lax.dot_general(a, b, dn, precision=lax.Precision.HIGHEST, preferred_element_type=F32)
    ah, bh = a.astype(BF16), b.astype(BF16)
    out = lax.dot_general(ah, bh, dn, preferred_element_type=F32)
    if mode == "x3":
        al = (a - ah.astype(F32)).astype(BF16)
        bl = (b - bh.astype(F32)).astype(BF16)
        out = out + lax.dot_general(ah, bl, dn, preferred_element_type=F32)
        out = out + lax.dot_general(al, bh, dn, preferred_element_type=F32)
    return out


P_GRAM, P_INV, P_SOL, P_SCAN, P_SCANB, P_BWD = "lo", "lo", "lo", "lo", "lo", "lo"
P_CUM = "x3"


def _params(sem=None, vmem=None):
    kw = {}
    if sem is not None:
        kw["dimension_semantics"] = sem
    if vmem is not None:
        kw["vmem_limit_bytes"] = int(min(max(vmem, 32 * 2**20), VMEM_V7X - 8 * 2**20))
    return pltpu.CompilerParams(**kw)


def _in_hbm(*arrays):
    return [pltpu.with_memory_space_constraint(a, pltpu.HBM) for a in arrays]


def _sigmoid(x):
    return 1.0 / (1.0 + jnp.exp(-x))


def _dsilu(x, s):
    return s * (1.0 + x * (1.0 - s))


def _rows(shape):
    return lax.broadcasted_iota(jnp.int32, shape, 0)


def _shift_down(x, s):
    if s == 0:
        return x
    return jnp.where(_rows(x.shape) >= s, pltpu.roll(x, s, 0), 0.0)


def _shift_up(x, s):
    if s == 0:
        return x
    n = x.shape[0]
    return jnp.where(_rows(x.shape) < n - s, pltpu.roll(x, n - s, 0), 0.0)


def _matmul(a, b, dims, out_dtype, tm, tn, tk, name, add=None, n=None, b_outer=False):
    if dims == NN:
        (m, k), n = a.shape, b.shape[1]
    elif dims == NT:
        (m, k), n = a.shape, (n or b.shape[0])
    else:
        (k, m), n = a.shape, b.shape[1]
    tm, tn, tk = min(tm, m), min(tn, n), min(tk, k)
    assert m % tm == 0 and n % tn == 0 and k % tk == 0, (name, m, n, k, tm, tn, tk)
    nk = k // tk

    def body(*refs):
        if add is None:
            a_ref, b_ref, o_ref = refs[:3]
            add_ref = None
        else:
            a_ref, b_ref, add_ref, o_ref = refs[:4]
        part = _dot(a_ref[...], b_ref[...], dims)
        if nk == 1:
            if add_ref is not None:
                part = part + add_ref[...]
            o_ref[...] = part.astype(out_dtype)
            return
        acc = refs[-1]
        kk = pl.program_id(2)

        @pl.when(kk == 0)
        def _():
            acc[...] = part

        @pl.when(kk > 0)
        def _():
            acc[...] += part

        @pl.when(kk == nk - 1)
        def _():
            r = acc[...]
            if add_ref is not None:
                r = r + add_ref[...]
            o_ref[...] = r.astype(out_dtype)

    ij = (lambda g0, g1: (g1, g0)) if b_outer else (lambda g0, g1: (g0, g1))

    def spec(shape, pick):
        return pl.BlockSpec(shape, lambda g0, g1, kk: pick(*ij(g0, g1), kk))

    a_spec = spec((tk, tm), lambda i, j, kk: (kk, i)) if dims == TN else spec((tm, tk), lambda i, j, kk: (i, kk))
    b_spec = spec((tn, tk), lambda i, j, kk: (j, kk)) if dims == NT else spec((tk, tn), lambda i, j, kk: (kk, j))
    o_spec = spec((tm, tn), lambda i, j, kk: (i, j))
    in_specs = [a_spec, b_spec]
    args = [a, b]
    if add is not None:
        in_specs.append(o_spec)
        args.append(add)
    osz = jnp.dtype(out_dtype).itemsize
    est = 2 * (tm * tk * a.dtype.itemsize + tk * tn * b.dtype.itemsize + tm * tn * osz)
    est += 3 * tm * tn * 4 + (2 * tm * tn * 4 if add is not None else 0)
    return pl.pallas_call(
        body, name=name, grid=(n // tn, m // tm, nk) if b_outer else (m // tm, n // tn, nk),
        in_specs=in_specs, out_specs=o_spec,
        out_shape=jax.ShapeDtypeStruct((m, n), out_dtype),
        scratch_shapes=[pltpu.VMEM((tm, tn), F32)] if nk > 1 else [],
        compiler_params=_params(("parallel", "parallel", "arbitrary"), est + 8 * 2**20),
    )(*args)


def _cast_bf16(a, rows, name):
    r, c = a.shape
    rows = min(rows, r)

    def body(a_ref, o_ref):
        o_ref[...] = a_ref[...].astype(BF16)

    return pl.pallas_call(
        body, name=name, grid=(r // rows,),
        in_specs=[pl.BlockSpec((rows, c), lambda i: (i, 0))],
        out_specs=pl.BlockSpec((rows, c), lambda i: (i, 0)),
        out_shape=jax.ShapeDtypeStruct((r, c), BF16),
        compiler_params=_params(("parallel",)),
    )(a)


def _align_shard(wt):
    r, _, d = wt.shape
    cols = min(256, d)

    def body(w_ref, o_ref, pad_ref):
        chip = 2 * lax.axis_index("x") + lax.axis_index("y")
        pad_ref[...] = jnp.zeros_like(pad_ref)
        pad_ref[0:r, :] = w_ref[:, 0, :]
        o_ref[...] = pltpu.roll(pad_ref[...], _by_chip(chip, SHIFTS), 0).astype(BF16)

    return pl.pallas_call(
        body, name="align_shard", grid=(d // cols,),
        in_specs=[pl.BlockSpec((r, 1, cols), lambda i: (0, 0, i))],
        out_specs=pl.BlockSpec((ALIGNED_W, cols), lambda i: (0, i)),
        out_shape=jax.ShapeDtypeStruct((ALIGNED_W, d), BF16),
        scratch_shapes=[pltpu.VMEM((ALIGNED_W, cols), F32)],
        compiler_params=_params(("parallel",)),
    )(wt)


def _rms_in(x, w):
    n, d = x.shape
    tr = min(256, n)

    def body(x_ref, w_ref, h_ref):
        xv = x_ref[...]
        r = lax.rsqrt(jnp.mean(xv * xv, axis=-1, keepdims=True) + EPS)
        h_ref[...] = (xv * r * w_ref[...]).astype(BF16)

    return pl.pallas_call(
        body, name="rms_in", grid=(n // tr,),
        in_specs=[pl.BlockSpec((tr, d), lambda i: (i, 0)), pl.BlockSpec((1, d), lambda i: (0, 0))],
        out_specs=pl.BlockSpec((tr, d), lambda i: (i, 0)),
        out_shape=jax.ShapeDtypeStruct((n, d), BF16),
        compiler_params=_params(("parallel",)),
    )(x, w)


def _conv_silu(p, w_ref, taps):
    c = None
    for j in range(taps):
        t = _shift_down(p, taps - 1 - j) * w_ref[j:j + 1, :]
        c = t if c is None else c + t
    return c


def _prep_qkv(proj, cw):
    n = proj.shape[0]

    def body(p3, wq, wk, wv, q_ref, k_ref, v_ref):
        for kind, (w_ref, o_ref) in enumerate(((wq, q_ref), (wk, k_ref), (wv, v_ref))):
            c = _conv_silu(p3[:, kind * DH:(kind + 1) * DH], w_ref, 4)
            a = c * _sigmoid(c)
            if kind < 2:
                r = lax.rsqrt(jnp.sum(a * a, axis=-1, keepdims=True) + EPS)
                a = a * (r * (DH ** -0.5 if kind == 0 else 1.0))
            o_ref[...] = a

    col = pl.BlockSpec((n, DH), lambda h: (0, h))
    wcol = lambda base: pl.BlockSpec((4, DH), lambda h: (0, base + h))
    out = jax.ShapeDtypeStruct((n, GW), F32)
    return pl.pallas_call(
        body, name="prep_qkv", grid=(HEADS,),
        in_specs=[pl.BlockSpec((n, 3 * DH), lambda h: (0, h)), wcol(QB), wcol(KB), wcol(VB)],
        out_specs=[col] * 3, out_shape=[out] * 3,
        compiler_params=_params(("parallel",), 40 * 2**20),
    )(proj, cw, cw, cw)


def _prep_qkv_bwd(proj, cw, dq, dk, dv, dproj):
    n = proj.shape[0]

    def body(p3, wq, wk, wv, dq_ref, dk_ref, dv_ref, _, o3, gq, gk, gv):
        for kind, (w_ref, d_ref, g_ref) in enumerate(((wq, dq_ref, gq), (wk, dk_ref, gk), (wv, dv_ref, gv))):
            p = p3[:, kind * DH:(kind + 1) * DH]
            shifted = [_shift_down(p, 3 - j) for j in range(4)]
            c = shifted[0] * w_ref[0:1, :]
            for j in range(1, 4):
                c = c + shifted[j] * w_ref[j:j + 1, :]
            s = _sigmoid(c)
            a = c * s
            d = d_ref[...]
            if kind < 2:
                r = lax.rsqrt(jnp.sum(a * a, axis=-1, keepdims=True) + EPS)
                sc = DH ** -0.5 if kind == 0 else 1.0
                d = (sc * r) * (d - a * ((r * r) * jnp.sum(d * a, axis=-1, keepdims=True)))
            dc = d * _dsilu(c, s)
            dp = None
            for j in range(4):
                g_ref[j:j + 1, :] = jnp.sum(dc * shifted[j], axis=0, keepdims=True)
                t = _shift_up(dc, 3 - j) * w_ref[j:j + 1, :]
                dp = t if dp is None else dp + t
            o3[:, kind * DH:(kind + 1) * DH] = dp.astype(BF16)

    col = pl.BlockSpec((n, DH), lambda h: (0, h))
    wcol = lambda base: pl.BlockSpec((4, DH), lambda h: (0, base + h))
    p3spec = pl.BlockSpec((n, 3 * DH), lambda h: (0, h))
    return pl.pallas_call(
        body, name="prep_qkv_bwd", grid=(HEADS,),
        in_specs=[p3spec, wcol(QB), wcol(KB), wcol(VB), col, col, col, ANY],
        out_specs=[p3spec] + [wcol(0)] * 3,
        out_shape=[jax.ShapeDtypeStruct(dproj.shape, BF16)] + [jax.ShapeDtypeStruct((4, GW), F32)] * 3,
        input_output_aliases={7: 0},
        compiler_params=_params(("parallel",), 48 * 2**20),
    )(proj, cw, cw, cw, dq, dk, dv, dproj)


CPB = 8
SCAN_CPS = 4


def _tri(lower, rows):
    i = lax.broadcasted_iota(jnp.int32, (rows, rows), 0)
    j = lax.broadcasted_iota(jnp.int32, (rows, rows), 1)
    return jnp.where((i // CH == j // CH) & ((i >= j) if lower else (j >= i)), 1.0, 0.0)


def _lane(shape):
    return lax.broadcasted_iota(jnp.int32, shape, 1)


def _prep_bg(proj, ad):
    n = proj.shape[0]
    nch = n // CH
    cpb = CPB if nch % CPB == 0 else 1
    rows = cpb * CH

    def body(p_ref, ad_ref, bg_ref, bgt_ref):
        p = p_ref[...]
        lane = _lane(p.shape)
        beta = _sigmoid(p)
        xa = p + ad_ref[1:2, :]
        sp = jnp.maximum(xa, 0.0) + jnp.log(1.0 + jnp.exp(-jnp.abs(xa)))
        g = pltpu.roll(-jnp.exp(ad_ref[0:1, :]) * sp, DH - A_LANE + HEADS, 1)
        gc = _dot(_tri(True, rows), g, NN, P_CUM)
        bg = jnp.where(lane < HEADS, beta, jnp.where(lane < 2 * HEADS, gc, 0.0))
        bg_ref[...] = bg
        for ci in range(cpb):
            bgt_ref[ci] = bg[ci * CH:(ci + 1) * CH, :].T

    return pl.pallas_call(
        body, name="prep_bg", grid=(nch // cpb,),
        in_specs=[pl.BlockSpec((rows, DH), lambda i: (i, BAB)), pl.BlockSpec((2, DH), lambda i: (0, 0))],
        out_specs=[pl.BlockSpec((rows, DH), lambda i: (i, 0)), pl.BlockSpec((cpb, DH, CH), lambda i: (i, 0, 0))],
        out_shape=[jax.ShapeDtypeStruct((n, DH), F32), jax.ShapeDtypeStruct((nch, DH, CH), F32)],
        compiler_params=_params(("parallel",)),
    )(*_in_hbm(proj, ad))


def _prep_bg_bwd(proj, ad, dbg, dproj):
    n = proj.shape[0]
    nch = n // CH
    cpb = CPB if nch % CPB == 0 else 1
    rows = cpb * CH

    def body(p_ref, ad_ref, d_ref, _, o_ref, ga_ref, gd_ref):
        p = p_ref[...]
        d = d_ref[...]
        lane = _lane(p.shape)
        beta = _sigmoid(p)
        xa = p + ad_ref[1:2, :]
        sp = jnp.maximum(xa, 0.0) + jnp.log(1.0 + jnp.exp(-jnp.abs(xa)))
        na = -jnp.exp(ad_ref[0:1, :])
        dg = pltpu.roll(_dot(_tri(False, rows), d, NN, P_CUM), A_LANE - HEADS, 1)
        da = dg * na * _sigmoid(xa)
        is_g = lane >= A_LANE
        o_ref[...] = jnp.where(lane < HEADS, d * beta * (1.0 - beta), jnp.where(is_g, da, 0.0)).astype(BF16)
        ga = jnp.sum(jnp.where(is_g, dg * na * sp, 0.0), axis=0, keepdims=True)
        gd = jnp.sum(jnp.where(is_g, da, 0.0), axis=0, keepdims=True)

        @pl.when(pl.program_id(0) == 0)
        def _():
            ga_ref[...] = jnp.zeros_like(ga_ref)
            gd_ref[...] = jnp.zeros_like(gd_ref)

        ga_ref[...] += ga
        gd_ref[...] += gd

    one = pl.BlockSpec((1, DH), lambda i: (0, 0))
    return pl.pallas_call(
        body, name="prep_bg_bwd", grid=(nch // cpb,),
        in_specs=[pl.BlockSpec((rows, DH), lambda i: (i, BAB)), pl.BlockSpec((2, DH), lambda i: (0, 0)),
                  pl.BlockSpec((rows, DH), lambda i: (i, 0)), ANY],
        out_specs=[pl.BlockSpec((rows, DH), lambda i: (i, BAB)), one, one],
        out_shape=[jax.ShapeDtypeStruct(dproj.shape, BF16), jax.ShapeDtypeStruct((1, DH), F32),
                   jax.ShapeDtypeStruct((1, DH), F32)],
        input_output_aliases={3: 0},
        compiler_params=_params(("arbitrary",)),
    )(proj, ad, dbg, dproj)


def _gdn_out(o, proj, wg):
    n = o.shape[0]

    def body(o_ref, z_ref, w_ref, y_ref):
        ov, z = o_ref[...], z_ref[...]
        r = lax.rsqrt(jnp.mean(ov * ov, axis=-1, keepdims=True) + EPS)
        y_ref[...] = (ov * r * w_ref[...] * (z * _sigmoid(z))).astype(BF16)

    return pl.pallas_call(
        body, name="gdn_out", grid=(HEADS,),
        in_specs=[pl.BlockSpec((n, DH), lambda h: (0, h)), pl.BlockSpec((n, DH), lambda h: (0, ZB + h)),
                  pl.BlockSpec((1, DH), lambda h: (0, 0))],
        out_specs=pl.BlockSpec((n, DH), lambda h: (0, h)),
        out_shape=jax.ShapeDtypeStruct((n, 2 * GW), BF16),
        compiler_params=_params(("parallel",)),
    )(o, proj, wg)


def _gdn_out_bwd(o, proj, wg, dout_b, w_out):
    n = o.shape[0]
    d_model = dout_b.shape[1]

    def body(o_ref, z_ref, w_ref, g_ref, wo_ref, do_ref, dz_ref, gw_ref):
        ov, z, w = o_ref[...], z_ref[...], w_ref[...]
        d = _dot(g_ref[...], wo_ref[...], NT)
        r = lax.rsqrt(jnp.mean(ov * ov, axis=-1, keepdims=True) + EPS)
        nrm = ov * r
        s = _sigmoid(z)
        dz_ref[...] = (d * (nrm * w) * _dsilu(z, s)).astype(BF16)
        dn_w = d * (z * s)
        gw = jnp.sum(dn_w * nrm, axis=0, keepdims=True)
        dn = dn_w * w
        do_ref[...] = (r * (dn - nrm * jnp.mean(dn * nrm, axis=-1, keepdims=True))).astype(BF16)

        @pl.when(pl.program_id(0) == 0)
        def _():
            gw_ref[...] = jnp.zeros_like(gw_ref)

        gw_ref[...] += gw

    return pl.pallas_call(
        body, name="gdn_out_bwd", grid=(HEADS,),
        in_specs=[pl.BlockSpec((n, DH), lambda h: (0, h)), pl.BlockSpec((n, DH), lambda h: (0, ZB + h)),
                  pl.BlockSpec((1, DH), lambda h: (0, 0)), pl.BlockSpec((n, d_model), lambda h: (0, 0)),
                  pl.BlockSpec((DH, d_model), lambda h: (h, 0))],
        out_specs=[pl.BlockSpec((n, DH), lambda h: (0, h)), pl.BlockSpec((n, DH), lambda h: (0, ZB + h)),
                   pl.BlockSpec((1, DH), lambda h: (0, 0))],
        out_shape=[jax.ShapeDtypeStruct((n, GW), BF16), jax.ShapeDtypeStruct((n, GW_COLS), BF16),
                   jax.ShapeDtypeStruct((1, DH), F32)],
        compiler_params=_params(("arbitrary",), 40 * 2**20),
    )(o, proj, wg, dout_b, w_out)


def _conv_branch(proj, w3, b, mix):
    n = proj.shape[0]

    def body(p4, w_ref, b_ref, _, y_ref):
        u = p4[:, DH:2 * DH] * p4[:, 2 * DH:3 * DH]
        cc = _conv_silu(u, w_ref, 3) + b_ref[...]
        z = p4[:, 3 * DH:4 * DH]
        y_ref[...] = (p4[:, 0:DH] * cc * (z * _sigmoid(z))).astype(BF16)

    return pl.pallas_call(
        body, name="conv_branch", grid=(HEADS,),
        in_specs=[pl.BlockSpec((n, 4 * DH), lambda h: (0, h)), pl.BlockSpec((3, DH), lambda h: (0, h)),
                  pl.BlockSpec((1, DH), lambda h: (0, h)), ANY],
        out_specs=pl.BlockSpec((n, DH), lambda h: (0, HEADS + h)),
        out_shape=jax.ShapeDtypeStruct(mix.shape, BF16),
        input_output_aliases={3: 0},
        compiler_params=_params(("parallel",), 40 * 2**20),
    )(*_in_hbm(proj, w3, b, mix))


def _conv_branch_bwd(proj, w3, b, dout_b, w_out):
    n = proj.shape[0]
    d_model = dout_b.shape[1]

    def body(p4, w_ref, b_ref, g_ref, wo_ref, o4, gw_ref, gbias_ref):
        gb, gcv, hc, z = p4[:, 0:DH], p4[:, DH:2 * DH], p4[:, 2 * DH:3 * DH], p4[:, 3 * DH:4 * DH]
        d = _dot(g_ref[...], wo_ref[...], NT)
        dgb, dgc, dhc, dzc = (o4.at[:, kk * DH:(kk + 1) * DH] for kk in range(4))
        u = gcv * hc
        cc = _conv_silu(u, w_ref, 3) + b_ref[...]
        s = _sigmoid(z)
        dzc[...] = (d * (gb * cc) * _dsilu(z, s)).astype(BF16)
        dp = d * (z * s)
        dgb[...] = (dp * cc).astype(BF16)
        dcc = dp * gb
        gbias_ref[...] = jnp.sum(dcc, axis=0, keepdims=True)
        du = None
        for j in range(3):
            gw_ref[j:j + 1, :] = jnp.sum(dcc * _shift_down(u, 2 - j), axis=0, keepdims=True)
            t = _shift_up(dcc, 2 - j) * w_ref[j:j + 1, :]
            du = t if du is None else du + t
        dgc[...] = (du * hc).astype(BF16)
        dhc[...] = (du * gcv).astype(BF16)

    p4spec = pl.BlockSpec((n, 4 * DH), lambda h: (0, h))
    return pl.pallas_call(
        body, name="conv_branch_bwd", grid=(HEADS,),
        in_specs=[p4spec, pl.BlockSpec((3, DH), lambda h: (0, h)), pl.BlockSpec((1, DH), lambda h: (0, h)),
                  pl.BlockSpec((n, d_model), lambda h: (0, 0)), pl.BlockSpec((DH, d_model), lambda h: (HEADS + h, 0))],
        out_specs=[p4spec, pl.BlockSpec((3, DH), lambda h: (0, h)), pl.BlockSpec((1, DH), lambda h: (0, h))],
        out_shape=[jax.ShapeDtypeStruct((n, CW_COLS), BF16), jax.ShapeDtypeStruct((3, GW), F32),
                   jax.ShapeDtypeStruct((1, GW), F32)],
        compiler_params=_params(("parallel",), 52 * 2**20),
    )(proj, w3, b, dout_b, w_out)


def _out_loss(mix, w_out, x, tgt, wf):
    n, d = x.shape
    kdim = mix.shape[1]
    tr = min(256, n)

    def body(m_ref, wo_ref, x_ref, t_ref, w_ref, do_ref, dob_ref, gw_ref, loss_ref):
        ov = _dot(m_ref[...], wo_ref[...], NN) + x_ref[...]
        w = w_ref[...]
        r = lax.rsqrt(jnp.mean(ov * ov, axis=-1, keepdims=True) + EPS)
        nrm = ov * r
        e = nrm * w - t_ref[...]
        dy = e * (1.0 / d)
        dn = dy * w
        dout = r * (dn - nrm * jnp.mean(dn * nrm, axis=-1, keepdims=True))
        do_ref[...] = dout
        dob_ref[...] = dout.astype(BF16)

        @pl.when(pl.program_id(0) == 0)
        def _():
            gw_ref[...] = jnp.zeros_like(gw_ref)
            loss_ref[...] = jnp.zeros_like(loss_ref)

        gw_ref[...] += jnp.sum(dy * nrm, axis=0, keepdims=True)
        loss_ref[...] += (0.5 / d) * jnp.sum(jnp.sum(e * e, axis=-1, keepdims=True), axis=0, keepdims=True)

    row = pl.BlockSpec((tr, d), lambda i: (i, 0))
    return pl.pallas_call(
        body, name="out_loss", grid=(n // tr,),
        in_specs=[pl.BlockSpec((tr, kdim), lambda i: (i, 0)), pl.BlockSpec((kdim, d), lambda i: (0, 0)), row, row,
                  pl.BlockSpec((1, d), lambda i: (0, 0))],
        out_specs=[row, row, pl.BlockSpec((1, d), lambda i: (0, 0)), pl.BlockSpec((1, 1), lambda i: (0, 0))],
        out_shape=[jax.ShapeDtypeStruct((n, d), F32), jax.ShapeDtypeStruct((n, d), BF16),
                   jax.ShapeDtypeStruct((1, d), F32), jax.ShapeDtypeStruct((1, 1), F32)],
        compiler_params=_params(("arbitrary",), 40 * 2**20),
    )(mix, w_out, x, tgt, wf)


def _dh_rms_bwd(dproj, w_t, dh0, x, w, dout, tk):
    n, d = x.shape
    kdim = dproj.shape[1]
    tm = min(512, n)
    tk = min(tk, kdim)
    nk = kdim // tk

    def body(a_ref, b_ref, dh0_ref, x_ref, w_ref, do_ref, dx_ref, gw_ref, acc):
        i, kk = pl.program_id(0), pl.program_id(1)
        part = _dot(a_ref[...], b_ref[...], NN)

        @pl.when(kk == 0)
        def _():
            acc[...] = part + dh0_ref[...]

        @pl.when(kk > 0)
        def _():
            acc[...] += part

        @pl.when((i == 0) & (kk == 0))
        def _():
            gw_ref[...] = jnp.zeros_like(gw_ref)

        @pl.when(kk == nk - 1)
        def _():
            xv, dhv = x_ref[...], acc[...]
            r = lax.rsqrt(jnp.mean(xv * xv, axis=-1, keepdims=True) + EPS)
            xn = xv * r
            dxn = dhv * w_ref[...]
            dx_ref[...] = r * (dxn - xn * jnp.mean(dxn * xn, axis=-1, keepdims=True)) + do_ref[...]
            gw_ref[...] += jnp.sum(dhv * xn, axis=0, keepdims=True)

    row = pl.BlockSpec((tm, d), lambda i, kk: (i, 0))
    one = pl.BlockSpec((1, d), lambda i, kk: (0, 0))
    return pl.pallas_call(
        body, name="dh_rms_bwd", grid=(n // tm, nk),
        in_specs=[pl.BlockSpec((tm, tk), lambda i, kk: (i, kk)), pl.BlockSpec((tk, d), lambda i, kk: (kk, 0)),
                  row, row, one, row],
        out_specs=[row, one],
        out_shape=[jax.ShapeDtypeStruct((n, d), F32), jax.ShapeDtypeStruct((1, d), F32)],
        scratch_shapes=[pltpu.VMEM((tm, d), F32)],
        compiler_params=_params(("arbitrary", "arbitrary"), 48 * 2**20),
    )(dproj, w_t, dh0, x, w, dout)


def _ij():
    i = lax.broadcasted_iota(jnp.int32, (CH, CH), 0)
    j = lax.broadcasted_iota(jnp.int32, (CH, CH), 1)
    return i, j


def _unit_lower_inverse(mats):
    i, j = _ij()
    eye = jnp.where(i == j, 1.0, 0.0)
    same16 = (i // 16) == (j // 16)
    same32 = (i // 32) == (j // 32)
    mm = lambda xs, ys: [_dot(x, y, NN, P_INV) for x, y in zip(xs, ys)]
    n1 = [jnp.where(same16, -a, 0.0) for a in mats]
    n2 = mm(n1, n1)
    n4 = mm(n2, n2)
    n8 = mm(n4, n4)
    t = [eye + x1 + x2 + x3 for x1, x2, x3 in zip(n1, n2, mm(n1, n2))]
    t = [x + y for x, y in zip(t, mm(t, n4))]
    t = [x + y for x, y in zip(t, mm(t, n8))]
    a1 = [jnp.where(same32 & jnp.logical_not(same16), a, 0.0) for a in mats]
    t = [x - y for x, y in zip(t, mm(t, mm(a1, t)))]
    a2 = [jnp.where(same32, 0.0, a) for a in mats]
    t = [x - y for x, y in zip(t, mm(t, mm(a2, t)))]
    return t


def _head_vectors(bg, bgt, h):
    bcol = bg[:, h:h + 1]
    gcol = bg[:, HEADS + h:HEADS + h + 1]
    grow = bgt[HEADS + h:HEADS + h + 1, :]
    return bcol, gcol, grow


def _decay(gcol, grow):
    i, j = _ij()
    return jnp.where(i >= j, jnp.exp(jnp.where(i >= j, gcol - grow, 0.0)), 0.0)


def _gdn_intra(q, k, v, bg, bgt):
    n = q.shape[0]
    nch = n // CH
    cps = 4 if nch % 4 == 0 else 1

    def body(q_ref, k_ref, v_ref, bg_ref, bgt_ref, u_ref, w_ref, p_ref, t_ref):
        i, j = _ij()
        items = [(ci, h) for ci in range(cps) for h in range(HEADS)]
        at = lambda ref, ci, h: ref.at[ci * CH:(ci + 1) * CH, h * DH:(h + 1) * DH]
        bgs = [bg_ref[ci * CH:(ci + 1) * CH, :] for ci in range(cps)]
        ks = [at(k_ref, ci, h)[...] for ci, h in items]
        vecs = [_head_vectors(bgs[ci], bgt_ref[ci], h) for ci, h in items]
        decs = [_decay(gcol, grow) for _, gcol, grow in vecs]
        kks = [_dot(kh, kh, NT, P_GRAM) for kh in ks]
        qks = [_dot(at(q_ref, ci, h)[...], kh, NT, P_GRAM) for (ci, h), kh in zip(items, ks)]
        ts = _unit_lower_inverse([jnp.where(i > j, bcol * kk * dec, 0.0)
                                  for (bcol, _, _), kk, dec in zip(vecs, kks, decs)])
        us = [_dot(t, at(v_ref, ci, h)[...] * bcol, NN, P_SOL) for t, (ci, h), (bcol, _, _) in zip(ts, items, vecs)]
        ws = [_dot(t, kh * (bcol * jnp.exp(gcol)), NN, P_SOL) for t, kh, (bcol, gcol, _) in zip(ts, ks, vecs)]
        for n_, (ci, h) in enumerate(items):
            p_ref[ci, h] = qks[n_] * decs[n_]
            t_ref[ci, h] = ts[n_].astype(BF16)
            at(u_ref, ci, h)[...] = us[n_]
            at(w_ref, ci, h)[...] = ws[n_].astype(BF16)

    row = pl.BlockSpec((cps * CH, GW), lambda c: (c, 0))
    sq = pl.BlockSpec((cps, HEADS, CH, CH), lambda c: (c, 0, 0, 0))
    big = jax.ShapeDtypeStruct((n, GW), F32)
    sqs = jax.ShapeDtypeStruct((nch, HEADS, CH, CH), F32)
    return pl.pallas_call(
        body, name="gdn_intra", grid=(nch // cps,),
        in_specs=[row, row, row, pl.BlockSpec((cps * CH, DH), lambda c: (c, 0)),
                  pl.BlockSpec((cps, DH, CH), lambda c: (c, 0, 0))],
        out_specs=[row, row, sq, sq],
        out_shape=[big, jax.ShapeDtypeStruct((n, GW), BF16), sqs, jax.ShapeDtypeStruct(sqs.shape, BF16)],
        compiler_params=_params(("parallel",)),
    )(q, k, v, bg, bgt)


def _gdn_scan(q, k, bg, u, w, p):
    n = q.shape[0]
    nch = n // CH
    cps = SCAN_CPS if nch % SCAN_CPS == 0 else 1

    def body(q_ref, k_ref, bg_ref, u_ref, w_ref, p_ref, o_ref, vn_ref, s_out, s_scr):
        @pl.when(pl.program_id(0) == 0)
        def _():
            s_scr[...] = jnp.zeros_like(s_scr)

        hs = range(HEADS)
        sls = [slice(h * DH, (h + 1) * DH) for h in hs]
        ss = [s_scr[h] for h in hs]
        for ci in range(cps):
            rs = slice(ci * CH, (ci + 1) * CH)
            bg = bg_ref[rs, :]
            gcols = [bg[:, HEADS + h:HEADS + h + 1] for h in hs]
            glasts = [g[CH - 1:CH, :] for g in gcols]
            wss = [_dot(w_ref[rs, sl], s, NN, P_SCAN) for sl, s in zip(sls, ss)]
            oqs = [_dot(q_ref[rs, sl] * jnp.exp(g), s, NN, P_SCAN) for sl, s, g in zip(sls, ss, gcols)]
            vns = [u_ref[rs, sl] - x for sl, x in zip(sls, wss)]
            ops = [_dot(p_ref[ci, h], vn, NN, P_SCAN) for h, vn in zip(hs, vns)]
            sns = [_dot(k_ref[rs, sl] * jnp.exp(gl - g), vn, TN, P_SCAN)
                   for sl, gl, g, vn in zip(sls, glasts, gcols, vns)]
            for h, sl in enumerate(sls):
                s_out[ci, :, sl] = ss[h].astype(BF16)
                vn_ref[rs, sl] = vns[h].astype(BF16)
                o_ref[rs, sl] = oqs[h] + ops[h]
            ss = [s * jnp.exp(gl) + sn for s, gl, sn in zip(ss, glasts, sns)]
        for h in hs:
            s_scr[h] = ss[h]

    row = pl.BlockSpec((cps * CH, GW), lambda c: (c, 0))
    big = jax.ShapeDtypeStruct((n, GW), F32)
    return pl.pallas_call(
        body, name="gdn_scan", grid=(nch // cps,),
        in_specs=[row, row, pl.BlockSpec((cps * CH, DH), lambda c: (c, 0)), row, row,
                  pl.BlockSpec((cps, HEADS, CH, CH), lambda c: (c, 0, 0, 0))],
        out_specs=[row, row, pl.BlockSpec((cps, DH, GW), lambda c: (c, 0, 0))],
        out_shape=[big, jax.ShapeDtypeStruct((n, GW), BF16), jax.ShapeDtypeStruct((nch, DH, GW), BF16)],
        scratch_shapes=[pltpu.VMEM((HEADS, DH, DH), F32)],
        compiler_params=_params(("arbitrary",)),
    )(q, k, bg, u, w, p)


def _gdn_scan_bwd(q, k, bg, w, p, vn, s_in, do):
    n = q.shape[0]
    nch = n // CH
    cps = SCAN_CPS if nch % SCAN_CPS == 0 else 1
    rev = lambda c: nch // cps - 1 - c

    def body(q_ref, k_ref, bg_ref, w_ref, p_ref, vn_ref, s_ref, do_ref,
             dqg_ref, dp_ref, du_ref, dw_ref, dks_ref, dgam_ref, ds_scr):
        @pl.when(pl.program_id(0) == 0)
        def _():
            ds_scr[...] = jnp.zeros_like(ds_scr)

        lane = _lane((1, DH))
        hs = range(HEADS)
        sls = [slice(h * DH, (h + 1) * DH) for h in hs]
        dss = [ds_scr[h] for h in hs]
        for ci in reversed(range(cps)):
            rs = slice(ci * CH, (ci + 1) * CH)
            bg = bg_ref[rs, :]
            gcols = [bg[:, HEADS + h:HEADS + h + 1] for h in hs]
            glasts = [g[CH - 1:CH, :] for g in gcols]
            ss = [s_ref[ci, :, sl] for sl in sls]
            dos = [do_ref[rs, sl] for sl in sls]
            vnl = [vn_ref[rs, sl] for sl in sls]
            dqgs = [_dot(d, s, NT, P_SCANB) for d, s in zip(dos, ss)]
            dps = [_dot(d, vn, NT, P_SCANB) for d, vn in zip(dos, vnl)]
            dvn1 = [_dot(p_ref[ci, h], d, TN, P_SCANB) for h, d in zip(hs, dos)]
            dvn2 = [_dot(k_ref[rs, sl] * jnp.exp(gl - g), ds, NN, P_SCANB)
                    for sl, gl, g, ds in zip(sls, glasts, gcols, dss)]
            dkss = [_dot(vn, ds, NT, P_SCANB) for vn, ds in zip(vnl, dss)]
            dsq = [_dot(q_ref[rs, sl] * jnp.exp(g), d, TN, P_SCANB) for sl, g, d in zip(sls, gcols, dos)]
            dvns = [a + b for a, b in zip(dvn1, dvn2)]
            dws = [_dot(dvn, s, NT, P_SCANB) for dvn, s in zip(dvns, ss)]
            dsw = [_dot(w_ref[rs, sl], dvn, TN, P_SCANB) for sl, dvn in zip(sls, dvns)]
            dgam = jnp.zeros((1, DH), F32)
            for h, sl in enumerate(sls):
                dqg_ref[rs, sl] = dqgs[h]
                dp_ref[ci, h] = dps[h]
                du_ref[rs, sl] = dvns[h].astype(BF16)
                dw_ref[rs, sl] = (-dws[h]).astype(BF16)
                dks_ref[rs, sl] = dkss[h]
                tot = jnp.sum(jnp.sum(dss[h] * ss[h], axis=-1, keepdims=True), axis=0, keepdims=True)
                dgam = dgam + jnp.where(lane == h, tot, 0.0)
            dgam_ref[ci] = jnp.broadcast_to(dgam, (8, DH))
            dss = [ds * jnp.exp(gl) + a - b for ds, gl, a, b in zip(dss, glasts, dsq, dsw)]
        for h in hs:
            ds_scr[h] = dss[h]

    row = pl.BlockSpec((cps * CH, GW), lambda c: (rev(c), 0))
    sq = pl.BlockSpec((cps, HEADS, CH, CH), lambda c: (rev(c), 0, 0, 0))
    big = jax.ShapeDtypeStruct((n, GW), F32)
    return pl.pallas_call(
        body, name="gdn_scan_bwd", grid=(nch // cps,),
        in_specs=[row, row, pl.BlockSpec((cps * CH, DH), lambda c: (rev(c), 0)), row, sq, row,
                  pl.BlockSpec((cps, DH, GW), lambda c: (rev(c), 0, 0)), row],
        out_specs=[row, sq, row, row, row, pl.BlockSpec((cps, 8, DH), lambda c: (rev(c), 0, 0))],
        out_shape=[big, jax.ShapeDtypeStruct((nch, HEADS, CH, CH), F32), jax.ShapeDtypeStruct((n, GW), BF16),
                   jax.ShapeDtypeStruct((n, GW), BF16), big,
                   jax.ShapeDtypeStruct((nch, 8, DH), F32)],
        scratch_shapes=[pltpu.VMEM((HEADS, DH, DH), F32)],
        compiler_params=_params(("arbitrary",)),
    )(q, k, bg, w, p, vn, s_in, do)


def _gdn_intra_bwd(q, k, v, bg, bgt, t, u, w, p, dqg, dp, du, dw, dks, dgam):
    n = q.shape[0]
    nch = n // CH
    cps = 2 if nch % 2 == 0 else 1

    def body(q_ref, k_ref, v_ref, bg_ref, bgt_ref, t_ref, u_ref, w_ref, p_ref,
             dqg_ref, dp_ref, du_ref, dw_ref, dks_ref, dgam_ref, dq_ref, dk_ref, dv_ref, dbg_ref):
        i, j = _ij()
        rows1 = lax.broadcasted_iota(jnp.int32, (CH, 1), 0)
        lane = _lane((CH, DH))
        rsum = lambda x: jnp.sum(x, axis=-1, keepdims=True)
        items = [(ci, h) for ci in range(cps) for h in range(HEADS)]
        at = lambda ref, it: ref.at[it[0] * CH:(it[0] + 1) * CH, it[1] * DH:(it[1] + 1) * DH]
        ld = lambda ref: [at(ref, it)[...] for it in items]
        bgs = [bg_ref[ci * CH:(ci + 1) * CH, :] for ci in range(cps)]
        qs, ks = ld(q_ref), ld(k_ref)
        vecs = [_head_vectors(bgs[ci], bgt_ref[ci], h) for ci, h in items]
        decs = [_decay(gcol, grow) for _, gcol, grow in vecs]
        ths = [t_ref[ci, h] for ci, h in items]
        drus = [_dot(th, x_, TN, P_BWD) for th, x_ in zip(ths, ld(du_ref))]
        drws = [_dot(th, x_, TN, P_BWD) for th, x_ in zip(ths, ld(dw_ref))]
        kks = [_dot(kh, kh, NT, P_GRAM) for kh in ks]
        da1 = [_dot(dru, x_, NT, P_BWD) for dru, x_ in zip(drus, ld(u_ref))]
        da2 = [_dot(drw, x_, NT, P_BWD) for drw, x_ in zip(drws, ld(w_ref))]
        das = [jnp.where(i > j, -(x_ + y_), 0.0) for x_, y_ in zip(da1, da2)]
        dkks = [da * bcol * dec for da, (bcol, _, _), dec in zip(das, vecs, decs)]
        dps = [dp_ref[ci, h] for ci, h in items]
        dqks = [dp_ * dec for dp_, dec in zip(dps, decs)]
        dq_ps = [_dot(dqk, kh, NN, P_BWD) for dqk, kh in zip(dqks, ks)]
        dk_ps = [_dot(dqk, qh, TN, P_BWD) for dqk, qh in zip(dqks, qs)]
        dk_as = [_dot(dkk, kh, NN, P_BWD) for dkk, kh in zip(dkks, ks)]
        dk_bs = [_dot(dkk, kh, TN, P_BWD) for dkk, kh in zip(dkks, ks)]
        bcols = [vc[0] for vc in vecs]
        gcols = [vc[1] for vc in vecs]
        gams = [jnp.exp(g) for g in gcols]
        glasts = [g[CH - 1:CH, :] for g in gcols]
        es = [jnp.exp(gl - g) for gl, g in zip(glasts, gcols)]
        kgs = [kh * gam for kh, gam in zip(ks, gams)]
        dqgs, dkss = ld(dqg_ref), ld(dks_ref)
        wks = [drw * kg for drw, kg in zip(drws, kgs)]
        kss = [dk_ * (kh * e) for dk_, kh, e in zip(dkss, ks, es)]
        r_beta = [rsum(dru * x_ + wk) for dru, x_, wk in zip(drus, ld(v_ref), wks)]
        r_ak = [rsum(da * kk * dec) for da, kk, dec in zip(das, kks, decs)]
        r_gc = [rsum(wk * bcol + dqg * (qh * gam) - ks_)
                for wk, bcol, dqg, qh, gam, ks_ in zip(wks, bcols, dqgs, qs, gams, kss)]
        tk_tot = [jnp.sum(jnp.sum(ks_, axis=0, keepdims=True), axis=-1, keepdims=True) for ks_ in kss]
        mdecs = [da * (bcol * kk * dec) + dp_ * p_ref[ci, h]
                 for (ci, h), da, bcol, kk, dec, dp_ in zip(items, das, bcols, kks, decs, dps)]
        r_md = [rsum(m) for m in mdecs]
        c_md = [rsum(jnp.where(i == j, jnp.sum(m, axis=0, keepdims=True), 0.0)) for m in mdecs]
        dbgs = [jnp.zeros((CH, DH), F32) for _ in range(cps)]
        for n_, (ci, h) in enumerate(items):
            at(dv_ref, (ci, h))[...] = bcols[n_] * drus[n_]
            at(dq_ref, (ci, h))[...] = gams[n_] * dqgs[n_] + dq_ps[n_]
            at(dk_ref, (ci, h))[...] = ((bcols[n_] * gams[n_]) * drws[n_] + dk_ps[n_] + dk_as[n_] + dk_bs[n_]
                                        + dkss[n_] * es[n_])
            dbeta = r_beta[n_] + r_ak[n_]
            dglast = tk_tot[n_] + dgam_ref[ci, 0:1, h:h + 1] * jnp.exp(glasts[n_])
            dgc = r_gc[n_] + r_md[n_] - c_md[n_] + jnp.where(rows1 == CH - 1, dglast, 0.0)
            dbgs[ci] = dbgs[ci] + jnp.where(lane == h, dbeta, 0.0) + jnp.where(lane == HEADS + h, dgc, 0.0)
        for ci in range(cps):
            dbg_ref[ci * CH:(ci + 1) * CH, :] = dbgs[ci]

    row = pl.BlockSpec((cps * CH, GW), lambda c: (c, 0))
    sq = pl.BlockSpec((cps, HEADS, CH, CH), lambda c: (c, 0, 0, 0))
    small = pl.BlockSpec((cps * CH, DH), lambda c: (c, 0))
    big = jax.ShapeDtypeStruct((n, GW), F32)
    return pl.pallas_call(
        body, name="gdn_intra_bwd", grid=(nch // cps,),
        in_specs=[row, row, row, small, pl.BlockSpec((cps, DH, CH), lambda c: (c, 0, 0)), sq, row, row, sq,
                  row, sq, row, row, row, pl.BlockSpec((cps, 8, DH), lambda c: (c, 0, 0))],
        out_specs=[row, row, row, small],
        out_shape=[big, big, big, jax.ShapeDtypeStruct((n, DH), F32)],
        compiler_params=_params(("parallel",)),
    )(q, k, v, bg, bgt, t, u, w, p, dqg, dp, du, dw, dks, dgam)


def _local_step(x, tgt, h, w_g, cqw, late, norm_in_w, ad, gdn_norm_w, conv_b, final_norm_w,
                on_grad_c=None, on_grad_g=None, on_q=None):
    proj_g = _matmul(h, w_g, NT, F32, 512, 1408, 1024, "mm_proj_g", n=GW_COLS, b_outer=True)
    q, k, v = _prep_qkv(proj_g, cqw)
    bg, bgt = _prep_bg(proj_g, ad)
    u, w, p, t = _gdn_intra(q, k, v, bg, bgt)
    if on_q is not None:
        u = on_q(u)
    o, vn, s_in = _gdn_scan(q, k, bg, u, w, p)
    w_c, w_out, conv_w = late(o)
    proj_c = _matmul(h, w_c, NT, F32, 512, 1024, 1024, "mm_proj_c", n=CW_COLS, b_outer=True)
    mix = _conv_branch(proj_c, conv_w, conv_b, _gdn_out(o, proj_g, gdn_norm_w))
    dout, dout_b, g_fn, loss = _out_loss(mix, w_out, x, tgt, final_norm_w)

    g_wout = _matmul(mix, dout_b, TN, BF16, 512, 512, 2048, "mm_gwout")
    do, dproj_g, g_gn = _gdn_out_bwd(o, proj_g, gdn_norm_w, dout_b, w_out)
    dproj_c, g_cw, g_cb = _conv_branch_bwd(proj_c, conv_w, conv_b, dout_b, w_out)
    g_c = _matmul(dproj_c, h, TN, BF16, 1024, 512, 2048, "mm_gwin_c")
    if on_grad_c is not None:
        do = on_grad_c(g_c, g_wout, do)
    dqg, dp, du, dw, dks, dgam = _gdn_scan_bwd(q, k, bg, w, p, vn, s_in, do)
    dq, dk, dv, dbg = _gdn_intra_bwd(q, k, v, bg, bgt, t, u, w, p, dqg, dp, du, dw, dks, dgam)
    dproj_g, gq, gk, gv = _prep_qkv_bwd(proj_g, cqw, dq, dk, dv, dproj_g)
    dproj_g, g_al, g_dt = _prep_bg_bwd(proj_g, ad, dbg, dproj_g)
    g_g = _matmul(dproj_g, h, TN, BF16, 1408, 512, 2048, "mm_gwin_g")
    if on_grad_g is not None:
        dproj_g = on_grad_g(g_g, dproj_g)
    dh = _matmul(dproj_g, w_g, NN, F32, 1024, 1024, 1408, "mm_dh_g")
    gx, g_nin = _dh_rms_bwd(dproj_c, w_c, dh, x, norm_in_w, dout, 1024)
    small = dict(nin=g_nin, cb=g_cb, fn=g_fn, al=g_al, dt=g_dt, gn=g_gn, cq=(gq, gk, gv), cw=g_cw, loss=loss)
    return gx, small, (g_g, g_c, g_wout)


def _place():
    x, y, c = lax.axis_index("x"), lax.axis_index("y"), lax.axis_index("c")
    chips = [(1 - x, y), (x, 1 - y), (1 - x, 1 - y)]
    return x, y, c, chips


def _blk(ref, b):
    if isinstance(b, int):
        return ref.at[b * DH:(b + 1) * DH, :]
    return ref.at[pl.ds(pl.multiple_of(b * DH, DH), DH), :]


HBM = pl.BlockSpec(memory_space=pltpu.HBM)
SEM = pl.BlockSpec(memory_space=pltpu.SEMAPHORE)
EFFECT = pltpu.SideEffectType.DATAFLOW_SIDE_EFFECTING


def _split_start(name, issue, bufs, n_sems):
    nbuf = len(bufs)

    def body(*refs):
        issue(refs[:nbuf], refs[nbuf], refs[nbuf + 1])
        refs[-1][...] = jnp.zeros_like(refs[-1])

    out = pl.pallas_call(
        body, name=name,
        out_shape=(pltpu.SemaphoreType.DMA((n_sems,)), pltpu.SemaphoreType.DMA((n_sems,)),
                   *[pltpu.HBM(b.shape, b.dtype) for b in bufs], jax.ShapeDtypeStruct((8, DH), F32)),
        in_specs=[HBM] * nbuf,
        out_specs=(SEM, SEM, *[HBM] * nbuf, pl.BlockSpec(memory_space=pltpu.VMEM)),
        input_output_aliases={a: 2 + a for a in range(nbuf)},
        compiler_params=pltpu.CompilerParams(has_side_effects=EFFECT),
    )(*[pltpu.with_memory_space_constraint(b, pltpu.HBM) for b in bufs])
    return out[0], out[1], list(out[2:2 + nbuf]), out[-1]


def _split_wait(name, await_, send_sems, recv_sems, bufs, after):
    nbuf = len(bufs)
    after = list(after) if isinstance(after, (list, tuple)) else [after]

    def body(*refs):
        await_(refs[:nbuf], refs[nbuf], refs[nbuf + 1])

    out = pl.pallas_call(
        body, name=name,
        out_shape=tuple(pltpu.HBM(b.shape, b.dtype) for b in bufs),
        in_specs=[HBM] * nbuf + [SEM, SEM] + [ANY] * len(after), out_specs=tuple([HBM] * nbuf),
        input_output_aliases={a: a for a in range(nbuf)},
        compiler_params=pltpu.CompilerParams(has_side_effects=EFFECT),
    )(*bufs, send_sems, recv_sems, *after)
    return list(out)


def _phase_blocks(chip, phase, edges, parity=None):
    return [(b, blk) for b, (grp, blk) in enumerate(_shard_blocks(chip, edges))
            if grp == phase and (parity is None or b % 2 == parity)]


def _cols(ref, nblk):
    return ref.at[0:nblk * DH, :]


def _block_table(chip, edges, spare_g, spare_c):
    rows = []
    for s in range(4):
        sb = _shard_blocks(s, edges)
        rows.append([[blk if grp == "g" else spare_g for grp, blk in sb],
                     [blk if grp == "c" else spare_c for grp, blk in sb],
                     [int(grp == "g") for grp, _ in sb], [s] * ALIGNED_BLOCKS])
    return jnp.asarray(rows, jnp.int32)[chip]


def _place_own(a_shard, wo, cq, cw, bufs):
    d = a_shard.shape[1]
    chip = 2 * lax.axis_index("x") + lax.axis_index("y")

    def body(t_ref, a_ref, wo_ref, cq_ref, cw_ref, *refs):
        wg_ref, wc_ref, wog_ref, cqg_ref, cwg_ref = refs[5:]
        wg_ref[...] = a_ref[...]
        wc_ref[...] = a_ref[...]

        @pl.when(pl.program_id(0) == 0)
        def _():
            wog_ref[0] = wo_ref[...]
            cqg_ref[0] = cq_ref[...]
            cwg_ref[0] = cw_ref[...]

    whole = lambda s: pl.BlockSpec(s.shape, lambda b, t: (0,) * s.ndim)
    slot = lambda s: pl.BlockSpec((1,) + s.shape, lambda b, t: (t[3, 0],) + (0,) * s.ndim)
    return pl.pallas_call(
        body, name="place_own",
        grid_spec=pltpu.PrefetchScalarGridSpec(
            num_scalar_prefetch=1, grid=(ALIGNED_BLOCKS,),
            in_specs=[pl.BlockSpec((DH, d), lambda b, t: (b, 0)), whole(wo), whole(cq), whole(cw)] + [ANY] * 5,
            out_specs=[pl.BlockSpec((DH, d), lambda b, t: (t[0, b], 0)),
                       pl.BlockSpec((DH, d), lambda b, t: (t[1, b], 0)), slot(wo), slot(cq), slot(cw)]),
        out_shape=[jax.ShapeDtypeStruct(b.shape, b.dtype) for b in bufs],
        input_output_aliases={5 + a: a for a in range(5)},
        compiler_params=_params(("arbitrary",)),
    )(_block_table(chip, True, G_SPARE, C_SPARE), a_shard, wo, cq, cw, *bufs)


def _tie(x, token, name):
    def body(x_ref, t_ref, o_ref):
        del x_ref, t_ref, o_ref

    return pl.pallas_call(
        body, name=name, in_specs=[ANY, ANY], out_specs=ANY,
        out_shape=jax.ShapeDtypeStruct(x.shape, x.dtype), input_output_aliases={0: 0},
    )(x, token)


def _gather_start(phase, a_shard, w_grp, singles):
    ns = len(singles)

    def issue(refs, send_sems, recv_sems):
        a_ref, w_ref = refs[0], refs[1]
        x, y, c, chips = _place()
        mine = 2 * x + y
        for jj, (px, py) in enumerate(chips):
            to = dict(device_id=(px, py, c), device_id_type=MESH)
            for a in range(ns):
                pltpu.make_async_remote_copy(
                    src_ref=refs[2 + 2 * a], dst_ref=refs[3 + 2 * a].at[mine],
                    send_sem=send_sems.at[(1 + ns) * jj + 1 + a], recv_sem=recv_sems.at[(1 + ns) * jj + 1 + a],
                    **to).start()
        for s in range(4):
            for par in range(2):
                blocks = _phase_blocks(s, phase, True, par)
                if blocks:
                    @pl.when((mine == s) & (c == par))
                    def _():
                        for jj, (px, py) in enumerate(chips):
                            for b, blk in blocks:
                                pltpu.make_async_remote_copy(
                                    src_ref=_blk(a_ref, b), dst_ref=_blk(w_ref, blk),
                                    send_sem=send_sems.at[(1 + ns) * jj], recv_sem=recv_sems.at[(1 + ns) * jj],
                                    device_id=(px, py, c), device_id_type=MESH).start()

    bufs = [a_shard, w_grp] + [t for pair in singles for t in pair]
    return _split_start("gather_start_" + phase, issue, bufs, 3 * (1 + ns))


def _gather_wait(phase, send_sems, recv_sems, bufs, after):
    ns = (len(bufs) - 2) // 2

    def await_(refs, send_sems, recv_sems):
        a_ref, w_ref = refs[0], refs[1]
        x, y, c, chips = _place()
        mine = 2 * x + y
        for jj, (px, py) in enumerate(chips):
            to = dict(device_id=(px, py, c), device_id_type=MESH)
            peer = 2 * px + py
            for a in range(ns):
                cp = pltpu.make_async_remote_copy(
                    src_ref=refs[2 + 2 * a], dst_ref=refs[3 + 2 * a].at[mine],
                    send_sem=send_sems.at[(1 + ns) * jj + 1 + a], recv_sem=recv_sems.at[(1 + ns) * jj + 1 + a], **to)
                cp.wait_recv()
                cp.wait_send()
            for s in range(4):
                for par in range(2):
                    nblk = len(_phase_blocks(s, phase, True, par))
                    if nblk:
                        both = pltpu.make_async_remote_copy(
                            src_ref=_cols(a_ref, nblk), dst_ref=_cols(w_ref, nblk),
                            send_sem=send_sems.at[(1 + ns) * jj], recv_sem=recv_sems.at[(1 + ns) * jj], **to)

                        @pl.when((peer == s) & (c == par))
                        def _():
                            both.wait_recv()

                        @pl.when((mine == s) & (c == par))
                        def _():
                            both.wait_send()

    return _split_wait("gather_wait_" + phase, await_, send_sems, recv_sems, bufs, after)


def _sibling_forward_parts(phase):
    def each(w_ref, send_sems, recv_sems, start):
        x, y, c, chips = _place()
        to = dict(device_id=(x, y, 1 - c), device_id_type=MESH)
        for jj, (px, py) in enumerate(chips):
            peer = 2 * px + py
            for s in range(4):
                for par in range(2):
                    mine_blocks = _phase_blocks(s, phase, True, par)
                    theirs = len(_phase_blocks(s, phase, True, 1 - par))
                    if not (mine_blocks or theirs):
                        continue

                    @pl.when((peer == s) & (c == par))
                    def _():
                        if start:
                            for _, blk in mine_blocks:
                                pltpu.make_async_remote_copy(
                                    src_ref=_blk(w_ref, blk), dst_ref=_blk(w_ref, blk),
                                    send_sem=send_sems.at[jj], recv_sem=recv_sems.at[jj], **to).start()
                            return
                        if theirs:
                            pltpu.make_async_remote_copy(
                                src_ref=_cols(w_ref, theirs), dst_ref=_cols(w_ref, theirs),
                                send_sem=send_sems.at[jj], recv_sem=recv_sems.at[jj], **to).wait_recv()
                        if mine_blocks:
                            pltpu.make_async_remote_copy(
                                src_ref=_cols(w_ref, len(mine_blocks)), dst_ref=_cols(w_ref, len(mine_blocks)),
                                send_sem=send_sems.at[jj], recv_sem=recv_sems.at[jj], **to).wait_send()

    issue = lambda refs, send_sems, recv_sems: each(refs[0], send_sems, recv_sems, True)
    await_ = lambda refs, send_sems, recv_sems: each(refs[0], send_sems, recv_sems, False)
    return issue, await_


def _sibling_forward(phase, w_grp):
    issue, await_ = _sibling_forward_parts(phase)

    def body(w_in_ref, w_ref, send_sems, recv_sems):
        del w_in_ref
        issue([w_ref], send_sems, recv_sems)
        await_([w_ref], send_sems, recv_sems)

    return pl.pallas_call(
        body, name="sibling_forward_" + phase, in_specs=[ANY], out_specs=ANY,
        out_shape=jax.ShapeDtypeStruct(w_grp.shape, w_grp.dtype), input_output_aliases={0: 0},
        scratch_shapes=[pltpu.SemaphoreType.DMA((3,)), pltpu.SemaphoreType.DMA((3,))],
    )(w_grp)


def _merge_edges(w, edge0, mixed, name):
    d = w.shape[1]

    def body(e_ref, o_ref):
        o_ref[...] = e_ref[0:DH, :] + e_ref[DH:2 * DH, :]

    def to_block(i):
        r = mixed[-1]
        for kk in range(len(mixed) - 2, -1, -1):
            r = jnp.where(i == kk, mixed[kk], r)
        return r

    return pl.pallas_call(
        body, name=name, grid=(len(mixed),),
        in_specs=[pl.BlockSpec((2 * DH, d), lambda i: (edge0 // 2 + i, 0))],
        out_specs=pl.BlockSpec((DH, d), lambda i: (to_block(i), 0)),
        out_shape=jax.ShapeDtypeStruct(w.shape, w.dtype),
        input_output_aliases={0: 0},
        compiler_params=_params(("arbitrary",)),
    )(w)


def _scatter_start(phase, g_grp, land, singles, halved=False):
    ns = len(singles)

    def issue(refs, send_sems, recv_sems):
        g_ref, land_ref = refs[0], refs[1]
        x, y, c, chips = _place()
        for jj, (px, py) in enumerate(chips):
            to = dict(device_id=(px, py, c), device_id_type=MESH)
            peer = 2 * px + py
            for a in range(ns):
                pltpu.make_async_remote_copy(
                    src_ref=refs[2 + 2 * a].at[peer], dst_ref=refs[3 + 2 * a].at[jj],
                    send_sem=send_sems.at[(1 + ns) * jj + 1 + a], recv_sem=recv_sems.at[(1 + ns) * jj + 1 + a],
                    **to).start()
            for s in range(4):
                for par in ((0, 1) if halved else (None,)):
                    blocks = _phase_blocks(s, phase, False, par)
                    if blocks:
                        @pl.when((peer == s) if par is None else ((peer == s) & (c == par)))
                        def _():
                            for b, blk in blocks:
                                pltpu.make_async_remote_copy(
                                    src_ref=_blk(g_ref, blk), dst_ref=_blk(land_ref.at[jj], b),
                                    send_sem=send_sems.at[(1 + ns) * jj], recv_sem=recv_sems.at[(1 + ns) * jj],
                                    **to).start()

    bufs = [g_grp, land] + [t for pair in singles for t in pair]
    return _split_start("scatter_start_" + phase, issue, bufs, 3 * (1 + ns))


def _scatter_wait(phase, send_sems, recv_sems, bufs, after, halved=False):
    ns = (len(bufs) - 2) // 2

    def await_(refs, send_sems, recv_sems):
        g_ref, land_ref = refs[0], refs[1]
        x, y, c, chips = _place()
        mine = 2 * x + y
        for jj, (px, py) in enumerate(chips):
            to = dict(device_id=(px, py, c), device_id_type=MESH)
            peer = 2 * px + py
            for a in range(ns):
                cp = pltpu.make_async_remote_copy(
                    src_ref=refs[2 + 2 * a].at[peer], dst_ref=refs[3 + 2 * a].at[jj],
                    send_sem=send_sems.at[(1 + ns) * jj + 1 + a], recv_sem=recv_sems.at[(1 + ns) * jj + 1 + a], **to)
                cp.wait_recv()
                cp.wait_send()
            for s in range(4):
                for par in ((0, 1) if halved else (None,)):
                    nblk = len(_phase_blocks(s, phase, False, par))
                    if nblk:
                        both = pltpu.make_async_remote_copy(
                            src_ref=_cols(g_ref, nblk), dst_ref=_cols(land_ref.at[jj], nblk),
                            send_sem=send_sems.at[(1 + ns) * jj], recv_sem=recv_sems.at[(1 + ns) * jj], **to)

                        @pl.when((mine == s) if par is None else ((mine == s) & (c == par)))
                        def _():
                            both.wait_recv()

                        @pl.when((peer == s) if par is None else ((peer == s) & (c == par)))
                        def _():
                            both.wait_send()

    return _split_wait("scatter_wait_" + phase, await_, send_sems, recv_sems, bufs, after)


def _needed_blocks(phase, parity):
    return sorted({blk for s in range(4) for _, blk in _phase_blocks(s, phase, False, parity)})


def _pair_reduce(phase, g_grp):
    n, d = g_grp.shape

    def swap(g_ref, sib_ref, send_sem, recv_sem):
        x, y, c, _ = _place()
        to = dict(device_id=(x, y, 1 - c), device_id_type=MESH)
        for par in range(2):
            give, get = _needed_blocks(phase, 1 - par), _needed_blocks(phase, par)

            @pl.when(c == par)
            def _():
                for blk in give:
                    pltpu.make_async_remote_copy(src_ref=_blk(g_ref, blk), dst_ref=_blk(sib_ref, blk),
                                                 send_sem=send_sem, recv_sem=recv_sem, **to).start()
                pltpu.make_async_remote_copy(src_ref=_cols(g_ref, len(get)), dst_ref=_cols(sib_ref, len(get)),
                                             send_sem=send_sem, recv_sem=recv_sem, **to).wait_recv()
                pltpu.make_async_remote_copy(src_ref=_cols(g_ref, len(give)), dst_ref=_cols(sib_ref, len(give)),
                                             send_sem=send_sem, recv_sem=recv_sem, **to).wait_send()

    sib = pl.pallas_call(
        swap, name="pair_swap_" + phase, in_specs=[ANY], out_specs=ANY,
        out_shape=jax.ShapeDtypeStruct((n, d), g_grp.dtype),
        scratch_shapes=[pltpu.SemaphoreType.DMA, pltpu.SemaphoreType.DMA],
    )(*_in_hbm(g_grp))

    lists = [_needed_blocks(phase, par) for par in range(2)]
    longest = max(len(t) for t in lists)
    table = jnp.asarray([t + [t[-1]] * (longest - len(t)) for t in lists], jnp.int32)[lax.axis_index("c")]

    def add(t_ref, a_ref, b_ref, o_ref):
        o_ref[...] = (a_ref[...].astype(F32) + b_ref[...].astype(F32)).astype(o_ref.dtype)

    blk = pl.BlockSpec((DH, d), lambda i, t: (t[i], 0))
    return pl.pallas_call(
        add, name="pair_add_" + phase,
        grid_spec=pltpu.PrefetchScalarGridSpec(num_scalar_prefetch=1, grid=(longest,),
                                               in_specs=[blk, blk], out_specs=blk),
        out_shape=jax.ShapeDtypeStruct((n, d), g_grp.dtype),
        compiler_params=_params(("arbitrary",)),
    )(table, g_grp, sib)


def _sum_shard(g_g, g_c, land):
    d = g_g.shape[1]
    chip = 2 * lax.axis_index("x") + lax.axis_index("y")

    def body(t_ref, gg_ref, gc_ref, land_ref, o_ref):
        b = pl.program_id(0)
        in_g = t_ref[2, b] == 1
        own = jnp.where(in_g, gg_ref[...].astype(F32), gc_ref[...].astype(F32))
        for jj in range(3):
            own = own + land_ref[jj].astype(F32)
        o_ref[...] = jnp.where(in_g & (b % 2 != lax.axis_index("c")), 0.0, own)

    return pl.pallas_call(
        body, name="sum_w_in",
        grid_spec=pltpu.PrefetchScalarGridSpec(
            num_scalar_prefetch=1, grid=(ALIGNED_BLOCKS,),
            in_specs=[pl.BlockSpec((DH, d), lambda b, t: (t[0, b], 0)), pl.BlockSpec((DH, d), lambda b, t: (t[1, b], 0)),
                      pl.BlockSpec((3, DH, d), lambda b, t: (0, b, 0))],
            out_specs=pl.BlockSpec((DH, d), lambda b, t: (b, 0))),
        out_shape=jax.ShapeDtypeStruct((ALIGNED_W, d), F32),
        compiler_params=_params(("arbitrary",)),
    )(_block_table(chip, False, 0, 0), g_g, g_c, land)


def _sum_rows(stack, land, rows):
    _, r, d = stack.shape
    rows = min(rows, r)
    chip = 2 * lax.axis_index("x") + lax.axis_index("y")

    def body(t_ref, own_ref, land_ref, o_ref):
        acc = own_ref[0].astype(F32)
        for jj in range(3):
            acc = acc + land_ref[jj].astype(F32)
        o_ref[...] = acc

    return pl.pallas_call(
        body, name="sum_w_out",
        grid_spec=pltpu.PrefetchScalarGridSpec(
            num_scalar_prefetch=1, grid=(r // rows,),
            in_specs=[pl.BlockSpec((1, rows, d), lambda i, t: (t[0], i, 0)),
                      pl.BlockSpec((3, rows, d), lambda i, t: (0, i, 0))],
            out_specs=pl.BlockSpec((rows, d), lambda i, t: (i, 0))),
        out_shape=jax.ShapeDtypeStruct((r, d), F32),
        compiler_params=_params(("arbitrary",)),
    )(jnp.reshape(chip, (1,)).astype(jnp.int32), stack, land)


def _exchange_parts(n_swap, with_pack):
    def copies(refs, send_sems, recv_sems):
        x, y, c, _ = _place()
        me = 4 * x + 2 * y + c
        cps = [pltpu.make_async_remote_copy(
            src_ref=refs[2 * a], dst_ref=refs[2 * a + 1], send_sem=send_sems.at[a], recv_sem=recv_sems.at[a],
            device_id=(x, y, 1 - c), device_id_type=MESH) for a in range(n_swap)]
        if with_pack:
            pack_ref, packs = refs[2 * n_swap], refs[2 * n_swap + 1]
            for r in range(1, 8):
                dx, dy, dc = (r >> 2) & 1, (r >> 1) & 1, r & 1
                peer = (x + dx - 2 * x * dx, y + dy - 2 * y * dy, c + dc - 2 * c * dc)
                cps.append(pltpu.make_async_remote_copy(
                    src_ref=pack_ref, dst_ref=packs.at[me], send_sem=send_sems.at[n_swap + r - 1],
                    recv_sem=recv_sems.at[n_swap + r - 1], device_id=peer, device_id_type=MESH))
        return cps

    def issue(refs, send_sems, recv_sems):
        for cp in copies(refs, send_sems, recv_sems):
            cp.start()

    def await_(refs, send_sems, recv_sems):
        cps = copies(refs, send_sems, recv_sems)
        for cp in cps:
            cp.wait_recv()
        for cp in cps:
            cp.wait_send()

    return issue, await_, n_swap + (7 if with_pack else 0)


def _sum_packs(pack, packs):
    x, y, c = lax.axis_index("x"), lax.axis_index("y"), lax.axis_index("c")
    me = jnp.reshape(4 * x + 2 * y + c, (1,)).astype(jnp.int32)

    def body(me_ref, own_ref, p_ref, o_ref):
        acc = jnp.where(me_ref[0] == 0, own_ref[...], p_ref[0])
        for d in range(1, 8):
            acc = acc + jnp.where(me_ref[0] == d, own_ref[...], p_ref[d])
        o_ref[...] = acc

    full = lambda s: pl.BlockSpec(s.shape, lambda i, t: (0,) * s.ndim)
    return pl.pallas_call(
        body, name="sum_packs",
        grid_spec=pltpu.PrefetchScalarGridSpec(num_scalar_prefetch=1, grid=(1,), in_specs=[full(pack), full(packs)],
                                               out_specs=full(pack)),
        out_shape=jax.ShapeDtypeStruct(pack.shape, F32),
    )(me, pack, packs)


def _adamw_update(g, w_ref, m_ref, v_ref, go, do, mo, vo):
    c1 = 1.0 / (1.0 - ADAM_B1 ** ADAM_STEP)
    c2 = 1.0 / (1.0 - ADAM_B2 ** ADAM_STEP)
    mn = ADAM_B1 * m_ref[...] + (1.0 - ADAM_B1) * g
    vn = ADAM_B2 * v_ref[...] + (1.0 - ADAM_B2) * (g * g)
    go[...] = g
    mo[...] = mn
    vo[...] = vn
    do[...] = -ADAM_LR * ((mn * c1) / (jnp.sqrt(vn * c2) + ADAM_EPS) + ADAM_WD * w_ref[...])


def _adamw(w, m, v, g1, g2, rows, name):
    r, cdim = w.shape
    rows = min(rows, r)

    def body(*refs):
        n_in = 4 if g2 is None else 5
        w_ref, m_ref, v_ref, g_ref = refs[:4]
        g = g_ref[...] if g2 is None else g_ref[...] + refs[4][...]
        _adamw_update(g, w_ref, m_ref, v_ref, *refs[n_in:n_in + 4])

    blk = pl.BlockSpec((rows, cdim), lambda i: (i, 0))
    args = [w, m, v, g1] + ([] if g2 is None else [g2])
    shp = jax.ShapeDtypeStruct((r, cdim), F32)
    return pl.pallas_call(
        body, name=name, grid=(r // rows,),
        in_specs=[blk] * len(args), out_specs=[blk] * 4, out_shape=[shp] * 4,
        compiler_params=_params(("parallel",), 20 * rows * cdim * 4 + 8 * 2**20),
    )(*_in_hbm(*args))


def _adamw_shard(wt, mt, vt, g1, g2):
    r, d = wt.shape
    cols = min(128, d)

    def body(w_ref, m_ref, v_ref, g_ref, g2_ref, go, do, mo, vo, pad_ref):
        chip = 2 * lax.axis_index("x") + lax.axis_index("y")
        back = [(ALIGNED_W - s) % ALIGNED_W for s in SHIFTS]
        pad_ref[...] = pltpu.roll(g_ref[...] + g2_ref[...], _by_chip(chip, back), 0)
        outs = [o.at[:, 0, :] for o in (go, do, mo, vo)]
        _adamw_update(pad_ref[0:r, :], w_ref, m_ref, v_ref, *outs)

    blk = pl.BlockSpec((r, cols), lambda i: (0, i))
    gblk = pl.BlockSpec((ALIGNED_W, cols), lambda i: (0, i))
    oblk = pl.BlockSpec((r, 1, cols), lambda i: (0, 0, i))
    shp = jax.ShapeDtypeStruct((r, 1, d), F32)
    return pl.pallas_call(
        body, name="adamw_w_in", grid=(d // cols,),
        in_specs=[blk] * 3 + [gblk] * 2, out_specs=[oblk] * 4, out_shape=[shp] * 4,
        scratch_shapes=[pltpu.VMEM((ALIGNED_W, cols), F32)],
        compiler_params=_params(("parallel",), 24 * ALIGNED_W * cols * 4 + 8 * 2**20),
    )(wt, mt, vt, g1, g2)


def _pad_lanes(a, width):
    return jnp.pad(a, ((0, 0), (0, width - a.shape[1])))


def _gathered_to_full(g):
    return jnp.transpose(g, (1, 0, 2)).reshape(g.shape[1], 4 * g.shape[2])


def _row(a):
    return _pad_lanes(a.reshape(1, -1), 1024)


def _small_pack(nin, cb, fn, al, dt, gn, cqw_shard, cw_shard):
    ad = jnp.concatenate([al.reshape(1, -1), dt.reshape(1, -1)], axis=1)
    rows = [_row(nin), _row(cb), _row(fn), _row(ad), _row(gn), cqw_shard.reshape(3, 1024), _row(cw_shard)]
    out = jnp.concatenate(rows, axis=0)
    return jnp.pad(out, ((0, 16 - out.shape[0]), (0, 0)))


def kernel(x, norm_in_w, w_in, conv_qkv_w, A_log, dt_bias, gdn_norm_w, conv_w, conv_b, w_out, final_norm_w, loss_target, m_norm_in_w, m_w_in, m_conv_qkv_w, m_A_log, m_dt_bias, m_gdn_norm_w, m_conv_w, m_conv_b, m_w_out, m_final_norm_w, v_norm_in_w, v_w_in, v_conv_qkv_w, v_A_log, v_dt_bias, v_gdn_norm_w, v_conv_w, v_conv_b, v_w_out, v_final_norm_w):
    chip = 2 * lax.axis_index("x") + lax.axis_index("y")
    a_shard = _align_shard(jnp.transpose(w_in, (2, 0, 1)))
    wo_b = _cast_bf16(w_out[0], 256, "cast_w_out")
    d_model = x.shape[-1]
    stack = lambda s: lax.empty((4,) + s.shape, s.dtype)
    wg0 = lax.empty((WG_BLOCKS * DH, d_model), BF16)
    wc0 = lax.empty((WC_BLOCKS * DH, d_model), BF16)
    ss_g, rs_g, bufs_g, tok_g = _gather_start("g", a_shard, wg0, [(conv_qkv_w[0], stack(conv_qkv_w[0]))])
    wg1, wc1, wog1, cqg1, cwg1 = _place_own(bufs_g[0], wo_b, bufs_g[2], conv_w[0],
                                            [bufs_g[1], wc0, stack(wo_b), bufs_g[3], stack(conv_w[0])])
    x0 = x[0]
    h = _rms_in(x0, _tie(norm_in_w, tok_g, "after_gather_start_g"))
    adam_in = [jnp.transpose(a[0]) for a in (w_in, m_w_in, v_w_in)]
    sp = lambda nin, cb, fn, al, dt, gn, cq, cwv: _small_pack(nin, cb, fn, al, dt, gn, cq[0], cwv[0])
    w_s = sp(norm_in_w, conv_b, final_norm_w, A_log, dt_bias, gdn_norm_w, conv_qkv_w, conv_w)
    m_s = sp(m_norm_in_w, m_conv_b, m_final_norm_w, m_A_log, m_dt_bias, m_gdn_norm_w, m_conv_qkv_w, m_conv_w)
    v_s = sp(v_norm_in_w, v_conv_b, v_final_norm_w, v_A_log, v_dt_bias, v_gdn_norm_w, v_conv_qkv_w, v_conv_w)
    a_thru, wg, _, cq_g = _gather_wait("g", ss_g, rs_g, [bufs_g[0], wg1, bufs_g[2], cqg1],
                                       [h, w_s, m_s, v_s] + adam_in[1:])
    ss_c, rs_c, bufs_c, tok_c = _gather_start("c", a_thru, wc1, [(conv_w[0], cwg1), (wo_b, wog1)])
    w_g = _merge_edges(_sibling_forward("g", _tie(wg, tok_c, "after_gather_start_c")),
                       G_EDGE, G_MIXED, "merge_edges_g")
    cqw = _gathered_to_full(cq_g)
    ad = jnp.pad(jnp.concatenate([A_log, dt_bias], axis=0), ((0, 0), (A_LANE, 0)))
    fwd_c = {}

    def on_q(q):
        _, wc, _, cw_g, _, wo_g = _gather_wait("c", ss_c, rs_c, bufs_c, q)
        issue, _ = _sibling_forward_parts("c")
        ss, rs, (wc,), tok = _split_start("sibling_forward_start_c", issue, [wc], 3)
        fwd_c.update(ss=ss, rs=rs, wc=wc, cw_g=cw_g, wo_g=wo_g)
        return _tie(q, tok, "after_sibling_forward_start_c")

    def late(o):
        _, await_ = _sibling_forward_parts("c")
        (wc,) = _split_wait("sibling_forward_wait_c", await_, fwd_c["ss"], fwd_c["rs"], [fwd_c["wc"]], o)
        return (_merge_edges(wc, C_EDGE, C_MIXED, "merge_edges_c"), fwd_c["wo_g"].reshape(2 * GW, d_model),
                _gathered_to_full(fwd_c["cw_g"]))

    scat = {}

    def on_grad_c(g_c, g_wout, do):
        go4 = g_wout.reshape(4, GW // 2, d_model)
        land = lax.empty((3, ALIGNED_W, d_model), BF16)
        land_o = lax.empty((3, GW // 2, d_model), BF16)
        ss, rs, bufs, tok = _scatter_start("c", g_c, land, [(go4, land_o)])
        scat["c"] = (ss, rs, bufs)
        return _tie(do, tok, "after_scatter_start_c")

    def on_grad_g(g_g, dproj_g):
        ss, rs, bufs, tok = _scatter_start("g", _pair_reduce("g", g_g), scat["c"][2][1], [], halved=True)
        scat["g"] = (ss, rs, bufs)
        return _tie(dproj_g, tok, "after_scatter_start_g")

    gx, sm, _ = _local_step(x0, loss_target[0], h, w_g, cqw, late, norm_in_w, ad, gdn_norm_w, conv_b,
                            final_norm_w.reshape(1, -1), on_grad_c, on_grad_g, on_q)

    ss, rs, bufs = scat["c"]
    g_c, land, go4, land_o = _scatter_wait("c", ss, rs, [bufs[0], scat["g"][2][1], bufs[2], bufs[3]], gx)
    part_out = _sum_rows(go4, land_o, 128)
    ad_g = jnp.concatenate([sm["al"][:, A_LANE:], sm["dt"][:, A_LANE:]], axis=1)
    pack = jnp.concatenate([_row(sm["nin"]), _row(sm["cb"]), _row(sm["fn"]), _row(ad_g), _row(sm["gn"]),
                            jnp.concatenate(sm["cq"], axis=1).reshape(12, 1024), sm["cw"], _row(sm["loss"])], axis=0)
    pack = jnp.pad(pack, ((0, PACK_ROWS - pack.shape[0]), (0, 0)))
    issue, await_a, nsem = _exchange_parts(1, True)
    ss_a, rs_a, bufs_a, tok_a = _split_start(
        "exchange_start_small", issue,
        [part_out, lax.empty(part_out.shape, F32), pack, lax.empty((8,) + pack.shape, F32)], nsem)
    ss, rs, bufs = scat["g"]
    g_g, land = _scatter_wait("g", ss, rs, [bufs[0], land], [gx, tok_a], halved=True)
    part_in = _sum_shard(g_g, g_c, land)
    issue, await_b, nsem = _exchange_parts(1, False)
    ss_b, rs_b, bufs_b, tok_b = _split_start("exchange_start_w_in", issue,
                                             [part_in, lax.empty(part_in.shape, F32)], nsem)
    part_out, sib_out, pack, packs = _split_wait("exchange_wait_small", await_a, ss_a, rs_a, bufs_a, tok_b)
    tot = _sum_packs(pack, packs)
    g_wo, d_wo, m_wo, v_wo = _adamw(w_out[0], m_w_out[0], v_w_out[0], part_out, sib_out, 128, "adamw_w_out")
    g_cq_sh = lax.dynamic_slice_in_dim(tot[R_CQ:R_CQ + 12].reshape(4, 3 * GW), chip * 768, 768, axis=1)
    g_cw_sh = lax.dynamic_slice_in_dim(tot[R_CW:R_CW + 3], chip * 256, 256, axis=1)
    g_s = _small_pack(tot[R_NIN], tot[R_CB], tot[R_FN], tot[R_AD, :HEADS], tot[R_AD, HEADS:2 * HEADS],
                      tot[R_GN, :DH], g_cq_sh, g_cw_sh)
    small = _adamw(w_s, m_s, v_s, g_s, None, 16, "adamw_small")
    part_in, sib_in = _split_wait("exchange_wait_w_in", await_b, ss_b, rs_b, bufs_b, [small[0], d_wo])
    g_wi, d_wi, m_wi, v_wi = [jnp.transpose(a, (1, 2, 0))[0] for a in _adamw_shard(*adam_in, part_in, sib_in)]

    def unpack(a, big_in, big_out):
        return (a[0:1], big_in[None], a[5:8].reshape(1, 4, 768), a[3:4, :HEADS], a[3:4, HEADS:2 * HEADS],
                a[4:5, :DH], a[8, :768].reshape(1, 3, 256), a[1:2], big_out[None], a[2])

    loss = tot[R_LOSS, 0]
    return (loss, gx[None], *unpack(small[0], g_wi, g_wo), *unpack(small[1], d_wi, d_wo),
            *unpack(small[2], m_wi, m_wo), *unpack(small[3], v_wi, v_wo))
```

```python
import functools
import math

import jax
import jax.numpy as jnp
from jax import lax
from jax.experimental import pallas as pl
from jax.experimental.pallas import tpu as pltpu

F32 = jnp.float32
BF16 = jnp.bfloat16
MESH = pl.DeviceIdType.MESH
ANY = pl.BlockSpec(memory_space=pl.ANY)

HEADS = 8
DH = 128
CH = 64
GW = HEADS * DH
EPS = 1e-6
VMEM_V7X = 64 * 1024 * 1024

QB, KB, VB, ZB, BAB = 0, 8, 16, 24, 32
A_LANE = 120
NG, NC = 33, 32
GW_COLS, CW_COLS = NG * DH, NC * DH

SHARD_W = 2052
ALIGNED_BLOCKS = 17
ALIGNED_W = ALIGNED_BLOCKS * DH
SHIFTS = (0, 4, ALIGNED_W - 8, ALIGNED_W - 4)
G_EDGE, C_EDGE = 34, 32
G_SPARE, C_SPARE = 33, 34
WG_BLOCKS, WC_BLOCKS = 38, 36
G_MIXED, C_MIXED = (2, BAB), (4 * 7 + 1,)


def _shard_blocks(chip, edges):
    g, c = "g", "c"
    if chip == 0:
        out = [(g, 3 * b) for b in range(8)] + [(g, 3 * b + 1) for b in range(8)] + [(g, G_EDGE, G_MIXED[0])]
    elif chip == 1:
        out = [(g, G_EDGE + 1, G_MIXED[0])] + [(g, 3 * b + 2) for b in range(1, 8)]
        out += [(g, ZB + b) for b in range(8)] + [(g, G_EDGE + 2, G_MIXED[1])]
    elif chip == 2:
        out = [(c, 4 * b) for b in range(8)] + [(c, 4 * b + 1) for b in range(7)]
        out += [(c, C_EDGE, C_MIXED[0]), (g, G_EDGE + 3, G_MIXED[1])]
    else:
        out = [(c, 4 * b + 2) for b in range(8)] + [(c, 4 * b + 3) for b in range(8)] + [(c, C_EDGE + 1, C_MIXED[0])]
    return [(o[0], o[1] if (edges or len(o) == 2) else o[2]) for o in out]


def _by_chip(chip, vals):
    if all(v == vals[0] for v in vals):
        return vals[0]
    r = vals[3]
    for kk in (2, 1, 0):
        r = jnp.where(chip == kk, vals[kk], r)
    return r

ADAM_LR, ADAM_B1, ADAM_B2, ADAM_EPS, ADAM_WD, ADAM_STEP = 0.001, 0.9, 0.999, 1e-08, 0.01, 10

R_NIN, R_CB, R_FN, R_AD, R_GN, R_CQ, R_CW, R_LOSS, PACK_ROWS = 0, 1, 2, 3, 4, 5, 17, 20, 24

NN = ((1,), (0,))
NT = ((1,), (1,))
TN = ((0,), (0,))


def _dot(a, b, dims=NN, mode="lo"):
    dn = (dims, ((), ()))
    if mode == "hi":
        return lax.dot_general(a, b, dn, precision=lax.Precision.HIGHEST, preferred_element_type=F32)
    ah, bh = a.astype(BF16), b.astype(BF16)
    out = lax.dot_general(ah, bh, dn, preferred_element_type=F32)
    if mode == "x3":
        al = (a - ah.astype(F32)).astype(BF16)
        bl = (b - bh.astype(F32)).astype(BF16)
        out = out + lax.dot_general(ah, bl, dn, preferred_element_type=F32)
        out = out + lax.dot_general(al, bh, dn, preferred_element_type=F32)
    return out


P_GRAM, P_INV, P_SOL, P_SCAN, P_SCANB, P_BWD = "lo", "lo", "lo", "lo", "lo", "lo"
P_CUM = "x3"


def _params(sem=None, vmem=None):
    kw = {}
    if sem is not None:
        kw["dimension_semantics"] = sem
    if vmem is not None:
        kw["vmem_limit_bytes"] = int(min(max(vmem, 32 * 2**20), VMEM_V7X - 8 * 2**20))
    return pltpu.CompilerParams(**kw)


def _in_hbm(*arrays):
    return [pltpu.with_memory_space_constraint(a, pltpu.HBM) for a in arrays]


def _sigmoid(x):
    return 1.0 / (1.0 + jnp.exp(-x))


def _dsilu(x, s):
    return s * (1.0 + x * (1.0 - s))


def _rows(shape):
    return lax.broadcasted_iota(jnp.int32, shape, 0)


def _shift_down(x, s):
    if s == 0:
        return x
    return jnp.where(_rows(x.shape) >= s, pltpu.roll(x, s, 0), 0.0)


def _shift_up(x, s):
    if s == 0:
        return x
    n = x.shape[0]
    return jnp.where(_rows(x.shape) < n - s, pltpu.roll(x, n - s, 0), 0.0)


def _matmul(a, b, dims, out_dtype, tm, tn, tk, name, add=None, n=None, b_outer=False):
    if dims == NN:
        (m, k), n = a.shape, b.shape[1]
    elif dims == NT:
        (m, k), n = a.shape, (n or b.shape[0])
    else:
        (k, m), n = a.shape, b.shape[1]
    tm, tn, tk = min(tm, m), min(tn, n), min(tk, k)
    assert m % tm == 0 and n % tn == 0 and k % tk == 0, (name, m, n, k, tm, tn, tk)
    nk = k // tk

    def body(*refs):
        if add is None:
            a_ref, b_ref, o_ref = refs[:3]
            add_ref = None
        else:
            a_ref, b_ref, add_ref, o_ref = refs[:4]
        part = _dot(a_ref[...], b_ref[...], dims)
        if nk == 1:
            if add_ref is not None:
                part = part + add_ref[...]
            o_ref[...] = part.astype(out_dtype)
            return
        acc = refs[-1]
        kk = pl.program_id(2)

        @pl.when(kk == 0)
        def _():
            acc[...] = part

        @pl.when(kk > 0)
        def _():
            acc[...] += part

        @pl.when(kk == nk - 1)
        def _():
            r = acc[...]
            if add_ref is not None:
                r = r + add_ref[...]
            o_ref[...] = r.astype(out_dtype)

    ij = (lambda g0, g1: (g1, g0)) if b_outer else (lambda g0, g1: (g0, g1))

    def spec(shape, pick):
        return pl.BlockSpec(shape, lambda g0, g1, kk: pick(*ij(g0, g1), kk))

    a_spec = spec((tk, tm), lambda i, j, kk: (kk, i)) if dims == TN else spec((tm, tk), lambda i, j, kk: (i, kk))
    b_spec = spec((tn, tk), lambda i, j, kk: (j, kk)) if dims == NT else spec((tk, tn), lambda i, j, kk: (kk, j))
    o_spec = spec((tm, tn), lambda i, j, kk: (i, j))
    in_specs = [a_spec, b_spec]
    args = [a, b]
    if add is not None:
        in_specs.append(o_spec)
        args.append(add)
    osz = jnp.dtype(out_dtype).itemsize
    est = 2 * (tm * tk * a.dtype.itemsize + tk * tn * b.dtype.itemsize + tm * tn * osz)
    est += 3 * tm * tn * 4 + (2 * tm * tn * 4 if add is not None else 0)
    return pl.pallas_call(
        body, name=name, grid=(n // tn, m // tm, nk) if b_outer else (m // tm, n // tn, nk),
        in_specs=in_specs, out_specs=o_spec,
        out_shape=jax.ShapeDtypeStruct((m, n), out_dtype),
        scratch_shapes=[pltpu.VMEM((tm, tn), F32)] if nk > 1 else [],
        compiler_params=_params(("parallel", "parallel", "arbitrary"), est + 8 * 2**20),
    )(*args)


def _cast_bf16(a, rows, name):
    r, c = a.shape
    rows = min(rows, r)

    def body(a_ref, o_ref):
        o_ref[...] = a_ref[...].astype(BF16)

    return pl.pallas_call(
        body, name=name, grid=(r // rows,),
        in_specs=[pl.BlockSpec((rows, c), lambda i: (i, 0))],
        out_specs=pl.BlockSpec((rows, c), lambda i: (i, 0)),
        out_shape=jax.ShapeDtypeStruct((r, c), BF16),
        compiler_params=_params(("parallel",)),
    )(a)


def _align_shard(wt):
    r, _, d = wt.shape
    cols = min(256, d)

    def body(w_ref, o_ref, pad_ref):
        chip = 2 * lax.axis_index("x") + lax.axis_index("y")
        pad_ref[...] = jnp.zeros_like(pad_ref)
        pad_ref[0:r, :] = w_ref[:, 0, :]
        o_ref[...] = pltpu.roll(pad_ref[...], _by_chip(chip, SHIFTS), 0).astype(BF16)

    return pl.pallas_call(
        body, name="align_shard", grid=(d // cols,),
        in_specs=[pl.BlockSpec((r, 1, cols), lambda i: (0, 0, i))],
        out_specs=pl.BlockSpec((ALIGNED_W, cols), lambda i: (0, i)),
        out_shape=jax.ShapeDtypeStruct((ALIGNED_W, d), BF16),
        scratch_shapes=[pltpu.VMEM((ALIGNED_W, cols), F32)],
        compiler_params=_params(("parallel",)),
    )(wt)


def _rms_in(x, w):
    n, d = x.shape
    tr = min(256, n)

    def body(x_ref, w_ref, h_ref):
        xv = x_ref[...]
        r = lax.rsqrt(jnp.mean(xv * xv, axis=-1, keepdims=True) + EPS)
        h_ref[...] = (xv * r * w_ref[...]).astype(BF16)

    return pl.pallas_call(
        body, name="rms_in", grid=(n // tr,),
        in_specs=[pl.BlockSpec((tr, d), lambda i: (i, 0)), pl.BlockSpec((1, d), lambda i: (0, 0))],
        out_specs=pl.BlockSpec((tr, d), lambda i: (i, 0)),
        out_shape=jax.ShapeDtypeStruct((n, d), BF16),
        compiler_params=_params(("parallel",)),
    )(x, w)


def _conv_silu(p, w_ref, taps):
    c = None
    for j in range(taps):
        t = _shift_down(p, taps - 1 - j) * w_ref[j:j + 1, :]
        c = t if c is None else c + t
    return c


def _prep_qkv(proj, cw):
    n = proj.shape[0]

    def body(p3, wq, wk, wv, q_ref, k_ref, v_ref):
        for kind, (w_ref, o_ref) in enumerate(((wq, q_ref), (wk, k_ref), (wv, v_ref))):
            c = _conv_silu(p3[:, kind * DH:(kind + 1) * DH], w_ref, 4)
            a = c * _sigmoid(c)
            if kind < 2:
                r = lax.rsqrt(jnp.sum(a * a, axis=-1, keepdims=True) + EPS)
                a = a * (r * (DH ** -0.5 if kind == 0 else 1.0))
            o_ref[...] = a

    col = pl.BlockSpec((n, DH), lambda h: (0, h))
    wcol = lambda base: pl.BlockSpec((4, DH), lambda h: (0, base + h))
    out = jax.ShapeDtypeStruct((n, GW), F32)
    return pl.pallas_call(
        body, name="prep_qkv", grid=(HEADS,),
        in_specs=[pl.BlockSpec((n, 3 * DH), lambda h: (0, h)), wcol(QB), wcol(KB), wcol(VB)],
        out_specs=[col] * 3, out_shape=[out] * 3,
        compiler_params=_params(("parallel",), 40 * 2**20),
    )(proj, cw, cw, cw)


def _prep_qkv_bwd(proj, cw, dq, dk, dv, dproj):
    n = proj.shape[0]

    def body(p3, wq, wk, wv, dq_ref, dk_ref, dv_ref, _, o3, gq, gk, gv):
        for kind, (w_ref, d_ref, g_ref) in enumerate(((wq, dq_ref, gq), (wk, dk_ref, gk), (wv, dv_ref, gv))):
            p = p3[:, kind * DH:(kind + 1) * DH]
            shifted = [_shift_down(p, 3 - j) for j in range(4)]
            c = shifted[0] * w_ref[0:1, :]
            for j in range(1, 4):
                c = c + shifted[j] * w_ref[j:j + 1, :]
            s = _sigmoid(c)
            a = c * s
            d = d_ref[...]
            if kind < 2:
                r = lax.rsqrt(jnp.sum(a * a, axis=-1, keepdims=True) + EPS)
                sc = DH ** -0.5 if kind == 0 else 1.0
                d = (sc * r) * (d - a * ((r * r) * jnp.sum(d * a, axis=-1, keepdims=True)))
            dc = d * _dsilu(c, s)
            dp = None
            for j in range(4):
                g_ref[j:j + 1, :] = jnp.sum(dc * shifted[j], axis=0, keepdims=True)
                t = _shift_up(dc, 3 - j) * w_ref[j:j + 1, :]
                dp = t if dp is None else dp + t
            o3[:, kind * DH:(kind + 1) * DH] = dp.astype(BF16)

    col = pl.BlockSpec((n, DH), lambda h: (0, h))
    wcol = lambda base: pl.BlockSpec((4, DH), lambda h: (0, base + h))
    p3spec = pl.BlockSpec((n, 3 * DH), lambda h: (0, h))
    return pl.pallas_call(
        body, name="prep_qkv_bwd", grid=(HEADS,),
        in_specs=[p3spec, wcol(QB), wcol(KB), wcol(VB), col, col, col, ANY],
        out_specs=[p3spec] + [wcol(0)] * 3,
        out_shape=[jax.ShapeDtypeStruct(dproj.shape, BF16)] + [jax.ShapeDtypeStruct((4, GW), F32)] * 3,
        input_output_aliases={7: 0},
        compiler_params=_params(("parallel",), 48 * 2**20),
    )(proj, cw, cw, cw, dq, dk, dv, dproj)


CPB = 8
SCAN_CPS = 4


def _tri(lower, rows):
    i = lax.broadcasted_iota(jnp.int32, (rows, rows), 0)
    j = lax.broadcasted_iota(jnp.int32, (rows, rows), 1)
    return jnp.where((i // CH == j // CH) & ((i >= j) if lower else (j >= i)), 1.0, 0.0)


def _lane(shape):
    return lax.broadcasted_iota(jnp.int32, shape, 1)


def _prep_bg(proj, ad):
    n = proj.shape[0]
    nch = n // CH
    cpb = CPB if nch % CPB == 0 else 1
    rows = cpb * CH

    def body(p_ref, ad_ref, bg_ref, bgt_ref):
        p = p_ref[...]
        lane = _lane(p.shape)
        beta = _sigmoid(p)
        xa = p + ad_ref[1:2, :]
        sp = jnp.maximum(xa, 0.0) + jnp.log(1.0 + jnp.exp(-jnp.abs(xa)))
        g = pltpu.roll(-jnp.exp(ad_ref[0:1, :]) * sp, DH - A_LANE + HEADS, 1)
        gc = _dot(_tri(True, rows), g, NN, P_CUM)
        bg = jnp.where(lane < HEADS, beta, jnp.where(lane < 2 * HEADS, gc, 0.0))
        bg_ref[...] = bg
        for ci in range(cpb):
            bgt_ref[ci] = bg[ci * CH:(ci + 1) * CH, :].T

    return pl.pallas_call(
        body, name="prep_bg", grid=(nch // cpb,),
        in_specs=[pl.BlockSpec((rows, DH), lambda i: (i, BAB)), pl.BlockSpec((2, DH), lambda i: (0, 0))],
        out_specs=[pl.BlockSpec((rows, DH), lambda i: (i, 0)), pl.BlockSpec((cpb, DH, CH), lambda i: (i, 0, 0))],
        out_shape=[jax.ShapeDtypeStruct((n, DH), F32), jax.ShapeDtypeStruct((nch, DH, CH), F32)],
        compiler_params=_params(("parallel",)),
    )(*_in_hbm(proj, ad))


def _prep_bg_bwd(proj, ad, dbg, dproj):
    n = proj.shape[0]
    nch = n // CH
    cpb = CPB if nch % CPB == 0 else 1
    rows = cpb * CH

    def body(p_ref, ad_ref, d_ref, _, o_ref, ga_ref, gd_ref):
        p = p_ref[...]
        d = d_ref[...]
        lane = _lane(p.shape)
        beta = _sigmoid(p)
        xa = p + ad_ref[1:2, :]
        sp = jnp.maximum(xa, 0.0) + jnp.log(1.0 + jnp.exp(-jnp.abs(xa)))
        na = -jnp.exp(ad_ref[0:1, :])
        dg = pltpu.roll(_dot(_tri(False, rows), d, NN, P_CUM), A_LANE - HEADS, 1)
        da = dg * na * _sigmoid(xa)
        is_g = lane >= A_LANE
        o_ref[...] = jnp.where(lane < HEADS, d * beta * (1.0 - beta), jnp.where(is_g, da, 0.0)).astype(BF16)
        ga = jnp.sum(jnp.where(is_g, dg * na * sp, 0.0), axis=0, keepdims=True)
        gd = jnp.sum(jnp.where(is_g, da, 0.0), axis=0, keepdims=True)

        @pl.when(pl.program_id(0) == 0)
        def _():
            ga_ref[...] = jnp.zeros_like(ga_ref)
            gd_ref[...] = jnp.zeros_like(gd_ref)

        ga_ref[...] += ga
        gd_ref[...] += gd

    one = pl.BlockSpec((1, DH), lambda i: (0, 0))
    return pl.pallas_call(
        body, name="prep_bg_bwd", grid=(nch // cpb,),
        in_specs=[pl.BlockSpec((rows, DH), lambda i: (i, BAB)), pl.BlockSpec((2, DH), lambda i: (0, 0)),
                  pl.BlockSpec((rows, DH), lambda i: (i, 0)), ANY],
        out_specs=[pl.BlockSpec((rows, DH), lambda i: (i, BAB)), one, one],
        out_shape=[jax.ShapeDtypeStruct(dproj.shape, BF16), jax.ShapeDtypeStruct((1, DH), F32),
                   jax.ShapeDtypeStruct((1, DH), F32)],
        input_output_aliases={3: 0},
        compiler_params=_params(("arbitrary",)),
    )(proj, ad, dbg, dproj)


def _gdn_out(o, proj, wg):
    n = o.shape[0]

    def body(o_ref, z_ref, w_ref, y_ref):
        ov, z = o_ref[...], z_ref[...]
        r = lax.rsqrt(jnp.mean(ov * ov, axis=-1, keepdims=True) + EPS)
        y_ref[...] = (ov * r * w_ref[...] * (z * _sigmoid(z))).astype(BF16)

    return pl.pallas_call(
        body, name="gdn_out", grid=(HEADS,),
        in_specs=[pl.BlockSpec((n, DH), lambda h: (0, h)), pl.BlockSpec((n, DH), lambda h: (0, ZB + h)),
                  pl.BlockSpec((1, DH), lambda h: (0, 0))],
        out_specs=pl.BlockSpec((n, DH), lambda h: (0, h)),
        out_shape=jax.ShapeDtypeStruct((n, 2 * GW), BF16),
        compiler_params=_params(("parallel",)),
    )(o, proj, wg)


def _gdn_out_bwd(o, proj, wg, dout_b, w_out):
    n = o.shape[0]
    d_model = dout_b.shape[1]

    def body(o_ref, z_ref, w_ref, g_ref, wo_ref, do_ref, dz_ref, gw_ref):
        ov, z, w = o_ref[...], z_ref[...], w_ref[...]
        d = _dot(g_ref[...], wo_ref[...], NT)
        r = lax.rsqrt(jnp.mean(ov * ov, axis=-1, keepdims=True) + EPS)
        nrm = ov * r
        s = _sigmoid(z)
        dz_ref[...] = (d * (nrm * w) * _dsilu(z, s)).astype(BF16)
        dn_w = d * (z * s)
        gw = jnp.sum(dn_w * nrm, axis=0, keepdims=True)
        dn = dn_w * w
        do_ref[...] = (r * (dn - nrm * jnp.mean(dn * nrm, axis=-1, keepdims=True))).astype(BF16)

        @pl.when(pl.program_id(0) == 0)
        def _():
            gw_ref[...] = jnp.zeros_like(gw_ref)

        gw_ref[...] += gw

    return pl.pallas_call(
        body, name="gdn_out_bwd", grid=(HEADS,),
        in_specs=[pl.BlockSpec((n, DH), lambda h: (0, h)), pl.BlockSpec((n, DH), lambda h: (0, ZB + h)),
                  pl.BlockSpec((1, DH), lambda h: (0, 0)), pl.BlockSpec((n, d_model), lambda h: (0, 0)),
                  pl.BlockSpec((DH, d_model), lambda h: (h, 0))],
        out_specs=[pl.BlockSpec((n, DH), lambda h: (0, h)), pl.BlockSpec((n, DH), lambda h: (0, ZB + h)),
                   pl.BlockSpec((1, DH), lambda h: (0, 0))],
        out_shape=[jax.ShapeDtypeStruct((n, GW), BF16), jax.ShapeDtypeStruct((n, GW_COLS), BF16),
                   jax.ShapeDtypeStruct((1, DH), F32)],
        compiler_params=_params(("arbitrary",), 40 * 2**20),
    )(o, proj, wg, dout_b, w_out)


def _conv_branch(proj, w3, b, mix):
    n = proj.shape[0]

    def body(p4, w_ref, b_ref, _, y_ref):
        u = p4[:, DH:2 * DH] * p4[:, 2 * DH:3 * DH]
        cc = _conv_silu(u, w_ref, 3) + b_ref[...]
        z = p4[:, 3 * DH:4 * DH]
        y_ref[...] = (p4[:, 0:DH] * cc * (z * _sigmoid(z))).astype(BF16)

    return pl.pallas_call(
        body, name="conv_branch", grid=(HEADS,),
        in_specs=[pl.BlockSpec((n, 4 * DH), lambda h: (0, h)), pl.BlockSpec((3, DH), lambda h: (0, h)),
                  pl.BlockSpec((1, DH), lambda h: (0, h)), ANY],
        out_specs=pl.BlockSpec((n, DH), lambda h: (0, HEADS + h)),
        out_shape=jax.ShapeDtypeStruct(mix.shape, BF16),
        input_output_aliases={3: 0},
        compiler_params=_params(("parallel",), 40 * 2**20),
    )(*_in_hbm(proj, w3, b, mix))


def _conv_branch_bwd(proj, w3, b, dout_b, w_out):
    n = proj.shape[0]
    d_model = dout_b.shape[1]

    def body(p4, w_ref, b_ref, g_ref, wo_ref, o4, gw_ref, gbias_ref):
        gb, gcv, hc, z = p4[:, 0:DH], p4[:, DH:2 * DH], p4[:, 2 * DH:3 * DH], p4[:, 3 * DH:4 * DH]
        d = _dot(g_ref[...], wo_ref[...], NT)
        dgb, dgc, dhc, dzc = (o4.at[:, kk * DH:(kk + 1) * DH] for kk in range(4))
        u = gcv * hc
        cc = _conv_silu(u, w_ref, 3) + b_ref[...]
        s = _sigmoid(z)
        dzc[...] = (d * (gb * cc) * _dsilu(z, s)).astype(BF16)
        dp = d * (z * s)
        dgb[...] = (dp * cc).astype(BF16)
        dcc = dp * gb
        gbias_ref[...] = jnp.sum(dcc, axis=0, keepdims=True)
        du = None
        for j in range(3):
            gw_ref[j:j + 1, :] = jnp.sum(dcc * _shift_down(u, 2 - j), axis=0, keepdims=True)
            t = _shift_up(dcc, 2 - j) * w_ref[j:j + 1, :]
            du = t if du is None else du + t
        dgc[...] = (du * hc).astype(BF16)
        dhc[...] = (du * gcv).astype(BF16)

    p4spec = pl.BlockSpec((n, 4 * DH), lambda h: (0, h))
    return pl.pallas_call(
        body, name="conv_branch_bwd", grid=(HEADS,),
        in_specs=[p4spec, pl.BlockSpec((3, DH), lambda h: (0, h)), pl.BlockSpec((1, DH), lambda h: (0, h)),
                  pl.BlockSpec((n, d_model), lambda h: (0, 0)), pl.BlockSpec((DH, d_model), lambda h: (HEADS + h, 0))],
        out_specs=[p4spec, pl.BlockSpec((3, DH), lambda h: (0, h)), pl.BlockSpec((1, DH), lambda h: (0, h))],
        out_shape=[jax.ShapeDtypeStruct((n, CW_COLS), BF16), jax.ShapeDtypeStruct((3, GW), F32),
                   jax.ShapeDtypeStruct((1, GW), F32)],
        compiler_params=_params(("parallel",), 52 * 2**20),
    )(proj, w3, b, dout_b, w_out)


def _out_loss(mix, w_out, x, tgt, wf):
    n, d = x.shape
    kdim = mix.shape[1]
    tr = min(256, n)

    def body(m_ref, wo_ref, x_ref, t_ref, w_ref, do_ref, dob_ref, gw_ref, loss_ref):
        ov = _dot(m_ref[...], wo_ref[...], NN) + x_ref[...]
        w = w_ref[...]
        r = lax.rsqrt(jnp.mean(ov * ov, axis=-1, keepdims=True) + EPS)
        nrm = ov * r
        e = nrm * w - t_ref[...]
        dy = e * (1.0 / d)
        dn = dy * w
        dout = r * (dn - nrm * jnp.mean(dn * nrm, axis=-1, keepdims=True))
        do_ref[...] = dout
        dob_ref[...] = dout.astype(BF16)

        @pl.when(pl.program_id(0) == 0)
        def _():
            gw_ref[...] = jnp.zeros_like(gw_ref)
            loss_ref[...] = jnp.zeros_like(loss_ref)

        gw_ref[...] += jnp.sum(dy * nrm, axis=0, keepdims=True)
        loss_ref[...] += (0.5 / d) * jnp.sum(jnp.sum(e * e, axis=-1, keepdims=True), axis=0, keepdims=True)

    row = pl.BlockSpec((tr, d), lambda i: (i, 0))
    return pl.pallas_call(
        body, name="out_loss", grid=(n // tr,),
        in_specs=[pl.BlockSpec((tr, kdim), lambda i: (i, 0)), pl.BlockSpec((kdim, d), lambda i: (0, 0)), row, row,
                  pl.BlockSpec((1, d), lambda i: (0, 0))],
        out_specs=[row, row, pl.BlockSpec((1, d), lambda i: (0, 0)), pl.BlockSpec((1, 1), lambda i: (0, 0))],
        out_shape=[jax.ShapeDtypeStruct((n, d), F32), jax.ShapeDtypeStruct((n, d), BF16),
                   jax.ShapeDtypeStruct((1, d), F32), jax.ShapeDtypeStruct((1, 1), F32)],
        compiler_params=_params(("arbitrary",), 40 * 2**20),
    )(mix, w_out, x, tgt, wf)


def _dh_rms_bwd(dproj, w_t, dh0, x, w, dout, tk):
    n, d = x.shape
    kdim = dproj.shape[1]
    tm = min(512, n)
    tk = min(tk, kdim)
    nk = kdim // tk

    def body(a_ref, b_ref, dh0_ref, x_ref, w_ref, do_ref, dx_ref, gw_ref, acc):
        i, kk = pl.program_id(0), pl.program_id(1)
        part = _dot(a_ref[...], b_ref[...], NN)

        @pl.when(kk == 0)
        def _():
            acc[...] = part + dh0_ref[...]

        @pl.when(kk > 0)
        def _():
            acc[...] += part

        @pl.when((i == 0) & (kk == 0))
        def _():
            gw_ref[...] = jnp.zeros_like(gw_ref)

        @pl.when(kk == nk - 1)
        def _():
            xv, dhv = x_ref[...], acc[...]
            r = lax.rsqrt(jnp.mean(xv * xv, axis=-1, keepdims=True) + EPS)
            xn = xv * r
            dxn = dhv * w_ref[...]
            dx_ref[...] = r * (dxn - xn * jnp.mean(dxn * xn, axis=-1, keepdims=True)) + do_ref[...]
            gw_ref[...] += jnp.sum(dhv * xn, axis=0, keepdims=True)

    row = pl.BlockSpec((tm, d), lambda i, kk: (i, 0))
    one = pl.BlockSpec((1, d), lambda i, kk: (0, 0))
    return pl.pallas_call(
        body, name="dh_rms_bwd", grid=(n // tm, nk),
        in_specs=[pl.BlockSpec((tm, tk), lambda i, kk: (i, kk)), pl.BlockSpec((tk, d), lambda i, kk: (kk, 0)),
                  row, row, one, row],
        out_specs=[row, one],
        out_shape=[jax.ShapeDtypeStruct((n, d), F32), jax.ShapeDtypeStruct((1, d), F32)],
        scratch_shapes=[pltpu.VMEM((tm, d), F32)],
        compiler_params=_params(("arbitrary", "arbitrary"), 48 * 2**20),
    )(dproj, w_t, dh0, x, w, dout)


def _ij():
    i = lax.broadcasted_iota(jnp.int32, (CH, CH), 0)
    j = lax.broadcasted_iota(jnp.int32, (CH, CH), 1)
    return i, j


def _unit_lower_inverse(mats):
    i, j = _ij()
    eye = jnp.where(i == j, 1.0, 0.0)
    same16 = (i // 16) == (j // 16)
    same32 = (i // 32) == (j // 32)
    mm = lambda xs, ys: [_dot(x, y, NN, P_INV) for x, y in zip(xs, ys)]
    n1 = [jnp.where(same16, -a, 0.0) for a in mats]
    n2 = mm(n1, n1)
    n4 = mm(n2, n2)
    n8 = mm(n4, n4)
    t = [eye + x1 + x2 + x3 for x1, x2, x3 in zip(n1, n2, mm(n1, n2))]
    t = [x + y for x, y in zip(t, mm(t, n4))]
    t = [x + y for x, y in zip(t, mm(t, n8))]
    a1 = [jnp.where(same32 & jnp.logical_not(same16), a, 0.0) for a in mats]
    t = [x - y for x, y in zip(t, mm(t, mm(a1, t)))]
    a2 = [jnp.where(same32, 0.0, a) for a in mats]
    t = [x - y for x, y in zip(t, mm(t, mm(a2, t)))]
    return t


def _head_vectors(bg, bgt, h):
    bcol = bg[:, h:h + 1]
    gcol = bg[:, HEADS + h:HEADS + h + 1]
    grow = bgt[HEADS + h:HEADS + h + 1, :]
    return bcol, gcol, grow


def _decay(gcol, grow):
    i, j = _ij()
    return jnp.where(i >= j, jnp.exp(jnp.where(i >= j, gcol - grow, 0.0)), 0.0)


def _gdn_intra(q, k, v, bg, bgt):
    n = q.shape[0]
    nch = n // CH
    cps = 4 if nch % 4 == 0 else 1

    def body(q_ref, k_ref, v_ref, bg_ref, bgt_ref, u_ref, w_ref, p_ref, t_ref):
        i, j = _ij()
        items = [(ci, h) for ci in range(cps) for h in range(HEADS)]
        at = lambda ref, ci, h: ref.at[ci * CH:(ci + 1) * CH, h * DH:(h + 1) * DH]
        bgs = [bg_ref[ci * CH:(ci + 1) * CH, :] for ci in range(cps)]
        ks = [at(k_ref, ci, h)[...] for ci, h in items]
        vecs = [_head_vectors(bgs[ci], bgt_ref[ci], h) for ci, h in items]
        decs = [_decay(gcol, grow) for _, gcol, grow in vecs]
        kks = [_dot(kh, kh, NT, P_GRAM) for kh in ks]
        qks = [_dot(at(q_ref, ci, h)[...], kh, NT, P_GRAM) for (ci, h), kh in zip(items, ks)]
        ts = _unit_lower_inverse([jnp.where(i > j, bcol * kk * dec, 0.0)
                                  for (bcol, _, _), kk, dec in zip(vecs, kks, decs)])
        us = [_dot(t, at(v_ref, ci, h)[...] * bcol, NN, P_SOL) for t, (ci, h), (bcol, _, _) in zip(ts, items, vecs)]
        ws = [_dot(t, kh * (bcol * jnp.exp(gcol)), NN, P_SOL) for t, kh, (bcol, gcol, _) in zip(ts, ks, vecs)]
        for n_, (ci, h) in enumerate(items):
            p_ref[ci, h] = qks[n_] * decs[n_]
            t_ref[ci, h] = ts[n_].astype(BF16)
            at(u_ref, ci, h)[...] = us[n_]
            at(w_ref, ci, h)[...] = ws[n_].astype(BF16)

    row = pl.BlockSpec((cps * CH, GW), lambda c: (c, 0))
    sq = pl.BlockSpec((cps, HEADS, CH, CH), lambda c: (c, 0, 0, 0))
    big = jax.ShapeDtypeStruct((n, GW), F32)
    sqs = jax.ShapeDtypeStruct((nch, HEADS, CH, CH), F32)
    return pl.pallas_call(
        body, name="gdn_intra", grid=(nch // cps,),
        in_specs=[row, row, row, pl.BlockSpec((cps * CH, DH), lambda c: (c, 0)),
                  pl.BlockSpec((cps, DH, CH), lambda c: (c, 0, 0))],
        out_specs=[row, row, sq, sq],
        out_shape=[big, jax.ShapeDtypeStruct((n, GW), BF16), sqs, jax.ShapeDtypeStruct(sqs.shape, BF16)],
        compiler_params=_params(("parallel",)),
    )(q, k, v, bg, bgt)


def _gdn_scan(q, k, bg, u, w, p):
    n = q.shape[0]
    nch = n // CH
    cps = SCAN_CPS if nch % SCAN_CPS == 0 else 1

    def body(q_ref, k_ref, bg_ref, u_ref, w_ref, p_ref, o_ref, vn_ref, s_out, s_scr):
        @pl.when(pl.program_id(0) == 0)
        def _():
            s_scr[...] = jnp.zeros_like(s_scr)

        hs = range(HEADS)
        sls = [slice(h * DH, (h + 1) * DH) for h in hs]
        ss = [s_scr[h] for h in hs]
        for ci in range(cps):
            rs = slice(ci * CH, (ci + 1) * CH)
            bg = bg_ref[rs, :]
            gcols = [bg[:, HEADS + h:HEADS + h + 1] for h in hs]
            glasts = [g[CH - 1:CH, :] for g in gcols]
            wss = [_dot(w_ref[rs, sl], s, NN, P_SCAN) for sl, s in zip(sls, ss)]
            oqs = [_dot(q_ref[rs, sl] * jnp.exp(g), s, NN, P_SCAN) for sl, s, g in zip(sls, ss, gcols)]
            vns = [u_ref[rs, sl] - x for sl, x in zip(sls, wss)]
            ops = [_dot(p_ref[ci, h], vn, NN, P_SCAN) for h, vn in zip(hs, vns)]
            sns = [_dot(k_ref[rs, sl] * jnp.exp(gl - g), vn, TN, P_SCAN)
                   for sl, gl, g, vn in zip(sls, glasts, gcols, vns)]
            for h, sl in enumerate(sls):
                s_out[ci, :, sl] = ss[h].astype(BF16)
                vn_ref[rs, sl] = vns[h].astype(BF16)
                o_ref[rs, sl] = oqs[h] + ops[h]
            ss = [s * jnp.exp(gl) + sn for s, gl, sn in zip(ss, glasts, sns)]
        for h in hs:
            s_scr[h] = ss[h]

    row = pl.BlockSpec((cps * CH, GW), lambda c: (c, 0))
    big = jax.ShapeDtypeStruct((n, GW), F32)
    return pl.pallas_call(
        body, name="gdn_scan", grid=(nch // cps,),
        in_specs=[row, row, pl.BlockSpec((cps * CH, DH), lambda c: (c, 0)), row, row,
                  pl.BlockSpec((cps, HEADS, CH, CH), lambda c: (c, 0, 0, 0))],
        out_specs=[row, row, pl.BlockSpec((cps, DH, GW), lambda c: (c, 0, 0))],
        out_shape=[big, jax.ShapeDtypeStruct((n, GW), BF16), jax.ShapeDtypeStruct((nch, DH, GW), BF16)],
        scratch_shapes=[pltpu.VMEM((HEADS, DH, DH), F32)],
        compiler_params=_params(("arbitrary",)),
    )(q, k, bg, u, w, p)


def _gdn_scan_bwd(q, k, bg, w, p, vn, s_in, do):
    n = q.shape[0]
    nch = n // CH
    cps = SCAN_CPS if nch % SCAN_CPS == 0 else 1
    rev = lambda c: nch // cps - 1 - c

    def body(q_ref, k_ref, bg_ref, w_ref, p_ref, vn_ref, s_ref, do_ref,
             dqg_ref, dp_ref, du_ref, dw_ref, dks_ref, dgam_ref, ds_scr):
        @pl.when(pl.program_id(0) == 0)
        def _():
            ds_scr[...] = jnp.zeros_like(ds_scr)

        lane = _lane((1, DH))
        hs = range(HEADS)
        sls = [slice(h * DH, (h + 1) * DH) for h in hs]
        dss = [ds_scr[h] for h in hs]
        for ci in reversed(range(cps)):
            rs = slice(ci * CH, (ci + 1) * CH)
            bg = bg_ref[rs, :]
            gcols = [bg[:, HEADS + h:HEADS + h + 1] for h in hs]
            glasts = [g[CH - 1:CH, :] for g in gcols]
            ss = [s_ref[ci, :, sl] for sl in sls]
            dos = [do_ref[rs, sl] for sl in sls]
            vnl = [vn_ref[rs, sl] for sl in sls]
            dqgs = [_dot(d, s, NT, P_SCANB) for d, s in zip(dos, ss)]
            dps = [_dot(d, vn, NT, P_SCANB) for d, vn in zip(dos, vnl)]
            dvn1 = [_dot(p_ref[ci, h], d, TN, P_SCANB) for h, d in zip(hs, dos)]
            dvn2 = [_dot(k_ref[rs, sl] * jnp.exp(gl - g), ds, NN, P_SCANB)
                    for sl, gl, g, ds in zip(sls, glasts, gcols, dss)]
            dkss = [_dot(vn, ds, NT, P_SCANB) for vn, ds in zip(vnl, dss)]
            dsq = [_dot(q_ref[rs, sl] * jnp.exp(g), d, TN, P_SCANB) for sl, g, d in zip(sls, gcols, dos)]
            dvns = [a + b for a, b in zip(dvn1, dvn2)]
            dws = [_dot(dvn, s, NT, P_SCANB) for dvn, s in zip(dvns, ss)]
            dsw = [_dot(w_ref[rs, sl], dvn, TN, P_SCANB) for sl, dvn in zip(sls, dvns)]
            dgam = jnp.zeros((1, DH), F32)
            for h, sl in enumerate(sls):
                dqg_ref[rs, sl] = dqgs[h]
                dp_ref[ci, h] = dps[h]
                du_ref[rs, sl] = dvns[h].astype(BF16)
                dw_ref[rs, sl] = (-dws[h]).astype(BF16)
                dks_ref[rs, sl] = dkss[h]
                tot = jnp.sum(jnp.sum(dss[h] * ss[h], axis=-1, keepdims=True), axis=0, keepdims=True)
                dgam = dgam + jnp.where(lane == h, tot, 0.0)
            dgam_ref[ci] = jnp.broadcast_to(dgam, (8, DH))
            dss = [ds * jnp.exp(gl) + a - b for ds, gl, a, b in zip(dss, glasts, dsq, dsw)]
        for h in hs:
            ds_scr[h] = dss[h]

    row = pl.BlockSpec((cps * CH, GW), lambda c: (rev(c), 0))
    sq = pl.BlockSpec((cps, HEADS, CH, CH), lambda c: (rev(c), 0, 0, 0))
    big = jax.ShapeDtypeStruct((n, GW), F32)
    return pl.pallas_call(
        body, name="gdn_scan_bwd", grid=(nch // cps,),
        in_specs=[row, row, pl.BlockSpec((cps * CH, DH), lambda c: (rev(c), 0)), row, sq, row,
                  pl.BlockSpec((cps, DH, GW), lambda c: (rev(c), 0, 0)), row],
        out_specs=[row, sq, row, row, row, pl.BlockSpec((cps, 8, DH), lambda c: (rev(c), 0, 0))],
        out_shape=[big, jax.ShapeDtypeStruct((nch, HEADS, CH, CH), F32), jax.ShapeDtypeStruct((n, GW), BF16),
                   jax.ShapeDtypeStruct((n, GW), BF16), big,
                   jax.ShapeDtypeStruct((nch, 8, DH), F32)],
        scratch_shapes=[pltpu.VMEM((HEADS, DH, DH), F32)],
        compiler_params=_params(("arbitrary",)),
    )(q, k, bg, w, p, vn, s_in, do)


def _gdn_intra_bwd(q, k, v, bg, bgt, t, u, w, p, dqg, dp, du, dw, dks, dgam):
    n = q.shape[0]
    nch = n // CH
    cps = 2 if nch % 2 == 0 else 1

    def body(q_ref, k_ref, v_ref, bg_ref, bgt_ref, t_ref, u_ref, w_ref, p_ref,
             dqg_ref, dp_ref, du_ref, dw_ref, dks_ref, dgam_ref, dq_ref, dk_ref, dv_ref, dbg_ref):
        i, j = _ij()
        rows1 = lax.broadcasted_iota(jnp.int32, (CH, 1), 0)
        lane = _lane((CH, DH))
        rsum = lambda x: jnp.sum(x, axis=-1, keepdims=True)
        items = [(ci, h) for ci in range(cps) for h in range(HEADS)]
        at = lambda ref, it: ref.at[it[0] * CH:(it[0] + 1) * CH, it[1] * DH:(it[1] + 1) * DH]
        ld = lambda ref: [at(ref, it)[...] for it in items]
        bgs = [bg_ref[ci * CH:(ci + 1) * CH, :] for ci in range(cps)]
        qs, ks = ld(q_ref), ld(k_ref)
        vecs = [_head_vectors(bgs[ci], bgt_ref[ci], h) for ci, h in items]
        decs = [_decay(gcol, grow) for _, gcol, grow in vecs]
        ths = [t_ref[ci, h] for ci, h in items]
        drus = [_dot(th, x_, TN, P_BWD) for th, x_ in zip(ths, ld(du_ref))]
        drws = [_dot(th, x_, TN, P_BWD) for th, x_ in zip(ths, ld(dw_ref))]
        kks = [_dot(kh, kh, NT, P_GRAM) for kh in ks]
        da1 = [_dot(dru, x_, NT, P_BWD) for dru, x_ in zip(drus, ld(u_ref))]
        da2 = [_dot(drw, x_, NT, P_BWD) for drw, x_ in zip(drws, ld(w_ref))]
        das = [jnp.where(i > j, -(x_ + y_), 0.0) for x_, y_ in zip(da1, da2)]
        dkks = [da * bcol * dec for da, (bcol, _, _), dec in zip(das, vecs, decs)]
        dps = [dp_ref[ci, h] for ci, h in items]
        dqks = [dp_ * dec for dp_, dec in zip(dps, decs)]
        dq_ps = [_dot(dqk, kh, NN, P_BWD) for dqk, kh in zip(dqks, ks)]
        dk_ps = [_dot(dqk, qh, TN, P_BWD) for dqk, qh in zip(dqks, qs)]
        dk_as = [_dot(dkk, kh, NN, P_BWD) for dkk, kh in zip(dkks, ks)]
        dk_bs = [_dot(dkk, kh, TN, P_BWD) for dkk, kh in zip(dkks, ks)]
        bcols = [vc[0] for vc in vecs]
        gcols = [vc[1] for vc in vecs]
        gams = [jnp.exp(g) for g in gcols]
        glasts = [g[CH - 1:CH, :] for g in gcols]
        es = [jnp.exp(gl - g) for gl, g in zip(glasts, gcols)]
        kgs = [kh * gam for kh, gam in zip(ks, gams)]
        dqgs, dkss = ld(dqg_ref), ld(dks_ref)
        wks = [drw * kg for drw, kg in zip(drws, kgs)]
        kss = [dk_ * (kh * e) for dk_, kh, e in zip(dkss, ks, es)]
        r_beta = [rsum(dru * x_ + wk) for dru, x_, wk in zip(drus, ld(v_ref), wks)]
        r_ak = [rsum(da * kk * dec) for da, kk, dec in zip(das, kks, decs)]
        r_gc = [rsum(wk * bcol + dqg * (qh * gam) - ks_)
                for wk, bcol, dqg, qh, gam, ks_ in zip(wks, bcols, dqgs, qs, gams, kss)]
        tk_tot = [jnp.sum(jnp.sum(ks_, axis=0, keepdims=True), axis=-1, keepdims=True) for ks_ in kss]
        mdecs = [da * (bcol * kk * dec) + dp_ * p_ref[ci, h]
                 for (ci, h), da, bcol, kk, dec, dp_ in zip(items, das, bcols, kks, decs, dps)]
        r_md = [rsum(m) for m in mdecs]
        c_md = [rsum(jnp.where(i == j, jnp.sum(m, axis=0, keepdims=True), 0.0)) for m in mdecs]
        dbgs = [jnp.zeros((CH, DH), F32) for _ in range(cps)]
        for n_, (ci, h) in enumerate(items):
            at(dv_ref, (ci, h))[...] = bcols[n_] * drus[n_]
            at(dq_ref, (ci, h))[...] = gams[n_] * dqgs[n_] + dq_ps[n_]
            at(dk_ref, (ci, h))[...] = ((bcols[n_] * gams[n_]) * drws[n_] + dk_ps[n_] + dk_as[n_] + dk_bs[n_]
                                        + dkss[n_] * es[n_])
            dbeta = r_beta[n_] + r_ak[n_]
            dglast = tk_tot[n_] + dgam_ref[ci, 0:1, h:h + 1] * jnp.exp(glasts[n_])
            dgc = r_gc[n_] + r_md[n_] - c_md[n_] + jnp.where(rows1 == CH - 1, dglast, 0.0)
            dbgs[ci] = dbgs[ci] + jnp.where(lane == h, dbeta, 0.0) + jnp.where(lane == HEADS + h, dgc, 0.0)
        for ci in range(cps):
            dbg_ref[ci * CH:(ci + 1) * CH, :] = dbgs[ci]

    row = pl.BlockSpec((cps * CH, GW), lambda c: (c, 0))
    sq = pl.BlockSpec((cps, HEADS, CH, CH), lambda c: (c, 0, 0, 0))
    small = pl.BlockSpec((cps * CH, DH), lambda c: (c, 0))
    big = jax.ShapeDtypeStruct((n, GW), F32)
    return pl.pallas_call(
        body, name="gdn_intra_bwd", grid=(nch // cps,),
        in_specs=[row, row, row, small, pl.BlockSpec((cps, DH, CH), lambda c: (c, 0, 0)), sq, row, row, sq,
                  row, sq, row, row, row, pl.BlockSpec((cps, 8, DH), lambda c: (c, 0, 0))],
        out_specs=[row, row, row, small],
        out_shape=[big, big, big, jax.ShapeDtypeStruct((n, DH), F32)],
        compiler_params=_params(("parallel",)),
    )(q, k, v, bg, bgt, t, u, w, p, dqg, dp, du, dw, dks, dgam)


def _local_step(x, tgt, h, w_g, cqw, late, norm_in_w, ad, gdn_norm_w, conv_b, final_norm_w,
                on_grad_c=None, on_grad_g=None, on_q=None):
    proj_g = _matmul(h, w_g, NT, F32, 512, 1408, 1024, "mm_proj_g", n=GW_COLS, b_outer=True)
    q, k, v = _prep_qkv(proj_g, cqw)
    if on_q is not None:
        q = on_q(q)
    bg, bgt = _prep_bg(proj_g, ad)
    u, w, p, t = _gdn_intra(q, k, v, bg, bgt)
    o, vn, s_in = _gdn_scan(q, k, bg, u, w, p)
    w_c, w_out, conv_w = late(o)
    proj_c = _matmul(h, w_c, NT, F32, 512, 1024, 1024, "mm_proj_c", n=CW_COLS, b_outer=True)
    mix = _conv_branch(proj_c, conv_w, conv_b, _gdn_out(o, proj_g, gdn_norm_w))
    dout, dout_b, g_fn, loss = _out_loss(mix, w_out, x, tgt, final_norm_w)

    g_wout = _matmul(mix, dout_b, TN, BF16, 512, 512, 2048, "mm_gwout")
    do, dproj_g, g_gn = _gdn_out_bwd(o, proj_g, gdn_norm_w, dout_b, w_out)
    dproj_c, g_cw, g_cb = _conv_branch_bwd(proj_c, conv_w, conv_b, dout_b, w_out)
    g_c = _matmul(dproj_c, h, TN, BF16, 1024, 512, 2048, "mm_gwin_c")
    if on_grad_c is not None:
        do = on_grad_c(g_c, g_wout, do)
    dqg, dp, du, dw, dks, dgam = _gdn_scan_bwd(q, k, bg, w, p, vn, s_in, do)
    dq, dk, dv, dbg = _gdn_intra_bwd(q, k, v, bg, bgt, t, u, w, p, dqg, dp, du, dw, dks, dgam)
    dproj_g, gq, gk, gv = _prep_qkv_bwd(proj_g, cqw, dq, dk, dv, dproj_g)
    dproj_g, g_al, g_dt = _prep_bg_bwd(proj_g, ad, dbg, dproj_g)
    g_g = _matmul(dproj_g, h, TN, BF16, 1408, 512, 2048, "mm_gwin_g")
    if on_grad_g is not None:
        dproj_g = on_grad_g(g_g, dproj_g)
    dh = _matmul(dproj_g, w_g, NN, F32, 1024, 1024, 1408, "mm_dh_g")
    gx, g_nin = _dh_rms_bwd(dproj_c, w_c, dh, x, norm_in_w, dout, 1024)
    small = dict(nin=g_nin, cb=g_cb, fn=g_fn, al=g_al, dt=g_dt, gn=g_gn, cq=(gq, gk, gv), cw=g_cw, loss=loss)
    return gx, small, (g_g, g_c, g_wout)


def _place():
    x, y, c = lax.axis_index("x"), lax.axis_index("y"), lax.axis_index("c")
    chips = [(1 - x, y), (x, 1 - y), (1 - x, 1 - y)]
    return x, y, c, chips


def _blk(ref, b):
    if isinstance(b, int):
        return ref.at[b * DH:(b + 1) * DH, :]
    return ref.at[pl.ds(pl.multiple_of(b * DH, DH), DH), :]


HBM = pl.BlockSpec(memory_space=pltpu.HBM)
SEM = pl.BlockSpec(memory_space=pltpu.SEMAPHORE)
EFFECT = pltpu.SideEffectType.DATAFLOW_SIDE_EFFECTING


def _split_start(name, issue, bufs, n_sems):
    nbuf = len(bufs)

    def body(*refs):
        issue(refs[:nbuf], refs[nbuf], refs[nbuf + 1])
        refs[-1][...] = jnp.zeros_like(refs[-1])

    out = pl.pallas_call(
        body, name=name,
        out_shape=(pltpu.SemaphoreType.DMA((n_sems,)), pltpu.SemaphoreType.DMA((n_sems,)),
                   *[pltpu.HBM(b.shape, b.dtype) for b in bufs], jax.ShapeDtypeStruct((8, DH), F32)),
        in_specs=[HBM] * nbuf,
        out_specs=(SEM, SEM, *[HBM] * nbuf, pl.BlockSpec(memory_space=pltpu.VMEM)),
        input_output_aliases={a: 2 + a for a in range(nbuf)},
        compiler_params=pltpu.CompilerParams(has_side_effects=EFFECT),
    )(*[pltpu.with_memory_space_constraint(b, pltpu.HBM) for b in bufs])
    return out[0], out[1], list(out[2:2 + nbuf]), out[-1]


def _split_wait(name, await_, send_sems, recv_sems, bufs, after):
    nbuf = len(bufs)
    after = list(after) if isinstance(after, (list, tuple)) else [after]

    def body(*refs):
        await_(refs[:nbuf], refs[nbuf], refs[nbuf + 1])

    out = pl.pallas_call(
        body, name=name,
        out_shape=tuple(pltpu.HBM(b.shape, b.dtype) for b in bufs),
        in_specs=[HBM] * nbuf + [SEM, SEM] + [ANY] * len(after), out_specs=tuple([HBM] * nbuf),
        input_output_aliases={a: a for a in range(nbuf)},
        compiler_params=pltpu.CompilerParams(has_side_effects=EFFECT),
    )(*bufs, send_sems, recv_sems, *after)
    return list(out)


def _phase_blocks(chip, phase, edges, parity=None):
    return [(b, blk) for b, (grp, blk) in enumerate(_shard_blocks(chip, edges))
            if grp == phase and (parity is None or b % 2 == parity)]


def _cols(ref, nblk):
    return ref.at[0:nblk * DH, :]


def _block_table(chip, edges, spare_g, spare_c):
    rows = []
    for s in range(4):
        sb = _shard_blocks(s, edges)
        rows.append([[blk if grp == "g" else spare_g for grp, blk in sb],
                     [blk if grp == "c" else spare_c for grp, blk in sb],
                     [int(grp == "g") for grp, _ in sb], [s] * ALIGNED_BLOCKS])
    return jnp.asarray(rows, jnp.int32)[chip]


def _place_own(a_shard, wo, cq, cw, bufs):
    d = a_shard.shape[1]
    chip = 2 * lax.axis_index("x") + lax.axis_index("y")

    def body(t_ref, a_ref, wo_ref, cq_ref, cw_ref, *refs):
        wg_ref, wc_ref, wog_ref, cqg_ref, cwg_ref = refs[5:]
        wg_ref[...] = a_ref[...]
        wc_ref[...] = a_ref[...]

        @pl.when(pl.program_id(0) == 0)
        def _():
            wog_ref[0] = wo_ref[...]
            cqg_ref[0] = cq_ref[...]
            cwg_ref[0] = cw_ref[...]

    whole = lambda s: pl.BlockSpec(s.shape, lambda b, t: (0,) * s.ndim)
    slot = lambda s: pl.BlockSpec((1,) + s.shape, lambda b, t: (t[3, 0],) + (0,) * s.ndim)
    return pl.pallas_call(
        body, name="place_own",
        grid_spec=pltpu.PrefetchScalarGridSpec(
            num_scalar_prefetch=1, grid=(ALIGNED_BLOCKS,),
            in_specs=[pl.BlockSpec((DH, d), lambda b, t: (b, 0)), whole(wo), whole(cq), whole(cw)] + [ANY] * 5,
            out_specs=[pl.BlockSpec((DH, d), lambda b, t: (t[0, b], 0)),
                       pl.BlockSpec((DH, d), lambda b, t: (t[1, b], 0)), slot(wo), slot(cq), slot(cw)]),
        out_shape=[jax.ShapeDtypeStruct(b.shape, b.dtype) for b in bufs],
        input_output_aliases={5 + a: a for a in range(5)},
        compiler_params=_params(("arbitrary",)),
    )(_block_table(chip, True, G_SPARE, C_SPARE), a_shard, wo, cq, cw, *bufs)


def _tie(x, token, name):
    def body(x_ref, t_ref, o_ref):
        del x_ref, t_ref, o_ref

    return pl.pallas_call(
        body, name=name, in_specs=[ANY, ANY], out_specs=ANY,
        out_shape=jax.ShapeDtypeStruct(x.shape, x.dtype), input_output_aliases={0: 0},
    )(x, token)


def _gather_start(phase, a_shard, w_grp, singles):
    ns = len(singles)

    def issue(refs, send_sems, recv_sems):
        a_ref, w_ref = refs[0], refs[1]
        x, y, c, chips = _place()
        mine = 2 * x + y
        for jj, (px, py) in enumerate(chips):
            to = dict(device_id=(px, py, c), device_id_type=MESH)
            for a in range(ns):
                pltpu.make_async_remote_copy(
                    src_ref=refs[2 + 2 * a], dst_ref=refs[3 + 2 * a].at[mine],
                    send_sem=send_sems.at[(1 + ns) * jj + 1 + a], recv_sem=recv_sems.at[(1 + ns) * jj + 1 + a],
                    **to).start()
        for s in range(4):
            for par in range(2):
                blocks = _phase_blocks(s, phase, True, par)
                if blocks:
                    @pl.when((mine == s) & (c == par))
                    def _():
                        for jj, (px, py) in enumerate(chips):
                            for b, blk in blocks:
                                pltpu.make_async_remote_copy(
                                    src_ref=_blk(a_ref, b), dst_ref=_blk(w_ref, blk),
                                    send_sem=send_sems.at[(1 + ns) * jj], recv_sem=recv_sems.at[(1 + ns) * jj],
                                    device_id=(px, py, c), device_id_type=MESH).start()

    bufs = [a_shard, w_grp] + [t for pair in singles for t in pair]
    return _split_start("gather_start_" + phase, issue, bufs, 3 * (1 + ns))


def _gather_wait(phase, send_sems, recv_sems, bufs, after):
    ns = (len(bufs) - 2) // 2

    def await_(refs, send_sems, recv_sems):
        a_ref, w_ref = refs[0], refs[1]
        x, y, c, chips = _place()
        mine = 2 * x + y
        for jj, (px, py) in enumerate(chips):
            to = dict(device_id=(px, py, c), device_id_type=MESH)
            peer = 2 * px + py
            for a in range(ns):
                cp = pltpu.make_async_remote_copy(
                    src_ref=refs[2 + 2 * a], dst_ref=refs[3 + 2 * a].at[mine],
                    send_sem=send_sems.at[(1 + ns) * jj + 1 + a], recv_sem=recv_sems.at[(1 + ns) * jj + 1 + a], **to)
                cp.wait_recv()
                cp.wait_send()
            for s in range(4):
                for par in range(2):
                    nblk = len(_phase_blocks(s, phase, True, par))
                    if nblk:
                        both = pltpu.make_async_remote_copy(
                            src_ref=_cols(a_ref, nblk), dst_ref=_cols(w_ref, nblk),
                            send_sem=send_sems.at[(1 + ns) * jj], recv_sem=recv_sems.at[(1 + ns) * jj], **to)

                        @pl.when((peer == s) & (c == par))
                        def _():
                            both.wait_recv()

                        @pl.when((mine == s) & (c == par))
                        def _():
                            both.wait_send()

    return _split_wait("gather_wait_" + phase, await_, send_sems, recv_sems, bufs, after)


def _sibling_forward_parts(phase):
    def each(w_ref, send_sems, recv_sems, start):
        x, y, c, chips = _place()
        to = dict(device_id=(x, y, 1 - c), device_id_type=MESH)
        for jj, (px, py) in enumerate(chips):
            peer = 2 * px + py
            for s in range(4):
                for par in range(2):
                    mine_blocks = _phase_blocks(s, phase, True, par)
                    theirs = len(_phase_blocks(s, phase, True, 1 - par))
                    if not (mine_blocks or theirs):
                        continue

                    @pl.when((peer == s) & (c == par))
                    def _():
                        if start:
                            for _, blk in mine_blocks:
                                pltpu.make_async_remote_copy(
                                    src_ref=_blk(w_ref, blk), dst_ref=_blk(w_ref, blk),
                                    send_sem=send_sems.at[jj], recv_sem=recv_sems.at[jj], **to).start()
                            return
                        if theirs:
                            pltpu.make_async_remote_copy(
                                src_ref=_cols(w_ref, theirs), dst_ref=_cols(w_ref, theirs),
                                send_sem=send_sems.at[jj], recv_sem=recv_sems.at[jj], **to).wait_recv()
                        if mine_blocks:
                            pltpu.make_async_remote_copy(
                                src_ref=_cols(w_ref, len(mine_blocks)), dst_ref=_cols(w_ref, len(mine_blocks)),
                                send_sem=send_sems.at[jj], recv_sem=recv_sems.at[jj], **to).wait_send()

    issue = lambda refs, send_sems, recv_sems: each(refs[0], send_sems, recv_sems, True)
    await_ = lambda refs, send_sems, recv_sems: each(refs[0], send_sems, recv_sems, False)
    return issue, await_


def _sibling_forward(phase, w_grp):
    issue, await_ = _sibling_forward_parts(phase)

    def body(w_in_ref, w_ref, send_sems, recv_sems):
        del w_in_ref
        issue([w_ref], send_sems, recv_sems)
        await_([w_ref], send_sems, recv_sems)

    return pl.pallas_call(
        body, name="sibling_forward_" + phase, in_specs=[ANY], out_specs=ANY,
        out_shape=jax.ShapeDtypeStruct(w_grp.shape, w_grp.dtype), input_output_aliases={0: 0},
        scratch_shapes=[pltpu.SemaphoreType.DMA((3,)), pltpu.SemaphoreType.DMA((3,))],
    )(w_grp)


def _merge_edges(w, edge0, mixed, name):
    d = w.shape[1]

    def body(e_ref, o_ref):
        o_ref[...] = e_ref[0:DH, :] + e_ref[DH:2 * DH, :]

    def to_block(i):
        r = mixed[-1]
        for kk in range(len(mixed) - 2, -1, -1):
            r = jnp.where(i == kk, mixed[kk], r)
        return r

    return pl.pallas_call(
        body, name=name, grid=(len(mixed),),
        in_specs=[pl.BlockSpec((2 * DH, d), lambda i: (edge0 // 2 + i, 0))],
        out_specs=pl.BlockSpec((DH, d), lambda i: (to_block(i), 0)),
        out_shape=jax.ShapeDtypeStruct(w.shape, w.dtype),
        input_output_aliases={0: 0},
        compiler_params=_params(("arbitrary",)),
    )(w)


def _scatter_start(phase, g_grp, land, singles, halved=False):
    ns = len(singles)

    def issue(refs, send_sems, recv_sems):
        g_ref, land_ref = refs[0], refs[1]
        x, y, c, chips = _place()
        for jj, (px, py) in enumerate(chips):
            to = dict(device_id=(px, py, c), device_id_type=MESH)
            peer = 2 * px + py
            for a in range(ns):
                pltpu.make_async_remote_copy(
                    src_ref=refs[2 + 2 * a].at[peer], dst_ref=refs[3 + 2 * a].at[jj],
                    send_sem=send_sems.at[(1 + ns) * jj + 1 + a], recv_sem=recv_sems.at[(1 + ns) * jj + 1 + a],
                    **to).start()
            for s in range(4):
                for par in ((0, 1) if halved else (None,)):
                    blocks = _phase_blocks(s, phase, False, par)
                    if blocks:
                        @pl.when((peer == s) if par is None else ((peer == s) & (c == par)))
                        def _():
                            for b, blk in blocks:
                                pltpu.make_async_remote_copy(
                                    src_ref=_blk(g_ref, blk), dst_ref=_blk(land_ref.at[jj], b),
                                    send_sem=send_sems.at[(1 + ns) * jj], recv_sem=recv_sems.at[(1 + ns) * jj],
                                    **to).start()

    bufs = [g_grp, land] + [t for pair in singles for t in pair]
    return _split_start("scatter_start_" + phase, issue, bufs, 3 * (1 + ns))


def _scatter_wait(phase, send_sems, recv_sems, bufs, after, halved=False):
    ns = (len(bufs) - 2) // 2

    def await_(refs, send_sems, recv_sems):
        g_ref, land_ref = refs[0], refs[1]
        x, y, c, chips = _place()
        mine = 2 * x + y
        for jj, (px, py) in enumerate(chips):
            to = dict(device_id=(px, py, c), device_id_type=MESH)
            peer = 2 * px + py
            for a in range(ns):
                cp = pltpu.make_async_remote_copy(
                    src_ref=refs[2 + 2 * a].at[peer], dst_ref=refs[3 + 2 * a].at[jj],
                    send_sem=send_sems.at[(1 + ns) * jj + 1 + a], recv_sem=recv_sems.at[(1 + ns) * jj + 1 + a], **to)
                cp.wait_recv()
                cp.wait_send()
            for s in range(4):
                for par in ((0, 1) if halved else (None,)):
                    nblk = len(_phase_blocks(s, phase, False, par))
                    if nblk:
                        both = pltpu.make_async_remote_copy(
                            src_ref=_cols(g_ref, nblk), dst_ref=_cols(land_ref.at[jj], nblk),
                            send_sem=send_sems.at[(1 + ns) * jj], recv_sem=recv_sems.at[(1 + ns) * jj], **to)

                        @pl.when((mine == s) if par is None else ((mine == s) & (c == par)))
                        def _():
                            both.wait_recv()

                        @pl.when((peer == s) if par is None else ((peer == s) & (c == par)))
                        def _():
                            both.wait_send()

    return _split_wait("scatter_wait_" + phase, await_, send_sems, recv_sems, bufs, after)


def _needed_blocks(phase, parity):
    return sorted({blk for s in range(4) for _, blk in _phase_blocks(s, phase, False, parity)})


def _pair_reduce(phase, g_grp):
    n, d = g_grp.shape

    def swap(g_ref, sib_ref, send_sem, recv_sem):
        x, y, c, _ = _place()
        to = dict(device_id=(x, y, 1 - c), device_id_type=MESH)
        for par in range(2):
            give, get = _needed_blocks(phase, 1 - par), _needed_blocks(phase, par)

            @pl.when(c == par)
            def _():
                for blk in give:
                    pltpu.make_async_remote_copy(src_ref=_blk(g_ref, blk), dst_ref=_blk(sib_ref, blk),
                                                 send_sem=send_sem, recv_sem=recv_sem, **to).start()
                pltpu.make_async_remote_copy(src_ref=_cols(g_ref, len(get)), dst_ref=_cols(sib_ref, len(get)),
                                             send_sem=send_sem, recv_sem=recv_sem, **to).wait_recv()
                pltpu.make_async_remote_copy(src_ref=_cols(g_ref, len(give)), dst_ref=_cols(sib_ref, len(give)),
                                             send_sem=send_sem, recv_sem=recv_sem, **to).wait_send()

    sib = pl.pallas_call(
        swap, name="pair_swap_" + phase, in_specs=[ANY], out_specs=ANY,
        out_shape=jax.ShapeDtypeStruct((n, d), g_grp.dtype),
        scratch_shapes=[pltpu.SemaphoreType.DMA, pltpu.SemaphoreType.DMA],
    )(*_in_hbm(g_grp))

    lists = [_needed_blocks(phase, par) for par in range(2)]
    longest = max(len(t) for t in lists)
    table = jnp.asarray([t + [t[-1]] * (longest - len(t)) for t in lists], jnp.int32)[lax.axis_index("c")]

    def add(t_ref, a_ref, b_ref, o_ref):
        o_ref[...] = (a_ref[...].astype(F32) + b_ref[...].astype(F32)).astype(o_ref.dtype)

    blk = pl.BlockSpec((DH, d), lambda i, t: (t[i], 0))
    return pl.pallas_call(
        add, name="pair_add_" + phase,
        grid_spec=pltpu.PrefetchScalarGridSpec(num_scalar_prefetch=1, grid=(longest,),
                                               in_specs=[blk, blk], out_specs=blk),
        out_shape=jax.ShapeDtypeStruct((n, d), g_grp.dtype),
        compiler_params=_params(("arbitrary",)),
    )(table, g_grp, sib)


def _sum_shard(g_g, g_c, land):
    d = g_g.shape[1]
    chip = 2 * lax.axis_index("x") + lax.axis_index("y")

    def body(t_ref, gg_ref, gc_ref, land_ref, o_ref):
        b = pl.program_id(0)
        in_g = t_ref[2, b] == 1
        own = jnp.where(in_g, gg_ref[...].astype(F32), gc_ref[...].astype(F32))
        for jj in range(3):
            own = own + land_ref[jj].astype(F32)
        o_ref[...] = jnp.where(in_g & (b % 2 != lax.axis_index("c")), 0.0, own)

    return pl.pallas_call(
        body, name="sum_w_in",
        grid_spec=pltpu.PrefetchScalarGridSpec(
            num_scalar_prefetch=1, grid=(ALIGNED_BLOCKS,),
            in_specs=[pl.BlockSpec((DH, d), lambda b, t: (t[0, b], 0)), pl.BlockSpec((DH, d), lambda b, t: (t[1, b], 0)),
                      pl.BlockSpec((3, DH, d), lambda b, t: (0, b, 0))],
            out_specs=pl.BlockSpec((DH, d), lambda b, t: (b, 0))),
        out_shape=jax.ShapeDtypeStruct((ALIGNED_W, d), F32),
        compiler_params=_params(("arbitrary",)),
    )(_block_table(chip, False, 0, 0), g_g, g_c, land)


def _sum_rows(stack, land, rows):
    _, r, d = stack.shape
    rows = min(rows, r)
    chip = 2 * lax.axis_index("x") + lax.axis_index("y")

    def body(t_ref, own_ref, land_ref, o_ref):
        acc = own_ref[0].astype(F32)
        for jj in range(3):
            acc = acc + land_ref[jj].astype(F32)
        o_ref[...] = acc

    return pl.pallas_call(
        body, name="sum_w_out",
        grid_spec=pltpu.PrefetchScalarGridSpec(
            num_scalar_prefetch=1, grid=(r // rows,),
            in_specs=[pl.BlockSpec((1, rows, d), lambda i, t: (t[0], i, 0)),
                      pl.BlockSpec((3, rows, d), lambda i, t: (0, i, 0))],
            out_specs=pl.BlockSpec((rows, d), lambda i, t: (i, 0))),
        out_shape=jax.ShapeDtypeStruct((r, d), F32),
        compiler_params=_params(("arbitrary",)),
    )(jnp.reshape(chip, (1,)).astype(jnp.int32), stack, land)


def _exchange_parts(n_swap, with_pack):
    def copies(refs, send_sems, recv_sems):
        x, y, c, _ = _place()
        me = 4 * x + 2 * y + c
        cps = [pltpu.make_async_remote_copy(
            src_ref=refs[2 * a], dst_ref=refs[2 * a + 1], send_sem=send_sems.at[a], recv_sem=recv_sems.at[a],
            device_id=(x, y, 1 - c), device_id_type=MESH) for a in range(n_swap)]
        if with_pack:
            pack_ref, packs = refs[2 * n_swap], refs[2 * n_swap + 1]
            for r in range(1, 8):
                dx, dy, dc = (r >> 2) & 1, (r >> 1) & 1, r & 1
                peer = (x + dx - 2 * x * dx, y + dy - 2 * y * dy, c + dc - 2 * c * dc)
                cps.append(pltpu.make_async_remote_copy(
                    src_ref=pack_ref, dst_ref=packs.at[me], send_sem=send_sems.at[n_swap + r - 1],
                    recv_sem=recv_sems.at[n_swap + r - 1], device_id=peer, device_id_type=MESH))
        return cps

    def issue(refs, send_sems, recv_sems):
        for cp in copies(refs, send_sems, recv_sems):
            cp.start()

    def await_(refs, send_sems, recv_sems):
        cps = copies(refs, send_sems, recv_sems)
        for cp in cps:
            cp.wait_recv()
        for cp in cps:
            cp.wait_send()

    return issue, await_, n_swap + (7 if with_pack else 0)


def _sum_packs(pack, packs):
    x, y, c = lax.axis_index("x"), lax.axis_index("y"), lax.axis_index("c")
    me = jnp.reshape(4 * x + 2 * y + c, (1,)).astype(jnp.int32)

    def body(me_ref, own_ref, p_ref, o_ref):
        acc = jnp.where(me_ref[0] == 0, own_ref[...], p_ref[0])
        for d in range(1, 8):
            acc = acc + jnp.where(me_ref[0] == d, own_ref[...], p_ref[d])
        o_ref[...] = acc

    full = lambda s: pl.BlockSpec(s.shape, lambda i, t: (0,) * s.ndim)
    return pl.pallas_call(
        body, name="sum_packs",
        grid_spec=pltpu.PrefetchScalarGridSpec(num_scalar_prefetch=1, grid=(1,), in_specs=[full(pack), full(packs)],
                                               out_specs=full(pack)),
        out_shape=jax.ShapeDtypeStruct(pack.shape, F32),
    )(me, pack, packs)


def _adamw_update(g, w_ref, m_ref, v_ref, go, do, mo, vo):
    c1 = 1.0 / (1.0 - ADAM_B1 ** ADAM_STEP)
    c2 = 1.0 / (1.0 - ADAM_B2 ** ADAM_STEP)
    mn = ADAM_B1 * m_ref[...] + (1.0 - ADAM_B1) * g
    vn = ADAM_B2 * v_ref[...] + (1.0 - ADAM_B2) * (g * g)
    go[...] = g
    mo[...] = mn
    vo[...] = vn
    do[...] = -ADAM_LR * ((mn * c1) / (jnp.sqrt(vn * c2) + ADAM_EPS) + ADAM_WD * w_ref[...])


def _adamw(w, m, v, g1, g2, rows, name):
    r, cdim = w.shape
    rows = min(rows, r)

    def body(*refs):
        n_in = 4 if g2 is None else 5
        w_ref, m_ref, v_ref, g_ref = refs[:4]
        g = g_ref[...] if g2 is None else g_ref[...] + refs[4][...]
        _adamw_update(g, w_ref, m_ref, v_ref, *refs[n_in:n_in + 4])

    blk = pl.BlockSpec((rows, cdim), lambda i: (i, 0))
    args = [w, m, v, g1] + ([] if g2 is None else [g2])
    shp = jax.ShapeDtypeStruct((r, cdim), F32)
    return pl.pallas_call(
        body, name=name, grid=(r // rows,),
        in_specs=[blk] * len(args), out_specs=[blk] * 4, out_shape=[shp] * 4,
        compiler_params=_params(("parallel",), 20 * rows * cdim * 4 + 8 * 2**20),
    )(*_in_hbm(*args))


def _adamw_shard(wt, mt, vt, g1, g2):
    r, d = wt.shape
    cols = min(256, d)

    def body(w_ref, m_ref, v_ref, g_ref, g2_ref, go, do, mo, vo, pad_ref):
        chip = 2 * lax.axis_index("x") + lax.axis_index("y")
        back = [(ALIGNED_W - s) % ALIGNED_W for s in SHIFTS]
        pad_ref[...] = pltpu.roll(g_ref[...] + g2_ref[...], _by_chip(chip, back), 0)
        outs = [o.at[:, 0, :] for o in (go, do, mo, vo)]
        _adamw_update(pad_ref[0:r, :], w_ref, m_ref, v_ref, *outs)

    blk = pl.BlockSpec((r, cols), lambda i: (0, i))
    gblk = pl.BlockSpec((ALIGNED_W, cols), lambda i: (0, i))
    oblk = pl.BlockSpec((r, 1, cols), lambda i: (0, 0, i))
    shp = jax.ShapeDtypeStruct((r, 1, d), F32)
    return pl.pallas_call(
        body, name="adamw_w_in", grid=(d // cols,),
        in_specs=[blk] * 3 + [gblk] * 2, out_specs=[oblk] * 4, out_shape=[shp] * 4,
        scratch_shapes=[pltpu.VMEM((ALIGNED_W, cols), F32)],
        compiler_params=_params(("parallel",), 24 * ALIGNED_W * cols * 4 + 8 * 2**20),
    )(wt, mt, vt, g1, g2)


def _pad_lanes(a, width):
    return jnp.pad(a, ((0, 0), (0, width - a.shape[1])))


def _gathered_to_full(g):
    return jnp.transpose(g, (1, 0, 2)).reshape(g.shape[1], 4 * g.shape[2])


def _row(a):
    return _pad_lanes(a.reshape(1, -1), 1024)


def _small_pack(nin, cb, fn, al, dt, gn, cqw_shard, cw_shard):
    ad = jnp.concatenate([al.reshape(1, -1), dt.reshape(1, -1)], axis=1)
    rows = [_row(nin), _row(cb), _row(fn), _row(ad), _row(gn), cqw_shard.reshape(3, 1024), _row(cw_shard)]
    out = jnp.concatenate(rows, axis=0)
    return jnp.pad(out, ((0, 16 - out.shape[0]), (0, 0)))


def kernel(x, norm_in_w, w_in, conv_qkv_w, A_log, dt_bias, gdn_norm_w, conv_w, conv_b, w_out, final_norm_w, loss_target, m_norm_in_w, m_w_in, m_conv_qkv_w, m_A_log, m_dt_bias, m_gdn_norm_w, m_conv_w, m_conv_b, m_w_out, m_final_norm_w, v_norm_in_w, v_w_in, v_conv_qkv_w, v_A_log, v_dt_bias, v_gdn_norm_w, v_conv_w, v_conv_b, v_w_out, v_final_norm_w):
    chip = 2 * lax.axis_index("x") + lax.axis_index("y")
    a_shard = _align_shard(jnp.transpose(w_in, (2, 0, 1)))
    wo_b = _cast_bf16(w_out[0], 256, "cast_w_out")
    d_model = x.shape[-1]
    stack = lambda s: lax.empty((4,) + s.shape, s.dtype)
    wg0 = lax.empty((WG_BLOCKS * DH, d_model), BF16)
    wc0 = lax.empty((WC_BLOCKS * DH, d_model), BF16)
    ss_g, rs_g, bufs_g, tok_g = _gather_start("g", a_shard, wg0, [(conv_qkv_w[0], stack(conv_qkv_w[0]))])
    ss_c, rs_c, bufs_c, tok_c = _gather_start("c", bufs_g[0], wc0,
                                              [(conv_w[0], stack(conv_w[0])), (wo_b, stack(wo_b))])
    wg1, wc1, wog1, cqg1, cwg1 = _place_own(bufs_c[0], bufs_c[4], bufs_g[2], bufs_c[2],
                                            [bufs_g[1], bufs_c[1], bufs_c[5], bufs_g[3], bufs_c[3]])
    x0 = x[0]
    h = _rms_in(x0, _tie(_tie(norm_in_w, tok_g, "after_gather_start_g"), tok_c, "after_gather_start_c"))
    adam_in = [jnp.transpose(a[0]) for a in (w_in, m_w_in, v_w_in)]
    sp = lambda nin, cb, fn, al, dt, gn, cq, cwv: _small_pack(nin, cb, fn, al, dt, gn, cq[0], cwv[0])
    w_s = sp(norm_in_w, conv_b, final_norm_w, A_log, dt_bias, gdn_norm_w, conv_qkv_w, conv_w)
    m_s = sp(m_norm_in_w, m_conv_b, m_final_norm_w, m_A_log, m_dt_bias, m_gdn_norm_w, m_conv_qkv_w, m_conv_w)
    v_s = sp(v_norm_in_w, v_conv_b, v_final_norm_w, v_A_log, v_dt_bias, v_gdn_norm_w, v_conv_qkv_w, v_conv_w)
    a_thru, wg, _, cq_g = _gather_wait("g", ss_g, rs_g, [bufs_c[0], wg1, bufs_g[2], cqg1],
                                       [h, w_s, m_s, v_s] + adam_in[1:])
    w_g = _merge_edges(_sibling_forward("g", wg), G_EDGE, G_MIXED, "merge_edges_g")
    cqw = _gathered_to_full(cq_g)
    ad = jnp.pad(jnp.concatenate([A_log, dt_bias], axis=0), ((0, 0), (A_LANE, 0)))
    fwd_c = {}

    def on_q(q):
        _, wc, _, cw_g, _, wo_g = _gather_wait("c", ss_c, rs_c,
                                               [a_thru, wc1, bufs_c[2], cwg1, bufs_c[4], wog1], q)
        issue, _ = _sibling_forward_parts("c")
        ss, rs, (wc,), tok = _split_start("sibling_forward_start_c", issue, [wc], 3)
        fwd_c.update(ss=ss, rs=rs, wc=wc, cw_g=cw_g, wo_g=wo_g)
        return _tie(q, tok, "after_sibling_forward_start_c")

    def late(o):
        _, await_ = _sibling_forward_parts("c")
        (wc,) = _split_wait("sibling_forward_wait_c", await_, fwd_c["ss"], fwd_c["rs"], [fwd_c["wc"]], o)
        return (_merge_edges(wc, C_EDGE, C_MIXED, "merge_edges_c"), fwd_c["wo_g"].reshape(2 * GW, d_model),
                _gathered_to_full(fwd_c["cw_g"]))

    scat = {}

    def on_grad_c(g_c, g_wout, do):
        go4 = g_wout.reshape(4, GW // 2, d_model)
        land = lax.empty((3, ALIGNED_W, d_model), BF16)
        land_o = lax.empty((3, GW // 2, d_model), BF16)
        ss, rs, bufs, tok = _scatter_start("c", g_c, land, [(go4, land_o)])
        scat["c"] = (ss, rs, bufs)
        return _tie(do, tok, "after_scatter_start_c")

    def on_grad_g(g_g, dproj_g):
        ss, rs, bufs, tok = _scatter_start("g", _pair_reduce("g", g_g), scat["c"][2][1], [], halved=True)
        scat["g"] = (ss, rs, bufs)
        return _tie(dproj_g, tok, "after_scatter_start_g")

    gx, sm, _ = _local_step(x0, loss_target[0], h, w_g, cqw, late, norm_in_w, ad, gdn_norm_w, conv_b,
                            final_norm_w.reshape(1, -1), on_grad_c, on_grad_g, on_q)

    ss, rs, bufs = scat["c"]
    g_c, land, go4, land_o = _scatter_wait("c", ss, rs, [bufs[0], scat["g"][2][1], bufs[2], bufs[3]], gx)
    part_out = _sum_rows(go4, land_o, 128)
    ad_g = jnp.concatenate([sm["al"][:, A_LANE:], sm["dt"][:, A_LANE:]], axis=1)
    pack = jnp.concatenate([_row(sm["nin"]), _row(sm["cb"]), _row(sm["fn"]), _row(ad_g), _row(sm["gn"]),
                            jnp.concatenate(sm["cq"], axis=1).reshape(12, 1024), sm["cw"], _row(sm["loss"])], axis=0)
    pack = jnp.pad(pack, ((0, PACK_ROWS - pack.shape[0]), (0, 0)))
    issue, await_a, nsem = _exchange_parts(1, True)
    ss_a, rs_a, bufs_a, tok_a = _split_start(
        "exchange_start_small", issue,
        [part_out, lax.empty(part_out.shape, F32), pack, lax.empty((8,) + pack.shape, F32)], nsem)
    ss, rs, bufs = scat["g"]
    g_g, land = _scatter_wait("g", ss, rs, [bufs[0], land], [gx, tok_a], halved=True)
    part_in = _sum_shard(g_g, g_c, land)
    issue, await_b, nsem = _exchange_parts(1, False)
    ss_b, rs_b, bufs_b, tok_b = _split_start("exchange_start_w_in", issue,
                                             [part_in, lax.empty(part_in.shape, F32)], nsem)
    part_out, sib_out, pack, packs = _split_wait("exchange_wait_small", await_a, ss_a, rs_a, bufs_a, tok_b)
    tot = _sum_packs(pack, packs)
    g_wo, d_wo, m_wo, v_wo = _adamw(w_out[0], m_w_out[0], v_w_out[0], part_out, sib_out, 128, "adamw_w_out")
    g_cq_sh = lax.dynamic_slice_in_dim(tot[R_CQ:R_CQ + 12].reshape(4, 3 * GW), chip * 768, 768, axis=1)
    g_cw_sh = lax.dynamic_slice_in_dim(tot[R_CW:R_CW + 3], chip * 256, 256, axis=1)
    g_s = _small_pack(tot[R_NIN], tot[R_CB], tot[R_FN], tot[R_AD, :HEADS], tot[R_AD, HEADS:2 * HEADS],
                      tot[R_GN, :DH], g_cq_sh, g_cw_sh)
    small = _adamw(w_s, m_s, v_s, g_s, None, 16, "adamw_small")
    part_in, sib_in = _split_wait("exchange_wait_w_in", await_b, ss_b, rs_b, bufs_b, [small[0], d_wo])
    g_wi, d_wi, m_wi, v_wi = [jnp.transpose(a, (1, 2, 0))[0] for a in _adamw_shard(*adam_in, part_in, sib_in)]

    def unpack(a, big_in, big_out):
        return (a[0:1], big_in[None], a[5:8].reshape(1, 4, 768), a[3:4, :HEADS], a[3:4, HEADS:2 * HEADS],
                a[4:5, :DH], a[8, :768].reshape(1, 3, 256), a[1:2], big_out[None], a[2])

    loss = tot[R_LOSS, 0]
    return (loss, gx[None], *unpack(small[0], g_wi, g_wo), *unpack(small[1], d_wi, d_wo),
            *unpack(small[2], m_wi, m_wo), *unpack(small[3], v_wi, v_wo))
```

```python
import functools
import math

import jax
import jax.numpy as jnp
from jax import lax
from jax.experimental import pallas as pl
from jax.experimental.pallas import tpu as pltpu

F32 = jnp.float32
BF16 = jnp.bfloat16
MESH = pl.DeviceIdType.MESH
ANY = pl.BlockSpec(memory_space=pl.ANY)

HEADS = 8
DH = 128
CH = 64
GW = HEADS * DH
EPS = 1e-6
VMEM_V7X = 64 * 1024 * 1024

QB, KB, VB, ZB, BAB = 0, 8, 16, 24, 32
A_LANE = 120
NG, NC = 33, 32
GW_COLS, CW_COLS = NG * DH, NC * DH

SHARD_W = 2052
ALIGNED_BLOCKS = 17
ALIGNED_W = ALIGNED_BLOCKS * DH
SHIFTS = (0, 4, ALIGNED_W - 8, ALIGNED_W - 4)
G_EDGE, C_EDGE = 34, 32
G_SPARE, C_SPARE = 33, 34
WG_BLOCKS, WC_BLOCKS = 38, 36
G_MIXED, C_MIXED = (2, BAB), (4 * 7 + 1,)


def _shard_blocks(chip, edges):
    g, c = "g", "c"
    if chip == 0:
        out = [(g, 3 * b) for b in range(8)] + [(g, 3 * b + 1) for b in range(8)] + [(g, G_EDGE, G_MIXED[0])]
    elif chip == 1:
        out = [(g, G_EDGE + 1, G_MIXED[0])] + [(g, 3 * b + 2) for b in range(1, 8)]
        out += [(g, ZB + b) for b in range(8)] + [(g, G_EDGE + 2, G_MIXED[1])]
    elif chip == 2:
        out = [(c, 4 * b) for b in range(8)] + [(c, 4 * b + 1) for b in range(7)]
        out += [(c, C_EDGE, C_MIXED[0]), (g, G_EDGE + 3, G_MIXED[1])]
    else:
        out = [(c, 4 * b + 2) for b in range(8)] + [(c, 4 * b + 3) for b in range(8)] + [(c, C_EDGE + 1, C_MIXED[0])]
    return [(o[0], o[1] if (edges or len(o) == 2) else o[2]) for o in out]


def _by_chip(chip, vals):
    if all(v == vals[0] for v in vals):
        return vals[0]
    r = vals[3]
    for kk in (2, 1, 0):
        r = jnp.where(chip == kk, vals[kk], r)
    return r

ADAM_LR, ADAM_B1, ADAM_B2, ADAM_EPS, ADAM_WD, ADAM_STEP = 0.001, 0.9, 0.999, 1e-08, 0.01, 10

R_NIN, R_CB, R_FN, R_AD, R_GN, R_CQ, R_CW, R_LOSS, PACK_ROWS = 0, 1, 2, 3, 4, 5, 17, 20, 24

NN = ((1,), (0,))
NT = ((1,), (1,))
TN = ((0,), (0,))


def _dot(a, b, dims=NN, mode="lo"):
    dn = (dims, ((), ()))
    if mode == "hi":
        return lax.dot_general(a, b, dn, precision=lax.Precision.HIGHEST, preferred_element_type=F32)
    ah, bh = a.astype(BF16), b.astype(BF16)
    out = lax.dot_general(ah, bh, dn, preferred_element_type=F32)
    if mode == "x3":
        al = (a - ah.astype(F32)).astype(BF16)
        bl = (b - bh.astype(F32)).astype(BF16)
        out = out + lax.dot_general(ah, bl, dn, preferred_element_type=F32)
        out = out + lax.dot_general(al, bh, dn, preferred_element_type=F32)
    return out


P_GRAM, P_INV, P_SOL, P_SCAN, P_SCANB, P_BWD = "lo", "lo", "lo", "lo", "lo", "lo"
P_CUM = "x3"


def _params(sem=None, vmem=None):
    kw = {}
    if sem is not None:
        kw["dimension_semantics"] = sem
    if vmem is not None:
        kw["vmem_limit_bytes"] = int(min(max(vmem, 32 * 2**20), VMEM_V7X - 8 * 2**20))
    return pltpu.CompilerParams(**kw)


def _in_hbm(*arrays):
    return [pltpu.with_memory_space_constraint(a, pltpu.HBM) for a in arrays]


def _sigmoid(x):
    return 1.0 / (1.0 + jnp.exp(-x))


def _dsilu(x, s):
    return s * (1.0 + x * (1.0 - s))


def _rows(shape):
    return lax.broadcasted_iota(jnp.int32, shape, 0)


def _shift_down(x, s):
    if s == 0:
        return x
    return jnp.where(_rows(x.shape) >= s, pltpu.roll(x, s, 0), 0.0)


def _shift_up(x, s):
    if s == 0:
        return x
    n = x.shape[0]
    return jnp.where(_rows(x.shape) < n - s, pltpu.roll(x, n - s, 0), 0.0)


def _matmul(a, b, dims, out_dtype, tm, tn, tk, name, add=None, n=None, b_outer=False):
    if dims == NN:
        (m, k), n = a.shape, b.shape[1]
    elif dims == NT:
        (m, k), n = a.shape, (n or b.shape[0])
    else:
        (k, m), n = a.shape, b.shape[1]
    tm, tn, tk = min(tm, m), min(tn, n), min(tk, k)
    assert m % tm == 0 and n % tn == 0 and k % tk == 0, (name, m, n, k, tm, tn, tk)
    nk = k // tk

    def body(*refs):
        if add is None:
            a_ref, b_ref, o_ref = refs[:3]
            add_ref = None
        else:
            a_ref, b_ref, add_ref, o_ref = refs[:4]
        part = _dot(a_ref[...], b_ref[...], dims)
        if nk == 1:
            if add_ref is not None:
                part = part + add_ref[...]
            o_ref[...] = part.astype(out_dtype)
            return
        acc = refs[-1]
        kk = pl.program_id(2)

        @pl.when(kk == 0)
        def _():
            acc[...] = part

        @pl.when(kk > 0)
        def _():
            acc[...] += part

        @pl.when(kk == nk - 1)
        def _():
            r = acc[...]
            if add_ref is not None:
                r = r + add_ref[...]
            o_ref[...] = r.astype(out_dtype)

    ij = (lambda g0, g1: (g1, g0)) if b_outer else (lambda g0, g1: (g0, g1))

    def spec(shape, pick):
        return pl.BlockSpec(shape, lambda g0, g1, kk: pick(*ij(g0, g1), kk))

    a_spec = spec((tk, tm), lambda i, j, kk: (kk, i)) if dims == TN else spec((tm, tk), lambda i, j, kk: (i, kk))
    b_spec = spec((tn, tk), lambda i, j, kk: (j, kk)) if dims == NT else spec((tk, tn), lambda i, j, kk: (kk, j))
    o_spec = spec((tm, tn), lambda i, j, kk: (i, j))
    in_specs = [a_spec, b_spec]
    args = [a, b]
    if add is not None:
        in_specs.append(o_spec)
        args.append(add)
    osz = jnp.dtype(out_dtype).itemsize
    est = 2 * (tm * tk * a.dtype.itemsize + tk * tn * b.dtype.itemsize + tm * tn * osz)
    est += 3 * tm * tn * 4 + (2 * tm * tn * 4 if add is not None else 0)
    return pl.pallas_call(
        body, name=name, grid=(n // tn, m // tm, nk) if b_outer else (m // tm, n // tn, nk),
        in_specs=in_specs, out_specs=o_spec,
        out_shape=jax.ShapeDtypeStruct((m, n), out_dtype),
        scratch_shapes=[pltpu.VMEM((tm, tn), F32)] if nk > 1 else [],
        compiler_params=_params(("parallel", "parallel", "arbitrary"), est + 8 * 2**20),
    )(*args)


def _cast_bf16(a, rows, name):
    r, c = a.shape
    rows = min(rows, r)

    def body(a_ref, o_ref):
        o_ref[...] = a_ref[...].astype(BF16)

    return pl.pallas_call(
        body, name=name, grid=(r // rows,),
        in_specs=[pl.BlockSpec((rows, c), lambda i: (i, 0))],
        out_specs=pl.BlockSpec((rows, c), lambda i: (i, 0)),
        out_shape=jax.ShapeDtypeStruct((r, c), BF16),
        compiler_params=_params(("parallel",)),
    )(a)


def _align_shard(wt):
    r, _, d = wt.shape
    cols = min(256, d)

    def body(w_ref, o_ref, pad_ref):
        chip = 2 * lax.axis_index("x") + lax.axis_index("y")
        pad_ref[...] = jnp.zeros_like(pad_ref)
        pad_ref[0:r, :] = w_ref[:, 0, :]
        o_ref[...] = pltpu.roll(pad_ref[...], _by_chip(chip, SHIFTS), 0).astype(BF16)

    return pl.pallas_call(
        body, name="align_shard", grid=(d // cols,),
        in_specs=[pl.BlockSpec((r, 1, cols), lambda i: (0, 0, i))],
        out_specs=pl.BlockSpec((ALIGNED_W, cols), lambda i: (0, i)),
        out_shape=jax.ShapeDtypeStruct((ALIGNED_W, d), BF16),
        scratch_shapes=[pltpu.VMEM((ALIGNED_W, cols), F32)],
        compiler_params=_params(("parallel",)),
    )(wt)


def _rms_in(x, w):
    n, d = x.shape
    tr = min(256, n)

    def body(x_ref, w_ref, h_ref):
        xv = x_ref[...]
        r = lax.rsqrt(jnp.mean(xv * xv, axis=-1, keepdims=True) + EPS)
        h_ref[...] = (xv * r * w_ref[...]).astype(BF16)

    return pl.pallas_call(
        body, name="rms_in", grid=(n // tr,),
        in_specs=[pl.BlockSpec((tr, d), lambda i: (i, 0)), pl.BlockSpec((1, d), lambda i: (0, 0))],
        out_specs=pl.BlockSpec((tr, d), lambda i: (i, 0)),
        out_shape=jax.ShapeDtypeStruct((n, d), BF16),
        compiler_params=_params(("parallel",)),
    )(x, w)


def _conv_silu(p, w_ref, taps):
    c = None
    for j in range(taps):
        t = _shift_down(p, taps - 1 - j) * w_ref[j:j + 1, :]
        c = t if c is None else c + t
    return c


def _prep_qkv(proj, cw):
    n = proj.shape[0]

    def body(p3, wq, wk, wv, q_ref, k_ref, v_ref):
        for kind, (w_ref, o_ref) in enumerate(((wq, q_ref), (wk, k_ref), (wv, v_ref))):
            c = _conv_silu(p3[:, kind * DH:(kind + 1) * DH], w_ref, 4)
            a = c * _sigmoid(c)
            if kind < 2:
                r = lax.rsqrt(jnp.sum(a * a, axis=-1, keepdims=True) + EPS)
                a = a * (r * (DH ** -0.5 if kind == 0 else 1.0))
            o_ref[...] = a

    col = pl.BlockSpec((n, DH), lambda h: (0, h))
    wcol = lambda base: pl.BlockSpec((4, DH), lambda h: (0, base + h))
    out = jax.ShapeDtypeStruct((n, GW), F32)
    return pl.pallas_call(
        body, name="prep_qkv", grid=(HEADS,),
        in_specs=[pl.BlockSpec((n, 3 * DH), lambda h: (0, h)), wcol(QB), wcol(KB), wcol(VB)],
        out_specs=[col] * 3, out_shape=[out] * 3,
        compiler_params=_params(("parallel",), 40 * 2**20),
    )(proj, cw, cw, cw)


def _prep_qkv_bwd(proj, cw, dq, dk, dv, dproj):
    n = proj.shape[0]

    def body(p3, wq, wk, wv, dq_ref, dk_ref, dv_ref, _, o3, gq, gk, gv):
        for kind, (w_ref, d_ref, g_ref) in enumerate(((wq, dq_ref, gq), (wk, dk_ref, gk), (wv, dv_ref, gv))):
            p = p3[:, kind * DH:(kind + 1) * DH]
            shifted = [_shift_down(p, 3 - j) for j in range(4)]
            c = shifted[0] * w_ref[0:1, :]
            for j in range(1, 4):
                c = c + shifted[j] * w_ref[j:j + 1, :]
            s = _sigmoid(c)
            a = c * s
            d = d_ref[...]
            if kind < 2:
                r = lax.rsqrt(jnp.sum(a * a, axis=-1, keepdims=True) + EPS)
                sc = DH ** -0.5 if kind == 0 else 1.0
                d = (sc * r) * (d - a * ((r * r) * jnp.sum(d * a, axis=-1, keepdims=True)))
            dc = d * _dsilu(c, s)
            dp = None
            for j in range(4):
                g_ref[j:j + 1, :] = jnp.sum(dc * shifted[j], axis=0, keepdims=True)
                t = _shift_up(dc, 3 - j) * w_ref[j:j + 1, :]
                dp = t if dp is None else dp + t
            o3[:, kind * DH:(kind + 1) * DH] = dp.astype(BF16)

    col = pl.BlockSpec((n, DH), lambda h: (0, h))
    wcol = lambda base: pl.BlockSpec((4, DH), lambda h: (0, base + h))
    p3spec = pl.BlockSpec((n, 3 * DH), lambda h: (0, h))
    return pl.pallas_call(
        body, name="prep_qkv_bwd", grid=(HEADS,),
        in_specs=[p3spec, wcol(QB), wcol(KB), wcol(VB), col, col, col, ANY],
        out_specs=[p3spec] + [wcol(0)] * 3,
        out_shape=[jax.ShapeDtypeStruct(dproj.shape, BF16)] + [jax.ShapeDtypeStruct((4, GW), F32)] * 3,
        input_output_aliases={7: 0},
        compiler_params=_params(("parallel",), 48 * 2**20),
    )(proj, cw, cw, cw, dq, dk, dv, dproj)


CPB = 8
SCAN_CPS = 4


def _tri(lower, rows):
    i = lax.broadcasted_iota(jnp.int32, (rows, rows), 0)
    j = lax.broadcasted_iota(jnp.int32, (rows, rows), 1)
    return jnp.where((i // CH == j // CH) & ((i >= j) if lower else (j >= i)), 1.0, 0.0)


def _lane(shape):
    return lax.broadcasted_iota(jnp.int32, shape, 1)


def _prep_bg(proj, ad):
    n = proj.shape[0]
    nch = n // CH
    cpb = CPB if nch % CPB == 0 else 1
    rows = cpb * CH

    def body(p_ref, ad_ref, bg_ref, bgt_ref):
        p = p_ref[...]
        lane = _lane(p.shape)
        beta = _sigmoid(p)
        xa = p + ad_ref[1:2, :]
        sp = jnp.maximum(xa, 0.0) + jnp.log(1.0 + jnp.exp(-jnp.abs(xa)))
        g = pltpu.roll(-jnp.exp(ad_ref[0:1, :]) * sp, DH - A_LANE + HEADS, 1)
        gc = _dot(_tri(True, rows), g, NN, P_CUM)
        bg = jnp.where(lane < HEADS, beta, jnp.where(lane < 2 * HEADS, gc, 0.0))
        bg_ref[...] = bg
        for ci in range(cpb):
            bgt_ref[ci] = bg[ci * CH:(ci + 1) * CH, :].T

    return pl.pallas_call(
        body, name="prep_bg", grid=(nch // cpb,),
        in_specs=[pl.BlockSpec((rows, DH), lambda i: (i, BAB)), pl.BlockSpec((2, DH), lambda i: (0, 0))],
        out_specs=[pl.BlockSpec((rows, DH), lambda i: (i, 0)), pl.BlockSpec((cpb, DH, CH), lambda i: (i, 0, 0))],
        out_shape=[jax.ShapeDtypeStruct((n, DH), F32), jax.ShapeDtypeStruct((nch, DH, CH), F32)],
        compiler_params=_params(("parallel",)),
    )(*_in_hbm(proj, ad))


def _prep_bg_bwd(proj, ad, dbg, dproj):
    n = proj.shape[0]
    nch = n // CH
    cpb = CPB if nch % CPB == 0 else 1
    rows = cpb * CH

    def body(p_ref, ad_ref, d_ref, _, o_ref, ga_ref, gd_ref):
        p = p_ref[...]
        d = d_ref[...]
        lane = _lane(p.shape)
        beta = _sigmoid(p)
        xa = p + ad_ref[1:2, :]
        sp = jnp.maximum(xa, 0.0) + jnp.log(1.0 + jnp.exp(-jnp.abs(xa)))
        na = -jnp.exp(ad_ref[0:1, :])
        dg = pltpu.roll(_dot(_tri(False, rows), d, NN, P_CUM), A_LANE - HEADS, 1)
        da = dg * na * _sigmoid(xa)
        is_g = lane >= A_LANE
        o_ref[...] = jnp.where(lane < HEADS, d * beta * (1.0 - beta), jnp.where(is_g, da, 0.0)).astype(BF16)
        ga = jnp.sum(jnp.where(is_g, dg * na * sp, 0.0), axis=0, keepdims=True)
        gd = jnp.sum(jnp.where(is_g, da, 0.0), axis=0, keepdims=True)

        @pl.when(pl.program_id(0) == 0)
        def _():
            ga_ref[...] = jnp.zeros_like(ga_ref)
            gd_ref[...] = jnp.zeros_like(gd_ref)

        ga_ref[...] += ga
        gd_ref[...] += gd

    one = pl.BlockSpec((1, DH), lambda i: (0, 0))
    return pl.pallas_call(
        body, name="prep_bg_bwd", grid=(nch // cpb,),
        in_specs=[pl.BlockSpec((rows, DH), lambda i: (i, BAB)), pl.BlockSpec((2, DH), lambda i: (0, 0)),
                  pl.BlockSpec((rows, DH), lambda i: (i, 0)), ANY],
        out_specs=[pl.BlockSpec((rows, DH), lambda i: (i, BAB)), one, one],
        out_shape=[jax.ShapeDtypeStruct(dproj.shape, BF16), jax.ShapeDtypeStruct((1, DH), F32),
                   jax.ShapeDtypeStruct((1, DH), F32)],
        input_output_aliases={3: 0},
        compiler_params=_params(("arbitrary",)),
    )(proj, ad, dbg, dproj)


def _gdn_out(o, proj, wg):
    n = o.shape[0]

    def body(o_ref, z_ref, w_ref, y_ref):
        ov, z = o_ref[...], z_ref[...]
        r = lax.rsqrt(jnp.mean(ov * ov, axis=-1, keepdims=True) + EPS)
        y_ref[...] = (ov * r * w_ref[...] * (z * _sigmoid(z))).astype(BF16)

    return pl.pallas_call(
        body, name="gdn_out", grid=(HEADS,),
        in_specs=[pl.BlockSpec((n, DH), lambda h: (0, h)), pl.BlockSpec((n, DH), lambda h: (0, ZB + h)),
                  pl.BlockSpec((1, DH), lambda h: (0, 0))],
        out_specs=pl.BlockSpec((n, DH), lambda h: (0, h)),
        out_shape=jax.ShapeDtypeStruct((n, 2 * GW), BF16),
        compiler_params=_params(("parallel",)),
    )(o, proj, wg)


def _gdn_out_bwd(o, proj, wg, dout_b, w_out):
    n = o.shape[0]
    d_model = dout_b.shape[1]

    def body(o_ref, z_ref, w_ref, g_ref, wo_ref, do_ref, dz_ref, gw_ref):
        ov, z, w = o_ref[...], z_ref[...], w_ref[...]
        d = _dot(g_ref[...], wo_ref[...], NT)
        r = lax.rsqrt(jnp.mean(ov * ov, axis=-1, keepdims=True) + EPS)
        nrm = ov * r
        s = _sigmoid(z)
        dz_ref[...] = (d * (nrm * w) * _dsilu(z, s)).astype(BF16)
        dn_w = d * (z * s)
        gw = jnp.sum(dn_w * nrm, axis=0, keepdims=True)
        dn = dn_w * w
        do_ref[...] = (r * (dn - nrm * jnp.mean(dn * nrm, axis=-1, keepdims=True))).astype(BF16)

        @pl.when(pl.program_id(0) == 0)
        def _():
            gw_ref[...] = jnp.zeros_like(gw_ref)

        gw_ref[...] += gw

    return pl.pallas_call(
        body, name="gdn_out_bwd", grid=(HEADS,),
        in_specs=[pl.BlockSpec((n, DH), lambda h: (0, h)), pl.BlockSpec((n, DH), lambda h: (0, ZB + h)),
                  pl.BlockSpec((1, DH), lambda h: (0, 0)), pl.BlockSpec((n, d_model), lambda h: (0, 0)),
                  pl.BlockSpec((DH, d_model), lambda h: (h, 0))],
        out_specs=[pl.BlockSpec((n, DH), lambda h: (0, h)), pl.BlockSpec((n, DH), lambda h: (0, ZB + h)),
                   pl.BlockSpec((1, DH), lambda h: (0, 0))],
        out_shape=[jax.ShapeDtypeStruct((n, GW), BF16), jax.ShapeDtypeStruct((n, GW_COLS), BF16),
                   jax.ShapeDtypeStruct((1, DH), F32)],
        compiler_params=_params(("arbitrary",), 40 * 2**20),
    )(o, proj, wg, dout_b, w_out)


def _conv_branch(proj, w3, b, mix):
    n = proj.shape[0]

    def body(p4, w_ref, b_ref, _, y_ref):
        u = p4[:, DH:2 * DH] * p4[:, 2 * DH:3 * DH]
        cc = _conv_silu(u, w_ref, 3) + b_ref[...]
        z = p4[:, 3 * DH:4 * DH]
        y_ref[...] = (p4[:, 0:DH] * cc * (z * _sigmoid(z))).astype(BF16)

    return pl.pallas_call(
        body, name="conv_branch", grid=(HEADS,),
        in_specs=[pl.BlockSpec((n, 4 * DH), lambda h: (0, h)), pl.BlockSpec((3, DH), lambda h: (0, h)),
                  pl.BlockSpec((1, DH), lambda h: (0, h)), ANY],
        out_specs=pl.BlockSpec((n, DH), lambda h: (0, HEADS + h)),
        out_shape=jax.ShapeDtypeStruct(mix.shape, BF16),
        input_output_aliases={3: 0},
        compiler_params=_params(("parallel",), 40 * 2**20),
    )(*_in_hbm(proj, w3, b, mix))


def _conv_branch_bwd(proj, w3, b, dout_b, w_out):
    n = proj.shape[0]
    d_model = dout_b.shape[1]

    def body(p4, w_ref, b_ref, g_ref, wo_ref, o4, gw_ref, gbias_ref):
        gb, gcv, hc, z = p4[:, 0:DH], p4[:, DH:2 * DH], p4[:, 2 * DH:3 * DH], p4[:, 3 * DH:4 * DH]
        d = _dot(g_ref[...], wo_ref[...], NT)
        dgb, dgc, dhc, dzc = (o4.at[:, kk * DH:(kk + 1) * DH] for kk in range(4))
        u = gcv * hc
        cc = _conv_silu(u, w_ref, 3) + b_ref[...]
        s = _sigmoid(z)
        dzc[...] = (d * (gb * cc) * _dsilu(z, s)).astype(BF16)
        dp = d * (z * s)
        dgb[...] = (dp * cc).astype(BF16)
        dcc = dp * gb
        gbias_ref[...] = jnp.sum(dcc, axis=0, keepdims=True)
        du = None
        for j in range(3):
            gw_ref[j:j + 1, :] = jnp.sum(dcc * _shift_down(u, 2 - j), axis=0, keepdims=True)
            t = _shift_up(dcc, 2 - j) * w_ref[j:j + 1, :]
            du = t if du is None else du + t
        dgc[...] = (du * hc).astype(BF16)
        dhc[...] = (du * gcv).astype(BF16)

    p4spec = pl.BlockSpec((n, 4 * DH), lambda h: (0, h))
    return pl.pallas_call(
        body, name="conv_branch_bwd", grid=(HEADS,),
        in_specs=[p4spec, pl.BlockSpec((3, DH), lambda h: (0, h)), pl.BlockSpec((1, DH), lambda h: (0, h)),
                  pl.BlockSpec((n, d_model), lambda h: (0, 0)), pl.BlockSpec((DH, d_model), lambda h: (HEADS + h, 0))],
        out_specs=[p4spec, pl.BlockSpec((3, DH), lambda h: (0, h)), pl.BlockSpec((1, DH), lambda h: (0, h))],
        out_shape=[jax.ShapeDtypeStruct((n, CW_COLS), BF16), jax.ShapeDtypeStruct((3, GW), F32),
                   jax.ShapeDtypeStruct((1, GW), F32)],
        compiler_params=_params(("parallel",), 52 * 2**20),
    )(proj, w3, b, dout_b, w_out)


def _out_loss(mix, w_out, x, tgt, wf):
    n, d = x.shape
    kdim = mix.shape[1]
    tr = min(256, n)

    def body(m_ref, wo_ref, x_ref, t_ref, w_ref, do_ref, dob_ref, gw_ref, loss_ref):
        ov = _dot(m_ref[...], wo_ref[...], NN) + x_ref[...]
        w = w_ref[...]
        r = lax.rsqrt(jnp.mean(ov * ov, axis=-1, keepdims=True) + EPS)
        nrm = ov * r
        e = nrm * w - t_ref[...]
        dy = e * (1.0 / d)
        dn = dy * w
        dout = r * (dn - nrm * jnp.mean(dn * nrm, axis=-1, keepdims=True))
        do_ref[...] = dout
        dob_ref[...] = dout.astype(BF16)

        @pl.when(pl.program_id(0) == 0)
        def _():
            gw_ref[...] = jnp.zeros_like(gw_ref)
            loss_ref[...] = jnp.zeros_like(loss_ref)

        gw_ref[...] += jnp.sum(dy * nrm, axis=0, keepdims=True)
        loss_ref[...] += (0.5 / d) * jnp.sum(jnp.sum(e * e, axis=-1, keepdims=True), axis=0, keepdims=True)

    row = pl.BlockSpec((tr, d), lambda i: (i, 0))
    return pl.pallas_call(
        body, name="out_loss", grid=(n // tr,),
        in_specs=[pl.BlockSpec((tr, kdim), lambda i: (i, 0)), pl.BlockSpec((kdim, d), lambda i: (0, 0)), row, row,
                  pl.BlockSpec((1, d), lambda i: (0, 0))],
        out_specs=[row, row, pl.BlockSpec((1, d), lambda i: (0, 0)), pl.BlockSpec((1, 1), lambda i: (0, 0))],
        out_shape=[jax.ShapeDtypeStruct((n, d), F32), jax.ShapeDtypeStruct((n, d), BF16),
                   jax.ShapeDtypeStruct((1, d), F32), jax.ShapeDtypeStruct((1, 1), F32)],
        compiler_params=_params(("arbitrary",), 40 * 2**20),
    )(mix, w_out, x, tgt, wf)


def _dh_rms_bwd(dproj, w_t, dh0, x, w, dout, tk):
    n, d = x.shape
    kdim = dproj.shape[1]
    tm = min(1024, n)
    tk = min(tk, kdim)
    nk = kdim // tk

    def body(a_ref, b_ref, dh0_ref, x_ref, w_ref, do_ref, dx_ref, gw_ref, acc):
        i, kk = pl.program_id(0), pl.program_id(1)
        part = _dot(a_ref[...], b_ref[...], NN)

        @pl.when(kk == 0)
        def _():
            acc[...] = part + dh0_ref[...]

        @pl.when(kk > 0)
        def _():
            acc[...] += part

        @pl.when((i == 0) & (kk == 0))
        def _():
            gw_ref[...] = jnp.zeros_like(gw_ref)

        @pl.when(kk == nk - 1)
        def _():
            xv, dhv = x_ref[...], acc[...]
            r = lax.rsqrt(jnp.mean(xv * xv, axis=-1, keepdims=True) + EPS)
            xn = xv * r
            dxn = dhv * w_ref[...]
            dx_ref[...] = r * (dxn - xn * jnp.mean(dxn * xn, axis=-1, keepdims=True)) + do_ref[...]
            gw_ref[...] += jnp.sum(dhv * xn, axis=0, keepdims=True)

    row = pl.BlockSpec((tm, d), lambda i, kk: (i, 0))
    one = pl.BlockSpec((1, d), lambda i, kk: (0, 0))
    return pl.pallas_call(
        body, name="dh_rms_bwd", grid=(n // tm, nk),
        in_specs=[pl.BlockSpec((tm, tk), lambda i, kk: (i, kk)), pl.BlockSpec((tk, d), lambda i, kk: (kk, 0)),
                  row, row, one, row],
        out_specs=[row, one],
        out_shape=[jax.ShapeDtypeStruct((n, d), F32), jax.ShapeDtypeStruct((1, d), F32)],
        scratch_shapes=[pltpu.VMEM((tm, d), F32)],
        compiler_params=_params(("arbitrary", "arbitrary"), 56 * 2**20),
    )(dproj, w_t, dh0, x, w, dout)


def _ij():
    i = lax.broadcasted_iota(jnp.int32, (CH, CH), 0)
    j = lax.broadcasted_iota(jnp.int32, (CH, CH), 1)
    return i, j


def _unit_lower_inverse(mats):
    i, j = _ij()
    eye = jnp.where(i == j, 1.0, 0.0)
    same16 = (i // 16) == (j // 16)
    same32 = (i // 32) == (j // 32)
    mm = lambda xs, ys: [_dot(x, y, NN, P_INV) for x, y in zip(xs, ys)]
    n1 = [jnp.where(same16, -a, 0.0) for a in mats]
    n2 = mm(n1, n1)
    n4 = mm(n2, n2)
    n8 = mm(n4, n4)
    t = [eye + x1 + x2 + x3 for x1, x2, x3 in zip(n1, n2, mm(n1, n2))]
    t = [x + y for x, y in zip(t, mm(t, n4))]
    t = [x + y for x, y in zip(t, mm(t, n8))]
    a1 = [jnp.where(same32 & jnp.logical_not(same16), a, 0.0) for a in mats]
    t = [x - y for x, y in zip(t, mm(t, mm(a1, t)))]
    a2 = [jnp.where(same32, 0.0, a) for a in mats]
    t = [x - y for x, y in zip(t, mm(t, mm(a2, t)))]
    return t


def _head_vectors(bg, bgt, h):
    bcol = bg[:, h:h + 1]
    gcol = bg[:, HEADS + h:HEADS + h + 1]
    grow = bgt[HEADS + h:HEADS + h + 1, :]
    return bcol, gcol, grow


def _decay(gcol, grow):
    i, j = _ij()
    return jnp.where(i >= j, jnp.exp(jnp.where(i >= j, gcol - grow, 0.0)), 0.0)


def _gdn_intra(q, k, v, bg, bgt):
    n = q.shape[0]
    nch = n // CH
    cps = 4 if nch % 4 == 0 else 1

    def body(q_ref, k_ref, v_ref, bg_ref, bgt_ref, u_ref, w_ref, p_ref, t_ref):
        i, j = _ij()
        items = [(ci, h) for ci in range(cps) for h in range(HEADS)]
        at = lambda ref, ci, h: ref.at[ci * CH:(ci + 1) * CH, h * DH:(h + 1) * DH]
        bgs = [bg_ref[ci * CH:(ci + 1) * CH, :] for ci in range(cps)]
        ks = [at(k_ref, ci, h)[...] for ci, h in items]
        vecs = [_head_vectors(bgs[ci], bgt_ref[ci], h) for ci, h in items]
        decs = [_decay(gcol, grow) for _, gcol, grow in vecs]
        kks = [_dot(kh, kh, NT, P_GRAM) for kh in ks]
        qks = [_dot(at(q_ref, ci, h)[...], kh, NT, P_GRAM) for (ci, h), kh in zip(items, ks)]
        ts = _unit_lower_inverse([jnp.where(i > j, bcol * kk * dec, 0.0)
                                  for (bcol, _, _), kk, dec in zip(vecs, kks, decs)])
        us = [_dot(t, at(v_ref, ci, h)[...] * bcol, NN, P_SOL) for t, (ci, h), (bcol, _, _) in zip(ts, items, vecs)]
        ws = [_dot(t, kh * (bcol * jnp.exp(gcol)), NN, P_SOL) for t, kh, (bcol, gcol, _) in zip(ts, ks, vecs)]
        for n_, (ci, h) in enumerate(items):
            p_ref[ci, h] = qks[n_] * decs[n_]
            t_ref[ci, h] = ts[n_].astype(BF16)
            at(u_ref, ci, h)[...] = us[n_]
            at(w_ref, ci, h)[...] = ws[n_].astype(BF16)

    row = pl.BlockSpec((cps * CH, GW), lambda c: (c, 0))
    sq = pl.BlockSpec((cps, HEADS, CH, CH), lambda c: (c, 0, 0, 0))
    big = jax.ShapeDtypeStruct((n, GW), F32)
    sqs = jax.ShapeDtypeStruct((nch, HEADS, CH, CH), F32)
    return pl.pallas_call(
        body, name="gdn_intra", grid=(nch // cps,),
        in_specs=[row, row, row, pl.BlockSpec((cps * CH, DH), lambda c: (c, 0)),
                  pl.BlockSpec((cps, DH, CH), lambda c: (c, 0, 0))],
        out_specs=[row, row, sq, sq],
        out_shape=[big, jax.ShapeDtypeStruct((n, GW), BF16), sqs, jax.ShapeDtypeStruct(sqs.shape, BF16)],
        compiler_params=_params(("parallel",)),
    )(q, k, v, bg, bgt)


def _gdn_scan(q, k, bg, u, w, p):
    n = q.shape[0]
    nch = n // CH
    cps = SCAN_CPS if nch % SCAN_CPS == 0 else 1

    def body(q_ref, k_ref, bg_ref, u_ref, w_ref, p_ref, o_ref, vn_ref, s_out, s_scr):
        @pl.when(pl.program_id(0) == 0)
        def _():
            s_scr[...] = jnp.zeros_like(s_scr)

        hs = range(HEADS)
        sls = [slice(h * DH, (h + 1) * DH) for h in hs]
        ss = [s_scr[h] for h in hs]
        for ci in range(cps):
            rs = slice(ci * CH, (ci + 1) * CH)
            bg = bg_ref[rs, :]
            gcols = [bg[:, HEADS + h:HEADS + h + 1] for h in hs]
            glasts = [g[CH - 1:CH, :] for g in gcols]
            wss = [_dot(w_ref[rs, sl], s, NN, P_SCAN) for sl, s in zip(sls, ss)]
            oqs = [_dot(q_ref[rs, sl] * jnp.exp(g), s, NN, P_SCAN) for sl, s, g in zip(sls, ss, gcols)]
            vns = [u_ref[rs, sl] - x for sl, x in zip(sls, wss)]
            ops = [_dot(p_ref[ci, h], vn, NN, P_SCAN) for h, vn in zip(hs, vns)]
            sns = [_dot(k_ref[rs, sl] * jnp.exp(gl - g), vn, TN, P_SCAN)
                   for sl, gl, g, vn in zip(sls, glasts, gcols, vns)]
            for h, sl in enumerate(sls):
                s_out[ci, :, sl] = ss[h].astype(BF16)
                vn_ref[rs, sl] = vns[h].astype(BF16)
                o_ref[rs, sl] = oqs[h] + ops[h]
            ss = [s * jnp.exp(gl) + sn for s, gl, sn in zip(ss, glasts, sns)]
        for h in hs:
            s_scr[h] = ss[h]

    row = pl.BlockSpec((cps * CH, GW), lambda c: (c, 0))
    big = jax.ShapeDtypeStruct((n, GW), F32)
    return pl.pallas_call(
        body, name="gdn_scan", grid=(nch // cps,),
        in_specs=[row, row, pl.BlockSpec((cps * CH, DH), lambda c: (c, 0)), row, row,
                  pl.BlockSpec((cps, HEADS, CH, CH), lambda c: (c, 0, 0, 0))],
        out_specs=[row, row, pl.BlockSpec((cps, DH, GW), lambda c: (c, 0, 0))],
        out_shape=[big, jax.ShapeDtypeStruct((n, GW), BF16), jax.ShapeDtypeStruct((nch, DH, GW), BF16)],
        scratch_shapes=[pltpu.VMEM((HEADS, DH, DH), F32)],
        compiler_params=_params(("arbitrary",)),
    )(q, k, bg, u, w, p)


def _gdn_scan_bwd(q, k, bg, w, p, vn, s_in, do):
    n = q.shape[0]
    nch = n // CH
    cps = SCAN_CPS if nch % SCAN_CPS == 0 else 1
    rev = lambda c: nch // cps - 1 - c

    def body(q_ref, k_ref, bg_ref, w_ref, p_ref, vn_ref, s_ref, do_ref,
             dqg_ref, dp_ref, du_ref, dw_ref, dks_ref, dgam_ref, ds_scr):
        @pl.when(pl.program_id(0) == 0)
        def _():
            ds_scr[...] = jnp.zeros_like(ds_scr)

        lane = _lane((1, DH))
        hs = range(HEADS)
        sls = [slice(h * DH, (h + 1) * DH) for h in hs]
        dss = [ds_scr[h] for h in hs]
        for ci in reversed(range(cps)):
            rs = slice(ci * CH, (ci + 1) * CH)
            bg = bg_ref[rs, :]
            gcols = [bg[:, HEADS + h:HEADS + h + 1] for h in hs]
            glasts = [g[CH - 1:CH, :] for g in gcols]
            ss = [s_ref[ci, :, sl] for sl in sls]
            dos = [do_ref[rs, sl] for sl in sls]
            vnl = [vn_ref[rs, sl] for sl in sls]
            dqgs = [_dot(d, s, NT, P_SCANB) for d, s in zip(dos, ss)]
            dps = [_dot(d, vn, NT, P_SCANB) for d, vn in zip(dos, vnl)]
            dvn1 = [_dot(p_ref[ci, h], d, TN, P_SCANB) for h, d in zip(hs, dos)]
            dvn2 = [_dot(k_ref[rs, sl] * jnp.exp(gl - g), ds, NN, P_SCANB)
                    for sl, gl, g, ds in zip(sls, glasts, gcols, dss)]
            dkss = [_dot(vn, ds, NT, P_SCANB) for vn, ds in zip(vnl, dss)]
            dsq = [_dot(q_ref[rs, sl] * jnp.exp(g), d, TN, P_SCANB) for sl, g, d in zip(sls, gcols, dos)]
            dvns = [a + b for a, b in zip(dvn1, dvn2)]
            dws = [_dot(dvn, s, NT, P_SCANB) for dvn, s in zip(dvns, ss)]
            dsw = [_dot(w_ref[rs, sl], dvn, TN, P_SCANB) for sl, dvn in zip(sls, dvns)]
            dgam = jnp.zeros((1, DH), F32)
            for h, sl in enumerate(sls):
                dqg_ref[rs, sl] = dqgs[h]
                dp_ref[ci, h] = dps[h]
                du_ref[rs, sl] = dvns[h].astype(BF16)
                dw_ref[rs, sl] = (-dws[h]).astype(BF16)
                dks_ref[rs, sl] = dkss[h]
                tot = jnp.sum(jnp.sum(dss[h] * ss[h], axis=-1, keepdims=True), axis=0, keepdims=True)
                dgam = dgam + jnp.where(lane == h, tot, 0.0)
            dgam_ref[ci] = jnp.broadcast_to(dgam, (8, DH))
            dss = [ds * jnp.exp(gl) + a - b for ds, gl, a, b in zip(dss, glasts, dsq, dsw)]
        for h in hs:
            ds_scr[h] = dss[h]

    row = pl.BlockSpec((cps * CH, GW), lambda c: (rev(c), 0))
    sq = pl.BlockSpec((cps, HEADS, CH, CH), lambda c: (rev(c), 0, 0, 0))
    big = jax.ShapeDtypeStruct((n, GW), F32)
    return pl.pallas_call(
        body, name="gdn_scan_bwd", grid=(nch // cps,),
        in_specs=[row, row, pl.BlockSpec((cps * CH, DH), lambda c: (rev(c), 0)), row, sq, row,
                  pl.BlockSpec((cps, DH, GW), lambda c: (rev(c), 0, 0)), row],
        out_specs=[row, sq, row, row, row, pl.BlockSpec((cps, 8, DH), lambda c: (rev(c), 0, 0))],
        out_shape=[big, jax.ShapeDtypeStruct((nch, HEADS, CH, CH), F32), jax.ShapeDtypeStruct((n, GW), BF16),
                   jax.ShapeDtypeStruct((n, GW), BF16), big,
                   jax.ShapeDtypeStruct((nch, 8, DH), F32)],
        scratch_shapes=[pltpu.VMEM((HEADS, DH, DH), F32)],
        compiler_params=_params(("arbitrary",)),
    )(q, k, bg, w, p, vn, s_in, do)


def _gdn_intra_bwd(q, k, v, bg, bgt, t, u, w, p, dqg, dp, du, dw, dks, dgam):
    n = q.shape[0]
    nch = n // CH
    cps = 2 if nch % 2 == 0 else 1

    def body(q_ref, k_ref, v_ref, bg_ref, bgt_ref, t_ref, u_ref, w_ref, p_ref,
             dqg_ref, dp_ref, du_ref, dw_ref, dks_ref, dgam_ref, dq_ref, dk_ref, dv_ref, dbg_ref):
        i, j = _ij()
        rows1 = lax.broadcasted_iota(jnp.int32, (CH, 1), 0)
        lane = _lane((CH, DH))
        rsum = lambda x: jnp.sum(x, axis=-1, keepdims=True)
        items = [(ci, h) for ci in range(cps) for h in range(HEADS)]
        at = lambda ref, it: ref.at[it[0] * CH:(it[0] + 1) * CH, it[1] * DH:(it[1] + 1) * DH]
        ld = lambda ref: [at(ref, it)[...] for it in items]
        bgs = [bg_ref[ci * CH:(ci + 1) * CH, :] for ci in range(cps)]
        qs, ks = ld(q_ref), ld(k_ref)
        vecs = [_head_vectors(bgs[ci], bgt_ref[ci], h) for ci, h in items]
        decs = [_decay(gcol, grow) for _, gcol, grow in vecs]
        ths = [t_ref[ci, h] for ci, h in items]
        drus = [_dot(th, x_, TN, P_BWD) for th, x_ in zip(ths, ld(du_ref))]
        drws = [_dot(th, x_, TN, P_BWD) for th, x_ in zip(ths, ld(dw_ref))]
        kks = [_dot(kh, kh, NT, P_GRAM) for kh in ks]
        da1 = [_dot(dru, x_, NT, P_BWD) for dru, x_ in zip(drus, ld(u_ref))]
        da2 = [_dot(drw, x_, NT, P_BWD) for drw, x_ in zip(drws, ld(w_ref))]
        das = [jnp.where(i > j, -(x_ + y_), 0.0) for x_, y_ in zip(da1, da2)]
        dkks = [da * bcol * dec for da, (bcol, _, _), dec in zip(das, vecs, decs)]
        dps = [dp_ref[ci, h] for ci, h in items]
        dqks = [dp_ * dec for dp_, dec in zip(dps, decs)]
        dq_ps = [_dot(dqk, kh, NN, P_BWD) for dqk, kh in zip(dqks, ks)]
        dk_ps = [_dot(dqk, qh, TN, P_BWD) for dqk, qh in zip(dqks, qs)]
        dk_as = [_dot(dkk, kh, NN, P_BWD) for dkk, kh in zip(dkks, ks)]
        dk_bs = [_dot(dkk, kh, TN, P_BWD) for dkk, kh in zip(dkks, ks)]
        bcols = [vc[0] for vc in vecs]
        gcols = [vc[1] for vc in vecs]
        gams = [jnp.exp(g) for g in gcols]
        glasts = [g[CH - 1:CH, :] for g in gcols]
        es = [jnp.exp(gl - g) for gl, g in zip(glasts, gcols)]
        kgs = [kh * gam for kh, gam in zip(ks, gams)]
        dqgs, dkss = ld(dqg_ref), ld(dks_ref)
        wks = [drw * kg for drw, kg in zip(drws, kgs)]
        kss = [dk_ * (kh * e) for dk_, kh, e in zip(dkss, ks, es)]
        r_beta = [rsum(dru * x_ + wk) for dru, x_, wk in zip(drus, ld(v_ref), wks)]
        r_ak = [rsum(da * kk * dec) for da, kk, dec in zip(das, kks, decs)]
        r_gc = [rsum(wk * bcol + dqg * (qh * gam) - ks_)
                for wk, bcol, dqg, qh, gam, ks_ in zip(wks, bcols, dqgs, qs, gams, kss)]
        tk_tot = [jnp.sum(jnp.sum(ks_, axis=0, keepdims=True), axis=-1, keepdims=True) for ks_ in kss]
        mdecs = [da * (bcol * kk * dec) + dp_ * p_ref[ci, h]
                 for (ci, h), da, bcol, kk, dec, dp_ in zip(items, das, bcols, kks, decs, dps)]
        r_md = [rsum(m) for m in mdecs]
        c_md = [rsum(jnp.where(i == j, jnp.sum(m, axis=0, keepdims=True), 0.0)) for m in mdecs]
        dbgs = [jnp.zeros((CH, DH), F32) for _ in range(cps)]
        for n_, (ci, h) in enumerate(items):
            at(dv_ref, (ci, h))[...] = bcols[n_] * drus[n_]
            at(dq_ref, (ci, h))[...] = gams[n_] * dqgs[n_] + dq_ps[n_]
            at(dk_ref, (ci, h))[...] = ((bcols[n_] * gams[n_]) * drws[n_] + dk_ps[n_] + dk_as[n_] + dk_bs[n_]
                                        + dkss[n_] * es[n_])
            dbeta = r_beta[n_] + r_ak[n_]
            dglast = tk_tot[n_] + dgam_ref[ci, 0:1, h:h + 1] * jnp.exp(glasts[n_])
            dgc = r_gc[n_] + r_md[n_] - c_md[n_] + jnp.where(rows1 == CH - 1, dglast, 0.0)
            dbgs[ci] = dbgs[ci] + jnp.where(lane == h, dbeta, 0.0) + jnp.where(lane == HEADS + h, dgc, 0.0)
        for ci in range(cps):
            dbg_ref[ci * CH:(ci + 1) * CH, :] = dbgs[ci]

    row = pl.BlockSpec((cps * CH, GW), lambda c: (c, 0))
    sq = pl.BlockSpec((cps, HEADS, CH, CH), lambda c: (c, 0, 0, 0))
    small = pl.BlockSpec((cps * CH, DH), lambda c: (c, 0))
    big = jax.ShapeDtypeStruct((n, GW), F32)
    return pl.pallas_call(
        body, name="gdn_intra_bwd", grid=(nch // cps,),
        in_specs=[row, row, row, small, pl.BlockSpec((cps, DH, CH), lambda c: (c, 0, 0)), sq, row, row, sq,
                  row, sq, row, row, row, pl.BlockSpec((cps, 8, DH), lambda c: (c, 0, 0))],
        out_specs=[row, row, row, small],
        out_shape=[big, big, big, jax.ShapeDtypeStruct((n, DH), F32)],
        compiler_params=_params(("parallel",)),
    )(q, k, v, bg, bgt, t, u, w, p, dqg, dp, du, dw, dks, dgam)


def _local_step(x, tgt, h, w_g, cqw, late, norm_in_w, ad, gdn_norm_w, conv_b, final_norm_w,
                on_grad_c=None, on_grad_g=None, on_q=None):
    proj_g = _matmul(h, w_g, NT, F32, 512, 1408, 1024, "mm_proj_g", n=GW_COLS, b_outer=True)
    q, k, v = _prep_qkv(proj_g, cqw)
    if on_q is not None:
        q = on_q(q)
    bg, bgt = _prep_bg(proj_g, ad)
    u, w, p, t = _gdn_intra(q, k, v, bg, bgt)
    o, vn, s_in = _gdn_scan(q, k, bg, u, w, p)
    w_c, w_out, conv_w = late(o)
    proj_c = _matmul(h, w_c, NT, F32, 512, 1024, 1024, "mm_proj_c", n=CW_COLS, b_outer=True)
    mix = _conv_branch(proj_c, conv_w, conv_b, _gdn_out(o, proj_g, gdn_norm_w))
    dout, dout_b, g_fn, loss = _out_loss(mix, w_out, x, tgt, final_norm_w)

    g_wout = _matmul(mix, dout_b, TN, BF16, 512, 512, 2048, "mm_gwout")
    do, dproj_g, g_gn = _gdn_out_bwd(o, proj_g, gdn_norm_w, dout_b, w_out)
    dproj_c, g_cw, g_cb = _conv_branch_bwd(proj_c, conv_w, conv_b, dout_b, w_out)
    g_c = _matmul(dproj_c, h, TN, BF16, 1024, 512, 2048, "mm_gwin_c")
    if on_grad_c is not None:
        do = on_grad_c(g_c, g_wout, do)
    dqg, dp, du, dw, dks, dgam = _gdn_scan_bwd(q, k, bg, w, p, vn, s_in, do)
    dq, dk, dv, dbg = _gdn_intra_bwd(q, k, v, bg, bgt, t, u, w, p, dqg, dp, du, dw, dks, dgam)
    dproj_g, gq, gk, gv = _prep_qkv_bwd(proj_g, cqw, dq, dk, dv, dproj_g)
    dproj_g, g_al, g_dt = _prep_bg_bwd(proj_g, ad, dbg, dproj_g)
    g_g = _matmul(dproj_g, h, TN, BF16, 1408, 512, 2048, "mm_gwin_g")
    if on_grad_g is not None:
        dproj_g = on_grad_g(g_g, dproj_g)
    dh = _matmul(dproj_g, w_g, NN, F32, 1024, 1024, 1408, "mm_dh_g")
    gx, g_nin = _dh_rms_bwd(dproj_c, w_c, dh, x, norm_in_w, dout, 1024)
    small = dict(nin=g_nin, cb=g_cb, fn=g_fn, al=g_al, dt=g_dt, gn=g_gn, cq=(gq, gk, gv), cw=g_cw, loss=loss)
    return gx, small, (g_g, g_c, g_wout)


def _place():
    x, y, c = lax.axis_index("x"), lax.axis_index("y"), lax.axis_index("c")
    chips = [(1 - x, y), (x, 1 - y), (1 - x, 1 - y)]
    return x, y, c, chips


def _blk(ref, b):
    if isinstance(b, int):
        return ref.at[b * DH:(b + 1) * DH, :]
    return ref.at[pl.ds(pl.multiple_of(b * DH, DH), DH), :]


HBM = pl.BlockSpec(memory_space=pltpu.HBM)
SEM = pl.BlockSpec(memory_space=pltpu.SEMAPHORE)
EFFECT = pltpu.SideEffectType.DATAFLOW_SIDE_EFFECTING


def _split_start(name, issue, bufs, n_sems):
    nbuf = len(bufs)

    def body(*refs):
        issue(refs[:nbuf], refs[nbuf], refs[nbuf + 1])
        refs[-1][...] = jnp.zeros_like(refs[-1])

    out = pl.pallas_call(
        body, name=name,
        out_shape=(pltpu.SemaphoreType.DMA((n_sems,)), pltpu.SemaphoreType.DMA((n_sems,)),
                   *[pltpu.HBM(b.shape, b.dtype) for b in bufs], jax.ShapeDtypeStruct((8, DH), F32)),
        in_specs=[HBM] * nbuf,
        out_specs=(SEM, SEM, *[HBM] * nbuf, pl.BlockSpec(memory_space=pltpu.VMEM)),
        input_output_aliases={a: 2 + a for a in range(nbuf)},
        compiler_params=pltpu.CompilerParams(has_side_effects=EFFECT),
    )(*[pltpu.with_memory_space_constraint(b, pltpu.HBM) for b in bufs])
    return out[0], out[1], list(out[2:2 + nbuf]), out[-1]


def _split_wait(name, await_, send_sems, recv_sems, bufs, after):
    nbuf = len(bufs)
    after = list(after) if isinstance(after, (list, tuple)) else [after]

    def body(*refs):
        await_(refs[:nbuf], refs[nbuf], refs[nbuf + 1])

    out = pl.pallas_call(
        body, name=name,
        out_shape=tuple(pltpu.HBM(b.shape, b.dtype) for b in bufs),
        in_specs=[HBM] * nbuf + [SEM, SEM] + [ANY] * len(after), out_specs=tuple([HBM] * nbuf),
        input_output_aliases={a: a for a in range(nbuf)},
        compiler_params=pltpu.CompilerParams(has_side_effects=EFFECT),
    )(*bufs, send_sems, recv_sems, *after)
    return list(out)


def _phase_blocks(chip, phase, edges, parity=None):
    return [(b, blk) for b, (grp, blk) in enumerate(_shard_blocks(chip, edges))
            if grp == phase and (parity is None or b % 2 == parity)]


def _cols(ref, nblk):
    return ref.at[0:nblk * DH, :]


def _block_table(chip, edges, spare_g, spare_c):
    rows = []
    for s in range(4):
        sb = _shard_blocks(s, edges)
        rows.append([[blk if grp == "g" else spare_g for grp, blk in sb],
                     [blk if grp == "c" else spare_c for grp, blk in sb],
                     [int(grp == "g") for grp, _ in sb], [s] * ALIGNED_BLOCKS])
    return jnp.asarray(rows, jnp.int32)[chip]


def _place_own(a_shard, wo, cq, cw, bufs):
    d = a_shard.shape[1]
    chip = 2 * lax.axis_index("x") + lax.axis_index("y")

    def body(t_ref, a_ref, wo_ref, cq_ref, cw_ref, *refs):
        wg_ref, wc_ref, wog_ref, cqg_ref, cwg_ref = refs[5:]
        wg_ref[...] = a_ref[...]
        wc_ref[...] = a_ref[...]

        @pl.when(pl.program_id(0) == 0)
        def _():
            wog_ref[0] = wo_ref[...]
            cqg_ref[0] = cq_ref[...]
            cwg_ref[0] = cw_ref[...]

    whole = lambda s: pl.BlockSpec(s.shape, lambda b, t: (0,) * s.ndim)
    slot = lambda s: pl.BlockSpec((1,) + s.shape, lambda b, t: (t[3, 0],) + (0,) * s.ndim)
    return pl.pallas_call(
        body, name="place_own",
        grid_spec=pltpu.PrefetchScalarGridSpec(
            num_scalar_prefetch=1, grid=(ALIGNED_BLOCKS,),
            in_specs=[pl.BlockSpec((DH, d), lambda b, t: (b, 0)), whole(wo), whole(cq), whole(cw)] + [ANY] * 5,
            out_specs=[pl.BlockSpec((DH, d), lambda b, t: (t[0, b], 0)),
                       pl.BlockSpec((DH, d), lambda b, t: (t[1, b], 0)), slot(wo), slot(cq), slot(cw)]),
        out_shape=[jax.ShapeDtypeStruct(b.shape, b.dtype) for b in bufs],
        input_output_aliases={5 + a: a for a in range(5)},
        compiler_params=_params(("arbitrary",)),
    )(_block_table(chip, True, G_SPARE, C_SPARE), a_shard, wo, cq, cw, *bufs)


def _tie(x, token, name):
    def body(x_ref, t_ref, o_ref):
        del x_ref, t_ref, o_ref

    return pl.pallas_call(
        body, name=name, in_specs=[ANY, ANY], out_specs=ANY,
        out_shape=jax.ShapeDtypeStruct(x.shape, x.dtype), input_output_aliases={0: 0},
    )(x, token)


def _gather_start(phase, a_shard, w_grp, singles):
    ns = len(singles)

    def issue(refs, send_sems, recv_sems):
        a_ref, w_ref = refs[0], refs[1]
        x, y, c, chips = _place()
        mine = 2 * x + y
        for jj, (px, py) in enumerate(chips):
            to = dict(device_id=(px, py, c), device_id_type=MESH)
            for a in range(ns):
                pltpu.make_async_remote_copy(
                    src_ref=refs[2 + 2 * a], dst_ref=refs[3 + 2 * a].at[mine],
                    send_sem=send_sems.at[(1 + ns) * jj + 1 + a], recv_sem=recv_sems.at[(1 + ns) * jj + 1 + a],
                    **to).start()
        for s in range(4):
            for par in range(2):
                blocks = _phase_blocks(s, phase, True, par)
                if blocks:
                    @pl.when((mine == s) & (c == par))
                    def _():
                        for jj, (px, py) in enumerate(chips):
                            for b, blk in blocks:
                                pltpu.make_async_remote_copy(
                                    src_ref=_blk(a_ref, b), dst_ref=_blk(w_ref, blk),
                                    send_sem=send_sems.at[(1 + ns) * jj], recv_sem=recv_sems.at[(1 + ns) * jj],
                                    device_id=(px, py, c), device_id_type=MESH).start()

    bufs = [a_shard, w_grp] + [t for pair in singles for t in pair]
    return _split_start("gather_start_" + phase, issue, bufs, 3 * (1 + ns))


def _gather_wait(phase, send_sems, recv_sems, bufs, after):
    ns = (len(bufs) - 2) // 2

    def await_(refs, send_sems, recv_sems):
        a_ref, w_ref = refs[0], refs[1]
        x, y, c, chips = _place()
        mine = 2 * x + y
        for jj, (px, py) in enumerate(chips):
            to = dict(device_id=(px, py, c), device_id_type=MESH)
            peer = 2 * px + py
            for a in range(ns):
                cp = pltpu.make_async_remote_copy(
                    src_ref=refs[2 + 2 * a], dst_ref=refs[3 + 2 * a].at[mine],
                    send_sem=send_sems.at[(1 + ns) * jj + 1 + a], recv_sem=recv_sems.at[(1 + ns) * jj + 1 + a], **to)
                cp.wait_recv()
                cp.wait_send()
            for s in range(4):
                for par in range(2):
                    nblk = len(_phase_blocks(s, phase, True, par))
                    if nblk:
                        both = pltpu.make_async_remote_copy(
                            src_ref=_cols(a_ref, nblk), dst_ref=_cols(w_ref, nblk),
                            send_sem=send_sems.at[(1 + ns) * jj], recv_sem=recv_sems.at[(1 + ns) * jj], **to)

                        @pl.when((peer == s) & (c == par))
                        def _():
                            both.wait_recv()

                        @pl.when((mine == s) & (c == par))
                        def _():
                            both.wait_send()

    return _split_wait("gather_wait_" + phase, await_, send_sems, recv_sems, bufs, after)


def _sibling_forward_parts(phase):
    def each(w_ref, send_sems, recv_sems, start):
        x, y, c, chips = _place()
        to = dict(device_id=(x, y, 1 - c), device_id_type=MESH)
        for jj, (px, py) in enumerate(chips):
            peer = 2 * px + py
            for s in range(4):
                for par in range(2):
                    mine_blocks = _phase_blocks(s, phase, True, par)
                    theirs = len(_phase_blocks(s, phase, True, 1 - par))
                    if not (mine_blocks or theirs):
                        continue

                    @pl.when((peer == s) & (c == par))
                    def _():
                        if start:
                            for _, blk in mine_blocks:
                                pltpu.make_async_remote_copy(
                                    src_ref=_blk(w_ref, blk), dst_ref=_blk(w_ref, blk),
                                    send_sem=send_sems.at[jj], recv_sem=recv_sems.at[jj], **to).start()
                            return
                        if theirs:
                            pltpu.make_async_remote_copy(
                                src_ref=_cols(w_ref, theirs), dst_ref=_cols(w_ref, theirs),
                                send_sem=send_sems.at[jj], recv_sem=recv_sems.at[jj], **to).wait_recv()
                        if mine_blocks:
                            pltpu.make_async_remote_copy(
                                src_ref=_cols(w_ref, len(mine_blocks)), dst_ref=_cols(w_ref, len(mine_blocks)),
                                send_sem=send_sems.at[jj], recv_sem=recv_sems.at[jj], **to).wait_send()

    issue = lambda refs, send_sems, recv_sems: each(refs[0], send_sems, recv_sems, True)
    await_ = lambda refs, send_sems, recv_sems: each(refs[0], send_sems, recv_sems, False)
    return issue, await_


def _sibling_forward(phase, w_grp):
    issue, await_ = _sibling_forward_parts(phase)

    def body(w_in_ref, w_ref, send_sems, recv_sems):
        del w_in_ref
        issue([w_ref], send_sems, recv_sems)
        await_([w_ref], send_sems, recv_sems)

    return pl.pallas_call(
        body, name="sibling_forward_" + phase, in_specs=[ANY], out_specs=ANY,
        out_shape=jax.ShapeDtypeStruct(w_grp.shape, w_grp.dtype), input_output_aliases={0: 0},
        scratch_shapes=[pltpu.SemaphoreType.DMA((3,)), pltpu.SemaphoreType.DMA((3,))],
    )(w_grp)


def _merge_edges(w, edge0, mixed, name):
    d = w.shape[1]

    def body(e_ref, o_ref):
        o_ref[...] = e_ref[0:DH, :] + e_ref[DH:2 * DH, :]

    def to_block(i):
        r = mixed[-1]
        for kk in range(len(mixed) - 2, -1, -1):
            r = jnp.where(i == kk, mixed[kk], r)
        return r

    return pl.pallas_call(
        body, name=name, grid=(len(mixed),),
        in_specs=[pl.BlockSpec((2 * DH, d), lambda i: (edge0 // 2 + i, 0))],
        out_specs=pl.BlockSpec((DH, d), lambda i: (to_block(i), 0)),
        out_shape=jax.ShapeDtypeStruct(w.shape, w.dtype),
        input_output_aliases={0: 0},
        compiler_params=_params(("arbitrary",)),
    )(w)


def _scatter_start(phase, g_grp, land, singles, halved=False):
    ns = len(singles)

    def issue(refs, send_sems, recv_sems):
        g_ref, land_ref = refs[0], refs[1]
        x, y, c, chips = _place()
        for jj, (px, py) in enumerate(chips):
            to = dict(device_id=(px, py, c), device_id_type=MESH)
            peer = 2 * px + py
            for a in range(ns):
                pltpu.make_async_remote_copy(
                    src_ref=refs[2 + 2 * a].at[peer], dst_ref=refs[3 + 2 * a].at[jj],
                    send_sem=send_sems.at[(1 + ns) * jj + 1 + a], recv_sem=recv_sems.at[(1 + ns) * jj + 1 + a],
                    **to).start()
            for s in range(4):
                for par in ((0, 1) if halved else (None,)):
                    blocks = _phase_blocks(s, phase, False, par)
                    if blocks:
                        @pl.when((peer == s) if par is None else ((peer == s) & (c == par)))
                        def _():
                            for b, blk in blocks:
                                pltpu.make_async_remote_copy(
                                    src_ref=_blk(g_ref, blk), dst_ref=_blk(land_ref.at[jj], b),
                                    send_sem=send_sems.at[(1 + ns) * jj], recv_sem=recv_sems.at[(1 + ns) * jj],
                                    **to).start()

    bufs = [g_grp, land] + [t for pair in singles for t in pair]
    return _split_start("scatter_start_" + phase, issue, bufs, 3 * (1 + ns))


def _scatter_wait(phase, send_sems, recv_sems, bufs, after, halved=False):
    ns = (len(bufs) - 2) // 2

    def await_(refs, send_sems, recv_sems):
        g_ref, land_ref = refs[0], refs[1]
        x, y, c, chips = _place()
        mine = 2 * x + y
        for jj, (px, py) in enumerate(chips):
            to = dict(device_id=(px, py, c), device_id_type=MESH)
            peer = 2 * px + py
            for a in range(ns):
                cp = pltpu.make_async_remote_copy(
                    src_ref=refs[2 + 2 * a].at[peer], dst_ref=refs[3 + 2 * a].at[jj],
                    send_sem=send_sems.at[(1 + ns) * jj + 1 + a], recv_sem=recv_sems.at[(1 + ns) * jj + 1 + a], **to)
                cp.wait_recv()
                cp.wait_send()
            for s in range(4):
                for par in ((0, 1) if halved else (None,)):
                    nblk = len(_phase_blocks(s, phase, False, par))
                    if nblk:
                        both = pltpu.make_async_remote_copy(
                            src_ref=_cols(g_ref, nblk), dst_ref=_cols(land_ref.at[jj], nblk),
                            send_sem=send_sems.at[(1 + ns) * jj], recv_sem=recv_sems.at[(1 + ns) * jj], **to)

                        @pl.when((mine == s) if par is None else ((mine == s) & (c == par)))
                        def _():
                            both.wait_recv()

                        @pl.when((peer == s) if par is None else ((peer == s) & (c == par)))
                        def _():
                            both.wait_send()

    return _split_wait("scatter_wait_" + phase, await_, send_sems, recv_sems, bufs, after)


def _needed_blocks(phase, parity):
    return sorted({blk for s in range(4) for _, blk in _phase_blocks(s, phase, False, parity)})


def _pair_reduce(phase, g_grp):
    n, d = g_grp.shape

    def swap(g_ref, sib_ref, send_sem, recv_sem):
        x, y, c, _ = _place()
        to = dict(device_id=(x, y, 1 - c), device_id_type=MESH)
        for par in range(2):
            give, get = _needed_blocks(phase, 1 - par), _needed_blocks(phase, par)

            @pl.when(c == par)
            def _():
                for blk in give:
                    pltpu.make_async_remote_copy(src_ref=_blk(g_ref, blk), dst_ref=_blk(sib_ref, blk),
                                                 send_sem=send_sem, recv_sem=recv_sem, **to).start()
                pltpu.make_async_remote_copy(src_ref=_cols(g_ref, len(get)), dst_ref=_cols(sib_ref, len(get)),
                                             send_sem=send_sem, recv_sem=recv_sem, **to).wait_recv()
                pltpu.make_async_remote_copy(src_ref=_cols(g_ref, len(give)), dst_ref=_cols(sib_ref, len(give)),
                                             send_sem=send_sem, recv_sem=recv_sem, **to).wait_send()

    sib = pl.pallas_call(
        swap, name="pair_swap_" + phase, in_specs=[ANY], out_specs=ANY,
        out_shape=jax.ShapeDtypeStruct((n, d), g_grp.dtype),
        scratch_shapes=[pltpu.SemaphoreType.DMA, pltpu.SemaphoreType.DMA],
    )(*_in_hbm(g_grp))

    lists = [_needed_blocks(phase, par) for par in range(2)]
    longest = max(len(t) for t in lists)
    table = jnp.asarray([t + [t[-1]] * (longest - len(t)) for t in lists], jnp.int32)[lax.axis_index("c")]

    def add(t_ref, a_ref, b_ref, o_ref):
        o_ref[...] = (a_ref[...].astype(F32) + b_ref[...].astype(F32)).astype(o_ref.dtype)

    blk = pl.BlockSpec((DH, d), lambda i, t: (t[i], 0))
    return pl.pallas_call(
        add, name="pair_add_" + phase,
        grid_spec=pltpu.PrefetchScalarGridSpec(num_scalar_prefetch=1, grid=(longest,),
                                               in_specs=[blk, blk], out_specs=blk),
        out_shape=jax.ShapeDtypeStruct((n, d), g_grp.dtype),
        compiler_params=_params(("arbitrary",)),
    )(table, g_grp, sib)


def _sum_shard(g_g, g_c, land):
    d = g_g.shape[1]
    chip = 2 * lax.axis_index("x") + lax.axis_index("y")

    def body(t_ref, gg_ref, gc_ref, land_ref, o_ref):
        b = pl.program_id(0)
        in_g = t_ref[2, b] == 1
        own = jnp.where(in_g, gg_ref[...].astype(F32), gc_ref[...].astype(F32))
        for jj in range(3):
            own = own + land_ref[jj].astype(F32)
        o_ref[...] = jnp.where(in_g & (b % 2 != lax.axis_index("c")), 0.0, own)

    return pl.pallas_call(
        body, name="sum_w_in",
        grid_spec=pltpu.PrefetchScalarGridSpec(
            num_scalar_prefetch=1, grid=(ALIGNED_BLOCKS,),
            in_specs=[pl.BlockSpec((DH, d), lambda b, t: (t[0, b], 0)), pl.BlockSpec((DH, d), lambda b, t: (t[1, b], 0)),
                      pl.BlockSpec((3, DH, d), lambda b, t: (0, b, 0))],
            out_specs=pl.BlockSpec((DH, d), lambda b, t: (b, 0))),
        out_shape=jax.ShapeDtypeStruct((ALIGNED_W, d), F32),
        compiler_params=_params(("arbitrary",)),
    )(_block_table(chip, False, 0, 0), g_g, g_c, land)


def _sum_rows(stack, land, rows):
    _, r, d = stack.shape
    rows = min(rows, r)
    chip = 2 * lax.axis_index("x") + lax.axis_index("y")

    def body(t_ref, own_ref, land_ref, o_ref):
        acc = own_ref[0].astype(F32)
        for jj in range(3):
            acc = acc + land_ref[jj].astype(F32)
        o_ref[...] = acc

    return pl.pallas_call(
        body, name="sum_w_out",
        grid_spec=pltpu.PrefetchScalarGridSpec(
            num_scalar_prefetch=1, grid=(r // rows,),
            in_specs=[pl.BlockSpec((1, rows, d), lambda i, t: (t[0], i, 0)),
                      pl.BlockSpec((3, rows, d), lambda i, t: (0, i, 0))],
            out_specs=pl.BlockSpec((rows, d), lambda i, t: (i, 0))),
        out_shape=jax.ShapeDtypeStruct((r, d), F32),
        compiler_params=_params(("arbitrary",)),
    )(jnp.reshape(chip, (1,)).astype(jnp.int32), stack, land)


def _exchange_parts(n_swap, with_pack):
    def copies(refs, send_sems, recv_sems):
        x, y, c, _ = _place()
        me = 4 * x + 2 * y + c
        cps = [pltpu.make_async_remote_copy(
            src_ref=refs[2 * a], dst_ref=refs[2 * a + 1], send_sem=send_sems.at[a], recv_sem=recv_sems.at[a],
            device_id=(x, y, 1 - c), device_id_type=MESH) for a in range(n_swap)]
        if with_pack:
            pack_ref, packs = refs[2 * n_swap], refs[2 * n_swap + 1]
            for r in range(1, 8):
                dx, dy, dc = (r >> 2) & 1, (r >> 1) & 1, r & 1
                peer = (x + dx - 2 * x * dx, y + dy - 2 * y * dy, c + dc - 2 * c * dc)
                cps.append(pltpu.make_async_remote_copy(
                    src_ref=pack_ref, dst_ref=packs.at[me], send_sem=send_sems.at[n_swap + r - 1],
                    recv_sem=recv_sems.at[n_swap + r - 1], device_id=peer, device_id_type=MESH))
        return cps

    def issue(refs, send_sems, recv_sems):
        for cp in copies(refs, send_sems, recv_sems):
            cp.start()

    def await_(refs, send_sems, recv_sems):
        cps = copies(refs, send_sems, recv_sems)
        for cp in cps:
            cp.wait_recv()
        for cp in cps:
            cp.wait_send()

    return issue, await_, n_swap + (7 if with_pack else 0)


def _sum_packs(pack, packs):
    x, y, c = lax.axis_index("x"), lax.axis_index("y"), lax.axis_index("c")
    me = jnp.reshape(4 * x + 2 * y + c, (1,)).astype(jnp.int32)

    def body(me_ref, own_ref, p_ref, o_ref):
        acc = jnp.where(me_ref[0] == 0, own_ref[...], p_ref[0])
        for d in range(1, 8):
            acc = acc + jnp.where(me_ref[0] == d, own_ref[...], p_ref[d])
        o_ref[...] = acc

    full = lambda s: pl.BlockSpec(s.shape, lambda i, t: (0,) * s.ndim)
    return pl.pallas_call(
        body, name="sum_packs",
        grid_spec=pltpu.PrefetchScalarGridSpec(num_scalar_prefetch=1, grid=(1,), in_specs=[full(pack), full(packs)],
                                               out_specs=full(pack)),
        out_shape=jax.ShapeDtypeStruct(pack.shape, F32),
    )(me, pack, packs)


def _adamw_update(g, w_ref, m_ref, v_ref, go, do, mo, vo):
    c1 = 1.0 / (1.0 - ADAM_B1 ** ADAM_STEP)
    c2 = 1.0 / (1.0 - ADAM_B2 ** ADAM_STEP)
    mn = ADAM_B1 * m_ref[...] + (1.0 - ADAM_B1) * g
    vn = ADAM_B2 * v_ref[...] + (1.0 - ADAM_B2) * (g * g)
    go[...] = g
    mo[...] = mn
    vo[...] = vn
    do[...] = -ADAM_LR * ((mn * c1) / (jnp.sqrt(vn * c2) + ADAM_EPS) + ADAM_WD * w_ref[...])


def _adamw(w, m, v, g1, g2, rows, name):
    r, cdim = w.shape
    rows = min(rows, r)

    def body(*refs):
        n_in = 4 if g2 is None else 5
        w_ref, m_ref, v_ref, g_ref = refs[:4]
        g = g_ref[...] if g2 is None else g_ref[...] + refs[4][...]
        _adamw_update(g, w_ref, m_ref, v_ref, *refs[n_in:n_in + 4])

    blk = pl.BlockSpec((rows, cdim), lambda i: (i, 0))
    args = [w, m, v, g1] + ([] if g2 is None else [g2])
    shp = jax.ShapeDtypeStruct((r, cdim), F32)
    return pl.pallas_call(
        body, name=name, grid=(r // rows,),
        in_specs=[blk] * len(args), out_specs=[blk] * 4, out_shape=[shp] * 4,
        compiler_params=_params(("parallel",), 20 * rows * cdim * 4 + 8 * 2**20),
    )(*_in_hbm(*args))


def _adamw_shard(wt, mt, vt, g1, g2):
    r, d = wt.shape
    cols = min(256, d)

    def body(w_ref, m_ref, v_ref, g_ref, g2_ref, go, do, mo, vo, pad_ref):
        chip = 2 * lax.axis_index("x") + lax.axis_index("y")
        back = [(ALIGNED_W - s) % ALIGNED_W for s in SHIFTS]
        pad_ref[...] = pltpu.roll(g_ref[...] + g2_ref[...], _by_chip(chip, back), 0)
        outs = [o.at[:, 0, :] for o in (go, do, mo, vo)]
        _adamw_update(pad_ref[0:r, :], w_ref, m_ref, v_ref, *outs)

    blk = pl.BlockSpec((r, cols), lambda i: (0, i))
    gblk = pl.BlockSpec((ALIGNED_W, cols), lambda i: (0, i))
    oblk = pl.BlockSpec((r, 1, cols), lambda i: (0, 0, i))
    shp = jax.ShapeDtypeStruct((r, 1, d), F32)
    return pl.pallas_call(
        body, name="adamw_w_in", grid=(d // cols,),
        in_specs=[blk] * 3 + [gblk] * 2, out_specs=[oblk] * 4, out_shape=[shp] * 4,
        scratch_shapes=[pltpu.VMEM((ALIGNED_W, cols), F32)],
        compiler_params=_params(("parallel",), 24 * ALIGNED_W * cols * 4 + 8 * 2**20),
    )(wt, mt, vt, g1, g2)


def _pad_lanes(a, width):
    return jnp.pad(a, ((0, 0), (0, width - a.shape[1])))


def _gathered_to_full(g):
    return jnp.transpose(g, (1, 0, 2)).reshape(g.shape[1], 4 * g.shape[2])


def _row(a):
    return _pad_lanes(a.reshape(1, -1), 1024)


def _small_pack(nin, cb, fn, al, dt, gn, cqw_shard, cw_shard):
    ad = jnp.concatenate([al.reshape(1, -1), dt.reshape(1, -1)], axis=1)
    rows = [_row(nin), _row(cb), _row(fn), _row(ad), _row(gn), cqw_shard.reshape(3, 1024), _row(cw_shard)]
    out = jnp.concatenate(rows, axis=0)
    return jnp.pad(out, ((0, 16 - out.shape[0]), (0, 0)))


def kernel(x, norm_in_w, w_in, conv_qkv_w, A_log, dt_bias, gdn_norm_w, conv_w, conv_b, w_out, final_norm_w, loss_target, m_norm_in_w, m_w_in, m_conv_qkv_w, m_A_log, m_dt_bias, m_gdn_norm_w, m_conv_w, m_conv_b, m_w_out, m_final_norm_w, v_norm_in_w, v_w_in, v_conv_qkv_w, v_A_log, v_dt_bias, v_gdn_norm_w, v_conv_w, v_conv_b, v_w_out, v_final_norm_w):
    chip = 2 * lax.axis_index("x") + lax.axis_index("y")
    a_shard = _align_shard(jnp.transpose(w_in, (2, 0, 1)))
    wo_b = _cast_bf16(w_out[0], 256, "cast_w_out")
    d_model = x.shape[-1]
    stack = lambda s: lax.empty((4,) + s.shape, s.dtype)
    wg0 = lax.empty((WG_BLOCKS * DH, d_model), BF16)
    wc0 = lax.empty((WC_BLOCKS * DH, d_model), BF16)
    ss_g, rs_g, bufs_g, tok_g = _gather_start("g", a_shard, wg0, [(conv_qkv_w[0], stack(conv_qkv_w[0]))])
    ss_c, rs_c, bufs_c, tok_c = _gather_start("c", bufs_g[0], wc0,
                                              [(conv_w[0], stack(conv_w[0])), (wo_b, stack(wo_b))])
    wg1, wc1, wog1, cqg1, cwg1 = _place_own(bufs_c[0], bufs_c[4], bufs_g[2], bufs_c[2],
                                            [bufs_g[1], bufs_c[1], bufs_c[5], bufs_g[3], bufs_c[3]])
    x0 = x[0]
    h = _rms_in(x0, _tie(_tie(norm_in_w, tok_g, "after_gather_start_g"), tok_c, "after_gather_start_c"))
    adam_in = [jnp.transpose(a[0]) for a in (w_in, m_w_in, v_w_in)]
    sp = lambda nin, cb, fn, al, dt, gn, cq, cwv: _small_pack(nin, cb, fn, al, dt, gn, cq[0], cwv[0])
    w_s = sp(norm_in_w, conv_b, final_norm_w, A_log, dt_bias, gdn_norm_w, conv_qkv_w, conv_w)
    m_s = sp(m_norm_in_w, m_conv_b, m_final_norm_w, m_A_log, m_dt_bias, m_gdn_norm_w, m_conv_qkv_w, m_conv_w)
    v_s = sp(v_norm_in_w, v_conv_b, v_final_norm_w, v_A_log, v_dt_bias, v_gdn_norm_w, v_conv_qkv_w, v_conv_w)
    a_thru, wg, _, cq_g = _gather_wait("g", ss_g, rs_g, [bufs_c[0], wg1, bufs_g[2], cqg1],
                                       [h, w_s, m_s, v_s] + adam_in[1:])
    w_g = _merge_edges(_sibling_forward("g", wg), G_EDGE, G_MIXED, "merge_edges_g")
    cqw = _gathered_to_full(cq_g)
    ad = jnp.pad(jnp.concatenate([A_log, dt_bias], axis=0), ((0, 0), (A_LANE, 0)))
    fwd_c = {}

    def on_q(q):
        _, wc, _, cw_g, _, wo_g = _gather_wait("c", ss_c, rs_c,
                                               [a_thru, wc1, bufs_c[2], cwg1, bufs_c[4], wog1], q)
        issue, _ = _sibling_forward_parts("c")
        ss, rs, (wc,), tok = _split_start("sibling_forward_start_c", issue, [wc], 3)
        fwd_c.update(ss=ss, rs=rs, wc=wc, cw_g=cw_g, wo_g=wo_g)
        return _tie(q, tok, "after_sibling_forward_start_c")

    def late(o):
        _, await_ = _sibling_forward_parts("c")
        (wc,) = _split_wait("sibling_forward_wait_c", await_, fwd_c["ss"], fwd_c["rs"], [fwd_c["wc"]], o)
        return (_merge_edges(wc, C_EDGE, C_MIXED, "merge_edges_c"), fwd_c["wo_g"].reshape(2 * GW, d_model),
                _gathered_to_full(fwd_c["cw_g"]))

    scat = {}

    def on_grad_c(g_c, g_wout, do):
        go4 = g_wout.reshape(4, GW // 2, d_model)
        land = lax.empty((3, ALIGNED_W, d_model), BF16)
        land_o = lax.empty((3, GW // 2, d_model), BF16)
        ss, rs, bufs, tok = _scatter_start("c", g_c, land, [(go4, land_o)])
        scat["c"] = (ss, rs, bufs)
        return _tie(do, tok, "after_scatter_start_c")

    def on_grad_g(g_g, dproj_g):
        ss, rs, bufs, tok = _scatter_start("g", _pair_reduce("g", g_g), scat["c"][2][1], [], halved=True)
        scat["g"] = (ss, rs, bufs)
        return _tie(dproj_g, tok, "after_scatter_start_g")

    gx, sm, _ = _local_step(x0, loss_target[0], h, w_g, cqw, late, norm_in_w, ad, gdn_norm_w, conv_b,
                            final_norm_w.reshape(1, -1), on_grad_c, on_grad_g, on_q)

    ss, rs, bufs = scat["c"]
    g_c, land, go4, land_o = _scatter_wait("c", ss, rs, [bufs[0], scat["g"][2][1], bufs[2], bufs[3]], gx)
    part_out = _sum_rows(go4, land_o, 128)
    ad_g = jnp.concatenate([sm["al"][:, A_LANE:], sm["dt"][:, A_LANE:]], axis=1)
    pack = jnp.concatenate([_row(sm["nin"]), _row(sm["cb"]), _row(sm["fn"]), _row(ad_g), _row(sm["gn"]),
                            jnp.concatenate(sm["cq"], axis=1).reshape(12, 1024), sm["cw"], _row(sm["loss"])], axis=0)
    pack = jnp.pad(pack, ((0, PACK_ROWS - pack.shape[0]), (0, 0)))
    issue, await_a, nsem = _exchange_parts(1, True)
    ss_a, rs_a, bufs_a, tok_a = _split_start(
        "exchange_start_small", issue,
        [part_out, lax.empty(part_out.shape, F32), pack, lax.empty((8,) + pack.shape, F32)], nsem)
    ss, rs, bufs = scat["g"]
    g_g, land = _scatter_wait("g", ss, rs, [bufs[0], land], [gx, tok_a], halved=True)
    part_in = _sum_shard(g_g, g_c, land)
    issue, await_b, nsem = _exchange_parts(1, False)
    ss_b, rs_b, bufs_b, tok_b = _split_start("exchange_start_w_in", issue,
                                             [part_in, lax.empty(part_in.shape, F32)], nsem)
    part_out, sib_out, pack, packs = _split_wait("exchange_wait_small", await_a, ss_a, rs_a, bufs_a, tok_b)
    tot = _sum_packs(pack, packs)
    g_wo, d_wo, m_wo, v_wo = _adamw(w_out[0], m_w_out[0], v_w_out[0], part_out, sib_out, 128, "adamw_w_out")
    g_cq_sh = lax.dynamic_slice_in_dim(tot[R_CQ:R_CQ + 12].reshape(4, 3 * GW), chip * 768, 768, axis=1)
    g_cw_sh = lax.dynamic_slice_in_dim(tot[R_CW:R_CW + 3], chip * 256, 256, axis=1)
    g_s = _small_pack(tot[R_NIN], tot[R_CB], tot[R_FN], tot[R_AD, :HEADS], tot[R_AD, HEADS:2 * HEADS],
                      tot[R_GN, :DH], g_cq_sh, g_cw_sh)
    small = _adamw(w_s, m_s, v_s, g_s, None, 16, "adamw_small")
    part_in, sib_in = _split_wait("exchange_wait_w_in", await_b, ss_b, rs_b, bufs_b, [small[0], d_wo])
    g_wi, d_wi, m_wi, v_wi = [jnp.transpose(a, (1, 2, 0))[0] for a in _adamw_shard(*adam_in, part_in, sib_in)]

    def unpack(a, big_in, big_out):
        return (a[0:1], big_in[None], a[5:8].reshape(1, 4, 768), a[3:4, :HEADS], a[3:4, HEADS:2 * HEADS],
                a[4:5, :DH], a[8, :768].reshape(1, 3, 256), a[1:2], big_out[None], a[2])

    loss = tot[R_LOSS, 0]
    return (loss, gx[None], *unpack(small[0], g_wi, g_wo), *unpack(small[1], d_wi, d_wo),
            *unpack(small[2], m_wi, m_wo), *unpack(small[3], v_wi, v_wo))
```

```python
import jax
import jax.numpy as jnp
from jax import lax
from jax.experimental import pallas as pl
from jax.experimental.pallas import tpu as pltpu

F32 = jnp.float32
BF16 = jnp.bfloat16
MESH = pl.DeviceIdType.MESH
ANY = pl.BlockSpec(memory_space=pl.ANY)

HEADS = 8
DH = 128
CH = 64
GW = HEADS * DH
EPS = 1e-6
VMEM_V7X = 64 * 1024 * 1024

QB, KB, VB, ZB, BAB = 0, 8, 16, 24, 32
A_LANE = 120
NG, NC = 33, 32
GW_COLS, CW_COLS = NG * DH, NC * DH

SHARD_W = 2052
ALIGNED_BLOCKS = 17
ALIGNED_W = ALIGNED_BLOCKS * DH
SHIFTS = (0, 4, ALIGNED_W - 8, ALIGNED_W - 4)
G_EDGE, C_EDGE = 34, 32
G_SPARE, C_SPARE = 33, 34
WG_BLOCKS, WC_BLOCKS = 38, 36
G_MIXED, C_MIXED = (2, BAB), (4 * 7 + 1,)


def _shard_blocks(chip, edges):
    g, c = "g", "c"
    if chip == 0:
        out = [(g, 3 * b) for b in range(8)] + [(g, 3 * b + 1) for b in range(8)] + [(g, G_EDGE, G_MIXED[0])]
    elif chip == 1:
        out = [(g, G_EDGE + 1, G_MIXED[0])] + [(g, 3 * b + 2) for b in range(1, 8)]
        out += [(g, ZB + b) for b in range(8)] + [(g, G_EDGE + 2, G_MIXED[1])]
    elif chip == 2:
        out = [(c, 4 * b) for b in range(8)] + [(c, 4 * b + 1) for b in range(7)]
        out += [(c, C_EDGE, C_MIXED[0]), (g, G_EDGE + 3, G_MIXED[1])]
    else:
        out = [(c, 4 * b + 2) for b in range(8)] + [(c, 4 * b + 3) for b in range(8)] + [(c, C_EDGE + 1, C_MIXED[0])]
    return [(o[0], o[1] if (edges or len(o) == 2) else o[2]) for o in out]


def _by_chip(chip, vals):
    if all(v == vals[0] for v in vals):
        return vals[0]
    r = vals[3]
    for kk in (2, 1, 0):
        r = jnp.where(chip == kk, vals[kk], r)
    return r

ADAM_LR, ADAM_B1, ADAM_B2, ADAM_EPS, ADAM_WD, ADAM_STEP = 0.001, 0.9, 0.999, 1e-08, 0.01, 10

R_NIN, R_CB, R_FN, R_AD, R_GN, R_CQ, R_CW, R_LOSS, PACK_ROWS = 0, 1, 2, 3, 4, 5, 17, 20, 24

NN = ((1,), (0,))
NT = ((1,), (1,))
TN = ((0,), (0,))


def _dot(a, b, dims=NN, mode="lo"):
    dn = (dims, ((), ()))
    if mode == "hi":
        return lax.dot_general(a, b, dn, precision=lax.Precision.HIGHEST, preferred_element_type=F32)
    ah, bh = a.astype(BF16), b.astype(BF16)
    out = lax.dot_general(ah, bh, dn, preferred_element_type=F32)
    if mode == "x3":
        al = (a - ah.astype(F32)).astype(BF16)
        bl = (b - bh.astype(F32)).astype(BF16)
        out = out + lax.dot_general(ah, bl, dn, preferred_element_type=F32)
        out = out + lax.dot_general(al, bh, dn, preferred_element_type=F32)
    return out


P_GRAM, P_INV, P_SOL, P_SCAN, P_SCANB, P_BWD = "lo", "lo", "lo", "lo", "lo", "lo"
P_CUM = "x3"


def _params(sem=None, vmem=None):
    kw = {}
    if sem is not None:
        kw["dimension_semantics"] = sem
    if vmem is not None:
        kw["vmem_limit_bytes"] = int(min(max(vmem, 32 * 2**20), VMEM_V7X - 8 * 2**20))
    return pltpu.CompilerParams(**kw)


def _in_hbm(*arrays):
    return [pltpu.with_memory_space_constraint(a, pltpu.HBM) for a in arrays]


def _sigmoid(x):
    return 1.0 / (1.0 + jnp.exp(-x))


def _dsilu(x, s):
    return s * (1.0 + x * (1.0 - s))


def _rows(shape):
    return lax.broadcasted_iota(jnp.int32, shape, 0)


def _shift_down(x, s):
    if s == 0:
        return x
    return jnp.where(_rows(x.shape) >= s, pltpu.roll(x, s, 0), 0.0)


def _shift_up(x, s):
    if s == 0:
        return x
    n = x.shape[0]
    return jnp.where(_rows(x.shape) < n - s, pltpu.roll(x, n - s, 0), 0.0)


def _matmul(a, b, dims, out_dtype, tm, tn, tk, name, add=None, n=None, b_outer=False):
    if dims == NN:
        (m, k), n = a.shape, b.shape[1]
    elif dims == NT:
        (m, k), n = a.shape, (n or b.shape[0])
    else:
        (k, m), n = a.shape, b.shape[1]
    tm, tn, tk = min(tm, m), min(tn, n), min(tk, k)
    assert m % tm == 0 and n % tn == 0 and k % tk == 0, (name, m, n, k, tm, tn, tk)
    nk = k // tk

    def body(*refs):
        if add is None:
            a_ref, b_ref, o_ref = refs[:3]
            add_ref = None
        else:
            a_ref, b_ref, add_ref, o_ref = refs[:4]
        part = _dot(a_ref[...], b_ref[...], dims)
        if nk == 1:
            if add_ref is not None:
                part = part + add_ref[...]
            o_ref[...] = part.astype(out_dtype)
            return
        acc = refs[-1]
        kk = pl.program_id(2)

        @pl.when(kk == 0)
        def _():
            acc[...] = part

        @pl.when(kk > 0)
        def _():
            acc[...] += part

        @pl.when(kk == nk - 1)
        def _():
            r = acc[...]
            if add_ref is not None:
                r = r + add_ref[...]
            o_ref[...] = r.astype(out_dtype)

    ij = (lambda g0, g1: (g1, g0)) if b_outer else (lambda g0, g1: (g0, g1))

    def spec(shape, pick):
        return pl.BlockSpec(shape, lambda g0, g1, kk: pick(*ij(g0, g1), kk))

    a_spec = spec((tk, tm), lambda i, j, kk: (kk, i)) if dims == TN else spec((tm, tk), lambda i, j, kk: (i, kk))
    b_spec = spec((tn, tk), lambda i, j, kk: (j, kk)) if dims == NT else spec((tk, tn), lambda i, j, kk: (kk, j))
    o_spec = spec((tm, tn), lambda i, j, kk: (i, j))
    in_specs = [a_spec, b_spec]
    args = [a, b]
    if add is not None:
        in_specs.append(o_spec)
        args.append(add)
    osz = jnp.dtype(out_dtype).itemsize
    est = 2 * (tm * tk * a.dtype.itemsize + tk * tn * b.dtype.itemsize + tm * tn * osz)
    est += 3 * tm * tn * 4 + (2 * tm * tn * 4 if add is not None else 0)
    return pl.pallas_call(
        body, name=name, grid=(n // tn, m // tm, nk) if b_outer else (m // tm, n // tn, nk),
        in_specs=in_specs, out_specs=o_spec,
        out_shape=jax.ShapeDtypeStruct((m, n), out_dtype),
        scratch_shapes=[pltpu.VMEM((tm, tn), F32)] if nk > 1 else [],
        compiler_params=_params(("parallel", "parallel", "arbitrary"), est + 8 * 2**20),
    )(*args)


def _cast_bf16(a, rows, name, after):
    r, c = a.shape
    rows = min(rows, r)

    def body(a_ref, t_ref, o_ref):
        del t_ref
        o_ref[...] = a_ref[...].astype(BF16)

    return pl.pallas_call(
        body, name=name, grid=(r // rows,),
        in_specs=[pl.BlockSpec((rows, c), lambda i: (i, 0)), ANY],
        out_specs=pl.BlockSpec((rows, c), lambda i: (i, 0)),
        out_shape=jax.ShapeDtypeStruct((r, c), BF16),
        compiler_params=_params(("parallel",)),
    )(a, after)


def _align_shard(wt):
    r, _, d = wt.shape
    cols = min(256, d)

    def body(w_ref, o_ref, pad_ref):
        chip = 2 * lax.axis_index("x") + lax.axis_index("y")
        pad_ref[...] = jnp.zeros_like(pad_ref)
        pad_ref[0:r, :] = w_ref[...]
        o_ref[...] = pltpu.roll(pad_ref[...], _by_chip(chip, SHIFTS), 0).astype(BF16)

    return pl.pallas_call(
        body, name="align_shard", grid=(d // cols,),
        in_specs=[pl.BlockSpec((r, None, cols), lambda i: (0, 0, i))],
        out_specs=pl.BlockSpec((ALIGNED_W, cols), lambda i: (0, i)),
        out_shape=jax.ShapeDtypeStruct((ALIGNED_W, d), BF16),
        scratch_shapes=[pltpu.VMEM((ALIGNED_W, cols), F32)],
        compiler_params=_params(("parallel",)),
    )(wt)


def _rms_in(x, w):
    n, d = x.shape
    tr = min(256, n)

    def body(x_ref, w_ref, h_ref):
        xv = x_ref[...]
        r = lax.rsqrt(jnp.mean(xv * xv, axis=-1, keepdims=True) + EPS)
        h_ref[...] = (xv * r * w_ref[...]).astype(BF16)

    return pl.pallas_call(
        body, name="rms_in", grid=(n // tr,),
        in_specs=[pl.BlockSpec((tr, d), lambda i: (i, 0)), pl.BlockSpec((1, d), lambda i: (0, 0))],
        out_specs=pl.BlockSpec((tr, d), lambda i: (i, 0)),
        out_shape=jax.ShapeDtypeStruct((n, d), BF16),
        compiler_params=_params(("parallel",)),
    )(x, w)


def _conv_silu(p, w_ref, taps):
    c = None
    for j in range(taps):
        t = _shift_down(p, taps - 1 - j) * w_ref[j:j + 1, :]
        c = t if c is None else c + t
    return c


def _prep_qkv(proj, cw):
    n = proj.shape[0]

    def body(p3, wq, wk, wv, q_ref, k_ref, v_ref):
        for kind, (w_ref, o_ref) in enumerate(((wq, q_ref), (wk, k_ref), (wv, v_ref))):
            c = _conv_silu(p3[:, kind * DH:(kind + 1) * DH], w_ref, 4)
            a = c * _sigmoid(c)
            if kind < 2:
                r = lax.rsqrt(jnp.sum(a * a, axis=-1, keepdims=True) + EPS)
                a = a * (r * (DH ** -0.5 if kind == 0 else 1.0))
            o_ref[...] = a

    col = pl.BlockSpec((n, DH), lambda h: (0, h))
    wcol = lambda base: pl.BlockSpec((4, DH), lambda h: (0, base + h))
    out = jax.ShapeDtypeStruct((n, GW), F32)
    return pl.pallas_call(
        body, name="prep_qkv", grid=(HEADS,),
        in_specs=[pl.BlockSpec((n, 3 * DH), lambda h: (0, h)), wcol(QB), wcol(KB), wcol(VB)],
        out_specs=[col] * 3, out_shape=[out] * 3,
        compiler_params=_params(("parallel",), 40 * 2**20),
    )(proj, cw, cw, cw)


def _prep_qkv_bwd(proj, cw, dq, dk, dv, dproj):
    n = proj.shape[0]

    def body(p3, wq, wk, wv, dq_ref, dk_ref, dv_ref, _, o3, gq, gk, gv):
        for kind, (w_ref, d_ref, g_ref) in enumerate(((wq, dq_ref, gq), (wk, dk_ref, gk), (wv, dv_ref, gv))):
            p = p3[:, kind * DH:(kind + 1) * DH]
            shifted = [_shift_down(p, 3 - j) for j in range(4)]
            c = shifted[0] * w_ref[0:1, :]
            for j in range(1, 4):
                c = c + shifted[j] * w_ref[j:j + 1, :]
            s = _sigmoid(c)
            a = c * s
            d = d_ref[...]
            if kind < 2:
                r = lax.rsqrt(jnp.sum(a * a, axis=-1, keepdims=True) + EPS)
                sc = DH ** -0.5 if kind == 0 else 1.0
                d = (sc * r) * (d - a * ((r * r) * jnp.sum(d * a, axis=-1, keepdims=True)))
            dc = d * _dsilu(c, s)
            dp = None
            for j in range(4):
                g_ref[j:j + 1, :] = jnp.sum(dc * shifted[j], axis=0, keepdims=True)
                t = _shift_up(dc, 3 - j) * w_ref[j:j + 1, :]
                dp = t if dp is None else dp + t
            o3[:, kind * DH:(kind + 1) * DH] = dp.astype(BF16)

    col = pl.BlockSpec((n, DH), lambda h: (0, h))
    wcol = lambda base: pl.BlockSpec((4, DH), lambda h: (0, base + h))
    p3spec = pl.BlockSpec((n, 3 * DH), lambda h: (0, h))
    return pl.pallas_call(
        body, name="prep_qkv_bwd", grid=(HEADS,),
        in_specs=[p3spec, wcol(QB), wcol(KB), wcol(VB), col, col, col, ANY],
        out_specs=[p3spec] + [wcol(0)] * 3,
        out_shape=[jax.ShapeDtypeStruct(dproj.shape, BF16)] + [jax.ShapeDtypeStruct((4, GW), F32)] * 3,
        input_output_aliases={7: 0},
        compiler_params=_params(("parallel",), 48 * 2**20),
    )(proj, cw, cw, cw, dq, dk, dv, dproj)


CPB = 8
SCAN_CPS = 4


def _tri(lower, rows):
    i = lax.broadcasted_iota(jnp.int32, (rows, rows), 0)
    j = lax.broadcasted_iota(jnp.int32, (rows, rows), 1)
    return jnp.where((i // CH == j // CH) & ((i >= j) if lower else (j >= i)), 1.0, 0.0)


def _lane(shape):
    return lax.broadcasted_iota(jnp.int32, shape, 1)


def _prep_bg(proj, ad):
    n = proj.shape[0]
    nch = n // CH
    cpb = CPB if nch % CPB == 0 else 1
    rows = cpb * CH

    def body(p_ref, ad_ref, bg_ref, bgt_ref):
        p = p_ref[...]
        lane = _lane(p.shape)
        beta = _sigmoid(p)
        xa = p + ad_ref[1:2, :]
        sp = jnp.maximum(xa, 0.0) + jnp.log(1.0 + jnp.exp(-jnp.abs(xa)))
        g = pltpu.roll(-jnp.exp(ad_ref[0:1, :]) * sp, DH - A_LANE + HEADS, 1)
        gc = _dot(_tri(True, rows), g, NN, P_CUM)
        bg = jnp.where(lane < HEADS, beta, jnp.where(lane < 2 * HEADS, gc, 0.0))
        bg_ref[...] = bg
        for ci in range(cpb):
            bgt_ref[ci] = bg[ci * CH:(ci + 1) * CH, :].T

    return pl.pallas_call(
        body, name="prep_bg", grid=(nch // cpb,),
        in_specs=[pl.BlockSpec((rows, DH), lambda i: (i, BAB)), pl.BlockSpec((2, DH), lambda i: (0, 0))],
        out_specs=[pl.BlockSpec((rows, DH), lambda i: (i, 0)), pl.BlockSpec((cpb, DH, CH), lambda i: (i, 0, 0))],
        out_shape=[jax.ShapeDtypeStruct((n, DH), F32), jax.ShapeDtypeStruct((nch, DH, CH), F32)],
        compiler_params=_params(("parallel",)),
    )(*_in_hbm(proj, ad))


def _prep_bg_bwd(proj, ad, dbg, dproj):
    n = proj.shape[0]
    nch = n // CH
    cpb = CPB if nch % CPB == 0 else 1
    rows = cpb * CH

    def body(p_ref, ad_ref, d_ref, _, o_ref, ga_ref, gd_ref):
        p = p_ref[...]
        d = d_ref[...]
        lane = _lane(p.shape)
        beta = _sigmoid(p)
        xa = p + ad_ref[1:2, :]
        sp = jnp.maximum(xa, 0.0) + jnp.log(1.0 + jnp.exp(-jnp.abs(xa)))
        na = -jnp.exp(ad_ref[0:1, :])
        dg = pltpu.roll(_dot(_tri(False, rows), d, NN, P_CUM), A_LANE - HEADS, 1)
        da = dg * na * _sigmoid(xa)
        is_g = lane >= A_LANE
        o_ref[...] = jnp.where(lane < HEADS, d * beta * (1.0 - beta), jnp.where(is_g, da, 0.0)).astype(BF16)
        ga = jnp.sum(jnp.where(is_g, dg * na * sp, 0.0), axis=0, keepdims=True)
        gd = jnp.sum(jnp.where(is_g, da, 0.0), axis=0, keepdims=True)

        @pl.when(pl.program_id(0) == 0)
        def _():
            ga_ref[...] = jnp.zeros_like(ga_ref)
            gd_ref[...] = jnp.zeros_like(gd_ref)

        ga_ref[...] += ga
        gd_ref[...] += gd

    one = pl.BlockSpec((1, DH), lambda i: (0, 0))
    return pl.pallas_call(
        body, name="prep_bg_bwd", grid=(nch // cpb,),
        in_specs=[pl.BlockSpec((rows, DH), lambda i: (i, BAB)), pl.BlockSpec((2, DH), lambda i: (0, 0)),
                  pl.BlockSpec((rows, DH), lambda i: (i, 0)), ANY],
        out_specs=[pl.BlockSpec((rows, DH), lambda i: (i, BAB)), one, one],
        out_shape=[jax.ShapeDtypeStruct(dproj.shape, BF16), jax.ShapeDtypeStruct((1, DH), F32),
                   jax.ShapeDtypeStruct((1, DH), F32)],
        input_output_aliases={3: 0},
        compiler_params=_params(("arbitrary",)),
    )(proj, ad, dbg, dproj)


def _gdn_out(o, proj, wg):
    n = o.shape[0]

    def body(o_ref, z_ref, w_ref, y_ref):
        ov, z = o_ref[...], z_ref[...]
        r = lax.rsqrt(jnp.mean(ov * ov, axis=-1, keepdims=True) + EPS)
        y_ref[...] = (ov * r * w_ref[...] * (z * _sigmoid(z))).astype(BF16)

    return pl.pallas_call(
        body, name="gdn_out", grid=(HEADS,),
        in_specs=[pl.BlockSpec((n, DH), lambda h: (0, h)), pl.BlockSpec((n, DH), lambda h: (0, ZB + h)),
                  pl.BlockSpec((1, DH), lambda h: (0, 0))],
        out_specs=pl.BlockSpec((n, DH), lambda h: (0, h)),
        out_shape=jax.ShapeDtypeStruct((n, 2 * GW), BF16),
        compiler_params=_params(("parallel",)),
    )(o, proj, wg)


def _gdn_out_bwd(o, proj, wg, dout_b, w_out):
    n = o.shape[0]
    d_model = dout_b.shape[1]

    def body(o_ref, z_ref, w_ref, g_ref, wo_ref, do_ref, dz_ref, gw_ref):
        ov, z, w = o_ref[...], z_ref[...], w_ref[...]
        d = _dot(g_ref[...], wo_ref[...], NT)
        r = lax.rsqrt(jnp.mean(ov * ov, axis=-1, keepdims=True) + EPS)
        nrm = ov * r
        s = _sigmoid(z)
        dz_ref[...] = (d * (nrm * w) * _dsilu(z, s)).astype(BF16)
        dn_w = d * (z * s)
        gw = jnp.sum(dn_w * nrm, axis=0, keepdims=True)
        dn = dn_w * w
        do_ref[...] = (r * (dn - nrm * jnp.mean(dn * nrm, axis=-1, keepdims=True))).astype(BF16)

        @pl.when(pl.program_id(0) == 0)
        def _():
            gw_ref[...] = jnp.zeros_like(gw_ref)

        gw_ref[...] += gw

    return pl.pallas_call(
        body, name="gdn_out_bwd", grid=(HEADS,),
        in_specs=[pl.BlockSpec((n, DH), lambda h: (0, h)), pl.BlockSpec((n, DH), lambda h: (0, ZB + h)),
                  pl.BlockSpec((1, DH), lambda h: (0, 0)), pl.BlockSpec((n, d_model), lambda h: (0, 0)),
                  pl.BlockSpec((DH, d_model), lambda h: (h, 0))],
        out_specs=[pl.BlockSpec((n, DH), lambda h: (0, h)), pl.BlockSpec((n, DH), lambda h: (0, ZB + h)),
                   pl.BlockSpec((1, DH), lambda h: (0, 0))],
        out_shape=[jax.ShapeDtypeStruct((n, GW), BF16), jax.ShapeDtypeStruct((n, GW_COLS), BF16),
                   jax.ShapeDtypeStruct((1, DH), F32)],
        compiler_params=_params(("arbitrary",), 40 * 2**20),
    )(o, proj, wg, dout_b, w_out)


def _conv_branch(proj, w3, b, mix):
    n = proj.shape[0]

    def body(p4, w_ref, b_ref, _, y_ref):
        u = p4[:, DH:2 * DH] * p4[:, 2 * DH:3 * DH]
        cc = _conv_silu(u, w_ref, 3) + b_ref[...]
        z = p4[:, 3 * DH:4 * DH]
        y_ref[...] = (p4[:, 0:DH] * cc * (z * _sigmoid(z))).astype(BF16)

    return pl.pallas_call(
        body, name="conv_branch", grid=(HEADS,),
        in_specs=[pl.BlockSpec((n, 4 * DH), lambda h: (0, h)), pl.BlockSpec((3, DH), lambda h: (0, h)),
                  pl.BlockSpec((1, DH), lambda h: (0, h)), ANY],
        out_specs=pl.BlockSpec((n, DH), lambda h: (0, HEADS + h)),
        out_shape=jax.ShapeDtypeStruct(mix.shape, BF16),
        input_output_aliases={3: 0},
        compiler_params=_params(("parallel",), 40 * 2**20),
    )(*_in_hbm(proj, w3, b, mix))


def _conv_branch_bwd(proj, w3, b, dout_b, w_out):
    n = proj.shape[0]
    d_model = dout_b.shape[1]

    def body(p4, w_ref, b_ref, g_ref, wo_ref, o4, gw_ref, gbias_ref):
        gb, gcv, hc, z = p4[:, 0:DH], p4[:, DH:2 * DH], p4[:, 2 * DH:3 * DH], p4[:, 3 * DH:4 * DH]
        d = _dot(g_ref[...], wo_ref[...], NT)
        dgb, dgc, dhc, dzc = (o4.at[:, kk * DH:(kk + 1) * DH] for kk in range(4))
        u = gcv * hc
        cc = _conv_silu(u, w_ref, 3) + b_ref[...]
        s = _sigmoid(z)
        dzc[...] = (d * (gb * cc) * _dsilu(z, s)).astype(BF16)
        dp = d * (z * s)
        dgb[...] = (dp * cc).astype(BF16)
        dcc = dp * gb
        gbias_ref[...] = jnp.sum(dcc, axis=0, keepdims=True)
        du = None
        for j in range(3):
            gw_ref[j:j + 1, :] = jnp.sum(dcc * _shift_down(u, 2 - j), axis=0, keepdims=True)
            t = _shift_up(dcc, 2 - j) * w_ref[j:j + 1, :]
            du = t if du is None else du + t
        dgc[...] = (du * hc).astype(BF16)
        dhc[...] = (du * gcv).astype(BF16)

    p4spec = pl.BlockSpec((n, 4 * DH), lambda h: (0, h))
    return pl.pallas_call(
        body, name="conv_branch_bwd", grid=(HEADS,),
        in_specs=[p4spec, pl.BlockSpec((3, DH), lambda h: (0, h)), pl.BlockSpec((1, DH), lambda h: (0, h)),
                  pl.BlockSpec((n, d_model), lambda h: (0, 0)), pl.BlockSpec((DH, d_model), lambda h: (HEADS + h, 0))],
        out_specs=[p4spec, pl.BlockSpec((3, DH), lambda h: (0, h)), pl.BlockSpec((1, DH), lambda h: (0, h))],
        out_shape=[jax.ShapeDtypeStruct((n, CW_COLS), BF16), jax.ShapeDtypeStruct((3, GW), F32),
                   jax.ShapeDtypeStruct((1, GW), F32)],
        compiler_params=_params(("parallel",), 52 * 2**20),
    )(proj, w3, b, dout_b, w_out)


def _out_loss(mix, w_out, x, tgt, wf):
    n, d = x.shape
    kdim = mix.shape[1]
    tr = min(256, n)

    def body(m_ref, wo_ref, x_ref, t_ref, w_ref, do_ref, dob_ref, gw_ref, loss_ref):
        ov = _dot(m_ref[...], wo_ref[...], NN) + x_ref[...]
        w = w_ref[...]
        r = lax.rsqrt(jnp.mean(ov * ov, axis=-1, keepdims=True) + EPS)
        nrm = ov * r
        e = nrm * w - t_ref[...]
        dy = e * (1.0 / d)
        dn = dy * w
        dout = r * (dn - nrm * jnp.mean(dn * nrm, axis=-1, keepdims=True))
        do_ref[...] = dout
        dob_ref[...] = dout.astype(BF16)

        @pl.when(pl.program_id(0) == 0)
        def _():
            gw_ref[...] = jnp.zeros_like(gw_ref)
            loss_ref[...] = jnp.zeros_like(loss_ref)

        gw_ref[...] += jnp.sum(dy * nrm, axis=0, keepdims=True)
        loss_ref[...] += (0.5 / d) * jnp.sum(jnp.sum(e * e, axis=-1, keepdims=True), axis=0, keepdims=True)

    row = pl.BlockSpec((tr, d), lambda i: (i, 0))
    return pl.pallas_call(
        body, name="out_loss", grid=(n // tr,),
        in_specs=[pl.BlockSpec((tr, kdim), lambda i: (i, 0)), pl.BlockSpec((kdim, d), lambda i: (0, 0)), row, row,
                  pl.BlockSpec((1, d), lambda i: (0, 0))],
        out_specs=[row, row, pl.BlockSpec((1, d), lambda i: (0, 0)), pl.BlockSpec((1, 1), lambda i: (0, 0))],
        out_shape=[jax.ShapeDtypeStruct((n, d), F32), jax.ShapeDtypeStruct((n, d), BF16),
                   jax.ShapeDtypeStruct((1, d), F32), jax.ShapeDtypeStruct((1, 1), F32)],
        compiler_params=_params(("arbitrary",), 40 * 2**20),
    )(mix, w_out, x, tgt, wf)


def _dh_rms_bwd(dproj, w_t, dh0, x, w, dout, tk):
    n, d = x.shape
    kdim = dproj.shape[1]
    tm = min(1024, n)
    tk = min(tk, kdim)
    nk = kdim // tk

    def body(a_ref, b_ref, dh0_ref, x_ref, w_ref, do_ref, dx_ref, gw_ref, acc):
        i, kk = pl.program_id(0), pl.program_id(1)
        part = _dot(a_ref[...], b_ref[...], NN)

        @pl.when(kk == 0)
        def _():
            acc[...] = part + dh0_ref[...]

        @pl.when(kk > 0)
        def _():
            acc[...] += part

        @pl.when((i == 0) & (kk == 0))
        def _():
            gw_ref[...] = jnp.zeros_like(gw_ref)

        @pl.when(kk == nk - 1)
        def _():
            xv, dhv = x_ref[...], acc[...]
            r = lax.rsqrt(jnp.mean(xv * xv, axis=-1, keepdims=True) + EPS)
            xn = xv * r
            dxn = dhv * w_ref[...]
            dx_ref[...] = r * (dxn - xn * jnp.mean(dxn * xn, axis=-1, keepdims=True)) + do_ref[...]
            gw_ref[...] += jnp.sum(dhv * xn, axis=0, keepdims=True)

    row = pl.BlockSpec((tm, d), lambda i, kk: (i, 0))
    one = pl.BlockSpec((1, d), lambda i, kk: (0, 0))
    return pl.pallas_call(
        body, name="dh_rms_bwd", grid=(n // tm, nk),
        in_specs=[pl.BlockSpec((tm, tk), lambda i, kk: (i, kk)), pl.BlockSpec((tk, d), lambda i, kk: (kk, 0)),
                  row, row, one, row],
        out_specs=[row, one],
        out_shape=[jax.ShapeDtypeStruct((n, d), F32), jax.ShapeDtypeStruct((1, d), F32)],
        scratch_shapes=[pltpu.VMEM((tm, d), F32)],
        compiler_params=_params(("arbitrary", "arbitrary"), 56 * 2**20),
    )(dproj, w_t, dh0, x, w, dout)


def _ij():
    i = lax.broadcasted_iota(jnp.int32, (CH, CH), 0)
    j = lax.broadcasted_iota(jnp.int32, (CH, CH), 1)
    return i, j


def _unit_lower_inverse(mats):
    i, j = _ij()
    eye = jnp.where(i == j, 1.0, 0.0)
    same16 = (i // 16) == (j // 16)
    same32 = (i // 32) == (j // 32)
    mm = lambda xs, ys: [_dot(x, y, NN, P_INV) for x, y in zip(xs, ys)]
    n1 = [jnp.where(same16, -a, 0.0) for a in mats]
    n2 = mm(n1, n1)
    n4 = mm(n2, n2)
    n8 = mm(n4, n4)
    t = [eye + x1 + x2 + x3 for x1, x2, x3 in zip(n1, n2, mm(n1, n2))]
    t = [x + y for x, y in zip(t, mm(t, n4))]
    t = [x + y for x, y in zip(t, mm(t, n8))]
    a1 = [jnp.where(same32 & jnp.logical_not(same16), a, 0.0) for a in mats]
    t = [x - y for x, y in zip(t, mm(t, mm(a1, t)))]
    a2 = [jnp.where(same32, 0.0, a) for a in mats]
    t = [x - y for x, y in zip(t, mm(t, mm(a2, t)))]
    return t


def _head_vectors(bg, bgt, h):
    bcol = bg[:, h:h + 1]
    gcol = bg[:, HEADS + h:HEADS + h + 1]
    grow = bgt[HEADS + h:HEADS + h + 1, :]
    return bcol, gcol, grow


def _decay(gcol, grow):
    i, j = _ij()
    return jnp.where(i >= j, jnp.exp(jnp.where(i >= j, gcol - grow, 0.0)), 0.0)


def _gdn_intra(q, k, v, bg, bgt):
    n = q.shape[0]
    nch = n // CH
    cps = 4 if nch % 4 == 0 else 1

    def body(q_ref, k_ref, v_ref, bg_ref, bgt_ref, u_ref, w_ref, p_ref, t_ref):
        i, j = _ij()
        items = [(ci, h) for ci in range(cps) for h in range(HEADS)]
        at = lambda ref, ci, h: ref.at[ci * CH:(ci + 1) * CH, h * DH:(h + 1) * DH]
        bgs = [bg_ref[ci * CH:(ci + 1) * CH, :] for ci in range(cps)]
        ks = [at(k_ref, ci, h)[...] for ci, h in items]
        vecs = [_head_vectors(bgs[ci], bgt_ref[ci], h) for ci, h in items]
        decs = [_decay(gcol, grow) for _, gcol, grow in vecs]
        kks = [_dot(kh, kh, NT, P_GRAM) for kh in ks]
        qks = [_dot(at(q_ref, ci, h)[...], kh, NT, P_GRAM) for (ci, h), kh in zip(items, ks)]
        ts = _unit_lower_inverse([jnp.where(i > j, bcol * kk * dec, 0.0)
                                  for (bcol, _, _), kk, dec in zip(vecs, kks, decs)])
        us = [_dot(t, at(v_ref, ci, h)[...] * bcol, NN, P_SOL) for t, (ci, h), (bcol, _, _) in zip(ts, items, vecs)]
        ws = [_dot(t, kh * (bcol * jnp.exp(gcol)), NN, P_SOL) for t, kh, (bcol, gcol, _) in zip(ts, ks, vecs)]
        for n_, (ci, h) in enumerate(items):
            p_ref[ci, h] = qks[n_] * decs[n_]
            t_ref[ci, h] = ts[n_].astype(BF16)
            at(u_ref, ci, h)[...] = us[n_]
            at(w_ref, ci, h)[...] = ws[n_].astype(BF16)

    row = pl.BlockSpec((cps * CH, GW), lambda c: (c, 0))
    sq = pl.BlockSpec((cps, HEADS, CH, CH), lambda c: (c, 0, 0, 0))
    big = jax.ShapeDtypeStruct((n, GW), F32)
    sqs = jax.ShapeDtypeStruct((nch, HEADS, CH, CH), F32)
    return pl.pallas_call(
        body, name="gdn_intra", grid=(nch // cps,),
        in_specs=[row, row, row, pl.BlockSpec((cps * CH, DH), lambda c: (c, 0)),
                  pl.BlockSpec((cps, DH, CH), lambda c: (c, 0, 0))],
        out_specs=[row, row, sq, sq],
        out_shape=[big, jax.ShapeDtypeStruct((n, GW), BF16), sqs, jax.ShapeDtypeStruct(sqs.shape, BF16)],
        compiler_params=_params(("parallel",)),
    )(q, k, v, bg, bgt)


def _gdn_scan(q, k, bg, u, w, p):
    n = q.shape[0]
    nch = n // CH
    cps = SCAN_CPS if nch % SCAN_CPS == 0 else 1

    def body(q_ref, k_ref, bg_ref, u_ref, w_ref, p_ref, o_ref, vn_ref, s_out, s_scr):
        @pl.when(pl.program_id(0) == 0)
        def _():
            s_scr[...] = jnp.zeros_like(s_scr)

        hs = range(HEADS)
        sls = [slice(h * DH, (h + 1) * DH) for h in hs]
        ss = [s_scr[h] for h in hs]
        for ci in range(cps):
            rs = slice(ci * CH, (ci + 1) * CH)
            bg = bg_ref[rs, :]
            gcols = [bg[:, HEADS + h:HEADS + h + 1] for h in hs]
            glasts = [g[CH - 1:CH, :] for g in gcols]
            wss = [_dot(w_ref[rs, sl], s, NN, P_SCAN) for sl, s in zip(sls, ss)]
            oqs = [_dot(q_ref[rs, sl] * jnp.exp(g), s, NN, P_SCAN) for sl, s, g in zip(sls, ss, gcols)]
            vns = [u_ref[rs, sl] - x for sl, x in zip(sls, wss)]
            ops = [_dot(p_ref[ci, h], vn, NN, P_SCAN) for h, vn in zip(hs, vns)]
            sns = [_dot(k_ref[rs, sl] * jnp.exp(gl - g), vn, TN, P_SCAN)
                   for sl, gl, g, vn in zip(sls, glasts, gcols, vns)]
            for h, sl in enumerate(sls):
                s_out[ci, :, sl] = ss[h].astype(BF16)
                vn_ref[rs, sl] = vns[h].astype(BF16)
                o_ref[rs, sl] = oqs[h] + ops[h]
            ss = [s * jnp.exp(gl) + sn for s, gl, sn in zip(ss, glasts, sns)]
        for h in hs:
            s_scr[h] = ss[h]

    row = pl.BlockSpec((cps * CH, GW), lambda c: (c, 0))
    big = jax.ShapeDtypeStruct((n, GW), F32)
    return pl.pallas_call(
        body, name="gdn_scan", grid=(nch // cps,),
        in_specs=[row, row, pl.BlockSpec((cps * CH, DH), lambda c: (c, 0)), row, row,
                  pl.BlockSpec((cps, HEADS, CH, CH), lambda c: (c, 0, 0, 0))],
        out_specs=[row, row, pl.BlockSpec((cps, DH, GW), lambda c: (c, 0, 0))],
        out_shape=[big, jax.ShapeDtypeStruct((n, GW), BF16), jax.ShapeDtypeStruct((nch, DH, GW), BF16)],
        scratch_shapes=[pltpu.VMEM((HEADS, DH, DH), F32)],
        compiler_params=_params(("arbitrary",)),
    )(q, k, bg, u, w, p)


def _gdn_scan_bwd(q, k, bg, w, p, vn, s_in, do):
    n = q.shape[0]
    nch = n // CH
    cps = SCAN_CPS if nch % SCAN_CPS == 0 else 1
    rev = lambda c: nch // cps - 1 - c

    def body(q_ref, k_ref, bg_ref, w_ref, p_ref, vn_ref, s_ref, do_ref,
             dqg_ref, dp_ref, du_ref, dw_ref, dks_ref, dgam_ref, ds_scr):
        @pl.when(pl.program_id(0) == 0)
        def _():
            ds_scr[...] = jnp.zeros_like(ds_scr)

        lane = _lane((1, DH))
        hs = range(HEADS)
        sls = [slice(h * DH, (h + 1) * DH) for h in hs]
        dss = [ds_scr[h] for h in hs]
        for ci in reversed(range(cps)):
            rs = slice(ci * CH, (ci + 1) * CH)
            bg = bg_ref[rs, :]
            gcols = [bg[:, HEADS + h:HEADS + h + 1] for h in hs]
            glasts = [g[CH - 1:CH, :] for g in gcols]
            ss = [s_ref[ci, :, sl] for sl in sls]
            dos = [do_ref[rs, sl] for sl in sls]
            vnl = [vn_ref[rs, sl] for sl in sls]
            dqgs = [_dot(d, s, NT, P_SCANB) for d, s in zip(dos, ss)]
            dps = [_dot(d, vn, NT, P_SCANB) for d, vn in zip(dos, vnl)]
            dvn1 = [_dot(p_ref[ci, h], d, TN, P_SCANB) for h, d in zip(hs, dos)]
            dvn2 = [_dot(k_ref[rs, sl] * jnp.exp(gl - g), ds, NN, P_SCANB)
                    for sl, gl, g, ds in zip(sls, glasts, gcols, dss)]
            dkss = [_dot(vn, ds, NT, P_SCANB) for vn, ds in zip(vnl, dss)]
            dsq = [_dot(q_ref[rs, sl] * jnp.exp(g), d, TN, P_SCANB) for sl, g, d in zip(sls, gcols, dos)]
            dvns = [a + b for a, b in zip(dvn1, dvn2)]
            dws = [_dot(dvn, s, NT, P_SCANB) for dvn, s in zip(dvns, ss)]
            dsw = [_dot(w_ref[rs, sl], dvn, TN, P_SCANB) for sl, dvn in zip(sls, dvns)]
            dgam = jnp.zeros((1, DH), F32)
            for h, sl in enumerate(sls):
                dqg_ref[rs, sl] = dqgs[h]
                dp_ref[ci, h] = dps[h]
                du_ref[rs, sl] = dvns[h].astype(BF16)
                dw_ref[rs, sl] = (-dws[h]).astype(BF16)
                dks_ref[rs, sl] = dkss[h]
                tot = jnp.sum(jnp.sum(dss[h] * ss[h], axis=-1, keepdims=True), axis=0, keepdims=True)
                dgam = dgam + jnp.where(lane == h, tot, 0.0)
            dgam_ref[ci] = jnp.broadcast_to(dgam, (8, DH))
            dss = [ds * jnp.exp(gl) + a - b for ds, gl, a, b in zip(dss, glasts, dsq, dsw)]
        for h in hs:
            ds_scr[h] = dss[h]

    row = pl.BlockSpec((cps * CH, GW), lambda c: (rev(c), 0))
    sq =pl.BlockSpec((cps, HEADS, CH, CH), lambda c: (rev(c), 0, 0, 0))
    big = jax.ShapeDtypeStruct((n, GW), F32)
    return pl.pallas_call(
        body, name="gdn_scan_bwd", grid=(nch // cps,),
        in_specs=[row, row, pl.BlockSpec((cps * CH, DH), lambda c: (rev(c), 0)), row, sq, row,
                  pl.BlockSpec((cps, DH, GW), lambda c: (rev(c), 0, 0)), row],
        out_specs=[row, sq, row, row, row, pl.BlockSpec((cps, 8, DH), lambda c: (rev(c), 0, 0))],
        out_shape=[big, jax.ShapeDtypeStruct((nch, HEADS, CH, CH), F32), jax.ShapeDtypeStruct((n, GW), BF16),
                   jax.ShapeDtypeStruct((n, GW), BF16), big,
                   jax.ShapeDtypeStruct((nch, 8, DH), F32)],
        scratch_shapes=[pltpu.VMEM((HEADS, DH, DH), F32)],
        compiler_params=_params(("arbitrary",)),
    )(q, k, bg, w, p, vn, s_in, do)


def _gdn_intra_bwd(q, k, v, bg, bgt, t, u, w, p, dqg, dp, du, dw, dks, dgam):
    n = q.shape[0]
    nch = n // CH
    cps = 2 if nch % 2 == 0 else 1

    def body(q_ref, k_ref, v_ref, bg_ref, bgt_ref, t_ref, u_ref, w_ref, p_ref,
             dqg_ref, dp_ref, du_ref, dw_ref, dks_ref, dgam_ref, dq_ref, dk_ref, dv_ref, dbg_ref):
        i, j = _ij()
        rows1 = lax.broadcasted_iota(jnp.int32, (CH, 1), 0)
        lane = _lane((CH, DH))
        rsum = lambda x: jnp.sum(x, axis=-1, keepdims=True)
        items = [(ci, h) for ci in range(cps) for h in range(HEADS)]
        at = lambda ref, it: ref.at[it[0] * CH:(it[0] + 1) * CH, it[1] * DH:(it[1] + 1) * DH]
        ld = lambda ref: [at(ref, it)[...] for it in items]
        bgs = [bg_ref[ci * CH:(ci + 1) * CH, :] for ci in range(cps)]
        qs, ks = ld(q_ref), ld(k_ref)
        vecs = [_head_vectors(bgs[ci], bgt_ref[ci], h) for ci, h in items]
        decs = [_decay(gcol, grow) for _, gcol, grow in vecs]
        ths = [t_ref[ci, h] for ci, h in items]
        drus = [_dot(th, x_, TN, P_BWD) for th, x_ in zip(ths, ld(du_ref))]
        drws = [_dot(th, x_, TN, P_BWD) for th, x_ in zip(ths, ld(dw_ref))]
        kks = [_dot(kh, kh, NT, P_GRAM) for kh in ks]
        da1 = [_dot(dru, x_, NT, P_BWD) for dru, x_ in zip(drus, ld(u_ref))]
        da2 = [_dot(drw, x_, NT, P_BWD) for drw, x_ in zip(drws, ld(w_ref))]
        das = [jnp.where(i > j, -(x_ + y_), 0.0) for x_, y_ in zip(da1, da2)]
        dkks = [da * bcol * dec for da, (bcol, _, _), dec in zip(das, vecs, decs)]
        dps = [dp_ref[ci, h] for ci, h in items]
        dqks = [dp_ * dec for dp_, dec in zip(dps, decs)]
        dq_ps = [_dot(dqk, kh, NN, P_BWD) for dqk, kh in zip(dqks, ks)]
        dk_ps = [_dot(dqk, qh, TN, P_BWD) for dqk, qh in zip(dqks, qs)]
        dk_as = [_dot(dkk, kh, NN, P_BWD) for dkk, kh in zip(dkks, ks)]
        dk_bs = [_dot(dkk, kh, TN, P_BWD) for dkk, kh in zip(dkks, ks)]
        bcols = [vc[0] for vc in vecs]
        gcols = [vc[1] for vc in vecs]
        gams = [jnp.exp(g) for g in gcols]
        glasts = [g[CH - 1:CH, :] for g in gcols]
        es = [jnp.exp(gl - g) for gl, g in zip(glasts, gcols)]
        kgs = [kh * gam for kh, gam in zip(ks, gams)]
        dqgs, dkss = ld(dqg_ref), ld(dks_ref)
        wks = [drw * kg for drw, kg in zip(drws, kgs)]
        kss = [dk_ * (kh * e) for dk_, kh, e in zip(dkss, ks, es)]
        r_beta = [rsum(dru * x_ + wk) for dru, x_, wk in zip(drus, ld(v_ref), wks)]
        r_ak = [rsum(da * kk * dec) for da, kk, dec in zip(das, kks, decs)]
        r_gc = [rsum(wk * bcol + dqg * (qh * gam) - ks_)
                for wk, bcol, dqg, qh, gam, ks_ in zip(wks, bcols, dqgs, qs, gams, kss)]
        tk_tot = [jnp.sum(jnp.sum(ks_, axis=0, keepdims=True), axis=-1, keepdims=True) for ks_ in kss]
        mdecs = [da * (bcol * kk * dec) + dp_ * p_ref[ci, h]
                 for (ci, h), da, bcol, kk, dec, dp_ in zip(items, das, bcols, kks, decs, dps)]
        r_md = [rsum(m) for m in mdecs]
        c_md = [rsum(jnp.where(i == j, jnp.sum(m, axis=0, keepdims=True), 0.0)) for m in mdecs]
        dbgs = [jnp.zeros((CH, DH), F32) for _ in range(cps)]
        for n_, (ci, h) in enumerate(items):
            at(dv_ref, (ci, h))[...] = bcols[n_] * drus[n_]
            at(dq_ref, (ci, h))[...] = gams[n_] * dqgs[n_] + dq_ps[n_]
            at(dk_ref, (ci, h))[...] = ((bcols[n_] * gams[n_]) * drws[n_] + dk_ps[n_] + dk_as[n_] + dk_bs[n_]
                                        + dkss[n_] * es[n_])
            dbeta = r_beta[n_] + r_ak[n_]
            dglast = tk_tot[n_] + dgam_ref[ci, 0:1, h:h + 1] * jnp.exp(glasts[n_])
            dgc = r_gc[n_] + r_md[n_] - c_md[n_] + jnp.where(rows1 == CH - 1, dglast, 0.0)
            dbgs[ci] = dbgs[ci] + jnp.where(lane == h, dbeta, 0.0) + jnp.where(lane == HEADS + h, dgc, 0.0)
        for ci in range(cps):
            dbg_ref[ci * CH:(ci + 1) * CH, :] = dbgs[ci]

    row = pl.BlockSpec((cps * CH, GW), lambda c: (c, 0))
    sq = pl.BlockSpec((cps, HEADS, CH, CH), lambda c: (c, 0, 0, 0))
    small = pl.BlockSpec((cps * CH, DH), lambda c: (c, 0))
    big = jax.ShapeDtypeStruct((n, GW), F32)
    return pl.pallas_call(
        body, name="gdn_intra_bwd", grid=(nch // cps,),
        in_specs=[row, row, row, small, pl.BlockSpec((cps, DH, CH), lambda c: (c, 0, 0)), sq, row, row, sq,
                  row, sq, row, row, row, pl.BlockSpec((cps, 8, DH), lambda c: (c, 0, 0))],
        out_specs=[row, row, row, small],
        out_shape=[big, big, big, jax.ShapeDtypeStruct((n, DH), F32)],
        compiler_params=_params(("parallel",)),
    )(q, k, v, bg, bgt, t, u, w, p, dqg, dp, du, dw, dks, dgam)


def _local_step(x, tgt, h, w_g, cqw, late, norm_in_w, ad, gdn_norm_w, conv_b, final_norm_w,
                on_grad_c=None, on_grad_g=None, on_q=None):
    proj_g = _matmul(h, w_g, NT, F32, 512, 1408, 1024, "mm_proj_g", n=GW_COLS, b_outer=True)
    q, k, v = _prep_qkv(proj_g, cqw)
    if on_q is not None:
        q = on_q(q)
    bg, bgt = _prep_bg(proj_g, ad)
    u, w, p, t = _gdn_intra(q, k, v, bg, bgt)
    o, vn, s_in = _gdn_scan(q, k, bg, u, w, p)
    w_c, w_out, conv_w = late(o)
    proj_c = _matmul(h, w_c, NT, F32, 512, 1024, 1024, "mm_proj_c", n=CW_COLS, b_outer=True)
    mix = _conv_branch(proj_c, conv_w, conv_b, _gdn_out(o, proj_g, gdn_norm_w))
    dout, dout_b, g_fn, loss = _out_loss(mix, w_out, x, tgt, final_norm_w)

    g_wout = _matmul(mix, dout_b, TN, BF16, 512, 512, 2048, "mm_gwout")
    do, dproj_g, g_gn = _gdn_out_bwd(o, proj_g, gdn_norm_w, dout_b, w_out)
    dproj_c, g_cw, g_cb = _conv_branch_bwd(proj_c, conv_w, conv_b, dout_b, w_out)
    g_c = _matmul(dproj_c, h, TN, BF16, 1024, 512, 2048, "mm_gwin_c")
    if on_grad_c is not None:
        do = on_grad_c(g_c, g_wout, do)
    dqg, dp, du, dw, dks, dgam = _gdn_scan_bwd(q, k, bg, w, p, vn, s_in, do)
    dq, dk, dv, dbg = _gdn_intra_bwd(q, k, v, bg, bgt, t, u, w, p, dqg, dp, du, dw, dks, dgam)
    dproj_g, gq, gk, gv = _prep_qkv_bwd(proj_g, cqw, dq, dk, dv, dproj_g)
    dproj_g, g_al, g_dt = _prep_bg_bwd(proj_g, ad, dbg, dproj_g)
    g_g = _matmul(dproj_g, h, TN, BF16, 1408, 512, 2048, "mm_gwin_g")
    if on_grad_g is not None:
        dproj_g = on_grad_g(g_g, dproj_g)
    dh = _matmul(dproj_g, w_g, NN, F32, 1024, 1024, 1408, "mm_dh_g")
    gx, g_nin = _dh_rms_bwd(dproj_c, w_c, dh, x, norm_in_w, dout, 1024)
    small = dict(nin=g_nin, cb=g_cb, fn=g_fn, al=g_al, dt=g_dt, gn=g_gn, cq=(gq, gk, gv), cw=g_cw, loss=loss)
    return gx, small, (g_g, g_c, g_wout)


def _place():
    x, y, c = lax.axis_index("x"), lax.axis_index("y"), lax.axis_index("c")
    chips = [(1 - x, y), (x, 1 - y), (1 - x, 1 - y)]
    return x, y, c, chips


def _blk(ref, b):
    if isinstance(b, int):
        return ref.at[b * DH:(b + 1) * DH, :]
    return ref.at[pl.ds(pl.multiple_of(b * DH, DH), DH), :]


HBM = pl.BlockSpec(memory_space=pltpu.HBM)
SEM = pl.BlockSpec(memory_space=pltpu.SEMAPHORE)
EFFECT = pltpu.SideEffectType.DATAFLOW_SIDE_EFFECTING


def _split_start(name, issue, bufs, n_sems):
    nbuf = len(bufs)

    def body(*refs):
        issue(refs[:nbuf], refs[nbuf], refs[nbuf + 1])
        refs[-1][...] = jnp.zeros_like(refs[-1])

    out = pl.pallas_call(
        body, name=name,
        out_shape=(pltpu.SemaphoreType.DMA((n_sems,)), pltpu.SemaphoreType.DMA((n_sems,)),
                   *[pltpu.HBM(b.shape, b.dtype) for b in bufs], jax.ShapeDtypeStruct((8, DH), F32)),
        in_specs=[HBM] * nbuf,
        out_specs=(SEM, SEM, *[HBM] * nbuf, pl.BlockSpec(memory_space=pltpu.VMEM)),
        input_output_aliases={a: 2 + a for a in range(nbuf)},
        compiler_params=pltpu.CompilerParams(has_side_effects=EFFECT),
    )(*[pltpu.with_memory_space_constraint(b, pltpu.HBM) for b in bufs])
    return out[0], out[1], list(out[2:2 + nbuf]), out[-1]


def _split_wait(name, await_, send_sems, recv_sems, bufs, after):
    nbuf = len(bufs)
    after = list(after) if isinstance(after, (list, tuple)) else [after]

    def body(*refs):
        await_(refs[:nbuf], refs[nbuf], refs[nbuf + 1])

    out = pl.pallas_call(
        body, name=name,
        out_shape=tuple(pltpu.HBM(b.shape, b.dtype) for b in bufs),
        in_specs=[HBM] * nbuf + [SEM, SEM] + [ANY] * len(after), out_specs=tuple([HBM] * nbuf),
        input_output_aliases={a: a for a in range(nbuf)},
        compiler_params=pltpu.CompilerParams(has_side_effects=EFFECT),
    )(*bufs, send_sems, recv_sems, *after)
    return list(out)


def _phase_blocks(chip, phase, edges, parity=None):
    return [(b, blk) for b, (grp, blk) in enumerate(_shard_blocks(chip, edges))
            if grp == phase and (parity is None or b % 2 == parity)]


def _cols(ref, nblk):
    return ref.at[0:nblk * DH, :]


def _block_table(chip, edges, spare_g, spare_c):
    rows = []
    for s in range(4):
        sb = _shard_blocks(s, edges)
        rows.append([[blk if grp == "g" else spare_g for grp, blk in sb],
                     [blk if grp == "c" else spare_c for grp, blk in sb],
                     [int(grp == "g") for grp, _ in sb], [s] * ALIGNED_BLOCKS])
    return jnp.asarray(rows, jnp.int32)[chip]


def _place_own(a_shard, wo, cq, cw, bufs):
    d = a_shard.shape[1]
    chip = 2 * lax.axis_index("x") + lax.axis_index("y")

    def body(t_ref, a_ref, wo_ref, cq_ref, cw_ref, *refs):
        wg_ref, wc_ref, wog_ref, cqg_ref, cwg_ref = refs[5:]
        wg_ref[...] = a_ref[...]
        wc_ref[...] = a_ref[...]

        @pl.when(pl.program_id(0) == 0)
        def _():
            wog_ref[0] = wo_ref[...]
            cqg_ref[0] = cq_ref[...]
            cwg_ref[0] = cw_ref[...]

    whole = lambda s: pl.BlockSpec(s.shape, lambda b, t: (0,) * s.ndim)
    slot = lambda s: pl.BlockSpec((1,) + s.shape, lambda b, t: (t[3, 0],) + (0,) * s.ndim)
    return pl.pallas_call(
        body, name="place_own",
        grid_spec=pltpu.PrefetchScalarGridSpec(
            num_scalar_prefetch=1, grid=(ALIGNED_BLOCKS,),
            in_specs=[pl.BlockSpec((DH, d), lambda b, t: (b, 0)), whole(wo), whole(cq), whole(cw)] + [ANY] * 5,
            out_specs=[pl.BlockSpec((DH, d), lambda b, t: (t[0, b], 0)),
                       pl.BlockSpec((DH, d), lambda b, t: (t[1, b], 0)), slot(wo), slot(cq), slot(cw)]),
        out_shape=[jax.ShapeDtypeStruct(b.shape, b.dtype) for b in bufs],
        input_output_aliases={5 + a: a for a in range(5)},
        compiler_params=_params(("arbitrary",)),
    )(_block_table(chip, True, G_SPARE, C_SPARE), a_shard, wo, cq, cw, *bufs)


def _tie(x, token, name):
    def body(x_ref, t_ref, o_ref):
        del x_ref, t_ref, o_ref

    return pl.pallas_call(
        body, name=name, in_specs=[ANY, ANY], out_specs=ANY,
        out_shape=jax.ShapeDtypeStruct(x.shape, x.dtype), input_output_aliases={0: 0},
    )(x, token)


def _gather_start(phase, a_shard, w_grp, singles):
    ns = len(singles)

    def issue(refs, send_sems, recv_sems):
        a_ref, w_ref = refs[0], refs[1]
        x, y, c, chips = _place()
        mine = 2 * x + y
        for jj, (px, py) in enumerate(chips):
            to = dict(device_id=(px, py, c), device_id_type=MESH)
            for a in range(ns):
                pltpu.make_async_remote_copy(
                    src_ref=refs[2 + 2 * a], dst_ref=refs[3 + 2 * a].at[mine],
                    send_sem=send_sems.at[(1 + ns) * jj + 1 + a], recv_sem=recv_sems.at[(1 + ns) * jj + 1 + a],
                    **to).start()
        for s in range(4):
            for par in range(2):
                blocks = _phase_blocks(s, phase, True, par)
                if blocks:
                    @pl.when((mine == s) & (c == par))
                    def _():
                        for jj, (px, py) in enumerate(chips):
                            for b, blk in blocks:
                                pltpu.make_async_remote_copy(
                                    src_ref=_blk(a_ref, b), dst_ref=_blk(w_ref, blk),
                                    send_sem=send_sems.at[(1 + ns) * jj], recv_sem=recv_sems.at[(1 + ns) * jj],
                                    device_id=(px, py, c), device_id_type=MESH).start()

    bufs = [a_shard, w_grp] + [t for pair in singles for t in pair]
    return _split_start("gather_start_" + phase, issue, bufs, 3 * (1 + ns))


def _gather_wait(phase, send_sems, recv_sems, bufs, after):
    ns = (len(bufs) - 2) // 2

    def await_(refs, send_sems, recv_sems):
        a_ref, w_ref = refs[0], refs[1]
        x, y, c, chips = _place()
        mine = 2 * x + y
        for jj, (px, py) in enumerate(chips):
            to = dict(device_id=(px, py, c), device_id_type=MESH)
            peer = 2 * px + py
            for a in range(ns):
                cp = pltpu.make_async_remote_copy(
                    src_ref=refs[2 + 2 * a], dst_ref=refs[3 + 2 * a].at[mine],
                    send_sem=send_sems.at[(1 + ns) * jj + 1 + a], recv_sem=recv_sems.at[(1 + ns) * jj + 1 + a], **to)
                cp.wait_recv()
                cp.wait_send()
            for s in range(4):
                for par in range(2):
                    nblk = len(_phase_blocks(s, phase, True, par))
                    if nblk:
                        both = pltpu.make_async_remote_copy(
                            src_ref=_cols(a_ref, nblk), dst_ref=_cols(w_ref, nblk),
                            send_sem=send_sems.at[(1 + ns) * jj], recv_sem=recv_sems.at[(1 + ns) * jj], **to)

                        @pl.when((peer == s) & (c == par))
                        def _():
                            both.wait_recv()

                        @pl.when((mine == s) & (c == par))
                        def _():
                            both.wait_send()

    return _split_wait("gather_wait_" + phase, await_, send_sems, recv_sems, bufs, after)


def _sibling_forward_parts(phase):
    def each(w_ref, send_sems, recv_sems, start):
        x, y, c, chips = _place()
        to = dict(device_id=(x, y, 1 - c), device_id_type=MESH)
        for jj, (px, py) in enumerate(chips):
            peer = 2 * px + py
            for s in range(4):
                for par in range(2):
                    mine_blocks = _phase_blocks(s, phase, True, par)
                    theirs = len(_phase_blocks(s, phase, True, 1 - par))
                    if not (mine_blocks or theirs):
                        continue

                    @pl.when((peer == s) & (c == par))
                    def _():
                        if start:
                            for _, blk in mine_blocks:
                                pltpu.make_async_remote_copy(
                                    src_ref=_blk(w_ref, blk), dst_ref=_blk(w_ref, blk),
                                    send_sem=send_sems.at[jj], recv_sem=recv_sems.at[jj], **to).start()
                            return
                        if theirs:
                            pltpu.make_async_remote_copy(
                                src_ref=_cols(w_ref, theirs), dst_ref=_cols(w_ref, theirs),
                                send_sem=send_sems.at[jj], recv_sem=recv_sems.at[jj], **to).wait_recv()
                        if mine_blocks:
                            pltpu.make_async_remote_copy(
                                src_ref=_cols(w_ref, len(mine_blocks)), dst_ref=_cols(w_ref, len(mine_blocks)),
                                send_sem=send_sems.at[jj], recv_sem=recv_sems.at[jj], **to).wait_send()

    issue = lambda refs, send_sems, recv_sems: each(refs[0], send_sems, recv_sems, True)
    await_ = lambda refs, send_sems, recv_sems: each(refs[0], send_sems, recv_sems, False)
    return issue, await_


def _sibling_forward(phase, w_grp):
    issue, await_ = _sibling_forward_parts(phase)

    def body(w_in_ref, w_ref, send_sems, recv_sems):
        del w_in_ref
        issue([w_ref], send_sems, recv_sems)
        await_([w_ref], send_sems, recv_sems)

    return pl.pallas_call(
        body, name="sibling_forward_" + phase, in_specs=[ANY], out_specs=ANY,
        out_shape=jax.ShapeDtypeStruct(w_grp.shape, w_grp.dtype), input_output_aliases={0: 0},
        scratch_shapes=[pltpu.SemaphoreType.DMA((3,)), pltpu.SemaphoreType.DMA((3,))],
    )(w_grp)


def _merge_edges(w, edge0, mixed, name):
    d = w.shape[1]

    def body(e_ref, o_ref):
        o_ref[...] = e_ref[0:DH, :] + e_ref[DH:2 * DH, :]

    def to_block(i):
        r = mixed[-1]
        for kk in range(len(mixed) - 2, -1, -1):
            r = jnp.where(i == kk, mixed[kk], r)
        return r

    return pl.pallas_call(
        body, name=name, grid=(len(mixed),),
        in_specs=[pl.BlockSpec((2 * DH, d), lambda i: (edge0 // 2 + i, 0))],
        out_specs=pl.BlockSpec((DH, d), lambda i: (to_block(i), 0)),
        out_shape=jax.ShapeDtypeStruct(w.shape, w.dtype),
        input_output_aliases={0: 0},
        compiler_params=_params(("arbitrary",)),
    )(w)


def _scatter_start(phase, g_grp, land, singles, halved=False):
    ns = len(singles)

    def issue(refs, send_sems, recv_sems):
        g_ref, land_ref = refs[0], refs[1]
        x, y, c, chips = _place()
        for jj, (px, py) in enumerate(chips):
            to = dict(device_id=(px, py, c), device_id_type=MESH)
            peer = 2 * px + py
            for a in range(ns):
                pltpu.make_async_remote_copy(
                    src_ref=refs[2 + 2 * a].at[peer], dst_ref=refs[3 + 2 * a].at[jj],
                    send_sem=send_sems.at[(1 + ns) * jj + 1 + a], recv_sem=recv_sems.at[(1 + ns) * jj + 1 + a],
                    **to).start()
            for s in range(4):
                for par in ((0, 1) if halved else (None,)):
                    blocks = _phase_blocks(s, phase, False, par)
                    if blocks:
                        @pl.when((peer == s) if par is None else ((peer == s) & (c == par)))
                        def _():
                            for b, blk in blocks:
                                pltpu.make_async_remote_copy(
                                    src_ref=_blk(g_ref, blk), dst_ref=_blk(land_ref.at[jj], b),
                                    send_sem=send_sems.at[(1 + ns) * jj], recv_sem=recv_sems.at[(1 + ns) * jj],
                                    **to).start()

    bufs = [g_grp, land] + [t for pair in singles for t in pair]
    return _split_start("scatter_start_" + phase, issue, bufs, 3 * (1 + ns))


def _scatter_wait(phase, send_sems, recv_sems, bufs, after, halved=False):
    ns = (len(bufs) - 2) // 2

    def await_(refs, send_sems, recv_sems):
        g_ref, land_ref = refs[0], refs[1]
        x, y, c, chips = _place()
        mine = 2 * x + y
        for jj, (px, py) in enumerate(chips):
            to = dict(device_id=(px, py, c), device_id_type=MESH)
            peer = 2 * px + py
            for a in range(ns):
                cp = pltpu.make_async_remote_copy(
                    src_ref=refs[2 + 2 * a].at[peer], dst_ref=refs[3 + 2 * a].at[jj],
                    send_sem=send_sems.at[(1 + ns) * jj + 1 + a], recv_sem=recv_sems.at[(1 + ns) * jj + 1 + a], **to)
                cp.wait_recv()
                cp.wait_send()
            for s in range(4):
                for par in ((0, 1) if halved else (None,)):
                    nblk = len(_phase_blocks(s, phase, False, par))
                    if nblk:
                        both = pltpu.make_async_remote_copy(
                            src_ref=_cols(g_ref, nblk), dst_ref=_cols(land_ref.at[jj], nblk),
                            send_sem=send_sems.at[(1 + ns) * jj], recv_sem=recv_sems.at[(1 + ns) * jj], **to)

                        @pl.when((mine == s) if par is None else ((mine == s) & (c == par)))
                        def _():
                            both.wait_recv()

                        @pl.when((peer == s) if par is None else ((peer == s) & (c == par)))
                        def _():
                            both.wait_send()

    return _split_wait("scatter_wait_" + phase, await_, send_sems, recv_sems, bufs, after)


def _needed_blocks(phase, parity):
    return sorted({blk for s in range(4) for _, blk in _phase_blocks(s, phase, False, parity)})


def _pair_reduce(phase, g_grp):
    n, d = g_grp.shape

    def swap(g_ref, sib_ref, send_sem, recv_sem):
        x, y, c, _ = _place()
        to = dict(device_id=(x, y, 1 - c), device_id_type=MESH)
        for par in range(2):
            give, get = _needed_blocks(phase, 1 - par), _needed_blocks(phase, par)

            @pl.when(c == par)
            def _():
                for blk in give:
                    pltpu.make_async_remote_copy(src_ref=_blk(g_ref, blk), dst_ref=_blk(sib_ref, blk),
                                                 send_sem=send_sem, recv_sem=recv_sem, **to).start()
                pltpu.make_async_remote_copy(src_ref=_cols(g_ref, len(get)), dst_ref=_cols(sib_ref, len(get)),
                                             send_sem=send_sem, recv_sem=recv_sem, **to).wait_recv()
                pltpu.make_async_remote_copy(src_ref=_cols(g_ref, len(give)), dst_ref=_cols(sib_ref, len(give)),
                                             send_sem=send_sem, recv_sem=recv_sem, **to).wait_send()

    sib = pl.pallas_call(
        swap, name="pair_swap_" + phase, in_specs=[ANY], out_specs=ANY,
        out_shape=jax.ShapeDtypeStruct((n, d), g_grp.dtype),
        scratch_shapes=[pltpu.SemaphoreType.DMA, pltpu.SemaphoreType.DMA],
    )(*_in_hbm(g_grp))

    lists = [_needed_blocks(phase, par) for par in range(2)]
    longest = max(len(t) for t in lists)
    table = jnp.asarray([t + [t[-1]] * (longest - len(t)) for t in lists], jnp.int32)[lax.axis_index("c")]

    def add(t_ref, a_ref, b_ref, o_ref):
        o_ref[...] = (a_ref[...].astype(F32) + b_ref[...].astype(F32)).astype(o_ref.dtype)

    blk = pl.BlockSpec((DH, d), lambda i, t: (t[i], 0))
    return pl.pallas_call(
        add, name="pair_add_" + phase,
        grid_spec=pltpu.PrefetchScalarGridSpec(num_scalar_prefetch=1, grid=(longest,),
                                               in_specs=[blk, blk], out_specs=blk),
        out_shape=jax.ShapeDtypeStruct((n, d), g_grp.dtype),
        compiler_params=_params(("arbitrary",)),
    )(table, g_grp, sib)


def _sum_shard(g_g, g_c, land):
    d = g_g.shape[1]
    chip = 2 * lax.axis_index("x") + lax.axis_index("y")

    def body(t_ref, gg_ref, gc_ref, land_ref, o_ref):
        b = pl.program_id(0)
        in_g = t_ref[2, b] == 1
        own = jnp.where(in_g, gg_ref[...].astype(F32), gc_ref[...].astype(F32))
        for jj in range(3):
            own = own + land_ref[jj].astype(F32)
        o_ref[...] = jnp.where(in_g & (b % 2 != lax.axis_index("c")), 0.0, own)

    return pl.pallas_call(
        body, name="sum_w_in",
        grid_spec=pltpu.PrefetchScalarGridSpec(
            num_scalar_prefetch=1, grid=(ALIGNED_BLOCKS,),
            in_specs=[pl.BlockSpec((DH, d), lambda b, t: (t[0, b], 0)), pl.BlockSpec((DH, d), lambda b, t: (t[1, b], 0)),
                      pl.BlockSpec((3, DH, d), lambda b, t: (0, b, 0))],
            out_specs=pl.BlockSpec((DH, d), lambda b, t: (b, 0))),
        out_shape=jax.ShapeDtypeStruct((ALIGNED_W, d), F32),
        compiler_params=_params(("arbitrary",)),
    )(_block_table(chip, False, 0, 0), g_g, g_c, land)


def _sum_rows(stack, land, rows):
    _, r, d = stack.shape
    rows = min(rows, r)
    chip = 2 * lax.axis_index("x") + lax.axis_index("y")

    def body(t_ref, own_ref, land_ref, o_ref):
        acc = own_ref[0].astype(F32)
        for jj in range(3):
            acc = acc + land_ref[jj].astype(F32)
        o_ref[...] = acc

    return pl.pallas_call(
        body, name="sum_w_out",
        grid_spec=pltpu.PrefetchScalarGridSpec(
            num_scalar_prefetch=1, grid=(r // rows,),
            in_specs=[pl.BlockSpec((1, rows, d), lambda i, t: (t[0], i, 0)),
                      pl.BlockSpec((3, rows, d), lambda i, t: (0, i, 0))],
            out_specs=pl.BlockSpec((rows, d), lambda i, t: (i, 0))),
        out_shape=jax.ShapeDtypeStruct((r, d), F32),
        compiler_params=_params(("arbitrary",)),
    )(jnp.reshape(chip, (1,)).astype(jnp.int32), stack, land)


def _exchange_parts(n_swap, with_pack):
    def copies(refs, send_sems, recv_sems):
        x, y, c, _ = _place()
        me = 4 * x + 2 * y + c
        cps = [pltpu.make_async_remote_copy(
            src_ref=refs[2 * a], dst_ref=refs[2 * a + 1], send_sem=send_sems.at[a], recv_sem=recv_sems.at[a],
            device_id=(x, y, 1 - c), device_id_type=MESH) for a in range(n_swap)]
        if with_pack:
            pack_ref, packs = refs[2 * n_swap], refs[2 * n_swap + 1]
            for r in range(1, 8):
                dx, dy, dc = (r >> 2) & 1, (r >> 1) & 1, r & 1
                peer = (x + dx - 2 * x * dx, y + dy - 2 * y * dy, c + dc - 2 * c * dc)
                cps.append(pltpu.make_async_remote_copy(
                    src_ref=pack_ref, dst_ref=packs.at[me], send_sem=send_sems.at[n_swap + r - 1],
                    recv_sem=recv_sems.at[n_swap + r - 1], device_id=peer, device_id_type=MESH))
        return cps

    def issue(refs, send_sems, recv_sems):
        for cp in copies(refs, send_sems, recv_sems):
            cp.start()

    def await_(refs, send_sems, recv_sems):
        cps = copies(refs, send_sems, recv_sems)
        for cp in cps:
            cp.wait_recv()
        for cp in cps:
            cp.wait_send()

    return issue, await_, n_swap + (7 if with_pack else 0)


def _sum_packs(pack, packs):
    x, y, c = lax.axis_index("x"), lax.axis_index("y"), lax.axis_index("c")
    me = jnp.reshape(4 * x + 2 * y + c, (1,)).astype(jnp.int32)

    def body(me_ref, own_ref, p_ref, o_ref):
        acc = jnp.where(me_ref[0] == 0, own_ref[...], p_ref[0])
        for d in range(1, 8):
            acc = acc + jnp.where(me_ref[0] == d, own_ref[...], p_ref[d])
        o_ref[...] = acc

    full = lambda s: pl.BlockSpec(s.shape, lambda i, t: (0,) * s.ndim)
    return pl.pallas_call(
        body, name="sum_packs",
        grid_spec=pltpu.PrefetchScalarGridSpec(num_scalar_prefetch=1, grid=(1,), in_specs=[full(pack), full(packs)],
                                               out_specs=full(pack)),
        out_shape=jax.ShapeDtypeStruct(pack.shape, F32),
    )(me, pack, packs)


def _adamw_update(g, w_ref, m_ref, v_ref, go, do, mo, vo):
    c1 = 1.0 / (1.0 - ADAM_B1 ** ADAM_STEP)
    c2 = 1.0 / (1.0 - ADAM_B2 ** ADAM_STEP)
    mn = ADAM_B1 * m_ref[...] + (1.0 - ADAM_B1) * g
    vn = ADAM_B2 * v_ref[...] + (1.0 - ADAM_B2) * (g * g)
    go[...] = g
    mo[...] = mn
    vo[...] = vn
    do[...] = -ADAM_LR * ((mn * c1) / (jnp.sqrt(vn * c2) + ADAM_EPS) + ADAM_WD * w_ref[...])


def _adamw(w, m, v, g1, g2, rows, name):
    r, cdim = w.shape
    rows = min(rows, r)

    def body(*refs):
        n_in = 4 if g2 is None else 5
        w_ref, m_ref, v_ref, g_ref = refs[:4]
        g = g_ref[...] if g2 is None else g_ref[...] + refs[4][...]
        _adamw_update(g, w_ref, m_ref, v_ref, *refs[n_in:n_in + 4])

    blk = pl.BlockSpec((rows, cdim), lambda i: (i, 0))
    args = [w, m, v, g1] + ([] if g2 is None else [g2])
    shp = jax.ShapeDtypeStruct((r, cdim), F32)
    return pl.pallas_call(
        body, name=name, grid=(r // rows,),
        in_specs=[blk] * len(args), out_specs=[blk] * 4, out_shape=[shp] * 4,
        compiler_params=_params(("parallel",), 20 * rows * cdim * 4 + 8 * 2**20),
    )(*_in_hbm(*args))


def _adamw_shard(wt, mt, vt, g1, g2):
    r, d = wt.shape
    cols = min(256, d)

    def body(w_ref, m_ref, v_ref, g_ref, g2_ref, go, do, mo, vo, pad_ref):
        chip = 2 * lax.axis_index("x") + lax.axis_index("y")
        back = [(ALIGNED_W - s) % ALIGNED_W for s in SHIFTS]
        pad_ref[...] = pltpu.roll(g_ref[...] + g2_ref[...], _by_chip(chip, back), 0)
        outs = [o.at[:, 0, :] for o in (go, do, mo, vo)]
        _adamw_update(pad_ref[0:r, :], w_ref, m_ref, v_ref, *outs)

    blk = pl.BlockSpec((r, cols), lambda i: (0, i))
    gblk = pl.BlockSpec((ALIGNED_W, cols), lambda i: (0, i))
    oblk = pl.BlockSpec((r, 1, cols), lambda i: (0, 0, i))
    shp = jax.ShapeDtypeStruct((r, 1, d), F32)
    return pl.pallas_call(
        body, name="adamw_w_in", grid=(d // cols,),
        in_specs=[blk] * 3 + [gblk] * 2, out_specs=[oblk] * 4, out_shape=[shp] * 4,
        scratch_shapes=[pltpu.VMEM((ALIGNED_W, cols), F32)],
        compiler_params=_params(("parallel",), 24 * ALIGNED_W * cols * 4 + 8 * 2**20),
    )(wt, mt, vt, g1, g2)


def _pad_lanes(a, width):
    return jnp.pad(a, ((0, 0), (0, width - a.shape[1])))


def _gathered_to_full(g):
    return jnp.transpose(g, (1, 0, 2)).reshape(g.shape[1], 4 * g.shape[2])


def _row(a):
    return _pad_lanes(a.reshape(1, -1), 1024)


def _small_pack(nin, cb, fn, al, dt, gn, cqw_shard, cw_shard):
    ad = jnp.concatenate([al.reshape(1, -1), dt.reshape(1, -1)], axis=1)
    rows = [_row(nin), _row(cb), _row(fn), _row(ad), _row(gn), cqw_shard.reshape(3, 1024), _row(cw_shard)]
    out = jnp.concatenate(rows, axis=0)
    return jnp.pad(out, ((0, 16 - out.shape[0]), (0, 0)))


def kernel(x, norm_in_w, w_in, conv_qkv_w, A_log, dt_bias, gdn_norm_w, conv_w, conv_b, w_out, final_norm_w, loss_target, m_norm_in_w, m_w_in, m_conv_qkv_w, m_A_log, m_dt_bias, m_gdn_norm_w, m_conv_w, m_conv_b, m_w_out, m_final_norm_w, v_norm_in_w, v_w_in, v_conv_qkv_w, v_A_log, v_dt_bias, v_gdn_norm_w, v_conv_w, v_conv_b, v_w_out, v_final_norm_w):
    chip = 2 * lax.axis_index("x") + lax.axis_index("y")
    a_shard = _align_shard(jnp.transpose(w_in, (2, 0, 1)))
    d_model = x.shape[-1]
    stack = lambda s: lax.empty((4,) + s.shape, s.dtype)
    wg0 = lax.empty((WG_BLOCKS * DH, d_model), BF16)
    wc0 = lax.empty((WC_BLOCKS * DH, d_model), BF16)
    ss_g, rs_g, bufs_g, tok_g = _gather_start("g", a_shard, wg0, [(conv_qkv_w[0], stack(conv_qkv_w[0]))])
    wo_b = _cast_bf16(w_out[0], 256, "cast_w_out", tok_g)
    ss_c, rs_c, bufs_c, tok_c = _gather_start("c", bufs_g[0], wc0,
                                              [(conv_w[0], stack(conv_w[0])), (wo_b, stack(wo_b))])
    wg1, wc1, wog1, cqg1, cwg1 = _place_own(bufs_c[0], bufs_c[4], bufs_g[2], bufs_c[2],
                                            [bufs_g[1], bufs_c[1], bufs_c[5], bufs_g[3], bufs_c[3]])
    x0 = x[0]
    h = _rms_in(x0, _tie(_tie(norm_in_w, tok_g, "after_gather_start_g"), tok_c, "after_gather_start_c"))
    adam_in = [jnp.transpose(a[0]) for a in (w_in, m_w_in, v_w_in)]
    sp = lambda nin, cb, fn, al, dt, gn, cq, cwv: _small_pack(nin, cb, fn, al, dt, gn, cq[0], cwv[0])
    w_s = sp(norm_in_w, conv_b, final_norm_w, A_log, dt_bias, gdn_norm_w, conv_qkv_w, conv_w)
    m_s = sp(m_norm_in_w, m_conv_b, m_final_norm_w, m_A_log, m_dt_bias, m_gdn_norm_w, m_conv_qkv_w, m_conv_w)
    v_s = sp(v_norm_in_w, v_conv_b, v_final_norm_w, v_A_log, v_dt_bias, v_gdn_norm_w, v_conv_qkv_w, v_conv_w)
    a_thru, wg, _, cq_g = _gather_wait("g", ss_g, rs_g, [bufs_c[0], wg1, bufs_g[2], cqg1],
                                       [h, w_s, m_s, v_s] + adam_in[1:])
    w_g = _merge_edges(_sibling_forward("g", wg), G_EDGE, G_MIXED, "merge_edges_g")
    cqw = _gathered_to_full(cq_g)
    ad = jnp.pad(jnp.concatenate([A_log, dt_bias], axis=0), ((0, 0), (A_LANE, 0)))
    fwd_c = {}

    def on_q(q):
        _, wc, _, cw_g, _, wo_g = _gather_wait("c", ss_c, rs_c,
                                               [a_thru, wc1, bufs_c[2], cwg1, bufs_c[4], wog1], q)
        issue, _ = _sibling_forward_parts("c")
        ss, rs, (wc,), tok = _split_start("sibling_forward_start_c", issue, [wc], 3)
        fwd_c.update(ss=ss, rs=rs, wc=wc, cw_g=cw_g, wo_g=wo_g)
        return _tie(q, tok, "after_sibling_forward_start_c")

    def late(o):
        _, await_ = _sibling_forward_parts("c")
        (wc,) = _split_wait("sibling_forward_wait_c", await_, fwd_c["ss"], fwd_c["rs"], [fwd_c["wc"]], o)
        return (_merge_edges(wc, C_EDGE, C_MIXED, "merge_edges_c"), fwd_c["wo_g"].reshape(2 * GW, d_model),
                _gathered_to_full(fwd_c["cw_g"]))

    scat = {}

    def on_grad_c(g_c, g_wout, do):
        go4 = g_wout.reshape(4, GW // 2, d_model)
        land = lax.empty((3, ALIGNED_W, d_model), BF16)
        land_o = lax.empty((3, GW // 2, d_model), BF16)
        ss, rs, bufs, tok = _scatter_start("c", g_c, land, [(go4, land_o)])
        scat["c"] = (ss, rs, bufs)
        return _tie(do, tok, "after_scatter_start_c")

    def on_grad_g(g_g, dproj_g):
        ss, rs, bufs, tok = _scatter_start("g", _pair_reduce("g", g_g), scat["c"][2][1], [], halved=True)
        scat["g"] = (ss, rs, bufs)
        return _tie(dproj_g, tok, "after_scatter_start_g")

    gx, sm, _ = _local_step(x0, loss_target[0], h, w_g, cqw, late, norm_in_w, ad, gdn_norm_w, conv_b,
                            final_norm_w.reshape(1, -1), on_grad_c, on_grad_g, on_q)

    ss, rs, bufs = scat["c"]
    g_c, land, go4, land_o = _scatter_wait("c", ss, rs, [bufs[0], scat["g"][2][1], bufs[2], bufs[3]], gx)
    part_out = _sum_rows(go4, land_o, 128)
    ad_g = jnp.concatenate([sm["al"][:, A_LANE:], sm["dt"][:, A_LANE:]], axis=1)
    pack = jnp.concatenate([_row(sm["nin"]), _row(sm["cb"]), _row(sm["fn"]), _row(ad_g), _row(sm["gn"]),
                            jnp.concatenate(sm["cq"], axis=1).reshape(12, 1024), sm["cw"], _row(sm["loss"])], axis=0)
    pack = jnp.pad(pack, ((0, PACK_ROWS - pack.shape[0]), (0, 0)))
    issue, await_a, nsem = _exchange_parts(1, True)
    ss_a, rs_a, bufs_a, tok_a = _split_start(
        "exchange_start_small", issue,
        [part_out, lax.empty(part_out.shape, F32), pack, lax.empty((8,) + pack.shape, F32)], nsem)
    part_out, sib_out, pack, packs = _split_wait("exchange_wait_small", await_a, ss_a, rs_a, bufs_a, tok_a)
    tot = _sum_packs(pack, packs)
    g_cq_sh = lax.dynamic_slice_in_dim(tot[R_CQ:R_CQ + 12].reshape(4, 3 * GW), chip * 768, 768, axis=1)
    g_cw_sh = lax.dynamic_slice_in_dim(tot[R_CW:R_CW + 3], chip * 256, 256, axis=1)
    g_s = _small_pack(tot[R_NIN], tot[R_CB], tot[R_FN], tot[R_AD, :HEADS], tot[R_AD, HEADS:2 * HEADS],
                      tot[R_GN, :DH], g_cq_sh, g_cw_sh)
    small = _adamw(w_s, m_s, v_s, g_s, None, 16, "adamw_small")
    ss, rs, bufs = scat["g"]
    g_g, land = _scatter_wait("g", ss, rs, [bufs[0], land], [gx] + list(small), halved=True)
    part_in = _sum_shard(g_g, g_c, land)
    issue, await_b, nsem = _exchange_parts(1, False)
    ss_b, rs_b, bufs_b, tok_b = _split_start("exchange_start_w_in", issue,
                                             [part_in, lax.empty(part_in.shape, F32)], nsem)
    part_out = _tie(part_out, tok_b, "after_exchange_start_w_in")
    g_wo, d_wo, m_wo, v_wo = _adamw(w_out[0], m_w_out[0], v_w_out[0], part_out, sib_out, 128, "adamw_w_out")
    part_in, sib_in = _split_wait("exchange_wait_w_in", await_b, ss_b, rs_b, bufs_b, d_wo)
    g_wi, d_wi, m_wi, v_wi = [jnp.transpose(a, (1, 2, 0))[0] for a in _adamw_shard(*adam_in, part_in, sib_in)]

    def unpack(a, big_in, big_out):
        return (a[0:1], big_in[None], a[5:8].reshape(1, 4, 768), a[3:4, :HEADS], a[3:4, HEADS:2 * HEADS],
                a[4:5, :DH], a[8, :768].reshape(1, 3, 256), a[1:2], big_out[None], a[2])

    loss = tot[R_LOSS, 0]
    return (loss, gx[None], *unpack(small[0], g_wi, g_wo), *unpack(small[1], d_wi, d_wo),
            *unpack(small[2], m_wi, m_wo), *unpack(small[3], v_wi, v_wo))
```

```python
import jax
import jax.numpy as jnp
from jax import lax
from jax.experimental import pallas as pl
from jax.experimental.pallas import tpu as pltpu

F32 = jnp.float32
BF16 = jnp.bfloat16
MESH = pl.DeviceIdType.MESH
ANY = pl.BlockSpec(memory_space=pl.ANY)

HEADS = 8
DH = 128
CH = 64
GW = HEADS * DH
EPS = 1e-6
VMEM_V7X = 64 * 1024 * 1024

QB, KB, VB, ZB, BAB = 0, 8, 16, 24, 32
A_LANE = 120
NG, NC = 33, 32
GW_COLS, CW_COLS = NG * DH, NC * DH

SHARD_W = 2052
ALIGNED_BLOCKS = 17
ALIGNED_W = ALIGNED_BLOCKS * DH
SHIFTS = (0, 4, ALIGNED_W - 8, ALIGNED_W - 4)
G_EDGE, C_EDGE = 34, 32
G_SPARE, C_SPARE = 33, 34
WG_BLOCKS, WC_BLOCKS = 38, 36
G_MIXED, C_MIXED = (2, BAB), (4 * 7 + 1,)


def _shard_blocks(chip, edges):
    g, c = "g", "c"
    if chip == 0:
        out = [(g, 3 * b) for b in range(8)] + [(g, 3 * b + 1) for b in range(8)] + [(g, G_EDGE, G_MIXED[0])]
    elif chip == 1:
        out = [(g, G_EDGE + 1, G_MIXED[0])] + [(g, 3 * b + 2) for b in range(1, 8)]
        out += [(g, ZB + b) for b in range(8)] + [(g, G_EDGE + 2, G_MIXED[1])]
    elif chip == 2:
        out = [(c, 4 * b) for b in range(8)] + [(c, 4 * b + 1) for b in range(7)]
        out += [(c, C_EDGE, C_MIXED[0]), (g, G_EDGE + 3, G_MIXED[1])]
    else:
        out = [(c, 4 * b + 2) for b in range(8)] + [(c, 4 * b + 3) for b in range(8)] + [(c, C_EDGE + 1, C_MIXED[0])]
    return [(o[0], o[1] if (edges or len(o) == 2) else o[2]) for o in out]


def _by_chip(chip, vals):
    if all(v == vals[0] for v in vals):
        return vals[0]
    r = vals[3]
    for kk in (2, 1, 0):
        r = jnp.where(chip == kk, vals[kk], r)
    return r

ADAM_LR, ADAM_B1, ADAM_B2, ADAM_EPS, ADAM_WD, ADAM_STEP = 0.001, 0.9, 0.999, 1e-08, 0.01, 10

R_NIN, R_CB, R_FN, R_AD, R_GN, R_CQ, R_CW, R_LOSS, PACK_ROWS = 0, 1, 2, 3, 4, 5, 17, 20, 24

NN = ((1,), (0,))
NT = ((1,), (1,))
TN = ((0,), (0,))


def _dot(a, b, dims=NN, mode="lo"):
    dn = (dims, ((), ()))
    if mode == "hi":
        return lax.dot_general(a, b, dn, precision=lax.Precision.HIGHEST, preferred_element_type=F32)
    ah, bh = a.astype(BF16), b.astype(BF16)
    out = lax.dot_general(ah, bh, dn, preferred_element_type=F32)
    if mode == "x3":
        al = (a - ah.astype(F32)).astype(BF16)
        bl = (b - bh.astype(F32)).astype(BF16)
        out = out + lax.dot_general(ah, bl, dn, preferred_element_type=F32)
        out = out + lax.dot_general(al, bh, dn, preferred_element_type=F32)
    return out


P_GRAM, P_INV, P_SOL, P_SCAN, P_SCANB, P_BWD = "lo", "lo", "lo", "lo", "lo", "lo"
P_CUM = "x3"


def _params(sem=None, vmem=None):
    kw = {}
    if sem is not None:
        kw["dimension_semantics"] = sem
    if vmem is not None:
        kw["vmem_limit_bytes"] = int(min(max(vmem, 32 * 2**20), VMEM_V7X - 8 * 2**20))
    return pltpu.CompilerParams(**kw)


def _in_hbm(*arrays):
    return [pltpu.with_memory_space_constraint(a, pltpu.HBM) for a in arrays]


def _sigmoid(x):
    return 1.0 / (1.0 + jnp.exp(-x))


def _dsilu(x, s):
    return s * (1.0 + x * (1.0 - s))


def _rows(shape):
    return lax.broadcasted_iota(jnp.int32, shape, 0)


def _shift_down(x, s):
    if s == 0:
        return x
    return jnp.where(_rows(x.shape) >= s, pltpu.roll(x, s, 0), 0.0)


def _shift_up(x, s):
    if s == 0:
        return x
    n = x.shape[0]
    return jnp.where(_rows(x.shape) < n - s, pltpu.roll(x, n - s, 0), 0.0)


def _matmul(a, b, dims, out_dtype, tm, tn, tk, name, add=None, n=None, b_outer=False):
    if dims == NN:
        (m, k), n = a.shape, b.shape[1]
    elif dims == NT:
        (m, k), n = a.shape, (n or b.shape[0])
    else:
        (k, m), n = a.shape, b.shape[1]
    tm, tn, tk = min(tm, m), min(tn, n), min(tk, k)
    assert m % tm == 0 and n % tn == 0 and k % tk == 0, (name, m, n, k, tm, tn, tk)
    nk = k // tk

    def body(*refs):
        if add is None:
            a_ref, b_ref, o_ref = refs[:3]
            add_ref = None
        else:
            a_ref, b_ref, add_ref, o_ref = refs[:4]
        part = _dot(a_ref[...], b_ref[...], dims)
        if nk == 1:
            if add_ref is not None:
                part = part + add_ref[...]
            o_ref[...] = part.astype(out_dtype)
            return
        acc = refs[-1]
        kk = pl.program_id(2)

        @pl.when(kk == 0)
        def _():
            acc[...] = part

        @pl.when(kk > 0)
        def _():
            acc[...] += part

        @pl.when(kk == nk - 1)
        def _():
            r = acc[...]
            if add_ref is not None:
                r = r + add_ref[...]
            o_ref[...] = r.astype(out_dtype)

    ij = (lambda g0, g1: (g1, g0)) if b_outer else (lambda g0, g1: (g0, g1))

    def spec(shape, pick):
        return pl.BlockSpec(shape, lambda g0, g1, kk: pick(*ij(g0, g1), kk))

    a_spec = spec((tk, tm), lambda i, j, kk: (kk, i)) if dims == TN else spec((tm, tk), lambda i, j, kk: (i, kk))
    b_spec = spec((tn, tk), lambda i, j, kk: (j, kk)) if dims == NT else spec((tk, tn), lambda i, j, kk: (kk, j))
    o_spec = spec((tm, tn), lambda i, j, kk: (i, j))
    in_specs = [a_spec, b_spec]
    args = [a, b]
    if add is not None:
        in_specs.append(o_spec)
        args.append(add)
    osz = jnp.dtype(out_dtype).itemsize
    est = 2 * (tm * tk * a.dtype.itemsize + tk * tn * b.dtype.itemsize + tm * tn * osz)
    est += 3 * tm * tn * 4 + (2 * tm * tn * 4 if add is not None else 0)
    return pl.pallas_call(
        body, name=name, grid=(n // tn, m // tm, nk) if b_outer else (m // tm, n // tn, nk),
        in_specs=in_specs, out_specs=o_spec,
        out_shape=jax.ShapeDtypeStruct((m, n), out_dtype),
        scratch_shapes=[pltpu.VMEM((tm, tn), F32)] if nk > 1 else [],
        compiler_params=_params(("parallel", "parallel", "arbitrary"), est + 8 * 2**20),
    )(*args)


def _cast_bf16(a, rows, name, after):
    r, c = a.shape
    rows = min(rows, r)

    def body(a_ref, t_ref, o_ref):
        del t_ref
        o_ref[...] = a_ref[...].astype(BF16)

    return pl.pallas_call(
        body, name=name, grid=(r // rows,),
        in_specs=[pl.BlockSpec((rows, c), lambda i: (i, 0)), ANY],
        out_specs=pl.BlockSpec((rows, c), lambda i: (i, 0)),
        out_shape=jax.ShapeDtypeStruct((r, c), BF16),
        compiler_params=_params(("parallel",)),
    )(a, after)


def _align_shard(wt):
    r, _, d = wt.shape
    cols = min(256, d)

    def body(w_ref, o_ref, pad_ref):
        chip = 2 * lax.axis_index("x") + lax.axis_index("y")
        pad_ref[...] = jnp.zeros_like(pad_ref)
        pad_ref[0:r, :] = w_ref[:, 0, :]
        o_ref[...] = pltpu.roll(pad_ref[...], _by_chip(chip, SHIFTS), 0).astype(BF16)

    return pl.pallas_call(
        body, name="align_shard", grid=(d // cols,),
        in_specs=[pl.BlockSpec((r, 1, cols), lambda i: (0, 0, i))],
        out_specs=pl.BlockSpec((ALIGNED_W, cols), lambda i: (0, i)),
        out_shape=jax.ShapeDtypeStruct((ALIGNED_W, d), BF16),
        scratch_shapes=[pltpu.VMEM((ALIGNED_W, cols), F32)],
        compiler_params=_params(("parallel",)),
    )(wt)


def _rms_in(x, w):
    n, d = x.shape
    tr = min(256, n)

    def body(x_ref, w_ref, h_ref):
        xv = x_ref[...]
        r = lax.rsqrt(jnp.mean(xv * xv, axis=-1, keepdims=True) + EPS)
        h_ref[...] = (xv * r * w_ref[...]).astype(BF16)

    return pl.pallas_call(
        body, name="rms_in", grid=(n // tr,),
        in_specs=[pl.BlockSpec((tr, d), lambda i: (i, 0)), pl.BlockSpec((1, d), lambda i: (0, 0))],
        out_specs=pl.BlockSpec((tr, d), lambda i: (i, 0)),
        out_shape=jax.ShapeDtypeStruct((n, d), BF16),
        compiler_params=_params(("parallel",)),
    )(x, w)


def _conv_silu(p, w_ref, taps):
    c = None
    for j in range(taps):
        t = _shift_down(p, taps - 1 - j) * w_ref[j:j + 1, :]
        c = t if c is None else c + t
    return c


def _prep_qkv(proj, cw):
    n = proj.shape[0]

    def body(p3, wq, wk, wv, q_ref, k_ref, v_ref):
        for kind, (w_ref, o_ref) in enumerate(((wq, q_ref), (wk, k_ref), (wv, v_ref))):
            c = _conv_silu(p3[:, kind * DH:(kind + 1) * DH], w_ref, 4)
            a = c * _sigmoid(c)
            if kind < 2:
                r = lax.rsqrt(jnp.sum(a * a, axis=-1, keepdims=True) + EPS)
                a = a * (r * (DH ** -0.5 if kind == 0 else 1.0))
            o_ref[...] = a

    col = pl.BlockSpec((n, DH), lambda h: (0, h))
    wcol = lambda base: pl.BlockSpec((4, DH), lambda h: (0, base + h))
    out = jax.ShapeDtypeStruct((n, GW), F32)
    return pl.pallas_call(
        body, name="prep_qkv", grid=(HEADS,),
        in_specs=[pl.BlockSpec((n, 3 * DH), lambda h: (0, h)), wcol(QB), wcol(KB), wcol(VB)],
        out_specs=[col] * 3, out_shape=[out] * 3,
        compiler_params=_params(("parallel",), 40 * 2**20),
    )(proj, cw, cw, cw)


def _prep_qkv_bwd(proj, cw, dq, dk, dv, dproj):
    n = proj.shape[0]

    def body(p3, wq, wk, wv, dq_ref, dk_ref, dv_ref, _, o3, gq, gk, gv):
        for kind, (w_ref, d_ref, g_ref) in enumerate(((wq, dq_ref, gq), (wk, dk_ref, gk), (wv, dv_ref, gv))):
            p = p3[:, kind * DH:(kind + 1) * DH]
            shifted = [_shift_down(p, 3 - j) for j in range(4)]
            c = shifted[0] * w_ref[0:1, :]
            for j in range(1, 4):
                c = c + shifted[j] * w_ref[j:j + 1, :]
            s = _sigmoid(c)
            a = c * s
            d = d_ref[...]
            if kind < 2:
                r = lax.rsqrt(jnp.sum(a * a, axis=-1, keepdims=True) + EPS)
                sc = DH ** -0.5 if kind == 0 else 1.0
                d = (sc * r) * (d - a * ((r * r) * jnp.sum(d * a, axis=-1, keepdims=True)))
            dc = d * _dsilu(c, s)
            dp = None
            for j in range(4):
                g_ref[j:j + 1, :] = jnp.sum(dc * shifted[j], axis=0, keepdims=True)
                t = _shift_up(dc, 3 - j) * w_ref[j:j + 1, :]
                dp = t if dp is None else dp + t
            o3[:, kind * DH:(kind + 1) * DH] = dp.astype(BF16)

    col = pl.BlockSpec((n, DH), lambda h: (0, h))
    wcol = lambda base: pl.BlockSpec((4, DH), lambda h: (0, base + h))
    p3spec = pl.BlockSpec((n, 3 * DH), lambda h: (0, h))
    return pl.pallas_call(
        body, name="prep_qkv_bwd", grid=(HEADS,),
        in_specs=[p3spec, wcol(QB), wcol(KB), wcol(VB), col, col, col, ANY],
        out_specs=[p3spec] + [wcol(0)] * 3,
        out_shape=[jax.ShapeDtypeStruct(dproj.shape, BF16)] + [jax.ShapeDtypeStruct((4, GW), F32)] * 3,
        input_output_aliases={7: 0},
        compiler_params=_params(("parallel",), 48 * 2**20),
    )(proj, cw, cw, cw, dq, dk, dv, dproj)


CPB = 8
SCAN_CPS = 4


def _tri(lower, rows):
    i = lax.broadcasted_iota(jnp.int32, (rows, rows), 0)
    j = lax.broadcasted_iota(jnp.int32, (rows, rows), 1)
    return jnp.where((i // CH == j // CH) & ((i >= j) if lower else (j >= i)), 1.0, 0.0)


def _lane(shape):
    return lax.broadcasted_iota(jnp.int32, shape, 1)


def _prep_bg(proj, ad):
    n = proj.shape[0]
    nch = n // CH
    cpb = CPB if nch % CPB == 0 else 1
    rows = cpb * CH

    def body(p_ref, ad_ref, bg_ref, bgt_ref):
        p = p_ref[...]
        lane = _lane(p.shape)
        beta = _sigmoid(p)
        xa = p + ad_ref[1:2, :]
        sp = jnp.maximum(xa, 0.0) + jnp.log(1.0 + jnp.exp(-jnp.abs(xa)))
        g = pltpu.roll(-jnp.exp(ad_ref[0:1, :]) * sp, DH - A_LANE + HEADS, 1)
        gc = _dot(_tri(True, rows), g, NN, P_CUM)
        bg = jnp.where(lane < HEADS, beta, jnp.where(lane < 2 * HEADS, gc, 0.0))
        bg_ref[...] = bg
        for ci in range(cpb):
            bgt_ref[ci] = bg[ci * CH:(ci + 1) * CH, :].T

    return pl.pallas_call(
        body, name="prep_bg", grid=(nch // cpb,),
        in_specs=[pl.BlockSpec((rows, DH), lambda i: (i, BAB)), pl.BlockSpec((2, DH), lambda i: (0, 0))],
        out_specs=[pl.BlockSpec((rows, DH), lambda i: (i, 0)), pl.BlockSpec((cpb, DH, CH), lambda i: (i, 0, 0))],
        out_shape=[jax.ShapeDtypeStruct((n, DH), F32), jax.ShapeDtypeStruct((nch, DH, CH), F32)],
        compiler_params=_params(("parallel",)),
    )(*_in_hbm(proj, ad))


def _prep_bg_bwd(proj, ad, dbg, dproj):
    n = proj.shape[0]
    nch = n // CH
    cpb = CPB if nch % CPB == 0 else 1
    rows = cpb * CH

    def body(p_ref, ad_ref, d_ref, _, o_ref, ga_ref, gd_ref):
        p = p_ref[...]
        d = d_ref[...]
        lane = _lane(p.shape)
        beta = _sigmoid(p)
        xa = p + ad_ref[1:2, :]
        sp = jnp.maximum(xa, 0.0) + jnp.log(1.0 + jnp.exp(-jnp.abs(xa)))
        na = -jnp.exp(ad_ref[0:1, :])
        dg = pltpu.roll(_dot(_tri(False, rows), d, NN, P_CUM), A_LANE - HEADS, 1)
        da = dg * na * _sigmoid(xa)
        is_g = lane >= A_LANE
        o_ref[...] = jnp.where(lane < HEADS, d * beta * (1.0 - beta), jnp.where(is_g, da, 0.0)).astype(BF16)
        ga = jnp.sum(jnp.where(is_g, dg * na * sp, 0.0), axis=0, keepdims=True)
        gd = jnp.sum(jnp.where(is_g, da, 0.0), axis=0, keepdims=True)

        @pl.when(pl.program_id(0) == 0)
        def _():
            ga_ref[...] = jnp.zeros_like(ga_ref)
            gd_ref[...] = jnp.zeros_like(gd_ref)

        ga_ref[...] += ga
        gd_ref[...] += gd

    one = pl.BlockSpec((1, DH), lambda i: (0, 0))
    return pl.pallas_call(
        body, name="prep_bg_bwd", grid=(nch // cpb,),
        in_specs=[pl.BlockSpec((rows, DH), lambda i: (i, BAB)), pl.BlockSpec((2, DH), lambda i: (0, 0)),
                  pl.BlockSpec((rows, DH), lambda i: (i, 0)), ANY],
        out_specs=[pl.BlockSpec((rows, DH), lambda i: (i, BAB)), one, one],
        out_shape=[jax.ShapeDtypeStruct(dproj.shape, BF16), jax.ShapeDtypeStruct((1, DH), F32),
                   jax.ShapeDtypeStruct((1, DH), F32)],
        input_output_aliases={3: 0},
        compiler_params=_params(("arbitrary",)),
    )(proj, ad, dbg, dproj)


def _gdn_out(o, proj, wg):
    n = o.shape[0]

    def body(o_ref, z_ref, w_ref, y_ref):
        ov, z = o_ref[...], z_ref[...]
        r = lax.rsqrt(jnp.mean(ov * ov, axis=-1, keepdims=True) + EPS)
        y_ref[...] = (ov * r * w_ref[...] * (z * _sigmoid(z))).astype(BF16)

    return pl.pallas_call(
        body, name="gdn_out", grid=(HEADS,),
        in_specs=[pl.BlockSpec((n, DH), lambda h: (0, h)), pl.BlockSpec((n, DH), lambda h: (0, ZB + h)),
                  pl.BlockSpec((1, DH), lambda h: (0, 0))],
        out_specs=pl.BlockSpec((n, DH), lambda h: (0, h)),
        out_shape=jax.ShapeDtypeStruct((n, 2 * GW), BF16),
        compiler_params=_params(("parallel",)),
    )(o, proj, wg)


def _gdn_out_bwd(o, proj, wg, dout_b, w_out):
    n = o.shape[0]
    d_model = dout_b.shape[1]

    def body(o_ref, z_ref, w_ref, g_ref, wo_ref, do_ref, dz_ref, gw_ref):
        ov, z, w = o_ref[...], z_ref[...], w_ref[...]
        d = _dot(g_ref[...], wo_ref[...], NT)
        r = lax.rsqrt(jnp.mean(ov * ov, axis=-1, keepdims=True) + EPS)
        nrm = ov * r
        s = _sigmoid(z)
        dz_ref[...] = (d * (nrm * w) * _dsilu(z, s)).astype(BF16)
        dn_w = d * (z * s)
        gw = jnp.sum(dn_w * nrm, axis=0, keepdims=True)
        dn = dn_w * w
        do_ref[...] = (r * (dn - nrm * jnp.mean(dn * nrm, axis=-1, keepdims=True))).astype(BF16)

        @pl.when(pl.program_id(0) == 0)
        def _():
            gw_ref[...] = jnp.zeros_like(gw_ref)

        gw_ref[...] += gw

    return pl.pallas_call(
        body, name="gdn_out_bwd", grid=(HEADS,),
        in_specs=[pl.BlockSpec((n, DH), lambda h: (0, h)), pl.BlockSpec((n, DH), lambda h: (0, ZB + h)),
                  pl.BlockSpec((1, DH), lambda h: (0, 0)), pl.BlockSpec((n, d_model), lambda h: (0, 0)),
                  pl.BlockSpec((DH, d_model), lambda h: (h, 0))],
        out_specs=[pl.BlockSpec((n, DH), lambda h: (0, h)), pl.BlockSpec((n, DH), lambda h: (0, ZB + h)),
                   pl.BlockSpec((1, DH), lambda h: (0, 0))],
        out_shape=[jax.ShapeDtypeStruct((n, GW), BF16), jax.ShapeDtypeStruct((n, GW_COLS), BF16),
                   jax.ShapeDtypeStruct((1, DH), F32)],
        compiler_params=_params(("arbitrary",), 40 * 2**20),
    )(o, proj, wg, dout_b, w_out)


def _conv_branch(proj, w3, b, mix):
    n = proj.shape[0]

    def body(p4, w_ref, b_ref, _, y_ref):
        u = p4[:, DH:2 * DH] * p4[:, 2 * DH:3 * DH]
        cc = _conv_silu(u, w_ref, 3) + b_ref[...]
        z = p4[:, 3 * DH:4 * DH]
        y_ref[...] = (p4[:, 0:DH] * cc * (z * _sigmoid(z))).astype(BF16)

    return pl.pallas_call(
        body, name="conv_branch", grid=(HEADS,),
        in_specs=[pl.BlockSpec((n, 4 * DH), lambda h: (0, h)), pl.BlockSpec((3, DH), lambda h: (0, h)),
                  pl.BlockSpec((1, DH), lambda h: (0, h)), ANY],
        out_specs=pl.BlockSpec((n, DH), lambda h: (0, HEADS + h)),
        out_shape=jax.ShapeDtypeStruct(mix.shape, BF16),
        input_output_aliases={3: 0},
        compiler_params=_params(("parallel",), 40 * 2**20),
    )(*_in_hbm(proj, w3, b, mix))


def _conv_branch_bwd(proj, w3, b, dout_b, w_out):
    n = proj.shape[0]
    d_model = dout_b.shape[1]

    def body(p4, w_ref, b_ref, g_ref, wo_ref, o4, gw_ref, gbias_ref):
        gb, gcv, hc, z = p4[:, 0:DH], p4[:, DH:2 * DH], p4[:, 2 * DH:3 * DH], p4[:, 3 * DH:4 * DH]
        d = _dot(g_ref[...], wo_ref[...], NT)
        dgb, dgc, dhc, dzc = (o4.at[:, kk * DH:(kk + 1) * DH] for kk in range(4))
        u = gcv * hc
        cc = _conv_silu(u, w_ref, 3) + b_ref[...]
        s = _sigmoid(z)
        dzc[...] = (d * (gb * cc) * _dsilu(z, s)).astype(BF16)
        dp = d * (z * s)
        dgb[...] = (dp * cc).astype(BF16)
        dcc = dp * gb
        gbias_ref[...] = jnp.sum(dcc, axis=0, keepdims=True)
        du = None
        for j in range(3):
            gw_ref[j:j + 1, :] = jnp.sum(dcc * _shift_down(u, 2 - j), axis=0, keepdims=True)
            t = _shift_up(dcc, 2 - j) * w_ref[j:j + 1, :]
            du = t if du is None else du + t
        dgc[...] = (du * hc).astype(BF16)
        dhc[...] = (du * gcv).astype(BF16)

    p4spec = pl.BlockSpec((n, 4 * DH), lambda h: (0, h))
    return pl.pallas_call(
        body, name="conv_branch_bwd", grid=(HEADS,),
        in_specs=[p4spec, pl.BlockSpec((3, DH), lambda h: (0, h)), pl.BlockSpec((1, DH), lambda h: (0, h)),
                  pl.BlockSpec((n, d_model), lambda h: (0, 0)), pl.BlockSpec((DH, d_model), lambda h: (HEADS + h, 0))],
        out_specs=[p4spec, pl.BlockSpec((3, DH), lambda h: (0, h)), pl.BlockSpec((1, DH), lambda h: (0, h))],
        out_shape=[jax.ShapeDtypeStruct((n, CW_COLS), BF16), jax.ShapeDtypeStruct((3, GW), F32),
                   jax.ShapeDtypeStruct((1, GW), F32)],
        compiler_params=_params(("parallel",), 52 * 2**20),
    )(proj, w3, b, dout_b, w_out)


def _out_loss(mix, w_out, x, tgt, wf):
    n, d = x.shape
    kdim = mix.shape[1]
    tr = min(256, n)

    def body(m_ref, wo_ref, x_ref, t_ref, w_ref, do_ref, dob_ref, gw_ref, loss_ref):
        ov = _dot(m_ref[...], wo_ref[...], NN) + x_ref[...]
        w = w_ref[...]
        r = lax.rsqrt(jnp.mean(ov * ov, axis=-1, keepdims=True) + EPS)
        nrm = ov * r
        e = nrm * w - t_ref[...]
        dy = e * (1.0 / d)
        dn = dy * w
        dout = r * (dn - nrm * jnp.mean(dn * nrm, axis=-1, keepdims=True))
        do_ref[...] = dout
        dob_ref[...] = dout.astype(BF16)

        @pl.when(pl.program_id(0) == 0)
        def _():
            gw_ref[...] = jnp.zeros_like(gw_ref)
            loss_ref[...] = jnp.zeros_like(loss_ref)

        gw_ref[...] += jnp.sum(dy * nrm, axis=0, keepdims=True)
        loss_ref[...] += (0.5 / d) * jnp.sum(jnp.sum(e * e, axis=-1, keepdims=True), axis=0, keepdims=True)

    row = pl.BlockSpec((tr, d), lambda i: (i, 0))
    return pl.pallas_call(
        body, name="out_loss", grid=(n // tr,),
        in_specs=[pl.BlockSpec((tr, kdim), lambda i: (i, 0)), pl.BlockSpec((kdim, d), lambda i: (0, 0)), row, row,
                  pl.BlockSpec((1, d), lambda i: (0, 0))],
        out_specs=[row, row, pl.BlockSpec((1, d), lambda i: (0, 0)), pl.BlockSpec((1, 1), lambda i: (0, 0))],
        out_shape=[jax.ShapeDtypeStruct((n, d), F32), jax.ShapeDtypeStruct((n, d), BF16),
                   jax.ShapeDtypeStruct((1, d), F32), jax.ShapeDtypeStruct((1, 1), F32)],
        compiler_params=_params(("arbitrary",), 40 * 2**20),
    )(mix, w_out, x, tgt, wf)


def _dh_rms_bwd(dproj, w_t, dh0, x, w, dout, tk):
    n, d = x.shape
    kdim = dproj.shape[1]
    tm = min(1024, n)
    tk = min(tk, kdim)
    nk = kdim // tk

    def body(a_ref, b_ref, dh0_ref, x_ref, w_ref, do_ref, dx_ref, gw_ref, acc):
        i, kk = pl.program_id(0), pl.program_id(1)
        part = _dot(a_ref[...], b_ref[...], NN)

        @pl.when(kk == 0)
        def _():
            acc[...] = part + dh0_ref[...]

        @pl.when(kk > 0)
        def _():
            acc[...] += part

        @pl.when((i == 0) & (kk == 0))
        def _():
            gw_ref[...] = jnp.zeros_like(gw_ref)

        @pl.when(kk == nk - 1)
        def _():
            xv, dhv = x_ref[...], acc[...]
            r = lax.rsqrt(jnp.mean(xv * xv, axis=-1, keepdims=True) + EPS)
            xn = xv * r
            dxn = dhv * w_ref[...]
            dx_ref[...] = r * (dxn - xn * jnp.mean(dxn * xn, axis=-1, keepdims=True)) + do_ref[...]
            gw_ref[...] += jnp.sum(dhv * xn, axis=0, keepdims=True)

    row = pl.BlockSpec((tm, d), lambda i, kk: (i, 0))
    one = pl.BlockSpec((1, d), lambda i, kk: (0, 0))
    return pl.pallas_call(
        body, name="dh_rms_bwd", grid=(n // tm, nk),
        in_specs=[pl.BlockSpec((tm, tk), lambda i, kk: (i, kk)), pl.BlockSpec((tk, d), lambda i, kk: (kk, 0)),
                  row, row, one, row],
        out_specs=[row, one],
        out_shape=[jax.ShapeDtypeStruct((n, d), F32), jax.ShapeDtypeStruct((1, d), F32)],
        scratch_shapes=[pltpu.VMEM((tm, d), F32)],
        compiler_params=_params(("arbitrary", "arbitrary"), 56 * 2**20),
    )(dproj, w_t, dh0, x, w, dout)


def _ij():
    i = lax.broadcasted_iota(jnp.int32, (CH, CH), 0)
    j = lax.broadcasted_iota(jnp.int32, (CH, CH), 1)
    return i, j


def _unit_lower_inverse(mats):
    i, j = _ij()
    eye = jnp.where(i == j, 1.0, 0.0)
    same16 = (i // 16) == (j // 16)
    same32 = (i // 32) == (j // 32)
    mm = lambda xs, ys: [_dot(x, y, NN, P_INV) for x, y in zip(xs, ys)]
    n1 = [jnp.where(same16, -a, 0.0) for a in mats]
    n2 = mm(n1, n1)
    n4 = mm(n2, n2)
    n8 = mm(n4, n4)
    t = [eye + x1 + x2 + x3 for x1, x2, x3 in zip(n1, n2, mm(n1, n2))]
    t = [x + y for x, y in zip(t, mm(t, n4))]
    t = [x + y for x, y in zip(t, mm(t, n8))]
    a1 = [jnp.where(same32 & jnp.logical_not(same16), a, 0.0) for a in mats]
    t = [x - y for x, y in zip(t, mm(t, mm(a1, t)))]
    a2 = [jnp.where(same32, 0.0, a) for a in mats]
    t = [x - y for x, y in zip(t, mm(t, mm(a2, t)))]
    return t


def _head_vectors(bg, bgt, h):
    bcol = bg[:, h:h + 1]
    gcol = bg[:, HEADS + h:HEADS + h + 1]
    grow = bgt[HEADS + h:HEADS + h + 1, :]
    return bcol, gcol, grow


def _decay(gcol, grow):
    i, j = _ij()
    return jnp.where(i >= j, jnp.exp(jnp.where(i >= j, gcol - grow, 0.0)), 0.0)


def _gdn_intra(q, k, v, bg, bgt):
    n = q.shape[0]
    nch = n // CH
    cps = 4 if nch % 4 == 0 else 1

    def body(q_ref, k_ref, v_ref, bg_ref, bgt_ref, u_ref, w_ref, p_ref, t_ref):
        i, j = _ij()
        items = [(ci, h) for ci in range(cps) for h in range(HEADS)]
        at = lambda ref, ci, h: ref.at[ci * CH:(ci + 1) * CH, h * DH:(h + 1) * DH]
        bgs = [bg_ref[ci * CH:(ci + 1) * CH, :] for ci in range(cps)]
        ks = [at(k_ref, ci, h)[...] for ci, h in items]
        vecs = [_head_vectors(bgs[ci], bgt_ref[ci], h) for ci, h in items]
        decs = [_decay(gcol, grow) for _, gcol, grow in vecs]
        kks = [_dot(kh, kh, NT, P_GRAM) for kh in ks]
        qks = [_dot(at(q_ref, ci, h)[...], kh, NT, P_GRAM) for (ci, h), kh in zip(items, ks)]
        ts = _unit_lower_inverse([jnp.where(i > j, bcol * kk * dec, 0.0)
                                  for (bcol, _, _), kk, dec in zip(vecs, kks, decs)])
        us = [_dot(t, at(v_ref, ci, h)[...] * bcol, NN, P_SOL) for t, (ci, h), (bcol, _, _) in zip(ts, items, vecs)]
        ws = [_dot(t, kh * (bcol * jnp.exp(gcol)), NN, P_SOL) for t, kh, (bcol, gcol, _) in zip(ts, ks, vecs)]
        for n_, (ci, h) in enumerate(items):
            p_ref[ci, h] = qks[n_] * decs[n_]
            t_ref[ci, h] = ts[n_].astype(BF16)
            at(u_ref, ci, h)[...] = us[n_]
            at(w_ref, ci, h)[...] = ws[n_].astype(BF16)

    row = pl.BlockSpec((cps * CH, GW), lambda c: (c, 0))
    sq = pl.BlockSpec((cps, HEADS, CH, CH), lambda c: (c, 0, 0, 0))
    big = jax.ShapeDtypeStruct((n, GW), F32)
    sqs = jax.ShapeDtypeStruct((nch, HEADS, CH, CH), F32)
    return pl.pallas_call(
        body, name="gdn_intra", grid=(nch // cps,),
        in_specs=[row, row, row, pl.BlockSpec((cps * CH, DH), lambda c: (c, 0)),
                  pl.BlockSpec((cps, DH, CH), lambda c: (c, 0, 0))],
        out_specs=[row, row, sq, sq],
        out_shape=[big, jax.ShapeDtypeStruct((n, GW), BF16), sqs, jax.ShapeDtypeStruct(sqs.shape, BF16)],
        compiler_params=_params(("parallel",)),
    )(q, k, v, bg, bgt)


def _gdn_scan(q, k, bg, u, w, p):
    n = q.shape[0]
    nch = n // CH
    cps = SCAN_CPS if nch % SCAN_CPS == 0 else 1

    def body(q_ref, k_ref, bg_ref, u_ref, w_ref, p_ref, o_ref, vn_ref, s_out, s_scr):
        @pl.when(pl.program_id(0) == 0)
        def _():
            s_scr[...] = jnp.zeros_like(s_scr)

        hs = range(HEADS)
        sls = [slice(h * DH, (h + 1) * DH) for h in hs]
        ss = [s_scr[h] for h in hs]
        for ci in range(cps):
            rs = slice(ci * CH, (ci + 1) * CH)
            bg = bg_ref[rs, :]
            gcols = [bg[:, HEADS + h:HEADS + h + 1] for h in hs]
            glasts = [g[CH - 1:CH, :] for g in gcols]
            wss = [_dot(w_ref[rs, sl], s, NN, P_SCAN) for sl, s in zip(sls, ss)]
            oqs = [_dot(q_ref[rs, sl] * jnp.exp(g), s, NN, P_SCAN) for sl, s, g in zip(sls, ss, gcols)]
            vns = [u_ref[rs, sl] - x for sl, x in zip(sls, wss)]
            ops = [_dot(p_ref[ci, h], vn, NN, P_SCAN) for h, vn in zip(hs, vns)]
            sns = [_dot(k_ref[rs, sl] * jnp.exp(gl - g), vn, TN, P_SCAN)
                   for sl, gl, g, vn in zip(sls, glasts, gcols, vns)]
            for h, sl in enumerate(sls):
                s_out[ci, :, sl] = ss[h].astype(BF16)
                vn_ref[rs, sl] = vns[h].astype(BF16)
                o_ref[rs, sl] = oqs[h] + ops[h]
            ss = [s * jnp.exp(gl) + sn for s, gl, sn in zip(ss, glasts, sns)]
        for h in hs:
            s_scr[h] = ss[h]

    row = pl.BlockSpec((cps * CH, GW), lambda c: (c, 0))
    big = jax.ShapeDtypeStruct((n, GW), F32)
    return pl.pallas_call(
        body, name="gdn_scan", grid=(nch // cps,),
        in_specs=[row, row, pl.BlockSpec((cps * CH, DH), lambda c: (c, 0)), row, row,
                  pl.BlockSpec((cps, HEADS, CH, CH), lambda c: (c, 0, 0, 0))],
        out_specs=[row, row, pl.BlockSpec((cps, DH, GW), lambda c: (c, 0, 0))],
        out_shape=[big, jax.ShapeDtypeStruct((n, GW), BF16), jax.ShapeDtypeStruct((nch, DH, GW), BF16)],
        scratch_shapes=[pltpu.VMEM((HEADS, DH, DH), F32)],
        compiler_params=_params(("arbitrary",)),
    )(q, k, bg, u, w, p)


def _gdn_scan_bwd(q, k, bg, w, p, vn, s_in, do):
    n = q.shape[0]
    nch = n // CH
    cps = SCAN_CPS if nch % SCAN_CPS == 0 else 1
    rev = lambda c: nch // cps - 1 - c

    def body(q_ref, k_ref, bg_ref, w_ref, p_ref, vn_ref, s_ref, do_ref,
             dqg_ref, dp_ref, du_ref, dw_ref, dks_ref, dgam_ref, ds_scr):
        @pl.when(pl.program_id(0) == 0)
        def _():
            ds_scr[...] = jnp.zeros_like(ds_scr)

        lane = _lane((1, DH))
        hs = range(HEADS)
        sls = [slice(h * DH, (h + 1) * DH) for h in hs]
        dss = [ds_scr[h] for h in hs]
        for ci in reversed(range(cps)):
            rs = slice(ci * CH, (ci + 1) * CH)
            bg = bg_ref[rs, :]
            gcols = [bg[:, HEADS + h:HEADS + h + 1] for h in hs]
            glasts = [g[CH - 1:CH, :] for g in gcols]
            ss = [s_ref[ci, :, sl] for sl in sls]
            dos = [do_ref[rs, sl] for sl in sls]
            vnl = [vn_ref[rs, sl] for sl in sls]
            dqgs = [_dot(d, s, NT, P_SCANB) for d, s in zip(dos, ss)]
            dps = [_dot(d, vn, NT, P_SCANB) for d, vn in zip(dos, vnl)]
            dvn1 = [_dot(p_ref[ci, h], d, TN, P_SCANB) for h, d in zip(hs, dos)]
            dvn2 = [_dot(k_ref[rs, sl] * jnp.exp(gl - g), ds, NN, P_SCANB)
                    for sl, gl, g, ds in zip(sls, glasts, gcols, dss)]
            dkss = [_dot(vn, ds, NT, P_SCANB) for vn, ds in zip(vnl, dss)]
            dsq = [_dot(q_ref[rs, sl] * jnp.exp(g), d, TN, P_SCANB) for sl, g, d in zip(sls, gcols, dos)]
            dvns = [a + b for a, b in zip(dvn1, dvn2)]
            dws = [_dot(dvn, s, NT, P_SCANB) for dvn, s in zip(dvns, ss)]
            dsw = [_dot(w_ref[rs, sl], dvn, TN, P_SCANB) for sl, dvn in zip(sls, dvns)]
            dgam = jnp.zeros((1, DH), F32)
            for h, sl in enumerate(sls):
                dqg_ref[rs, sl] = dqgs[h]
                dp_ref[ci, h] = dps[h]
                du_ref[rs, sl] = dvns[h].astype(BF16)
                dw_ref[rs, sl] = (-dws[h]).astype(BF16)
                dks_ref[rs, sl] = dkss[h]
                tot = jnp.sum(jnp.sum(dss[h] * ss[h], axis=-1, keepdims=True), axis=0, keepdims=True)
                dgam = dgam + jnp.where(lane == h, tot, 0.0)
            dgam_ref[ci] = jnp.broadcast_to(dgam, (8, DH))
            dss = [ds * jnp.exp(gl) + a - b for ds, gl, a, b in zip(dss, glasts, dsq, dsw)]
        for h in hs:
            ds_scr[h] = dss[h]

    row = pl.BlockSpec((cps * CH, GW), lambda c: (rev(c), 0))
    sq =pl.BlockSpec((cps, HEADS, CH, CH), lambda c: (rev(c), 0, 0, 0))
    big = jax.ShapeDtypeStruct((n, GW), F32)
    return pl.pallas_call(
        body, name="gdn_scan_bwd", grid=(nch // cps,),
        in_specs=[row, row, pl.BlockSpec((cps * CH, DH), lambda c: (rev(c), 0)), row, sq, row,
                  pl.BlockSpec((cps, DH, GW), lambda c: (rev(c), 0, 0)), row],
        out_specs=[row, sq, row, row, row, pl.BlockSpec((cps, 8, DH), lambda c: (rev(c), 0, 0))],
        out_shape=[big, jax.ShapeDtypeStruct((nch, HEADS, CH, CH), F32), jax.ShapeDtypeStruct((n, GW), BF16),
                   jax.ShapeDtypeStruct((n, GW), BF16), big,
                   jax.ShapeDtypeStruct((nch, 8, DH), F32)],
        scratch_shapes=[pltpu.VMEM((HEADS, DH, DH), F32)],
        compiler_params=_params(("arbitrary",)),
    )(q, k, bg, w, p, vn, s_in, do)


def _gdn_intra_bwd(q, k, v, bg, bgt, t, u, w, p, dqg, dp, du, dw, dks, dgam):
    n = q.shape[0]
    nch = n // CH
    cps = 2 if nch % 2 == 0 else 1

    def body(q_ref, k_ref, v_ref, bg_ref, bgt_ref, t_ref, u_ref, w_ref, p_ref,
             dqg_ref, dp_ref, du_ref, dw_ref, dks_ref, dgam_ref, dq_ref, dk_ref, dv_ref, dbg_ref):
        i, j = _ij()
        rows1 = lax.broadcasted_iota(jnp.int32, (CH, 1), 0)
        lane = _lane((CH, DH))
        rsum = lambda x: jnp.sum(x, axis=-1, keepdims=True)
        items = [(ci, h) for ci in range(cps) for h in range(HEADS)]
        at = lambda ref, it: ref.at[it[0] * CH:(it[0] + 1) * CH, it[1] * DH:(it[1] + 1) * DH]
        ld = lambda ref: [at(ref, it)[...] for it in items]
        bgs = [bg_ref[ci * CH:(ci + 1) * CH, :] for ci in range(cps)]
        qs, ks = ld(q_ref), ld(k_ref)
        vecs = [_head_vectors(bgs[ci], bgt_ref[ci], h) for ci, h in items]
        decs = [_decay(gcol, grow) for _, gcol, grow in vecs]
        ths = [t_ref[ci, h] for ci, h in items]
        drus = [_dot(th, x_, TN, P_BWD) for th, x_ in zip(ths, ld(du_ref))]
        drws = [_dot(th, x_, TN, P_BWD) for th, x_ in zip(ths, ld(dw_ref))]
        kks = [_dot(kh, kh, NT, P_GRAM) for kh in ks]
        da1 = [_dot(dru, x_, NT, P_BWD) for dru, x_ in zip(drus, ld(u_ref))]
        da2 = [_dot(drw, x_, NT, P_BWD) for drw, x_ in zip(drws, ld(w_ref))]
        das = [jnp.where(i > j, -(x_ + y_), 0.0) for x_, y_ in zip(da1, da2)]
        dkks = [da * bcol * dec for da, (bcol, _, _), dec in zip(das, vecs, decs)]
        dps = [dp_ref[ci, h] for ci, h in items]
        dqks = [dp_ * dec for dp_, dec in zip(dps, decs)]
        dq_ps = [_dot(dqk, kh, NN, P_BWD) for dqk, kh in zip(dqks, ks)]
        dk_ps = [_dot(dqk, qh, TN, P_BWD) for dqk, qh in zip(dqks, qs)]
        dk_as = [_dot(dkk, kh, NN, P_BWD) for dkk, kh in zip(dkks, ks)]
        dk_bs = [_dot(dkk, kh, TN, P_BWD) for dkk, kh in zip(dkks, ks)]
        bcols = [vc[0] for vc in vecs]
        gcols = [vc[1] for vc in vecs]
        gams = [jnp.exp(g) for g in gcols]
        glasts = [g[CH - 1:CH, :] for g in gcols]
        es = [jnp.exp(gl - g) for gl, g in zip(glasts, gcols)]
        kgs = [kh * gam for kh, gam in zip(ks, gams)]
        dqgs, dkss = ld(dqg_ref), ld(dks_ref)
        wks = [drw * kg for drw, kg in zip(drws, kgs)]
        kss = [dk_ * (kh * e) for dk_, kh, e in zip(dkss, ks, es)]
        r_beta = [rsum(dru * x_ + wk) for dru, x_, wk in zip(drus, ld(v_ref), wks)]
        r_ak = [rsum(da * kk * dec) for da, kk, dec in zip(das, kks, decs)]
        r_gc = [rsum(wk * bcol + dqg * (qh * gam) - ks_)
                for wk, bcol, dqg, qh, gam, ks_ in zip(wks, bcols, dqgs, qs, gams, kss)]
        tk_tot = [jnp.sum(jnp.sum(ks_, axis=0, keepdims=True), axis=-1, keepdims=True) for ks_ in kss]
        mdecs = [da * (bcol * kk * dec) + dp_ * p_ref[ci, h]
                 for (ci, h), da, bcol, kk, dec, dp_ in zip(items, das, bcols, kks, decs, dps)]
        r_md = [rsum(m) for m in mdecs]
        c_md = [rsum(jnp.where(i == j, jnp.sum(m, axis=0, keepdims=True), 0.0)) for m in mdecs]
        dbgs = [jnp.zeros((CH, DH), F32) for _ in range(cps)]
        for n_, (ci, h) in enumerate(items):
            at(dv_ref, (ci, h))[...] = bcols[n_] * drus[n_]
            at(dq_ref, (ci, h))[...] = gams[n_] * dqgs[n_] + dq_ps[n_]
            at(dk_ref, (ci, h))[...] = ((bcols[n_] * gams[n_]) * drws[n_] + dk_ps[n_] + dk_as[n_] + dk_bs[n_]
                                        + dkss[n_] * es[n_])
            dbeta = r_beta[n_] + r_ak[n_]
            dglast = tk_tot[n_] + dgam_ref[ci, 0:1, h:h + 1] * jnp.exp(glasts[n_])
            dgc = r_gc[n_] + r_md[n_] - c_md[n_] + jnp.where(rows1 == CH - 1, dglast, 0.0)
            dbgs[ci] = dbgs[ci] + jnp.where(lane == h, dbeta, 0.0) + jnp.where(lane == HEADS + h, dgc, 0.0)
        for ci in range(cps):
            dbg_ref[ci * CH:(ci + 1) * CH, :] = dbgs[ci]

    row = pl.BlockSpec((cps * CH, GW), lambda c: (c, 0))
    sq = pl.BlockSpec((cps, HEADS, CH, CH), lambda c: (c, 0, 0, 0))
    small = pl.BlockSpec((cps * CH, DH), lambda c: (c, 0))
    big = jax.ShapeDtypeStruct((n, GW), F32)
    return pl.pallas_call(
        body, name="gdn_intra_bwd", grid=(nch // cps,),
        in_specs=[row, row, row, small, pl.BlockSpec((cps, DH, CH), lambda c: (c, 0, 0)), sq, row, row, sq,
                  row, sq, row, row, row, pl.BlockSpec((cps, 8, DH), lambda c: (c, 0, 0))],
        out_specs=[row, row, row, small],
        out_shape=[big, big, big, jax.ShapeDtypeStruct((n, DH), F32)],
        compiler_params=_params(("parallel",)),
    )(q, k, v, bg, bgt, t, u, w, p, dqg, dp, du, dw, dks, dgam)


def _local_step(x, tgt, h, w_g, cqw, late, norm_in_w, ad, gdn_norm_w, conv_b, final_norm_w,
                on_grad_c=None, on_grad_g=None, on_q=None):
    proj_g = _matmul(h, w_g, NT, F32, 512, 1408, 1024, "mm_proj_g", n=GW_COLS, b_outer=True)
    q, k, v = _prep_qkv(proj_g, cqw)
    if on_q is not None:
        q = on_q(q)
    bg, bgt = _prep_bg(proj_g, ad)
    u, w, p, t = _gdn_intra(q, k, v, bg, bgt)
    o, vn, s_in = _gdn_scan(q, k, bg, u, w, p)
    w_c, w_out, conv_w = late(o)
    proj_c = _matmul(h, w_c, NT, F32, 512, 1024, 1024, "mm_proj_c", n=CW_COLS, b_outer=True)
    mix = _conv_branch(proj_c, conv_w, conv_b, _gdn_out(o, proj_g, gdn_norm_w))
    dout, dout_b, g_fn, loss = _out_loss(mix, w_out, x, tgt, final_norm_w)

    g_wout = _matmul(mix, dout_b, TN, BF16, 512, 512, 2048, "mm_gwout")
    do, dproj_g, g_gn = _gdn_out_bwd(o, proj_g, gdn_norm_w, dout_b, w_out)
    dproj_c, g_cw, g_cb = _conv_branch_bwd(proj_c, conv_w, conv_b, dout_b, w_out)
    g_c = _matmul(dproj_c, h, TN, BF16, 1024, 512, 2048, "mm_gwin_c")
    if on_grad_c is not None:
        do = on_grad_c(g_c, g_wout, do)
    dqg, dp, du, dw, dks, dgam = _gdn_scan_bwd(q, k, bg, w, p, vn, s_in, do)
    dq, dk, dv, dbg = _gdn_intra_bwd(q, k, v, bg, bgt, t, u, w, p, dqg, dp, du, dw, dks, dgam)
    dproj_g, gq, gk, gv = _prep_qkv_bwd(proj_g, cqw, dq, dk, dv, dproj_g)
    dproj_g, g_al, g_dt = _prep_bg_bwd(proj_g, ad, dbg, dproj_g)
    g_g = _matmul(dproj_g, h, TN, BF16, 1408, 512, 2048, "mm_gwin_g")
    if on_grad_g is not None:
        dproj_g = on_grad_g(g_g, dproj_g)
    dh = _matmul(dproj_g, w_g, NN, F32, 1024, 1024, 1408, "mm_dh_g")
    gx, g_nin = _dh_rms_bwd(dproj_c, w_c, dh, x, norm_in_w, dout, 1024)
    small = dict(nin=g_nin, cb=g_cb, fn=g_fn, al=g_al, dt=g_dt, gn=g_gn, cq=(gq, gk, gv), cw=g_cw, loss=loss)
    return gx, small, (g_g, g_c, g_wout)


def _place():
    x, y, c = lax.axis_index("x"), lax.axis_index("y"), lax.axis_index("c")
    chips = [(1 - x, y), (x, 1 - y), (1 - x, 1 - y)]
    return x, y, c, chips


def _blk(ref, b):
    if isinstance(b, int):
        return ref.at[b * DH:(b + 1) * DH, :]
    return ref.at[pl.ds(pl.multiple_of(b * DH, DH), DH), :]


HBM = pl.BlockSpec(memory_space=pltpu.HBM)
SEM = pl.BlockSpec(memory_space=pltpu.SEMAPHORE)
EFFECT = pltpu.SideEffectType.DATAFLOW_SIDE_EFFECTING


def _split_start(name, issue, bufs, n_sems):
    nbuf = len(bufs)

    def body(*refs):
        issue(refs[:nbuf], refs[nbuf], refs[nbuf + 1])
        refs[-1][...] = jnp.zeros_like(refs[-1])

    out = pl.pallas_call(
        body, name=name,
        out_shape=(pltpu.SemaphoreType.DMA((n_sems,)), pltpu.SemaphoreType.DMA((n_sems,)),
                   *[pltpu.HBM(b.shape, b.dtype) for b in bufs], jax.ShapeDtypeStruct((8, DH), F32)),
        in_specs=[HBM] * nbuf,
        out_specs=(SEM, SEM, *[HBM] * nbuf, pl.BlockSpec(memory_space=pltpu.VMEM)),
        input_output_aliases={a: 2 + a for a in range(nbuf)},
        compiler_params=pltpu.CompilerParams(has_side_effects=EFFECT),
    )(*[pltpu.with_memory_space_constraint(b, pltpu.HBM) for b in bufs])
    return out[0], out[1], list(out[2:2 + nbuf]), out[-1]


def _split_wait(name, await_, send_sems, recv_sems, bufs, after):
    nbuf = len(bufs)
    after = list(after) if isinstance(after, (list, tuple)) else [after]

    def body(*refs):
        await_(refs[:nbuf], refs[nbuf], refs[nbuf + 1])

    out = pl.pallas_call(
        body, name=name,
        out_shape=tuple(pltpu.HBM(b.shape, b.dtype) for b in bufs),
        in_specs=[HBM] * nbuf + [SEM, SEM] + [ANY] * len(after), out_specs=tuple([HBM] * nbuf),
        input_output_aliases={a: a for a in range(nbuf)},
        compiler_params=pltpu.CompilerParams(has_side_effects=EFFECT),
    )(*bufs, send_sems, recv_sems, *after)
    return list(out)


def _phase_blocks(chip, phase, edges, parity=None):
    return [(b, blk) for b, (grp, blk) in enumerate(_shard_blocks(chip, edges))
            if grp == phase and (parity is None or b % 2 == parity)]


def _cols(ref, nblk):
    return ref.at[0:nblk * DH, :]


def _block_table(chip, edges, spare_g, spare_c):
    rows = []
    for s in range(4):
        sb = _shard_blocks(s, edges)
        rows.append([[blk if grp == "g" else spare_g for grp, blk in sb],
                     [blk if grp == "c" else spare_c for grp, blk in sb],
                     [int(grp == "g") for grp, _ in sb], [s] * ALIGNED_BLOCKS])
    return jnp.asarray(rows, jnp.int32)[chip]


def _place_own(a_shard, wo, cq, cw, bufs):
    d = a_shard.shape[1]
    chip = 2 * lax.axis_index("x") + lax.axis_index("y")

    def body(t_ref, a_ref, wo_ref, cq_ref, cw_ref, *refs):
        wg_ref, wc_ref, wog_ref, cqg_ref, cwg_ref = refs[5:]
        wg_ref[...] = a_ref[...]
        wc_ref[...] = a_ref[...]

        @pl.when(pl.program_id(0) == 0)
        def _():
            wog_ref[0] = wo_ref[...]
            cqg_ref[0] = cq_ref[...]
            cwg_ref[0] = cw_ref[...]

    whole = lambda s: pl.BlockSpec(s.shape, lambda b, t: (0,) * s.ndim)
    slot = lambda s: pl.BlockSpec((1,) + s.shape, lambda b, t: (t[3, 0],) + (0,) * s.ndim)
    return pl.pallas_call(
        body, name="place_own",
        grid_spec=pltpu.PrefetchScalarGridSpec(
            num_scalar_prefetch=1, grid=(ALIGNED_BLOCKS,),
            in_specs=[pl.BlockSpec((DH, d), lambda b, t: (b, 0)), whole(wo), whole(cq), whole(cw)] + [ANY] * 5,
            out_specs=[pl.BlockSpec((DH, d), lambda b, t: (t[0, b], 0)),
                       pl.BlockSpec((DH, d), lambda b, t: (t[1, b], 0)), slot(wo), slot(cq), slot(cw)]),
        out_shape=[jax.ShapeDtypeStruct(b.shape, b.dtype) for b in bufs],
        input_output_aliases={5 + a: a for a in range(5)},
        compiler_params=_params(("arbitrary",)),
    )(_block_table(chip, True, G_SPARE, C_SPARE), a_shard, wo, cq, cw, *bufs)


def _tie(x, token, name):
    def body(x_ref, t_ref, o_ref):
        del x_ref, t_ref, o_ref

    return pl.pallas_call(
        body, name=name, in_specs=[ANY, ANY], out_specs=ANY,
        out_shape=jax.ShapeDtypeStruct(x.shape, x.dtype), input_output_aliases={0: 0},
    )(x, token)


def _gather_start(phase, a_shard, w_grp, singles):
    ns = len(singles)

    def issue(refs, send_sems, recv_sems):
        a_ref, w_ref = refs[0], refs[1]
        x, y, c, chips = _place()
        mine = 2 * x + y
        for jj, (px, py) in enumerate(chips):
            to = dict(device_id=(px, py, c), device_id_type=MESH)
            for a in range(ns):
                pltpu.make_async_remote_copy(
                    src_ref=refs[2 + 2 * a], dst_ref=refs[3 + 2 * a].at[mine],
                    send_sem=send_sems.at[(1 + ns) * jj + 1 + a], recv_sem=recv_sems.at[(1 + ns) * jj + 1 + a],
                    **to).start()
        for s in range(4):
            for par in range(2):
                blocks = _phase_blocks(s, phase, True, par)
                if blocks:
                    @pl.when((mine == s) & (c == par))
                    def _():
                        for b, blk in blocks:
                            for jj, (px, py) in enumerate(chips):
                                pltpu.make_async_remote_copy(
                                    src_ref=_blk(a_ref, b), dst_ref=_blk(w_ref, blk),
                                    send_sem=send_sems.at[(1 + ns) * jj], recv_sem=recv_sems.at[(1 + ns) * jj],
                                    device_id=(px, py, c), device_id_type=MESH).start()

    bufs = [a_shard, w_grp] + [t for pair in singles for t in pair]
    return _split_start("gather_start_" + phase, issue, bufs, 3 * (1 + ns))


def _gather_wait(phase, send_sems, recv_sems, bufs, after):
    ns = (len(bufs) - 2) // 2

    def await_(refs, send_sems, recv_sems):
        a_ref, w_ref = refs[0], refs[1]
        x, y, c, chips = _place()
        mine = 2 * x + y
        for jj, (px, py) in enumerate(chips):
            to = dict(device_id=(px, py, c), device_id_type=MESH)
            peer = 2 * px + py
            for a in range(ns):
                cp = pltpu.make_async_remote_copy(
                    src_ref=refs[2 + 2 * a], dst_ref=refs[3 + 2 * a].at[mine],
                    send_sem=send_sems.at[(1 + ns) * jj + 1 + a], recv_sem=recv_sems.at[(1 + ns) * jj + 1 + a], **to)
                cp.wait_recv()
                cp.wait_send()
            for s in range(4):
                for par in range(2):
                    nblk = len(_phase_blocks(s, phase, True, par))
                    if nblk:
                        both = pltpu.make_async_remote_copy(
                            src_ref=_cols(a_ref, nblk), dst_ref=_cols(w_ref, nblk),
                            send_sem=send_sems.at[(1 + ns) * jj], recv_sem=recv_sems.at[(1 + ns) * jj], **to)

                        @pl.when((peer == s) & (c == par))
                        def _():
                            both.wait_recv()

                        @pl.when((mine == s) & (c == par))
                        def _():
                            both.wait_send()

    return _split_wait("gather_wait_" + phase, await_, send_sems, recv_sems, bufs, after)


def _sibling_forward_parts(phase):
    def each(w_ref, send_sems, recv_sems, start):
        x, y, c, chips = _place()
        to = dict(device_id=(x, y, 1 - c), device_id_type=MESH)
        for jj, (px, py) in enumerate(chips):
            peer = 2 * px + py
            for s in range(4):
                for par in range(2):
                    mine_blocks = _phase_blocks(s, phase, True, par)
                    theirs = len(_phase_blocks(s, phase, True, 1 - par))
                    if not (mine_blocks or theirs):
                        continue

                    @pl.when((peer == s) & (c == par))
                    def _():
                        if start:
                            for _, blk in mine_blocks:
                                pltpu.make_async_remote_copy(
                                    src_ref=_blk(w_ref, blk), dst_ref=_blk(w_ref, blk),
                                    send_sem=send_sems.at[jj], recv_sem=recv_sems.at[jj], **to).start()
                            return
                        if theirs:
                            pltpu.make_async_remote_copy(
                                src_ref=_cols(w_ref, theirs), dst_ref=_cols(w_ref, theirs),
                                send_sem=send_sems.at[jj], recv_sem=recv_sems.at[jj], **to).wait_recv()
                        if mine_blocks:
                            pltpu.make_async_remote_copy(
                                src_ref=_cols(w_ref, len(mine_blocks)), dst_ref=_cols(w_ref, len(mine_blocks)),
                                send_sem=send_sems.at[jj], recv_sem=recv_sems.at[jj], **to).wait_send()

    issue = lambda refs, send_sems, recv_sems: each(refs[0], send_sems, recv_sems, True)
    await_ = lambda refs, send_sems, recv_sems: each(refs[0], send_sems, recv_sems, False)
    return issue, await_


def _sibling_forward(phase, w_grp):
    issue, await_ = _sibling_forward_parts(phase)

    def body(w_in_ref, w_ref, send_sems, recv_sems):
        del w_in_ref
        issue([w_ref], send_sems, recv_sems)
        await_([w_ref], send_sems, recv_sems)

    return pl.pallas_call(
        body, name="sibling_forward_" + phase, in_specs=[ANY], out_specs=ANY,
        out_shape=jax.ShapeDtypeStruct(w_grp.shape, w_grp.dtype), input_output_aliases={0: 0},
        scratch_shapes=[pltpu.SemaphoreType.DMA((3,)), pltpu.SemaphoreType.DMA((3,))],
    )(w_grp)


def _merge_edges(w, edge0, mixed, name):
    d = w.shape[1]

    def body(e_ref, o_ref):
        o_ref[...] = e_ref[0:DH, :] + e_ref[DH:2 * DH, :]

    def to_block(i):
        r = mixed[-1]
        for kk in range(len(mixed) - 2, -1, -1):
            r = jnp.where(i == kk, mixed[kk], r)
        return r

    return pl.pallas_call(
        body, name=name, grid=(len(mixed),),
        in_specs=[pl.BlockSpec((2 * DH, d), lambda i: (edge0 // 2 + i, 0))],
        out_specs=pl.BlockSpec((DH, d), lambda i: (to_block(i), 0)),
        out_shape=jax.ShapeDtypeStruct(w.shape, w.dtype),
        input_output_aliases={0: 0},
        compiler_params=_params(("arbitrary",)),
    )(w)


def _scatter_start(phase, g_grp, land, singles, halved=False):
    ns = len(singles)

    def issue(refs, send_sems, recv_sems):
        g_ref, land_ref = refs[0], refs[1]
        x, y, c, chips = _place()
        for jj, (px, py) in enumerate(chips):
            to = dict(device_id=(px, py, c), device_id_type=MESH)
            peer = 2 * px + py
            for a in range(ns):
                pltpu.make_async_remote_copy(
                    src_ref=refs[2 + 2 * a].at[peer], dst_ref=refs[3 + 2 * a].at[jj],
                    send_sem=send_sems.at[(1 + ns) * jj + 1 + a], recv_sem=recv_sems.at[(1 + ns) * jj + 1 + a],
                    **to).start()
            for s in range(4):
                for par in ((0, 1) if halved else (None,)):
                    blocks = _phase_blocks(s, phase, False, par)
                    if blocks:
                        @pl.when((peer == s) if par is None else ((peer == s) & (c == par)))
                        def _():
                            for b, blk in blocks:
                                pltpu.make_async_remote_copy(
                                    src_ref=_blk(g_ref, blk), dst_ref=_blk(land_ref.at[jj], b),
                                    send_sem=send_sems.at[(1 + ns) * jj], recv_sem=recv_sems.at[(1 + ns) * jj],
                                    **to).start()

    bufs = [g_grp, land] + [t for pair in singles for t in pair]
    return _split_start("scatter_start_" + phase, issue, bufs, 3 * (1 + ns))


def _scatter_wait(phase, send_sems, recv_sems, bufs, after, halved=False):
    ns = (len(bufs) - 2) // 2

    def await_(refs, send_sems, recv_sems):
        g_ref, land_ref = refs[0], refs[1]
        x, y, c, chips = _place()
        mine = 2 * x + y
        for jj, (px, py) in enumerate(chips):
            to = dict(device_id=(px, py, c), device_id_type=MESH)
            peer = 2 * px + py
            for a in range(ns):
                cp = pltpu.make_async_remote_copy(
                    src_ref=refs[2 + 2 * a].at[peer], dst_ref=refs[3 + 2 * a].at[jj],
                    send_sem=send_sems.at[(1 + ns) * jj + 1 + a], recv_sem=recv_sems.at[(1 + ns) * jj + 1 + a], **to)
                cp.wait_recv()
                cp.wait_send()
            for s in range(4):
                for par in ((0, 1) if halved else (None,)):
                    nblk = len(_phase_blocks(s, phase, False, par))
                    if nblk:
                        both = pltpu.make_async_remote_copy(
                            src_ref=_cols(g_ref, nblk), dst_ref=_cols(land_ref.at[jj], nblk),
                            send_sem=send_sems.at[(1 + ns) * jj], recv_sem=recv_sems.at[(1 + ns) * jj], **to)

                        @pl.when((mine == s) if par is None else ((mine == s) & (c == par)))
                        def _():
                            both.wait_recv()

                        @pl.when((peer == s) if par is None else ((peer == s) & (c == par)))
                        def _():
                            both.wait_send()

    return _split_wait("scatter_wait_" + phase, await_, send_sems, recv_sems, bufs, after)


def _needed_blocks(phase, parity):
    return sorted({blk for s in range(4) for _, blk in _phase_blocks(s, phase, False, parity)})


def _pair_reduce(phase, g_grp):
    n, d = g_grp.shape

    def swap(g_ref, sib_ref, send_sem, recv_sem):
        x, y, c, _ = _place()
        to = dict(device_id=(x, y, 1 - c), device_id_type=MESH)
        for par in range(2):
            give, get = _needed_blocks(phase, 1 - par), _needed_blocks(phase, par)

            @pl.when(c == par)
            def _():
                for blk in give:
                    pltpu.make_async_remote_copy(src_ref=_blk(g_ref, blk), dst_ref=_blk(sib_ref, blk),
                                                 send_sem=send_sem, recv_sem=recv_sem, **to).start()
                pltpu.make_async_remote_copy(src_ref=_cols(g_ref, len(get)), dst_ref=_cols(sib_ref, len(get)),
                                             send_sem=send_sem, recv_sem=recv_sem, **to).wait_recv()
                pltpu.make_async_remote_copy(src_ref=_cols(g_ref, len(give)), dst_ref=_cols(sib_ref, len(give)),
                                             send_sem=send_sem, recv_sem=recv_sem, **to).wait_send()

    sib = pl.pallas_call(
        swap, name="pair_swap_" + phase, in_specs=[ANY], out_specs=ANY,
        out_shape=jax.ShapeDtypeStruct((n, d), g_grp.dtype),
        scratch_shapes=[pltpu.SemaphoreType.DMA, pltpu.SemaphoreType.DMA],
    )(*_in_hbm(g_grp))

    lists = [_needed_blocks(phase, par) for par in range(2)]
    longest = max(len(t) for t in lists)
    table = jnp.asarray([t + [t[-1]] * (longest - len(t)) for t in lists], jnp.int32)[lax.axis_index("c")]

    def add(t_ref, a_ref, b_ref, o_ref):
        o_ref[...] = (a_ref[...].astype(F32) + b_ref[...].astype(F32)).astype(o_ref.dtype)

    blk = pl.BlockSpec((DH, d), lambda i, t: (t[i], 0))
    return pl.pallas_call(
        add, name="pair_add_" + phase,
        grid_spec=pltpu.PrefetchScalarGridSpec(num_scalar_prefetch=1, grid=(longest,),
                                               in_specs=[blk, blk], out_specs=blk),
        out_shape=jax.ShapeDtypeStruct((n, d), g_grp.dtype),
        compiler_params=_params(("arbitrary",)),
    )(table, g_grp, sib)


def _sum_shard(g_g, g_c, land):
    d = g_g.shape[1]
    chip = 2 * lax.axis_index("x") + lax.axis_index("y")

    def body(t_ref, gg_ref, gc_ref, land_ref, o_ref):
        b = pl.program_id(0)
        in_g = t_ref[2, b] == 1
        own = jnp.where(in_g, gg_ref[...].astype(F32), gc_ref[...].astype(F32))
        for jj in range(3):
            own = own + land_ref[jj].astype(F32)
        o_ref[...] = jnp.where(in_g & (b % 2 != lax.axis_index("c")), 0.0, own)

    return pl.pallas_call(
        body, name="sum_w_in",
        grid_spec=pltpu.PrefetchScalarGridSpec(
            num_scalar_prefetch=1, grid=(ALIGNED_BLOCKS,),
            in_specs=[pl.BlockSpec((DH, d), lambda b, t: (t[0, b], 0)), pl.BlockSpec((DH, d), lambda b, t: (t[1, b], 0)),
                      pl.BlockSpec((3, DH, d), lambda b, t: (0, b, 0))],
            out_specs=pl.BlockSpec((DH, d), lambda b, t: (b, 0))),
        out_shape=jax.ShapeDtypeStruct((ALIGNED_W, d), F32),
        compiler_params=_params(("arbitrary",)),
    )(_block_table(chip, False, 0, 0), g_g, g_c, land)


def _sum_rows(stack, land, rows):
    _, r, d = stack.shape
    rows = min(rows, r)
    chip = 2 * lax.axis_index("x") + lax.axis_index("y")

    def body(t_ref, own_ref, land_ref, o_ref):
        acc = own_ref[0].astype(F32)
        for jj in range(3):
            acc = acc + land_ref[jj].astype(F32)
        o_ref[...] = acc

    return pl.pallas_call(
        body, name="sum_w_out",
        grid_spec=pltpu.PrefetchScalarGridSpec(
            num_scalar_prefetch=1, grid=(r // rows,),
            in_specs=[pl.BlockSpec((1, rows, d), lambda i, t: (t[0], i, 0)),
                      pl.BlockSpec((3, rows, d), lambda i, t: (0, i, 0))],
            out_specs=pl.BlockSpec((rows, d), lambda i, t: (i, 0))),
        out_shape=jax.ShapeDtypeStruct((r, d), F32),
        compiler_params=_params(("arbitrary",)),
    )(jnp.reshape(chip, (1,)).astype(jnp.int32), stack, land)


def _exchange_parts(n_swap, with_pack):
    def copies(refs, send_sems, recv_sems):
        x, y, c, _ = _place()
        me = 4 * x + 2 * y + c
        cps = [pltpu.make_async_remote_copy(
            src_ref=refs[2 * a], dst_ref=refs[2 * a + 1], send_sem=send_sems.at[a], recv_sem=recv_sems.at[a],
            device_id=(x, y, 1 - c), device_id_type=MESH) for a in range(n_swap)]
        if with_pack:
            pack_ref, packs = refs[2 * n_swap], refs[2 * n_swap + 1]
            for r in range(1, 8):
                dx, dy, dc = (r >> 2) & 1, (r >> 1) & 1, r & 1
                peer = (x + dx - 2 * x * dx, y + dy - 2 * y * dy, c + dc - 2 * c * dc)
                cps.append(pltpu.make_async_remote_copy(
                    src_ref=pack_ref, dst_ref=packs.at[me], send_sem=send_sems.at[n_swap + r - 1],
                    recv_sem=recv_sems.at[n_swap + r - 1], device_id=peer, device_id_type=MESH))
        return cps

    def issue(refs, send_sems, recv_sems):
        for cp in copies(refs, send_sems, recv_sems):
            cp.start()

    def await_(refs, send_sems, recv_sems):
        cps = copies(refs, send_sems, recv_sems)
        for cp in cps:
            cp.wait_recv()
        for cp in cps:
            cp.wait_send()

    return issue, await_, n_swap + (7 if with_pack else 0)


def _sum_packs(pack, packs):
    x, y, c = lax.axis_index("x"), lax.axis_index("y"), lax.axis_index("c")
    me = jnp.reshape(4 * x + 2 * y + c, (1,)).astype(jnp.int32)

    def body(me_ref, own_ref, p_ref, o_ref):
        acc = jnp.where(me_ref[0] == 0, own_ref[...], p_ref[0])
        for d in range(1, 8):
            acc = acc + jnp.where(me_ref[0] == d, own_ref[...], p_ref[d])
        o_ref[...] = acc

    full = lambda s: pl.BlockSpec(s.shape, lambda i, t: (0,) * s.ndim)
    return pl.pallas_call(
        body, name="sum_packs",
        grid_spec=pltpu.PrefetchScalarGridSpec(num_scalar_prefetch=1, grid=(1,), in_specs=[full(pack), full(packs)],
                                               out_specs=full(pack)),
        out_shape=jax.ShapeDtypeStruct(pack.shape, F32),
    )(me, pack, packs)


def _adamw_update(g, w_ref, m_ref, v_ref, go, do, mo, vo):
    c1 = 1.0 / (1.0 - ADAM_B1 ** ADAM_STEP)
    c2 = 1.0 / (1.0 - ADAM_B2 ** ADAM_STEP)
    mn = ADAM_B1 * m_ref[...] + (1.0 - ADAM_B1) * g
    vn = ADAM_B2 * v_ref[...] + (1.0 - ADAM_B2) * (g * g)
    go[...] = g
    mo[...] = mn
    vo[...] = vn
    do[...] = -ADAM_LR * ((mn * c1) / (jnp.sqrt(vn * c2) + ADAM_EPS) + ADAM_WD * w_ref[...])


def _adamw(w, m, v, g1, g2, rows, name):
    r, cdim = w.shape
    rows = min(rows, r)

    def body(*refs):
        n_in = 4 if g2 is None else 5
        w_ref, m_ref, v_ref, g_ref = refs[:4]
        g = g_ref[...] if g2 is None else g_ref[...] + refs[4][...]
        _adamw_update(g, w_ref, m_ref, v_ref, *refs[n_in:n_in + 4])

    blk = pl.BlockSpec((rows, cdim), lambda i: (i, 0))
    args = [w, m, v, g1] + ([] if g2 is None else [g2])
    shp = jax.ShapeDtypeStruct((r, cdim), F32)
    return pl.pallas_call(
        body, name=name, grid=(r // rows,),
        in_specs=[blk] * len(args), out_specs=[blk] * 4, out_shape=[shp] * 4,
        compiler_params=_params(("parallel",), 20 * rows * cdim * 4 + 8 * 2**20),
    )(*_in_hbm(*args))


def _adamw_shard(wt, mt, vt, g1, g2):
    r, d = wt.shape
    cols = min(256, d)

    def body(w_ref, m_ref, v_ref, g_ref, g2_ref, go, do, mo, vo, pad_ref):
        chip = 2 * lax.axis_index("x") + lax.axis_index("y")
        back = [(ALIGNED_W - s) % ALIGNED_W for s in SHIFTS]
        pad_ref[...] = pltpu.roll(g_ref[...] + g2_ref[...], _by_chip(chip, back), 0)
        outs = [o.at[:, 0, :] for o in (go, do, mo, vo)]
        _adamw_update(pad_ref[0:r, :], w_ref, m_ref, v_ref, *outs)

    blk = pl.BlockSpec((r, cols), lambda i: (0, i))
    gblk = pl.BlockSpec((ALIGNED_W, cols), lambda i: (0, i))
    oblk = pl.BlockSpec((r, 1, cols), lambda i: (0, 0, i))
    shp = jax.ShapeDtypeStruct((r, 1, d), F32)
    return pl.pallas_call(
        body, name="adamw_w_in", grid=(d // cols,),
        in_specs=[blk] * 3 + [gblk] * 2, out_specs=[oblk] * 4, out_shape=[shp] * 4,
        scratch_shapes=[pltpu.VMEM((ALIGNED_W, cols), F32)],
        compiler_params=_params(("parallel",), 24 * ALIGNED_W * cols * 4 + 8 * 2**20),
    )(wt, mt, vt, g1, g2)


def _pad_lanes(a, width):
    return jnp.pad(a, ((0, 0), (0, width - a.shape[1])))


def _gathered_to_full(g):
    return jnp.transpose(g, (1, 0, 2)).reshape(g.shape[1], 4 * g.shape[2])


def _row(a):
    return _pad_lanes(a.reshape(1, -1), 1024)


def _small_pack(nin, cb, fn, al, dt, gn, cqw_shard, cw_shard):
    ad = jnp.concatenate([al.reshape(1, -1), dt.reshape(1, -1)], axis=1)
    rows = [_row(nin), _row(cb), _row(fn), _row(ad), _row(gn), cqw_shard.reshape(3, 1024), _row(cw_shard)]
    out = jnp.concatenate(rows, axis=0)
    return jnp.pad(out, ((0, 16 - out.shape[0]), (0, 0)))


def kernel(x, norm_in_w, w_in, conv_qkv_w, A_log, dt_bias, gdn_norm_w, conv_w, conv_b, w_out, final_norm_w, loss_target, m_norm_in_w, m_w_in, m_conv_qkv_w, m_A_log, m_dt_bias, m_gdn_norm_w, m_conv_w, m_conv_b, m_w_out, m_final_norm_w, v_norm_in_w, v_w_in, v_conv_qkv_w, v_A_log, v_dt_bias, v_gdn_norm_w, v_conv_w, v_conv_b, v_w_out, v_final_norm_w):
    chip = 2 * lax.axis_index("x") + lax.axis_index("y")
    a_shard = _align_shard(jnp.transpose(w_in, (2, 0, 1)))
    d_model = x.shape[-1]
    stack = lambda s: lax.empty((4,) + s.shape, s.dtype)
    wg0 = lax.empty((WG_BLOCKS * DH, d_model), BF16)
    wc0 = lax.empty((WC_BLOCKS * DH, d_model), BF16)
    ss_g, rs_g, bufs_g, tok_g = _gather_start("g", a_shard, wg0, [(conv_qkv_w[0], stack(conv_qkv_w[0]))])
    wo_b = _cast_bf16(w_out[0], 256, "cast_w_out", tok_g)
    ss_c, rs_c, bufs_c, tok_c = _gather_start("c", bufs_g[0], wc0,
                                              [(conv_w[0], stack(conv_w[0])), (wo_b, stack(wo_b))])
    wg1, wc1, wog1, cqg1, cwg1 = _place_own(bufs_c[0], bufs_c[4], bufs_g[2], bufs_c[2],
                                            [bufs_g[1], bufs_c[1], bufs_c[5], bufs_g[3], bufs_c[3]])
    x0 = x[0]
    h = _rms_in(x0, _tie(_tie(norm_in_w, tok_g, "after_gather_start_g"), tok_c, "after_gather_start_c"))
    adam_in = [jnp.transpose(a[0]) for a in (w_in, m_w_in, v_w_in)]
    sp = lambda nin, cb, fn, al, dt, gn, cq, cwv: _small_pack(nin, cb, fn, al, dt, gn, cq[0], cwv[0])
    w_s = sp(norm_in_w, conv_b, final_norm_w, A_log, dt_bias, gdn_norm_w, conv_qkv_w, conv_w)
    m_s = sp(m_norm_in_w, m_conv_b, m_final_norm_w, m_A_log, m_dt_bias, m_gdn_norm_w, m_conv_qkv_w, m_conv_w)
    v_s = sp(v_norm_in_w, v_conv_b, v_final_norm_w, v_A_log, v_dt_bias, v_gdn_norm_w, v_conv_qkv_w, v_conv_w)
    a_thru, wg, _, cq_g = _gather_wait("g", ss_g, rs_g, [bufs_c[0], wg1, bufs_g[2], cqg1],
                                       [h, w_s, m_s, v_s] + adam_in[1:])
    w_g = _merge_edges(_sibling_forward("g", wg), G_EDGE, G_MIXED, "merge_edges_g")
    cqw = _gathered_to_full(cq_g)
    ad = jnp.pad(jnp.concatenate([A_log, dt_bias], axis=0), ((0, 0), (A_LANE, 0)))
    fwd_c = {}

    def on_q(q):
        _, wc, _, cw_g, _, wo_g = _gather_wait("c", ss_c, rs_c,
                                               [a_thru, wc1, bufs_c[2], cwg1, bufs_c[4], wog1], q)
        issue, _ = _sibling_forward_parts("c")
        ss, rs, (wc,), tok = _split_start("sibling_forward_start_c", issue, [wc], 3)
        fwd_c.update(ss=ss, rs=rs, wc=wc, cw_g=cw_g, wo_g=wo_g)
        return _tie(q, tok, "after_sibling_forward_start_c")

    def late(o):
        _, await_ = _sibling_forward_parts("c")
        (wc,) = _split_wait("sibling_forward_wait_c", await_, fwd_c["ss"], fwd_c["rs"], [fwd_c["wc"]], o)
        return (_merge_edges(wc, C_EDGE, C_MIXED, "merge_edges_c"), fwd_c["wo_g"].reshape(2 * GW, d_model),
                _gathered_to_full(fwd_c["cw_g"]))

    scat = {}

    def on_grad_c(g_c, g_wout, do):
        go4 = g_wout.reshape(4, GW // 2, d_model)
        land = lax.empty((3, ALIGNED_W, d_model), BF16)
        land_o = lax.empty((3, GW // 2, d_model), BF16)
        ss, rs, bufs, tok = _scatter_start("c", g_c, land, [(go4, land_o)])
        scat["c"] = (ss, rs, bufs)
        return _tie(do, tok, "after_scatter_start_c")

    def on_grad_g(g_g, dproj_g):
        ss, rs, bufs, tok = _scatter_start("g", _pair_reduce("g", g_g), scat["c"][2][1], [], halved=True)
        scat["g"] = (ss, rs, bufs)
        return _tie(dproj_g, tok, "after_scatter_start_g")

    gx, sm, _ = _local_step(x0, loss_target[0], h, w_g, cqw, late, norm_in_w, ad, gdn_norm_w, conv_b,
                            final_norm_w.reshape(1, -1), on_grad_c, on_grad_g, on_q)

    ss, rs, bufs = scat["c"]
    g_c, land, go4, land_o = _scatter_wait("c", ss, rs, [bufs[0], scat["g"][2][1], bufs[2], bufs[3]], gx)
    part_out = _sum_rows(go4, land_o, 128)
    ad_g = jnp.concatenate([sm["al"][:, A_LANE:], sm["dt"][:, A_LANE:]], axis=1)
    pack = jnp.concatenate([_row(sm["nin"]), _row(sm["cb"]), _row(sm["fn"]), _row(ad_g), _row(sm["gn"]),
                            jnp.concatenate(sm["cq"], axis=1).reshape(12, 1024), sm["cw"], _row(sm["loss"])], axis=0)
    pack = jnp.pad(pack, ((0, PACK_ROWS - pack.shape[0]), (0, 0)))
    issue, await_a, nsem = _exchange_parts(1, True)
    ss_a, rs_a, bufs_a, tok_a = _split_start(
        "exchange_start_small", issue,
        [part_out, lax.empty(part_out.shape, F32), pack, lax.empty((8,) + pack.shape, F32)], nsem)
    ss, rs, bufs = scat["g"]
    g_g, land = _scatter_wait("g", ss, rs, [bufs[0], land], [gx, tok_a], halved=True)
    part_in = _sum_shard(g_g, g_c, land)
    issue, await_b, nsem = _exchange_parts(1, False)
    ss_b, rs_b, bufs_b, tok_b = _split_start("exchange_start_w_in", issue,
                                             [part_in, lax.empty(part_in.shape, F32)], nsem)
    part_out, sib_out, pack, packs = _split_wait("exchange_wait_small", await_a, ss_a, rs_a, bufs_a, tok_b)
    tot = _sum_packs(pack, packs)
    g_wo, d_wo, m_wo, v_wo = _adamw(w_out[0], m_w_out[0], v_w_out[0], part_out, sib_out, 128, "adamw_w_out")
    g_cq_sh = lax.dynamic_slice_in_dim(tot[R_CQ:R_CQ + 12].reshape(4, 3 * GW), chip * 768, 768, axis=1)
    g_cw_sh = lax.dynamic_slice_in_dim(tot[R_CW:R_CW + 3], chip * 256, 256, axis=1)
    g_s = _small_pack(tot[R_NIN], tot[R_CB], tot[R_FN], tot[R_AD, :HEADS], tot[R_AD, HEADS:2 * HEADS],
                      tot[R_GN, :DH], g_cq_sh, g_cw_sh)
    small = _adamw(w_s, m_s, v_s, g_s, None, 16, "adamw_small")
    part_in, sib_in = _split_wait("exchange_wait_w_in", await_b, ss_b, rs_b, bufs_b, [small[0], d_wo])
    g_wi, d_wi, m_wi, v_wi = [jnp.transpose(a, (1, 2, 0))[0] for a in _adamw_shard(*adam_in, part_in, sib_in)]

    def unpack(a, big_in, big_out):
        return (a[0:1], big_in[None], a[5:8].reshape(1, 4, 768), a[3:4, :HEADS], a[3:4, HEADS:2 * HEADS],
                a[4:5, :DH], a[8, :768].reshape(1, 3, 256), a[1:2], big_out[None], a[2])

    loss = tot[R_LOSS, 0]
    return (loss, gx[None], *unpack(small[0], g_wi, g_wo), *unpack(small[1], d_wi, d_wo),
            *unpack(small[2], m_wi, m_wo), *unpack(small[3], v_wi, v_wo))
```

```python
import jax
import jax.numpy as jnp
from jax import lax
from jax.experimental import pallas as pl
from jax.experimental.pallas import tpu as pltpu

F32 = jnp.float32
BF16 = jnp.bfloat16
MESH = pl.DeviceIdType.MESH
ANY = pl.BlockSpec(memory_space=pl.ANY)

HEADS = 8
DH = 128
CH = 64
GW = HEADS * DH
EPS = 1e-6
VMEM_V7X = 64 * 1024 * 1024

QB, KB, VB, ZB, BAB = 0, 8, 16, 24, 32
A_LANE = 120
NG, NC = 33, 32
GW_COLS, CW_COLS = NG * DH, NC * DH

SHARD_W = 2052
ALIGNED_BLOCKS = 17
ALIGNED_W = ALIGNED_BLOCKS * DH
SHIFTS = (0, 4, ALIGNED_W - 8, ALIGNED_W - 4)
G_EDGE, C_EDGE = 34, 32
G_SPARE, C_SPARE = 33, 34
WG_BLOCKS, WC_BLOCKS = 38, 36
G_MIXED, C_MIXED = (2, BAB), (4 * 7 + 1,)


def _shard_blocks(chip, edges):
    g, c = "g", "c"
    if chip == 0:
        out = [(g, 3 * b) for b in range(8)] + [(g, 3 * b + 1) for b in range(8)] + [(g, G_EDGE, G_MIXED[0])]
    elif chip == 1:
        out = [(g, G_EDGE + 1, G_MIXED[0])] + [(g, 3 * b + 2) for b in range(1, 8)]
        out += [(g, ZB + b) for b in range(8)] + [(g, G_EDGE + 2, G_MIXED[1])]
    elif chip == 2:
        out = [(c, 4 * b) for b in range(8)] + [(c, 4 * b + 1) for b in range(7)]
        out += [(c, C_EDGE, C_MIXED[0]), (g, G_EDGE + 3, G_MIXED[1])]
    else:
        out = [(c, 4 * b + 2) for b in range(8)] + [(c, 4 * b + 3) for b in range(8)] + [(c, C_EDGE + 1, C_MIXED[0])]
    return [(o[0], o[1] if (edges or len(o) == 2) else o[2]) for o in out]


def _by_chip(chip, vals):
    if all(v == vals[0] for v in vals):
        return vals[0]
    r = vals[3]
    for kk in (2, 1, 0):
        r = jnp.where(chip == kk, vals[kk], r)
    return r

ADAM_LR, ADAM_B1, ADAM_B2, ADAM_EPS, ADAM_WD, ADAM_STEP = 0.001, 0.9, 0.999, 1e-08, 0.01, 10

R_NIN, R_CB, R_FN, R_AD, R_GN, R_CQ, R_CW, R_LOSS, PACK_ROWS = 0, 1, 2, 3, 4, 5, 17, 20, 24

NN = ((1,), (0,))
NT = ((1,), (1,))
TN = ((0,), (0,))


def _dot(a, b, dims=NN, mode="lo"):
    dn = (dims, ((), ()))
    if mode == "hi":
        return lax.dot_general(a, b, dn, precision=lax.Precision.HIGHEST, preferred_element_type=F32)
    ah, bh = a.astype(BF16), b.astype(BF16)
    out = lax.dot_general(ah, bh, dn, preferred_element_type=F32)
    if mode == "x3":
        al = (a - ah.astype(F32)).astype(BF16)
        bl = (b - bh.astype(F32)).astype(BF16)
        out = out + lax.dot_general(ah, bl, dn, preferred_element_type=F32)
        out = out + lax.dot_general(al, bh, dn, preferred_element_type=F32)
    return out


P_GRAM, P_INV, P_SOL, P_SCAN, P_SCANB, P_BWD = "lo", "lo", "lo", "lo", "lo", "lo"
P_CUM = "x3"


def _params(sem=None, vmem=None):
    kw = {}
    if sem is not None:
        kw["dimension_semantics"] = sem
    if vmem is not None:
        kw["vmem_limit_bytes"] = int(min(max(vmem, 32 * 2**20), VMEM_V7X - 8 * 2**20))
    return pltpu.CompilerParams(**kw)


def _in_hbm(*arrays):
    return [pltpu.with_memory_space_constraint(a, pltpu.HBM) for a in arrays]


def _sigmoid(x):
    return 1.0 / (1.0 + jnp.exp(-x))


def _dsilu(x, s):
    return s * (1.0 + x * (1.0 - s))


def _rows(shape):
    return lax.broadcasted_iota(jnp.int32, shape, 0)


def _shift_down(x, s):
    if s == 0:
        return x
    return jnp.where(_rows(x.shape) >= s, pltpu.roll(x, s, 0), 0.0)


def _shift_up(x, s):
    if s == 0:
        return x
    n = x.shape[0]
    return jnp.where(_rows(x.shape) < n - s, pltpu.roll(x, n - s, 0), 0.0)


def _matmul(a, b, dims, out_dtype, tm, tn, tk, name, add=None, n=None, b_outer=False):
    if dims == NN:
        (m, k), n = a.shape, b.shape[1]
    elif dims == NT:
        (m, k), n = a.shape, (n or b.shape[0])
    else:
        (k, m), n = a.shape, b.shape[1]
    tm, tn, tk = min(tm, m), min(tn, n), min(tk, k)
    assert m % tm == 0 and n % tn == 0 and k % tk == 0, (name, m, n, k, tm, tn, tk)
    nk = k // tk

    def body(*refs):
        if add is None:
            a_ref, b_ref, o_ref = refs[:3]
            add_ref = None
        else:
            a_ref, b_ref, add_ref, o_ref = refs[:4]
        part = _dot(a_ref[...], b_ref[...], dims)
        if nk == 1:
            if add_ref is not None:
                part = part + add_ref[...]
            o_ref[...] = part.astype(out_dtype)
            return
        acc = refs[-1]
        kk = pl.program_id(2)

        @pl.when(kk == 0)
        def _():
            acc[...] = part

        @pl.when(kk > 0)
        def _():
            acc[...] += part

        @pl.when(kk == nk - 1)
        def _():
            r = acc[...]
            if add_ref is not None:
                r = r + add_ref[...]
            o_ref[...] = r.astype(out_dtype)

    ij = (lambda g0, g1: (g1, g0)) if b_outer else (lambda g0, g1: (g0, g1))

    def spec(shape, pick):
        return pl.BlockSpec(shape, lambda g0, g1, kk: pick(*ij(g0, g1), kk))

    a_spec = spec((tk, tm), lambda i, j, kk: (kk, i)) if dims == TN else spec((tm, tk), lambda i, j, kk: (i, kk))
    b_spec = spec((tn, tk), lambda i, j, kk: (j, kk)) if dims == NT else spec((tk, tn), lambda i, j, kk: (kk, j))
    o_spec = spec((tm, tn), lambda i, j, kk: (i, j))
    in_specs = [a_spec, b_spec]
    args = [a, b]
    if add is not None:
        in_specs.append(o_spec)
        args.append(add)
    osz = jnp.dtype(out_dtype).itemsize
    est = 2 * (tm * tk * a.dtype.itemsize + tk * tn * b.dtype.itemsize + tm * tn * osz)
    est += 3 * tm * tn * 4 + (2 * tm * tn * 4 if add is not None else 0)
    return pl.pallas_call(
        body, name=name, grid=(n // tn, m // tm, nk) if b_outer else (m // tm, n // tn, nk),
        in_specs=in_specs, out_specs=o_spec,
        out_shape=jax.ShapeDtypeStruct((m, n), out_dtype),
        scratch_shapes=[pltpu.VMEM((tm, tn), F32)] if nk > 1 else [],
        compiler_params=_params(("parallel", "parallel", "arbitrary"), est + 8 * 2**20),
    )(*args)


def _cast_bf16(a, rows, name, after):
    r, c = a.shape
    rows = min(rows, r)

    def body(a_ref, t_ref, o_ref):
        del t_ref
        o_ref[...] = a_ref[...].astype(BF16)

    return pl.pallas_call(
        body, name=name, grid=(r // rows,),
        in_specs=[pl.BlockSpec((rows, c), lambda i: (i, 0)), ANY],
        out_specs=pl.BlockSpec((rows, c), lambda i: (i, 0)),
        out_shape=jax.ShapeDtypeStruct((r, c), BF16),
        compiler_params=_params(("parallel",)),
    )(a, after)


def _align_shard(wt):
    r, _, d = wt.shape
    cols = min(256, d)

    def body(w_ref, o_ref, pad_ref):
        chip = 2 * lax.axis_index("x") + lax.axis_index("y")
        pad_ref[...] = jnp.zeros_like(pad_ref)
        pad_ref[0:r, :] = w_ref[:, 0, :]
        o_ref[...] = pltpu.roll(pad_ref[...], _by_chip(chip, SHIFTS), 0).astype(BF16)

    return pl.pallas_call(
        body, name="align_shard", grid=(d // cols,),
        in_specs=[pl.BlockSpec((r, 1, cols), lambda i: (0, 0, i))],
        out_specs=pl.BlockSpec((ALIGNED_W, cols), lambda i: (0, i)),
        out_shape=jax.ShapeDtypeStruct((ALIGNED_W, d), BF16),
        scratch_shapes=[pltpu.VMEM((ALIGNED_W, cols), F32)],
        compiler_params=_params(("parallel",)),
    )(wt)


def _rms_in(x, w):
    n, d = x.shape
    tr = min(256, n)

    def body(x_ref, w_ref, h_ref):
        xv = x_ref[...]
        r = lax.rsqrt(jnp.mean(xv * xv, axis=-1, keepdims=True) + EPS)
        h_ref[...] = (xv * r * w_ref[...]).astype(BF16)

    return pl.pallas_call(
        body, name="rms_in", grid=(n // tr,),
        in_specs=[pl.BlockSpec((tr, d), lambda i: (i, 0)), pl.BlockSpec((1, d), lambda i: (0, 0))],
        out_specs=pl.BlockSpec((tr, d), lambda i: (i, 0)),
        out_shape=jax.ShapeDtypeStruct((n, d), BF16),
        compiler_params=_params(("parallel",)),
    )(x, w)


def _conv_silu(p, w_ref, taps):
    c = None
    for j in range(taps):
        t = _shift_down(p, taps - 1 - j) * w_ref[j:j + 1, :]
        c = t if c is None else c + t
    return c


def _prep_qkv(proj, cw):
    n = proj.shape[0]

    def body(p3, wq, wk, wv, q_ref, k_ref, v_ref):
        for kind, (w_ref, o_ref) in enumerate(((wq, q_ref), (wk, k_ref), (wv, v_ref))):
            c = _conv_silu(p3[:, kind * DH:(kind + 1) * DH], w_ref, 4)
            a = c * _sigmoid(c)
            if kind < 2:
                r = lax.rsqrt(jnp.sum(a * a, axis=-1, keepdims=True) + EPS)
                a = a * (r * (DH ** -0.5 if kind == 0 else 1.0))
            o_ref[...] = a

    col = pl.BlockSpec((n, DH), lambda h: (0, h))
    wcol = lambda base: pl.BlockSpec((4, DH), lambda h: (0, base + h))
    out = jax.ShapeDtypeStruct((n, GW), F32)
    return pl.pallas_call(
        body, name="prep_qkv", grid=(HEADS,),
        in_specs=[pl.BlockSpec((n, 3 * DH), lambda h: (0, h)), wcol(QB), wcol(KB), wcol(VB)],
        out_specs=[col] * 3, out_shape=[out] * 3,
        compiler_params=_params(("parallel",), 40 * 2**20),
    )(proj, cw, cw, cw)


def _prep_qkv_bwd(proj, cw, dq, dk, dv, dproj):
    n = proj.shape[0]

    def body(p3, wq, wk, wv, dq_ref, dk_ref, dv_ref, _, o3, gq, gk, gv):
        for kind, (w_ref, d_ref, g_ref) in enumerate(((wq, dq_ref, gq), (wk, dk_ref, gk), (wv, dv_ref, gv))):
            p = p3[:, kind * DH:(kind + 1) * DH]
            shifted = [_shift_down(p, 3 - j) for j in range(4)]
            c = shifted[0] * w_ref[0:1, :]
            for j in range(1, 4):
                c = c + shifted[j] * w_ref[j:j + 1, :]
            s = _sigmoid(c)
            a = c * s
            d = d_ref[...]
            if kind < 2:
                r = lax.rsqrt(jnp.sum(a * a, axis=-1, keepdims=True) + EPS)
                sc = DH ** -0.5 if kind == 0 else 1.0
                d = (sc * r) * (d - a * ((r * r) * jnp.sum(d * a, axis=-1, keepdims=True)))
            dc = d * _dsilu(c, s)
            dp = None
            for j in range(4):
                g_ref[j:j + 1, :] = jnp.sum(dc * shifted[j], axis=0, keepdims=True)
                t = _shift_up(dc, 3 - j) * w_ref[j:j + 1, :]
                dp = t if dp is None else dp + t
            o3[:, kind * DH:(kind + 1) * DH] = dp.astype(BF16)

    col = pl.BlockSpec((n, DH), lambda h: (0, h))
    wcol = lambda base: pl.BlockSpec((4, DH), lambda h: (0, base + h))
    p3spec = pl.BlockSpec((n, 3 * DH), lambda h: (0, h))
    return pl.pallas_call(
        body, name="prep_qkv_bwd", grid=(HEADS,),
        in_specs=[p3spec, wcol(QB), wcol(KB), wcol(VB), col, col, col, ANY],
        out_specs=[p3spec] + [wcol(0)] * 3,
        out_shape=[jax.ShapeDtypeStruct(dproj.shape, BF16)] + [jax.ShapeDtypeStruct((4, GW), F32)] * 3,
        input_output_aliases={7: 0},
        compiler_params=_params(("parallel",), 48 * 2**20),
    )(proj, cw, cw, cw, dq, dk, dv, dproj)


CPB = 8
SCAN_CPS = 4


def _tri(lower, rows):
    i = lax.broadcasted_iota(jnp.int32, (rows, rows), 0)
    j = lax.broadcasted_iota(jnp.int32, (rows, rows), 1)
    return jnp.where((i // CH == j // CH) & ((i >= j) if lower else (j >= i)), 1.0, 0.0)


def _lane(shape):
    return lax.broadcasted_iota(jnp.int32, shape, 1)


def _prep_bg(proj, ad):
    n = proj.shape[0]
    nch = n // CH
    cpb = CPB if nch % CPB == 0 else 1
    rows = cpb * CH

    def body(p_ref, ad_ref, bg_ref, bgt_ref):
        p = p_ref[...]
        lane = _lane(p.shape)
        beta = _sigmoid(p)
        xa = p + ad_ref[1:2, :]
        sp = jnp.maximum(xa, 0.0) + jnp.log(1.0 + jnp.exp(-jnp.abs(xa)))
        g = pltpu.roll(-jnp.exp(ad_ref[0:1, :]) * sp, DH - A_LANE + HEADS, 1)
        gc = _dot(_tri(True, rows), g, NN, P_CUM)
        bg = jnp.where(lane < HEADS, beta, jnp.where(lane < 2 * HEADS, gc, 0.0))
        bg_ref[...] = bg
        for ci in range(cpb):
            bgt_ref[ci] = bg[ci * CH:(ci + 1) * CH, :].T

    return pl.pallas_call(
        body, name="prep_bg", grid=(nch // cpb,),
        in_specs=[pl.BlockSpec((rows, DH), lambda i: (i, BAB)), pl.BlockSpec((2, DH), lambda i: (0, 0))],
        out_specs=[pl.BlockSpec((rows, DH), lambda i: (i, 0)), pl.BlockSpec((cpb, DH, CH), lambda i: (i, 0, 0))],
        out_shape=[jax.ShapeDtypeStruct((n, DH), F32), jax.ShapeDtypeStruct((nch, DH, CH), F32)],
        compiler_params=_params(("parallel",)),
    )(*_in_hbm(proj, ad))


def _prep_bg_bwd(proj, ad, dbg, dproj):
    n = proj.shape[0]
    nch = n // CH
    cpb = CPB if nch % CPB == 0 else 1
    rows = cpb * CH

    def body(p_ref, ad_ref, d_ref, _, o_ref, ga_ref, gd_ref):
        p = p_ref[...]
        d = d_ref[...]
        lane = _lane(p.shape)
        beta = _sigmoid(p)
        xa = p + ad_ref[1:2, :]
        sp = jnp.maximum(xa, 0.0) + jnp.log(1.0 + jnp.exp(-jnp.abs(xa)))
        na = -jnp.exp(ad_ref[0:1, :])
        dg = pltpu.roll(_dot(_tri(False, rows), d, NN, P_CUM), A_LANE - HEADS, 1)
        da = dg * na * _sigmoid(xa)
        is_g = lane >= A_LANE
        o_ref[...] = jnp.where(lane < HEADS, d * beta * (1.0 - beta), jnp.where(is_g, da, 0.0)).astype(BF16)
        ga = jnp.sum(jnp.where(is_g, dg * na * sp, 0.0), axis=0, keepdims=True)
        gd = jnp.sum(jnp.where(is_g, da, 0.0), axis=0, keepdims=True)

        @pl.when(pl.program_id(0) == 0)
        def _():
            ga_ref[...] = jnp.zeros_like(ga_ref)
            gd_ref[...] = jnp.zeros_like(gd_ref)

        ga_ref[...] += ga
        gd_ref[...] += gd

    one = pl.BlockSpec((1, DH), lambda i: (0, 0))
    return pl.pallas_call(
        body, name="prep_bg_bwd", grid=(nch // cpb,),
        in_specs=[pl.BlockSpec((rows, DH), lambda i: (i, BAB)), pl.BlockSpec((2, DH), lambda i: (0, 0)),
                  pl.BlockSpec((rows, DH), lambda i: (i, 0)), ANY],
        out_specs=[pl.BlockSpec((rows, DH), lambda i: (i, BAB)), one, one],
        out_shape=[jax.ShapeDtypeStruct(dproj.shape, BF16), jax.ShapeDtypeStruct((1, DH), F32),
                   jax.ShapeDtypeStruct((1, DH), F32)],
        input_output_aliases={3: 0},
        compiler_params=_params(("arbitrary",)),
    )(proj, ad, dbg, dproj)


def _gdn_out(o, proj, wg):
    n = o.shape[0]

    def body(o_ref, z_ref, w_ref, y_ref):
        ov, z = o_ref[...], z_ref[...]
        r = lax.rsqrt(jnp.mean(ov * ov, axis=-1, keepdims=True) + EPS)
        y_ref[...] = (ov * r * w_ref[...] * (z * _sigmoid(z))).astype(BF16)

    return pl.pallas_call(
        body, name="gdn_out", grid=(HEADS,),
        in_specs=[pl.BlockSpec((n, DH), lambda h: (0, h)), pl.BlockSpec((n, DH), lambda h: (0, ZB + h)),
                  pl.BlockSpec((1, DH), lambda h: (0, 0))],
        out_specs=pl.BlockSpec((n, DH), lambda h: (0, h)),
        out_shape=jax.ShapeDtypeStruct((n, 2 * GW), BF16),
        compiler_params=_params(("parallel",)),
    )(o, proj, wg)


def _gdn_out_bwd(o, proj, wg, dout_b, w_out):
    n = o.shape[0]
    d_model = dout_b.shape[1]

    def body(o_ref, z_ref, w_ref, g_ref, wo_ref, do_ref, dz_ref, gw_ref):
        ov, z, w = o_ref[...], z_ref[...], w_ref[...]
        d = _dot(g_ref[...], wo_ref[...], NT)
        r = lax.rsqrt(jnp.mean(ov * ov, axis=-1, keepdims=True) + EPS)
        nrm = ov * r
        s = _sigmoid(z)
        dz_ref[...] = (d * (nrm * w) * _dsilu(z, s)).astype(BF16)
        dn_w = d * (z * s)
        gw = jnp.sum(dn_w * nrm, axis=0, keepdims=True)
        dn = dn_w * w
        do_ref[...] = (r * (dn - nrm * jnp.mean(dn * nrm, axis=-1, keepdims=True))).astype(BF16)

        @pl.when(pl.program_id(0) == 0)
        def _():
            gw_ref[...] = jnp.zeros_like(gw_ref)

        gw_ref[...] += gw

    return pl.pallas_call(
        body, name="gdn_out_bwd", grid=(HEADS,),
        in_specs=[pl.BlockSpec((n, DH), lambda h: (0, h)), pl.BlockSpec((n, DH), lambda h: (0, ZB + h)),
                  pl.BlockSpec((1, DH), lambda h: (0, 0)), pl.BlockSpec((n, d_model), lambda h: (0, 0)),
                  pl.BlockSpec((DH, d_model), lambda h: (h, 0))],
        out_specs=[pl.BlockSpec((n, DH), lambda h: (0, h)), pl.BlockSpec((n, DH), lambda h: (0, ZB + h)),
                   pl.BlockSpec((1, DH), lambda h: (0, 0))],
        out_shape=[jax.ShapeDtypeStruct((n, GW), BF16), jax.ShapeDtypeStruct((n, GW_COLS), BF16),
                   jax.ShapeDtypeStruct((1, DH), F32)],
        compiler_params=_params(("arbitrary",), 40 * 2**20),
    )(o, proj, wg, dout_b, w_out)


def _conv_branch(proj, w3, b, mix):
    n = proj.shape[0]

    def body(p4, w_ref, b_ref, _, y_ref):
        u = p4[:, DH:2 * DH] * p4[:, 2 * DH:3 * DH]
        cc = _conv_silu(u, w_ref, 3) + b_ref[...]
        z = p4[:, 3 * DH:4 * DH]
        y_ref[...] = (p4[:, 0:DH] * cc * (z * _sigmoid(z))).astype(BF16)

    return pl.pallas_call(
        body, name="conv_branch", grid=(HEADS,),
        in_specs=[pl.BlockSpec((n, 4 * DH), lambda h: (0, h)), pl.BlockSpec((3, DH), lambda h: (0, h)),
                  pl.BlockSpec((1, DH), lambda h: (0, h)), ANY],
        out_specs=pl.BlockSpec((n, DH), lambda h: (0, HEADS + h)),
        out_shape=jax.ShapeDtypeStruct(mix.shape, BF16),
        input_output_aliases={3: 0},
        compiler_params=_params(("parallel",), 40 * 2**20),
    )(*_in_hbm(proj, w3, b, mix))


def _conv_branch_bwd(proj, w3, b, dout_b, w_out):
    n = proj.shape[0]
    d_model = dout_b.shape[1]

    def body(p4, w_ref, b_ref, g_ref, wo_ref, o4, gw_ref, gbias_ref):
        gb, gcv, hc, z = p4[:, 0:DH], p4[:, DH:2 * DH], p4[:, 2 * DH:3 * DH], p4[:, 3 * DH:4 * DH]
        d = _dot(g_ref[...], wo_ref[...], NT)
        dgb, dgc, dhc, dzc = (o4.at[:, kk * DH:(kk + 1) * DH] for kk in range(4))
        u = gcv * hc
        cc = _conv_silu(u, w_ref, 3) + b_ref[...]
        s = _sigmoid(z)
        dzc[...] = (d * (gb * cc) * _dsilu(z, s)).astype(BF16)
        dp = d * (z * s)
        dgb[...] = (dp * cc).astype(BF16)
        dcc = dp * gb
        gbias_ref[...] = jnp.sum(dcc, axis=0, keepdims=True)
        du = None
        for j in range(3):
            gw_ref[j:j + 1, :] = jnp.sum(dcc * _shift_down(u, 2 - j), axis=0, keepdims=True)
            t = _shift_up(dcc, 2 - j) * w_ref[j:j + 1, :]
            du = t if du is None else du + t
        dgc[...] = (du * hc).astype(BF16)
        dhc[...] = (du * gcv).astype(BF16)

    p4spec = pl.BlockSpec((n, 4 * DH), lambda h: (0, h))
    return pl.pallas_call(
        body, name="conv_branch_bwd", grid=(HEADS,),
        in_specs=[p4spec, pl.BlockSpec((3, DH), lambda h: (0, h)), pl.BlockSpec((1, DH), lambda h: (0, h)),
                  pl.BlockSpec((n, d_model), lambda h: (0, 0)), pl.BlockSpec((DH, d_model), lambda h: (HEADS + h, 0))],
        out_specs=[p4spec, pl.BlockSpec((3, DH), lambda h: (0, h)), pl.BlockSpec((1, DH), lambda h: (0, h))],
        out_shape=[jax.ShapeDtypeStruct((n, CW_COLS), BF16), jax.ShapeDtypeStruct((3, GW), F32),
                   jax.ShapeDtypeStruct((1, GW), F32)],
        compiler_params=_params(("parallel",), 52 * 2**20),
    )(proj, w3, b, dout_b, w_out)


def _out_loss(mix, w_out, x, tgt, wf):
    n, d = x.shape
    kdim = mix.shape[1]
    tr = min(256, n)

    def body(m_ref, wo_ref, x_ref, t_ref, w_ref, do_ref, dob_ref, gw_ref, loss_ref):
        ov = _dot(m_ref[...], wo_ref[...], NN) + x_ref[...]
        w = w_ref[...]
        r = lax.rsqrt(jnp.mean(ov * ov, axis=-1, keepdims=True) + EPS)
        nrm = ov * r
        e = nrm * w - t_ref[...]
        dy = e * (1.0 / d)
        dn = dy * w
        dout = r * (dn - nrm * jnp.mean(dn * nrm, axis=-1, keepdims=True))
        do_ref[...] = dout
        dob_ref[...] = dout.astype(BF16)

        @pl.when(pl.program_id(0) == 0)
        def _():
            gw_ref[...] = jnp.zeros_like(gw_ref)
            loss_ref[...] = jnp.zeros_like(loss_ref)

        gw_ref[...] += jnp.sum(dy * nrm, axis=0, keepdims=True)
        loss_ref[...] += (0.5 / d) * jnp.sum(jnp.sum(e * e, axis=-1, keepdims=True), axis=0, keepdims=True)

    row = pl.BlockSpec((tr, d), lambda i: (i, 0))
    return pl.pallas_call(
        body, name="out_loss", grid=(n // tr,),
        in_specs=[pl.BlockSpec((tr, kdim), lambda i: (i, 0)), pl.BlockSpec((kdim, d), lambda i: (0, 0)), row, row,
                  pl.BlockSpec((1, d), lambda i: (0, 0))],
        out_specs=[row, row, pl.BlockSpec((1, d), lambda i: (0, 0)), pl.BlockSpec((1, 1), lambda i: (0, 0))],
        out_shape=[jax.ShapeDtypeStruct((n, d), F32), jax.ShapeDtypeStruct((n, d), BF16),
                   jax.ShapeDtypeStruct((1, d), F32), jax.ShapeDtypeStruct((1, 1), F32)],
        compiler_params=_params(("arbitrary",), 40 * 2**20),
    )(mix, w_out, x, tgt, wf)


def _dh_rms_bwd(dproj, w_t, dh0, x, w, dout, tk):
    n, d = x.shape
    kdim = dproj.shape[1]
    tm = min(1024, n)
    tk = min(tk, kdim)
    nk = kdim // tk

    def body(a_ref, b_ref, dh0_ref, x_ref, w_ref, do_ref, dx_ref, gw_ref, acc):
        i, kk = pl.program_id(0), pl.program_id(1)
        part = _dot(a_ref[...], b_ref[...], NN)

        @pl.when(kk == 0)
        def _():
            acc[...] = part + dh0_ref[...]

        @pl.when(kk > 0)
        def _():
            acc[...] += part

        @pl.when((i == 0) & (kk == 0))
        def _():
            gw_ref[...] = jnp.zeros_like(gw_ref)

        @pl.when(kk == nk - 1)
        def _():
            xv, dhv = x_ref[...], acc[...]
            r = lax.rsqrt(jnp.mean(xv * xv, axis=-1, keepdims=True) + EPS)
            xn = xv * r
            dxn = dhv * w_ref[...]
            dx_ref[...] = r * (dxn - xn * jnp.mean(dxn * xn, axis=-1, keepdims=True)) + do_ref[...]
            gw_ref[...] += jnp.sum(dhv * xn, axis=0, keepdims=True)

    row = pl.BlockSpec((tm, d), lambda i, kk: (i, 0))
    one = pl.BlockSpec((1, d), lambda i, kk: (0, 0))
    return pl.pallas_call(
        body, name="dh_rms_bwd", grid=(n // tm, nk),
        in_specs=[pl.BlockSpec((tm, tk), lambda i, kk: (i, kk)), pl.BlockSpec((tk, d), lambda i, kk: (kk, 0)),
                  row, row, one, row],
        out_specs=[row, one],
        out_shape=[jax.ShapeDtypeStruct((n, d), F32), jax.ShapeDtypeStruct((1, d), F32)],
        scratch_shapes=[pltpu.VMEM((tm, d), F32)],
        compiler_params=_params(("arbitrary", "arbitrary"), 56 * 2**20),
    )(dproj, w_t, dh0, x, w, dout)


def _ij():
    i = lax.broadcasted_iota(jnp.int32, (CH, CH), 0)
    j = lax.broadcasted_iota(jnp.int32, (CH, CH), 1)
    return i, j


def _unit_lower_inverse(mats):
    i, j = _ij()
    eye = jnp.where(i == j, 1.0, 0.0)
    same16 = (i // 16) == (j // 16)
    same32 = (i // 32) == (j // 32)
    mm = lambda xs, ys: [_dot(x, y, NN, P_INV) for x, y in zip(xs, ys)]
    n1 = [jnp.where(same16, -a, 0.0) for a in mats]
    n2 = mm(n1, n1)
    n4 = mm(n2, n2)
    n8 = mm(n4, n4)
    t = [eye + x1 + x2 + x3 for x1, x2, x3 in zip(n1, n2, mm(n1, n2))]
    t = [x + y for x, y in zip(t, mm(t, n4))]
    t = [x + y for x, y in zip(t, mm(t, n8))]
    a1 = [jnp.where(same32 & jnp.logical_not(same16), a, 0.0) for a in mats]
    t = [x - y for x, y in zip(t, mm(t, mm(a1, t)))]
    a2 = [jnp.where(same32, 0.0, a) for a in mats]
    t = [x - y for x, y in zip(t, mm(t, mm(a2, t)))]
    return t


def _head_vectors(bg, bgt, h):
    bcol = bg[:, h:h + 1]
    gcol = bg[:, HEADS + h:HEADS + h + 1]
    grow = bgt[HEADS + h:HEADS + h + 1, :]
    return bcol, gcol, grow


def _decay(gcol, grow):
    i, j = _ij()
    return jnp.where(i >= j, jnp.exp(jnp.where(i >= j, gcol - grow, 0.0)), 0.0)


def _gdn_intra(q, k, v, bg, bgt):
    n = q.shape[0]
    nch = n // CH
    cps = 4 if nch % 4 == 0 else 1

    def body(q_ref, k_ref, v_ref, bg_ref, bgt_ref, u_ref, w_ref, p_ref, t_ref):
        i, j = _ij()
        items = [(ci, h) for ci in range(cps) for h in range(HEADS)]
        at = lambda ref, ci, h: ref.at[ci * CH:(ci + 1) * CH, h * DH:(h + 1) * DH]
        bgs = [bg_ref[ci * CH:(ci + 1) * CH, :] for ci in range(cps)]
        ks = [at(k_ref, ci, h)[...] for ci, h in items]
        vecs = [_head_vectors(bgs[ci], bgt_ref[ci], h) for ci, h in items]
        decs = [_decay(gcol, grow) for _, gcol, grow in vecs]
        kks = [_dot(kh, kh, NT, P_GRAM) for kh in ks]
        qks = [_dot(at(q_ref, ci, h)[...], kh, NT, P_GRAM) for (ci, h), kh in zip(items, ks)]
        ts = _unit_lower_inverse([jnp.where(i > j, bcol * kk * dec, 0.0)
                                  for (bcol, _, _), kk, dec in zip(vecs, kks, decs)])
        us = [_dot(t, at(v_ref, ci, h)[...] * bcol, NN, P_SOL) for t, (ci, h), (bcol, _, _) in zip(ts, items, vecs)]
        ws = [_dot(t, kh * (bcol * jnp.exp(gcol)), NN, P_SOL) for t, kh, (bcol, gcol, _) in zip(ts, ks, vecs)]
        for n_, (ci, h) in enumerate(items):
            p_ref[ci, h] = qks[n_] * decs[n_]
            t_ref[ci, h] = ts[n_].astype(BF16)
            at(u_ref, ci, h)[...] = us[n_]
            at(w_ref, ci, h)[...] = ws[n_].astype(BF16)

    row = pl.BlockSpec((cps * CH, GW), lambda c: (c, 0))
    sq = pl.BlockSpec((cps, HEADS, CH, CH), lambda c: (c, 0, 0, 0))
    big = jax.ShapeDtypeStruct((n, GW), F32)
    sqs = jax.ShapeDtypeStruct((nch, HEADS, CH, CH), F32)
    return pl.pallas_call(
        body, name="gdn_intra", grid=(nch // cps,),
        in_specs=[row, row, row, pl.BlockSpec((cps * CH, DH), lambda c: (c, 0)),
                  pl.BlockSpec((cps, DH, CH), lambda c: (c, 0, 0))],
        out_specs=[row, row, sq, sq],
        out_shape=[big, jax.ShapeDtypeStruct((n, GW), BF16), sqs, jax.ShapeDtypeStruct(sqs.shape, BF16)],
        compiler_params=_params(("parallel",)),
    )(q, k, v, bg, bgt)


def _gdn_scan(q, k, bg, u, w, p):
    n = q.shape[0]
    nch = n // CH
    cps = SCAN_CPS if nch % SCAN_CPS == 0 else 1

    def body(q_ref, k_ref, bg_ref, u_ref, w_ref, p_ref, o_ref, vn_ref, s_out, s_scr):
        @pl.when(pl.program_id(0) == 0)
        def _():
            s_scr[...] = jnp.zeros_like(s_scr)

        hs = range(HEADS)
        sls = [slice(h * DH, (h + 1) * DH) for h in hs]
        ss = [s_scr[h] for h in hs]
        for ci in range(cps):
            rs = slice(ci * CH, (ci + 1) * CH)
            bg = bg_ref[rs, :]
            gcols = [bg[:, HEADS + h:HEADS + h + 1] for h in hs]
            glasts = [g[CH - 1:CH, :] for g in gcols]
            wss = [_dot(w_ref[rs, sl], s, NN, P_SCAN) for sl, s in zip(sls, ss)]
            oqs = [_dot(q_ref[rs, sl] * jnp.exp(g), s, NN, P_SCAN) for sl, s, g in zip(sls, ss, gcols)]
            vns = [u_ref[rs, sl] - x for sl, x in zip(sls, wss)]
            ops = [_dot(p_ref[ci, h], vn, NN, P_SCAN) for h, vn in zip(hs, vns)]
            sns = [_dot(k_ref[rs, sl] * jnp.exp(gl - g), vn, TN, P_SCAN)
                   for sl, gl, g, vn in zip(sls, glasts, gcols, vns)]
            for h, sl in enumerate(sls):
                s_out[ci, :, sl] = ss[h].astype(BF16)
                vn_ref[rs, sl] = vns[h].astype(BF16)
                o_ref[rs, sl] = oqs[h] + ops[h]
            ss = [s * jnp.exp(gl) + sn for s, gl, sn in zip(ss, glasts, sns)]
        for h in hs:
            s_scr[h] = ss[h]

    row = pl.BlockSpec((cps * CH, GW), lambda c: (c, 0))
    big = jax.ShapeDtypeStruct((n, GW), F32)
    return pl.pallas_call(
        body, name="gdn_scan", grid=(nch // cps,),
        in_specs=[row, row, pl.BlockSpec((cps * CH, DH), lambda c: (c, 0)), row, row,
                  pl.BlockSpec((cps, HEADS, CH, CH), lambda c: (c, 0, 0, 0))],
        out_specs=[row, row, pl.BlockSpec((cps, DH, GW), lambda c: (c, 0, 0))],
        out_shape=[big, jax.ShapeDtypeStruct((n, GW), BF16), jax.ShapeDtypeStruct((nch, DH, GW), BF16)],
        scratch_shapes=[pltpu.VMEM((HEADS, DH, DH), F32)],
        compiler_params=_params(("arbitrary",)),
    )(q, k, bg, u, w, p)


def _gdn_scan_bwd(q, k, bg, w, p, vn, s_in, do):
    n = q.shape[0]
    nch = n // CH
    cps = SCAN_CPS if nch % SCAN_CPS == 0 else 1
    rev = lambda c: nch // cps - 1 - c

    def body(q_ref, k_ref, bg_ref, w_ref, p_ref, vn_ref, s_ref, do_ref,
             dqg_ref, dp_ref, du_ref, dw_ref, dks_ref, dgam_ref, ds_scr):
        @pl.when(pl.program_id(0) == 0)
        def _():
            ds_scr[...] = jnp.zeros_like(ds_scr)

        lane = _lane((1, DH))
        hs = range(HEADS)
        sls = [slice(h * DH, (h + 1) * DH) for h in hs]
        dss = [ds_scr[h] for h in hs]
        for ci in reversed(range(cps)):
            rs = slice(ci * CH, (ci + 1) * CH)
            bg = bg_ref[rs, :]
            gcols = [bg[:, HEADS + h:HEADS + h + 1] for h in hs]
            glasts = [g[CH - 1:CH, :] for g in gcols]
            ss = [s_ref[ci, :, sl] for sl in sls]
            dos = [do_ref[rs, sl] for sl in sls]
            vnl = [vn_ref[rs, sl] for sl in sls]
            dqgs = [_dot(d, s, NT, P_SCANB) for d, s in zip(dos, ss)]
            dps = [_dot(d, vn, NT, P_SCANB) for d, vn in zip(dos, vnl)]
            dvn1 = [_dot(p_ref[ci, h], d, TN, P_SCANB) for h, d in zip(hs, dos)]
            dvn2 = [_dot(k_ref[rs, sl] * jnp.exp(gl - g), ds, NN, P_SCANB)
                    for sl, gl, g, ds in zip(sls, glasts, gcols, dss)]
            dkss = [_dot(vn, ds, NT, P_SCANB) for vn, ds in zip(vnl, dss)]
            dsq = [_dot(q_ref[rs, sl] * jnp.exp(g), d, TN, P_SCANB) for sl, g, d in zip(sls, gcols, dos)]
            dvns = [a + b for a, b in zip(dvn1, dvn2)]
            dws = [_dot(dvn, s, NT, P_SCANB) for dvn, s in zip(dvns, ss)]
            dsw = [_dot(w_ref[rs, sl], dvn, TN, P_SCANB) for sl, dvn in zip(sls, dvns)]
            dgam = jnp.zeros((1, DH), F32)
            for h, sl in enumerate(sls):
                dqg_ref[rs, sl] = dqgs[h]
                dp_ref[ci, h] = dps[h]
                du_ref[rs, sl] = dvns[h].astype(BF16)
                dw_ref[rs, sl] = (-dws[h]).astype(BF16)
                dks_ref[rs, sl] = dkss[h]
                tot = jnp.sum(jnp.sum(dss[h] * ss[h], axis=-1, keepdims=True), axis=0, keepdims=True)
                dgam = dgam + jnp.where(lane == h, tot, 0.0)
            dgam_ref[ci] = jnp.broadcast_to(dgam, (8, DH))
            dss = [ds * jnp.exp(gl) + a - b for ds, gl, a, b in zip(dss, glasts, dsq, dsw)]
        for h in hs:
            ds_scr[h] = dss[h]

    row = pl.BlockSpec((cps * CH, GW), lambda c: (rev(c), 0))
    sq =pl.BlockSpec((cps, HEADS, CH, CH), lambda c: (rev(c), 0, 0, 0))
    big = jax.ShapeDtypeStruct((n, GW), F32)
    return pl.pallas_call(
        body, name="gdn_scan_bwd", grid=(nch // cps,),
        in_specs=[row, row, pl.BlockSpec((cps * CH, DH), lambda c: (rev(c), 0)), row, sq, row,
                  pl.BlockSpec((cps, DH, GW), lambda c: (rev(c), 0, 0)), row],
        out_specs=[row, sq, row, row, row, pl.BlockSpec((cps, 8, DH), lambda c: (rev(c), 0, 0))],
        out_shape=[big, jax.ShapeDtypeStruct((nch, HEADS, CH, CH), F32), jax.ShapeDtypeStruct((n, GW), BF16),
                   jax.ShapeDtypeStruct((n, GW), BF16), big,
                   jax.ShapeDtypeStruct((nch, 8, DH), F32)],
        scratch_shapes=[pltpu.VMEM((HEADS, DH, DH), F32)],
        compiler_params=_params(("arbitrary",)),
    )(q, k, bg, w, p, vn, s_in, do)


def _gdn_intra_bwd(q, k, v, bg, bgt, t, u, w, p, dqg, dp, du, dw, dks, dgam):
    n = q.shape[0]
    nch = n // CH
    cps = 2 if nch % 2 == 0 else 1

    def body(q_ref, k_ref, v_ref, bg_ref, bgt_ref, t_ref, u_ref, w_ref, p_ref,
             dqg_ref, dp_ref, du_ref, dw_ref, dks_ref, dgam_ref, dq_ref, dk_ref, dv_ref, dbg_ref):
        i, j = _ij()
        rows1 = lax.broadcasted_iota(jnp.int32, (CH, 1), 0)
        lane = _lane((CH, DH))
        rsum = lambda x: jnp.sum(x, axis=-1, keepdims=True)
        items = [(ci, h) for ci in range(cps) for h in range(HEADS)]
        at = lambda ref, it: ref.at[it[0] * CH:(it[0] + 1) * CH, it[1] * DH:(it[1] + 1) * DH]
        ld = lambda ref: [at(ref, it)[...] for it in items]
        bgs = [bg_ref[ci * CH:(ci + 1) * CH, :] for ci in range(cps)]
        qs, ks = ld(q_ref), ld(k_ref)
        vecs = [_head_vectors(bgs[ci], bgt_ref[ci], h) for ci, h in items]
        decs = [_decay(gcol, grow) for _, gcol, grow in vecs]
        ths = [t_ref[ci, h] for ci, h in items]
        drus = [_dot(th, x_, TN, P_BWD) for th, x_ in zip(ths, ld(du_ref))]
        drws = [_dot(th, x_, TN, P_BWD) for th, x_ in zip(ths, ld(dw_ref))]
        kks = [_dot(kh, kh, NT, P_GRAM) for kh in ks]
        da1 = [_dot(dru, x_, NT, P_BWD) for dru, x_ in zip(drus, ld(u_ref))]
        da2 = [_dot(drw, x_, NT, P_BWD) for drw, x_ in zip(drws, ld(w_ref))]
        das = [jnp.where(i > j, -(x_ + y_), 0.0) for x_, y_ in zip(da1, da2)]
        dkks = [da * bcol * dec for da, (bcol, _, _), dec in zip(das, vecs, decs)]
        dps = [dp_ref[ci, h] for ci, h in items]
        dqks = [dp_ * dec for dp_, dec in zip(dps, decs)]
        dq_ps = [_dot(dqk, kh, NN, P_BWD) for dqk, kh in zip(dqks, ks)]
        dk_ps = [_dot(dqk, qh, TN, P_BWD) for dqk, qh in zip(dqks, qs)]
        dk_as = [_dot(dkk, kh, NN, P_BWD) for dkk, kh in zip(dkks, ks)]
        dk_bs = [_dot(dkk, kh, TN, P_BWD) for dkk, kh in zip(dkks, ks)]
        bcols = [vc[0] for vc in vecs]
        gcols = [vc[1] for vc in vecs]
        gams = [jnp.exp(g) for g in gcols]
        glasts = [g[CH - 1:CH, :] for g in gcols]
        es = [jnp.exp(gl - g) for gl, g in zip(glasts, gcols)]
        kgs = [kh * gam for kh, gam in zip(ks, gams)]
        dqgs, dkss = ld(dqg_ref), ld(dks_ref)
        wks = [drw * kg for drw, kg in zip(drws, kgs)]
        kss = [dk_ * (kh * e) for dk_, kh, e in zip(dkss, ks, es)]
        r_beta = [rsum(dru * x_ + wk) for dru, x_, wk in zip(drus, ld(v_ref), wks)]
        r_ak = [rsum(da * kk * dec) for da, kk, dec in zip(das, kks, decs)]
        r_gc = [rsum(wk * bcol + dqg * (qh * gam) - ks_)
                for wk, bcol, dqg, qh, gam, ks_ in zip(wks, bcols, dqgs, qs, gams, kss)]
        tk_tot = [jnp.sum(jnp.sum(ks_, axis=0, keepdims=True), axis=-1, keepdims=True) for ks_ in kss]
        mdecs = [da * (bcol * kk * dec) + dp_ * p_ref[ci, h]
                 for (ci, h), da, bcol, kk, dec, dp_ in zip(items, das, bcols, kks, decs, dps)]
        r_md = [rsum(m) for m in mdecs]
        c_md = [rsum(jnp.where(i == j, jnp.sum(m, axis=0, keepdims=True), 0.0)) for m in mdecs]
        dbgs = [jnp.zeros((CH, DH), F32) for _ in range(cps)]
        for n_, (ci, h) in enumerate(items):
            at(dv_ref, (ci, h))[...] = bcols[n_] * drus[n_]
            at(dq_ref, (ci, h))[...] = gams[n_] * dqgs[n_] + dq_ps[n_]
            at(dk_ref, (ci, h))[...] = ((bcols[n_] * gams[n_]) * drws[n_] + dk_ps[n_] + dk_as[n_] + dk_bs[n_]
                                        + dkss[n_] * es[n_])
            dbeta = r_beta[n_] + r_ak[n_]
            dglast = tk_tot[n_] + dgam_ref[ci, 0:1, h:h + 1] * jnp.exp(glasts[n_])
            dgc = r_gc[n_] + r_md[n_] - c_md[n_] + jnp.where(rows1 == CH - 1, dglast, 0.0)
            dbgs[ci] = dbgs[ci] + jnp.where(lane == h, dbeta, 0.0) + jnp.where(lane == HEADS + h, dgc, 0.0)
        for ci in range(cps):
            dbg_ref[ci * CH:(ci + 1) * CH, :] = dbgs[ci]

    row = pl.BlockSpec((cps * CH, GW), lambda c: (c, 0))
    sq = pl.BlockSpec((cps, HEADS, CH, CH), lambda c: (c, 0, 0, 0))
    small = pl.BlockSpec((cps * CH, DH), lambda c: (c, 0))
    big = jax.ShapeDtypeStruct((n, GW), F32)
    return pl.pallas_call(
        body, name="gdn_intra_bwd", grid=(nch // cps,),
        in_specs=[row, row, row, small, pl.BlockSpec((cps, DH, CH), lambda c: (c, 0, 0)), sq, row, row, sq,
                  row, sq, row, row, row, pl.BlockSpec((cps, 8, DH), lambda c: (c, 0, 0))],
        out_specs=[row, row, row, small],
        out_shape=[big, big, big, jax.ShapeDtypeStruct((n, DH), F32)],
        compiler_params=_params(("parallel",)),
    )(q, k, v, bg, bgt, t, u, w, p, dqg, dp, du, dw, dks, dgam)


def _local_step(x, tgt, h, w_g, cqw, late, norm_in_w, ad, gdn_norm_w, conv_b, final_norm_w,
                on_grad_c=None, on_grad_g=None, on_q=None):
    proj_g = _matmul(h, w_g, NT, F32, 512, 1408, 1024, "mm_proj_g", n=GW_COLS, b_outer=True)
    q, k, v = _prep_qkv(proj_g, cqw)
    if on_q is not None:
        q = on_q(q)
    bg, bgt = _prep_bg(proj_g, ad)
    u, w, p, t = _gdn_intra(q, k, v, bg, bgt)
    o, vn, s_in = _gdn_scan(q, k, bg, u, w, p)
    w_c, w_out, conv_w = late(o)
    proj_c = _matmul(h, w_c, NT, F32, 512, 1024, 1024, "mm_proj_c", n=CW_COLS, b_outer=True)
    mix = _conv_branch(proj_c, conv_w, conv_b, _gdn_out(o, proj_g, gdn_norm_w))
    dout, dout_b, g_fn, loss = _out_loss(mix, w_out, x, tgt, final_norm_w)

    g_wout = _matmul(mix, dout_b, TN, BF16, 512, 512, 2048, "mm_gwout")
    do, dproj_g, g_gn = _gdn_out_bwd(o, proj_g, gdn_norm_w, dout_b, w_out)
    dproj_c, g_cw, g_cb = _conv_branch_bwd(proj_c, conv_w, conv_b, dout_b, w_out)
    g_c = _matmul(dproj_c, h, TN, BF16, 1024, 512, 2048, "mm_gwin_c")
    if on_grad_c is not None:
        do = on_grad_c(g_c, g_wout, do)
    dqg, dp, du, dw, dks, dgam = _gdn_scan_bwd(q, k, bg, w, p, vn, s_in, do)
    dq, dk, dv, dbg = _gdn_intra_bwd(q, k, v, bg, bgt, t, u, w, p, dqg, dp, du, dw, dks, dgam)
    dproj_g, gq, gk, gv = _prep_qkv_bwd(proj_g, cqw, dq, dk, dv, dproj_g)
    dproj_g, g_al, g_dt = _prep_bg_bwd(proj_g, ad, dbg, dproj_g)
    g_g = _matmul(dproj_g, h, TN, BF16, 1408, 512, 2048, "mm_gwin_g")
    if on_grad_g is not None:
        dproj_g = on_grad_g(g_g, dproj_g)
    dh = _matmul(dproj_g, w_g, NN, F32, 1024, 1024, 1408, "mm_dh_g")
    gx, g_nin = _dh_rms_bwd(dproj_c, w_c, dh, x, norm_in_w, dout, 1024)
    small = dict(nin=g_nin, cb=g_cb, fn=g_fn, al=g_al, dt=g_dt, gn=g_gn, cq=(gq, gk, gv), cw=g_cw, loss=loss)
    return gx, small, (g_g, g_c, g_wout)


def _place():
    x, y, c = lax.axis_index("x"), lax.axis_index("y"), lax.axis_index("c")
    chips = [(1 - x, y), (x, 1 - y), (1 - x, 1 - y)]
    return x, y, c, chips


def _blk(ref, b):
    if isinstance(b, int):
        return ref.at[b * DH:(b + 1) * DH, :]
    return ref.at[pl.ds(pl.multiple_of(b * DH, DH), DH), :]


HBM = pl.BlockSpec(memory_space=pltpu.HBM)
SEM = pl.BlockSpec(memory_space=pltpu.SEMAPHORE)
EFFECT = pltpu.SideEffectType.DATAFLOW_SIDE_EFFECTING


def _split_start(name, issue, bufs, n_sems):
    nbuf = len(bufs)

    def body(*refs):
        issue(refs[:nbuf], refs[nbuf], refs[nbuf + 1])
        refs[-1][...] = jnp.zeros_like(refs[-1])

    out = pl.pallas_call(
        body, name=name,
        out_shape=(pltpu.SemaphoreType.DMA((n_sems,)), pltpu.SemaphoreType.DMA((n_sems,)),
                   *[pltpu.HBM(b.shape, b.dtype) for b in bufs], jax.ShapeDtypeStruct((8, DH), F32)),
        in_specs=[HBM] * nbuf,
        out_specs=(SEM, SEM, *[HBM] * nbuf, pl.BlockSpec(memory_space=pltpu.VMEM)),
        input_output_aliases={a: 2 + a for a in range(nbuf)},
        compiler_params=pltpu.CompilerParams(has_side_effects=EFFECT),
    )(*[pltpu.with_memory_space_constraint(b, pltpu.HBM) for b in bufs])
    return out[0], out[1], list(out[2:2 + nbuf]), out[-1]


def _split_wait(name, await_, send_sems, recv_sems, bufs, after):
    nbuf = len(bufs)
    after = list(after) if isinstance(after, (list, tuple)) else [after]

    def body(*refs):
        await_(refs[:nbuf], refs[nbuf], refs[nbuf + 1])

    out = pl.pallas_call(
        body, name=name,
        out_shape=tuple(pltpu.HBM(b.shape, b.dtype) for b in bufs),
        in_specs=[HBM] * nbuf + [SEM, SEM] + [ANY] * len(after), out_specs=tuple([HBM] * nbuf),
        input_output_aliases={a: a for a in range(nbuf)},
        compiler_params=pltpu.CompilerParams(has_side_effects=EFFECT),
    )(*bufs, send_sems, recv_sems, *after)
    return list(out)


def _phase_blocks(chip, phase, edges, parity=None):
    return [(b, blk) for b, (grp, blk) in enumerate(_shard_blocks(chip, edges))
            if grp == phase and (parity is None or b % 2 == parity)]


def _cols(ref, nblk):
    return ref.at[0:nblk * DH, :]


def _block_table(chip, edges, spare_g, spare_c):
    rows = []
    for s in range(4):
        sb = _shard_blocks(s, edges)
        rows.append([[blk if grp == "g" else spare_g for grp, blk in sb],
                     [blk if grp == "c" else spare_c for grp, blk in sb],
                     [int(grp == "g") for grp, _ in sb], [s] * ALIGNED_BLOCKS])
    return jnp.asarray(rows, jnp.int32)[chip]


def _place_own(a_shard, wo, cq, cw, bufs):
    d = a_shard.shape[1]
    chip = 2 * lax.axis_index("x") + lax.axis_index("y")

    def body(t_ref, a_ref, wo_ref, cq_ref, cw_ref, *refs):
        wg_ref, wc_ref, wog_ref, cqg_ref, cwg_ref = refs[5:]
        wg_ref[...] = a_ref[...]
        wc_ref[...] = a_ref[...]

        @pl.when(pl.program_id(0) == 0)
        def _():
            wog_ref[0] = wo_ref[...]
            cqg_ref[0] = cq_ref[...]
            cwg_ref[0] = cw_ref[...]

    whole = lambda s: pl.BlockSpec(s.shape, lambda b, t: (0,) * s.ndim)
    slot = lambda s: pl.BlockSpec((1,) + s.shape, lambda b, t: (t[3, 0],) + (0,) * s.ndim)
    return pl.pallas_call(
        body, name="place_own",
        grid_spec=pltpu.PrefetchScalarGridSpec(
            num_scalar_prefetch=1, grid=(ALIGNED_BLOCKS,),
            in_specs=[pl.BlockSpec((DH, d), lambda b, t: (b, 0)), whole(wo), whole(cq), whole(cw)] + [ANY] * 5,
            out_specs=[pl.BlockSpec((DH, d), lambda b, t: (t[0, b], 0)),
                       pl.BlockSpec((DH, d), lambda b, t: (t[1, b], 0)), slot(wo), slot(cq), slot(cw)]),
        out_shape=[jax.ShapeDtypeStruct(b.shape, b.dtype) for b in bufs],
        input_output_aliases={5 + a: a for a in range(5)},
        compiler_params=_params(("arbitrary",)),
    )(_block_table(chip, True, G_SPARE, C_SPARE), a_shard, wo, cq, cw, *bufs)


def _tie(x, token, name):
    def body(x_ref, t_ref, o_ref):
        del x_ref, t_ref, o_ref

    return pl.pallas_call(
        body, name=name, in_specs=[ANY, ANY], out_specs=ANY,
        out_shape=jax.ShapeDtypeStruct(x.shape, x.dtype), input_output_aliases={0: 0},
    )(x, token)


def _gather_start(phase, a_shard, w_grp, singles):
    ns = len(singles)

    def issue(refs, send_sems, recv_sems):
        a_ref, w_ref = refs[0], refs[1]
        x, y, c, chips = _place()
        mine = 2 * x + y
        for jj, (px, py) in enumerate(chips):
            to = dict(device_id=(px, py, c), device_id_type=MESH)
            for a in range(ns):
                pltpu.make_async_remote_copy(
                    src_ref=refs[2 + 2 * a], dst_ref=refs[3 + 2 * a].at[mine],
                    send_sem=send_sems.at[(1 + ns) * jj + 1 + a], recv_sem=recv_sems.at[(1 + ns) * jj + 1 + a],
                    **to).start()
        for s in range(4):
            for par in range(2):
                blocks = _phase_blocks(s, phase, True, par)
                if blocks:
                    @pl.when((mine == s) & (c == par))
                    def _():
                        for b, blk in blocks:
                            for jj, (px, py) in enumerate(chips[:2]):
                                pltpu.make_async_remote_copy(
                                    src_ref=_blk(a_ref, b), dst_ref=_blk(w_ref, blk),
                                    send_sem=send_sems.at[(1 + ns) * jj], recv_sem=recv_sems.at[(1 + ns) * jj],
                                    device_id=(px, py, c), device_id_type=MESH).start()

    bufs = [a_shard, w_grp] + [t for pair in singles for t in pair]
    return _split_start("gather_start_" + phase, issue, bufs, 3 * (1 + ns))


def _gather_wait(phase, send_sems, recv_sems, bufs, after):
    ns = (len(bufs) - 2) // 2

    def await_(refs, send_sems, recv_sems):
        a_ref, w_ref = refs[0], refs[1]
        x, y, c, chips = _place()
        mine = 2 * x + y
        for jj, (px, py) in enumerate(chips):
            to = dict(device_id=(px, py, c), device_id_type=MESH)
            peer = 2 * px + py
            for a in range(ns):
                cp = pltpu.make_async_remote_copy(
                    src_ref=refs[2 + 2 * a], dst_ref=refs[3 + 2 * a].at[mine],
                    send_sem=send_sems.at[(1 + ns) * jj + 1 + a], recv_sem=recv_sems.at[(1 + ns) * jj + 1 + a], **to)
                cp.wait_recv()
                cp.wait_send()
            for s in range(4):
                for par in range(2):
                    nblk = len(_phase_blocks(s, phase, True, par))
                    if nblk and jj < 2:
                        both = pltpu.make_async_remote_copy(
                            src_ref=_cols(a_ref, nblk), dst_ref=_cols(w_ref, nblk),
                            send_sem=send_sems.at[(1 + ns) * jj], recv_sem=recv_sems.at[(1 + ns) * jj], **to)

                        @pl.when((peer == s) & (c == par))
                        def _():
                            both.wait_recv()

                        @pl.when((mine == s) & (c == par))
                        def _():
                            both.wait_send()

    return _split_wait("gather_wait_" + phase, await_, send_sems, recv_sems, bufs, after)


def _diag_forward_parts(phase, base):
    def half(s, par, which):
        return [blk for b, blk in _phase_blocks(s, phase, True, par) if (b // 2) % 2 == which]

    def each(w_ref, send_sems, recv_sems, start):
        x, y, c, (xn, yn, dg) = _place()
        arriving = 2 * dg[0] + dg[1]
        for which, (src, dst) in enumerate(((xn, yn), (yn, xn))):
            leaving = 2 * src[0] + src[1]
            via = dict(send_sem=send_sems.at[base + which], recv_sem=recv_sems.at[base + which],
                       device_id=(dst[0], dst[1], c), device_id_type=MESH)
            for s in range(4):
                for par in range(2):
                    blocks = half(s, par, which)
                    if not blocks:
                        continue
                    if start:
                        @pl.when((leaving == s) & (c == par))
                        def _():
                            for blk in blocks:
                                pltpu.make_async_remote_copy(
                                    src_ref=_blk(w_ref, blk), dst_ref=_blk(w_ref, blk), **via).start()
                        continue
                    whole = pltpu.make_async_remote_copy(
                        src_ref=_cols(w_ref, len(blocks)), dst_ref=_cols(w_ref, len(blocks)), **via)

                    @pl.when((arriving == s) & (c == par))
                    def _():
                        whole.wait_recv()

                    @pl.when((leaving == s) & (c == par))
                    def _():
                        whole.wait_send()

    issue = lambda refs, send_sems, recv_sems: each(refs[0], send_sems, recv_sems, True)
    await_ = lambda refs, send_sems, recv_sems: each(refs[0], send_sems, recv_sems, False)
    return issue, await_


def _sibling_forward_parts(phase, peers):
    def each(w_ref, send_sems, recv_sems, start):
        x, y, c, chips = _place()
        to = dict(device_id=(x, y, 1 - c), device_id_type=MESH)
        for jj, (px, py) in enumerate(chips):
            if jj not in peers:
                continue
            peer = 2 * px + py
            for s in range(4):
                for par in range(2):
                    mine_blocks = _phase_blocks(s, phase, True, par)
                    theirs = len(_phase_blocks(s, phase, True, 1 - par))
                    if not (mine_blocks or theirs):
                        continue

                    @pl.when((peer == s) & (c == par))
                    def _():
                        if start:
                            for _, blk in mine_blocks:
                                pltpu.make_async_remote_copy(
                                    src_ref=_blk(w_ref, blk), dst_ref=_blk(w_ref, blk),
                                    send_sem=send_sems.at[jj], recv_sem=recv_sems.at[jj], **to).start()
                            return
                        if theirs:
                            pltpu.make_async_remote_copy(
                                src_ref=_cols(w_ref, theirs), dst_ref=_cols(w_ref, theirs),
                                send_sem=send_sems.at[jj], recv_sem=recv_sems.at[jj], **to).wait_recv()
                        if mine_blocks:
                            pltpu.make_async_remote_copy(
                                src_ref=_cols(w_ref, len(mine_blocks)), dst_ref=_cols(w_ref, len(mine_blocks)),
                                send_sem=send_sems.at[jj], recv_sem=recv_sems.at[jj], **to).wait_send()

    issue = lambda refs, send_sems, recv_sems: each(refs[0], send_sems, recv_sems, True)
    await_ = lambda refs, send_sems, recv_sems: each(refs[0], send_sems, recv_sems, False)
    return issue, await_


FORWARD_SEMS = 5


def _forward_first_parts(phase):
    issue_d, await_d = _diag_forward_parts(phase, 3)
    issue_s, await_s = _sibling_forward_parts(phase, (0, 1))

    def issue(refs, send_sems, recv_sems):
        issue_d(refs, send_sems, recv_sems)
        issue_s(refs, send_sems, recv_sems)

    def await_(refs, send_sems, recv_sems):
        await_d(refs, send_sems, recv_sems)
        await_s(refs, send_sems, recv_sems)

    return issue, await_


def _forward(phase, w_grp, first):
    steps = ([_forward_first_parts(phase)] if first else []) + [_sibling_forward_parts(phase, (2,))]

    def body(w_in_ref, w_ref, send_sems, recv_sems):
        del w_in_ref
        for issue, await_ in steps:
            issue([w_ref], send_sems, recv_sems)
            await_([w_ref], send_sems, recv_sems)

    return pl.pallas_call(
        body, name=("forward_" if first else "forward_last_") + phase, in_specs=[ANY], out_specs=ANY,
        out_shape=jax.ShapeDtypeStruct(w_grp.shape, w_grp.dtype), input_output_aliases={0: 0},
        scratch_shapes=[pltpu.SemaphoreType.DMA((FORWARD_SEMS,)), pltpu.SemaphoreType.DMA((FORWARD_SEMS,))],
    )(w_grp)


def _merge_edges(w, edge0, mixed, name):
    d = w.shape[1]

    def body(e_ref, o_ref):
        o_ref[...] = e_ref[0:DH, :] + e_ref[DH:2 * DH, :]

    def to_block(i):
        r = mixed[-1]
        for kk in range(len(mixed) - 2, -1, -1):
            r = jnp.where(i == kk, mixed[kk], r)
        return r

    return pl.pallas_call(
        body, name=name, grid=(len(mixed),),
        in_specs=[pl.BlockSpec((2 * DH, d), lambda i: (edge0 // 2 + i, 0))],
        out_specs=pl.BlockSpec((DH, d), lambda i: (to_block(i), 0)),
        out_shape=jax.ShapeDtypeStruct(w.shape, w.dtype),
        input_output_aliases={0: 0},
        compiler_params=_params(("arbitrary",)),
    )(w)


def _scatter_start(phase, g_grp, land, singles, halved=False):
    ns = len(singles)

    def issue(refs, send_sems, recv_sems):
        g_ref, land_ref = refs[0], refs[1]
        x, y, c, chips = _place()
        for jj, (px, py) in enumerate(chips):
            to = dict(device_id=(px, py, c), device_id_type=MESH)
            peer = 2 * px + py
            for a in range(ns):
                pltpu.make_async_remote_copy(
                    src_ref=refs[2 + 2 * a].at[peer], dst_ref=refs[3 + 2 * a].at[jj],
                    send_sem=send_sems.at[(1 + ns) * jj + 1 + a], recv_sem=recv_sems.at[(1 + ns) * jj + 1 + a],
                    **to).start()
            for s in range(4):
                for par in ((0, 1) if halved else (None,)):
                    blocks = _phase_blocks(s, phase, False, par)
                    if blocks:
                        @pl.when((peer == s) if par is None else ((peer == s) & (c == par)))
                        def _():
                            for b, blk in blocks:
                                pltpu.make_async_remote_copy(
                                    src_ref=_blk(g_ref, blk), dst_ref=_blk(land_ref.at[jj], b),
                                    send_sem=send_sems.at[(1 + ns) * jj], recv_sem=recv_sems.at[(1 + ns) * jj],
                                    **to).start()

    bufs = [g_grp, land] + [t for pair in singles for t in pair]
    return _split_start("scatter_start_" + phase, issue, bufs, 3 * (1 + ns))


def _scatter_wait(phase, send_sems, recv_sems, bufs, after, halved=False):
    ns = (len(bufs) - 2) // 2

    def await_(refs, send_sems, recv_sems):
        g_ref, land_ref = refs[0], refs[1]
        x, y, c, chips = _place()
        mine = 2 * x + y
        for jj, (px, py) in enumerate(chips):
            to = dict(device_id=(px, py, c), device_id_type=MESH)
            peer = 2 * px + py
            for a in range(ns):
                cp = pltpu.make_async_remote_copy(
                    src_ref=refs[2 + 2 * a].at[peer], dst_ref=refs[3 + 2 * a].at[jj],
                    send_sem=send_sems.at[(1 + ns) * jj + 1 + a], recv_sem=recv_sems.at[(1 + ns) * jj + 1 + a], **to)
                cp.wait_recv()
                cp.wait_send()
            for s in range(4):
                for par in ((0, 1) if halved else (None,)):
                    nblk = len(_phase_blocks(s, phase, False, par))
                    if nblk:
                        both = pltpu.make_async_remote_copy(
                            src_ref=_cols(g_ref, nblk), dst_ref=_cols(land_ref.at[jj], nblk),
                            send_sem=send_sems.at[(1 + ns) * jj], recv_sem=recv_sems.at[(1 + ns) * jj], **to)

                        @pl.when((mine == s) if par is None else ((mine == s) & (c == par)))
                        def _():
                            both.wait_recv()

                        @pl.when((peer == s) if par is None else ((peer == s) & (c == par)))
                        def _():
                            both.wait_send()

    return _split_wait("scatter_wait_" + phase, await_, send_sems, recv_sems, bufs, after)


def _needed_blocks(phase, parity):
    return sorted({blk for s in range(4) for _, blk in _phase_blocks(s, phase, False, parity)})


def _pair_reduce(phase, g_grp):
    n, d = g_grp.shape

    def swap(g_ref, sib_ref, send_sem, recv_sem):
        x, y, c, _ = _place()
        to = dict(device_id=(x, y, 1 - c), device_id_type=MESH)
        for par in range(2):
            give, get = _needed_blocks(phase, 1 - par), _needed_blocks(phase, par)

            @pl.when(c == par)
            def _():
                for blk in give:
                    pltpu.make_async_remote_copy(src_ref=_blk(g_ref, blk), dst_ref=_blk(sib_ref, blk),
                                                 send_sem=send_sem, recv_sem=recv_sem, **to).start()
                pltpu.make_async_remote_copy(src_ref=_cols(g_ref, len(get)), dst_ref=_cols(sib_ref, len(get)),
                                             send_sem=send_sem, recv_sem=recv_sem, **to).wait_recv()
                pltpu.make_async_remote_copy(src_ref=_cols(g_ref, len(give)), dst_ref=_cols(sib_ref, len(give)),
                                             send_sem=send_sem, recv_sem=recv_sem, **to).wait_send()

    sib = pl.pallas_call(
        swap, name="pair_swap_" + phase, in_specs=[ANY], out_specs=ANY,
        out_shape=jax.ShapeDtypeStruct((n, d), g_grp.dtype),
        scratch_shapes=[pltpu.SemaphoreType.DMA, pltpu.SemaphoreType.DMA],
    )(*_in_hbm(g_grp))

    lists = [_needed_blocks(phase, par) for par in range(2)]
    longest = max(len(t) for t in lists)
    table = jnp.asarray([t + [t[-1]] * (longest - len(t)) for t in lists], jnp.int32)[lax.axis_index("c")]

    def add(t_ref, a_ref, b_ref, o_ref):
        o_ref[...] = (a_ref[...].astype(F32) + b_ref[...].astype(F32)).astype(o_ref.dtype)

    blk = pl.BlockSpec((DH, d), lambda i, t: (t[i], 0))
    return pl.pallas_call(
        add, name="pair_add_" + phase,
        grid_spec=pltpu.PrefetchScalarGridSpec(num_scalar_prefetch=1, grid=(longest,),
                                               in_specs=[blk, blk], out_specs=blk),
        out_shape=jax.ShapeDtypeStruct((n, d), g_grp.dtype),
        compiler_params=_params(("arbitrary",)),
    )(table, g_grp, sib)


def _sum_shard(g_g, g_c, land):
    d = g_g.shape[1]
    chip = 2 * lax.axis_index("x") + lax.axis_index("y")

    def body(t_ref, gg_ref, gc_ref, land_ref, o_ref):
        b = pl.program_id(0)
        in_g = t_ref[2, b] == 1
        own = jnp.where(in_g, gg_ref[...].astype(F32), gc_ref[...].astype(F32))
        for jj in range(3):
            own = own + land_ref[jj].astype(F32)
        o_ref[...] = jnp.where(in_g & (b % 2 != lax.axis_index("c")), 0.0, own)

    return pl.pallas_call(
        body, name="sum_w_in",
        grid_spec=pltpu.PrefetchScalarGridSpec(
            num_scalar_prefetch=1, grid=(ALIGNED_BLOCKS,),
            in_specs=[pl.BlockSpec((DH, d), lambda b, t: (t[0, b], 0)), pl.BlockSpec((DH, d), lambda b, t: (t[1, b], 0)),
                      pl.BlockSpec((3, DH, d), lambda b, t: (0, b, 0))],
            out_specs=pl.BlockSpec((DH, d), lambda b, t: (b, 0))),
        out_shape=jax.ShapeDtypeStruct((ALIGNED_W, d), F32),
        compiler_params=_params(("arbitrary",)),
    )(_block_table(chip, False, 0, 0), g_g, g_c, land)


def _sum_rows(stack, land, rows):
    _, r, d = stack.shape
    rows = min(rows, r)
    chip = 2 * lax.axis_index("x") + lax.axis_index("y")

    def body(t_ref, own_ref, land_ref, o_ref):
        acc = own_ref[0].astype(F32)
        for jj in range(3):
            acc = acc + land_ref[jj].astype(F32)
        o_ref[...] = acc

    return pl.pallas_call(
        body, name="sum_w_out",
        grid_spec=pltpu.PrefetchScalarGridSpec(
            num_scalar_prefetch=1, grid=(r // rows,),
            in_specs=[pl.BlockSpec((1, rows, d), lambda i, t: (t[0], i, 0)),
                      pl.BlockSpec((3, rows, d), lambda i, t: (0, i, 0))],
            out_specs=pl.BlockSpec((rows, d), lambda i, t: (i, 0))),
        out_shape=jax.ShapeDtypeStruct((r, d), F32),
        compiler_params=_params(("arbitrary",)),
    )(jnp.reshape(chip, (1,)).astype(jnp.int32), stack, land)


def _exchange_parts(n_swap, with_pack):
    def copies(refs, send_sems, recv_sems):
        x, y, c, _ = _place()
        me = 4 * x + 2 * y + c
        cps = [pltpu.make_async_remote_copy(
            src_ref=refs[2 * a], dst_ref=refs[2 * a + 1], send_sem=send_sems.at[a], recv_sem=recv_sems.at[a],
            device_id=(x, y, 1 - c), device_id_type=MESH) for a in range(n_swap)]
        if with_pack:
            pack_ref, packs = refs[2 * n_swap], refs[2 * n_swap + 1]
            for r in range(1, 8):
                dx, dy, dc = (r >> 2) & 1, (r >> 1) & 1, r & 1
                peer = (x + dx - 2 * x * dx, y + dy - 2 * y * dy, c + dc - 2 * c * dc)
                cps.append(pltpu.make_async_remote_copy(
                    src_ref=pack_ref, dst_ref=packs.at[me], send_sem=send_sems.at[n_swap + r - 1],
                    recv_sem=recv_sems.at[n_swap + r - 1], device_id=peer, device_id_type=MESH))
        return cps

    def issue(refs, send_sems, recv_sems):
        for cp in copies(refs, send_sems, recv_sems):
            cp.start()

    def await_(refs, send_sems, recv_sems):
        cps = copies(refs, send_sems, recv_sems)
        for cp in cps:
            cp.wait_recv()
        for cp in cps:
            cp.wait_send()

    return issue, await_, n_swap + (7 if with_pack else 0)


def _sum_packs(pack, packs):
    x, y, c = lax.axis_index("x"), lax.axis_index("y"), lax.axis_index("c")
    me = jnp.reshape(4 * x + 2 * y + c, (1,)).astype(jnp.int32)

    def body(me_ref, own_ref, p_ref, o_ref):
        acc = jnp.where(me_ref[0] == 0, own_ref[...], p_ref[0])
        for d in range(1, 8):
            acc = acc + jnp.where(me_ref[0] == d, own_ref[...], p_ref[d])
        o_ref[...] = acc

    full = lambda s: pl.BlockSpec(s.shape, lambda i, t: (0,) * s.ndim)
    return pl.pallas_call(
        body, name="sum_packs",
        grid_spec=pltpu.PrefetchScalarGridSpec(num_scalar_prefetch=1, grid=(1,), in_specs=[full(pack), full(packs)],
                                               out_specs=full(pack)),
        out_shape=jax.ShapeDtypeStruct(pack.shape, F32),
    )(me, pack, packs)


def _adamw_update(g, w_ref, m_ref, v_ref, go, do, mo, vo):
    c1 = 1.0 / (1.0 - ADAM_B1 ** ADAM_STEP)
    c2 = 1.0 / (1.0 - ADAM_B2 ** ADAM_STEP)
    mn = ADAM_B1 * m_ref[...] + (1.0 - ADAM_B1) * g
    vn = ADAM_B2 * v_ref[...] + (1.0 - ADAM_B2) * (g * g)
    go[...] = g
    mo[...] = mn
    vo[...] = vn
    do[...] = -ADAM_LR * ((mn * c1) / (jnp.sqrt(vn * c2) + ADAM_EPS) + ADAM_WD * w_ref[...])


def _adamw(w, m, v, g1, g2, rows, name):
    r, cdim = w.shape
    rows = min(rows, r)

    def body(*refs):
        n_in = 4 if g2 is None else 5
        w_ref, m_ref, v_ref, g_ref = refs[:4]
        g = g_ref[...] if g2 is None else g_ref[...] + refs[4][...]
        _adamw_update(g, w_ref, m_ref, v_ref, *refs[n_in:n_in + 4])

    blk = pl.BlockSpec((rows, cdim), lambda i: (i, 0))
    args = [w, m, v, g1] + ([] if g2 is None else [g2])
    shp = jax.ShapeDtypeStruct((r, cdim), F32)
    return pl.pallas_call(
        body, name=name, grid=(r // rows,),
        in_specs=[blk] * len(args), out_specs=[blk] * 4, out_shape=[shp] * 4,
        compiler_params=_params(("parallel",), 20 * rows * cdim * 4 + 8 * 2**20),
    )(*_in_hbm(*args))


def _adamw_shard(wt, mt, vt, g1, g2):
    r, d = wt.shape
    cols = min(256, d)

    def body(w_ref, m_ref, v_ref, g_ref, g2_ref, go, do, mo, vo, pad_ref):
        chip = 2 * lax.axis_index("x") + lax.axis_index("y")
        back = [(ALIGNED_W - s) % ALIGNED_W for s in SHIFTS]
        pad_ref[...] = pltpu.roll(g_ref[...] + g2_ref[...], _by_chip(chip, back), 0)
        outs = [o.at[:, 0, :] for o in (go, do, mo, vo)]
        _adamw_update(pad_ref[0:r, :], w_ref, m_ref, v_ref, *outs)

    blk = pl.BlockSpec((r, cols), lambda i: (0, i))
    gblk = pl.BlockSpec((ALIGNED_W, cols), lambda i: (0, i))
    oblk = pl.BlockSpec((r, 1, cols), lambda i: (0, 0, i))
    shp = jax.ShapeDtypeStruct((r, 1, d), F32)
    return pl.pallas_call(
        body, name="adamw_w_in", grid=(d // cols,),
        in_specs=[blk] * 3 + [gblk] * 2, out_specs=[oblk] * 4, out_shape=[shp] * 4,
        scratch_shapes=[pltpu.VMEM((ALIGNED_W, cols), F32)],
        compiler_params=_params(("parallel",), 24 * ALIGNED_W * cols * 4 + 8 * 2**20),
    )(wt, mt, vt, g1, g2)


def _pad_lanes(a, width):
    return jnp.pad(a, ((0, 0), (0, width - a.shape[1])))


def _gathered_to_full(g):
    return jnp.transpose(g, (1, 0, 2)).reshape(g.shape[1], 4 * g.shape[2])


def _row(a):
    return _pad_lanes(a.reshape(1, -1), 1024)


def _small_pack(nin, cb, fn, al, dt, gn, cqw_shard, cw_shard):
    ad = jnp.concatenate([al.reshape(1, -1), dt.reshape(1, -1)], axis=1)
    rows = [_row(nin), _row(cb), _row(fn), _row(ad), _row(gn), cqw_shard.reshape(3, 1024), _row(cw_shard)]
    out = jnp.concatenate(rows, axis=0)
    return jnp.pad(out, ((0, 16 - out.shape[0]), (0, 0)))


def kernel(x, norm_in_w, w_in, conv_qkv_w, A_log, dt_bias, gdn_norm_w, conv_w, conv_b, w_out, final_norm_w, loss_target, m_norm_in_w, m_w_in, m_conv_qkv_w, m_A_log, m_dt_bias, m_gdn_norm_w, m_conv_w, m_conv_b, m_w_out, m_final_norm_w, v_norm_in_w, v_w_in, v_conv_qkv_w, v_A_log, v_dt_bias, v_gdn_norm_w, v_conv_w, v_conv_b, v_w_out, v_final_norm_w):
    chip = 2 * lax.axis_index("x") + lax.axis_index("y")
    a_shard = _align_shard(jnp.transpose(w_in, (2, 0, 1)))
    d_model = x.shape[-1]
    stack = lambda s: lax.empty((4,) + s.shape, s.dtype)
    wg0 = lax.empty((WG_BLOCKS * DH, d_model), BF16)
    wc0 = lax.empty((WC_BLOCKS * DH, d_model), BF16)
    ss_g, rs_g, bufs_g, tok_g = _gather_start("g", a_shard, wg0, [(conv_qkv_w[0], stack(conv_qkv_w[0]))])
    wo_b = _cast_bf16(w_out[0], 256, "cast_w_out", tok_g)
    ss_c, rs_c, bufs_c, tok_c = _gather_start("c", bufs_g[0], wc0,
                                              [(conv_w[0], stack(conv_w[0])), (wo_b, stack(wo_b))])
    wg1, wc1, wog1, cqg1, cwg1 = _place_own(bufs_c[0], bufs_c[4], bufs_g[2], bufs_c[2],
                                            [bufs_g[1], bufs_c[1], bufs_c[5], bufs_g[3], bufs_c[3]])
    x0 = x[0]
    h = _rms_in(x0, _tie(_tie(norm_in_w, tok_g, "after_gather_start_g"), tok_c, "after_gather_start_c"))
    adam_in = [jnp.transpose(a[0]) for a in (w_in, m_w_in, v_w_in)]
    sp = lambda nin, cb, fn, al, dt, gn, cq, cwv: _small_pack(nin, cb, fn, al, dt, gn, cq[0], cwv[0])
    w_s = sp(norm_in_w, conv_b, final_norm_w, A_log, dt_bias, gdn_norm_w, conv_qkv_w, conv_w)
    m_s = sp(m_norm_in_w, m_conv_b, m_final_norm_w, m_A_log, m_dt_bias, m_gdn_norm_w, m_conv_qkv_w, m_conv_w)
    v_s = sp(v_norm_in_w, v_conv_b, v_final_norm_w, v_A_log, v_dt_bias, v_gdn_norm_w, v_conv_qkv_w, v_conv_w)
    a_thru, wg, _, cq_g = _gather_wait("g", ss_g, rs_g, [bufs_c[0], wg1, bufs_g[2], cqg1],
                                       [h, w_s, m_s, v_s] + adam_in[1:])
    w_g = _merge_edges(_forward("g", wg, True), G_EDGE, G_MIXED, "merge_edges_g")
    cqw = _gathered_to_full(cq_g)
    ad = jnp.pad(jnp.concatenate([A_log, dt_bias], axis=0), ((0, 0), (A_LANE, 0)))
    fwd_c = {}

    def on_q(q):
        _, wc, _, cw_g, _, wo_g = _gather_wait("c", ss_c, rs_c,
                                               [a_thru, wc1, bufs_c[2], cwg1, bufs_c[4], wog1], q)
        issue, _ = _forward_first_parts("c")
        ss, rs, (wc,), tok = _split_start("forward_start_c", issue, [wc], FORWARD_SEMS)
        fwd_c.update(ss=ss, rs=rs, wc=wc, cw_g=cw_g, wo_g=wo_g)
        return _tie(q, tok, "after_forward_start_c")

    def late(o):
        _, await_ = _forward_first_parts("c")
        (wc,) = _split_wait("forward_wait_c", await_, fwd_c["ss"], fwd_c["rs"], [fwd_c["wc"]], o)
        wc = _forward("c", wc, False)
        return (_merge_edges(wc, C_EDGE, C_MIXED, "merge_edges_c"), fwd_c["wo_g"].reshape(2 * GW, d_model),
                _gathered_to_full(fwd_c["cw_g"]))

    scat = {}

    def on_grad_c(g_c, g_wout, do):
        go4 = g_wout.reshape(4, GW // 2, d_model)
        land = lax.empty((3, ALIGNED_W, d_model), BF16)
        land_o = lax.empty((3, GW // 2, d_model), BF16)
        ss, rs, bufs, tok = _scatter_start("c", g_c, land, [(go4, land_o)])
        scat["c"] = (ss, rs, bufs)
        return _tie(do, tok, "after_scatter_start_c")

    def on_grad_g(g_g, dproj_g):
        ss, rs, bufs, tok = _scatter_start("g", _pair_reduce("g", g_g), scat["c"][2][1], [], halved=True)
        scat["g"] = (ss, rs, bufs)
        return _tie(dproj_g, tok, "after_scatter_start_g")

    gx, sm, _ = _local_step(x0, loss_target[0], h, w_g, cqw, late, norm_in_w, ad, gdn_norm_w, conv_b,
                            final_norm_w.reshape(1, -1), on_grad_c, on_grad_g, on_q)

    ss, rs, bufs = scat["c"]
    g_c, land, go4, land_o = _scatter_wait("c", ss, rs, [bufs[0], scat["g"][2][1], bufs[2], bufs[3]], gx)
    part_out = _sum_rows(go4, land_o, 128)
    ad_g = jnp.concatenate([sm["al"][:, A_LANE:], sm["dt"][:, A_LANE:]], axis=1)
    pack = jnp.concatenate([_row(sm["nin"]), _row(sm["cb"]), _row(sm["fn"]), _row(ad_g), _row(sm["gn"]),
                            jnp.concatenate(sm["cq"], axis=1).reshape(12, 1024), sm["cw"], _row(sm["loss"])], axis=0)
    pack = jnp.pad(pack, ((0, PACK_ROWS - pack.shape[0]), (0, 0)))
    issue, await_a, nsem = _exchange_parts(1, True)
    ss_a, rs_a, bufs_a, tok_a = _split_start(
        "exchange_start_small", issue,
        [part_out, lax.empty(part_out.shape, F32), pack, lax.empty((8,) + pack.shape, F32)], nsem)
    ss, rs, bufs = scat["g"]
    g_g, land = _scatter_wait("g", ss, rs, [bufs[0], land], [gx, tok_a], halved=True)
    part_in = _sum_shard(g_g, g_c, land)
    issue, await_b, nsem = _exchange_parts(1, False)
    ss_b, rs_b, bufs_b, tok_b = _split_start("exchange_start_w_in", issue,
                                             [part_in, lax.empty(part_in.shape, F32)], nsem)
    part_out, sib_out, pack, packs = _split_wait("exchange_wait_small", await_a, ss_a, rs_a, bufs_a, tok_b)
    tot = _sum_packs(pack, packs)
    g_wo, d_wo, m_wo, v_wo = _adamw(w_out[0], m_w_out[0], v_w_out[0], part_out, sib_out, 128, "adamw_w_out")
    g_cq_sh = lax.dynamic_slice_in_dim(tot[R_CQ:R_CQ + 12].reshape(4, 3 * GW), chip * 768, 768, axis=1)
    g_cw_sh = lax.dynamic_slice_in_dim(tot[R_CW:R_CW + 3], chip * 256, 256, axis=1)
    g_s = _small_pack(tot[R_NIN], tot[R_CB], tot[R_FN], tot[R_AD, :HEADS], tot[R_AD, HEADS:2 * HEADS],
                      tot[R_GN, :DH], g_cq_sh, g_cw_sh)
    small = _adamw(w_s, m_s, v_s, g_s, None, 16, "adamw_small")
    part_in, sib_in = _split_wait("exchange_wait_w_in", await_b, ss_b, rs_b, bufs_b, [small[0], d_wo])
    g_wi, d_wi, m_wi, v_wi = [jnp.transpose(a, (1, 2, 0))[0] for a in _adamw_shard(*adam_in, part_in, sib_in)]

    def unpack(a, big_in, big_out):
        return (a[0:1], big_in[None], a[5:8].reshape(1, 4, 768), a[3:4, :HEADS], a[3:4, HEADS:2 * HEADS],
                a[4:5, :DH], a[8, :768].reshape(1, 3, 256), a[1:2], big_out[None], a[2])

    loss = tot[R_LOSS, 0]
    return (loss, gx[None], *unpack(small[0], g_wi, g_wo), *unpack(small[1], d_wi, d_wo),
            *unpack(small[2], m_wi, m_wo), *unpack(small[3], v_wi, v_wo))
```

```python
import jax
import jax.numpy as jnp
from jax import lax
from jax.experimental import pallas as pl
from jax.experimental.pallas import tpu as pltpu

F32 = jnp.float32
BF16 = jnp.bfloat16
MESH = pl.DeviceIdType.MESH
ANY = pl.BlockSpec(memory_space=pl.ANY)

HEADS = 8
DH = 128
CH = 64
GW = HEADS * DH
EPS = 1e-6
VMEM_V7X = 64 * 1024 * 1024

QB, KB, VB, ZB, BAB = 0, 8, 16, 24, 32
A_LANE = 120
NG, NC = 33, 32
GW_COLS, CW_COLS = NG * DH, NC * DH

SHARD_W = 2052
ALIGNED_BLOCKS = 17
ALIGNED_W = ALIGNED_BLOCKS * DH
SHIFTS = (0, 4, ALIGNED_W - 8, ALIGNED_W - 4)
G_EDGE, C_EDGE = 34, 32
G_SPARE, C_SPARE = 33, 34
WG_BLOCKS, WC_BLOCKS = 38, 36
G_MIXED, C_MIXED = (2, BAB), (4 * 7 + 1,)


def _shard_blocks(chip, edges):
    g, c = "g", "c"
    if chip == 0:
        out = [(g, 3 * b) for b in range(8)] + [(g, 3 * b + 1) for b in range(8)] + [(g, G_EDGE, G_MIXED[0])]
    elif chip == 1:
        out = [(g, G_EDGE + 1, G_MIXED[0])] + [(g, 3 * b + 2) for b in range(1, 8)]
        out += [(g, ZB + b) for b in range(8)] + [(g, G_EDGE + 2, G_MIXED[1])]
    elif chip == 2:
        out = [(c, 4 * b) for b in range(8)] + [(c, 4 * b + 1) for b in range(7)]
        out += [(c, C_EDGE, C_MIXED[0]), (g, G_EDGE + 3, G_MIXED[1])]
    else:
        out = [(c, 4 * b + 2) for b in range(8)] + [(c, 4 * b + 3) for b in range(8)] + [(c, C_EDGE + 1, C_MIXED[0])]
    return [(o[0], o[1] if (edges or len(o) == 2) else o[2]) for o in out]


def _by_chip(chip, vals):
    if all(v == vals[0] for v in vals):
        return vals[0]
    r = vals[3]
    for kk in (2, 1, 0):
        r = jnp.where(chip == kk, vals[kk], r)
    return r

ADAM_LR, ADAM_B1, ADAM_B2, ADAM_EPS, ADAM_WD, ADAM_STEP = 0.001, 0.9, 0.999, 1e-08, 0.01, 10

R_NIN, R_CB, R_FN, R_AD, R_GN, R_CQ, R_CW, R_LOSS, PACK_ROWS = 0, 1, 2, 3, 4, 5, 17, 20, 24

NN = ((1,), (0,))
NT = ((1,), (1,))
TN = ((0,), (0,))


def _dot(a, b, dims=NN, mode="lo"):
    dn = (dims, ((), ()))
    if mode == "hi":
        return lax.dot_general(a, b, dn, precision=lax.Precision.HIGHEST, preferred_element_type=F32)
    ah, bh = a.astype(BF16), b.astype(BF16)
    out = lax.dot_general(ah, bh, dn, preferred_element_type=F32)
    if mode == "x3":
        al = (a - ah.astype(F32)).astype(BF16)
        bl = (b - bh.astype(F32)).astype(BF16)
        out = out + lax.dot_general(ah, bl, dn, preferred_element_type=F32)
        out = out + lax.dot_general(al, bh, dn, preferred_element_type=F32)
    return out


P_GRAM, P_INV, P_SOL, P_SCAN, P_SCANB, P_BWD = "lo", "lo", "lo", "lo", "lo", "lo"
P_CUM = "x3"


def _params(sem=None, vmem=None):
    kw = {}
    if sem is not None:
        kw["dimension_semantics"] = sem
    if vmem is not None:
        kw["vmem_limit_bytes"] = int(min(max(vmem, 32 * 2**20), VMEM_V7X - 8 * 2**20))
    return pltpu.CompilerParams(**kw)


def _in_hbm(*arrays):
    return [pltpu.with_memory_space_constraint(a, pltpu.HBM) for a in arrays]


def _sigmoid(x):
    return 1.0 / (1.0 + jnp.exp(-x))


def _dsilu(x, s):
    return s * (1.0 + x * (1.0 - s))


def _rows(shape):
    return lax.broadcasted_iota(jnp.int32, shape, 0)


def _shift_down(x, s):
    if s == 0:
        return x
    return jnp.where(_rows(x.shape) >= s, pltpu.roll(x, s, 0), 0.0)


def _shift_up(x, s):
    if s == 0:
        return x
    n = x.shape[0]
    return jnp.where(_rows(x.shape) < n - s, pltpu.roll(x, n - s, 0), 0.0)


def _matmul(a, b, dims, out_dtype, tm, tn, tk, name, add=None, n=None, b_outer=False):
    if dims == NN:
        (m, k), n = a.shape, b.shape[1]
    elif dims == NT:
        (m, k), n = a.shape, (n or b.shape[0])
    else:
        (k, m), n = a.shape, b.shape[1]
    tm, tn, tk = min(tm, m), min(tn, n), min(tk, k)
    assert m % tm == 0 and n % tn == 0 and k % tk == 0, (name, m, n, k, tm, tn, tk)
    nk = k // tk

    def body(*refs):
        if add is None:
            a_ref, b_ref, o_ref = refs[:3]
            add_ref = None
        else:
            a_ref, b_ref, add_ref, o_ref = refs[:4]
        part = _dot(a_ref[...], b_ref[...], dims)
        if nk == 1:
            if add_ref is not None:
                part = part + add_ref[...]
            o_ref[...] = part.astype(out_dtype)
            return
        acc = refs[-1]
        kk = pl.program_id(2)

        @pl.when(kk == 0)
        def _():
            acc[...] = part

        @pl.when(kk > 0)
        def _():
            acc[...] += part

        @pl.when(kk == nk - 1)
        def _():
            r = acc[...]
            if add_ref is not None:
                r = r + add_ref[...]
            o_ref[...] = r.astype(out_dtype)

    ij = (lambda g0, g1: (g1, g0)) if b_outer else (lambda g0, g1: (g0, g1))

    def spec(shape, pick):
        return pl.BlockSpec(shape, lambda g0, g1, kk: pick(*ij(g0, g1), kk))

    a_spec = spec((tk, tm), lambda i, j, kk: (kk, i)) if dims == TN else spec((tm, tk), lambda i, j, kk: (i, kk))
    b_spec = spec((tn, tk), lambda i, j, kk: (j, kk)) if dims == NT else spec((tk, tn), lambda i, j, kk: (kk, j))
    o_spec = spec((tm, tn), lambda i, j, kk: (i, j))
    in_specs = [a_spec, b_spec]
    args = [a, b]
    if add is not None:
        in_specs.append(o_spec)
        args.append(add)
    osz = jnp.dtype(out_dtype).itemsize
    est = 2 * (tm * tk * a.dtype.itemsize + tk * tn * b.dtype.itemsize + tm * tn * osz)
    est += 3 * tm * tn * 4 + (2 * tm * tn * 4 if add is not None else 0)
    return pl.pallas_call(
        body, name=name, grid=(n // tn, m // tm, nk) if b_outer else (m // tm, n // tn, nk),
        in_specs=in_specs, out_specs=o_spec,
        out_shape=jax.ShapeDtypeStruct((m, n), out_dtype),
        scratch_shapes=[pltpu.VMEM((tm, tn), F32)] if nk > 1 else [],
        compiler_params=_params(("parallel", "parallel", "arbitrary"), est + 8 * 2**20),
    )(*args)


def _cast_bf16(a, rows, name, after):
    r, c = a.shape
    rows = min(rows, r)

    def body(a_ref, t_ref, o_ref):
        del t_ref
        o_ref[...] = a_ref[...].astype(BF16)

    return pl.pallas_call(
        body, name=name, grid=(r // rows,),
        in_specs=[pl.BlockSpec((rows, c), lambda i: (i, 0)), ANY],
        out_specs=pl.BlockSpec((rows, c), lambda i: (i, 0)),
        out_shape=jax.ShapeDtypeStruct((r, c), BF16),
        compiler_params=_params(("parallel",)),
    )(a, after)


def _align_shard(wt):
    r, _, d = wt.shape
    nj = d // DH

    def body(w_ref, o_ref, pad_ref):
        chip = 2 * lax.axis_index("x") + lax.axis_index("y")
        shift = _by_chip(chip, SHIFTS)
        pad_ref[...] = jnp.zeros_like(pad_ref)
        for j in range(nj):
            pad_ref[0:r, :] = w_ref[pl.ds(j, r, stride=nj), :]
            o_ref[:, j * DH:(j + 1) * DH] = pltpu.roll(pad_ref[...], shift, 0).astype(BF16)

    return pl.pallas_call(
        body, name="align_shard",
        out_shape=jax.ShapeDtypeStruct((ALIGNED_W, d), BF16),
        scratch_shapes=[pltpu.VMEM((ALIGNED_W, DH), F32)],
        compiler_params=_params(vmem=48 * 2**20),
    )(wt.reshape(r * nj, DH))


def _rms_in(x, w):
    n, d = x.shape
    tr = min(256, n)

    def body(x_ref, w_ref, h_ref):
        xv = x_ref[...]
        r = lax.rsqrt(jnp.mean(xv * xv, axis=-1, keepdims=True) + EPS)
        h_ref[...] = (xv * r * w_ref[...]).astype(BF16)

    return pl.pallas_call(
        body, name="rms_in", grid=(n // tr,),
        in_specs=[pl.BlockSpec((tr, d), lambda i: (i, 0)), pl.BlockSpec((1, d), lambda i: (0, 0))],
        out_specs=pl.BlockSpec((tr, d), lambda i: (i, 0)),
        out_shape=jax.ShapeDtypeStruct((n, d), BF16),
        compiler_params=_params(("parallel",)),
    )(x, w)


def _conv_silu(p, w_ref, taps):
    c = None
    for j in range(taps):
        t = _shift_down(p, taps - 1 - j) * w_ref[j:j + 1, :]
        c = t if c is None else c + t
    return c


def _prep_qkv(proj, cw):
    n = proj.shape[0]

    def body(p3, wq, wk, wv, q_ref, k_ref, v_ref):
        for kind, (w_ref, o_ref) in enumerate(((wq, q_ref), (wk, k_ref), (wv, v_ref))):
            c = _conv_silu(p3[:, kind * DH:(kind + 1) * DH], w_ref, 4)
            a = c * _sigmoid(c)
            if kind < 2:
                r = lax.rsqrt(jnp.sum(a * a, axis=-1, keepdims=True) + EPS)
                a = a * (r * (DH ** -0.5 if kind == 0 else 1.0))
            o_ref[...] = a

    col = pl.BlockSpec((n, DH), lambda h: (0, h))
    wcol = lambda base: pl.BlockSpec((4, DH), lambda h: (0, base + h))
    out = jax.ShapeDtypeStruct((n, GW), F32)
    return pl.pallas_call(
        body, name="prep_qkv", grid=(HEADS,),
        in_specs=[pl.BlockSpec((n, 3 * DH), lambda h: (0, h)), wcol(QB), wcol(KB), wcol(VB)],
        out_specs=[col] * 3, out_shape=[out] * 3,
        compiler_params=_params(("parallel",), 40 * 2**20),
    )(proj, cw, cw, cw)


def _prep_qkv_bwd(proj, cw, dq, dk, dv, dproj):
    n = proj.shape[0]

    def body(p3, wq, wk, wv, dq_ref, dk_ref, dv_ref, _, o3, gq, gk, gv):
        for kind, (w_ref, d_ref, g_ref) in enumerate(((wq, dq_ref, gq), (wk, dk_ref, gk), (wv, dv_ref, gv))):
            p = p3[:, kind * DH:(kind + 1) * DH]
            shifted = [_shift_down(p, 3 - j) for j in range(4)]
            c = shifted[0] * w_ref[0:1, :]
            for j in range(1, 4):
                c = c + shifted[j] * w_ref[j:j + 1, :]
            s = _sigmoid(c)
            a = c * s
            d = d_ref[...]
            if kind < 2:
                r = lax.rsqrt(jnp.sum(a * a, axis=-1, keepdims=True) + EPS)
                sc = DH ** -0.5 if kind == 0 else 1.0
                d = (sc * r) * (d - a * ((r * r) * jnp.sum(d * a, axis=-1, keepdims=True)))
            dc = d * _dsilu(c, s)
            dp = None
            for j in range(4):
                g_ref[j:j + 1, :] = jnp.sum(dc * shifted[j], axis=0, keepdims=True)
                t = _shift_up(dc, 3 - j) * w_ref[j:j + 1, :]
                dp = t if dp is None else dp + t
            o3[:, kind * DH:(kind + 1) * DH] = dp.astype(BF16)

    col = pl.BlockSpec((n, DH), lambda h: (0, h))
    wcol = lambda base: pl.BlockSpec((4, DH), lambda h: (0, base + h))
    p3spec = pl.BlockSpec((n, 3 * DH), lambda h: (0, h))
    return pl.pallas_call(
        body, name="prep_qkv_bwd", grid=(HEADS,),
        in_specs=[p3spec, wcol(QB), wcol(KB), wcol(VB), col, col, col, ANY],
        out_specs=[p3spec] + [wcol(0)] * 3,
        out_shape=[jax.ShapeDtypeStruct(dproj.shape, BF16)] + [jax.ShapeDtypeStruct((4, GW), F32)] * 3,
        input_output_aliases={7: 0},
        compiler_params=_params(("parallel",), 48 * 2**20),
    )(proj, cw, cw, cw, dq, dk, dv, dproj)


CPB = 8
SCAN_CPS = 4


def _tri(lower, rows):
    i = lax.broadcasted_iota(jnp.int32, (rows, rows), 0)
    j = lax.broadcasted_iota(jnp.int32, (rows, rows), 1)
    return jnp.where((i // CH == j // CH) & ((i >= j) if lower else (j >= i)), 1.0, 0.0)


def _lane(shape):
    return lax.broadcasted_iota(jnp.int32, shape, 1)


def _prep_bg(proj, ad):
    n = proj.shape[0]
    nch = n // CH
    cpb = CPB if nch % CPB == 0 else 1
    rows = cpb * CH

    def body(p_ref, ad_ref, bg_ref, bgt_ref):
        p = p_ref[...]
        lane = _lane(p.shape)
        beta = _sigmoid(p)
        xa = p + ad_ref[1:2, :]
        sp = jnp.maximum(xa, 0.0) + jnp.log(1.0 + jnp.exp(-jnp.abs(xa)))
        g = pltpu.roll(-jnp.exp(ad_ref[0:1, :]) * sp, DH - A_LANE + HEADS, 1)
        gc = _dot(_tri(True, rows), g, NN, P_CUM)
        bg = jnp.where(lane < HEADS, beta, jnp.where(lane < 2 * HEADS, gc, 0.0))
        bg_ref[...] = bg
        for ci in range(cpb):
            bgt_ref[ci] = bg[ci * CH:(ci + 1) * CH, :].T

    return pl.pallas_call(
        body, name="prep_bg", grid=(nch // cpb,),
        in_specs=[pl.BlockSpec((rows, DH), lambda i: (i, BAB)), pl.BlockSpec((2, DH), lambda i: (0, 0))],
        out_specs=[pl.BlockSpec((rows, DH), lambda i: (i, 0)), pl.BlockSpec((cpb, DH, CH), lambda i: (i, 0, 0))],
        out_shape=[jax.ShapeDtypeStruct((n, DH), F32), jax.ShapeDtypeStruct((nch, DH, CH), F32)],
        compiler_params=_params(("parallel",)),
    )(*_in_hbm(proj, ad))


def _prep_bg_bwd(proj, ad, dbg, dproj):
    n = proj.shape[0]
    nch = n // CH
    cpb = CPB if nch % CPB == 0 else 1
    rows = cpb * CH

    def body(p_ref, ad_ref, d_ref, _, o_ref, ga_ref, gd_ref):
        p = p_ref[...]
        d = d_ref[...]
        lane = _lane(p.shape)
        beta = _sigmoid(p)
        xa = p + ad_ref[1:2, :]
        sp = jnp.maximum(xa, 0.0) + jnp.log(1.0 + jnp.exp(-jnp.abs(xa)))
        na = -jnp.exp(ad_ref[0:1, :])
        dg = pltpu.roll(_dot(_tri(False, rows), d, NN, P_CUM), A_LANE - HEADS, 1)
        da = dg * na * _sigmoid(xa)
        is_g = lane >= A_LANE
        o_ref[...] = jnp.where(lane < HEADS, d * beta * (1.0 - beta), jnp.where(is_g, da, 0.0)).astype(BF16)
        ga = jnp.sum(jnp.where(is_g, dg * na * sp, 0.0), axis=0, keepdims=True)
        gd = jnp.sum(jnp.where(is_g, da, 0.0), axis=0, keepdims=True)

        @pl.when(pl.program_id(0) == 0)
        def _():
            ga_ref[...] = jnp.zeros_like(ga_ref)
            gd_ref[...] = jnp.zeros_like(gd_ref)

        ga_ref[...] += ga
        gd_ref[...] += gd

    one = pl.BlockSpec((1, DH), lambda i: (0, 0))
    return pl.pallas_call(
        body, name="prep_bg_bwd", grid=(nch // cpb,),
        in_specs=[pl.BlockSpec((rows, DH), lambda i: (i, BAB)), pl.BlockSpec((2, DH), lambda i: (0, 0)),
                  pl.BlockSpec((rows, DH), lambda i: (i, 0)), ANY],
        out_specs=[pl.BlockSpec((rows, DH), lambda i: (i, BAB)), one, one],
        out_shape=[jax.ShapeDtypeStruct(dproj.shape, BF16), jax.ShapeDtypeStruct((1, DH), F32),
                   jax.ShapeDtypeStruct((1, DH), F32)],
        input_output_aliases={3: 0},
        compiler_params=_params(("arbitrary",)),
    )(proj, ad, dbg, dproj)


def _gdn_out(o, proj, wg):
    n = o.shape[0]

    def body(o_ref, z_ref, w_ref, y_ref):
        ov, z = o_ref[...], z_ref[...]
        r = lax.rsqrt(jnp.mean(ov * ov, axis=-1, keepdims=True) + EPS)
        y_ref[...] = (ov * r * w_ref[...] * (z * _sigmoid(z))).astype(BF16)

    return pl.pallas_call(
        body, name="gdn_out", grid=(HEADS,),
        in_specs=[pl.BlockSpec((n, DH), lambda h: (0, h)), pl.BlockSpec((n, DH), lambda h: (0, ZB + h)),
                  pl.BlockSpec((1, DH), lambda h: (0, 0))],
        out_specs=pl.BlockSpec((n, DH), lambda h: (0, h)),
        out_shape=jax.ShapeDtypeStruct((n, 2 * GW), BF16),
        compiler_params=_params(("parallel",)),
    )(o, proj, wg)


def _gdn_out_bwd(o, proj, wg, dout_b, w_out):
    n = o.shape[0]
    d_model = dout_b.shape[1]

    def body(o_ref, z_ref, w_ref, g_ref, wo_ref, do_ref, dz_ref, gw_ref):
        ov, z, w = o_ref[...], z_ref[...], w_ref[...]
        d = _dot(g_ref[...], wo_ref[...], NT)
        r = lax.rsqrt(jnp.mean(ov * ov, axis=-1, keepdims=True) + EPS)
        nrm = ov * r
        s = _sigmoid(z)
        dz_ref[...] = (d * (nrm * w) * _dsilu(z, s)).astype(BF16)
        dn_w = d * (z * s)
        gw = jnp.sum(dn_w * nrm, axis=0, keepdims=True)
        dn = dn_w * w
        do_ref[...] = (r * (dn - nrm * jnp.mean(dn * nrm, axis=-1, keepdims=True))).astype(BF16)

        @pl.when(pl.program_id(0) == 0)
        def _():
            gw_ref[...] = jnp.zeros_like(gw_ref)

        gw_ref[...] += gw

    return pl.pallas_call(
        body, name="gdn_out_bwd", grid=(HEADS,),
        in_specs=[pl.BlockSpec((n, DH), lambda h: (0, h)), pl.BlockSpec((n, DH), lambda h: (0, ZB + h)),
                  pl.BlockSpec((1, DH), lambda h: (0, 0)), pl.BlockSpec((n, d_model), lambda h: (0, 0)),
                  pl.BlockSpec((DH, d_model), lambda h: (h, 0))],
        out_specs=[pl.BlockSpec((n, DH), lambda h: (0, h)), pl.BlockSpec((n, DH), lambda h: (0, ZB + h)),
                   pl.BlockSpec((1, DH), lambda h: (0, 0))],
        out_shape=[jax.ShapeDtypeStruct((n, GW), BF16), jax.ShapeDtypeStruct((n, GW_COLS), BF16),
                   jax.ShapeDtypeStruct((1, DH), F32)],
        compiler_params=_params(("arbitrary",), 40 * 2**20),
    )(o, proj, wg, dout_b, w_out)


def _conv_branch(proj, w3, b, mix):
    n = proj.shape[0]

    def body(p4, w_ref, b_ref, _, y_ref):
        u = p4[:, DH:2 * DH] * p4[:, 2 * DH:3 * DH]
        cc = _conv_silu(u, w_ref, 3) + b_ref[...]
        z = p4[:, 3 * DH:4 * DH]
        y_ref[...] = (p4[:, 0:DH] * cc * (z * _sigmoid(z))).astype(BF16)

    return pl.pallas_call(
        body, name="conv_branch", grid=(HEADS,),
        in_specs=[pl.BlockSpec((n, 4 * DH), lambda h: (0, h)), pl.BlockSpec((3, DH), lambda h: (0, h)),
                  pl.BlockSpec((1, DH), lambda h: (0, h)), ANY],
        out_specs=pl.BlockSpec((n, DH), lambda h: (0, HEADS + h)),
        out_shape=jax.ShapeDtypeStruct(mix.shape, BF16),
        input_output_aliases={3: 0},
        compiler_params=_params(("parallel",), 40 * 2**20),
    )(*_in_hbm(proj, w3, b, mix))


def _conv_branch_bwd(proj, w3, b, dout_b, w_out):
    n = proj.shape[0]
    d_model = dout_b.shape[1]

    def body(p4, w_ref, b_ref, g_ref, wo_ref, o4, gw_ref, gbias_ref):
        gb, gcv, hc, z = p4[:, 0:DH], p4[:, DH:2 * DH], p4[:, 2 * DH:3 * DH], p4[:, 3 * DH:4 * DH]
        d = _dot(g_ref[...], wo_ref[...], NT)
        dgb, dgc, dhc, dzc = (o4.at[:, kk * DH:(kk + 1) * DH] for kk in range(4))
        u = gcv * hc
        cc = _conv_silu(u, w_ref, 3) + b_ref[...]
        s = _sigmoid(z)
        dzc[...] = (d * (gb * cc) * _dsilu(z, s)).astype(BF16)
        dp = d * (z * s)
        dgb[...] = (dp * cc).astype(BF16)
        dcc = dp * gb
        gbias_ref[...] = jnp.sum(dcc, axis=0, keepdims=True)
        du = None
        for j in range(3):
            gw_ref[j:j + 1, :] = jnp.sum(dcc * _shift_down(u, 2 - j), axis=0, keepdims=True)
            t = _shift_up(dcc, 2 - j) * w_ref[j:j + 1, :]
            du = t if du is None else du + t
        dgc[...] = (du * hc).astype(BF16)
        dhc[...] = (du * gcv).astype(BF16)

    p4spec = pl.BlockSpec((n, 4 * DH), lambda h: (0, h))
    return pl.pallas_call(
        body, name="conv_branch_bwd", grid=(HEADS,),
        in_specs=[p4spec, pl.BlockSpec((3, DH), lambda h: (0, h)), pl.BlockSpec((1, DH), lambda h: (0, h)),
                  pl.BlockSpec((n, d_model), lambda h: (0, 0)), pl.BlockSpec((DH, d_model), lambda h: (HEADS + h, 0))],
        out_specs=[p4spec, pl.BlockSpec((3, DH), lambda h: (0, h)), pl.BlockSpec((1, DH), lambda h: (0, h))],
        out_shape=[jax.ShapeDtypeStruct((n, CW_COLS), BF16), jax.ShapeDtypeStruct((3, GW), F32),
                   jax.ShapeDtypeStruct((1, GW), F32)],
        compiler_params=_params(("parallel",), 52 * 2**20),
    )(proj, w3, b, dout_b, w_out)


def _out_loss(mix, w_out, x, tgt, wf):
    n, d = x.shape
    kdim = mix.shape[1]
    tr = min(256, n)

    def body(m_ref, wo_ref, x_ref, t_ref, w_ref, do_ref, dob_ref, gw_ref, loss_ref):
        ov = _dot(m_ref[...], wo_ref[...], NN) + x_ref[...]
        w = w_ref[...]
        r = lax.rsqrt(jnp.mean(ov * ov, axis=-1, keepdims=True) + EPS)
        nrm = ov * r
        e = nrm * w - t_ref[...]
        dy = e * (1.0 / d)
        dn = dy * w
        dout = r * (dn - nrm * jnp.mean(dn * nrm, axis=-1, keepdims=True))
        do_ref[...] = dout
        dob_ref[...] = dout.astype(BF16)

        @pl.when(pl.program_id(0) == 0)
        def _():
            gw_ref[...] = jnp.zeros_like(gw_ref)
            loss_ref[...] = jnp.zeros_like(loss_ref)

        gw_ref[...] += jnp.sum(dy * nrm, axis=0, keepdims=True)
        loss_ref[...] += (0.5 / d) * jnp.sum(jnp.sum(e * e, axis=-1, keepdims=True), axis=0, keepdims=True)

    row = pl.BlockSpec((tr, d), lambda i: (i, 0))
    return pl.pallas_call(
        body, name="out_loss", grid=(n // tr,),
        in_specs=[pl.BlockSpec((tr, kdim), lambda i: (i, 0)), pl.BlockSpec((kdim, d), lambda i: (0, 0)), row, row,
                  pl.BlockSpec((1, d), lambda i: (0, 0))],
        out_specs=[row, row, pl.BlockSpec((1, d), lambda i: (0, 0)), pl.BlockSpec((1, 1), lambda i: (0, 0))],
        out_shape=[jax.ShapeDtypeStruct((n, d), F32), jax.ShapeDtypeStruct((n, d), BF16),
                   jax.ShapeDtypeStruct((1, d), F32), jax.ShapeDtypeStruct((1, 1), F32)],
        compiler_params=_params(("arbitrary",), 40 * 2**20),
    )(mix, w_out, x, tgt, wf)


def _dh_rms_bwd(dproj, w_t, dh0, x, w, dout, tk):
    n, d = x.shape
    kdim = dproj.shape[1]
    tm = min(1024, n)
    tk = min(tk, kdim)
    nk = kdim // tk

    def body(a_ref, b_ref, dh0_ref, x_ref, w_ref, do_ref, dx_ref, gw_ref, acc):
        i, kk = pl.program_id(0), pl.program_id(1)
        part = _dot(a_ref[...], b_ref[...], NN)

        @pl.when(kk == 0)
        def _():
            acc[...] = part + dh0_ref[...]

        @pl.when(kk > 0)
        def _():
            acc[...] += part

        @pl.when((i == 0) & (kk == 0))
        def _():
            gw_ref[...] = jnp.zeros_like(gw_ref)

        @pl.when(kk == nk - 1)
        def _():
            xv, dhv = x_ref[...], acc[...]
            r = lax.rsqrt(jnp.mean(xv * xv, axis=-1, keepdims=True) + EPS)
            xn = xv * r
            dxn = dhv * w_ref[...]
            dx_ref[...] = r * (dxn - xn * jnp.mean(dxn * xn, axis=-1, keepdims=True)) + do_ref[...]
            gw_ref[...] += jnp.sum(dhv * xn, axis=0, keepdims=True)

    row = pl.BlockSpec((tm, d), lambda i, kk: (i, 0))
    one = pl.BlockSpec((1, d), lambda i, kk: (0, 0))
    return pl.pallas_call(
        body, name="dh_rms_bwd", grid=(n // tm, nk),
        in_specs=[pl.BlockSpec((tm, tk), lambda i, kk: (i, kk)), pl.BlockSpec((tk, d), lambda i, kk: (kk, 0)),
                  row, row, one, row],
        out_specs=[row, one],
        out_shape=[jax.ShapeDtypeStruct((n, d), F32), jax.ShapeDtypeStruct((1, d), F32)],
        scratch_shapes=[pltpu.VMEM((tm, d), F32)],
        compiler_params=_params(("arbitrary", "arbitrary"), 56 * 2**20),
    )(dproj, w_t, dh0, x, w, dout)


def _ij():
    i = lax.broadcasted_iota(jnp.int32, (CH, CH), 0)
    j = lax.broadcasted_iota(jnp.int32, (CH, CH), 1)
    return i, j


def _unit_lower_inverse(mats):
    i, j = _ij()
    eye = jnp.where(i == j, 1.0, 0.0)
    same16 = (i // 16) == (j // 16)
    same32 = (i // 32) == (j // 32)
    mm = lambda xs, ys: [_dot(x, y, NN, P_INV) for x, y in zip(xs, ys)]
    n1 = [jnp.where(same16, -a, 0.0) for a in mats]
    n2 = mm(n1, n1)
    n4 = mm(n2, n2)
    n8 = mm(n4, n4)
    t = [eye + x1 + x2 + x3 for x1, x2, x3 in zip(n1, n2, mm(n1, n2))]
    t = [x + y for x, y in zip(t, mm(t, n4))]
    t = [x + y for x, y in zip(t, mm(t, n8))]
    a1 = [jnp.where(same32 & jnp.logical_not(same16), a, 0.0) for a in mats]
    t = [x - y for x, y in zip(t, mm(t, mm(a1, t)))]
    a2 = [jnp.where(same32, 0.0, a) for a in mats]
    t = [x - y for x, y in zip(t, mm(t, mm(a2, t)))]
    return t


def _head_vectors(bg, bgt, h):
    bcol = bg[:, h:h + 1]
    gcol = bg[:, HEADS + h:HEADS + h + 1]
    grow = bgt[HEADS + h:HEADS + h + 1, :]
    return bcol, gcol, grow


def _decay(gcol, grow):
    i, j = _ij()
    return jnp.where(i >= j, jnp.exp(jnp.where(i >= j, gcol - grow, 0.0)), 0.0)


def _gdn_intra(q, k, v, bg, bgt):
    n = q.shape[0]
    nch = n // CH
    cps = 4 if nch % 4 == 0 else 1

    def body(q_ref, k_ref, v_ref, bg_ref, bgt_ref, u_ref, w_ref, p_ref, t_ref):
        i, j = _ij()
        items = [(ci, h) for ci in range(cps) for h in range(HEADS)]
        at = lambda ref, ci, h: ref.at[ci * CH:(ci + 1) * CH, h * DH:(h + 1) * DH]
        bgs = [bg_ref[ci * CH:(ci + 1) * CH, :] for ci in range(cps)]
        ks = [at(k_ref, ci, h)[...] for ci, h in items]
        vecs = [_head_vectors(bgs[ci], bgt_ref[ci], h) for ci, h in items]
        decs = [_decay(gcol, grow) for _, gcol, grow in vecs]
        kks = [_dot(kh, kh, NT, P_GRAM) for kh in ks]
        qks = [_dot(at(q_ref, ci, h)[...], kh, NT, P_GRAM) for (ci, h), kh in zip(items, ks)]
        ts = _unit_lower_inverse([jnp.where(i > j, bcol * kk * dec, 0.0)
                                  for (bcol, _, _), kk, dec in zip(vecs, kks, decs)])
        us = [_dot(t, at(v_ref, ci, h)[...] * bcol, NN, P_SOL) for t, (ci, h), (bcol, _, _) in zip(ts, items, vecs)]
        ws = [_dot(t, kh * (bcol * jnp.exp(gcol)), NN, P_SOL) for t, kh, (bcol, gcol, _) in zip(ts, ks, vecs)]
        for n_, (ci, h) in enumerate(items):
            p_ref[ci, h] = qks[n_] * decs[n_]
            t_ref[ci, h] = ts[n_].astype(BF16)
            at(u_ref, ci, h)[...] = us[n_]
            at(w_ref, ci, h)[...] = ws[n_].astype(BF16)

    row = pl.BlockSpec((cps * CH, GW), lambda c: (c, 0))
    sq = pl.BlockSpec((cps, HEADS, CH, CH), lambda c: (c, 0, 0, 0))
    big = jax.ShapeDtypeStruct((n, GW), F32)
    sqs = jax.ShapeDtypeStruct((nch, HEADS, CH, CH), F32)
    return pl.pallas_call(
        body, name="gdn_intra", grid=(nch // cps,),
        in_specs=[row, row, row, pl.BlockSpec((cps * CH, DH), lambda c: (c, 0)),
                  pl.BlockSpec((cps, DH, CH), lambda c: (c, 0, 0))],
        out_specs=[row, row, sq, sq],
        out_shape=[big, jax.ShapeDtypeStruct((n, GW), BF16), sqs, jax.ShapeDtypeStruct(sqs.shape, BF16)],
        compiler_params=_params(("parallel",)),
    )(q, k, v, bg, bgt)


def _gdn_scan(q, k, bg, u, w, p):
    n = q.shape[0]
    nch = n // CH
    cps = SCAN_CPS if nch % SCAN_CPS == 0 else 1

    def body(q_ref, k_ref, bg_ref, u_ref, w_ref, p_ref, o_ref, vn_ref, s_out, s_scr):
        @pl.when(pl.program_id(0) == 0)
        def _():
            s_scr[...] = jnp.zeros_like(s_scr)

        hs = range(HEADS)
        sls = [slice(h * DH, (h + 1) * DH) for h in hs]
        ss = [s_scr[h] for h in hs]
        for ci in range(cps):
            rs = slice(ci * CH, (ci + 1) * CH)
            bg = bg_ref[rs, :]
            gcols = [bg[:, HEADS + h:HEADS + h + 1] for h in hs]
            glasts = [g[CH - 1:CH, :] for g in gcols]
            wss = [_dot(w_ref[rs, sl], s, NN, P_SCAN) for sl, s in zip(sls, ss)]
            oqs = [_dot(q_ref[rs, sl] * jnp.exp(g), s, NN, P_SCAN) for sl, s, g in zip(sls, ss, gcols)]
            vns = [u_ref[rs, sl] - x for sl, x in zip(sls, wss)]
            ops = [_dot(p_ref[ci, h], vn, NN, P_SCAN) for h, vn in zip(hs, vns)]
            sns = [_dot(k_ref[rs, sl] * jnp.exp(gl - g), vn, TN, P_SCAN)
                   for sl, gl, g, vn in zip(sls, glasts, gcols, vns)]
            for h, sl in enumerate(sls):
                s_out[ci, :, sl] = ss[h].astype(BF16)
                vn_ref[rs, sl] = vns[h].astype(BF16)
                o_ref[rs, sl] = oqs[h] + ops[h]
            ss = [s * jnp.exp(gl) + sn for s, gl, sn in zip(ss, glasts, sns)]
        for h in hs:
            s_scr[h] = ss[h]

    row = pl.BlockSpec((cps * CH, GW), lambda c: (c, 0))
    big = jax.ShapeDtypeStruct((n, GW), F32)
    return pl.pallas_call(
        body, name="gdn_scan", grid=(nch // cps,),
        in_specs=[row, row, pl.BlockSpec((cps * CH, DH), lambda c: (c, 0)), row, row,
                  pl.BlockSpec((cps, HEADS, CH, CH), lambda c: (c, 0, 0, 0))],
        out_specs=[row, row, pl.BlockSpec((cps, DH, GW), lambda c: (c, 0, 0))],
        out_shape=[big, jax.ShapeDtypeStruct((n, GW), BF16), jax.ShapeDtypeStruct((nch, DH, GW), BF16)],
        scratch_shapes=[pltpu.VMEM((HEADS, DH, DH), F32)],
        compiler_params=_params(("arbitrary",)),
    )(q, k, bg, u, w, p)


def _gdn_scan_bwd(q, k, bg, w, p, vn, s_in, do):
    n = q.shape[0]
    nch = n // CH
    cps = SCAN_CPS if nch % SCAN_CPS == 0 else 1
    rev = lambda c: nch // cps - 1 - c

    def body(q_ref, k_ref, bg_ref, w_ref, p_ref, vn_ref, s_ref, do_ref,
             dqg_ref, dp_ref, du_ref, dw_ref, dks_ref, dgam_ref, ds_scr):
        @pl.when(pl.program_id(0) == 0)
        def _():
            ds_scr[...] = jnp.zeros_like(ds_scr)

        lane = _lane((1, DH))
        hs = range(HEADS)
        sls = [slice(h * DH, (h + 1) * DH) for h in hs]
        dss = [ds_scr[h] for h in hs]
        for ci in reversed(range(cps)):
            rs = slice(ci * CH, (ci + 1) * CH)
            bg = bg_ref[rs, :]
            gcols = [bg[:, HEADS + h:HEADS + h + 1] for h in hs]
            glasts = [g[CH - 1:CH, :] for g in gcols]
            ss = [s_ref[ci, :, sl] for sl in sls]
            dos = [do_ref[rs, sl] for sl in sls]
            vnl = [vn_ref[rs, sl] for sl in sls]
            dqgs = [_dot(d, s, NT, P_SCANB) for d, s in zip(dos, ss)]
            dps = [_dot(d, vn, NT, P_SCANB) for d, vn in zip(dos, vnl)]
            dvn1 = [_dot(p_ref[ci, h], d, TN, P_SCANB) for h, d in zip(hs, dos)]
            dvn2 = [_dot(k_ref[rs, sl] * jnp.exp(gl - g), ds, NN, P_SCANB)
                    for sl, gl, g, ds in zip(sls, glasts, gcols, dss)]
            dkss = [_dot(vn, ds, NT, P_SCANB) for vn, ds in zip(vnl, dss)]
            dsq = [_dot(q_ref[rs, sl] * jnp.exp(g), d, TN, P_SCANB) for sl, g, d in zip(sls, gcols, dos)]
            dvns = [a + b for a, b in zip(dvn1, dvn2)]
            dws = [_dot(dvn, s, NT, P_SCANB) for dvn, s in zip(dvns, ss)]
            dsw = [_dot(w_ref[rs, sl], dvn, TN, P_SCANB) for sl, dvn in zip(sls, dvns)]
            dgam = jnp.zeros((1, DH), F32)
            for h, sl in enumerate(sls):
                dqg_ref[rs, sl] = dqgs[h]
                dp_ref[ci, h] = dps[h]
                du_ref[rs, sl] = dvns[h].astype(BF16)
                dw_ref[rs, sl] = (-dws[h]).astype(BF16)
                dks_ref[rs, sl] = dkss[h]
                tot = jnp.sum(jnp.sum(dss[h] * ss[h], axis=-1, keepdims=True), axis=0, keepdims=True)
                dgam = dgam + jnp.where(lane == h, tot, 0.0)
            dgam_ref[ci] = jnp.broadcast_to(dgam, (8, DH))
            dss = [ds * jnp.exp(gl) + a - b for ds, gl, a, b in zip(dss, glasts, dsq, dsw)]
        for h in hs:
            ds_scr[h] = dss[h]

    row = pl.BlockSpec((cps * CH, GW), lambda c: (rev(c), 0))
    sq =pl.BlockSpec((cps, HEADS, CH, CH), lambda c: (rev(c), 0, 0, 0))
    big = jax.ShapeDtypeStruct((n, GW), F32)
    return pl.pallas_call(
        body, name="gdn_scan_bwd", grid=(nch // cps,),
        in_specs=[row, row, pl.BlockSpec((cps * CH, DH), lambda c: (rev(c), 0)), row, sq, row,
                  pl.BlockSpec((cps, DH, GW), lambda c: (rev(c), 0, 0)), row],
        out_specs=[row, sq, row, row, row, pl.BlockSpec((cps, 8, DH), lambda c: (rev(c), 0, 0))],
        out_shape=[big, jax.ShapeDtypeStruct((nch, HEADS, CH, CH), F32), jax.ShapeDtypeStruct((n, GW), BF16),
                   jax.ShapeDtypeStruct((n, GW), BF16), big,
                   jax.ShapeDtypeStruct((nch, 8, DH), F32)],
        scratch_shapes=[pltpu.VMEM((HEADS, DH, DH), F32)],
        compiler_params=_params(("arbitrary",)),
    )(q, k, bg, w, p, vn, s_in, do)


def _gdn_intra_bwd(q, k, v, bg, bgt, t, u, w, p, dqg, dp, du, dw, dks, dgam):
    n = q.shape[0]
    nch = n // CH
    cps = 2 if nch % 2 == 0 else 1

    def body(q_ref, k_ref, v_ref, bg_ref, bgt_ref, t_ref, u_ref, w_ref, p_ref,
             dqg_ref, dp_ref, du_ref, dw_ref, dks_ref, dgam_ref, dq_ref, dk_ref, dv_ref, dbg_ref):
        i, j = _ij()
        rows1 = lax.broadcasted_iota(jnp.int32, (CH, 1), 0)
        lane = _lane((CH, DH))
        rsum = lambda x: jnp.sum(x, axis=-1, keepdims=True)
        items = [(ci, h) for ci in range(cps) for h in range(HEADS)]
        at = lambda ref, it: ref.at[it[0] * CH:(it[0] + 1) * CH, it[1] * DH:(it[1] + 1) * DH]
        ld = lambda ref: [at(ref, it)[...] for it in items]
        bgs = [bg_ref[ci * CH:(ci + 1) * CH, :] for ci in range(cps)]
        qs, ks = ld(q_ref), ld(k_ref)
        vecs = [_head_vectors(bgs[ci], bgt_ref[ci], h) for ci, h in items]
        decs = [_decay(gcol, grow) for _, gcol, grow in vecs]
        ths = [t_ref[ci, h] for ci, h in items]
        drus = [_dot(th, x_, TN, P_BWD) for th, x_ in zip(ths, ld(du_ref))]
        drws = [_dot(th, x_, TN, P_BWD) for th, x_ in zip(ths, ld(dw_ref))]
        kks = [_dot(kh, kh, NT, P_GRAM) for kh in ks]
        da1 = [_dot(dru, x_, NT, P_BWD) for dru, x_ in zip(drus, ld(u_ref))]
        da2 = [_dot(drw, x_, NT, P_BWD) for drw, x_ in zip(drws, ld(w_ref))]
        das = [jnp.where(i > j, -(x_ + y_), 0.0) for x_, y_ in zip(da1, da2)]
        dkks = [da * bcol * dec for da, (bcol, _, _), dec in zip(das, vecs, decs)]
        dps = [dp_ref[ci, h] for ci, h in items]
        dqks = [dp_ * dec for dp_, dec in zip(dps, decs)]
        dq_ps = [_dot(dqk, kh, NN, P_BWD) for dqk, kh in zip(dqks, ks)]
        dk_ps = [_dot(dqk, qh, TN, P_BWD) for dqk, qh in zip(dqks, qs)]
        dk_as = [_dot(dkk, kh, NN, P_BWD) for dkk, kh in zip(dkks, ks)]
        dk_bs = [_dot(dkk, kh, TN, P_BWD) for dkk, kh in zip(dkks, ks)]
        bcols = [vc[0] for vc in vecs]
        gcols = [vc[1] for vc in vecs]
        gams = [jnp.exp(g) for g in gcols]
        glasts = [g[CH - 1:CH, :] for g in gcols]
        es = [jnp.exp(gl - g) for gl, g in zip(glasts, gcols)]
        kgs = [kh * gam for kh, gam in zip(ks, gams)]
        dqgs, dkss = ld(dqg_ref), ld(dks_ref)
        wks = [drw * kg for drw, kg in zip(drws, kgs)]
        kss = [dk_ * (kh * e) for dk_, kh, e in zip(dkss, ks, es)]
        r_beta = [rsum(dru * x_ + wk) for dru, x_, wk in zip(drus, ld(v_ref), wks)]
        r_ak = [rsum(da * kk * dec) for da, kk, dec in zip(das, kks, decs)]
        r_gc = [rsum(wk * bcol + dqg * (qh * gam) - ks_)
                for wk, bcol, dqg, qh, gam, ks_ in zip(wks, bcols, dqgs, qs, gams, kss)]
        tk_tot = [jnp.sum(jnp.sum(ks_, axis=0, keepdims=True), axis=-1, keepdims=True) for ks_ in kss]
        mdecs = [da * (bcol * kk * dec) + dp_ * p_ref[ci, h]
                 for (ci, h), da, bcol, kk, dec, dp_ in zip(items, das, bcols, kks, decs, dps)]
        r_md = [rsum(m) for m in mdecs]
        c_md = [rsum(jnp.where(i == j, jnp.sum(m, axis=0, keepdims=True), 0.0)) for m in mdecs]
        dbgs = [jnp.zeros((CH, DH), F32) for _ in range(cps)]
        for n_, (ci, h) in enumerate(items):
            at(dv_ref, (ci, h))[...] = bcols[n_] * drus[n_]
            at(dq_ref, (ci, h))[...] = gams[n_] * dqgs[n_] + dq_ps[n_]
            at(dk_ref, (ci, h))[...] = ((bcols[n_] * gams[n_]) * drws[n_] + dk_ps[n_] + dk_as[n_] + dk_bs[n_]
                                        + dkss[n_] * es[n_])
            dbeta = r_beta[n_] + r_ak[n_]
            dglast = tk_tot[n_] + dgam_ref[ci, 0:1, h:h + 1] * jnp.exp(glasts[n_])
            dgc = r_gc[n_] + r_md[n_] - c_md[n_] + jnp.where(rows1 == CH - 1, dglast, 0.0)
            dbgs[ci] = dbgs[ci] + jnp.where(lane == h, dbeta, 0.0) + jnp.where(lane == HEADS + h, dgc, 0.0)
        for ci in range(cps):
            dbg_ref[ci * CH:(ci + 1) * CH, :] = dbgs[ci]

    row = pl.BlockSpec((cps * CH, GW), lambda c: (c, 0))
    sq = pl.BlockSpec((cps, HEADS, CH, CH), lambda c: (c, 0, 0, 0))
    small = pl.BlockSpec((cps * CH, DH), lambda c: (c, 0))
    big = jax.ShapeDtypeStruct((n, GW), F32)
    return pl.pallas_call(
        body, name="gdn_intra_bwd", grid=(nch // cps,),
        in_specs=[row, row, row, small, pl.BlockSpec((cps, DH, CH), lambda c: (c, 0, 0)), sq, row, row, sq,
                  row, sq, row, row, row, pl.BlockSpec((cps, 8, DH), lambda c: (c, 0, 0))],
        out_specs=[row, row, row, small],
        out_shape=[big, big, big, jax.ShapeDtypeStruct((n, DH), F32)],
        compiler_params=_params(("parallel",)),
    )(q, k, v, bg, bgt, t, u, w, p, dqg, dp, du, dw, dks, dgam)


def _local_step(x, tgt, h, w_g, cqw, late, norm_in_w, ad, gdn_norm_w, conv_b, final_norm_w,
                on_grad_c=None, on_grad_g=None, on_q=None):
    proj_g = _matmul(h, w_g, NT, F32, 512, 1408, 1024, "mm_proj_g", n=GW_COLS, b_outer=True)
    q, k, v = _prep_qkv(proj_g, cqw)
    if on_q is not None:
        q = on_q(q)
    bg, bgt = _prep_bg(proj_g, ad)
    u, w, p, t = _gdn_intra(q, k, v, bg, bgt)
    o, vn, s_in = _gdn_scan(q, k, bg, u, w, p)
    w_c, w_out, conv_w = late(o)
    proj_c = _matmul(h, w_c, NT, F32, 512, 1024, 1024, "mm_proj_c", n=CW_COLS, b_outer=True)
    mix = _conv_branch(proj_c, conv_w, conv_b, _gdn_out(o, proj_g, gdn_norm_w))
    dout, dout_b, g_fn, loss = _out_loss(mix, w_out, x, tgt, final_norm_w)

    g_wout = _matmul(mix, dout_b, TN, BF16, 512, 512, 2048, "mm_gwout")
    do, dproj_g, g_gn = _gdn_out_bwd(o, proj_g, gdn_norm_w, dout_b, w_out)
    dproj_c, g_cw, g_cb = _conv_branch_bwd(proj_c, conv_w, conv_b, dout_b, w_out)
    g_c = _matmul(dproj_c, h, TN, BF16, 1024, 512, 2048, "mm_gwin_c")
    if on_grad_c is not None:
        do = on_grad_c(g_c, g_wout, do)
    dqg, dp, du, dw, dks, dgam = _gdn_scan_bwd(q, k, bg, w, p, vn, s_in, do)
    dq, dk, dv, dbg = _gdn_intra_bwd(q, k, v, bg, bgt, t, u, w, p, dqg, dp, du, dw, dks, dgam)
    dproj_g, gq, gk, gv = _prep_qkv_bwd(proj_g, cqw, dq, dk, dv, dproj_g)
    dproj_g, g_al, g_dt = _prep_bg_bwd(proj_g, ad, dbg, dproj_g)
    g_g = _matmul(dproj_g, h, TN, BF16, 1408, 512, 2048, "mm_gwin_g")
    if on_grad_g is not None:
        dproj_g = on_grad_g(g_g, dproj_g)
    dh = _matmul(dproj_g, w_g, NN, F32, 1024, 1024, 1408, "mm_dh_g")
    gx, g_nin = _dh_rms_bwd(dproj_c, w_c, dh, x, norm_in_w, dout, 1024)
    small = dict(nin=g_nin, cb=g_cb, fn=g_fn, al=g_al, dt=g_dt, gn=g_gn, cq=(gq, gk, gv), cw=g_cw, loss=loss)
    return gx, small, (g_g, g_c, g_wout)


def _place():
    x, y, c = lax.axis_index("x"), lax.axis_index("y"), lax.axis_index("c")
    chips = [(1 - x, y), (x, 1 - y), (1 - x, 1 - y)]
    return x, y, c, chips


def _blk(ref, b):
    if isinstance(b, int):
        return ref.at[b * DH:(b + 1) * DH, :]
    return ref.at[pl.ds(pl.multiple_of(b * DH, DH), DH), :]


HBM = pl.BlockSpec(memory_space=pltpu.HBM)
SEM = pl.BlockSpec(memory_space=pltpu.SEMAPHORE)
EFFECT = pltpu.SideEffectType.DATAFLOW_SIDE_EFFECTING


def _split_start(name, issue, bufs, n_sems):
    nbuf = len(bufs)

    def body(*refs):
        issue(refs[:nbuf], refs[nbuf], refs[nbuf + 1])
        refs[-1][...] = jnp.zeros_like(refs[-1])

    out = pl.pallas_call(
        body, name=name,
        out_shape=(pltpu.SemaphoreType.DMA((n_sems,)), pltpu.SemaphoreType.DMA((n_sems,)),
                   *[pltpu.HBM(b.shape, b.dtype) for b in bufs], jax.ShapeDtypeStruct((8, DH), F32)),
        in_specs=[HBM] * nbuf,
        out_specs=(SEM, SEM, *[HBM] * nbuf, pl.BlockSpec(memory_space=pltpu.VMEM)),
        input_output_aliases={a: 2 + a for a in range(nbuf)},
        compiler_params=pltpu.CompilerParams(has_side_effects=EFFECT),
    )(*[pltpu.with_memory_space_constraint(b, pltpu.HBM) for b in bufs])
    return out[0], out[1], list(out[2:2 + nbuf]), out[-1]


def _split_wait(name, await_, send_sems, recv_sems, bufs, after):
    nbuf = len(bufs)
    after = list(after) if isinstance(after, (list, tuple)) else [after]

    def body(*refs):
        await_(refs[:nbuf], refs[nbuf], refs[nbuf + 1])

    out = pl.pallas_call(
        body, name=name,
        out_shape=tuple(pltpu.HBM(b.shape, b.dtype) for b in bufs),
        in_specs=[HBM] * nbuf + [SEM, SEM] + [ANY] * len(after), out_specs=tuple([HBM] * nbuf),
        input_output_aliases={a: a for a in range(nbuf)},
        compiler_params=pltpu.CompilerParams(has_side_effects=EFFECT),
    )(*bufs, send_sems, recv_sems, *after)
    return list(out)


def _phase_blocks(chip, phase, edges, parity=None):
    return [(b, blk) for b, (grp, blk) in enumerate(_shard_blocks(chip, edges))
            if grp == phase and (parity is None or b % 2 == parity)]


def _cols(ref, nblk):
    return ref.at[0:nblk * DH, :]


def _block_table(chip, edges, spare_g, spare_c):
    rows = []
    for s in range(4):
        sb = _shard_blocks(s, edges)
        rows.append([[blk if grp == "g" else spare_g for grp, blk in sb],
                     [blk if grp == "c" else spare_c for grp, blk in sb],
                     [int(grp == "g") for grp, _ in sb], [s] * ALIGNED_BLOCKS])
    return jnp.asarray(rows, jnp.int32)[chip]


def _place_own(a_shard, wo, cq, cw, bufs):
    d = a_shard.shape[1]
    chip = 2 * lax.axis_index("x") + lax.axis_index("y")

    def body(t_ref, a_ref, wo_ref, cq_ref, cw_ref, *refs):
        wg_ref, wc_ref, wog_ref, cqg_ref, cwg_ref = refs[5:]
        wg_ref[...] = a_ref[...]
        wc_ref[...] = a_ref[...]

        @pl.when(pl.program_id(0) == 0)
        def _():
            wog_ref[0] = wo_ref[...]
            cqg_ref[0] = cq_ref[...]
            cwg_ref[0] = cw_ref[...]

    whole = lambda s: pl.BlockSpec(s.shape, lambda b, t: (0,) * s.ndim)
    slot = lambda s: pl.BlockSpec((1,) + s.shape, lambda b, t: (t[3, 0],) + (0,) * s.ndim)
    return pl.pallas_call(
        body, name="place_own",
        grid_spec=pltpu.PrefetchScalarGridSpec(
            num_scalar_prefetch=1, grid=(ALIGNED_BLOCKS,),
            in_specs=[pl.BlockSpec((DH, d), lambda b, t: (b, 0)), whole(wo), whole(cq), whole(cw)] + [ANY] * 5,
            out_specs=[pl.BlockSpec((DH, d), lambda b, t: (t[0, b], 0)),
                       pl.BlockSpec((DH, d), lambda b, t: (t[1, b], 0)), slot(wo), slot(cq), slot(cw)]),
        out_shape=[jax.ShapeDtypeStruct(b.shape, b.dtype) for b in bufs],
        input_output_aliases={5 + a: a for a in range(5)},
        compiler_params=_params(("arbitrary",)),
    )(_block_table(chip, True, G_SPARE, C_SPARE), a_shard, wo, cq, cw, *bufs)


def _tie(x, token, name):
    def body(x_ref, t_ref, o_ref):
        del x_ref, t_ref, o_ref

    return pl.pallas_call(
        body, name=name, in_specs=[ANY, ANY], out_specs=ANY,
        out_shape=jax.ShapeDtypeStruct(x.shape, x.dtype), input_output_aliases={0: 0},
    )(x, token)


def _gather_start(phase, a_shard, w_grp, singles):
    ns = len(singles)

    def issue(refs, send_sems, recv_sems):
        a_ref, w_ref = refs[0], refs[1]
        x, y, c, chips = _place()
        mine = 2 * x + y
        for jj, (px, py) in enumerate(chips):
            to = dict(device_id=(px, py, c), device_id_type=MESH)
            for a in range(ns):
                pltpu.make_async_remote_copy(
                    src_ref=refs[2 + 2 * a], dst_ref=refs[3 + 2 * a].at[mine],
                    send_sem=send_sems.at[(1 + ns) * jj + 1 + a], recv_sem=recv_sems.at[(1 + ns) * jj + 1 + a],
                    **to).start()
        for s in range(4):
            for par in range(2):
                blocks = _phase_blocks(s, phase, True, par)
                if blocks:
                    @pl.when((mine == s) & (c == par))
                    def _():
                        for b, blk in blocks:
                            for jj, (px, py) in enumerate(chips):
                                pltpu.make_async_remote_copy(
                                    src_ref=_blk(a_ref, b), dst_ref=_blk(w_ref, blk),
                                    send_sem=send_sems.at[(1 + ns) * jj], recv_sem=recv_sems.at[(1 + ns) * jj],
                                    device_id=(px, py, c), device_id_type=MESH).start()

    bufs = [a_shard, w_grp] + [t for pair in singles for t in pair]
    return _split_start("gather_start_" + phase, issue, bufs, 3 * (1 + ns))


def _gather_wait(phase, send_sems, recv_sems, bufs, after):
    ns = (len(bufs) - 2) // 2

    def await_(refs, send_sems, recv_sems):
        a_ref, w_ref = refs[0], refs[1]
        x, y, c, chips = _place()
        mine = 2 * x + y
        for jj, (px, py) in enumerate(chips):
            to = dict(device_id=(px, py, c), device_id_type=MESH)
            peer = 2 * px + py
            for a in range(ns):
                cp = pltpu.make_async_remote_copy(
                    src_ref=refs[2 + 2 * a], dst_ref=refs[3 + 2 * a].at[mine],
                    send_sem=send_sems.at[(1 + ns) * jj + 1 + a], recv_sem=recv_sems.at[(1 + ns) * jj + 1 + a], **to)
                cp.wait_recv()
                cp.wait_send()
            for s in range(4):
                for par in range(2):
                    nblk = len(_phase_blocks(s, phase, True, par))
                    if nblk:
                        both = pltpu.make_async_remote_copy(
                            src_ref=_cols(a_ref, nblk), dst_ref=_cols(w_ref, nblk),
                            send_sem=send_sems.at[(1 + ns) * jj], recv_sem=recv_sems.at[(1 + ns) * jj], **to)

                        @pl.when((peer == s) & (c == par))
                        def _():
                            both.wait_recv()

                        @pl.when((mine == s) & (c == par))
                        def _():
                            both.wait_send()

    return _split_wait("gather_wait_" + phase, await_, send_sems, recv_sems, bufs, after)


def _sibling_forward_parts(phase):
    def each(w_ref, send_sems, recv_sems, start):
        x, y, c, chips = _place()
        to = dict(device_id=(x, y, 1 - c), device_id_type=MESH)
        for jj, (px, py) in enumerate(chips):
            peer = 2 * px + py
            for s in range(4):
                for par in range(2):
                    mine_blocks = _phase_blocks(s, phase, True, par)
                    theirs = len(_phase_blocks(s, phase, True, 1 - par))
                    if not (mine_blocks or theirs):
                        continue

                    @pl.when((peer == s) & (c == par))
                    def _():
                        if start:
                            for _, blk in mine_blocks:
                                pltpu.make_async_remote_copy(
                                    src_ref=_blk(w_ref, blk), dst_ref=_blk(w_ref, blk),
                                    send_sem=send_sems.at[jj], recv_sem=recv_sems.at[jj], **to).start()
                            return
                        if theirs:
                            pltpu.make_async_remote_copy(
                                src_ref=_cols(w_ref, theirs), dst_ref=_cols(w_ref, theirs),
                                send_sem=send_sems.at[jj], recv_sem=recv_sems.at[jj], **to).wait_recv()
                        if mine_blocks:
                            pltpu.make_async_remote_copy(
                                src_ref=_cols(w_ref, len(mine_blocks)), dst_ref=_cols(w_ref, len(mine_blocks)),
                                send_sem=send_sems.at[jj], recv_sem=recv_sems.at[jj], **to).wait_send()

    issue = lambda refs, send_sems, recv_sems: each(refs[0], send_sems, recv_sems, True)
    await_ = lambda refs, send_sems, recv_sems: each(refs[0], send_sems, recv_sems, False)
    return issue, await_


def _sibling_forward(phase, w_grp):
    issue, await_ = _sibling_forward_parts(phase)

    def body(w_in_ref, w_ref, send_sems, recv_sems):
        del w_in_ref
        issue([w_ref], send_sems, recv_sems)
        await_([w_ref], send_sems, recv_sems)

    return pl.pallas_call(
        body, name="sibling_forward_" + phase, in_specs=[ANY], out_specs=ANY,
        out_shape=jax.ShapeDtypeStruct(w_grp.shape, w_grp.dtype), input_output_aliases={0: 0},
        scratch_shapes=[pltpu.SemaphoreType.DMA((3,)), pltpu.SemaphoreType.DMA((3,))],
    )(w_grp)


def _merge_edges(w, edge0, mixed, name):
    d = w.shape[1]

    def body(e_ref, o_ref):
        o_ref[...] = e_ref[0:DH, :] + e_ref[DH:2 * DH, :]

    def to_block(i):
        r = mixed[-1]
        for kk in range(len(mixed) - 2, -1, -1):
            r = jnp.where(i == kk, mixed[kk], r)
        return r

    return pl.pallas_call(
        body, name=name, grid=(len(mixed),),
        in_specs=[pl.BlockSpec((2 * DH, d), lambda i: (edge0 // 2 + i, 0))],
        out_specs=pl.BlockSpec((DH, d), lambda i: (to_block(i), 0)),
        out_shape=jax.ShapeDtypeStruct(w.shape, w.dtype),
        input_output_aliases={0: 0},
        compiler_params=_params(("arbitrary",)),
    )(w)


def _scatter_start(phase, g_grp, land, singles, halved=False):
    ns = len(singles)

    def issue(refs, send_sems, recv_sems):
        g_ref, land_ref = refs[0], refs[1]
        x, y, c, chips = _place()
        for jj, (px, py) in enumerate(chips):
            to = dict(device_id=(px, py, c), device_id_type=MESH)
            peer = 2 * px + py
            for a in range(ns):
                pltpu.make_async_remote_copy(
                    src_ref=refs[2 + 2 * a].at[peer], dst_ref=refs[3 + 2 * a].at[jj],
                    send_sem=send_sems.at[(1 + ns) * jj + 1 + a], recv_sem=recv_sems.at[(1 + ns) * jj + 1 + a],
                    **to).start()
            for s in range(4):
                for par in ((0, 1) if halved else (None,)):
                    blocks = _phase_blocks(s, phase, False, par)
                    if blocks:
                        @pl.when((peer == s) if par is None else ((peer == s) & (c == par)))
                        def _():
                            for b, blk in blocks:
                                pltpu.make_async_remote_copy(
                                    src_ref=_blk(g_ref, blk), dst_ref=_blk(land_ref.at[jj], b),
                                    send_sem=send_sems.at[(1 + ns) * jj], recv_sem=recv_sems.at[(1 + ns) * jj],
                                    **to).start()

    bufs = [g_grp, land] + [t for pair in singles for t in pair]
    return _split_start("scatter_start_" + phase, issue, bufs, 3 * (1 + ns))


def _scatter_wait(phase, send_sems, recv_sems, bufs, after, halved=False):
    ns = (len(bufs) - 2) // 2

    def await_(refs, send_sems, recv_sems):
        g_ref, land_ref = refs[0], refs[1]
        x, y, c, chips = _place()
        mine = 2 * x + y
        for jj, (px, py) in enumerate(chips):
            to = dict(device_id=(px, py, c), device_id_type=MESH)
            peer = 2 * px + py
            for a in range(ns):
                cp = pltpu.make_async_remote_copy(
                    src_ref=refs[2 + 2 * a].at[peer], dst_ref=refs[3 + 2 * a].at[jj],
                    send_sem=send_sems.at[(1 + ns) * jj + 1 + a], recv_sem=recv_sems.at[(1 + ns) * jj + 1 + a], **to)
                cp.wait_recv()
                cp.wait_send()
            for s in range(4):
                for par in ((0, 1) if halved else (None,)):
                    nblk = len(_phase_blocks(s, phase, False, par))
                    if nblk:
                        both = pltpu.make_async_remote_copy(
                            src_ref=_cols(g_ref, nblk), dst_ref=_cols(land_ref.at[jj], nblk),
                            send_sem=send_sems.at[(1 + ns) * jj], recv_sem=recv_sems.at[(1 + ns) * jj], **to)

                        @pl.when((mine == s) if par is None else ((mine == s) & (c == par)))
                        def _():
                            both.wait_recv()

                        @pl.when((peer == s) if par is None else ((peer == s) & (c == par)))
                        def _():
                            both.wait_send()

    return _split_wait("scatter_wait_" + phase, await_, send_sems, recv_sems, bufs, after)


def _needed_blocks(phase, parity):
    return sorted({blk for s in range(4) for _, blk in _phase_blocks(s, phase, False, parity)})


def _pair_reduce(phase, g_grp):
    n, d = g_grp.shape

    def swap(g_ref, sib_ref, send_sem, recv_sem):
        x, y, c, _ = _place()
        to = dict(device_id=(x, y, 1 - c), device_id_type=MESH)
        for par in range(2):
            give, get = _needed_blocks(phase, 1 - par), _needed_blocks(phase, par)

            @pl.when(c == par)
            def _():
                for blk in give:
                    pltpu.make_async_remote_copy(src_ref=_blk(g_ref, blk), dst_ref=_blk(sib_ref, blk),
                                                 send_sem=send_sem, recv_sem=recv_sem, **to).start()
                pltpu.make_async_remote_copy(src_ref=_cols(g_ref, len(get)), dst_ref=_cols(sib_ref, len(get)),
                                             send_sem=send_sem, recv_sem=recv_sem, **to).wait_recv()
                pltpu.make_async_remote_copy(src_ref=_cols(g_ref, len(give)), dst_ref=_cols(sib_ref, len(give)),
                                             send_sem=send_sem, recv_sem=recv_sem, **to).wait_send()

    sib = pl.pallas_call(
        swap, name="pair_swap_" + phase, in_specs=[ANY], out_specs=ANY,
        out_shape=jax.ShapeDtypeStruct((n, d), g_grp.dtype),
        scratch_shapes=[pltpu.SemaphoreType.DMA, pltpu.SemaphoreType.DMA],
    )(*_in_hbm(g_grp))

    lists = [_needed_blocks(phase, par) for par in range(2)]
    longest = max(len(t) for t in lists)
    table = jnp.asarray([t + [t[-1]] * (longest - len(t)) for t in lists], jnp.int32)[lax.axis_index("c")]

    def add(t_ref, a_ref, b_ref, o_ref):
        o_ref[...] = (a_ref[...].astype(F32) + b_ref[...].astype(F32)).astype(o_ref.dtype)

    blk = pl.BlockSpec((DH, d), lambda i, t: (t[i], 0))
    return pl.pallas_call(
        add, name="pair_add_" + phase,
        grid_spec=pltpu.PrefetchScalarGridSpec(num_scalar_prefetch=1, grid=(longest,),
                                               in_specs=[blk, blk], out_specs=blk),
        out_shape=jax.ShapeDtypeStruct((n, d), g_grp.dtype),
        compiler_params=_params(("arbitrary",)),
    )(table, g_grp, sib)


def _sum_shard(g_g, g_c, land):
    d = g_g.shape[1]
    chip = 2 * lax.axis_index("x") + lax.axis_index("y")

    def body(t_ref, gg_ref, gc_ref, land_ref, o_ref):
        b = pl.program_id(0)
        in_g = t_ref[2, b] == 1
        own = jnp.where(in_g, gg_ref[...].astype(F32), gc_ref[...].astype(F32))
        for jj in range(3):
            own = own + land_ref[jj].astype(F32)
        o_ref[...] = jnp.where(in_g & (b % 2 != lax.axis_index("c")), 0.0, own)

    return pl.pallas_call(
        body, name="sum_w_in",
        grid_spec=pltpu.PrefetchScalarGridSpec(
            num_scalar_prefetch=1, grid=(ALIGNED_BLOCKS,),
            in_specs=[pl.BlockSpec((DH, d), lambda b, t: (t[0, b], 0)), pl.BlockSpec((DH, d), lambda b, t: (t[1, b], 0)),
                      pl.BlockSpec((3, DH, d), lambda b, t: (0, b, 0))],
            out_specs=pl.BlockSpec((DH, d), lambda b, t: (b, 0))),
        out_shape=jax.ShapeDtypeStruct((ALIGNED_W, d), F32),
        compiler_params=_params(("arbitrary",)),
    )(_block_table(chip, False, 0, 0), g_g, g_c, land)


def _sum_rows(stack, land, rows):
    _, r, d = stack.shape
    rows = min(rows, r)
    chip = 2 * lax.axis_index("x") + lax.axis_index("y")

    def body(t_ref, own_ref, land_ref, o_ref):
        acc = own_ref[0].astype(F32)
        for jj in range(3):
            acc = acc + land_ref[jj].astype(F32)
        o_ref[...] = acc

    return pl.pallas_call(
        body, name="sum_w_out",
        grid_spec=pltpu.PrefetchScalarGridSpec(
            num_scalar_prefetch=1, grid=(r // rows,),
            in_specs=[pl.BlockSpec((1, rows, d), lambda i, t: (t[0], i, 0)),
                      pl.BlockSpec((3, rows, d), lambda i, t: (0, i, 0))],
            out_specs=pl.BlockSpec((rows, d), lambda i, t: (i, 0))),
        out_shape=jax.ShapeDtypeStruct((r, d), F32),
        compiler_params=_params(("arbitrary",)),
    )(jnp.reshape(chip, (1,)).astype(jnp.int32), stack, land)


def _exchange_parts(n_swap, with_pack):
    def copies(refs, send_sems, recv_sems):
        x, y, c, _ = _place()
        me = 4 * x + 2 * y + c
        cps = [pltpu.make_async_remote_copy(
            src_ref=refs[2 * a], dst_ref=refs[2 * a + 1], send_sem=send_sems.at[a], recv_sem=recv_sems.at[a],
            device_id=(x, y, 1 - c), device_id_type=MESH) for a in range(n_swap)]
        if with_pack:
            pack_ref, packs = refs[2 * n_swap], refs[2 * n_swap + 1]
            for r in range(1, 8):
                dx, dy, dc = (r >> 2) & 1, (r >> 1) & 1, r & 1
                peer = (x + dx - 2 * x * dx, y + dy - 2 * y * dy, c + dc - 2 * c * dc)
                cps.append(pltpu.make_async_remote_copy(
                    src_ref=pack_ref, dst_ref=packs.at[me], send_sem=send_sems.at[n_swap + r - 1],
                    recv_sem=recv_sems.at[n_swap + r - 1], device_id=peer, device_id_type=MESH))
        return cps

    def issue(refs, send_sems, recv_sems):
        for cp in copies(refs, send_sems, recv_sems):
            cp.start()

    def await_(refs, send_sems, recv_sems):
        cps = copies(refs, send_sems, recv_sems)
        for cp in cps:
            cp.wait_recv()
        for cp in cps:
            cp.wait_send()

    return issue, await_, n_swap + (7 if with_pack else 0)


def _sum_packs(pack, packs):
    x, y, c = lax.axis_index("x"), lax.axis_index("y"), lax.axis_index("c")
    me = jnp.reshape(4 * x + 2 * y + c, (1,)).astype(jnp.int32)

    def body(me_ref, own_ref, p_ref, o_ref):
        acc = jnp.where(me_ref[0] == 0, own_ref[...], p_ref[0])
        for d in range(1, 8):
            acc = acc + jnp.where(me_ref[0] == d, own_ref[...], p_ref[d])
        o_ref[...] = acc

    full = lambda s: pl.BlockSpec(s.shape, lambda i, t: (0,) * s.ndim)
    return pl.pallas_call(
        body, name="sum_packs",
        grid_spec=pltpu.PrefetchScalarGridSpec(num_scalar_prefetch=1, grid=(1,), in_specs=[full(pack), full(packs)],
                                               out_specs=full(pack)),
        out_shape=jax.ShapeDtypeStruct(pack.shape, F32),
    )(me, pack, packs)


def _adamw_update(g, w_ref, m_ref, v_ref, go, do, mo, vo):
    c1 = 1.0 / (1.0 - ADAM_B1 ** ADAM_STEP)
    c2 = 1.0 / (1.0 - ADAM_B2 ** ADAM_STEP)
    mn = ADAM_B1 * m_ref[...] + (1.0 - ADAM_B1) * g
    vn = ADAM_B2 * v_ref[...] + (1.0 - ADAM_B2) * (g * g)
    go[...] = g
    mo[...] = mn
    vo[...] = vn
    do[...] = -ADAM_LR * ((mn * c1) / (jnp.sqrt(vn * c2) + ADAM_EPS) + ADAM_WD * w_ref[...])


def _adamw(w, m, v, g1, g2, rows, name):
    r, cdim = w.shape
    rows = min(rows, r)

    def body(*refs):
        n_in = 4 if g2 is None else 5
        w_ref, m_ref, v_ref, g_ref = refs[:4]
        g = g_ref[...] if g2 is None else g_ref[...] + refs[4][...]
        _adamw_update(g, w_ref, m_ref, v_ref, *refs[n_in:n_in + 4])

    blk = pl.BlockSpec((rows, cdim), lambda i: (i, 0))
    args = [w, m, v, g1] + ([] if g2 is None else [g2])
    shp = jax.ShapeDtypeStruct((r, cdim), F32)
    return pl.pallas_call(
        body, name=name, grid=(r // rows,),
        in_specs=[blk] * len(args), out_specs=[blk] * 4, out_shape=[shp] * 4,
        compiler_params=_params(("parallel",), 20 * rows * cdim * 4 + 8 * 2**20),
    )(*_in_hbm(*args))


def _adamw_shard(wt, mt, vt, g1, g2):
    r, d = wt.shape
    cols = min(256, d)

    def body(w_ref, m_ref, v_ref, g_ref, g2_ref, go, do, mo, vo, pad_ref):
        chip = 2 * lax.axis_index("x") + lax.axis_index("y")
        back = [(ALIGNED_W - s) % ALIGNED_W for s in SHIFTS]
        pad_ref[...] = pltpu.roll(g_ref[...] + g2_ref[...], _by_chip(chip, back), 0)
        outs = [o.at[:, 0, :] for o in (go, do, mo, vo)]
        _adamw_update(pad_ref[0:r, :], w_ref, m_ref, v_ref, *outs)

    blk = pl.BlockSpec((r, cols), lambda i: (0, i))
    gblk = pl.BlockSpec((ALIGNED_W, cols), lambda i: (0, i))
    oblk = pl.BlockSpec((r, 1, cols), lambda i: (0, 0, i))
    shp = jax.ShapeDtypeStruct((r, 1, d), F32)
    return pl.pallas_call(
        body, name="adamw_w_in", grid=(d // cols,),
        in_specs=[blk] * 3 + [gblk] * 2, out_specs=[oblk] * 4, out_shape=[shp] * 4,
        scratch_shapes=[pltpu.VMEM((ALIGNED_W, cols), F32)],
        compiler_params=_params(("parallel",), 24 * ALIGNED_W * cols * 4 + 8 * 2**20),
    )(wt, mt, vt, g1, g2)


def _pad_lanes(a, width):
    return jnp.pad(a, ((0, 0), (0, width - a.shape[1])))


def _gathered_to_full(g):
    return jnp.transpose(g, (1, 0, 2)).reshape(g.shape[1], 4 * g.shape[2])


def _row(a):
    return _pad_lanes(a.reshape(1, -1), 1024)


def _small_pack(nin, cb, fn, al, dt, gn, cqw_shard, cw_shard):
    ad = jnp.concatenate([al.reshape(1, -1), dt.reshape(1, -1)], axis=1)
    rows = [_row(nin), _row(cb), _row(fn), _row(ad), _row(gn), cqw_shard.reshape(3, 1024), _row(cw_shard)]
    out = jnp.concatenate(rows, axis=0)
    return jnp.pad(out, ((0, 16 - out.shape[0]), (0, 0)))


def kernel(x, norm_in_w, w_in, conv_qkv_w, A_log, dt_bias, gdn_norm_w, conv_w, conv_b, w_out, final_norm_w, loss_target, m_norm_in_w, m_w_in, m_conv_qkv_w, m_A_log, m_dt_bias, m_gdn_norm_w, m_conv_w, m_conv_b, m_w_out, m_final_norm_w, v_norm_in_w, v_w_in, v_conv_qkv_w, v_A_log, v_dt_bias, v_gdn_norm_w, v_conv_w, v_conv_b, v_w_out, v_final_norm_w):
    chip = 2 * lax.axis_index("x") + lax.axis_index("y")
    a_shard = _align_shard(jnp.transpose(w_in, (2, 0, 1)))
    d_model = x.shape[-1]
    stack = lambda s: lax.empty((4,) + s.shape, s.dtype)
    wg0 = lax.empty((WG_BLOCKS * DH, d_model), BF16)
    wc0 = lax.empty((WC_BLOCKS * DH, d_model), BF16)
    ss_g, rs_g, bufs_g, tok_g = _gather_start("g", a_shard, wg0, [(conv_qkv_w[0], stack(conv_qkv_w[0]))])
    wo_b = _cast_bf16(w_out[0], 256, "cast_w_out", tok_g)
    ss_c, rs_c, bufs_c, tok_c = _gather_start("c", bufs_g[0], wc0,
                                              [(conv_w[0], stack(conv_w[0])), (wo_b, stack(wo_b))])
    wg1, wc1, wog1, cqg1, cwg1 = _place_own(bufs_c[0], bufs_c[4], bufs_g[2], bufs_c[2],
                                            [bufs_g[1], bufs_c[1], bufs_c[5], bufs_g[3], bufs_c[3]])
    x0 = x[0]
    h = _rms_in(x0, _tie(_tie(norm_in_w, tok_g, "after_gather_start_g"), tok_c, "after_gather_start_c"))
    adam_in = [jnp.transpose(a[0]) for a in (w_in, m_w_in, v_w_in)]
    sp = lambda nin, cb, fn, al, dt, gn, cq, cwv: _small_pack(nin, cb, fn, al, dt, gn, cq[0], cwv[0])
    w_s = sp(norm_in_w, conv_b, final_norm_w, A_log, dt_bias, gdn_norm_w, conv_qkv_w, conv_w)
    m_s = sp(m_norm_in_w, m_conv_b, m_final_norm_w, m_A_log, m_dt_bias, m_gdn_norm_w, m_conv_qkv_w, m_conv_w)
    v_s = sp(v_norm_in_w, v_conv_b, v_final_norm_w, v_A_log, v_dt_bias, v_gdn_norm_w, v_conv_qkv_w, v_conv_w)
    a_thru, wg, _, cq_g = _gather_wait("g", ss_g, rs_g, [bufs_c[0], wg1, bufs_g[2], cqg1],
                                       [h, w_s, m_s, v_s] + adam_in[1:])
    w_g = _merge_edges(_sibling_forward("g", wg), G_EDGE, G_MIXED, "merge_edges_g")
    cqw = _gathered_to_full(cq_g)
    ad = jnp.pad(jnp.concatenate([A_log, dt_bias], axis=0), ((0, 0), (A_LANE, 0)))
    fwd_c = {}

    def on_q(q):
        _, wc, _, cw_g, _, wo_g = _gather_wait("c", ss_c, rs_c,
                                               [a_thru, wc1, bufs_c[2], cwg1, bufs_c[4], wog1], q)
        issue, _ = _sibling_forward_parts("c")
        ss, rs, (wc,), tok = _split_start("sibling_forward_start_c", issue, [wc], 3)
        fwd_c.update(ss=ss, rs=rs, wc=wc, cw_g=cw_g, wo_g=wo_g)
        return _tie(q, tok, "after_sibling_forward_start_c")

    def late(o):
        _, await_ = _sibling_forward_parts("c")
        (wc,) = _split_wait("sibling_forward_wait_c", await_, fwd_c["ss"], fwd_c["rs"], [fwd_c["wc"]], o)
        return (_merge_edges(wc, C_EDGE, C_MIXED, "merge_edges_c"), fwd_c["wo_g"].reshape(2 * GW, d_model),
                _gathered_to_full(fwd_c["cw_g"]))

    scat = {}

    def on_grad_c(g_c, g_wout, do):
        go4 = g_wout.reshape(4, GW // 2, d_model)
        land = lax.empty((3, ALIGNED_W, d_model), BF16)
        land_o = lax.empty((3, GW // 2, d_model), BF16)
        ss, rs, bufs, tok = _scatter_start("c", g_c, land, [(go4, land_o)])
        scat["c"] = (ss, rs, bufs)
        return _tie(do, tok, "after_scatter_start_c")

    def on_grad_g(g_g, dproj_g):
        ss, rs, bufs, tok = _scatter_start("g", _pair_reduce("g", g_g), scat["c"][2][1], [], halved=True)
        scat["g"] = (ss, rs, bufs)
        return _tie(dproj_g, tok, "after_scatter_start_g")

    gx, sm, _ = _local_step(x0, loss_target[0], h, w_g, cqw, late, norm_in_w, ad, gdn_norm_w, conv_b,
                            final_norm_w.reshape(1, -1), on_grad_c, on_grad_g, on_q)

    ss, rs, bufs = scat["c"]
    g_c, land, go4, land_o = _scatter_wait("c", ss, rs, [bufs[0], scat["g"][2][1], bufs[2], bufs[3]], gx)
    part_out = _sum_rows(go4, land_o, 128)
    ad_g = jnp.concatenate([sm["al"][:, A_LANE:], sm["dt"][:, A_LANE:]], axis=1)
    pack = jnp.concatenate([_row(sm["nin"]), _row(sm["cb"]), _row(sm["fn"]), _row(ad_g), _row(sm["gn"]),
                            jnp.concatenate(sm["cq"], axis=1).reshape(12, 1024), sm["cw"], _row(sm["loss"])], axis=0)
    pack = jnp.pad(pack, ((0, PACK_ROWS - pack.shape[0]), (0, 0)))
    issue, await_a, nsem = _exchange_parts(1, True)
    ss_a, rs_a, bufs_a, tok_a = _split_start(
        "exchange_start_small", issue,
        [part_out, lax.empty(part_out.shape, F32), pack, lax.empty((8,) + pack.shape, F32)], nsem)
    ss, rs, bufs = scat["g"]
    g_g, land = _scatter_wait("g", ss, rs, [bufs[0], land], [gx, tok_a], halved=True)
    part_in = _sum_shard(g_g, g_c, land)
    issue, await_b, nsem = _exchange_parts(1, False)
    ss_b, rs_b, bufs_b, tok_b = _split_start("exchange_start_w_in", issue,
                                             [part_in, lax.empty(part_in.shape, F32)], nsem)
    part_out, sib_out, pack, packs = _split_wait("exchange_wait_small", await_a, ss_a, rs_a, bufs_a, tok_b)
    tot = _sum_packs(pack, packs)
    g_wo, d_wo, m_wo, v_wo = _adamw(w_out[0], m_w_out[0], v_w_out[0], part_out, sib_out, 128, "adamw_w_out")
    g_cq_sh = lax.dynamic_slice_in_dim(tot[R_CQ:R_CQ + 12].reshape(4, 3 * GW), chip * 768, 768, axis=1)
    g_cw_sh = lax.dynamic_slice_in_dim(tot[R_CW:R_CW + 3], chip * 256, 256, axis=1)
    g_s = _small_pack(tot[R_NIN], tot[R_CB], tot[R_FN], tot[R_AD, :HEADS], tot[R_AD, HEADS:2 * HEADS],
                      tot[R_GN, :DH], g_cq_sh, g_cw_sh)
    small = _adamw(w_s, m_s, v_s, g_s, None, 16, "adamw_small")
    part_in, sib_in = _split_wait("exchange_wait_w_in", await_b, ss_b, rs_b, bufs_b, [small[0], d_wo])
    g_wi, d_wi, m_wi, v_wi = [jnp.transpose(a, (1, 2, 0))[0] for a in _adamw_shard(*adam_in, part_in, sib_in)]

    def unpack(a, big_in, big_out):
        return (a[0:1], big_in[None], a[5:8].reshape(1, 4, 768), a[3:4, :HEADS], a[3:4, HEADS:2 * HEADS],
                a[4:5, :DH], a[8, :768].reshape(1, 3, 256), a[1:2], big_out[None], a[2])

    loss = tot[R_LOSS, 0]
    return (loss, gx[None], *unpack(small[0], g_wi, g_wo), *unpack(small[1], d_wi, d_wo),
            *unpack(small[2], m_wi, m_wo), *unpack(small[3], v_wi, v_wo))
```

```python
import jax
import jax.numpy as jnp
from jax import lax
from jax.experimental import pallas as pl
from jax.experimental.pallas import tpu as pltpu

F32 = jnp.float32
BF16 = jnp.bfloat16
MESH = pl.DeviceIdType.MESH
ANY = pl.BlockSpec(memory_space=pl.ANY)

HEADS = 8
DH = 128
CH = 64
GW = HEADS * DH
EPS = 1e-6
VMEM_V7X = 64 * 1024 * 1024

QB, KB, VB, ZB, BAB = 0, 8, 16, 24, 32
A_LANE = 120
NG, NC = 33, 32
GW_COLS, CW_COLS = NG * DH, NC * DH

SHARD_W = 2052
ALIGNED_BLOCKS = 17
ALIGNED_W = ALIGNED_BLOCKS * DH
SHIFTS = (0, 4, ALIGNED_W - 8, ALIGNED_W - 4)
G_EDGE, C_EDGE = 34, 32
G_SPARE, C_SPARE = 33, 34
WG_BLOCKS, WC_BLOCKS = 38, 36
G_MIXED, C_MIXED = (2, BAB), (4 * 7 + 1,)


def _shard_blocks(chip, edges):
    g, c = "g", "c"
    if chip == 0:
        out = [(g, 3 * b) for b in range(8)] + [(g, 3 * b + 1) for b in range(8)] + [(g, G_EDGE, G_MIXED[0])]
    elif chip == 1:
        out = [(g, G_EDGE + 1, G_MIXED[0])] + [(g, 3 * b + 2) for b in range(1, 8)]
        out += [(g, ZB + b) for b in range(8)] + [(g, G_EDGE + 2, G_MIXED[1])]
    elif chip == 2:
        out = [(c, 4 * b) for b in range(8)] + [(c, 4 * b + 1) for b in range(7)]
        out += [(c, C_EDGE, C_MIXED[0]), (g, G_EDGE + 3, G_MIXED[1])]
    else:
        out = [(c, 4 * b + 2) for b in range(8)] + [(c, 4 * b + 3) for b in range(8)] + [(c, C_EDGE + 1, C_MIXED[0])]
    return [(o[0], o[1] if (edges or len(o) == 2) else o[2]) for o in out]


def _by_chip(chip, vals):
    if all(v == vals[0] for v in vals):
        return vals[0]
    r = vals[3]
    for kk in (2, 1, 0):
        r = jnp.where(chip == kk, vals[kk], r)
    return r

ADAM_LR, ADAM_B1, ADAM_B2, ADAM_EPS, ADAM_WD, ADAM_STEP = 0.001, 0.9, 0.999, 1e-08, 0.01, 10

R_NIN, R_CB, R_FN, R_AD, R_GN, R_CQ, R_CW, R_LOSS, PACK_ROWS = 0, 1, 2, 3, 4, 5, 17, 20, 24

NN = ((1,), (0,))
NT = ((1,), (1,))
TN = ((0,), (0,))


def _dot(a, b, dims=NN, mode="lo"):
    dn = (dims, ((), ()))
    if mode == "hi":
        return lax.dot_general(a, b, dn, precision=lax.Precision.HIGHEST, preferred_element_type=F32)
    ah, bh = a.astype(BF16), b.astype(BF16)
    out = lax.dot_general(ah, bh, dn, preferred_element_type=F32)
    if mode == "x3":
        al = (a - ah.astype(F32)).astype(BF16)
        bl = (b - bh.astype(F32)).astype(BF16)
        out = out + lax.dot_general(ah, bl, dn, preferred_element_type=F32)
        out = out + lax.dot_general(al, bh, dn, preferred_element_type=F32)
    return out


P_GRAM, P_INV, P_SOL, P_SCAN, P_SCANB, P_BWD = "lo", "lo", "lo", "lo", "lo", "lo"
P_CUM = "x3"


def _params(sem=None, vmem=None):
    kw = {}
    if sem is not None:
        kw["dimension_semantics"] = sem
    if vmem is not None:
        kw["vmem_limit_bytes"] = int(min(max(vmem, 32 * 2**20), VMEM_V7X - 8 * 2**20))
    return pltpu.CompilerParams(**kw)


def _in_hbm(*arrays):
    return [pltpu.with_memory_space_constraint(a, pltpu.HBM) for a in arrays]


def _sigmoid(x):
    return 1.0 / (1.0 + jnp.exp(-x))


def _dsilu(x, s):
    return s * (1.0 + x * (1.0 - s))


def _rows(shape):
    return lax.broadcasted_iota(jnp.int32, shape, 0)


def _shift_down(x, s):
    if s == 0:
        return x
    return jnp.where(_rows(x.shape) >= s, pltpu.roll(x, s, 0), 0.0)


def _shift_up(x, s):
    if s == 0:
        return x
    n = x.shape[0]
    return jnp.where(_rows(x.shape) < n - s, pltpu.roll(x, n - s, 0), 0.0)


def _matmul(a, b, dims, out_dtype, tm, tn, tk, name, add=None, n=None, b_outer=False):
    if dims == NN:
        (m, k), n = a.shape, b.shape[1]
    elif dims == NT:
        (m, k), n = a.shape, (n or b.shape[0])
    else:
        (k, m), n = a.shape, b.shape[1]
    tm, tn, tk = min(tm, m), min(tn, n), min(tk, k)
    assert m % tm == 0 and n % tn == 0 and k % tk == 0, (name, m, n, k, tm, tn, tk)
    nk = k // tk

    def body(*refs):
        if add is None:
            a_ref, b_ref, o_ref = refs[:3]
            add_ref = None
        else:
            a_ref, b_ref, add_ref, o_ref = refs[:4]
        part = _dot(a_ref[...], b_ref[...], dims)
        if nk == 1:
            if add_ref is not None:
                part = part + add_ref[...]
            o_ref[...] = part.astype(out_dtype)
            return
        acc = refs[-1]
        kk = pl.program_id(2)

        @pl.when(kk == 0)
        def _():
            acc[...] = part

        @pl.when(kk > 0)
        def _():
            acc[...] += part

        @pl.when(kk == nk - 1)
        def _():
            r = acc[...]
            if add_ref is not None:
                r = r + add_ref[...]
            o_ref[...] = r.astype(out_dtype)

    ij = (lambda g0, g1: (g1, g0)) if b_outer else (lambda g0, g1: (g0, g1))

    def spec(shape, pick):
        return pl.BlockSpec(shape, lambda g0, g1, kk: pick(*ij(g0, g1), kk))

    a_spec = spec((tk, tm), lambda i, j, kk: (kk, i)) if dims == TN else spec((tm, tk), lambda i, j, kk: (i, kk))
    b_spec = spec((tn, tk), lambda i, j, kk: (j, kk)) if dims == NT else spec((tk, tn), lambda i, j, kk: (kk, j))
    o_spec = spec((tm, tn), lambda i, j, kk: (i, j))
    in_specs = [a_spec, b_spec]
    args = [a, b]
    if add is not None:
        in_specs.append(o_spec)
        args.append(add)
    osz = jnp.dtype(out_dtype).itemsize
    est = 2 * (tm * tk * a.dtype.itemsize + tk * tn * b.dtype.itemsize + tm * tn * osz)
    est += 3 * tm * tn * 4 + (2 * tm * tn * 4 if add is not None else 0)
    return pl.pallas_call(
        body, name=name, grid=(n // tn, m // tm, nk) if b_outer else (m // tm, n // tn, nk),
        in_specs=in_specs, out_specs=o_spec,
        out_shape=jax.ShapeDtypeStruct((m, n), out_dtype),
        scratch_shapes=[pltpu.VMEM((tm, tn), F32)] if nk > 1 else [],
        compiler_params=_params(("parallel", "parallel", "arbitrary"), est + 8 * 2**20),
    )(*args)


def _cast_bf16(a, rows, name, after):
    r, c = a.shape
    rows = min(rows, r)

    def body(a_ref, t_ref, o_ref):
        del t_ref
        o_ref[...] = a_ref[...].astype(BF16)

    return pl.pallas_call(
        body, name=name, grid=(r // rows,),
        in_specs=[pl.BlockSpec((rows, c), lambda i: (i, 0)), ANY],
        out_specs=pl.BlockSpec((rows, c), lambda i: (i, 0)),
        out_shape=jax.ShapeDtypeStruct((r, c), BF16),
        compiler_params=_params(("parallel",)),
    )(a, after)


def _align_shard(wt):
    r, _, d = wt.shape
    cols = min(256, d)

    def body(w_ref, o_ref, pad_ref):
        chip = 2 * lax.axis_index("x") + lax.axis_index("y")
        pad_ref[...] = jnp.zeros_like(pad_ref)
        pad_ref[0:r, :] = w_ref[:, 0, :]
        o_ref[...] = pltpu.roll(pad_ref[...], _by_chip(chip, SHIFTS), 0).astype(BF16)

    return pl.pallas_call(
        body, name="align_shard", grid=(d // cols,),
        in_specs=[pl.BlockSpec((r, 1, cols), lambda i: (0, 0, i))],
        out_specs=pl.BlockSpec((ALIGNED_W, cols), lambda i: (0, i)),
        out_shape=jax.ShapeDtypeStruct((ALIGNED_W, d), BF16),
        scratch_shapes=[pltpu.VMEM((ALIGNED_W, cols), F32)],
        compiler_params=_params(("parallel",)),
    )(wt)


def _rms_in(x, w):
    n, d = x.shape
    tr = min(256, n)

    def body(x_ref, w_ref, h_ref):
        xv = x_ref[...]
        r = lax.rsqrt(jnp.mean(xv * xv, axis=-1, keepdims=True) + EPS)
        h_ref[...] = (xv * r * w_ref[...]).astype(BF16)

    return pl.pallas_call(
        body, name="rms_in", grid=(n // tr,),
        in_specs=[pl.BlockSpec((tr, d), lambda i: (i, 0)), pl.BlockSpec((1, d), lambda i: (0, 0))],
        out_specs=pl.BlockSpec((tr, d), lambda i: (i, 0)),
        out_shape=jax.ShapeDtypeStruct((n, d), BF16),
        compiler_params=_params(("parallel",)),
    )(x, w)


def _conv_silu(p, w_ref, taps):
    c = None
    for j in range(taps):
        t = _shift_down(p, taps - 1 - j) * w_ref[j:j + 1, :]
        c = t if c is None else c + t
    return c


def _prep_qkv(proj, cw):
    n = proj.shape[0]

    def body(p3, wq, wk, wv, q_ref, k_ref, v_ref):
        for kind, (w_ref, o_ref) in enumerate(((wq, q_ref), (wk, k_ref), (wv, v_ref))):
            c = _conv_silu(p3[:, kind * DH:(kind + 1) * DH], w_ref, 4)
            a = c * _sigmoid(c)
            if kind < 2:
                r = lax.rsqrt(jnp.sum(a * a, axis=-1, keepdims=True) + EPS)
                a = a * (r * (DH ** -0.5 if kind == 0 else 1.0))
            o_ref[...] = a

    col = pl.BlockSpec((n, DH), lambda h: (0, h))
    wcol = lambda base: pl.BlockSpec((4, DH), lambda h: (0, base + h))
    out = jax.ShapeDtypeStruct((n, GW), F32)
    return pl.pallas_call(
        body, name="prep_qkv", grid=(HEADS,),
        in_specs=[pl.BlockSpec((n, 3 * DH), lambda h: (0, h)), wcol(QB), wcol(KB), wcol(VB)],
        out_specs=[col] * 3, out_shape=[out] * 3,
        compiler_params=_params(("parallel",), 40 * 2**20),
    )(proj, cw, cw, cw)


def _prep_qkv_bwd(proj, cw, dq, dk, dv, dproj):
    n = proj.shape[0]

    def body(p3, wq, wk, wv, dq_ref, dk_ref, dv_ref, _, o3, gq, gk, gv):
        for kind, (w_ref, d_ref, g_ref) in enumerate(((wq, dq_ref, gq), (wk, dk_ref, gk), (wv, dv_ref, gv))):
            p = p3[:, kind * DH:(kind + 1) * DH]
            shifted = [_shift_down(p, 3 - j) for j in range(4)]
            c = shifted[0] * w_ref[0:1, :]
            for j in range(1, 4):
                c = c + shifted[j] * w_ref[j:j + 1, :]
            s = _sigmoid(c)
            a = c * s
            d = d_ref[...]
            if kind < 2:
                r = lax.rsqrt(jnp.sum(a * a, axis=-1, keepdims=True) + EPS)
                sc = DH ** -0.5 if kind == 0 else 1.0
                d = (sc * r) * (d - a * ((r * r) * jnp.sum(d * a, axis=-1, keepdims=True)))
            dc = d * _dsilu(c, s)
            dp = None
            for j in range(4):
                g_ref[j:j + 1, :] = jnp.sum(dc * shifted[j], axis=0, keepdims=True)
                t = _shift_up(dc, 3 - j) * w_ref[j:j + 1, :]
                dp = t if dp is None else dp + t
            o3[:, kind * DH:(kind + 1) * DH] = dp.astype(BF16)

    col = pl.BlockSpec((n, DH), lambda h: (0, h))
    wcol = lambda base: pl.BlockSpec((4, DH), lambda h: (0, base + h))
    p3spec = pl.BlockSpec((n, 3 * DH), lambda h: (0, h))
    return pl.pallas_call(
        body, name="prep_qkv_bwd", grid=(HEADS,),
        in_specs=[p3spec, wcol(QB), wcol(KB), wcol(VB), col, col, col, ANY],
        out_specs=[p3spec] + [wcol(0)] * 3,
        out_shape=[jax.ShapeDtypeStruct(dproj.shape, BF16)] + [jax.ShapeDtypeStruct((4, GW), F32)] * 3,
        input_output_aliases={7: 0},
        compiler_params=_params(("parallel",), 48 * 2**20),
    )(proj, cw, cw, cw, dq, dk, dv, dproj)


CPB = 8
SCAN_CPS = 4


def _tri(lower, rows):
    i = lax.broadcasted_iota(jnp.int32, (rows, rows), 0)
    j = lax.broadcasted_iota(jnp.int32, (rows, rows), 1)
    return jnp.where((i // CH == j // CH) & ((i >= j) if lower else (j >= i)), 1.0, 0.0)


def _lane(shape):
    return lax.broadcasted_iota(jnp.int32, shape, 1)


def _prep_bg(proj, ad):
    n = proj.shape[0]
    nch = n // CH
    cpb = CPB if nch % CPB == 0 else 1
    rows = cpb * CH

    def body(p_ref, ad_ref, bg_ref, bgt_ref):
        p = p_ref[...]
        lane = _lane(p.shape)
        beta = _sigmoid(p)
        xa = p + ad_ref[1:2, :]
        sp = jnp.maximum(xa, 0.0) + jnp.log(1.0 + jnp.exp(-jnp.abs(xa)))
        g = pltpu.roll(-jnp.exp(ad_ref[0:1, :]) * sp, DH - A_LANE + HEADS, 1)
        gc = _dot(_tri(True, rows), g, NN, P_CUM)
        bg = jnp.where(lane < HEADS, beta, jnp.where(lane < 2 * HEADS, gc, 0.0))
        bg_ref[...] = bg
        for ci in range(cpb):
            bgt_ref[ci] = bg[ci * CH:(ci + 1) * CH, :].T

    return pl.pallas_call(
        body, name="prep_bg", grid=(nch // cpb,),
        in_specs=[pl.BlockSpec((rows, DH), lambda i: (i, BAB)), pl.BlockSpec((2, DH), lambda i: (0, 0))],
        out_specs=[pl.BlockSpec((rows, DH), lambda i: (i, 0)), pl.BlockSpec((cpb, DH, CH), lambda i: (i, 0, 0))],
        out_shape=[jax.ShapeDtypeStruct((n, DH), F32), jax.ShapeDtypeStruct((nch, DH, CH), F32)],
        compiler_params=_params(("parallel",)),
    )(*_in_hbm(proj, ad))


def _prep_bg_bwd(proj, ad, dbg, dproj):
    n = proj.shape[0]
    nch = n // CH
    cpb = CPB if nch % CPB == 0 else 1
    rows = cpb * CH

    def body(p_ref, ad_ref, d_ref, _, o_ref, ga_ref, gd_ref):
        p = p_ref[...]
        d = d_ref[...]
        lane = _lane(p.shape)
        beta = _sigmoid(p)
        xa = p + ad_ref[1:2, :]
        sp = jnp.maximum(xa, 0.0) + jnp.log(1.0 + jnp.exp(-jnp.abs(xa)))
        na = -jnp.exp(ad_ref[0:1, :])
        dg = pltpu.roll(_dot(_tri(False, rows), d, NN, P_CUM), A_LANE - HEADS, 1)
        da = dg * na * _sigmoid(xa)
        is_g = lane >= A_LANE
        o_ref[...] = jnp.where(lane < HEADS, d * beta * (1.0 - beta), jnp.where(is_g, da, 0.0)).astype(BF16)
        ga = jnp.sum(jnp.where(is_g, dg * na * sp, 0.0), axis=0, keepdims=True)
        gd = jnp.sum(jnp.where(is_g, da, 0.0), axis=0, keepdims=True)

        @pl.when(pl.program_id(0) == 0)
        def _():
            ga_ref[...] = jnp.zeros_like(ga_ref)
            gd_ref[...] = jnp.zeros_like(gd_ref)

        ga_ref[...] += ga
        gd_ref[...] += gd

    one = pl.BlockSpec((1, DH), lambda i: (0, 0))
    return pl.pallas_call(
        body, name="prep_bg_bwd", grid=(nch // cpb,),
        in_specs=[pl.BlockSpec((rows, DH), lambda i: (i, BAB)), pl.BlockSpec((2, DH), lambda i: (0, 0)),
                  pl.BlockSpec((rows, DH), lambda i: (i, 0)), ANY],
        out_specs=[pl.BlockSpec((rows, DH), lambda i: (i, BAB)), one, one],
        out_shape=[jax.ShapeDtypeStruct(dproj.shape, BF16), jax.ShapeDtypeStruct((1, DH), F32),
                   jax.ShapeDtypeStruct((1, DH), F32)],
        input_output_aliases={3: 0},
        compiler_params=_params(("arbitrary",)),
    )(proj, ad, dbg, dproj)


def _gdn_out(o, proj, wg):
    n = o.shape[0]

    def body(o_ref, z_ref, w_ref, y_ref):
        ov, z = o_ref[...], z_ref[...]
        r = lax.rsqrt(jnp.mean(ov * ov, axis=-1, keepdims=True) + EPS)
        y_ref[...] = (ov * r * w_ref[...] * (z * _sigmoid(z))).astype(BF16)

    return pl.pallas_call(
        body, name="gdn_out", grid=(HEADS,),
        in_specs=[pl.BlockSpec((n, DH), lambda h: (0, h)), pl.BlockSpec((n, DH), lambda h: (0, ZB + h)),
                  pl.BlockSpec((1, DH), lambda h: (0, 0))],
        out_specs=pl.BlockSpec((n, DH), lambda h: (0, h)),
        out_shape=jax.ShapeDtypeStruct((n, 2 * GW), BF16),
        compiler_params=_params(("parallel",)),
    )(o, proj, wg)


def _gdn_out_bwd(o, proj, wg, dout_b, w_out):
    n = o.shape[0]
    d_model = dout_b.shape[1]

    def body(o_ref, z_ref, w_ref, g_ref, wo_ref, do_ref, dz_ref, gw_ref):
        ov, z, w = o_ref[...], z_ref[...], w_ref[...]
        d = _dot(g_ref[...], wo_ref[...], NT)
        r = lax.rsqrt(jnp.mean(ov * ov, axis=-1, keepdims=True) + EPS)
        nrm = ov * r
        s = _sigmoid(z)
        dz_ref[...] = (d * (nrm * w) * _dsilu(z, s)).astype(BF16)
        dn_w = d * (z * s)
        gw = jnp.sum(dn_w * nrm, axis=0, keepdims=True)
        dn = dn_w * w
        do_ref[...] = (r * (dn - nrm * jnp.mean(dn * nrm, axis=-1, keepdims=True))).astype(BF16)

        @pl.when(pl.program_id(0) == 0)
        def _():
            gw_ref[...] = jnp.zeros_like(gw_ref)

        gw_ref[...] += gw

    return pl.pallas_call(
        body, name="gdn_out_bwd", grid=(HEADS,),
        in_specs=[pl.BlockSpec((n, DH), lambda h: (0, h)), pl.BlockSpec((n, DH), lambda h: (0, ZB + h)),
                  pl.BlockSpec((1, DH), lambda h: (0, 0)), pl.BlockSpec((n, d_model), lambda h: (0, 0)),
                  pl.BlockSpec((DH, d_model), lambda h: (h, 0))],
        out_specs=[pl.BlockSpec((n, DH), lambda h: (0, h)), pl.BlockSpec((n, DH), lambda h: (0, ZB + h)),
                   pl.BlockSpec((1, DH), lambda h: (0, 0))],
        out_shape=[jax.ShapeDtypeStruct((n, GW), BF16), jax.ShapeDtypeStruct((n, GW_COLS), BF16),
                   jax.ShapeDtypeStruct((1, DH), F32)],
        compiler_params=_params(("arbitrary",), 40 * 2**20),
    )(o, proj, wg, dout_b, w_out)


def _conv_branch(proj, w3, b, mix):
    n = proj.shape[0]

    def body(p4, w_ref, b_ref, _, y_ref):
        u = p4[:, DH:2 * DH] * p4[:, 2 * DH:3 * DH]
        cc = _conv_silu(u, w_ref, 3) + b_ref[...]
        z = p4[:, 3 * DH:4 * DH]
        y_ref[...] = (p4[:, 0:DH] * cc * (z * _sigmoid(z))).astype(BF16)

    return pl.pallas_call(
        body, name="conv_branch", grid=(HEADS,),
        in_specs=[pl.BlockSpec((n, 4 * DH), lambda h: (0, h)), pl.BlockSpec((3, DH), lambda h: (0, h)),
                  pl.BlockSpec((1, DH), lambda h: (0, h)), ANY],
        out_specs=pl.BlockSpec((n, DH), lambda h: (0, HEADS + h)),
        out_shape=jax.ShapeDtypeStruct(mix.shape, BF16),
        input_output_aliases={3: 0},
        compiler_params=_params(("parallel",), 40 * 2**20),
    )(*_in_hbm(proj, w3, b, mix))


def _conv_branch_bwd(proj, w3, b, dout_b, w_out):
    n = proj.shape[0]
    d_model = dout_b.shape[1]

    def body(p4, w_ref, b_ref, g_ref, wo_ref, o4, gw_ref, gbias_ref):
        gb, gcv, hc, z = p4[:, 0:DH], p4[:, DH:2 * DH], p4[:, 2 * DH:3 * DH], p4[:, 3 * DH:4 * DH]
        d = _dot(g_ref[...], wo_ref[...], NT)
        dgb, dgc, dhc, dzc = (o4.at[:, kk * DH:(kk + 1) * DH] for kk in range(4))
        u = gcv * hc
        cc = _conv_silu(u, w_ref, 3) + b_ref[...]
        s = _sigmoid(z)
        dzc[...] = (d * (gb * cc) * _dsilu(z, s)).astype(BF16)
        dp = d * (z * s)
        dgb[...] = (dp * cc).astype(BF16)
        dcc = dp * gb
        gbias_ref[...] = jnp.sum(dcc, axis=0, keepdims=True)
        du = None
        for j in range(3):
            gw_ref[j:j + 1, :] = jnp.sum(dcc * _shift_down(u, 2 - j), axis=0, keepdims=True)
            t = _shift_up(dcc, 2 - j) * w_ref[j:j + 1, :]
            du = t if du is None else du + t
        dgc[...] = (du * hc).astype(BF16)
        dhc[...] = (du * gcv).astype(BF16)

    p4spec = pl.BlockSpec((n, 4 * DH), lambda h: (0, h))
    return pl.pallas_call(
        body, name="conv_branch_bwd", grid=(HEADS,),
        in_specs=[p4spec, pl.BlockSpec((3, DH), lambda h: (0, h)), pl.BlockSpec((1, DH), lambda h: (0, h)),
                  pl.BlockSpec((n, d_model), lambda h: (0, 0)), pl.BlockSpec((DH, d_model), lambda h: (HEADS + h, 0))],
        out_specs=[p4spec, pl.BlockSpec((3, DH), lambda h: (0, h)), pl.BlockSpec((1, DH), lambda h: (0, h))],
        out_shape=[jax.ShapeDtypeStruct((n, CW_COLS), BF16), jax.ShapeDtypeStruct((3, GW), F32),
                   jax.ShapeDtypeStruct((1, GW), F32)],
        compiler_params=_params(("parallel",), 52 * 2**20),
    )(proj, w3, b, dout_b, w_out)


def _out_loss(mix, w_out, x, tgt, wf):
    n, d = x.shape
    kdim = mix.shape[1]
    tr = min(256, n)

    def body(m_ref, wo_ref, x_ref, t_ref, w_ref, do_ref, dob_ref, gw_ref, loss_ref):
        ov = _dot(m_ref[...], wo_ref[...], NN) + x_ref[...]
        w = w_ref[...]
        r = lax.rsqrt(jnp.mean(ov * ov, axis=-1, keepdims=True) + EPS)
        nrm = ov * r
        e = nrm * w - t_ref[...]
        dy = e * (1.0 / d)
        dn = dy * w
        dout = r * (dn - nrm * jnp.mean(dn * nrm, axis=-1, keepdims=True))
        do_ref[...] = dout
        dob_ref[...] = dout.astype(BF16)

        @pl.when(pl.program_id(0) == 0)
        def _():
            gw_ref[...] = jnp.zeros_like(gw_ref)
            loss_ref[...] = jnp.zeros_like(loss_ref)

        gw_ref[...] += jnp.sum(dy * nrm, axis=0, keepdims=True)
        loss_ref[...] += (0.5 / d) * jnp.sum(jnp.sum(e * e, axis=-1, keepdims=True), axis=0, keepdims=True)

    row = pl.BlockSpec((tr, d), lambda i: (i, 0))
    return pl.pallas_call(
        body, name="out_loss", grid=(n // tr,),
        in_specs=[pl.BlockSpec((tr, kdim), lambda i: (i, 0)), pl.BlockSpec((kdim, d), lambda i: (0, 0)), row, row,
                  pl.BlockSpec((1, d), lambda i: (0, 0))],
        out_specs=[row, row, pl.BlockSpec((1, d), lambda i: (0, 0)), pl.BlockSpec((1, 1), lambda i: (0, 0))],
        out_shape=[jax.ShapeDtypeStruct((n, d), F32), jax.ShapeDtypeStruct((n, d), BF16),
                   jax.ShapeDtypeStruct((1, d), F32), jax.ShapeDtypeStruct((1, 1), F32)],
        compiler_params=_params(("arbitrary",), 40 * 2**20),
    )(mix, w_out, x, tgt, wf)


def _dh_rms_bwd(dproj, w_t, dh0, x, w, dout, tk):
    n, d = x.shape
    kdim = dproj.shape[1]
    tm = min(1024, n)
    tk = min(tk, kdim)
    nk = kdim // tk

    def body(a_ref, b_ref, dh0_ref, x_ref, w_ref, do_ref, dx_ref, gw_ref, acc):
        i, kk = pl.program_id(0), pl.program_id(1)
        part = _dot(a_ref[...], b_ref[...], NN)

        @pl.when(kk == 0)
        def _():
            acc[...] = part + dh0_ref[...]

        @pl.when(kk > 0)
        def _():
            acc[...] += part

        @pl.when((i == 0) & (kk == 0))
        def _():
            gw_ref[...] = jnp.zeros_like(gw_ref)

        @pl.when(kk == nk - 1)
        def _():
            xv, dhv = x_ref[...], acc[...]
            r = lax.rsqrt(jnp.mean(xv * xv, axis=-1, keepdims=True) + EPS)
            xn = xv * r
            dxn = dhv * w_ref[...]
            dx_ref[...] = r * (dxn - xn * jnp.mean(dxn * xn, axis=-1, keepdims=True)) + do_ref[...]
            gw_ref[...] += jnp.sum(dhv * xn, axis=0, keepdims=True)

    row = pl.BlockSpec((tm, d), lambda i, kk: (i, 0))
    one = pl.BlockSpec((1, d), lambda i, kk: (0, 0))
    return pl.pallas_call(
        body, name="dh_rms_bwd", grid=(n // tm, nk),
        in_specs=[pl.BlockSpec((tm, tk), lambda i, kk: (i, kk)), pl.BlockSpec((tk, d), lambda i, kk: (kk, 0)),
                  row, row, one, row],
        out_specs=[row, one],
        out_shape=[jax.ShapeDtypeStruct((n, d), F32), jax.ShapeDtypeStruct((1, d), F32)],
        scratch_shapes=[pltpu.VMEM((tm, d), F32)],
        compiler_params=_params(("arbitrary", "arbitrary"), 56 * 2**20),
    )(dproj, w_t, dh0, x, w, dout)


def _ij():
    i = lax.broadcasted_iota(jnp.int32, (CH, CH), 0)
    j = lax.broadcasted_iota(jnp.int32, (CH, CH), 1)
    return i, j


def _unit_lower_inverse(mats):
    i, j = _ij()
    eye = jnp.where(i == j, 1.0, 0.0)
    same16 = (i // 16) == (j // 16)
    same32 = (i // 32) == (j // 32)
    mm = lambda xs, ys: [_dot(x, y, NN, P_INV) for x, y in zip(xs, ys)]
    n1 = [jnp.where(same16, -a, 0.0) for a in mats]
    n2 = mm(n1, n1)
    n4 = mm(n2, n2)
    n8 = mm(n4, n4)
    t = [eye + x1 + x2 + x3 for x1, x2, x3 in zip(n1, n2, mm(n1, n2))]
    t = [x + y for x, y in zip(t, mm(t, n4))]
    t = [x + y for x, y in zip(t, mm(t, n8))]
    a1 = [jnp.where(same32 & jnp.logical_not(same16), a, 0.0) for a in mats]
    t = [x - y for x, y in zip(t, mm(t, mm(a1, t)))]
    a2 = [jnp.where(same32, 0.0, a) for a in mats]
    t = [x - y for x, y in zip(t, mm(t, mm(a2, t)))]
    return t


def _head_vectors(bg, bgt, h):
    bcol = bg[:, h:h + 1]
    gcol = bg[:, HEADS + h:HEADS + h + 1]
    grow = bgt[HEADS + h:HEADS + h + 1, :]
    return bcol, gcol, grow


def _decay(gcol, grow):
    i, j = _ij()
    return jnp.where(i >= j, jnp.exp(jnp.where(i >= j, gcol - grow, 0.0)), 0.0)


def _gdn_intra(q, k, v, bg, bgt):
    n = q.shape[0]
    nch = n // CH
    cps = 4 if nch % 4 == 0 else 1

    def body(q_ref, k_ref, v_ref, bg_ref, bgt_ref, u_ref, w_ref, p_ref, t_ref):
        i, j = _ij()
        items = [(ci, h) for ci in range(cps) for h in range(HEADS)]
        at = lambda ref, ci, h: ref.at[ci * CH:(ci + 1) * CH, h * DH:(h + 1) * DH]
        bgs = [bg_ref[ci * CH:(ci + 1) * CH, :] for ci in range(cps)]
        ks = [at(k_ref, ci, h)[...] for ci, h in items]
        vecs = [_head_vectors(bgs[ci], bgt_ref[ci], h) for ci, h in items]
        decs = [_decay(gcol, grow) for _, gcol, grow in vecs]
        kks = [_dot(kh, kh, NT, P_GRAM) for kh in ks]
        qks = [_dot(at(q_ref, ci, h)[...], kh, NT, P_GRAM) for (ci, h), kh in zip(items, ks)]
        ts = _unit_lower_inverse([jnp.where(i > j, bcol * kk * dec, 0.0)
                                  for (bcol, _, _), kk, dec in zip(vecs, kks, decs)])
        us = [_dot(t, at(v_ref, ci, h)[...] * bcol, NN, P_SOL) for t, (ci, h), (bcol, _, _) in zip(ts, items, vecs)]
        ws = [_dot(t, kh * (bcol * jnp.exp(gcol)), NN, P_SOL) for t, kh, (bcol, gcol, _) in zip(ts, ks, vecs)]
        for n_, (ci, h) in enumerate(items):
            p_ref[ci, h] = qks[n_] * decs[n_]
            t_ref[ci, h] = ts[n_].astype(BF16)
            at(u_ref, ci, h)[...] = us[n_]
            at(w_ref, ci, h)[...] = ws[n_].astype(BF16)

    row = pl.BlockSpec((cps * CH, GW), lambda c: (c, 0))
    sq = pl.BlockSpec((cps, HEADS, CH, CH), lambda c: (c, 0, 0, 0))
    big = jax.ShapeDtypeStruct((n, GW), F32)
    sqs = jax.ShapeDtypeStruct((nch, HEADS, CH, CH), F32)
    return pl.pallas_call(
        body, name="gdn_intra", grid=(nch // cps,),
        in_specs=[row, row, row, pl.BlockSpec((cps * CH, DH), lambda c: (c, 0)),
                  pl.BlockSpec((cps, DH, CH), lambda c: (c, 0, 0))],
        out_specs=[row, row, sq, sq],
        out_shape=[big, jax.ShapeDtypeStruct((n, GW), BF16), sqs, jax.ShapeDtypeStruct(sqs.shape, BF16)],
        compiler_params=_params(("parallel",)),
    )(q, k, v, bg, bgt)


def _gdn_scan(q, k, bg, u, w, p):
    n = q.shape[0]
    nch = n // CH
    cps = SCAN_CPS if nch % SCAN_CPS == 0 else 1

    def body(q_ref, k_ref, bg_ref, u_ref, w_ref, p_ref, o_ref, vn_ref, s_out, s_scr):
        @pl.when(pl.program_id(0) == 0)
        def _():
            s_scr[...] = jnp.zeros_like(s_scr)

        hs = range(HEADS)
        sls = [slice(h * DH, (h + 1) * DH) for h in hs]
        ss = [s_scr[h] for h in hs]
        for ci in range(cps):
            rs = slice(ci * CH, (ci + 1) * CH)
            bg = bg_ref[rs, :]
            gcols = [bg[:, HEADS + h:HEADS + h + 1] for h in hs]
            glasts = [g[CH - 1:CH, :] for g in gcols]
            wss = [_dot(w_ref[rs, sl], s, NN, P_SCAN) for sl, s in zip(sls, ss)]
            oqs = [_dot(q_ref[rs, sl] * jnp.exp(g), s, NN, P_SCAN) for sl, s, g in zip(sls, ss, gcols)]
            vns = [u_ref[rs, sl] - x for sl, x in zip(sls, wss)]
            ops = [_dot(p_ref[ci, h], vn, NN, P_SCAN) for h, vn in zip(hs, vns)]
            sns = [_dot(k_ref[rs, sl] * jnp.exp(gl - g), vn, TN, P_SCAN)
                   for sl, gl, g, vn in zip(sls, glasts, gcols, vns)]
            for h, sl in enumerate(sls):
                s_out[ci, :, sl] = ss[h].astype(BF16)
                vn_ref[rs, sl] = vns[h].astype(BF16)
                o_ref[rs, sl] = oqs[h] + ops[h]
            ss = [s * jnp.exp(gl) + sn for s, gl, sn in zip(ss, glasts, sns)]
        for h in hs:
            s_scr[h] = ss[h]

    row = pl.BlockSpec((cps * CH, GW), lambda c: (c, 0))
    big = jax.ShapeDtypeStruct((n, GW), F32)
    return pl.pallas_call(
        body, name="gdn_scan", grid=(nch // cps,),
        in_specs=[row, row, pl.BlockSpec((cps * CH, DH), lambda c: (c, 0)), row, row,
                  pl.BlockSpec((cps, HEADS, CH, CH), lambda c: (c, 0, 0, 0))],
        out_specs=[row, row, pl.BlockSpec((cps, DH, GW), lambda c: (c, 0, 0))],
        out_shape=[big, jax.ShapeDtypeStruct((n, GW), BF16), jax.ShapeDtypeStruct((nch, DH, GW), BF16)],
        scratch_shapes=[pltpu.VMEM((HEADS, DH, DH), F32)],
        compiler_params=_params(("arbitrary",)),
    )(q, k, bg, u, w, p)


def _gdn_scan_bwd(q, k, bg, w, p, vn, s_in, do):
    n = q.shape[0]
    nch = n // CH
    cps = SCAN_CPS if nch % SCAN_CPS == 0 else 1
    rev = lambda c: nch // cps - 1 - c

    def body(q_ref, k_ref, bg_ref, w_ref, p_ref, vn_ref, s_ref, do_ref,
             dqg_ref, dp_ref, du_ref, dw_ref, dks_ref, dgam_ref, ds_scr):
        @pl.when(pl.program_id(0) == 0)
        def _():
            ds_scr[...] = jnp.zeros_like(ds_scr)

        lane = _lane((1, DH))
        hs = range(HEADS)
        sls = [slice(h * DH, (h + 1) * DH) for h in hs]
        dss = [ds_scr[h] for h in hs]
        for ci in reversed(range(cps)):
            rs = slice(ci * CH, (ci + 1) * CH)
            bg = bg_ref[rs, :]
            gcols = [bg[:, HEADS + h:HEADS + h + 1] for h in hs]
            glasts = [g[CH - 1:CH, :] for g in gcols]
            ss = [s_ref[ci, :, sl] for sl in sls]
            dos = [do_ref[rs, sl] for sl in sls]
            vnl = [vn_ref[rs, sl] for sl in sls]
            dqgs = [_dot(d, s, NT, P_SCANB) for d, s in zip(dos, ss)]
            dps = [_dot(d, vn, NT, P_SCANB) for d, vn in zip(dos, vnl)]
            dvn1 = [_dot(p_ref[ci, h], d, TN, P_SCANB) for h, d in zip(hs, dos)]
            dvn2 = [_dot(k_ref[rs, sl] * jnp.exp(gl - g), ds, NN, P_SCANB)
                    for sl, gl, g, ds in zip(sls, glasts, gcols, dss)]
            dkss = [_dot(vn, ds, NT, P_SCANB) for vn, ds in zip(vnl, dss)]
            dsq = [_dot(q_ref[rs, sl] * jnp.exp(g), d, TN, P_SCANB) for sl, g, d in zip(sls, gcols, dos)]
            dvns = [a + b for a, b in zip(dvn1, dvn2)]
            dws = [_dot(dvn, s, NT, P_SCANB) for dvn, s in zip(dvns, ss)]
            dsw = [_dot(w_ref[rs, sl], dvn, TN, P_SCANB) for sl, dvn in zip(sls, dvns)]
            dgam = jnp.zeros((1, DH), F32)
            for h, sl in enumerate(sls):
                dqg_ref[rs, sl] = dqgs[h]
                dp_ref[ci, h] = dps[h]
                du_ref[rs, sl] = dvns[h].astype(BF16)
                dw_ref[rs, sl] = (-dws[h]).astype(BF16)
                dks_ref[rs, sl] = dkss[h]
                tot = jnp.sum(jnp.sum(dss[h] * ss[h], axis=-1, keepdims=True), axis=0, keepdims=True)
                dgam = dgam + jnp.where(lane == h, tot, 0.0)
            dgam_ref[ci] = jnp.broadcast_to(dgam, (8, DH))
            dss = [ds * jnp.exp(gl) + a - b for ds, gl, a, b in zip(dss, glasts, dsq, dsw)]
        for h in hs:
            ds_scr[h] = dss[h]

    row = pl.BlockSpec((cps * CH, GW), lambda c: (rev(c), 0))
    sq =pl.BlockSpec((cps, HEADS, CH, CH), lambda c: (rev(c), 0, 0, 0))
    big = jax.ShapeDtypeStruct((n, GW), F32)
    return pl.pallas_call(
        body, name="gdn_scan_bwd", grid=(nch // cps,),
        in_specs=[row, row, pl.BlockSpec((cps * CH, DH), lambda c: (rev(c), 0)), row, sq, row,
                  pl.BlockSpec((cps, DH, GW), lambda c: (rev(c), 0, 0)), row],
        out_specs=[row, sq, row, row, row, pl.BlockSpec((cps, 8, DH), lambda c: (rev(c), 0, 0))],
        out_shape=[big, jax.ShapeDtypeStruct((nch, HEADS, CH, CH), F32), jax.ShapeDtypeStruct((n, GW), BF16),
                   jax.ShapeDtypeStruct((n, GW), BF16), big,
                   jax.ShapeDtypeStruct((nch, 8, DH), F32)],
        scratch_shapes=[pltpu.VMEM((HEADS, DH, DH), F32)],
        compiler_params=_params(("arbitrary",)),
    )(q, k, bg, w, p, vn, s_in, do)


def _gdn_intra_bwd(q, k, v, bg, bgt, t, u, w, p, dqg, dp, du, dw, dks, dgam):
    n = q.shape[0]
    nch = n // CH
    cps = 2 if nch % 2 == 0 else 1

    def body(q_ref, k_ref, v_ref, bg_ref, bgt_ref, t_ref, u_ref, w_ref, p_ref,
             dqg_ref, dp_ref, du_ref, dw_ref, dks_ref, dgam_ref, dq_ref, dk_ref, dv_ref, dbg_ref):
        i, j = _ij()
        rows1 = lax.broadcasted_iota(jnp.int32, (CH, 1), 0)
        lane = _lane((CH, DH))
        rsum = lambda x: jnp.sum(x, axis=-1, keepdims=True)
        items = [(ci, h) for ci in range(cps) for h in range(HEADS)]
        at = lambda ref, it: ref.at[it[0] * CH:(it[0] + 1) * CH, it[1] * DH:(it[1] + 1) * DH]
        ld = lambda ref: [at(ref, it)[...] for it in items]
        bgs = [bg_ref[ci * CH:(ci + 1) * CH, :] for ci in range(cps)]
        qs, ks = ld(q_ref), ld(k_ref)
        vecs = [_head_vectors(bgs[ci], bgt_ref[ci], h) for ci, h in items]
        decs = [_decay(gcol, grow) for _, gcol, grow in vecs]
        ths = [t_ref[ci, h] for ci, h in items]
        drus = [_dot(th, x_, TN, P_BWD) for th, x_ in zip(ths, ld(du_ref))]
        drws = [_dot(th, x_, TN, P_BWD) for th, x_ in zip(ths, ld(dw_ref))]
        kks = [_dot(kh, kh, NT, P_GRAM) for kh in ks]
        da1 = [_dot(dru, x_, NT, P_BWD) for dru, x_ in zip(drus, ld(u_ref))]
        da2 = [_dot(drw, x_, NT, P_BWD) for drw, x_ in zip(drws, ld(w_ref))]
        das = [jnp.where(i > j, -(x_ + y_), 0.0) for x_, y_ in zip(da1, da2)]
        dkks = [da * bcol * dec for da, (bcol, _, _), dec in zip(das, vecs, decs)]
        dps = [dp_ref[ci, h] for ci, h in items]
        dqks = [dp_ * dec for dp_, dec in zip(dps, decs)]
        dq_ps = [_dot(dqk, kh, NN, P_BWD) for dqk, kh in zip(dqks, ks)]
        dk_ps = [_dot(dqk, qh, TN, P_BWD) for dqk, qh in zip(dqks, qs)]
        dk_as = [_dot(dkk, kh, NN, P_BWD) for dkk, kh in zip(dkks, ks)]
        dk_bs = [_dot(dkk, kh, TN, P_BWD) for dkk, kh in zip(dkks, ks)]
        bcols = [vc[0] for vc in vecs]
        gcols = [vc[1] for vc in vecs]
        gams = [jnp.exp(g) for g in gcols]
        glasts = [g[CH - 1:CH, :] for g in gcols]
        es = [jnp.exp(gl - g) for gl, g in zip(glasts, gcols)]
        kgs = [kh * gam for kh, gam in zip(ks, gams)]
        dqgs, dkss = ld(dqg_ref), ld(dks_ref)
        wks = [drw * kg for drw, kg in zip(drws, kgs)]
        kss = [dk_ * (kh * e) for dk_, kh, e in zip(dkss, ks, es)]
        r_beta = [rsum(dru * x_ + wk) for dru, x_, wk in zip(drus, ld(v_ref), wks)]
        r_ak = [rsum(da * kk * dec) for da, kk, dec in zip(das, kks, decs)]
        r_gc = [rsum(wk * bcol + dqg * (qh * gam) - ks_)
                for wk, bcol, dqg, qh, gam, ks_ in zip(wks, bcols, dqgs, qs, gams, kss)]
        tk_tot = [jnp.sum(jnp.sum(ks_, axis=0, keepdims=True), axis=-1, keepdims=True) for ks_ in kss]
        mdecs = [da * (bcol * kk * dec) + dp_ * p_ref[ci, h]
                 for (ci, h), da, bcol, kk, dec, dp_ in zip(items, das, bcols, kks, decs, dps)]
        r_md = [rsum(m) for m in mdecs]
        c_md = [rsum(jnp.where(i == j, jnp.sum(m, axis=0, keepdims=True), 0.0)) for m in mdecs]
        dbgs = [jnp.zeros((CH, DH), F32) for _ in range(cps)]
        for n_, (ci, h) in enumerate(items):
            at(dv_ref, (ci, h))[...] = bcols[n_] * drus[n_]
            at(dq_ref, (ci, h))[...] = gams[n_] * dqgs[n_] + dq_ps[n_]
            at(dk_ref, (ci, h))[...] = ((bcols[n_] * gams[n_]) * drws[n_] + dk_ps[n_] + dk_as[n_] + dk_bs[n_]
                                        + dkss[n_] * es[n_])
            dbeta = r_beta[n_] + r_ak[n_]
            dglast = tk_tot[n_] + dgam_ref[ci, 0:1, h:h + 1] * jnp.exp(glasts[n_])
            dgc = r_gc[n_] + r_md[n_] - c_md[n_] + jnp.where(rows1 == CH - 1, dglast, 0.0)
            dbgs[ci] = dbgs[ci] + jnp.where(lane == h, dbeta, 0.0) + jnp.where(lane == HEADS + h, dgc, 0.0)
        for ci in range(cps):
            dbg_ref[ci * CH:(ci + 1) * CH, :] = dbgs[ci]

    row = pl.BlockSpec((cps * CH, GW), lambda c: (c, 0))
    sq = pl.BlockSpec((cps, HEADS, CH, CH), lambda c: (c, 0, 0, 0))
    small = pl.BlockSpec((cps * CH, DH), lambda c: (c, 0))
    big = jax.ShapeDtypeStruct((n, GW), F32)
    return pl.pallas_call(
        body, name="gdn_intra_bwd", grid=(nch // cps,),
        in_specs=[row, row, row, small, pl.BlockSpec((cps, DH, CH), lambda c: (c, 0, 0)), sq, row, row, sq,
                  row, sq, row, row, row, pl.BlockSpec((cps, 8, DH), lambda c: (c, 0, 0))],
        out_specs=[row, row, row, small],
        out_shape=[big, big, big, jax.ShapeDtypeStruct((n, DH), F32)],
        compiler_params=_params(("parallel",)),
    )(q, k, v, bg, bgt, t, u, w, p, dqg, dp, du, dw, dks, dgam)


def _local_step(x, tgt, h, w_g, cqw, late, norm_in_w, ad, gdn_norm_w, conv_b, final_norm_w,
                on_grad_c=None, on_grad_g=None, on_q=None):
    proj_g = _matmul(h, w_g, NT, F32, 512, 1408, 1024, "mm_proj_g", n=GW_COLS, b_outer=True)
    q, k, v = _prep_qkv(proj_g, cqw)
    if on_q is not None:
        q = on_q(q)
    bg, bgt = _prep_bg(proj_g, ad)
    u, w, p, t = _gdn_intra(q, k, v, bg, bgt)
    o, vn, s_in = _gdn_scan(q, k, bg, u, w, p)
    w_c, w_out, conv_w = late(o)
    proj_c = _matmul(h, w_c, NT, F32, 512, 1024, 1024, "mm_proj_c", n=CW_COLS, b_outer=True)
    mix = _conv_branch(proj_c, conv_w, conv_b, _gdn_out(o, proj_g, gdn_norm_w))
    dout, dout_b, g_fn, loss = _out_loss(mix, w_out, x, tgt, final_norm_w)

    g_wout = _matmul(mix, dout_b, TN, BF16, 512, 512, 2048, "mm_gwout")
    do, dproj_g, g_gn = _gdn_out_bwd(o, proj_g, gdn_norm_w, dout_b, w_out)
    dproj_c, g_cw, g_cb = _conv_branch_bwd(proj_c, conv_w, conv_b, dout_b, w_out)
    g_c = _matmul(dproj_c, h, TN, BF16, 1024, 512, 2048, "mm_gwin_c")
    if on_grad_c is not None:
        do = on_grad_c(g_c, g_wout, do)
    dqg, dp, du, dw, dks, dgam = _gdn_scan_bwd(q, k, bg, w, p, vn, s_in, do)
    dq, dk, dv, dbg = _gdn_intra_bwd(q, k, v, bg, bgt, t, u, w, p, dqg, dp, du, dw, dks, dgam)
    dproj_g, gq, gk, gv = _prep_qkv_bwd(proj_g, cqw, dq, dk, dv, dproj_g)
    dproj_g, g_al, g_dt = _prep_bg_bwd(proj_g, ad, dbg, dproj_g)
    g_g = _matmul(dproj_g, h, TN, BF16, 1408, 512, 2048, "mm_gwin_g")
    if on_grad_g is not None:
        dproj_g = on_grad_g(g_g, dproj_g)
    dh = _matmul(dproj_g, w_g, NN, F32, 1024, 1024, 1408, "mm_dh_g")
    gx, g_nin = _dh_rms_bwd(dproj_c, w_c, dh, x, norm_in_w, dout, 1024)
    small = dict(nin=g_nin, cb=g_cb, fn=g_fn, al=g_al, dt=g_dt, gn=g_gn, cq=(gq, gk, gv), cw=g_cw, loss=loss)
    return gx, small, (g_g, g_c, g_wout)


def _place():
    x, y, c = lax.axis_index("x"), lax.axis_index("y"), lax.axis_index("c")
    chips = [(1 - x, y), (x, 1 - y), (1 - x, 1 - y)]
    return x, y, c, chips


def _blk(ref, b):
    if isinstance(b, int):
        return ref.at[b * DH:(b + 1) * DH, :]
    return ref.at[pl.ds(pl.multiple_of(b * DH, DH), DH), :]


HBM = pl.BlockSpec(memory_space=pltpu.HBM)
SEM = pl.BlockSpec(memory_space=pltpu.SEMAPHORE)
EFFECT = pltpu.SideEffectType.DATAFLOW_SIDE_EFFECTING


def _split_start(name, issue, bufs, n_sems):
    nbuf = len(bufs)

    def body(*refs):
        issue(refs[:nbuf], refs[nbuf], refs[nbuf + 1])
        refs[-1][...] = jnp.zeros_like(refs[-1])

    out = pl.pallas_call(
        body, name=name,
        out_shape=(pltpu.SemaphoreType.DMA((n_sems,)), pltpu.SemaphoreType.DMA((n_sems,)),
                   *[pltpu.HBM(b.shape, b.dtype) for b in bufs], jax.ShapeDtypeStruct((8, DH), F32)),
        in_specs=[HBM] * nbuf,
        out_specs=(SEM, SEM, *[HBM] * nbuf, pl.BlockSpec(memory_space=pltpu.VMEM)),
        input_output_aliases={a: 2 + a for a in range(nbuf)},
        compiler_params=pltpu.CompilerParams(has_side_effects=EFFECT),
    )(*[pltpu.with_memory_space_constraint(b, pltpu.HBM) for b in bufs])
    return out[0], out[1], list(out[2:2 + nbuf]), out[-1]


def _split_wait(name, await_, send_sems, recv_sems, bufs, after):
    nbuf = len(bufs)
    after = list(after) if isinstance(after, (list, tuple)) else [after]

    def body(*refs):
        await_(refs[:nbuf], refs[nbuf], refs[nbuf + 1])

    out = pl.pallas_call(
        body, name=name,
        out_shape=tuple(pltpu.HBM(b.shape, b.dtype) for b in bufs),
        in_specs=[HBM] * nbuf + [SEM, SEM] + [ANY] * len(after), out_specs=tuple([HBM] * nbuf),
        input_output_aliases={a: a for a in range(nbuf)},
        compiler_params=pltpu.CompilerParams(has_side_effects=EFFECT),
    )(*bufs, send_sems, recv_sems, *after)
    return list(out)


def _phase_blocks(chip, phase, edges, parity=None):
    return [(b, blk) for b, (grp, blk) in enumerate(_shard_blocks(chip, edges))
            if grp == phase and (parity is None or b % 2 == parity)]


def _cols(ref, nblk):
    return ref.at[0:nblk * DH, :]


def _block_table(chip, edges, spare_g, spare_c):
    rows = []
    for s in range(4):
        sb = _shard_blocks(s, edges)
        rows.append([[blk if grp == "g" else spare_g for grp, blk in sb],
                     [blk if grp == "c" else spare_c for grp, blk in sb],
                     [int(grp == "g") for grp, _ in sb], [s] * ALIGNED_BLOCKS])
    return jnp.asarray(rows, jnp.int32)[chip]


def _place_own(a_shard, wo, cq, cw, bufs):
    d = a_shard.shape[1]
    chip = 2 * lax.axis_index("x") + lax.axis_index("y")

    def body(t_ref, a_ref, wo_ref, cq_ref, cw_ref, *refs):
        wg_ref, wc_ref, wog_ref, cqg_ref, cwg_ref = refs[5:]
        wg_ref[...] = a_ref[...]
        wc_ref[...] = a_ref[...]

        @pl.when(pl.program_id(0) == 0)
        def _():
            wog_ref[0] = wo_ref[...]
            cqg_ref[0] = cq_ref[...]
            cwg_ref[0] = cw_ref[...]

    whole = lambda s: pl.BlockSpec(s.shape, lambda b, t: (0,) * s.ndim)
    slot = lambda s: pl.BlockSpec((1,) + s.shape, lambda b, t: (t[3, 0],) + (0,) * s.ndim)
    return pl.pallas_call(
        body, name="place_own",
        grid_spec=pltpu.PrefetchScalarGridSpec(
            num_scalar_prefetch=1, grid=(ALIGNED_BLOCKS,),
            in_specs=[pl.BlockSpec((DH, d), lambda b, t: (b, 0)), whole(wo), whole(cq), whole(cw)] + [ANY] * 5,
            out_specs=[pl.BlockSpec((DH, d), lambda b, t: (t[0, b], 0)),
                       pl.BlockSpec((DH, d), lambda b, t: (t[1, b], 0)), slot(wo), slot(cq), slot(cw)]),
        out_shape=[jax.ShapeDtypeStruct(b.shape, b.dtype) for b in bufs],
        input_output_aliases={5 + a: a for a in range(5)},
        compiler_params=_params(("arbitrary",)),
    )(_block_table(chip, True, G_SPARE, C_SPARE), a_shard, wo, cq, cw, *bufs)


def _tie(x, token, name):
    def body(x_ref, t_ref, o_ref):
        del x_ref, t_ref, o_ref

    return pl.pallas_call(
        body, name=name, in_specs=[ANY, ANY], out_specs=ANY,
        out_shape=jax.ShapeDtypeStruct(x.shape, x.dtype), input_output_aliases={0: 0},
    )(x, token)


def _gather_start(phase, a_shard, w_grp, singles):
    ns = len(singles)

    def issue(refs, send_sems, recv_sems):
        a_ref, w_ref = refs[0], refs[1]
        x, y, c, chips = _place()
        mine = 2 * x + y
        for jj, (px, py) in enumerate(chips):
            to = dict(device_id=(px, py, c), device_id_type=MESH)
            for a in range(ns):
                pltpu.make_async_remote_copy(
                    src_ref=refs[2 + 2 * a], dst_ref=refs[3 + 2 * a].at[mine],
                    send_sem=send_sems.at[(1 + ns) * jj + 1 + a], recv_sem=recv_sems.at[(1 + ns) * jj + 1 + a],
                    **to).start()
        for s in range(4):
            for par in range(2):
                blocks = _phase_blocks(s, phase, True, par)
                if blocks:
                    @pl.when((mine == s) & (c == par))
                    def _():
                        for b, blk in blocks:
                            for jj, (px, py) in enumerate(chips):
                                pltpu.make_async_remote_copy(
                                    src_ref=_blk(a_ref, b), dst_ref=_blk(w_ref, blk),
                                    send_sem=send_sems.at[(1 + ns) * jj], recv_sem=recv_sems.at[(1 + ns) * jj],
                                    device_id=(px, py, c), device_id_type=MESH).start()

    bufs = [a_shard, w_grp] + [t for pair in singles for t in pair]
    return _split_start("gather_start_" + phase, issue, bufs, 3 * (1 + ns))


def _gather_wait(phase, send_sems, recv_sems, bufs, after):
    ns = (len(bufs) - 2) // 2

    def await_(refs, send_sems, recv_sems):
        a_ref, w_ref = refs[0], refs[1]
        x, y, c, chips = _place()
        mine = 2 * x + y
        for jj, (px, py) in enumerate(chips):
            to = dict(device_id=(px, py, c), device_id_type=MESH)
            peer = 2 * px + py
            for a in range(ns):
                cp = pltpu.make_async_remote_copy(
                    src_ref=refs[2 + 2 * a], dst_ref=refs[3 + 2 * a].at[mine],
                    send_sem=send_sems.at[(1 + ns) * jj + 1 + a], recv_sem=recv_sems.at[(1 + ns) * jj + 1 + a], **to)
                cp.wait_recv()
                cp.wait_send()
            for s in range(4):
                for par in range(2):
                    nblk = len(_phase_blocks(s, phase, True, par))
                    if nblk:
                        both = pltpu.make_async_remote_copy(
                            src_ref=_cols(a_ref, nblk), dst_ref=_cols(w_ref, nblk),
                            send_sem=send_sems.at[(1 + ns) * jj], recv_sem=recv_sems.at[(1 + ns) * jj], **to)

                        @pl.when((peer == s) & (c == par))
                        def _():
                            both.wait_recv()

                        @pl.when((mine == s) & (c == par))
                        def _():
                            both.wait_send()

    return _split_wait("gather_wait_" + phase, await_, send_sems, recv_sems, bufs, after)


def _sibling_forward_parts(phase):
    def each(w_ref, send_sems, recv_sems, start):
        x, y, c, chips = _place()
        to = dict(device_id=(x, y, 1 - c), device_id_type=MESH)
        for jj, (px, py) in enumerate(chips):
            peer = 2 * px + py
            for s in range(4):
                for par in range(2):
                    mine_blocks = _phase_blocks(s, phase, True, par)
                    theirs = len(_phase_blocks(s, phase, True, 1 - par))
                    if not (mine_blocks or theirs):
                        continue

                    @pl.when((peer == s) & (c == par))
                    def _():
                        if start:
                            for _, blk in mine_blocks:
                                pltpu.make_async_remote_copy(
                                    src_ref=_blk(w_ref, blk), dst_ref=_blk(w_ref, blk),
                                    send_sem=send_sems.at[jj], recv_sem=recv_sems.at[jj], **to).start()
                            return
                        if theirs:
                            pltpu.make_async_remote_copy(
                                src_ref=_cols(w_ref, theirs), dst_ref=_cols(w_ref, theirs),
                                send_sem=send_sems.at[jj], recv_sem=recv_sems.at[jj], **to).wait_recv()
                        if mine_blocks:
                            pltpu.make_async_remote_copy(
                                src_ref=_cols(w_ref, len(mine_blocks)), dst_ref=_cols(w_ref, len(mine_blocks)),
                                send_sem=send_sems.at[jj], recv_sem=recv_sems.at[jj], **to).wait_send()

    issue = lambda refs, send_sems, recv_sems: each(refs[0], send_sems, recv_sems, True)
    await_ = lambda refs, send_sems, recv_sems: each(refs[0], send_sems, recv_sems, False)
    return issue, await_


def _sibling_forward(phase, w_grp):
    issue, await_ = _sibling_forward_parts(phase)

    def body(w_in_ref, w_ref, send_sems, recv_sems):
        del w_in_ref
        issue([w_ref], send_sems, recv_sems)
        await_([w_ref], send_sems, recv_sems)

    return pl.pallas_call(
        body, name="sibling_forward_" + phase, in_specs=[ANY], out_specs=ANY,
        out_shape=jax.ShapeDtypeStruct(w_grp.shape, w_grp.dtype), input_output_aliases={0: 0},
        scratch_shapes=[pltpu.SemaphoreType.DMA((3,)), pltpu.SemaphoreType.DMA((3,))],
    )(w_grp)


def _merge_edges(w, edge0, mixed, name):
    d = w.shape[1]

    def body(e_ref, o_ref):
        o_ref[...] = e_ref[0:DH, :] + e_ref[DH:2 * DH, :]

    def to_block(i):
        r = mixed[-1]
        for kk in range(len(mixed) - 2, -1, -1):
            r = jnp.where(i == kk, mixed[kk], r)
        return r

    return pl.pallas_call(
        body, name=name, grid=(len(mixed),),
        in_specs=[pl.BlockSpec((2 * DH, d), lambda i: (edge0 // 2 + i, 0))],
        out_specs=pl.BlockSpec((DH, d), lambda i: (to_block(i), 0)),
        out_shape=jax.ShapeDtypeStruct(w.shape, w.dtype),
        input_output_aliases={0: 0},
        compiler_params=_params(("arbitrary",)),
    )(w)


def _scatter_start(phase, g_grp, land, singles, halved=False):
    ns = len(singles)

    def issue(refs, send_sems, recv_sems):
        g_ref, land_ref = refs[0], refs[1]
        x, y, c, chips = _place()
        for jj, (px, py) in enumerate(chips):
            to = dict(device_id=(px, py, c), device_id_type=MESH)
            peer = 2 * px + py
            for a in range(ns):
                pltpu.make_async_remote_copy(
                    src_ref=refs[2 + 2 * a].at[peer], dst_ref=refs[3 + 2 * a].at[jj],
                    send_sem=send_sems.at[(1 + ns) * jj + 1 + a], recv_sem=recv_sems.at[(1 + ns) * jj + 1 + a],
                    **to).start()
            for s in range(4):
                for par in ((0, 1) if halved else (None,)):
                    blocks = _phase_blocks(s, phase, False, par)
                    if blocks:
                        @pl.when((peer == s) if par is None else ((peer == s) & (c == par)))
                        def _():
                            for b, blk in blocks:
                                pltpu.make_async_remote_copy(
                                    src_ref=_blk(g_ref, blk), dst_ref=_blk(land_ref.at[jj], b),
                                    send_sem=send_sems.at[(1 + ns) * jj], recv_sem=recv_sems.at[(1 + ns) * jj],
                                    **to).start()

    bufs = [g_grp, land] + [t for pair in singles for t in pair]
    return _split_start("scatter_start_" + phase, issue, bufs, 3 * (1 + ns))


def _scatter_wait(phase, send_sems, recv_sems, bufs, after, halved=False):
    ns = (len(bufs) - 2) // 2

    def await_(refs, send_sems, recv_sems):
        g_ref, land_ref = refs[0], refs[1]
        x, y, c, chips = _place()
        mine = 2 * x + y
        for jj, (px, py) in enumerate(chips):
            to = dict(device_id=(px, py, c), device_id_type=MESH)
            peer = 2 * px + py
            for a in range(ns):
                cp = pltpu.make_async_remote_copy(
                    src_ref=refs[2 + 2 * a].at[peer], dst_ref=refs[3 + 2 * a].at[jj],
                    send_sem=send_sems.at[(1 + ns) * jj + 1 + a], recv_sem=recv_sems.at[(1 + ns) * jj + 1 + a], **to)
                cp.wait_recv()
                cp.wait_send()
            for s in range(4):
                for par in ((0, 1) if halved else (None,)):
                    nblk = len(_phase_blocks(s, phase, False, par))
                    if nblk:
                        both = pltpu.make_async_remote_copy(
                            src_ref=_cols(g_ref, nblk), dst_ref=_cols(land_ref.at[jj], nblk),
                            send_sem=send_sems.at[(1 + ns) * jj], recv_sem=recv_sems.at[(1 + ns) * jj], **to)

                        @pl.when((mine == s) if par is None else ((mine == s) & (c == par)))
                        def _():
                            both.wait_recv()

                        @pl.when((peer == s) if par is None else ((peer == s) & (c == par)))
                        def _():
                            both.wait_send()

    return _split_wait("scatter_wait_" + phase, await_, send_sems, recv_sems, bufs, after)


def _needed_blocks(phase, parity):
    return sorted({blk for s in range(4) for _, blk in _phase_blocks(s, phase, False, parity)})


def _pair_reduce(phase, g_grp):
    n, d = g_grp.shape

    def swap(g_ref, sib_ref, send_sem, recv_sem):
        x, y, c, _ = _place()
        to = dict(device_id=(x, y, 1 - c), device_id_type=MESH)
        for par in range(2):
            give, get = _needed_blocks(phase, 1 - par), _needed_blocks(phase, par)

            @pl.when(c == par)
            def _():
                for blk in give:
                    pltpu.make_async_remote_copy(src_ref=_blk(g_ref, blk), dst_ref=_blk(sib_ref, blk),
                                                 send_sem=send_sem, recv_sem=recv_sem, **to).start()
                pltpu.make_async_remote_copy(src_ref=_cols(g_ref, len(get)), dst_ref=_cols(sib_ref, len(get)),
                                             send_sem=send_sem, recv_sem=recv_sem, **to).wait_recv()
                pltpu.make_async_remote_copy(src_ref=_cols(g_ref, len(give)), dst_ref=_cols(sib_ref, len(give)),
                                             send_sem=send_sem, recv_sem=recv_sem, **to).wait_send()

    sib = pl.pallas_call(
        swap, name="pair_swap_" + phase, in_specs=[ANY], out_specs=ANY,
        out_shape=jax.ShapeDtypeStruct((n, d), g_grp.dtype),
        scratch_shapes=[pltpu.SemaphoreType.DMA, pltpu.SemaphoreType.DMA],
    )(*_in_hbm(g_grp))

    lists = [_needed_blocks(phase, par) for par in range(2)]
    longest = max(len(t) for t in lists)
    table = jnp.asarray([t + [t[-1]] * (longest - len(t)) for t in lists], jnp.int32)[lax.axis_index("c")]

    def add(t_ref, a_ref, b_ref, o_ref):
        o_ref[...] = (a_ref[...].astype(F32) + b_ref[...].astype(F32)).astype(o_ref.dtype)

    blk = pl.BlockSpec((DH, d), lambda i, t: (t[i], 0))
    return pl.pallas_call(
        add, name="pair_add_" + phase,
        grid_spec=pltpu.PrefetchScalarGridSpec(num_scalar_prefetch=1, grid=(longest,),
                                               in_specs=[blk, blk], out_specs=blk),
        out_shape=jax.ShapeDtypeStruct((n, d), g_grp.dtype),
        compiler_params=_params(("arbitrary",)),
    )(table, g_grp, sib)


def _sum_shard(g_g, g_c, land):
    d = g_g.shape[1]
    chip = 2 * lax.axis_index("x") + lax.axis_index("y")

    def body(t_ref, gg_ref, gc_ref, land_ref, o_ref):
        b = pl.program_id(0)
        in_g = t_ref[2, b] == 1
        own = jnp.where(in_g, gg_ref[...].astype(F32), gc_ref[...].astype(F32))
        for jj in range(3):
            own = own + land_ref[jj].astype(F32)
        o_ref[...] = jnp.where(in_g & (b % 2 != lax.axis_index("c")), 0.0, own)

    return pl.pallas_call(
        body, name="sum_w_in",
        grid_spec=pltpu.PrefetchScalarGridSpec(
            num_scalar_prefetch=1, grid=(ALIGNED_BLOCKS,),
            in_specs=[pl.BlockSpec((DH, d), lambda b, t: (t[0, b], 0)), pl.BlockSpec((DH, d), lambda b, t: (t[1, b], 0)),
                      pl.BlockSpec((3, DH, d), lambda b, t: (0, b, 0))],
            out_specs=pl.BlockSpec((DH, d), lambda b, t: (b, 0))),
        out_shape=jax.ShapeDtypeStruct((ALIGNED_W, d), F32),
        compiler_params=_params(("arbitrary",)),
    )(_block_table(chip, False, 0, 0), g_g, g_c, land)


def _sum_rows(stack, land, rows):
    _, r, d = stack.shape
    rows = min(rows, r)
    chip = 2 * lax.axis_index("x") + lax.axis_index("y")

    def body(t_ref, own_ref, land_ref, o_ref):
        acc = own_ref[0].astype(F32)
        for jj in range(3):
            acc = acc + land_ref[jj].astype(F32)
        o_ref[...] = acc

    return pl.pallas_call(
        body, name="sum_w_out",
        grid_spec=pltpu.PrefetchScalarGridSpec(
            num_scalar_prefetch=1, grid=(r // rows,),
            in_specs=[pl.BlockSpec((1, rows, d), lambda i, t: (t[0], i, 0)),
                      pl.BlockSpec((3, rows, d), lambda i, t: (0, i, 0))],
            out_specs=pl.BlockSpec((rows, d), lambda i, t: (i, 0))),
        out_shape=jax.ShapeDtypeStruct((r, d), F32),
        compiler_params=_params(("arbitrary",)),
    )(jnp.reshape(chip, (1,)).astype(jnp.int32), stack, land)


def _exchange_parts(n_swap, with_pack):
    def copies(refs, send_sems, recv_sems):
        x, y, c, _ = _place()
        me = 4 * x + 2 * y + c
        cps = [pltpu.make_async_remote_copy(
            src_ref=refs[2 * a], dst_ref=refs[2 * a + 1], send_sem=send_sems.at[a], recv_sem=recv_sems.at[a],
            device_id=(x, y, 1 - c), device_id_type=MESH) for a in range(n_swap)]
        if with_pack:
            pack_ref, packs = refs[2 * n_swap], refs[2 * n_swap + 1]
            for r in range(1, 8):
                dx, dy, dc = (r >> 2) & 1, (r >> 1) & 1, r & 1
                peer = (x + dx - 2 * x * dx, y + dy - 2 * y * dy, c + dc - 2 * c * dc)
                cps.append(pltpu.make_async_remote_copy(
                    src_ref=pack_ref, dst_ref=packs.at[me], send_sem=send_sems.at[n_swap + r - 1],
                    recv_sem=recv_sems.at[n_swap + r - 1], device_id=peer, device_id_type=MESH))
        return cps

    def issue(refs, send_sems, recv_sems):
        for cp in copies(refs, send_sems, recv_sems):
            cp.start()

    def await_(refs, send_sems, recv_sems):
        cps = copies(refs, send_sems, recv_sems)
        for cp in cps:
            cp.wait_recv()
        for cp in cps:
            cp.wait_send()

    return issue, await_, n_swap + (7 if with_pack else 0)


def _sum_packs(pack, packs):
    x, y, c = lax.axis_index("x"), lax.axis_index("y"), lax.axis_index("c")
    me = jnp.reshape(4 * x + 2 * y + c, (1,)).astype(jnp.int32)

    def body(me_ref, own_ref, p_ref, o_ref):
        acc = jnp.where(me_ref[0] == 0, own_ref[...], p_ref[0])
        for d in range(1, 8):
            acc = acc + jnp.where(me_ref[0] == d, own_ref[...], p_ref[d])
        o_ref[...] = acc

    full = lambda s: pl.BlockSpec(s.shape, lambda i, t: (0,) * s.ndim)
    return pl.pallas_call(
        body, name="sum_packs",
        grid_spec=pltpu.PrefetchScalarGridSpec(num_scalar_prefetch=1, grid=(1,), in_specs=[full(pack), full(packs)],
                                               out_specs=full(pack)),
        out_shape=jax.ShapeDtypeStruct(pack.shape, F32),
    )(me, pack, packs)


def _adamw_update(g, w_ref, m_ref, v_ref, go, do, mo, vo):
    c1 = 1.0 / (1.0 - ADAM_B1 ** ADAM_STEP)
    c2 = 1.0 / (1.0 - ADAM_B2 ** ADAM_STEP)
    mn = ADAM_B1 * m_ref[...] + (1.0 - ADAM_B1) * g
    vn = ADAM_B2 * v_ref[...] + (1.0 - ADAM_B2) * (g * g)
    go[...] = g
    mo[...] = mn
    vo[...] = vn
    do[...] = -ADAM_LR * ((mn * c1) / (jnp.sqrt(vn * c2) + ADAM_EPS) + ADAM_WD * w_ref[...])


def _adamw(w, m, v, g1, g2, rows, name):
    r, cdim = w.shape
    rows = min(rows, r)

    def body(w_ref, m_ref, v_ref, g1_ref, g2_ref, *outs):
        _adamw_update(g1_ref[...] + g2_ref[...], w_ref, m_ref, v_ref, *outs)

    blk = pl.BlockSpec((rows, cdim), lambda i: (i, 0))
    shp = jax.ShapeDtypeStruct((r, cdim), F32)
    return pl.pallas_call(
        body, name=name, grid=(r // rows,),
        in_specs=[blk] * 5, out_specs=[blk] * 4, out_shape=[shp] * 4,
        compiler_params=_params(("parallel",), 20 * rows * cdim * 4 + 8 * 2**20),
    )(*_in_hbm(w, m, v, g1, g2))


def _adamw_small(w_s, m_s, v_s, g_s):
    cq_lanes, cw_lanes = 3 * GW // 4, GW // 4
    shapes = [(1, GW), (1, 4, cq_lanes), (1, HEADS), (1, HEADS), (1, DH), (1, 3, cw_lanes), (1, GW), (1, GW)]

    def body(w_ref, m_ref, v_ref, g_ref, *refs):
        kinds = refs[len(refs) - 4:]
        _adamw_update(g_ref[...], w_ref, m_ref, v_ref, *kinds)
        for kk, a in enumerate(kinds):
            nin, cq, al, dt, gn, cw, cb, fn = refs[8 * kk:8 * kk + 8]
            nin[...] = a[S_NIN:S_NIN + 1, :]
            cq[0] = a[S_CQ:S_CQ + 4, 0:cq_lanes]
            al[...] = a[S_AD:S_AD + 1, 0:HEADS]
            dt[...] = a[S_AD:S_AD + 1, HEADS:2 * HEADS]
            gn[...] = a[S_GN:S_GN + 1, 0:DH]
            cw[0] = a[S_CW:S_CW + 3, 0:cw_lanes]
            cb[...] = a[S_CB:S_CB + 1, :]
            fn[...] = a[S_FN:S_FN + 1, :]

    out = pl.pallas_call(
        body, name="adamw_small",
        out_shape=[jax.ShapeDtypeStruct(s, F32) for s in shapes] * 4,
        scratch_shapes=[pltpu.VMEM(w_s.shape, F32)] * 4,
    )(w_s, m_s, v_s, g_s)
    return [out[8 * kk:8 * kk + 8] for kk in range(4)]


def _adamw_shard(wt, mt, vt, g1, g2):
    r, d = wt.shape
    cols = min(256, d)

    def body(w_ref, m_ref, v_ref, g_ref, g2_ref, go, do, mo, vo, pad_ref):
        chip = 2 * lax.axis_index("x") + lax.axis_index("y")
        back = [(ALIGNED_W - s) % ALIGNED_W for s in SHIFTS]
        pad_ref[...] = pltpu.roll(g_ref[...] + g2_ref[...], _by_chip(chip, back), 0)
        outs = [o.at[:, 0, :] for o in (go, do, mo, vo)]
        _adamw_update(pad_ref[0:r, :], w_ref, m_ref, v_ref, *outs)

    blk = pl.BlockSpec((r, cols), lambda i: (0, i))
    gblk = pl.BlockSpec((ALIGNED_W, cols), lambda i: (0, i))
    oblk = pl.BlockSpec((r, 1, cols), lambda i: (0, 0, i))
    shp = jax.ShapeDtypeStruct((r, 1, d), F32)
    return pl.pallas_call(
        body, name="adamw_w_in", grid=(d // cols,),
        in_specs=[blk] * 3 + [gblk] * 2, out_specs=[oblk] * 4, out_shape=[shp] * 4,
        scratch_shapes=[pltpu.VMEM((ALIGNED_W, cols), F32)],
        compiler_params=_params(("parallel",), 24 * ALIGNED_W * cols * 4 + 8 * 2**20),
    )(wt, mt, vt, g1, g2)


def _pad_lanes(a, width):
    return jnp.pad(a, ((0, 0), (0, width - a.shape[1])))


def _gathered_to_full(g):
    return jnp.transpose(g, (1, 0, 2)).reshape(g.shape[1], 4 * g.shape[2])


def _row(a):
    return _pad_lanes(a.reshape(1, -1), 1024)


S_NIN, S_CB, S_FN, S_AD, S_GN, S_CQ, S_CW = 0, 1, 2, 3, 4, 5, 9


def _small_pack(nin, cb, fn, al, dt, gn, cqw_shard, cw_shard):
    ad = jnp.concatenate([al.reshape(1, -1), dt.reshape(1, -1)], axis=1)
    rows = [_row(nin), _row(cb), _row(fn), _row(ad), _row(gn), _pad_lanes(cqw_shard, 1024),
            _pad_lanes(cw_shard, 1024)]
    out = jnp.concatenate(rows, axis=0)
    return jnp.pad(out, ((0, 16 - out.shape[0]), (0, 0)))


def kernel(x, norm_in_w, w_in, conv_qkv_w, A_log, dt_bias, gdn_norm_w, conv_w, conv_b, w_out, final_norm_w, loss_target, m_norm_in_w, m_w_in, m_conv_qkv_w, m_A_log, m_dt_bias, m_gdn_norm_w, m_conv_w, m_conv_b, m_w_out, m_final_norm_w, v_norm_in_w, v_w_in, v_conv_qkv_w, v_A_log, v_dt_bias, v_gdn_norm_w, v_conv_w, v_conv_b, v_w_out, v_final_norm_w):
    chip = 2 * lax.axis_index("x") + lax.axis_index("y")
    a_shard = _align_shard(jnp.transpose(w_in, (2, 0, 1)))
    d_model = x.shape[-1]
    stack = lambda s: lax.empty((4,) + s.shape, s.dtype)
    wg0 = lax.empty((WG_BLOCKS * DH, d_model), BF16)
    wc0 = lax.empty((WC_BLOCKS * DH, d_model), BF16)
    ss_g, rs_g, bufs_g, tok_g = _gather_start("g", a_shard, wg0, [(conv_qkv_w[0], stack(conv_qkv_w[0]))])
    wo_b = _cast_bf16(w_out[0], 256, "cast_w_out", tok_g)
    ss_c, rs_c, bufs_c, tok_c = _gather_start("c", bufs_g[0], wc0,
                                              [(conv_w[0], stack(conv_w[0])), (wo_b, stack(wo_b))])
    wg1, wc1, wog1, cqg1, cwg1 = _place_own(bufs_c[0], bufs_c[4], bufs_g[2], bufs_c[2],
                                            [bufs_g[1], bufs_c[1], bufs_c[5], bufs_g[3], bufs_c[3]])
    x0 = x[0]
    h = _rms_in(x0, _tie(_tie(norm_in_w, tok_g, "after_gather_start_g"), tok_c, "after_gather_start_c"))
    adam_in = [jnp.transpose(a[0]) for a in (w_in, m_w_in, v_w_in)]
    sp = lambda nin, cb, fn, al, dt, gn, cq, cwv: _small_pack(nin, cb, fn, al, dt, gn, cq[0], cwv[0])
    w_s = sp(norm_in_w, conv_b, final_norm_w, A_log, dt_bias, gdn_norm_w, conv_qkv_w, conv_w)
    m_s = sp(m_norm_in_w, m_conv_b, m_final_norm_w, m_A_log, m_dt_bias, m_gdn_norm_w, m_conv_qkv_w, m_conv_w)
    v_s = sp(v_norm_in_w, v_conv_b, v_final_norm_w, v_A_log, v_dt_bias, v_gdn_norm_w, v_conv_qkv_w, v_conv_w)
    a_thru, wg, _, cq_g = _gather_wait("g", ss_g, rs_g, [bufs_c[0], wg1, bufs_g[2], cqg1],
                                       [h, w_s, m_s, v_s] + adam_in[1:])
    w_g = _merge_edges(_sibling_forward("g", wg), G_EDGE, G_MIXED, "merge_edges_g")
    cqw = _gathered_to_full(cq_g)
    ad = jnp.pad(jnp.concatenate([A_log, dt_bias], axis=0), ((0, 0), (A_LANE, 0)))
    fwd_c = {}

    def on_q(q):
        _, wc, _, cw_g, _, wo_g = _gather_wait("c", ss_c, rs_c,
                                               [a_thru, wc1, bufs_c[2], cwg1, bufs_c[4], wog1], q)
        issue, _ = _sibling_forward_parts("c")
        ss, rs, (wc,), tok = _split_start("sibling_forward_start_c", issue, [wc], 3)
        fwd_c.update(ss=ss, rs=rs, wc=wc, cw_g=cw_g, wo_g=wo_g)
        return _tie(q, tok, "after_sibling_forward_start_c")

    def late(o):
        _, await_ = _sibling_forward_parts("c")
        (wc,) = _split_wait("sibling_forward_wait_c", await_, fwd_c["ss"], fwd_c["rs"], [fwd_c["wc"]], o)
        return (_merge_edges(wc, C_EDGE, C_MIXED, "merge_edges_c"), fwd_c["wo_g"].reshape(2 * GW, d_model),
                _gathered_to_full(fwd_c["cw_g"]))

    scat = {}

    def on_grad_c(g_c, g_wout, do):
        go4 = g_wout.reshape(4, GW // 2, d_model)
        land = lax.empty((3, ALIGNED_W, d_model), BF16)
        land_o = lax.empty((3, GW // 2, d_model), BF16)
        ss, rs, bufs, tok = _scatter_start("c", g_c, land, [(go4, land_o)])
        scat["c"] = (ss, rs, bufs)
        return _tie(do, tok, "after_scatter_start_c")

    def on_grad_g(g_g, dproj_g):
        ss, rs, bufs, tok = _scatter_start("g", _pair_reduce("g", g_g), scat["c"][2][1], [], halved=True)
        scat["g"] = (ss, rs, bufs)
        return _tie(dproj_g, tok, "after_scatter_start_g")

    gx, sm, _ = _local_step(x0, loss_target[0], h, w_g, cqw, late, norm_in_w, ad, gdn_norm_w, conv_b,
                            final_norm_w.reshape(1, -1), on_grad_c, on_grad_g, on_q)

    ss, rs, bufs = scat["c"]
    g_c, land, go4, land_o = _scatter_wait("c", ss, rs, [bufs[0], scat["g"][2][1], bufs[2], bufs[3]], gx)
    part_out = _sum_rows(go4, land_o, 128)
    ad_g = jnp.concatenate([sm["al"][:, A_LANE:], sm["dt"][:, A_LANE:]], axis=1)
    pack = jnp.concatenate([_row(sm["nin"]), _row(sm["cb"]), _row(sm["fn"]), _row(ad_g), _row(sm["gn"]),
                            jnp.concatenate(sm["cq"], axis=1).reshape(12, 1024), sm["cw"], _row(sm["loss"])], axis=0)
    pack = jnp.pad(pack, ((0, PACK_ROWS - pack.shape[0]), (0, 0)))
    issue, await_a, nsem = _exchange_parts(1, True)
    ss_a, rs_a, bufs_a, tok_a = _split_start(
        "exchange_start_small", issue,
        [part_out, lax.empty(part_out.shape, F32), pack, lax.empty((8,) + pack.shape, F32)], nsem)
    ss, rs, bufs = scat["g"]
    g_g, land = _scatter_wait("g", ss, rs, [bufs[0], land], [gx, tok_a], halved=True)
    part_in = _sum_shard(g_g, g_c, land)
    issue, await_b, nsem = _exchange_parts(1, False)
    ss_b, rs_b, bufs_b, tok_b = _split_start("exchange_start_w_in", issue,
                                             [part_in, lax.empty(part_in.shape, F32)], nsem)
    part_out, sib_out, pack, packs = _split_wait("exchange_wait_small", await_a, ss_a, rs_a, bufs_a, tok_b)
    tot = _sum_packs(pack, packs)
    g_wo, d_wo, m_wo, v_wo = _adamw(w_out[0], m_w_out[0], v_w_out[0], part_out, sib_out, 128, "adamw_w_out")
    g_cq_sh = lax.dynamic_slice_in_dim(tot[R_CQ:R_CQ + 12].reshape(4, 3 * GW), chip * 768, 768, axis=1)
    g_cw_sh = lax.dynamic_slice_in_dim(tot[R_CW:R_CW + 3], chip * 256, 256, axis=1)
    g_s = _small_pack(tot[R_NIN], tot[R_CB], tot[R_FN], tot[R_AD, :HEADS], tot[R_AD, HEADS:2 * HEADS],
                      tot[R_GN, :DH], g_cq_sh, g_cw_sh)
    small = _adamw_small(w_s, m_s, v_s, g_s)
    part_in, sib_in = _split_wait("exchange_wait_w_in", await_b, ss_b, rs_b, bufs_b, [small[0][0], d_wo])
    g_wi, d_wi, m_wi, v_wi = [jnp.transpose(a, (1, 2, 0))[0] for a in _adamw_shard(*adam_in, part_in, sib_in)]

    def unpack(leaves, big_in, big_out):
        nin, cq, al, dt, gn, cw, cb, fn = leaves
        return (nin, big_in[None], cq, al, dt, gn, cw, cb, big_out[None], fn.reshape(-1))

    loss = tot[R_LOSS, 0]
    return (loss, gx[None], *unpack(small[0], g_wi, g_wo), *unpack(small[1], d_wi, d_wo),
            *unpack(small[2], m_wi, m_wo), *unpack(small[3], v_wi, v_wo))
```

```python
import jax
import jax.numpy as jnp
from jax import lax
from jax.experimental import pallas as pl
from jax.experimental.pallas import tpu as pltpu

F32 = jnp.float32
BF16 = jnp.bfloat16
MESH = pl.DeviceIdType.MESH
ANY = pl.BlockSpec(memory_space=pl.ANY)

HEADS = 8
DH = 128
CH = 64
GW = HEADS * DH
EPS = 1e-6
VMEM_V7X = 64 * 1024 * 1024

QB, KB, VB, ZB, BAB = 0, 8, 16, 24, 32
A_LANE = 120
NG, NC = 33, 32
GW_COLS, CW_COLS = NG * DH, NC * DH

SHARD_W = 2052
ALIGNED_BLOCKS = 17
ALIGNED_W = ALIGNED_BLOCKS * DH
SHIFTS = (0, 4, ALIGNED_W - 8, ALIGNED_W - 4)
G_EDGE, C_EDGE = 34, 32
G_SPARE, C_SPARE = 33, 34
WG_BLOCKS, WC_BLOCKS = 38, 36
G_MIXED, C_MIXED = (2, BAB), (4 * 7 + 1,)


def _shard_blocks(chip, edges):
    g, c = "g", "c"
    if chip == 0:
        out = [(g, 3 * b) for b in range(8)] + [(g, 3 * b + 1) for b in range(8)] + [(g, G_EDGE, G_MIXED[0])]
    elif chip == 1:
        out = [(g, G_EDGE + 1, G_MIXED[0])] + [(g, 3 * b + 2) for b in range(1, 8)]
        out += [(g, ZB + b) for b in range(8)] + [(g, G_EDGE + 2, G_MIXED[1])]
    elif chip == 2:
        out = [(c, 4 * b) for b in range(8)] + [(c, 4 * b + 1) for b in range(7)]
        out += [(c, C_EDGE, C_MIXED[0]), (g, G_EDGE + 3, G_MIXED[1])]
    else:
        out = [(c, 4 * b + 2) for b in range(8)] + [(c, 4 * b + 3) for b in range(8)] + [(c, C_EDGE + 1, C_MIXED[0])]
    return [(o[0], o[1] if (edges or len(o) == 2) else o[2]) for o in out]


def _by_chip(chip, vals):
    if all(v == vals[0] for v in vals):
        return vals[0]
    r = vals[3]
    for kk in (2, 1, 0):
        r = jnp.where(chip == kk, vals[kk], r)
    return r

ADAM_LR, ADAM_B1, ADAM_B2, ADAM_EPS, ADAM_WD, ADAM_STEP = 0.001, 0.9, 0.999, 1e-08, 0.01, 10

R_NIN, R_CB, R_FN, R_AD, R_GN, R_CQ, R_CW, R_LOSS, PACK_ROWS = 0, 1, 2, 3, 4, 5, 17, 20, 24

NN = ((1,), (0,))
NT = ((1,), (1,))
TN = ((0,), (0,))


def _dot(a, b, dims=NN, mode="lo"):
    dn = (dims, ((), ()))
    if mode == "hi":
        return lax.dot_general(a, b, dn, precision=lax.Precision.HIGHEST, preferred_element_type=F32)
    ah, bh = a.astype(BF16), b.astype(BF16)
    out = lax.dot_general(ah, bh, dn, preferred_element_type=F32)
    if mode == "x3":
        al = (a - ah.astype(F32)).astype(BF16)
        bl = (b - bh.astype(F32)).astype(BF16)
        out = out + lax.dot_general(ah, bl, dn, preferred_element_type=F32)
        out = out + lax.dot_general(al, bh, dn, preferred_element_type=F32)
    return out


P_GRAM, P_INV, P_SOL, P_SCAN, P_SCANB, P_BWD = "lo", "lo", "lo", "lo", "lo", "lo"
P_CUM = "x3"


def _params(sem=None, vmem=None):
    kw = {}
    if sem is not None:
        kw["dimension_semantics"] = sem
    if vmem is not None:
        kw["vmem_limit_bytes"] = int(min(max(vmem, 32 * 2**20), VMEM_V7X - 8 * 2**20))
    return pltpu.CompilerParams(**kw)


def _in_hbm(*arrays):
    return [pltpu.with_memory_space_constraint(a, pltpu.HBM) for a in arrays]


def _sigmoid(x):
    return 1.0 / (1.0 + jnp.exp(-x))


def _dsilu(x, s):
    return s * (1.0 + x * (1.0 - s))


def _rows(shape):
    return lax.broadcasted_iota(jnp.int32, shape, 0)


def _shift_down(x, s):
    if s == 0:
        return x
    return jnp.where(_rows(x.shape) >= s, pltpu.roll(x, s, 0), 0.0)


def _shift_up(x, s):
    if s == 0:
        return x
    n = x.shape[0]
    return jnp.where(_rows(x.shape) < n - s, pltpu.roll(x, n - s, 0), 0.0)


def _matmul(a, b, dims, out_dtype, tm, tn, tk, name, add=None, n=None, b_outer=False):
    if dims == NN:
        (m, k), n = a.shape, b.shape[1]
    elif dims == NT:
        (m, k), n = a.shape, (n or b.shape[0])
    else:
        (k, m), n = a.shape, b.shape[1]
    tm, tn, tk = min(tm, m), min(tn, n), min(tk, k)
    assert m % tm == 0 and n % tn == 0 and k % tk == 0, (name, m, n, k, tm, tn, tk)
    nk = k // tk

    def body(*refs):
        if add is None:
            a_ref, b_ref, o_ref = refs[:3]
            add_ref = None
        else:
            a_ref, b_ref, add_ref, o_ref = refs[:4]
        part = _dot(a_ref[...], b_ref[...], dims)
        if nk == 1:
            if add_ref is not None:
                part = part + add_ref[...]
            o_ref[...] = part.astype(out_dtype)
            return
        acc = refs[-1]
        kk = pl.program_id(2)

        @pl.when(kk == 0)
        def _():
            acc[...] = part

        @pl.when(kk > 0)
        def _():
            acc[...] += part

        @pl.when(kk == nk - 1)
        def _():
            r = acc[...]
            if add_ref is not None:
                r = r + add_ref[...]
            o_ref[...] = r.astype(out_dtype)

    ij = (lambda g0, g1: (g1, g0)) if b_outer else (lambda g0, g1: (g0, g1))

    def spec(shape, pick):
        return pl.BlockSpec(shape, lambda g0, g1, kk: pick(*ij(g0, g1), kk))

    a_spec = spec((tk, tm), lambda i, j, kk: (kk, i)) if dims == TN else spec((tm, tk), lambda i, j, kk: (i, kk))
    b_spec = spec((tn, tk), lambda i, j, kk: (j, kk)) if dims == NT else spec((tk, tn), lambda i, j, kk: (kk, j))
    o_spec = spec((tm, tn), lambda i, j, kk: (i, j))
    in_specs = [a_spec, b_spec]
    args = [a, b]
    if add is not None:
        in_specs.append(o_spec)
        args.append(add)
    osz = jnp.dtype(out_dtype).itemsize
    est = 2 * (tm * tk * a.dtype.itemsize + tk * tn * b.dtype.itemsize + tm * tn * osz)
    est += 3 * tm * tn * 4 + (2 * tm * tn * 4 if add is not None else 0)
    return pl.pallas_call(
        body, name=name, grid=(n // tn, m // tm, nk) if b_outer else (m // tm, n // tn, nk),
        in_specs=in_specs, out_specs=o_spec,
        out_shape=jax.ShapeDtypeStruct((m, n), out_dtype),
        scratch_shapes=[pltpu.VMEM((tm, tn), F32)] if nk > 1 else [],
        compiler_params=_params(("parallel", "parallel", "arbitrary"), est + 8 * 2**20),
    )(*args)


def _cast_bf16(a, rows, name, after):
    r, c = a.shape
    rows = min(rows, r)

    def body(a_ref, t_ref, o_ref):
        del t_ref
        o_ref[...] = a_ref[...].astype(BF16)

    return pl.pallas_call(
        body, name=name, grid=(r // rows,),
        in_specs=[pl.BlockSpec((rows, c), lambda i: (i, 0)), ANY],
        out_specs=pl.BlockSpec((rows, c), lambda i: (i, 0)),
        out_shape=jax.ShapeDtypeStruct((r, c), BF16),
        compiler_params=_params(("parallel",)),
    )(a, after)


def _align_shard(wt):
    r, _, d = wt.shape
    cols = min(256, d)

    def body(w_ref, o_ref, pad_ref):
        chip = 2 * lax.axis_index("x") + lax.axis_index("y")
        pad_ref[...] = jnp.zeros_like(pad_ref)
        pad_ref[0:r, :] = w_ref[:, 0, :]
        o_ref[...] = pltpu.roll(pad_ref[...], _by_chip(chip, SHIFTS), 0).astype(BF16)

    return pl.pallas_call(
        body, name="align_shard", grid=(d // cols,),
        in_specs=[pl.BlockSpec((r, 1, cols), lambda i: (0, 0, i))],
        out_specs=pl.BlockSpec((ALIGNED_W, cols), lambda i: (0, i)),
        out_shape=jax.ShapeDtypeStruct((ALIGNED_W, d), BF16),
        scratch_shapes=[pltpu.VMEM((ALIGNED_W, cols), F32)],
        compiler_params=_params(("parallel",)),
    )(wt)


def _rms_in(x, w):
    n, d = x.shape
    tr = min(256, n)

    def body(x_ref, w_ref, h_ref):
        xv = x_ref[...]
        r = lax.rsqrt(jnp.mean(xv * xv, axis=-1, keepdims=True) + EPS)
        h_ref[...] = (xv * r * w_ref[...]).astype(BF16)

    return pl.pallas_call(
        body, name="rms_in", grid=(n // tr,),
        in_specs=[pl.BlockSpec((tr, d), lambda i: (i, 0)), pl.BlockSpec((1, d), lambda i: (0, 0))],
        out_specs=pl.BlockSpec((tr, d), lambda i: (i, 0)),
        out_shape=jax.ShapeDtypeStruct((n, d), BF16),
        compiler_params=_params(("parallel",)),
    )(x, w)


def _conv_silu(p, w_ref, taps):
    c = None
    for j in range(taps):
        t = _shift_down(p, taps - 1 - j) * w_ref[j:j + 1, :]
        c = t if c is None else c + t
    return c


def _prep_qkv(proj, cw):
    n = proj.shape[0]

    def body(p3, wq, wk, wv, q_ref, k_ref, v_ref):
        for kind, (w_ref, o_ref) in enumerate(((wq, q_ref), (wk, k_ref), (wv, v_ref))):
            c = _conv_silu(p3[:, kind * DH:(kind + 1) * DH], w_ref, 4)
            a = c * _sigmoid(c)
            if kind < 2:
                r = lax.rsqrt(jnp.sum(a * a, axis=-1, keepdims=True) + EPS)
                a = a * (r * (DH ** -0.5 if kind == 0 else 1.0))
            o_ref[...] = a

    col = pl.BlockSpec((n, DH), lambda h: (0, h))
    wcol = lambda base: pl.BlockSpec((4, DH), lambda h: (0, base + h))
    out = jax.ShapeDtypeStruct((n, GW), F32)
    return pl.pallas_call(
        body, name="prep_qkv", grid=(HEADS,),
        in_specs=[pl.BlockSpec((n, 3 * DH), lambda h: (0, h)), wcol(QB), wcol(KB), wcol(VB)],
        out_specs=[col] * 3, out_shape=[out] * 3,
        compiler_params=_params(("parallel",), 40 * 2**20),
    )(proj, cw, cw, cw)


def _prep_qkv_bwd(proj, cw, dq, dk, dv, dproj):
    n = proj.shape[0]

    def body(p3, wq, wk, wv, dq_ref, dk_ref, dv_ref, _, o3, gq, gk, gv):
        for kind, (w_ref, d_ref, g_ref) in enumerate(((wq, dq_ref, gq), (wk, dk_ref, gk), (wv, dv_ref, gv))):
            p = p3[:, kind * DH:(kind + 1) * DH]
            shifted = [_shift_down(p, 3 - j) for j in range(4)]
            c = shifted[0] * w_ref[0:1, :]
            for j in range(1, 4):
                c = c + shifted[j] * w_ref[j:j + 1, :]
            s = _sigmoid(c)
            a = c * s
            d = d_ref[...]
            if kind < 2:
                r = lax.rsqrt(jnp.sum(a * a, axis=-1, keepdims=True) + EPS)
                sc = DH ** -0.5 if kind == 0 else 1.0
                d = (sc * r) * (d - a * ((r * r) * jnp.sum(d * a, axis=-1, keepdims=True)))
            dc = d * _dsilu(c, s)
            dp = None
            for j in range(4):
                g_ref[j:j + 1, :] = jnp.sum(dc * shifted[j], axis=0, keepdims=True)
                t = _shift_up(dc, 3 - j) * w_ref[j:j + 1, :]
                dp = t if dp is None else dp + t
            o3[:, kind * DH:(kind + 1) * DH] = dp.astype(BF16)

    col = pl.BlockSpec((n, DH), lambda h: (0, h))
    wcol = lambda base: pl.BlockSpec((4, DH), lambda h: (0, base + h))
    p3spec = pl.BlockSpec((n, 3 * DH), lambda h: (0, h))
    return pl.pallas_call(
        body, name="prep_qkv_bwd", grid=(HEADS,),
        in_specs=[p3spec, wcol(QB), wcol(KB), wcol(VB), col, col, col, ANY],
        out_specs=[p3spec] + [wcol(0)] * 3,
        out_shape=[jax.ShapeDtypeStruct(dproj.shape, BF16)] + [jax.ShapeDtypeStruct((4, GW), F32)] * 3,
        input_output_aliases={7: 0},
        compiler_params=_params(("parallel",), 48 * 2**20),
    )(proj, cw, cw, cw, dq, dk, dv, dproj)


CPB = 8
SCAN_CPS = 4


def _tri(lower, rows):
    i = lax.broadcasted_iota(jnp.int32, (rows, rows), 0)
    j = lax.broadcasted_iota(jnp.int32, (rows, rows), 1)
    return jnp.where((i // CH == j // CH) & ((i >= j) if lower else (j >= i)), 1.0, 0.0)


def _lane(shape):
    return lax.broadcasted_iota(jnp.int32, shape, 1)


def _prep_bg(proj, ad):
    n = proj.shape[0]
    nch = n // CH
    cpb = CPB if nch % CPB == 0 else 1
    rows = cpb * CH

    def body(p_ref, ad_ref, bg_ref, bgt_ref):
        p = p_ref[...]
        lane = _lane(p.shape)
        beta = _sigmoid(p)
        xa = p + ad_ref[1:2, :]
        sp = jnp.maximum(xa, 0.0) + jnp.log(1.0 + jnp.exp(-jnp.abs(xa)))
        g = pltpu.roll(-jnp.exp(ad_ref[0:1, :]) * sp, DH - A_LANE + HEADS, 1)
        gc = _dot(_tri(True, rows), g, NN, P_CUM)
        bg = jnp.where(lane < HEADS, beta, jnp.where(lane < 2 * HEADS, gc, 0.0))
        bg_ref[...] = bg
        for ci in range(cpb):
            bgt_ref[ci] = bg[ci * CH:(ci + 1) * CH, :].T

    return pl.pallas_call(
        body, name="prep_bg", grid=(nch // cpb,),
        in_specs=[pl.BlockSpec((rows, DH), lambda i: (i, BAB)), pl.BlockSpec((2, DH), lambda i: (0, 0))],
        out_specs=[pl.BlockSpec((rows, DH), lambda i: (i, 0)), pl.BlockSpec((cpb, DH, CH), lambda i: (i, 0, 0))],
        out_shape=[jax.ShapeDtypeStruct((n, DH), F32), jax.ShapeDtypeStruct((nch, DH, CH), F32)],
        compiler_params=_params(("parallel",)),
    )(*_in_hbm(proj, ad))


def _prep_bg_bwd(proj, ad, dbg, dproj):
    n = proj.shape[0]
    nch = n // CH
    cpb = CPB if nch % CPB == 0 else 1
    rows = cpb * CH

    def body(p_ref, ad_ref, d_ref, _, o_ref, ga_ref, gd_ref):
        p = p_ref[...]
        d = d_ref[...]
        lane = _lane(p.shape)
        beta = _sigmoid(p)
        xa = p + ad_ref[1:2, :]
        sp = jnp.maximum(xa, 0.0) + jnp.log(1.0 + jnp.exp(-jnp.abs(xa)))
        na = -jnp.exp(ad_ref[0:1, :])
        dg = pltpu.roll(_dot(_tri(False, rows), d, NN, P_CUM), A_LANE - HEADS, 1)
        da = dg * na * _sigmoid(xa)
        is_g = lane >= A_LANE
        o_ref[...] = jnp.where(lane < HEADS, d * beta * (1.0 - beta), jnp.where(is_g, da, 0.0)).astype(BF16)
        ga = jnp.sum(jnp.where(is_g, dg * na * sp, 0.0), axis=0, keepdims=True)
        gd = jnp.sum(jnp.where(is_g, da, 0.0), axis=0, keepdims=True)

        @pl.when(pl.program_id(0) == 0)
        def _():
            ga_ref[...] = jnp.zeros_like(ga_ref)
            gd_ref[...] = jnp.zeros_like(gd_ref)

        ga_ref[...] += ga
        gd_ref[...] += gd

    one = pl.BlockSpec((1, DH), lambda i: (0, 0))
    return pl.pallas_call(
        body, name="prep_bg_bwd", grid=(nch // cpb,),
        in_specs=[pl.BlockSpec((rows, DH), lambda i: (i, BAB)), pl.BlockSpec((2, DH), lambda i: (0, 0)),
                  pl.BlockSpec((rows, DH), lambda i: (i, 0)), ANY],
        out_specs=[pl.BlockSpec((rows, DH), lambda i: (i, BAB)), one, one],
        out_shape=[jax.ShapeDtypeStruct(dproj.shape, BF16), jax.ShapeDtypeStruct((1, DH), F32),
                   jax.ShapeDtypeStruct((1, DH), F32)],
        input_output_aliases={3: 0},
        compiler_params=_params(("arbitrary",)),
    )(proj, ad, dbg, dproj)


def _gdn_out(o, proj, wg):
    n = o.shape[0]

    def body(o_ref, z_ref, w_ref, y_ref):
        ov, z = o_ref[...], z_ref[...]
        r = lax.rsqrt(jnp.mean(ov * ov, axis=-1, keepdims=True) + EPS)
        y_ref[...] = (ov * r * w_ref[...] * (z * _sigmoid(z))).astype(BF16)

    return pl.pallas_call(
        body, name="gdn_out", grid=(HEADS,),
        in_specs=[pl.BlockSpec((n, DH), lambda h: (0, h)), pl.BlockSpec((n, DH), lambda h: (0, ZB + h)),
                  pl.BlockSpec((1, DH), lambda h: (0, 0))],
        out_specs=pl.BlockSpec((n, DH), lambda h: (0, h)),
        out_shape=jax.ShapeDtypeStruct((n, 2 * GW), BF16),
        compiler_params=_params(("parallel",)),
    )(o, proj, wg)


def _gdn_out_bwd(o, proj, wg, dout_b, w_out):
    n = o.shape[0]
    d_model = dout_b.shape[1]

    def body(o_ref, z_ref, w_ref, g_ref, wo_ref, do_ref, dz_ref, gw_ref):
        ov, z, w = o_ref[...], z_ref[...], w_ref[...]
        d = _dot(g_ref[...], wo_ref[...], NT)
        r = lax.rsqrt(jnp.mean(ov * ov, axis=-1, keepdims=True) + EPS)
        nrm = ov * r
        s = _sigmoid(z)
        dz_ref[...] = (d * (nrm * w) * _dsilu(z, s)).astype(BF16)
        dn_w = d * (z * s)
        gw = jnp.sum(dn_w * nrm, axis=0, keepdims=True)
        dn = dn_w * w
        do_ref[...] = (r * (dn - nrm * jnp.mean(dn * nrm, axis=-1, keepdims=True))).astype(BF16)

        @pl.when(pl.program_id(0) == 0)
        def _():
            gw_ref[...] = jnp.zeros_like(gw_ref)

        gw_ref[...] += gw

    return pl.pallas_call(
        body, name="gdn_out_bwd", grid=(HEADS,),
        in_specs=[pl.BlockSpec((n, DH), lambda h: (0, h)), pl.BlockSpec((n, DH), lambda h: (0, ZB + h)),
                  pl.BlockSpec((1, DH), lambda h: (0, 0)), pl.BlockSpec((n, d_model), lambda h: (0, 0)),
                  pl.BlockSpec((DH, d_model), lambda h: (h, 0))],
        out_specs=[pl.BlockSpec((n, DH), lambda h: (0, h)), pl.BlockSpec((n, DH), lambda h: (0, ZB + h)),
                   pl.BlockSpec((1, DH), lambda h: (0, 0))],
        out_shape=[jax.ShapeDtypeStruct((n, GW), BF16), jax.ShapeDtypeStruct((n, GW_COLS), BF16),
                   jax.ShapeDtypeStruct((1, DH), F32)],
        compiler_params=_params(("arbitrary",), 40 * 2**20),
    )(o, proj, wg, dout_b, w_out)


def _conv_branch(proj, w3, b, mix):
    n = proj.shape[0]

    def body(p4, w_ref, b_ref, _, y_ref):
        u = p4[:, DH:2 * DH] * p4[:, 2 * DH:3 * DH]
        cc = _conv_silu(u, w_ref, 3) + b_ref[...]
        z = p4[:, 3 * DH:4 * DH]
        y_ref[...] = (p4[:, 0:DH] * cc * (z * _sigmoid(z))).astype(BF16)

    return pl.pallas_call(
        body, name="conv_branch", grid=(HEADS,),
        in_specs=[pl.BlockSpec((n, 4 * DH), lambda h: (0, h)), pl.BlockSpec((3, DH), lambda h: (0, h)),
                  pl.BlockSpec((1, DH), lambda h: (0, h)), ANY],
        out_specs=pl.BlockSpec((n, DH), lambda h: (0, HEADS + h)),
        out_shape=jax.ShapeDtypeStruct(mix.shape, BF16),
        input_output_aliases={3: 0},
        compiler_params=_params(("parallel",), 40 * 2**20),
    )(*_in_hbm(proj, w3, b, mix))


def _conv_branch_bwd(proj, w3, b, dout_b, w_out):
    n = proj.shape[0]
    d_model = dout_b.shape[1]

    def body(p4, w_ref, b_ref, g_ref, wo_ref, o4, gw_ref, gbias_ref):
        gb, gcv, hc, z = p4[:, 0:DH], p4[:, DH:2 * DH], p4[:, 2 * DH:3 * DH], p4[:, 3 * DH:4 * DH]
        d = _dot(g_ref[...], wo_ref[...], NT)
        dgb, dgc, dhc, dzc = (o4.at[:, kk * DH:(kk + 1) * DH] for kk in range(4))
        u = gcv * hc
        cc = _conv_silu(u, w_ref, 3) + b_ref[...]
        s = _sigmoid(z)
        dzc[...] = (d * (gb * cc) * _dsilu(z, s)).astype(BF16)
        dp = d * (z * s)
        dgb[...] = (dp * cc).astype(BF16)
        dcc = dp * gb
        gbias_ref[...] = jnp.sum(dcc, axis=0, keepdims=True)
        du = None
        for j in range(3):
            gw_ref[j:j + 1, :] = jnp.sum(dcc * _shift_down(u, 2 - j), axis=0, keepdims=True)
            t = _shift_up(dcc, 2 - j) * w_ref[j:j + 1, :]
            du = t if du is None else du + t
        dgc[...] = (du * hc).astype(BF16)
        dhc[...] = (du * gcv).astype(BF16)

    p4spec = pl.BlockSpec((n, 4 * DH), lambda h: (0, h))
    return pl.pallas_call(
        body, name="conv_branch_bwd", grid=(HEADS,),
        in_specs=[p4spec, pl.BlockSpec((3, DH), lambda h: (0, h)), pl.BlockSpec((1, DH), lambda h: (0, h)),
                  pl.BlockSpec((n, d_model), lambda h: (0, 0)), pl.BlockSpec((DH, d_model), lambda h: (HEADS + h, 0))],
        out_specs=[p4spec, pl.BlockSpec((3, DH), lambda h: (0, h)), pl.BlockSpec((1, DH), lambda h: (0, h))],
        out_shape=[jax.ShapeDtypeStruct((n, CW_COLS), BF16), jax.ShapeDtypeStruct((3, GW), F32),
                   jax.ShapeDtypeStruct((1, GW), F32)],
        compiler_params=_params(("parallel",), 52 * 2**20),
    )(proj, w3, b, dout_b, w_out)


def _out_loss(mix, w_out, x, tgt, wf):
    n, d = x.shape
    kdim = mix.shape[1]
    tr = min(256, n)

    def body(m_ref, wo_ref, x_ref, t_ref, w_ref, do_ref, dob_ref, gw_ref, loss_ref):
        ov = _dot(m_ref[...], wo_ref[...], NN) + x_ref[...]
        w = w_ref[...]
        r = lax.rsqrt(jnp.mean(ov * ov, axis=-1, keepdims=True) + EPS)
        nrm = ov * r
        e = nrm * w - t_ref[...]
        dy = e * (1.0 / d)
        dn = dy * w
        dout = r * (dn - nrm * jnp.mean(dn * nrm, axis=-1, keepdims=True))
        do_ref[...] = dout
        dob_ref[...] = dout.astype(BF16)

        @pl.when(pl.program_id(0) == 0)
        def _():
            gw_ref[...] = jnp.zeros_like(gw_ref)
            loss_ref[...] = jnp.zeros_like(loss_ref)

        gw_ref[...] += jnp.sum(dy * nrm, axis=0, keepdims=True)
        loss_ref[...] += (0.5 / d) * jnp.sum(jnp.sum(e * e, axis=-1, keepdims=True), axis=0, keepdims=True)

    row = pl.BlockSpec((tr, d), lambda i: (i, 0))
    return pl.pallas_call(
        body, name="out_loss", grid=(n // tr,),
        in_specs=[pl.BlockSpec((tr, kdim), lambda i: (i, 0)), pl.BlockSpec((kdim, d), lambda i: (0, 0)), row, row,
                  pl.BlockSpec((1, d), lambda i: (0, 0))],
        out_specs=[row, row, pl.BlockSpec((1, d), lambda i: (0, 0)), pl.BlockSpec((1, 1), lambda i: (0, 0))],
        out_shape=[jax.ShapeDtypeStruct((n, d), F32), jax.ShapeDtypeStruct((n, d), BF16),
                   jax.ShapeDtypeStruct((1, d), F32), jax.ShapeDtypeStruct((1, 1), F32)],
        compiler_params=_params(("arbitrary",), 40 * 2**20),
    )(mix, w_out, x, tgt, wf)


def _dh_rms_bwd(dproj, w_t, dh0, x, w, dout, tk):
    n, d = x.shape
    kdim = dproj.shape[1]
    tm = min(1024, n)
    tk = min(tk, kdim)
    nk = kdim // tk

    def body(a_ref, b_ref, dh0_ref, x_ref, w_ref, do_ref, dx_ref, gw_ref, acc):
        i, kk = pl.program_id(0), pl.program_id(1)
        part = _dot(a_ref[...], b_ref[...], NN)

        @pl.when(kk == 0)
        def _():
            acc[...] = part + dh0_ref[...]

        @pl.when(kk > 0)
        def _():
            acc[...] += part

        @pl.when((i == 0) & (kk == 0))
        def _():
            gw_ref[...] = jnp.zeros_like(gw_ref)

        @pl.when(kk == nk - 1)
        def _():
            xv, dhv = x_ref[...], acc[...]
            r = lax.rsqrt(jnp.mean(xv * xv, axis=-1, keepdims=True) + EPS)
            xn = xv * r
            dxn = dhv * w_ref[...]
            dx_ref[...] = r * (dxn - xn * jnp.mean(dxn * xn, axis=-1, keepdims=True)) + do_ref[...]
            gw_ref[...] += jnp.sum(dhv * xn, axis=0, keepdims=True)

    row = pl.BlockSpec((tm, d), lambda i, kk: (i, 0))
    one = pl.BlockSpec((1, d), lambda i, kk: (0, 0))
    return pl.pallas_call(
        body, name="dh_rms_bwd", grid=(n // tm, nk),
        in_specs=[pl.BlockSpec((tm, tk), lambda i, kk: (i, kk)), pl.BlockSpec((tk, d), lambda i, kk: (kk, 0)),
                  row, row, one, row],
        out_specs=[row, one],
        out_shape=[jax.ShapeDtypeStruct((n, d), F32), jax.ShapeDtypeStruct((1, d), F32)],
        scratch_shapes=[pltpu.VMEM((tm, d), F32)],
        compiler_params=_params(("arbitrary", "arbitrary"), 56 * 2**20),
    )(dproj, w_t, dh0, x, w, dout)


def _ij():
    i = lax.broadcasted_iota(jnp.int32, (CH, CH), 0)
    j = lax.broadcasted_iota(jnp.int32, (CH, CH), 1)
    return i, j


def _unit_lower_inverse(mats):
    i, j = _ij()
    eye = jnp.where(i == j, 1.0, 0.0)
    same16 = (i // 16) == (j // 16)
    same32 = (i // 32) == (j // 32)
    mm = lambda xs, ys: [_dot(x, y, NN, P_INV) for x, y in zip(xs, ys)]
    n1 = [jnp.where(same16, -a, 0.0) for a in mats]
    n2 = mm(n1, n1)
    n4 = mm(n2, n2)
    n8 = mm(n4, n4)
    t = [eye + x1 + x2 + x3 for x1, x2, x3 in zip(n1, n2, mm(n1, n2))]
    t = [x + y for x, y in zip(t, mm(t, n4))]
    t = [x + y for x, y in zip(t, mm(t, n8))]
    a1 = [jnp.where(same32 & jnp.logical_not(same16), a, 0.0) for a in mats]
    t = [x - y for x, y in zip(t, mm(t, mm(a1, t)))]
    a2 = [jnp.where(same32, 0.0, a) for a in mats]
    t = [x - y for x, y in zip(t, mm(t, mm(a2, t)))]
    return t


def _head_vectors(bg, bgt, h):
    bcol = bg[:, h:h + 1]
    gcol = bg[:, HEADS + h:HEADS + h + 1]
    grow = bgt[HEADS + h:HEADS + h + 1, :]
    return bcol, gcol, grow


def _decay(gcol, grow):
    i, j = _ij()
    return jnp.where(i >= j, jnp.exp(jnp.where(i >= j, gcol - grow, 0.0)), 0.0)


def _gdn_intra(q, k, v, bg, bgt):
    n = q.shape[0]
    nch = n // CH
    cps = 4 if nch % 4 == 0 else 1

    def body(q_ref, k_ref, v_ref, bg_ref, bgt_ref, u_ref, w_ref, p_ref, t_ref):
        i, j = _ij()
        items = [(ci, h) for ci in range(cps) for h in range(HEADS)]
        at = lambda ref, ci, h: ref.at[ci * CH:(ci + 1) * CH, h * DH:(h + 1) * DH]
        bgs = [bg_ref[ci * CH:(ci + 1) * CH, :] for ci in range(cps)]
        ks = [at(k_ref, ci, h)[...] for ci, h in items]
        vecs = [_head_vectors(bgs[ci], bgt_ref[ci], h) for ci, h in items]
        decs = [_decay(gcol, grow) for _, gcol, grow in vecs]
        kks = [_dot(kh, kh, NT, P_GRAM) for kh in ks]
        qks = [_dot(at(q_ref, ci, h)[...], kh, NT, P_GRAM) for (ci, h), kh in zip(items, ks)]
        ts = _unit_lower_inverse([jnp.where(i > j, bcol * kk * dec, 0.0)
                                  for (bcol, _, _), kk, dec in zip(vecs, kks, decs)])
        us = [_dot(t, at(v_ref, ci, h)[...] * bcol, NN, P_SOL) for t, (ci, h), (bcol, _, _) in zip(ts, items, vecs)]
        ws = [_dot(t, kh * (bcol * jnp.exp(gcol)), NN, P_SOL) for t, kh, (bcol, gcol, _) in zip(ts, ks, vecs)]
        for n_, (ci, h) in enumerate(items):
            p_ref[ci, h] = qks[n_] * decs[n_]
            t_ref[ci, h] = ts[n_].astype(BF16)
            at(u_ref, ci, h)[...] = us[n_]
            at(w_ref, ci, h)[...] = ws[n_].astype(BF16)

    row = pl.BlockSpec((cps * CH, GW), lambda c: (c, 0))
    sq = pl.BlockSpec((cps, HEADS, CH, CH), lambda c: (c, 0, 0, 0))
    big = jax.ShapeDtypeStruct((n, GW), F32)
    sqs = jax.ShapeDtypeStruct((nch, HEADS, CH, CH), F32)
    return pl.pallas_call(
        body, name="gdn_intra", grid=(nch // cps,),
        in_specs=[row, row, row, pl.BlockSpec((cps * CH, DH), lambda c: (c, 0)),
                  pl.BlockSpec((cps, DH, CH), lambda c: (c, 0, 0))],
        out_specs=[row, row, sq, sq],
        out_shape=[big, jax.ShapeDtypeStruct((n, GW), BF16), sqs, jax.ShapeDtypeStruct(sqs.shape, BF16)],
        compiler_params=_params(("parallel",)),
    )(q, k, v, bg, bgt)


def _gdn_scan(q, k, bg, u, w, p):
    n = q.shape[0]
    nch = n // CH
    cps = SCAN_CPS if nch % SCAN_CPS == 0 else 1

    def body(q_ref, k_ref, bg_ref, u_ref, w_ref, p_ref, o_ref, vn_ref, s_out, s_scr):
        @pl.when(pl.program_id(0) == 0)
        def _():
            s_scr[...] = jnp.zeros_like(s_scr)

        hs = range(HEADS)
        sls = [slice(h * DH, (h + 1) * DH) for h in hs]
        ss = [s_scr[h] for h in hs]
        for ci in range(cps):
            rs = slice(ci * CH, (ci + 1) * CH)
            bg = bg_ref[rs, :]
            gcols = [bg[:, HEADS + h:HEADS + h + 1] for h in hs]
            glasts = [g[CH - 1:CH, :] for g in gcols]
            wss = [_dot(w_ref[rs, sl], s, NN, P_SCAN) for sl, s in zip(sls, ss)]
            oqs = [_dot(q_ref[rs, sl] * jnp.exp(g), s, NN, P_SCAN) for sl, s, g in zip(sls, ss, gcols)]
            vns = [u_ref[rs, sl] - x for sl, x in zip(sls, wss)]
            ops = [_dot(p_ref[ci, h], vn, NN, P_SCAN) for h, vn in zip(hs, vns)]
            sns = [_dot(k_ref[rs, sl] * jnp.exp(gl - g), vn, TN, P_SCAN)
                   for sl, gl, g, vn in zip(sls, glasts, gcols, vns)]
            for h, sl in enumerate(sls):
                s_out[ci, :, sl] = ss[h].astype(BF16)
                vn_ref[rs, sl] = vns[h].astype(BF16)
                o_ref[rs, sl] = oqs[h] + ops[h]
            ss = [s * jnp.exp(gl) + sn for s, gl, sn in zip(ss, glasts, sns)]
        for h in hs:
            s_scr[h] = ss[h]

    row = pl.BlockSpec((cps * CH, GW), lambda c: (c, 0))
    big = jax.ShapeDtypeStruct((n, GW), F32)
    return pl.pallas_call(
        body, name="gdn_scan", grid=(nch // cps,),
        in_specs=[row, row, pl.BlockSpec((cps * CH, DH), lambda c: (c, 0)), row, row,
                  pl.BlockSpec((cps, HEADS, CH, CH), lambda c: (c, 0, 0, 0))],
        out_specs=[row, row, pl.BlockSpec((cps, DH, GW), lambda c: (c, 0, 0))],
        out_shape=[big, jax.ShapeDtypeStruct((n, GW), BF16), jax.ShapeDtypeStruct((nch, DH, GW), BF16)],
        scratch_shapes=[pltpu.VMEM((HEADS, DH, DH), F32)],
        compiler_params=_params(("arbitrary",)),
    )(q, k, bg, u, w, p)


def _gdn_scan_bwd(q, k, bg, w, p, vn, s_in, do):
    n = q.shape[0]
    nch = n // CH
    cps = SCAN_CPS if nch % SCAN_CPS == 0 else 1
    rev = lambda c: nch // cps - 1 - c

    def body(q_ref, k_ref, bg_ref, w_ref, p_ref, vn_ref, s_ref, do_ref,
             dqg_ref, dp_ref, du_ref, dw_ref, dks_ref, dgam_ref, ds_scr):
        @pl.when(pl.program_id(0) == 0)
        def _():
            ds_scr[...] = jnp.zeros_like(ds_scr)

        lane = _lane((1, DH))
        hs = range(HEADS)
        sls = [slice(h * DH, (h + 1) * DH) for h in hs]
        dss = [ds_scr[h] for h in hs]
        for ci in reversed(range(cps)):
            rs = slice(ci * CH, (ci + 1) * CH)
            bg = bg_ref[rs, :]
            gcols = [bg[:, HEADS + h:HEADS + h + 1] for h in hs]
            glasts = [g[CH - 1:CH, :] for g in gcols]
            ss = [s_ref[ci, :, sl] for sl in sls]
            dos = [do_ref[rs, sl] for sl in sls]
            vnl = [vn_ref[rs, sl] for sl in sls]
            dqgs = [_dot(d, s, NT, P_SCANB) for d, s in zip(dos, ss)]
            dps = [_dot(d, vn, NT, P_SCANB) for d, vn in zip(dos, vnl)]
            dvn1 = [_dot(p_ref[ci, h], d, TN, P_SCANB) for h, d in zip(hs, dos)]
            dvn2 = [_dot(k_ref[rs, sl] * jnp.exp(gl - g), ds, NN, P_SCANB)
                    for sl, gl, g, ds in zip(sls, glasts, gcols, dss)]
            dkss = [_dot(vn, ds, NT, P_SCANB) for vn, ds in zip(vnl, dss)]
            dsq = [_dot(q_ref[rs, sl] * jnp.exp(g), d, TN, P_SCANB) for sl, g, d in zip(sls, gcols, dos)]
            dvns = [a + b for a, b in zip(dvn1, dvn2)]
            dws = [_dot(dvn, s, NT, P_SCANB) for dvn, s in zip(dvns, ss)]
            dsw = [_dot(w_ref[rs, sl], dvn, TN, P_SCANB) for sl, dvn in zip(sls, dvns)]
            dgam = jnp.zeros((1, DH), F32)
            for h, sl in enumerate(sls):
                dqg_ref[rs, sl] = dqgs[h]
                dp_ref[ci, h] = dps[h]
                du_ref[rs, sl] = dvns[h].astype(BF16)
                dw_ref[rs, sl] = (-dws[h]).astype(BF16)
                dks_ref[rs, sl] = dkss[h]
                tot = jnp.sum(jnp.sum(dss[h] * ss[h], axis=-1, keepdims=True), axis=0, keepdims=True)
                dgam = dgam + jnp.where(lane == h, tot, 0.0)
            dgam_ref[ci] = jnp.broadcast_to(dgam, (8, DH))
            dss = [ds * jnp.exp(gl) + a - b for ds, gl, a, b in zip(dss, glasts, dsq, dsw)]
        for h in hs:
            ds_scr[h] = dss[h]

    row = pl.BlockSpec((cps * CH, GW), lambda c: (rev(c), 0))
    sq =pl.BlockSpec((cps, HEADS, CH, CH), lambda c: (rev(c), 0, 0, 0))
    big = jax.ShapeDtypeStruct((n, GW), F32)
    return pl.pallas_call(
        body, name="gdn_scan_bwd", grid=(nch // cps,),
        in_specs=[row, row, pl.BlockSpec((cps * CH, DH), lambda c: (rev(c), 0)), row, sq, row,
                  pl.BlockSpec((cps, DH, GW), lambda c: (rev(c), 0, 0)), row],
        out_specs=[row, sq, row, row, row, pl.BlockSpec((cps, 8, DH), lambda c: (rev(c), 0, 0))],
        out_shape=[big, jax.ShapeDtypeStruct((nch, HEADS, CH, CH), F32), jax.ShapeDtypeStruct((n, GW), BF16),
                   jax.ShapeDtypeStruct((n, GW), BF16), big,
                   jax.ShapeDtypeStruct((nch, 8, DH), F32)],
        scratch_shapes=[pltpu.VMEM((HEADS, DH, DH), F32)],
        compiler_params=_params(("arbitrary",)),
    )(q, k, bg, w, p, vn, s_in, do)


def _gdn_intra_bwd(q, k, v, bg, bgt, t, u, w, p, dqg, dp, du, dw, dks, dgam):
    n = q.shape[0]
    nch = n // CH
    cps = 2 if nch % 2 == 0 else 1

    def body(q_ref, k_ref, v_ref, bg_ref, bgt_ref, t_ref, u_ref, w_ref, p_ref,
             dqg_ref, dp_ref, du_ref, dw_ref, dks_ref, dgam_ref, dq_ref, dk_ref, dv_ref, dbg_ref):
        i, j = _ij()
        rows1 = lax.broadcasted_iota(jnp.int32, (CH, 1), 0)
        lane = _lane((CH, DH))
        rsum = lambda x: jnp.sum(x, axis=-1, keepdims=True)
        items = [(ci, h) for ci in range(cps) for h in range(HEADS)]
        at = lambda ref, it: ref.at[it[0] * CH:(it[0] + 1) * CH, it[1] * DH:(it[1] + 1) * DH]
        ld = lambda ref: [at(ref, it)[...] for it in items]
        bgs = [bg_ref[ci * CH:(ci + 1) * CH, :] for ci in range(cps)]
        qs, ks = ld(q_ref), ld(k_ref)
        vecs = [_head_vectors(bgs[ci], bgt_ref[ci], h) for ci, h in items]
        decs = [_decay(gcol, grow) for _, gcol, grow in vecs]
        ths = [t_ref[ci, h] for ci, h in items]
        drus = [_dot(th, x_, TN, P_BWD) for th, x_ in zip(ths, ld(du_ref))]
        drws = [_dot(th, x_, TN, P_BWD) for th, x_ in zip(ths, ld(dw_ref))]
        kks = [_dot(kh, kh, NT, P_GRAM) for kh in ks]
        da1 = [_dot(dru, x_, NT, P_BWD) for dru, x_ in zip(drus, ld(u_ref))]
        da2 = [_dot(drw, x_, NT, P_BWD) for drw, x_ in zip(drws, ld(w_ref))]
        das = [jnp.where(i > j, -(x_ + y_), 0.0) for x_, y_ in zip(da1, da2)]
        dkks = [da * bcol * dec for da, (bcol, _, _), dec in zip(das, vecs, decs)]
        dps = [dp_ref[ci, h] for ci, h in items]
        dqks = [dp_ * dec for dp_, dec in zip(dps, decs)]
        dq_ps = [_dot(dqk, kh, NN, P_BWD) for dqk, kh in zip(dqks, ks)]
        dk_ps = [_dot(dqk, qh, TN, P_BWD) for dqk, qh in zip(dqks, qs)]
        dk_as = [_dot(dkk, kh, NN, P_BWD) for dkk, kh in zip(dkks, ks)]
        dk_bs = [_dot(dkk, kh, TN, P_BWD) for dkk, kh in zip(dkks, ks)]
        bcols = [vc[0] for vc in vecs]
        gcols = [vc[1] for vc in vecs]
        gams = [jnp.exp(g) for g in gcols]
        glasts = [g[CH - 1:CH, :] for g in gcols]
        es = [jnp.exp(gl - g) for gl, g in zip(glasts, gcols)]
        kgs = [kh * gam for kh, gam in zip(ks, gams)]
        dqgs, dkss = ld(dqg_ref), ld(dks_ref)
        wks = [drw * kg for drw, kg in zip(drws, kgs)]
        kss = [dk_ * (kh * e) for dk_, kh, e in zip(dkss, ks, es)]
        r_beta = [rsum(dru * x_ + wk) for dru, x_, wk in zip(drus, ld(v_ref), wks)]
        r_ak = [rsum(da * kk * dec) for da, kk, dec in zip(das, kks, decs)]
        r_gc = [rsum(wk * bcol + dqg * (qh * gam) - ks_)
                for wk, bcol, dqg, qh, gam, ks_ in zip(wks, bcols, dqgs, qs, gams, kss)]
        tk_tot = [jnp.sum(jnp.sum(ks_, axis=0, keepdims=True), axis=-1, keepdims=True) for ks_ in kss]
        mdecs = [da * (bcol * kk * dec) + dp_ * p_ref[ci, h]
                 for (ci, h), da, bcol, kk, dec, dp_ in zip(items, das, bcols, kks, decs, dps)]
        r_md = [rsum(m) for m in mdecs]
        c_md = [rsum(jnp.where(i == j, jnp.sum(m, axis=0, keepdims=True), 0.0)) for m in mdecs]
        dbgs = [jnp.zeros((CH, DH), F32) for _ in range(cps)]
        for n_, (ci, h) in enumerate(items):
            at(dv_ref, (ci, h))[...] = bcols[n_] * drus[n_]
            at(dq_ref, (ci, h))[...] = gams[n_] * dqgs[n_] + dq_ps[n_]
            at(dk_ref, (ci, h))[...] = ((bcols[n_] * gams[n_]) * drws[n_] + dk_ps[n_] + dk_as[n_] + dk_bs[n_]
                                        + dkss[n_] * es[n_])
            dbeta = r_beta[n_] + r_ak[n_]
            dglast = tk_tot[n_] + dgam_ref[ci, 0:1, h:h + 1] * jnp.exp(glasts[n_])
            dgc = r_gc[n_] + r_md[n_] - c_md[n_] + jnp.where(rows1 == CH - 1, dglast, 0.0)
            dbgs[ci] = dbgs[ci] + jnp.where(lane == h, dbeta, 0.0) + jnp.where(lane == HEADS + h, dgc, 0.0)
        for ci in range(cps):
            dbg_ref[ci * CH:(ci + 1) * CH, :] = dbgs[ci]

    row = pl.BlockSpec((cps * CH, GW), lambda c: (c, 0))
    sq = pl.BlockSpec((cps, HEADS, CH, CH), lambda c: (c, 0, 0, 0))
    small = pl.BlockSpec((cps * CH, DH), lambda c: (c, 0))
    big = jax.ShapeDtypeStruct((n, GW), F32)
    return pl.pallas_call(
        body, name="gdn_intra_bwd", grid=(nch // cps,),
        in_specs=[row, row, row, small, pl.BlockSpec((cps, DH, CH), lambda c: (c, 0, 0)), sq, row, row, sq,
                  row, sq, row, row, row, pl.BlockSpec((cps, 8, DH), lambda c: (c, 0, 0))],
        out_specs=[row, row, row, small],
        out_shape=[big, big, big, jax.ShapeDtypeStruct((n, DH), F32)],
        compiler_params=_params(("parallel",)),
    )(q, k, v, bg, bgt, t, u, w, p, dqg, dp, du, dw, dks, dgam)


def _local_step(x, tgt, h, w_g, cqw, late, norm_in_w, ad, gdn_norm_w, conv_b, final_norm_w,
                on_grad_c=None, on_grad_g=None, on_q=None):
    proj_g = _matmul(h, w_g, NT, F32, 512, 1408, 1024, "mm_proj_g", n=GW_COLS, b_outer=True)
    q, k, v = _prep_qkv(proj_g, cqw)
    if on_q is not None:
        q = on_q(q)
    bg, bgt = _prep_bg(proj_g, ad)
    u, w, p, t = _gdn_intra(q, k, v, bg, bgt)
    o, vn, s_in = _gdn_scan(q, k, bg, u, w, p)
    w_c, w_out, conv_w = late(o)
    proj_c = _matmul(h, w_c, NT, F32, 512, 1024, 1024, "mm_proj_c", n=CW_COLS, b_outer=True)
    mix = _conv_branch(proj_c, conv_w, conv_b, _gdn_out(o, proj_g, gdn_norm_w))
    dout, dout_b, g_fn, loss = _out_loss(mix, w_out, x, tgt, final_norm_w)

    g_wout = _matmul(mix, dout_b, TN, BF16, 512, 512, 2048, "mm_gwout")
    do, dproj_g, g_gn = _gdn_out_bwd(o, proj_g, gdn_norm_w, dout_b, w_out)
    dproj_c, g_cw, g_cb = _conv_branch_bwd(proj_c, conv_w, conv_b, dout_b, w_out)
    g_c = _matmul(dproj_c, h, TN, BF16, 1024, 512, 2048, "mm_gwin_c")
    if on_grad_c is not None:
        do = on_grad_c(g_c, g_wout, do)
    dqg, dp, du, dw, dks, dgam = _gdn_scan_bwd(q, k, bg, w, p, vn, s_in, do)
    dq, dk, dv, dbg = _gdn_intra_bwd(q, k, v, bg, bgt, t, u, w, p, dqg, dp, du, dw, dks, dgam)
    dproj_g, gq, gk, gv = _prep_qkv_bwd(proj_g, cqw, dq, dk, dv, dproj_g)
    dproj_g, g_al, g_dt = _prep_bg_bwd(proj_g, ad, dbg, dproj_g)
    g_g = _matmul(dproj_g, h, TN, BF16, 1408, 512, 2048, "mm_gwin_g")
    if on_grad_g is not None:
        dproj_g = on_grad_g(g_g, dproj_g)
    dh = _matmul(dproj_g, w_g, NN, F32, 1024, 1024, 1408, "mm_dh_g")
    gx, g_nin = _dh_rms_bwd(dproj_c, w_c, dh, x, norm_in_w, dout, 1024)
    small = dict(nin=g_nin, cb=g_cb, fn=g_fn, al=g_al, dt=g_dt, gn=g_gn, cq=(gq, gk, gv), cw=g_cw, loss=loss)
    return gx, small, (g_g, g_c, g_wout)


def _place():
    x, y, c = lax.axis_index("x"), lax.axis_index("y"), lax.axis_index("c")
    chips = [(1 - x, y), (x, 1 - y), (1 - x, 1 - y)]
    return x, y, c, chips


def _blk(ref, b):
    if isinstance(b, int):
        return ref.at[b * DH:(b + 1) * DH, :]
    return ref.at[pl.ds(pl.multiple_of(b * DH, DH), DH), :]


HBM = pl.BlockSpec(memory_space=pltpu.HBM)
SEM = pl.BlockSpec(memory_space=pltpu.SEMAPHORE)
EFFECT = pltpu.SideEffectType.DATAFLOW_SIDE_EFFECTING


def _split_start(name, issue, bufs, n_sems):
    nbuf = len(bufs)

    def body(*refs):
        issue(refs[:nbuf], refs[nbuf], refs[nbuf + 1])
        refs[-1][...] = jnp.zeros_like(refs[-1])

    out = pl.pallas_call(
        body, name=name,
        out_shape=(pltpu.SemaphoreType.DMA((n_sems,)), pltpu.SemaphoreType.DMA((n_sems,)),
                   *[pltpu.HBM(b.shape, b.dtype) for b in bufs], jax.ShapeDtypeStruct((8, DH), F32)),
        in_specs=[HBM] * nbuf,
        out_specs=(SEM, SEM, *[HBM] * nbuf, pl.BlockSpec(memory_space=pltpu.VMEM)),
        input_output_aliases={a: 2 + a for a in range(nbuf)},
        compiler_params=pltpu.CompilerParams(has_side_effects=EFFECT),
    )(*[pltpu.with_memory_space_constraint(b, pltpu.HBM) for b in bufs])
    return out[0], out[1], list(out[2:2 + nbuf]), out[-1]


def _split_wait(name, await_, send_sems, recv_sems, bufs, after):
    nbuf = len(bufs)
    after = list(after) if isinstance(after, (list, tuple)) else [after]

    def body(*refs):
        await_(refs[:nbuf], refs[nbuf], refs[nbuf + 1])

    out = pl.pallas_call(
        body, name=name,
        out_shape=tuple(pltpu.HBM(b.shape, b.dtype) for b in bufs),
        in_specs=[HBM] * nbuf + [SEM, SEM] + [ANY] * len(after), out_specs=tuple([HBM] * nbuf),
        input_output_aliases={a: a for a in range(nbuf)},
        compiler_params=pltpu.CompilerParams(has_side_effects=EFFECT),
    )(*bufs, send_sems, recv_sems, *after)
    return list(out)


def _phase_blocks(chip, phase, edges, parity=None):
    return [(b, blk) for b, (grp, blk) in enumerate(_shard_blocks(chip, edges))
            if grp == phase and (parity is None or b % 2 == parity)]


def _cols(ref, nblk):
    return ref.at[0:nblk * DH, :]


def _block_table(chip, edges, spare_g, spare_c):
    rows = []
    for s in range(4):
        sb = _shard_blocks(s, edges)
        rows.append([[blk if grp == "g" else spare_g for grp, blk in sb],
                     [blk if grp == "c" else spare_c for grp, blk in sb],
                     [int(grp == "g") for grp, _ in sb], [s] * ALIGNED_BLOCKS])
    return jnp.asarray(rows, jnp.int32)[chip]


def _place_own(a_shard, wo, cq, cw, bufs):
    d = a_shard.shape[1]
    chip = 2 * lax.axis_index("x") + lax.axis_index("y")

    def body(t_ref, a_ref, wo_ref, cq_ref, cw_ref, *refs):
        wg_ref, wc_ref, wog_ref, cqg_ref, cwg_ref = refs[5:]
        wg_ref[...] = a_ref[...]
        wc_ref[...] = a_ref[...]

        @pl.when(pl.program_id(0) == 0)
        def _():
            wog_ref[0] = wo_ref[...]
            cqg_ref[0] = cq_ref[...]
            cwg_ref[0] = cw_ref[...]

    whole = lambda s: pl.BlockSpec(s.shape, lambda b, t: (0,) * s.ndim)
    slot = lambda s: pl.BlockSpec((1,) + s.shape, lambda b, t: (t[3, 0],) + (0,) * s.ndim)
    return pl.pallas_call(
        body, name="place_own",
        grid_spec=pltpu.PrefetchScalarGridSpec(
            num_scalar_prefetch=1, grid=(ALIGNED_BLOCKS,),
            in_specs=[pl.BlockSpec((DH, d), lambda b, t: (b, 0)), whole(wo), whole(cq), whole(cw)] + [ANY] * 5,
            out_specs=[pl.BlockSpec((DH, d), lambda b, t: (t[0, b], 0)),
                       pl.BlockSpec((DH, d), lambda b, t: (t[1, b], 0)), slot(wo), slot(cq), slot(cw)]),
        out_shape=[jax.ShapeDtypeStruct(b.shape, b.dtype) for b in bufs],
        input_output_aliases={5 + a: a for a in range(5)},
        compiler_params=_params(("arbitrary",)),
    )(_block_table(chip, True, G_SPARE, C_SPARE), a_shard, wo, cq, cw, *bufs)


def _tie(x, token, name):
    def body(x_ref, t_ref, o_ref):
        del x_ref, t_ref, o_ref

    return pl.pallas_call(
        body, name=name, in_specs=[ANY, ANY], out_specs=ANY,
        out_shape=jax.ShapeDtypeStruct(x.shape, x.dtype), input_output_aliases={0: 0},
    )(x, token)


def _gather_start(phase, a_shard, w_grp, singles):
    ns = len(singles)

    def issue(refs, send_sems, recv_sems):
        a_ref, w_ref = refs[0], refs[1]
        x, y, c, chips = _place()
        mine = 2 * x + y
        for jj, (px, py) in enumerate(chips):
            to = dict(device_id=(px, py, c), device_id_type=MESH)
            for a in range(ns):
                pltpu.make_async_remote_copy(
                    src_ref=refs[2 + 2 * a], dst_ref=refs[3 + 2 * a].at[mine],
                    send_sem=send_sems.at[(1 + ns) * jj + 1 + a], recv_sem=recv_sems.at[(1 + ns) * jj + 1 + a],
                    **to).start()
        for s in range(4):
            for par in range(2):
                blocks = _phase_blocks(s, phase, True, par)
                if blocks:
                    @pl.when((mine == s) & (c == par))
                    def _():
                        for b, blk in blocks:
                            for jj, (px, py) in enumerate(chips):
                                pltpu.make_async_remote_copy(
                                    src_ref=_blk(a_ref, b), dst_ref=_blk(w_ref, blk),
                                    send_sem=send_sems.at[(1 + ns) * jj], recv_sem=recv_sems.at[(1 + ns) * jj],
                                    device_id=(px, py, c), device_id_type=MESH).start()

    bufs = [a_shard, w_grp] + [t for pair in singles for t in pair]
    return _split_start("gather_start_" + phase, issue, bufs, 3 * (1 + ns))


def _gather_wait(phase, send_sems, recv_sems, bufs, after):
    ns = (len(bufs) - 2) // 2

    def await_(refs, send_sems, recv_sems):
        a_ref, w_ref = refs[0], refs[1]
        x, y, c, chips = _place()
        mine = 2 * x + y
        for jj, (px, py) in enumerate(chips):
            to = dict(device_id=(px, py, c), device_id_type=MESH)
            peer = 2 * px + py
            for a in range(ns):
                cp = pltpu.make_async_remote_copy(
                    src_ref=refs[2 + 2 * a], dst_ref=refs[3 + 2 * a].at[mine],
                    send_sem=send_sems.at[(1 + ns) * jj + 1 + a], recv_sem=recv_sems.at[(1 + ns) * jj + 1 + a], **to)
                cp.wait_recv()
                cp.wait_send()
            for s in range(4):
                for par in range(2):
                    nblk = len(_phase_blocks(s, phase, True, par))
                    if nblk:
                        both = pltpu.make_async_remote_copy(
                            src_ref=_cols(a_ref, nblk), dst_ref=_cols(w_ref, nblk),
                            send_sem=send_sems.at[(1 + ns) * jj], recv_sem=recv_sems.at[(1 + ns) * jj], **to)

                        @pl.when((peer == s) & (c == par))
                        def _():
                            both.wait_recv()

                        @pl.when((mine == s) & (c == par))
                        def _():
                            both.wait_send()

    return _split_wait("gather_wait_" + phase, await_, send_sems, recv_sems, bufs, after)


def _sibling_forward_parts(phase):
    def each(w_ref, send_sems, recv_sems, start):
        x, y, c, chips = _place()
        to = dict(device_id=(x, y, 1 - c), device_id_type=MESH)
        for jj, (px, py) in enumerate(chips):
            peer = 2 * px + py
            for s in range(4):
                for par in range(2):
                    mine_blocks = _phase_blocks(s, phase, True, par)
                    theirs = len(_phase_blocks(s, phase, True, 1 - par))
                    if not (mine_blocks or theirs):
                        continue

                    @pl.when((peer == s) & (c == par))
                    def _():
                        if start:
                            for _, blk in mine_blocks:
                                pltpu.make_async_remote_copy(
                                    src_ref=_blk(w_ref, blk), dst_ref=_blk(w_ref, blk),
                                    send_sem=send_sems.at[jj], recv_sem=recv_sems.at[jj], **to).start()
                            return
                        if theirs:
                            pltpu.make_async_remote_copy(
                                src_ref=_cols(w_ref, theirs), dst_ref=_cols(w_ref, theirs),
                                send_sem=send_sems.at[jj], recv_sem=recv_sems.at[jj], **to).wait_recv()
                        if mine_blocks:
                            pltpu.make_async_remote_copy(
                                src_ref=_cols(w_ref, len(mine_blocks)), dst_ref=_cols(w_ref, len(mine_blocks)),
                                send_sem=send_sems.at[jj], recv_sem=recv_sems.at[jj], **to).wait_send()

    issue = lambda refs, send_sems, recv_sems: each(refs[0], send_sems, recv_sems, True)
    await_ = lambda refs, send_sems, recv_sems: each(refs[0], send_sems, recv_sems, False)
    return issue, await_


def _sibling_forward(phase, w_grp):
    issue, await_ = _sibling_forward_parts(phase)

    def body(w_in_ref, w_ref, send_sems, recv_sems):
        del w_in_ref
        issue([w_ref], send_sems, recv_sems)
        await_([w_ref], send_sems, recv_sems)

    return pl.pallas_call(
        body, name="sibling_forward_" + phase, in_specs=[ANY], out_specs=ANY,
        out_shape=jax.ShapeDtypeStruct(w_grp.shape, w_grp.dtype), input_output_aliases={0: 0},
        scratch_shapes=[pltpu.SemaphoreType.DMA((3,)), pltpu.SemaphoreType.DMA((3,))],
    )(w_grp)


def _merge_edges(w, edge0, mixed, name):
    d = w.shape[1]

    def body(e_ref, o_ref):
        o_ref[...] = e_ref[0:DH, :] + e_ref[DH:2 * DH, :]

    def to_block(i):
        r = mixed[-1]
        for kk in range(len(mixed) - 2, -1, -1):
            r = jnp.where(i == kk, mixed[kk], r)
        return r

    return pl.pallas_call(
        body, name=name, grid=(len(mixed),),
        in_specs=[pl.BlockSpec((2 * DH, d), lambda i: (edge0 // 2 + i, 0))],
        out_specs=pl.BlockSpec((DH, d), lambda i: (to_block(i), 0)),
        out_shape=jax.ShapeDtypeStruct(w.shape, w.dtype),
        input_output_aliases={0: 0},
        compiler_params=_params(("arbitrary",)),
    )(w)


def _scatter_start(phase, g_grp, land, singles, halved=False):
    ns = len(singles)

    def issue(refs, send_sems, recv_sems):
        g_ref, land_ref = refs[0], refs[1]
        x, y, c, chips = _place()
        for jj, (px, py) in enumerate(chips):
            to = dict(device_id=(px, py, c), device_id_type=MESH)
            peer = 2 * px + py
            for a in range(ns):
                pltpu.make_async_remote_copy(
                    src_ref=refs[2 + 2 * a].at[peer], dst_ref=refs[3 + 2 * a].at[jj],
                    send_sem=send_sems.at[(1 + ns) * jj + 1 + a], recv_sem=recv_sems.at[(1 + ns) * jj + 1 + a],
                    **to).start()
            for s in range(4):
                for par in ((0, 1) if halved else (None,)):
                    blocks = _phase_blocks(s, phase, False, par)
                    if blocks:
                        @pl.when((peer == s) if par is None else ((peer == s) & (c == par)))
                        def _():
                            for b, blk in blocks:
                                pltpu.make_async_remote_copy(
                                    src_ref=_blk(g_ref, blk), dst_ref=_blk(land_ref.at[jj], b),
                                    send_sem=send_sems.at[(1 + ns) * jj], recv_sem=recv_sems.at[(1 + ns) * jj],
                                    **to).start()

    bufs = [g_grp, land] + [t for pair in singles for t in pair]
    return _split_start("scatter_start_" + phase, issue, bufs, 3 * (1 + ns))


def _scatter_wait(phase, send_sems, recv_sems, bufs, after, halved=False):
    ns = (len(bufs) - 2) // 2

    def await_(refs, send_sems, recv_sems):
        g_ref, land_ref = refs[0], refs[1]
        x, y, c, chips = _place()
        mine = 2 * x + y
        for jj, (px, py) in enumerate(chips):
            to = dict(device_id=(px, py, c), device_id_type=MESH)
            peer = 2 * px + py
            for a in range(ns):
                cp = pltpu.make_async_remote_copy(
                    src_ref=refs[2 + 2 * a].at[peer], dst_ref=refs[3 + 2 * a].at[jj],
                    send_sem=send_sems.at[(1 + ns) * jj + 1 + a], recv_sem=recv_sems.at[(1 + ns) * jj + 1 + a], **to)
                cp.wait_recv()
                cp.wait_send()
            for s in range(4):
                for par in ((0, 1) if halved else (None,)):
                    nblk = len(_phase_blocks(s, phase, False, par))
                    if nblk:
                        both = pltpu.make_async_remote_copy(
                            src_ref=_cols(g_ref, nblk), dst_ref=_cols(land_ref.at[jj], nblk),
                            send_sem=send_sems.at[(1 + ns) * jj], recv_sem=recv_sems.at[(1 + ns) * jj], **to)

                        @pl.when((mine == s) if par is None else ((mine == s) & (c == par)))
                        def _():
                            both.wait_recv()

                        @pl.when((peer == s) if par is None else ((peer == s) & (c == par)))
                        def _():
                            both.wait_send()

    return _split_wait("scatter_wait_" + phase, await_, send_sems, recv_sems, bufs, after)


def _needed_blocks(phase, parity):
    return sorted({blk for s in range(4) for _, blk in _phase_blocks(s, phase, False, parity)})


def _pair_reduce(phase, g_grp):
    n, d = g_grp.shape

    def swap(g_ref, sib_ref, send_sem, recv_sem):
        x, y, c, _ = _place()
        to = dict(device_id=(x, y, 1 - c), device_id_type=MESH)
        for par in range(2):
            give, get = _needed_blocks(phase, 1 - par), _needed_blocks(phase, par)

            @pl.when(c == par)
            def _():
                for blk in give:
                    pltpu.make_async_remote_copy(src_ref=_blk(g_ref, blk), dst_ref=_blk(sib_ref, blk),
                                                 send_sem=send_sem, recv_sem=recv_sem, **to).start()
                pltpu.make_async_remote_copy(src_ref=_cols(g_ref, len(get)), dst_ref=_cols(sib_ref, len(get)),
                                             send_sem=send_sem, recv_sem=recv_sem, **to).wait_recv()
                pltpu.make_async_remote_copy(src_ref=_cols(g_ref, len(give)), dst_ref=_cols(sib_ref, len(give)),
                                             send_sem=send_sem, recv_sem=recv_sem, **to).wait_send()

    sib = pl.pallas_call(
        swap, name="pair_swap_" + phase, in_specs=[ANY], out_specs=ANY,
        out_shape=jax.ShapeDtypeStruct((n, d), g_grp.dtype),
        scratch_shapes=[pltpu.SemaphoreType.DMA, pltpu.SemaphoreType.DMA],
    )(*_in_hbm(g_grp))

    lists = [_needed_blocks(phase, par) for par in range(2)]
    longest = max(len(t) for t in lists)
    table = jnp.asarray([t + [t[-1]] * (longest - len(t)) for t in lists], jnp.int32)[lax.axis_index("c")]

    def add(t_ref, a_ref, b_ref, o_ref):
        o_ref[...] = (a_ref[...].astype(F32) + b_ref[...].astype(F32)).astype(o_ref.dtype)

    blk = pl.BlockSpec((DH, d), lambda i, t: (t[i], 0))
    return pl.pallas_call(
        add, name="pair_add_" + phase,
        grid_spec=pltpu.PrefetchScalarGridSpec(num_scalar_prefetch=1, grid=(longest,),
                                               in_specs=[blk, blk], out_specs=blk),
        out_shape=jax.ShapeDtypeStruct((n, d), g_grp.dtype),
        compiler_params=_params(("arbitrary",)),
    )(table, g_grp, sib)


def _sum_shard(g_g, g_c, land):
    d = g_g.shape[1]
    chip = 2 * lax.axis_index("x") + lax.axis_index("y")

    def body(t_ref, gg_ref, gc_ref, land_ref, o_ref):
        b = pl.program_id(0)
        in_g = t_ref[2, b] == 1
        own = jnp.where(in_g, gg_ref[...].astype(F32), gc_ref[...].astype(F32))
        for jj in range(3):
            own = own + land_ref[jj].astype(F32)
        o_ref[...] = jnp.where(in_g & (b % 2 != lax.axis_index("c")), 0.0, own)

    return pl.pallas_call(
        body, name="sum_w_in",
        grid_spec=pltpu.PrefetchScalarGridSpec(
            num_scalar_prefetch=1, grid=(ALIGNED_BLOCKS,),
            in_specs=[pl.BlockSpec((DH, d), lambda b, t: (t[0, b], 0)), pl.BlockSpec((DH, d), lambda b, t: (t[1, b], 0)),
                      pl.BlockSpec((3, DH, d), lambda b, t: (0, b, 0))],
            out_specs=pl.BlockSpec((DH, d), lambda b, t: (b, 0))),
        out_shape=jax.ShapeDtypeStruct((ALIGNED_W, d), F32),
        compiler_params=_params(("arbitrary",)),
    )(_block_table(chip, False, 0, 0), g_g, g_c, land)


def _sum_rows(stack, land, rows):
    _, r, d = stack.shape
    rows = min(rows, r)
    chip = 2 * lax.axis_index("x") + lax.axis_index("y")

    def body(t_ref, own_ref, land_ref, o_ref):
        acc = own_ref[0].astype(F32)
        for jj in range(3):
            acc = acc + land_ref[jj].astype(F32)
        o_ref[...] = acc

    return pl.pallas_call(
        body, name="sum_w_out",
        grid_spec=pltpu.PrefetchScalarGridSpec(
            num_scalar_prefetch=1, grid=(r // rows,),
            in_specs=[pl.BlockSpec((1, rows, d), lambda i, t: (t[0], i, 0)),
                      pl.BlockSpec((3, rows, d), lambda i, t: (0, i, 0))],
            out_specs=pl.BlockSpec((rows, d), lambda i, t: (i, 0))),
        out_shape=jax.ShapeDtypeStruct((r, d), F32),
        compiler_params=_params(("arbitrary",)),
    )(jnp.reshape(chip, (1,)).astype(jnp.int32), stack, land)


def _exchange_parts(n_swap, with_pack):
    def copies(refs, send_sems, recv_sems):
        x, y, c, _ = _place()
        me = 4 * x + 2 * y + c
        cps = [pltpu.make_async_remote_copy(
            src_ref=refs[2 * a], dst_ref=refs[2 * a + 1], send_sem=send_sems.at[a], recv_sem=recv_sems.at[a],
            device_id=(x, y, 1 - c), device_id_type=MESH) for a in range(n_swap)]
        if with_pack:
            pack_ref, packs = refs[2 * n_swap], refs[2 * n_swap + 1]
            for r in range(1, 8):
                dx, dy, dc = (r >> 2) & 1, (r >> 1) & 1, r & 1
                peer = (x + dx - 2 * x * dx, y + dy - 2 * y * dy, c + dc - 2 * c * dc)
                cps.append(pltpu.make_async_remote_copy(
                    src_ref=pack_ref, dst_ref=packs.at[me], send_sem=send_sems.at[n_swap + r - 1],
                    recv_sem=recv_sems.at[n_swap + r - 1], device_id=peer, device_id_type=MESH))
        return cps

    def issue(refs, send_sems, recv_sems):
        for cp in copies(refs, send_sems, recv_sems):
            cp.start()

    def await_(refs, send_sems, recv_sems):
        cps = copies(refs, send_sems, recv_sems)
        for cp in cps:
            cp.wait_recv()
        for cp in cps:
            cp.wait_send()

    return issue, await_, n_swap + (7 if with_pack else 0)


def _sum_packs(pack, packs):
    x, y, c = lax.axis_index("x"), lax.axis_index("y"), lax.axis_index("c")
    me = jnp.reshape(4 * x + 2 * y + c, (1,)).astype(jnp.int32)

    def body(me_ref, own_ref, p_ref, o_ref):
        acc = jnp.where(me_ref[0] == 0, own_ref[...], p_ref[0])
        for d in range(1, 8):
            acc = acc + jnp.where(me_ref[0] == d, own_ref[...], p_ref[d])
        o_ref[...] = acc

    full = lambda s: pl.BlockSpec(s.shape, lambda i, t: (0,) * s.ndim)
    return pl.pallas_call(
        body, name="sum_packs",
        grid_spec=pltpu.PrefetchScalarGridSpec(num_scalar_prefetch=1, grid=(1,), in_specs=[full(pack), full(packs)],
                                               out_specs=full(pack)),
        out_shape=jax.ShapeDtypeStruct(pack.shape, F32),
    )(me, pack, packs)


def _adamw_update(g, w_ref, m_ref, v_ref, go, do, mo, vo):
    c1 = 1.0 / (1.0 - ADAM_B1 ** ADAM_STEP)
    c2 = 1.0 / (1.0 - ADAM_B2 ** ADAM_STEP)
    mn = ADAM_B1 * m_ref[...] + (1.0 - ADAM_B1) * g
    vn = ADAM_B2 * v_ref[...] + (1.0 - ADAM_B2) * (g * g)
    go[...] = g
    mo[...] = mn
    vo[...] = vn
    do[...] = -ADAM_LR * ((mn * c1) / (jnp.sqrt(vn * c2) + ADAM_EPS) + ADAM_WD * w_ref[...])


def _adamw(w, m, v, g1, g2, rows, name):
    r, cdim = w.shape
    rows = min(rows, r)

    def body(w_ref, m_ref, v_ref, g1_ref, g2_ref, *outs):
        _adamw_update(g1_ref[...] + g2_ref[...], w_ref, m_ref, v_ref, *outs)

    blk = pl.BlockSpec((rows, cdim), lambda i: (i, 0))
    shp = jax.ShapeDtypeStruct((r, cdim), F32)
    return pl.pallas_call(
        body, name=name, grid=(r // rows,),
        in_specs=[blk] * 5, out_specs=[blk] * 4, out_shape=[shp] * 4,
        compiler_params=_params(("parallel",), 20 * rows * cdim * 4 + 8 * 2**20),
    )(*_in_hbm(w, m, v, g1, g2))


def _adamw_small(w_s, m_s, v_s, tot, g_cq, g_cw):
    cq_lanes, cw_lanes = g_cq.shape[1], g_cw.shape[1]
    shapes = [(1, GW), (1, 4, cq_lanes), (1, HEADS), (1, HEADS), (1, DH), (1, 3, cw_lanes), (1, GW), (GW,)]

    def body(t_ref, gcq_ref, gcw_ref, w_ref, m_ref, v_ref, *refs):
        g_scr, kinds = refs[len(refs) - 5], refs[len(refs) - 4:]
        g_scr[...] = jnp.zeros_like(g_scr)
        for src, dst in ((R_NIN, S_NIN), (R_CB, S_CB), (R_FN, S_FN), (R_AD, S_AD), (R_GN, S_GN)):
            g_scr[dst:dst + 1, :] = t_ref[src:src + 1, :]
        g_scr[S_CQ:S_CQ + 4, 0:cq_lanes] = gcq_ref[...]
        g_scr[S_CW:S_CW + 3, 0:cw_lanes] = gcw_ref[...]
        _adamw_update(g_scr[...], w_ref, m_ref, v_ref, *kinds)
        for kk, a in enumerate(kinds):
            nin, cq, al, dt, gn, cw, cb, fn = refs[8 * kk:8 * kk + 8]
            nin[...] = a[S_NIN:S_NIN + 1, :]
            cq[0] = a[S_CQ:S_CQ + 4, 0:cq_lanes]
            al[...] = a[S_AD:S_AD + 1, 0:HEADS]
            dt[...] = a[S_AD:S_AD + 1, HEADS:2 * HEADS]
            gn[...] = a[S_GN:S_GN + 1, 0:DH]
            cw[0] = a[S_CW:S_CW + 3, 0:cw_lanes]
            cb[...] = a[S_CB:S_CB + 1, :]
            fn[...] = a[S_FN, :]

    out = pl.pallas_call(
        body, name="adamw_small",
        out_shape=[jax.ShapeDtypeStruct(s, F32) for s in shapes] * 4,
        scratch_shapes=[pltpu.VMEM(w_s.shape, F32)] * 5,
    )(tot, g_cq, g_cw, w_s, m_s, v_s)
    return [out[8 * kk:8 * kk + 8] for kk in range(4)]


def _adamw_shard(wt, mt, vt, g1, g2):
    r, d = wt.shape
    cols = min(256, d)

    def body(w_ref, m_ref, v_ref, g_ref, g2_ref, go, do, mo, vo, pad_ref):
        chip = 2 * lax.axis_index("x") + lax.axis_index("y")
        back = [(ALIGNED_W - s) % ALIGNED_W for s in SHIFTS]
        pad_ref[...] = pltpu.roll(g_ref[...] + g2_ref[...], _by_chip(chip, back), 0)
        outs = [o.at[:, 0, :] for o in (go, do, mo, vo)]
        _adamw_update(pad_ref[0:r, :], w_ref, m_ref, v_ref, *outs)

    blk = pl.BlockSpec((r, cols), lambda i: (0, i))
    gblk = pl.BlockSpec((ALIGNED_W, cols), lambda i: (0, i))
    oblk = pl.BlockSpec((r, 1, cols), lambda i: (0, 0, i))
    shp = jax.ShapeDtypeStruct((r, 1, d), F32)
    return pl.pallas_call(
        body, name="adamw_w_in", grid=(d // cols,),
        in_specs=[blk] * 3 + [gblk] * 2, out_specs=[oblk] * 4, out_shape=[shp] * 4,
        scratch_shapes=[pltpu.VMEM((ALIGNED_W, cols), F32)],
        compiler_params=_params(("parallel",), 24 * ALIGNED_W * cols * 4 + 8 * 2**20),
    )(wt, mt, vt, g1, g2)


def _pad_lanes(a, width):
    return jnp.pad(a, ((0, 0), (0, width - a.shape[1])))


def _gathered_to_full(g):
    return jnp.transpose(g, (1, 0, 2)).reshape(g.shape[1], 4 * g.shape[2])


def _row(a):
    return _pad_lanes(a.reshape(1, -1), 1024)


S_NIN, S_CB, S_FN, S_AD, S_GN, S_CQ, S_CW = 0, 1, 2, 3, 4, 5, 9


def _small_pack(nin, cb, fn, al, dt, gn, cqw_shard, cw_shard):
    ad = jnp.concatenate([al.reshape(1, -1), dt.reshape(1, -1)], axis=1)
    rows = [_row(nin), _row(cb), _row(fn), _row(ad), _row(gn), _pad_lanes(cqw_shard, 1024),
            _pad_lanes(cw_shard, 1024)]
    out = jnp.concatenate(rows, axis=0)
    return jnp.pad(out, ((0, 16 - out.shape[0]), (0, 0)))


def kernel(x, norm_in_w, w_in, conv_qkv_w, A_log, dt_bias, gdn_norm_w, conv_w, conv_b, w_out, final_norm_w, loss_target, m_norm_in_w, m_w_in, m_conv_qkv_w, m_A_log, m_dt_bias, m_gdn_norm_w, m_conv_w, m_conv_b, m_w_out, m_final_norm_w, v_norm_in_w, v_w_in, v_conv_qkv_w, v_A_log, v_dt_bias, v_gdn_norm_w, v_conv_w, v_conv_b, v_w_out, v_final_norm_w):
    chip = 2 * lax.axis_index("x") + lax.axis_index("y")
    a_shard = _align_shard(jnp.transpose(w_in, (2, 0, 1)))
    d_model = x.shape[-1]
    stack = lambda s: lax.empty((4,) + s.shape, s.dtype)
    wg0 = lax.empty((WG_BLOCKS * DH, d_model), BF16)
    wc0 = lax.empty((WC_BLOCKS * DH, d_model), BF16)
    ss_g, rs_g, bufs_g, tok_g = _gather_start("g", a_shard, wg0, [(conv_qkv_w[0], stack(conv_qkv_w[0]))])
    wo_b = _cast_bf16(w_out[0], 256, "cast_w_out", tok_g)
    ss_c, rs_c, bufs_c, tok_c = _gather_start("c", bufs_g[0], wc0,
                                              [(conv_w[0], stack(conv_w[0])), (wo_b, stack(wo_b))])
    wg1, wc1, wog1, cqg1, cwg1 = _place_own(bufs_c[0], bufs_c[4], bufs_g[2], bufs_c[2],
                                            [bufs_g[1], bufs_c[1], bufs_c[5], bufs_g[3], bufs_c[3]])
    x0 = x[0]
    h = _rms_in(x0, _tie(_tie(norm_in_w, tok_g, "after_gather_start_g"), tok_c, "after_gather_start_c"))
    adam_in = [jnp.transpose(a[0]) for a in (w_in, m_w_in, v_w_in)]
    sp = lambda nin, cb, fn, al, dt, gn, cq, cwv: _small_pack(nin, cb, fn, al, dt, gn, cq[0], cwv[0])
    w_s = sp(norm_in_w, conv_b, final_norm_w, A_log, dt_bias, gdn_norm_w, conv_qkv_w, conv_w)
    m_s = sp(m_norm_in_w, m_conv_b, m_final_norm_w, m_A_log, m_dt_bias, m_gdn_norm_w, m_conv_qkv_w, m_conv_w)
    v_s = sp(v_norm_in_w, v_conv_b, v_final_norm_w, v_A_log, v_dt_bias, v_gdn_norm_w, v_conv_qkv_w, v_conv_w)
    a_thru, wg, _, cq_g = _gather_wait("g", ss_g, rs_g, [bufs_c[0], wg1, bufs_g[2], cqg1],
                                       [h, w_s, m_s, v_s] + adam_in[1:])
    w_g = _merge_edges(_sibling_forward("g", wg), G_EDGE, G_MIXED, "merge_edges_g")
    cqw = _gathered_to_full(cq_g)
    ad = jnp.pad(jnp.concatenate([A_log, dt_bias], axis=0), ((0, 0), (A_LANE, 0)))
    fwd_c = {}

    def on_q(q):
        _, wc, _, cw_g, _, wo_g = _gather_wait("c", ss_c, rs_c,
                                               [a_thru, wc1, bufs_c[2], cwg1, bufs_c[4], wog1], q)
        issue, _ = _sibling_forward_parts("c")
        ss, rs, (wc,), tok = _split_start("sibling_forward_start_c", issue, [wc], 3)
        fwd_c.update(ss=ss, rs=rs, wc=wc, cw_g=cw_g, wo_g=wo_g)
        return _tie(q, tok, "after_sibling_forward_start_c")

    def late(o):
        _, await_ = _sibling_forward_parts("c")
        (wc,) = _split_wait("sibling_forward_wait_c", await_, fwd_c["ss"], fwd_c["rs"], [fwd_c["wc"]], o)
        return (_merge_edges(wc, C_EDGE, C_MIXED, "merge_edges_c"), fwd_c["wo_g"].reshape(2 * GW, d_model),
                _gathered_to_full(fwd_c["cw_g"]))

    scat = {}

    def on_grad_c(g_c, g_wout, do):
        go4 = g_wout.reshape(4, GW // 2, d_model)
        land = lax.empty((3, ALIGNED_W, d_model), BF16)
        land_o = lax.empty((3, GW // 2, d_model), BF16)
        ss, rs, bufs, tok = _scatter_start("c", g_c, land, [(go4, land_o)])
        scat["c"] = (ss, rs, bufs)
        return _tie(do, tok, "after_scatter_start_c")

    def on_grad_g(g_g, dproj_g):
        ss, rs, bufs, tok = _scatter_start("g", _pair_reduce("g", g_g), scat["c"][2][1], [], halved=True)
        scat["g"] = (ss, rs, bufs)
        return _tie(dproj_g, tok, "after_scatter_start_g")

    gx, sm, _ = _local_step(x0, loss_target[0], h, w_g, cqw, late, norm_in_w, ad, gdn_norm_w, conv_b,
                            final_norm_w.reshape(1, -1), on_grad_c, on_grad_g, on_q)

    ss, rs, bufs = scat["c"]
    g_c, land, go4, land_o = _scatter_wait("c", ss, rs, [bufs[0], scat["g"][2][1], bufs[2], bufs[3]], gx)
    part_out = _sum_rows(go4, land_o, 128)
    ad_g = jnp.concatenate([sm["al"][:, A_LANE:], sm["dt"][:, A_LANE:]], axis=1)
    pack = jnp.concatenate([_row(sm["nin"]), _row(sm["cb"]), _row(sm["fn"]), _row(ad_g), _row(sm["gn"]),
                            jnp.concatenate(sm["cq"], axis=1).reshape(12, 1024), sm["cw"], _row(sm["loss"])], axis=0)
    pack = jnp.pad(pack, ((0, PACK_ROWS - pack.shape[0]), (0, 0)))
    issue, await_a, nsem = _exchange_parts(1, True)
    ss_a, rs_a, bufs_a, tok_a = _split_start(
        "exchange_start_small", issue,
        [part_out, lax.empty(part_out.shape, F32), pack, lax.empty((8,) + pack.shape, F32)], nsem)
    ss, rs, bufs = scat["g"]
    g_g, land = _scatter_wait("g", ss, rs, [bufs[0], land], [gx, tok_a], halved=True)
    part_in = _sum_shard(g_g, g_c, land)
    issue, await_b, nsem = _exchange_parts(1, False)
    ss_b, rs_b, bufs_b, tok_b = _split_start("exchange_start_w_in", issue,
                                             [part_in, lax.empty(part_in.shape, F32)], nsem)
    part_out, sib_out, pack, packs = _split_wait("exchange_wait_small", await_a, ss_a, rs_a, bufs_a, tok_b)
    tot = _sum_packs(pack, packs)
    g_wo, d_wo, m_wo, v_wo = _adamw(w_out[0], m_w_out[0], v_w_out[0], part_out, sib_out, 128, "adamw_w_out")
    g_cq_sh = lax.dynamic_slice_in_dim(tot[R_CQ:R_CQ + 12].reshape(4, 3 * GW), chip * 768, 768, axis=1)
    g_cw_sh = lax.dynamic_slice_in_dim(tot[R_CW:R_CW + 3], chip * 256, 256, axis=1)
    small = _adamw_small(w_s, m_s, v_s, tot, g_cq_sh, g_cw_sh)
    part_in, sib_in = _split_wait("exchange_wait_w_in", await_b, ss_b, rs_b, bufs_b, [small[0][0], d_wo])
    g_wi, d_wi, m_wi, v_wi = [jnp.transpose(a, (1, 2, 0))[0] for a in _adamw_shard(*adam_in, part_in, sib_in)]

    def unpack(leaves, big_in, big_out):
        nin, cq, al, dt, gn, cw, cb, fn = leaves
        return (nin, big_in[None], cq, al, dt, gn, cw, cb, big_out[None], fn)

    loss = tot[R_LOSS, 0]
    return (loss, gx[None], *unpack(small[0], g_wi, g_wo), *unpack(small[1], d_wi, d_wo),
            *unpack(small[2], m_wi, m_wo), *unpack(small[3], v_wi, v_wo))
```

```python
import jax
import jax.numpy as jnp
from jax import lax
from jax.experimental import pallas as pl
from jax.experimental.pallas import tpu as pltpu

F32 = jnp.float32
BF16 = jnp.bfloat16
MESH = pl.DeviceIdType.MESH
ANY = pl.BlockSpec(memory_space=pl.ANY)

HEADS = 8
DH = 128
CH = 64
GW = HEADS * DH
EPS = 1e-6
VMEM_V7X = 64 * 1024 * 1024

QB, KB, VB, ZB, BAB = 0, 8, 16, 24, 32
A_LANE = 120
NG, NC = 33, 32
GW_COLS, CW_COLS = NG * DH, NC * DH

SHARD_W = 2052
ALIGNED_BLOCKS = 17
ALIGNED_W = ALIGNED_BLOCKS * DH
SHIFTS = (0, 4, ALIGNED_W - 8, ALIGNED_W - 4)
G_EDGE, C_EDGE = 34, 32
G_SPARE, C_SPARE = 33, 34
WG_BLOCKS, WC_BLOCKS = 38, 36
G_MIXED, C_MIXED = (2, BAB), (4 * 7 + 1,)


def _shard_blocks(chip, edges):
    g, c = "g", "c"
    if chip == 0:
        out = [(g, 3 * b) for b in range(8)] + [(g, 3 * b + 1) for b in range(8)] + [(g, G_EDGE, G_MIXED[0])]
    elif chip == 1:
        out = [(g, G_EDGE + 1, G_MIXED[0])] + [(g, 3 * b + 2) for b in range(1, 8)]
        out += [(g, ZB + b) for b in range(8)] + [(g, G_EDGE + 2, G_MIXED[1])]
    elif chip == 2:
        out = [(c, 4 * b) for b in range(8)] + [(c, 4 * b + 1) for b in range(7)]
        out += [(c, C_EDGE, C_MIXED[0]), (g, G_EDGE + 3, G_MIXED[1])]
    else:
        out = [(c, 4 * b + 2) for b in range(8)] + [(c, 4 * b + 3) for b in range(8)] + [(c, C_EDGE + 1, C_MIXED[0])]
    return [(o[0], o[1] if (edges or len(o) == 2) else o[2]) for o in out]


def _by_chip(chip, vals):
    if all(v == vals[0] for v in vals):
        return vals[0]
    r = vals[3]
    for kk in (2, 1, 0):
        r = jnp.where(chip == kk, vals[kk], r)
    return r

ADAM_LR, ADAM_B1, ADAM_B2, ADAM_EPS, ADAM_WD, ADAM_STEP = 0.001, 0.9, 0.999, 1e-08, 0.01, 10

R_NIN, R_CB, R_FN, R_AD, R_GN, R_CQ, R_CW, R_LOSS, PACK_ROWS = 0, 1, 2, 3, 4, 5, 17, 20, 24

NN = ((1,), (0,))
NT = ((1,), (1,))
TN = ((0,), (0,))


def _dot(a, b, dims=NN, mode="lo"):
    dn = (dims, ((), ()))
    if mode == "hi":
        return lax.dot_general(a, b, dn, precision=lax.Precision.HIGHEST, preferred_element_type=F32)
    ah, bh = a.astype(BF16), b.astype(BF16)
    out = lax.dot_general(ah, bh, dn, preferred_element_type=F32)
    if mode == "x3":
        al = (a - ah.astype(F32)).astype(BF16)
        bl = (b - bh.astype(F32)).astype(BF16)
        out = out + lax.dot_general(ah, bl, dn, preferred_element_type=F32)
        out = out + lax.dot_general(al, bh, dn, preferred_element_type=F32)
    return out


P_GRAM, P_INV, P_SOL, P_SCAN, P_SCANB, P_BWD = "lo", "lo", "lo", "lo", "lo", "lo"
P_CUM = "x3"


def _params(sem=None, vmem=None):
    kw = {}
    if sem is not None:
        kw["dimension_semantics"] = sem
    if vmem is not None:
        kw["vmem_limit_bytes"] = int(min(max(vmem, 32 * 2**20), VMEM_V7X - 8 * 2**20))
    return pltpu.CompilerParams(**kw)


def _in_hbm(*arrays):
    return [pltpu.with_memory_space_constraint(a, pltpu.HBM) for a in arrays]


def _sigmoid(x):
    return 1.0 / (1.0 + jnp.exp(-x))


def _dsilu(x, s):
    return s * (1.0 + x * (1.0 - s))


def _rows(shape):
    return lax.broadcasted_iota(jnp.int32, shape, 0)


def _shift_down(x, s):
    if s == 0:
        return x
    return jnp.where(_rows(x.shape) >= s, pltpu.roll(x, s, 0), 0.0)


def _shift_up(x, s):
    if s == 0:
        return x
    n = x.shape[0]
    return jnp.where(_rows(x.shape) < n - s, pltpu.roll(x, n - s, 0), 0.0)


def _matmul(a, b, dims, out_dtype, tm, tn, tk, name, add=None, n=None, b_outer=False):
    if dims == NN:
        (m, k), n = a.shape, b.shape[1]
    elif dims == NT:
        (m, k), n = a.shape, (n or b.shape[0])
    else:
        (k, m), n = a.shape, b.shape[1]
    tm, tn, tk = min(tm, m), min(tn, n), min(tk, k)
    assert m % tm == 0 and n % tn == 0 and k % tk == 0, (name, m, n, k, tm, tn, tk)
    nk = k // tk

    def body(*refs):
        if add is None:
            a_ref, b_ref, o_ref = refs[:3]
            add_ref = None
        else:
            a_ref, b_ref, add_ref, o_ref = refs[:4]
        part = _dot(a_ref[...], b_ref[...], dims)
        if nk == 1:
            if add_ref is not None:
                part = part + add_ref[...]
            o_ref[...] = part.astype(out_dtype)
            return
        acc = refs[-1]
        kk = pl.program_id(2)

        @pl.when(kk == 0)
        def _():
            acc[...] = part

        @pl.when(kk > 0)
        def _():
            acc[...] += part

        @pl.when(kk == nk - 1)
        def _():
            r = acc[...]
            if add_ref is not None:
                r = r + add_ref[...]
            o_ref[...] = r.astype(out_dtype)

    ij = (lambda g0, g1: (g1, g0)) if b_outer else (lambda g0, g1: (g0, g1))

    def spec(shape, pick):
        return pl.BlockSpec(shape, lambda g0, g1, kk: pick(*ij(g0, g1), kk))

    a_spec = spec((tk, tm), lambda i, j, kk: (kk, i)) if dims == TN else spec((tm, tk), lambda i, j, kk: (i, kk))
    b_spec = spec((tn, tk), lambda i, j, kk: (j, kk)) if dims == NT else spec((tk, tn), lambda i, j, kk: (kk, j))
    o_spec = spec((tm, tn), lambda i, j, kk: (i, j))
    in_specs = [a_spec, b_spec]
    args = [a, b]
    if add is not None:
        in_specs.append(o_spec)
        args.append(add)
    osz = jnp.dtype(out_dtype).itemsize
    est = 2 * (tm * tk * a.dtype.itemsize + tk * tn * b.dtype.itemsize + tm * tn * osz)
    est += 3 * tm * tn * 4 + (2 * tm * tn * 4 if add is not None else 0)
    return pl.pallas_call(
        body, name=name, grid=(n // tn, m // tm, nk) if b_outer else (m // tm, n // tn, nk),
        in_specs=in_specs, out_specs=o_spec,
        out_shape=jax.ShapeDtypeStruct((m, n), out_dtype),
        scratch_shapes=[pltpu.VMEM((tm, tn), F32)] if nk > 1 else [],
        compiler_params=_params(("parallel", "parallel", "arbitrary"), est + 8 * 2**20),
    )(*args)


def _cast_bf16(a, rows, name, after):
    r, c = a.shape
    rows = min(rows, r)

    def body(a_ref, t_ref, o_ref):
        del t_ref
        o_ref[...] = a_ref[...].astype(BF16)

    return pl.pallas_call(
        body, name=name, grid=(r // rows,),
        in_specs=[pl.BlockSpec((rows, c), lambda i: (i, 0)), ANY],
        out_specs=pl.BlockSpec((rows, c), lambda i: (i, 0)),
        out_shape=jax.ShapeDtypeStruct((r, c), BF16),
        compiler_params=_params(("parallel",)),
    )(a, after)


def _align_shard(wt):
    r, _, d = wt.shape
    cols = min(256, d)

    def body(w_ref, o_ref, pad_ref):
        chip = 2 * lax.axis_index("x") + lax.axis_index("y")
        pad_ref[...] = jnp.zeros_like(pad_ref)
        pad_ref[0:r, :] = w_ref[:, 0, :]
        o_ref[...] = pltpu.roll(pad_ref[...], _by_chip(chip, SHIFTS), 0).astype(BF16)

    return pl.pallas_call(
        body, name="align_shard", grid=(d // cols,),
        in_specs=[pl.BlockSpec((r, 1, cols), lambda i: (0, 0, i))],
        out_specs=pl.BlockSpec((ALIGNED_W, cols), lambda i: (0, i)),
        out_shape=jax.ShapeDtypeStruct((ALIGNED_W, d), BF16),
        scratch_shapes=[pltpu.VMEM((ALIGNED_W, cols), F32)],
        compiler_params=_params(("parallel",)),
    )(wt)


def _rms_in(x, w):
    n, d = x.shape
    tr = min(256, n)

    def body(x_ref, w_ref, h_ref):
        xv = x_ref[...]
        r = lax.rsqrt(jnp.mean(xv * xv, axis=-1, keepdims=True) + EPS)
        h_ref[...] = (xv * r * w_ref[...]).astype(BF16)

    return pl.pallas_call(
        body, name="rms_in", grid=(n // tr,),
        in_specs=[pl.BlockSpec((tr, d), lambda i: (i, 0)), pl.BlockSpec((1, d), lambda i: (0, 0))],
        out_specs=pl.BlockSpec((tr, d), lambda i: (i, 0)),
        out_shape=jax.ShapeDtypeStruct((n, d), BF16),
        compiler_params=_params(("parallel",)),
    )(x, w)


def _conv_silu(p, w_ref, taps):
    c = None
    for j in range(taps):
        t = _shift_down(p, taps - 1 - j) * w_ref[j:j + 1, :]
        c = t if c is None else c + t
    return c


def _prep_qkv(proj, cw):
    n = proj.shape[0]

    def body(p3, wq, wk, wv, q_ref, k_ref, v_ref):
        for kind, (w_ref, o_ref) in enumerate(((wq, q_ref), (wk, k_ref), (wv, v_ref))):
            c = _conv_silu(p3[:, kind * DH:(kind + 1) * DH], w_ref, 4)
            a = c * _sigmoid(c)
            if kind < 2:
                r = lax.rsqrt(jnp.sum(a * a, axis=-1, keepdims=True) + EPS)
                a = a * (r * (DH ** -0.5 if kind == 0 else 1.0))
            o_ref[...] = a

    col = pl.BlockSpec((n, DH), lambda h: (0, h))
    wcol = lambda base: pl.BlockSpec((4, DH), lambda h: (0, base + h))
    out = jax.ShapeDtypeStruct((n, GW), F32)
    return pl.pallas_call(
        body, name="prep_qkv", grid=(HEADS,),
        in_specs=[pl.BlockSpec((n, 3 * DH), lambda h: (0, h)), wcol(QB), wcol(KB), wcol(VB)],
        out_specs=[col] * 3, out_shape=[out] * 3,
        compiler_params=_params(("parallel",), 40 * 2**20),
    )(proj, cw, cw, cw)


def _prep_qkv_bwd(proj, cw, dq, dk, dv, dproj):
    n = proj.shape[0]

    def body(p3, wq, wk, wv, dq_ref, dk_ref, dv_ref, _, o3, gq, gk, gv):
        for kind, (w_ref, d_ref, g_ref) in enumerate(((wq, dq_ref, gq), (wk, dk_ref, gk), (wv, dv_ref, gv))):
            p = p3[:, kind * DH:(kind + 1) * DH]
            shifted = [_shift_down(p, 3 - j) for j in range(4)]
            c = shifted[0] * w_ref[0:1, :]
            for j in range(1, 4):
                c = c + shifted[j] * w_ref[j:j + 1, :]
            s = _sigmoid(c)
            a = c * s
            d = d_ref[...]
            if kind < 2:
                r = lax.rsqrt(jnp.sum(a * a, axis=-1, keepdims=True) + EPS)
                sc = DH ** -0.5 if kind == 0 else 1.0
                d = (sc * r) * (d - a * ((r * r) * jnp.sum(d * a, axis=-1, keepdims=True)))
            dc = d * _dsilu(c, s)
            dp = None
            for j in range(4):
                g_ref[j:j + 1, :] = jnp.sum(dc * shifted[j], axis=0, keepdims=True)
                t = _shift_up(dc, 3 - j) * w_ref[j:j + 1, :]
                dp = t if dp is None else dp + t
            o3[:, kind * DH:(kind + 1) * DH] = dp.astype(BF16)

    col = pl.BlockSpec((n, DH), lambda h: (0, h))
    wcol = lambda base: pl.BlockSpec((4, DH), lambda h: (0, base + h))
    p3spec = pl.BlockSpec((n, 3 * DH), lambda h: (0, h))
    return pl.pallas_call(
        body, name="prep_qkv_bwd", grid=(HEADS,),
        in_specs=[p3spec, wcol(QB), wcol(KB), wcol(VB), col, col, col, ANY],
        out_specs=[p3spec] + [wcol(0)] * 3,
        out_shape=[jax.ShapeDtypeStruct(dproj.shape, BF16)] + [jax.ShapeDtypeStruct((4, GW), F32)] * 3,
        input_output_aliases={7: 0},
        compiler_params=_params(("parallel",), 48 * 2**20),
    )(proj, cw, cw, cw, dq, dk, dv, dproj)


CPB = 8
SCAN_CPS = 4


def _tri(lower, rows):
    i = lax.broadcasted_iota(jnp.int32, (rows, rows), 0)
    j = lax.broadcasted_iota(jnp.int32, (rows, rows), 1)
    return jnp.where((i // CH == j // CH) & ((i >= j) if lower else (j >= i)), 1.0, 0.0)


def _lane(shape):
    return lax.broadcasted_iota(jnp.int32, shape, 1)


def _prep_bg(proj, ad):
    n = proj.shape[0]
    nch = n // CH
    cpb = CPB if nch % CPB == 0 else 1
    rows = cpb * CH

    def body(p_ref, ad_ref, bg_ref, bgt_ref):
        p = p_ref[...]
        lane = _lane(p.shape)
        beta = _sigmoid(p)
        xa = p + ad_ref[1:2, :]
        sp = jnp.maximum(xa, 0.0) + jnp.log(1.0 + jnp.exp(-jnp.abs(xa)))
        g = pltpu.roll(-jnp.exp(ad_ref[0:1, :]) * sp, DH - A_LANE + HEADS, 1)
        gc = _dot(_tri(True, rows), g, NN, P_CUM)
        bg = jnp.where(lane < HEADS, beta, jnp.where(lane < 2 * HEADS, gc, 0.0))
        bg_ref[...] = bg
        for ci in range(cpb):
            bgt_ref[ci] = bg[ci * CH:(ci + 1) * CH, :].T

    return pl.pallas_call(
        body, name="prep_bg", grid=(nch // cpb,),
        in_specs=[pl.BlockSpec((rows, DH), lambda i: (i, BAB)), pl.BlockSpec((2, DH), lambda i: (0, 0))],
        out_specs=[pl.BlockSpec((rows, DH), lambda i: (i, 0)), pl.BlockSpec((cpb, DH, CH), lambda i: (i, 0, 0))],
        out_shape=[jax.ShapeDtypeStruct((n, DH), F32), jax.ShapeDtypeStruct((nch, DH, CH), F32)],
        compiler_params=_params(("parallel",)),
    )(*_in_hbm(proj, ad))


def _prep_bg_bwd(proj, ad, dbg, dproj):
    n = proj.shape[0]
    nch = n // CH
    cpb = CPB if nch % CPB == 0 else 1
    rows = cpb * CH

    def body(p_ref, ad_ref, d_ref, _, o_ref, ga_ref, gd_ref):
        p = p_ref[...]
        d = d_ref[...]
        lane = _lane(p.shape)
        beta = _sigmoid(p)
        xa = p + ad_ref[1:2, :]
        sp = jnp.maximum(xa, 0.0) + jnp.log(1.0 + jnp.exp(-jnp.abs(xa)))
        na = -jnp.exp(ad_ref[0:1, :])
        dg = pltpu.roll(_dot(_tri(False, rows), d, NN, P_CUM), A_LANE - HEADS, 1)
        da = dg * na * _sigmoid(xa)
        is_g = lane >= A_LANE
        o_ref[...] = jnp.where(lane < HEADS, d * beta * (1.0 - beta), jnp.where(is_g, da, 0.0)).astype(BF16)
        ga = jnp.sum(jnp.where(is_g, dg * na * sp, 0.0), axis=0, keepdims=True)
        gd = jnp.sum(jnp.where(is_g, da, 0.0), axis=0, keepdims=True)

        @pl.when(pl.program_id(0) == 0)
        def _():
            ga_ref[...] = jnp.zeros_like(ga_ref)
            gd_ref[...] = jnp.zeros_like(gd_ref)

        ga_ref[...] += ga
        gd_ref[...] += gd

    one = pl.BlockSpec((1, DH), lambda i: (0, 0))
    return pl.pallas_call(
        body, name="prep_bg_bwd", grid=(nch // cpb,),
        in_specs=[pl.BlockSpec((rows, DH), lambda i: (i, BAB)), pl.BlockSpec((2, DH), lambda i: (0, 0)),
                  pl.BlockSpec((rows, DH), lambda i: (i, 0)), ANY],
        out_specs=[pl.BlockSpec((rows, DH), lambda i: (i, BAB)), one, one],
        out_shape=[jax.ShapeDtypeStruct(dproj.shape, BF16), jax.ShapeDtypeStruct((1, DH), F32),
                   jax.ShapeDtypeStruct((1, DH), F32)],
        input_output_aliases={3: 0},
        compiler_params=_params(("arbitrary",)),
    )(proj, ad, dbg, dproj)


def _gdn_out(o, proj, wg):
    n = o.shape[0]

    def body(o_ref, z_ref, w_ref, y_ref):
        ov, z = o_ref[...], z_ref[...]
        r = lax.rsqrt(jnp.mean(ov * ov, axis=-1, keepdims=True) + EPS)
        y_ref[...] = (ov * r * w_ref[...] * (z * _sigmoid(z))).astype(BF16)

    return pl.pallas_call(
        body, name="gdn_out", grid=(HEADS,),
        in_specs=[pl.BlockSpec((n, DH), lambda h: (0, h)), pl.BlockSpec((n, DH), lambda h: (0, ZB + h)),
                  pl.BlockSpec((1, DH), lambda h: (0, 0))],
        out_specs=pl.BlockSpec((n, DH), lambda h: (0, h)),
        out_shape=jax.ShapeDtypeStruct((n, 2 * GW), BF16),
        compiler_params=_params(("parallel",)),
    )(o, proj, wg)


def _gdn_out_bwd(o, proj, wg, dout_b, w_out):
    n = o.shape[0]
    d_model = dout_b.shape[1]

    def body(o_ref, z_ref, w_ref, g_ref, wo_ref, do_ref, dz_ref, gw_ref):
        ov, z, w = o_ref[...], z_ref[...], w_ref[...]
        d = _dot(g_ref[...], wo_ref[...], NT)
        r = lax.rsqrt(jnp.mean(ov * ov, axis=-1, keepdims=True) + EPS)
        nrm = ov * r
        s = _sigmoid(z)
        dz_ref[...] = (d * (nrm * w) * _dsilu(z, s)).astype(BF16)
        dn_w = d * (z * s)
        gw = jnp.sum(dn_w * nrm, axis=0, keepdims=True)
        dn = dn_w * w
        do_ref[...] = (r * (dn - nrm * jnp.mean(dn * nrm, axis=-1, keepdims=True))).astype(BF16)

        @pl.when(pl.program_id(0) == 0)
        def _():
            gw_ref[...] = jnp.zeros_like(gw_ref)

        gw_ref[...] += gw

    return pl.pallas_call(
        body, name="gdn_out_bwd", grid=(HEADS,),
        in_specs=[pl.BlockSpec((n, DH), lambda h: (0, h)), pl.BlockSpec((n, DH), lambda h: (0, ZB + h)),
                  pl.BlockSpec((1, DH), lambda h: (0, 0)), pl.BlockSpec((n, d_model), lambda h: (0, 0)),
                  pl.BlockSpec((DH, d_model), lambda h: (h, 0))],
        out_specs=[pl.BlockSpec((n, DH), lambda h: (0, h)), pl.BlockSpec((n, DH), lambda h: (0, ZB + h)),
                   pl.BlockSpec((1, DH), lambda h: (0, 0))],
        out_shape=[jax.ShapeDtypeStruct((n, GW), BF16), jax.ShapeDtypeStruct((n, GW_COLS), BF16),
                   jax.ShapeDtypeStruct((1, DH), F32)],
        compiler_params=_params(("arbitrary",), 40 * 2**20),
    )(o, proj, wg, dout_b, w_out)


def _conv_branch(proj, w3, b, mix):
    n = proj.shape[0]

    def body(p4, w_ref, b_ref, _, y_ref):
        u = p4[:, DH:2 * DH] * p4[:, 2 * DH:3 * DH]
        cc = _conv_silu(u, w_ref, 3) + b_ref[...]
        z = p4[:, 3 * DH:4 * DH]
        y_ref[...] = (p4[:, 0:DH] * cc * (z * _sigmoid(z))).astype(BF16)

    return pl.pallas_call(
        body, name="conv_branch", grid=(HEADS,),
        in_specs=[pl.BlockSpec((n, 4 * DH), lambda h: (0, h)), pl.BlockSpec((3, DH), lambda h: (0, h)),
                  pl.BlockSpec((1, DH), lambda h: (0, h)), ANY],
        out_specs=pl.BlockSpec((n, DH), lambda h: (0, HEADS + h)),
        out_shape=jax.ShapeDtypeStruct(mix.shape, BF16),
        input_output_aliases={3: 0},
        compiler_params=_params(("parallel",), 40 * 2**20),
    )(*_in_hbm(proj, w3, b, mix))


def _conv_branch_bwd(proj, w3, b, dout_b, w_out):
    n = proj.shape[0]
    d_model = dout_b.shape[1]

    def body(p4, w_ref, b_ref, g_ref, wo_ref, o4, gw_ref, gbias_ref):
        gb, gcv, hc, z = p4[:, 0:DH], p4[:, DH:2 * DH], p4[:, 2 * DH:3 * DH], p4[:, 3 * DH:4 * DH]
        d = _dot(g_ref[...], wo_ref[...], NT)
        dgb, dgc, dhc, dzc = (o4.at[:, kk * DH:(kk + 1) * DH] for kk in range(4))
        u = gcv * hc
        cc = _conv_silu(u, w_ref, 3) + b_ref[...]
        s = _sigmoid(z)
        dzc[...] = (d * (gb * cc) * _dsilu(z, s)).astype(BF16)
        dp = d * (z * s)
        dgb[...] = (dp * cc).astype(BF16)
        dcc = dp * gb
        gbias_ref[...] = jnp.sum(dcc, axis=0, keepdims=True)
        du = None
        for j in range(3):
            gw_ref[j:j + 1, :] = jnp.sum(dcc * _shift_down(u, 2 - j), axis=0, keepdims=True)
            t = _shift_up(dcc, 2 - j) * w_ref[j:j + 1, :]
            du = t if du is None else du + t
        dgc[...] = (du * hc).astype(BF16)
        dhc[...] = (du * gcv).astype(BF16)

    p4spec = pl.BlockSpec((n, 4 * DH), lambda h: (0, h))
    return pl.pallas_call(
        body, name="conv_branch_bwd", grid=(HEADS,),
        in_specs=[p4spec, pl.BlockSpec((3, DH), lambda h: (0, h)), pl.BlockSpec((1, DH), lambda h: (0, h)),
                  pl.BlockSpec((n, d_model), lambda h: (0, 0)), pl.BlockSpec((DH, d_model), lambda h: (HEADS + h, 0))],
        out_specs=[p4spec, pl.BlockSpec((3, DH), lambda h: (0, h)), pl.BlockSpec((1, DH), lambda h: (0, h))],
        out_shape=[jax.ShapeDtypeStruct((n, CW_COLS), BF16), jax.ShapeDtypeStruct((3, GW), F32),
                   jax.ShapeDtypeStruct((1, GW), F32)],
        compiler_params=_params(("parallel",), 52 * 2**20),
    )(proj, w3, b, dout_b, w_out)


def _out_loss(mix, w_out, x, tgt, wf):
    n, d = x.shape
    kdim = mix.shape[1]
    tr = min(256, n)

    def body(m_ref, wo_ref, x_ref, t_ref, w_ref, do_ref, dob_ref, gw_ref, loss_ref):
        ov = _dot(m_ref[...], wo_ref[...], NN) + x_ref[...]
        w = w_ref[...]
        r = lax.rsqrt(jnp.mean(ov * ov, axis=-1, keepdims=True) + EPS)
        nrm = ov * r
        e = nrm * w - t_ref[...]
        dy = e * (1.0 / d)
        dn = dy * w
        dout = r * (dn - nrm * jnp.mean(dn * nrm, axis=-1, keepdims=True))
        do_ref[...] = dout
        dob_ref[...] = dout.astype(BF16)

        @pl.when(pl.program_id(0) == 0)
        def _():
            gw_ref[...] = jnp.zeros_like(gw_ref)
            loss_ref[...] = jnp.zeros_like(loss_ref)

        gw_ref[...] += jnp.sum(dy * nrm, axis=0, keepdims=True)
        loss_ref[...] += (0.5 / d) * jnp.sum(jnp.sum(e * e, axis=-1, keepdims=True), axis=0, keepdims=True)

    row = pl.BlockSpec((tr, d), lambda i: (i, 0))
    return pl.pallas_call(
        body, name="out_loss", grid=(n // tr,),
        in_specs=[pl.BlockSpec((tr, kdim), lambda i: (i, 0)), pl.BlockSpec((kdim, d), lambda i: (0, 0)), row, row,
                  pl.BlockSpec((1, d), lambda i: (0, 0))],
        out_specs=[row, row, pl.BlockSpec((1, d), lambda i: (0, 0)), pl.BlockSpec((1, 1), lambda i: (0, 0))],
        out_shape=[jax.ShapeDtypeStruct((n, d), F32), jax.ShapeDtypeStruct((n, d), BF16),
                   jax.ShapeDtypeStruct((1, d), F32), jax.ShapeDtypeStruct((1, 1), F32)],
        compiler_params=_params(("arbitrary",), 40 * 2**20),
    )(mix, w_out, x, tgt, wf)


def _dh_rms_bwd(dproj, w_t, dh0, x, w, dout, tk):
    n, d = x.shape
    kdim = dproj.shape[1]
    tm = min(1024, n)
    tk = min(tk, kdim)
    nk = kdim // tk

    def body(a_ref, b_ref, dh0_ref, x_ref, w_ref, do_ref, dx_ref, gw_ref, acc):
        i, kk = pl.program_id(0), pl.program_id(1)
        part = _dot(a_ref[...], b_ref[...], NN)

        @pl.when(kk == 0)
        def _():
            acc[...] = part + dh0_ref[...]

        @pl.when(kk > 0)
        def _():
            acc[...] += part

        @pl.when((i == 0) & (kk == 0))
        def _():
            gw_ref[...] = jnp.zeros_like(gw_ref)

        @pl.when(kk == nk - 1)
        def _():
            xv, dhv = x_ref[...], acc[...]
            r = lax.rsqrt(jnp.mean(xv * xv, axis=-1, keepdims=True) + EPS)
            xn = xv * r
            dxn = dhv * w_ref[...]
            dx_ref[...] = r * (dxn - xn * jnp.mean(dxn * xn, axis=-1, keepdims=True)) + do_ref[...]
            gw_ref[...] += jnp.sum(dhv * xn, axis=0, keepdims=True)

    row = pl.BlockSpec((tm, d), lambda i, kk: (i, 0))
    one = pl.BlockSpec((1, d), lambda i, kk: (0, 0))
    return pl.pallas_call(
        body, name="dh_rms_bwd", grid=(n // tm, nk),
        in_specs=[pl.BlockSpec((tm, tk), lambda i, kk: (i, kk)), pl.BlockSpec((tk, d), lambda i, kk: (kk, 0)),
                  row, row, one, row],
        out_specs=[row, one],
        out_shape=[jax.ShapeDtypeStruct((n, d), F32), jax.ShapeDtypeStruct((1, d), F32)],
        scratch_shapes=[pltpu.VMEM((tm, d), F32)],
        compiler_params=_params(("arbitrary", "arbitrary"), 56 * 2**20),
    )(dproj, w_t, dh0, x, w, dout)


def _ij():
    i = lax.broadcasted_iota(jnp.int32, (CH, CH), 0)
    j = lax.broadcasted_iota(jnp.int32, (CH, CH), 1)
    return i, j


def _unit_lower_inverse(mats):
    i, j = _ij()
    eye = jnp.where(i == j, 1.0, 0.0)
    same16 = (i // 16) == (j // 16)
    same32 = (i // 32) == (j // 32)
    mm = lambda xs, ys: [_dot(x, y, NN, P_INV) for x, y in zip(xs, ys)]
    n1 = [jnp.where(same16, -a, 0.0) for a in mats]
    n2 = mm(n1, n1)
    n4 = mm(n2, n2)
    n8 = mm(n4, n4)
    t = [eye + x1 + x2 + x3 for x1, x2, x3 in zip(n1, n2, mm(n1, n2))]
    t = [x + y for x, y in zip(t, mm(t, n4))]
    t = [x + y for x, y in zip(t, mm(t, n8))]
    a1 = [jnp.where(same32 & jnp.logical_not(same16), a, 0.0) for a in mats]
    t = [x - y for x, y in zip(t, mm(t, mm(a1, t)))]
    a2 = [jnp.where(same32, 0.0, a) for a in mats]
    t = [x - y for x, y in zip(t, mm(t, mm(a2, t)))]
    return t


def _head_vectors(bg, bgt, h):
    bcol = bg[:, h:h + 1]
    gcol = bg[:, HEADS + h:HEADS + h + 1]
    grow = bgt[HEADS + h:HEADS + h + 1, :]
    return bcol, gcol, grow


def _decay(gcol, grow):
    i, j = _ij()
    return jnp.where(i >= j, jnp.exp(jnp.where(i >= j, gcol - grow, 0.0)), 0.0)


def _gdn_intra(q, k, v, bg, bgt):
    n = q.shape[0]
    nch = n // CH
    cps = 4 if nch % 4 == 0 else 1

    def body(q_ref, k_ref, v_ref, bg_ref, bgt_ref, u_ref, w_ref, p_ref, t_ref):
        i, j = _ij()
        items = [(ci, h) for ci in range(cps) for h in range(HEADS)]
        at = lambda ref, ci, h: ref.at[ci * CH:(ci + 1) * CH, h * DH:(h + 1) * DH]
        bgs = [bg_ref[ci * CH:(ci + 1) * CH, :] for ci in range(cps)]
        ks = [at(k_ref, ci, h)[...] for ci, h in items]
        vecs = [_head_vectors(bgs[ci], bgt_ref[ci], h) for ci, h in items]
        decs = [_decay(gcol, grow) for _, gcol, grow in vecs]
        kks = [_dot(kh, kh, NT, P_GRAM) for kh in ks]
        qks = [_dot(at(q_ref, ci, h)[...], kh, NT, P_GRAM) for (ci, h), kh in zip(items, ks)]
        ts = _unit_lower_inverse([jnp.where(i > j, bcol * kk * dec, 0.0)
                                  for (bcol, _, _), kk, dec in zip(vecs, kks, decs)])
        us = [_dot(t, at(v_ref, ci, h)[...] * bcol, NN, P_SOL) for t, (ci, h), (bcol, _, _) in zip(ts, items, vecs)]
        ws = [_dot(t, kh * (bcol * jnp.exp(gcol)), NN, P_SOL) for t, kh, (bcol, gcol, _) in zip(ts, ks, vecs)]
        for n_, (ci, h) in enumerate(items):
            p_ref[ci, h] = qks[n_] * decs[n_]
            t_ref[ci, h] = ts[n_].astype(BF16)
            at(u_ref, ci, h)[...] = us[n_]
            at(w_ref, ci, h)[...] = ws[n_].astype(BF16)

    row = pl.BlockSpec((cps * CH, GW), lambda c: (c, 0))
    sq = pl.BlockSpec((cps, HEADS, CH, CH), lambda c: (c, 0, 0, 0))
    big = jax.ShapeDtypeStruct((n, GW), F32)
    sqs = jax.ShapeDtypeStruct((nch, HEADS, CH, CH), F32)
    return pl.pallas_call(
        body, name="gdn_intra", grid=(nch // cps,),
        in_specs=[row, row, row, pl.BlockSpec((cps * CH, DH), lambda c: (c, 0)),
                  pl.BlockSpec((cps, DH, CH), lambda c: (c, 0, 0))],
        out_specs=[row, row, sq, sq],
        out_shape=[big, jax.ShapeDtypeStruct((n, GW), BF16), sqs, jax.ShapeDtypeStruct(sqs.shape, BF16)],
        compiler_params=_params(("parallel",)),
    )(q, k, v, bg, bgt)


def _gdn_scan(q, k, bg, u, w, p):
    n = q.shape[0]
    nch = n // CH
    cps = SCAN_CPS if nch % SCAN_CPS == 0 else 1

    def body(q_ref, k_ref, bg_ref, u_ref, w_ref, p_ref, o_ref, vn_ref, s_out, s_scr):
        @pl.when(pl.program_id(0) == 0)
        def _():
            s_scr[...] = jnp.zeros_like(s_scr)

        hs = range(HEADS)
        sls = [slice(h * DH, (h + 1) * DH) for h in hs]
        ss = [s_scr[h] for h in hs]
        for ci in range(cps):
            rs = slice(ci * CH, (ci + 1) * CH)
            bg = bg_ref[rs, :]
            gcols = [bg[:, HEADS + h:HEADS + h + 1] for h in hs]
            glasts = [g[CH - 1:CH, :] for g in gcols]
            wss = [_dot(w_ref[rs, sl], s, NN, P_SCAN) for sl, s in zip(sls, ss)]
            oqs = [_dot(q_ref[rs, sl] * jnp.exp(g), s, NN, P_SCAN) for sl, s, g in zip(sls, ss, gcols)]
            vns = [u_ref[rs, sl] - x for sl, x in zip(sls, wss)]
            ops = [_dot(p_ref[ci, h], vn, NN, P_SCAN) for h, vn in zip(hs, vns)]
            sns = [_dot(k_ref[rs, sl] * jnp.exp(gl - g), vn, TN, P_SCAN)
                   for sl, gl, g, vn in zip(sls, glasts, gcols, vns)]
            for h, sl in enumerate(sls):
                s_out[ci, :, sl] = ss[h].astype(BF16)
                vn_ref[rs, sl] = vns[h].astype(BF16)
                o_ref[rs, sl] = oqs[h] + ops[h]
            ss = [s * jnp.exp(gl) + sn for s, gl, sn in zip(ss, glasts, sns)]
        for h in hs:
            s_scr[h] = ss[h]

    row = pl.BlockSpec((cps * CH, GW), lambda c: (c, 0))
    big = jax.ShapeDtypeStruct((n, GW), F32)
    return pl.pallas_call(
        body, name="gdn_scan", grid=(nch // cps,),
        in_specs=[row, row, pl.BlockSpec((cps * CH, DH), lambda c: (c, 0)), row, row,
                  pl.BlockSpec((cps, HEADS, CH, CH), lambda c: (c, 0, 0, 0))],
        out_specs=[row, row, pl.BlockSpec((cps, DH, GW), lambda c: (c, 0, 0))],
        out_shape=[big, jax.ShapeDtypeStruct((n, GW), BF16), jax.ShapeDtypeStruct((nch, DH, GW), BF16)],
        scratch_shapes=[pltpu.VMEM((HEADS, DH, DH), F32)],
        compiler_params=_params(("arbitrary",)),
    )(q, k, bg, u, w, p)


def _gdn_scan_bwd(q, k, bg, w, p, vn, s_in, do):
    n = q.shape[0]
    nch = n // CH
    cps = SCAN_CPS if nch % SCAN_CPS == 0 else 1
    rev = lambda c: nch // cps - 1 - c

    def body(q_ref, k_ref, bg_ref, w_ref, p_ref, vn_ref, s_ref, do_ref,
             dqg_ref, dp_ref, du_ref, dw_ref, dks_ref, dgam_ref, ds_scr):
        @pl.when(pl.program_id(0) == 0)
        def _():
            ds_scr[...] = jnp.zeros_like(ds_scr)

        lane = _lane((1, DH))
        hs = range(HEADS)
        sls = [slice(h * DH, (h + 1) * DH) for h in hs]
        dss = [ds_scr[h] for h in hs]
        for ci in reversed(range(cps)):
            rs = slice(ci * CH, (ci + 1) * CH)
            bg = bg_ref[rs, :]
            gcols = [bg[:, HEADS + h:HEADS + h + 1] for h in hs]
            glasts = [g[CH - 1:CH, :] for g in gcols]
            ss = [s_ref[ci, :, sl] for sl in sls]
            dos = [do_ref[rs, sl] for sl in sls]
            vnl = [vn_ref[rs, sl] for sl in sls]
            dqgs = [_dot(d, s, NT, P_SCANB) for d, s in zip(dos, ss)]
            dps = [_dot(d, vn, NT, P_SCANB) for d, vn in zip(dos, vnl)]
            dvn1 = [_dot(p_ref[ci, h], d, TN, P_SCANB) for h, d in zip(hs, dos)]
            dvn2 = [_dot(k_ref[rs, sl] * jnp.exp(gl - g), ds, NN, P_SCANB)
                    for sl, gl, g, ds in zip(sls, glasts, gcols, dss)]
            dkss = [_dot(vn, ds, NT, P_SCANB) for vn, ds in zip(vnl, dss)]
            dsq = [_dot(q_ref[rs, sl] * jnp.exp(g), d, TN, P_SCANB) for sl, g, d in zip(sls, gcols, dos)]
            dvns = [a + b for a, b in zip(dvn1, dvn2)]
            dws = [_dot(dvn, s, NT, P_SCANB) for dvn, s in zip(dvns, ss)]
            dsw = [_dot(w_ref[rs, sl], dvn, TN, P_SCANB) for sl, dvn in zip(sls, dvns)]
            dgam = jnp.zeros((1, DH), F32)
            for h, sl in enumerate(sls):
                dqg_ref[rs, sl] = dqgs[h]
                dp_ref[ci, h] = dps[h]
                du_ref[rs, sl] = dvns[h].astype(BF16)
                dw_ref[rs, sl] = (-dws[h]).astype(BF16)
                dks_ref[rs, sl] = dkss[h]
                tot = jnp.sum(jnp.sum(dss[h] * ss[h], axis=-1, keepdims=True), axis=0, keepdims=True)
                dgam = dgam + jnp.where(lane == h, tot, 0.0)
            dgam_ref[ci] = jnp.broadcast_to(dgam, (8, DH))
            dss = [ds * jnp.exp(gl) + a - b for ds, gl, a, b in zip(dss, glasts, dsq, dsw)]
        for h in hs:
            ds_scr[h] = dss[h]

    row = pl.BlockSpec((cps * CH, GW), lambda c: (rev(c), 0))
    sq =pl.BlockSpec((cps, HEADS, CH, CH), lambda c: (rev(c), 0, 0, 0))
    big = jax.ShapeDtypeStruct((n, GW), F32)
    return pl.pallas_call(
        body, name="gdn_scan_bwd", grid=(nch // cps,),
        in_specs=[row, row, pl.BlockSpec((cps * CH, DH), lambda c: (rev(c), 0)), row, sq, row,
                  pl.BlockSpec((cps, DH, GW), lambda c: (rev(c), 0, 0)), row],
        out_specs=[row, sq, row, row, row, pl.BlockSpec((cps, 8, DH), lambda c: (rev(c), 0, 0))],
        out_shape=[big, jax.ShapeDtypeStruct((nch, HEADS, CH, CH), F32), jax.ShapeDtypeStruct((n, GW), BF16),
                   jax.ShapeDtypeStruct((n, GW), BF16), big,
                   jax.ShapeDtypeStruct((nch, 8, DH), F32)],
        scratch_shapes=[pltpu.VMEM((HEADS, DH, DH), F32)],
        compiler_params=_params(("arbitrary",)),
    )(q, k, bg, w, p, vn, s_in, do)


def _gdn_intra_bwd(q, k, v, bg, bgt, t, u, w, p, dqg, dp, du, dw, dks, dgam):
    n = q.shape[0]
    nch = n // CH
    cps = 2 if nch % 2 == 0 else 1

    def body(q_ref, k_ref, v_ref, bg_ref, bgt_ref, t_ref, u_ref, w_ref, p_ref,
             dqg_ref, dp_ref, du_ref, dw_ref, dks_ref, dgam_ref, dq_ref, dk_ref, dv_ref, dbg_ref):
        i, j = _ij()
        rows1 = lax.broadcasted_iota(jnp.int32, (CH, 1), 0)
        lane = _lane((CH, DH))
        rsum = lambda x: jnp.sum(x, axis=-1, keepdims=True)
        items = [(ci, h) for ci in range(cps) for h in range(HEADS)]
        at = lambda ref, it: ref.at[it[0] * CH:(it[0] + 1) * CH, it[1] * DH:(it[1] + 1) * DH]
        ld = lambda ref: [at(ref, it)[...] for it in items]
        bgs = [bg_ref[ci * CH:(ci + 1) * CH, :] for ci in range(cps)]
        qs, ks = ld(q_ref), ld(k_ref)
        vecs = [_head_vectors(bgs[ci], bgt_ref[ci], h) for ci, h in items]
        decs = [_decay(gcol, grow) for _, gcol, grow in vecs]
        ths = [t_ref[ci, h] for ci, h in items]
        drus = [_dot(th, x_, TN, P_BWD) for th, x_ in zip(ths, ld(du_ref))]
        drws = [_dot(th, x_, TN, P_BWD) for th, x_ in zip(ths, ld(dw_ref))]
        kks = [_dot(kh, kh, NT, P_GRAM) for kh in ks]
        da1 = [_dot(dru, x_, NT, P_BWD) for dru, x_ in zip(drus, ld(u_ref))]
        da2 = [_dot(drw, x_, NT, P_BWD) for drw, x_ in zip(drws, ld(w_ref))]
        das = [jnp.where(i > j, -(x_ + y_), 0.0) for x_, y_ in zip(da1, da2)]
        dkks = [da * bcol * dec for da, (bcol, _, _), dec in zip(das, vecs, decs)]
        dps = [dp_ref[ci, h] for ci, h in items]
        dqks = [dp_ * dec for dp_, dec in zip(dps, decs)]
        dq_ps = [_dot(dqk, kh, NN, P_BWD) for dqk, kh in zip(dqks, ks)]
        dk_ps = [_dot(dqk, qh, TN, P_BWD) for dqk, qh in zip(dqks, qs)]
        dk_as = [_dot(dkk, kh, NN, P_BWD) for dkk, kh in zip(dkks, ks)]
        dk_bs = [_dot(dkk, kh, TN, P_BWD) for dkk, kh in zip(dkks, ks)]
        bcols = [vc[0] for vc in vecs]
        gcols = [vc[1] for vc in vecs]
        gams = [jnp.exp(g) for g in gcols]
        glasts = [g[CH - 1:CH, :] for g in gcols]
        es = [jnp.exp(gl - g) for gl, g in zip(glasts, gcols)]
        kgs = [kh * gam for kh, gam in zip(ks, gams)]
        dqgs, dkss = ld(dqg_ref), ld(dks_ref)
        wks = [drw * kg for drw, kg in zip(drws, kgs)]
        kss = [dk_ * (kh * e) for dk_, kh, e in zip(dkss, ks, es)]
        r_beta = [rsum(dru * x_ + wk) for dru, x_, wk in zip(drus, ld(v_ref), wks)]
        r_ak = [rsum(da * kk * dec) for da, kk, dec in zip(das, kks, decs)]
        r_gc = [rsum(wk * bcol + dqg * (qh * gam) - ks_)
                for wk, bcol, dqg, qh, gam, ks_ in zip(wks, bcols, dqgs, qs, gams, kss)]
        tk_tot = [jnp.sum(jnp.sum(ks_, axis=0, keepdims=True), axis=-1, keepdims=True) for ks_ in kss]
        mdecs = [da * (bcol * kk * dec) + dp_ * p_ref[ci, h]
                 for (ci, h), da, bcol, kk, dec, dp_ in zip(items, das, bcols, kks, decs, dps)]
        r_md = [rsum(m) for m in mdecs]
        c_md = [rsum(jnp.where(i == j, jnp.sum(m, axis=0, keepdims=True), 0.0)) for m in mdecs]
        dbgs = [jnp.zeros((CH, DH), F32) for _ in range(cps)]
        for n_, (ci, h) in enumerate(items):
            at(dv_ref, (ci, h))[...] = bcols[n_] * drus[n_]
            at(dq_ref, (ci, h))[...] = gams[n_] * dqgs[n_] + dq_ps[n_]
            at(dk_ref, (ci, h))[...] = ((bcols[n_] * gams[n_]) * drws[n_] + dk_ps[n_] + dk_as[n_] + dk_bs[n_]
                                        + dkss[n_] * es[n_])
            dbeta = r_beta[n_] + r_ak[n_]
            dglast = tk_tot[n_] + dgam_ref[ci, 0:1, h:h + 1] * jnp.exp(glasts[n_])
            dgc = r_gc[n_] + r_md[n_] - c_md[n_] + jnp.where(rows1 == CH - 1, dglast, 0.0)
            dbgs[ci] = dbgs[ci] + jnp.where(lane == h, dbeta, 0.0) + jnp.where(lane == HEADS + h, dgc, 0.0)
        for ci in range(cps):
            dbg_ref[ci * CH:(ci + 1) * CH, :] = dbgs[ci]

    row = pl.BlockSpec((cps * CH, GW), lambda c: (c, 0))
    sq = pl.BlockSpec((cps, HEADS, CH, CH), lambda c: (c, 0, 0, 0))
    small = pl.BlockSpec((cps * CH, DH), lambda c: (c, 0))
    big = jax.ShapeDtypeStruct((n, GW), F32)
    return pl.pallas_call(
        body, name="gdn_intra_bwd", grid=(nch // cps,),
        in_specs=[row, row, row, small, pl.BlockSpec((cps, DH, CH), lambda c: (c, 0, 0)), sq, row, row, sq,
                  row, sq, row, row, row, pl.BlockSpec((cps, 8, DH), lambda c: (c, 0, 0))],
        out_specs=[row, row, row, small],
        out_shape=[big, big, big, jax.ShapeDtypeStruct((n, DH), F32)],
        compiler_params=_params(("parallel",)),
    )(q, k, v, bg, bgt, t, u, w, p, dqg, dp, du, dw, dks, dgam)


def _local_step(x, tgt, h, w_g, cqw, late, norm_in_w, ad, gdn_norm_w, conv_b, final_norm_w,
                on_grad_c=None, on_grad_g=None, on_q=None):
    proj_g = _matmul(h, w_g, NT, F32, 512, 1408, 1024, "mm_proj_g", n=GW_COLS, b_outer=True)
    q, k, v = _prep_qkv(proj_g, cqw)
    if on_q is not None:
        q = on_q(q)
    bg, bgt = _prep_bg(proj_g, ad)
    u, w, p, t = _gdn_intra(q, k, v, bg, bgt)
    o, vn, s_in = _gdn_scan(q, k, bg, u, w, p)
    w_c, w_out, conv_w = late(o)
    proj_c = _matmul(h, w_c, NT, F32, 512, 1024, 1024, "mm_proj_c", n=CW_COLS, b_outer=True)
    mix = _conv_branch(proj_c, conv_w, conv_b, _gdn_out(o, proj_g, gdn_norm_w))
    dout, dout_b, g_fn, loss = _out_loss(mix, w_out, x, tgt, final_norm_w)

    g_wout = _matmul(mix, dout_b, TN, BF16, 512, 512, 2048, "mm_gwout")
    do, dproj_g, g_gn = _gdn_out_bwd(o, proj_g, gdn_norm_w, dout_b, w_out)
    dproj_c, g_cw, g_cb = _conv_branch_bwd(proj_c, conv_w, conv_b, dout_b, w_out)
    g_c = _matmul(dproj_c, h, TN, BF16, 1024, 512, 2048, "mm_gwin_c")
    if on_grad_c is not None:
        do = on_grad_c(g_c, g_wout, do)
    dqg, dp, du, dw, dks, dgam = _gdn_scan_bwd(q, k, bg, w, p, vn, s_in, do)
    dq, dk, dv, dbg = _gdn_intra_bwd(q, k, v, bg, bgt, t, u, w, p, dqg, dp, du, dw, dks, dgam)
    dproj_g, gq, gk, gv = _prep_qkv_bwd(proj_g, cqw, dq, dk, dv, dproj_g)
    dproj_g, g_al, g_dt = _prep_bg_bwd(proj_g, ad, dbg, dproj_g)
    g_g = _matmul(dproj_g, h, TN, BF16, 1408, 512, 2048, "mm_gwin_g")
    if on_grad_g is not None:
        dproj_g = on_grad_g(g_g, dproj_g)
    dh = _matmul(dproj_g, w_g, NN, F32, 1024, 1024, 1408, "mm_dh_g")
    gx, g_nin = _dh_rms_bwd(dproj_c, w_c, dh, x, norm_in_w, dout, 1024)
    small = dict(nin=g_nin, cb=g_cb, fn=g_fn, al=g_al, dt=g_dt, gn=g_gn, cq=(gq, gk, gv), cw=g_cw, loss=loss)
    return gx, small, (g_g, g_c, g_wout)


def _place():
    x, y, c = lax.axis_index("x"), lax.axis_index("y"), lax.axis_index("c")
    chips = [(1 - x, y), (x, 1 - y), (1 - x, 1 - y)]
    return x, y, c, chips


def _blk(ref, b):
    if isinstance(b, int):
        return ref.at[b * DH:(b + 1) * DH, :]
    return ref.at[pl.ds(pl.multiple_of(b * DH, DH), DH), :]


HBM = pl.BlockSpec(memory_space=pltpu.HBM)
SEM = pl.BlockSpec(memory_space=pltpu.SEMAPHORE)
EFFECT = pltpu.SideEffectType.DATAFLOW_SIDE_EFFECTING


def _split_start(name, issue, bufs, n_sems):
    nbuf = len(bufs)

    def body(*refs):
        issue(refs[:nbuf], refs[nbuf], refs[nbuf + 1])
        refs[-1][...] = jnp.zeros_like(refs[-1])

    out = pl.pallas_call(
        body, name=name,
        out_shape=(pltpu.SemaphoreType.DMA((n_sems,)), pltpu.SemaphoreType.DMA((n_sems,)),
                   *[pltpu.HBM(b.shape, b.dtype) for b in bufs], jax.ShapeDtypeStruct((8, DH), F32)),
        in_specs=[HBM] * nbuf,
        out_specs=(SEM, SEM, *[HBM] * nbuf, pl.BlockSpec(memory_space=pltpu.VMEM)),
        input_output_aliases={a: 2 + a for a in range(nbuf)},
        compiler_params=pltpu.CompilerParams(has_side_effects=EFFECT),
    )(*[pltpu.with_memory_space_constraint(b, pltpu.HBM) for b in bufs])
    return out[0], out[1], list(out[2:2 + nbuf]), out[-1]


def _split_wait(name, await_, send_sems, recv_sems, bufs, after):
    nbuf = len(bufs)
    after = list(after) if isinstance(after, (list, tuple)) else [after]

    def body(*refs):
        await_(refs[:nbuf], refs[nbuf], refs[nbuf + 1])

    out = pl.pallas_call(
        body, name=name,
        out_shape=tuple(pltpu.HBM(b.shape, b.dtype) for b in bufs),
        in_specs=[HBM] * nbuf + [SEM, SEM] + [ANY] * len(after), out_specs=tuple([HBM] * nbuf),
        input_output_aliases={a: a for a in range(nbuf)},
        compiler_params=pltpu.CompilerParams(has_side_effects=EFFECT),
    )(*bufs, send_sems, recv_sems, *after)
    return list(out)


def _phase_blocks(chip, phase, edges, parity=None):
    return [(b, blk) for b, (grp, blk) in enumerate(_shard_blocks(chip, edges))
            if grp == phase and (parity is None or b % 2 == parity)]


def _cols(ref, nblk):
    return ref.at[0:nblk * DH, :]


def _block_table(chip, edges, spare_g, spare_c):
    rows = []
    for s in range(4):
        sb = _shard_blocks(s, edges)
        rows.append([[blk if grp == "g" else spare_g for grp, blk in sb],
                     [blk if grp == "c" else spare_c for grp, blk in sb],
                     [int(grp == "g") for grp, _ in sb], [s] * ALIGNED_BLOCKS])
    return jnp.asarray(rows, jnp.int32)[chip]


def _place_own(a_shard, wo, cq, cw, bufs):
    d = a_shard.shape[1]
    chip = 2 * lax.axis_index("x") + lax.axis_index("y")

    def body(t_ref, a_ref, wo_ref, cq_ref, cw_ref, *refs):
        wg_ref, wc_ref, wog_ref, cqg_ref, cwg_ref = refs[5:]
        wg_ref[...] = a_ref[...]
        wc_ref[...] = a_ref[...]

        @pl.when(pl.program_id(0) == 0)
        def _():
            wog_ref[0] = wo_ref[...]
            cqg_ref[0] = cq_ref[...]
            cwg_ref[0] = cw_ref[...]

    whole = lambda s: pl.BlockSpec(s.shape, lambda b, t: (0,) * s.ndim)
    slot = lambda s: pl.BlockSpec((1,) + s.shape, lambda b, t: (t[3, 0],) + (0,) * s.ndim)
    return pl.pallas_call(
        body, name="place_own",
        grid_spec=pltpu.PrefetchScalarGridSpec(
            num_scalar_prefetch=1, grid=(ALIGNED_BLOCKS,),
            in_specs=[pl.BlockSpec((DH, d), lambda b, t: (b, 0)), whole(wo), whole(cq), whole(cw)] + [ANY] * 5,
            out_specs=[pl.BlockSpec((DH, d), lambda b, t: (t[0, b], 0)),
                       pl.BlockSpec((DH, d), lambda b, t: (t[1, b], 0)), slot(wo), slot(cq), slot(cw)]),
        out_shape=[jax.ShapeDtypeStruct(b.shape, b.dtype) for b in bufs],
        input_output_aliases={5 + a: a for a in range(5)},
        compiler_params=_params(("arbitrary",)),
    )(_block_table(chip, True, G_SPARE, C_SPARE), a_shard, wo, cq, cw, *bufs)


def _tie(x, token, name):
    def body(x_ref, t_ref, o_ref):
        del x_ref, t_ref, o_ref

    return pl.pallas_call(
        body, name=name, in_specs=[ANY, ANY], out_specs=ANY,
        out_shape=jax.ShapeDtypeStruct(x.shape, x.dtype), input_output_aliases={0: 0},
    )(x, token)


def _gather_start(phase, a_shard, w_grp, singles):
    ns = len(singles)

    def issue(refs, send_sems, recv_sems):
        a_ref, w_ref = refs[0], refs[1]
        x, y, c, chips = _place()
        mine = 2 * x + y
        for jj, (px, py) in enumerate(chips):
            to = dict(device_id=(px, py, c), device_id_type=MESH)
            for a in range(ns):
                pltpu.make_async_remote_copy(
                    src_ref=refs[2 + 2 * a], dst_ref=refs[3 + 2 * a].at[mine],
                    send_sem=send_sems.at[(1 + ns) * jj + 1 + a], recv_sem=recv_sems.at[(1 + ns) * jj + 1 + a],
                    **to).start()
        for s in range(4):
            for par in range(2):
                blocks = _phase_blocks(s, phase, True, par)
                if blocks:
                    @pl.when((mine == s) & (c == par))
                    def _():
                        for b, blk in blocks:
                            for jj, (px, py) in enumerate(chips):
                                pltpu.make_async_remote_copy(
                                    src_ref=_blk(a_ref, b), dst_ref=_blk(w_ref, blk),
                                    send_sem=send_sems.at[(1 + ns) * jj], recv_sem=recv_sems.at[(1 + ns) * jj],
                                    device_id=(px, py, c), device_id_type=MESH).start()

    bufs = [a_shard, w_grp] + [t for pair in singles for t in pair]
    return _split_start("gather_start_" + phase, issue, bufs, 3 * (1 + ns))


def _gather_wait(phase, send_sems, recv_sems, bufs, after):
    ns = (len(bufs) - 2) // 2

    def await_(refs, send_sems, recv_sems):
        a_ref, w_ref = refs[0], refs[1]
        x, y, c, chips = _place()
        mine = 2 * x + y
        for jj, (px, py) in enumerate(chips):
            to = dict(device_id=(px, py, c), device_id_type=MESH)
            peer = 2 * px + py
            for a in range(ns):
                cp = pltpu.make_async_remote_copy(
                    src_ref=refs[2 + 2 * a], dst_ref=refs[3 + 2 * a].at[mine],
                    send_sem=send_sems.at[(1 + ns) * jj + 1 + a], recv_sem=recv_sems.at[(1 + ns) * jj + 1 + a], **to)
                cp.wait_recv()
                cp.wait_send()
            for s in range(4):
                for par in range(2):
                    nblk = len(_phase_blocks(s, phase, True, par))
                    if nblk:
                        both = pltpu.make_async_remote_copy(
                            src_ref=_cols(a_ref, nblk), dst_ref=_cols(w_ref, nblk),
                            send_sem=send_sems.at[(1 + ns) * jj], recv_sem=recv_sems.at[(1 + ns) * jj], **to)

                        @pl.when((peer == s) & (c == par))
                        def _():
                            both.wait_recv()

                        @pl.when((mine == s) & (c == par))
                        def _():
                            both.wait_send()

    return _split_wait("gather_wait_" + phase, await_, send_sems, recv_sems, bufs, after)


def _sibling_forward_parts(phase):
    def each(w_ref, send_sems, recv_sems, start):
        x, y, c, chips = _place()
        to = dict(device_id=(x, y, 1 - c), device_id_type=MESH)
        for jj, (px, py) in enumerate(chips):
            peer = 2 * px + py
            for s in range(4):
                for par in range(2):
                    mine_blocks = _phase_blocks(s, phase, True, par)
                    theirs = len(_phase_blocks(s, phase, True, 1 - par))
                    if not (mine_blocks or theirs):
                        continue

                    @pl.when((peer == s) & (c == par))
                    def _():
                        if start:
                            for _, blk in mine_blocks:
                                pltpu.make_async_remote_copy(
                                    src_ref=_blk(w_ref, blk), dst_ref=_blk(w_ref, blk),
                                    send_sem=send_sems.at[jj], recv_sem=recv_sems.at[jj], **to).start()
                            return
                        if theirs:
                            pltpu.make_async_remote_copy(
                                src_ref=_cols(w_ref, theirs), dst_ref=_cols(w_ref, theirs),
                                send_sem=send_sems.at[jj], recv_sem=recv_sems.at[jj], **to).wait_recv()
                        if mine_blocks:
                            pltpu.make_async_remote_copy(
                                src_ref=_cols(w_ref, len(mine_blocks)), dst_ref=_cols(w_ref, len(mine_blocks)),
                                send_sem=send_sems.at[jj], recv_sem=recv_sems.at[jj], **to).wait_send()

    issue = lambda refs, send_sems, recv_sems: each(refs[0], send_sems, recv_sems, True)
    await_ = lambda refs, send_sems, recv_sems: each(refs[0], send_sems, recv_sems, False)
    return issue, await_


def _sibling_forward(phase, w_grp):
    issue, await_ = _sibling_forward_parts(phase)

    def body(w_in_ref, w_ref, send_sems, recv_sems):
        del w_in_ref
        issue([w_ref], send_sems, recv_sems)
        await_([w_ref], send_sems, recv_sems)

    return pl.pallas_call(
        body, name="sibling_forward_" + phase, in_specs=[ANY], out_specs=ANY,
        out_shape=jax.ShapeDtypeStruct(w_grp.shape, w_grp.dtype), input_output_aliases={0: 0},
        scratch_shapes=[pltpu.SemaphoreType.DMA((3,)), pltpu.SemaphoreType.DMA((3,))],
    )(w_grp)


def _merge_edges(w, edge0, mixed, name):
    d = w.shape[1]

    def body(e_ref, o_ref):
        o_ref[...] = e_ref[0:DH, :] + e_ref[DH:2 * DH, :]

    def to_block(i):
        r = mixed[-1]
        for kk in range(len(mixed) - 2, -1, -1):
            r = jnp.where(i == kk, mixed[kk], r)
        return r

    return pl.pallas_call(
        body, name=name, grid=(len(mixed),),
        in_specs=[pl.BlockSpec((2 * DH, d), lambda i: (edge0 // 2 + i, 0))],
        out_specs=pl.BlockSpec((DH, d), lambda i: (to_block(i), 0)),
        out_shape=jax.ShapeDtypeStruct(w.shape, w.dtype),
        input_output_aliases={0: 0},
        compiler_params=_params(("arbitrary",)),
    )(w)


def _scatter_start(phase, g_grp, land, singles, halved=False):
    ns = len(singles)

    def issue(refs, send_sems, recv_sems):
        g_ref, land_ref = refs[0], refs[1]
        x, y, c, chips = _place()
        for jj, (px, py) in enumerate(chips):
            to = dict(device_id=(px, py, c), device_id_type=MESH)
            peer = 2 * px + py
            for a in range(ns):
                pltpu.make_async_remote_copy(
                    src_ref=refs[2 + 2 * a].at[peer], dst_ref=refs[3 + 2 * a].at[jj],
                    send_sem=send_sems.at[(1 + ns) * jj + 1 + a], recv_sem=recv_sems.at[(1 + ns) * jj + 1 + a],
                    **to).start()
            for s in range(4):
                for par in ((0, 1) if halved else (None,)):
                    blocks = _phase_blocks(s, phase, False, par)
                    if blocks:
                        @pl.when((peer == s) if par is None else ((peer == s) & (c == par)))
                        def _():
                            for b, blk in blocks:
                                pltpu.make_async_remote_copy(
                                    src_ref=_blk(g_ref, blk), dst_ref=_blk(land_ref.at[jj], b),
                                    send_sem=send_sems.at[(1 + ns) * jj], recv_sem=recv_sems.at[(1 + ns) * jj],
                                    **to).start()

    bufs = [g_grp, land] + [t for pair in singles for t in pair]
    return _split_start("scatter_start_" + phase, issue, bufs, 3 * (1 + ns))


def _scatter_wait(phase, send_sems, recv_sems, bufs, after, halved=False):
    ns = (len(bufs) - 2) // 2

    def await_(refs, send_sems, recv_sems):
        g_ref, land_ref = refs[0], refs[1]
        x, y, c, chips = _place()
        mine = 2 * x + y
        for jj, (px, py) in enumerate(chips):
            to = dict(device_id=(px, py, c), device_id_type=MESH)
            peer = 2 * px + py
            for a in range(ns):
                cp = pltpu.make_async_remote_copy(
                    src_ref=refs[2 + 2 * a].at[peer], dst_ref=refs[3 + 2 * a].at[jj],
                    send_sem=send_sems.at[(1 + ns) * jj + 1 + a], recv_sem=recv_sems.at[(1 + ns) * jj + 1 + a], **to)
                cp.wait_recv()
                cp.wait_send()
            for s in range(4):
                for par in ((0, 1) if halved else (None,)):
                    nblk = len(_phase_blocks(s, phase, False, par))
                    if nblk:
                        both = pltpu.make_async_remote_copy(
                            src_ref=_cols(g_ref, nblk), dst_ref=_cols(land_ref.at[jj], nblk),
                            send_sem=send_sems.at[(1 + ns) * jj], recv_sem=recv_sems.at[(1 + ns) * jj], **to)

                        @pl.when((mine == s) if par is None else ((mine == s) & (c == par)))
                        def _():
                            both.wait_recv()

                        @pl.when((peer == s) if par is None else ((peer == s) & (c == par)))
                        def _():
                            both.wait_send()

    return _split_wait("scatter_wait_" + phase, await_, send_sems, recv_sems, bufs, after)


def _needed_blocks(phase, parity):
    return sorted({blk for s in range(4) for _, blk in _phase_blocks(s, phase, False, parity)})


def _pair_reduce(phase, g_grp):
    n, d = g_grp.shape

    def swap(g_ref, sib_ref, send_sem, recv_sem):
        x, y, c, _ = _place()
        to = dict(device_id=(x, y, 1 - c), device_id_type=MESH)
        for par in range(2):
            give, get = _needed_blocks(phase, 1 - par), _needed_blocks(phase, par)

            @pl.when(c == par)
            def _():
                for blk in give:
                    pltpu.make_async_remote_copy(src_ref=_blk(g_ref, blk), dst_ref=_blk(sib_ref, blk),
                                                 send_sem=send_sem, recv_sem=recv_sem, **to).start()
                pltpu.make_async_remote_copy(src_ref=_cols(g_ref, len(get)), dst_ref=_cols(sib_ref, len(get)),
                                             send_sem=send_sem, recv_sem=recv_sem, **to).wait_recv()
                pltpu.make_async_remote_copy(src_ref=_cols(g_ref, len(give)), dst_ref=_cols(sib_ref, len(give)),
                                             send_sem=send_sem, recv_sem=recv_sem, **to).wait_send()

    sib = pl.pallas_call(
        swap, name="pair_swap_" + phase, in_specs=[ANY], out_specs=ANY,
        out_shape=jax.ShapeDtypeStruct((n, d), g_grp.dtype),
        scratch_shapes=[pltpu.SemaphoreType.DMA, pltpu.SemaphoreType.DMA],
    )(*_in_hbm(g_grp))

    lists = [_needed_blocks(phase, par) for par in range(2)]
    longest = max(len(t) for t in lists)
    table = jnp.asarray([t + [t[-1]] * (longest - len(t)) for t in lists], jnp.int32)[lax.axis_index("c")]

    def add(t_ref, a_ref, b_ref, o_ref):
        o_ref[...] = (a_ref[...].astype(F32) + b_ref[...].astype(F32)).astype(o_ref.dtype)

    blk = pl.BlockSpec((DH, d), lambda i, t: (t[i], 0))
    return pl.pallas_call(
        add, name="pair_add_" + phase,
        grid_spec=pltpu.PrefetchScalarGridSpec(num_scalar_prefetch=1, grid=(longest,),
                                               in_specs=[blk, blk], out_specs=blk),
        out_shape=jax.ShapeDtypeStruct((n, d), g_grp.dtype),
        compiler_params=_params(("arbitrary",)),
    )(table, g_grp, sib)


def _sum_shard(g_g, g_c, land):
    d = g_g.shape[1]
    chip = 2 * lax.axis_index("x") + lax.axis_index("y")

    def body(t_ref, gg_ref, gc_ref, land_ref, o_ref):
        b = pl.program_id(0)
        in_g = t_ref[2, b] == 1
        own = jnp.where(in_g, gg_ref[...].astype(F32), gc_ref[...].astype(F32))
        for jj in range(3):
            own = own + land_ref[jj].astype(F32)
        o_ref[...] = jnp.where(in_g & (b % 2 != lax.axis_index("c")), 0.0, own)

    return pl.pallas_call(
        body, name="sum_w_in",
        grid_spec=pltpu.PrefetchScalarGridSpec(
            num_scalar_prefetch=1, grid=(ALIGNED_BLOCKS,),
            in_specs=[pl.BlockSpec((DH, d), lambda b, t: (t[0, b], 0)), pl.BlockSpec((DH, d), lambda b, t: (t[1, b], 0)),
                      pl.BlockSpec((3, DH, d), lambda b, t: (0, b, 0))],
            out_specs=pl.BlockSpec((DH, d), lambda b, t: (b, 0))),
        out_shape=jax.ShapeDtypeStruct((ALIGNED_W, d), F32),
        compiler_params=_params(("arbitrary",)),
    )(_block_table(chip, False, 0, 0), g_g, g_c, land)


def _sum_rows(stack, land, rows):
    _, r, d = stack.shape
    rows = min(rows, r)
    chip = 2 * lax.axis_index("x") + lax.axis_index("y")

    def body(t_ref, own_ref, land_ref, o_ref):
        acc = own_ref[0].astype(F32)
        for jj in range(3):
            acc = acc + land_ref[jj].astype(F32)
        o_ref[...] = acc

    return pl.pallas_call(
        body, name="sum_w_out",
        grid_spec=pltpu.PrefetchScalarGridSpec(
            num_scalar_prefetch=1, grid=(r // rows,),
            in_specs=[pl.BlockSpec((1, rows, d), lambda i, t: (t[0], i, 0)),
                      pl.BlockSpec((3, rows, d), lambda i, t: (0, i, 0))],
            out_specs=pl.BlockSpec((rows, d), lambda i, t: (i, 0))),
        out_shape=jax.ShapeDtypeStruct((r, d), F32),
        compiler_params=_params(("arbitrary",)),
    )(jnp.reshape(chip, (1,)).astype(jnp.int32), stack, land)


def _exchange_parts(n_swap, with_pack):
    def copies(refs, send_sems, recv_sems):
        x, y, c, _ = _place()
        me = 4 * x + 2 * y + c
        cps = [pltpu.make_async_remote_copy(
            src_ref=refs[2 * a], dst_ref=refs[2 * a + 1], send_sem=send_sems.at[a], recv_sem=recv_sems.at[a],
            device_id=(x, y, 1 - c), device_id_type=MESH) for a in range(n_swap)]
        if with_pack:
            pack_ref, packs = refs[2 * n_swap], refs[2 * n_swap + 1]
            for r in range(1, 8):
                dx, dy, dc = (r >> 2) & 1, (r >> 1) & 1, r & 1
                peer = (x + dx - 2 * x * dx, y + dy - 2 * y * dy, c + dc - 2 * c * dc)
                cps.append(pltpu.make_async_remote_copy(
                    src_ref=pack_ref, dst_ref=packs.at[me], send_sem=send_sems.at[n_swap + r - 1],
                    recv_sem=recv_sems.at[n_swap + r - 1], device_id=peer, device_id_type=MESH))
        return cps

    def issue(refs, send_sems, recv_sems):
        for cp in copies(refs, send_sems, recv_sems):
            cp.start()

    def await_(refs, send_sems, recv_sems):
        cps = copies(refs, send_sems, recv_sems)
        for cp in cps:
            cp.wait_recv()
        for cp in cps:
            cp.wait_send()

    return issue, await_, n_swap + (7 if with_pack else 0)


def _sum_packs(pack, packs):
    x, y, c = lax.axis_index("x"), lax.axis_index("y"), lax.axis_index("c")
    me = jnp.reshape(4 * x + 2 * y + c, (1,)).astype(jnp.int32)

    def body(me_ref, own_ref, p_ref, o_ref):
        acc = jnp.where(me_ref[0] == 0, own_ref[...], p_ref[0])
        for d in range(1, 8):
            acc = acc + jnp.where(me_ref[0] == d, own_ref[...], p_ref[d])
        o_ref[...] = acc

    full = lambda s: pl.BlockSpec(s.shape, lambda i, t: (0,) * s.ndim)
    return pl.pallas_call(
        body, name="sum_packs",
        grid_spec=pltpu.PrefetchScalarGridSpec(num_scalar_prefetch=1, grid=(1,), in_specs=[full(pack), full(packs)],
                                               out_specs=full(pack)),
        out_shape=jax.ShapeDtypeStruct(pack.shape, F32),
    )(me, pack, packs)


def _adamw_update(g, w_ref, m_ref, v_ref, go, do, mo, vo):
    c1 = 1.0 / (1.0 - ADAM_B1 ** ADAM_STEP)
    c2 = 1.0 / (1.0 - ADAM_B2 ** ADAM_STEP)
    mn = ADAM_B1 * m_ref[...] + (1.0 - ADAM_B1) * g
    vn = ADAM_B2 * v_ref[...] + (1.0 - ADAM_B2) * (g * g)
    go[...] = g
    mo[...] = mn
    vo[...] = vn
    do[...] = -ADAM_LR * ((mn * c1) / (jnp.sqrt(vn * c2) + ADAM_EPS) + ADAM_WD * w_ref[...])


def _adamw(w, m, v, g1, g2, rows, name):
    r, cdim = w.shape
    rows = min(rows, r)

    def body(w_ref, m_ref, v_ref, g1_ref, g2_ref, *outs):
        _adamw_update(g1_ref[...] + g2_ref[...], w_ref, m_ref, v_ref, *outs)

    blk = pl.BlockSpec((rows, cdim), lambda i: (i, 0))
    shp = jax.ShapeDtypeStruct((r, cdim), F32)
    return pl.pallas_call(
        body, name=name, grid=(r // rows,),
        in_specs=[blk] * 5, out_specs=[blk] * 4, out_shape=[shp] * 4,
        compiler_params=_params(("parallel",), 20 * rows * cdim * 4 + 8 * 2**20),
    )(*_in_hbm(w, m, v, g1, g2))


def _adamw_small(w_s, m_s, v_s, tot):
    per_row = GW // DH
    cq_blocks, cw_blocks = 3 * per_row // 4, per_row // 4
    cq_lanes, cw_lanes = cq_blocks * DH, cw_blocks * DH
    shapes = [(1, GW), (1, 4, cq_lanes), (1, HEADS), (1, HEADS), (1, DH), (1, 3, cw_lanes), (1, GW), (per_row, DH)]

    def body(t_ref, w_ref, m_ref, v_ref, *refs):
        g_scr, kinds = refs[len(refs) - 5], refs[len(refs) - 4:]
        chip = 2 * lax.axis_index("x") + lax.axis_index("y")

        def mine(first_row, rows_per_tap, nblk, t, jb):
            out = None
            for k in reversed(range(4)):
                b = nblk * k + jb
                row = first_row + rows_per_tap * t + b // per_row
                cand = t_ref[row:row + 1, (b % per_row) * DH:(b % per_row + 1) * DH]
                out = cand if out is None else jnp.where(chip == k, cand, out)
            return out

        g_scr[...] = jnp.zeros_like(g_scr)
        for src, dst in ((R_NIN, S_NIN), (R_CB, S_CB), (R_FN, S_FN), (R_AD, S_AD), (R_GN, S_GN)):
            g_scr[dst:dst + 1, :] = t_ref[src:src + 1, :]
        for t in range(4):
            for jb in range(cq_blocks):
                g_scr[S_CQ + t:S_CQ + t + 1, jb * DH:(jb + 1) * DH] = mine(R_CQ, 3, cq_blocks, t, jb)
        for t in range(3):
            for jb in range(cw_blocks):
                g_scr[S_CW + t:S_CW + t + 1, jb * DH:(jb + 1) * DH] = mine(R_CW, 1, cw_blocks, t, jb)
        _adamw_update(g_scr[...], w_ref, m_ref, v_ref, *kinds)
        for kk, a in enumerate(kinds):
            nin, cq, al, dt, gn, cw, cb, fn = refs[8 * kk:8 * kk + 8]
            nin[...] = a[S_NIN:S_NIN + 1, :]
            cq[0] = a[S_CQ:S_CQ + 4, 0:cq_lanes]
            al[...] = a[S_AD:S_AD + 1, 0:HEADS]
            dt[...] = a[S_AD:S_AD + 1, HEADS:2 * HEADS]
            gn[...] = a[S_GN:S_GN + 1, 0:DH]
            cw[0] = a[S_CW:S_CW + 3, 0:cw_lanes]
            cb[...] = a[S_CB:S_CB + 1, :]
            for k in range(per_row):
                fn[k:k + 1, :] = a[S_FN:S_FN + 1, k * DH:(k + 1) * DH]

    out = pl.pallas_call(
        body, name="adamw_small",
        out_shape=[jax.ShapeDtypeStruct(s, F32) for s in shapes] * 4,
        scratch_shapes=[pltpu.VMEM(w_s.shape, F32)] * 5,
    )(tot, w_s, m_s, v_s)
    return [out[8 * kk:8 * kk + 8] for kk in range(4)]


def _adamw_shard(wt, mt, vt, g1, g2):
    r, d = wt.shape
    cols = min(256, d)

    def body(w_ref, m_ref, v_ref, g_ref, g2_ref, go, do, mo, vo, pad_ref):
        chip = 2 * lax.axis_index("x") + lax.axis_index("y")
        back = [(ALIGNED_W - s) % ALIGNED_W for s in SHIFTS]
        pad_ref[...] = pltpu.roll(g_ref[...] + g2_ref[...], _by_chip(chip, back), 0)
        outs = [o.at[:, 0, :] for o in (go, do, mo, vo)]
        _adamw_update(pad_ref[0:r, :], w_ref, m_ref, v_ref, *outs)

    blk = pl.BlockSpec((r, cols), lambda i: (0, i))
    gblk = pl.BlockSpec((ALIGNED_W, cols), lambda i: (0, i))
    oblk = pl.BlockSpec((r, 1, cols), lambda i: (0, 0, i))
    shp = jax.ShapeDtypeStruct((r, 1, d), F32)
    return pl.pallas_call(
        body, name="adamw_w_in", grid=(d // cols,),
        in_specs=[blk] * 3 + [gblk] * 2, out_specs=[oblk] * 4, out_shape=[shp] * 4,
        scratch_shapes=[pltpu.VMEM((ALIGNED_W, cols), F32)],
        compiler_params=_params(("parallel",), 24 * ALIGNED_W * cols * 4 + 8 * 2**20),
    )(wt, mt, vt, g1, g2)


def _pad_lanes(a, width):
    return jnp.pad(a, ((0, 0), (0, width - a.shape[1])))


def _gathered_to_full(g):
    return jnp.transpose(g, (1, 0, 2)).reshape(g.shape[1], 4 * g.shape[2])


def _row(a):
    return _pad_lanes(a.reshape(1, -1), 1024)


S_NIN, S_CB, S_FN, S_AD, S_GN, S_CQ, S_CW = 0, 1, 2, 3, 4, 5, 9


def _small_pack(nin, cb, fn, al, dt, gn, cqw_shard, cw_shard):
    ad = jnp.concatenate([al.reshape(1, -1), dt.reshape(1, -1)], axis=1)
    rows = [_row(nin), _row(cb), _row(fn), _row(ad), _row(gn), _pad_lanes(cqw_shard, 1024),
            _pad_lanes(cw_shard, 1024)]
    out = jnp.concatenate(rows, axis=0)
    return jnp.pad(out, ((0, 16 - out.shape[0]), (0, 0)))


def kernel(x, norm_in_w, w_in, conv_qkv_w, A_log, dt_bias, gdn_norm_w, conv_w, conv_b, w_out, final_norm_w, loss_target, m_norm_in_w, m_w_in, m_conv_qkv_w, m_A_log, m_dt_bias, m_gdn_norm_w, m_conv_w, m_conv_b, m_w_out, m_final_norm_w, v_norm_in_w, v_w_in, v_conv_qkv_w, v_A_log, v_dt_bias, v_gdn_norm_w, v_conv_w, v_conv_b, v_w_out, v_final_norm_w):
    a_shard = _align_shard(jnp.transpose(w_in, (2, 0, 1)))
    d_model = x.shape[-1]
    stack = lambda s: lax.empty((4,) + s.shape, s.dtype)
    wg0 = lax.empty((WG_BLOCKS * DH, d_model), BF16)
    wc0 = lax.empty((WC_BLOCKS * DH, d_model), BF16)
    ss_g, rs_g, bufs_g, tok_g = _gather_start("g", a_shard, wg0, [(conv_qkv_w[0], stack(conv_qkv_w[0]))])
    wo_b = _cast_bf16(w_out[0], 256, "cast_w_out", tok_g)
    ss_c, rs_c, bufs_c, tok_c = _gather_start("c", bufs_g[0], wc0,
                                              [(conv_w[0], stack(conv_w[0])), (wo_b, stack(wo_b))])
    wg1, wc1, wog1, cqg1, cwg1 = _place_own(bufs_c[0], bufs_c[4], bufs_g[2], bufs_c[2],
                                            [bufs_g[1], bufs_c[1], bufs_c[5], bufs_g[3], bufs_c[3]])
    x0 = x[0]
    h = _rms_in(x0, _tie(_tie(norm_in_w, tok_g, "after_gather_start_g"), tok_c, "after_gather_start_c"))
    adam_in = [jnp.transpose(a[0]) for a in (w_in, m_w_in, v_w_in)]
    sp = lambda nin, cb, fn, al, dt, gn, cq, cwv: _small_pack(nin, cb, fn, al, dt, gn, cq[0], cwv[0])
    w_s = sp(norm_in_w, conv_b, final_norm_w, A_log, dt_bias, gdn_norm_w, conv_qkv_w, conv_w)
    m_s = sp(m_norm_in_w, m_conv_b, m_final_norm_w, m_A_log, m_dt_bias, m_gdn_norm_w, m_conv_qkv_w, m_conv_w)
    v_s = sp(v_norm_in_w, v_conv_b, v_final_norm_w, v_A_log, v_dt_bias, v_gdn_norm_w, v_conv_qkv_w, v_conv_w)
    a_thru, wg, _, cq_g = _gather_wait("g", ss_g, rs_g, [bufs_c[0], wg1, bufs_g[2], cqg1],
                                       [h, w_s, m_s, v_s] + adam_in[1:])
    w_g = _merge_edges(_sibling_forward("g", wg), G_EDGE, G_MIXED, "merge_edges_g")
    cqw = _gathered_to_full(cq_g)
    ad = jnp.pad(jnp.concatenate([A_log, dt_bias], axis=0), ((0, 0), (A_LANE, 0)))
    fwd_c = {}

    def on_q(q):
        _, wc, _, cw_g, _, wo_g = _gather_wait("c", ss_c, rs_c,
                                               [a_thru, wc1, bufs_c[2], cwg1, bufs_c[4], wog1], q)
        issue, _ = _sibling_forward_parts("c")
        ss, rs, (wc,), tok = _split_start("sibling_forward_start_c", issue, [wc], 3)
        fwd_c.update(ss=ss, rs=rs, wc=wc, cw_g=cw_g, wo_g=wo_g)
        return _tie(q, tok, "after_sibling_forward_start_c")

    def late(o):
        _, await_ = _sibling_forward_parts("c")
        (wc,) = _split_wait("sibling_forward_wait_c", await_, fwd_c["ss"], fwd_c["rs"], [fwd_c["wc"]], o)
        return (_merge_edges(wc, C_EDGE, C_MIXED, "merge_edges_c"), fwd_c["wo_g"].reshape(2 * GW, d_model),
                _gathered_to_full(fwd_c["cw_g"]))

    scat = {}

    def on_grad_c(g_c, g_wout, do):
        go4 = g_wout.reshape(4, GW // 2, d_model)
        land = lax.empty((3, ALIGNED_W, d_model), BF16)
        land_o = lax.empty((3, GW // 2, d_model), BF16)
        ss, rs, bufs, tok = _scatter_start("c", g_c, land, [(go4, land_o)])
        scat["c"] = (ss, rs, bufs)
        return _tie(do, tok, "after_scatter_start_c")

    def on_grad_g(g_g, dproj_g):
        ss, rs, bufs, tok = _scatter_start("g", _pair_reduce("g", g_g), scat["c"][2][1], [], halved=True)
        scat["g"] = (ss, rs, bufs)
        return _tie(dproj_g, tok, "after_scatter_start_g")

    gx, sm, _ = _local_step(x0, loss_target[0], h, w_g, cqw, late, norm_in_w, ad, gdn_norm_w, conv_b,
                            final_norm_w.reshape(1, -1), on_grad_c, on_grad_g, on_q)

    ss, rs, bufs = scat["c"]
    g_c, land, go4, land_o = _scatter_wait("c", ss, rs, [bufs[0], scat["g"][2][1], bufs[2], bufs[3]], gx)
    part_out = _sum_rows(go4, land_o, 128)
    ad_g = jnp.concatenate([sm["al"][:, A_LANE:], sm["dt"][:, A_LANE:]], axis=1)
    pack = jnp.concatenate([_row(sm["nin"]), _row(sm["cb"]), _row(sm["fn"]), _row(ad_g), _row(sm["gn"]),
                            jnp.concatenate(sm["cq"], axis=1).reshape(12, 1024), sm["cw"], _row(sm["loss"])], axis=0)
    pack = jnp.pad(pack, ((0, PACK_ROWS - pack.shape[0]), (0, 0)))
    issue, await_a, nsem = _exchange_parts(1, True)
    ss_a, rs_a, bufs_a, tok_a = _split_start(
        "exchange_start_small", issue,
        [part_out, lax.empty(part_out.shape, F32), pack, lax.empty((8,) + pack.shape, F32)], nsem)
    ss, rs, bufs = scat["g"]
    g_g, land = _scatter_wait("g", ss, rs, [bufs[0], land], [gx, tok_a], halved=True)
    part_in = _sum_shard(g_g, g_c, land)
    issue, await_b, nsem = _exchange_parts(1, False)
    ss_b, rs_b, bufs_b, tok_b = _split_start("exchange_start_w_in", issue,
                                             [part_in, lax.empty(part_in.shape, F32)], nsem)
    part_out, sib_out, pack, packs = _split_wait("exchange_wait_small", await_a, ss_a, rs_a, bufs_a, tok_b)
    tot = _sum_packs(pack, packs)
    g_wo, d_wo, m_wo, v_wo = _adamw(w_out[0], m_w_out[0], v_w_out[0], part_out, sib_out, 128, "adamw_w_out")
    small = _adamw_small(w_s, m_s, v_s, tot)
    part_in, sib_in = _split_wait("exchange_wait_w_in", await_b, ss_b, rs_b, bufs_b, [small[0][0], d_wo])
    g_wi, d_wi, m_wi, v_wi = [jnp.transpose(a, (1, 2, 0))[0] for a in _adamw_shard(*adam_in, part_in, sib_in)]

    def unpack(leaves, big_in, big_out):
        nin, cq, al, dt, gn, cw, cb, fn = leaves
        return (nin, big_in[None], cq, al, dt, gn, cw, cb, big_out[None], fn.reshape(-1))

    loss = tot[R_LOSS, 0]
    return (loss, gx[None], *unpack(small[0], g_wi, g_wo), *unpack(small[1], d_wi, d_wo),
            *unpack(small[2], m_wi, m_wo), *unpack(small[3], v_wi, v_wo))
```

```python
import jax
import jax.numpy as jnp
from jax import lax
from jax.experimental import pallas as pl
from jax.experimental.pallas import tpu as pltpu

F32 = jnp.float32
BF16 = jnp.bfloat16
MESH = pl.DeviceIdType.MESH
ANY = pl.BlockSpec(memory_space=pl.ANY)

HEADS = 8
DH = 128
CH = 64
GW = HEADS * DH
EPS = 1e-6
VMEM_V7X = 64 * 1024 * 1024

QB, KB, VB, ZB, BAB = 0, 8, 16, 24, 32
A_LANE = 120
NG, NC = 33, 32
GW_COLS, CW_COLS = NG * DH, NC * DH

SHARD_W = 2052
ALIGNED_BLOCKS = 17
ALIGNED_W = ALIGNED_BLOCKS * DH
SHIFTS = (0, 4, ALIGNED_W - 8, ALIGNED_W - 4)
G_EDGE, C_EDGE = 34, 32
G_SPARE, C_SPARE = 33, 34
WG_BLOCKS, WC_BLOCKS = 38, 36
G_MIXED, C_MIXED = (2, BAB), (4 * 7 + 1,)


def _shard_blocks(chip, edges):
    g, c = "g", "c"
    if chip == 0:
        out = [(g, 3 * b) for b in range(8)] + [(g, 3 * b + 1) for b in range(8)] + [(g, G_EDGE, G_MIXED[0])]
    elif chip == 1:
        out = [(g, G_EDGE + 1, G_MIXED[0])] + [(g, 3 * b + 2) for b in range(1, 8)]
        out += [(g, ZB + b) for b in range(8)] + [(g, G_EDGE + 2, G_MIXED[1])]
    elif chip == 2:
        out = [(c, 4 * b) for b in range(8)] + [(c, 4 * b + 1) for b in range(7)]
        out += [(c, C_EDGE, C_MIXED[0]), (g, G_EDGE + 3, G_MIXED[1])]
    else:
        out = [(c, 4 * b + 2) for b in range(8)] + [(c, 4 * b + 3) for b in range(8)] + [(c, C_EDGE + 1, C_MIXED[0])]
    return [(o[0], o[1] if (edges or len(o) == 2) else o[2]) for o in out]


def _by_chip(chip, vals):
    if all(v == vals[0] for v in vals):
        return vals[0]
    r = vals[3]
    for kk in (2, 1, 0):
        r = jnp.where(chip == kk, vals[kk], r)
    return r

ADAM_LR, ADAM_B1, ADAM_B2, ADAM_EPS, ADAM_WD, ADAM_STEP = 0.001, 0.9, 0.999, 1e-08, 0.01, 10

R_NIN, R_CB, R_FN, R_AD, R_GN, R_CQ, R_CW, R_LOSS, PACK_ROWS = 0, 1, 2, 3, 4, 5, 17, 20, 24

NN = ((1,), (0,))
NT = ((1,), (1,))
TN = ((0,), (0,))


def _dot(a, b, dims=NN, mode="lo"):
    dn = (dims, ((), ()))
    if mode == "hi":
        return lax.dot_general(a, b, dn, precision=lax.Precision.HIGHEST, preferred_element_type=F32)
    ah, bh = a.astype(BF16), b.astype(BF16)
    out = lax.dot_general(ah, bh, dn, preferred_element_type=F32)
    if mode == "x3":
        al = (a - ah.astype(F32)).astype(BF16)
        bl = (b - bh.astype(F32)).astype(BF16)
        out = out + lax.dot_general(ah, bl, dn, preferred_element_type=F32)
        out = out + lax.dot_general(al, bh, dn, preferred_element_type=F32)
    return out


P_GRAM, P_INV, P_SOL, P_SCAN, P_SCANB, P_BWD = "lo", "lo", "lo", "lo", "lo", "lo"
P_CUM = "x3"


def _params(sem=None, vmem=None):
    kw = {}
    if sem is not None:
        kw["dimension_semantics"] = sem
    if vmem is not None:
        kw["vmem_limit_bytes"] = int(min(max(vmem, 32 * 2**20), VMEM_V7X - 8 * 2**20))
    return pltpu.CompilerParams(**kw)


def _in_hbm(*arrays):
    return [pltpu.with_memory_space_constraint(a, pltpu.HBM) for a in arrays]


def _sigmoid(x):
    return 1.0 / (1.0 + jnp.exp(-x))


def _dsilu(x, s):
    return s * (1.0 + x * (1.0 - s))


def _rows(shape):
    return lax.broadcasted_iota(jnp.int32, shape, 0)


def _shift_down(x, s):
    if s == 0:
        return x
    return jnp.where(_rows(x.shape) >= s, pltpu.roll(x, s, 0), 0.0)


def _shift_up(x, s):
    if s == 0:
        return x
    n = x.shape[0]
    return jnp.where(_rows(x.shape) < n - s, pltpu.roll(x, n - s, 0), 0.0)


def _matmul(a, b, dims, out_dtype, tm, tn, tk, name, add=None, n=None, b_outer=False):
    if dims == NN:
        (m, k), n = a.shape, b.shape[1]
    elif dims == NT:
        (m, k), n = a.shape, (n or b.shape[0])
    else:
        (k, m), n = a.shape, b.shape[1]
    tm, tn, tk = min(tm, m), min(tn, n), min(tk, k)
    assert m % tm == 0 and n % tn == 0 and k % tk == 0, (name, m, n, k, tm, tn, tk)
    nk = k // tk

    def body(*refs):
        if add is None:
            a_ref, b_ref, o_ref = refs[:3]
            add_ref = None
        else:
            a_ref, b_ref, add_ref, o_ref = refs[:4]
        part = _dot(a_ref[...], b_ref[...], dims)
        if nk == 1:
            if add_ref is not None:
                part = part + add_ref[...]
            o_ref[...] = part.astype(out_dtype)
            return
        acc = refs[-1]
        kk = pl.program_id(2)

        @pl.when(kk == 0)
        def _():
            acc[...] = part

        @pl.when(kk > 0)
        def _():
            acc[...] += part

        @pl.when(kk == nk - 1)
        def _():
            r = acc[...]
            if add_ref is not None:
                r = r + add_ref[...]
            o_ref[...] = r.astype(out_dtype)

    ij = (lambda g0, g1: (g1, g0)) if b_outer else (lambda g0, g1: (g0, g1))

    def spec(shape, pick):
        return pl.BlockSpec(shape, lambda g0, g1, kk: pick(*ij(g0, g1), kk))

    a_spec = spec((tk, tm), lambda i, j, kk: (kk, i)) if dims == TN else spec((tm, tk), lambda i, j, kk: (i, kk))
    b_spec = spec((tn, tk), lambda i, j, kk: (j, kk)) if dims == NT else spec((tk, tn), lambda i, j, kk: (kk, j))
    o_spec = spec((tm, tn), lambda i, j, kk: (i, j))
    in_specs = [a_spec, b_spec]
    args = [a, b]
    if add is not None:
        in_specs.append(o_spec)
        args.append(add)
    osz = jnp.dtype(out_dtype).itemsize
    est = 2 * (tm * tk * a.dtype.itemsize + tk * tn * b.dtype.itemsize + tm * tn * osz)
    est += 3 * tm * tn * 4 + (2 * tm * tn * 4 if add is not None else 0)
    return pl.pallas_call(
        body, name=name, grid=(n // tn, m // tm, nk) if b_outer else (m // tm, n // tn, nk),
        in_specs=in_specs, out_specs=o_spec,
        out_shape=jax.ShapeDtypeStruct((m, n), out_dtype),
        scratch_shapes=[pltpu.VMEM((tm, tn), F32)] if nk > 1 else [],
        compiler_params=_params(("parallel", "parallel", "arbitrary"), est + 8 * 2**20),
    )(*args)


def _cast_bf16(a, rows, name, after):
    r, c = a.shape
    rows = min(rows, r)

    def body(a_ref, t_ref, o_ref):
        del t_ref
        o_ref[...] = a_ref[...].astype(BF16)

    return pl.pallas_call(
        body, name=name, grid=(r // rows,),
        in_specs=[pl.BlockSpec((rows, c), lambda i: (i, 0)), ANY],
        out_specs=pl.BlockSpec((rows, c), lambda i: (i, 0)),
        out_shape=jax.ShapeDtypeStruct((r, c), BF16),
        compiler_params=_params(("parallel",)),
    )(a, after)


def _align_shard(wt):
    r, _, d = wt.shape
    cols = min(256, d)

    def body(w_ref, o_ref, pad_ref):
        chip = 2 * lax.axis_index("x") + lax.axis_index("y")
        pad_ref[...] = jnp.zeros_like(pad_ref)
        pad_ref[0:r, :] = w_ref[:, 0, :]
        o_ref[...] = pltpu.roll(pad_ref[...], _by_chip(chip, SHIFTS), 0).astype(BF16)

    return pl.pallas_call(
        body, name="align_shard", grid=(d // cols,),
        in_specs=[pl.BlockSpec((r, 1, cols), lambda i: (0, 0, i))],
        out_specs=pl.BlockSpec((ALIGNED_W, cols), lambda i: (0, i)),
        out_shape=jax.ShapeDtypeStruct((ALIGNED_W, d), BF16),
        scratch_shapes=[pltpu.VMEM((ALIGNED_W, cols), F32)],
        compiler_params=_params(("parallel",)),
    )(wt)


def _rms_in(x, w):
    n, d = x.shape
    tr = min(256, n)

    def body(x_ref, w_ref, h_ref):
        xv = x_ref[...]
        r = lax.rsqrt(jnp.mean(xv * xv, axis=-1, keepdims=True) + EPS)
        h_ref[...] = (xv * r * w_ref[...]).astype(BF16)

    return pl.pallas_call(
        body, name="rms_in", grid=(n // tr,),
        in_specs=[pl.BlockSpec((tr, d), lambda i: (i, 0)), pl.BlockSpec((1, d), lambda i: (0, 0))],
        out_specs=pl.BlockSpec((tr, d), lambda i: (i, 0)),
        out_shape=jax.ShapeDtypeStruct((n, d), BF16),
        compiler_params=_params(("parallel",)),
    )(x, w)


def _conv_silu(p, w_ref, taps):
    c = None
    for j in range(taps):
        t = _shift_down(p, taps - 1 - j) * w_ref[j:j + 1, :]
        c = t if c is None else c + t
    return c


def _prep_qkv(proj, cw):
    n = proj.shape[0]

    def body(p3, wq, wk, wv, q_ref, k_ref, v_ref):
        for kind, (w_ref, o_ref) in enumerate(((wq, q_ref), (wk, k_ref), (wv, v_ref))):
            c = _conv_silu(p3[:, kind * DH:(kind + 1) * DH], w_ref, 4)
            a = c * _sigmoid(c)
            if kind < 2:
                r = lax.rsqrt(jnp.sum(a * a, axis=-1, keepdims=True) + EPS)
                a = a * (r * (DH ** -0.5 if kind == 0 else 1.0))
            o_ref[...] = a

    col = pl.BlockSpec((n, DH), lambda h: (0, h))
    wcol = lambda base: pl.BlockSpec((4, DH), lambda h: (0, base + h))
    out = jax.ShapeDtypeStruct((n, GW), F32)
    return pl.pallas_call(
        body, name="prep_qkv", grid=(HEADS,),
        in_specs=[pl.BlockSpec((n, 3 * DH), lambda h: (0, h)), wcol(QB), wcol(KB), wcol(VB)],
        out_specs=[col] * 3, out_shape=[out] * 3,
        compiler_params=_params(("parallel",), 40 * 2**20),
    )(proj, cw, cw, cw)


def _prep_qkv_bwd(proj, cw, dq, dk, dv, dproj):
    n = proj.shape[0]

    def body(p3, wq, wk, wv, dq_ref, dk_ref, dv_ref, _, o3, gq, gk, gv):
        for kind, (w_ref, d_ref, g_ref) in enumerate(((wq, dq_ref, gq), (wk, dk_ref, gk), (wv, dv_ref, gv))):
            p = p3[:, kind * DH:(kind + 1) * DH]
            shifted = [_shift_down(p, 3 - j) for j in range(4)]
            c = shifted[0] * w_ref[0:1, :]
            for j in range(1, 4):
                c = c + shifted[j] * w_ref[j:j + 1, :]
            s = _sigmoid(c)
            a = c * s
            d = d_ref[...]
            if kind < 2:
                r = lax.rsqrt(jnp.sum(a * a, axis=-1, keepdims=True) + EPS)
                sc = DH ** -0.5 if kind == 0 else 1.0
                d = (sc * r) * (d - a * ((r * r) * jnp.sum(d * a, axis=-1, keepdims=True)))
            dc = d * _dsilu(c, s)
            dp = None
            for j in range(4):
                g_ref[j:j + 1, :] = jnp.sum(dc * shifted[j], axis=0, keepdims=True)
                t = _shift_up(dc, 3 - j) * w_ref[j:j + 1, :]
                dp = t if dp is None else dp + t
            o3[:, kind * DH:(kind + 1) * DH] = dp.astype(BF16)

    col = pl.BlockSpec((n, DH), lambda h: (0, h))
    wcol = lambda base: pl.BlockSpec((4, DH), lambda h: (0, base + h))
    p3spec = pl.BlockSpec((n, 3 * DH), lambda h: (0, h))
    return pl.pallas_call(
        body, name="prep_qkv_bwd", grid=(HEADS,),
        in_specs=[p3spec, wcol(QB), wcol(KB), wcol(VB), col, col, col, ANY],
        out_specs=[p3spec] + [wcol(0)] * 3,
        out_shape=[jax.ShapeDtypeStruct(dproj.shape, BF16)] + [jax.ShapeDtypeStruct((4, GW), F32)] * 3,
        input_output_aliases={7: 0},
        compiler_params=_params(("parallel",), 48 * 2**20),
    )(proj, cw, cw, cw, dq, dk, dv, dproj)


CPB = 8
SCAN_CPS = 4


def _tri(lower, rows):
    i = lax.broadcasted_iota(jnp.int32, (rows, rows), 0)
    j = lax.broadcasted_iota(jnp.int32, (rows, rows), 1)
    return jnp.where((i // CH == j // CH) & ((i >= j) if lower else (j >= i)), 1.0, 0.0)


def _lane(shape):
    return lax.broadcasted_iota(jnp.int32, shape, 1)


def _prep_bg(proj, ad):
    n = proj.shape[0]
    nch = n // CH
    cpb = CPB if nch % CPB == 0 else 1
    rows = cpb * CH

    def body(p_ref, ad_ref, bg_ref, bgt_ref):
        p = p_ref[...]
        lane = _lane(p.shape)
        beta = _sigmoid(p)
        xa = p + ad_ref[1:2, :]
        sp = jnp.maximum(xa, 0.0) + jnp.log(1.0 + jnp.exp(-jnp.abs(xa)))
        g = pltpu.roll(-jnp.exp(ad_ref[0:1, :]) * sp, DH - A_LANE + HEADS, 1)
        gc = _dot(_tri(True, rows), g, NN, P_CUM)
        bg = jnp.where(lane < HEADS, beta, jnp.where(lane < 2 * HEADS, gc, 0.0))
        bg_ref[...] = bg
        for ci in range(cpb):
            bgt_ref[ci] = bg[ci * CH:(ci + 1) * CH, :].T

    return pl.pallas_call(
        body, name="prep_bg", grid=(nch // cpb,),
        in_specs=[pl.BlockSpec((rows, DH), lambda i: (i, BAB)), pl.BlockSpec((2, DH), lambda i: (0, 0))],
        out_specs=[pl.BlockSpec((rows, DH), lambda i: (i, 0)), pl.BlockSpec((cpb, DH, CH), lambda i: (i, 0, 0))],
        out_shape=[jax.ShapeDtypeStruct((n, DH), F32), jax.ShapeDtypeStruct((nch, DH, CH), F32)],
        compiler_params=_params(("parallel",)),
    )(*_in_hbm(proj, ad))


def _prep_bg_bwd(proj, ad, dbg, dproj):
    n = proj.shape[0]
    nch = n // CH
    cpb = CPB if nch % CPB == 0 else 1
    rows = cpb * CH

    def body(p_ref, ad_ref, d_ref, _, o_ref, ga_ref, gd_ref):
        p = p_ref[...]
        d = d_ref[...]
        lane = _lane(p.shape)
        beta = _sigmoid(p)
        xa = p + ad_ref[1:2, :]
        sp = jnp.maximum(xa, 0.0) + jnp.log(1.0 + jnp.exp(-jnp.abs(xa)))
        na = -jnp.exp(ad_ref[0:1, :])
        dg = pltpu.roll(_dot(_tri(False, rows), d, NN, P_CUM), A_LANE - HEADS, 1)
        da = dg * na * _sigmoid(xa)
        is_g = lane >= A_LANE
        o_ref[...] = jnp.where(lane < HEADS, d * beta * (1.0 - beta), jnp.where(is_g, da, 0.0)).astype(BF16)
        ga = jnp.sum(jnp.where(is_g, dg * na * sp, 0.0), axis=0, keepdims=True)
        gd = jnp.sum(jnp.where(is_g, da, 0.0), axis=0, keepdims=True)

        @pl.when(pl.program_id(0) == 0)
        def _():
            ga_ref[...] = jnp.zeros_like(ga_ref)
            gd_ref[...] = jnp.zeros_like(gd_ref)

        ga_ref[...] += ga
        gd_ref[...] += gd

    one = pl.BlockSpec((1, DH), lambda i: (0, 0))
    return pl.pallas_call(
        body, name="prep_bg_bwd", grid=(nch // cpb,),
        in_specs=[pl.BlockSpec((rows, DH), lambda i: (i, BAB)), pl.BlockSpec((2, DH), lambda i: (0, 0)),
                  pl.BlockSpec((rows, DH), lambda i: (i, 0)), ANY],
        out_specs=[pl.BlockSpec((rows, DH), lambda i: (i, BAB)), one, one],
        out_shape=[jax.ShapeDtypeStruct(dproj.shape, BF16), jax.ShapeDtypeStruct((1, DH), F32),
                   jax.ShapeDtypeStruct((1, DH), F32)],
        input_output_aliases={3: 0},
        compiler_params=_params(("arbitrary",)),
    )(proj, ad, dbg, dproj)


def _gdn_out(o, proj, wg):
    n = o.shape[0]

    def body(o_ref, z_ref, w_ref, y_ref):
        ov, z = o_ref[...], z_ref[...]
        r = lax.rsqrt(jnp.mean(ov * ov, axis=-1, keepdims=True) + EPS)
        y_ref[...] = (ov * r * w_ref[...] * (z * _sigmoid(z))).astype(BF16)

    return pl.pallas_call(
        body, name="gdn_out", grid=(HEADS,),
        in_specs=[pl.BlockSpec((n, DH), lambda h: (0, h)), pl.BlockSpec((n, DH), lambda h: (0, ZB + h)),
                  pl.BlockSpec((1, DH), lambda h: (0, 0))],
        out_specs=pl.BlockSpec((n, DH), lambda h: (0, h)),
        out_shape=jax.ShapeDtypeStruct((n, 2 * GW), BF16),
        compiler_params=_params(("parallel",)),
    )(o, proj, wg)


def _gdn_out_bwd(o, proj, wg, dout_b, w_out):
    n = o.shape[0]
    d_model = dout_b.shape[1]

    def body(o_ref, z_ref, w_ref, g_ref, wo_ref, do_ref, dz_ref, gw_ref):
        ov, z, w = o_ref[...], z_ref[...], w_ref[...]
        d = _dot(g_ref[...], wo_ref[...], NT)
        r = lax.rsqrt(jnp.mean(ov * ov, axis=-1, keepdims=True) + EPS)
        nrm = ov * r
        s = _sigmoid(z)
        dz_ref[...] = (d * (nrm * w) * _dsilu(z, s)).astype(BF16)
        dn_w = d * (z * s)
        gw = jnp.sum(dn_w * nrm, axis=0, keepdims=True)
        dn = dn_w * w
        do_ref[...] = (r * (dn - nrm * jnp.mean(dn * nrm, axis=-1, keepdims=True))).astype(BF16)

        @pl.when(pl.program_id(0) == 0)
        def _():
            gw_ref[...] = jnp.zeros_like(gw_ref)

        gw_ref[...] += gw

    return pl.pallas_call(
        body, name="gdn_out_bwd", grid=(HEADS,),
        in_specs=[pl.BlockSpec((n, DH), lambda h: (0, h)), pl.BlockSpec((n, DH), lambda h: (0, ZB + h)),
                  pl.BlockSpec((1, DH), lambda h: (0, 0)), pl.BlockSpec((n, d_model), lambda h: (0, 0)),
                  pl.BlockSpec((DH, d_model), lambda h: (h, 0))],
        out_specs=[pl.BlockSpec((n, DH), lambda h: (0, h)), pl.BlockSpec((n, DH), lambda h: (0, ZB + h)),
                   pl.BlockSpec((1, DH), lambda h: (0, 0))],
        out_shape=[jax.ShapeDtypeStruct((n, GW), BF16), jax.ShapeDtypeStruct((n, GW_COLS), BF16),
                   jax.ShapeDtypeStruct((1, DH), F32)],
        compiler_params=_params(("arbitrary",), 40 * 2**20),
    )(o, proj, wg, dout_b, w_out)


def _conv_branch(proj, w3, b, mix):
    n = proj.shape[0]

    def body(p4, w_ref, b_ref, _, y_ref):
        u = p4[:, DH:2 * DH] * p4[:, 2 * DH:3 * DH]
        cc = _conv_silu(u, w_ref, 3) + b_ref[...]
        z = p4[:, 3 * DH:4 * DH]
        y_ref[...] = (p4[:, 0:DH] * cc * (z * _sigmoid(z))).astype(BF16)

    return pl.pallas_call(
        body, name="conv_branch", grid=(HEADS,),
        in_specs=[pl.BlockSpec((n, 4 * DH), lambda h: (0, h)), pl.BlockSpec((3, DH), lambda h: (0, h)),
                  pl.BlockSpec((1, DH), lambda h: (0, h)), ANY],
        out_specs=pl.BlockSpec((n, DH), lambda h: (0, HEADS + h)),
        out_shape=jax.ShapeDtypeStruct(mix.shape, BF16),
        input_output_aliases={3: 0},
        compiler_params=_params(("parallel",), 40 * 2**20),
    )(*_in_hbm(proj, w3, b, mix))


def _conv_branch_bwd(proj, w3, b, dout_b, w_out):
    n = proj.shape[0]
    d_model = dout_b.shape[1]

    def body(p4, w_ref, b_ref, g_ref, wo_ref, o4, gw_ref, gbias_ref):
        gb, gcv, hc, z = p4[:, 0:DH], p4[:, DH:2 * DH], p4[:, 2 * DH:3 * DH], p4[:, 3 * DH:4 * DH]
        d = _dot(g_ref[...], wo_ref[...], NT)
        dgb, dgc, dhc, dzc = (o4.at[:, kk * DH:(kk + 1) * DH] for kk in range(4))
        u = gcv * hc
        cc = _conv_silu(u, w_ref, 3) + b_ref[...]
        s = _sigmoid(z)
        dzc[...] = (d * (gb * cc) * _dsilu(z, s)).astype(BF16)
        dp = d * (z * s)
        dgb[...] = (dp * cc).astype(BF16)
        dcc = dp * gb
        gbias_ref[...] = jnp.sum(dcc, axis=0, keepdims=True)
        du = None
        for j in range(3):
            gw_ref[j:j + 1, :] = jnp.sum(dcc * _shift_down(u, 2 - j), axis=0, keepdims=True)
            t = _shift_up(dcc, 2 - j) * w_ref[j:j + 1, :]
            du = t if du is None else du + t
        dgc[...] = (du * hc).astype(BF16)
        dhc[...] = (du * gcv).astype(BF16)

    p4spec = pl.BlockSpec((n, 4 * DH), lambda h: (0, h))
    return pl.pallas_call(
        body, name="conv_branch_bwd", grid=(HEADS,),
        in_specs=[p4spec, pl.BlockSpec((3, DH), lambda h: (0, h)), pl.BlockSpec((1, DH), lambda h: (0, h)),
                  pl.BlockSpec((n, d_model), lambda h: (0, 0)), pl.BlockSpec((DH, d_model), lambda h: (HEADS + h, 0))],
        out_specs=[p4spec, pl.BlockSpec((3, DH), lambda h: (0, h)), pl.BlockSpec((1, DH), lambda h: (0, h))],
        out_shape=[jax.ShapeDtypeStruct((n, CW_COLS), BF16), jax.ShapeDtypeStruct((3, GW), F32),
                   jax.ShapeDtypeStruct((1, GW), F32)],
        compiler_params=_params(("parallel",), 52 * 2**20),
    )(proj, w3, b, dout_b, w_out)


def _out_loss(mix, w_out, x, tgt, wf):
    n, d = x.shape
    kdim = mix.shape[1]
    tr = min(256, n)

    def body(m_ref, wo_ref, x_ref, t_ref, w_ref, do_ref, dob_ref, gw_ref, loss_ref):
        ov = _dot(m_ref[...], wo_ref[...], NN) + x_ref[...]
        w = w_ref[...]
        r = lax.rsqrt(jnp.mean(ov * ov, axis=-1, keepdims=True) + EPS)
        nrm = ov * r
        e = nrm * w - t_ref[...]
        dy = e * (1.0 / d)
        dn = dy * w
        dout = r * (dn - nrm * jnp.mean(dn * nrm, axis=-1, keepdims=True))
        do_ref[...] = dout
        dob_ref[...] = dout.astype(BF16)

        @pl.when(pl.program_id(0) == 0)
        def _():
            gw_ref[...] = jnp.zeros_like(gw_ref)
            loss_ref[...] = jnp.zeros_like(loss_ref)

        gw_ref[...] += jnp.sum(dy * nrm, axis=0, keepdims=True)
        loss_ref[...] += (0.5 / d) * jnp.sum(jnp.sum(e * e, axis=-1, keepdims=True), axis=0, keepdims=True)

    row = pl.BlockSpec((tr, d), lambda i: (i, 0))
    return pl.pallas_call(
        body, name="out_loss", grid=(n // tr,),
        in_specs=[pl.BlockSpec((tr, kdim), lambda i: (i, 0)), pl.BlockSpec((kdim, d), lambda i: (0, 0)), row, row,
                  pl.BlockSpec((1, d), lambda i: (0, 0))],
        out_specs=[row, row, pl.BlockSpec((1, d), lambda i: (0, 0)), pl.BlockSpec((1, 1), lambda i: (0, 0))],
        out_shape=[jax.ShapeDtypeStruct((n, d), F32), jax.ShapeDtypeStruct((n, d), BF16),
                   jax.ShapeDtypeStruct((1, d), F32), jax.ShapeDtypeStruct((1, 1), F32)],
        compiler_params=_params(("arbitrary",), 40 * 2**20),
    )(mix, w_out, x, tgt, wf)


def _dh_rms_bwd(dproj, w_t, dh0, x, w, dout, tk):
    n, d = x.shape
    kdim = dproj.shape[1]
    tm = min(1024, n)
    tk = min(tk, kdim)
    nk = kdim // tk

    def body(a_ref, b_ref, dh0_ref, x_ref, w_ref, do_ref, dx_ref, gw_ref, acc):
        i, kk = pl.program_id(0), pl.program_id(1)
        part = _dot(a_ref[...], b_ref[...], NN)

        @pl.when(kk == 0)
        def _():
            acc[...] = part + dh0_ref[...]

        @pl.when(kk > 0)
        def _():
            acc[...] += part

        @pl.when((i == 0) & (kk == 0))
        def _():
            gw_ref[...] = jnp.zeros_like(gw_ref)

        @pl.when(kk == nk - 1)
        def _():
            xv, dhv = x_ref[...], acc[...]
            r = lax.rsqrt(jnp.mean(xv * xv, axis=-1, keepdims=True) + EPS)
            xn = xv * r
            dxn = dhv * w_ref[...]
            dx_ref[...] = r * (dxn - xn * jnp.mean(dxn * xn, axis=-1, keepdims=True)) + do_ref[...]
            gw_ref[...] += jnp.sum(dhv * xn, axis=0, keepdims=True)

    row = pl.BlockSpec((tm, d), lambda i, kk: (i, 0))
    one = pl.BlockSpec((1, d), lambda i, kk: (0, 0))
    return pl.pallas_call(
        body, name="dh_rms_bwd", grid=(n // tm, nk),
        in_specs=[pl.BlockSpec((tm, tk), lambda i, kk: (i, kk)), pl.BlockSpec((tk, d), lambda i, kk: (kk, 0)),
                  row, row, one, row],
        out_specs=[row, one],
        out_shape=[jax.ShapeDtypeStruct((n, d), F32), jax.ShapeDtypeStruct((1, d), F32)],
        scratch_shapes=[pltpu.VMEM((tm, d), F32)],
        compiler_params=_params(("arbitrary", "arbitrary"), 56 * 2**20),
    )(dproj, w_t, dh0, x, w, dout)


def _ij():
    i = lax.broadcasted_iota(jnp.int32, (CH, CH), 0)
    j = lax.broadcasted_iota(jnp.int32, (CH, CH), 1)
    return i, j


def _unit_lower_inverse(mats):
    i, j = _ij()
    eye = jnp.where(i == j, 1.0, 0.0)
    same16 = (i // 16) == (j // 16)
    same32 = (i // 32) == (j // 32)
    mm = lambda xs, ys: [_dot(x, y, NN, P_INV) for x, y in zip(xs, ys)]
    n1 = [jnp.where(same16, -a, 0.0) for a in mats]
    n2 = mm(n1, n1)
    n4 = mm(n2, n2)
    n8 = mm(n4, n4)
    t = [eye + x1 + x2 + x3 for x1, x2, x3 in zip(n1, n2, mm(n1, n2))]
    t = [x + y for x, y in zip(t, mm(t, n4))]
    t = [x + y for x, y in zip(t, mm(t, n8))]
    a1 = [jnp.where(same32 & jnp.logical_not(same16), a, 0.0) for a in mats]
    t = [x - y for x, y in zip(t, mm(t, mm(a1, t)))]
    a2 = [jnp.where(same32, 0.0, a) for a in mats]
    t = [x - y for x, y in zip(t, mm(t, mm(a2, t)))]
    return t


def _head_vectors(bg, bgt, h):
    bcol = bg[:, h:h + 1]
    gcol = bg[:, HEADS + h:HEADS + h + 1]
    grow = bgt[HEADS + h:HEADS + h + 1, :]
    return bcol, gcol, grow


def _decay(gcol, grow):
    i, j = _ij()
    return jnp.where(i >= j, jnp.exp(jnp.where(i >= j, gcol - grow, 0.0)), 0.0)


def _gdn_intra(q, k, v, bg, bgt):
    n = q.shape[0]
    nch = n // CH
    cps = 4 if nch % 4 == 0 else 1

    def body(q_ref, k_ref, v_ref, bg_ref, bgt_ref, u_ref, w_ref, p_ref, t_ref):
        i, j = _ij()
        items = [(ci, h) for ci in range(cps) for h in range(HEADS)]
        at = lambda ref, ci, h: ref.at[ci * CH:(ci + 1) * CH, h * DH:(h + 1) * DH]
        bgs = [bg_ref[ci * CH:(ci + 1) * CH, :] for ci in range(cps)]
        ks = [at(k_ref, ci, h)[...] for ci, h in items]
        vecs = [_head_vectors(bgs[ci], bgt_ref[ci], h) for ci, h in items]
        decs = [_decay(gcol, grow) for _, gcol, grow in vecs]
        kks = [_dot(kh, kh, NT, P_GRAM) for kh in ks]
        qks = [_dot(at(q_ref, ci, h)[...], kh, NT, P_GRAM) for (ci, h), kh in zip(items, ks)]
        ts = _unit_lower_inverse([jnp.where(i > j, bcol * kk * dec, 0.0)
                                  for (bcol, _, _), kk, dec in zip(vecs, kks, decs)])
        us = [_dot(t, at(v_ref, ci, h)[...] * bcol, NN, P_SOL) for t, (ci, h), (bcol, _, _) in zip(ts, items, vecs)]
        ws = [_dot(t, kh * (bcol * jnp.exp(gcol)), NN, P_SOL) for t, kh, (bcol, gcol, _) in zip(ts, ks, vecs)]
        for n_, (ci, h) in enumerate(items):
            p_ref[ci, h] = qks[n_] * decs[n_]
            t_ref[ci, h] = ts[n_].astype(BF16)
            at(u_ref, ci, h)[...] = us[n_]
            at(w_ref, ci, h)[...] = ws[n_].astype(BF16)

    row = pl.BlockSpec((cps * CH, GW), lambda c: (c, 0))
    sq = pl.BlockSpec((cps, HEADS, CH, CH), lambda c: (c, 0, 0, 0))
    big = jax.ShapeDtypeStruct((n, GW), F32)
    sqs = jax.ShapeDtypeStruct((nch, HEADS, CH, CH), F32)
    return pl.pallas_call(
        body, name="gdn_intra", grid=(nch // cps,),
        in_specs=[row, row, row, pl.BlockSpec((cps * CH, DH), lambda c: (c, 0)),
                  pl.BlockSpec((cps, DH, CH), lambda c: (c, 0, 0))],
        out_specs=[row, row, sq, sq],
        out_shape=[big, jax.ShapeDtypeStruct((n, GW), BF16), sqs, jax.ShapeDtypeStruct(sqs.shape, BF16)],
        compiler_params=_params(("parallel",)),
    )(q, k, v, bg, bgt)


def _gdn_scan(q, k, bg, u, w, p):
    n = q.shape[0]
    nch = n // CH
    cps = SCAN_CPS if nch % SCAN_CPS == 0 else 1

    def body(q_ref, k_ref, bg_ref, u_ref, w_ref, p_ref, o_ref, vn_ref, s_out, s_scr):
        @pl.when(pl.program_id(0) == 0)
        def _():
            s_scr[...] = jnp.zeros_like(s_scr)

        hs = range(HEADS)
        sls = [slice(h * DH, (h + 1) * DH) for h in hs]
        ss = [s_scr[h] for h in hs]
        for ci in range(cps):
            rs = slice(ci * CH, (ci + 1) * CH)
            bg = bg_ref[rs, :]
            gcols = [bg[:, HEADS + h:HEADS + h + 1] for h in hs]
            glasts = [g[CH - 1:CH, :] for g in gcols]
            wss = [_dot(w_ref[rs, sl], s, NN, P_SCAN) for sl, s in zip(sls, ss)]
            oqs = [_dot(q_ref[rs, sl] * jnp.exp(g), s, NN, P_SCAN) for sl, s, g in zip(sls, ss, gcols)]
            vns = [u_ref[rs, sl] - x for sl, x in zip(sls, wss)]
            ops = [_dot(p_ref[ci, h], vn, NN, P_SCAN) for h, vn in zip(hs, vns)]
            sns = [_dot(k_ref[rs, sl] * jnp.exp(gl - g), vn, TN, P_SCAN)
                   for sl, gl, g, vn in zip(sls, glasts, gcols, vns)]
            for h, sl in enumerate(sls):
                s_out[ci, :, sl] = ss[h].astype(BF16)
                vn_ref[rs, sl] = vns[h].astype(BF16)
                o_ref[rs, sl] = oqs[h] + ops[h]
            ss = [s * jnp.exp(gl) + sn for s, gl, sn in zip(ss, glasts, sns)]
        for h in hs:
            s_scr[h] = ss[h]

    row = pl.BlockSpec((cps * CH, GW), lambda c: (c, 0))
    big = jax.ShapeDtypeStruct((n, GW), F32)
    return pl.pallas_call(
        body, name="gdn_scan", grid=(nch // cps,),
        in_specs=[row, row, pl.BlockSpec((cps * CH, DH), lambda c: (c, 0)), row, row,
                  pl.BlockSpec((cps, HEADS, CH, CH), lambda c: (c, 0, 0, 0))],
        out_specs=[row, row, pl.BlockSpec((cps, DH, GW), lambda c: (c, 0, 0))],
        out_shape=[big, jax.ShapeDtypeStruct((n, GW), BF16), jax.ShapeDtypeStruct((nch, DH, GW), BF16)],
        scratch_shapes=[pltpu.VMEM((HEADS, DH, DH), F32)],
        compiler_params=_params(("arbitrary",)),
    )(q, k, bg, u, w, p)


def _gdn_scan_bwd(q, k, bg, w, p, vn, s_in, do):
    n = q.shape[0]
    nch = n // CH
    cps = SCAN_CPS if nch % SCAN_CPS == 0 else 1
    rev = lambda c: nch // cps - 1 - c

    def body(q_ref, k_ref, bg_ref, w_ref, p_ref, vn_ref, s_ref, do_ref,
             dqg_ref, dp_ref, du_ref, dw_ref, dks_ref, dgam_ref, ds_scr):
        @pl.when(pl.program_id(0) == 0)
        def _():
            ds_scr[...] = jnp.zeros_like(ds_scr)

        lane = _lane((1, DH))
        hs = range(HEADS)
        sls = [slice(h * DH, (h + 1) * DH) for h in hs]
        dss = [ds_scr[h] for h in hs]
        for ci in reversed(range(cps)):
            rs = slice(ci * CH, (ci + 1) * CH)
            bg = bg_ref[rs, :]
            gcols = [bg[:, HEADS + h:HEADS + h + 1] for h in hs]
            glasts = [g[CH - 1:CH, :] for g in gcols]
            ss = [s_ref[ci, :, sl] for sl in sls]
            dos = [do_ref[rs, sl] for sl in sls]
            vnl = [vn_ref[rs, sl] for sl in sls]
            dqgs = [_dot(d, s, NT, P_SCANB) for d, s in zip(dos, ss)]
            dps = [_dot(d, vn, NT, P_SCANB) for d, vn in zip(dos, vnl)]
            dvn1 = [_dot(p_ref[ci, h], d, TN, P_SCANB) for h, d in zip(hs, dos)]
            dvn2 = [_dot(k_ref[rs, sl] * jnp.exp(gl - g), ds, NN, P_SCANB)
                    for sl, gl, g, ds in zip(sls, glasts, gcols, dss)]
            dkss = [_dot(vn, ds, NT, P_SCANB) for vn, ds in zip(vnl, dss)]
            dsq = [_dot(q_ref[rs, sl] * jnp.exp(g), d, TN, P_SCANB) for sl, g, d in zip(sls, gcols, dos)]
            dvns = [a + b for a, b in zip(dvn1, dvn2)]
            dws = [_dot(dvn, s, NT, P_SCANB) for dvn, s in zip(dvns, ss)]
            dsw = [_dot(w_ref[rs, sl], dvn, TN, P_SCANB) for sl, dvn in zip(sls, dvns)]
            dgam = jnp.zeros((1, DH), F32)
            for h, sl in enumerate(sls):
                dqg_ref[rs, sl] = dqgs[h]
                dp_ref[ci, h] = dps[h]
                du_ref[rs, sl] = dvns[h].astype(BF16)
                dw_ref[rs, sl] = (-dws[h]).astype(BF16)
                dks_ref[rs, sl] = dkss[h]
                tot = jnp.sum(jnp.sum(dss[h] * ss[h], axis=-1, keepdims=True), axis=0, keepdims=True)
                dgam = dgam + jnp.where(lane == h, tot, 0.0)
            dgam_ref[ci] = jnp.broadcast_to(dgam, (8, DH))
            dss = [ds * jnp.exp(gl) + a - b for ds, gl, a, b in zip(dss, glasts, dsq, dsw)]
        for h in hs:
            ds_scr[h] = dss[h]

    row = pl.BlockSpec((cps * CH, GW), lambda c: (rev(c), 0))
    sq =pl.BlockSpec((cps, HEADS, CH, CH), lambda c: (rev(c), 0, 0, 0))
    big = jax.ShapeDtypeStruct((n, GW), F32)
    return pl.pallas_call(
        body, name="gdn_scan_bwd", grid=(nch // cps,),
        in_specs=[row, row, pl.BlockSpec((cps * CH, DH), lambda c: (rev(c), 0)), row, sq, row,
                  pl.BlockSpec((cps, DH, GW), lambda c: (rev(c), 0, 0)), row],
        out_specs=[row, sq, row, row, row, pl.BlockSpec((cps, 8, DH), lambda c: (rev(c), 0, 0))],
        out_shape=[big, jax.ShapeDtypeStruct((nch, HEADS, CH, CH), F32), jax.ShapeDtypeStruct((n, GW), BF16),
                   jax.ShapeDtypeStruct((n, GW), BF16), big,
                   jax.ShapeDtypeStruct((nch, 8, DH), F32)],
        scratch_shapes=[pltpu.VMEM((HEADS, DH, DH), F32)],
        compiler_params=_params(("arbitrary",)),
    )(q, k, bg, w, p, vn, s_in, do)


def _gdn_intra_bwd(q, k, v, bg, bgt, t, u, w, p, dqg, dp, du, dw, dks, dgam):
    n = q.shape[0]
    nch = n // CH
    cps = 2 if nch % 2 == 0 else 1

    def body(q_ref, k_ref, v_ref, bg_ref, bgt_ref, t_ref, u_ref, w_ref, p_ref,
             dqg_ref, dp_ref, du_ref, dw_ref, dks_ref, dgam_ref, dq_ref, dk_ref, dv_ref, dbg_ref):
        i, j = _ij()
        rows1 = lax.broadcasted_iota(jnp.int32, (CH, 1), 0)
        lane = _lane((CH, DH))
        rsum = lambda x: jnp.sum(x, axis=-1, keepdims=True)
        items = [(ci, h) for ci in range(cps) for h in range(HEADS)]
        at = lambda ref, it: ref.at[it[0] * CH:(it[0] + 1) * CH, it[1] * DH:(it[1] + 1) * DH]
        ld = lambda ref: [at(ref, it)[...] for it in items]
        bgs = [bg_ref[ci * CH:(ci + 1) * CH, :] for ci in range(cps)]
        qs, ks = ld(q_ref), ld(k_ref)
        vecs = [_head_vectors(bgs[ci], bgt_ref[ci], h) for ci, h in items]
        decs = [_decay(gcol, grow) for _, gcol, grow in vecs]
        ths = [t_ref[ci, h] for ci, h in items]
        drus = [_dot(th, x_, TN, P_BWD) for th, x_ in zip(ths, ld(du_ref))]
        drws = [_dot(th, x_, TN, P_BWD) for th, x_ in zip(ths, ld(dw_ref))]
        kks = [_dot(kh, kh, NT, P_GRAM) for kh in ks]
        da1 = [_dot(dru, x_, NT, P_BWD) for dru, x_ in zip(drus, ld(u_ref))]
        da2 = [_dot(drw, x_, NT, P_BWD) for drw, x_ in zip(drws, ld(w_ref))]
        das = [jnp.where(i > j, -(x_ + y_), 0.0) for x_, y_ in zip(da1, da2)]
        dkks = [da * bcol * dec for da, (bcol, _, _), dec in zip(das, vecs, decs)]
        dps = [dp_ref[ci, h] for ci, h in items]
        dqks = [dp_ * dec for dp_, dec in zip(dps, decs)]
        dq_ps = [_dot(dqk, kh, NN, P_BWD) for dqk, kh in zip(dqks, ks)]
        dk_ps = [_dot(dqk, qh, TN, P_BWD) for dqk, qh in zip(dqks, qs)]
        dk_as = [_dot(dkk, kh, NN, P_BWD) for dkk, kh in zip(dkks, ks)]
        dk_bs = [_dot(dkk, kh, TN, P_BWD) for dkk, kh in zip(dkks, ks)]
        bcols = [vc[0] for vc in vecs]
        gcols = [vc[1] for vc in vecs]
        gams = [jnp.exp(g) for g in gcols]
        glasts = [g[CH - 1:CH, :] for g in gcols]
        es = [jnp.exp(gl - g) for gl, g in zip(glasts, gcols)]
        kgs = [kh * gam for kh, gam in zip(ks, gams)]
        dqgs, dkss = ld(dqg_ref), ld(dks_ref)
        wks = [drw * kg for drw, kg in zip(drws, kgs)]
        kss = [dk_ * (kh * e) for dk_, kh, e in zip(dkss, ks, es)]
        r_beta = [rsum(dru * x_ + wk) for dru, x_, wk in zip(drus, ld(v_ref), wks)]
        r_ak = [rsum(da * kk * dec) for da, kk, dec in zip(das, kks, decs)]
        r_gc = [rsum(wk * bcol + dqg * (qh * gam) - ks_)
                for wk, bcol, dqg, qh, gam, ks_ in zip(wks, bcols, dqgs, qs, gams, kss)]
        tk_tot = [jnp.sum(jnp.sum(ks_, axis=0, keepdims=True), axis=-1, keepdims=True) for ks_ in kss]
        mdecs = [da * (bcol * kk * dec) + dp_ * p_ref[ci, h]
                 for (ci, h), da, bcol, kk, dec, dp_ in zip(items, das, bcols, kks, decs, dps)]
        r_md = [rsum(m) for m in mdecs]
        c_md = [rsum(jnp.where(i == j, jnp.sum(m, axis=0, keepdims=True), 0.0)) for m in mdecs]
        dbgs = [jnp.zeros((CH, DH), F32) for _ in range(cps)]
        for n_, (ci, h) in enumerate(items):
            at(dv_ref, (ci, h))[...] = bcols[n_] * drus[n_]
            at(dq_ref, (ci, h))[...] = gams[n_] * dqgs[n_] + dq_ps[n_]
            at(dk_ref, (ci, h))[...] = ((bcols[n_] * gams[n_]) * drws[n_] + dk_ps[n_] + dk_as[n_] + dk_bs[n_]
                                        + dkss[n_] * es[n_])
            dbeta = r_beta[n_] + r_ak[n_]
            dglast = tk_tot[n_] + dgam_ref[ci, 0:1, h:h + 1] * jnp.exp(glasts[n_])
            dgc = r_gc[n_] + r_md[n_] - c_md[n_] + jnp.where(rows1 == CH - 1, dglast, 0.0)
            dbgs[ci] = dbgs[ci] + jnp.where(lane == h, dbeta, 0.0) + jnp.where(lane == HEADS + h, dgc, 0.0)
        for ci in range(cps):
            dbg_ref[ci * CH:(ci + 1) * CH, :] = dbgs[ci]

    row = pl.BlockSpec((cps * CH, GW), lambda c: (c, 0))
    sq = pl.BlockSpec((cps, HEADS, CH, CH), lambda c: (c, 0, 0, 0))
    small = pl.BlockSpec((cps * CH, DH), lambda c: (c, 0))
    big = jax.ShapeDtypeStruct((n, GW), F32)
    return pl.pallas_call(
        body, name="gdn_intra_bwd", grid=(nch // cps,),
        in_specs=[row, row, row, small, pl.BlockSpec((cps, DH, CH), lambda c: (c, 0, 0)), sq, row, row, sq,
                  row, sq, row, row, row, pl.BlockSpec((cps, 8, DH), lambda c: (c, 0, 0))],
        out_specs=[row, row, row, small],
        out_shape=[big, big, big, jax.ShapeDtypeStruct((n, DH), F32)],
        compiler_params=_params(("parallel",)),
    )(q, k, v, bg, bgt, t, u, w, p, dqg, dp, du, dw, dks, dgam)


def _local_step(x, tgt, h, w_g, cqw, late, norm_in_w, ad, gdn_norm_w, conv_b, final_norm_w,
                on_grad_c=None, on_grad_g=None, on_q=None):
    proj_g = _matmul(h, w_g, NT, F32, 512, 1408, 1024, "mm_proj_g", n=GW_COLS, b_outer=True)
    q, k, v = _prep_qkv(proj_g, cqw)
    if on_q is not None:
        q = on_q(q)
    bg, bgt = _prep_bg(proj_g, ad)
    u, w, p, t = _gdn_intra(q, k, v, bg, bgt)
    o, vn, s_in = _gdn_scan(q, k, bg, u, w, p)
    w_c, w_out, conv_w = late(o)
    proj_c = _matmul(h, w_c, NT, F32, 512, 1024, 1024, "mm_proj_c", n=CW_COLS, b_outer=True)
    mix = _conv_branch(proj_c, conv_w, conv_b, _gdn_out(o, proj_g, gdn_norm_w))
    dout, dout_b, g_fn, loss = _out_loss(mix, w_out, x, tgt, final_norm_w)

    g_wout = _matmul(mix, dout_b, TN, BF16, 512, 512, 2048, "mm_gwout")
    do, dproj_g, g_gn = _gdn_out_bwd(o, proj_g, gdn_norm_w, dout_b, w_out)
    dproj_c, g_cw, g_cb = _conv_branch_bwd(proj_c, conv_w, conv_b, dout_b, w_out)
    g_c = _matmul(dproj_c, h, TN, BF16, 1024, 512, 2048, "mm_gwin_c")
    if on_grad_c is not None:
        do = on_grad_c(g_c, g_wout, do)
    dqg, dp, du, dw, dks, dgam = _gdn_scan_bwd(q, k, bg, w, p, vn, s_in, do)
    dq, dk, dv, dbg = _gdn_intra_bwd(q, k, v, bg, bgt, t, u, w, p, dqg, dp, du, dw, dks, dgam)
    dproj_g, gq, gk, gv = _prep_qkv_bwd(proj_g, cqw, dq, dk, dv, dproj_g)
    dproj_g, g_al, g_dt = _prep_bg_bwd(proj_g, ad, dbg, dproj_g)
    g_g = _matmul(dproj_g, h, TN, BF16, 1408, 512, 2048, "mm_gwin_g")
    if on_grad_g is not None:
        dproj_g = on_grad_g(g_g, dproj_g)
    dh = _matmul(dproj_g, w_g, NN, F32, 1024, 1024, 1408, "mm_dh_g")
    gx, g_nin = _dh_rms_bwd(dproj_c, w_c, dh, x, norm_in_w, dout, 1024)
    small = dict(nin=g_nin, cb=g_cb, fn=g_fn, al=g_al, dt=g_dt, gn=g_gn, cq=(gq, gk, gv), cw=g_cw, loss=loss)
    return gx, small, (g_g, g_c, g_wout)


def _place():
    x, y, c = lax.axis_index("x"), lax.axis_index("y"), lax.axis_index("c")
    chips = [(1 - x, y), (x, 1 - y), (1 - x, 1 - y)]
    return x, y, c, chips


def _blk(ref, b):
    if isinstance(b, int):
        return ref.at[b * DH:(b + 1) * DH, :]
    return ref.at[pl.ds(pl.multiple_of(b * DH, DH), DH), :]


HBM = pl.BlockSpec(memory_space=pltpu.HBM)
SEM = pl.BlockSpec(memory_space=pltpu.SEMAPHORE)
EFFECT = pltpu.SideEffectType.DATAFLOW_SIDE_EFFECTING


def _split_start(name, issue, bufs, n_sems):
    nbuf = len(bufs)

    def body(*refs):
        issue(refs[:nbuf], refs[nbuf], refs[nbuf + 1])
        refs[-1][...] = jnp.zeros_like(refs[-1])

    out = pl.pallas_call(
        body, name=name,
        out_shape=(pltpu.SemaphoreType.DMA((n_sems,)), pltpu.SemaphoreType.DMA((n_sems,)),
                   *[pltpu.HBM(b.shape, b.dtype) for b in bufs], jax.ShapeDtypeStruct((8, DH), F32)),
        in_specs=[HBM] * nbuf,
        out_specs=(SEM, SEM, *[HBM] * nbuf, pl.BlockSpec(memory_space=pltpu.VMEM)),
        input_output_aliases={a: 2 + a for a in range(nbuf)},
        compiler_params=pltpu.CompilerParams(has_side_effects=EFFECT),
    )(*[pltpu.with_memory_space_constraint(b, pltpu.HBM) for b in bufs])
    return out[0], out[1], list(out[2:2 + nbuf]), out[-1]


def _split_wait(name, await_, send_sems, recv_sems, bufs, after):
    nbuf = len(bufs)
    after = list(after) if isinstance(after, (list, tuple)) else [after]

    def body(*refs):
        await_(refs[:nbuf], refs[nbuf], refs[nbuf + 1])

    out = pl.pallas_call(
        body, name=name,
        out_shape=tuple(pltpu.HBM(b.shape, b.dtype) for b in bufs),
        in_specs=[HBM] * nbuf + [SEM, SEM] + [ANY] * len(after), out_specs=tuple([HBM] * nbuf),
        input_output_aliases={a: a for a in range(nbuf)},
        compiler_params=pltpu.CompilerParams(has_side_effects=EFFECT),
    )(*bufs, send_sems, recv_sems, *after)
    return list(out)


def _phase_blocks(chip, phase, edges, parity=None):
    return [(b, blk) for b, (grp, blk) in enumerate(_shard_blocks(chip, edges))
            if grp == phase and (parity is None or b % 2 == parity)]


def _cols(ref, nblk):
    return ref.at[0:nblk * DH, :]


def _block_table(chip, edges, spare_g, spare_c):
    rows = []
    for s in range(4):
        sb = _shard_blocks(s, edges)
        rows.append([[blk if grp == "g" else spare_g for grp, blk in sb],
                     [blk if grp == "c" else spare_c for grp, blk in sb],
                     [int(grp == "g") for grp, _ in sb], [s] * ALIGNED_BLOCKS])
    return jnp.asarray(rows, jnp.int32)[chip]


def _place_own(a_shard, wo, cq, cw, bufs):
    d = a_shard.shape[1]
    chip = 2 * lax.axis_index("x") + lax.axis_index("y")

    def body(t_ref, a_ref, wo_ref, cq_ref, cw_ref, *refs):
        wg_ref, wc_ref, wog_ref, cqg_ref, cwg_ref = refs[5:]
        wg_ref[...] = a_ref[...]
        wc_ref[...] = a_ref[...]

        @pl.when(pl.program_id(0) == 0)
        def _():
            wog_ref[0] = wo_ref[...]
            cqg_ref[0] = cq_ref[...]
            cwg_ref[0] = cw_ref[...]

    whole = lambda s: pl.BlockSpec(s.shape, lambda b, t: (0,) * s.ndim)
    slot = lambda s: pl.BlockSpec((1,) + s.shape, lambda b, t: (t[3, 0],) + (0,) * s.ndim)
    return pl.pallas_call(
        body, name="place_own",
        grid_spec=pltpu.PrefetchScalarGridSpec(
            num_scalar_prefetch=1, grid=(ALIGNED_BLOCKS,),
            in_specs=[pl.BlockSpec((DH, d), lambda b, t: (b, 0)), whole(wo), whole(cq), whole(cw)] + [ANY] * 5,
            out_specs=[pl.BlockSpec((DH, d), lambda b, t: (t[0, b], 0)),
                       pl.BlockSpec((DH, d), lambda b, t: (t[1, b], 0)), slot(wo), slot(cq), slot(cw)]),
        out_shape=[jax.ShapeDtypeStruct(b.shape, b.dtype) for b in bufs],
        input_output_aliases={5 + a: a for a in range(5)},
        compiler_params=_params(("arbitrary",)),
    )(_block_table(chip, True, G_SPARE, C_SPARE), a_shard, wo, cq, cw, *bufs)


def _tie(x, token, name):
    def body(x_ref, t_ref, o_ref):
        del x_ref, t_ref, o_ref

    return pl.pallas_call(
        body, name=name, in_specs=[ANY, ANY], out_specs=ANY,
        out_shape=jax.ShapeDtypeStruct(x.shape, x.dtype), input_output_aliases={0: 0},
    )(x, token)


def _gather_start(phase, a_shard, w_grp, singles):
    ns = len(singles)

    def issue(refs, send_sems, recv_sems):
        a_ref, w_ref = refs[0], refs[1]
        x, y, c, chips = _place()
        mine = 2 * x + y
        for jj, (px, py) in enumerate(chips):
            to = dict(device_id=(px, py, c), device_id_type=MESH)
            for a in range(ns):
                pltpu.make_async_remote_copy(
                    src_ref=refs[2 + 2 * a], dst_ref=refs[3 + 2 * a].at[mine],
                    send_sem=send_sems.at[(1 + ns) * jj + 1 + a], recv_sem=recv_sems.at[(1 + ns) * jj + 1 + a],
                    **to).start()
        for s in range(4):
            for par in range(2):
                blocks = _phase_blocks(s, phase, True, par)
                if blocks:
                    @pl.when((mine == s) & (c == par))
                    def _():
                        for b, blk in blocks:
                            for jj, (px, py) in enumerate(chips):
                                pltpu.make_async_remote_copy(
                                    src_ref=_blk(a_ref, b), dst_ref=_blk(w_ref, blk),
                                    send_sem=send_sems.at[(1 + ns) * jj], recv_sem=recv_sems.at[(1 + ns) * jj],
                                    device_id=(px, py, c), device_id_type=MESH).start()

    bufs = [a_shard, w_grp] + [t for pair in singles for t in pair]
    return _split_start("gather_start_" + phase, issue, bufs, 3 * (1 + ns))


def _gather_wait(phase, send_sems, recv_sems, bufs, after):
    ns = (len(bufs) - 2) // 2

    def await_(refs, send_sems, recv_sems):
        a_ref, w_ref = refs[0], refs[1]
        x, y, c, chips = _place()
        mine = 2 * x + y
        for jj, (px, py) in enumerate(chips):
            to = dict(device_id=(px, py, c), device_id_type=MESH)
            peer = 2 * px + py
            for a in range(ns):
                cp = pltpu.make_async_remote_copy(
                    src_ref=refs[2 + 2 * a], dst_ref=refs[3 + 2 * a].at[mine],
                    send_sem=send_sems.at[(1 + ns) * jj + 1 + a], recv_sem=recv_sems.at[(1 + ns) * jj + 1 + a], **to)
                cp.wait_recv()
                cp.wait_send()
            for s in range(4):
                for par in range(2):
                    nblk = len(_phase_blocks(s, phase, True, par))
                    if nblk:
                        both = pltpu.make_async_remote_copy(
                            src_ref=_cols(a_ref, nblk), dst_ref=_cols(w_ref, nblk),
                            send_sem=send_sems.at[(1 + ns) * jj], recv_sem=recv_sems.at[(1 + ns) * jj], **to)

                        @pl.when((peer == s) & (c == par))
                        def _():
                            both.wait_recv()

                        @pl.when((mine == s) & (c == par))
                        def _():
                            both.wait_send()

    return _split_wait("gather_wait_" + phase, await_, send_sems, recv_sems, bufs, after)


def _sibling_forward_parts(phase):
    def each(w_ref, send_sems, recv_sems, start):
        x, y, c, chips = _place()
        to = dict(device_id=(x, y, 1 - c), device_id_type=MESH)
        for jj, (px, py) in enumerate(chips):
            peer = 2 * px + py
            for s in range(4):
                for par in range(2):
                    mine_blocks = _phase_blocks(s, phase, True, par)
                    theirs = len(_phase_blocks(s, phase, True, 1 - par))
                    if not (mine_blocks or theirs):
                        continue

                    @pl.when((peer == s) & (c == par))
                    def _():
                        if start:
                            for _, blk in mine_blocks:
                                pltpu.make_async_remote_copy(
                                    src_ref=_blk(w_ref, blk), dst_ref=_blk(w_ref, blk),
                                    send_sem=send_sems.at[jj], recv_sem=recv_sems.at[jj], **to).start()
                            return
                        if theirs:
                            pltpu.make_async_remote_copy(
                                src_ref=_cols(w_ref, theirs), dst_ref=_cols(w_ref, theirs),
                                send_sem=send_sems.at[jj], recv_sem=recv_sems.at[jj], **to).wait_recv()
                        if mine_blocks:
                            pltpu.make_async_remote_copy(
                                src_ref=_cols(w_ref, len(mine_blocks)), dst_ref=_cols(w_ref, len(mine_blocks)),
                                send_sem=send_sems.at[jj], recv_sem=recv_sems.at[jj], **to).wait_send()

    issue = lambda refs, send_sems, recv_sems: each(refs[0], send_sems, recv_sems, True)
    await_ = lambda refs, send_sems, recv_sems: each(refs[0], send_sems, recv_sems, False)
    return issue, await_


def _sibling_forward(phase, w_grp):
    issue, await_ = _sibling_forward_parts(phase)

    def body(w_in_ref, w_ref, send_sems, recv_sems):
        del w_in_ref
        issue([w_ref], send_sems, recv_sems)
        await_([w_ref], send_sems, recv_sems)

    return pl.pallas_call(
        body, name="sibling_forward_" + phase, in_specs=[ANY], out_specs=ANY,
        out_shape=jax.ShapeDtypeStruct(w_grp.shape, w_grp.dtype), input_output_aliases={0: 0},
        scratch_shapes=[pltpu.SemaphoreType.DMA((3,)), pltpu.SemaphoreType.DMA((3,))],
    )(w_grp)


def _merge_edges(w, edge0, mixed, name):
    d = w.shape[1]

    def body(e_ref, o_ref):
        o_ref[...] = e_ref[0:DH, :] + e_ref[DH:2 * DH, :]

    def to_block(i):
        r = mixed[-1]
        for kk in range(len(mixed) - 2, -1, -1):
            r = jnp.where(i == kk, mixed[kk], r)
        return r

    return pl.pallas_call(
        body, name=name, grid=(len(mixed),),
        in_specs=[pl.BlockSpec((2 * DH, d), lambda i: (edge0 // 2 + i, 0))],
        out_specs=pl.BlockSpec((DH, d), lambda i: (to_block(i), 0)),
        out_shape=jax.ShapeDtypeStruct(w.shape, w.dtype),
        input_output_aliases={0: 0},
        compiler_params=_params(("arbitrary",)),
    )(w)


def _scatter_start(phase, g_grp, land, singles, halved=False):
    ns = len(singles)

    def issue(refs, send_sems, recv_sems):
        g_ref, land_ref = refs[0], refs[1]
        x, y, c, chips = _place()
        for jj, (px, py) in enumerate(chips):
            to = dict(device_id=(px, py, c), device_id_type=MESH)
            peer = 2 * px + py
            for a in range(ns):
                pltpu.make_async_remote_copy(
                    src_ref=refs[2 + 2 * a].at[peer], dst_ref=refs[3 + 2 * a].at[jj],
                    send_sem=send_sems.at[(1 + ns) * jj + 1 + a], recv_sem=recv_sems.at[(1 + ns) * jj + 1 + a],
                    **to).start()
            for s in range(4):
                for par in ((0, 1) if halved else (None,)):
                    blocks = _phase_blocks(s, phase, False, par)
                    if blocks:
                        @pl.when((peer == s) if par is None else ((peer == s) & (c == par)))
                        def _():
                            for b, blk in blocks:
                                pltpu.make_async_remote_copy(
                                    src_ref=_blk(g_ref, blk), dst_ref=_blk(land_ref.at[jj], b),
                                    send_sem=send_sems.at[(1 + ns) * jj], recv_sem=recv_sems.at[(1 + ns) * jj],
                                    **to).start()

    bufs = [g_grp, land] + [t for pair in singles for t in pair]
    return _split_start("scatter_start_" + phase, issue, bufs, 3 * (1 + ns))


def _scatter_wait(phase, send_sems, recv_sems, bufs, after, halved=False):
    ns = (len(bufs) - 2) // 2

    def await_(refs, send_sems, recv_sems):
        g_ref, land_ref = refs[0], refs[1]
        x, y, c, chips = _place()
        mine = 2 * x + y
        for jj, (px, py) in enumerate(chips):
            to = dict(device_id=(px, py, c), device_id_type=MESH)
            peer = 2 * px + py
            for a in range(ns):
                cp = pltpu.make_async_remote_copy(
                    src_ref=refs[2 + 2 * a].at[peer], dst_ref=refs[3 + 2 * a].at[jj],
                    send_sem=send_sems.at[(1 + ns) * jj + 1 + a], recv_sem=recv_sems.at[(1 + ns) * jj + 1 + a], **to)
                cp.wait_recv()
                cp.wait_send()
            for s in range(4):
                for par in ((0, 1) if halved else (None,)):
                    nblk = len(_phase_blocks(s, phase, False, par))
                    if nblk:
                        both = pltpu.make_async_remote_copy(
                            src_ref=_cols(g_ref, nblk), dst_ref=_cols(land_ref.at[jj], nblk),
                            send_sem=send_sems.at[(1 + ns) * jj], recv_sem=recv_sems.at[(1 + ns) * jj], **to)

                        @pl.when((mine == s) if par is None else ((mine == s) & (c == par)))
                        def _():
                            both.wait_recv()

                        @pl.when((peer == s) if par is None else ((peer == s) & (c == par)))
                        def _():
                            both.wait_send()

    return _split_wait("scatter_wait_" + phase, await_, send_sems, recv_sems, bufs, after)


def _needed_blocks(phase, parity):
    return sorted({blk for s in range(4) for _, blk in _phase_blocks(s, phase, False, parity)})


def _pair_reduce(phase, g_grp):
    n, d = g_grp.shape

    def swap(g_ref, sib_ref, send_sem, recv_sem):
        x, y, c, _ = _place()
        to = dict(device_id=(x, y, 1 - c), device_id_type=MESH)
        for par in range(2):
            give, get = _needed_blocks(phase, 1 - par), _needed_blocks(phase, par)

            @pl.when(c == par)
            def _():
                for blk in give:
                    pltpu.make_async_remote_copy(src_ref=_blk(g_ref, blk), dst_ref=_blk(sib_ref, blk),
                                                 send_sem=send_sem, recv_sem=recv_sem, **to).start()
                pltpu.make_async_remote_copy(src_ref=_cols(g_ref, len(get)), dst_ref=_cols(sib_ref, len(get)),
                                             send_sem=send_sem, recv_sem=recv_sem, **to).wait_recv()
                pltpu.make_async_remote_copy(src_ref=_cols(g_ref, len(give)), dst_ref=_cols(sib_ref, len(give)),
                                             send_sem=send_sem, recv_sem=recv_sem, **to).wait_send()

    sib = pl.pallas_call(
        swap, name="pair_swap_" + phase, in_specs=[ANY], out_specs=ANY,
        out_shape=jax.ShapeDtypeStruct((n, d), g_grp.dtype),
        scratch_shapes=[pltpu.SemaphoreType.DMA, pltpu.SemaphoreType.DMA],
    )(*_in_hbm(g_grp))

    lists = [_needed_blocks(phase, par) for par in range(2)]
    longest = max(len(t) for t in lists)
    table = jnp.asarray([t + [t[-1]] * (longest - len(t)) for t in lists], jnp.int32)[lax.axis_index("c")]

    def add(t_ref, a_ref, b_ref, o_ref):
        o_ref[...] = (a_ref[...].astype(F32) + b_ref[...].astype(F32)).astype(o_ref.dtype)

    blk = pl.BlockSpec((DH, d), lambda i, t: (t[i], 0))
    return pl.pallas_call(
        add, name="pair_add_" + phase,
        grid_spec=pltpu.PrefetchScalarGridSpec(num_scalar_prefetch=1, grid=(longest,),
                                               in_specs=[blk, blk], out_specs=blk),
        out_shape=jax.ShapeDtypeStruct((n, d), g_grp.dtype),
        compiler_params=_params(("arbitrary",)),
    )(table, g_grp, sib)


def _sum_shard(g_g, g_c, land):
    d = g_g.shape[1]
    chip = 2 * lax.axis_index("x") + lax.axis_index("y")

    def body(t_ref, gg_ref, gc_ref, land_ref, o_ref):
        b = pl.program_id(0)
        in_g = t_ref[2, b] == 1
        own = jnp.where(in_g, gg_ref[...].astype(F32), gc_ref[...].astype(F32))
        for jj in range(3):
            own = own + land_ref[jj].astype(F32)
        o_ref[...] = jnp.where(in_g & (b % 2 != lax.axis_index("c")), 0.0, own)

    return pl.pallas_call(
        body, name="sum_w_in",
        grid_spec=pltpu.PrefetchScalarGridSpec(
            num_scalar_prefetch=1, grid=(ALIGNED_BLOCKS,),
            in_specs=[pl.BlockSpec((DH, d), lambda b, t: (t[0, b], 0)), pl.BlockSpec((DH, d), lambda b, t: (t[1, b], 0)),
                      pl.BlockSpec((3, DH, d), lambda b, t: (0, b, 0))],
            out_specs=pl.BlockSpec((DH, d), lambda b, t: (b, 0))),
        out_shape=jax.ShapeDtypeStruct((ALIGNED_W, d), F32),
        compiler_params=_params(("arbitrary",)),
    )(_block_table(chip, False, 0, 0), g_g, g_c, land)


def _sum_rows(stack, land, rows):
    _, r, d = stack.shape
    rows = min(rows, r)
    chip = 2 * lax.axis_index("x") + lax.axis_index("y")

    def body(t_ref, own_ref, land_ref, o_ref):
        acc = own_ref[0].astype(F32)
        for jj in range(3):
            acc = acc + land_ref[jj].astype(F32)
        o_ref[...] = acc

    return pl.pallas_call(
        body, name="sum_w_out",
        grid_spec=pltpu.PrefetchScalarGridSpec(
            num_scalar_prefetch=1, grid=(r // rows,),
            in_specs=[pl.BlockSpec((1, rows, d), lambda i, t: (t[0], i, 0)),
                      pl.BlockSpec((3, rows, d), lambda i, t: (0, i, 0))],
            out_specs=pl.BlockSpec((rows, d), lambda i, t: (i, 0))),
        out_shape=jax.ShapeDtypeStruct((r, d), F32),
        compiler_params=_params(("arbitrary",)),
    )(jnp.reshape(chip, (1,)).astype(jnp.int32), stack, land)


def _exchange_parts(n_swap, with_pack):
    def copies(refs, send_sems, recv_sems):
        x, y, c, _ = _place()
        me = 4 * x + 2 * y + c
        cps = [pltpu.make_async_remote_copy(
            src_ref=refs[2 * a], dst_ref=refs[2 * a + 1], send_sem=send_sems.at[a], recv_sem=recv_sems.at[a],
            device_id=(x, y, 1 - c), device_id_type=MESH) for a in range(n_swap)]
        if with_pack:
            pack_ref, packs = refs[2 * n_swap], refs[2 * n_swap + 1]
            for r in range(1, 8):
                dx, dy, dc = (r >> 2) & 1, (r >> 1) & 1, r & 1
                peer = (x + dx - 2 * x * dx, y + dy - 2 * y * dy, c + dc - 2 * c * dc)
                cps.append(pltpu.make_async_remote_copy(
                    src_ref=pack_ref, dst_ref=packs.at[me], send_sem=send_sems.at[n_swap + r - 1],
                    recv_sem=recv_sems.at[n_swap + r - 1], device_id=peer, device_id_type=MESH))
        return cps

    def issue(refs, send_sems, recv_sems):
        for cp in copies(refs, send_sems, recv_sems):
            cp.start()

    def await_(refs, send_sems, recv_sems):
        cps = copies(refs, send_sems, recv_sems)
        for cp in cps:
            cp.wait_recv()
        for cp in cps:
            cp.wait_send()

    return issue, await_, n_swap + (7 if with_pack else 0)


def _sum_packs(pack, packs):
    x, y, c = lax.axis_index("x"), lax.axis_index("y"), lax.axis_index("c")
    me = jnp.reshape(4 * x + 2 * y + c, (1,)).astype(jnp.int32)

    def body(me_ref, own_ref, p_ref, o_ref):
        acc = jnp.where(me_ref[0] == 0, own_ref[...], p_ref[0])
        for d in range(1, 8):
            acc = acc + jnp.where(me_ref[0] == d, own_ref[...], p_ref[d])
        o_ref[...] = acc

    full = lambda s: pl.BlockSpec(s.shape, lambda i, t: (0,) * s.ndim)
    return pl.pallas_call(
        body, name="sum_packs",
        grid_spec=pltpu.PrefetchScalarGridSpec(num_scalar_prefetch=1, grid=(1,), in_specs=[full(pack), full(packs)],
                                               out_specs=full(pack)),
        out_shape=jax.ShapeDtypeStruct(pack.shape, F32),
    )(me, pack, packs)


def _adamw_update(g, w_ref, m_ref, v_ref, go, do, mo, vo):
    c1 = 1.0 / (1.0 - ADAM_B1 ** ADAM_STEP)
    c2 = 1.0 / (1.0 - ADAM_B2 ** ADAM_STEP)
    mn = ADAM_B1 * m_ref[...] + (1.0 - ADAM_B1) * g
    vn = ADAM_B2 * v_ref[...] + (1.0 - ADAM_B2) * (g * g)
    go[...] = g
    mo[...] = mn
    vo[...] = vn
    do[...] = -ADAM_LR * ((mn * c1) / (jnp.sqrt(vn * c2) + ADAM_EPS) + ADAM_WD * w_ref[...])


def _adamw(w, m, v, g1, g2, rows, name):
    r, cdim = w.shape
    rows = min(rows, r)

    def body(w_ref, m_ref, v_ref, g1_ref, g2_ref, *outs):
        _adamw_update(g1_ref[...] + g2_ref[...], w_ref, m_ref, v_ref, *outs)

    blk = pl.BlockSpec((rows, cdim), lambda i: (i, 0))
    shp = jax.ShapeDtypeStruct((r, cdim), F32)
    return pl.pallas_call(
        body, name=name, grid=(r // rows,),
        in_specs=[blk] * 5, out_specs=[blk] * 4, out_shape=[shp] * 4,
        compiler_params=_params(("parallel",), 20 * rows * cdim * 4 + 8 * 2**20),
    )(*_in_hbm(w, m, v, g1, g2))


def _adamw_small(w_s, m_s, v_s, tot):
    per_row = GW // DH
    cq_blocks, cw_blocks = 3 * per_row // 4, per_row // 4
    cq_lanes, cw_lanes = cq_blocks * DH, cw_blocks * DH
    shapes = [(1, GW), (4, cq_lanes), (1, HEADS), (1, HEADS), (1, DH), (3, cw_lanes), (1, GW), (per_row, DH)]

    def body(t_ref, w_ref, m_ref, v_ref, *refs):
        g_scr, kinds = refs[len(refs) - 5], refs[len(refs) - 4:]
        chip = 2 * lax.axis_index("x") + lax.axis_index("y")

        def mine(first_row, rows_per_tap, nblk, t, jb):
            out = None
            for k in reversed(range(4)):
                b = nblk * k + jb
                row = first_row + rows_per_tap * t + b // per_row
                cand = t_ref[row:row + 1, (b % per_row) * DH:(b % per_row + 1) * DH]
                out = cand if out is None else jnp.where(chip == k, cand, out)
            return out

        g_scr[...] = jnp.zeros_like(g_scr)
        for src, dst in ((R_NIN, S_NIN), (R_CB, S_CB), (R_FN, S_FN), (R_AD, S_AD), (R_GN, S_GN)):
            g_scr[dst:dst + 1, :] = t_ref[src:src + 1, :]
        for t in range(4):
            for jb in range(cq_blocks):
                g_scr[S_CQ + t:S_CQ + t + 1, jb * DH:(jb + 1) * DH] = mine(R_CQ, 3, cq_blocks, t, jb)
        for t in range(3):
            for jb in range(cw_blocks):
                g_scr[S_CW + t:S_CW + t + 1, jb * DH:(jb + 1) * DH] = mine(R_CW, 1, cw_blocks, t, jb)
        _adamw_update(g_scr[...], w_ref, m_ref, v_ref, *kinds)
        for kk, a in enumerate(kinds):
            nin, cq, al, dt, gn, cw, cb, fn = refs[8 * kk:8 * kk + 8]
            nin[...] = a[S_NIN:S_NIN + 1, :]
            cq[...] = a[S_CQ:S_CQ + 4, 0:cq_lanes]
            al[...] = a[S_AD:S_AD + 1, 0:HEADS]
            dt[...] = a[S_AD:S_AD + 1, HEADS:2 * HEADS]
            gn[...] = a[S_GN:S_GN + 1, 0:DH]
            cw[...] = a[S_CW:S_CW + 3, 0:cw_lanes]
            cb[...] = a[S_CB:S_CB + 1, :]
            for k in range(per_row):
                fn[k:k + 1, :] = a[S_FN:S_FN + 1, k * DH:(k + 1) * DH]

    out = pl.pallas_call(
        body, name="adamw_small",
        out_shape=[jax.ShapeDtypeStruct(s, F32) for s in shapes] * 4,
        scratch_shapes=[pltpu.VMEM(w_s.shape, F32)] * 5,
    )(tot, w_s, m_s, v_s)
    return [out[8 * kk:8 * kk + 8] for kk in range(4)]


def _adamw_shard(wt, mt, vt, g1, g2):
    r, d = wt.shape
    cols = min(256, d)

    def body(w_ref, m_ref, v_ref, g_ref, g2_ref, go, do, mo, vo, pad_ref):
        chip = 2 * lax.axis_index("x") + lax.axis_index("y")
        back = [(ALIGNED_W - s) % ALIGNED_W for s in SHIFTS]
        pad_ref[...] = pltpu.roll(g_ref[...] + g2_ref[...], _by_chip(chip, back), 0)
        outs = [o.at[:, 0, :] for o in (go, do, mo, vo)]
        _adamw_update(pad_ref[0:r, :], w_ref, m_ref, v_ref, *outs)

    blk = pl.BlockSpec((r, cols), lambda i: (0, i))
    gblk = pl.BlockSpec((ALIGNED_W, cols), lambda i: (0, i))
    oblk = pl.BlockSpec((r, 1, cols), lambda i: (0, 0, i))
    shp = jax.ShapeDtypeStruct((r, 1, d), F32)
    return pl.pallas_call(
        body, name="adamw_w_in", grid=(d // cols,),
        in_specs=[blk] * 3 + [gblk] * 2, out_specs=[oblk] * 4, out_shape=[shp] * 4,
        scratch_shapes=[pltpu.VMEM((ALIGNED_W, cols), F32)],
        compiler_params=_params(("parallel",), 24 * ALIGNED_W * cols * 4 + 8 * 2**20),
    )(wt, mt, vt, g1, g2)


def _pad_lanes(a, width):
    return jnp.pad(a, ((0, 0), (0, width - a.shape[1])))


def _gathered_to_full(g):
    return jnp.transpose(g, (1, 0, 2)).reshape(g.shape[1], 4 * g.shape[2])


def _row(a):
    return _pad_lanes(a.reshape(1, -1), 1024)


S_NIN, S_CB, S_FN, S_AD, S_GN, S_CQ, S_CW = 0, 1, 2, 3, 4, 5, 9


def _small_pack(nin, cb, fn, al, dt, gn, cqw_shard, cw_shard):
    ad = jnp.concatenate([al.reshape(1, -1), dt.reshape(1, -1)], axis=1)
    rows = [_row(nin), _row(cb), _row(fn), _row(ad), _row(gn), _pad_lanes(cqw_shard, 1024),
            _pad_lanes(cw_shard, 1024)]
    out = jnp.concatenate(rows, axis=0)
    return jnp.pad(out, ((0, 16 - out.shape[0]), (0, 0)))


def kernel(x, norm_in_w, w_in, conv_qkv_w, A_log, dt_bias, gdn_norm_w, conv_w, conv_b, w_out, final_norm_w, loss_target, m_norm_in_w, m_w_in, m_conv_qkv_w, m_A_log, m_dt_bias, m_gdn_norm_w, m_conv_w, m_conv_b, m_w_out, m_final_norm_w, v_norm_in_w, v_w_in, v_conv_qkv_w, v_A_log, v_dt_bias, v_gdn_norm_w, v_conv_w, v_conv_b, v_w_out, v_final_norm_w):
    a_shard = _align_shard(jnp.transpose(w_in, (2, 0, 1)))
    d_model = x.shape[-1]
    stack = lambda s: lax.empty((4,) + s.shape, s.dtype)
    wg0 = lax.empty((WG_BLOCKS * DH, d_model), BF16)
    wc0 = lax.empty((WC_BLOCKS * DH, d_model), BF16)
    ss_g, rs_g, bufs_g, tok_g = _gather_start("g", a_shard, wg0, [(conv_qkv_w[0], stack(conv_qkv_w[0]))])
    wo_b = _cast_bf16(w_out[0], 256, "cast_w_out", tok_g)
    ss_c, rs_c, bufs_c, tok_c = _gather_start("c", bufs_g[0], wc0,
                                              [(conv_w[0], stack(conv_w[0])), (wo_b, stack(wo_b))])
    wg1, wc1, wog1, cqg1, cwg1 = _place_own(bufs_c[0], bufs_c[4], bufs_g[2], bufs_c[2],
                                            [bufs_g[1], bufs_c[1], bufs_c[5], bufs_g[3], bufs_c[3]])
    x0 = x[0]
    h = _rms_in(x0, _tie(_tie(norm_in_w, tok_g, "after_gather_start_g"), tok_c, "after_gather_start_c"))
    adam_in = [jnp.transpose(a[0]) for a in (w_in, m_w_in, v_w_in)]
    sp = lambda nin, cb, fn, al, dt, gn, cq, cwv: _small_pack(nin, cb, fn, al, dt, gn, cq[0], cwv[0])
    w_s = sp(norm_in_w, conv_b, final_norm_w, A_log, dt_bias, gdn_norm_w, conv_qkv_w, conv_w)
    m_s = sp(m_norm_in_w, m_conv_b, m_final_norm_w, m_A_log, m_dt_bias, m_gdn_norm_w, m_conv_qkv_w, m_conv_w)
    v_s = sp(v_norm_in_w, v_conv_b, v_final_norm_w, v_A_log, v_dt_bias, v_gdn_norm_w, v_conv_qkv_w, v_conv_w)
    a_thru, wg, _, cq_g = _gather_wait("g", ss_g, rs_g, [bufs_c[0], wg1, bufs_g[2], cqg1],
                                       [h, w_s, m_s, v_s] + adam_in[1:])
    w_g = _merge_edges(_sibling_forward("g", wg), G_EDGE, G_MIXED, "merge_edges_g")
    cqw = _gathered_to_full(cq_g)
    ad = jnp.pad(jnp.concatenate([A_log, dt_bias], axis=0), ((0, 0), (A_LANE, 0)))
    fwd_c = {}

    def on_q(q):
        _, wc, _, cw_g, _, wo_g = _gather_wait("c", ss_c, rs_c,
                                               [a_thru, wc1, bufs_c[2], cwg1, bufs_c[4], wog1], q)
        issue, _ = _sibling_forward_parts("c")
        ss, rs, (wc,), tok = _split_start("sibling_forward_start_c", issue, [wc], 3)
        fwd_c.update(ss=ss, rs=rs, wc=wc, cw_g=cw_g, wo_g=wo_g)
        return _tie(q, tok, "after_sibling_forward_start_c")

    def late(o):
        _, await_ = _sibling_forward_parts("c")
        (wc,) = _split_wait("sibling_forward_wait_c", await_, fwd_c["ss"], fwd_c["rs"], [fwd_c["wc"]], o)
        return (_merge_edges(wc, C_EDGE, C_MIXED, "merge_edges_c"), fwd_c["wo_g"].reshape(2 * GW, d_model),
                _gathered_to_full(fwd_c["cw_g"]))

    scat = {}

    def on_grad_c(g_c, g_wout, do):
        go4 = g_wout.reshape(4, GW // 2, d_model)
        land = lax.empty((3, ALIGNED_W, d_model), BF16)
        land_o = lax.empty((3, GW // 2, d_model), BF16)
        ss, rs, bufs, tok = _scatter_start("c", g_c, land, [(go4, land_o)])
        scat["c"] = (ss, rs, bufs)
        return _tie(do, tok, "after_scatter_start_c")

    def on_grad_g(g_g, dproj_g):
        ss, rs, bufs, tok = _scatter_start("g", _pair_reduce("g", g_g), scat["c"][2][1], [], halved=True)
        scat["g"] = (ss, rs, bufs)
        return _tie(dproj_g, tok, "after_scatter_start_g")

    gx, sm, _ = _local_step(x0, loss_target[0], h, w_g, cqw, late, norm_in_w, ad, gdn_norm_w, conv_b,
                            final_norm_w.reshape(1, -1), on_grad_c, on_grad_g, on_q)

    ss, rs, bufs = scat["c"]
    g_c, land, go4, land_o = _scatter_wait("c", ss, rs, [bufs[0], scat["g"][2][1], bufs[2], bufs[3]], gx)
    part_out = _sum_rows(go4, land_o, 128)
    ad_g = jnp.concatenate([sm["al"][:, A_LANE:], sm["dt"][:, A_LANE:]], axis=1)
    pack = jnp.concatenate([_row(sm["nin"]), _row(sm["cb"]), _row(sm["fn"]), _row(ad_g), _row(sm["gn"]),
                            jnp.concatenate(sm["cq"], axis=1).reshape(12, 1024), sm["cw"], _row(sm["loss"])], axis=0)
    pack = jnp.pad(pack, ((0, PACK_ROWS - pack.shape[0]), (0, 0)))
    issue, await_a, nsem = _exchange_parts(1, True)
    ss_a, rs_a, bufs_a, tok_a = _split_start(
        "exchange_start_small", issue,
        [part_out, lax.empty(part_out.shape, F32), pack, lax.empty((8,) + pack.shape, F32)], nsem)
    ss, rs, bufs = scat["g"]
    g_g, land = _scatter_wait("g", ss, rs, [bufs[0], land], [gx, tok_a], halved=True)
    part_in = _sum_shard(g_g, g_c, land)
    issue, await_b, nsem = _exchange_parts(1, False)
    ss_b, rs_b, bufs_b, tok_b = _split_start("exchange_start_w_in", issue,
                                             [part_in, lax.empty(part_in.shape, F32)], nsem)
    part_out, sib_out, pack, packs = _split_wait("exchange_wait_small", await_a, ss_a, rs_a, bufs_a, tok_b)
    tot = _sum_packs(pack, packs)
    g_wo, d_wo, m_wo, v_wo = _adamw(w_out[0], m_w_out[0], v_w_out[0], part_out, sib_out, 128, "adamw_w_out")
    small = _adamw_small(w_s, m_s, v_s, tot)
    part_in, sib_in = _split_wait("exchange_wait_w_in", await_b, ss_b, rs_b, bufs_b, [small[0][0], d_wo])
    g_wi, d_wi, m_wi, v_wi = [jnp.transpose(a, (1, 2, 0))[0] for a in _adamw_shard(*adam_in, part_in, sib_in)]

    def unpack(leaves, big_in, big_out):
        nin, cq, al, dt, gn, cw, cb, fn = leaves
        return (nin, big_in[None], cq[None], al, dt, gn, cw[None], cb, big_out[None], fn.reshape(-1))

    loss = tot[R_LOSS, 0]
    return (loss, gx[None], *unpack(small[0], g_wi, g_wo), *unpack(small[1], d_wi, d_wo),
            *unpack(small[2], m_wi, m_wo), *unpack(small[3], v_wi, v_wo))
```

```python
import jax
import jax.numpy as jnp
from jax import lax
from jax.experimental import pallas as pl
from jax.experimental.pallas import tpu as pltpu

F32 = jnp.float32
BF16 = jnp.bfloat16
MESH = pl.DeviceIdType.MESH
ANY = pl.BlockSpec(memory_space=pl.ANY)

HEADS = 8
DH = 128
CH = 64
GW = HEADS * DH
EPS = 1e-6
VMEM_V7X = 64 * 1024 * 1024

QB, KB, VB, ZB, BAB = 0, 8, 16, 24, 32
A_LANE = 120
NG, NC = 33, 32
GW_COLS, CW_COLS = NG * DH, NC * DH

SHARD_W = 2052
ALIGNED_BLOCKS = 17
ALIGNED_W = ALIGNED_BLOCKS * DH
SHIFTS = (0, 4, ALIGNED_W - 8, ALIGNED_W - 4)
G_EDGE, C_EDGE = 34, 32
G_SPARE, C_SPARE = 33, 34
WG_BLOCKS, WC_BLOCKS = 38, 36
G_MIXED, C_MIXED = (2, BAB), (4 * 7 + 1,)


def _shard_blocks(chip, edges):
    g, c = "g", "c"
    if chip == 0:
        out = [(g, 3 * b) for b in range(8)] + [(g, 3 * b + 1) for b in range(8)] + [(g, G_EDGE, G_MIXED[0])]
    elif chip == 1:
        out = [(g, G_EDGE + 1, G_MIXED[0])] + [(g, 3 * b + 2) for b in range(1, 8)]
        out += [(g, ZB + b) for b in range(8)] + [(g, G_EDGE + 2, G_MIXED[1])]
    elif chip == 2:
        out = [(c, 4 * b) for b in range(8)] + [(c, 4 * b + 1) for b in range(7)]
        out += [(c, C_EDGE, C_MIXED[0]), (g, G_EDGE + 3, G_MIXED[1])]
    else:
        out = [(c, 4 * b + 2) for b in range(8)] + [(c, 4 * b + 3) for b in range(8)] + [(c, C_EDGE + 1, C_MIXED[0])]
    return [(o[0], o[1] if (edges or len(o) == 2) else o[2]) for o in out]


def _by_chip(chip, vals):
    if all(v == vals[0] for v in vals):
        return vals[0]
    r = vals[3]
    for kk in (2, 1, 0):
        r = jnp.where(chip == kk, vals[kk], r)
    return r

ADAM_LR, ADAM_B1, ADAM_B2, ADAM_EPS, ADAM_WD, ADAM_STEP = 0.001, 0.9, 0.999, 1e-08, 0.01, 10

R_NIN, R_CB, R_FN, R_AD, R_GN, R_CQ, R_CW, R_LOSS, PACK_ROWS = 0, 1, 2, 3, 4, 5, 17, 20, 24

NN = ((1,), (0,))
NT = ((1,), (1,))
TN = ((0,), (0,))


def _dot(a, b, dims=NN, mode="lo"):
    dn = (dims, ((), ()))
    if mode == "hi":
        return lax.dot_general(a, b, dn, precision=lax.Precision.HIGHEST, preferred_element_type=F32)
    ah, bh = a.astype(BF16), b.astype(BF16)
    out = lax.dot_general(ah, bh, dn, preferred_element_type=F32)
    if mode == "x3":
        al = (a - ah.astype(F32)).astype(BF16)
        bl = (b - bh.astype(F32)).astype(BF16)
        out = out + lax.dot_general(ah, bl, dn, preferred_element_type=F32)
        out = out + lax.dot_general(al, bh, dn, preferred_element_type=F32)
    return out


P_GRAM, P_INV, P_SOL, P_SCAN, P_SCANB, P_BWD = "lo", "lo", "lo", "lo", "lo", "lo"
P_CUM = "x3"


def _params(sem=None, vmem=None):
    kw = {}
    if sem is not None:
        kw["dimension_semantics"] = sem
    if vmem is not None:
        kw["vmem_limit_bytes"] = int(min(max(vmem, 32 * 2**20), VMEM_V7X - 8 * 2**20))
    return pltpu.CompilerParams(**kw)


def _in_hbm(*arrays):
    return [pltpu.with_memory_space_constraint(a, pltpu.HBM) for a in arrays]


def _sigmoid(x):
    return 1.0 / (1.0 + jnp.exp(-x))


def _dsilu(x, s):
    return s * (1.0 + x * (1.0 - s))


def _rows(shape):
    return lax.broadcasted_iota(jnp.int32, shape, 0)


def _shift_down(x, s):
    if s == 0:
        return x
    return jnp.where(_rows(x.shape) >= s, pltpu.roll(x, s, 0), 0.0)


def _shift_up(x, s):
    if s == 0:
        return x
    n = x.shape[0]
    return jnp.where(_rows(x.shape) < n - s, pltpu.roll(x, n - s, 0), 0.0)


def _matmul(a, b, dims, out_dtype, tm, tn, tk, name, add=None, n=None, b_outer=False):
    if dims == NN:
        (m, k), n = a.shape, b.shape[1]
    elif dims == NT:
        (m, k), n = a.shape, (n or b.shape[0])
    else:
        (k, m), n = a.shape, b.shape[1]
    tm, tn, tk = min(tm, m), min(tn, n), min(tk, k)
    assert m % tm == 0 and n % tn == 0 and k % tk == 0, (name, m, n, k, tm, tn, tk)
    nk = k // tk

    def body(*refs):
        if add is None:
            a_ref, b_ref, o_ref = refs[:3]
            add_ref = None
        else:
            a_ref, b_ref, add_ref, o_ref = refs[:4]
        part = _dot(a_ref[...], b_ref[...], dims)
        if nk == 1:
            if add_ref is not None:
                part = part + add_ref[...]
            o_ref[...] = part.astype(out_dtype)
            return
        acc = refs[-1]
        kk = pl.program_id(2)

        @pl.when(kk == 0)
        def _():
            acc[...] = part

        @pl.when(kk > 0)
        def _():
            acc[...] += part

        @pl.when(kk == nk - 1)
        def _():
            r = acc[...]
            if add_ref is not None:
                r = r + add_ref[...]
            o_ref[...] = r.astype(out_dtype)

    ij = (lambda g0, g1: (g1, g0)) if b_outer else (lambda g0, g1: (g0, g1))

    def spec(shape, pick):
        return pl.BlockSpec(shape, lambda g0, g1, kk: pick(*ij(g0, g1), kk))

    a_spec = spec((tk, tm), lambda i, j, kk: (kk, i)) if dims == TN else spec((tm, tk), lambda i, j, kk: (i, kk))
    b_spec = spec((tn, tk), lambda i, j, kk: (j, kk)) if dims == NT else spec((tk, tn), lambda i, j, kk: (kk, j))
    o_spec = spec((tm, tn), lambda i, j, kk: (i, j))
    in_specs = [a_spec, b_spec]
    args = [a, b]
    if add is not None:
        in_specs.append(o_spec)
        args.append(add)
    osz = jnp.dtype(out_dtype).itemsize
    est = 2 * (tm * tk * a.dtype.itemsize + tk * tn * b.dtype.itemsize + tm * tn * osz)
    est += 3 * tm * tn * 4 + (2 * tm * tn * 4 if add is not None else 0)
    return pl.pallas_call(
        body, name=name, grid=(n // tn, m // tm, nk) if b_outer else (m // tm, n // tn, nk),
        in_specs=in_specs, out_specs=o_spec,
        out_shape=jax.ShapeDtypeStruct((m, n), out_dtype),
        scratch_shapes=[pltpu.VMEM((tm, tn), F32)] if nk > 1 else [],
        compiler_params=_params(("parallel", "parallel", "arbitrary"), est + 8 * 2**20),
    )(*args)


def _cast_bf16(a, rows, name, after):
    r, c = a.shape
    rows = min(rows, r)

    def body(a_ref, t_ref, o_ref):
        del t_ref
        o_ref[...] = a_ref[...].astype(BF16)

    return pl.pallas_call(
        body, name=name, grid=(r // rows,),
        in_specs=[pl.BlockSpec((rows, c), lambda i: (i, 0)), ANY],
        out_specs=pl.BlockSpec((rows, c), lambda i: (i, 0)),
        out_shape=jax.ShapeDtypeStruct((r, c), BF16),
        compiler_params=_params(("parallel",)),
    )(a, after)


def _align_shard(wt):
    r, _, d = wt.shape
    cols = min(256, d)

    def body(w_ref, o_ref, pad_ref):
        chip = 2 * lax.axis_index("x") + lax.axis_index("y")
        pad_ref[...] = jnp.zeros_like(pad_ref)
        pad_ref[0:r, :] = w_ref[:, 0, :]
        o_ref[...] = pltpu.roll(pad_ref[...], _by_chip(chip, SHIFTS), 0).astype(BF16)

    return pl.pallas_call(
        body, name="align_shard", grid=(d // cols,),
        in_specs=[pl.BlockSpec((r, 1, cols), lambda i: (0, 0, i))],
        out_specs=pl.BlockSpec((ALIGNED_W, cols), lambda i: (0, i)),
        out_shape=jax.ShapeDtypeStruct((ALIGNED_W, d), BF16),
        scratch_shapes=[pltpu.VMEM((ALIGNED_W, cols), F32)],
        compiler_params=_params(("parallel",)),
    )(wt)


def _rms_in(x, w):
    n, d = x.shape
    tr = min(256, n)

    def body(x_ref, w_ref, h_ref):
        xv = x_ref[...]
        r = lax.rsqrt(jnp.mean(xv * xv, axis=-1, keepdims=True) + EPS)
        h_ref[...] = (xv * r * w_ref[...]).astype(BF16)

    return pl.pallas_call(
        body, name="rms_in", grid=(n // tr,),
        in_specs=[pl.BlockSpec((tr, d), lambda i: (i, 0)), pl.BlockSpec((1, d), lambda i: (0, 0))],
        out_specs=pl.BlockSpec((tr, d), lambda i: (i, 0)),
        out_shape=jax.ShapeDtypeStruct((n, d), BF16),
        compiler_params=_params(("parallel",)),
    )(x, w)


def _conv_silu(p, w_ref, taps):
    c = None
    for j in range(taps):
        t = _shift_down(p, taps - 1 - j) * w_ref[j:j + 1, :]
        c = t if c is None else c + t
    return c


def _prep_qkv(proj, cw):
    n = proj.shape[0]

    def body(p3, wq, wk, wv, q_ref, k_ref, v_ref):
        for kind, (w_ref, o_ref) in enumerate(((wq, q_ref), (wk, k_ref), (wv, v_ref))):
            c = _conv_silu(p3[:, kind * DH:(kind + 1) * DH], w_ref, 4)
            a = c * _sigmoid(c)
            if kind < 2:
                r = lax.rsqrt(jnp.sum(a * a, axis=-1, keepdims=True) + EPS)
                a = a * (r * (DH ** -0.5 if kind == 0 else 1.0))
            o_ref[...] = a

    col = pl.BlockSpec((n, DH), lambda h: (0, h))
    wcol = lambda base: pl.BlockSpec((4, DH), lambda h: (0, base + h))
    out = jax.ShapeDtypeStruct((n, GW), F32)
    return pl.pallas_call(
        body, name="prep_qkv", grid=(HEADS,),
        in_specs=[pl.BlockSpec((n, 3 * DH), lambda h: (0, h)), wcol(QB), wcol(KB), wcol(VB)],
        out_specs=[col] * 3, out_shape=[out] * 3,
        compiler_params=_params(("parallel",), 40 * 2**20),
    )(proj, cw, cw, cw)


def _prep_qkv_bwd(proj, cw, dq, dk, dv, dproj):
    n = proj.shape[0]

    def body(p3, wq, wk, wv, dq_ref, dk_ref, dv_ref, _, o3, gq, gk, gv):
        for kind, (w_ref, d_ref, g_ref) in enumerate(((wq, dq_ref, gq), (wk, dk_ref, gk), (wv, dv_ref, gv))):
            p = p3[:, kind * DH:(kind + 1) * DH]
            shifted = [_shift_down(p, 3 - j) for j in range(4)]
            c = shifted[0] * w_ref[0:1, :]
            for j in range(1, 4):
                c = c + shifted[j] * w_ref[j:j + 1, :]
            s = _sigmoid(c)
            a = c * s
            d = d_ref[...]
            if kind < 2:
                r = lax.rsqrt(jnp.sum(a * a, axis=-1, keepdims=True) + EPS)
                sc = DH ** -0.5 if kind == 0 else 1.0
                d = (sc * r) * (d - a * ((r * r) * jnp.sum(d * a, axis=-1, keepdims=True)))
            dc = d * _dsilu(c, s)
            dp = None
            for j in range(4):
                g_ref[j:j + 1, :] = jnp.sum(dc * shifted[j], axis=0, keepdims=True)
                t = _shift_up(dc, 3 - j) * w_ref[j:j + 1, :]
                dp = t if dp is None else dp + t
            o3[:, kind * DH:(kind + 1) * DH] = dp.astype(BF16)

    col = pl.BlockSpec((n, DH), lambda h: (0, h))
    wcol = lambda base: pl.BlockSpec((4, DH), lambda h: (0, base + h))
    p3spec = pl.BlockSpec((n, 3 * DH), lambda h: (0, h))
    return pl.pallas_call(
        body, name="prep_qkv_bwd", grid=(HEADS,),
        in_specs=[p3spec, wcol(QB), wcol(KB), wcol(VB), col, col, col, ANY],
        out_specs=[p3spec] + [wcol(0)] * 3,
        out_shape=[jax.ShapeDtypeStruct(dproj.shape, BF16)] + [jax.ShapeDtypeStruct((4, GW), F32)] * 3,
        input_output_aliases={7: 0},
        compiler_params=_params(("parallel",), 48 * 2**20),
    )(proj, cw, cw, cw, dq, dk, dv, dproj)


CPB = 8
SCAN_CPS = 4


def _tri(lower, rows):
    i = lax.broadcasted_iota(jnp.int32, (rows, rows), 0)
    j = lax.broadcasted_iota(jnp.int32, (rows, rows), 1)
    return jnp.where((i // CH == j // CH) & ((i >= j) if lower else (j >= i)), 1.0, 0.0)


def _lane(shape):
    return lax.broadcasted_iota(jnp.int32, shape, 1)


def _prep_bg(proj, ad):
    n = proj.shape[0]
    nch = n // CH
    cpb = CPB if nch % CPB == 0 else 1
    rows = cpb * CH

    def body(p_ref, ad_ref, bg_ref, bgt_ref):
        p = p_ref[...]
        lane = _lane(p.shape)
        beta = _sigmoid(p)
        xa = p + ad_ref[1:2, :]
        sp = jnp.maximum(xa, 0.0) + jnp.log(1.0 + jnp.exp(-jnp.abs(xa)))
        g = pltpu.roll(-jnp.exp(ad_ref[0:1, :]) * sp, DH - A_LANE + HEADS, 1)
        gc = _dot(_tri(True, rows), g, NN, P_CUM)
        bg = jnp.where(lane < HEADS, beta, jnp.where(lane < 2 * HEADS, gc, 0.0))
        bg_ref[...] = bg
        for ci in range(cpb):
            bgt_ref[ci] = bg[ci * CH:(ci + 1) * CH, :].T

    return pl.pallas_call(
        body, name="prep_bg", grid=(nch // cpb,),
        in_specs=[pl.BlockSpec((rows, DH), lambda i: (i, BAB)), pl.BlockSpec((2, DH), lambda i: (0, 0))],
        out_specs=[pl.BlockSpec((rows, DH), lambda i: (i, 0)), pl.BlockSpec((cpb, DH, CH), lambda i: (i, 0, 0))],
        out_shape=[jax.ShapeDtypeStruct((n, DH), F32), jax.ShapeDtypeStruct((nch, DH, CH), F32)],
        compiler_params=_params(("parallel",)),
    )(*_in_hbm(proj, ad))


def _prep_bg_bwd(proj, ad, dbg, dproj):
    n = proj.shape[0]
    nch = n // CH
    cpb = CPB if nch % CPB == 0 else 1
    rows = cpb * CH

    def body(p_ref, ad_ref, d_ref, _, o_ref, ga_ref, gd_ref):
        p = p_ref[...]
        d = d_ref[...]
        lane = _lane(p.shape)
        beta = _sigmoid(p)
        xa = p + ad_ref[1:2, :]
        sp = jnp.maximum(xa, 0.0) + jnp.log(1.0 + jnp.exp(-jnp.abs(xa)))
        na = -jnp.exp(ad_ref[0:1, :])
        dg = pltpu.roll(_dot(_tri(False, rows), d, NN, P_CUM), A_LANE - HEADS, 1)
        da = dg * na * _sigmoid(xa)
        is_g = lane >= A_LANE
        o_ref[...] = jnp.where(lane < HEADS, d * beta * (1.0 - beta), jnp.where(is_g, da, 0.0)).astype(BF16)
        ga = jnp.sum(jnp.where(is_g, dg * na * sp, 0.0), axis=0, keepdims=True)
        gd = jnp.sum(jnp.where(is_g, da, 0.0), axis=0, keepdims=True)

        @pl.when(pl.program_id(0) == 0)
        def _():
            ga_ref[...] = jnp.zeros_like(ga_ref)
            gd_ref[...] = jnp.zeros_like(gd_ref)

        ga_ref[...] += ga
        gd_ref[...] += gd

    one = pl.BlockSpec((1, DH), lambda i: (0, 0))
    return pl.pallas_call(
        body, name="prep_bg_bwd", grid=(nch // cpb,),
        in_specs=[pl.BlockSpec((rows, DH), lambda i: (i, BAB)), pl.BlockSpec((2, DH), lambda i: (0, 0)),
                  pl.BlockSpec((rows, DH), lambda i: (i, 0)), ANY],
        out_specs=[pl.BlockSpec((rows, DH), lambda i: (i, BAB)), one, one],
        out_shape=[jax.ShapeDtypeStruct(dproj.shape, BF16), jax.ShapeDtypeStruct((1, DH), F32),
                   jax.ShapeDtypeStruct((1, DH), F32)],
        input_output_aliases={3: 0},
        compiler_params=_params(("arbitrary",)),
    )(proj, ad, dbg, dproj)


def _gdn_out(o, proj, wg):
    n = o.shape[0]

    def body(o_ref, z_ref, w_ref, y_ref):
        ov, z = o_ref[...], z_ref[...]
        r = lax.rsqrt(jnp.mean(ov * ov, axis=-1, keepdims=True) + EPS)
        y_ref[...] = (ov * r * w_ref[...] * (z * _sigmoid(z))).astype(BF16)

    return pl.pallas_call(
        body, name="gdn_out", grid=(HEADS,),
        in_specs=[pl.BlockSpec((n, DH), lambda h: (0, h)), pl.BlockSpec((n, DH), lambda h: (0, ZB + h)),
                  pl.BlockSpec((1, DH), lambda h: (0, 0))],
        out_specs=pl.BlockSpec((n, DH), lambda h: (0, h)),
        out_shape=jax.ShapeDtypeStruct((n, 2 * GW), BF16),
        compiler_params=_params(("parallel",)),
    )(o, proj, wg)


def _gdn_out_bwd(o, proj, wg, dout_b, w_out):
    n = o.shape[0]
    d_model = dout_b.shape[1]

    def body(o_ref, z_ref, w_ref, g_ref, wo_ref, do_ref, dz_ref, gw_ref):
        ov, z, w = o_ref[...], z_ref[...], w_ref[...]
        d = _dot(g_ref[...], wo_ref[...], NT)
        r = lax.rsqrt(jnp.mean(ov * ov, axis=-1, keepdims=True) + EPS)
        nrm = ov * r
        s = _sigmoid(z)
        dz_ref[...] = (d * (nrm * w) * _dsilu(z, s)).astype(BF16)
        dn_w = d * (z * s)
        gw = jnp.sum(dn_w * nrm, axis=0, keepdims=True)
        dn = dn_w * w
        do_ref[...] = (r * (dn - nrm * jnp.mean(dn * nrm, axis=-1, keepdims=True))).astype(BF16)

        @pl.when(pl.program_id(0) == 0)
        def _():
            gw_ref[...] = jnp.zeros_like(gw_ref)

        gw_ref[...] += gw

    return pl.pallas_call(
        body, name="gdn_out_bwd", grid=(HEADS,),
        in_specs=[pl.BlockSpec((n, DH), lambda h: (0, h)), pl.BlockSpec((n, DH), lambda h: (0, ZB + h)),
                  pl.BlockSpec((1, DH), lambda h: (0, 0)), pl.BlockSpec((n, d_model), lambda h: (0, 0)),
                  pl.BlockSpec((DH, d_model), lambda h: (h, 0))],
        out_specs=[pl.BlockSpec((n, DH), lambda h: (0, h)), pl.BlockSpec((n, DH), lambda h: (0, ZB + h)),
                   pl.BlockSpec((1, DH), lambda h: (0, 0))],
        out_shape=[jax.ShapeDtypeStruct((n, GW), BF16), jax.ShapeDtypeStruct((n, GW_COLS), BF16),
                   jax.ShapeDtypeStruct((1, DH), F32)],
        compiler_params=_params(("arbitrary",), 40 * 2**20),
    )(o, proj, wg, dout_b, w_out)


def _conv_branch(proj, w3, b, mix):
    n = proj.shape[0]

    def body(p4, w_ref, b_ref, _, y_ref):
        u = p4[:, DH:2 * DH] * p4[:, 2 * DH:3 * DH]
        cc = _conv_silu(u, w_ref, 3) + b_ref[...]
        z = p4[:, 3 * DH:4 * DH]
        y_ref[...] = (p4[:, 0:DH] * cc * (z * _sigmoid(z))).astype(BF16)

    return pl.pallas_call(
        body, name="conv_branch", grid=(HEADS,),
        in_specs=[pl.BlockSpec((n, 4 * DH), lambda h: (0, h)), pl.BlockSpec((3, DH), lambda h: (0, h)),
                  pl.BlockSpec((1, DH), lambda h: (0, h)), ANY],
        out_specs=pl.BlockSpec((n, DH), lambda h: (0, HEADS + h)),
        out_shape=jax.ShapeDtypeStruct(mix.shape, BF16),
        input_output_aliases={3: 0},
        compiler_params=_params(("parallel",), 40 * 2**20),
    )(*_in_hbm(proj, w3, b, mix))


def _conv_branch_bwd(proj, w3, b, dout_b, w_out):
    n = proj.shape[0]
    d_model = dout_b.shape[1]

    def body(p4, w_ref, b_ref, g_ref, wo_ref, o4, gw_ref, gbias_ref):
        gb, gcv, hc, z = p4[:, 0:DH], p4[:, DH:2 * DH], p4[:, 2 * DH:3 * DH], p4[:, 3 * DH:4 * DH]
        d = _dot(g_ref[...], wo_ref[...], NT)
        dgb, dgc, dhc, dzc = (o4.at[:, kk * DH:(kk + 1) * DH] for kk in range(4))
        u = gcv * hc
        cc = _conv_silu(u, w_ref, 3) + b_ref[...]
        s = _sigmoid(z)
        dzc[...] = (d * (gb * cc) * _dsilu(z, s)).astype(BF16)
        dp = d * (z * s)
        dgb[...] = (dp * cc).astype(BF16)
        dcc = dp * gb
        gbias_ref[...] = jnp.sum(dcc, axis=0, keepdims=True)
        du = None
        for j in range(3):
            gw_ref[j:j + 1, :] = jnp.sum(dcc * _shift_down(u, 2 - j), axis=0, keepdims=True)
            t = _shift_up(dcc, 2 - j) * w_ref[j:j + 1, :]
            du = t if du is None else du + t
        dgc[...] = (du * hc).astype(BF16)
        dhc[...] = (du * gcv).astype(BF16)

    p4spec = pl.BlockSpec((n, 4 * DH), lambda h: (0, h))
    return pl.pallas_call(
        body, name="conv_branch_bwd", grid=(HEADS,),
        in_specs=[p4spec, pl.BlockSpec((3, DH), lambda h: (0, h)), pl.BlockSpec((1, DH), lambda h: (0, h)),
                  pl.BlockSpec((n, d_model), lambda h: (0, 0)), pl.BlockSpec((DH, d_model), lambda h: (HEADS + h, 0))],
        out_specs=[p4spec, pl.BlockSpec((3, DH), lambda h: (0, h)), pl.BlockSpec((1, DH), lambda h: (0, h))],
        out_shape=[jax.ShapeDtypeStruct((n, CW_COLS), BF16), jax.ShapeDtypeStruct((3, GW), F32),
                   jax.ShapeDtypeStruct((1, GW), F32)],
        compiler_params=_params(("parallel",), 52 * 2**20),
    )(proj, w3, b, dout_b, w_out)


def _out_loss(mix, w_out, x, tgt, wf):
    n, d = x.shape
    kdim = mix.shape[1]
    tr = min(256, n)

    def body(m_ref, wo_ref, x_ref, t_ref, w_ref, do_ref, dob_ref, gw_ref, loss_ref):
        ov = _dot(m_ref[...], wo_ref[...], NN) + x_ref[...]
        w = w_ref[...]
        r = lax.rsqrt(jnp.mean(ov * ov, axis=-1, keepdims=True) + EPS)
        nrm = ov * r
        e = nrm * w - t_ref[...]
        dy = e * (1.0 / d)
        dn = dy * w
        dout = r * (dn - nrm * jnp.mean(dn * nrm, axis=-1, keepdims=True))
        do_ref[...] = dout
        dob_ref[...] = dout.astype(BF16)

        @pl.when(pl.program_id(0) == 0)
        def _():
            gw_ref[...] = jnp.zeros_like(gw_ref)
            loss_ref[...] = jnp.zeros_like(loss_ref)

        gw_ref[...] += jnp.sum(dy * nrm, axis=0, keepdims=True)
        loss_ref[...] += (0.5 / d) * jnp.sum(jnp.sum(e * e, axis=-1, keepdims=True), axis=0, keepdims=True)

    row = pl.BlockSpec((tr, d), lambda i: (i, 0))
    return pl.pallas_call(
        body, name="out_loss", grid=(n // tr,),
        in_specs=[pl.BlockSpec((tr, kdim), lambda i: (i, 0)), pl.BlockSpec((kdim, d), lambda i: (0, 0)), row, row,
                  pl.BlockSpec((1, d), lambda i: (0, 0))],
        out_specs=[row, row, pl.BlockSpec((1, d), lambda i: (0, 0)), pl.BlockSpec((1, 1), lambda i: (0, 0))],
        out_shape=[jax.ShapeDtypeStruct((n, d), F32), jax.ShapeDtypeStruct((n, d), BF16),
                   jax.ShapeDtypeStruct((1, d), F32), jax.ShapeDtypeStruct((1, 1), F32)],
        compiler_params=_params(("arbitrary",), 40 * 2**20),
    )(mix, w_out, x, tgt, wf)


def _dh_rms_bwd(dproj, w_t, dh0, x, w, dout, tk):
    n, d = x.shape
    kdim = dproj.shape[1]
    tm = min(1024, n)
    tk = min(tk, kdim)
    nk = kdim // tk

    def body(a_ref, b_ref, dh0_ref, x_ref, w_ref, do_ref, dx_ref, gw_ref, acc):
        i, kk = pl.program_id(0), pl.program_id(1)
        part = _dot(a_ref[...], b_ref[...], NN)

        @pl.when(kk == 0)
        def _():
            acc[...] = part + dh0_ref[...]

        @pl.when(kk > 0)
        def _():
            acc[...] += part

        @pl.when((i == 0) & (kk == 0))
        def _():
            gw_ref[...] = jnp.zeros_like(gw_ref)

        @pl.when(kk == nk - 1)
        def _():
            xv, dhv = x_ref[...], acc[...]
            r = lax.rsqrt(jnp.mean(xv * xv, axis=-1, keepdims=True) + EPS)
            xn = xv * r
            dxn = dhv * w_ref[...]
            dx_ref[...] = r * (dxn - xn * jnp.mean(dxn * xn, axis=-1, keepdims=True)) + do_ref[...]
            gw_ref[...] += jnp.sum(dhv * xn, axis=0, keepdims=True)

    row = pl.BlockSpec((tm, d), lambda i, kk: (i, 0))
    one = pl.BlockSpec((1, d), lambda i, kk: (0, 0))
    return pl.pallas_call(
        body, name="dh_rms_bwd", grid=(n // tm, nk),
        in_specs=[pl.BlockSpec((tm, tk), lambda i, kk: (i, kk)), pl.BlockSpec((tk, d), lambda i, kk: (kk, 0)),
                  row, row, one, row],
        out_specs=[row, one],
        out_shape=[jax.ShapeDtypeStruct((n, d), F32), jax.ShapeDtypeStruct((1, d), F32)],
        scratch_shapes=[pltpu.VMEM((tm, d), F32)],
        compiler_params=_params(("arbitrary", "arbitrary"), 56 * 2**20),
    )(dproj, w_t, dh0, x, w, dout)


def _ij():
    i = lax.broadcasted_iota(jnp.int32, (CH, CH), 0)
    j = lax.broadcasted_iota(jnp.int32, (CH, CH), 1)
    return i, j


def _unit_lower_inverse(mats):
    i, j = _ij()
    eye = jnp.where(i == j, 1.0, 0.0)
    same16 = (i // 16) == (j // 16)
    same32 = (i // 32) == (j // 32)
    mm = lambda xs, ys: [_dot(x, y, NN, P_INV) for x, y in zip(xs, ys)]
    n1 = [jnp.where(same16, -a, 0.0) for a in mats]
    n2 = mm(n1, n1)
    n4 = mm(n2, n2)
    n8 = mm(n4, n4)
    t = [eye + x1 + x2 + x3 for x1, x2, x3 in zip(n1, n2, mm(n1, n2))]
    t = [x + y for x, y in zip(t, mm(t, n4))]
    t = [x + y for x, y in zip(t, mm(t, n8))]
    a1 = [jnp.where(same32 & jnp.logical_not(same16), a, 0.0) for a in mats]
    t = [x - y for x, y in zip(t, mm(t, mm(a1, t)))]
    a2 = [jnp.where(same32, 0.0, a) for a in mats]
    t = [x - y for x, y in zip(t, mm(t, mm(a2, t)))]
    return t


def _head_vectors(bg, bgt, h):
    bcol = bg[:, h:h + 1]
    gcol = bg[:, HEADS + h:HEADS + h + 1]
    grow = bgt[HEADS + h:HEADS + h + 1, :]
    return bcol, gcol, grow


def _decay(gcol, grow):
    i, j = _ij()
    return jnp.where(i >= j, jnp.exp(jnp.where(i >= j, gcol - grow, 0.0)), 0.0)


def _gdn_intra(q, k, v, bg, bgt):
    n = q.shape[0]
    nch = n // CH
    cps = 4 if nch % 4 == 0 else 1

    def body(q_ref, k_ref, v_ref, bg_ref, bgt_ref, u_ref, w_ref, p_ref, t_ref):
        i, j = _ij()
        items = [(ci, h) for ci in range(cps) for h in range(HEADS)]
        at = lambda ref, ci, h: ref.at[ci * CH:(ci + 1) * CH, h * DH:(h + 1) * DH]
        bgs = [bg_ref[ci * CH:(ci + 1) * CH, :] for ci in range(cps)]
        ks = [at(k_ref, ci, h)[...] for ci, h in items]
        vecs = [_head_vectors(bgs[ci], bgt_ref[ci], h) for ci, h in items]
        decs = [_decay(gcol, grow) for _, gcol, grow in vecs]
        kks = [_dot(kh, kh, NT, P_GRAM) for kh in ks]
        qks = [_dot(at(q_ref, ci, h)[...], kh, NT, P_GRAM) for (ci, h), kh in zip(items, ks)]
        ts = _unit_lower_inverse([jnp.where(i > j, bcol * kk * dec, 0.0)
                                  for (bcol, _, _), kk, dec in zip(vecs, kks, decs)])
        us = [_dot(t, at(v_ref, ci, h)[...] * bcol, NN, P_SOL) for t, (ci, h), (bcol, _, _) in zip(ts, items, vecs)]
        ws = [_dot(t, kh * (bcol * jnp.exp(gcol)), NN, P_SOL) for t, kh, (bcol, gcol, _) in zip(ts, ks, vecs)]
        for n_, (ci, h) in enumerate(items):
            p_ref[ci, h] = qks[n_] * decs[n_]
            t_ref[ci, h] = ts[n_].astype(BF16)
            at(u_ref, ci, h)[...] = us[n_]
            at(w_ref, ci, h)[...] = ws[n_].astype(BF16)

    row = pl.BlockSpec((cps * CH, GW), lambda c: (c, 0))
    sq = pl.BlockSpec((cps, HEADS, CH, CH), lambda c: (c, 0, 0, 0))
    big = jax.ShapeDtypeStruct((n, GW), F32)
    sqs = jax.ShapeDtypeStruct((nch, HEADS, CH, CH), F32)
    return pl.pallas_call(
        body, name="gdn_intra", grid=(nch // cps,),
        in_specs=[row, row, row, pl.BlockSpec((cps * CH, DH), lambda c: (c, 0)),
                  pl.BlockSpec((cps, DH, CH), lambda c: (c, 0, 0))],
        out_specs=[row, row, sq, sq],
        out_shape=[big, jax.ShapeDtypeStruct((n, GW), BF16), sqs, jax.ShapeDtypeStruct(sqs.shape, BF16)],
        compiler_params=_params(("parallel",)),
    )(q, k, v, bg, bgt)


def _gdn_scan(q, k, bg, u, w, p):
    n = q.shape[0]
    nch = n // CH
    cps = SCAN_CPS if nch % SCAN_CPS == 0 else 1

    def body(q_ref, k_ref, bg_ref, u_ref, w_ref, p_ref, o_ref, vn_ref, s_out, s_scr):
        @pl.when(pl.program_id(0) == 0)
        def _():
            s_scr[...] = jnp.zeros_like(s_scr)

        hs = range(HEADS)
        sls = [slice(h * DH, (h + 1) * DH) for h in hs]
        ss = [s_scr[h] for h in hs]
        for ci in range(cps):
            rs = slice(ci * CH, (ci + 1) * CH)
            bg = bg_ref[rs, :]
            gcols = [bg[:, HEADS + h:HEADS + h + 1] for h in hs]
            glasts = [g[CH - 1:CH, :] for g in gcols]
            wss = [_dot(w_ref[rs, sl], s, NN, P_SCAN) for sl, s in zip(sls, ss)]
            oqs = [_dot(q_ref[rs, sl] * jnp.exp(g), s, NN, P_SCAN) for sl, s, g in zip(sls, ss, gcols)]
            vns = [u_ref[rs, sl] - x for sl, x in zip(sls, wss)]
            ops = [_dot(p_ref[ci, h], vn, NN, P_SCAN) for h, vn in zip(hs, vns)]
            sns = [_dot(k_ref[rs, sl] * jnp.exp(gl - g), vn, TN, P_SCAN)
                   for sl, gl, g, vn in zip(sls, glasts, gcols, vns)]
            for h, sl in enumerate(sls):
                s_out[ci, :, sl] = ss[h].astype(BF16)
                vn_ref[rs, sl] = vns[h].astype(BF16)
                o_ref[rs, sl] = oqs[h] + ops[h]
            ss = [s * jnp.exp(gl) + sn for s, gl, sn in zip(ss, glasts, sns)]
        for h in hs:
            s_scr[h] = ss[h]

    row = pl.BlockSpec((cps * CH, GW), lambda c: (c, 0))
    big = jax.ShapeDtypeStruct((n, GW), F32)
    return pl.pallas_call(
        body, name="gdn_scan", grid=(nch // cps,),
        in_specs=[row, row, pl.BlockSpec((cps * CH, DH), lambda c: (c, 0)), row, row,
                  pl.BlockSpec((cps, HEADS, CH, CH), lambda c: (c, 0, 0, 0))],
        out_specs=[row, row, pl.BlockSpec((cps, DH, GW), lambda c: (c, 0, 0))],
        out_shape=[big, jax.ShapeDtypeStruct((n, GW), BF16), jax.ShapeDtypeStruct((nch, DH, GW), BF16)],
        scratch_shapes=[pltpu.VMEM((HEADS, DH, DH), F32)],
        compiler_params=_params(("arbitrary",)),
    )(q, k, bg, u, w, p)


def _gdn_scan_bwd(q, k, bg, w, p, vn, s_in, do):
    n = q.shape[0]
    nch = n // CH
    cps = SCAN_CPS if nch % SCAN_CPS == 0 else 1
    rev = lambda c: nch // cps - 1 - c

    def body(q_ref, k_ref, bg_ref, w_ref, p_ref, vn_ref, s_ref, do_ref,
             dqg_ref, dp_ref, du_ref, dw_ref, dks_ref, dgam_ref, ds_scr):
        @pl.when(pl.program_id(0) == 0)
        def _():
            ds_scr[...] = jnp.zeros_like(ds_scr)

        lane = _lane((1, DH))
        hs = range(HEADS)
        sls = [slice(h * DH, (h + 1) * DH) for h in hs]
        dss = [ds_scr[h] for h in hs]
        for ci in reversed(range(cps)):
            rs = slice(ci * CH, (ci + 1) * CH)
            bg = bg_ref[rs, :]
            gcols = [bg[:, HEADS + h:HEADS + h + 1] for h in hs]
            glasts = [g[CH - 1:CH, :] for g in gcols]
            ss = [s_ref[ci, :, sl] for sl in sls]
            dos = [do_ref[rs, sl] for sl in sls]
            vnl = [vn_ref[rs, sl] for sl in sls]
            dqgs = [_dot(d, s, NT, P_SCANB) for d, s in zip(dos, ss)]
            dps = [_dot(d, vn, NT, P_SCANB) for d, vn in zip(dos, vnl)]
            dvn1 = [_dot(p_ref[ci, h], d, TN, P_SCANB) for h, d in zip(hs, dos)]
            dvn2 = [_dot(k_ref[rs, sl] * jnp.exp(gl - g), ds, NN, P_SCANB)
                    for sl, gl, g, ds in zip(sls, glasts, gcols, dss)]
            dkss = [_dot(vn, ds, NT, P_SCANB) for vn, ds in zip(vnl, dss)]
            dsq = [_dot(q_ref[rs, sl] * jnp.exp(g), d, TN, P_SCANB) for sl, g, d in zip(sls, gcols, dos)]
            dvns = [a + b for a, b in zip(dvn1, dvn2)]
            dws = [_dot(dvn, s, NT, P_SCANB) for dvn, s in zip(dvns, ss)]
            dsw = [_dot(w_ref[rs, sl], dvn, TN, P_SCANB) for sl, dvn in zip(sls, dvns)]
            dgam = jnp.zeros((1, DH), F32)
            for h, sl in enumerate(sls):
                dqg_ref[rs, sl] = dqgs[h]
                dp_ref[ci, h] = dps[h]
                du_ref[rs, sl] = dvns[h].astype(BF16)
                dw_ref[rs, sl] = (-dws[h]).astype(BF16)
                dks_ref[rs, sl] = dkss[h]
                tot = jnp.sum(jnp.sum(dss[h] * ss[h], axis=-1, keepdims=True), axis=0, keepdims=True)
                dgam = dgam + jnp.where(lane == h, tot, 0.0)
            dgam_ref[ci] = jnp.broadcast_to(dgam, (8, DH))
            dss = [ds * jnp.exp(gl) + a - b for ds, gl, a, b in zip(dss, glasts, dsq, dsw)]
        for h in hs:
            ds_scr[h] = dss[h]

    row = pl.BlockSpec((cps * CH, GW), lambda c: (rev(c), 0))
    sq =pl.BlockSpec((cps, HEADS, CH, CH), lambda c: (rev(c), 0, 0, 0))
    big = jax.ShapeDtypeStruct((n, GW), F32)
    return pl.pallas_call(
        body, name="gdn_scan_bwd", grid=(nch // cps,),
        in_specs=[row, row, pl.BlockSpec((cps * CH, DH), lambda c: (rev(c), 0)), row, sq, row,
                  pl.BlockSpec((cps, DH, GW), lambda c: (rev(c), 0, 0)), row],
        out_specs=[row, sq, row, row, row, pl.BlockSpec((cps, 8, DH), lambda c: (rev(c), 0, 0))],
        out_shape=[big, jax.ShapeDtypeStruct((nch, HEADS, CH, CH), F32), jax.ShapeDtypeStruct((n, GW), BF16),
                   jax.ShapeDtypeStruct((n, GW), BF16), big,
                   jax.ShapeDtypeStruct((nch, 8, DH), F32)],
        scratch_shapes=[pltpu.VMEM((HEADS, DH, DH), F32)],
        compiler_params=_params(("arbitrary",)),
    )(q, k, bg, w, p, vn, s_in, do)


def _gdn_intra_bwd(q, k, v, bg, bgt, t, u, w, p, dqg, dp, du, dw, dks, dgam):
    n = q.shape[0]
    nch = n // CH
    cps = 2 if nch % 2 == 0 else 1

    def body(q_ref, k_ref, v_ref, bg_ref, bgt_ref, t_ref, u_ref, w_ref, p_ref,
             dqg_ref, dp_ref, du_ref, dw_ref, dks_ref, dgam_ref, dq_ref, dk_ref, dv_ref, dbg_ref):
        i, j = _ij()
        rows1 = lax.broadcasted_iota(jnp.int32, (CH, 1), 0)
        lane = _lane((CH, DH))
        rsum = lambda x: jnp.sum(x, axis=-1, keepdims=True)
        items = [(ci, h) for ci in range(cps) for h in range(HEADS)]
        at = lambda ref, it: ref.at[it[0] * CH:(it[0] + 1) * CH, it[1] * DH:(it[1] + 1) * DH]
        ld = lambda ref: [at(ref, it)[...] for it in items]
        bgs = [bg_ref[ci * CH:(ci + 1) * CH, :] for ci in range(cps)]
        qs, ks = ld(q_ref), ld(k_ref)
        vecs = [_head_vectors(bgs[ci], bgt_ref[ci], h) for ci, h in items]
        decs = [_decay(gcol, grow) for _, gcol, grow in vecs]
        ths = [t_ref[ci, h] for ci, h in items]
        drus = [_dot(th, x_, TN, P_BWD) for th, x_ in zip(ths, ld(du_ref))]
        drws = [_dot(th, x_, TN, P_BWD) for th, x_ in zip(ths, ld(dw_ref))]
        kks = [_dot(kh, kh, NT, P_GRAM) for kh in ks]
        da1 = [_dot(dru, x_, NT, P_BWD) for dru, x_ in zip(drus, ld(u_ref))]
        da2 = [_dot(drw, x_, NT, P_BWD) for drw, x_ in zip(drws, ld(w_ref))]
        das = [jnp.where(i > j, -(x_ + y_), 0.0) for x_, y_ in zip(da1, da2)]
        dkks = [da * bcol * dec for da, (bcol, _, _), dec in zip(das, vecs, decs)]
        dps = [dp_ref[ci, h] for ci, h in items]
        dqks = [dp_ * dec for dp_, dec in zip(dps, decs)]
        dq_ps = [_dot(dqk, kh, NN, P_BWD) for dqk, kh in zip(dqks, ks)]
        dk_ps = [_dot(dqk, qh, TN, P_BWD) for dqk, qh in zip(dqks, qs)]
        dk_as = [_dot(dkk, kh, NN, P_BWD) for dkk, kh in zip(dkks, ks)]
        dk_bs = [_dot(dkk, kh, TN, P_BWD) for dkk, kh in zip(dkks, ks)]
        bcols = [vc[0] for vc in vecs]
        gcols = [vc[1] for vc in vecs]
        gams = [jnp.exp(g) for g in gcols]
        glasts = [g[CH - 1:CH, :] for g in gcols]
        es = [jnp.exp(gl - g) for gl, g in zip(glasts, gcols)]
        kgs = [kh * gam for kh, gam in zip(ks, gams)]
        dqgs, dkss = ld(dqg_ref), ld(dks_ref)
        wks = [drw * kg for drw, kg in zip(drws, kgs)]
        kss = [dk_ * (kh * e) for dk_, kh, e in zip(dkss, ks, es)]
        r_beta = [rsum(dru * x_ + wk) for dru, x_, wk in zip(drus, ld(v_ref), wks)]
        r_ak = [rsum(da * kk * dec) for da, kk, dec in zip(das, kks, decs)]
        r_gc = [rsum(wk * bcol + dqg * (qh * gam) - ks_)
                for wk, bcol, dqg, qh, gam, ks_ in zip(wks, bcols, dqgs, qs, gams, kss)]
        tk_tot = [jnp.sum(jnp.sum(ks_, axis=0, keepdims=True), axis=-1, keepdims=True) for ks_ in kss]
        mdecs = [da * (bcol * kk * dec) + dp_ * p_ref[ci, h]
                 for (ci, h), da, bcol, kk, dec, dp_ in zip(items, das, bcols, kks, decs, dps)]
        r_md = [rsum(m) for m in mdecs]
        c_md = [rsum(jnp.where(i == j, jnp.sum(m, axis=0, keepdims=True), 0.0)) for m in mdecs]
        dbgs = [jnp.zeros((CH, DH), F32) for _ in range(cps)]
        for n_, (ci, h) in enumerate(items):
            at(dv_ref, (ci, h))[...] = bcols[n_] * drus[n_]
            at(dq_ref, (ci, h))[...] = gams[n_] * dqgs[n_] + dq_ps[n_]
            at(dk_ref, (ci, h))[...] = ((bcols[n_] * gams[n_]) * drws[n_] + dk_ps[n_] + dk_as[n_] + dk_bs[n_]
                                        + dkss[n_] * es[n_])
            dbeta = r_beta[n_] + r_ak[n_]
            dglast = tk_tot[n_] + dgam_ref[ci, 0:1, h:h + 1] * jnp.exp(glasts[n_])
            dgc = r_gc[n_] + r_md[n_] - c_md[n_] + jnp.where(rows1 == CH - 1, dglast, 0.0)
            dbgs[ci] = dbgs[ci] + jnp.where(lane == h, dbeta, 0.0) + jnp.where(lane == HEADS + h, dgc, 0.0)
        for ci in range(cps):
            dbg_ref[ci * CH:(ci + 1) * CH, :] = dbgs[ci]

    row = pl.BlockSpec((cps * CH, GW), lambda c: (c, 0))
    sq = pl.BlockSpec((cps, HEADS, CH, CH), lambda c: (c, 0, 0, 0))
    small = pl.BlockSpec((cps * CH, DH), lambda c: (c, 0))
    big = jax.ShapeDtypeStruct((n, GW), F32)
    return pl.pallas_call(
        body, name="gdn_intra_bwd", grid=(nch // cps,),
        in_specs=[row, row, row, small, pl.BlockSpec((cps, DH, CH), lambda c: (c, 0, 0)), sq, row, row, sq,
                  row, sq, row, row, row, pl.BlockSpec((cps, 8, DH), lambda c: (c, 0, 0))],
        out_specs=[row, row, row, small],
        out_shape=[big, big, big, jax.ShapeDtypeStruct((n, DH), F32)],
        compiler_params=_params(("parallel",)),
    )(q, k, v, bg, bgt, t, u, w, p, dqg, dp, du, dw, dks, dgam)


def _local_step(x, tgt, h, w_g, cqw, late, norm_in_w, ad, gdn_norm_w, conv_b, final_norm_w,
                on_grad_c=None, on_grad_g=None, on_q=None):
    proj_g = _matmul(h, w_g, NT, F32, 512, 1408, 1024, "mm_proj_g", n=GW_COLS, b_outer=True)
    q, k, v = _prep_qkv(proj_g, cqw)
    if on_q is not None:
        q = on_q(q)
    bg, bgt = _prep_bg(proj_g, ad)
    u, w, p, t = _gdn_intra(q, k, v, bg, bgt)
    o, vn, s_in = _gdn_scan(q, k, bg, u, w, p)
    w_c, w_out, conv_w = late(o)
    proj_c = _matmul(h, w_c, NT, F32, 512, 1024, 1024, "mm_proj_c", n=CW_COLS, b_outer=True)
    mix = _conv_branch(proj_c, conv_w, conv_b, _gdn_out(o, proj_g, gdn_norm_w))
    dout, dout_b, g_fn, loss = _out_loss(mix, w_out, x, tgt, final_norm_w)

    g_wout = _matmul(mix, dout_b, TN, BF16, 512, 512, 2048, "mm_gwout")
    do, dproj_g, g_gn = _gdn_out_bwd(o, proj_g, gdn_norm_w, dout_b, w_out)
    dproj_c, g_cw, g_cb = _conv_branch_bwd(proj_c, conv_w, conv_b, dout_b, w_out)
    g_c = _matmul(dproj_c, h, TN, BF16, 1024, 512, 2048, "mm_gwin_c")
    if on_grad_c is not None:
        do = on_grad_c(g_c, g_wout, do)
    dqg, dp, du, dw, dks, dgam = _gdn_scan_bwd(q, k, bg, w, p, vn, s_in, do)
    dq, dk, dv, dbg = _gdn_intra_bwd(q, k, v, bg, bgt, t, u, w, p, dqg, dp, du, dw, dks, dgam)
    dproj_g, gq, gk, gv = _prep_qkv_bwd(proj_g, cqw, dq, dk, dv, dproj_g)
    dproj_g, g_al, g_dt = _prep_bg_bwd(proj_g, ad, dbg, dproj_g)
    g_g = _matmul(dproj_g, h, TN, BF16, 1408, 512, 2048, "mm_gwin_g")
    if on_grad_g is not None:
        dproj_g = on_grad_g(g_g, dproj_g)
    dh = _matmul(dproj_g, w_g, NN, F32, 1024, 1024, 1408, "mm_dh_g")
    gx, g_nin = _dh_rms_bwd(dproj_c, w_c, dh, x, norm_in_w, dout, 1024)
    small = dict(nin=g_nin, cb=g_cb, fn=g_fn, al=g_al, dt=g_dt, gn=g_gn, cq=(gq, gk, gv), cw=g_cw, loss=loss)
    return gx, small, (g_g, g_c, g_wout)


def _place():
    x, y, c = lax.axis_index("x"), lax.axis_index("y"), lax.axis_index("c")
    chips = [(1 - x, y), (x, 1 - y), (1 - x, 1 - y)]
    return x, y, c, chips


def _blk(ref, b):
    if isinstance(b, int):
        return ref.at[b * DH:(b + 1) * DH, :]
    return ref.at[pl.ds(pl.multiple_of(b * DH, DH), DH), :]


HBM = pl.BlockSpec(memory_space=pltpu.HBM)
SEM = pl.BlockSpec(memory_space=pltpu.SEMAPHORE)
EFFECT = pltpu.SideEffectType.DATAFLOW_SIDE_EFFECTING


def _split_start(name, issue, bufs, n_sems):
    nbuf = len(bufs)

    def body(*refs):
        issue(refs[:nbuf], refs[nbuf], refs[nbuf + 1])
        refs[-1][...] = jnp.zeros_like(refs[-1])

    out = pl.pallas_call(
        body, name=name,
        out_shape=(pltpu.SemaphoreType.DMA((n_sems,)), pltpu.SemaphoreType.DMA((n_sems,)),
                   *[pltpu.HBM(b.shape, b.dtype) for b in bufs], jax.ShapeDtypeStruct((8, DH), F32)),
        in_specs=[HBM] * nbuf,
        out_specs=(SEM, SEM, *[HBM] * nbuf, pl.BlockSpec(memory_space=pltpu.VMEM)),
        input_output_aliases={a: 2 + a for a in range(nbuf)},
        compiler_params=pltpu.CompilerParams(has_side_effects=EFFECT),
    )(*[pltpu.with_memory_space_constraint(b, pltpu.HBM) for b in bufs])
    return out[0], out[1], list(out[2:2 + nbuf]), out[-1]


def _split_wait(name, await_, send_sems, recv_sems, bufs, after):
    nbuf = len(bufs)
    after = list(after) if isinstance(after, (list, tuple)) else [after]

    def body(*refs):
        await_(refs[:nbuf], refs[nbuf], refs[nbuf + 1])

    out = pl.pallas_call(
        body, name=name,
        out_shape=tuple(pltpu.HBM(b.shape, b.dtype) for b in bufs),
        in_specs=[HBM] * nbuf + [SEM, SEM] + [ANY] * len(after), out_specs=tuple([HBM] * nbuf),
        input_output_aliases={a: a for a in range(nbuf)},
        compiler_params=pltpu.CompilerParams(has_side_effects=EFFECT),
    )(*bufs, send_sems, recv_sems, *after)
    return list(out)


def _phase_blocks(chip, phase, edges, parity=None):
    return [(b, blk) for b, (grp, blk) in enumerate(_shard_blocks(chip, edges))
            if grp == phase and (parity is None or b % 2 == parity)]


def _cols(ref, nblk):
    return ref.at[0:nblk * DH, :]


def _block_table(chip, edges, spare_g, spare_c):
    rows = []
    for s in range(4):
        sb = _shard_blocks(s, edges)
        rows.append([[blk if grp == "g" else spare_g for grp, blk in sb],
                     [blk if grp == "c" else spare_c for grp, blk in sb],
                     [int(grp == "g") for grp, _ in sb], [s] * ALIGNED_BLOCKS])
    return jnp.asarray(rows, jnp.int32)[chip]


def _place_own(a_shard, wo, cq, cw, bufs):
    d = a_shard.shape[1]
    chip = 2 * lax.axis_index("x") + lax.axis_index("y")

    def body(t_ref, a_ref, wo_ref, cq_ref, cw_ref, *refs):
        wg_ref, wc_ref, wog_ref, cqg_ref, cwg_ref = refs[5:]
        wg_ref[...] = a_ref[...]
        wc_ref[...] = a_ref[...]

        @pl.when(pl.program_id(0) == 0)
        def _():
            wog_ref[0] = wo_ref[...]
            cqg_ref[0] = cq_ref[...]
            cwg_ref[0] = cw_ref[...]

    whole = lambda s: pl.BlockSpec(s.shape, lambda b, t: (0,) * s.ndim)
    slot = lambda s: pl.BlockSpec((1,) + s.shape, lambda b, t: (t[3, 0],) + (0,) * s.ndim)
    return pl.pallas_call(
        body, name="place_own",
        grid_spec=pltpu.PrefetchScalarGridSpec(
            num_scalar_prefetch=1, grid=(ALIGNED_BLOCKS,),
            in_specs=[pl.BlockSpec((DH, d), lambda b, t: (b, 0)), whole(wo), whole(cq), whole(cw)] + [ANY] * 5,
            out_specs=[pl.BlockSpec((DH, d), lambda b, t: (t[0, b], 0)),
                       pl.BlockSpec((DH, d), lambda b, t: (t[1, b], 0)), slot(wo), slot(cq), slot(cw)]),
        out_shape=[jax.ShapeDtypeStruct(b.shape, b.dtype) for b in bufs],
        input_output_aliases={5 + a: a for a in range(5)},
        compiler_params=_params(("arbitrary",)),
    )(_block_table(chip, True, G_SPARE, C_SPARE), a_shard, wo, cq, cw, *bufs)


def _tie(x, token, name):
    def body(x_ref, t_ref, o_ref):
        del x_ref, t_ref, o_ref

    return pl.pallas_call(
        body, name=name, in_specs=[ANY, ANY], out_specs=ANY,
        out_shape=jax.ShapeDtypeStruct(x.shape, x.dtype), input_output_aliases={0: 0},
    )(x, token)


def _gather_start(phase, a_shard, w_grp, singles):
    ns = len(singles)

    def issue(refs, send_sems, recv_sems):
        a_ref, w_ref = refs[0], refs[1]
        x, y, c, chips = _place()
        mine = 2 * x + y
        for jj, (px, py) in enumerate(chips):
            to = dict(device_id=(px, py, c), device_id_type=MESH)
            for a in range(ns):
                pltpu.make_async_remote_copy(
                    src_ref=refs[2 + 2 * a], dst_ref=refs[3 + 2 * a].at[mine],
                    send_sem=send_sems.at[(1 + ns) * jj + 1 + a], recv_sem=recv_sems.at[(1 + ns) * jj + 1 + a],
                    **to).start()
        for s in range(4):
            for par in range(2):
                blocks = _phase_blocks(s, phase, True, par)
                if blocks:
                    @pl.when((mine == s) & (c == par))
                    def _():
                        for b, blk in blocks:
                            for jj, (px, py) in enumerate(chips):
                                pltpu.make_async_remote_copy(
                                    src_ref=_blk(a_ref, b), dst_ref=_blk(w_ref, blk),
                                    send_sem=send_sems.at[(1 + ns) * jj], recv_sem=recv_sems.at[(1 + ns) * jj],
                                    device_id=(px, py, c), device_id_type=MESH).start()

    bufs = [a_shard, w_grp] + [t for pair in singles for t in pair]
    return _split_start("gather_start_" + phase, issue, bufs, 3 * (1 + ns))


def _gather_wait(phase, send_sems, recv_sems, bufs, after):
    ns = (len(bufs) - 2) // 2

    def await_(refs, send_sems, recv_sems):
        a_ref, w_ref = refs[0], refs[1]
        x, y, c, chips = _place()
        mine = 2 * x + y
        for jj, (px, py) in enumerate(chips):
            to = dict(device_id=(px, py, c), device_id_type=MESH)
            peer = 2 * px + py
            for a in range(ns):
                cp = pltpu.make_async_remote_copy(
                    src_ref=refs[2 + 2 * a], dst_ref=refs[3 + 2 * a].at[mine],
                    send_sem=send_sems.at[(1 + ns) * jj + 1 + a], recv_sem=recv_sems.at[(1 + ns) * jj + 1 + a], **to)
                cp.wait_recv()
                cp.wait_send()
            for s in range(4):
                for par in range(2):
                    nblk = len(_phase_blocks(s, phase, True, par))
                    if nblk:
                        both = pltpu.make_async_remote_copy(
                            src_ref=_cols(a_ref, nblk), dst_ref=_cols(w_ref, nblk),
                            send_sem=send_sems.at[(1 + ns) * jj], recv_sem=recv_sems.at[(1 + ns) * jj], **to)

                        @pl.when((peer == s) & (c == par))
                        def _():
                            both.wait_recv()

                        @pl.when((mine == s) & (c == par))
                        def _():
                            both.wait_send()

    return _split_wait("gather_wait_" + phase, await_, send_sems, recv_sems, bufs, after)


def _sibling_forward_parts(phase):
    def each(w_ref, send_sems, recv_sems, start):
        x, y, c, chips = _place()
        to = dict(device_id=(x, y, 1 - c), device_id_type=MESH)
        for jj, (px, py) in enumerate(chips):
            peer = 2 * px + py
            for s in range(4):
                for par in range(2):
                    mine_blocks = _phase_blocks(s, phase, True, par)
                    theirs = len(_phase_blocks(s, phase, True, 1 - par))
                    if not (mine_blocks or theirs):
                        continue

                    @pl.when((peer == s) & (c == par))
                    def _():
                        if start:
                            for _, blk in mine_blocks:
                                pltpu.make_async_remote_copy(
                                    src_ref=_blk(w_ref, blk), dst_ref=_blk(w_ref, blk),
                                    send_sem=send_sems.at[jj], recv_sem=recv_sems.at[jj], **to).start()
                            return
                        if theirs:
                            pltpu.make_async_remote_copy(
                                src_ref=_cols(w_ref, theirs), dst_ref=_cols(w_ref, theirs),
                                send_sem=send_sems.at[jj], recv_sem=recv_sems.at[jj], **to).wait_recv()
                        if mine_blocks:
                            pltpu.make_async_remote_copy(
                                src_ref=_cols(w_ref, len(mine_blocks)), dst_ref=_cols(w_ref, len(mine_blocks)),
                                send_sem=send_sems.at[jj], recv_sem=recv_sems.at[jj], **to).wait_send()

    issue = lambda refs, send_sems, recv_sems: each(refs[0], send_sems, recv_sems, True)
    await_ = lambda refs, send_sems, recv_sems: each(refs[0], send_sems, recv_sems, False)
    return issue, await_


def _sibling_forward(phase, w_grp):
    issue, await_ = _sibling_forward_parts(phase)

    def body(w_in_ref, w_ref, send_sems, recv_sems):
        del w_in_ref
        issue([w_ref], send_sems, recv_sems)
        await_([w_ref], send_sems, recv_sems)

    return pl.pallas_call(
        body, name="sibling_forward_" + phase, in_specs=[ANY], out_specs=ANY,
        out_shape=jax.ShapeDtypeStruct(w_grp.shape, w_grp.dtype), input_output_aliases={0: 0},
        scratch_shapes=[pltpu.SemaphoreType.DMA((3,)), pltpu.SemaphoreType.DMA((3,))],
    )(w_grp)


def _merge_edges(w, edge0, mixed, name):
    d = w.shape[1]

    def body(e_ref, o_ref):
        o_ref[...] = e_ref[0:DH, :] + e_ref[DH:2 * DH, :]

    def to_block(i):
        r = mixed[-1]
        for kk in range(len(mixed) - 2, -1, -1):
            r = jnp.where(i == kk, mixed[kk], r)
        return r

    return pl.pallas_call(
        body, name=name, grid=(len(mixed),),
        in_specs=[pl.BlockSpec((2 * DH, d), lambda i: (edge0 // 2 + i, 0))],
        out_specs=pl.BlockSpec((DH, d), lambda i: (to_block(i), 0)),
        out_shape=jax.ShapeDtypeStruct(w.shape, w.dtype),
        input_output_aliases={0: 0},
        compiler_params=_params(("arbitrary",)),
    )(w)


def _scatter_start(phase, g_grp, land, singles, halved=False):
    ns = len(singles)

    def issue(refs, send_sems, recv_sems):
        g_ref, land_ref = refs[0], refs[1]
        x, y, c, chips = _place()
        for jj, (px, py) in enumerate(chips):
            to = dict(device_id=(px, py, c), device_id_type=MESH)
            peer = 2 * px + py
            for a in range(ns):
                pltpu.make_async_remote_copy(
                    src_ref=refs[2 + 2 * a].at[peer], dst_ref=refs[3 + 2 * a].at[jj],
                    send_sem=send_sems.at[(1 + ns) * jj + 1 + a], recv_sem=recv_sems.at[(1 + ns) * jj + 1 + a],
                    **to).start()
            for s in range(4):
                for par in ((0, 1) if halved else (None,)):
                    blocks = _phase_blocks(s, phase, False, par)
                    if blocks:
                        @pl.when((peer == s) if par is None else ((peer == s) & (c == par)))
                        def _():
                            for b, blk in blocks:
                                pltpu.make_async_remote_copy(
                                    src_ref=_blk(g_ref, blk), dst_ref=_blk(land_ref.at[jj], b),
                                    send_sem=send_sems.at[(1 + ns) * jj], recv_sem=recv_sems.at[(1 + ns) * jj],
                                    **to).start()

    bufs = [g_grp, land] + [t for pair in singles for t in pair]
    return _split_start("scatter_start_" + phase, issue, bufs, 3 * (1 + ns))


def _scatter_wait(phase, send_sems, recv_sems, bufs, after, halved=False):
    ns = (len(bufs) - 2) // 2

    def await_(refs, send_sems, recv_sems):
        g_ref, land_ref = refs[0], refs[1]
        x, y, c, chips = _place()
        mine = 2 * x + y
        for jj, (px, py) in enumerate(chips):
            to = dict(device_id=(px, py, c), device_id_type=MESH)
            peer = 2 * px + py
            for a in range(ns):
                cp = pltpu.make_async_remote_copy(
                    src_ref=refs[2 + 2 * a].at[peer], dst_ref=refs[3 + 2 * a].at[jj],
                    send_sem=send_sems.at[(1 + ns) * jj + 1 + a], recv_sem=recv_sems.at[(1 + ns) * jj + 1 + a], **to)
                cp.wait_recv()
                cp.wait_send()
            for s in range(4):
                for par in ((0, 1) if halved else (None,)):
                    nblk = len(_phase_blocks(s, phase, False, par))
                    if nblk:
                        both = pltpu.make_async_remote_copy(
                            src_ref=_cols(g_ref, nblk), dst_ref=_cols(land_ref.at[jj], nblk),
                            send_sem=send_sems.at[(1 + ns) * jj], recv_sem=recv_sems.at[(1 + ns) * jj], **to)

                        @pl.when((mine == s) if par is None else ((mine == s) & (c == par)))
                        def _():
                            both.wait_recv()

                        @pl.when((peer == s) if par is None else ((peer == s) & (c == par)))
                        def _():
                            both.wait_send()

    return _split_wait("scatter_wait_" + phase, await_, send_sems, recv_sems, bufs, after)


def _needed_blocks(phase, parity):
    return sorted({blk for s in range(4) for _, blk in _phase_blocks(s, phase, False, parity)})


def _pair_reduce(phase, g_grp):
    n, d = g_grp.shape

    def swap(g_ref, sib_ref, send_sem, recv_sem):
        x, y, c, _ = _place()
        to = dict(device_id=(x, y, 1 - c), device_id_type=MESH)
        for par in range(2):
            give, get = _needed_blocks(phase, 1 - par), _needed_blocks(phase, par)

            @pl.when(c == par)
            def _():
                for blk in give:
                    pltpu.make_async_remote_copy(src_ref=_blk(g_ref, blk), dst_ref=_blk(sib_ref, blk),
                                                 send_sem=send_sem, recv_sem=recv_sem, **to).start()
                pltpu.make_async_remote_copy(src_ref=_cols(g_ref, len(get)), dst_ref=_cols(sib_ref, len(get)),
                                             send_sem=send_sem, recv_sem=recv_sem, **to).wait_recv()
                pltpu.make_async_remote_copy(src_ref=_cols(g_ref, len(give)), dst_ref=_cols(sib_ref, len(give)),
                                             send_sem=send_sem, recv_sem=recv_sem, **to).wait_send()

    sib = pl.pallas_call(
        swap, name="pair_swap_" + phase, in_specs=[ANY], out_specs=ANY,
        out_shape=jax.ShapeDtypeStruct((n, d), g_grp.dtype),
        scratch_shapes=[pltpu.SemaphoreType.DMA, pltpu.SemaphoreType.DMA],
    )(*_in_hbm(g_grp))

    lists = [_needed_blocks(phase, par) for par in range(2)]
    longest = max(len(t) for t in lists)
    table = jnp.asarray([t + [t[-1]] * (longest - len(t)) for t in lists], jnp.int32)[lax.axis_index("c")]

    def add(t_ref, a_ref, b_ref, o_ref):
        o_ref[...] = (a_ref[...].astype(F32) + b_ref[...].astype(F32)).astype(o_ref.dtype)

    blk = pl.BlockSpec((DH, d), lambda i, t: (t[i], 0))
    return pl.pallas_call(
        add, name="pair_add_" + phase,
        grid_spec=pltpu.PrefetchScalarGridSpec(num_scalar_prefetch=1, grid=(longest,),
                                               in_specs=[blk, blk], out_specs=blk),
        out_shape=jax.ShapeDtypeStruct((n, d), g_grp.dtype),
        compiler_params=_params(("arbitrary",)),
    )(table, g_grp, sib)


def _sum_shard(g_g, g_c, land):
    d = g_g.shape[1]
    chip = 2 * lax.axis_index("x") + lax.axis_index("y")

    def body(t_ref, gg_ref, gc_ref, land_ref, o_ref):
        b = pl.program_id(0)
        in_g = t_ref[2, b] == 1
        own = jnp.where(in_g, gg_ref[...].astype(F32), gc_ref[...].astype(F32))
        for jj in range(3):
            own = own + land_ref[jj].astype(F32)
        o_ref[...] = jnp.where(in_g & (b % 2 != lax.axis_index("c")), 0.0, own)

    return pl.pallas_call(
        body, name="sum_w_in",
        grid_spec=pltpu.PrefetchScalarGridSpec(
            num_scalar_prefetch=1, grid=(ALIGNED_BLOCKS,),
            in_specs=[pl.BlockSpec((DH, d), lambda b, t: (t[0, b], 0)), pl.BlockSpec((DH, d), lambda b, t: (t[1, b], 0)),
                      pl.BlockSpec((3, DH, d), lambda b, t: (0, b, 0))],
            out_specs=pl.BlockSpec((DH, d), lambda b, t: (b, 0))),
        out_shape=jax.ShapeDtypeStruct((ALIGNED_W, d), F32),
        compiler_params=_params(("arbitrary",)),
    )(_block_table(chip, False, 0, 0), g_g, g_c, land)


def _sum_rows(stack, land, rows):
    _, r, d = stack.shape
    rows = min(rows, r)
    chip = 2 * lax.axis_index("x") + lax.axis_index("y")

    def body(t_ref, own_ref, land_ref, o_ref):
        acc = own_ref[0].astype(F32)
        for jj in range(3):
            acc = acc + land_ref[jj].astype(F32)
        o_ref[...] = acc

    return pl.pallas_call(
        body, name="sum_w_out",
        grid_spec=pltpu.PrefetchScalarGridSpec(
            num_scalar_prefetch=1, grid=(r // rows,),
            in_specs=[pl.BlockSpec((1, rows, d), lambda i, t: (t[0], i, 0)),
                      pl.BlockSpec((3, rows, d), lambda i, t: (0, i, 0))],
            out_specs=pl.BlockSpec((rows, d), lambda i, t: (i, 0))),
        out_shape=jax.ShapeDtypeStruct((r, d), F32),
        compiler_params=_params(("arbitrary",)),
    )(jnp.reshape(chip, (1,)).astype(jnp.int32), stack, land)


def _exchange_parts(n_swap, with_pack):
    def copies(refs, send_sems, recv_sems):
        x, y, c, _ = _place()
        me = 4 * x + 2 * y + c
        cps = [pltpu.make_async_remote_copy(
            src_ref=refs[2 * a], dst_ref=refs[2 * a + 1], send_sem=send_sems.at[a], recv_sem=recv_sems.at[a],
            device_id=(x, y, 1 - c), device_id_type=MESH) for a in range(n_swap)]
        if with_pack:
            pack_ref, packs = refs[2 * n_swap], refs[2 * n_swap + 1]
            for r in range(1, 8):
                dx, dy, dc = (r >> 2) & 1, (r >> 1) & 1, r & 1
                peer = (x + dx - 2 * x * dx, y + dy - 2 * y * dy, c + dc - 2 * c * dc)
                cps.append(pltpu.make_async_remote_copy(
                    src_ref=pack_ref, dst_ref=packs.at[me], send_sem=send_sems.at[n_swap + r - 1],
                    recv_sem=recv_sems.at[n_swap + r - 1], device_id=peer, device_id_type=MESH))
        return cps

    def issue(refs, send_sems, recv_sems):
        for cp in copies(refs, send_sems, recv_sems):
            cp.start()

    def await_(refs, send_sems, recv_sems):
        cps = copies(refs, send_sems, recv_sems)
        for cp in cps:
            cp.wait_recv()
        for cp in cps:
            cp.wait_send()

    return issue, await_, n_swap + (7 if with_pack else 0)


def _sum_packs(pack, packs):
    x, y, c = lax.axis_index("x"), lax.axis_index("y"), lax.axis_index("c")
    me = jnp.reshape(4 * x + 2 * y + c, (1,)).astype(jnp.int32)

    def body(me_ref, own_ref, p_ref, o_ref):
        acc = jnp.where(me_ref[0] == 0, own_ref[...], p_ref[0])
        for d in range(1, 8):
            acc = acc + jnp.where(me_ref[0] == d, own_ref[...], p_ref[d])
        o_ref[...] = acc

    full = lambda s: pl.BlockSpec(s.shape, lambda i, t: (0,) * s.ndim)
    return pl.pallas_call(
        body, name="sum_packs",
        grid_spec=pltpu.PrefetchScalarGridSpec(num_scalar_prefetch=1, grid=(1,), in_specs=[full(pack), full(packs)],
                                               out_specs=full(pack)),
        out_shape=jax.ShapeDtypeStruct(pack.shape, F32),
    )(me, pack, packs)


def _adamw_update(g, w_ref, m_ref, v_ref, go, do, mo, vo):
    c1 = 1.0 / (1.0 - ADAM_B1 ** ADAM_STEP)
    c2 = 1.0 / (1.0 - ADAM_B2 ** ADAM_STEP)
    mn = ADAM_B1 * m_ref[...] + (1.0 - ADAM_B1) * g
    vn = ADAM_B2 * v_ref[...] + (1.0 - ADAM_B2) * (g * g)
    go[...] = g
    mo[...] = mn
    vo[...] = vn
    do[...] = -ADAM_LR * ((mn * c1) / (jnp.sqrt(vn * c2) + ADAM_EPS) + ADAM_WD * w_ref[...])


def _adamw(w, m, v, g1, g2, rows, name):
    r, cdim = w.shape
    rows = min(rows, r)

    def body(w_ref, m_ref, v_ref, g1_ref, g2_ref, *outs):
        _adamw_update(g1_ref[...] + g2_ref[...], w_ref, m_ref, v_ref, *outs)

    blk = pl.BlockSpec((rows, cdim), lambda i: (i, 0))
    shp = jax.ShapeDtypeStruct((r, cdim), F32)
    return pl.pallas_call(
        body, name=name, grid=(r // rows,),
        in_specs=[blk] * 5, out_specs=[blk] * 4, out_shape=[shp] * 4,
        compiler_params=_params(("parallel",), 20 * rows * cdim * 4 + 8 * 2**20),
    )(w, m, v, *_in_hbm(g1, g2))


def _adamw_small(w_s, m_s, v_s, tot):
    per_row = GW // DH
    cq_blocks, cw_blocks = 3 * per_row // 4, per_row // 4
    cq_lanes, cw_lanes = cq_blocks * DH, cw_blocks * DH
    shapes = [(1, GW), (4, cq_lanes), (1, HEADS), (1, HEADS), (1, DH), (3, cw_lanes), (1, GW), (per_row, DH)]

    def body(t_ref, w_ref, m_ref, v_ref, *refs):
        g_scr, kinds = refs[len(refs) - 5], refs[len(refs) - 4:]
        chip = 2 * lax.axis_index("x") + lax.axis_index("y")

        def mine(first_row, rows_per_tap, nblk, t, jb):
            out = None
            for k in reversed(range(4)):
                b = nblk * k + jb
                row = first_row + rows_per_tap * t + b // per_row
                cand = t_ref[row:row + 1, (b % per_row) * DH:(b % per_row + 1) * DH]
                out = cand if out is None else jnp.where(chip == k, cand, out)
            return out

        g_scr[...] = jnp.zeros_like(g_scr)
        for src, dst in ((R_NIN, S_NIN), (R_CB, S_CB), (R_FN, S_FN), (R_AD, S_AD), (R_GN, S_GN)):
            g_scr[dst:dst + 1, :] = t_ref[src:src + 1, :]
        for t in range(4):
            for jb in range(cq_blocks):
                g_scr[S_CQ + t:S_CQ + t + 1, jb * DH:(jb + 1) * DH] = mine(R_CQ, 3, cq_blocks, t, jb)
        for t in range(3):
            for jb in range(cw_blocks):
                g_scr[S_CW + t:S_CW + t + 1, jb * DH:(jb + 1) * DH] = mine(R_CW, 1, cw_blocks, t, jb)
        _adamw_update(g_scr[...], w_ref, m_ref, v_ref, *kinds)
        for kk, a in enumerate(kinds):
            nin, cq, al, dt, gn, cw, cb, fn = refs[8 * kk:8 * kk + 8]
            nin[...] = a[S_NIN:S_NIN + 1, :]
            cq[...] = a[S_CQ:S_CQ + 4, 0:cq_lanes]
            al[...] = a[S_AD:S_AD + 1, 0:HEADS]
            dt[...] = a[S_AD:S_AD + 1, HEADS:2 * HEADS]
            gn[...] = a[S_GN:S_GN + 1, 0:DH]
            cw[...] = a[S_CW:S_CW + 3, 0:cw_lanes]
            cb[...] = a[S_CB:S_CB + 1, :]
            for k in range(per_row):
                fn[k:k + 1, :] = a[S_FN:S_FN + 1, k * DH:(k + 1) * DH]

    out = pl.pallas_call(
        body, name="adamw_small",
        out_shape=[jax.ShapeDtypeStruct(s, F32) for s in shapes] * 4,
        scratch_shapes=[pltpu.VMEM(w_s.shape, F32)] * 5,
    )(tot, w_s, m_s, v_s)
    return [out[8 * kk:8 * kk + 8] for kk in range(4)]


def _adamw_shard(wt, mt, vt, g1, g2):
    r, d = wt.shape
    cols = min(256, d)

    def body(w_ref, m_ref, v_ref, g_ref, g2_ref, go, do, mo, vo, pad_ref):
        chip = 2 * lax.axis_index("x") + lax.axis_index("y")
        back = [(ALIGNED_W - s) % ALIGNED_W for s in SHIFTS]
        pad_ref[...] = pltpu.roll(g_ref[...] + g2_ref[...], _by_chip(chip, back), 0)
        outs = [o.at[:, 0, :] for o in (go, do, mo, vo)]
        _adamw_update(pad_ref[0:r, :], w_ref, m_ref, v_ref, *outs)

    blk = pl.BlockSpec((r, cols), lambda i: (0, i))
    gblk = pl.BlockSpec((ALIGNED_W, cols), lambda i: (0, i))
    oblk = pl.BlockSpec((r, 1, cols), lambda i: (0, 0, i))
    shp = jax.ShapeDtypeStruct((r, 1, d), F32)
    return pl.pallas_call(
        body, name="adamw_w_in", grid=(d // cols,),
        in_specs=[blk] * 3 + [gblk] * 2, out_specs=[oblk] * 4, out_shape=[shp] * 4,
        scratch_shapes=[pltpu.VMEM((ALIGNED_W, cols), F32)],
        compiler_params=_params(("parallel",), 24 * ALIGNED_W * cols * 4 + 8 * 2**20),
    )(wt, mt, vt, g1, g2)


def _pad_lanes(a, width):
    return jnp.pad(a, ((0, 0), (0, width - a.shape[1])))


def _gathered_to_full(g):
    return jnp.transpose(g, (1, 0, 2)).reshape(g.shape[1], 4 * g.shape[2])


def _row(a):
    return _pad_lanes(a.reshape(1, -1), 1024)


S_NIN, S_CB, S_FN, S_AD, S_GN, S_CQ, S_CW = 0, 1, 2, 3, 4, 5, 9


def _small_pack(nin, cb, fn, al, dt, gn, cqw_shard, cw_shard):
    ad = jnp.concatenate([al.reshape(1, -1), dt.reshape(1, -1)], axis=1)
    rows = [_row(nin), _row(cb), _row(fn), _row(ad), _row(gn), _pad_lanes(cqw_shard, 1024),
            _pad_lanes(cw_shard, 1024)]
    out = jnp.concatenate(rows, axis=0)
    return jnp.pad(out, ((0, 16 - out.shape[0]), (0, 0)))


def kernel(x, norm_in_w, w_in, conv_qkv_w, A_log, dt_bias, gdn_norm_w, conv_w, conv_b, w_out, final_norm_w, loss_target, m_norm_in_w, m_w_in, m_conv_qkv_w, m_A_log, m_dt_bias, m_gdn_norm_w, m_conv_w, m_conv_b, m_w_out, m_final_norm_w, v_norm_in_w, v_w_in, v_conv_qkv_w, v_A_log, v_dt_bias, v_gdn_norm_w, v_conv_w, v_conv_b, v_w_out, v_final_norm_w):
    a_shard = _align_shard(jnp.transpose(w_in, (2, 0, 1)))
    d_model = x.shape[-1]
    stack = lambda s: lax.empty((4,) + s.shape, s.dtype)
    wg0 = lax.empty((WG_BLOCKS * DH, d_model), BF16)
    wc0 = lax.empty((WC_BLOCKS * DH, d_model), BF16)
    ss_g, rs_g, bufs_g, tok_g = _gather_start("g", a_shard, wg0, [(conv_qkv_w[0], stack(conv_qkv_w[0]))])
    wo_b = _cast_bf16(w_out[0], 256, "cast_w_out", tok_g)
    ss_c, rs_c, bufs_c, tok_c = _gather_start("c", bufs_g[0], wc0,
                                              [(conv_w[0], stack(conv_w[0])), (wo_b, stack(wo_b))])
    wg1, wc1, wog1, cqg1, cwg1 = _place_own(bufs_c[0], bufs_c[4], bufs_g[2], bufs_c[2],
                                            [bufs_g[1], bufs_c[1], bufs_c[5], bufs_g[3], bufs_c[3]])
    x0 = x[0]
    h = _rms_in(x0, _tie(_tie(norm_in_w, tok_g, "after_gather_start_g"), tok_c, "after_gather_start_c"))
    adam_in = [jnp.transpose(a[0]) for a in (w_in, m_w_in, v_w_in)]
    sp = lambda nin, cb, fn, al, dt, gn, cq, cwv: _small_pack(nin, cb, fn, al, dt, gn, cq[0], cwv[0])
    w_s = sp(norm_in_w, conv_b, final_norm_w, A_log, dt_bias, gdn_norm_w, conv_qkv_w, conv_w)
    m_s = sp(m_norm_in_w, m_conv_b, m_final_norm_w, m_A_log, m_dt_bias, m_gdn_norm_w, m_conv_qkv_w, m_conv_w)
    v_s = sp(v_norm_in_w, v_conv_b, v_final_norm_w, v_A_log, v_dt_bias, v_gdn_norm_w, v_conv_qkv_w, v_conv_w)
    a_thru, wg, _, cq_g = _gather_wait("g", ss_g, rs_g, [bufs_c[0], wg1, bufs_g[2], cqg1],
                                       [h, w_s, m_s, v_s] + adam_in[1:])
    w_g = _merge_edges(_sibling_forward("g", wg), G_EDGE, G_MIXED, "merge_edges_g")
    cqw = _gathered_to_full(cq_g)
    ad = jnp.pad(jnp.concatenate([A_log, dt_bias], axis=0), ((0, 0), (A_LANE, 0)))
    fwd_c = {}

    def on_q(q):
        _, wc, _, cw_g, _, wo_g = _gather_wait("c", ss_c, rs_c,
                                               [a_thru, wc1, bufs_c[2], cwg1, bufs_c[4], wog1], q)
        issue, _ = _sibling_forward_parts("c")
        ss, rs, (wc,), tok = _split_start("sibling_forward_start_c", issue, [wc], 3)
        fwd_c.update(ss=ss, rs=rs, wc=wc, cw_g=cw_g, wo_g=wo_g)
        return _tie(q, tok, "after_sibling_forward_start_c")

    def late(o):
        _, await_ = _sibling_forward_parts("c")
        (wc,) = _split_wait("sibling_forward_wait_c", await_, fwd_c["ss"], fwd_c["rs"], [fwd_c["wc"]], o)
        return (_merge_edges(wc, C_EDGE, C_MIXED, "merge_edges_c"), fwd_c["wo_g"].reshape(2 * GW, d_model),
                _gathered_to_full(fwd_c["cw_g"]))

    scat = {}

    def on_grad_c(g_c, g_wout, do):
        go4 = g_wout.reshape(4, GW // 2, d_model)
        land = lax.empty((3, ALIGNED_W, d_model), BF16)
        land_o = lax.empty((3, GW // 2, d_model), BF16)
        ss, rs, bufs, tok = _scatter_start("c", g_c, land, [(go4, land_o)])
        scat["c"] = (ss, rs, bufs)
        return _tie(do, tok, "after_scatter_start_c")

    def on_grad_g(g_g, dproj_g):
        ss, rs, bufs, tok = _scatter_start("g", _pair_reduce("g", g_g), scat["c"][2][1], [], halved=True)
        scat["g"] = (ss, rs, bufs)
        return _tie(dproj_g, tok, "after_scatter_start_g")

    gx, sm, _ = _local_step(x0, loss_target[0], h, w_g, cqw, late, norm_in_w, ad, gdn_norm_w, conv_b,
                            final_norm_w.reshape(1, -1), on_grad_c, on_grad_g, on_q)

    ss, rs, bufs = scat["c"]
    g_c, land, go4, land_o = _scatter_wait("c", ss, rs, [bufs[0], scat["g"][2][1], bufs[2], bufs[3]], gx)
    part_out = _sum_rows(go4, land_o, 128)
    ad_g = jnp.concatenate([sm["al"][:, A_LANE:], sm["dt"][:, A_LANE:]], axis=1)
    pack = jnp.concatenate([_row(sm["nin"]), _row(sm["cb"]), _row(sm["fn"]), _row(ad_g), _row(sm["gn"]),
                            jnp.concatenate(sm["cq"], axis=1).reshape(12, 1024), sm["cw"], _row(sm["loss"])], axis=0)
    pack = jnp.pad(pack, ((0, PACK_ROWS - pack.shape[0]), (0, 0)))
    issue, await_a, nsem = _exchange_parts(1, True)
    ss_a, rs_a, bufs_a, tok_a = _split_start(
        "exchange_start_small", issue,
        [part_out, lax.empty(part_out.shape, F32), pack, lax.empty((8,) + pack.shape, F32)], nsem)
    ss, rs, bufs = scat["g"]
    g_g, land = _scatter_wait("g", ss, rs, [bufs[0], land], [gx, tok_a], halved=True)
    part_in = _sum_shard(g_g, g_c, land)
    issue, await_b, nsem = _exchange_parts(1, False)
    ss_b, rs_b, bufs_b, tok_b = _split_start("exchange_start_w_in", issue,
                                             [part_in, lax.empty(part_in.shape, F32)], nsem)
    part_out, sib_out, pack, packs = _split_wait("exchange_wait_small", await_a, ss_a, rs_a, bufs_a, tok_b)
    tot = _sum_packs(pack, packs)
    g_wo, d_wo, m_wo, v_wo = _adamw(w_out[0], m_w_out[0], v_w_out[0], part_out, sib_out, 128, "adamw_w_out")
    small = _adamw_small(w_s, m_s, v_s, tot)
    part_in, sib_in = _split_wait("exchange_wait_w_in", await_b, ss_b, rs_b, bufs_b, [small[0][0], d_wo])
    g_wi, d_wi, m_wi, v_wi = [jnp.transpose(a, (1, 2, 0))[0] for a in _adamw_shard(*adam_in, part_in, sib_in)]

    def unpack(leaves, big_in, big_out):
        nin, cq, al, dt, gn, cw, cb, fn = leaves
        return (nin, big_in[None], cq[None], al, dt, gn, cw[None], cb, big_out[None], fn.reshape(-1))

    loss = tot[R_LOSS, 0]
    return (loss, gx[None], *unpack(small[0], g_wi, g_wo), *unpack(small[1], d_wi, d_wo),
            *unpack(small[2], m_wi, m_wo), *unpack(small[3], v_wi, v_wo))
```

```python
import jax
import jax.numpy as jnp
from jax import lax
from jax.experimental import pallas as pl
from jax.experimental.pallas import tpu as pltpu

F32 = jnp.float32
BF16 = jnp.bfloat16
MESH = pl.DeviceIdType.MESH
ANY = pl.BlockSpec(memory_space=pl.ANY)

HEADS = 8
DH = 128
CH = 64
GW = HEADS * DH
EPS = 1e-6
VMEM_V7X = 64 * 1024 * 1024

QB, KB, VB, ZB, BAB = 0, 8, 16, 24, 32
A_LANE = 120
NG, NC = 33, 32
GW_COLS, CW_COLS = NG * DH, NC * DH

SHARD_W = 2052
ALIGNED_BLOCKS = 17
ALIGNED_W = ALIGNED_BLOCKS * DH
SHIFTS = (0, 4, ALIGNED_W - 8, ALIGNED_W - 4)
G_EDGE, C_EDGE = 34, 32
G_SPARE, C_SPARE = 33, 34
WG_BLOCKS, WC_BLOCKS = 38, 36
G_MIXED, C_MIXED = (2, BAB), (4 * 7 + 1,)


def _shard_blocks(chip, edges):
    g, c = "g", "c"
    if chip == 0:
        out = [(g, 3 * b) for b in range(8)] + [(g, 3 * b + 1) for b in range(8)] + [(g, G_EDGE, G_MIXED[0])]
    elif chip == 1:
        out = [(g, G_EDGE + 1, G_MIXED[0])] + [(g, 3 * b + 2) for b in range(1, 8)]
        out += [(g, ZB + b) for b in range(8)] + [(g, G_EDGE + 2, G_MIXED[1])]
    elif chip == 2:
        out = [(c, 4 * b) for b in range(8)] + [(c, 4 * b + 1) for b in range(7)]
        out += [(c, C_EDGE, C_MIXED[0]), (g, G_EDGE + 3, G_MIXED[1])]
    else:
        out = [(c, 4 * b + 2) for b in range(8)] + [(c, 4 * b + 3) for b in range(8)] + [(c, C_EDGE + 1, C_MIXED[0])]
    return [(o[0], o[1] if (edges or len(o) == 2) else o[2]) for o in out]


def _by_chip(chip, vals):
    if all(v == vals[0] for v in vals):
        return vals[0]
    r = vals[3]
    for kk in (2, 1, 0):
        r = jnp.where(chip == kk, vals[kk], r)
    return r

ADAM_LR, ADAM_B1, ADAM_B2, ADAM_EPS, ADAM_WD, ADAM_STEP = 0.001, 0.9, 0.999, 1e-08, 0.01, 10

R_NIN, R_CB, R_FN, R_AD, R_GN, R_CQ, R_CW, R_LOSS, PACK_ROWS = 0, 1, 2, 3, 4, 5, 17, 20, 24

NN = ((1,), (0,))
NT = ((1,), (1,))
TN = ((0,), (0,))


def _dot(a, b, dims=NN, mode="lo"):
    dn = (dims, ((), ()))
    if mode == "hi":
        return lax.dot_general(a, b, dn, precision=lax.Precision.HIGHEST, preferred_element_type=F32)
    ah, bh = a.astype(BF16), b.astype(BF16)
    out = lax.dot_general(ah, bh, dn, preferred_element_type=F32)
    if mode == "x3":
        al = (a - ah.astype(F32)).astype(BF16)
        bl = (b - bh.astype(F32)).astype(BF16)
        out = out + lax.dot_general(ah, bl, dn, preferred_element_type=F32)
        out = out + lax.dot_general(al, bh, dn, preferred_element_type=F32)
    return out


P_GRAM, P_INV, P_SOL, P_SCAN, P_SCANB, P_BWD = "lo", "lo", "lo", "lo", "lo", "lo"
P_CUM = "x3"


def _params(sem=None, vmem=None):
    kw = {}
    if sem is not None:
        kw["dimension_semantics"] = sem
    if vmem is not None:
        kw["vmem_limit_bytes"] = int(min(max(vmem, 32 * 2**20), VMEM_V7X - 8 * 2**20))
    return pltpu.CompilerParams(**kw)


def _in_hbm(*arrays):
    return [pltpu.with_memory_space_constraint(a, pltpu.HBM) for a in arrays]


def _sigmoid(x):
    return 1.0 / (1.0 + jnp.exp(-x))


def _dsilu(x, s):
    return s * (1.0 + x * (1.0 - s))


def _rows(shape):
    return lax.broadcasted_iota(jnp.int32, shape, 0)


def _shift_down(x, s):
    if s == 0:
        return x
    return jnp.where(_rows(x.shape) >= s, pltpu.roll(x, s, 0), 0.0)


def _shift_up(x, s):
    if s == 0:
        return x
    n = x.shape[0]
    return jnp.where(_rows(x.shape) < n - s, pltpu.roll(x, n - s, 0), 0.0)


def _matmul(a, b, dims, out_dtype, tm, tn, tk, name, add=None, n=None, b_outer=False):
    if dims == NN:
        (m, k), n = a.shape, b.shape[1]
    elif dims == NT:
        (m, k), n = a.shape, (n or b.shape[0])
    else:
        (k, m), n = a.shape, b.shape[1]
    tm, tn, tk = min(tm, m), min(tn, n), min(tk, k)
    assert m % tm == 0 and n % tn == 0 and k % tk == 0, (name, m, n, k, tm, tn, tk)
    nk = k // tk

    def body(*refs):
        if add is None:
            a_ref, b_ref, o_ref = refs[:3]
            add_ref = None
        else:
            a_ref, b_ref, add_ref, o_ref = refs[:4]
        part = _dot(a_ref[...], b_ref[...], dims)
        if nk == 1:
            if add_ref is not None:
                part = part + add_ref[...]
            o_ref[...] = part.astype(out_dtype)
            return
        acc = refs[-1]
        kk = pl.program_id(2)

        @pl.when(kk == 0)
        def _():
            acc[...] = part

        @pl.when(kk > 0)
        def _():
            acc[...] += part

        @pl.when(kk == nk - 1)
        def _():
            r = acc[...]
            if add_ref is not None:
                r = r + add_ref[...]
            o_ref[...] = r.astype(out_dtype)

    ij = (lambda g0, g1: (g1, g0)) if b_outer else (lambda g0, g1: (g0, g1))

    def spec(shape, pick):
        return pl.BlockSpec(shape, lambda g0, g1, kk: pick(*ij(g0, g1), kk))

    a_spec = spec((tk, tm), lambda i, j, kk: (kk, i)) if dims == TN else spec((tm, tk), lambda i, j, kk: (i, kk))
    b_spec = spec((tn, tk), lambda i, j, kk: (j, kk)) if dims == NT else spec((tk, tn), lambda i, j, kk: (kk, j))
    o_spec = spec((tm, tn), lambda i, j, kk: (i, j))
    in_specs = [a_spec, b_spec]
    args = [a, b]
    if add is not None:
        in_specs.append(o_spec)
        args.append(add)
    osz = jnp.dtype(out_dtype).itemsize
    est = 2 * (tm * tk * a.dtype.itemsize + tk * tn * b.dtype.itemsize + tm * tn * osz)
    est += 3 * tm * tn * 4 + (2 * tm * tn * 4 if add is not None else 0)
    return pl.pallas_call(
        body, name=name, grid=(n // tn, m // tm, nk) if b_outer else (m // tm, n // tn, nk),
        in_specs=in_specs, out_specs=o_spec,
        out_shape=jax.ShapeDtypeStruct((m, n), out_dtype),
        scratch_shapes=[pltpu.VMEM((tm, tn), F32)] if nk > 1 else [],
        compiler_params=_params(("parallel", "parallel", "arbitrary"), est + 8 * 2**20),
    )(*args)


def _cast_bf16(a, rows, name, after):
    r, c = a.shape
    rows = min(rows, r)

    def body(a_ref, t_ref, o_ref):
        del t_ref
        o_ref[...] = a_ref[...].astype(BF16)

    return pl.pallas_call(
        body, name=name, grid=(r // rows,),
        in_specs=[pl.BlockSpec((rows, c), lambda i: (i, 0)), ANY],
        out_specs=pl.BlockSpec((rows, c), lambda i: (i, 0)),
        out_shape=jax.ShapeDtypeStruct((r, c), BF16),
        compiler_params=_params(("parallel",)),
    )(a, after)


def _align_shard(wt):
    r, _, d = wt.shape
    cols = min(256, d)

    def body(w_ref, o_ref, pad_ref):
        chip = 2 * lax.axis_index("x") + lax.axis_index("y")
        pad_ref[...] = jnp.zeros_like(pad_ref)
        pad_ref[0:r, :] = w_ref[:, 0, :]
        o_ref[...] = pltpu.roll(pad_ref[...], _by_chip(chip, SHIFTS), 0).astype(BF16)

    return pl.pallas_call(
        body, name="align_shard", grid=(d // cols,),
        in_specs=[pl.BlockSpec((r, 1, cols), lambda i: (0, 0, i))],
        out_specs=pl.BlockSpec((ALIGNED_W, cols), lambda i: (0, i)),
        out_shape=jax.ShapeDtypeStruct((ALIGNED_W, d), BF16),
        scratch_shapes=[pltpu.VMEM((ALIGNED_W, cols), F32)],
        compiler_params=_params(("parallel",)),
    )(wt)


def _rms_in(x, w):
    n, d = x.shape
    tr = min(256, n)

    def body(x_ref, w_ref, h_ref):
        xv = x_ref[...]
        r = lax.rsqrt(jnp.mean(xv * xv, axis=-1, keepdims=True) + EPS)
        h_ref[...] = (xv * r * w_ref[...]).astype(BF16)

    return pl.pallas_call(
        body, name="rms_in", grid=(n // tr,),
        in_specs=[pl.BlockSpec((tr, d), lambda i: (i, 0)), pl.BlockSpec((1, d), lambda i: (0, 0))],
        out_specs=pl.BlockSpec((tr, d), lambda i: (i, 0)),
        out_shape=jax.ShapeDtypeStruct((n, d), BF16),
        compiler_params=_params(("parallel",)),
    )(x, w)


def _conv_silu(p, w_ref, taps):
    c = None
    for j in range(taps):
        t = _shift_down(p, taps - 1 - j) * w_ref[j:j + 1, :]
        c = t if c is None else c + t
    return c


def _prep_qkv(proj, cw):
    n = proj.shape[0]

    def body(p3, wq, wk, wv, q_ref, k_ref, v_ref):
        for kind, (w_ref, o_ref) in enumerate(((wq, q_ref), (wk, k_ref), (wv, v_ref))):
            c = _conv_silu(p3[:, kind * DH:(kind + 1) * DH], w_ref, 4)
            a = c * _sigmoid(c)
            if kind < 2:
                r = lax.rsqrt(jnp.sum(a * a, axis=-1, keepdims=True) + EPS)
                a = a * (r * (DH ** -0.5 if kind == 0 else 1.0))
            o_ref[...] = a

    col = pl.BlockSpec((n, DH), lambda h: (0, h))
    wcol = lambda base: pl.BlockSpec((4, DH), lambda h: (0, base + h))
    out = jax.ShapeDtypeStruct((n, GW), F32)
    return pl.pallas_call(
        body, name="prep_qkv", grid=(HEADS,),
        in_specs=[pl.BlockSpec((n, 3 * DH), lambda h: (0, h)), wcol(QB), wcol(KB), wcol(VB)],
        out_specs=[col] * 3, out_shape=[out] * 3,
        compiler_params=_params(("parallel",), 40 * 2**20),
    )(proj, cw, cw, cw)


def _prep_qkv_bwd(proj, cw, dq, dk, dv, dproj):
    n = proj.shape[0]

    def body(p3, wq, wk, wv, dq_ref, dk_ref, dv_ref, _, o3, gq, gk, gv):
        for kind, (w_ref, d_ref, g_ref) in enumerate(((wq, dq_ref, gq), (wk, dk_ref, gk), (wv, dv_ref, gv))):
            p = p3[:, kind * DH:(kind + 1) * DH]
            shifted = [_shift_down(p, 3 - j) for j in range(4)]
            c = shifted[0] * w_ref[0:1, :]
            for j in range(1, 4):
                c = c + shifted[j] * w_ref[j:j + 1, :]
            s = _sigmoid(c)
            a = c * s
            d = d_ref[...]
            if kind < 2:
                r = lax.rsqrt(jnp.sum(a * a, axis=-1, keepdims=True) + EPS)
                sc = DH ** -0.5 if kind == 0 else 1.0
                d = (sc * r) * (d - a * ((r * r) * jnp.sum(d * a, axis=-1, keepdims=True)))
            dc = d * _dsilu(c, s)
            dp = None
            for j in range(4):
                g_ref[j:j + 1, :] = jnp.sum(dc * shifted[j], axis=0, keepdims=True)
                t = _shift_up(dc, 3 - j) * w_ref[j:j + 1, :]
                dp = t if dp is None else dp + t
            o3[:, kind * DH:(kind + 1) * DH] = dp.astype(BF16)

    col = pl.BlockSpec((n, DH), lambda h: (0, h))
    wcol = lambda base: pl.BlockSpec((4, DH), lambda h: (0, base + h))
    p3spec = pl.BlockSpec((n, 3 * DH), lambda h: (0, h))
    return pl.pallas_call(
        body, name="prep_qkv_bwd", grid=(HEADS,),
        in_specs=[p3spec, wcol(QB), wcol(KB), wcol(VB), col, col, col, ANY],
        out_specs=[p3spec] + [wcol(0)] * 3,
        out_shape=[jax.ShapeDtypeStruct(dproj.shape, BF16)] + [jax.ShapeDtypeStruct((4, GW), F32)] * 3,
        input_output_aliases={7: 0},
        compiler_params=_params(("parallel",), 48 * 2**20),
    )(proj, cw, cw, cw, dq, dk, dv, dproj)


CPB = 8
SCAN_CPS = 4


def _tri(lower, rows):
    i = lax.broadcasted_iota(jnp.int32, (rows, rows), 0)
    j = lax.broadcasted_iota(jnp.int32, (rows, rows), 1)
    return jnp.where((i // CH == j // CH) & ((i >= j) if lower else (j >= i)), 1.0, 0.0)


def _lane(shape):
    return lax.broadcasted_iota(jnp.int32, shape, 1)


def _prep_bg(proj, ad):
    n = proj.shape[0]
    nch = n // CH
    cpb = CPB if nch % CPB == 0 else 1
    rows = cpb * CH

    def body(p_ref, ad_ref, bg_ref, bgt_ref):
        p = p_ref[...]
        lane = _lane(p.shape)
        beta = _sigmoid(p)
        xa = p + ad_ref[1:2, :]
        sp = jnp.maximum(xa, 0.0) + jnp.log(1.0 + jnp.exp(-jnp.abs(xa)))
        g = pltpu.roll(-jnp.exp(ad_ref[0:1, :]) * sp, DH - A_LANE + HEADS, 1)
        gc = _dot(_tri(True, rows), g, NN, P_CUM)
        bg = jnp.where(lane < HEADS, beta, jnp.where(lane < 2 * HEADS, gc, 0.0))
        bg_ref[...] = bg
        for ci in range(cpb):
            bgt_ref[ci] = bg[ci * CH:(ci + 1) * CH, :].T

    return pl.pallas_call(
        body, name="prep_bg", grid=(nch // cpb,),
        in_specs=[pl.BlockSpec((rows, DH), lambda i: (i, BAB)), pl.BlockSpec((2, DH), lambda i: (0, 0))],
        out_specs=[pl.BlockSpec((rows, DH), lambda i: (i, 0)), pl.BlockSpec((cpb, DH, CH), lambda i: (i, 0, 0))],
        out_shape=[jax.ShapeDtypeStruct((n, DH), F32), jax.ShapeDtypeStruct((nch, DH, CH), F32)],
        compiler_params=_params(("parallel",)),
    )(*_in_hbm(proj, ad))


def _prep_bg_bwd(proj, ad, dbg, dproj):
    n = proj.shape[0]
    nch = n // CH
    cpb = CPB if nch % CPB == 0 else 1
    rows = cpb * CH

    def body(p_ref, ad_ref, d_ref, _, o_ref, ga_ref, gd_ref):
        p = p_ref[...]
        d = d_ref[...]
        lane = _lane(p.shape)
        beta = _sigmoid(p)
        xa = p + ad_ref[1:2, :]
        sp = jnp.maximum(xa, 0.0) + jnp.log(1.0 + jnp.exp(-jnp.abs(xa)))
        na = -jnp.exp(ad_ref[0:1, :])
        dg = pltpu.roll(_dot(_tri(False, rows), d, NN, P_CUM), A_LANE - HEADS, 1)
        da = dg * na * _sigmoid(xa)
        is_g = lane >= A_LANE
        o_ref[...] = jnp.where(lane < HEADS, d * beta * (1.0 - beta), jnp.where(is_g, da, 0.0)).astype(BF16)
        ga = jnp.sum(jnp.where(is_g, dg * na * sp, 0.0), axis=0, keepdims=True)
        gd = jnp.sum(jnp.where(is_g, da, 0.0), axis=0, keepdims=True)

        @pl.when(pl.program_id(0) == 0)
        def _():
            ga_ref[...] = jnp.zeros_like(ga_ref)
            gd_ref[...] = jnp.zeros_like(gd_ref)

        ga_ref[...] += ga
        gd_ref[...] += gd

    one = pl.BlockSpec((1, DH), lambda i: (0, 0))
    return pl.pallas_call(
        body, name="prep_bg_bwd", grid=(nch // cpb,),
        in_specs=[pl.BlockSpec((rows, DH), lambda i: (i, BAB)), pl.BlockSpec((2, DH), lambda i: (0, 0)),
                  pl.BlockSpec((rows, DH), lambda i: (i, 0)), ANY],
        out_specs=[pl.BlockSpec((rows, DH), lambda i: (i, BAB)), one, one],
        out_shape=[jax.ShapeDtypeStruct(dproj.shape, BF16), jax.ShapeDtypeStruct((1, DH), F32),
                   jax.ShapeDtypeStruct((1, DH), F32)],
        input_output_aliases={3: 0},
        compiler_params=_params(("arbitrary",)),
    )(proj, ad, dbg, dproj)


def _gdn_out(o, proj, wg):
    n = o.shape[0]

    def body(o_ref, z_ref, w_ref, y_ref):
        ov, z = o_ref[...], z_ref[...]
        r = lax.rsqrt(jnp.mean(ov * ov, axis=-1, keepdims=True) + EPS)
        y_ref[...] = (ov * r * w_ref[...] * (z * _sigmoid(z))).astype(BF16)

    return pl.pallas_call(
        body, name="gdn_out", grid=(HEADS,),
        in_specs=[pl.BlockSpec((n, DH), lambda h: (0, h)), pl.BlockSpec((n, DH), lambda h: (0, ZB + h)),
                  pl.BlockSpec((1, DH), lambda h: (0, 0))],
        out_specs=pl.BlockSpec((n, DH), lambda h: (0, h)),
        out_shape=jax.ShapeDtypeStruct((n, 2 * GW), BF16),
        compiler_params=_params(("parallel",)),
    )(o, proj, wg)


def _gdn_out_bwd(o, proj, wg, dout_b, w_out):
    n = o.shape[0]
    d_model = dout_b.shape[1]

    def body(o_ref, z_ref, w_ref, g_ref, wo_ref, do_ref, dz_ref, gw_ref):
        ov, z, w = o_ref[...], z_ref[...], w_ref[...]
        d = _dot(g_ref[...], wo_ref[...], NT)
        r = lax.rsqrt(jnp.mean(ov * ov, axis=-1, keepdims=True) + EPS)
        nrm = ov * r
        s = _sigmoid(z)
        dz_ref[...] = (d * (nrm * w) * _dsilu(z, s)).astype(BF16)
        dn_w = d * (z * s)
        gw = jnp.sum(dn_w * nrm, axis=0, keepdims=True)
        dn = dn_w * w
        do_ref[...] = (r * (dn - nrm * jnp.mean(dn * nrm, axis=-1, keepdims=True))).astype(BF16)

        @pl.when(pl.program_id(0) == 0)
        def _():
            gw_ref[...] = jnp.zeros_like(gw_ref)

        gw_ref[...] += gw

    return pl.pallas_call(
        body, name="gdn_out_bwd", grid=(HEADS,),
        in_specs=[pl.BlockSpec((n, DH), lambda h: (0, h)), pl.BlockSpec((n, DH), lambda h: (0, ZB + h)),
                  pl.BlockSpec((1, DH), lambda h: (0, 0)), pl.BlockSpec((n, d_model), lambda h: (0, 0)),
                  pl.BlockSpec((DH, d_model), lambda h: (h, 0))],
        out_specs=[pl.BlockSpec((n, DH), lambda h: (0, h)), pl.BlockSpec((n, DH), lambda h: (0, ZB + h)),
                   pl.BlockSpec((1, DH), lambda h: (0, 0))],
        out_shape=[jax.ShapeDtypeStruct((n, GW), BF16), jax.ShapeDtypeStruct((n, GW_COLS), BF16),
                   jax.ShapeDtypeStruct((1, DH), F32)],
        compiler_params=_params(("arbitrary",), 40 * 2**20),
    )(o, proj, wg, dout_b, w_out)


def _conv_branch(proj, w3, b, mix):
    n = proj.shape[0]

    def body(p4, w_ref, b_ref, _, y_ref):
        u = p4[:, DH:2 * DH] * p4[:, 2 * DH:3 * DH]
        cc = _conv_silu(u, w_ref, 3) + b_ref[...]
        z = p4[:, 3 * DH:4 * DH]
        y_ref[...] = (p4[:, 0:DH] * cc * (z * _sigmoid(z))).astype(BF16)

    return pl.pallas_call(
        body, name="conv_branch", grid=(HEADS,),
        in_specs=[pl.BlockSpec((n, 4 * DH), lambda h: (0, h)), pl.BlockSpec((3, DH), lambda h: (0, h)),
                  pl.BlockSpec((1, DH), lambda h: (0, h)), ANY],
        out_specs=pl.BlockSpec((n, DH), lambda h: (0, HEADS + h)),
        out_shape=jax.ShapeDtypeStruct(mix.shape, BF16),
        input_output_aliases={3: 0},
        compiler_params=_params(("parallel",), 40 * 2**20),
    )(*_in_hbm(proj, w3, b, mix))


def _conv_branch_bwd(proj, w3, b, dout_b, w_out):
    n = proj.shape[0]
    d_model = dout_b.shape[1]

    def body(p4, w_ref, b_ref, g_ref, wo_ref, o4, gw_ref, gbias_ref):
        gb, gcv, hc, z = p4[:, 0:DH], p4[:, DH:2 * DH], p4[:, 2 * DH:3 * DH], p4[:, 3 * DH:4 * DH]
        d = _dot(g_ref[...], wo_ref[...], NT)
        dgb, dgc, dhc, dzc = (o4.at[:, kk * DH:(kk + 1) * DH] for kk in range(4))
        u = gcv * hc
        cc = _conv_silu(u, w_ref, 3) + b_ref[...]
        s = _sigmoid(z)
        dzc[...] = (d * (gb * cc) * _dsilu(z, s)).astype(BF16)
        dp = d * (z * s)
        dgb[...] = (dp * cc).astype(BF16)
        dcc = dp * gb
        gbias_ref[...] = jnp.sum(dcc, axis=0, keepdims=True)
        du = None
        for j in range(3):
            gw_ref[j:j + 1, :] = jnp.sum(dcc * _shift_down(u, 2 - j), axis=0, keepdims=True)
            t = _shift_up(dcc, 2 - j) * w_ref[j:j + 1, :]
            du = t if du is None else du + t
        dgc[...] = (du * hc).astype(BF16)
        dhc[...] = (du * gcv).astype(BF16)

    p4spec = pl.BlockSpec((n, 4 * DH), lambda h: (0, h))
    return pl.pallas_call(
        body, name="conv_branch_bwd", grid=(HEADS,),
        in_specs=[p4spec, pl.BlockSpec((3, DH), lambda h: (0, h)), pl.BlockSpec((1, DH), lambda h: (0, h)),
                  pl.BlockSpec((n, d_model), lambda h: (0, 0)), pl.BlockSpec((DH, d_model), lambda h: (HEADS + h, 0))],
        out_specs=[p4spec, pl.BlockSpec((3, DH), lambda h: (0, h)), pl.BlockSpec((1, DH), lambda h: (0, h))],
        out_shape=[jax.ShapeDtypeStruct((n, CW_COLS), BF16), jax.ShapeDtypeStruct((3, GW), F32),
                   jax.ShapeDtypeStruct((1, GW), F32)],
        compiler_params=_params(("parallel",), 52 * 2**20),
    )(proj, w3, b, dout_b, w_out)


def _out_loss(mix, w_out, x, tgt, wf):
    n, d = x.shape
    kdim = mix.shape[1]
    tr = min(256, n)

    def body(m_ref, wo_ref, x_ref, t_ref, w_ref, do_ref, dob_ref, gw_ref, loss_ref):
        ov = _dot(m_ref[...], wo_ref[...], NN) + x_ref[...]
        w = w_ref[...]
        r = lax.rsqrt(jnp.mean(ov * ov, axis=-1, keepdims=True) + EPS)
        nrm = ov * r
        e = nrm * w - t_ref[...]
        dy = e * (1.0 / d)
        dn = dy * w
        dout = r * (dn - nrm * jnp.mean(dn * nrm, axis=-1, keepdims=True))
        do_ref[...] = dout
        dob_ref[...] = dout.astype(BF16)

        @pl.when(pl.program_id(0) == 0)
        def _():
            gw_ref[...] = jnp.zeros_like(gw_ref)
            loss_ref[...] = jnp.zeros_like(loss_ref)

        gw_ref[...] += jnp.sum(dy * nrm, axis=0, keepdims=True)
        loss_ref[...] += (0.5 / d) * jnp.sum(jnp.sum(e * e, axis=-1, keepdims=True), axis=0, keepdims=True)

    row = pl.BlockSpec((tr, d), lambda i: (i, 0))
    return pl.pallas_call(
        body, name="out_loss", grid=(n // tr,),
        in_specs=[pl.BlockSpec((tr, kdim), lambda i: (i, 0)), pl.BlockSpec((kdim, d), lambda i: (0, 0)), row, row,
                  pl.BlockSpec((1, d), lambda i: (0, 0))],
        out_specs=[row, row, pl.BlockSpec((1, d), lambda i: (0, 0)), pl.BlockSpec((1, 1), lambda i: (0, 0))],
        out_shape=[jax.ShapeDtypeStruct((n, d), F32), jax.ShapeDtypeStruct((n, d), BF16),
                   jax.ShapeDtypeStruct((1, d), F32), jax.ShapeDtypeStruct((1, 1), F32)],
        compiler_params=_params(("arbitrary",), 40 * 2**20),
    )(mix, w_out, x, tgt, wf)


def _dh_rms_bwd(dproj, w_t, dh0, x, w, dout, tk):
    n, d = x.shape
    kdim = dproj.shape[1]
    tm = min(1024, n)
    tk = min(tk, kdim)
    nk = kdim // tk

    def body(a_ref, b_ref, dh0_ref, x_ref, w_ref, do_ref, dx_ref, gw_ref, acc):
        i, kk = pl.program_id(0), pl.program_id(1)
        part = _dot(a_ref[...], b_ref[...], NN)

        @pl.when(kk == 0)
        def _():
            acc[...] = part + dh0_ref[...]

        @pl.when(kk > 0)
        def _():
            acc[...] += part

        @pl.when((i == 0) & (kk == 0))
        def _():
            gw_ref[...] = jnp.zeros_like(gw_ref)

        @pl.when(kk == nk - 1)
        def _():
            xv, dhv = x_ref[...], acc[...]
            r = lax.rsqrt(jnp.mean(xv * xv, axis=-1, keepdims=True) + EPS)
            xn = xv * r
            dxn = dhv * w_ref[...]
            dx_ref[...] = r * (dxn - xn * jnp.mean(dxn * xn, axis=-1, keepdims=True)) + do_ref[...]
            gw_ref[...] += jnp.sum(dhv * xn, axis=0, keepdims=True)

    row = pl.BlockSpec((tm, d), lambda i, kk: (i, 0))
    one = pl.BlockSpec((1, d), lambda i, kk: (0, 0))
    return pl.pallas_call(
        body, name="dh_rms_bwd", grid=(n // tm, nk),
        in_specs=[pl.BlockSpec((tm, tk), lambda i, kk: (i, kk)), pl.BlockSpec((tk, d), lambda i, kk: (kk, 0)),
                  row, row, one, row],
        out_specs=[row, one],
        out_shape=[jax.ShapeDtypeStruct((n, d), F32), jax.ShapeDtypeStruct((1, d), F32)],
        scratch_shapes=[pltpu.VMEM((tm, d), F32)],
        compiler_params=_params(("arbitrary", "arbitrary"), 56 * 2**20),
    )(dproj, w_t, dh0, x, w, dout)


def _ij():
    i = lax.broadcasted_iota(jnp.int32, (CH, CH), 0)
    j = lax.broadcasted_iota(jnp.int32, (CH, CH), 1)
    return i, j


def _unit_lower_inverse(mats):
    i, j = _ij()
    eye = jnp.where(i == j, 1.0, 0.0)
    same16 = (i // 16) == (j // 16)
    same32 = (i // 32) == (j // 32)
    mm = lambda xs, ys: [_dot(x, y, NN, P_INV) for x, y in zip(xs, ys)]
    n1 = [jnp.where(same16, -a, 0.0) for a in mats]
    n2 = mm(n1, n1)
    n4 = mm(n2, n2)
    n8 = mm(n4, n4)
    t = [eye + x1 + x2 + x3 for x1, x2, x3 in zip(n1, n2, mm(n1, n2))]
    t = [x + y for x, y in zip(t, mm(t, n4))]
    t = [x + y for x, y in zip(t, mm(t, n8))]
    a1 = [jnp.where(same32 & jnp.logical_not(same16), a, 0.0) for a in mats]
    t = [x - y for x, y in zip(t, mm(t, mm(a1, t)))]
    a2 = [jnp.where(same32, 0.0, a) for a in mats]
    t = [x - y for x, y in zip(t, mm(t, mm(a2, t)))]
    return t


def _head_vectors(bg, bgt, h):
    bcol = bg[:, h:h + 1]
    gcol = bg[:, HEADS + h:HEADS + h + 1]
    grow = bgt[HEADS + h:HEADS + h + 1, :]
    return bcol, gcol, grow


def _decay(gcol, grow):
    i, j = _ij()
    return jnp.where(i >= j, jnp.exp(jnp.where(i >= j, gcol - grow, 0.0)), 0.0)


def _gdn_intra(q, k, v, bg, bgt):
    n = q.shape[0]
    nch = n // CH
    cps = 4 if nch % 4 == 0 else 1

    def body(q_ref, k_ref, v_ref, bg_ref, bgt_ref, u_ref, w_ref, p_ref, t_ref):
        i, j = _ij()
        items = [(ci, h) for ci in range(cps) for h in range(HEADS)]
        at = lambda ref, ci, h: ref.at[ci * CH:(ci + 1) * CH, h * DH:(h + 1) * DH]
        bgs = [bg_ref[ci * CH:(ci + 1) * CH, :] for ci in range(cps)]
        ks = [at(k_ref, ci, h)[...] for ci, h in items]
        vecs = [_head_vectors(bgs[ci], bgt_ref[ci], h) for ci, h in items]
        decs = [_decay(gcol, grow) for _, gcol, grow in vecs]
        kks = [_dot(kh, kh, NT, P_GRAM) for kh in ks]
        qks = [_dot(at(q_ref, ci, h)[...], kh, NT, P_GRAM) for (ci, h), kh in zip(items, ks)]
        ts = _unit_lower_inverse([jnp.where(i > j, bcol * kk * dec, 0.0)
                                  for (bcol, _, _), kk, dec in zip(vecs, kks, decs)])
        us = [_dot(t, at(v_ref, ci, h)[...] * bcol, NN, P_SOL) for t, (ci, h), (bcol, _, _) in zip(ts, items, vecs)]
        ws = [_dot(t, kh * (bcol * jnp.exp(gcol)), NN, P_SOL) for t, kh, (bcol, gcol, _) in zip(ts, ks, vecs)]
        for n_, (ci, h) in enumerate(items):
            p_ref[ci, h] = qks[n_] * decs[n_]
            t_ref[ci, h] = ts[n_].astype(BF16)
            at(u_ref, ci, h)[...] = us[n_]
            at(w_ref, ci, h)[...] = ws[n_].astype(BF16)

    row = pl.BlockSpec((cps * CH, GW), lambda c: (c, 0))
    sq = pl.BlockSpec((cps, HEADS, CH, CH), lambda c: (c, 0, 0, 0))
    big = jax.ShapeDtypeStruct((n, GW), F32)
    sqs = jax.ShapeDtypeStruct((nch, HEADS, CH, CH), F32)
    return pl.pallas_call(
        body, name="gdn_intra", grid=(nch // cps,),
        in_specs=[row, row, row, pl.BlockSpec((cps * CH, DH), lambda c: (c, 0)),
                  pl.BlockSpec((cps, DH, CH), lambda c: (c, 0, 0))],
        out_specs=[row, row, sq, sq],
        out_shape=[big, jax.ShapeDtypeStruct((n, GW), BF16), sqs, jax.ShapeDtypeStruct(sqs.shape, BF16)],
        compiler_params=_params(("parallel",)),
    )(q, k, v, bg, bgt)


def _gdn_scan(q, k, bg, u, w, p):
    n = q.shape[0]
    nch = n // CH
    cps = SCAN_CPS if nch % SCAN_CPS == 0 else 1

    def step(q_ref, k_ref, bg_ref, u_ref, w_ref, p_ref, o_ref, vn_ref, s_out, s_scr):
        hs = range(HEADS)
        sls = [slice(h * DH, (h + 1) * DH) for h in hs]
        ss = [s_scr[h] for h in hs]
        for ci in range(cps):
            rs = slice(ci * CH, (ci + 1) * CH)
            bg = bg_ref[rs, :]
            gcols = [bg[:, HEADS + h:HEADS + h + 1] for h in hs]
            glasts = [g[CH - 1:CH, :] for g in gcols]
            wss = [_dot(w_ref[rs, sl], s, NN, P_SCAN) for sl, s in zip(sls, ss)]
            oqs = [_dot(q_ref[rs, sl] * jnp.exp(g), s, NN, P_SCAN) for sl, s, g in zip(sls, ss, gcols)]
            vns = [u_ref[rs, sl] - x for sl, x in zip(sls, wss)]
            ops = [_dot(p_ref[ci, h], vn, NN, P_SCAN) for h, vn in zip(hs, vns)]
            sns = [_dot(k_ref[rs, sl] * jnp.exp(gl - g), vn, TN, P_SCAN)
                   for sl, gl, g, vn in zip(sls, glasts, gcols, vns)]
            for h, sl in enumerate(sls):
                s_out[ci, :, sl] = ss[h].astype(BF16)
                vn_ref[rs, sl] = vns[h].astype(BF16)
                o_ref[rs, sl] = oqs[h] + ops[h]
            ss = [s * jnp.exp(gl) + sn for s, gl, sn in zip(ss, glasts, sns)]
        for h in hs:
            s_scr[h] = ss[h]

    row = pl.BlockSpec((cps * CH, GW), lambda c: (c, 0))
    deep = pl.BlockSpec((cps * CH, GW), lambda c: (c, 0), pipeline_mode=pl.Buffered(3))
    in_specs = [deep, deep, pl.BlockSpec((cps * CH, DH), lambda c: (c, 0)), deep, deep,
                pl.BlockSpec((cps, HEADS, CH, CH), lambda c: (c, 0, 0, 0))]
    out_specs = [row, row, pl.BlockSpec((cps, DH, GW), lambda c: (c, 0, 0))]

    def body(*refs):
        s_scr = refs[-1]
        s_scr[...] = jnp.zeros_like(s_scr)
        pltpu.emit_pipeline(lambda *blocks: step(*blocks, s_scr), grid=(nch // cps,),
                            in_specs=in_specs, out_specs=out_specs)(*refs[:-1])

    big = jax.ShapeDtypeStruct((n, GW), F32)
    return pl.pallas_call(
        body, name="gdn_scan", in_specs=[ANY] * 6, out_specs=[ANY] * 3,
        out_shape=[big, jax.ShapeDtypeStruct((n, GW), BF16), jax.ShapeDtypeStruct((nch, DH, GW), BF16)],
        scratch_shapes=[pltpu.VMEM((HEADS, DH, DH), F32)],
        compiler_params=_params(vmem=48 * 2**20),
    )(q, k, bg, u, w, p)


def _gdn_scan_bwd(q, k, bg, w, p, vn, s_in, do):
    n = q.shape[0]
    nch = n // CH
    cps = SCAN_CPS if nch % SCAN_CPS == 0 else 1
    rev = lambda c: nch // cps - 1 - c

    def step(q_ref, k_ref, bg_ref, w_ref, p_ref, vn_ref, s_ref, do_ref,
             dqg_ref, dp_ref, du_ref, dw_ref, dks_ref, dgam_ref, ds_scr):
        lane = _lane((1, DH))
        hs = range(HEADS)
        sls = [slice(h * DH, (h + 1) * DH) for h in hs]
        dss = [ds_scr[h] for h in hs]
        for ci in reversed(range(cps)):
            rs = slice(ci * CH, (ci + 1) * CH)
            bg = bg_ref[rs, :]
            gcols = [bg[:, HEADS + h:HEADS + h + 1] for h in hs]
            glasts = [g[CH - 1:CH, :] for g in gcols]
            ss = [s_ref[ci, :, sl] for sl in sls]
            dos = [do_ref[rs, sl] for sl in sls]
            vnl = [vn_ref[rs, sl] for sl in sls]
            dqgs = [_dot(d, s, NT, P_SCANB) for d, s in zip(dos, ss)]
            dps = [_dot(d, vn, NT, P_SCANB) for d, vn in zip(dos, vnl)]
            dvn1 = [_dot(p_ref[ci, h], d, TN, P_SCANB) for h, d in zip(hs, dos)]
            dvn2 = [_dot(k_ref[rs, sl] * jnp.exp(gl - g), ds, NN, P_SCANB)
                    for sl, gl, g, ds in zip(sls, glasts, gcols, dss)]
            dkss = [_dot(vn, ds, NT, P_SCANB) for vn, ds in zip(vnl, dss)]
            dsq = [_dot(q_ref[rs, sl] * jnp.exp(g), d, TN, P_SCANB) for sl, g, d in zip(sls, gcols, dos)]
            dvns = [a + b for a, b in zip(dvn1, dvn2)]
            dws = [_dot(dvn, s, NT, P_SCANB) for dvn, s in zip(dvns, ss)]
            dsw = [_dot(w_ref[rs, sl], dvn, TN, P_SCANB) for sl, dvn in zip(sls, dvns)]
            dgam = jnp.zeros((1, DH), F32)
            for h, sl in enumerate(sls):
                dqg_ref[rs, sl] = dqgs[h]
                dp_ref[ci, h] = dps[h]
                du_ref[rs, sl] = dvns[h].astype(BF16)
                dw_ref[rs, sl] = (-dws[h]).astype(BF16)
                dks_ref[rs, sl] = dkss[h]
                tot = jnp.sum(jnp.sum(dss[h] * ss[h], axis=-1, keepdims=True), axis=0, keepdims=True)
                dgam = dgam + jnp.where(lane == h, tot, 0.0)
            dgam_ref[ci] = jnp.broadcast_to(dgam, (8, DH))
            dss = [ds * jnp.exp(gl) + a - b for ds, gl, a, b in zip(dss, glasts, dsq, dsw)]
        for h in hs:
            ds_scr[h] = dss[h]

    row = pl.BlockSpec((cps * CH, GW), lambda c: (rev(c), 0))
    deep = pl.BlockSpec((cps * CH, GW), lambda c: (rev(c), 0), pipeline_mode=pl.Buffered(3))
    sq = pl.BlockSpec((cps, HEADS, CH, CH), lambda c: (rev(c), 0, 0, 0))
    in_specs = [deep, deep, pl.BlockSpec((cps * CH, DH), lambda c: (rev(c), 0)), deep, sq, deep,
                pl.BlockSpec((cps, DH, GW), lambda c: (rev(c), 0, 0)), deep]
    out_specs = [row, sq, row, row, row, pl.BlockSpec((cps, 8, DH), lambda c: (rev(c), 0, 0))]

    def body(*refs):
        ds_scr = refs[-1]
        ds_scr[...] = jnp.zeros_like(ds_scr)
        pltpu.emit_pipeline(lambda *blocks: step(*blocks, ds_scr), grid=(nch // cps,),
                            in_specs=in_specs, out_specs=out_specs)(*refs[:-1])

    big = jax.ShapeDtypeStruct((n, GW), F32)
    return pl.pallas_call(
        body, name="gdn_scan_bwd", in_specs=[ANY] * 8, out_specs=[ANY] * 6,
        out_shape=[big, jax.ShapeDtypeStruct((nch, HEADS, CH, CH), F32), jax.ShapeDtypeStruct((n, GW), BF16),
                   jax.ShapeDtypeStruct((n, GW), BF16), big,
                   jax.ShapeDtypeStruct((nch, 8, DH), F32)],
        scratch_shapes=[pltpu.VMEM((HEADS, DH, DH), F32)],
        compiler_params=_params(vmem=56 * 2**20),
    )(q, k, bg, w, p, vn, s_in, do)


def _gdn_intra_bwd(q, k, v, bg, bgt, t, u, w, p, dqg, dp, du, dw, dks, dgam):
    n = q.shape[0]
    nch = n // CH
    cps = 2 if nch % 2 == 0 else 1

    def body(q_ref, k_ref, v_ref, bg_ref, bgt_ref, t_ref, u_ref, w_ref, p_ref,
             dqg_ref, dp_ref, du_ref, dw_ref, dks_ref, dgam_ref, dq_ref, dk_ref, dv_ref, dbg_ref):
        i, j = _ij()
        rows1 = lax.broadcasted_iota(jnp.int32, (CH, 1), 0)
        lane = _lane((CH, DH))
        rsum = lambda x: jnp.sum(x, axis=-1, keepdims=True)
        items = [(ci, h) for ci in range(cps) for h in range(HEADS)]
        at = lambda ref, it: ref.at[it[0] * CH:(it[0] + 1) * CH, it[1] * DH:(it[1] + 1) * DH]
        ld = lambda ref: [at(ref, it)[...] for it in items]
        bgs = [bg_ref[ci * CH:(ci + 1) * CH, :] for ci in range(cps)]
        qs, ks = ld(q_ref), ld(k_ref)
        vecs = [_head_vectors(bgs[ci], bgt_ref[ci], h) for ci, h in items]
        decs = [_decay(gcol, grow) for _, gcol, grow in vecs]
        ths = [t_ref[ci, h] for ci, h in items]
        drus = [_dot(th, x_, TN, P_BWD) for th, x_ in zip(ths, ld(du_ref))]
        drws = [_dot(th, x_, TN, P_BWD) for th, x_ in zip(ths, ld(dw_ref))]
        kks = [_dot(kh, kh, NT, P_GRAM) for kh in ks]
        da1 = [_dot(dru, x_, NT, P_BWD) for dru, x_ in zip(drus, ld(u_ref))]
        da2 = [_dot(drw, x_, NT, P_BWD) for drw, x_ in zip(drws, ld(w_ref))]
        das = [jnp.where(i > j, -(x_ + y_), 0.0) for x_, y_ in zip(da1, da2)]
        dkks = [da * bcol * dec for da, (bcol, _, _), dec in zip(das, vecs, decs)]
        dps = [dp_ref[ci, h] for ci, h in items]
        dqks = [dp_ * dec for dp_, dec in zip(dps, decs)]
        dq_ps = [_dot(dqk, kh, NN, P_BWD) for dqk, kh in zip(dqks, ks)]
        dk_ps = [_dot(dqk, qh, TN, P_BWD) for dqk, qh in zip(dqks, qs)]
        dk_as = [_dot(dkk, kh, NN, P_BWD) for dkk, kh in zip(dkks, ks)]
        dk_bs = [_dot(dkk, kh, TN, P_BWD) for dkk, kh in zip(dkks, ks)]
        bcols = [vc[0] for vc in vecs]
        gcols = [vc[1] for vc in vecs]
        gams = [jnp.exp(g) for g in gcols]
        glasts = [g[CH - 1:CH, :] for g in gcols]
        es = [jnp.exp(gl - g) for gl, g in zip(glasts, gcols)]
        kgs = [kh * gam for kh, gam in zip(ks, gams)]
        dqgs, dkss = ld(dqg_ref), ld(dks_ref)
        wks = [drw * kg for drw, kg in zip(drws, kgs)]
        kss = [dk_ * (kh * e) for dk_, kh, e in zip(dkss, ks, es)]
        r_beta = [rsum(dru * x_ + wk) for dru, x_, wk in zip(drus, ld(v_ref), wks)]
        r_ak = [rsum(da * kk * dec) for da, kk, dec in zip(das, kks, decs)]
        r_gc = [rsum(wk * bcol + dqg * (qh * gam) - ks_)
                for wk, bcol, dqg, qh, gam, ks_ in zip(wks, bcols, dqgs, qs, gams, kss)]
        tk_tot = [jnp.sum(jnp.sum(ks_, axis=0, keepdims=True), axis=-1, keepdims=True) for ks_ in kss]
        mdecs = [da * (bcol * kk * dec) + dp_ * p_ref[ci, h]
                 for (ci, h), da, bcol, kk, dec, dp_ in zip(items, das, bcols, kks, decs, dps)]
        r_md = [rsum(m) for m in mdecs]
        c_md = [rsum(jnp.where(i == j, jnp.sum(m, axis=0, keepdims=True), 0.0)) for m in mdecs]
        dbgs = [jnp.zeros((CH, DH), F32) for _ in range(cps)]
        for n_, (ci, h) in enumerate(items):
            at(dv_ref, (ci, h))[...] = bcols[n_] * drus[n_]
            at(dq_ref, (ci, h))[...] = gams[n_] * dqgs[n_] + dq_ps[n_]
            at(dk_ref, (ci, h))[...] = ((bcols[n_] * gams[n_]) * drws[n_] + dk_ps[n_] + dk_as[n_] + dk_bs[n_]
                                        + dkss[n_] * es[n_])
            dbeta = r_beta[n_] + r_ak[n_]
            dglast = tk_tot[n_] + dgam_ref[ci, 0:1, h:h + 1] * jnp.exp(glasts[n_])
            dgc = r_gc[n_] + r_md[n_] - c_md[n_] + jnp.where(rows1 == CH - 1, dglast, 0.0)
            dbgs[ci] = dbgs[ci] + jnp.where(lane == h, dbeta, 0.0) + jnp.where(lane == HEADS + h, dgc, 0.0)
        for ci in range(cps):
            dbg_ref[ci * CH:(ci + 1) * CH, :] = dbgs[ci]

    row = pl.BlockSpec((cps * CH, GW), lambda c: (c, 0))
    sq = pl.BlockSpec((cps, HEADS, CH, CH), lambda c: (c, 0, 0, 0))
    small = pl.BlockSpec((cps * CH, DH), lambda c: (c, 0))
    big = jax.ShapeDtypeStruct((n, GW), F32)
    return pl.pallas_call(
        body, name="gdn_intra_bwd", grid=(nch // cps,),
        in_specs=[row, row, row, small, pl.BlockSpec((cps, DH, CH), lambda c: (c, 0, 0)), sq, row, row, sq,
                  row, sq, row, row, row, pl.BlockSpec((cps, 8, DH), lambda c: (c, 0, 0))],
        out_specs=[row, row, row, small],
        out_shape=[big, big, big, jax.ShapeDtypeStruct((n, DH), F32)],
        compiler_params=_params(("parallel",)),
    )(q, k, v, bg, bgt, t, u, w, p, dqg, dp, du, dw, dks, dgam)


def _local_step(x, tgt, h, w_g, cqw, late, norm_in_w, ad, gdn_norm_w, conv_b, final_norm_w,
                on_grad_c=None, on_grad_g=None, on_q=None):
    proj_g = _matmul(h, w_g, NT, F32, 512, 1408, 1024, "mm_proj_g", n=GW_COLS, b_outer=True)
    q, k, v = _prep_qkv(proj_g, cqw)
    if on_q is not None:
        q = on_q(q)
    bg, bgt = _prep_bg(proj_g, ad)
    u, w, p, t = _gdn_intra(q, k, v, bg, bgt)
    o, vn, s_in = _gdn_scan(q, k, bg, u, w, p)
    w_c, w_out, conv_w = late(o)
    proj_c = _matmul(h, w_c, NT, F32, 512, 1024, 1024, "mm_proj_c", n=CW_COLS, b_outer=True)
    mix = _conv_branch(proj_c, conv_w, conv_b, _gdn_out(o, proj_g, gdn_norm_w))
    dout, dout_b, g_fn, loss = _out_loss(mix, w_out, x, tgt, final_norm_w)

    g_wout = _matmul(mix, dout_b, TN, BF16, 512, 512, 2048, "mm_gwout")
    do, dproj_g, g_gn = _gdn_out_bwd(o, proj_g, gdn_norm_w, dout_b, w_out)
    dproj_c, g_cw, g_cb = _conv_branch_bwd(proj_c, conv_w, conv_b, dout_b, w_out)
    g_c = _matmul(dproj_c, h, TN, BF16, 1024, 512, 2048, "mm_gwin_c")
    if on_grad_c is not None:
        do = on_grad_c(g_c, g_wout, do)
    dqg, dp, du, dw, dks, dgam = _gdn_scan_bwd(q, k, bg, w, p, vn, s_in, do)
    dq, dk, dv, dbg = _gdn_intra_bwd(q, k, v, bg, bgt, t, u, w, p, dqg, dp, du, dw, dks, dgam)
    dproj_g, gq, gk, gv = _prep_qkv_bwd(proj_g, cqw, dq, dk, dv, dproj_g)
    dproj_g, g_al, g_dt = _prep_bg_bwd(proj_g, ad, dbg, dproj_g)
    g_g = _matmul(dproj_g, h, TN, BF16, 1408, 512, 2048, "mm_gwin_g")
    if on_grad_g is not None:
        dproj_g = on_grad_g(g_g, dproj_g)
    dh = _matmul(dproj_g, w_g, NN, F32, 1024, 1024, 1408, "mm_dh_g")
    gx, g_nin = _dh_rms_bwd(dproj_c, w_c, dh, x, norm_in_w, dout, 1024)
    small = dict(nin=g_nin, cb=g_cb, fn=g_fn, al=g_al, dt=g_dt, gn=g_gn, cq=(gq, gk, gv), cw=g_cw, loss=loss)
    return gx, small, (g_g, g_c, g_wout)


def _place():
    x, y, c = lax.axis_index("x"), lax.axis_index("y"), lax.axis_index("c")
    chips = [(1 - x, y), (x, 1 - y), (1 - x, 1 - y)]
    return x, y, c, chips


def _blk(ref, b):
    if isinstance(b, int):
        return ref.at[b * DH:(b + 1) * DH, :]
    return ref.at[pl.ds(pl.multiple_of(b * DH, DH), DH), :]


HBM = pl.BlockSpec(memory_space=pltpu.HBM)
SEM = pl.BlockSpec(memory_space=pltpu.SEMAPHORE)
EFFECT = pltpu.SideEffectType.DATAFLOW_SIDE_EFFECTING


def _split_start(name, issue, bufs, n_sems):
    nbuf = len(bufs)

    def body(*refs):
        issue(refs[:nbuf], refs[nbuf], refs[nbuf + 1])
        refs[-1][...] = jnp.zeros_like(refs[-1])

    out = pl.pallas_call(
        body, name=name,
        out_shape=(pltpu.SemaphoreType.DMA((n_sems,)), pltpu.SemaphoreType.DMA((n_sems,)),
                   *[pltpu.HBM(b.shape, b.dtype) for b in bufs], jax.ShapeDtypeStruct((8, DH), F32)),
        in_specs=[HBM] * nbuf,
        out_specs=(SEM, SEM, *[HBM] * nbuf, pl.BlockSpec(memory_space=pltpu.VMEM)),
        input_output_aliases={a: 2 + a for a in range(nbuf)},
        compiler_params=pltpu.CompilerParams(has_side_effects=EFFECT),
    )(*[pltpu.with_memory_space_constraint(b, pltpu.HBM) for b in bufs])
    return out[0], out[1], list(out[2:2 + nbuf]), out[-1]


def _split_wait(name, await_, send_sems, recv_sems, bufs, after):
    nbuf = len(bufs)
    after = list(after) if isinstance(after, (list, tuple)) else [after]

    def body(*refs):
        await_(refs[:nbuf], refs[nbuf], refs[nbuf + 1])

    out = pl.pallas_call(
        body, name=name,
        out_shape=tuple(pltpu.HBM(b.shape, b.dtype) for b in bufs),
        in_specs=[HBM] * nbuf + [SEM, SEM] + [ANY] * len(after), out_specs=tuple([HBM] * nbuf),
        input_output_aliases={a: a for a in range(nbuf)},
        compiler_params=pltpu.CompilerParams(has_side_effects=EFFECT),
    )(*bufs, send_sems, recv_sems, *after)
    return list(out)


def _phase_blocks(chip, phase, edges, parity=None):
    return [(b, blk) for b, (grp, blk) in enumerate(_shard_blocks(chip, edges))
            if grp == phase and (parity is None or b % 2 == parity)]


def _cols(ref, nblk):
    return ref.at[0:nblk * DH, :]


def _block_table(chip, edges, spare_g, spare_c):
    rows = []
    for s in range(4):
        sb = _shard_blocks(s, edges)
        rows.append([[blk if grp == "g" else spare_g for grp, blk in sb],
                     [blk if grp == "c" else spare_c for grp, blk in sb],
                     [int(grp == "g") for grp, _ in sb], [s] * ALIGNED_BLOCKS])
    return jnp.asarray(rows, jnp.int32)[chip]


def _place_own(a_shard, wo, cq, cw, bufs):
    d = a_shard.shape[1]
    chip = 2 * lax.axis_index("x") + lax.axis_index("y")

    def body(t_ref, a_ref, wo_ref, cq_ref, cw_ref, *refs):
        wg_ref, wc_ref, wog_ref, cqg_ref, cwg_ref = refs[5:]
        wg_ref[...] = a_ref[...]
        wc_ref[...] = a_ref[...]

        @pl.when(pl.program_id(0) == 0)
        def _():
            wog_ref[0] = wo_ref[...]
            cqg_ref[0] = cq_ref[...]
            cwg_ref[0] = cw_ref[...]

    whole = lambda s: pl.BlockSpec(s.shape, lambda b, t: (0,) * s.ndim)
    slot = lambda s: pl.BlockSpec((1,) + s.shape, lambda b, t: (t[3, 0],) + (0,) * s.ndim)
    return pl.pallas_call(
        body, name="place_own",
        grid_spec=pltpu.PrefetchScalarGridSpec(
            num_scalar_prefetch=1, grid=(ALIGNED_BLOCKS,),
            in_specs=[pl.BlockSpec((DH, d), lambda b, t: (b, 0)), whole(wo), whole(cq), whole(cw)] + [ANY] * 5,
            out_specs=[pl.BlockSpec((DH, d), lambda b, t: (t[0, b], 0)),
                       pl.BlockSpec((DH, d), lambda b, t: (t[1, b], 0)), slot(wo), slot(cq), slot(cw)]),
        out_shape=[jax.ShapeDtypeStruct(b.shape, b.dtype) for b in bufs],
        input_output_aliases={5 + a: a for a in range(5)},
        compiler_params=_params(("arbitrary",)),
    )(_block_table(chip, True, G_SPARE, C_SPARE), a_shard, wo, cq, cw, *bufs)


def _tie(x, token, name):
    def body(x_ref, t_ref, o_ref):
        del x_ref, t_ref, o_ref

    return pl.pallas_call(
        body, name=name, in_specs=[ANY, ANY], out_specs=ANY,
        out_shape=jax.ShapeDtypeStruct(x.shape, x.dtype), input_output_aliases={0: 0},
    )(x, token)


def _gather_start(phase, a_shard, w_grp, singles):
    ns = len(singles)

    def issue(refs, send_sems, recv_sems):
        a_ref, w_ref = refs[0], refs[1]
        x, y, c, chips = _place()
        mine = 2 * x + y
        for jj, (px, py) in enumerate(chips):
            to = dict(device_id=(px, py, c), device_id_type=MESH)
            for a in range(ns):
                pltpu.make_async_remote_copy(
                    src_ref=refs[2 + 2 * a], dst_ref=refs[3 + 2 * a].at[mine],
                    send_sem=send_sems.at[(1 + ns) * jj + 1 + a], recv_sem=recv_sems.at[(1 + ns) * jj + 1 + a],
                    **to).start()
        for s in range(4):
            for par in range(2):
                blocks = _phase_blocks(s, phase, True, par)
                if blocks:
                    @pl.when((mine == s) & (c == par))
                    def _():
                        for b, blk in blocks:
                            for jj, (px, py) in enumerate(chips):
                                pltpu.make_async_remote_copy(
                                    src_ref=_blk(a_ref, b), dst_ref=_blk(w_ref, blk),
                                    send_sem=send_sems.at[(1 + ns) * jj], recv_sem=recv_sems.at[(1 + ns) * jj],
                                    device_id=(px, py, c), device_id_type=MESH).start()

    bufs = [a_shard, w_grp] + [t for pair in singles for t in pair]
    return _split_start("gather_start_" + phase, issue, bufs, 3 * (1 + ns))


def _gather_wait(phase, send_sems, recv_sems, bufs, after):
    ns = (len(bufs) - 2) // 2

    def await_(refs, send_sems, recv_sems):
        a_ref, w_ref = refs[0], refs[1]
        x, y, c, chips = _place()
        mine = 2 * x + y
        for jj, (px, py) in enumerate(chips):
            to = dict(device_id=(px, py, c), device_id_type=MESH)
            peer = 2 * px + py
            for a in range(ns):
                cp = pltpu.make_async_remote_copy(
                    src_ref=refs[2 + 2 * a], dst_ref=refs[3 + 2 * a].at[mine],
                    send_sem=send_sems.at[(1 + ns) * jj + 1 + a], recv_sem=recv_sems.at[(1 + ns) * jj + 1 + a], **to)
                cp.wait_recv()
                cp.wait_send()
            for s in range(4):
                for par in range(2):
                    nblk = len(_phase_blocks(s, phase, True, par))
                    if nblk:
                        both = pltpu.make_async_remote_copy(
                            src_ref=_cols(a_ref, nblk), dst_ref=_cols(w_ref, nblk),
                            send_sem=send_sems.at[(1 + ns) * jj], recv_sem=recv_sems.at[(1 + ns) * jj], **to)

                        @pl.when((peer == s) & (c == par))
                        def _():
                            both.wait_recv()

                        @pl.when((mine == s) & (c == par))
                        def _():
                            both.wait_send()

    return _split_wait("gather_wait_" + phase, await_, send_sems, recv_sems, bufs, after)


def _sibling_forward_parts(phase):
    def each(w_ref, send_sems, recv_sems, start):
        x, y, c, chips = _place()
        to = dict(device_id=(x, y, 1 - c), device_id_type=MESH)
        for jj, (px, py) in enumerate(chips):
            peer = 2 * px + py
            for s in range(4):
                for par in range(2):
                    mine_blocks = _phase_blocks(s, phase, True, par)
                    theirs = len(_phase_blocks(s, phase, True, 1 - par))
                    if not (mine_blocks or theirs):
                        continue

                    @pl.when((peer == s) & (c == par))
                    def _():
                        if start:
                            for _, blk in mine_blocks:
                                pltpu.make_async_remote_copy(
                                    src_ref=_blk(w_ref, blk), dst_ref=_blk(w_ref, blk),
                                    send_sem=send_sems.at[jj], recv_sem=recv_sems.at[jj], **to).start()
                            return
                        if theirs:
                            pltpu.make_async_remote_copy(
                                src_ref=_cols(w_ref, theirs), dst_ref=_cols(w_ref, theirs),
                                send_sem=send_sems.at[jj], recv_sem=recv_sems.at[jj], **to).wait_recv()
                        if mine_blocks:
                            pltpu.make_async_remote_copy(
                                src_ref=_cols(w_ref, len(mine_blocks)), dst_ref=_cols(w_ref, len(mine_blocks)),
                                send_sem=send_sems.at[jj], recv_sem=recv_sems.at[jj], **to).wait_send()

    issue = lambda refs, send_sems, recv_sems: each(refs[0], send_sems, recv_sems, True)
    await_ = lambda refs, send_sems, recv_sems: each(refs[0], send_sems, recv_sems, False)
    return issue, await_


def _sibling_forward(phase, w_grp):
    issue, await_ = _sibling_forward_parts(phase)

    def body(w_in_ref, w_ref, send_sems, recv_sems):
        del w_in_ref
        issue([w_ref], send_sems, recv_sems)
        await_([w_ref], send_sems, recv_sems)

    return pl.pallas_call(
        body, name="sibling_forward_" + phase, in_specs=[ANY], out_specs=ANY,
        out_shape=jax.ShapeDtypeStruct(w_grp.shape, w_grp.dtype), input_output_aliases={0: 0},
        scratch_shapes=[pltpu.SemaphoreType.DMA((3,)), pltpu.SemaphoreType.DMA((3,))],
    )(w_grp)


def _merge_edges(w, edge0, mixed, name):
    d = w.shape[1]

    def body(e_ref, o_ref):
        o_ref[...] = e_ref[0:DH, :] + e_ref[DH:2 * DH, :]

    def to_block(i):
        r = mixed[-1]
        for kk in range(len(mixed) - 2, -1, -1):
            r = jnp.where(i == kk, mixed[kk], r)
        return r

    return pl.pallas_call(
        body, name=name, grid=(len(mixed),),
        in_specs=[pl.BlockSpec((2 * DH, d), lambda i: (edge0 // 2 + i, 0))],
        out_specs=pl.BlockSpec((DH, d), lambda i: (to_block(i), 0)),
        out_shape=jax.ShapeDtypeStruct(w.shape, w.dtype),
        input_output_aliases={0: 0},
        compiler_params=_params(("arbitrary",)),
    )(w)


def _scatter_start(phase, g_grp, land, singles, halved=False):
    ns = len(singles)

    def issue(refs, send_sems, recv_sems):
        g_ref, land_ref = refs[0], refs[1]
        x, y, c, chips = _place()
        for jj, (px, py) in enumerate(chips):
            to = dict(device_id=(px, py, c), device_id_type=MESH)
            peer = 2 * px + py
            for a in range(ns):
                pltpu.make_async_remote_copy(
                    src_ref=refs[2 + 2 * a].at[peer], dst_ref=refs[3 + 2 * a].at[jj],
                    send_sem=send_sems.at[(1 + ns) * jj + 1 + a], recv_sem=recv_sems.at[(1 + ns) * jj + 1 + a],
                    **to).start()
            for s in range(4):
                for par in ((0, 1) if halved else (None,)):
                    blocks = _phase_blocks(s, phase, False, par)
                    if blocks:
                        @pl.when((peer == s) if par is None else ((peer == s) & (c == par)))
                        def _():
                            for b, blk in blocks:
                                pltpu.make_async_remote_copy(
                                    src_ref=_blk(g_ref, blk), dst_ref=_blk(land_ref.at[jj], b),
                                    send_sem=send_sems.at[(1 + ns) * jj], recv_sem=recv_sems.at[(1 + ns) * jj],
                                    **to).start()

    bufs = [g_grp, land] + [t for pair in singles for t in pair]
    return _split_start("scatter_start_" + phase, issue, bufs, 3 * (1 + ns))


def _scatter_wait(phase, send_sems, recv_sems, bufs, after, halved=False):
    ns = (len(bufs) - 2) // 2

    def await_(refs, send_sems, recv_sems):
        g_ref, land_ref = refs[0], refs[1]
        x, y, c, chips = _place()
        mine = 2 * x + y
        for jj, (px, py) in enumerate(chips):
            to = dict(device_id=(px, py, c), device_id_type=MESH)
            peer = 2 * px + py
            for a in range(ns):
                cp = pltpu.make_async_remote_copy(
                    src_ref=refs[2 + 2 * a].at[peer], dst_ref=refs[3 + 2 * a].at[jj],
                    send_sem=send_sems.at[(1 + ns) * jj + 1 + a], recv_sem=recv_sems.at[(1 + ns) * jj + 1 + a], **to)
                cp.wait_recv()
                cp.wait_send()
            for s in range(4):
                for par in ((0, 1) if halved else (None,)):
                    nblk = len(_phase_blocks(s, phase, False, par))
                    if nblk:
                        both = pltpu.make_async_remote_copy(
                            src_ref=_cols(g_ref, nblk), dst_ref=_cols(land_ref.at[jj], nblk),
                            send_sem=send_sems.at[(1 + ns) * jj], recv_sem=recv_sems.at[(1 + ns) * jj], **to)

                        @pl.when((mine == s) if par is None else ((mine == s) & (c == par)))
                        def _():
                            both.wait_recv()

                        @pl.when((peer == s) if par is None else ((peer == s) & (c == par)))
                        def _():
                            both.wait_send()

    return _split_wait("scatter_wait_" + phase, await_, send_sems, recv_sems, bufs, after)


def _needed_blocks(phase, parity):
    return sorted({blk for s in range(4) for _, blk in _phase_blocks(s, phase, False, parity)})


def _pair_reduce(phase, g_grp):
    n, d = g_grp.shape

    def swap(g_ref, sib_ref, send_sem, recv_sem):
        x, y, c, _ = _place()
        to = dict(device_id=(x, y, 1 - c), device_id_type=MESH)
        for par in range(2):
            give, get = _needed_blocks(phase, 1 - par), _needed_blocks(phase, par)

            @pl.when(c == par)
            def _():
                for blk in give:
                    pltpu.make_async_remote_copy(src_ref=_blk(g_ref, blk), dst_ref=_blk(sib_ref, blk),
                                                 send_sem=send_sem, recv_sem=recv_sem, **to).start()
                pltpu.make_async_remote_copy(src_ref=_cols(g_ref, len(get)), dst_ref=_cols(sib_ref, len(get)),
                                             send_sem=send_sem, recv_sem=recv_sem, **to).wait_recv()
                pltpu.make_async_remote_copy(src_ref=_cols(g_ref, len(give)), dst_ref=_cols(sib_ref, len(give)),
                                             send_sem=send_sem, recv_sem=recv_sem, **to).wait_send()

    sib = pl.pallas_call(
        swap, name="pair_swap_" + phase, in_specs=[ANY], out_specs=ANY,
        out_shape=jax.ShapeDtypeStruct((n, d), g_grp.dtype),
        scratch_shapes=[pltpu.SemaphoreType.DMA, pltpu.SemaphoreType.DMA],
    )(*_in_hbm(g_grp))

    lists = [_needed_blocks(phase, par) for par in range(2)]
    longest = max(len(t) for t in lists)
    table = jnp.asarray([t + [t[-1]] * (longest - len(t)) for t in lists], jnp.int32)[lax.axis_index("c")]

    def add(t_ref, a_ref, b_ref, o_ref):
        o_ref[...] = (a_ref[...].astype(F32) + b_ref[...].astype(F32)).astype(o_ref.dtype)

    blk = pl.BlockSpec((DH, d), lambda i, t: (t[i], 0))
    return pl.pallas_call(
        add, name="pair_add_" + phase,
        grid_spec=pltpu.PrefetchScalarGridSpec(num_scalar_prefetch=1, grid=(longest,),
                                               in_specs=[blk, blk], out_specs=blk),
        out_shape=jax.ShapeDtypeStruct((n, d), g_grp.dtype),
        compiler_params=_params(("arbitrary",)),
    )(table, g_grp, sib)


def _sum_shard(g_g, g_c, land):
    d = g_g.shape[1]
    chip = 2 * lax.axis_index("x") + lax.axis_index("y")

    def body(t_ref, gg_ref, gc_ref, land_ref, o_ref):
        b = pl.program_id(0)
        in_g = t_ref[2, b] == 1
        own = jnp.where(in_g, gg_ref[...].astype(F32), gc_ref[...].astype(F32))
        for jj in range(3):
            own = own + land_ref[jj].astype(F32)
        o_ref[...] = jnp.where(in_g & (b % 2 != lax.axis_index("c")), 0.0, own)

    return pl.pallas_call(
        body, name="sum_w_in",
        grid_spec=pltpu.PrefetchScalarGridSpec(
            num_scalar_prefetch=1, grid=(ALIGNED_BLOCKS,),
            in_specs=[pl.BlockSpec((DH, d), lambda b, t: (t[0, b], 0)), pl.BlockSpec((DH, d), lambda b, t: (t[1, b], 0)),
                      pl.BlockSpec((3, DH, d), lambda b, t: (0, b, 0))],
            out_specs=pl.BlockSpec((DH, d), lambda b, t: (b, 0))),
        out_shape=jax.ShapeDtypeStruct((ALIGNED_W, d), F32),
        compiler_params=_params(("arbitrary",)),
    )(_block_table(chip, False, 0, 0), g_g, g_c, land)


def _sum_rows(stack, land, rows):
    _, r, d = stack.shape
    rows = min(rows, r)
    chip = 2 * lax.axis_index("x") + lax.axis_index("y")

    def body(t_ref, own_ref, land_ref, o_ref):
        acc = own_ref[0].astype(F32)
        for jj in range(3):
            acc = acc + land_ref[jj].astype(F32)
        o_ref[...] = acc

    return pl.pallas_call(
        body, name="sum_w_out",
        grid_spec=pltpu.PrefetchScalarGridSpec(
            num_scalar_prefetch=1, grid=(r // rows,),
            in_specs=[pl.BlockSpec((1, rows, d), lambda i, t: (t[0], i, 0)),
                      pl.BlockSpec((3, rows, d), lambda i, t: (0, i, 0))],
            out_specs=pl.BlockSpec((rows, d), lambda i, t: (i, 0))),
        out_shape=jax.ShapeDtypeStruct((r, d), F32),
        compiler_params=_params(("arbitrary",)),
    )(jnp.reshape(chip, (1,)).astype(jnp.int32), stack, land)


def _exchange_parts(n_swap, with_pack):
    def copies(refs, send_sems, recv_sems):
        x, y, c, _ = _place()
        me = 4 * x + 2 * y + c
        cps = [pltpu.make_async_remote_copy(
            src_ref=refs[2 * a], dst_ref=refs[2 * a + 1], send_sem=send_sems.at[a], recv_sem=recv_sems.at[a],
            device_id=(x, y, 1 - c), device_id_type=MESH) for a in range(n_swap)]
        if with_pack:
            pack_ref, packs = refs[2 * n_swap], refs[2 * n_swap + 1]
            for r in range(1, 8):
                dx, dy, dc = (r >> 2) & 1, (r >> 1) & 1, r & 1
                peer = (x + dx - 2 * x * dx, y + dy - 2 * y * dy, c + dc - 2 * c * dc)
                cps.append(pltpu.make_async_remote_copy(
                    src_ref=pack_ref, dst_ref=packs.at[me], send_sem=send_sems.at[n_swap + r - 1],
                    recv_sem=recv_sems.at[n_swap + r - 1], device_id=peer, device_id_type=MESH))
        return cps

    def issue(refs, send_sems, recv_sems):
        for cp in copies(refs, send_sems, recv_sems):
            cp.start()

    def await_(refs, send_sems, recv_sems):
        cps = copies(refs, send_sems, recv_sems)
        for cp in cps:
            cp.wait_recv()
        for cp in cps:
            cp.wait_send()

    return issue, await_, n_swap + (7 if with_pack else 0)


def _sum_packs(pack, packs):
    x, y, c = lax.axis_index("x"), lax.axis_index("y"), lax.axis_index("c")
    me = jnp.reshape(4 * x + 2 * y + c, (1,)).astype(jnp.int32)

    def body(me_ref, own_ref, p_ref, o_ref):
        acc = jnp.where(me_ref[0] == 0, own_ref[...], p_ref[0])
        for d in range(1, 8):
            acc = acc + jnp.where(me_ref[0] == d, own_ref[...], p_ref[d])
        o_ref[...] = acc

    full = lambda s: pl.BlockSpec(s.shape, lambda i, t: (0,) * s.ndim)
    return pl.pallas_call(
        body, name="sum_packs",
        grid_spec=pltpu.PrefetchScalarGridSpec(num_scalar_prefetch=1, grid=(1,), in_specs=[full(pack), full(packs)],
                                               out_specs=full(pack)),
        out_shape=jax.ShapeDtypeStruct(pack.shape, F32),
    )(me, pack, packs)


def _adamw_update(g, w_ref, m_ref, v_ref, go, do, mo, vo):
    c1 = 1.0 / (1.0 - ADAM_B1 ** ADAM_STEP)
    c2 = 1.0 / (1.0 - ADAM_B2 ** ADAM_STEP)
    mn = ADAM_B1 * m_ref[...] + (1.0 - ADAM_B1) * g
    vn = ADAM_B2 * v_ref[...] + (1.0 - ADAM_B2) * (g * g)
    go[...] = g
    mo[...] = mn
    vo[...] = vn
    do[...] = -ADAM_LR * ((mn * c1) / (jnp.sqrt(vn * c2) + ADAM_EPS) + ADAM_WD * w_ref[...])


def _adamw(w, m, v, g1, g2, rows, name):
    r, cdim = w.shape
    rows = min(rows, r)

    def body(w_ref, m_ref, v_ref, g1_ref, g2_ref, *outs):
        _adamw_update(g1_ref[...] + g2_ref[...], w_ref, m_ref, v_ref, *outs)

    blk = pl.BlockSpec((rows, cdim), lambda i: (i, 0))
    shp = jax.ShapeDtypeStruct((r, cdim), F32)
    return pl.pallas_call(
        body, name=name, grid=(r // rows,),
        in_specs=[blk] * 5, out_specs=[blk] * 4, out_shape=[shp] * 4,
        compiler_params=_params(("parallel",), 20 * rows * cdim * 4 + 8 * 2**20),
    )(*_in_hbm(w, m, v, g1, g2))


def _adamw_small(w_s, m_s, v_s, tot):
    per_row = GW // DH
    cq_blocks, cw_blocks = 3 * per_row // 4, per_row // 4
    cq_lanes, cw_lanes = cq_blocks * DH, cw_blocks * DH
    shapes = [(1, GW), (4, cq_lanes), (1, HEADS), (1, HEADS), (1, DH), (3, cw_lanes), (1, GW), (per_row, DH)]

    def body(t_ref, w_ref, m_ref, v_ref, *refs):
        g_scr, kinds = refs[len(refs) - 5], refs[len(refs) - 4:]
        chip = 2 * lax.axis_index("x") + lax.axis_index("y")

        def mine(first_row, rows_per_tap, nblk, t, jb):
            out = None
            for k in reversed(range(4)):
                b = nblk * k + jb
                row = first_row + rows_per_tap * t + b // per_row
                cand = t_ref[row:row + 1, (b % per_row) * DH:(b % per_row + 1) * DH]
                out = cand if out is None else jnp.where(chip == k, cand, out)
            return out

        g_scr[...] = jnp.zeros_like(g_scr)
        for src, dst in ((R_NIN, S_NIN), (R_CB, S_CB), (R_FN, S_FN), (R_AD, S_AD), (R_GN, S_GN)):
            g_scr[dst:dst + 1, :] = t_ref[src:src + 1, :]
        for t in range(4):
            for jb in range(cq_blocks):
                g_scr[S_CQ + t:S_CQ + t + 1, jb * DH:(jb + 1) * DH] = mine(R_CQ, 3, cq_blocks, t, jb)
        for t in range(3):
            for jb in range(cw_blocks):
                g_scr[S_CW + t:S_CW + t + 1, jb * DH:(jb + 1) * DH] = mine(R_CW, 1, cw_blocks, t, jb)
        _adamw_update(g_scr[...], w_ref, m_ref, v_ref, *kinds)
        for kk, a in enumerate(kinds):
            nin, cq, al, dt, gn, cw, cb, fn = refs[8 * kk:8 * kk + 8]
            nin[...] = a[S_NIN:S_NIN + 1, :]
            cq[...] = a[S_CQ:S_CQ + 4, 0:cq_lanes]
            al[...] = a[S_AD:S_AD + 1, 0:HEADS]
            dt[...] = a[S_AD:S_AD + 1, HEADS:2 * HEADS]
            gn[...] = a[S_GN:S_GN + 1, 0:DH]
            cw[...] = a[S_CW:S_CW + 3, 0:cw_lanes]
            cb[...] = a[S_CB:S_CB + 1, :]
            for k in range(per_row):
                fn[k:k + 1, :] = a[S_FN:S_FN + 1, k * DH:(k + 1) * DH]

    out = pl.pallas_call(
        body, name="adamw_small",
        out_shape=[jax.ShapeDtypeStruct(s, F32) for s in shapes] * 4,
        scratch_shapes=[pltpu.VMEM(w_s.shape, F32)] * 5,
    )(tot, w_s, m_s, v_s)
    return [out[8 * kk:8 * kk + 8] for kk in range(4)]


def _adamw_shard(wt, mt, vt, g1, g2):
    r, d = wt.shape
    cols = min(256, d)

    def body(w_ref, m_ref, v_ref, g_ref, g2_ref, go, do, mo, vo, pad_ref):
        chip = 2 * lax.axis_index("x") + lax.axis_index("y")
        back = [(ALIGNED_W - s) % ALIGNED_W for s in SHIFTS]
        pad_ref[...] = pltpu.roll(g_ref[...] + g2_ref[...], _by_chip(chip, back), 0)
        outs = [o.at[:, 0, :] for o in (go, do, mo, vo)]
        _adamw_update(pad_ref[0:r, :], w_ref, m_ref, v_ref, *outs)

    blk = pl.BlockSpec((r, cols), lambda i: (0, i))
    gblk = pl.BlockSpec((ALIGNED_W, cols), lambda i: (0, i))
    oblk = pl.BlockSpec((r, 1, cols), lambda i: (0, 0, i))
    shp = jax.ShapeDtypeStruct((r, 1, d), F32)
    return pl.pallas_call(
        body, name="adamw_w_in", grid=(d // cols,),
        in_specs=[blk] * 3 + [gblk] * 2, out_specs=[oblk] * 4, out_shape=[shp] * 4,
        scratch_shapes=[pltpu.VMEM((ALIGNED_W, cols), F32)],
        compiler_params=_params(("parallel",), 24 * ALIGNED_W * cols * 4 + 8 * 2**20),
    )(wt, mt, vt, g1, g2)


def _pad_lanes(a, width):
    return jnp.pad(a, ((0, 0), (0, width - a.shape[1])))


def _gathered_to_full(g):
    return jnp.transpose(g, (1, 0, 2)).reshape(g.shape[1], 4 * g.shape[2])


def _row(a):
    return _pad_lanes(a.reshape(1, -1), 1024)


S_NIN, S_CB, S_FN, S_AD, S_GN, S_CQ, S_CW = 0, 1, 2, 3, 4, 5, 9


def _small_pack(nin, cb, fn, al, dt, gn, cqw_shard, cw_shard):
    ad = jnp.concatenate([al.reshape(1, -1), dt.reshape(1, -1)], axis=1)
    rows = [_row(nin), _row(cb), _row(fn), _row(ad), _row(gn), _pad_lanes(cqw_shard, 1024),
            _pad_lanes(cw_shard, 1024)]
    out = jnp.concatenate(rows, axis=0)
    return jnp.pad(out, ((0, 16 - out.shape[0]), (0, 0)))


def kernel(x, norm_in_w, w_in, conv_qkv_w, A_log, dt_bias, gdn_norm_w, conv_w, conv_b, w_out, final_norm_w, loss_target, m_norm_in_w, m_w_in, m_conv_qkv_w, m_A_log, m_dt_bias, m_gdn_norm_w, m_conv_w, m_conv_b, m_w_out, m_final_norm_w, v_norm_in_w, v_w_in, v_conv_qkv_w, v_A_log, v_dt_bias, v_gdn_norm_w, v_conv_w, v_conv_b, v_w_out, v_final_norm_w):
    a_shard = _align_shard(jnp.transpose(w_in, (2, 0, 1)))
    d_model = x.shape[-1]
    stack = lambda s: lax.empty((4,) + s.shape, s.dtype)
    wg0 = lax.empty((WG_BLOCKS * DH, d_model), BF16)
    wc0 = lax.empty((WC_BLOCKS * DH, d_model), BF16)
    ss_g, rs_g, bufs_g, tok_g = _gather_start("g", a_shard, wg0, [(conv_qkv_w[0], stack(conv_qkv_w[0]))])
    wo_b = _cast_bf16(w_out[0], 256, "cast_w_out", tok_g)
    ss_c, rs_c, bufs_c, tok_c = _gather_start("c", bufs_g[0], wc0,
                                              [(conv_w[0], stack(conv_w[0])), (wo_b, stack(wo_b))])
    wg1, wc1, wog1, cqg1, cwg1 = _place_own(bufs_c[0], bufs_c[4], bufs_g[2], bufs_c[2],
                                            [bufs_g[1], bufs_c[1], bufs_c[5], bufs_g[3], bufs_c[3]])
    x0 = x[0]
    h = _rms_in(x0, _tie(_tie(norm_in_w, tok_g, "after_gather_start_g"), tok_c, "after_gather_start_c"))
    adam_in = [jnp.transpose(a[0]) for a in (w_in, m_w_in, v_w_in)]
    sp = lambda nin, cb, fn, al, dt, gn, cq, cwv: _small_pack(nin, cb, fn, al, dt, gn, cq[0], cwv[0])
    w_s = sp(norm_in_w, conv_b, final_norm_w, A_log, dt_bias, gdn_norm_w, conv_qkv_w, conv_w)
    m_s = sp(m_norm_in_w, m_conv_b, m_final_norm_w, m_A_log, m_dt_bias, m_gdn_norm_w, m_conv_qkv_w, m_conv_w)
    v_s = sp(v_norm_in_w, v_conv_b, v_final_norm_w, v_A_log, v_dt_bias, v_gdn_norm_w, v_conv_qkv_w, v_conv_w)
    a_thru, wg, _, cq_g = _gather_wait("g", ss_g, rs_g, [bufs_c[0], wg1, bufs_g[2], cqg1],
                                       [h, w_s, m_s, v_s] + adam_in[1:])
    w_g = _merge_edges(_sibling_forward("g", wg), G_EDGE, G_MIXED, "merge_edges_g")
    cqw = _gathered_to_full(cq_g)
    ad = jnp.pad(jnp.concatenate([A_log, dt_bias], axis=0), ((0, 0), (A_LANE, 0)))
    fwd_c = {}

    def on_q(q):
        _, wc, _, cw_g, _, wo_g = _gather_wait("c", ss_c, rs_c,
                                               [a_thru, wc1, bufs_c[2], cwg1, bufs_c[4], wog1], q)
        issue, _ = _sibling_forward_parts("c")
        ss, rs, (wc,), tok = _split_start("sibling_forward_start_c", issue, [wc], 3)
        fwd_c.update(ss=ss, rs=rs, wc=wc, cw_g=cw_g, wo_g=wo_g)
        return _tie(q, tok, "after_sibling_forward_start_c")

    def late(o):
        _, await_ = _sibling_forward_parts("c")
        (wc,) = _split_wait("sibling_forward_wait_c", await_, fwd_c["ss"], fwd_c["rs"], [fwd_c["wc"]], o)
        return (_merge_edges(wc, C_EDGE, C_MIXED, "merge_edges_c"), fwd_c["wo_g"].reshape(2 * GW, d_model),
                _gathered_to_full(fwd_c["cw_g"]))

    scat = {}

    def on_grad_c(g_c, g_wout, do):
        go4 = g_wout.reshape(4, GW // 2, d_model)
        land = lax.empty((3, ALIGNED_W, d_model), BF16)
        land_o = lax.empty((3, GW // 2, d_model), BF16)
        ss, rs, bufs, tok = _scatter_start("c", g_c, land, [(go4, land_o)])
        scat["c"] = (ss, rs, bufs)
        return _tie(do, tok, "after_scatter_start_c")

    def on_grad_g(g_g, dproj_g):
        ss, rs, bufs, tok = _scatter_start("g", _pair_reduce("g", g_g), scat["c"][2][1], [], halved=True)
        scat["g"] = (ss, rs, bufs)
        return _tie(dproj_g, tok, "after_scatter_start_g")

    gx, sm, _ = _local_step(x0, loss_target[0], h, w_g, cqw, late, norm_in_w, ad, gdn_norm_w, conv_b,
                            final_norm_w.reshape(1, -1), on_grad_c, on_grad_g, on_q)

    ss, rs, bufs = scat["c"]
    g_c, land, go4, land_o = _scatter_wait("c", ss, rs, [bufs[0], scat["g"][2][1], bufs[2], bufs[3]], gx)
    part_out = _sum_rows(go4, land_o, 128)
    ad_g = jnp.concatenate([sm["al"][:, A_LANE:], sm["dt"][:, A_LANE:]], axis=1)
    pack = jnp.concatenate([_row(sm["nin"]), _row(sm["cb"]), _row(sm["fn"]), _row(ad_g), _row(sm["gn"]),
                            jnp.concatenate(sm["cq"], axis=1).reshape(12, 1024), sm["cw"], _row(sm["loss"])], axis=0)
    pack = jnp.pad(pack, ((0, PACK_ROWS - pack.shape[0]), (0, 0)))
    issue, await_a, nsem = _exchange_parts(1, True)
    ss_a, rs_a, bufs_a, tok_a = _split_start(
        "exchange_start_small", issue,
        [part_out, lax.empty(part_out.shape, F32), pack, lax.empty((8,) + pack.shape, F32)], nsem)
    ss, rs, bufs = scat["g"]
    g_g, land = _scatter_wait("g", ss, rs, [bufs[0], land], [gx, tok_a], halved=True)
    part_in = _sum_shard(g_g, g_c, land)
    issue, await_b, nsem = _exchange_parts(1, False)
    ss_b, rs_b, bufs_b, tok_b = _split_start("exchange_start_w_in", issue,
                                             [part_in, lax.empty(part_in.shape, F32)], nsem)
    part_out, sib_out, pack, packs = _split_wait("exchange_wait_small", await_a, ss_a, rs_a, bufs_a, tok_b)
    tot = _sum_packs(pack, packs)
    g_wo, d_wo, m_wo, v_wo = _adamw(w_out[0], m_w_out[0], v_w_out[0], part_out, sib_out, 128, "adamw_w_out")
    small = _adamw_small(w_s, m_s, v_s, tot)
    part_in, sib_in = _split_wait("exchange_wait_w_in", await_b, ss_b, rs_b, bufs_b, [small[0][0], d_wo])
    g_wi, d_wi, m_wi, v_wi = [jnp.transpose(a, (1, 2, 0))[0] for a in _adamw_shard(*adam_in, part_in, sib_in)]

    def unpack(leaves, big_in, big_out):
        nin, cq, al, dt, gn, cw, cb, fn = leaves
        return (nin, big_in[None], cq[None], al, dt, gn, cw[None], cb, big_out[None], fn.reshape(-1))

    loss = tot[R_LOSS, 0]
    return (loss, gx[None], *unpack(small[0], g_wi, g_wo), *unpack(small[1], d_wi, d_wo),
            *unpack(small[2], m_wi, m_wo), *unpack(small[3], v_wi, v_wo))
```

```python
import jax
import jax.numpy as jnp
from jax import lax
from jax.experimental import pallas as pl
from jax.experimental.pallas import tpu as pltpu

F32 = jnp.float32
BF16 = jnp.bfloat16
MESH = pl.DeviceIdType.MESH
ANY = pl.BlockSpec(memory_space=pl.ANY)

HEADS = 8
DH = 128
CH = 64
GW = HEADS * DH
EPS = 1e-6
VMEM_V7X = 64 * 1024 * 1024

QB, KB, VB, ZB, BAB = 0, 8, 16, 24, 32
A_LANE = 120
NG, NC = 33, 32
GW_COLS, CW_COLS = NG * DH, NC * DH

SHARD_W = 2052
ALIGNED_BLOCKS = 17
ALIGNED_W = ALIGNED_BLOCKS * DH
SHIFTS = (0, 4, ALIGNED_W - 8, ALIGNED_W - 4)
G_EDGE, C_EDGE = 34, 32
G_SPARE, C_SPARE = 33, 34
WG_BLOCKS, WC_BLOCKS = 38, 36
G_MIXED, C_MIXED = (2, BAB), (4 * 7 + 1,)


def _shard_blocks(chip, edges):
    g, c = "g", "c"
    if chip == 0:
        out = [(g, 3 * b) for b in range(8)] + [(g, 3 * b + 1) for b in range(8)] + [(g, G_EDGE, G_MIXED[0])]
    elif chip == 1:
        out = [(g, G_EDGE + 1, G_MIXED[0])] + [(g, 3 * b + 2) for b in range(1, 8)]
        out += [(g, ZB + b) for b in range(8)] + [(g, G_EDGE + 2, G_MIXED[1])]
    elif chip == 2:
        out = [(c, 4 * b) for b in range(8)] + [(c, 4 * b + 1) for b in range(7)]
        out += [(c, C_EDGE, C_MIXED[0]), (g, G_EDGE + 3, G_MIXED[1])]
    else:
        out = [(c, 4 * b + 2) for b in range(8)] + [(c, 4 * b + 3) for b in range(8)] + [(c, C_EDGE + 1, C_MIXED[0])]
    return [(o[0], o[1] if (edges or len(o) == 2) else o[2]) for o in out]


def _by_chip(chip, vals):
    if all(v == vals[0] for v in vals):
        return vals[0]
    r = vals[3]
    for kk in (2, 1, 0):
        r = jnp.where(chip == kk, vals[kk], r)
    return r

ADAM_LR, ADAM_B1, ADAM_B2, ADAM_EPS, ADAM_WD, ADAM_STEP = 0.001, 0.9, 0.999, 1e-08, 0.01, 10

R_NIN, R_CB, R_FN, R_AD, R_GN, R_CQ, R_CW, R_LOSS, PACK_ROWS = 0, 1, 2, 3, 4, 5, 17, 20, 24

NN = ((1,), (0,))
NT = ((1,), (1,))
TN = ((0,), (0,))


def _dot(a, b, dims=NN, mode="lo"):
    dn = (dims, ((), ()))
    if mode == "hi":
        return lax.dot_general(a, b, dn, precision=lax.Precision.HIGHEST, preferred_element_type=F32)
    ah, bh = a.astype(BF16), b.astype(BF16)
    out = lax.dot_general(ah, bh, dn, preferred_element_type=F32)
    if mode == "x3":
        al = (a - ah.astype(F32)).astype(BF16)
        bl = (b - bh.astype(F32)).astype(BF16)
        out = out + lax.dot_general(ah, bl, dn, preferred_element_type=F32)
        out = out + lax.dot_general(al, bh, dn, preferred_element_type=F32)
    return out


P_GRAM, P_INV, P_SOL, P_SCAN, P_SCANB, P_BWD = "lo", "lo", "lo", "lo", "lo", "lo"
P_CUM = "x3"


def _params(sem=None, vmem=None):
    kw = {}
    if sem is not None:
        kw["dimension_semantics"] = sem
    if vmem is not None:
        kw["vmem_limit_bytes"] = int(min(max(vmem, 32 * 2**20), VMEM_V7X - 8 * 2**20))
    return pltpu.CompilerParams(**kw)


def _in_hbm(*arrays):
    return [pltpu.with_memory_space_constraint(a, pltpu.HBM) for a in arrays]


def _sigmoid(x):
    return 1.0 / (1.0 + jnp.exp(-x))


def _dsilu(x, s):
    return s * (1.0 + x * (1.0 - s))


def _rows(shape):
    return lax.broadcasted_iota(jnp.int32, shape, 0)


def _shift_down(x, s):
    if s == 0:
        return x
    return jnp.where(_rows(x.shape) >= s, pltpu.roll(x, s, 0), 0.0)


def _shift_up(x, s):
    if s == 0:
        return x
    n = x.shape[0]
    return jnp.where(_rows(x.shape) < n - s, pltpu.roll(x, n - s, 0), 0.0)


def _matmul(a, b, dims, out_dtype, tm, tn, tk, name, add=None, n=None, b_outer=False):
    if dims == NN:
        (m, k), n = a.shape, b.shape[1]
    elif dims == NT:
        (m, k), n = a.shape, (n or b.shape[0])
    else:
        (k, m), n = a.shape, b.shape[1]
    tm, tn, tk = min(tm, m), min(tn, n), min(tk, k)
    assert m % tm == 0 and n % tn == 0 and k % tk == 0, (name, m, n, k, tm, tn, tk)
    nk = k // tk

    def body(*refs):
        if add is None:
            a_ref, b_ref, o_ref = refs[:3]
            add_ref = None
        else:
            a_ref, b_ref, add_ref, o_ref = refs[:4]
        part = _dot(a_ref[...], b_ref[...], dims)
        if nk == 1:
            if add_ref is not None:
                part = part + add_ref[...]
            o_ref[...] = part.astype(out_dtype)
            return
        acc = refs[-1]
        kk = pl.program_id(2)

        @pl.when(kk == 0)
        def _():
            acc[...] = part

        @pl.when(kk > 0)
        def _():
            acc[...] += part

        @pl.when(kk == nk - 1)
        def _():
            r = acc[...]
            if add_ref is not None:
                r = r + add_ref[...]
            o_ref[...] = r.astype(out_dtype)

    ij = (lambda g0, g1: (g1, g0)) if b_outer else (lambda g0, g1: (g0, g1))

    def spec(shape, pick):
        return pl.BlockSpec(shape, lambda g0, g1, kk: pick(*ij(g0, g1), kk))

    a_spec = spec((tk, tm), lambda i, j, kk: (kk, i)) if dims == TN else spec((tm, tk), lambda i, j, kk: (i, kk))
    b_spec = spec((tn, tk), lambda i, j, kk: (j, kk)) if dims == NT else spec((tk, tn), lambda i, j, kk: (kk, j))
    o_spec = spec((tm, tn), lambda i, j, kk: (i, j))
    in_specs = [a_spec, b_spec]
    args = [a, b]
    if add is not None:
        in_specs.append(o_spec)
        args.append(add)
    osz = jnp.dtype(out_dtype).itemsize
    est = 2 * (tm * tk * a.dtype.itemsize + tk * tn * b.dtype.itemsize + tm * tn * osz)
    est += 3 * tm * tn * 4 + (2 * tm * tn * 4 if add is not None else 0)
    return pl.pallas_call(
        body, name=name, grid=(n // tn, m // tm, nk) if b_outer else (m // tm, n // tn, nk),
        in_specs=in_specs, out_specs=o_spec,
        out_shape=jax.ShapeDtypeStruct((m, n), out_dtype),
        scratch_shapes=[pltpu.VMEM((tm, tn), F32)] if nk > 1 else [],
        compiler_params=_params(("parallel", "parallel", "arbitrary"), est + 8 * 2**20),
    )(*args)


def _cast_bf16(a, rows, name, after):
    r, c = a.shape
    rows = min(rows, r)

    def body(a_ref, t_ref, o_ref):
        del t_ref
        o_ref[...] = a_ref[...].astype(BF16)

    return pl.pallas_call(
        body, name=name, grid=(r // rows,),
        in_specs=[pl.BlockSpec((rows, c), lambda i: (i, 0)), ANY],
        out_specs=pl.BlockSpec((rows, c), lambda i: (i, 0)),
        out_shape=jax.ShapeDtypeStruct((r, c), BF16),
        compiler_params=_params(("parallel",)),
    )(a, after)


def _align_shard(wt):
    r, _, d = wt.shape
    cols = min(256, d)

    def body(w_ref, o_ref, pad_ref):
        chip = 2 * lax.axis_index("x") + lax.axis_index("y")
        pad_ref[...] = jnp.zeros_like(pad_ref)
        pad_ref[0:r, :] = w_ref[:, 0, :]
        o_ref[...] = pltpu.roll(pad_ref[...], _by_chip(chip, SHIFTS), 0).astype(BF16)

    return pl.pallas_call(
        body, name="align_shard", grid=(d // cols,),
        in_specs=[pl.BlockSpec((r, 1, cols), lambda i: (0, 0, i))],
        out_specs=pl.BlockSpec((ALIGNED_W, cols), lambda i: (0, i)),
        out_shape=jax.ShapeDtypeStruct((ALIGNED_W, d), BF16),
        scratch_shapes=[pltpu.VMEM((ALIGNED_W, cols), F32)],
        compiler_params=_params(("parallel",)),
    )(wt)


def _rms_in(x, w):
    n, d = x.shape
    tr = min(256, n)

    def body(x_ref, w_ref, h_ref):
        xv = x_ref[...]
        r = lax.rsqrt(jnp.mean(xv * xv, axis=-1, keepdims=True) + EPS)
        h_ref[...] = (xv * r * w_ref[...]).astype(BF16)

    return pl.pallas_call(
        body, name="rms_in", grid=(n // tr,),
        in_specs=[pl.BlockSpec((tr, d), lambda i: (i, 0)), pl.BlockSpec((1, d), lambda i: (0, 0))],
        out_specs=pl.BlockSpec((tr, d), lambda i: (i, 0)),
        out_shape=jax.ShapeDtypeStruct((n, d), BF16),
        compiler_params=_params(("parallel",)),
    )(x, w)


def _conv_silu(p, w_ref, taps):
    c = None
    for j in range(taps):
        t = _shift_down(p, taps - 1 - j) * w_ref[j:j + 1, :]
        c = t if c is None else c + t
    return c


def _prep_qkv(proj, cw):
    n = proj.shape[0]

    def body(p3, wq, wk, wv, q_ref, k_ref, v_ref):
        for kind, (w_ref, o_ref) in enumerate(((wq, q_ref), (wk, k_ref), (wv, v_ref))):
            c = _conv_silu(p3[:, kind * DH:(kind + 1) * DH], w_ref, 4)
            a = c * _sigmoid(c)
            if kind < 2:
                r = lax.rsqrt(jnp.sum(a * a, axis=-1, keepdims=True) + EPS)
                a = a * (r * (DH ** -0.5 if kind == 0 else 1.0))
            o_ref[...] = a

    col = pl.BlockSpec((n, DH), lambda h: (0, h))
    wcol = lambda base: pl.BlockSpec((4, DH), lambda h: (0, base + h))
    out = jax.ShapeDtypeStruct((n, GW), F32)
    return pl.pallas_call(
        body, name="prep_qkv", grid=(HEADS,),
        in_specs=[pl.BlockSpec((n, 3 * DH), lambda h: (0, h)), wcol(QB), wcol(KB), wcol(VB)],
        out_specs=[col] * 3, out_shape=[out] * 3,
        compiler_params=_params(("parallel",), 40 * 2**20),
    )(proj, cw, cw, cw)


def _prep_qkv_bwd(proj, cw, dq, dk, dv, dproj):
    n = proj.shape[0]

    def body(p3, wq, wk, wv, dq_ref, dk_ref, dv_ref, _, o3, gq, gk, gv):
        for kind, (w_ref, d_ref, g_ref) in enumerate(((wq, dq_ref, gq), (wk, dk_ref, gk), (wv, dv_ref, gv))):
            p = p3[:, kind * DH:(kind + 1) * DH]
            shifted = [_shift_down(p, 3 - j) for j in range(4)]
            c = shifted[0] * w_ref[0:1, :]
            for j in range(1, 4):
                c = c + shifted[j] * w_ref[j:j + 1, :]
            s = _sigmoid(c)
            a = c * s
            d = d_ref[...]
            if kind < 2:
                r = lax.rsqrt(jnp.sum(a * a, axis=-1, keepdims=True) + EPS)
                sc = DH ** -0.5 if kind == 0 else 1.0
                d = (sc * r) * (d - a * ((r * r) * jnp.sum(d * a, axis=-1, keepdims=True)))
            dc = d * _dsilu(c, s)
            dp = None
            for j in range(4):
                g_ref[j:j + 1, :] = jnp.sum(dc * shifted[j], axis=0, keepdims=True)
                t = _shift_up(dc, 3 - j) * w_ref[j:j + 1, :]
                dp = t if dp is None else dp + t
            o3[:, kind * DH:(kind + 1) * DH] = dp.astype(BF16)

    col = pl.BlockSpec((n, DH), lambda h: (0, h))
    wcol = lambda base: pl.BlockSpec((4, DH), lambda h: (0, base + h))
    p3spec = pl.BlockSpec((n, 3 * DH), lambda h: (0, h))
    return pl.pallas_call(
        body, name="prep_qkv_bwd", grid=(HEADS,),
        in_specs=[p3spec, wcol(QB), wcol(KB), wcol(VB), col, col, col, ANY],
        out_specs=[p3spec] + [wcol(0)] * 3,
        out_shape=[jax.ShapeDtypeStruct(dproj.shape, BF16)] + [jax.ShapeDtypeStruct((4, GW), F32)] * 3,
        input_output_aliases={7: 0},
        compiler_params=_params(("parallel",), 48 * 2**20),
    )(proj, cw, cw, cw, dq, dk, dv, dproj)


CPB = 8
SCAN_CPS = 4


def _tri(lower, rows):
    i = lax.broadcasted_iota(jnp.int32, (rows, rows), 0)
    j = lax.broadcasted_iota(jnp.int32, (rows, rows), 1)
    return jnp.where((i // CH == j // CH) & ((i >= j) if lower else (j >= i)), 1.0, 0.0)


def _lane(shape):
    return lax.broadcasted_iota(jnp.int32, shape, 1)


def _prep_bg(proj, ad):
    n = proj.shape[0]
    nch = n // CH
    cpb = CPB if nch % CPB == 0 else 1
    rows = cpb * CH

    def body(p_ref, ad_ref, bg_ref, bgt_ref):
        p = p_ref[...]
        lane = _lane(p.shape)
        beta = _sigmoid(p)
        xa = p + ad_ref[1:2, :]
        sp = jnp.maximum(xa, 0.0) + jnp.log(1.0 + jnp.exp(-jnp.abs(xa)))
        g = pltpu.roll(-jnp.exp(ad_ref[0:1, :]) * sp, DH - A_LANE + HEADS, 1)
        gc = _dot(_tri(True, rows), g, NN, P_CUM)
        bg = jnp.where(lane < HEADS, beta, jnp.where(lane < 2 * HEADS, gc, 0.0))
        bg_ref[...] = bg
        for ci in range(cpb):
            bgt_ref[ci] = bg[ci * CH:(ci + 1) * CH, :].T

    return pl.pallas_call(
        body, name="prep_bg", grid=(nch // cpb,),
        in_specs=[pl.BlockSpec((rows, DH), lambda i: (i, BAB)), pl.BlockSpec((2, DH), lambda i: (0, 0))],
        out_specs=[pl.BlockSpec((rows, DH), lambda i: (i, 0)), pl.BlockSpec((cpb, DH, CH), lambda i: (i, 0, 0))],
        out_shape=[jax.ShapeDtypeStruct((n, DH), F32), jax.ShapeDtypeStruct((nch, DH, CH), F32)],
        compiler_params=_params(("parallel",)),
    )(*_in_hbm(proj, ad))


def _prep_bg_bwd(proj, ad, dbg, dproj):
    n = proj.shape[0]
    nch = n // CH
    cpb = CPB if nch % CPB == 0 else 1
    rows = cpb * CH

    def body(p_ref, ad_ref, d_ref, _, o_ref, ga_ref, gd_ref):
        p = p_ref[...]
        d = d_ref[...]
        lane = _lane(p.shape)
        beta = _sigmoid(p)
        xa = p + ad_ref[1:2, :]
        sp = jnp.maximum(xa, 0.0) + jnp.log(1.0 + jnp.exp(-jnp.abs(xa)))
        na = -jnp.exp(ad_ref[0:1, :])
        dg = pltpu.roll(_dot(_tri(False, rows), d, NN, P_CUM), A_LANE - HEADS, 1)
        da = dg * na * _sigmoid(xa)
        is_g = lane >= A_LANE
        o_ref[...] = jnp.where(lane < HEADS, d * beta * (1.0 - beta), jnp.where(is_g, da, 0.0)).astype(BF16)
        ga = jnp.sum(jnp.where(is_g, dg * na * sp, 0.0), axis=0, keepdims=True)
        gd = jnp.sum(jnp.where(is_g, da, 0.0), axis=0, keepdims=True)

        @pl.when(pl.program_id(0) == 0)
        def _():
            ga_ref[...] = jnp.zeros_like(ga_ref)
            gd_ref[...] = jnp.zeros_like(gd_ref)

        ga_ref[...] += ga
        gd_ref[...] += gd

    one = pl.BlockSpec((1, DH), lambda i: (0, 0))
    return pl.pallas_call(
        body, name="prep_bg_bwd", grid=(nch // cpb,),
        in_specs=[pl.BlockSpec((rows, DH), lambda i: (i, BAB)), pl.BlockSpec((2, DH), lambda i: (0, 0)),
                  pl.BlockSpec((rows, DH), lambda i: (i, 0)), ANY],
        out_specs=[pl.BlockSpec((rows, DH), lambda i: (i, BAB)), one, one],
        out_shape=[jax.ShapeDtypeStruct(dproj.shape, BF16), jax.ShapeDtypeStruct((1, DH), F32),
                   jax.ShapeDtypeStruct((1, DH), F32)],
        input_output_aliases={3: 0},
        compiler_params=_params(("arbitrary",)),
    )(proj, ad, dbg, dproj)


def _gdn_out(o, proj, wg):
    n = o.shape[0]

    def body(o_ref, z_ref, w_ref, y_ref):
        ov, z = o_ref[...], z_ref[...]
        r = lax.rsqrt(jnp.mean(ov * ov, axis=-1, keepdims=True) + EPS)
        y_ref[...] = (ov * r * w_ref[...] * (z * _sigmoid(z))).astype(BF16)

    return pl.pallas_call(
        body, name="gdn_out", grid=(HEADS,),
        in_specs=[pl.BlockSpec((n, DH), lambda h: (0, h)), pl.BlockSpec((n, DH), lambda h: (0, ZB + h)),
                  pl.BlockSpec((1, DH), lambda h: (0, 0))],
        out_specs=pl.BlockSpec((n, DH), lambda h: (0, h)),
        out_shape=jax.ShapeDtypeStruct((n, 2 * GW), BF16),
        compiler_params=_params(("parallel",)),
    )(o, proj, wg)


def _gdn_out_bwd(o, proj, wg, dout_b, w_out):
    n = o.shape[0]
    d_model = dout_b.shape[1]

    def body(o_ref, z_ref, w_ref, g_ref, wo_ref, do_ref, dz_ref, gw_ref):
        ov, z, w = o_ref[...], z_ref[...], w_ref[...]
        d = _dot(g_ref[...], wo_ref[...], NT)
        r = lax.rsqrt(jnp.mean(ov * ov, axis=-1, keepdims=True) + EPS)
        nrm = ov * r
        s = _sigmoid(z)
        dz_ref[...] = (d * (nrm * w) * _dsilu(z, s)).astype(BF16)
        dn_w = d * (z * s)
        gw = jnp.sum(dn_w * nrm, axis=0, keepdims=True)
        dn = dn_w * w
        do_ref[...] = (r * (dn - nrm * jnp.mean(dn * nrm, axis=-1, keepdims=True))).astype(BF16)

        @pl.when(pl.program_id(0) == 0)
        def _():
            gw_ref[...] = jnp.zeros_like(gw_ref)

        gw_ref[...] += gw

    return pl.pallas_call(
        body, name="gdn_out_bwd", grid=(HEADS,),
        in_specs=[pl.BlockSpec((n, DH), lambda h: (0, h)), pl.BlockSpec((n, DH), lambda h: (0, ZB + h)),
                  pl.BlockSpec((1, DH), lambda h: (0, 0)), pl.BlockSpec((n, d_model), lambda h: (0, 0)),
                  pl.BlockSpec((DH, d_model), lambda h: (h, 0))],
        out_specs=[pl.BlockSpec((n, DH), lambda h: (0, h)), pl.BlockSpec((n, DH), lambda h: (0, ZB + h)),
                   pl.BlockSpec((1, DH), lambda h: (0, 0))],
        out_shape=[jax.ShapeDtypeStruct((n, GW), BF16), jax.ShapeDtypeStruct((n, GW_COLS), BF16),
                   jax.ShapeDtypeStruct((1, DH), F32)],
        compiler_params=_params(("arbitrary",), 40 * 2**20),
    )(o, proj, wg, dout_b, w_out)


def _conv_branch(proj, w3, b, mix):
    n = proj.shape[0]

    def body(p4, w_ref, b_ref, _, y_ref):
        u = p4[:, DH:2 * DH] * p4[:, 2 * DH:3 * DH]
        cc = _conv_silu(u, w_ref, 3) + b_ref[...]
        z = p4[:, 3 * DH:4 * DH]
        y_ref[...] = (p4[:, 0:DH] * cc * (z * _sigmoid(z))).astype(BF16)

    return pl.pallas_call(
        body, name="conv_branch", grid=(HEADS,),
        in_specs=[pl.BlockSpec((n, 4 * DH), lambda h: (0, h)), pl.BlockSpec((3, DH), lambda h: (0, h)),
                  pl.BlockSpec((1, DH), lambda h: (0, h)), ANY],
        out_specs=pl.BlockSpec((n, DH), lambda h: (0, HEADS + h)),
        out_shape=jax.ShapeDtypeStruct(mix.shape, BF16),
        input_output_aliases={3: 0},
        compiler_params=_params(("parallel",), 40 * 2**20),
    )(*_in_hbm(proj, w3, b, mix))


def _conv_branch_bwd(proj, w3, b, dout_b, w_out):
    n = proj.shape[0]
    d_model = dout_b.shape[1]

    def body(p4, w_ref, b_ref, g_ref, wo_ref, o4, gw_ref, gbias_ref):
        gb, gcv, hc, z = p4[:, 0:DH], p4[:, DH:2 * DH], p4[:, 2 * DH:3 * DH], p4[:, 3 * DH:4 * DH]
        d = _dot(g_ref[...], wo_ref[...], NT)
        dgb, dgc, dhc, dzc = (o4.at[:, kk * DH:(kk + 1) * DH] for kk in range(4))
        u = gcv * hc
        cc = _conv_silu(u, w_ref, 3) + b_ref[...]
        s = _sigmoid(z)
        dzc[...] = (d * (gb * cc) * _dsilu(z, s)).astype(BF16)
        dp = d * (z * s)
        dgb[...] = (dp * cc).astype(BF16)
        dcc = dp * gb
        gbias_ref[...] = jnp.sum(dcc, axis=0, keepdims=True)
        du = None
        for j in range(3):
            gw_ref[j:j + 1, :] = jnp.sum(dcc * _shift_down(u, 2 - j), axis=0, keepdims=True)
            t = _shift_up(dcc, 2 - j) * w_ref[j:j + 1, :]
            du = t if du is None else du + t
        dgc[...] = (du * hc).astype(BF16)
        dhc[...] = (du * gcv).astype(BF16)

    p4spec = pl.BlockSpec((n, 4 * DH), lambda h: (0, h))
    return pl.pallas_call(
        body, name="conv_branch_bwd", grid=(HEADS,),
        in_specs=[p4spec, pl.BlockSpec((3, DH), lambda h: (0, h)), pl.BlockSpec((1, DH), lambda h: (0, h)),
                  pl.BlockSpec((n, d_model), lambda h: (0, 0)), pl.BlockSpec((DH, d_model), lambda h: (HEADS + h, 0))],
        out_specs=[p4spec, pl.BlockSpec((3, DH), lambda h: (0, h)), pl.BlockSpec((1, DH), lambda h: (0, h))],
        out_shape=[jax.ShapeDtypeStruct((n, CW_COLS), BF16), jax.ShapeDtypeStruct((3, GW), F32),
                   jax.ShapeDtypeStruct((1, GW), F32)],
        compiler_params=_params(("parallel",), 52 * 2**20),
    )(proj, w3, b, dout_b, w_out)


def _out_loss(mix, w_out, x, tgt, wf):
    n, d = x.shape
    kdim = mix.shape[1]
    tr = min(256, n)

    def body(m_ref, wo_ref, x_ref, t_ref, w_ref, do_ref, dob_ref, gw_ref, loss_ref):
        ov = _dot(m_ref[...], wo_ref[...], NN) + x_ref[...]
        w = w_ref[...]
        r = lax.rsqrt(jnp.mean(ov * ov, axis=-1, keepdims=True) + EPS)
        nrm = ov * r
        e = nrm * w - t_ref[...]
        dy = e * (1.0 / d)
        dn = dy * w
        dout = r * (dn - nrm * jnp.mean(dn * nrm, axis=-1, keepdims=True))
        do_ref[...] = dout
        dob_ref[...] = dout.astype(BF16)

        @pl.when(pl.program_id(0) == 0)
        def _():
            gw_ref[...] = jnp.zeros_like(gw_ref)
            loss_ref[...] = jnp.zeros_like(loss_ref)

        gw_ref[...] += jnp.sum(dy * nrm, axis=0, keepdims=True)
        loss_ref[...] += (0.5 / d) * jnp.sum(jnp.sum(e * e, axis=-1, keepdims=True), axis=0, keepdims=True)

    row = pl.BlockSpec((tr, d), lambda i: (i, 0))
    return pl.pallas_call(
        body, name="out_loss", grid=(n // tr,),
        in_specs=[pl.BlockSpec((tr, kdim), lambda i: (i, 0)), pl.BlockSpec((kdim, d), lambda i: (0, 0)), row, row,
                  pl.BlockSpec((1, d), lambda i: (0, 0))],
        out_specs=[row, row, pl.BlockSpec((1, d), lambda i: (0, 0)), pl.BlockSpec((1, 1), lambda i: (0, 0))],
        out_shape=[jax.ShapeDtypeStruct((n, d), F32), jax.ShapeDtypeStruct((n, d), BF16),
                   jax.ShapeDtypeStruct((1, d), F32), jax.ShapeDtypeStruct((1, 1), F32)],
        compiler_params=_params(("arbitrary",), 40 * 2**20),
    )(mix, w_out, x, tgt, wf)


def _dh_rms_bwd(dproj, w_t, dh0, x, w, dout, tk):
    n, d = x.shape
    kdim = dproj.shape[1]
    tm = min(1024, n)
    tk = min(tk, kdim)
    nk = kdim // tk

    def body(a_ref, b_ref, dh0_ref, x_ref, w_ref, do_ref, dx_ref, gw_ref, acc):
        i, kk = pl.program_id(0), pl.program_id(1)
        part = _dot(a_ref[...], b_ref[...], NN)

        @pl.when(kk == 0)
        def _():
            acc[...] = part + dh0_ref[...]

        @pl.when(kk > 0)
        def _():
            acc[...] += part

        @pl.when((i == 0) & (kk == 0))
        def _():
            gw_ref[...] = jnp.zeros_like(gw_ref)

        @pl.when(kk == nk - 1)
        def _():
            xv, dhv = x_ref[...], acc[...]
            r = lax.rsqrt(jnp.mean(xv * xv, axis=-1, keepdims=True) + EPS)
            xn = xv * r
            dxn = dhv * w_ref[...]
            dx_ref[...] = r * (dxn - xn * jnp.mean(dxn * xn, axis=-1, keepdims=True)) + do_ref[...]
            gw_ref[...] += jnp.sum(dhv * xn, axis=0, keepdims=True)

    row = pl.BlockSpec((tm, d), lambda i, kk: (i, 0))
    one = pl.BlockSpec((1, d), lambda i, kk: (0, 0))
    return pl.pallas_call(
        body, name="dh_rms_bwd", grid=(n // tm, nk),
        in_specs=[pl.BlockSpec((tm, tk), lambda i, kk: (i, kk)), pl.BlockSpec((tk, d), lambda i, kk: (kk, 0)),
                  row, row, one, row],
        out_specs=[row, one],
        out_shape=[jax.ShapeDtypeStruct((n, d), F32), jax.ShapeDtypeStruct((1, d), F32)],
        scratch_shapes=[pltpu.VMEM((tm, d), F32)],
        compiler_params=_params(("arbitrary", "arbitrary"), 56 * 2**20),
    )(dproj, w_t, dh0, x, w, dout)


def _ij():
    i = lax.broadcasted_iota(jnp.int32, (CH, CH), 0)
    j = lax.broadcasted_iota(jnp.int32, (CH, CH), 1)
    return i, j


def _unit_lower_inverse(mats):
    i, j = _ij()
    eye = jnp.where(i == j, 1.0, 0.0)
    same16 = (i // 16) == (j // 16)
    same32 = (i // 32) == (j // 32)
    mm = lambda xs, ys: [_dot(x, y, NN, P_INV) for x, y in zip(xs, ys)]
    n1 = [jnp.where(same16, -a, 0.0) for a in mats]
    n2 = mm(n1, n1)
    n4 = mm(n2, n2)
    n8 = mm(n4, n4)
    t = [eye + x1 + x2 + x3 for x1, x2, x3 in zip(n1, n2, mm(n1, n2))]
    t = [x + y for x, y in zip(t, mm(t, n4))]
    t = [x + y for x, y in zip(t, mm(t, n8))]
    a1 = [jnp.where(same32 & jnp.logical_not(same16), a, 0.0) for a in mats]
    t = [x - y for x, y in zip(t, mm(t, mm(a1, t)))]
    a2 = [jnp.where(same32, 0.0, a) for a in mats]
    t = [x - y for x, y in zip(t, mm(t, mm(a2, t)))]
    return t


def _head_vectors(bg, bgt, h):
    bcol = bg[:, h:h + 1]
    gcol = bg[:, HEADS + h:HEADS + h + 1]
    grow = bgt[HEADS + h:HEADS + h + 1, :]
    return bcol, gcol, grow


def _decay(gcol, grow):
    i, j = _ij()
    return jnp.where(i >= j, jnp.exp(jnp.where(i >= j, gcol - grow, 0.0)), 0.0)


def _gdn_intra(q, k, v, bg, bgt):
    n = q.shape[0]
    nch = n // CH
    cps = 4 if nch % 4 == 0 else 1

    def body(q_ref, k_ref, v_ref, bg_ref, bgt_ref, u_ref, w_ref, p_ref, t_ref):
        i, j = _ij()
        items = [(ci, h) for ci in range(cps) for h in range(HEADS)]
        at = lambda ref, ci, h: ref.at[ci * CH:(ci + 1) * CH, h * DH:(h + 1) * DH]
        bgs = [bg_ref[ci * CH:(ci + 1) * CH, :] for ci in range(cps)]
        ks = [at(k_ref, ci, h)[...] for ci, h in items]
        vecs = [_head_vectors(bgs[ci], bgt_ref[ci], h) for ci, h in items]
        decs = [_decay(gcol, grow) for _, gcol, grow in vecs]
        kks = [_dot(kh, kh, NT, P_GRAM) for kh in ks]
        qks = [_dot(at(q_ref, ci, h)[...], kh, NT, P_GRAM) for (ci, h), kh in zip(items, ks)]
        ts = _unit_lower_inverse([jnp.where(i > j, bcol * kk * dec, 0.0)
                                  for (bcol, _, _), kk, dec in zip(vecs, kks, decs)])
        us = [_dot(t, at(v_ref, ci, h)[...] * bcol, NN, P_SOL) for t, (ci, h), (bcol, _, _) in zip(ts, items, vecs)]
        ws = [_dot(t, kh * (bcol * jnp.exp(gcol)), NN, P_SOL) for t, kh, (bcol, gcol, _) in zip(ts, ks, vecs)]
        for n_, (ci, h) in enumerate(items):
            p_ref[ci, h] = qks[n_] * decs[n_]
            t_ref[ci, h] = ts[n_].astype(BF16)
            at(u_ref, ci, h)[...] = us[n_]
            at(w_ref, ci, h)[...] = ws[n_].astype(BF16)

    row = pl.BlockSpec((cps * CH, GW), lambda c: (c, 0))
    sq = pl.BlockSpec((cps, HEADS, CH, CH), lambda c: (c, 0, 0, 0))
    big = jax.ShapeDtypeStruct((n, GW), F32)
    sqs = jax.ShapeDtypeStruct((nch, HEADS, CH, CH), F32)
    return pl.pallas_call(
        body, name="gdn_intra", grid=(nch // cps,),
        in_specs=[row, row, row, pl.BlockSpec((cps * CH, DH), lambda c: (c, 0)),
                  pl.BlockSpec((cps, DH, CH), lambda c: (c, 0, 0))],
        out_specs=[row, row, sq, sq],
        out_shape=[big, jax.ShapeDtypeStruct((n, GW), BF16), sqs, jax.ShapeDtypeStruct(sqs.shape, BF16)],
        compiler_params=_params(("parallel",)),
    )(q, k, v, bg, bgt)


def _gdn_scan(q, k, bg, u, w, p):
    n = q.shape[0]
    nch = n // CH
    cps = SCAN_CPS if nch % SCAN_CPS == 0 else 1

    def body(q_ref, k_ref, bg_ref, u_ref, w_ref, p_ref, o_ref, vn_ref, s_out, s_scr):
        @pl.when(pl.program_id(0) == 0)
        def _():
            s_scr[...] = jnp.zeros_like(s_scr)

        hs = range(HEADS)
        sls = [slice(h * DH, (h + 1) * DH) for h in hs]
        ss = [s_scr[h] for h in hs]
        for ci in range(cps):
            rs = slice(ci * CH, (ci + 1) * CH)
            bg = bg_ref[rs, :]
            gcols = [bg[:, HEADS + h:HEADS + h + 1] for h in hs]
            glasts = [g[CH - 1:CH, :] for g in gcols]
            wss = [_dot(w_ref[rs, sl], s, NN, P_SCAN) for sl, s in zip(sls, ss)]
            oqs = [_dot(q_ref[rs, sl] * jnp.exp(g), s, NN, P_SCAN) for sl, s, g in zip(sls, ss, gcols)]
            vns = [u_ref[rs, sl] - x for sl, x in zip(sls, wss)]
            ops = [_dot(p_ref[ci, h], vn, NN, P_SCAN) for h, vn in zip(hs, vns)]
            sns = [_dot(k_ref[rs, sl] * jnp.exp(gl - g), vn, TN, P_SCAN)
                   for sl, gl, g, vn in zip(sls, glasts, gcols, vns)]
            for h, sl in enumerate(sls):
                s_out[ci, :, sl] = ss[h].astype(BF16)
                vn_ref[rs, sl] = vns[h].astype(BF16)
                o_ref[rs, sl] = oqs[h] + ops[h]
            ss = [s * jnp.exp(gl) + sn for s, gl, sn in zip(ss, glasts, sns)]
        for h in hs:
            s_scr[h] = ss[h]

    row = pl.BlockSpec((cps * CH, GW), lambda c: (c, 0))
    big = jax.ShapeDtypeStruct((n, GW), F32)
    return pl.pallas_call(
        body, name="gdn_scan", grid=(nch // cps,),
        in_specs=[row, row, pl.BlockSpec((cps * CH, DH), lambda c: (c, 0)), row, row,
                  pl.BlockSpec((cps, HEADS, CH, CH), lambda c: (c, 0, 0, 0))],
        out_specs=[row, row, pl.BlockSpec((cps, DH, GW), lambda c: (c, 0, 0))],
        out_shape=[big, jax.ShapeDtypeStruct((n, GW), BF16), jax.ShapeDtypeStruct((nch, DH, GW), BF16)],
        scratch_shapes=[pltpu.VMEM((HEADS, DH, DH), F32)],
        compiler_params=_params(("arbitrary",)),
    )(q, k, bg, u, w, p)


def _gdn_scan_bwd(q, k, bg, w, p, vn, s_in, do):
    n = q.shape[0]
    nch = n // CH
    cps = SCAN_CPS if nch % SCAN_CPS == 0 else 1
    rev = lambda c: nch // cps - 1 - c

    def body(q_ref, k_ref, bg_ref, w_ref, p_ref, vn_ref, s_ref, do_ref,
             dqg_ref, dp_ref, du_ref, dw_ref, dks_ref, dgam_ref, ds_scr):
        @pl.when(pl.program_id(0) == 0)
        def _():
            ds_scr[...] = jnp.zeros_like(ds_scr)

        lane = _lane((1, DH))
        hs = range(HEADS)
        sls = [slice(h * DH, (h + 1) * DH) for h in hs]
        dss = [ds_scr[h] for h in hs]
        for ci in reversed(range(cps)):
            rs = slice(ci * CH, (ci + 1) * CH)
            bg = bg_ref[rs, :]
            gcols = [bg[:, HEADS + h:HEADS + h + 1] for h in hs]
            glasts = [g[CH - 1:CH, :] for g in gcols]
            ss = [s_ref[ci, :, sl] for sl in sls]
            dos = [do_ref[rs, sl] for sl in sls]
            vnl = [vn_ref[rs, sl] for sl in sls]
            dqgs = [_dot(d, s, NT, P_SCANB) for d, s in zip(dos, ss)]
            dps = [_dot(d, vn, NT, P_SCANB) for d, vn in zip(dos, vnl)]
            dvn1 = [_dot(p_ref[ci, h], d, TN, P_SCANB) for h, d in zip(hs, dos)]
            dvn2 = [_dot(k_ref[rs, sl] * jnp.exp(gl - g), ds, NN, P_SCANB)
                    for sl, gl, g, ds in zip(sls, glasts, gcols, dss)]
            dkss = [_dot(vn, ds, NT, P_SCANB) for vn, ds in zip(vnl, dss)]
            dsq = [_dot(q_ref[rs, sl] * jnp.exp(g), d, TN, P_SCANB) for sl, g, d in zip(sls, gcols, dos)]
            dvns = [a + b for a, b in zip(dvn1, dvn2)]
            dws = [_dot(dvn, s, NT, P_SCANB) for dvn, s in zip(dvns, ss)]
            dsw = [_dot(w_ref[rs, sl], dvn, TN, P_SCANB) for sl, dvn in zip(sls, dvns)]
            dgam = jnp.zeros((1, DH), F32)
            for h, sl in enumerate(sls):
                dqg_ref[rs, sl] = dqgs[h]
                dp_ref[ci, h] = dps[h]
                du_ref[rs, sl] = dvns[h].astype(BF16)
                dw_ref[rs, sl] = (-dws[h]).astype(BF16)
                dks_ref[rs, sl] = dkss[h]
                tot = jnp.sum(jnp.sum(dss[h] * ss[h], axis=-1, keepdims=True), axis=0, keepdims=True)
                dgam = dgam + jnp.where(lane == h, tot, 0.0)
            dgam_ref[ci] = jnp.broadcast_to(dgam, (8, DH))
            dss = [ds * jnp.exp(gl) + a - b for ds, gl, a, b in zip(dss, glasts, dsq, dsw)]
        for h in hs:
            ds_scr[h] = dss[h]

    row = pl.BlockSpec((cps * CH, GW), lambda c: (rev(c), 0))
    sq =pl.BlockSpec((cps, HEADS, CH, CH), lambda c: (rev(c), 0, 0, 0))
    big = jax.ShapeDtypeStruct((n, GW), F32)
    return pl.pallas_call(
        body, name="gdn_scan_bwd", grid=(nch // cps,),
        in_specs=[row, row, pl.BlockSpec((cps * CH, DH), lambda c: (rev(c), 0)), row, sq, row,
                  pl.BlockSpec((cps, DH, GW), lambda c: (rev(c), 0, 0)), row],
        out_specs=[row, sq, row, row, row, pl.BlockSpec((cps, 8, DH), lambda c: (rev(c), 0, 0))],
        out_shape=[big, jax.ShapeDtypeStruct((nch, HEADS, CH, CH), F32), jax.ShapeDtypeStruct((n, GW), BF16),
                   jax.ShapeDtypeStruct((n, GW), BF16), big,
                   jax.ShapeDtypeStruct((nch, 8, DH), F32)],
        scratch_shapes=[pltpu.VMEM((HEADS, DH, DH), F32)],
        compiler_params=_params(("arbitrary",)),
    )(q, k, bg, w, p, vn, s_in, do)


def _gdn_intra_bwd(q, k, v, bg, bgt, t, u, w, p, dqg, dp, du, dw, dks, dgam):
    n = q.shape[0]
    nch = n // CH
    cps = 2 if nch % 2 == 0 else 1

    def body(q_ref, k_ref, v_ref, bg_ref, bgt_ref, t_ref, u_ref, w_ref, p_ref,
             dqg_ref, dp_ref, du_ref, dw_ref, dks_ref, dgam_ref, dq_ref, dk_ref, dv_ref, dbg_ref):
        i, j = _ij()
        rows1 = lax.broadcasted_iota(jnp.int32, (CH, 1), 0)
        lane = _lane((CH, DH))
        rsum = lambda x: jnp.sum(x, axis=-1, keepdims=True)
        items = [(ci, h) for ci in range(cps) for h in range(HEADS)]
        at = lambda ref, it: ref.at[it[0] * CH:(it[0] + 1) * CH, it[1] * DH:(it[1] + 1) * DH]
        ld = lambda ref: [at(ref, it)[...] for it in items]
        bgs = [bg_ref[ci * CH:(ci + 1) * CH, :] for ci in range(cps)]
        qs, ks = ld(q_ref), ld(k_ref)
        vecs = [_head_vectors(bgs[ci], bgt_ref[ci], h) for ci, h in items]
        decs = [_decay(gcol, grow) for _, gcol, grow in vecs]
        ths = [t_ref[ci, h] for ci, h in items]
        drus = [_dot(th, x_, TN, P_BWD) for th, x_ in zip(ths, ld(du_ref))]
        drws = [_dot(th, x_, TN, P_BWD) for th, x_ in zip(ths, ld(dw_ref))]
        kks = [_dot(kh, kh, NT, P_GRAM) for kh in ks]
        da1 = [_dot(dru, x_, NT, P_BWD) for dru, x_ in zip(drus, ld(u_ref))]
        da2 = [_dot(drw, x_, NT, P_BWD) for drw, x_ in zip(drws, ld(w_ref))]
        das = [jnp.where(i > j, -(x_ + y_), 0.0) for x_, y_ in zip(da1, da2)]
        dkks = [da * bcol * dec for da, (bcol, _, _), dec in zip(das, vecs, decs)]
        dps = [dp_ref[ci, h] for ci, h in items]
        dqks = [dp_ * dec for dp_, dec in zip(dps, decs)]
        dq_ps = [_dot(dqk, kh, NN, P_BWD) for dqk, kh in zip(dqks, ks)]
        dk_ps = [_dot(dqk, qh, TN, P_BWD) for dqk, qh in zip(dqks, qs)]
        dk_as = [_dot(dkk, kh, NN, P_BWD) for dkk, kh in zip(dkks, ks)]
        dk_bs = [_dot(dkk, kh, TN, P_BWD) for dkk, kh in zip(dkks, ks)]
        bcols = [vc[0] for vc in vecs]
        gcols = [vc[1] for vc in vecs]
        gams = [jnp.exp(g) for g in gcols]
        glasts = [g[CH - 1:CH, :] for g in gcols]
        es = [jnp.exp(gl - g) for gl, g in zip(glasts, gcols)]
        kgs = [kh * gam for kh, gam in zip(ks, gams)]
        dqgs, dkss = ld(dqg_ref), ld(dks_ref)
        wks = [drw * kg for drw, kg in zip(drws, kgs)]
        kss = [dk_ * (kh * e) for dk_, kh, e in zip(dkss, ks, es)]
        r_beta = [rsum(dru * x_ + wk) for dru, x_, wk in zip(drus, ld(v_ref), wks)]
        r_ak = [rsum(da * kk * dec) for da, kk, dec in zip(das, kks, decs)]
        r_gc = [rsum(wk * bcol + dqg * (qh * gam) - ks_)
                for wk, bcol, dqg, qh, gam, ks_ in zip(wks, bcols, dqgs, qs, gams, kss)]
        tk_tot = [jnp.sum(jnp.sum(ks_, axis=0, keepdims=True), axis=-1, keepdims=True) for ks_ in kss]
        mdecs = [da * (bcol * kk * dec) + dp_ * p_ref[ci, h]
                 for (ci, h), da, bcol, kk, dec, dp_ in zip(items, das, bcols, kks, decs, dps)]
        r_md = [rsum(m) for m in mdecs]
        c_md = [rsum(jnp.where(i == j, jnp.sum(m, axis=0, keepdims=True), 0.0)) for m in mdecs]
        dbgs = [jnp.zeros((CH, DH), F32) for _ in range(cps)]
        for n_, (ci, h) in enumerate(items):
            at(dv_ref, (ci, h))[...] = bcols[n_] * drus[n_]
            at(dq_ref, (ci, h))[...] = gams[n_] * dqgs[n_] + dq_ps[n_]
            at(dk_ref, (ci, h))[...] = ((bcols[n_] * gams[n_]) * drws[n_] + dk_ps[n_] + dk_as[n_] + dk_bs[n_]
                                        + dkss[n_] * es[n_])
            dbeta = r_beta[n_] + r_ak[n_]
            dglast = tk_tot[n_] + dgam_ref[ci, 0:1, h:h + 1] * jnp.exp(glasts[n_])
            dgc = r_gc[n_] + r_md[n_] - c_md[n_] + jnp.where(rows1 == CH - 1, dglast, 0.0)
            dbgs[ci] = dbgs[ci] + jnp.where(lane == h, dbeta, 0.0) + jnp.where(lane == HEADS + h, dgc, 0.0)
        for ci in range(cps):
            dbg_ref[ci * CH:(ci + 1) * CH, :] = dbgs[ci]

    row = pl.BlockSpec((cps * CH, GW), lambda c: (c, 0))
    sq = pl.BlockSpec((cps, HEADS, CH, CH), lambda c: (c, 0, 0, 0))
    small = pl.BlockSpec((cps * CH, DH), lambda c: (c, 0))
    big = jax.ShapeDtypeStruct((n, GW), F32)
    return pl.pallas_call(
        body, name="gdn_intra_bwd", grid=(nch // cps,),
        in_specs=[row, row, row, small, pl.BlockSpec((cps, DH, CH), lambda c: (c, 0, 0)), sq, row, row, sq,
                  row, sq, row, row, row, pl.BlockSpec((cps, 8, DH), lambda c: (c, 0, 0))],
        out_specs=[row, row, row, small],
        out_shape=[big, big, big, jax.ShapeDtypeStruct((n, DH), F32)],
        compiler_params=_params(("parallel",)),
    )(q, k, v, bg, bgt, t, u, w, p, dqg, dp, du, dw, dks, dgam)


def _local_step(x, tgt, h, w_g, cqw, late, norm_in_w, ad, gdn_norm_w, conv_b, final_norm_w,
                on_grad_c=None, on_grad_g=None, on_q=None):
    proj_g = _matmul(h, w_g, NT, F32, 512, 1408, 1024, "mm_proj_g", n=GW_COLS, b_outer=True)
    q, k, v = _prep_qkv(proj_g, cqw)
    if on_q is not None:
        q = on_q(q)
    bg, bgt = _prep_bg(proj_g, ad)
    u, w, p, t = _gdn_intra(q, k, v, bg, bgt)
    o, vn, s_in = _gdn_scan(q, k, bg, u, w, p)
    w_c, w_out, conv_w = late(o)
    proj_c = _matmul(h, w_c, NT, F32, 512, 1024, 1024, "mm_proj_c", n=CW_COLS, b_outer=True)
    mix = _conv_branch(proj_c, conv_w, conv_b, _gdn_out(o, proj_g, gdn_norm_w))
    dout, dout_b, g_fn, loss = _out_loss(mix, w_out, x, tgt, final_norm_w)

    g_wout = _matmul(mix, dout_b, TN, BF16, 512, 512, 2048, "mm_gwout")
    do, dproj_g, g_gn = _gdn_out_bwd(o, proj_g, gdn_norm_w, dout_b, w_out)
    dproj_c, g_cw, g_cb = _conv_branch_bwd(proj_c, conv_w, conv_b, dout_b, w_out)
    g_c = _matmul(dproj_c, h, TN, BF16, 1024, 512, 2048, "mm_gwin_c")
    if on_grad_c is not None:
        do = on_grad_c(g_c, g_wout, do)
    dqg, dp, du, dw, dks, dgam = _gdn_scan_bwd(q, k, bg, w, p, vn, s_in, do)
    dq, dk, dv, dbg = _gdn_intra_bwd(q, k, v, bg, bgt, t, u, w, p, dqg, dp, du, dw, dks, dgam)
    dproj_g, gq, gk, gv = _prep_qkv_bwd(proj_g, cqw, dq, dk, dv, dproj_g)
    dproj_g, g_al, g_dt = _prep_bg_bwd(proj_g, ad, dbg, dproj_g)
    g_g = _matmul(dproj_g, h, TN, BF16, 1408, 512, 2048, "mm_gwin_g")
    if on_grad_g is not None:
        dproj_g = on_grad_g(g_g, dproj_g)
    dh = _matmul(dproj_g, w_g, NN, F32, 1024, 1024, 1408, "mm_dh_g")
    gx, g_nin = _dh_rms_bwd(dproj_c, w_c, dh, x, norm_in_w, dout, 1024)
    small = dict(nin=g_nin, cb=g_cb, fn=g_fn, al=g_al, dt=g_dt, gn=g_gn, cq=(gq, gk, gv), cw=g_cw, loss=loss)
    return gx, small, (g_g, g_c, g_wout)


def _place():
    x, y, c = lax.axis_index("x"), lax.axis_index("y"), lax.axis_index("c")
    chips = [(1 - x, y), (x, 1 - y), (1 - x, 1 - y)]
    return x, y, c, chips


def _blk(ref, b):
    if isinstance(b, int):
        return ref.at[b * DH:(b + 1) * DH, :]
    return ref.at[pl.ds(pl.multiple_of(b * DH, DH), DH), :]


HBM = pl.BlockSpec(memory_space=pltpu.HBM)
SEM = pl.BlockSpec(memory_space=pltpu.SEMAPHORE)
EFFECT = pltpu.SideEffectType.DATAFLOW_SIDE_EFFECTING


def _split_start(name, issue, bufs, n_sems):
    nbuf = len(bufs)

    def body(*refs):
        issue(refs[:nbuf], refs[nbuf], refs[nbuf + 1])
        refs[-1][...] = jnp.zeros_like(refs[-1])

    out = pl.pallas_call(
        body, name=name,
        out_shape=(pltpu.SemaphoreType.DMA((n_sems,)), pltpu.SemaphoreType.DMA((n_sems,)),
                   *[pltpu.HBM(b.shape, b.dtype) for b in bufs], jax.ShapeDtypeStruct((8, DH), F32)),
        in_specs=[HBM] * nbuf,
        out_specs=(SEM, SEM, *[HBM] * nbuf, pl.BlockSpec(memory_space=pltpu.VMEM)),
        input_output_aliases={a: 2 + a for a in range(nbuf)},
        compiler_params=pltpu.CompilerParams(has_side_effects=EFFECT),
    )(*[pltpu.with_memory_space_constraint(b, pltpu.HBM) for b in bufs])
    return out[0], out[1], list(out[2:2 + nbuf]), out[-1]


def _split_wait(name, await_, send_sems, recv_sems, bufs, after):
    nbuf = len(bufs)
    after = list(after) if isinstance(after, (list, tuple)) else [after]

    def body(*refs):
        await_(refs[:nbuf], refs[nbuf], refs[nbuf + 1])

    out = pl.pallas_call(
        body, name=name,
        out_shape=tuple(pltpu.HBM(b.shape, b.dtype) for b in bufs),
        in_specs=[HBM] * nbuf + [SEM, SEM] + [ANY] * len(after), out_specs=tuple([HBM] * nbuf),
        input_output_aliases={a: a for a in range(nbuf)},
        compiler_params=pltpu.CompilerParams(has_side_effects=EFFECT),
    )(*bufs, send_sems, recv_sems, *after)
    return list(out)


def _phase_blocks(chip, phase, edges, parity=None):
    return [(b, blk) for b, (grp, blk) in enumerate(_shard_blocks(chip, edges))
            if grp == phase and (parity is None or b % 2 == parity)]


def _cols(ref, nblk):
    return ref.at[0:nblk * DH, :]


def _block_table(chip, edges, spare_g, spare_c):
    rows = []
    for s in range(4):
        sb = _shard_blocks(s, edges)
        rows.append([[blk if grp == "g" else spare_g for grp, blk in sb],
                     [blk if grp == "c" else spare_c for grp, blk in sb],
                     [int(grp == "g") for grp, _ in sb], [s] * ALIGNED_BLOCKS])
    return jnp.asarray(rows, jnp.int32)[chip]


def _place_own(a_shard, wo, cq, cw, bufs):
    d = a_shard.shape[1]
    chip = 2 * lax.axis_index("x") + lax.axis_index("y")

    def body(t_ref, a_ref, wo_ref, cq_ref, cw_ref, *refs):
        wg_ref, wc_ref, wog_ref, cqg_ref, cwg_ref = refs[5:]
        wg_ref[...] = a_ref[...]
        wc_ref[...] = a_ref[...]

        @pl.when(pl.program_id(0) == 0)
        def _():
            wog_ref[0] = wo_ref[...]
            cqg_ref[0] = cq_ref[...]
            cwg_ref[0] = cw_ref[...]

    whole = lambda s: pl.BlockSpec(s.shape, lambda b, t: (0,) * s.ndim)
    slot = lambda s: pl.BlockSpec((1,) + s.shape, lambda b, t: (t[3, 0],) + (0,) * s.ndim)
    return pl.pallas_call(
        body, name="place_own",
        grid_spec=pltpu.PrefetchScalarGridSpec(
            num_scalar_prefetch=1, grid=(ALIGNED_BLOCKS,),
            in_specs=[pl.BlockSpec((DH, d), lambda b, t: (b, 0)), whole(wo), whole(cq), whole(cw)] + [ANY] * 5,
            out_specs=[pl.BlockSpec((DH, d), lambda b, t: (t[0, b], 0)),
                       pl.BlockSpec((DH, d), lambda b, t: (t[1, b], 0)), slot(wo), slot(cq), slot(cw)]),
        out_shape=[jax.ShapeDtypeStruct(b.shape, b.dtype) for b in bufs],
        input_output_aliases={5 + a: a for a in range(5)},
        compiler_params=_params(("arbitrary",)),
    )(_block_table(chip, True, G_SPARE, C_SPARE), a_shard, wo, cq, cw, *bufs)


def _tie(x, token, name):
    def body(x_ref, t_ref, o_ref):
        del x_ref, t_ref, o_ref

    return pl.pallas_call(
        body, name=name, in_specs=[ANY, ANY], out_specs=ANY,
        out_shape=jax.ShapeDtypeStruct(x.shape, x.dtype), input_output_aliases={0: 0},
    )(x, token)


def _gather_start(phase, a_shard, w_grp, singles):
    ns = len(singles)

    def issue(refs, send_sems, recv_sems):
        a_ref, w_ref = refs[0], refs[1]
        x, y, c, chips = _place()
        mine = 2 * x + y
        for jj, (px, py) in enumerate(chips):
            to = dict(device_id=(px, py, c), device_id_type=MESH)
            for a in range(ns):
                pltpu.make_async_remote_copy(
                    src_ref=refs[2 + 2 * a], dst_ref=refs[3 + 2 * a].at[mine],
                    send_sem=send_sems.at[(1 + ns) * jj + 1 + a], recv_sem=recv_sems.at[(1 + ns) * jj + 1 + a],
                    **to).start()
        for s in range(4):
            for par in range(2):
                blocks = _phase_blocks(s, phase, True, par)
                if blocks:
                    @pl.when((mine == s) & (c == par))
                    def _():
                        for b, blk in blocks:
                            for jj, (px, py) in enumerate(chips):
                                pltpu.make_async_remote_copy(
                                    src_ref=_blk(a_ref, b), dst_ref=_blk(w_ref, blk),
                                    send_sem=send_sems.at[(1 + ns) * jj], recv_sem=recv_sems.at[(1 + ns) * jj],
                                    device_id=(px, py, c), device_id_type=MESH).start()

    bufs = [a_shard, w_grp] + [t for pair in singles for t in pair]
    return _split_start("gather_start_" + phase, issue, bufs, 3 * (1 + ns))


def _gather_wait(phase, send_sems, recv_sems, bufs, after):
    ns = (len(bufs) - 2) // 2

    def await_(refs, send_sems, recv_sems):
        a_ref, w_ref = refs[0], refs[1]
        x, y, c, chips = _place()
        mine = 2 * x + y
        for jj, (px, py) in enumerate(chips):
            to = dict(device_id=(px, py, c), device_id_type=MESH)
            peer = 2 * px + py
            for a in range(ns):
                cp = pltpu.make_async_remote_copy(
                    src_ref=refs[2 + 2 * a], dst_ref=refs[3 + 2 * a].at[mine],
                    send_sem=send_sems.at[(1 + ns) * jj + 1 + a], recv_sem=recv_sems.at[(1 + ns) * jj + 1 + a], **to)
                cp.wait_recv()
                cp.wait_send()
            for s in range(4):
                for par in range(2):
                    nblk = len(_phase_blocks(s, phase, True, par))
                    if nblk:
                        both = pltpu.make_async_remote_copy(
                            src_ref=_cols(a_ref, nblk), dst_ref=_cols(w_ref, nblk),
                            send_sem=send_sems.at[(1 + ns) * jj], recv_sem=recv_sems.at[(1 + ns) * jj], **to)

                        @pl.when((peer == s) & (c == par))
                        def _():
                            both.wait_recv()

                        @pl.when((mine == s) & (c == par))
                        def _():
                            both.wait_send()

    return _split_wait("gather_wait_" + phase, await_, send_sems, recv_sems, bufs, after)


def _sibling_forward_parts(phase):
    def each(w_ref, send_sems, recv_sems, start):
        x, y, c, chips = _place()
        to = dict(device_id=(x, y, 1 - c), device_id_type=MESH)
        for jj, (px, py) in enumerate(chips):
            peer = 2 * px + py
            for s in range(4):
                for par in range(2):
                    mine_blocks = _phase_blocks(s, phase, True, par)
                    theirs = len(_phase_blocks(s, phase, True, 1 - par))
                    if not (mine_blocks or theirs):
                        continue

                    @pl.when((peer == s) & (c == par))
                    def _():
                        if start:
                            for _, blk in mine_blocks:
                                pltpu.make_async_remote_copy(
                                    src_ref=_blk(w_ref, blk), dst_ref=_blk(w_ref, blk),
                                    send_sem=send_sems.at[jj], recv_sem=recv_sems.at[jj], **to).start()
                            return
                        if theirs:
                            pltpu.make_async_remote_copy(
                                src_ref=_cols(w_ref, theirs), dst_ref=_cols(w_ref, theirs),
                                send_sem=send_sems.at[jj], recv_sem=recv_sems.at[jj], **to).wait_recv()
                        if mine_blocks:
                            pltpu.make_async_remote_copy(
                                src_ref=_cols(w_ref, len(mine_blocks)), dst_ref=_cols(w_ref, len(mine_blocks)),
                                send_sem=send_sems.at[jj], recv_sem=recv_sems.at[jj], **to).wait_send()

    issue = lambda refs, send_sems, recv_sems: each(refs[0], send_sems, recv_sems, True)
    await_ = lambda refs, send_sems, recv_sems: each(refs[0], send_sems, recv_sems, False)
    return issue, await_


def _sibling_forward(phase, w_grp):
    issue, await_ = _sibling_forward_parts(phase)

    def body(w_in_ref, w_ref, send_sems, recv_sems):
        del w_in_ref
        issue([w_ref], send_sems, recv_sems)
        await_([w_ref], send_sems, recv_sems)

    return pl.pallas_call(
        body, name="sibling_forward_" + phase, in_specs=[ANY], out_specs=ANY,
        out_shape=jax.ShapeDtypeStruct(w_grp.shape, w_grp.dtype), input_output_aliases={0: 0},
        scratch_shapes=[pltpu.SemaphoreType.DMA((3,)), pltpu.SemaphoreType.DMA((3,))],
    )(w_grp)


def _merge_edges(w, edge0, mixed, name):
    d = w.shape[1]

    def body(e_ref, o_ref):
        o_ref[...] = e_ref[0:DH, :] + e_ref[DH:2 * DH, :]

    def to_block(i):
        r = mixed[-1]
        for kk in range(len(mixed) - 2, -1, -1):
            r = jnp.where(i == kk, mixed[kk], r)
        return r

    return pl.pallas_call(
        body, name=name, grid=(len(mixed),),
        in_specs=[pl.BlockSpec((2 * DH, d), lambda i: (edge0 // 2 + i, 0))],
        out_specs=pl.BlockSpec((DH, d), lambda i: (to_block(i), 0)),
        out_shape=jax.ShapeDtypeStruct(w.shape, w.dtype),
        input_output_aliases={0: 0},
        compiler_params=_params(("arbitrary",)),
    )(w)


def _scatter_start(phase, g_grp, land, singles, halved=False):
    ns = len(singles)

    def issue(refs, send_sems, recv_sems):
        g_ref, land_ref = refs[0], refs[1]
        x, y, c, chips = _place()
        for jj, (px, py) in enumerate(chips):
            to = dict(device_id=(px, py, c), device_id_type=MESH)
            peer = 2 * px + py
            for a in range(ns):
                pltpu.make_async_remote_copy(
                    src_ref=refs[2 + 2 * a].at[peer], dst_ref=refs[3 + 2 * a].at[jj],
                    send_sem=send_sems.at[(1 + ns) * jj + 1 + a], recv_sem=recv_sems.at[(1 + ns) * jj + 1 + a],
                    **to).start()
            for s in range(4):
                for par in ((0, 1) if halved else (None,)):
                    blocks = _phase_blocks(s, phase, False, par)
                    if blocks:
                        @pl.when((peer == s) if par is None else ((peer == s) & (c == par)))
                        def _():
                            for b, blk in blocks:
                                pltpu.make_async_remote_copy(
                                    src_ref=_blk(g_ref, blk), dst_ref=_blk(land_ref.at[jj], b),
                                    send_sem=send_sems.at[(1 + ns) * jj], recv_sem=recv_sems.at[(1 + ns) * jj],
                                    **to).start()

    bufs = [g_grp, land] + [t for pair in singles for t in pair]
    return _split_start("scatter_start_" + phase, issue, bufs, 3 * (1 + ns))


def _scatter_wait(phase, send_sems, recv_sems, bufs, after, halved=False):
    ns = (len(bufs) - 2) // 2

    def await_(refs, send_sems, recv_sems):
        g_ref, land_ref = refs[0], refs[1]
        x, y, c, chips = _place()
        mine = 2 * x + y
        for jj, (px, py) in enumerate(chips):
            to = dict(device_id=(px, py, c), device_id_type=MESH)
            peer = 2 * px + py
            for a in range(ns):
                cp = pltpu.make_async_remote_copy(
                    src_ref=refs[2 + 2 * a].at[peer], dst_ref=refs[3 + 2 * a].at[jj],
                    send_sem=send_sems.at[(1 + ns) * jj + 1 + a], recv_sem=recv_sems.at[(1 + ns) * jj + 1 + a], **to)
                cp.wait_recv()
                cp.wait_send()
            for s in range(4):
                for par in ((0, 1) if halved else (None,)):
                    nblk = len(_phase_blocks(s, phase, False, par))
                    if nblk:
                        both = pltpu.make_async_remote_copy(
                            src_ref=_cols(g_ref, nblk), dst_ref=_cols(land_ref.at[jj], nblk),
                            send_sem=send_sems.at[(1 + ns) * jj], recv_sem=recv_sems.at[(1 + ns) * jj], **to)

                        @pl.when((mine == s) if par is None else ((mine == s) & (c == par)))
                        def _():
                            both.wait_recv()

                        @pl.when((peer == s) if par is None else ((peer == s) & (c == par)))
                        def _():
                            both.wait_send()

    return _split_wait("scatter_wait_" + phase, await_, send_sems, recv_sems, bufs, after)


def _needed_blocks(phase, parity):
    return sorted({blk for s in range(4) for _, blk in _phase_blocks(s, phase, False, parity)})


def _pair_reduce(phase, g_grp):
    n, d = g_grp.shape

    def swap(g_ref, sib_ref, send_sem, recv_sem):
        x, y, c, _ = _place()
        to = dict(device_id=(x, y, 1 - c), device_id_type=MESH)
        for par in range(2):
            give, get = _needed_blocks(phase, 1 - par), _needed_blocks(phase, par)

            @pl.when(c == par)
            def _():
                for blk in give:
                    pltpu.make_async_remote_copy(src_ref=_blk(g_ref, blk), dst_ref=_blk(sib_ref, blk),
                                                 send_sem=send_sem, recv_sem=recv_sem, **to).start()
                pltpu.make_async_remote_copy(src_ref=_cols(g_ref, len(get)), dst_ref=_cols(sib_ref, len(get)),
                                             send_sem=send_sem, recv_sem=recv_sem, **to).wait_recv()
                pltpu.make_async_remote_copy(src_ref=_cols(g_ref, len(give)), dst_ref=_cols(sib_ref, len(give)),
                                             send_sem=send_sem, recv_sem=recv_sem, **to).wait_send()

    sib = pl.pallas_call(
        swap, name="pair_swap_" + phase, in_specs=[ANY], out_specs=ANY,
        out_shape=jax.ShapeDtypeStruct((n, d), g_grp.dtype),
        scratch_shapes=[pltpu.SemaphoreType.DMA, pltpu.SemaphoreType.DMA],
    )(*_in_hbm(g_grp))

    lists = [_needed_blocks(phase, par) for par in range(2)]
    longest = max(len(t) for t in lists)
    table = jnp.asarray([t + [t[-1]] * (longest - len(t)) for t in lists], jnp.int32)[lax.axis_index("c")]

    def add(t_ref, a_ref, b_ref, o_ref):
        o_ref[...] = (a_ref[...].astype(F32) + b_ref[...].astype(F32)).astype(o_ref.dtype)

    blk = pl.BlockSpec((DH, d), lambda i, t: (t[i], 0))
    return pl.pallas_call(
        add, name="pair_add_" + phase,
        grid_spec=pltpu.PrefetchScalarGridSpec(num_scalar_prefetch=1, grid=(longest,),
                                               in_specs=[blk, blk], out_specs=blk),
        out_shape=jax.ShapeDtypeStruct((n, d), g_grp.dtype),
        compiler_params=_params(("arbitrary",)),
    )(table, g_grp, sib)


def _sum_shard(g_g, g_c, land):
    d = g_g.shape[1]
    chip = 2 * lax.axis_index("x") + lax.axis_index("y")

    def body(t_ref, gg_ref, gc_ref, land_ref, o_ref):
        b = pl.program_id(0)
        in_g = t_ref[2, b] == 1
        own = jnp.where(in_g, gg_ref[...].astype(F32), gc_ref[...].astype(F32))
        for jj in range(3):
            own = own + land_ref[jj].astype(F32)
        o_ref[...] = jnp.where(in_g & (b % 2 != lax.axis_index("c")), 0.0, own)

    return pl.pallas_call(
        body, name="sum_w_in",
        grid_spec=pltpu.PrefetchScalarGridSpec(
            num_scalar_prefetch=1, grid=(ALIGNED_BLOCKS,),
            in_specs=[pl.BlockSpec((DH, d), lambda b, t: (t[0, b], 0)), pl.BlockSpec((DH, d), lambda b, t: (t[1, b], 0)),
                      pl.BlockSpec((3, DH, d), lambda b, t: (0, b, 0))],
            out_specs=pl.BlockSpec((DH, d), lambda b, t: (b, 0))),
        out_shape=jax.ShapeDtypeStruct((ALIGNED_W, d), F32),
        compiler_params=_params(("arbitrary",)),
    )(_block_table(chip, False, 0, 0), g_g, g_c, land)


def _sum_rows(stack, land, rows):
    _, r, d = stack.shape
    rows = min(rows, r)
    chip = 2 * lax.axis_index("x") + lax.axis_index("y")

    def body(t_ref, own_ref, land_ref, o_ref):
        acc = own_ref[0].astype(F32)
        for jj in range(3):
            acc = acc + land_ref[jj].astype(F32)
        o_ref[...] = acc

    return pl.pallas_call(
        body, name="sum_w_out",
        grid_spec=pltpu.PrefetchScalarGridSpec(
            num_scalar_prefetch=1, grid=(r // rows,),
            in_specs=[pl.BlockSpec((1, rows, d), lambda i, t: (t[0], i, 0)),
                      pl.BlockSpec((3, rows, d), lambda i, t: (0, i, 0))],
            out_specs=pl.BlockSpec((rows, d), lambda i, t: (i, 0))),
        out_shape=jax.ShapeDtypeStruct((r, d), F32),
        compiler_params=_params(("arbitrary",)),
    )(jnp.reshape(chip, (1,)).astype(jnp.int32), stack, land)


def _exchange_parts(n_swap, with_pack):
    def copies(refs, send_sems, recv_sems):
        x, y, c, _ = _place()
        me = 4 * x + 2 * y + c
        cps = [pltpu.make_async_remote_copy(
            src_ref=refs[2 * a], dst_ref=refs[2 * a + 1], send_sem=send_sems.at[a], recv_sem=recv_sems.at[a],
            device_id=(x, y, 1 - c), device_id_type=MESH) for a in range(n_swap)]
        if with_pack:
            pack_ref, packs = refs[2 * n_swap], refs[2 * n_swap + 1]
            for r in range(1, 8):
                dx, dy, dc = (r >> 2) & 1, (r >> 1) & 1, r & 1
                peer = (x + dx - 2 * x * dx, y + dy - 2 * y * dy, c + dc - 2 * c * dc)
                cps.append(pltpu.make_async_remote_copy(
                    src_ref=pack_ref, dst_ref=packs.at[me], send_sem=send_sems.at[n_swap + r - 1],
                    recv_sem=recv_sems.at[n_swap + r - 1], device_id=peer, device_id_type=MESH))
        return cps

    def issue(refs, send_sems, recv_sems):
        for cp in copies(refs, send_sems, recv_sems):
            cp.start()

    def await_(refs, send_sems, recv_sems):
        cps = copies(refs, send_sems, recv_sems)
        for cp in cps:
            cp.wait_recv()
        for cp in cps:
            cp.wait_send()

    return issue, await_, n_swap + (7 if with_pack else 0)


def _adamw_update(g, w_ref, m_ref, v_ref, go, do, mo, vo):
    c1 = 1.0 / (1.0 - ADAM_B1 ** ADAM_STEP)
    c2 = 1.0 / (1.0 - ADAM_B2 ** ADAM_STEP)
    mn = ADAM_B1 * m_ref[...] + (1.0 - ADAM_B1) * g
    vn = ADAM_B2 * v_ref[...] + (1.0 - ADAM_B2) * (g * g)
    go[...] = g
    mo[...] = mn
    vo[...] = vn
    do[...] = -ADAM_LR * ((mn * c1) / (jnp.sqrt(vn * c2) + ADAM_EPS) + ADAM_WD * w_ref[...])


def _adamw(w, m, v, g1, g2, rows, name):
    r, cdim = w.shape
    rows = min(rows, r)

    def body(w_ref, m_ref, v_ref, g1_ref, g2_ref, *outs):
        _adamw_update(g1_ref[...] + g2_ref[...], w_ref, m_ref, v_ref, *outs)

    blk = pl.BlockSpec((rows, cdim), lambda i: (i, 0))
    shp = jax.ShapeDtypeStruct((r, cdim), F32)
    return pl.pallas_call(
        body, name=name, grid=(r // rows,),
        in_specs=[blk] * 5, out_specs=[blk] * 4, out_shape=[shp] * 4,
        compiler_params=_params(("parallel",), 20 * rows * cdim * 4 + 8 * 2**20),
    )(*_in_hbm(w, m, v, g1, g2))


def _adamw_small(w_s, m_s, v_s, pack, packs):
    per_row = GW // DH
    cq_blocks, cw_blocks = 3 * per_row // 4, per_row // 4
    cq_lanes, cw_lanes = cq_blocks * DH, cw_blocks * DH
    shapes = [(1, GW), (4, cq_lanes), (1, HEADS), (1, HEADS), (1, DH), (3, cw_lanes), (1, GW), (per_row, DH)]

    def body(own_ref, p_ref, w_ref, m_ref, v_ref, *refs):
        t_ref, g_scr, kinds = refs[len(refs) - 6], refs[len(refs) - 5], refs[len(refs) - 4:]
        chip = 2 * lax.axis_index("x") + lax.axis_index("y")
        me = 2 * chip + lax.axis_index("c")
        acc = jnp.where(me == 0, own_ref[...], p_ref[0])
        for d in range(1, 8):
            acc = acc + jnp.where(me == d, own_ref[...], p_ref[d])
        t_ref[...] = acc
        refs[32][...] = acc[R_LOSS:R_LOSS + 1, 0:1]

        def mine(first_row, rows_per_tap, nblk, t, jb):
            out = None
            for k in reversed(range(4)):
                b = nblk * k + jb
                row = first_row + rows_per_tap * t + b // per_row
                cand = t_ref[row:row + 1, (b % per_row) * DH:(b % per_row + 1) * DH]
                out = cand if out is None else jnp.where(chip == k, cand, out)
            return out

        g_scr[...] = jnp.zeros_like(g_scr)
        for src, dst in ((R_NIN, S_NIN), (R_CB, S_CB), (R_FN, S_FN), (R_AD, S_AD), (R_GN, S_GN)):
            g_scr[dst:dst + 1, :] = t_ref[src:src + 1, :]
        for t in range(4):
            for jb in range(cq_blocks):
                g_scr[S_CQ + t:S_CQ + t + 1, jb * DH:(jb + 1) * DH] = mine(R_CQ, 3, cq_blocks, t, jb)
        for t in range(3):
            for jb in range(cw_blocks):
                g_scr[S_CW + t:S_CW + t + 1, jb * DH:(jb + 1) * DH] = mine(R_CW, 1, cw_blocks, t, jb)
        _adamw_update(g_scr[...], w_ref, m_ref, v_ref, *kinds)
        for kk, a in enumerate(kinds):
            nin, cq, al, dt, gn, cw, cb, fn = refs[8 * kk:8 * kk + 8]
            nin[...] = a[S_NIN:S_NIN + 1, :]
            cq[...] = a[S_CQ:S_CQ + 4, 0:cq_lanes]
            al[...] = a[S_AD:S_AD + 1, 0:HEADS]
            dt[...] = a[S_AD:S_AD + 1, HEADS:2 * HEADS]
            gn[...] = a[S_GN:S_GN + 1, 0:DH]
            cw[...] = a[S_CW:S_CW + 3, 0:cw_lanes]
            cb[...] = a[S_CB:S_CB + 1, :]
            for k in range(per_row):
                fn[k:k + 1, :] = a[S_FN:S_FN + 1, k * DH:(k + 1) * DH]

    out = pl.pallas_call(
        body, name="adamw_small",
        out_shape=[jax.ShapeDtypeStruct(s, F32) for s in shapes] * 4 + [jax.ShapeDtypeStruct((1, 1), F32)],
        scratch_shapes=[pltpu.VMEM(pack.shape, F32)] + [pltpu.VMEM(w_s.shape, F32)] * 5,
    )(pack, packs, w_s, m_s, v_s)
    return [out[8 * kk:8 * kk + 8] for kk in range(4)], out[32]


def _adamw_shard(wt, mt, vt, g1, g2):
    r, d = wt.shape
    cols = min(256, d)

    def body(w_ref, m_ref, v_ref, g_ref, g2_ref, go, do, mo, vo, pad_ref):
        chip = 2 * lax.axis_index("x") + lax.axis_index("y")
        back = [(ALIGNED_W - s) % ALIGNED_W for s in SHIFTS]
        pad_ref[...] = pltpu.roll(g_ref[...] + g2_ref[...], _by_chip(chip, back), 0)
        outs = [o.at[:, 0, :] for o in (go, do, mo, vo)]
        _adamw_update(pad_ref[0:r, :], w_ref, m_ref, v_ref, *outs)

    blk = pl.BlockSpec((r, cols), lambda i: (0, i))
    gblk = pl.BlockSpec((ALIGNED_W, cols), lambda i: (0, i))
    oblk = pl.BlockSpec((r, 1, cols), lambda i: (0, 0, i))
    shp = jax.ShapeDtypeStruct((r, 1, d), F32)
    return pl.pallas_call(
        body, name="adamw_w_in", grid=(d // cols,),
        in_specs=[blk] * 3 + [gblk] * 2, out_specs=[oblk] * 4, out_shape=[shp] * 4,
        scratch_shapes=[pltpu.VMEM((ALIGNED_W, cols), F32)],
        compiler_params=_params(("parallel",), 24 * ALIGNED_W * cols * 4 + 8 * 2**20),
    )(wt, mt, vt, g1, g2)


def _pad_lanes(a, width):
    return jnp.pad(a, ((0, 0), (0, width - a.shape[1])))


def _gathered_to_full(g):
    return jnp.transpose(g, (1, 0, 2)).reshape(g.shape[1], 4 * g.shape[2])


def _row(a):
    return _pad_lanes(a.reshape(1, -1), 1024)


S_NIN, S_CB, S_FN, S_AD, S_GN, S_CQ, S_CW = 0, 1, 2, 3, 4, 5, 9


def _small_pack(nin, cb, fn, al, dt, gn, cqw_shard, cw_shard):
    ad = jnp.concatenate([al.reshape(1, -1), dt.reshape(1, -1)], axis=1)
    rows = [_row(nin), _row(cb), _row(fn), _row(ad), _row(gn), _pad_lanes(cqw_shard, 1024),
            _pad_lanes(cw_shard, 1024)]
    out = jnp.concatenate(rows, axis=0)
    return jnp.pad(out, ((0, 16 - out.shape[0]), (0, 0)))


def kernel(x, norm_in_w, w_in, conv_qkv_w, A_log, dt_bias, gdn_norm_w, conv_w, conv_b, w_out, final_norm_w, loss_target, m_norm_in_w, m_w_in, m_conv_qkv_w, m_A_log, m_dt_bias, m_gdn_norm_w, m_conv_w, m_conv_b, m_w_out, m_final_norm_w, v_norm_in_w, v_w_in, v_conv_qkv_w, v_A_log, v_dt_bias, v_gdn_norm_w, v_conv_w, v_conv_b, v_w_out, v_final_norm_w):
    a_shard = _align_shard(jnp.transpose(w_in, (2, 0, 1)))
    d_model = x.shape[-1]
    stack = lambda s: lax.empty((4,) + s.shape, s.dtype)
    wg0 = lax.empty((WG_BLOCKS * DH, d_model), BF16)
    wc0 = lax.empty((WC_BLOCKS * DH, d_model), BF16)
    ss_g, rs_g, bufs_g, tok_g = _gather_start("g", a_shard, wg0, [(conv_qkv_w[0], stack(conv_qkv_w[0]))])
    wo_b = _cast_bf16(w_out[0], 256, "cast_w_out", tok_g)
    ss_c, rs_c, bufs_c, tok_c = _gather_start("c", bufs_g[0], wc0,
                                              [(conv_w[0], stack(conv_w[0])), (wo_b, stack(wo_b))])
    wg1, wc1, wog1, cqg1, cwg1 = _place_own(bufs_c[0], bufs_c[4], bufs_g[2], bufs_c[2],
                                            [bufs_g[1], bufs_c[1], bufs_c[5], bufs_g[3], bufs_c[3]])
    x0 = x[0]
    h = _rms_in(x0, _tie(_tie(norm_in_w, tok_g, "after_gather_start_g"), tok_c, "after_gather_start_c"))
    adam_in = [jnp.transpose(a[0]) for a in (w_in, m_w_in, v_w_in)]
    sp = lambda nin, cb, fn, al, dt, gn, cq, cwv: _small_pack(nin, cb, fn, al, dt, gn, cq[0], cwv[0])
    w_s = sp(norm_in_w, conv_b, final_norm_w, A_log, dt_bias, gdn_norm_w, conv_qkv_w, conv_w)
    m_s = sp(m_norm_in_w, m_conv_b, m_final_norm_w, m_A_log, m_dt_bias, m_gdn_norm_w, m_conv_qkv_w, m_conv_w)
    v_s = sp(v_norm_in_w, v_conv_b, v_final_norm_w, v_A_log, v_dt_bias, v_gdn_norm_w, v_conv_qkv_w, v_conv_w)
    a_thru, wg, _, cq_g = _gather_wait("g", ss_g, rs_g, [bufs_c[0], wg1, bufs_g[2], cqg1],
                                       [h, w_s, m_s, v_s] + adam_in[1:])
    w_g = _merge_edges(_sibling_forward("g", wg), G_EDGE, G_MIXED, "merge_edges_g")
    cqw = _gathered_to_full(cq_g)
    ad = jnp.pad(jnp.concatenate([A_log, dt_bias], axis=0), ((0, 0), (A_LANE, 0)))
    fwd_c = {}

    def on_q(q):
        _, wc, _, cw_g, _, wo_g = _gather_wait("c", ss_c, rs_c,
                                               [a_thru, wc1, bufs_c[2], cwg1, bufs_c[4], wog1], q)
        issue, _ = _sibling_forward_parts("c")
        ss, rs, (wc,), tok = _split_start("sibling_forward_start_c", issue, [wc], 3)
        fwd_c.update(ss=ss, rs=rs, wc=wc, cw_g=cw_g, wo_g=wo_g)
        return _tie(q, tok, "after_sibling_forward_start_c")

    def late(o):
        _, await_ = _sibling_forward_parts("c")
        (wc,) = _split_wait("sibling_forward_wait_c", await_, fwd_c["ss"], fwd_c["rs"], [fwd_c["wc"]], o)
        return (_merge_edges(wc, C_EDGE, C_MIXED, "merge_edges_c"), fwd_c["wo_g"].reshape(2 * GW, d_model),
                _gathered_to_full(fwd_c["cw_g"]))

    scat = {}

    def on_grad_c(g_c, g_wout, do):
        go4 = g_wout.reshape(4, GW // 2, d_model)
        land = lax.empty((3, ALIGNED_W, d_model), BF16)
        land_o = lax.empty((3, GW // 2, d_model), BF16)
        ss, rs, bufs, tok = _scatter_start("c", g_c, land, [(go4, land_o)])
        scat["c"] = (ss, rs, bufs)
        return _tie(do, tok, "after_scatter_start_c")

    def on_grad_g(g_g, dproj_g):
        ss, rs, bufs, tok = _scatter_start("g", _pair_reduce("g", g_g), scat["c"][2][1], [], halved=True)
        scat["g"] = (ss, rs, bufs)
        return _tie(dproj_g, tok, "after_scatter_start_g")

    gx, sm, _ = _local_step(x0, loss_target[0], h, w_g, cqw, late, norm_in_w, ad, gdn_norm_w, conv_b,
                            final_norm_w.reshape(1, -1), on_grad_c, on_grad_g, on_q)

    ss, rs, bufs = scat["c"]
    g_c, land, go4, land_o = _scatter_wait("c", ss, rs, [bufs[0], scat["g"][2][1], bufs[2], bufs[3]], gx)
    part_out = _sum_rows(go4, land_o, 128)
    ad_g = jnp.concatenate([sm["al"][:, A_LANE:], sm["dt"][:, A_LANE:]], axis=1)
    pack = jnp.concatenate([_row(sm["nin"]), _row(sm["cb"]), _row(sm["fn"]), _row(ad_g), _row(sm["gn"]),
                            jnp.concatenate(sm["cq"], axis=1).reshape(12, 1024), sm["cw"], _row(sm["loss"])], axis=0)
    pack = jnp.pad(pack, ((0, PACK_ROWS - pack.shape[0]), (0, 0)))
    issue, await_a, nsem = _exchange_parts(1, True)
    ss_a, rs_a, bufs_a, tok_a = _split_start(
        "exchange_start_small", issue,
        [part_out, lax.empty(part_out.shape, F32), pack, lax.empty((8,) + pack.shape, F32)], nsem)
    ss, rs, bufs = scat["g"]
    g_g, land = _scatter_wait("g", ss, rs, [bufs[0], land], [gx, tok_a], halved=True)
    part_in = _sum_shard(g_g, g_c, land)
    issue, await_b, nsem = _exchange_parts(1, False)
    ss_b, rs_b, bufs_b, tok_b = _split_start("exchange_start_w_in", issue,
                                             [part_in, lax.empty(part_in.shape, F32)], nsem)
    part_out, sib_out, pack, packs = _split_wait("exchange_wait_small", await_a, ss_a, rs_a, bufs_a, tok_b)
    g_wo, d_wo, m_wo, v_wo = _adamw(w_out[0], m_w_out[0], v_w_out[0], part_out, sib_out, 128, "adamw_w_out")
    small, loss = _adamw_small(w_s, m_s, v_s, pack, packs)
    part_in, sib_in = _split_wait("exchange_wait_w_in", await_b, ss_b, rs_b, bufs_b, [small[0][0], d_wo])
    g_wi, d_wi, m_wi, v_wi = [jnp.transpose(a, (1, 2, 0))[0] for a in _adamw_shard(*adam_in, part_in, sib_in)]

    def unpack(leaves, big_in, big_out):
        nin, cq, al, dt, gn, cw, cb, fn = leaves
        return (nin, big_in[None], cq[None], al, dt, gn, cw[None], cb, big_out[None], fn.reshape(-1))

    return (loss[0, 0], gx[None], *unpack(small[0], g_wi, g_wo), *unpack(small[1], d_wi, d_wo),
            *unpack(small[2], m_wi, m_wo), *unpack(small[3], v_wi, v_wo))
```
